```python
import jax, jax.numpy as jnp
from jax import lax
import numpy as np

D_MODEL = 1024
BATCH = 8
SEQ = 2048
DEPTH = 2

RET_HEADS = 4
RET_DK = 128
RET_DV = 128
RET_CHUNK = 128
ROPE_BASE = 10000.0
GDN_HEADS = 4
GDN_DK = 128
GDN_DV = 128
GDN_CHUNK = 64
SHORT_CONV = 4
RET_WIDTH = RET_HEADS * RET_DV
GDN_WIDTH = GDN_HEADS * GDN_DV
MIX0_OUT = RET_WIDTH + GDN_WIDTH
MIX0_SPLITS = (RET_HEADS * RET_DK, RET_HEADS * RET_DK, RET_WIDTH, RET_WIDTH,
               GDN_HEADS * GDN_DK, GDN_HEADS * GDN_DK, GDN_WIDTH, GDN_WIDTH,
               GDN_HEADS, GDN_HEADS)
MIX0_IN = sum(MIX0_SPLITS)
D_RNN = D_MODEL
LRU_BLOCKS = 8
LRU_BLOCK = D_RNN // LRU_BLOCKS
LRU_CONV = 4
LRU_C = 8.0
D_FF = ((8 * D_MODEL // 3 + 127) // 128) * 128
FFN_CONV = 3
EPS = 1e-6
N_EVEN = (DEPTH + 1) // 2
N_ODD = DEPTH // 2

kernel_name = "hybrid_retention_gdn_rglru_convffn"


def _rms(x):
    xf = x.astype(jnp.float32)
    return xf * lax.rsqrt(jnp.mean(xf * xf, axis=-1, keepdims=True) + EPS)


def rms_norm(x, gain):
    return (_rms(x) * gain.astype(jnp.float32)).astype(x.dtype)


def l2norm(x):
    return x * lax.rsqrt(jnp.sum(x * x, axis=-1, keepdims=True) + EPS)


def causal_dwconv(x, w):
    width, ch = w.shape
    return lax.conv_general_dilated(
        x, w[:, None, :].astype(x.dtype), window_strides=(1,),
        padding=[(width - 1, 0)], dimension_numbers=('NWC', 'WIO', 'NWC'),
        feature_group_count=ch)


def rotary(x, pos):
    half = x.shape[-1] // 2
    inv_freq = ROPE_BASE ** (-jnp.arange(half, dtype=jnp.float32) / half)
    ang = pos.astype(jnp.float32)[:, None] * inv_freq[None, :]
    cos = jnp.cos(ang)[None, :, None, :]
    sin = jnp.sin(ang)[None, :, None, :]
    x1, x2 = x[..., :half], x[..., half:]
    return jnp.concatenate([x1 * cos - x2 * sin, x1 * sin + x2 * cos], axis=-1)


def to_chunks(x, c):
    b, t, h, d = x.shape
    return x.reshape(b, t // c, c, h, d).transpose(0, 3, 1, 2, 4)


def from_chunks(x):
    b, h, n, c, d = x.shape
    return x.transpose(0, 2, 3, 1, 4).reshape(b, n * c, h, d)


def retention_chunkwise(q, k, v):
    b, t, h, dk = q.shape
    dv = v.shape[-1]
    c = RET_CHUNK
    log_gamma = jnp.log1p(-jnp.exp2(-5.0 - jnp.arange(h, dtype=jnp.float32)))
    qc, kc, vc = to_chunks(q, c), to_chunks(k * dk ** -0.5, c), to_chunks(v, c)
    idx = jnp.arange(c, dtype=jnp.float32)
    rel = idx[:, None] - idx[None, :]
    causal = rel >= 0
    dmask = jnp.where(causal, jnp.exp(log_gamma[:, None, None] * jnp.where(causal, rel, 0.0)), 0.0)
    scores = jnp.einsum('bhnid,bhnjd->bhnij', qc, kc) * dmask[:, None]
    intra = jnp.einsum('bhnij,bhnjv->bhniv', scores, vc)
    k_tail = kc * jnp.exp(log_gamma[:, None, None] * (c - 1 - idx))[..., None]
    kv = jnp.einsum('bhncd,bhncv->nbhdv', k_tail, vc)
    chunk_decay = jnp.exp(log_gamma * c)[None, :, None, None]

    def step(s, kv_n):
        return s * chunk_decay + kv_n, s

    _, s_prev = lax.scan(step, jnp.zeros((b, h, dk, dv), jnp.float32), kv)
    q_decay = qc * jnp.exp(log_gamma[:, None, None] * (idx + 1.0))[..., None]
    inter = jnp.einsum('bhncd,nbhdv->bhncv', q_decay, s_prev)
    return from_chunks(intra + inter)


def gated_delta_chunkwise(q, k, v, g, beta):
    b, t, h, dk = q.shape
    dv = v.shape[-1]
    c = GDN_CHUNK
    qc = to_chunks(q * dk ** -0.5, c)
    kc = to_chunks(k, c)
    vc = to_chunks(v, c)
    gc = jnp.cumsum(to_chunks(g[..., None], c)[..., 0], axis=-1)
    bc = to_chunks(beta[..., None], c)
    tril = jnp.tril(jnp.ones((c, c), bool))
    strict = jnp.tril(jnp.ones((c, c), bool), -1)
    decay = jnp.exp(jnp.where(tril, gc[..., :, None] - gc[..., None, :], -jnp.inf))
    k_beta = kc * bc
    lmat = jnp.where(strict, jnp.einsum('bhnid,bhnjd->bhnij', k_beta, kc) * decay, 0.0)
    rhs = jnp.concatenate([vc * bc, k_beta * jnp.exp(gc)[..., None]], axis=-1)
    sol = lax.linalg.triangular_solve(lmat + jnp.eye(c, dtype=jnp.float32), rhs,
                                      left_side=True, lower=True)
    u, w = sol[..., :dv], sol[..., dv:]
    attn = jnp.where(tril, jnp.einsum('bhnid,bhnjd->bhnij', qc, kc) * decay, 0.0)
    q_decay = qc * jnp.exp(gc)[..., None]
    g_last = gc[..., -1:]
    k_tail = kc * jnp.exp(g_last - gc)[..., None]
    chunk_decay = jnp.exp(g_last)[..., None]
    xs = tuple(jnp.moveaxis(a, 2, 0) for a in (u, w, attn, q_decay, k_tail, chunk_decay))

    def step(s, inp):
        u_n, w_n, a_n, qd_n, kt_n, cd_n = inp
        v_new = u_n - jnp.einsum('bhcd,bhdv->bhcv', w_n, s)
        o = jnp.einsum('bhcd,bhdv->bhcv', qd_n, s) + jnp.einsum('bhij,bhjv->bhiv', a_n, v_new)
        s = s * cd_n + jnp.einsum('bhcd,bhcv->bhdv', kt_n, v_new)
        return s, o

    _, o = lax.scan(step, jnp.zeros((b, h, dk, dv), jnp.float32), xs)
    return from_chunks(jnp.moveaxis(o, 0, 2))


def retention_deltanet_mixer(hn, pos, w_in, conv_w, a_log, dt_bias, out_gain, w_out):
    b, t, _ = hn.shape
    f32 = jnp.float32
    split_at = np.cumsum(MIX0_SPLITS)[:-1].tolist()
    q_r, k_r, v_r, g_r, q_d, k_d, v_d, g_d, b_d, a_d = jnp.split(hn @ w_in, split_at, axis=-1)
    q_r = rotary(q_r.reshape(b, t, RET_HEADS, RET_DK).astype(f32), pos)
    k_r = rotary(k_r.reshape(b, t, RET_HEADS, RET_DK).astype(f32), pos)
    v_r = v_r.reshape(b, t, RET_HEADS, RET_DV).astype(f32)
    y_r = _rms(retention_chunkwise(q_r, k_r, v_r)).reshape(b, t, RET_WIDTH)
    y_r = y_r * jax.nn.silu(g_r.astype(f32))
    qkv = jax.nn.silu(causal_dwconv(jnp.concatenate([q_d, k_d, v_d], axis=-1), conv_w)).astype(f32)
    q_d, k_d, v_d = jnp.split(qkv, [GDN_HEADS * GDN_DK, 2 * GDN_HEADS * GDN_DK], axis=-1)
    q_d = l2norm(q_d.reshape(b, t, GDN_HEADS, GDN_DK))
    k_d = l2norm(k_d.reshape(b, t, GDN_HEADS, GDN_DK))
    v_d = v_d.reshape(b, t, GDN_HEADS, GDN_DV)
    beta = jax.nn.sigmoid(b_d.astype(f32))
    g = -jnp.exp(a_log.astype(f32)) * jax.nn.softplus(a_d.astype(f32) + dt_bias.astype(f32))
    y_d = gated_delta_chunkwise(q_d, k_d, v_d, g, beta)
    y_d = rms_norm(y_d, out_gain) * jax.nn.silu(g_d.astype(f32).reshape(b, t, GDN_HEADS, GDN_DV))
    y = jnp.concatenate([y_r, y_d.reshape(b, t, GDN_WIDTH)], axis=-1).astype(hn.dtype)
    return y @ w_out


def _linear_combine(e1, e2):
    a1, b1 = e1
    a2, b2 = e2
    return a1 * a2, a2 * b1 + b2


def rglru_mixer(hn, w_in, conv_w, conv_b, w_a, b_a, w_x, b_x, lam, w_out):
    b, t, _ = hn.shape
    f32 = jnp.float32
    gate, xr = jnp.split(hn @ w_in, 2, axis=-1)
    xr = (causal_dwconv(xr, conv_w) + conv_b).astype(f32)
    xb = xr.reshape(b, t, LRU_BLOCKS, LRU_BLOCK)
    r = jax.nn.sigmoid(jnp.einsum('btni,nij->btnj', xb, w_a.astype(f32)).reshape(b, t, D_RNN) + b_a)
    i = jax.nn.sigmoid(jnp.einsum('btni,nij->btnj', xb, w_x.astype(f32)).reshape(b, t, D_RNN) + b_x)
    log_a = -LRU_C * r * jax.nn.softplus(-lam.astype(f32))
    a = jnp.exp(log_a)
    u = jnp.sqrt(-jnp.expm1(2.0 * log_a)) * (i * xr)
    _, hs = lax.associative_scan(_linear_combine, (a, u), axis=1)
    y = jax.nn.gelu(gate.astype(f32)) * hs
    return y.astype(hn.dtype) @ w_out


def conv_ffn(hn, w_up, conv_w, conv_b, w_down):
    up = causal_dwconv(hn @ w_up, conv_w) + conv_b
    gate, val = jnp.split(up, 2, axis=-1)
    return (jax.nn.silu(gate) * val) @ w_down


def _fwd_setup_inputs(seed: int = 0) -> dict:
    key = jax.random.key(seed)
    ks = jax.random.split(key, 24)
    nrm = jax.random.normal
    f32 = jnp.float32
    x = nrm(ks[0], (BATCH, SEQ, D_MODEL), f32)
    norm_mix = 1.0 + 0.02 * nrm(ks[1], (DEPTH, D_MODEL), f32)
    norm_ffn = 1.0 + 0.02 * nrm(ks[2], (DEPTH, D_MODEL), f32)
    ret_gdn_w_in = nrm(ks[3], (N_EVEN, D_MODEL, MIX0_IN), f32) * D_MODEL ** -0.5
    gdn_conv_w = nrm(ks[4], (N_EVEN, SHORT_CONV, 2 * GDN_HEADS * GDN_DK + GDN_WIDTH), f32) * SHORT_CONV ** -0.5
    gdn_a_log = jnp.log(jax.random.uniform(ks[5], (N_EVEN, GDN_HEADS), f32, 1.0, 16.0))
    dt = jnp.exp(jax.random.uniform(ks[6], (N_EVEN, GDN_HEADS), f32, np.log(1e-3), np.log(1e-1)))
    gdn_dt_bias = dt + jnp.log(-jnp.expm1(-dt))
    gdn_out_gain = 1.0 + 0.02 * nrm(ks[7], (N_EVEN, GDN_DV), f32)
    ret_gdn_w_out = nrm(ks[8], (N_EVEN, MIX0_OUT, D_MODEL), f32) * MIX0_OUT ** -0.5
    lru_w_in = nrm(ks[9], (N_ODD, D_MODEL, 2 * D_RNN), f32) * D_MODEL ** -0.5
    lru_conv_w = nrm(ks[10], (N_ODD, LRU_CONV, D_RNN), f32) * LRU_CONV ** -0.5
    lru_conv_b = 0.01 * nrm(ks[11], (N_ODD, D_RNN), f32)
    lru_w_a = nrm(ks[12], (N_ODD, LRU_BLOCKS, LRU_BLOCK, LRU_BLOCK), f32) * LRU_BLOCK ** -0.5
    lru_b_a = 0.01 * nrm(ks[13], (N_ODD, D_RNN), f32)
    lru_w_x = nrm(ks[14], (N_ODD, LRU_BLOCKS, LRU_BLOCK, LRU_BLOCK), f32) * LRU_BLOCK ** -0.5
    lru_b_x = 0.01 * nrm(ks[15], (N_ODD, D_RNN), f32)
    a_c = jax.random.uniform(ks[16], (N_ODD, D_RNN), f32, 0.9, 0.999)
    a0 = a_c ** (1.0 / LRU_C)
    lru_lambda = jnp.log(a0) - jnp.log1p(-a0)
    lru_w_out = nrm(ks[17], (N_ODD, D_RNN, D_MODEL), f32) * D_RNN ** -0.5
    ffn_w_up = nrm(ks[18], (DEPTH, D_MODEL, 2 * D_FF), f32) * D_MODEL ** -0.5
    ffn_conv_w = nrm(ks[19], (DEPTH, FFN_CONV, 2 * D_FF), f32) * FFN_CONV ** -0.5
    ffn_conv_b = 0.01 * nrm(ks[20], (DEPTH, 2 * D_FF), f32)
    ffn_w_down = nrm(ks[21], (DEPTH, D_FF, D_MODEL), f32) * D_FF ** -0.5
    norm_final = 1.0 + 0.02 * nrm(ks[22], (D_MODEL,), f32)
    return {"x": x, "norm_mix": norm_mix, "norm_ffn": norm_ffn,
            "ret_gdn_w_in": ret_gdn_w_in, "gdn_conv_w": gdn_conv_w, "gdn_a_log": gdn_a_log,
            "gdn_dt_bias": gdn_dt_bias, "gdn_out_gain": gdn_out_gain, "ret_gdn_w_out": ret_gdn_w_out,
            "lru_w_in": lru_w_in, "lru_conv_w": lru_conv_w, "lru_conv_b": lru_conv_b,
            "lru_w_a": lru_w_a, "lru_b_a": lru_b_a, "lru_w_x": lru_w_x, "lru_b_x": lru_b_x,
            "lru_lambda": lru_lambda, "lru_w_out": lru_w_out,
            "ffn_w_up": ffn_w_up, "ffn_conv_w": ffn_conv_w, "ffn_conv_b": ffn_conv_b,
            "ffn_w_down": ffn_w_down, "norm_final": norm_final}


def _fwd_reference(x, norm_mix, norm_ffn, ret_gdn_w_in, gdn_conv_w, gdn_a_log, gdn_dt_bias,
              gdn_out_gain, ret_gdn_w_out, lru_w_in, lru_conv_w, lru_conv_b, lru_w_a, lru_b_a,
              lru_w_x, lru_b_x, lru_lambda, lru_w_out, ffn_w_up, ffn_conv_w, ffn_conv_b,
              ffn_w_down, norm_final):
    pos = jnp.arange(x.shape[1], dtype=jnp.int32)
    h = x
    for layer in range(DEPTH):
        hn = rms_norm(h, norm_mix[layer])
        if layer % 2 == 0:
            e = layer // 2
            h = h + retention_deltanet_mixer(hn, pos, ret_gdn_w_in[e], gdn_conv_w[e], gdn_a_log[e],
                                             gdn_dt_bias[e], gdn_out_gain[e], ret_gdn_w_out[e])
        else:
            o = layer // 2
            h = h + rglru_mixer(hn, lru_w_in[o], lru_conv_w[o], lru_conv_b[o], lru_w_a[o], lru_b_a[o],
                                lru_w_x[o], lru_b_x[o], lru_lambda[o], lru_w_out[o])
        h = h + conv_ffn(rms_norm(h, norm_ffn[layer]), ffn_w_up[layer], ffn_conv_w[layer],
                         ffn_conv_b[layer], ffn_w_down[layer])
    return rms_norm(h, norm_final)


import jax as _jax
import jax.numpy as _jnp

TWIN_FORMAT = 'train_step'
FWD_PARAMS = ['x', 'norm_mix', 'norm_ffn', 'ret_gdn_w_in', 'gdn_conv_w', 'gdn_a_log', 'gdn_dt_bias', 'gdn_out_gain', 'ret_gdn_w_out', 'lru_w_in', 'lru_conv_w', 'lru_conv_b', 'lru_w_a', 'lru_b_a', 'lru_w_x', 'lru_b_x', 'lru_lambda', 'lru_w_out', 'ffn_w_up', 'ffn_conv_w', 'ffn_conv_b', 'ffn_w_down', 'norm_final']
TWIN_WEIGHTS = ['norm_mix', 'norm_ffn', 'ret_gdn_w_in', 'gdn_conv_w', 'gdn_a_log', 'gdn_dt_bias', 'gdn_out_gain', 'ret_gdn_w_out', 'lru_w_in', 'lru_conv_w', 'lru_conv_b', 'lru_w_a', 'lru_b_a', 'lru_w_x', 'lru_b_x', 'lru_lambda', 'lru_w_out', 'ffn_w_up', 'ffn_conv_w', 'ffn_conv_b', 'ffn_w_down', 'norm_final']
TWIN_DIFF_INPUT = 'x'
TWIN_INPUTS = ['x', 'norm_mix', 'norm_ffn', 'ret_gdn_w_in', 'gdn_conv_w', 'gdn_a_log', 'gdn_dt_bias', 'gdn_out_gain', 'ret_gdn_w_out', 'lru_w_in', 'lru_conv_w', 'lru_conv_b', 'lru_w_a', 'lru_b_a', 'lru_w_x', 'lru_b_x', 'lru_lambda', 'lru_w_out', 'ffn_w_up', 'ffn_conv_w', 'ffn_conv_b', 'ffn_w_down', 'norm_final', 'loss_target', 'm_norm_mix', 'm_norm_ffn', 'm_ret_gdn_w_in', 'm_gdn_conv_w', 'm_gdn_a_log', 'm_gdn_dt_bias', 'm_gdn_out_gain', 'm_ret_gdn_w_out', 'm_lru_w_in', 'm_lru_conv_w', 'm_lru_conv_b', 'm_lru_w_a', 'm_lru_b_a', 'm_lru_w_x', 'm_lru_b_x', 'm_lru_lambda', 'm_lru_w_out', 'm_ffn_w_up', 'm_ffn_conv_w', 'm_ffn_conv_b', 'm_ffn_w_down', 'm_norm_final', 'v_norm_mix', 'v_norm_ffn', 'v_ret_gdn_w_in', 'v_gdn_conv_w', 'v_gdn_a_log', 'v_gdn_dt_bias', 'v_gdn_out_gain', 'v_ret_gdn_w_out', 'v_lru_w_in', 'v_lru_conv_w', 'v_lru_conv_b', 'v_lru_w_a', 'v_lru_b_a', 'v_lru_w_x', 'v_lru_b_x', 'v_lru_lambda', 'v_lru_w_out', 'v_ffn_w_up', 'v_ffn_conv_w', 'v_ffn_conv_b', 'v_ffn_w_down', 'v_norm_final']
TWIN_OUTPUTS = ['loss', 'grad_x', 'grad_norm_mix', 'grad_norm_ffn', 'grad_ret_gdn_w_in', 'grad_gdn_conv_w', 'grad_gdn_a_log', 'grad_gdn_dt_bias', 'grad_gdn_out_gain', 'grad_ret_gdn_w_out', 'grad_lru_w_in', 'grad_lru_conv_w', 'grad_lru_conv_b', 'grad_lru_w_a', 'grad_lru_b_a', 'grad_lru_w_x', 'grad_lru_b_x', 'grad_lru_lambda', 'grad_lru_w_out', 'grad_ffn_w_up', 'grad_ffn_conv_w', 'grad_ffn_conv_b', 'grad_ffn_w_down', 'grad_norm_final', 'delta_norm_mix', 'delta_norm_ffn', 'delta_ret_gdn_w_in', 'delta_gdn_conv_w', 'delta_gdn_a_log', 'delta_gdn_dt_bias', 'delta_gdn_out_gain', 'delta_ret_gdn_w_out', 'delta_lru_w_in', 'delta_lru_conv_w', 'delta_lru_conv_b', 'delta_lru_w_a', 'delta_lru_b_a', 'delta_lru_w_x', 'delta_lru_b_x', 'delta_lru_lambda', 'delta_lru_w_out', 'delta_ffn_w_up', 'delta_ffn_conv_w', 'delta_ffn_conv_b', 'delta_ffn_w_down', 'delta_norm_final', 'new_m_norm_mix', 'new_m_norm_ffn', 'new_m_ret_gdn_w_in', 'new_m_gdn_conv_w', 'new_m_gdn_a_log', 'new_m_gdn_dt_bias', 'new_m_gdn_out_gain', 'new_m_ret_gdn_w_out', 'new_m_lru_w_in', 'new_m_lru_conv_w', 'new_m_lru_conv_b', 'new_m_lru_w_a', 'new_m_lru_b_a', 'new_m_lru_w_x', 'new_m_lru_b_x', 'new_m_lru_lambda', 'new_m_lru_w_out', 'new_m_ffn_w_up', 'new_m_ffn_conv_w', 'new_m_ffn_conv_b', 'new_m_ffn_w_down', 'new_m_norm_final', 'new_v_norm_mix', 'new_v_norm_ffn', 'new_v_ret_gdn_w_in', 'new_v_gdn_conv_w', 'new_v_gdn_a_log', 'new_v_gdn_dt_bias', 'new_v_gdn_out_gain', 'new_v_ret_gdn_w_out', 'new_v_lru_w_in', 'new_v_lru_conv_w', 'new_v_lru_conv_b', 'new_v_lru_w_a', 'new_v_lru_b_a', 'new_v_lru_w_x', 'new_v_lru_b_x', 'new_v_lru_lambda', 'new_v_lru_w_out', 'new_v_ffn_w_up', 'new_v_ffn_conv_w', 'new_v_ffn_conv_b', 'new_v_ffn_w_down', 'new_v_norm_final']
TWIN_LEAF_KINDS = {'loss': 'loss', 'grad_x': 'grad_x', 'grad_norm_mix': 'grad_w', 'grad_norm_ffn': 'grad_w', 'grad_ret_gdn_w_in': 'grad_w', 'grad_gdn_conv_w': 'grad_w', 'grad_gdn_a_log': 'grad_w', 'grad_gdn_dt_bias': 'grad_w', 'grad_gdn_out_gain': 'grad_w', 'grad_ret_gdn_w_out': 'grad_w', 'grad_lru_w_in': 'grad_w', 'grad_lru_conv_w': 'grad_w', 'grad_lru_conv_b': 'grad_w', 'grad_lru_w_a': 'grad_w', 'grad_lru_b_a': 'grad_w', 'grad_lru_w_x': 'grad_w', 'grad_lru_b_x': 'grad_w', 'grad_lru_lambda': 'grad_w', 'grad_lru_w_out': 'grad_w', 'grad_ffn_w_up': 'grad_w', 'grad_ffn_conv_w': 'grad_w', 'grad_ffn_conv_b': 'grad_w', 'grad_ffn_w_down': 'grad_w', 'grad_norm_final': 'grad_w', 'delta_norm_mix': 'delta_w', 'delta_norm_ffn': 'delta_w', 'delta_ret_gdn_w_in': 'delta_w', 'delta_gdn_conv_w': 'delta_w', 'delta_gdn_a_log': 'delta_w', 'delta_gdn_dt_bias': 'delta_w', 'delta_gdn_out_gain': 'delta_w', 'delta_ret_gdn_w_out': 'delta_w', 'delta_lru_w_in': 'delta_w', 'delta_lru_conv_w': 'delta_w', 'delta_lru_conv_b': 'delta_w', 'delta_lru_w_a': 'delta_w', 'delta_lru_b_a': 'delta_w', 'delta_lru_w_x': 'delta_w', 'delta_lru_b_x': 'delta_w', 'delta_lru_lambda': 'delta_w', 'delta_lru_w_out': 'delta_w', 'delta_ffn_w_up': 'delta_w', 'delta_ffn_conv_w': 'delta_w', 'delta_ffn_conv_b': 'delta_w', 'delta_ffn_w_down': 'delta_w', 'delta_norm_final': 'delta_w', 'new_m_norm_mix': 'new_m', 'new_m_norm_ffn': 'new_m', 'new_m_ret_gdn_w_in': 'new_m', 'new_m_gdn_conv_w': 'new_m', 'new_m_gdn_a_log': 'new_m', 'new_m_gdn_dt_bias': 'new_m', 'new_m_gdn_out_gain': 'new_m', 'new_m_ret_gdn_w_out': 'new_m', 'new_m_lru_w_in': 'new_m', 'new_m_lru_conv_w': 'new_m', 'new_m_lru_conv_b': 'new_m', 'new_m_lru_w_a': 'new_m', 'new_m_lru_b_a': 'new_m', 'new_m_lru_w_x': 'new_m', 'new_m_lru_b_x': 'new_m', 'new_m_lru_lambda': 'new_m', 'new_m_lru_w_out': 'new_m', 'new_m_ffn_w_up': 'new_m', 'new_m_ffn_conv_w': 'new_m', 'new_m_ffn_conv_b': 'new_m', 'new_m_ffn_w_down': 'new_m', 'new_m_norm_final': 'new_m', 'new_v_norm_mix': 'new_v', 'new_v_norm_ffn': 'new_v', 'new_v_ret_gdn_w_in': 'new_v', 'new_v_gdn_conv_w': 'new_v', 'new_v_gdn_a_log': 'new_v', 'new_v_gdn_dt_bias': 'new_v', 'new_v_gdn_out_gain': 'new_v', 'new_v_ret_gdn_w_out': 'new_v', 'new_v_lru_w_in': 'new_v', 'new_v_lru_conv_w': 'new_v', 'new_v_lru_conv_b': 'new_v', 'new_v_lru_w_a': 'new_v', 'new_v_lru_b_a': 'new_v', 'new_v_lru_w_x': 'new_v', 'new_v_lru_b_x': 'new_v', 'new_v_lru_lambda': 'new_v', 'new_v_lru_w_out': 'new_v', 'new_v_ffn_w_up': 'new_v', 'new_v_ffn_conv_w': 'new_v', 'new_v_ffn_conv_b': 'new_v', 'new_v_ffn_w_down': 'new_v', 'new_v_norm_final': 'new_v'}


def _forward(args):
    return _fwd_reference(*[args[k] for k in FWD_PARAMS])


def _output_shape():
    out = _jax.eval_shape(lambda: _forward(_fwd_setup_inputs(0)))
    return out.shape, out.dtype

N_MICROBATCH = 1
ADAM_LR = 0.001
ADAM_B1 = 0.9
ADAM_B2 = 0.999
ADAM_EPS = 1e-08
ADAM_WD = 0.01
ADAM_STEP = 10
PER_EXAMPLE_BATCH_AXIS = {'x': 0, 'loss_target': 0}
SHARED_INPUTS = []
_WEIGHT_DTYPES = {'norm_mix': _jnp.float32, 'norm_ffn': _jnp.float32, 'ret_gdn_w_in': _jnp.float32, 'gdn_conv_w': _jnp.float32, 'gdn_a_log': _jnp.float32, 'gdn_dt_bias': _jnp.float32, 'gdn_out_gain': _jnp.float32, 'ret_gdn_w_out': _jnp.float32, 'lru_w_in': _jnp.float32, 'lru_conv_w': _jnp.float32, 'lru_conv_b': _jnp.float32, 'lru_w_a': _jnp.float32, 'lru_b_a': _jnp.float32, 'lru_w_x': _jnp.float32, 'lru_b_x': _jnp.float32, 'lru_lambda': _jnp.float32, 'lru_w_out': _jnp.float32, 'ffn_w_up': _jnp.float32, 'ffn_conv_w': _jnp.float32, 'ffn_conv_b': _jnp.float32, 'ffn_w_down': _jnp.float32, 'norm_final': _jnp.float32}
MOMENT_SCALE = {'norm_mix': 1.212566e-01, 'norm_ffn': 8.550332e-02, 'ret_gdn_w_in': 7.561200e-02, 'gdn_conv_w': 6.707512e-02, 'gdn_a_log': 6.980516e-01, 'gdn_dt_bias': 6.864469e-01, 'gdn_out_gain': 1.812571e-01, 'ret_gdn_w_out': 8.273513e-02, 'lru_w_in': 4.861609e-02, 'lru_conv_w': 5.226609e-02, 'lru_conv_b': 4.214724e-01, 'lru_w_a': 1.289068e-02, 'lru_b_a': 1.232490e-02, 'lru_w_x': 2.298036e-02, 'lru_b_x': 1.678592e-02, 'lru_lambda': 2.585897e-02, 'lru_w_out': 4.894684e-02, 'ffn_w_up': 3.599116e-02, 'ffn_conv_w': 3.584317e-02, 'ffn_conv_b': 4.111925e-02, 'ffn_w_down': 5.885113e-02, 'norm_final': 1.601363e+01}


def _to_microbatches(a, axis):
    t = _jnp.moveaxis(a, axis, 0)
    t = t.reshape((N_MICROBATCH, t.shape[0] // N_MICROBATCH) + t.shape[1:])
    return _jnp.moveaxis(t, 1, axis + 1)


def setup_inputs(seed: int = 0) -> dict:
    inp = _fwd_setup_inputs(seed)
    key = _jax.random.fold_in(_jax.random.key(seed), 7919)
    shape, _ = _output_shape()
    out = dict(inp)
    out["loss_target"] = _jax.random.normal(_jax.random.fold_in(key, 0), shape, _jnp.float32)
    for i, name in enumerate(TWIN_WEIGHTS):
        w = inp[name].astype(_jnp.float32)
        if MOMENT_SCALE is None:
            s = _jnp.sqrt(_jnp.mean(_jnp.square(w)) + 1e-30)
        else:
            s = MOMENT_SCALE[name]
        km, kv = _jax.random.split(_jax.random.fold_in(key, i + 1))
        out[name] = w
        out["m_" + name] = s * _jax.random.normal(km, w.shape, _jnp.float32)
        out["v_" + name] = (s * s) * _jax.random.uniform(kv, w.shape, _jnp.float32, 0.5, 1.5)
    if N_MICROBATCH > 1:
        for name, axis in PER_EXAMPLE_BATCH_AXIS.items():
            out[name] = _to_microbatches(out[name], axis)
    return {'x': out['x'], 'norm_mix': out['norm_mix'], 'norm_ffn': out['norm_ffn'], 'ret_gdn_w_in': out['ret_gdn_w_in'], 'gdn_conv_w': out['gdn_conv_w'], 'gdn_a_log': out['gdn_a_log'], 'gdn_dt_bias': out['gdn_dt_bias'], 'gdn_out_gain': out['gdn_out_gain'], 'ret_gdn_w_out': out['ret_gdn_w_out'], 'lru_w_in': out['lru_w_in'], 'lru_conv_w': out['lru_conv_w'], 'lru_conv_b': out['lru_conv_b'], 'lru_w_a': out['lru_w_a'], 'lru_b_a': out['lru_b_a'], 'lru_w_x': out['lru_w_x'], 'lru_b_x': out['lru_b_x'], 'lru_lambda': out['lru_lambda'], 'lru_w_out': out['lru_w_out'], 'ffn_w_up': out['ffn_w_up'], 'ffn_conv_w': out['ffn_conv_w'], 'ffn_conv_b': out['ffn_conv_b'], 'ffn_w_down': out['ffn_w_down'], 'norm_final': out['norm_final'], 'loss_target': out['loss_target'], 'm_norm_mix': out['m_norm_mix'], 'm_norm_ffn': out['m_norm_ffn'], 'm_ret_gdn_w_in': out['m_ret_gdn_w_in'], 'm_gdn_conv_w': out['m_gdn_conv_w'], 'm_gdn_a_log': out['m_gdn_a_log'], 'm_gdn_dt_bias': out['m_gdn_dt_bias'], 'm_gdn_out_gain': out['m_gdn_out_gain'], 'm_ret_gdn_w_out': out['m_ret_gdn_w_out'], 'm_lru_w_in': out['m_lru_w_in'], 'm_lru_conv_w': out['m_lru_conv_w'], 'm_lru_conv_b': out['m_lru_conv_b'], 'm_lru_w_a': out['m_lru_w_a'], 'm_lru_b_a': out['m_lru_b_a'], 'm_lru_w_x': out['m_lru_w_x'], 'm_lru_b_x': out['m_lru_b_x'], 'm_lru_lambda': out['m_lru_lambda'], 'm_lru_w_out': out['m_lru_w_out'], 'm_ffn_w_up': out['m_ffn_w_up'], 'm_ffn_conv_w': out['m_ffn_conv_w'], 'm_ffn_conv_b': out['m_ffn_conv_b'], 'm_ffn_w_down': out['m_ffn_w_down'], 'm_norm_final': out['m_norm_final'], 'v_norm_mix': out['v_norm_mix'], 'v_norm_ffn': out['v_norm_ffn'], 'v_ret_gdn_w_in': out['v_ret_gdn_w_in'], 'v_gdn_conv_w': out['v_gdn_conv_w'], 'v_gdn_a_log': out['v_gdn_a_log'], 'v_gdn_dt_bias': out['v_gdn_dt_bias'], 'v_gdn_out_gain': out['v_gdn_out_gain'], 'v_ret_gdn_w_out': out['v_ret_gdn_w_out'], 'v_lru_w_in': out['v_lru_w_in'], 'v_lru_conv_w': out['v_lru_conv_w'], 'v_lru_conv_b': out['v_lru_conv_b'], 'v_lru_w_a': out['v_lru_w_a'], 'v_lru_b_a': out['v_lru_b_a'], 'v_lru_w_x': out['v_lru_w_x'], 'v_lru_b_x': out['v_lru_b_x'], 'v_lru_lambda': out['v_lru_lambda'], 'v_lru_w_out': out['v_lru_w_out'], 'v_ffn_w_up': out['v_ffn_w_up'], 'v_ffn_conv_w': out['v_ffn_conv_w'], 'v_ffn_conv_b': out['v_ffn_conv_b'], 'v_ffn_w_down': out['v_ffn_w_down'], 'v_norm_final': out['v_norm_final']}


def _loss(weights, diff, rest, loss_target):
    with _jax.named_scope("forward"):
        args = {**rest, TWIN_DIFF_INPUT: diff, **{k: w.astype(_WEIGHT_DTYPES[k]) for k, w in weights.items()}}
        y = _forward(args)
    with _jax.named_scope("loss_head"):
        err = _jnp.square(y.astype(_jnp.float32) - loss_target)
        return 0.5 * _jnp.sum(_jnp.mean(err, axis=-1)) if err.ndim else 0.5 * err


def _adamw(w, g, m, v):
    m = ADAM_B1 * m + (1.0 - ADAM_B1) * g
    v = ADAM_B2 * v + (1.0 - ADAM_B2) * _jnp.square(g)
    m_hat = m / (1.0 - ADAM_B1 ** ADAM_STEP)
    v_hat = v / (1.0 - ADAM_B2 ** ADAM_STEP)
    delta = -ADAM_LR * (m_hat / (_jnp.sqrt(v_hat) + ADAM_EPS) + ADAM_WD * w)
    return delta, m, v


def reference(x, norm_mix, norm_ffn, ret_gdn_w_in, gdn_conv_w, gdn_a_log, gdn_dt_bias, gdn_out_gain, ret_gdn_w_out, lru_w_in, lru_conv_w, lru_conv_b, lru_w_a, lru_b_a, lru_w_x, lru_b_x, lru_lambda, lru_w_out, ffn_w_up, ffn_conv_w, ffn_conv_b, ffn_w_down, norm_final, loss_target, m_norm_mix, m_norm_ffn, m_ret_gdn_w_in, m_gdn_conv_w, m_gdn_a_log, m_gdn_dt_bias, m_gdn_out_gain, m_ret_gdn_w_out, m_lru_w_in, m_lru_conv_w, m_lru_conv_b, m_lru_w_a, m_lru_b_a, m_lru_w_x, m_lru_b_x, m_lru_lambda, m_lru_w_out, m_ffn_w_up, m_ffn_conv_w, m_ffn_conv_b, m_ffn_w_down, m_norm_final, v_norm_mix, v_norm_ffn, v_ret_gdn_w_in, v_gdn_conv_w, v_gdn_a_log, v_gdn_dt_bias, v_gdn_out_gain, v_ret_gdn_w_out, v_lru_w_in, v_lru_conv_w, v_lru_conv_b, v_lru_w_a, v_lru_b_a, v_lru_w_x, v_lru_b_x, v_lru_lambda, v_lru_w_out, v_ffn_w_up, v_ffn_conv_w, v_ffn_conv_b, v_ffn_w_down, v_norm_final):
    given = dict(x=x, norm_mix=norm_mix, norm_ffn=norm_ffn, ret_gdn_w_in=ret_gdn_w_in, gdn_conv_w=gdn_conv_w, gdn_a_log=gdn_a_log, gdn_dt_bias=gdn_dt_bias, gdn_out_gain=gdn_out_gain, ret_gdn_w_out=ret_gdn_w_out, lru_w_in=lru_w_in, lru_conv_w=lru_conv_w, lru_conv_b=lru_conv_b, lru_w_a=lru_w_a, lru_b_a=lru_b_a, lru_w_x=lru_w_x, lru_b_x=lru_b_x, lru_lambda=lru_lambda, lru_w_out=lru_w_out, ffn_w_up=ffn_w_up, ffn_conv_w=ffn_conv_w, ffn_conv_b=ffn_conv_b, ffn_w_down=ffn_w_down, norm_final=norm_final, loss_target=loss_target, m_norm_mix=m_norm_mix, m_norm_ffn=m_norm_ffn, m_ret_gdn_w_in=m_ret_gdn_w_in, m_gdn_conv_w=m_gdn_conv_w, m_gdn_a_log=m_gdn_a_log, m_gdn_dt_bias=m_gdn_dt_bias, m_gdn_out_gain=m_gdn_out_gain, m_ret_gdn_w_out=m_ret_gdn_w_out, m_lru_w_in=m_lru_w_in, m_lru_conv_w=m_lru_conv_w, m_lru_conv_b=m_lru_conv_b, m_lru_w_a=m_lru_w_a, m_lru_b_a=m_lru_b_a, m_lru_w_x=m_lru_w_x, m_lru_b_x=m_lru_b_x, m_lru_lambda=m_lru_lambda, m_lru_w_out=m_lru_w_out, m_ffn_w_up=m_ffn_w_up, m_ffn_conv_w=m_ffn_conv_w, m_ffn_conv_b=m_ffn_conv_b, m_ffn_w_down=m_ffn_w_down, m_norm_final=m_norm_final, v_norm_mix=v_norm_mix, v_norm_ffn=v_norm_ffn, v_ret_gdn_w_in=v_ret_gdn_w_in, v_gdn_conv_w=v_gdn_conv_w, v_gdn_a_log=v_gdn_a_log, v_gdn_dt_bias=v_gdn_dt_bias, v_gdn_out_gain=v_gdn_out_gain, v_ret_gdn_w_out=v_ret_gdn_w_out, v_lru_w_in=v_lru_w_in, v_lru_conv_w=v_lru_conv_w, v_lru_conv_b=v_lru_conv_b, v_lru_w_a=v_lru_w_a, v_lru_b_a=v_lru_b_a, v_lru_w_x=v_lru_w_x, v_lru_b_x=v_lru_b_x, v_lru_lambda=v_lru_lambda, v_lru_w_out=v_lru_w_out, v_ffn_w_up=v_ffn_w_up, v_ffn_conv_w=v_ffn_conv_w, v_ffn_conv_b=v_ffn_conv_b, v_ffn_w_down=v_ffn_w_down, v_norm_final=v_norm_final)
    weights = {n: given[n] for n in TWIN_WEIGHTS}
    shared = {n: given[n] for n in SHARED_INPUTS}
    per_example = {n: given[n] for n in ['x']}
    grad_fn = _jax.value_and_grad(_loss, argnums=(0, 1))

    def one_microbatch(ex, loss_target):
        ex = dict(ex)
        diff = ex.pop(TWIN_DIFF_INPUT)
        return grad_fn(weights, diff, {**shared, **ex}, loss_target)

    if N_MICROBATCH == 1:
        loss, (grad_w, grad_x) = one_microbatch(per_example, given["loss_target"])
    else:
        def body(carry, xs):
            loss_sum, grad_sum = carry
            l_k, (gw_k, gx_k) = one_microbatch(xs[0], xs[1])
            with _jax.named_scope("update"):
                return (loss_sum + l_k, _jax.tree.map(_jnp.add, grad_sum, gw_k)), gx_k

        init = (_jnp.zeros((), _jnp.float32), _jax.tree.map(_jnp.zeros_like, weights))
        (loss, grad_w), grad_x = _jax.lax.scan(body, init, (per_example, given["loss_target"]))
    with _jax.named_scope("update"):
        delta_w, new_m, new_v = {}, {}, {}
        for n in TWIN_WEIGHTS:
            delta_w[n], new_m[n], new_v[n] = _adamw(weights[n], grad_w[n], given["m_" + n], given["v_" + n])
    return (loss, grad_x, *[grad_w[n] for n in TWIN_WEIGHTS], *[delta_w[n] for n in TWIN_WEIGHTS],
            *[new_m[n] for n in TWIN_WEIGHTS], *[new_v[n] for n in TWIN_WEIGHTS])
```

```python
import functools

import numpy as np
import jax
import jax.numpy as jnp
from jax import lax
from jax.experimental import pallas as pl
from jax.experimental.pallas import tpu as pltpu

F32 = jnp.float32
BF16 = jnp.bfloat16
HI = lax.Precision.HIGHEST
MESH = pl.DeviceIdType.MESH

SEQ = 2048
D_MODEL = 1024
N_HEADS = 4
HEAD = 128
RET_CHUNK = 128
GDN_CHUNK = 64
GROUP = N_HEADS * HEAD
MIX_MAIN = 8 * GROUP
D_FF = 2816
LRU_BLOCKS = 8
LRU_C = 8.0
ROPE_BASE = 10000.0
EPS = 1e-6
N_SHARD = 4
LANES = 128

ADAM_LR, ADAM_B1, ADAM_B2, ADAM_EPS, ADAM_WD, ADAM_STEP = 0.001, 0.9, 0.999, 1e-08, 0.01, 10

VMEM_LIMIT_BYTES = 56 * 1024 * 1024

_roll = pltpu.roll


def _params(**kw):
    return pltpu.CompilerParams(vmem_limit_bytes=VMEM_LIMIT_BYTES, **kw)


def _sds(shape, dtype):
    return jax.ShapeDtypeStruct(tuple(shape), dtype)


def _shift_raw(x, d):
    n = x.shape[0]
    t = lax.broadcasted_iota(jnp.int32, x.shape, 0)
    if d > 0:
        return jnp.where(t >= d, _roll(x, d, 0), 0.0)
    return jnp.where(t < n + d, _roll(x, n + d, 0), 0.0)


@functools.partial(jax.custom_vjp, nondiff_argnums=(1,))
def shift_rows(x, d):
    return _shift_raw(x, d)


def _shift_fwd(x, d):
    return _shift_raw(x, d), None


def _shift_bwd(d, _, g):
    return (_shift_raw(g, -d),)


shift_rows.defvjp(_shift_fwd, _shift_bwd)


@jax.custom_vjp
def swap_halves(x):
    return _roll(x, HEAD // 2, 1)


def _swap_fwd(x):
    return _roll(x, HEAD // 2, 1), None


def _swap_bwd(_, g):
    return (_roll(g, HEAD // 2, 1),)


swap_halves.defvjp(_swap_fwd, _swap_bwd)


def _scan_raw(a, u, reverse):
    n = a.shape[0]
    t = lax.broadcasted_iota(jnp.int32, a.shape, 0)
    d = 1
    while d < n:
        if reverse:
            m = t < n - d
            a_s, u_s = _roll(a, n - d, 0), _roll(u, n - d, 0)
        else:
            m = t >= d
            a_s, u_s = _roll(a, d, 0), _roll(u, d, 0)
        u = a * jnp.where(m, u_s, 0.0) + u
        a = a * jnp.where(m, a_s, 1.0)
        d *= 2
    return u


@jax.custom_vjp
def lin_scan(a, u):
    return _scan_raw(a, u, False)


def _lin_scan_fwd(a, u):
    hs = _scan_raw(a, u, False)
    return hs, (a, hs)


def _lin_scan_bwd(res, g):
    a, hs = res
    lam = _scan_raw(_shift_raw(a, -1), g, True)
    return lam * _shift_raw(hs, 1), lam


lin_scan.defvjp(_lin_scan_fwd, _lin_scan_bwd)


def _hdot(a, b):
    return jnp.dot(a, b, precision=HI, preferred_element_type=F32)


def _eye(n):
    i = lax.broadcasted_iota(jnp.int32, (n, n), 0)
    j = lax.broadcasted_iota(jnp.int32, (n, n), 1)
    return (i == j).astype(F32)


def _unit_lower_inverse_raw(lmat):
    n = lmat.shape[0]
    x = -lmat
    inv = _eye(n) + x
    p = x
    k = 1
    while 2 * k < n:
        p = _hdot(p, p)
        inv = inv + _hdot(inv, p)
        k *= 2
    return inv


@jax.custom_vjp
def unit_lower_inverse(lmat):
    return _unit_lower_inverse_raw(lmat)


def _uli_fwd(lmat):
    inv = _unit_lower_inverse_raw(lmat)
    return inv, inv


def _uli_bwd(inv, g):
    m = lax.dot_general(inv, g, (((0,), (0,)), ((), ())), precision=HI, preferred_element_type=F32)
    return (-lax.dot_general(m, inv, (((1,), (1,)), ((), ())), precision=HI, preferred_element_type=F32),)


unit_lower_inverse.defvjp(_uli_fwd, _uli_bwd)


def _bdot(a, b, dims=(((1,), (0,)), ((), ()))):
    return lax.dot_general(a.astype(BF16), b.astype(BF16), dims, preferred_element_type=F32)


_NT = (((1,), (1,)), ((), ()))
_TN = (((0,), (0,)), ((), ()))


def _softplus(x):
    return jnp.maximum(x, 0.0) + jnp.log1p(jnp.exp(-jnp.abs(x)))


def _expm1_nonpos(x):
    poly = x * (1.0 + x * (0.5 + x * (1.0 / 6 + x * (1.0 / 24 + x * (1.0 / 120 + x * (1.0 / 720))))))
    return jnp.where(x > -0.25, poly, jnp.exp(x) - 1.0)


def _rms(x):
    return x * lax.rsqrt(jnp.mean(x * x, axis=-1, keepdims=True) + EPS)


def _causal_conv(x, w, width):
    y = w[width - 1:width, :] * x
    for j in range(width - 1):
        y = y + w[j:j + 1, :] * shift_rows(x, width - 1 - j)
    return y


def _norm_fn(x, g):
    return _rms(x) * g


def _ffn_act_fn(ug, uv, wg, wv, bg, bv):
    return jax.nn.silu(_causal_conv(ug, wg, 3) + bg) * (_causal_conv(uv, wv, 3) + bv)


def _gdn_conv_fn(x, w):
    return jax.nn.silu(_causal_conv(x, w, 4))


def _lru_fn(gate, x, cw, cb, wa, ba, wx, bx, lam):
    xr = _causal_conv(x, cw, 4) + cb
    r = jax.nn.sigmoid(_bdot(xr, wa) + ba)
    i = jax.nn.sigmoid(_bdot(xr, wx) + bx)
    log_a = -LRU_C * r * _softplus(-lam)
    a = jnp.exp(log_a)
    u = jnp.sqrt(-_expm1_nonpos(2.0 * log_a)) * (i * xr)
    hs = lin_scan(a, u)
    return jax.nn.gelu(gate) * hs


def _ret_fn(q, k, v, gate, state, cos2, sin2, dmask, ktail, qdec, cdec):
    qr = q * cos2 + swap_halves(q) * sin2
    kr = (k * cos2 + swap_halves(k) * sin2) * (HEAD ** -0.5)
    scores = _bdot(qr, kr, _NT) * dmask
    o = _bdot(scores, v) + _bdot(qr * qdec, state)
    new_state = state * cdec + _bdot(kr * ktail, v, _TN)
    return _rms(o) * jax.nn.silu(gate), new_state


def _pick_lane(x, lane_idx):
    lane = lax.broadcasted_iota(jnp.int32, x.shape, 1)
    return jnp.sum(jnp.where(lane == lane_idx, x, 0.0), axis=1, keepdims=True)


def _l2norm(x):
    return x * lax.rsqrt(jnp.sum(x * x, axis=-1, keepdims=True) + EPS)


def _gdn_fn(qc, kc, vc, gate, small, a_log, dt_bias, gain, state, head):
    c = GDN_CHUNK
    q = _l2norm(qc) * (HEAD ** -0.5)
    k = _l2norm(kc)
    beta = jax.nn.sigmoid(_pick_lane(small, head))
    a_in = _pick_lane(small, head + N_HEADS)
    g = -jnp.exp(_pick_lane(a_log, head)) * _softplus(a_in + _pick_lane(dt_bias, head))
    i = lax.broadcasted_iota(jnp.int32, (c, c), 0)
    j = lax.broadcasted_iota(jnp.int32, (c, c), 1)
    tril = i >= j
    gc_rows = _hdot(tril.astype(F32), jnp.broadcast_to(g, (c, c)))
    gc = gc_rows[:, :1]
    decay = jnp.where(tril, jnp.exp(jnp.where(tril, gc_rows - gc_rows.T, 0.0)), 0.0)
    kb = k * beta
    lmat = jnp.where(i > j, _bdot(kb, k, _NT) * decay, 0.0)
    inv = unit_lower_inverse(lmat)
    u = _hdot(inv, vc * beta)
    w = _hdot(inv, kb * jnp.exp(gc))
    attn = jnp.where(tril, _bdot(q, k, _NT) * decay, 0.0)
    g_last = jnp.sum(g, axis=0, keepdims=True)
    v_new = u - _bdot(w, state)
    o = _bdot(q * jnp.exp(gc), state) + _bdot(attn, v_new)
    new_state = state * jnp.exp(g_last) + _bdot(k * jnp.exp(g_last - gc), v_new, _TN)
    return _rms(o) * gain * jax.nn.silu(gate), new_state


def _final_fn(h, g, target):
    y = _rms(h) * g
    return 0.5 * jnp.sum(jnp.mean(jnp.square(y - target), axis=-1, keepdims=True), axis=0, keepdims=True)


def _tile(n, candidates):
    for t in candidates:
        if n % t == 0:
            return t
    raise ValueError(f"no tile for {n}")


def matmul(a, b, *, ta=False, tb=False, add=None, out_dtype=F32, tm=None, tn=None, name):
    m = a.shape[1] if ta else a.shape[0]
    k = a.shape[0] if ta else a.shape[1]
    n = b.shape[0] if tb else b.shape[1]
    assert k == (b.shape[1] if tb else b.shape[0])
    tm = tm or _tile(m, (1024, 512, 1408, 256, 128))
    tn = tn or _tile(n, (512, 1408, 256, 128))
    dims = (((0 if ta else 1,), (1 if tb else 0,)), ((), ()))

    def body(*refs):
        if add is None:
            a_ref, b_ref, o_ref = refs
        else:
            a_ref, b_ref, r_ref, o_ref = refs
        acc = lax.dot_general(a_ref[...].astype(BF16), b_ref[...].astype(BF16), dims, preferred_element_type=F32)
        if add is not None:
            acc = acc + r_ref[...]
        o_ref[...] = acc.astype(out_dtype)

    a_spec = pl.BlockSpec((k, tm), lambda i, j: (0, i)) if ta else pl.BlockSpec((tm, k), lambda i, j: (i, 0))
    b_spec = pl.BlockSpec((tn, k), lambda i, j: (j, 0)) if tb else pl.BlockSpec((k, tn), lambda i, j: (0, j))
    o_spec = pl.BlockSpec((tm, tn), lambda i, j: (i, j))
    in_specs, args = [a_spec, b_spec], [a, b]
    if add is not None:
        in_specs.append(o_spec)
        args.append(add)
    return pl.pallas_call(body, out_shape=_sds((m, n), out_dtype), grid=(m // tm, n // tn), in_specs=in_specs,
                          out_specs=o_spec, compiler_params=_params(), name=name)(*args)


ROW_TILE = 256


def norm_fwd(x, g, *, name):
    t, d = x.shape

    def body(x_ref, g_ref, o_ref):
        o_ref[...] = _norm_fn(x_ref[...], g_ref[...]).astype(BF16)

    return pl.pallas_call(body, out_shape=_sds((t, d), BF16), grid=(t // ROW_TILE,),
                          in_specs=[pl.BlockSpec((ROW_TILE, d), lambda i: (i, 0)), pl.BlockSpec((1, d), lambda i: (0, 0))],
                          out_specs=pl.BlockSpec((ROW_TILE, d), lambda i: (i, 0)), compiler_params=_params(), name=name)(x, g)


def norm_bwd(x, g, dy, dres, *, name):
    t, d = x.shape

    def body(x_ref, g_ref, dy_ref, dres_ref, dx_ref, dg_ref):
        _, vjp = jax.vjp(_norm_fn, x_ref[...], g_ref[...])
        dx, dg = vjp(dy_ref[...])
        dx_ref[...] = dx + dres_ref[...]

        @pl.when(pl.program_id(0) == 0)
        def _():
            dg_ref[...] = jnp.zeros_like(dg_ref)

        dg_ref[...] += dg

    row = pl.BlockSpec((ROW_TILE, d), lambda i: (i, 0))
    vec = pl.BlockSpec((1, d), lambda i: (0, 0))
    return pl.pallas_call(body, out_shape=(_sds((t, d), F32), _sds((1, d), F32)), grid=(t // ROW_TILE,),
                          in_specs=[row, vec, row, row], out_specs=(row, vec), compiler_params=_params(), name=name)(x, g, dy, dres)


def final_fwd_bwd(h, g, target, *, name):
    t, d = h.shape

    def body(h_ref, g_ref, t_ref, loss_ref, dh_ref, dg_ref):
        tgt = t_ref[...]
        loss, vjp = jax.vjp(lambda hh, gg: _final_fn(hh, gg, tgt), h_ref[...], g_ref[...])
        dh, dg = vjp(jnp.ones((1, 1), F32))
        dh_ref[...] = dh

        @pl.when(pl.program_id(0) == 0)
        def _():
            dg_ref[...] = jnp.zeros_like(dg_ref)
            loss_ref[...] = jnp.zeros_like(loss_ref)

        dg_ref[...] += dg
        loss_ref[...] += jnp.broadcast_to(loss, loss_ref.shape)

    row = pl.BlockSpec((ROW_TILE, d), lambda i: (i, 0))
    vec = pl.BlockSpec((1, d), lambda i: (0, 0))
    return pl.pallas_call(body, out_shape=(_sds((1, LANES), F32), _sds((t, d), F32), _sds((1, d), F32)), grid=(t // ROW_TILE,),
                          in_specs=[row, vec, row], out_specs=(pl.BlockSpec((1, LANES), lambda i: (0, 0)), row, vec),
                          compiler_params=_params(), name=name)(h, g, target)


FFN_FWD_COLS = 256
FFN_BWD_COLS = 128


def ffn_act_fwd(u, cw, cb, *, name):
    t = u.shape[0]
    w = FFN_FWD_COLS
    nb = D_FF // w

    def body(ug_ref, uv_ref, wg_ref, wv_ref, bg_ref, bv_ref, o_ref):
        o_ref[...] = _ffn_act_fn(ug_ref[...], uv_ref[...], wg_ref[...], wv_ref[...], bg_ref[...], bv_ref[...]).astype(BF16)

    def col(rows, off):
        return pl.BlockSpec((rows, w), lambda j: (0, j + off))

    return pl.pallas_call(body, out_shape=_sds((t, D_FF), BF16), grid=(nb,),
                          in_specs=[col(t, 0), col(t, nb), col(3, 0), col(3, nb), col(1, 0), col(1, nb)],
                          out_specs=col(t, 0), compiler_params=_params(), name=name)(u, u, cw, cw, cb, cb)


def ffn_act_bwd(u, cw, cb, da, *, name):
    t = u.shape[0]
    w = FFN_BWD_COLS
    nb = D_FF // w

    def body(ug_ref, uv_ref, wg_ref, wv_ref, bg_ref, bv_ref, da_ref, dug_ref, duv_ref, dwg_ref, dwv_ref, dbg_ref, dbv_ref):
        _, vjp = jax.vjp(_ffn_act_fn, ug_ref[...], uv_ref[...], wg_ref[...], wv_ref[...], bg_ref[...], bv_ref[...])
        dug, duv, dwg, dwv, dbg, dbv = vjp(da_ref[...])
        dug_ref[...] = dug.astype(BF16)
        duv_ref[...] = duv.astype(BF16)
        dwg_ref[...] = dwg
        dwv_ref[...] = dwv
        dbg_ref[...] = dbg
        dbv_ref[...] = dbv

    def col(rows, off):
        return pl.BlockSpec((rows, w), lambda j: (0, j + off))

    outs = pl.pallas_call(
        body, out_shape=(_sds((t, D_FF), BF16), _sds((t, D_FF), BF16), _sds((3, D_FF), F32), _sds((3, D_FF), F32),
                         _sds((1, D_FF), F32), _sds((1, D_FF), F32)),
        grid=(nb,), in_specs=[col(t, 0), col(t, nb), col(3, 0), col(3, nb), col(1, 0), col(1, nb), col(t, 0)],
        out_specs=(col(t, 0), col(t, 0), col(3, 0), col(3, 0), col(1, 0), col(1, 0)), compiler_params=_params(), name=name,
    )(u, u, cw, cw, cb, cb, da)
    dug, duv, dwg, dwv, dbg, dbv = outs
    return dug, duv, jnp.concatenate([dwg, dwv], axis=1), jnp.concatenate([dbg, dbv], axis=1)


GDN_CONV_COLS = 256
GDN_CONV_OFF = 4 * GROUP


def gdn_conv_fwd(p, cw, *, name):
    t = p.shape[0]
    w = GDN_CONV_COLS
    nb = 3 * GROUP // w
    off = GDN_CONV_OFF // w

    def body(x_ref, w_ref, o_ref):
        o_ref[...] = _gdn_conv_fn(x_ref[...], w_ref[...])

    return pl.pallas_call(body, out_shape=_sds((t, 3 * GROUP), F32), grid=(nb,),
                          in_specs=[pl.BlockSpec((t, w), lambda j: (0, j + off)), pl.BlockSpec((4, w), lambda j: (0, j))],
                          out_specs=pl.BlockSpec((t, w), lambda j: (0, j)), compiler_params=_params(), name=name)(p, cw)


def gdn_conv_bwd(p, cw, dc, *, name):
    t = p.shape[0]
    w = GDN_CONV_COLS
    nb = 3 * GROUP // w
    off = GDN_CONV_OFF // w

    def body(x_ref, w_ref, dc_ref, dx_ref, dw_ref):
        _, vjp = jax.vjp(_gdn_conv_fn, x_ref[...], w_ref[...])
        dx, dw = vjp(dc_ref[...])
        dx_ref[...] = dx.astype(BF16)
        dw_ref[...] = dw

    blk = pl.BlockSpec((t, w), lambda j: (0, j))
    wblk = pl.BlockSpec((4, w), lambda j: (0, j))
    return pl.pallas_call(body, out_shape=(_sds((t, 3 * GROUP), BF16), _sds((4, 3 * GROUP), F32)), grid=(nb,),
                          in_specs=[pl.BlockSpec((t, w), lambda j: (0, j + off)), wblk, blk], out_specs=(blk, wblk),
                          compiler_params=_params(), name=name)(p, cw, dc)


def _lru_specs(t):
    w = D_MODEL // LRU_BLOCKS
    gate = pl.BlockSpec((t, w), lambda j: (0, j))
    xin = pl.BlockSpec((t, w), lambda j: (0, j + LRU_BLOCKS))
    cw = pl.BlockSpec((4, w), lambda j: (0, j))
    vec = pl.BlockSpec((1, w), lambda j: (0, j))
    mat = pl.BlockSpec((None, w, w), lambda j: (j, 0, 0))
    return gate, xin, cw, vec, mat


def lru_fwd(gx, cw, cb, wa, ba, wx, bx, lam, *, name):
    t = gx.shape[0]
    gate, xin, cws, vec, mat = _lru_specs(t)

    def body(g_ref, x_ref, cw_ref, cb_ref, wa_ref, ba_ref, wx_ref, bx_ref, lam_ref, o_ref):
        o_ref[...] = _lru_fn(g_ref[...], x_ref[...], cw_ref[...], cb_ref[...], wa_ref[...], ba_ref[...], wx_ref[...],
                             bx_ref[...], lam_ref[...]).astype(BF16)

    return pl.pallas_call(body, out_shape=_sds((t, D_MODEL), BF16), grid=(LRU_BLOCKS,),
                          in_specs=[gate, xin, cws, vec, mat, vec, mat, vec, vec], out_specs=gate,
                          compiler_params=_params(), name=name)(gx, gx, cw, cb, wa, ba, wx, bx, lam)


def lru_bwd(gx, cw, cb, wa, ba, wx, bx, lam, dy, *, name):
    t = gx.shape[0]
    gate, xin, cws, vec, mat = _lru_specs(t)

    def body(g_ref, x_ref, cw_ref, cb_ref, wa_ref, ba_ref, wx_ref, bx_ref, lam_ref, dy_ref,
             dg_ref, dx_ref, dcw_ref, dcb_ref, dwa_ref, dba_ref, dwx_ref, dbx_ref, dlam_ref):
        _, vjp = jax.vjp(_lru_fn, g_ref[...], x_ref[...], cw_ref[...], cb_ref[...], wa_ref[...], ba_ref[...], wx_ref[...],
                         bx_ref[...], lam_ref[...])
        dg, dx, dcw, dcb, dwa, dba, dwx, dbx, dlam = vjp(dy_ref[...])
        dg_ref[...] = dg.astype(BF16)
        dx_ref[...] = dx.astype(BF16)
        dcw_ref[...] = dcw
        dcb_ref[...] = dcb
        dwa_ref[...] = dwa
        dba_ref[...] = dba
        dwx_ref[...] = dwx
        dbx_ref[...] = dbx
        dlam_ref[...] = dlam

    d = D_MODEL
    w = d // LRU_BLOCKS
    out_shape = (_sds((t, d), BF16), _sds((t, d), BF16), _sds((4, d), F32), _sds((1, d), F32), _sds((LRU_BLOCKS, w, w), F32),
                 _sds((1, d), F32), _sds((LRU_BLOCKS, w, w), F32), _sds((1, d), F32), _sds((1, d), F32))
    return pl.pallas_call(body, out_shape=out_shape, grid=(LRU_BLOCKS,),
                          in_specs=[gate, xin, cws, vec, mat, vec, mat, vec, vec, gate],
                          out_specs=(gate, gate, cws, vec, mat, vec, mat, vec, vec), compiler_params=_params(), name=name,
                          )(gx, gx, cw, cb, wa, ba, wx, bx, lam, dy)


def _ret_tables():
    half = HEAD // 2
    inv_freq = (np.float32(ROPE_BASE) ** (-np.arange(half, dtype=np.float32) / np.float32(half))).astype(np.float32)
    ang = (np.arange(SEQ, dtype=np.float32)[:, None] * inv_freq[None, :]).astype(np.float64)
    cos2 = np.concatenate([np.cos(ang), np.cos(ang)], axis=1).astype(np.float32)
    sin2 = np.concatenate([-np.sin(ang), np.sin(ang)], axis=1).astype(np.float32)
    c = RET_CHUNK
    log_gamma = np.log1p(-np.exp2(-5.0 - np.arange(N_HEADS, dtype=np.float64)))
    idx = np.arange(c, dtype=np.float64)
    rel = idx[:, None] - idx[None, :]
    dmask = np.where(rel >= 0, np.exp(log_gamma[:, None, None] * np.maximum(rel, 0.0)), 0.0)
    ones = np.ones((N_HEADS, c, HEAD))
    ktail = np.exp(log_gamma[:, None] * (c - 1 - idx))[:, :, None] * ones
    qdec = np.exp(log_gamma[:, None] * (idx + 1.0))[:, :, None] * ones
    cdec = np.exp(log_gamma * c)[:, None, None] * ones
    return tuple(jnp.asarray(a, F32) for a in (cos2, sin2, dmask, ktail, qdec, cdec))


def _ret_specs(rev):
    c = RET_CHUNK
    nc = SEQ // c

    def n_of(n):
        return nc - 1 - n if rev else n

    def pcol(off):
        return pl.BlockSpec((c, HEAD), lambda h, n: (n_of(n), h + off))

    tab = pl.BlockSpec((c, HEAD), lambda h, n: (n_of(n), 0))
    const = pl.BlockSpec((None, c, HEAD), lambda h, n: (h, 0, 0))
    state = pl.BlockSpec((None, None, HEAD, HEAD), lambda h, n: (h, n_of(n), 0, 0))
    return pcol, tab, const, state, nc


def ret_fwd(p, tables, *, name):
    pcol, tab, const, state, nc = _ret_specs(False)

    def body(q_ref, k_ref, v_ref, g_ref, cos_ref, sin_ref, dm_ref, kt_ref, qd_ref, cd_ref, y_ref, st_ref, s_scr):
        @pl.when(pl.program_id(1) == 0)
        def _():
            s_scr[...] = jnp.zeros_like(s_scr)

        s = s_scr[...]
        st_ref[...] = s
        y, s_new = _ret_fn(q_ref[...], k_ref[...], v_ref[...], g_ref[...], s, cos_ref[...], sin_ref[...], dm_ref[...],
                           kt_ref[...], qd_ref[...], cd_ref[...])
        y_ref[...] = y.astype(BF16)
        s_scr[...] = s_new

    return pl.pallas_call(
        body, out_shape=(_sds((SEQ, GROUP), BF16), _sds((N_HEADS, nc, HEAD, HEAD), F32)), grid=(N_HEADS, nc),
        in_specs=[pcol(0), pcol(4), pcol(8), pcol(12), tab, tab, const, const, const, const],
        out_specs=(pcol(0), state), scratch_shapes=[pltpu.VMEM((HEAD, HEAD), F32)], compiler_params=_params(), name=name,
    )(p, p, p, p, *tables)


def ret_bwd(p, tables, states, dy, *, name):
    pcol, tab, const, state, nc = _ret_specs(True)

    def body(q_ref, k_ref, v_ref, g_ref, cos_ref, sin_ref, dm_ref, kt_ref, qd_ref, cd_ref, st_ref, dy_ref,
             dq_ref, dk_ref, dv_ref, dg_ref, ds_scr):
        @pl.when(pl.program_id(1) == 0)
        def _():
            ds_scr[...] = jnp.zeros_like(ds_scr)

        consts = (cos_ref[...], sin_ref[...], dm_ref[...], kt_ref[...], qd_ref[...], cd_ref[...])
        _, vjp = jax.vjp(lambda q, k, v, g, s: _ret_fn(q, k, v, g, s, *consts), q_ref[...], k_ref[...], v_ref[...],
                         g_ref[...], st_ref[...])
        dq, dk, dv, dg, ds = vjp((dy_ref[...], ds_scr[...]))
        dq_ref[...] = dq.astype(BF16)
        dk_ref[...] = dk.astype(BF16)
        dv_ref[...] = dv.astype(BF16)
        dg_ref[...] = dg.astype(BF16)
        ds_scr[...] = ds

    out = _sds((SEQ, GROUP), BF16)
    return pl.pallas_call(
        body, out_shape=(out, out, out, out), grid=(N_HEADS, nc),
        in_specs=[pcol(0), pcol(4), pcol(8), pcol(12), tab, tab, const, const, const, const, state, pcol(0)],
        out_specs=(pcol(0), pcol(0), pcol(0), pcol(0)), scratch_shapes=[pltpu.VMEM((HEAD, HEAD), F32)],
        compiler_params=_params(), name=name,
    )(p, p, p, p, *tables, states, dy)


def _gdn_specs(rev):
    c = GDN_CHUNK
    nc = SEQ // c

    def n_of(n):
        return nc - 1 - n if rev else n

    def col(off):
        return pl.BlockSpec((c, HEAD), lambda h, n: (n_of(n), h + off))

    small = pl.BlockSpec((c, LANES), lambda h, n: (n_of(n), 0))
    vec = pl.BlockSpec((1, LANES), lambda h, n: (0, 0))
    state = pl.BlockSpec((None, None, HEAD, HEAD), lambda h, n: (h, n_of(n), 0, 0))
    return col, small, vec, state, nc


def gdn_fwd(conv, p, small, a_log, dt_bias, gain, *, name):
    col, sm, vec, state, nc = _gdn_specs(False)

    def body(q_ref, k_ref, v_ref, g_ref, sm_ref, al_ref, dt_ref, gn_ref, y_ref, st_ref, s_scr):
        @pl.when(pl.program_id(1) == 0)
        def _():
            s_scr[...] = jnp.zeros_like(s_scr)

        s = s_scr[...]
        st_ref[...] = s
        y, s_new = _gdn_fn(q_ref[...], k_ref[...], v_ref[...], g_ref[...], sm_ref[...], al_ref[...], dt_ref[...], gn_ref[...],
                           s, pl.program_id(0))
        y_ref[...] = y.astype(BF16)
        s_scr[...] = s_new

    return pl.pallas_call(
        body, out_shape=(_sds((SEQ, GROUP), BF16), _sds((N_HEADS, nc, HEAD, HEAD), F32)), grid=(N_HEADS, nc),
        in_specs=[col(0), col(4), col(8), col(28), sm, vec, vec, vec], out_specs=(col(0), state),
        scratch_shapes=[pltpu.VMEM((HEAD, HEAD), F32)], compiler_params=_params(), name=name,
    )(conv, conv, conv, p, small, a_log, dt_bias, gain)


def gdn_bwd(conv, p, small, a_log, dt_bias, gain, states, dy, *, name):
    col, sm, vec, state, nc = _gdn_specs(True)

    def body(q_ref, k_ref, v_ref, g_ref, sm_ref, al_ref, dt_ref, gn_ref, st_ref, dy_ref,
             dq_ref, dk_ref, dv_ref, dg_ref, dsm_ref, dal_ref, ddt_ref, dgn_ref, ds_scr):
        head = pl.program_id(0)

        @pl.when(pl.program_id(1) == 0)
        def _():
            ds_scr[...] = jnp.zeros_like(ds_scr)
            dal_ref[...] = jnp.zeros_like(dal_ref)
            ddt_ref[...] = jnp.zeros_like(ddt_ref)
            dgn_ref[...] = jnp.zeros_like(dgn_ref)

        _, vjp = jax.vjp(lambda *a: _gdn_fn(*a, head), q_ref[...], k_ref[...], v_ref[...], g_ref[...], sm_ref[...],
                         al_ref[...], dt_ref[...], gn_ref[...], st_ref[...])
        dq, dk, dv, dg, dsm, dal, ddt, dgn, ds = vjp((dy_ref[...], ds_scr[...]))
        dq_ref[...] = dq
        dk_ref[...] = dk
        dv_ref[...] = dv
        dg_ref[...] = dg.astype(BF16)
        dsm_ref[...] = dsm
        dal_ref[...] += dal
        ddt_ref[...] += ddt
        dgn_ref[...] += dgn
        ds_scr[...] = ds

    f = _sds((SEQ, GROUP), F32)
    hvec = pl.BlockSpec((None, 1, LANES), lambda h, n: (h, 0, 0))
    dsmall = pl.BlockSpec((None, GDN_CHUNK, LANES), lambda h, n: (h, nc - 1 - n, 0))
    pv = _sds((N_HEADS, 1, LANES), F32)
    return pl.pallas_call(
        body, out_shape=(f, f, f, _sds((SEQ, GROUP), BF16), _sds((N_HEADS, SEQ, LANES), F32), pv, pv, pv), grid=(N_HEADS, nc),
        in_specs=[col(0), col(4), col(8), col(28), sm, vec, vec, vec, state, col(4)],
        out_specs=(col(0), col(0), col(0), col(0), dsmall, hvec, hvec, hvec), scratch_shapes=[pltpu.VMEM((HEAD, HEAD), F32)],
        compiler_params=_params(), name=name,
    )(conv, conv, conv, p, small, a_log, dt_bias, gain, states, dy)


PACK_ROW_TILE = 1024


def adamw(w, g, m, v, *, name):
    r = w.shape[0]
    tr = _tile(r, (PACK_ROW_TILE, 256, 128, 64, 32, 16, 8))

    def body(w_ref, g_ref, m_ref, v_ref, d_ref, nm_ref, nv_ref):
        gg = g_ref[...]
        nm = ADAM_B1 * m_ref[...] + (1.0 - ADAM_B1) * gg
        nv = ADAM_B2 * v_ref[...] + (1.0 - ADAM_B2) * jnp.square(gg)
        m_hat = nm / (1.0 - ADAM_B1 ** ADAM_STEP)
        v_hat = nv / (1.0 - ADAM_B2 ** ADAM_STEP)
        d_ref[...] = -ADAM_LR * (m_hat / (jnp.sqrt(v_hat) + ADAM_EPS) + ADAM_WD * w_ref[...])
        nm_ref[...] = nm
        nv_ref[...] = nv

    blk = pl.BlockSpec((tr, LANES), lambda i: (i, 0))
    o = _sds((r, LANES), F32)
    return pl.pallas_call(body, out_shape=(o, o, o), grid=(r // tr,), in_specs=[blk] * 4, out_specs=(blk, blk, blk),
                          compiler_params=_params(), name=name)(w, g, m, v)


def sum_slots(land, *, name):
    n, r, _ = land.shape
    tr = _tile(r, (PACK_ROW_TILE, 256, 128, 64, 32, 16, 8))

    def body(l_ref, o_ref):
        acc = l_ref[0]
        for i in range(1, n):
            acc = acc + l_ref[i]
        o_ref[...] = acc

    return pl.pallas_call(body, out_shape=_sds((r, LANES), F32), grid=(r // tr,),
                          in_specs=[pl.BlockSpec((n, tr, LANES), lambda i: (0, i, 0))],
                          out_specs=pl.BlockSpec((tr, LANES), lambda i: (i, 0)), compiler_params=_params(), name=name)(land)


def add_core_halves(g2, land, *, name):
    _, ns, rh, _ = g2.shape
    tr = _tile(rh, (PACK_ROW_TILE, 256, 128, 64, 32, 16, 8))

    def body(c_ref, a_ref, b_ref, o_ref):
        o_ref[...] = a_ref[...] + b_ref[...]

    c = lax.axis_index("c").astype(jnp.int32).reshape(1)
    gs = pltpu.PrefetchScalarGridSpec(
        num_scalar_prefetch=1, grid=(ns, rh // tr),
        in_specs=[pl.BlockSpec((None, None, tr, LANES), lambda s, i, cr: (cr[0], s, i, 0)),
                  pl.BlockSpec((None, tr, LANES), lambda s, i, cr: (s, i, 0))],
        out_specs=pl.BlockSpec((None, tr, LANES), lambda s, i, cr: (s, i, 0)))
    return pl.pallas_call(body, out_shape=_sds((ns, rh, LANES), F32), grid_spec=gs, compiler_params=_params(), name=name)(c, g2, land)


_ANY = pl.BlockSpec(memory_space=pl.ANY)


def xy_exchange(src, *, scatter, name):
    rh = src.shape[1]

    def body(src_ref, land_ref, send_sems, recv_sems, loc_sem):
        x, y, c = lax.axis_index("x"), lax.axis_index("y"), lax.axis_index("c")
        mine = 2 * x + y
        peers = [(1 - x, y), (x, 1 - y), (1 - x, 1 - y)]

        def piece(shard):
            return src_ref.at[shard] if scatter else src_ref.at[c]

        def copy(k, px, py, dst_slot):
            return pltpu.make_async_remote_copy(src_ref=piece(2 * px + py), dst_ref=land_ref.at[dst_slot], send_sem=send_sems.at[k],
                                                recv_sem=recv_sems.at[k], device_id=(px, py, c), device_id_type=MESH)

        keep = pltpu.make_async_copy(piece(mine), land_ref.at[mine], loc_sem)
        keep.start()
        sends = [copy(k, px, py, mine) for k, (px, py) in enumerate(peers)]
        for cp in sends:
            cp.start()
        for cp in sends:
            cp.wait_send()
        for k, (px, py) in enumerate(peers):
            copy(k, px, py, 2 * px + py).wait_recv()
        keep.wait()

    return pl.pallas_call(body, out_shape=_sds((N_SHARD, rh, LANES), src.dtype), in_specs=[_ANY], out_specs=_ANY,
                          scratch_shapes=[pltpu.SemaphoreType.DMA((3,)), pltpu.SemaphoreType.DMA((3,)), pltpu.SemaphoreType.DMA(())],
                          name=name)(src)


def core_exchange(src, *, send_other_half, name):
    def body(src_ref, out_ref, send_sem, recv_sem, loc_sem):
        x, y, c = lax.axis_index("x"), lax.axis_index("y"), lax.axis_index("c")
        if send_other_half:
            cp = pltpu.make_async_remote_copy(src_ref=src_ref.at[1 - c], dst_ref=out_ref, send_sem=send_sem, recv_sem=recv_sem,
                                              device_id=(x, y, 1 - c), device_id_type=MESH)
            cp.start()
            cp.wait_send()
            cp.wait_recv()
        else:
            keep = pltpu.make_async_copy(src_ref, out_ref.at[c], loc_sem)
            keep.start()
            cp = pltpu.make_async_remote_copy(src_ref=src_ref, dst_ref=out_ref.at[c], send_sem=send_sem, recv_sem=recv_sem,
                                              device_id=(x, y, 1 - c), device_id_type=MESH)
            cp.start()
            cp.wait_send()
            pltpu.make_async_remote_copy(src_ref=src_ref, dst_ref=out_ref.at[1 - c], send_sem=send_sem, recv_sem=recv_sem,
                                         device_id=(x, y, 1 - c), device_id_type=MESH).wait_recv()
            keep.wait()

    out_shape = _sds(src.shape[1:], src.dtype) if send_other_half else _sds((2,) + src.shape, src.dtype)
    return pl.pallas_call(body, out_shape=out_shape, in_specs=[_ANY], out_specs=_ANY,
                          scratch_shapes=[pltpu.SemaphoreType.DMA(()), pltpu.SemaphoreType.DMA(()), pltpu.SemaphoreType.DMA(())],
                          name=name)(src)


def gather_all(src, *, name):
    def body(src_ref, out_ref, send_sems, recv_sems, loc_sem):
        x, y, c = lax.axis_index("x"), lax.axis_index("y"), lax.axis_index("c")
        mine = 4 * x + 2 * y + c

        def peer(mask):
            return (1 - x if mask & 4 else x, 1 - y if mask & 2 else y, 1 - c if mask & 1 else c)

        def copy(mask, slot):
            return pltpu.make_async_remote_copy(src_ref=src_ref, dst_ref=out_ref.at[slot], send_sem=send_sems.at[mask - 1],
                                                recv_sem=recv_sems.at[mask - 1], device_id=peer(mask), device_id_type=MESH)

        keep = pltpu.make_async_copy(src_ref, out_ref.at[mine], loc_sem)
        keep.start()
        sends = [copy(mask, mine) for mask in range(1, 8)]
        for cp in sends:
            cp.start()
        for cp in sends:
            cp.wait_send()
        for mask in range(1, 8):
            px, py, pc = peer(mask)
            copy(mask, 4 * px + 2 * py + pc).wait_recv()
        keep.wait()

    return pl.pallas_call(body, out_shape=_sds((8,) + src.shape, src.dtype), in_specs=[_ANY], out_specs=_ANY,
                          scratch_shapes=[pltpu.SemaphoreType.DMA((7,)), pltpu.SemaphoreType.DMA((7,)), pltpu.SemaphoreType.DMA(())],
                          name=name)(src)


def _pack_rows(n_elems, row_multiple):
    rows = -(-n_elems // LANES)
    return -(-rows // row_multiple) * row_multiple


def _pack(arrays, rows, dtype):
    flat = jnp.concatenate([a.reshape(-1).astype(dtype) for a in arrays])
    return jnp.pad(flat, (0, rows * LANES - flat.shape[0])).reshape(rows, LANES)


def _unpack(packed, shapes):
    flat = packed.reshape(-1)
    out, off = [], 0
    for s in shapes:
        n = int(np.prod(s))
        out.append(flat[off:off + n].reshape(s))
        off += n
    return out


def all_gather_shards(shards, axes, dtype, row_multiple, tag):
    shapes = [s.shape for s in shards]
    rows = _pack_rows(sum(int(np.prod(s)) for s in shapes), row_multiple)
    packed = _pack(shards, rows, dtype).reshape(2, rows // 2, LANES)
    land = xy_exchange(packed, scatter=False, name=f"gather_xy_{tag}")
    both = core_exchange(land, send_other_half=False, name=f"gather_c_{tag}")
    per_shard = jnp.swapaxes(both, 0, 1).reshape(N_SHARD, rows, LANES)
    pieces = [_unpack(per_shard[s], shapes) for s in range(N_SHARD)]
    return [jnp.concatenate([pieces[s][i] for s in range(N_SHARD)], axis=ax) for i, ax in enumerate(axes)]


def reduce_scatter_grads(grads, axes, rows, tag):
    parts = [jnp.split(g, N_SHARD, axis=ax) for g, ax in zip(grads, axes)]
    packed = jnp.stack([_pack([p[s] for p in parts], rows, F32) for s in range(N_SHARD)])
    g2 = jnp.swapaxes(packed.reshape(N_SHARD, 2, rows // 2, LANES), 0, 1)
    land_a = core_exchange(g2, send_other_half=True, name=f"reduce_c_{tag}")
    chip_sum = add_core_halves(g2, land_a, name=f"reduce_add_c_{tag}")
    land_b = xy_exchange(chip_sum, scatter=True, name=f"reduce_xy_{tag}")
    half = sum_slots(land_b, name=f"reduce_add_xy_{tag}")
    return core_exchange(half, send_other_half=False, name=f"reduce_bcast_c_{tag}").reshape(rows, LANES)


def _ffn_layer_fwd(h, norm_g, w_up, cw, cb, w_down, tag):
    hn = norm_fwd(h, norm_g, name=f"ffn_norm_{tag}")
    u = matmul(hn, w_up, name=f"ffn_up_{tag}")
    act = ffn_act_fwd(u, cw, cb, name=f"ffn_act_{tag}")
    out = matmul(act, w_down, add=h, name=f"ffn_down_{tag}")
    return out, (h, hn, u, act)


def _ffn_layer_bwd(saved, dout, norm_g, w_up, cw, cb, w_down, tag):
    h, hn, u, act = saved
    dact = matmul(dout, w_down, tb=True, name=f"ffn_down_dx_{tag}")
    d_w_down = matmul(act, dout, ta=True, name=f"ffn_down_dw_{tag}")
    dug, duv, dcw, dcb = ffn_act_bwd(u, cw, cb, dact, name=f"ffn_act_bwd_{tag}")
    du = jnp.concatenate([dug, duv], axis=1)
    dhn = matmul(du, w_up, tb=True, name=f"ffn_up_dx_{tag}")
    d_w_up = matmul(hn, du, ta=True, name=f"ffn_up_dw_{tag}")
    dh, dg = norm_bwd(h, norm_g, dhn, dout, name=f"ffn_norm_bwd_{tag}")
    return dh, dg, d_w_up, dcw, dcb, d_w_down


def local_step(x, target, w):
    g = {}
    tables = _ret_tables()
    w_in = w["ret_gdn_w_in"]
    w_main = w_in[:, :MIX_MAIN]
    w_small = jnp.pad(w_in[:, MIX_MAIN:], ((0, 0), (0, LANES - 2 * N_HEADS)))
    a_log = jnp.pad(w["gdn_a_log"], ((0, 0), (0, LANES - N_HEADS)))
    dt_bias = jnp.pad(w["gdn_dt_bias"], ((0, 0), (0, LANES - N_HEADS)))

    hn0 = norm_fwd(x, w["norm_mix"][0:1], name="mix0_norm")
    p = matmul(hn0, w_main, name="mix0_in")
    small = matmul(hn0, w_small, name="mix0_in_small")
    y_ret, s_ret = ret_fwd(p, tables, name="ret_fwd")
    conv = gdn_conv_fwd(p, w["gdn_conv_w"], name="gdn_conv")
    y_gdn, s_gdn = gdn_fwd(conv, p, small, a_log, dt_bias, w["gdn_out_gain"], name="gdn_fwd")
    y0 = jnp.concatenate([y_ret, y_gdn], axis=1)
    h1 = matmul(y0, w["ret_gdn_w_out"], add=x, name="mix0_out")
    h2, ffn0 = _ffn_layer_fwd(h1, w["norm_ffn"][0:1], w["ffn_w_up"][0], w["ffn_conv_w"][0], w["ffn_conv_b"][0:1], w["ffn_w_down"][0], "0")

    hn1 = norm_fwd(h2, w["norm_mix"][1:2], name="mix1_norm")
    gx = matmul(hn1, w["lru_w_in"], name="mix1_in")
    lru_p = (w["lru_conv_w"], w["lru_conv_b"], w["lru_w_a"], w["lru_b_a"], w["lru_w_x"], w["lru_b_x"], w["lru_lambda"])
    y1 = lru_fwd(gx, *lru_p, name="lru_fwd")
    h3 = matmul(y1, w["lru_w_out"], add=h2, name="mix1_out")
    h4, ffn1 = _ffn_layer_fwd(h3, w["norm_ffn"][1:2], w["ffn_w_up"][1], w["ffn_conv_w"][1], w["ffn_conv_b"][1:2], w["ffn_w_down"][1], "1")

    loss, dh4, g["norm_final"] = final_fwd_bwd(h4, w["norm_final"], target, name="final")

    dh3, dgf1, dwu1, dcw1, dcb1, dwd1 = _ffn_layer_bwd(ffn1, dh4, w["norm_ffn"][1:2], w["ffn_w_up"][1], w["ffn_conv_w"][1],
                                                     w["ffn_conv_b"][1:2], w["ffn_w_down"][1], "1")
    dy1 = matmul(dh3, w["lru_w_out"], tb=True, name="mix1_out_dx")
    g["lru_w_out"] = matmul(y1, dh3, ta=True, name="mix1_out_dw")
    dgate, dxr, g["lru_conv_w"], g["lru_conv_b"], g["lru_w_a"], g["lru_b_a"], g["lru_w_x"], g["lru_b_x"], g["lru_lambda"] = lru_bwd(
        gx, *lru_p, dy1, name="lru_bwd")
    dgx = jnp.concatenate([dgate, dxr], axis=1)
    dhn1 = matmul(dgx, w["lru_w_in"], tb=True, name="mix1_in_dx")
    g["lru_w_in"] = matmul(hn1, dgx, ta=True, name="mix1_in_dw")
    dh2, dgm1 = norm_bwd(h2, w["norm_mix"][1:2], dhn1, dh3, name="mix1_norm_bwd")

    dh1, dgf0, dwu0, dcw0, dcb0, dwd0 = _ffn_layer_bwd(ffn0, dh2, w["norm_ffn"][0:1], w["ffn_w_up"][0], w["ffn_conv_w"][0],
                                                     w["ffn_conv_b"][0:1], w["ffn_w_down"][0], "0")
    dy0 = matmul(dh1, w["ret_gdn_w_out"], tb=True, name="mix0_out_dx")
    g["ret_gdn_w_out"] = matmul(y0, dh1, ta=True, name="mix0_out_dw")
    dq_r, dk_r, dv_r, dg_r = ret_bwd(p, tables, s_ret, dy0, name="ret_bwd")
    dcq, dck, dcv, dg_d, dsmall_h, dal, ddt, dgain = gdn_bwd(conv, p, small, a_log, dt_bias, w["gdn_out_gain"], s_gdn, dy0, name="gdn_bwd")
    dconv = jnp.concatenate([dcq, dck, dcv], axis=1)
    dp_conv, g["gdn_conv_w"] = gdn_conv_bwd(p, w["gdn_conv_w"], dconv, name="gdn_conv_bwd")
    dp = jnp.concatenate([dq_r, dk_r, dv_r, dg_r, dp_conv, dg_d], axis=1)
    dsmall = jnp.sum(dsmall_h, axis=0)
    dhn0 = matmul(dp, w_main, tb=True, name="mix0_in_dx")
    dhn0 = matmul(dsmall, w_small, tb=True, add=dhn0, name="mix0_in_small_dx")
    d_w_main = matmul(hn0, dp, ta=True, name="mix0_in_dw")
    d_w_small = matmul(hn0, dsmall, ta=True, name="mix0_in_small_dw")
    g["ret_gdn_w_in"] = jnp.concatenate([d_w_main, d_w_small[:, :2 * N_HEADS]], axis=1)
    dx, dgm0 = norm_bwd(x, w["norm_mix"][0:1], dhn0, dh1, name="mix0_norm_bwd")

    g["gdn_a_log"] = jnp.sum(dal, axis=0)[:, :N_HEADS]
    g["gdn_dt_bias"] = jnp.sum(ddt, axis=0)[:, :N_HEADS]
    g["gdn_out_gain"] = jnp.sum(dgain, axis=0)
    g["norm_mix"] = jnp.concatenate([dgm0, dgm1], axis=0)
    g["norm_ffn"] = jnp.concatenate([dgf0, dgf1], axis=0)
    g["ffn_w_up"] = jnp.stack([dwu0, dwu1])
    g["ffn_conv_w"] = jnp.stack([dcw0, dcw1])
    g["ffn_conv_b"] = jnp.concatenate([dcb0, dcb1], axis=0)
    g["ffn_w_down"] = jnp.stack([dwd0, dwd1])
    return loss, dx, g


WEIGHTS = ("norm_mix", "norm_ffn", "ret_gdn_w_in", "gdn_conv_w", "gdn_a_log", "gdn_dt_bias", "gdn_out_gain", "ret_gdn_w_out",
           "lru_w_in", "lru_conv_w", "lru_conv_b", "lru_w_a", "lru_b_a", "lru_w_x", "lru_b_x", "lru_lambda", "lru_w_out",
           "ffn_w_up", "ffn_conv_w", "ffn_conv_b", "ffn_w_down", "norm_final")
MATMUL_SHARDED = {"ret_gdn_w_in": 1, "ret_gdn_w_out": 0, "lru_w_in": 1, "lru_w_out": 0, "ffn_w_up": 2, "ffn_w_down": 1}
VECTOR_SHARDED = {"gdn_conv_w": 1, "lru_conv_w": 1, "lru_conv_b": 1, "lru_b_a": 1, "lru_b_x": 1, "lru_lambda": 1, "ffn_conv_w": 2}
SHARDED = {**MATMUL_SHARDED, **VECTOR_SHARDED}
REPLICATED = tuple(n for n in WEIGHTS if n not in SHARDED)
SQUEEZE = {"ret_gdn_w_in", "gdn_conv_w", "ret_gdn_w_out", "lru_w_in", "lru_conv_w", "lru_w_a", "lru_w_x", "lru_w_out"}
PACK_ROW_MULTIPLE = 2 * PACK_ROW_TILE


def _local_view(name, a):
    if name in SQUEEZE:
        return a[0]
    if a.ndim == 1:
        return a[None, :]
    return a


def kernel(x, norm_mix, norm_ffn, ret_gdn_w_in, gdn_conv_w, gdn_a_log, gdn_dt_bias, gdn_out_gain, ret_gdn_w_out, lru_w_in, lru_conv_w, lru_conv_b, lru_w_a, lru_b_a, lru_w_x, lru_b_x, lru_lambda, lru_w_out, ffn_w_up, ffn_conv_w, ffn_conv_b, ffn_w_down, norm_final, loss_target, m_norm_mix, m_norm_ffn, m_ret_gdn_w_in, m_gdn_conv_w, m_gdn_a_log, m_gdn_dt_bias, m_gdn_out_gain, m_ret_gdn_w_out, m_lru_w_in, m_lru_conv_w, m_lru_conv_b, m_lru_w_a, m_lru_b_a, m_lru_w_x, m_lru_b_x, m_lru_lambda, m_lru_w_out, m_ffn_w_up, m_ffn_conv_w, m_ffn_conv_b, m_ffn_w_down, m_norm_final, v_norm_mix, v_norm_ffn, v_ret_gdn_w_in, v_gdn_conv_w, v_gdn_a_log, v_gdn_dt_bias, v_gdn_out_gain, v_ret_gdn_w_out, v_lru_w_in, v_lru_conv_w, v_lru_conv_b, v_lru_w_a, v_lru_b_a, v_lru_w_x, v_lru_b_x, v_lru_lambda, v_lru_w_out, v_ffn_w_up, v_ffn_conv_w, v_ffn_conv_b, v_ffn_w_down, v_norm_final):
    given = dict(norm_mix=norm_mix, norm_ffn=norm_ffn, ret_gdn_w_in=ret_gdn_w_in, gdn_conv_w=gdn_conv_w, gdn_a_log=gdn_a_log, gdn_dt_bias=gdn_dt_bias, gdn_out_gain=gdn_out_gain, ret_gdn_w_out=ret_gdn_w_out, lru_w_in=lru_w_in, lru_conv_w=lru_conv_w, lru_conv_b=lru_conv_b, lru_w_a=lru_w_a, lru_b_a=lru_b_a, lru_w_x=lru_w_x, lru_b_x=lru_b_x, lru_lambda=lru_lambda, lru_w_out=lru_w_out, ffn_w_up=ffn_w_up, ffn_conv_w=ffn_conv_w, ffn_conv_b=ffn_conv_b, ffn_w_down=ffn_w_down, norm_final=norm_final)
    mom1 = dict(norm_mix=m_norm_mix, norm_ffn=m_norm_ffn, ret_gdn_w_in=m_ret_gdn_w_in, gdn_conv_w=m_gdn_conv_w, gdn_a_log=m_gdn_a_log, gdn_dt_bias=m_gdn_dt_bias, gdn_out_gain=m_gdn_out_gain, ret_gdn_w_out=m_ret_gdn_w_out, lru_w_in=m_lru_w_in, lru_conv_w=m_lru_conv_w, lru_conv_b=m_lru_conv_b, lru_w_a=m_lru_w_a, lru_b_a=m_lru_b_a, lru_w_x=m_lru_w_x, lru_b_x=m_lru_b_x, lru_lambda=m_lru_lambda, lru_w_out=m_lru_w_out, ffn_w_up=m_ffn_w_up, ffn_conv_w=m_ffn_conv_w, ffn_conv_b=m_ffn_conv_b, ffn_w_down=m_ffn_w_down, norm_final=m_norm_final)
    mom2 = dict(norm_mix=v_norm_mix, norm_ffn=v_norm_ffn, ret_gdn_w_in=v_ret_gdn_w_in, gdn_conv_w=v_gdn_conv_w, gdn_a_log=v_gdn_a_log, gdn_dt_bias=v_gdn_dt_bias, gdn_out_gain=v_gdn_out_gain, ret_gdn_w_out=v_ret_gdn_w_out, lru_w_in=v_lru_w_in, lru_conv_w=v_lru_conv_w, lru_conv_b=v_lru_conv_b, lru_w_a=v_lru_w_a, lru_b_a=v_lru_b_a, lru_w_x=v_lru_w_x, lru_b_x=v_lru_b_x, lru_lambda=v_lru_lambda, lru_w_out=v_lru_w_out, ffn_w_up=v_ffn_w_up, ffn_conv_w=v_ffn_conv_w, ffn_conv_b=v_ffn_conv_b, ffn_w_down=v_ffn_w_down, norm_final=v_norm_final)

    local = {n: _local_view(n, a) for n, a in given.items()}

    mm_names, vec_names = list(MATMUL_SHARDED), list(VECTOR_SHARDED)
    full = dict(zip(mm_names, all_gather_shards([local[n] for n in mm_names], [SHARDED[n] for n in mm_names], BF16,
                                                PACK_ROW_MULTIPLE, "w")))
    full.update(zip(vec_names, all_gather_shards([local[n] for n in vec_names], [SHARDED[n] for n in vec_names], F32, 32, "p")))
    for n in REPLICATED:
        full[n] = local[n]

    loss_part, dx, grads = local_step(x[0], loss_target[0], full)
    loss = lax.psum(loss_part[0, 0], ("x", "y", "c"))

    sh_names = list(SHARDED)
    sh_shapes = [local[n].shape for n in sh_names]
    rows = _pack_rows(sum(int(np.prod(s)) for s in sh_shapes), PACK_ROW_MULTIPLE)
    g_sh = reduce_scatter_grads([grads[n] for n in sh_names], [SHARDED[n] for n in sh_names], rows, "g")
    d_sh, m_sh, v_sh = adamw(_pack([local[n] for n in sh_names], rows, F32), g_sh,
                             _pack([mom1[n] for n in sh_names], rows, F32), _pack([mom2[n] for n in sh_names], rows, F32), name="adamw_sharded")

    rp_names = list(REPLICATED)
    rp_shapes = [local[n].shape for n in rp_names]
    rp_rows = _pack_rows(sum(int(np.prod(s)) for s in rp_shapes), 256)
    g_rp = sum_slots(gather_all(_pack([grads[n] for n in rp_names], rp_rows, F32), name="gather_small_grads"), name="sum_small_grads")
    d_rp, m_rp, v_rp = adamw(_pack([local[n] for n in rp_names], rp_rows, F32), g_rp,
                             _pack([mom1[n] for n in rp_names], rp_rows, F32), _pack([mom2[n] for n in rp_names], rp_rows, F32), name="adamw_replicated")

    def collect(packed_sh, packed_rp):
        vals = dict(zip(sh_names, _unpack(packed_sh, sh_shapes)))
        vals.update(zip(rp_names, _unpack(packed_rp, rp_shapes)))
        return [vals[n].reshape(given[n].shape) for n in WEIGHTS]

    return (loss, dx[None], *collect(g_sh, g_rp), *collect(d_sh, d_rp), *collect(m_sh, m_rp), *collect(v_sh, v_rp))
```

```python
import functools

import numpy as np
import jax
import jax.numpy as jnp
from jax import lax
from jax.experimental import pallas as pl
from jax.experimental.pallas import tpu as pltpu

F32 = jnp.float32
BF16 = jnp.bfloat16
HI = lax.Precision.HIGHEST
MESH = pl.DeviceIdType.MESH

SEQ = 2048
D_MODEL = 1024
N_HEADS = 4
HEAD = 128
RET_CHUNK = 128
GDN_CHUNK = 64
GROUP = N_HEADS * HEAD
MIX_MAIN = 8 * GROUP
D_FF = 2816
LRU_BLOCKS = 8
LRU_C = 8.0
ROPE_BASE = 10000.0
EPS = 1e-6
N_SHARD = 4
LANES = 128

ADAM_LR, ADAM_B1, ADAM_B2, ADAM_EPS, ADAM_WD, ADAM_STEP = 0.001, 0.9, 0.999, 1e-08, 0.01, 10

VMEM_LIMIT_BYTES = 56 * 1024 * 1024

_roll = pltpu.roll


def _params(**kw):
    return pltpu.CompilerParams(vmem_limit_bytes=VMEM_LIMIT_BYTES, **kw)


def _sds(shape, dtype):
    return jax.ShapeDtypeStruct(tuple(shape), dtype)


def _shift_raw(x, d):
    n = x.shape[0]
    t = lax.broadcasted_iota(jnp.int32, x.shape, 0)
    if d > 0:
        return jnp.where(t >= d, _roll(x, d, 0), 0.0)
    return jnp.where(t < n + d, _roll(x, n + d, 0), 0.0)


@functools.partial(jax.custom_vjp, nondiff_argnums=(1,))
def shift_rows(x, d):
    return _shift_raw(x, d)


def _shift_fwd(x, d):
    return _shift_raw(x, d), None


def _shift_bwd(d, _, g):
    return (_shift_raw(g, -d),)


shift_rows.defvjp(_shift_fwd, _shift_bwd)


@jax.custom_vjp
def swap_halves(x):
    return _roll(x, HEAD // 2, 1)


def _swap_fwd(x):
    return _roll(x, HEAD // 2, 1), None


def _swap_bwd(_, g):
    return (_roll(g, HEAD // 2, 1),)


swap_halves.defvjp(_swap_fwd, _swap_bwd)


def _scan_raw(a, u, reverse):
    n = a.shape[0]
    t = lax.broadcasted_iota(jnp.int32, a.shape, 0)
    d = 1
    while d < n:
        if reverse:
            m = t < n - d
            a_s, u_s = _roll(a, n - d, 0), _roll(u, n - d, 0)
        else:
            m = t >= d
            a_s, u_s = _roll(a, d, 0), _roll(u, d, 0)
        u = a * jnp.where(m, u_s, 0.0) + u
        a = a * jnp.where(m, a_s, 1.0)
        d *= 2
    return u


@jax.custom_vjp
def lin_scan(a, u):
    return _scan_raw(a, u, False)


def _lin_scan_fwd(a, u):
    hs = _scan_raw(a, u, False)
    return hs, (a, hs)


def _lin_scan_bwd(res, g):
    a, hs = res
    lam = _scan_raw(_shift_raw(a, -1), g, True)
    return lam * _shift_raw(hs, 1), lam


lin_scan.defvjp(_lin_scan_fwd, _lin_scan_bwd)


def _hdot(a, b):
    return jnp.dot(a, b, precision=HI, preferred_element_type=F32)


def _eye(n):
    i = lax.broadcasted_iota(jnp.int32, (n, n), 0)
    j = lax.broadcasted_iota(jnp.int32, (n, n), 1)
    return (i == j).astype(F32)


def _unit_lower_inverse_raw(lmat):
    n = lmat.shape[0]
    x = -lmat
    inv = _eye(n) + x
    p = x
    k = 1
    while 2 * k < n:
        p = _hdot(p, p)
        inv = inv + _hdot(inv, p)
        k *= 2
    return inv


@jax.custom_vjp
def unit_lower_inverse(lmat):
    return _unit_lower_inverse_raw(lmat)


def _uli_fwd(lmat):
    inv = _unit_lower_inverse_raw(lmat)
    return inv, inv


def _uli_bwd(inv, g):
    m = lax.dot_general(inv, g, (((0,), (0,)), ((), ())), precision=HI, preferred_element_type=F32)
    return (-lax.dot_general(m, inv, (((1,), (1,)), ((), ())), precision=HI, preferred_element_type=F32),)


unit_lower_inverse.defvjp(_uli_fwd, _uli_bwd)


def _bdot(a, b, dims=(((1,), (0,)), ((), ()))):
    return lax.dot_general(a.astype(BF16), b.astype(BF16), dims, preferred_element_type=F32)


_NT = (((1,), (1,)), ((), ()))
_TN = (((0,), (0,)), ((), ()))


def _softplus(x):
    return jnp.maximum(x, 0.0) + jnp.log1p(jnp.exp(-jnp.abs(x)))


def _expm1_nonpos(x):
    poly = x * (1.0 + x * (0.5 + x * (1.0 / 6 + x * (1.0 / 24 + x * (1.0 / 120 + x * (1.0 / 720))))))
    return jnp.where(x > -0.25, poly, jnp.exp(x) - 1.0)


def _rms(x):
    return x * lax.rsqrt(jnp.mean(x * x, axis=-1, keepdims=True) + EPS)


def _causal_conv(x, w, width):
    y = w[width - 1:width, :] * x
    for j in range(width - 1):
        y = y + w[j:j + 1, :] * shift_rows(x, width - 1 - j)
    return y


def _norm_fn(x, g):
    return _rms(x) * g


def _ffn_act_fn(ug, uv, wg, wv, bg, bv):
    return jax.nn.silu(_causal_conv(ug, wg, 3) + bg) * (_causal_conv(uv, wv, 3) + bv)


def _gdn_conv_fn(x, w):
    return jax.nn.silu(_causal_conv(x, w, 4))


def _lru_fn(gate, x, cw, cb, wa, ba, wx, bx, lam):
    xr = _causal_conv(x, cw, 4) + cb
    r = jax.nn.sigmoid(_bdot(xr, wa) + ba)
    i = jax.nn.sigmoid(_bdot(xr, wx) + bx)
    log_a = -LRU_C * r * _softplus(-lam)
    a = jnp.exp(log_a)
    u = jnp.sqrt(-_expm1_nonpos(2.0 * log_a)) * (i * xr)
    hs = lin_scan(a, u)
    return jax.nn.gelu(gate) * hs


def _ret_fn(q, k, v, gate, state, cos2, sin2, dmask, ktail, qdec, cdec):
    qr = q * cos2 + swap_halves(q) * sin2
    kr = (k * cos2 + swap_halves(k) * sin2) * (HEAD ** -0.5)
    scores = _bdot(qr, kr, _NT) * dmask
    o = _bdot(scores, v) + _bdot(qr * qdec, state)
    new_state = state * cdec + _bdot(kr * ktail, v, _TN)
    return _rms(o) * jax.nn.silu(gate), new_state


def _pick_lane(x, lane_idx):
    lane = lax.broadcasted_iota(jnp.int32, x.shape, 1)
    return jnp.sum(jnp.where(lane == lane_idx, x, 0.0), axis=1, keepdims=True)


def _l2norm(x):
    return x * lax.rsqrt(jnp.sum(x * x, axis=-1, keepdims=True) + EPS)


def _gdn_fn(qc, kc, vc, gate, small, a_log, dt_bias, gain, state, head):
    c = GDN_CHUNK
    q = _l2norm(qc) * (HEAD ** -0.5)
    k = _l2norm(kc)
    beta = jax.nn.sigmoid(_pick_lane(small, head))
    a_in = _pick_lane(small, head + N_HEADS)
    g = -jnp.exp(_pick_lane(a_log, head)) * _softplus(a_in + _pick_lane(dt_bias, head))
    i = lax.broadcasted_iota(jnp.int32, (c, c), 0)
    j = lax.broadcasted_iota(jnp.int32, (c, c), 1)
    tril = i >= j
    gc_rows = _hdot(tril.astype(F32), jnp.broadcast_to(g, (c, c)))
    gc = gc_rows[:, :1]
    decay = jnp.where(tril, jnp.exp(jnp.where(tril, gc_rows - gc_rows.T, 0.0)), 0.0)
    kb = k * beta
    lmat = jnp.where(i > j, _bdot(kb, k, _NT) * decay, 0.0)
    inv = unit_lower_inverse(lmat)
    u = _hdot(inv, vc * beta)
    w = _hdot(inv, kb * jnp.exp(gc))
    attn = jnp.where(tril, _bdot(q, k, _NT) * decay, 0.0)
    g_last = jnp.sum(g, axis=0, keepdims=True)
    v_new = u - _bdot(w, state)
    o = _bdot(q * jnp.exp(gc), state) + _bdot(attn, v_new)
    new_state = state * jnp.exp(g_last) + _bdot(k * jnp.exp(g_last - gc), v_new, _TN)
    return _rms(o) * gain * jax.nn.silu(gate), new_state


def _final_fn(h, g, target):
    y = _rms(h) * g
    return 0.5 * jnp.sum(jnp.mean(jnp.square(y - target), axis=-1, keepdims=True), axis=0, keepdims=True)


def _tile(n, candidates):
    for t in candidates:
        if n % t == 0:
            return t
    raise ValueError(f"no tile for {n}")


def matmul(a, b, *, ta=False, tb=False, add=None, out_dtype=F32, tm=None, tn=None, name):
    m = a.shape[1] if ta else a.shape[0]
    k = a.shape[0] if ta else a.shape[1]
    n = b.shape[0] if tb else b.shape[1]
    assert k == (b.shape[1] if tb else b.shape[0])
    tm = tm or _tile(m, (1024, 512, 1408, 256, 128))
    tn = tn or _tile(n, (512, 1408, 256, 128))
    dims = (((0 if ta else 1,), (1 if tb else 0,)), ((), ()))

    def body(*refs):
        if add is None:
            a_ref, b_ref, o_ref = refs
        else:
            a_ref, b_ref, r_ref, o_ref = refs
        acc = lax.dot_general(a_ref[...].astype(BF16), b_ref[...].astype(BF16), dims, preferred_element_type=F32)
        if add is not None:
            acc = acc + r_ref[...]
        o_ref[...] = acc.astype(out_dtype)

    a_spec = pl.BlockSpec((k, tm), lambda i, j: (0, i)) if ta else pl.BlockSpec((tm, k), lambda i, j: (i, 0))
    b_spec = pl.BlockSpec((tn, k), lambda i, j: (j, 0)) if tb else pl.BlockSpec((k, tn), lambda i, j: (0, j))
    o_spec = pl.BlockSpec((tm, tn), lambda i, j: (i, j))
    in_specs, args = [a_spec, b_spec], [a, b]
    if add is not None:
        in_specs.append(o_spec)
        args.append(add)
    return pl.pallas_call(body, out_shape=_sds((m, n), out_dtype), grid=(m // tm, n // tn), in_specs=in_specs,
                          out_specs=o_spec, compiler_params=_params(), name=name)(*args)


ROW_TILE = 256


def norm_fwd(x, g, *, name):
    t, d = x.shape

    def body(x_ref, g_ref, o_ref):
        o_ref[...] = _norm_fn(x_ref[...], g_ref[...]).astype(BF16)

    return pl.pallas_call(body, out_shape=_sds((t, d), BF16), grid=(t // ROW_TILE,),
                          in_specs=[pl.BlockSpec((ROW_TILE, d), lambda i: (i, 0)), pl.BlockSpec((1, d), lambda i: (0, 0))],
                          out_specs=pl.BlockSpec((ROW_TILE, d), lambda i: (i, 0)), compiler_params=_params(), name=name)(x, g)


def norm_bwd(x, g, dy, dres, *, name):
    t, d = x.shape

    def body(x_ref, g_ref, dy_ref, dres_ref, dx_ref, dg_ref):
        _, vjp = jax.vjp(_norm_fn, x_ref[...], g_ref[...])
        dx, dg = vjp(dy_ref[...])
        dx_ref[...] = dx + dres_ref[...]

        @pl.when(pl.program_id(0) == 0)
        def _():
            dg_ref[...] = jnp.zeros_like(dg_ref)

        dg_ref[...] += dg

    row = pl.BlockSpec((ROW_TILE, d), lambda i: (i, 0))
    vec = pl.BlockSpec((1, d), lambda i: (0, 0))
    return pl.pallas_call(body, out_shape=(_sds((t, d), F32), _sds((1, d), F32)), grid=(t // ROW_TILE,),
                          in_specs=[row, vec, row, row], out_specs=(row, vec), compiler_params=_params(), name=name)(x, g, dy, dres)


def final_fwd_bwd(h, g, target, *, name):
    t, d = h.shape

    def body(h_ref, g_ref, t_ref, loss_ref, dh_ref, dg_ref):
        tgt = t_ref[...]
        loss, vjp = jax.vjp(lambda hh, gg: _final_fn(hh, gg, tgt), h_ref[...], g_ref[...])
        dh, dg = vjp(jnp.ones((1, 1), F32))
        dh_ref[...] = dh

        @pl.when(pl.program_id(0) == 0)
        def _():
            dg_ref[...] = jnp.zeros_like(dg_ref)
            loss_ref[...] = jnp.zeros_like(loss_ref)

        dg_ref[...] += dg
        loss_ref[...] += jnp.broadcast_to(loss, loss_ref.shape)

    row = pl.BlockSpec((ROW_TILE, d), lambda i: (i, 0))
    vec = pl.BlockSpec((1, d), lambda i: (0, 0))
    return pl.pallas_call(body, out_shape=(_sds((1, LANES), F32), _sds((t, d), F32), _sds((1, d), F32)), grid=(t // ROW_TILE,),
                          in_specs=[row, vec, row], out_specs=(pl.BlockSpec((1, LANES), lambda i: (0, 0)), row, vec),
                          compiler_params=_params(), name=name)(h, g, target)


FFN_FWD_COLS = 256
FFN_BWD_COLS = 128


def ffn_act_fwd(u, cw, cb, *, name):
    t = u.shape[0]
    w = FFN_FWD_COLS
    nb = D_FF // w

    def body(ug_ref, uv_ref, wg_ref, wv_ref, bg_ref, bv_ref, o_ref):
        o_ref[...] = _ffn_act_fn(ug_ref[...], uv_ref[...], wg_ref[...], wv_ref[...], bg_ref[...], bv_ref[...]).astype(BF16)

    def col(rows, off):
        return pl.BlockSpec((rows, w), lambda j: (0, j + off))

    return pl.pallas_call(body, out_shape=_sds((t, D_FF), BF16), grid=(nb,),
                          in_specs=[col(t, 0), col(t, nb), col(3, 0), col(3, nb), col(1, 0), col(1, nb)],
                          out_specs=col(t, 0), compiler_params=_params(), name=name)(u, u, cw, cw, cb, cb)


def ffn_act_bwd(u, cw, cb, da, *, name):
    t = u.shape[0]
    w = FFN_BWD_COLS
    nb = D_FF // w

    def body(ug_ref, uv_ref, wg_ref, wv_ref, bg_ref, bv_ref, da_ref, dug_ref, duv_ref, dwg_ref, dwv_ref, dbg_ref, dbv_ref):
        _, vjp = jax.vjp(_ffn_act_fn, ug_ref[...], uv_ref[...], wg_ref[...], wv_ref[...], bg_ref[...], bv_ref[...])
        dug, duv, dwg, dwv, dbg, dbv = vjp(da_ref[...])
        dug_ref[...] = dug.astype(BF16)
        duv_ref[...] = duv.astype(BF16)
        dwg_ref[...] = dwg
        dwv_ref[...] = dwv
        dbg_ref[...] = dbg
        dbv_ref[...] = dbv

    def col(rows, off):
        return pl.BlockSpec((rows, w), lambda j: (0, j + off))

    outs = pl.pallas_call(
        body, out_shape=(_sds((t, D_FF), BF16), _sds((t, D_FF), BF16), _sds((3, D_FF), F32), _sds((3, D_FF), F32),
                         _sds((1, D_FF), F32), _sds((1, D_FF), F32)),
        grid=(nb,), in_specs=[col(t, 0), col(t, nb), col(3, 0), col(3, nb), col(1, 0), col(1, nb), col(t, 0)],
        out_specs=(col(t, 0), col(t, 0), col(3, 0), col(3, 0), col(1, 0), col(1, 0)), compiler_params=_params(), name=name,
    )(u, u, cw, cw, cb, cb, da)
    dug, duv, dwg, dwv, dbg, dbv = outs
    return dug, duv, jnp.concatenate([dwg, dwv], axis=1), jnp.concatenate([dbg, dbv], axis=1)


GDN_CONV_COLS = 256
GDN_CONV_OFF = 4 * GROUP


def gdn_conv_fwd(p, cw, *, name):
    t = p.shape[0]
    w = GDN_CONV_COLS
    nb = 3 * GROUP // w
    off = GDN_CONV_OFF // w

    def body(x_ref, w_ref, o_ref):
        o_ref[...] = _gdn_conv_fn(x_ref[...], w_ref[...])

    return pl.pallas_call(body, out_shape=_sds((t, 3 * GROUP), F32), grid=(nb,),
                          in_specs=[pl.BlockSpec((t, w), lambda j: (0, j + off)), pl.BlockSpec((4, w), lambda j: (0, j))],
                          out_specs=pl.BlockSpec((t, w), lambda j: (0, j)), compiler_params=_params(), name=name)(p, cw)


def gdn_conv_bwd(p, cw, dc, *, name):
    t = p.shape[0]
    w = GDN_CONV_COLS
    nb = 3 * GROUP // w
    off = GDN_CONV_OFF // w

    def body(x_ref, w_ref, dc_ref, dx_ref, dw_ref):
        _, vjp = jax.vjp(_gdn_conv_fn, x_ref[...], w_ref[...])
        dx, dw = vjp(dc_ref[...])
        dx_ref[...] = dx.astype(BF16)
        dw_ref[...] = dw

    blk = pl.BlockSpec((t, w), lambda j: (0, j))
    wblk = pl.BlockSpec((4, w), lambda j: (0, j))
    return pl.pallas_call(body, out_shape=(_sds((t, 3 * GROUP), BF16), _sds((4, 3 * GROUP), F32)), grid=(nb,),
                          in_specs=[pl.BlockSpec((t, w), lambda j: (0, j + off)), wblk, blk], out_specs=(blk, wblk),
                          compiler_params=_params(), name=name)(p, cw, dc)


def _lru_specs(t):
    w = D_MODEL // LRU_BLOCKS
    gate = pl.BlockSpec((t, w), lambda j: (0, j))
    xin = pl.BlockSpec((t, w), lambda j: (0, j + LRU_BLOCKS))
    cw = pl.BlockSpec((4, w), lambda j: (0, j))
    vec = pl.BlockSpec((1, w), lambda j: (0, j))
    mat = pl.BlockSpec((None, w, w), lambda j: (j, 0, 0))
    return gate, xin, cw, vec, mat


def lru_fwd(gx, cw, cb, wa, ba, wx, bx, lam, *, name):
    t = gx.shape[0]
    gate, xin, cws, vec, mat = _lru_specs(t)

    def body(g_ref, x_ref, cw_ref, cb_ref, wa_ref, ba_ref, wx_ref, bx_ref, lam_ref, o_ref):
        o_ref[...] = _lru_fn(g_ref[...], x_ref[...], cw_ref[...], cb_ref[...], wa_ref[...], ba_ref[...], wx_ref[...],
                             bx_ref[...], lam_ref[...]).astype(BF16)

    return pl.pallas_call(body, out_shape=_sds((t, D_MODEL), BF16), grid=(LRU_BLOCKS,),
                          in_specs=[gate, xin, cws, vec, mat, vec, mat, vec, vec], out_specs=gate,
                          compiler_params=_params(), name=name)(gx, gx, cw, cb, wa, ba, wx, bx, lam)


def lru_bwd(gx, cw, cb, wa, ba, wx, bx, lam, dy, *, name):
    t = gx.shape[0]
    gate, xin, cws, vec, mat = _lru_specs(t)

    def body(g_ref, x_ref, cw_ref, cb_ref, wa_ref, ba_ref, wx_ref, bx_ref, lam_ref, dy_ref,
             dg_ref, dx_ref, dcw_ref, dcb_ref, dwa_ref, dba_ref, dwx_ref, dbx_ref, dlam_ref):
        _, vjp = jax.vjp(_lru_fn, g_ref[...], x_ref[...], cw_ref[...], cb_ref[...], wa_ref[...], ba_ref[...], wx_ref[...],
                         bx_ref[...], lam_ref[...])
        dg, dx, dcw, dcb, dwa, dba, dwx, dbx, dlam = vjp(dy_ref[...])
        dg_ref[...] = dg.astype(BF16)
        dx_ref[...] = dx.astype(BF16)
        dcw_ref[...] = dcw
        dcb_ref[...] = dcb
        dwa_ref[...] = dwa
        dba_ref[...] = dba
        dwx_ref[...] = dwx
        dbx_ref[...] = dbx
        dlam_ref[...] = dlam

    d = D_MODEL
    w = d // LRU_BLOCKS
    out_shape = (_sds((t, d), BF16), _sds((t, d), BF16), _sds((4, d), F32), _sds((1, d), F32), _sds((LRU_BLOCKS, w, w), F32),
                 _sds((1, d), F32), _sds((LRU_BLOCKS, w, w), F32), _sds((1, d), F32), _sds((1, d), F32))
    return pl.pallas_call(body, out_shape=out_shape, grid=(LRU_BLOCKS,),
                          in_specs=[gate, xin, cws, vec, mat, vec, mat, vec, vec, gate],
                          out_specs=(gate, gate, cws, vec, mat, vec, mat, vec, vec), compiler_params=_params(), name=name,
                          )(gx, gx, cw, cb, wa, ba, wx, bx, lam, dy)


def _ret_tables():
    half = HEAD // 2
    inv_freq = (np.float32(ROPE_BASE) ** (-np.arange(half, dtype=np.float32) / np.float32(half))).astype(np.float32)
    ang = (np.arange(SEQ, dtype=np.float32)[:, None] * inv_freq[None, :]).astype(np.float64)
    cos2 = np.concatenate([np.cos(ang), np.cos(ang)], axis=1).astype(np.float32)
    sin2 = np.concatenate([-np.sin(ang), np.sin(ang)], axis=1).astype(np.float32)
    c = RET_CHUNK
    log_gamma = np.log1p(-np.exp2(-5.0 - np.arange(N_HEADS, dtype=np.float64)))
    idx = np.arange(c, dtype=np.float64)
    rel = idx[:, None] - idx[None, :]
    dmask = np.where(rel >= 0, np.exp(log_gamma[:, None, None] * np.maximum(rel, 0.0)), 0.0)
    ones = np.ones((N_HEADS, c, HEAD))
    ktail = np.exp(log_gamma[:, None] * (c - 1 - idx))[:, :, None] * ones
    qdec = np.exp(log_gamma[:, None] * (idx + 1.0))[:, :, None] * ones
    cdec = np.exp(log_gamma * c)[:, None, None] * ones
    return tuple(jnp.asarray(a, F32) for a in (cos2, sin2, dmask, ktail, qdec, cdec))


def _ret_specs(rev):
    c = RET_CHUNK
    nc = SEQ // c

    def n_of(n):
        return nc - 1 - n if rev else n

    def pcol(off):
        return pl.BlockSpec((c, HEAD), lambda h, n: (n_of(n), h + off))

    tab = pl.BlockSpec((c, HEAD), lambda h, n: (n_of(n), 0))
    const = pl.BlockSpec((None, c, HEAD), lambda h, n: (h, 0, 0))
    state = pl.BlockSpec((None, None, HEAD, HEAD), lambda h, n: (h, n_of(n), 0, 0))
    return pcol, tab, const, state, nc


def ret_fwd(p, tables, *, name):
    pcol, tab, const, state, nc = _ret_specs(False)

    def body(q_ref, k_ref, v_ref, g_ref, cos_ref, sin_ref, dm_ref, kt_ref, qd_ref, cd_ref, y_ref, st_ref, s_scr):
        @pl.when(pl.program_id(1) == 0)
        def _():
            s_scr[...] = jnp.zeros_like(s_scr)

        s = s_scr[...]
        st_ref[...] = s
        y, s_new = _ret_fn(q_ref[...], k_ref[...], v_ref[...], g_ref[...], s, cos_ref[...], sin_ref[...], dm_ref[...],
                           kt_ref[...], qd_ref[...], cd_ref[...])
        y_ref[...] = y.astype(BF16)
        s_scr[...] = s_new

    return pl.pallas_call(
        body, out_shape=(_sds((SEQ, GROUP), BF16), _sds((N_HEADS, nc, HEAD, HEAD), F32)), grid=(N_HEADS, nc),
        in_specs=[pcol(0), pcol(4), pcol(8), pcol(12), tab, tab, const, const, const, const],
        out_specs=(pcol(0), state), scratch_shapes=[pltpu.VMEM((HEAD, HEAD), F32)], compiler_params=_params(), name=name,
    )(p, p, p, p, *tables)


def ret_bwd(p, tables, states, dy, *, name):
    pcol, tab, const, state, nc = _ret_specs(True)

    def body(q_ref, k_ref, v_ref, g_ref, cos_ref, sin_ref, dm_ref, kt_ref, qd_ref, cd_ref, st_ref, dy_ref,
             dq_ref, dk_ref, dv_ref, dg_ref, ds_scr):
        @pl.when(pl.program_id(1) == 0)
        def _():
            ds_scr[...] = jnp.zeros_like(ds_scr)

        consts = (cos_ref[...], sin_ref[...], dm_ref[...], kt_ref[...], qd_ref[...], cd_ref[...])
        _, vjp = jax.vjp(lambda q, k, v, g, s: _ret_fn(q, k, v, g, s, *consts), q_ref[...], k_ref[...], v_ref[...],
                         g_ref[...], st_ref[...])
        dq, dk, dv, dg, ds = vjp((dy_ref[...], ds_scr[...]))
        dq_ref[...] = dq.astype(BF16)
        dk_ref[...] = dk.astype(BF16)
        dv_ref[...] = dv.astype(BF16)
        dg_ref[...] = dg.astype(BF16)
        ds_scr[...] = ds

    out = _sds((SEQ, GROUP), BF16)
    return pl.pallas_call(
        body, out_shape=(out, out, out, out), grid=(N_HEADS, nc),
        in_specs=[pcol(0), pcol(4), pcol(8), pcol(12), tab, tab, const, const, const, const, state, pcol(0)],
        out_specs=(pcol(0), pcol(0), pcol(0), pcol(0)), scratch_shapes=[pltpu.VMEM((HEAD, HEAD), F32)],
        compiler_params=_params(), name=name,
    )(p, p, p, p, *tables, states, dy)


def _gdn_specs(rev):
    c = GDN_CHUNK
    nc = SEQ // c

    def n_of(n):
        return nc - 1 - n if rev else n

    def col(off):
        return pl.BlockSpec((c, HEAD), lambda h, n: (n_of(n), h + off))

    small = pl.BlockSpec((c, LANES), lambda h, n: (n_of(n), 0))
    vec = pl.BlockSpec((1, LANES), lambda h, n: (0, 0))
    state = pl.BlockSpec((None, None, HEAD, HEAD), lambda h, n: (h, n_of(n), 0, 0))
    return col, small, vec, state, nc


def gdn_fwd(conv, p, small, a_log, dt_bias, gain, *, name):
    col, sm, vec, state, nc = _gdn_specs(False)

    def body(q_ref, k_ref, v_ref, g_ref, sm_ref, al_ref, dt_ref, gn_ref, y_ref, st_ref, s_scr):
        @pl.when(pl.program_id(1) == 0)
        def _():
            s_scr[...] = jnp.zeros_like(s_scr)

        s = s_scr[...]
        st_ref[...] = s
        y, s_new = _gdn_fn(q_ref[...], k_ref[...], v_ref[...], g_ref[...], sm_ref[...], al_ref[...], dt_ref[...], gn_ref[...],
                           s, pl.program_id(0))
        y_ref[...] = y.astype(BF16)
        s_scr[...] = s_new

    return pl.pallas_call(
        body, out_shape=(_sds((SEQ, GROUP), BF16), _sds((N_HEADS, nc, HEAD, HEAD), F32)), grid=(N_HEADS, nc),
        in_specs=[col(0), col(4), col(8), col(28), sm, vec, vec, vec], out_specs=(col(0), state),
        scratch_shapes=[pltpu.VMEM((HEAD, HEAD), F32)], compiler_params=_params(), name=name,
    )(conv, conv, conv, p, small, a_log, dt_bias, gain)


def gdn_bwd(conv, p, small, a_log, dt_bias, gain, states, dy, *, name):
    col, sm, vec, state, nc = _gdn_specs(True)

    def body(q_ref, k_ref, v_ref, g_ref, sm_ref, al_ref, dt_ref, gn_ref, st_ref, dy_ref,
             dq_ref, dk_ref, dv_ref, dg_ref, dsm_ref, dal_ref, ddt_ref, dgn_ref, ds_scr):
        head = pl.program_id(0)

        @pl.when(pl.program_id(1) == 0)
        def _():
            ds_scr[...] = jnp.zeros_like(ds_scr)
            dal_ref[...] = jnp.zeros_like(dal_ref)
            ddt_ref[...] = jnp.zeros_like(ddt_ref)
            dgn_ref[...] = jnp.zeros_like(dgn_ref)

        _, vjp = jax.vjp(lambda *a: _gdn_fn(*a, head), q_ref[...], k_ref[...], v_ref[...], g_ref[...], sm_ref[...],
                         al_ref[...], dt_ref[...], gn_ref[...], st_ref[...])
        dq, dk, dv, dg, dsm, dal, ddt, dgn, ds = vjp((dy_ref[...], ds_scr[...]))
        dq_ref[...] = dq
        dk_ref[...] = dk
        dv_ref[...] = dv
        dg_ref[...] = dg.astype(BF16)
        dsm_ref[...] = dsm
        dal_ref[...] += dal
        ddt_ref[...] += ddt
        dgn_ref[...] += dgn
        ds_scr[...] = ds

    f = _sds((SEQ, GROUP), F32)
    hvec = pl.BlockSpec((None, 1, LANES), lambda h, n: (h, 0, 0))
    dsmall = pl.BlockSpec((None, GDN_CHUNK, LANES), lambda h, n: (h, nc - 1 - n, 0))
    pv = _sds((N_HEADS, 1, LANES), F32)
    return pl.pallas_call(
        body, out_shape=(f, f, f, _sds((SEQ, GROUP), BF16), _sds((N_HEADS, SEQ, LANES), F32), pv, pv, pv), grid=(N_HEADS, nc),
        in_specs=[col(0), col(4), col(8), col(28), sm, vec, vec, vec, state, col(4)],
        out_specs=(col(0), col(0), col(0), col(0), dsmall, hvec, hvec, hvec), scratch_shapes=[pltpu.VMEM((HEAD, HEAD), F32)],
        compiler_params=_params(), name=name,
    )(conv, conv, conv, p, small, a_log, dt_bias, gain, states, dy)


PACK_ROW_TILE = 1024


def adamw(w, g, m, v, *, name):
    r = w.shape[0]
    tr = _tile(r, (PACK_ROW_TILE, 256, 128, 64, 32, 16, 8))

    def body(w_ref, g_ref, m_ref, v_ref, d_ref, nm_ref, nv_ref):
        gg = g_ref[...]
        nm = ADAM_B1 * m_ref[...] + (1.0 - ADAM_B1) * gg
        nv = ADAM_B2 * v_ref[...] + (1.0 - ADAM_B2) * jnp.square(gg)
        m_hat = nm / (1.0 - ADAM_B1 ** ADAM_STEP)
        v_hat = nv / (1.0 - ADAM_B2 ** ADAM_STEP)
        d_ref[...] = -ADAM_LR * (m_hat / (jnp.sqrt(v_hat) + ADAM_EPS) + ADAM_WD * w_ref[...])
        nm_ref[...] = nm
        nv_ref[...] = nv

    blk = pl.BlockSpec((tr, LANES), lambda i: (i, 0))
    o = _sds((r, LANES), F32)
    return pl.pallas_call(body, out_shape=(o, o, o), grid=(r // tr,), in_specs=[blk] * 4, out_specs=(blk, blk, blk),
                          compiler_params=_params(), name=name)(w, g, m, v)


ELEMENTWISE_BLOCK_BYTES = 2 * 1024 * 1024


def _row_tile(r, c):
    best = None
    for tr in range(8, r + 1, 8):
        if r % tr == 0 and tr * c * 4 <= ELEMENTWISE_BLOCK_BYTES:
            best = tr
    if best is None:
        raise ValueError(f"no row tile for ({r}, {c})")
    return best


def _core_index():
    return lax.axis_index("c").astype(jnp.int32).reshape(1)


def _chip_index():
    return (2 * lax.axis_index("x") + lax.axis_index("y")).astype(jnp.int32).reshape(1)


def adamw_halves(w, m, v, g_own, g_sib, *, name):
    rows, c = w.shape
    r = rows // 2
    tr = _row_tile(r, c)
    nb = r // tr

    def body(c_ref, w_ref, m_ref, v_ref, own_ref, sib_ref, g_ref, d_ref, nm_ref, nv_ref):
        gg = jnp.where(pl.program_id(0) == c_ref[0], own_ref[...], sib_ref[...])
        nm = ADAM_B1 * m_ref[...] + (1.0 - ADAM_B1) * gg
        nv = ADAM_B2 * v_ref[...] + (1.0 - ADAM_B2) * jnp.square(gg)
        m_hat = nm / (1.0 - ADAM_B1 ** ADAM_STEP)
        v_hat = nv / (1.0 - ADAM_B2 ** ADAM_STEP)
        g_ref[...] = gg
        d_ref[...] = -ADAM_LR * (m_hat / (jnp.sqrt(v_hat) + ADAM_EPS) + ADAM_WD * w_ref[...])
        nm_ref[...] = nm
        nv_ref[...] = nv

    full = pl.BlockSpec((tr, c), lambda h, i, cr: (h * nb + i, 0))
    half = pl.BlockSpec((tr, c), lambda h, i, cr: (i, 0))
    o = _sds((rows, c), F32)
    gs = pltpu.PrefetchScalarGridSpec(num_scalar_prefetch=1, grid=(2, nb), in_specs=[full, full, full, half, half],
                                      out_specs=(full, full, full, full))
    return pl.pallas_call(body, out_shape=(o, o, o, o), grid_spec=gs, compiler_params=_params(), name=name)(
        _core_index(), w, m, v, g_own, g_sib)


def add_core_halves(g2, land, *, name):
    _, ns, r, cols = g2.shape
    tr = _row_tile(r, cols)

    def body(c_ref, a_ref, b_ref, o_ref):
        o_ref[...] = a_ref[...] + b_ref[...]

    gs = pltpu.PrefetchScalarGridSpec(
        num_scalar_prefetch=1, grid=(ns, r // tr),
        in_specs=[pl.BlockSpec((None, None, tr, cols), lambda s, i, cr: (cr[0], s, i, 0)),
                  pl.BlockSpec((None, tr, cols), lambda s, i, cr: (s, i, 0))],
        out_specs=pl.BlockSpec((None, tr, cols), lambda s, i, cr: (s, i, 0)))
    return pl.pallas_call(body, out_shape=_sds((ns, r, cols), F32), grid_spec=gs, compiler_params=_params(), name=name)(
        _core_index(), g2, land)


def sum_over_chips(own, land, *, scatter, name):
    _, r, cols = own.shape
    tr = _row_tile(r, cols)

    def body(mine_ref, own_ref, l0, l1, l2, l3, o_ref):
        mine = mine_ref[0]
        mine_val = own_ref[...]
        acc = None
        for s, l_ref in enumerate((l0, l1, l2, l3)):
            val = jnp.where(mine == s, mine_val, l_ref[...])
            acc = val if acc is None else acc + val
        o_ref[...] = acc

    def slot(s):
        return pl.BlockSpec((None, tr, cols), lambda i, mr: (jnp.where(mr[0] == s, (s + 1) % N_SHARD, s), i, 0))

    own_spec = pl.BlockSpec((None, tr, cols), lambda i, mr: (mr[0] if scatter else 0, i, 0))
    gs = pltpu.PrefetchScalarGridSpec(num_scalar_prefetch=1, grid=(r // tr,), in_specs=[own_spec] + [slot(s) for s in range(N_SHARD)],
                                      out_specs=pl.BlockSpec((tr, cols), lambda i, mr: (i, 0)))
    return pl.pallas_call(body, out_shape=_sds((r, cols), F32), grid_spec=gs, compiler_params=_params(), name=name)(
        _chip_index(), own, land, land, land, land)


_ANY = pl.BlockSpec(memory_space=pl.ANY)


def xy_exchange(src, *, scatter, name):
    rh = src.shape[1]

    def body(src_ref, land_ref, send_sems, recv_sems, loc_sem):
        x, y, c = lax.axis_index("x"), lax.axis_index("y"), lax.axis_index("c")
        mine = 2 * x + y
        peers = [(1 - x, y), (x, 1 - y), (1 - x, 1 - y)]

        def piece(shard):
            return src_ref.at[shard] if scatter else src_ref.at[c]

        def copy(k, px, py, dst_slot):
            return pltpu.make_async_remote_copy(src_ref=piece(2 * px + py), dst_ref=land_ref.at[dst_slot], send_sem=send_sems.at[k],
                                                recv_sem=recv_sems.at[k], device_id=(px, py, c), device_id_type=MESH)

        keep = pltpu.make_async_copy(piece(mine), land_ref.at[mine], loc_sem)
        keep.start()
        sends = [copy(k, px, py, mine) for k, (px, py) in enumerate(peers)]
        for cp in sends:
            cp.start()
        for cp in sends:
            cp.wait_send()
        for k, (px, py) in enumerate(peers):
            copy(k, px, py, 2 * px + py).wait_recv()
        keep.wait()

    return pl.pallas_call(body, out_shape=_sds((N_SHARD, rh, LANES), src.dtype), in_specs=[_ANY], out_specs=_ANY,
                          scratch_shapes=[pltpu.SemaphoreType.DMA((3,)), pltpu.SemaphoreType.DMA((3,)), pltpu.SemaphoreType.DMA(())],
                          name=name)(src)


def core_exchange(src, *, send_other_half, name):
    def body(src_ref, out_ref, send_sem, recv_sem, loc_sem):
        x, y, c = lax.axis_index("x"), lax.axis_index("y"), lax.axis_index("c")
        if send_other_half:
            cp = pltpu.make_async_remote_copy(src_ref=src_ref.at[1 - c], dst_ref=out_ref, send_sem=send_sem, recv_sem=recv_sem,
                                              device_id=(x, y, 1 - c), device_id_type=MESH)
            cp.start()
            cp.wait_send()
            cp.wait_recv()
        else:
            keep = pltpu.make_async_copy(src_ref, out_ref.at[c], loc_sem)
            keep.start()
            cp = pltpu.make_async_remote_copy(src_ref=src_ref, dst_ref=out_ref.at[c], send_sem=send_sem, recv_sem=recv_sem,
                                              device_id=(x, y, 1 - c), device_id_type=MESH)
            cp.start()
            cp.wait_send()
            pltpu.make_async_remote_copy(src_ref=src_ref, dst_ref=out_ref.at[1 - c], send_sem=send_sem, recv_sem=recv_sem,
                                         device_id=(x, y, 1 - c), device_id_type=MESH).wait_recv()
            keep.wait()

    out_shape = _sds(src.shape[1:], src.dtype) if send_other_half else _sds((2,) + src.shape, src.dtype)
    return pl.pallas_call(body, out_shape=out_shape, in_specs=[_ANY], out_specs=_ANY,
                          scratch_shapes=[pltpu.SemaphoreType.DMA(()), pltpu.SemaphoreType.DMA(()), pltpu.SemaphoreType.DMA(())],
                          name=name)(src)


def _comm_call(body, ins, out_shapes, sem_counts, name):
    return pl.pallas_call(body, out_shape=tuple(out_shapes), in_specs=[_ANY] * len(ins), out_specs=tuple([_ANY] * len(out_shapes)),
                          scratch_shapes=[pltpu.SemaphoreType.DMA((k,)) for k in sem_counts], name=name)(*ins)


def _xy_peers(x, y):
    return [(1 - x, y), (x, 1 - y), (1 - x, 1 - y)]


def gather_halves(halves, *, name):
    n = len(halves)

    def body(*refs):
        ins, lands, sibs = refs[:n], refs[n:2 * n], refs[2 * n:3 * n]
        ici_send, ici_recv, d2d_send, d2d_recv = refs[3 * n:]
        x, y, c = lax.axis_index("x"), lax.axis_index("y"), lax.axis_index("c")
        mine = 2 * x + y
        peers = _xy_peers(x, y)

        def ici(i, k, slot):
            px, py = peers[k]
            return pltpu.make_async_remote_copy(src_ref=ins[i].at[c], dst_ref=lands[i].at[slot], send_sem=ici_send.at[3 * i + k],
                                                recv_sem=ici_recv.at[3 * i + k], device_id=(px, py, c), device_id_type=MESH)

        def pass_on(i, k):
            px, py = peers[k]
            slot = 2 * px + py
            return pltpu.make_async_remote_copy(src_ref=lands[i].at[slot], dst_ref=sibs[i].at[slot], send_sem=d2d_send.at[3 * i + k],
                                                recv_sem=d2d_recv.at[3 * i + k], device_id=(x, y, 1 - c), device_id_type=MESH)

        sends = [ici(i, k, mine) for i in range(n) for k in range(3)]
        for cp in sends:
            cp.start()
        passed = []
        for i in range(n):
            for k in range(3):
                px, py = peers[k]
                ici(i, k, 2 * px + py).wait_recv()
                cp = pass_on(i, k)
                cp.start()
                passed.append(cp)
        for cp in passed:
            cp.wait_recv()
        for cp in sends + passed:
            cp.wait_send()

    outs = [_sds((N_SHARD,) + h.shape[1:], h.dtype) for h in halves]
    res = _comm_call(body, halves, outs + outs, [3 * n] * 4, name)
    return res[:n], res[n:]


def send_other_half(arrays, *, name):
    n = len(arrays)

    def body(*refs):
        ins, lands = refs[:n], refs[n:2 * n]
        send_sems, recv_sems = refs[2 * n:]
        x, y, c = lax.axis_index("x"), lax.axis_index("y"), lax.axis_index("c")
        copies = [pltpu.make_async_remote_copy(src_ref=ins[i].at[1 - c], dst_ref=lands[i], send_sem=send_sems.at[i],
                                               recv_sem=recv_sems.at[i], device_id=(x, y, 1 - c), device_id_type=MESH) for i in range(n)]
        for cp in copies:
            cp.start()
        for cp in copies:
            cp.wait_recv()
        for cp in copies:
            cp.wait_send()

    return _comm_call(body, arrays, [_sds(a.shape[1:], a.dtype) for a in arrays], [n, n], name)


def send_to_chips(arrays, scatter, *, name):
    n = len(arrays)

    def body(*refs):
        ins, lands = refs[:n], refs[n:2 * n]
        send_sems, recv_sems = refs[2 * n:]
        x, y, c = lax.axis_index("x"), lax.axis_index("y"), lax.axis_index("c")
        mine = 2 * x + y
        peers = _xy_peers(x, y)

        def copy(i, k, dst_slot):
            px, py = peers[k]
            src = ins[i].at[2 * px + py] if scatter[i] else ins[i].at[0]
            return pltpu.make_async_remote_copy(src_ref=src, dst_ref=lands[i].at[dst_slot], send_sem=send_sems.at[3 * i + k],
                                                recv_sem=recv_sems.at[3 * i + k], device_id=(px, py, c), device_id_type=MESH)

        sends = [copy(i, k, mine) for i in range(n) for k in range(3)]
        for cp in sends:
            cp.start()
        for i in range(n):
            for k in range(3):
                px, py = peers[k]
                copy(i, k, 2 * px + py).wait_recv()
        for cp in sends:
            cp.wait_send()

    return _comm_call(body, arrays, [_sds((N_SHARD,) + a.shape[1:], a.dtype) for a in arrays], [3 * n, 3 * n], name)


def swap_with_other_core(arrays, *, name):
    n = len(arrays)

    def body(*refs):
        ins, lands = refs[:n], refs[n:2 * n]
        send_sems, recv_sems = refs[2 * n:]
        x, y, c = lax.axis_index("x"), lax.axis_index("y"), lax.axis_index("c")
        copies = [pltpu.make_async_remote_copy(src_ref=ins[i], dst_ref=lands[i], send_sem=send_sems.at[i], recv_sem=recv_sems.at[i],
                                               device_id=(x, y, 1 - c), device_id_type=MESH) for i in range(n)]
        for cp in copies:
            cp.start()
        for cp in copies:
            cp.wait_recv()
        for cp in copies:
            cp.wait_send()

    return _comm_call(body, arrays, [_sds(a.shape, a.dtype) for a in arrays], [n, n], name)


def _pack_rows(n_elems, row_multiple):
    rows = -(-n_elems // LANES)
    return -(-rows // row_multiple) * row_multiple


def _pack(arrays, rows, dtype):
    flat = jnp.concatenate([a.reshape(-1).astype(dtype) for a in arrays])
    return jnp.pad(flat, (0, rows * LANES - flat.shape[0])).reshape(rows, LANES)


def _unpack(packed, shapes):
    flat = packed.reshape(-1)
    out, off = [], 0
    for s in shapes:
        n = int(np.prod(s))
        out.append(flat[off:off + n].reshape(s))
        off += n
    return out


def all_gather_shards(shards, axes, dtype, row_multiple, tag):
    shapes = [s.shape for s in shards]
    rows = _pack_rows(sum(int(np.prod(s)) for s in shapes), row_multiple)
    packed = _pack(shards, rows, dtype).reshape(2, rows // 2, LANES)
    land = xy_exchange(packed, scatter=False, name=f"gather_xy_{tag}")
    both = core_exchange(land, send_other_half=False, name=f"gather_c_{tag}")
    per_shard = jnp.swapaxes(both, 0, 1).reshape(N_SHARD, rows, LANES)
    pieces = [_unpack(per_shard[s], shapes) for s in range(N_SHARD)]
    return [jnp.concatenate([pieces[s][i] for s in range(N_SHARD)], axis=ax) for i, ax in enumerate(axes)]


def reduce_over_devices(arrays, scatter):
    land = send_other_half(arrays, name="reduce_core_send")
    chip = [add_core_halves(a, l, name=f"reduce_core_add_{i}") for i, (a, l) in enumerate(zip(arrays, land))]
    land = send_to_chips(chip, scatter, name="reduce_chip_send")
    own = [sum_over_chips(ch, l, scatter=sc, name=f"reduce_chip_add_{i}") for i, (ch, l, sc) in enumerate(zip(chip, land, scatter))]
    sib = swap_with_other_core(own, name="reduce_core_swap")
    return own, sib


def _ffn_layer_fwd(h, norm_g, w_up, cw, cb, w_down, tag):
    hn = norm_fwd(h, norm_g, name=f"ffn_norm_{tag}")
    u = matmul(hn, w_up, name=f"ffn_up_{tag}")
    act = ffn_act_fwd(u, cw, cb, name=f"ffn_act_{tag}")
    out = matmul(act, w_down, add=h, name=f"ffn_down_{tag}")
    return out, (h, hn, u, act)


def _ffn_layer_bwd(saved, dout, norm_g, w_up, cw, cb, w_down, tag):
    h, hn, u, act = saved
    dact = matmul(dout, w_down, tb=True, name=f"ffn_down_dx_{tag}")
    d_w_down = matmul(act, dout, ta=True, name=f"ffn_down_dw_{tag}")
    dug, duv, dcw, dcb = ffn_act_bwd(u, cw, cb, dact, name=f"ffn_act_bwd_{tag}")
    du = jnp.concatenate([dug, duv], axis=1)
    dhn = matmul(du, w_up, tb=True, name=f"ffn_up_dx_{tag}")
    d_w_up = matmul(hn, du, ta=True, name=f"ffn_up_dw_{tag}")
    dh, dg = norm_bwd(h, norm_g, dhn, dout, name=f"ffn_norm_bwd_{tag}")
    return dh, dg, d_w_up, dcw, dcb, d_w_down


def local_step(x, target, w):
    g = {}
    tables = _ret_tables()
    w_in = w["ret_gdn_w_in"]
    w_main = w_in[:, :MIX_MAIN]
    w_small = jnp.pad(w_in[:, MIX_MAIN:], ((0, 0), (0, LANES - 2 * N_HEADS)))
    a_log = jnp.pad(w["gdn_a_log"], ((0, 0), (0, LANES - N_HEADS)))
    dt_bias = jnp.pad(w["gdn_dt_bias"], ((0, 0), (0, LANES - N_HEADS)))

    hn0 = norm_fwd(x, w["norm_mix"][0:1], name="mix0_norm")
    p = matmul(hn0, w_main, name="mix0_in")
    small = matmul(hn0, w_small, name="mix0_in_small")
    y_ret, s_ret = ret_fwd(p, tables, name="ret_fwd")
    conv = gdn_conv_fwd(p, w["gdn_conv_w"], name="gdn_conv")
    y_gdn, s_gdn = gdn_fwd(conv, p, small, a_log, dt_bias, w["gdn_out_gain"], name="gdn_fwd")
    y0 = jnp.concatenate([y_ret, y_gdn], axis=1)
    h1 = matmul(y0, w["ret_gdn_w_out"], add=x, name="mix0_out")
    h2, ffn0 = _ffn_layer_fwd(h1, w["norm_ffn"][0:1], w["ffn_w_up"][0], w["ffn_conv_w"][0], w["ffn_conv_b"][0:1], w["ffn_w_down"][0], "0")

    hn1 = norm_fwd(h2, w["norm_mix"][1:2], name="mix1_norm")
    gx = matmul(hn1, w["lru_w_in"], name="mix1_in")
    lru_p = (w["lru_conv_w"], w["lru_conv_b"], w["lru_w_a"], w["lru_b_a"], w["lru_w_x"], w["lru_b_x"], w["lru_lambda"])
    y1 = lru_fwd(gx, *lru_p, name="lru_fwd")
    h3 = matmul(y1, w["lru_w_out"], add=h2, name="mix1_out")
    h4, ffn1 = _ffn_layer_fwd(h3, w["norm_ffn"][1:2], w["ffn_w_up"][1], w["ffn_conv_w"][1], w["ffn_conv_b"][1:2], w["ffn_w_down"][1], "1")

    loss, dh4, g["norm_final"] = final_fwd_bwd(h4, w["norm_final"], target, name="final")

    dh3, dgf1, dwu1, dcw1, dcb1, dwd1 = _ffn_layer_bwd(ffn1, dh4, w["norm_ffn"][1:2], w["ffn_w_up"][1], w["ffn_conv_w"][1],
                                                     w["ffn_conv_b"][1:2], w["ffn_w_down"][1], "1")
    dy1 = matmul(dh3, w["lru_w_out"], tb=True, name="mix1_out_dx")
    g["lru_w_out"] = matmul(y1, dh3, ta=True, name="mix1_out_dw")
    dgate, dxr, g["lru_conv_w"], g["lru_conv_b"], g["lru_w_a"], g["lru_b_a"], g["lru_w_x"], g["lru_b_x"], g["lru_lambda"] = lru_bwd(
        gx, *lru_p, dy1, name="lru_bwd")
    dgx = jnp.concatenate([dgate, dxr], axis=1)
    dhn1 = matmul(dgx, w["lru_w_in"], tb=True, name="mix1_in_dx")
    g["lru_w_in"] = matmul(hn1, dgx, ta=True, name="mix1_in_dw")
    dh2, dgm1 = norm_bwd(h2, w["norm_mix"][1:2], dhn1, dh3, name="mix1_norm_bwd")

    dh1, dgf0, dwu0, dcw0, dcb0, dwd0 = _ffn_layer_bwd(ffn0, dh2, w["norm_ffn"][0:1], w["ffn_w_up"][0], w["ffn_conv_w"][0],
                                                     w["ffn_conv_b"][0:1], w["ffn_w_down"][0], "0")
    dy0 = matmul(dh1, w["ret_gdn_w_out"], tb=True, name="mix0_out_dx")
    g["ret_gdn_w_out"] = matmul(y0, dh1, ta=True, name="mix0_out_dw")
    dq_r, dk_r, dv_r, dg_r = ret_bwd(p, tables, s_ret, dy0, name="ret_bwd")
    dcq, dck, dcv, dg_d, dsmall_h, dal, ddt, dgain = gdn_bwd(conv, p, small, a_log, dt_bias, w["gdn_out_gain"], s_gdn, dy0, name="gdn_bwd")
    dconv = jnp.concatenate([dcq, dck, dcv], axis=1)
    dp_conv, g["gdn_conv_w"] = gdn_conv_bwd(p, w["gdn_conv_w"], dconv, name="gdn_conv_bwd")
    dp = jnp.concatenate([dq_r, dk_r, dv_r, dg_r, dp_conv, dg_d], axis=1)
    dsmall = jnp.sum(dsmall_h, axis=0)
    dhn0 = matmul(dp, w_main, tb=True, name="mix0_in_dx")
    dhn0 = matmul(dsmall, w_small, tb=True, add=dhn0, name="mix0_in_small_dx")
    d_w_main = matmul(hn0, dp, ta=True, name="mix0_in_dw")
    d_w_small = matmul(hn0, dsmall, ta=True, name="mix0_in_small_dw")
    g["ret_gdn_w_in"] = jnp.concatenate([d_w_main, d_w_small[:, :2 * N_HEADS]], axis=1)
    dx, dgm0 = norm_bwd(x, w["norm_mix"][0:1], dhn0, dh1, name="mix0_norm_bwd")

    g["gdn_a_log"] = jnp.sum(dal, axis=0)[:, :N_HEADS]
    g["gdn_dt_bias"] = jnp.sum(ddt, axis=0)[:, :N_HEADS]
    g["gdn_out_gain"] = jnp.sum(dgain, axis=0)
    g["norm_mix"] = jnp.concatenate([dgm0, dgm1], axis=0)
    g["norm_ffn"] = jnp.concatenate([dgf0, dgf1], axis=0)
    g["ffn_w_up"] = jnp.stack([dwu0, dwu1])
    g["ffn_conv_w"] = jnp.stack([dcw0, dcw1])
    g["ffn_conv_b"] = jnp.concatenate([dcb0, dcb1], axis=0)
    g["ffn_w_down"] = jnp.stack([dwd0, dwd1])
    return loss, dx, g


WEIGHTS = ("norm_mix", "norm_ffn", "ret_gdn_w_in", "gdn_conv_w", "gdn_a_log", "gdn_dt_bias", "gdn_out_gain", "ret_gdn_w_out",
           "lru_w_in", "lru_conv_w", "lru_conv_b", "lru_w_a", "lru_b_a", "lru_w_x", "lru_b_x", "lru_lambda", "lru_w_out",
           "ffn_w_up", "ffn_conv_w", "ffn_conv_b", "ffn_w_down", "norm_final")
MATMUL_SHARDED = {"ret_gdn_w_in": 1, "ret_gdn_w_out": 0, "lru_w_in": 1, "lru_w_out": 0, "ffn_w_up": 2, "ffn_w_down": 1}
VECTOR_SHARDED = {"gdn_conv_w": 1, "lru_conv_w": 1, "lru_conv_b": 1, "lru_b_a": 1, "lru_b_x": 1, "lru_lambda": 1, "ffn_conv_w": 2}
SHARDED = {**MATMUL_SHARDED, **VECTOR_SHARDED}
REPLICATED = tuple(n for n in WEIGHTS if n not in SHARDED)
SQUEEZE = {"ret_gdn_w_in", "gdn_conv_w", "ret_gdn_w_out", "lru_w_in", "lru_conv_w", "lru_w_a", "lru_w_x", "lru_w_out"}
MIX_IN = MIX_MAIN + 2 * N_HEADS
BIG_LAYOUT = {
    "ret_gdn_w_in": ((D_MODEL, MIX_IN), (2, D_MODEL // 2, N_SHARD, MIX_IN // N_SHARD), (0, 2, 1, 3)),
    "ret_gdn_w_out": ((2 * GROUP, D_MODEL), (N_SHARD, 2, GROUP // N_SHARD, D_MODEL), (1, 0, 2, 3)),
    "lru_w_in": ((D_MODEL, 2 * D_MODEL), (2, D_MODEL // 2, N_SHARD, 2 * D_MODEL // N_SHARD), (0, 2, 1, 3)),
    "lru_w_out": ((D_MODEL, D_MODEL), (N_SHARD, 2, D_MODEL // (2 * N_SHARD), D_MODEL), (1, 0, 2, 3)),
    "ffn_w_up": ((2, D_MODEL, 2 * D_FF), (2, D_MODEL, N_SHARD, 2 * D_FF // N_SHARD), (0, 2, 1, 3)),
    "ffn_w_down": ((2, D_FF, D_MODEL), (2, N_SHARD, D_FF // N_SHARD, D_MODEL), (0, 1, 2, 3)),
}


def _local_view(name, a):
    if name in SQUEEZE:
        return a[0]
    if a.ndim == 1:
        return a[None, :]
    return a


def kernel(x, norm_mix, norm_ffn, ret_gdn_w_in, gdn_conv_w, gdn_a_log, gdn_dt_bias, gdn_out_gain, ret_gdn_w_out, lru_w_in, lru_conv_w, lru_conv_b, lru_w_a, lru_b_a, lru_w_x, lru_b_x, lru_lambda, lru_w_out, ffn_w_up, ffn_conv_w, ffn_conv_b, ffn_w_down, norm_final, loss_target, m_norm_mix, m_norm_ffn, m_ret_gdn_w_in, m_gdn_conv_w, m_gdn_a_log, m_gdn_dt_bias, m_gdn_out_gain, m_ret_gdn_w_out, m_lru_w_in, m_lru_conv_w, m_lru_conv_b, m_lru_w_a, m_lru_b_a, m_lru_w_x, m_lru_b_x, m_lru_lambda, m_lru_w_out, m_ffn_w_up, m_ffn_conv_w, m_ffn_conv_b, m_ffn_w_down, m_norm_final, v_norm_mix, v_norm_ffn, v_ret_gdn_w_in, v_gdn_conv_w, v_gdn_a_log, v_gdn_dt_bias, v_gdn_out_gain, v_ret_gdn_w_out, v_lru_w_in, v_lru_conv_w, v_lru_conv_b, v_lru_w_a, v_lru_b_a, v_lru_w_x, v_lru_b_x, v_lru_lambda, v_lru_w_out, v_ffn_w_up, v_ffn_conv_w, v_ffn_conv_b, v_ffn_w_down, v_norm_final):
    given = dict(norm_mix=norm_mix, norm_ffn=norm_ffn, ret_gdn_w_in=ret_gdn_w_in, gdn_conv_w=gdn_conv_w, gdn_a_log=gdn_a_log, gdn_dt_bias=gdn_dt_bias, gdn_out_gain=gdn_out_gain, ret_gdn_w_out=ret_gdn_w_out, lru_w_in=lru_w_in, lru_conv_w=lru_conv_w, lru_conv_b=lru_conv_b, lru_w_a=lru_w_a, lru_b_a=lru_b_a, lru_w_x=lru_w_x, lru_b_x=lru_b_x, lru_lambda=lru_lambda, lru_w_out=lru_w_out, ffn_w_up=ffn_w_up, ffn_conv_w=ffn_conv_w, ffn_conv_b=ffn_conv_b, ffn_w_down=ffn_w_down, norm_final=norm_final)
    mom1 = dict(norm_mix=m_norm_mix, norm_ffn=m_norm_ffn, ret_gdn_w_in=m_ret_gdn_w_in, gdn_conv_w=m_gdn_conv_w, gdn_a_log=m_gdn_a_log, gdn_dt_bias=m_gdn_dt_bias, gdn_out_gain=m_gdn_out_gain, ret_gdn_w_out=m_ret_gdn_w_out, lru_w_in=m_lru_w_in, lru_conv_w=m_lru_conv_w, lru_conv_b=m_lru_conv_b, lru_w_a=m_lru_w_a, lru_b_a=m_lru_b_a, lru_w_x=m_lru_w_x, lru_b_x=m_lru_b_x, lru_lambda=m_lru_lambda, lru_w_out=m_lru_w_out, ffn_w_up=m_ffn_w_up, ffn_conv_w=m_ffn_conv_w, ffn_conv_b=m_ffn_conv_b, ffn_w_down=m_ffn_w_down, norm_final=m_norm_final)
    mom2 = dict(norm_mix=v_norm_mix, norm_ffn=v_norm_ffn, ret_gdn_w_in=v_ret_gdn_w_in, gdn_conv_w=v_gdn_conv_w, gdn_a_log=v_gdn_a_log, gdn_dt_bias=v_gdn_dt_bias, gdn_out_gain=v_gdn_out_gain, ret_gdn_w_out=v_ret_gdn_w_out, lru_w_in=v_lru_w_in, lru_conv_w=v_lru_conv_w, lru_conv_b=v_lru_conv_b, lru_w_a=v_lru_w_a, lru_b_a=v_lru_b_a, lru_w_x=v_lru_w_x, lru_b_x=v_lru_b_x, lru_lambda=v_lru_lambda, lru_w_out=v_lru_w_out, ffn_w_up=v_ffn_w_up, ffn_conv_w=v_ffn_conv_w, ffn_conv_b=v_ffn_conv_b, ffn_w_down=v_ffn_w_down, norm_final=v_norm_final)

    local = {n: _local_view(n, a) for n, a in given.items()}

    core = lax.axis_index("c")
    chip = 2 * lax.axis_index("x") + lax.axis_index("y")
    is_my_chip = lax.broadcasted_iota(jnp.int32, (N_SHARD, 1, 1), 0) == chip

    def by_core(mine, other):
        return jnp.where(core == 0, jnp.stack([mine, other]), jnp.stack([other, mine]))

    mm_names, vec_names, rp_names = list(BIG_LAYOUT), list(VECTOR_SHARDED), list(REPLICATED)
    halves = []
    for n in mm_names:
        _, split, perm = BIG_LAYOUT[n]
        halves.append(local[n].astype(BF16).reshape((2,) + tuple(split[p] for p in perm)[2:]))
    lands, sibs = gather_halves(halves, name="gather_weights")
    full = {}
    for n, mine, land, sib in zip(mm_names, halves, lands, sibs):
        full_shape, split, perm = BIG_LAYOUT[n]
        half_mine = jnp.where(is_my_chip, jnp.where(core == 0, mine[0], mine[1])[None], land)
        half_other = jnp.where(is_my_chip, jnp.where(core == 0, mine[1], mine[0])[None], sib)
        full[n] = by_core(half_mine, half_other).transpose(perm).reshape(full_shape)
    full.update(zip(vec_names, all_gather_shards([local[n] for n in vec_names], [SHARDED[n] for n in vec_names], F32, 32, "p")))
    for n in rp_names:
        full[n] = local[n]

    loss_part, dx, grads = local_step(x[0], loss_target[0], full)
    loss = lax.psum(loss_part[0, 0], ("x", "y", "c"))

    small_names = rp_names + vec_names
    small_shapes = [grads[n].shape for n in small_names]
    small_rows = _pack_rows(sum(int(np.prod(s)) for s in small_shapes), 16)
    small = _pack([grads[n] for n in small_names], small_rows, F32).reshape(2, 1, small_rows // 2, LANES)
    arrays = [grads[n].reshape(BIG_LAYOUT[n][1]).transpose(BIG_LAYOUT[n][2]) for n in mm_names] + [small]
    g_own, g_sib = reduce_over_devices(arrays, [True] * len(mm_names) + [False])

    result = {}
    for i, n in enumerate(mm_names):
        r, cols = g_own[i].shape
        w2, m2, v2 = (_local_view(n, a).reshape(2 * r, cols) for a in (given[n], mom1[n], mom2[n]))
        result[n] = adamw_halves(w2, m2, v2, g_own[i], g_sib[i], name=f"adamw_{n}")

    g_small = dict(zip(small_names, _unpack(by_core(g_own[-1], g_sib[-1]).reshape(small_rows, LANES), small_shapes)))
    for n in vec_names:
        size = local[n].shape[SHARDED[n]]
        g_small[n] = lax.dynamic_slice_in_dim(g_small[n], chip * size, size, axis=SHARDED[n])
    loc_shapes = [local[n].shape for n in small_names]
    loc_rows = _pack_rows(sum(int(np.prod(s)) for s in loc_shapes), 8)
    packs = [_pack([src[n] for n in small_names], loc_rows, F32) for src in (given, g_small, mom1, mom2)]
    d_s, m_s, v_s = adamw(*packs, name="adamw_small")
    for n, d, nm, nv in zip(small_names, _unpack(d_s, loc_shapes), _unpack(m_s, loc_shapes), _unpack(v_s, loc_shapes)):
        result[n] = (g_small[n], d, nm, nv)

    outs = [[result[n][k].reshape(given[n].shape) for n in WEIGHTS] for k in range(4)]
    return (loss, dx[None], *outs[0], *outs[1], *outs[2], *outs[3])
```

```python
import functools

import numpy as np
import jax
import jax.numpy as jnp
from jax import lax
from jax.experimental import pallas as pl
from jax.experimental.pallas import tpu as pltpu

F32 = jnp.float32
BF16 = jnp.bfloat16
HI = lax.Precision.HIGHEST
MESH = pl.DeviceIdType.MESH

SEQ = 2048
D_MODEL = 1024
N_HEADS = 4
HEAD = 128
RET_CHUNK = 128
GDN_CHUNK = 64
GROUP = N_HEADS * HEAD
MIX_MAIN = 8 * GROUP
D_FF = 2816
LRU_BLOCKS = 8
LRU_C = 8.0
ROPE_BASE = 10000.0
EPS = 1e-6
N_SHARD = 4
LANES = 128

ADAM_LR, ADAM_B1, ADAM_B2, ADAM_EPS, ADAM_WD, ADAM_STEP = 0.001, 0.9, 0.999, 1e-08, 0.01, 10

VMEM_LIMIT_BYTES = 56 * 1024 * 1024

_roll = pltpu.roll


def _params(**kw):
    return pltpu.CompilerParams(vmem_limit_bytes=VMEM_LIMIT_BYTES, **kw)


def _sds(shape, dtype):
    return jax.ShapeDtypeStruct(tuple(shape), dtype)


def _shift_raw(x, d):
    n = x.shape[0]
    t = lax.broadcasted_iota(jnp.int32, x.shape, 0)
    if d > 0:
        return jnp.where(t >= d, _roll(x, d, 0), 0.0)
    return jnp.where(t < n + d, _roll(x, n + d, 0), 0.0)


@functools.partial(jax.custom_vjp, nondiff_argnums=(1,))
def shift_rows(x, d):
    return _shift_raw(x, d)


def _shift_fwd(x, d):
    return _shift_raw(x, d), None


def _shift_bwd(d, _, g):
    return (_shift_raw(g, -d),)


shift_rows.defvjp(_shift_fwd, _shift_bwd)


@jax.custom_vjp
def swap_halves(x):
    return _roll(x, HEAD // 2, 1)


def _swap_fwd(x):
    return _roll(x, HEAD // 2, 1), None


def _swap_bwd(_, g):
    return (_roll(g, HEAD // 2, 1),)


swap_halves.defvjp(_swap_fwd, _swap_bwd)


def _scan_raw(a, u, reverse):
    n = a.shape[0]
    t = lax.broadcasted_iota(jnp.int32, a.shape, 0)
    d = 1
    while d < n:
        if reverse:
            m = t < n - d
            a_s, u_s = _roll(a, n - d, 0), _roll(u, n - d, 0)
        else:
            m = t >= d
            a_s, u_s = _roll(a, d, 0), _roll(u, d, 0)
        u = a * jnp.where(m, u_s, 0.0) + u
        a = a * jnp.where(m, a_s, 1.0)
        d *= 2
    return u


@jax.custom_vjp
def lin_scan(a, u):
    return _scan_raw(a, u, False)


def _lin_scan_fwd(a, u):
    hs = _scan_raw(a, u, False)
    return hs, (a, hs)


def _lin_scan_bwd(res, g):
    a, hs = res
    lam = _scan_raw(_shift_raw(a, -1), g, True)
    return lam * _shift_raw(hs, 1), lam


lin_scan.defvjp(_lin_scan_fwd, _lin_scan_bwd)


def _hdot(a, b):
    return jnp.dot(a, b, precision=HI, preferred_element_type=F32)


def _eye(n):
    i = lax.broadcasted_iota(jnp.int32, (n, n), 0)
    j = lax.broadcasted_iota(jnp.int32, (n, n), 1)
    return (i == j).astype(F32)


def _unit_lower_inverse_raw(lmat):
    n = lmat.shape[0]
    x = -lmat
    inv = _eye(n) + x
    p = x
    k = 1
    while 2 * k < n:
        p = _hdot(p, p)
        inv = inv + _hdot(inv, p)
        k *= 2
    return inv


@jax.custom_vjp
def unit_lower_inverse(lmat):
    return _unit_lower_inverse_raw(lmat)


def _uli_fwd(lmat):
    inv = _unit_lower_inverse_raw(lmat)
    return inv, inv


def _uli_bwd(inv, g):
    m = lax.dot_general(inv, g, (((0,), (0,)), ((), ())), precision=HI, preferred_element_type=F32)
    return (-lax.dot_general(m, inv, (((1,), (1,)), ((), ())), precision=HI, preferred_element_type=F32),)


unit_lower_inverse.defvjp(_uli_fwd, _uli_bwd)


def _bdot(a, b, dims=(((1,), (0,)), ((), ()))):
    return lax.dot_general(a.astype(BF16), b.astype(BF16), dims, preferred_element_type=F32)


_NT = (((1,), (1,)), ((), ()))
_TN = (((0,), (0,)), ((), ()))


def _softplus(x):
    return jnp.maximum(x, 0.0) + jnp.log1p(jnp.exp(-jnp.abs(x)))


def _expm1_nonpos(x):
    poly = x * (1.0 + x * (0.5 + x * (1.0 / 6 + x * (1.0 / 24 + x * (1.0 / 120 + x * (1.0 / 720))))))
    return jnp.where(x > -0.25, poly, jnp.exp(x) - 1.0)


def _rms(x):
    return x * lax.rsqrt(jnp.mean(x * x, axis=-1, keepdims=True) + EPS)


def _causal_conv(x, w, width):
    y = w[width - 1:width, :] * x
    for j in range(width - 1):
        y = y + w[j:j + 1, :] * shift_rows(x, width - 1 - j)
    return y


def _norm_fn(x, g):
    return _rms(x) * g


def _ffn_act_fn(ug, uv, wg, wv, bg, bv):
    return jax.nn.silu(_causal_conv(ug, wg, 3) + bg) * (_causal_conv(uv, wv, 3) + bv)


def _gdn_conv_fn(x, w):
    return jax.nn.silu(_causal_conv(x, w, 4))


def _lru_fn(gate, x, cw, cb, wa, ba, wx, bx, lam):
    xr = _causal_conv(x, cw, 4) + cb
    r = jax.nn.sigmoid(_bdot(xr, wa) + ba)
    i = jax.nn.sigmoid(_bdot(xr, wx) + bx)
    log_a = -LRU_C * r * _softplus(-lam)
    a = jnp.exp(log_a)
    u = jnp.sqrt(-_expm1_nonpos(2.0 * log_a)) * (i * xr)
    hs = lin_scan(a, u)
    return jax.nn.gelu(gate) * hs


def _ret_fn(q, k, v, gate, state, cos2, sin2, dmask, ktail, qdec, cdec):
    qr = q * cos2 + swap_halves(q) * sin2
    kr = (k * cos2 + swap_halves(k) * sin2) * (HEAD ** -0.5)
    scores = _bdot(qr, kr, _NT) * dmask
    o = _bdot(scores, v) + _bdot(qr * qdec, state)
    new_state = state * cdec + _bdot(kr * ktail, v, _TN)
    return _rms(o) * jax.nn.silu(gate), new_state


def _pick_lane(x, lane_idx):
    lane = lax.broadcasted_iota(jnp.int32, x.shape, 1)
    return jnp.sum(jnp.where(lane == lane_idx, x, 0.0), axis=1, keepdims=True)


def _l2norm(x):
    return x * lax.rsqrt(jnp.sum(x * x, axis=-1, keepdims=True) + EPS)


def _gdn_fn(qc, kc, vc, gate, small, a_log, dt_bias, gain, state, head):
    c = GDN_CHUNK
    q = _l2norm(qc) * (HEAD ** -0.5)
    k = _l2norm(kc)
    beta = jax.nn.sigmoid(_pick_lane(small, head))
    a_in = _pick_lane(small, head + N_HEADS)
    g = -jnp.exp(_pick_lane(a_log, head)) * _softplus(a_in + _pick_lane(dt_bias, head))
    i = lax.broadcasted_iota(jnp.int32, (c, c), 0)
    j = lax.broadcasted_iota(jnp.int32, (c, c), 1)
    tril = i >= j
    gc_rows = _hdot(tril.astype(F32), jnp.broadcast_to(g, (c, c)))
    gc = gc_rows[:, :1]
    decay = jnp.where(tril, jnp.exp(jnp.where(tril, gc_rows - gc_rows.T, 0.0)), 0.0)
    kb = k * beta
    lmat = jnp.where(i > j, _bdot(kb, k, _NT) * decay, 0.0)
    inv = unit_lower_inverse(lmat)
    u = _hdot(inv, vc * beta)
    w = _hdot(inv, kb * jnp.exp(gc))
    attn = jnp.where(tril, _bdot(q, k, _NT) * decay, 0.0)
    g_last = jnp.sum(g, axis=0, keepdims=True)
    v_new = u - _bdot(w, state)
    o = _bdot(q * jnp.exp(gc), state) + _bdot(attn, v_new)
    new_state = state * jnp.exp(g_last) + _bdot(k * jnp.exp(g_last - gc), v_new, _TN)
    return _rms(o) * gain * jax.nn.silu(gate), new_state


def _final_fn(h, g, target):
    y = _rms(h) * g
    return 0.5 * jnp.sum(jnp.mean(jnp.square(y - target), axis=-1, keepdims=True), axis=0, keepdims=True)


def _tile(n, candidates):
    for t in candidates:
        if n % t == 0:
            return t
    raise ValueError(f"no tile for {n}")


def matmul(a, b, *, ta=False, tb=False, add=None, out_dtype=F32, tm=None, tn=None, name):
    m = a.shape[1] if ta else a.shape[0]
    k = a.shape[0] if ta else a.shape[1]
    n = b.shape[0] if tb else b.shape[1]
    assert k == (b.shape[1] if tb else b.shape[0])
    tm = tm or _tile(m, (1024, 512, 1408, 256, 128))
    tn = tn or _tile(n, (512, 1408, 256, 128))
    dims = (((0 if ta else 1,), (1 if tb else 0,)), ((), ()))

    def body(*refs):
        if add is None:
            a_ref, b_ref, o_ref = refs
        else:
            a_ref, b_ref, r_ref, o_ref = refs
        acc = lax.dot_general(a_ref[...].astype(BF16), b_ref[...].astype(BF16), dims, preferred_element_type=F32)
        if add is not None:
            acc = acc + r_ref[...]
        o_ref[...] = acc.astype(out_dtype)

    a_spec = pl.BlockSpec((k, tm), lambda i, j: (0, i)) if ta else pl.BlockSpec((tm, k), lambda i, j: (i, 0))
    b_spec = pl.BlockSpec((tn, k), lambda i, j: (j, 0)) if tb else pl.BlockSpec((k, tn), lambda i, j: (0, j))
    o_spec = pl.BlockSpec((tm, tn), lambda i, j: (i, j))
    in_specs, args = [a_spec, b_spec], [a, b]
    if add is not None:
        in_specs.append(o_spec)
        args.append(add)
    return pl.pallas_call(body, out_shape=_sds((m, n), out_dtype), grid=(m // tm, n // tn), in_specs=in_specs,
                          out_specs=o_spec, compiler_params=_params(), name=name)(*args)


ROW_TILE = 256


def norm_fwd(x, g, *, name):
    t, d = x.shape

    def body(x_ref, g_ref, o_ref):
        o_ref[...] = _norm_fn(x_ref[...], g_ref[...]).astype(BF16)

    return pl.pallas_call(body, out_shape=_sds((t, d), BF16), grid=(t // ROW_TILE,),
                          in_specs=[pl.BlockSpec((ROW_TILE, d), lambda i: (i, 0)), pl.BlockSpec((1, d), lambda i: (0, 0))],
                          out_specs=pl.BlockSpec((ROW_TILE, d), lambda i: (i, 0)), compiler_params=_params(), name=name)(x, g)


def norm_bwd(x, g, dy, dres, *, name):
    t, d = x.shape

    def body(x_ref, g_ref, dy_ref, dres_ref, dx_ref, dg_ref):
        _, vjp = jax.vjp(_norm_fn, x_ref[...], g_ref[...])
        dx, dg = vjp(dy_ref[...])
        dx_ref[...] = dx + dres_ref[...]

        @pl.when(pl.program_id(0) == 0)
        def _():
            dg_ref[...] = jnp.zeros_like(dg_ref)

        dg_ref[...] += dg

    row = pl.BlockSpec((ROW_TILE, d), lambda i: (i, 0))
    vec = pl.BlockSpec((1, d), lambda i: (0, 0))
    return pl.pallas_call(body, out_shape=(_sds((t, d), F32), _sds((1, d), F32)), grid=(t // ROW_TILE,),
                          in_specs=[row, vec, row, row], out_specs=(row, vec), compiler_params=_params(), name=name)(x, g, dy, dres)


def final_fwd_bwd(h, g, target, *, name):
    t, d = h.shape

    def body(h_ref, g_ref, t_ref, loss_ref, dh_ref, dg_ref):
        tgt = t_ref[...]
        loss, vjp = jax.vjp(lambda hh, gg: _final_fn(hh, gg, tgt), h_ref[...], g_ref[...])
        dh, dg = vjp(jnp.ones((1, 1), F32))
        dh_ref[...] = dh

        @pl.when(pl.program_id(0) == 0)
        def _():
            dg_ref[...] = jnp.zeros_like(dg_ref)
            loss_ref[...] = jnp.zeros_like(loss_ref)

        dg_ref[...] += dg
        loss_ref[...] += jnp.broadcast_to(loss, loss_ref.shape)

    row = pl.BlockSpec((ROW_TILE, d), lambda i: (i, 0))
    vec = pl.BlockSpec((1, d), lambda i: (0, 0))
    return pl.pallas_call(body, out_shape=(_sds((1, LANES), F32), _sds((t, d), F32), _sds((1, d), F32)), grid=(t // ROW_TILE,),
                          in_specs=[row, vec, row], out_specs=(pl.BlockSpec((1, LANES), lambda i: (0, 0)), row, vec),
                          compiler_params=_params(), name=name)(h, g, target)


FFN_FWD_COLS = 256
FFN_BWD_COLS = 128


def ffn_act_fwd(u, cw, cb, *, name):
    t = u.shape[0]
    w = FFN_FWD_COLS
    nb = D_FF // w

    def body(ug_ref, uv_ref, wg_ref, wv_ref, bg_ref, bv_ref, o_ref):
        o_ref[...] = _ffn_act_fn(ug_ref[...], uv_ref[...], wg_ref[...], wv_ref[...], bg_ref[...], bv_ref[...]).astype(BF16)

    def col(rows, off):
        return pl.BlockSpec((rows, w), lambda j: (0, j + off))

    return pl.pallas_call(body, out_shape=_sds((t, D_FF), BF16), grid=(nb,),
                          in_specs=[col(t, 0), col(t, nb), col(3, 0), col(3, nb), col(1, 0), col(1, nb)],
                          out_specs=col(t, 0), compiler_params=_params(), name=name)(u, u, cw, cw, cb, cb)


def ffn_act_bwd(u, cw, cb, da, *, name):
    t = u.shape[0]
    w = FFN_BWD_COLS
    nb = D_FF // w

    def body(ug_ref, uv_ref, wg_ref, wv_ref, bg_ref, bv_ref, da_ref, dug_ref, duv_ref, dwg_ref, dwv_ref, dbg_ref, dbv_ref):
        _, vjp = jax.vjp(_ffn_act_fn, ug_ref[...], uv_ref[...], wg_ref[...], wv_ref[...], bg_ref[...], bv_ref[...])
        dug, duv, dwg, dwv, dbg, dbv = vjp(da_ref[...])
        dug_ref[...] = dug.astype(BF16)
        duv_ref[...] = duv.astype(BF16)
        dwg_ref[...] = dwg
        dwv_ref[...] = dwv
        dbg_ref[...] = dbg
        dbv_ref[...] = dbv

    def col(rows, off):
        return pl.BlockSpec((rows, w), lambda j: (0, j + off))

    outs = pl.pallas_call(
        body, out_shape=(_sds((t, D_FF), BF16), _sds((t, D_FF), BF16), _sds((3, D_FF), F32), _sds((3, D_FF), F32),
                         _sds((1, D_FF), F32), _sds((1, D_FF), F32)),
        grid=(nb,), in_specs=[col(t, 0), col(t, nb), col(3, 0), col(3, nb), col(1, 0), col(1, nb), col(t, 0)],
        out_specs=(col(t, 0), col(t, 0), col(3, 0), col(3, 0), col(1, 0), col(1, 0)), compiler_params=_params(), name=name,
    )(u, u, cw, cw, cb, cb, da)
    dug, duv, dwg, dwv, dbg, dbv = outs
    return dug, duv, jnp.concatenate([dwg, dwv], axis=1), jnp.concatenate([dbg, dbv], axis=1)


GDN_CONV_COLS = 256
GDN_CONV_OFF = 4 * GROUP


def gdn_conv_fwd(p, cw, *, name):
    t = p.shape[0]
    w = GDN_CONV_COLS
    nb = 3 * GROUP // w
    off = GDN_CONV_OFF // w

    def body(x_ref, w_ref, o_ref):
        o_ref[...] = _gdn_conv_fn(x_ref[...], w_ref[...])

    return pl.pallas_call(body, out_shape=_sds((t, 3 * GROUP), F32), grid=(nb,),
                          in_specs=[pl.BlockSpec((t, w), lambda j: (0, j + off)), pl.BlockSpec((4, w), lambda j: (0, j))],
                          out_specs=pl.BlockSpec((t, w), lambda j: (0, j)), compiler_params=_params(), name=name)(p, cw)


def gdn_conv_bwd(p, cw, dc, *, name):
    t = p.shape[0]
    w = GDN_CONV_COLS
    nb = 3 * GROUP // w
    off = GDN_CONV_OFF // w

    def body(x_ref, w_ref, dc_ref, dx_ref, dw_ref):
        _, vjp = jax.vjp(_gdn_conv_fn, x_ref[...], w_ref[...])
        dx, dw = vjp(dc_ref[...])
        dx_ref[...] = dx.astype(BF16)
        dw_ref[...] = dw

    blk = pl.BlockSpec((t, w), lambda j: (0, j))
    wblk = pl.BlockSpec((4, w), lambda j: (0, j))
    return pl.pallas_call(body, out_shape=(_sds((t, 3 * GROUP), BF16), _sds((4, 3 * GROUP), F32)), grid=(nb,),
                          in_specs=[pl.BlockSpec((t, w), lambda j: (0, j + off)), wblk, blk], out_specs=(blk, wblk),
                          compiler_params=_params(), name=name)(p, cw, dc)


def _lru_specs(t):
    w = D_MODEL // LRU_BLOCKS
    gate = pl.BlockSpec((t, w), lambda j: (0, j))
    xin = pl.BlockSpec((t, w), lambda j: (0, j + LRU_BLOCKS))
    cw = pl.BlockSpec((4, w), lambda j: (0, j))
    vec = pl.BlockSpec((1, w), lambda j: (0, j))
    mat = pl.BlockSpec((None, w, w), lambda j: (j, 0, 0))
    return gate, xin, cw, vec, mat


def lru_fwd(gx, cw, cb, wa, ba, wx, bx, lam, *, name):
    t = gx.shape[0]
    gate, xin, cws, vec, mat = _lru_specs(t)

    def body(g_ref, x_ref, cw_ref, cb_ref, wa_ref, ba_ref, wx_ref, bx_ref, lam_ref, o_ref):
        o_ref[...] = _lru_fn(g_ref[...], x_ref[...], cw_ref[...], cb_ref[...], wa_ref[...], ba_ref[...], wx_ref[...],
                             bx_ref[...], lam_ref[...]).astype(BF16)

    return pl.pallas_call(body, out_shape=_sds((t, D_MODEL), BF16), grid=(LRU_BLOCKS,),
                          in_specs=[gate, xin, cws, vec, mat, vec, mat, vec, vec], out_specs=gate,
                          compiler_params=_params(), name=name)(gx, gx, cw, cb, wa, ba, wx, bx, lam)


def lru_bwd(gx, cw, cb, wa, ba, wx, bx, lam, dy, *, name):
    t = gx.shape[0]
    gate, xin, cws, vec, mat = _lru_specs(t)

    def body(g_ref, x_ref, cw_ref, cb_ref, wa_ref, ba_ref, wx_ref, bx_ref, lam_ref, dy_ref,
             dg_ref, dx_ref, dcw_ref, dcb_ref, dwa_ref, dba_ref, dwx_ref, dbx_ref, dlam_ref):
        _, vjp = jax.vjp(_lru_fn, g_ref[...], x_ref[...], cw_ref[...], cb_ref[...], wa_ref[...], ba_ref[...], wx_ref[...],
                         bx_ref[...], lam_ref[...])
        dg, dx, dcw, dcb, dwa, dba, dwx, dbx, dlam = vjp(dy_ref[...])
        dg_ref[...] = dg.astype(BF16)
        dx_ref[...] = dx.astype(BF16)
        dcw_ref[...] = dcw
        dcb_ref[...] = dcb
        dwa_ref[...] = dwa
        dba_ref[...] = dba
        dwx_ref[...] = dwx
        dbx_ref[...] = dbx
        dlam_ref[...] = dlam

    d = D_MODEL
    w = d // LRU_BLOCKS
    out_shape = (_sds((t, d), BF16), _sds((t, d), BF16), _sds((4, d), F32), _sds((1, d), F32), _sds((LRU_BLOCKS, w, w), F32),
                 _sds((1, d), F32), _sds((LRU_BLOCKS, w, w), F32), _sds((1, d), F32), _sds((1, d), F32))
    return pl.pallas_call(body, out_shape=out_shape, grid=(LRU_BLOCKS,),
                          in_specs=[gate, xin, cws, vec, mat, vec, mat, vec, vec, gate],
                          out_specs=(gate, gate, cws, vec, mat, vec, mat, vec, vec), compiler_params=_params(), name=name,
                          )(gx, gx, cw, cb, wa, ba, wx, bx, lam, dy)


def _ret_tables():
    half = HEAD // 2
    inv_freq = (np.float32(ROPE_BASE) ** (-np.arange(half, dtype=np.float32) / np.float32(half))).astype(np.float32)
    ang = (np.arange(SEQ, dtype=np.float32)[:, None] * inv_freq[None, :]).astype(np.float64)
    cos2 = np.concatenate([np.cos(ang), np.cos(ang)], axis=1).astype(np.float32)
    sin2 = np.concatenate([-np.sin(ang), np.sin(ang)], axis=1).astype(np.float32)
    c = RET_CHUNK
    log_gamma = np.log1p(-np.exp2(-5.0 - np.arange(N_HEADS, dtype=np.float64)))
    idx = np.arange(c, dtype=np.float64)
    rel = idx[:, None] - idx[None, :]
    dmask = np.where(rel >= 0, np.exp(log_gamma[:, None, None] * np.maximum(rel, 0.0)), 0.0)
    ones = np.ones((N_HEADS, c, HEAD))
    ktail = np.exp(log_gamma[:, None] * (c - 1 - idx))[:, :, None] * ones
    qdec = np.exp(log_gamma[:, None] * (idx + 1.0))[:, :, None] * ones
    cdec = np.exp(log_gamma * c)[:, None, None] * ones
    return tuple(jnp.asarray(a, F32) for a in (cos2, sin2, dmask, ktail, qdec, cdec))


def _ret_specs(rev):
    c = RET_CHUNK
    nc = SEQ // c

    def n_of(n):
        return nc - 1 - n if rev else n

    def group(off):
        return pl.BlockSpec((c, GROUP), lambda n: (n_of(n), off))

    tab = pl.BlockSpec((c, HEAD), lambda n: (n_of(n), 0))
    const = pl.BlockSpec((N_HEADS, c, HEAD), lambda n: (0, 0, 0))
    state = pl.BlockSpec((N_HEADS, None, HEAD, HEAD), lambda n: (0, n_of(n), 0, 0))
    return group, tab, const, state, nc


def _head(h):
    return slice(h * HEAD, (h + 1) * HEAD)


def ret_fwd(p, tables, *, name):
    group, tab, const, state, nc = _ret_specs(False)

    def body(q_ref, k_ref, v_ref, g_ref, cos_ref, sin_ref, dm_ref, kt_ref, qd_ref, cd_ref, y_ref, st_ref, s_scr):
        @pl.when(pl.program_id(0) == 0)
        def _():
            s_scr[...] = jnp.zeros_like(s_scr)

        cos2, sin2 = cos_ref[...], sin_ref[...]
        for h in range(N_HEADS):
            s = s_scr[h]
            st_ref[h] = s
            y, s_new = _ret_fn(q_ref[:, _head(h)], k_ref[:, _head(h)], v_ref[:, _head(h)], g_ref[:, _head(h)], s, cos2, sin2,
                               dm_ref[h], kt_ref[h], qd_ref[h], cd_ref[h])
            y_ref[:, _head(h)] = y.astype(BF16)
            s_scr[h] = s_new

    return pl.pallas_call(
        body, out_shape=(_sds((SEQ, GROUP), BF16), _sds((N_HEADS, nc, HEAD, HEAD), F32)), grid=(nc,),
        in_specs=[group(0), group(1), group(2), group(3), tab, tab, const, const, const, const],
        out_specs=(group(0), state), scratch_shapes=[pltpu.VMEM((N_HEADS, HEAD, HEAD), F32)], compiler_params=_params(), name=name,
    )(p, p, p, p, *tables)


def ret_bwd(p, tables, states, dy, *, name):
    group, tab, const, state, nc = _ret_specs(True)

    def body(q_ref, k_ref, v_ref, g_ref, cos_ref, sin_ref, dm_ref, kt_ref, qd_ref, cd_ref, st_ref, dy_ref,
             dq_ref, dk_ref, dv_ref, dg_ref, ds_scr):
        @pl.when(pl.program_id(0) == 0)
        def _():
            ds_scr[...] = jnp.zeros_like(ds_scr)

        cos2, sin2 = cos_ref[...], sin_ref[...]
        for h in range(N_HEADS):
            consts = (cos2, sin2, dm_ref[h], kt_ref[h], qd_ref[h], cd_ref[h])
            _, vjp = jax.vjp(lambda q, k, v, g, s: _ret_fn(q, k, v, g, s, *consts), q_ref[:, _head(h)], k_ref[:, _head(h)],
                             v_ref[:, _head(h)], g_ref[:, _head(h)], st_ref[h])
            dq, dk, dv, dg, ds = vjp((dy_ref[:, _head(h)], ds_scr[h]))
            dq_ref[:, _head(h)] = dq.astype(BF16)
            dk_ref[:, _head(h)] = dk.astype(BF16)
            dv_ref[:, _head(h)] = dv.astype(BF16)
            dg_ref[:, _head(h)] = dg.astype(BF16)
            ds_scr[h] = ds

    out = _sds((SEQ, GROUP), BF16)
    return pl.pallas_call(
        body, out_shape=(out, out, out, out), grid=(nc,),
        in_specs=[group(0), group(1), group(2), group(3), tab, tab, const, const, const, const, state, group(0)],
        out_specs=(group(0), group(0), group(0), group(0)), scratch_shapes=[pltpu.VMEM((N_HEADS, HEAD, HEAD), F32)],
        compiler_params=_params(), name=name,
    )(p, p, p, p, *tables, states, dy)


def _gdn_specs(rev):
    c = GDN_CHUNK
    nc = SEQ // c

    def n_of(n):
        return nc - 1 - n if rev else n

    def group(off):
        return pl.BlockSpec((c, GROUP), lambda n: (n_of(n), off))

    small = pl.BlockSpec((c, LANES), lambda n: (n_of(n), 0))
    vec = pl.BlockSpec((1, LANES), lambda n: (0, 0))
    state = pl.BlockSpec((N_HEADS, None, HEAD, HEAD), lambda n: (0, n_of(n), 0, 0))
    return group, small, vec, state, nc


GDN_GATE_GROUP = 7


def gdn_fwd(conv, p, small, a_log, dt_bias, gain, *, name):
    group, sm, vec, state, nc = _gdn_specs(False)

    def body(q_ref, k_ref, v_ref, g_ref, sm_ref, al_ref, dt_ref, gn_ref, y_ref, st_ref, s_scr):
        @pl.when(pl.program_id(0) == 0)
        def _():
            s_scr[...] = jnp.zeros_like(s_scr)

        shared = (sm_ref[...], al_ref[...], dt_ref[...], gn_ref[...])
        for h in range(N_HEADS):
            s = s_scr[h]
            st_ref[h] = s
            y, s_new = _gdn_fn(q_ref[:, _head(h)], k_ref[:, _head(h)], v_ref[:, _head(h)], g_ref[:, _head(h)], *shared, s, h)
            y_ref[:, _head(h)] = y.astype(BF16)
            s_scr[h] = s_new

    return pl.pallas_call(
        body, out_shape=(_sds((SEQ, GROUP), BF16), _sds((N_HEADS, nc, HEAD, HEAD), F32)), grid=(nc,),
        in_specs=[group(0), group(1), group(2), group(GDN_GATE_GROUP), sm, vec, vec, vec], out_specs=(group(0), state),
        scratch_shapes=[pltpu.VMEM((N_HEADS, HEAD, HEAD), F32)], compiler_params=_params(), name=name,
    )(conv, conv, conv, p, small, a_log, dt_bias, gain)


def gdn_bwd(conv, p, small, a_log, dt_bias, gain, states, dy, *, name):
    group, sm, vec, state, nc = _gdn_specs(True)

    def body(q_ref, k_ref, v_ref, g_ref, sm_ref, al_ref, dt_ref, gn_ref, st_ref, dy_ref,
             dq_ref, dk_ref, dv_ref, dg_ref, dsm_ref, dal_ref, ddt_ref, dgn_ref, ds_scr):
        @pl.when(pl.program_id(0) == 0)
        def _():
            ds_scr[...] = jnp.zeros_like(ds_scr)
            dal_ref[...] = jnp.zeros_like(dal_ref)
            ddt_ref[...] = jnp.zeros_like(ddt_ref)
            dgn_ref[...] = jnp.zeros_like(dgn_ref)

        shared = (sm_ref[...], al_ref[...], dt_ref[...], gn_ref[...])
        dsm_sum = dal_sum = ddt_sum = dgn_sum = None
        for h in range(N_HEADS):
            _, vjp = jax.vjp(lambda *a: _gdn_fn(*a, h), q_ref[:, _head(h)], k_ref[:, _head(h)], v_ref[:, _head(h)],
                             g_ref[:, _head(h)], *shared, st_ref[h])
            dq, dk, dv, dg, dsm, dal, ddt, dgn, ds = vjp((dy_ref[:, _head(h)], ds_scr[h]))
            dq_ref[:, _head(h)] = dq
            dk_ref[:, _head(h)] = dk
            dv_ref[:, _head(h)] = dv
            dg_ref[:, _head(h)] = dg.astype(BF16)
            ds_scr[h] = ds
            dsm_sum = dsm if h == 0 else dsm_sum + dsm
            dal_sum = dal if h == 0 else dal_sum + dal
            ddt_sum = ddt if h == 0 else ddt_sum + ddt
            dgn_sum = dgn if h == 0 else dgn_sum + dgn
        dsm_ref[...] = dsm_sum
        dal_ref[...] += dal_sum
        ddt_ref[...] += ddt_sum
        dgn_ref[...] += dgn_sum

    f = _sds((SEQ, GROUP), F32)
    pv = _sds((1, LANES), F32)
    return pl.pallas_call(
        body, out_shape=(f, f, f, _sds((SEQ, GROUP), BF16), _sds((SEQ, LANES), F32), pv, pv, pv), grid=(nc,),
        in_specs=[group(0), group(1), group(2), group(GDN_GATE_GROUP), sm, vec, vec, vec, state, group(1)],
        out_specs=(group(0), group(0), group(0), group(0), sm, vec, vec, vec), scratch_shapes=[pltpu.VMEM((N_HEADS, HEAD, HEAD), F32)],
        compiler_params=_params(), name=name,
    )(conv, conv, conv, p, small, a_log, dt_bias, gain, states, dy)


PACK_ROW_TILE = 1024


def adamw(w, g, m, v, *, name):
    r = w.shape[0]
    tr = _tile(r, (PACK_ROW_TILE, 256, 128, 64, 32, 16, 8))

    def body(w_ref, g_ref, m_ref, v_ref, d_ref, nm_ref, nv_ref):
        gg = g_ref[...]
        nm = ADAM_B1 * m_ref[...] + (1.0 - ADAM_B1) * gg
        nv = ADAM_B2 * v_ref[...] + (1.0 - ADAM_B2) * jnp.square(gg)
        m_hat = nm / (1.0 - ADAM_B1 ** ADAM_STEP)
        v_hat = nv / (1.0 - ADAM_B2 ** ADAM_STEP)
        d_ref[...] = -ADAM_LR * (m_hat / (jnp.sqrt(v_hat) + ADAM_EPS) + ADAM_WD * w_ref[...])
        nm_ref[...] = nm
        nv_ref[...] = nv

    blk = pl.BlockSpec((tr, LANES), lambda i: (i, 0))
    o = _sds((r, LANES), F32)
    return pl.pallas_call(body, out_shape=(o, o, o), grid=(r // tr,), in_specs=[blk] * 4, out_specs=(blk, blk, blk),
                          compiler_params=_params(), name=name)(w, g, m, v)


ELEMENTWISE_BLOCK_BYTES = 2 * 1024 * 1024


def _row_tile(r, c):
    best = None
    for tr in range(8, r + 1, 8):
        if r % tr == 0 and tr * c * 4 <= ELEMENTWISE_BLOCK_BYTES:
            best = tr
    if best is None:
        raise ValueError(f"no row tile for ({r}, {c})")
    return best


def _core_index():
    return lax.axis_index("c").astype(jnp.int32).reshape(1)


def _chip_index():
    return (2 * lax.axis_index("x") + lax.axis_index("y")).astype(jnp.int32).reshape(1)


def adamw_halves(w, m, v, g_own, g_sib, *, name):
    rows, c = w.shape
    r = rows // 2
    tr = _row_tile(r, c)
    nb = r // tr

    def body(c_ref, w_ref, m_ref, v_ref, own_ref, sib_ref, g_ref, d_ref, nm_ref, nv_ref):
        gg = jnp.where(pl.program_id(0) == c_ref[0], own_ref[...], sib_ref[...])
        nm = ADAM_B1 * m_ref[...] + (1.0 - ADAM_B1) * gg
        nv = ADAM_B2 * v_ref[...] + (1.0 - ADAM_B2) * jnp.square(gg)
        m_hat = nm / (1.0 - ADAM_B1 ** ADAM_STEP)
        v_hat = nv / (1.0 - ADAM_B2 ** ADAM_STEP)
        g_ref[...] = gg
        d_ref[...] = -ADAM_LR * (m_hat / (jnp.sqrt(v_hat) + ADAM_EPS) + ADAM_WD * w_ref[...])
        nm_ref[...] = nm
        nv_ref[...] = nv

    full = pl.BlockSpec((tr, c), lambda h, i, cr: (h * nb + i, 0))
    half = pl.BlockSpec((tr, c), lambda h, i, cr: (i, 0))
    o = _sds((rows, c), F32)
    gs = pltpu.PrefetchScalarGridSpec(num_scalar_prefetch=1, grid=(2, nb), in_specs=[full, full, full, half, half],
                                      out_specs=(full, full, full, full))
    return pl.pallas_call(body, out_shape=(o, o, o, o), grid_spec=gs, compiler_params=_params(), name=name)(
        _core_index(), w, m, v, g_own, g_sib)


def add_core_halves(g2, land, *, out_dtype, name):
    _, ns, r, cols = g2.shape
    tr = _row_tile(r, cols)

    def body(c_ref, a_ref, b_ref, o_ref):
        o_ref[...] = (a_ref[...] + b_ref[...]).astype(out_dtype)

    gs = pltpu.PrefetchScalarGridSpec(
        num_scalar_prefetch=1, grid=(ns, r // tr),
        in_specs=[pl.BlockSpec((None, None, tr, cols), lambda s, i, cr: (cr[0], s, i, 0)),
                  pl.BlockSpec((None, tr, cols), lambda s, i, cr: (s, i, 0))],
        out_specs=pl.BlockSpec((None, tr, cols), lambda s, i, cr: (s, i, 0)))
    return pl.pallas_call(body, out_shape=_sds((ns, r, cols), out_dtype), grid_spec=gs, compiler_params=_params(), name=name)(
        _core_index(), g2, land)


def sum_over_chips(own, land, *, scatter, name):
    _, r, cols = own.shape
    tr = _row_tile(r, cols)

    def body(mine_ref, own_ref, l0, l1, l2, l3, o_ref):
        mine = mine_ref[0]
        mine_val = own_ref[...]
        acc = None
        for s, l_ref in enumerate((l0, l1, l2, l3)):
            val = jnp.where(mine == s, mine_val, l_ref[...]).astype(F32)
            acc = val if acc is None else acc + val
        o_ref[...] = acc

    def slot(s):
        return pl.BlockSpec((None, tr, cols), lambda i, mr: (jnp.where(mr[0] == s, (s + 1) % N_SHARD, s), i, 0))

    own_spec = pl.BlockSpec((None, tr, cols), lambda i, mr: (mr[0] if scatter else 0, i, 0))
    gs = pltpu.PrefetchScalarGridSpec(num_scalar_prefetch=1, grid=(r // tr,), in_specs=[own_spec] + [slot(s) for s in range(N_SHARD)],
                                      out_specs=pl.BlockSpec((tr, cols), lambda i, mr: (i, 0)))
    return pl.pallas_call(body, out_shape=_sds((r, cols), F32), grid_spec=gs, compiler_params=_params(), name=name)(
        _chip_index(), own, land, land, land, land)


_ANY = pl.BlockSpec(memory_space=pl.ANY)


def xy_exchange(src, *, scatter, name):
    rh = src.shape[1]

    def body(src_ref, land_ref, send_sems, recv_sems, loc_sem):
        x, y, c = lax.axis_index("x"), lax.axis_index("y"), lax.axis_index("c")
        mine = 2 * x + y
        peers = [(1 - x, y), (x, 1 - y), (1 - x, 1 - y)]

        def piece(shard):
            return src_ref.at[shard] if scatter else src_ref.at[c]

        def copy(k, px, py, dst_slot):
            return pltpu.make_async_remote_copy(src_ref=piece(2 * px + py), dst_ref=land_ref.at[dst_slot], send_sem=send_sems.at[k],
                                                recv_sem=recv_sems.at[k], device_id=(px, py, c), device_id_type=MESH)

        keep = pltpu.make_async_copy(piece(mine), land_ref.at[mine], loc_sem)
        keep.start()
        sends = [copy(k, px, py, mine) for k, (px, py) in enumerate(peers)]
        for cp in sends:
            cp.start()
        for cp in sends:
            cp.wait_send()
        for k, (px, py) in enumerate(peers):
            copy(k, px, py, 2 * px + py).wait_recv()
        keep.wait()

    return pl.pallas_call(body, out_shape=_sds((N_SHARD, rh, LANES), src.dtype), in_specs=[_ANY], out_specs=_ANY,
                          scratch_shapes=[pltpu.SemaphoreType.DMA((3,)), pltpu.SemaphoreType.DMA((3,)), pltpu.SemaphoreType.DMA(())],
                          name=name)(src)


def core_exchange(src, *, send_other_half, name):
    def body(src_ref, out_ref, send_sem, recv_sem, loc_sem):
        x, y, c = lax.axis_index("x"), lax.axis_index("y"), lax.axis_index("c")
        if send_other_half:
            cp = pltpu.make_async_remote_copy(src_ref=src_ref.at[1 - c], dst_ref=out_ref, send_sem=send_sem, recv_sem=recv_sem,
                                              device_id=(x, y, 1 - c), device_id_type=MESH)
            cp.start()
            cp.wait_send()
            cp.wait_recv()
        else:
            keep = pltpu.make_async_copy(src_ref, out_ref.at[c], loc_sem)
            keep.start()
            cp = pltpu.make_async_remote_copy(src_ref=src_ref, dst_ref=out_ref.at[c], send_sem=send_sem, recv_sem=recv_sem,
                                              device_id=(x, y, 1 - c), device_id_type=MESH)
            cp.start()
            cp.wait_send()
            pltpu.make_async_remote_copy(src_ref=src_ref, dst_ref=out_ref.at[1 - c], send_sem=send_sem, recv_sem=recv_sem,
                                         device_id=(x, y, 1 - c), device_id_type=MESH).wait_recv()
            keep.wait()

    out_shape = _sds(src.shape[1:], src.dtype) if send_other_half else _sds((2,) + src.shape, src.dtype)
    return pl.pallas_call(body, out_shape=out_shape, in_specs=[_ANY], out_specs=_ANY,
                          scratch_shapes=[pltpu.SemaphoreType.DMA(()), pltpu.SemaphoreType.DMA(()), pltpu.SemaphoreType.DMA(())],
                          name=name)(src)


def _comm_call(body, ins, out_shapes, sem_counts, name):
    return pl.pallas_call(body, out_shape=tuple(out_shapes), in_specs=[_ANY] * len(ins), out_specs=tuple([_ANY] * len(out_shapes)),
                          scratch_shapes=[pltpu.SemaphoreType.DMA((k,)) for k in sem_counts], name=name)(*ins)


def _xy_peers(x, y):
    return [(1 - x, y), (x, 1 - y), (1 - x, 1 - y)]


def gather_halves(halves, *, name):
    n = len(halves)

    def body(*refs):
        ins, lands, sibs = refs[:n], refs[n:2 * n], refs[2 * n:3 * n]
        ici_send, ici_recv, d2d_send, d2d_recv = refs[3 * n:]
        x, y, c = lax.axis_index("x"), lax.axis_index("y"), lax.axis_index("c")
        mine = 2 * x + y
        peers = _xy_peers(x, y)

        def ici(i, k, slot):
            px, py = peers[k]
            return pltpu.make_async_remote_copy(src_ref=ins[i].at[c], dst_ref=lands[i].at[slot], send_sem=ici_send.at[3 * i + k],
                                                recv_sem=ici_recv.at[3 * i + k], device_id=(px, py, c), device_id_type=MESH)

        def pass_on(i, k):
            px, py = peers[k]
            slot = 2 * px + py
            return pltpu.make_async_remote_copy(src_ref=lands[i].at[slot], dst_ref=sibs[i].at[slot], send_sem=d2d_send.at[3 * i + k],
                                                recv_sem=d2d_recv.at[3 * i + k], device_id=(x, y, 1 - c), device_id_type=MESH)

        sends = [ici(i, k, mine) for i in range(n) for k in range(3)]
        for cp in sends:
            cp.start()
        passed = []
        for i in range(n):
            for k in range(3):
                px, py = peers[k]
                ici(i, k, 2 * px + py).wait_recv()
                cp = pass_on(i, k)
                cp.start()
                passed.append(cp)
        for cp in passed:
            cp.wait_recv()
        for cp in sends + passed:
            cp.wait_send()

    outs = [_sds((N_SHARD,) + h.shape[1:], h.dtype) for h in halves]
    res = _comm_call(body, halves, outs + outs, [3 * n] * 4, name)
    return res[:n], res[n:]


def send_other_half(arrays, *, name):
    n = len(arrays)

    def body(*refs):
        ins, lands = refs[:n], refs[n:2 * n]
        send_sems, recv_sems = refs[2 * n:]
        x, y, c = lax.axis_index("x"), lax.axis_index("y"), lax.axis_index("c")
        copies = [pltpu.make_async_remote_copy(src_ref=ins[i].at[1 - c], dst_ref=lands[i], send_sem=send_sems.at[i],
                                               recv_sem=recv_sems.at[i], device_id=(x, y, 1 - c), device_id_type=MESH) for i in range(n)]
        for cp in copies:
            cp.start()
        for cp in copies:
            cp.wait_recv()
        for cp in copies:
            cp.wait_send()

    return _comm_call(body, arrays, [_sds(a.shape[1:], a.dtype) for a in arrays], [n, n], name)


def send_to_chips(arrays, scatter, *, name):
    n = len(arrays)

    def body(*refs):
        ins, lands = refs[:n], refs[n:2 * n]
        send_sems, recv_sems = refs[2 * n:]
        x, y, c = lax.axis_index("x"), lax.axis_index("y"), lax.axis_index("c")
        mine = 2 * x + y
        peers = _xy_peers(x, y)

        def copy(i, k, dst_slot):
            px, py = peers[k]
            src = ins[i].at[2 * px + py] if scatter[i] else ins[i].at[0]
            return pltpu.make_async_remote_copy(src_ref=src, dst_ref=lands[i].at[dst_slot], send_sem=send_sems.at[3 * i + k],
                                                recv_sem=recv_sems.at[3 * i + k], device_id=(px, py, c), device_id_type=MESH)

        sends = [copy(i, k, mine) for i in range(n) for k in range(3)]
        for cp in sends:
            cp.start()
        for i in range(n):
            for k in range(3):
                px, py = peers[k]
                copy(i, k, 2 * px + py).wait_recv()
        for cp in sends:
            cp.wait_send()

    return _comm_call(body, arrays, [_sds((N_SHARD,) + a.shape[1:], a.dtype) for a in arrays], [3 * n, 3 * n], name)


def swap_with_other_core(arrays, *, name):
    n = len(arrays)

    def body(*refs):
        ins, lands = refs[:n], refs[n:2 * n]
        send_sems, recv_sems = refs[2 * n:]
        x, y, c = lax.axis_index("x"), lax.axis_index("y"), lax.axis_index("c")
        copies = [pltpu.make_async_remote_copy(src_ref=ins[i], dst_ref=lands[i], send_sem=send_sems.at[i], recv_sem=recv_sems.at[i],
                                               device_id=(x, y, 1 - c), device_id_type=MESH) for i in range(n)]
        for cp in copies:
            cp.start()
        for cp in copies:
            cp.wait_recv()
        for cp in copies:
            cp.wait_send()

    return _comm_call(body, arrays, [_sds(a.shape, a.dtype) for a in arrays], [n, n], name)


def _pack_rows(n_elems, row_multiple):
    rows = -(-n_elems // LANES)
    return -(-rows // row_multiple) * row_multiple


def _pack(arrays, rows, dtype):
    flat = jnp.concatenate([a.reshape(-1).astype(dtype) for a in arrays])
    return jnp.pad(flat, (0, rows * LANES - flat.shape[0])).reshape(rows, LANES)


def _unpack(packed, shapes):
    flat = packed.reshape(-1)
    out, off = [], 0
    for s in shapes:
        n = int(np.prod(s))
        out.append(flat[off:off + n].reshape(s))
        off += n
    return out


def all_gather_shards(shards, axes, dtype, row_multiple, tag):
    shapes = [s.shape for s in shards]
    rows = _pack_rows(sum(int(np.prod(s)) for s in shapes), row_multiple)
    packed = _pack(shards, rows, dtype).reshape(2, rows // 2, LANES)
    land = xy_exchange(packed, scatter=False, name=f"gather_xy_{tag}")
    both = core_exchange(land, send_other_half=False, name=f"gather_c_{tag}")
    per_shard = jnp.swapaxes(both, 0, 1).reshape(N_SHARD, rows, LANES)
    pieces = [_unpack(per_shard[s], shapes) for s in range(N_SHARD)]
    return [jnp.concatenate([pieces[s][i] for s in range(N_SHARD)], axis=ax) for i, ax in enumerate(axes)]


def reduce_over_devices(arrays, scatter):
    land = send_other_half(arrays, name="reduce_core_send")
    chip = [add_core_halves(a, l, out_dtype=BF16 if sc else F32, name=f"reduce_core_add_{i}")
            for i, (a, l, sc) in enumerate(zip(arrays, land, scatter))]
    land = send_to_chips(chip, scatter, name="reduce_chip_send")
    own = [sum_over_chips(ch, l, scatter=sc, name=f"reduce_chip_add_{i}") for i, (ch, l, sc) in enumerate(zip(chip, land, scatter))]
    sib = swap_with_other_core(own, name="reduce_core_swap")
    return own, sib


def _ffn_layer_fwd(h, norm_g, w_up, cw, cb, w_down, tag):
    hn = norm_fwd(h, norm_g, name=f"ffn_norm_{tag}")
    u = matmul(hn, w_up, name=f"ffn_up_{tag}")
    act = ffn_act_fwd(u, cw, cb, name=f"ffn_act_{tag}")
    out = matmul(act, w_down, add=h, name=f"ffn_down_{tag}")
    return out, (h, hn, u, act)


def _ffn_layer_bwd(saved, dout, norm_g, w_up, cw, cb, w_down, tag):
    h, hn, u, act = saved
    dact = matmul(dout, w_down, tb=True, name=f"ffn_down_dx_{tag}")
    d_w_down = matmul(act, dout, ta=True, name=f"ffn_down_dw_{tag}")
    dug, duv, dcw, dcb = ffn_act_bwd(u, cw, cb, dact, name=f"ffn_act_bwd_{tag}")
    du = jnp.concatenate([dug, duv], axis=1)
    dhn = matmul(du, w_up, tb=True, name=f"ffn_up_dx_{tag}")
    d_w_up = matmul(hn, du, ta=True, name=f"ffn_up_dw_{tag}")
    dh, dg = norm_bwd(h, norm_g, dhn, dout, name=f"ffn_norm_bwd_{tag}")
    return dh, dg, d_w_up, dcw, dcb, d_w_down


def local_step(x, target, w):
    g = {}
    tables = _ret_tables()
    w_in = w["ret_gdn_w_in"]
    w_main = w_in[:, :MIX_MAIN]
    w_small = jnp.pad(w_in[:, MIX_MAIN:], ((0, 0), (0, LANES - 2 * N_HEADS)))
    a_log = jnp.pad(w["gdn_a_log"], ((0, 0), (0, LANES - N_HEADS)))
    dt_bias = jnp.pad(w["gdn_dt_bias"], ((0, 0), (0, LANES - N_HEADS)))

    hn0 = norm_fwd(x, w["norm_mix"][0:1], name="mix0_norm")
    p = matmul(hn0, w_main, name="mix0_in")
    small = matmul(hn0, w_small, name="mix0_in_small")
    y_ret, s_ret = ret_fwd(p, tables, name="ret_fwd")
    conv = gdn_conv_fwd(p, w["gdn_conv_w"], name="gdn_conv")
    y_gdn, s_gdn = gdn_fwd(conv, p, small, a_log, dt_bias, w["gdn_out_gain"], name="gdn_fwd")
    y0 = jnp.concatenate([y_ret, y_gdn], axis=1)
    h1 = matmul(y0, w["ret_gdn_w_out"], add=x, name="mix0_out")
    h2, ffn0 = _ffn_layer_fwd(h1, w["norm_ffn"][0:1], w["ffn_w_up"][0], w["ffn_conv_w"][0], w["ffn_conv_b"][0:1], w["ffn_w_down"][0], "0")

    hn1 = norm_fwd(h2, w["norm_mix"][1:2], name="mix1_norm")
    gx = matmul(hn1, w["lru_w_in"], name="mix1_in")
    lru_p = (w["lru_conv_w"], w["lru_conv_b"], w["lru_w_a"], w["lru_b_a"], w["lru_w_x"], w["lru_b_x"], w["lru_lambda"])
    y1 = lru_fwd(gx, *lru_p, name="lru_fwd")
    h3 = matmul(y1, w["lru_w_out"], add=h2, name="mix1_out")
    h4, ffn1 = _ffn_layer_fwd(h3, w["norm_ffn"][1:2], w["ffn_w_up"][1], w["ffn_conv_w"][1], w["ffn_conv_b"][1:2], w["ffn_w_down"][1], "1")

    loss, dh4, g["norm_final"] = final_fwd_bwd(h4, w["norm_final"], target, name="final")

    dh3, dgf1, dwu1, dcw1, dcb1, dwd1 = _ffn_layer_bwd(ffn1, dh4, w["norm_ffn"][1:2], w["ffn_w_up"][1], w["ffn_conv_w"][1],
                                                     w["ffn_conv_b"][1:2], w["ffn_w_down"][1], "1")
    dy1 = matmul(dh3, w["lru_w_out"], tb=True, name="mix1_out_dx")
    g["lru_w_out"] = matmul(y1, dh3, ta=True, name="mix1_out_dw")
    dgate, dxr, g["lru_conv_w"], g["lru_conv_b"], g["lru_w_a"], g["lru_b_a"], g["lru_w_x"], g["lru_b_x"], g["lru_lambda"] = lru_bwd(
        gx, *lru_p, dy1, name="lru_bwd")
    dgx = jnp.concatenate([dgate, dxr], axis=1)
    dhn1 = matmul(dgx, w["lru_w_in"], tb=True, name="mix1_in_dx")
    g["lru_w_in"] = matmul(hn1, dgx, ta=True, name="mix1_in_dw")
    dh2, dgm1 = norm_bwd(h2, w["norm_mix"][1:2], dhn1, dh3, name="mix1_norm_bwd")

    dh1, dgf0, dwu0, dcw0, dcb0, dwd0 = _ffn_layer_bwd(ffn0, dh2, w["norm_ffn"][0:1], w["ffn_w_up"][0], w["ffn_conv_w"][0],
                                                     w["ffn_conv_b"][0:1], w["ffn_w_down"][0], "0")
    dy0 = matmul(dh1, w["ret_gdn_w_out"], tb=True, name="mix0_out_dx")
    g["ret_gdn_w_out"] = matmul(y0, dh1, ta=True, name="mix0_out_dw")
    dq_r, dk_r, dv_r, dg_r = ret_bwd(p, tables, s_ret, dy0, name="ret_bwd")
    dcq, dck, dcv, dg_d, dsmall, dal, ddt, dgain = gdn_bwd(conv, p, small, a_log, dt_bias, w["gdn_out_gain"], s_gdn, dy0, name="gdn_bwd")
    dconv = jnp.concatenate([dcq, dck, dcv], axis=1)
    dp_conv, g["gdn_conv_w"] = gdn_conv_bwd(p, w["gdn_conv_w"], dconv, name="gdn_conv_bwd")
    dp = jnp.concatenate([dq_r, dk_r, dv_r, dg_r, dp_conv, dg_d], axis=1)
    dhn0 = matmul(dp, w_main, tb=True, name="mix0_in_dx")
    dhn0 = matmul(dsmall, w_small, tb=True, add=dhn0, name="mix0_in_small_dx")
    d_w_main = matmul(hn0, dp, ta=True, name="mix0_in_dw")
    d_w_small = matmul(hn0, dsmall, ta=True, name="mix0_in_small_dw")
    g["ret_gdn_w_in"] = jnp.concatenate([d_w_main, d_w_small[:, :2 * N_HEADS]], axis=1)
    dx, dgm0 = norm_bwd(x, w["norm_mix"][0:1], dhn0, dh1, name="mix0_norm_bwd")

    g["gdn_a_log"] = dal[:, :N_HEADS]
    g["gdn_dt_bias"] = ddt[:, :N_HEADS]
    g["gdn_out_gain"] = dgain
    g["norm_mix"] = jnp.concatenate([dgm0, dgm1], axis=0)
    g["norm_ffn"] = jnp.concatenate([dgf0, dgf1], axis=0)
    g["ffn_w_up"] = jnp.stack([dwu0, dwu1])
    g["ffn_conv_w"] = jnp.stack([dcw0, dcw1])
    g["ffn_conv_b"] = jnp.concatenate([dcb0, dcb1], axis=0)
    g["ffn_w_down"] = jnp.stack([dwd0, dwd1])
    return loss, dx, g


WEIGHTS = ("norm_mix", "norm_ffn", "ret_gdn_w_in", "gdn_conv_w", "gdn_a_log", "gdn_dt_bias", "gdn_out_gain", "ret_gdn_w_out",
           "lru_w_in", "lru_conv_w", "lru_conv_b", "lru_w_a", "lru_b_a", "lru_w_x", "lru_b_x", "lru_lambda", "lru_w_out",
           "ffn_w_up", "ffn_conv_w", "ffn_conv_b", "ffn_w_down", "norm_final")
MATMUL_SHARDED = {"ret_gdn_w_in": 1, "ret_gdn_w_out": 0, "lru_w_in": 1, "lru_w_out": 0, "ffn_w_up": 2, "ffn_w_down": 1}
VECTOR_SHARDED = {"gdn_conv_w": 1, "lru_conv_w": 1, "lru_conv_b": 1, "lru_b_a": 1, "lru_b_x": 1, "lru_lambda": 1, "ffn_conv_w": 2}
SHARDED = {**MATMUL_SHARDED, **VECTOR_SHARDED}
REPLICATED = tuple(n for n in WEIGHTS if n not in SHARDED)
SQUEEZE = {"ret_gdn_w_in", "gdn_conv_w", "ret_gdn_w_out", "lru_w_in", "lru_conv_w", "lru_w_a", "lru_w_x", "lru_w_out"}
MIX_IN = MIX_MAIN + 2 * N_HEADS
BIG_LAYOUT = {
    "ret_gdn_w_in": ((D_MODEL, MIX_IN), (2, D_MODEL // 2, N_SHARD, MIX_IN // N_SHARD), (0, 2, 1, 3)),
    "ret_gdn_w_out": ((2 * GROUP, D_MODEL), (N_SHARD, 2, GROUP // N_SHARD, D_MODEL), (1, 0, 2, 3)),
    "lru_w_in": ((D_MODEL, 2 * D_MODEL), (2, D_MODEL // 2, N_SHARD, 2 * D_MODEL // N_SHARD), (0, 2, 1, 3)),
    "lru_w_out": ((D_MODEL, D_MODEL), (N_SHARD, 2, D_MODEL // (2 * N_SHARD), D_MODEL), (1, 0, 2, 3)),
    "ffn_w_up": ((2, D_MODEL, 2 * D_FF), (2, D_MODEL, N_SHARD, 2 * D_FF // N_SHARD), (0, 2, 1, 3)),
    "ffn_w_down": ((2, D_FF, D_MODEL), (2, N_SHARD, D_FF // N_SHARD, D_MODEL), (0, 1, 2, 3)),
}


def _local_view(name, a):
    if name in SQUEEZE:
        return a[0]
    if a.ndim == 1:
        return a[None, :]
    return a


def kernel(x, norm_mix, norm_ffn, ret_gdn_w_in, gdn_conv_w, gdn_a_log, gdn_dt_bias, gdn_out_gain, ret_gdn_w_out, lru_w_in, lru_conv_w, lru_conv_b, lru_w_a, lru_b_a, lru_w_x, lru_b_x, lru_lambda, lru_w_out, ffn_w_up, ffn_conv_w, ffn_conv_b, ffn_w_down, norm_final, loss_target, m_norm_mix, m_norm_ffn, m_ret_gdn_w_in, m_gdn_conv_w, m_gdn_a_log, m_gdn_dt_bias, m_gdn_out_gain, m_ret_gdn_w_out, m_lru_w_in, m_lru_conv_w, m_lru_conv_b, m_lru_w_a, m_lru_b_a, m_lru_w_x, m_lru_b_x, m_lru_lambda, m_lru_w_out, m_ffn_w_up, m_ffn_conv_w, m_ffn_conv_b, m_ffn_w_down, m_norm_final, v_norm_mix, v_norm_ffn, v_ret_gdn_w_in, v_gdn_conv_w, v_gdn_a_log, v_gdn_dt_bias, v_gdn_out_gain, v_ret_gdn_w_out, v_lru_w_in, v_lru_conv_w, v_lru_conv_b, v_lru_w_a, v_lru_b_a, v_lru_w_x, v_lru_b_x, v_lru_lambda, v_lru_w_out, v_ffn_w_up, v_ffn_conv_w, v_ffn_conv_b, v_ffn_w_down, v_norm_final):
    given = dict(norm_mix=norm_mix, norm_ffn=norm_ffn, ret_gdn_w_in=ret_gdn_w_in, gdn_conv_w=gdn_conv_w, gdn_a_log=gdn_a_log, gdn_dt_bias=gdn_dt_bias, gdn_out_gain=gdn_out_gain, ret_gdn_w_out=ret_gdn_w_out, lru_w_in=lru_w_in, lru_conv_w=lru_conv_w, lru_conv_b=lru_conv_b, lru_w_a=lru_w_a, lru_b_a=lru_b_a, lru_w_x=lru_w_x, lru_b_x=lru_b_x, lru_lambda=lru_lambda, lru_w_out=lru_w_out, ffn_w_up=ffn_w_up, ffn_conv_w=ffn_conv_w, ffn_conv_b=ffn_conv_b, ffn_w_down=ffn_w_down, norm_final=norm_final)
    mom1 = dict(norm_mix=m_norm_mix, norm_ffn=m_norm_ffn, ret_gdn_w_in=m_ret_gdn_w_in, gdn_conv_w=m_gdn_conv_w, gdn_a_log=m_gdn_a_log, gdn_dt_bias=m_gdn_dt_bias, gdn_out_gain=m_gdn_out_gain, ret_gdn_w_out=m_ret_gdn_w_out, lru_w_in=m_lru_w_in, lru_conv_w=m_lru_conv_w, lru_conv_b=m_lru_conv_b, lru_w_a=m_lru_w_a, lru_b_a=m_lru_b_a, lru_w_x=m_lru_w_x, lru_b_x=m_lru_b_x, lru_lambda=m_lru_lambda, lru_w_out=m_lru_w_out, ffn_w_up=m_ffn_w_up, ffn_conv_w=m_ffn_conv_w, ffn_conv_b=m_ffn_conv_b, ffn_w_down=m_ffn_w_down, norm_final=m_norm_final)
    mom2 = dict(norm_mix=v_norm_mix, norm_ffn=v_norm_ffn, ret_gdn_w_in=v_ret_gdn_w_in, gdn_conv_w=v_gdn_conv_w, gdn_a_log=v_gdn_a_log, gdn_dt_bias=v_gdn_dt_bias, gdn_out_gain=v_gdn_out_gain, ret_gdn_w_out=v_ret_gdn_w_out, lru_w_in=v_lru_w_in, lru_conv_w=v_lru_conv_w, lru_conv_b=v_lru_conv_b, lru_w_a=v_lru_w_a, lru_b_a=v_lru_b_a, lru_w_x=v_lru_w_x, lru_b_x=v_lru_b_x, lru_lambda=v_lru_lambda, lru_w_out=v_lru_w_out, ffn_w_up=v_ffn_w_up, ffn_conv_w=v_ffn_conv_w, ffn_conv_b=v_ffn_conv_b, ffn_w_down=v_ffn_w_down, norm_final=v_norm_final)

    local = {n: _local_view(n, a) for n, a in given.items()}

    core = lax.axis_index("c")
    chip = 2 * lax.axis_index("x") + lax.axis_index("y")
    is_my_chip = lax.broadcasted_iota(jnp.int32, (N_SHARD, 1, 1), 0) == chip

    def by_core(mine, other):
        return jnp.where(core == 0, jnp.stack([mine, other]), jnp.stack([other, mine]))

    mm_names, vec_names, rp_names = list(BIG_LAYOUT), list(VECTOR_SHARDED), list(REPLICATED)
    halves = []
    for n in mm_names:
        _, split, perm = BIG_LAYOUT[n]
        halves.append(local[n].astype(BF16).reshape((2,) + tuple(split[p] for p in perm)[2:]))
    lands, sibs = gather_halves(halves, name="gather_weights")
    full = {}
    for n, mine, land, sib in zip(mm_names, halves, lands, sibs):
        full_shape, split, perm = BIG_LAYOUT[n]
        half_mine = jnp.where(is_my_chip, jnp.where(core == 0, mine[0], mine[1])[None], land)
        half_other = jnp.where(is_my_chip, jnp.where(core == 0, mine[1], mine[0])[None], sib)
        full[n] = by_core(half_mine, half_other).transpose(perm).reshape(full_shape)
    full.update(zip(vec_names, all_gather_shards([local[n] for n in vec_names], [SHARDED[n] for n in vec_names], F32, 32, "p")))
    for n in rp_names:
        full[n] = local[n]

    loss_part, dx, grads = local_step(x[0], loss_target[0], full)
    loss = lax.psum(loss_part[0, 0], ("x", "y", "c"))

    small_names = rp_names + vec_names
    small_shapes = [grads[n].shape for n in small_names]
    small_rows = _pack_rows(sum(int(np.prod(s)) for s in small_shapes), 16)
    small = _pack([grads[n] for n in small_names], small_rows, F32).reshape(2, 1, small_rows // 2, LANES)
    arrays = [grads[n].reshape(BIG_LAYOUT[n][1]).transpose(BIG_LAYOUT[n][2]) for n in mm_names] + [small]
    g_own, g_sib = reduce_over_devices(arrays, [True] * len(mm_names) + [False])

    result = {}
    for i, n in enumerate(mm_names):
        r, cols = g_own[i].shape
        w2, m2, v2 = (_local_view(n, a).reshape(2 * r, cols) for a in (given[n], mom1[n], mom2[n]))
        result[n] = adamw_halves(w2, m2, v2, g_own[i], g_sib[i], name=f"adamw_{n}")

    g_small = dict(zip(small_names, _unpack(by_core(g_own[-1], g_sib[-1]).reshape(small_rows, LANES), small_shapes)))
    for n in vec_names:
        size = local[n].shape[SHARDED[n]]
        g_small[n] = lax.dynamic_slice_in_dim(g_small[n], chip * size, size, axis=SHARDED[n])
    loc_shapes = [local[n].shape for n in small_names]
    loc_rows = _pack_rows(sum(int(np.prod(s)) for s in loc_shapes), 8)
    packs = [_pack([src[n] for n in small_names], loc_rows, F32) for src in (given, g_small, mom1, mom2)]
    d_s, m_s, v_s = adamw(*packs, name="adamw_small")
    for n, d, nm, nv in zip(small_names, _unpack(d_s, loc_shapes), _unpack(m_s, loc_shapes), _unpack(v_s, loc_shapes)):
        result[n] = (g_small[n], d, nm, nv)

    outs = [[result[n][k].reshape(given[n].shape) for n in WEIGHTS] for k in range(4)]
    return (loss, dx[None], *outs[0], *outs[1], *outs[2], *outs[3])
```

```python
import functools

import numpy as np
import jax
import jax.numpy as jnp
from jax import lax
from jax.experimental import pallas as pl
from jax.experimental.pallas import tpu as pltpu

F32 = jnp.float32
BF16 = jnp.bfloat16
HI = lax.Precision.HIGHEST
MESH = pl.DeviceIdType.MESH

SEQ = 2048
D_MODEL = 1024
N_HEADS = 4
HEAD = 128
RET_CHUNK = 128
GDN_CHUNK = 64
GROUP = N_HEADS * HEAD
MIX_MAIN = 8 * GROUP
D_FF = 2816
LRU_BLOCKS = 8
LRU_C = 8.0
ROPE_BASE = 10000.0
EPS = 1e-6
N_SHARD = 4
LANES = 128

ADAM_LR, ADAM_B1, ADAM_B2, ADAM_EPS, ADAM_WD, ADAM_STEP = 0.001, 0.9, 0.999, 1e-08, 0.01, 10

VMEM_LIMIT_BYTES = 56 * 1024 * 1024

_roll = pltpu.roll


def _params(**kw):
    return pltpu.CompilerParams(vmem_limit_bytes=VMEM_LIMIT_BYTES, **kw)


def _sds(shape, dtype):
    return jax.ShapeDtypeStruct(tuple(shape), dtype)


def _shift_raw(x, d):
    n = x.shape[0]
    t = lax.broadcasted_iota(jnp.int32, x.shape, 0)
    if d > 0:
        return jnp.where(t >= d, _roll(x, d, 0), 0.0)
    return jnp.where(t < n + d, _roll(x, n + d, 0), 0.0)


@functools.partial(jax.custom_vjp, nondiff_argnums=(1,))
def shift_rows(x, d):
    return _shift_raw(x, d)


def _shift_fwd(x, d):
    return _shift_raw(x, d), None


def _shift_bwd(d, _, g):
    return (_shift_raw(g, -d),)


shift_rows.defvjp(_shift_fwd, _shift_bwd)


@jax.custom_vjp
def swap_halves(x):
    return _roll(x, HEAD // 2, 1)


def _swap_fwd(x):
    return _roll(x, HEAD // 2, 1), None


def _swap_bwd(_, g):
    return (_roll(g, HEAD // 2, 1),)


swap_halves.defvjp(_swap_fwd, _swap_bwd)


def _scan_raw(a, u, reverse):
    n = a.shape[0]
    t = lax.broadcasted_iota(jnp.int32, a.shape, 0)
    d = 1
    while d < n:
        if reverse:
            m = t < n - d
            a_s, u_s = _roll(a, n - d, 0), _roll(u, n - d, 0)
        else:
            m = t >= d
            a_s, u_s = _roll(a, d, 0), _roll(u, d, 0)
        u = a * jnp.where(m, u_s, 0.0) + u
        a = a * jnp.where(m, a_s, 1.0)
        d *= 2
    return u


@jax.custom_vjp
def lin_scan(a, u):
    return _scan_raw(a, u, False)


def _lin_scan_fwd(a, u):
    hs = _scan_raw(a, u, False)
    return hs, (a, hs)


def _lin_scan_bwd(res, g):
    a, hs = res
    lam = _scan_raw(_shift_raw(a, -1), g, True)
    return lam * _shift_raw(hs, 1), lam


lin_scan.defvjp(_lin_scan_fwd, _lin_scan_bwd)


def _bdot(a, b, dims=(((1,), (0,)), ((), ()))):
    return lax.dot_general(a.astype(BF16), b.astype(BF16), dims, preferred_element_type=F32)


def _each(f, *seqs):
    return tuple(f(*a) for a in zip(*seqs))


def _split_bf16(a):
    hi = a.astype(BF16)
    return hi, (a - hi.astype(F32)).astype(BF16)


def _dot3_raw(a_s, b_s):
    a_hl = _each(_split_bf16, a_s)
    b_hl = _each(_split_bf16, b_s)
    hh = _each(lambda a, b: _bdot(a[0], b[0]), a_hl, b_hl)
    hl = _each(lambda a, b: _bdot(a[0], b[1]), a_hl, b_hl)
    lh = _each(lambda a, b: _bdot(a[1], b[0]), a_hl, b_hl)
    return _each(lambda x, y, z: x + (y + z), hh, hl, lh)


@jax.custom_vjp
def dot3(a_s, b_s):
    return _dot3_raw(a_s, b_s)


def _dot3_fwd(a_s, b_s):
    return _dot3_raw(a_s, b_s), (a_s, b_s)


def _dot3_bwd(res, g_s):
    a_s, b_s = res
    return (_each(lambda g, b: _bdot(g, b, (((1,), (1,)), ((), ()))), g_s, b_s),
            _each(lambda a, g: _bdot(a, g, (((0,), (0,)), ((), ()))), a_s, g_s))


dot3.defvjp(_dot3_fwd, _dot3_bwd)


def _eye(n):
    i = lax.broadcasted_iota(jnp.int32, (n, n), 0)
    j = lax.broadcasted_iota(jnp.int32, (n, n), 1)
    return (i == j).astype(F32)


def _unit_lower_inverse_raw(lmats):
    n = lmats[0].shape[0]
    eye = _eye(n)
    ps = _each(lambda l: -l, lmats)
    invs = _each(lambda x: eye + x, ps)
    k = 1
    while 2 * k < n:
        ps = _each(lambda p: _bdot(p, p), ps)
        invs = _each(lambda inv, p: inv + _bdot(inv, p), invs, ps)
        k *= 2
    prods = _dot3_raw(lmats, invs)
    resids = _each(lambda inv, pr: eye - inv - pr, invs, prods)
    return _each(lambda inv, r: inv + _bdot(inv, r), invs, resids)


@jax.custom_vjp
def unit_lower_inverse(lmats):
    return _unit_lower_inverse_raw(lmats)


def _uli_fwd(lmats):
    invs = _unit_lower_inverse_raw(lmats)
    return invs, invs


def _uli_bwd(invs, g_s):
    ms = _each(lambda inv, g: _bdot(inv, g, (((0,), (0,)), ((), ()))), invs, g_s)
    return (_each(lambda m, inv: -_bdot(m, inv, (((1,), (1,)), ((), ()))), ms, invs),)


unit_lower_inverse.defvjp(_uli_fwd, _uli_bwd)


def _cumsum_raw(x, reverse):
    n = x.shape[0]
    t = lax.broadcasted_iota(jnp.int32, x.shape, 0)
    d = 1
    while d < n:
        if reverse:
            x = x + jnp.where(t < n - d, _roll(x, n - d, 0), 0.0)
        else:
            x = x + jnp.where(t >= d, _roll(x, d, 0), 0.0)
        d *= 2
    return x


@jax.custom_vjp
def cumsum_rows(x):
    return _cumsum_raw(x, False)


def _cumsum_fwd(x):
    return _cumsum_raw(x, False), None


def _cumsum_bwd(_, g):
    return (_cumsum_raw(g, True),)


cumsum_rows.defvjp(_cumsum_fwd, _cumsum_bwd)


_NT = (((1,), (1,)), ((), ()))
_TN = (((0,), (0,)), ((), ()))


def _softplus(x):
    return jnp.maximum(x, 0.0) + jnp.log1p(jnp.exp(-jnp.abs(x)))


def _expm1_nonpos(x):
    poly = x * (1.0 + x * (0.5 + x * (1.0 / 6 + x * (1.0 / 24 + x * (1.0 / 120 + x * (1.0 / 720))))))
    return jnp.where(x > -0.25, poly, jnp.exp(x) - 1.0)


def _rms(x):
    return x * lax.rsqrt(jnp.mean(x * x, axis=-1, keepdims=True) + EPS)


def _causal_conv(x, w, width):
    y = w[width - 1:width, :] * x
    for j in range(width - 1):
        y = y + w[j:j + 1, :] * shift_rows(x, width - 1 - j)
    return y


def _norm_fn(x, g):
    return _rms(x) * g


def _ffn_act_fn(ug, uv, wg, wv, bg, bv):
    return jax.nn.silu(_causal_conv(ug, wg, 3) + bg) * (_causal_conv(uv, wv, 3) + bv)


def _gdn_conv_fn(x, w):
    return jax.nn.silu(_causal_conv(x, w, 4))


def _lru_fn(gate, x, cw, cb, wa, ba, wx, bx, lam):
    xr = _causal_conv(x, cw, 4) + cb
    r = jax.nn.sigmoid(_bdot(xr, wa) + ba)
    i = jax.nn.sigmoid(_bdot(xr, wx) + bx)
    log_a = -LRU_C * r * _softplus(-lam)
    a = jnp.exp(log_a)
    u = jnp.sqrt(-_expm1_nonpos(2.0 * log_a)) * (i * xr)
    hs = lin_scan(a, u)
    return jax.nn.gelu(gate) * hs


def _ret_fn(qs, ks, vs, gates, states, cos2, sin2, dmasks, ktails, qdecs, cdecs):
    qrs = _each(lambda q: q * cos2 + swap_halves(q) * sin2, qs)
    krs = _each(lambda k: (k * cos2 + swap_halves(k) * sin2) * (HEAD ** -0.5), ks)
    scores = _each(lambda q, k, m: _bdot(q, k, _NT) * m, qrs, krs, dmasks)
    inter = _each(lambda q, d, s: _bdot(q * d, s), qrs, qdecs, states)
    os_ = _each(lambda sc, v, x: _bdot(sc, v) + x, scores, vs, inter)
    new_states = _each(lambda s, cd, k, kt, v: s * cd + _bdot(k * kt, v, _TN), states, cdecs, krs, ktails, vs)
    ys = _each(lambda o, g: _rms(o) * jax.nn.silu(g), os_, gates)
    return ys, new_states


def _pick_lane(x, lane_idx):
    lane = lax.broadcasted_iota(jnp.int32, x.shape, 1)
    return jnp.sum(jnp.where(lane == lane_idx, x, 0.0), axis=1, keepdims=True)


def _l2norm(x):
    return x * lax.rsqrt(jnp.sum(x * x, axis=-1, keepdims=True) + EPS)


def _gdn_fn(qcs, kcs, vcs, gates, small, a_log, dt_bias, gain, states):
    c = GDN_CHUNK
    heads = tuple(range(len(qcs)))
    qs = _each(lambda x: _l2norm(x) * (HEAD ** -0.5), qcs)
    ks = _each(_l2norm, kcs)
    betas = _each(lambda h: jax.nn.sigmoid(_pick_lane(small, h)), heads)
    gs = _each(lambda h: -jnp.exp(_pick_lane(a_log, h)) * _softplus(_pick_lane(small, h + N_HEADS) + _pick_lane(dt_bias, h)), heads)
    i = lax.broadcasted_iota(jnp.int32, (c, c), 0)
    j = lax.broadcasted_iota(jnp.int32, (c, c), 1)
    tril = i >= j
    gcs = _each(lambda g: cumsum_rows(jnp.broadcast_to(g, (c, LANES)))[:, :1], gs)
    gc_rows = _each(lambda gc: jnp.broadcast_to(gc, (c, c)), gcs)
    decays = _each(lambda r: jnp.where(tril, jnp.exp(jnp.where(tril, r - r.T, 0.0)), 0.0), gc_rows)
    kbs = _each(lambda k, b: k * b, ks, betas)
    lmats = _each(lambda kb, k, d: jnp.where(i > j, _bdot(kb, k, _NT) * d, 0.0), kbs, ks, decays)
    attns = _each(lambda q, k, d: jnp.where(tril, _bdot(q, k, _NT) * d, 0.0), qs, ks, decays)
    invs = unit_lower_inverse(lmats)
    us = dot3(invs, _each(lambda v, b: v * b, vcs, betas))
    ws = dot3(invs, _each(lambda kb, gc: kb * jnp.exp(gc), kbs, gcs))
    g_lasts = _each(lambda g: jnp.sum(g, axis=0, keepdims=True), gs)
    v_news = _each(lambda u, w, s: u - _bdot(w, s), us, ws, states)
    inter = _each(lambda q, gc, s: _bdot(q * jnp.exp(gc), s), qs, gcs, states)
    os_ = _each(lambda x, a, v: x + _bdot(a, v), inter, attns, v_news)
    new_states = _each(lambda s, gl, k, gc, v: s * jnp.exp(gl) + _bdot(k * jnp.exp(gl - gc), v, _TN), states, g_lasts, ks, gcs, v_news)
    ys = _each(lambda o, gate: _rms(o) * gain * jax.nn.silu(gate), os_, gates)
    return ys, new_states


def _final_fn(h, g, target):
    y = _rms(h) * g
    return 0.5 * jnp.sum(jnp.mean(jnp.square(y - target), axis=-1, keepdims=True), axis=0, keepdims=True)


def _tile(n, candidates):
    for t in candidates:
        if n % t == 0:
            return t
    raise ValueError(f"no tile for {n}")


def matmul(a, b, *, ta=False, tb=False, add=None, out_dtype=F32, tm=None, tn=None, name):
    m = a.shape[1] if ta else a.shape[0]
    k = a.shape[0] if ta else a.shape[1]
    n = b.shape[0] if tb else b.shape[1]
    assert k == (b.shape[1] if tb else b.shape[0])
    tm = tm or _tile(m, (1024, 512, 1408, 256, 128))
    tn = tn or _tile(n, (512, 1408, 256, 128))
    dims = (((0 if ta else 1,), (1 if tb else 0,)), ((), ()))

    def body(*refs):
        if add is None:
            a_ref, b_ref, o_ref = refs
        else:
            a_ref, b_ref, r_ref, o_ref = refs
        acc = lax.dot_general(a_ref[...].astype(BF16), b_ref[...].astype(BF16), dims, preferred_element_type=F32)
        if add is not None:
            acc = acc + r_ref[...]
        o_ref[...] = acc.astype(out_dtype)

    a_spec = pl.BlockSpec((k, tm), lambda i, j: (0, i)) if ta else pl.BlockSpec((tm, k), lambda i, j: (i, 0))
    b_spec = pl.BlockSpec((tn, k), lambda i, j: (j, 0)) if tb else pl.BlockSpec((k, tn), lambda i, j: (0, j))
    o_spec = pl.BlockSpec((tm, tn), lambda i, j: (i, j))
    in_specs, args = [a_spec, b_spec], [a, b]
    if add is not None:
        in_specs.append(o_spec)
        args.append(add)
    return pl.pallas_call(body, out_shape=_sds((m, n), out_dtype), grid=(m // tm, n // tn), in_specs=in_specs,
                          out_specs=o_spec, compiler_params=_params(), name=name)(*args)


ROW_TILE = 256


def norm_fwd(x, g, *, name):
    t, d = x.shape

    def body(x_ref, g_ref, o_ref):
        o_ref[...] = _norm_fn(x_ref[...], g_ref[...]).astype(BF16)

    return pl.pallas_call(body, out_shape=_sds((t, d), BF16), grid=(t // ROW_TILE,),
                          in_specs=[pl.BlockSpec((ROW_TILE, d), lambda i: (i, 0)), pl.BlockSpec((1, d), lambda i: (0, 0))],
                          out_specs=pl.BlockSpec((ROW_TILE, d), lambda i: (i, 0)), compiler_params=_params(), name=name)(x, g)


def norm_bwd(x, g, dy, dres, *, name):
    t, d = x.shape

    def body(x_ref, g_ref, dy_ref, dres_ref, dx_ref, dg_ref):
        _, vjp = jax.vjp(_norm_fn, x_ref[...], g_ref[...])
        dx, dg = vjp(dy_ref[...])
        dx_ref[...] = dx + dres_ref[...]

        @pl.when(pl.program_id(0) == 0)
        def _():
            dg_ref[...] = jnp.zeros_like(dg_ref)

        dg_ref[...] += dg

    row = pl.BlockSpec((ROW_TILE, d), lambda i: (i, 0))
    vec = pl.BlockSpec((1, d), lambda i: (0, 0))
    return pl.pallas_call(body, out_shape=(_sds((t, d), F32), _sds((1, d), F32)), grid=(t // ROW_TILE,),
                          in_specs=[row, vec, row, row], out_specs=(row, vec), compiler_params=_params(), name=name)(x, g, dy, dres)


def final_fwd_bwd(h, g, target, *, name):
    t, d = h.shape

    def body(h_ref, g_ref, t_ref, loss_ref, dh_ref, dg_ref):
        tgt = t_ref[...]
        loss, vjp = jax.vjp(lambda hh, gg: _final_fn(hh, gg, tgt), h_ref[...], g_ref[...])
        dh, dg = vjp(jnp.ones((1, 1), F32))
        dh_ref[...] = dh

        @pl.when(pl.program_id(0) == 0)
        def _():
            dg_ref[...] = jnp.zeros_like(dg_ref)
            loss_ref[...] = jnp.zeros_like(loss_ref)

        dg_ref[...] += dg
        loss_ref[...] += jnp.broadcast_to(loss, loss_ref.shape)

    row = pl.BlockSpec((ROW_TILE, d), lambda i: (i, 0))
    vec = pl.BlockSpec((1, d), lambda i: (0, 0))
    return pl.pallas_call(body, out_shape=(_sds((1, LANES), F32), _sds((t, d), F32), _sds((1, d), F32)), grid=(t // ROW_TILE,),
                          in_specs=[row, vec, row], out_specs=(pl.BlockSpec((1, LANES), lambda i: (0, 0)), row, vec),
                          compiler_params=_params(), name=name)(h, g, target)


FFN_FWD_COLS = 256
FFN_BWD_COLS = 128


def ffn_act_fwd(u, cw, cb, *, name):
    t = u.shape[0]
    w = FFN_FWD_COLS
    nb = D_FF // w

    def body(ug_ref, uv_ref, wg_ref, wv_ref, bg_ref, bv_ref, o_ref):
        o_ref[...] = _ffn_act_fn(ug_ref[...], uv_ref[...], wg_ref[...], wv_ref[...], bg_ref[...], bv_ref[...]).astype(BF16)

    def col(rows, off):
        return pl.BlockSpec((rows, w), lambda j: (0, j + off))

    return pl.pallas_call(body, out_shape=_sds((t, D_FF), BF16), grid=(nb,),
                          in_specs=[col(t, 0), col(t, nb), col(3, 0), col(3, nb), col(1, 0), col(1, nb)],
                          out_specs=col(t, 0), compiler_params=_params(), name=name)(u, u, cw, cw, cb, cb)


def ffn_act_bwd(u, cw, cb, da, *, name):
    t = u.shape[0]
    w = FFN_BWD_COLS
    nb = D_FF // w

    def body(ug_ref, uv_ref, wg_ref, wv_ref, bg_ref, bv_ref, da_ref, dug_ref, duv_ref, dwg_ref, dwv_ref, dbg_ref, dbv_ref):
        _, vjp = jax.vjp(_ffn_act_fn, ug_ref[...], uv_ref[...], wg_ref[...], wv_ref[...], bg_ref[...], bv_ref[...])
        dug, duv, dwg, dwv, dbg, dbv = vjp(da_ref[...])
        dug_ref[...] = dug.astype(BF16)
        duv_ref[...] = duv.astype(BF16)
        dwg_ref[...] = dwg
        dwv_ref[...] = dwv
        dbg_ref[...] = dbg
        dbv_ref[...] = dbv

    def col(rows, off):
        return pl.BlockSpec((rows, w), lambda j: (0, j + off))

    outs = pl.pallas_call(
        body, out_shape=(_sds((t, D_FF), BF16), _sds((t, D_FF), BF16), _sds((3, D_FF), F32), _sds((3, D_FF), F32),
                         _sds((1, D_FF), F32), _sds((1, D_FF), F32)),
        grid=(nb,), in_specs=[col(t, 0), col(t, nb), col(3, 0), col(3, nb), col(1, 0), col(1, nb), col(t, 0)],
        out_specs=(col(t, 0), col(t, 0), col(3, 0), col(3, 0), col(1, 0), col(1, 0)), compiler_params=_params(), name=name,
    )(u, u, cw, cw, cb, cb, da)
    dug, duv, dwg, dwv, dbg, dbv = outs
    return dug, duv, jnp.concatenate([dwg, dwv], axis=1), jnp.concatenate([dbg, dbv], axis=1)


GDN_CONV_COLS = 256
GDN_CONV_OFF = 4 * GROUP


def gdn_conv_fwd(p, cw, *, name):
    t = p.shape[0]
    w = GDN_CONV_COLS
    nb = 3 * GROUP // w
    off = GDN_CONV_OFF // w

    def body(x_ref, w_ref, o_ref):
        o_ref[...] = _gdn_conv_fn(x_ref[...], w_ref[...])

    return pl.pallas_call(body, out_shape=_sds((t, 3 * GROUP), F32), grid=(nb,),
                          in_specs=[pl.BlockSpec((t, w), lambda j: (0, j + off)), pl.BlockSpec((4, w), lambda j: (0, j))],
                          out_specs=pl.BlockSpec((t, w), lambda j: (0, j)), compiler_params=_params(), name=name)(p, cw)


def gdn_conv_bwd(p, cw, dc, *, name):
    t = p.shape[0]
    w = GDN_CONV_COLS
    nb = 3 * GROUP // w
    off = GDN_CONV_OFF // w

    def body(x_ref, w_ref, dc_ref, dx_ref, dw_ref):
        _, vjp = jax.vjp(_gdn_conv_fn, x_ref[...], w_ref[...])
        dx, dw = vjp(dc_ref[...])
        dx_ref[...] = dx.astype(BF16)
        dw_ref[...] = dw

    blk = pl.BlockSpec((t, w), lambda j: (0, j))
    wblk = pl.BlockSpec((4, w), lambda j: (0, j))
    return pl.pallas_call(body, out_shape=(_sds((t, 3 * GROUP), BF16), _sds((4, 3 * GROUP), F32)), grid=(nb,),
                          in_specs=[pl.BlockSpec((t, w), lambda j: (0, j + off)), wblk, blk], out_specs=(blk, wblk),
                          compiler_params=_params(), name=name)(p, cw, dc)


def _lru_specs(t):
    w = D_MODEL // LRU_BLOCKS
    gate = pl.BlockSpec((t, w), lambda j: (0, j))
    xin = pl.BlockSpec((t, w), lambda j: (0, j + LRU_BLOCKS))
    cw = pl.BlockSpec((4, w), lambda j: (0, j))
    vec = pl.BlockSpec((1, w), lambda j: (0, j))
    mat = pl.BlockSpec((None, w, w), lambda j: (j, 0, 0))
    return gate, xin, cw, vec, mat


def lru_fwd(gx, cw, cb, wa, ba, wx, bx, lam, *, name):
    t = gx.shape[0]
    gate, xin, cws, vec, mat = _lru_specs(t)

    def body(g_ref, x_ref, cw_ref, cb_ref, wa_ref, ba_ref, wx_ref, bx_ref, lam_ref, o_ref):
        o_ref[...] = _lru_fn(g_ref[...], x_ref[...], cw_ref[...], cb_ref[...], wa_ref[...], ba_ref[...], wx_ref[...],
                             bx_ref[...], lam_ref[...]).astype(BF16)

    return pl.pallas_call(body, out_shape=_sds((t, D_MODEL), BF16), grid=(LRU_BLOCKS,),
                          in_specs=[gate, xin, cws, vec, mat, vec, mat, vec, vec], out_specs=gate,
                          compiler_params=_params(), name=name)(gx, gx, cw, cb, wa, ba, wx, bx, lam)


def lru_bwd(gx, cw, cb, wa, ba, wx, bx, lam, dy, *, name):
    t = gx.shape[0]
    gate, xin, cws, vec, mat = _lru_specs(t)

    def body(g_ref, x_ref, cw_ref, cb_ref, wa_ref, ba_ref, wx_ref, bx_ref, lam_ref, dy_ref,
             dg_ref, dx_ref, dcw_ref, dcb_ref, dwa_ref, dba_ref, dwx_ref, dbx_ref, dlam_ref):
        _, vjp = jax.vjp(_lru_fn, g_ref[...], x_ref[...], cw_ref[...], cb_ref[...], wa_ref[...], ba_ref[...], wx_ref[...],
                         bx_ref[...], lam_ref[...])
        dg, dx, dcw, dcb, dwa, dba, dwx, dbx, dlam = vjp(dy_ref[...])
        dg_ref[...] = dg.astype(BF16)
        dx_ref[...] = dx.astype(BF16)
        dcw_ref[...] = dcw
        dcb_ref[...] = dcb
        dwa_ref[...] = dwa
        dba_ref[...] = dba
        dwx_ref[...] = dwx
        dbx_ref[...] = dbx
        dlam_ref[...] = dlam

    d = D_MODEL
    w = d // LRU_BLOCKS
    out_shape = (_sds((t, d), BF16), _sds((t, d), BF16), _sds((4, d), F32), _sds((1, d), F32), _sds((LRU_BLOCKS, w, w), F32),
                 _sds((1, d), F32), _sds((LRU_BLOCKS, w, w), F32), _sds((1, d), F32), _sds((1, d), F32))
    return pl.pallas_call(body, out_shape=out_shape, grid=(LRU_BLOCKS,),
                          in_specs=[gate, xin, cws, vec, mat, vec, mat, vec, vec, gate],
                          out_specs=(gate, gate, cws, vec, mat, vec, mat, vec, vec), compiler_params=_params(), name=name,
                          )(gx, gx, cw, cb, wa, ba, wx, bx, lam, dy)


def _ret_tables():
    half = HEAD // 2
    inv_freq = (np.float32(ROPE_BASE) ** (-np.arange(half, dtype=np.float32) / np.float32(half))).astype(np.float32)
    ang = (np.arange(SEQ, dtype=np.float32)[:, None] * inv_freq[None, :]).astype(np.float64)
    cos2 = np.concatenate([np.cos(ang), np.cos(ang)], axis=1).astype(np.float32)
    sin2 = np.concatenate([-np.sin(ang), np.sin(ang)], axis=1).astype(np.float32)
    c = RET_CHUNK
    log_gamma = np.log1p(-np.exp2(-5.0 - np.arange(N_HEADS, dtype=np.float64)))
    idx = np.arange(c, dtype=np.float64)
    rel = idx[:, None] - idx[None, :]
    dmask = np.where(rel >= 0, np.exp(log_gamma[:, None, None] * np.maximum(rel, 0.0)), 0.0)
    ones = np.ones((N_HEADS, c, HEAD))
    ktail = np.exp(log_gamma[:, None] * (c - 1 - idx))[:, :, None] * ones
    qdec = np.exp(log_gamma[:, None] * (idx + 1.0))[:, :, None] * ones
    cdec = np.exp(log_gamma * c)[:, None, None] * ones
    return tuple(jnp.asarray(a, F32) for a in (cos2, sin2, dmask, ktail, qdec, cdec))


def _ret_specs(rev):
    c = RET_CHUNK
    nc = SEQ // c

    def n_of(n):
        return nc - 1 - n if rev else n

    def group(off):
        return pl.BlockSpec((c, GROUP), lambda n: (n_of(n), off))

    tab = pl.BlockSpec((c, HEAD), lambda n: (n_of(n), 0))
    const = pl.BlockSpec((N_HEADS, c, HEAD), lambda n: (0, 0, 0))
    state = pl.BlockSpec((N_HEADS, None, HEAD, HEAD), lambda n: (0, n_of(n), 0, 0))
    return group, tab, const, state, nc


def _head(h):
    return slice(h * HEAD, (h + 1) * HEAD)


def ret_fwd(p, tables, *, name):
    group, tab, const, state, nc = _ret_specs(False)

    def body(q_ref, k_ref, v_ref, g_ref, cos_ref, sin_ref, dm_ref, kt_ref, qd_ref, cd_ref, y_ref, st_ref, s_scr):
        @pl.when(pl.program_id(0) == 0)
        def _():
            s_scr[...] = jnp.zeros_like(s_scr)

        heads = range(N_HEADS)
        states = tuple(s_scr[h] for h in heads)
        ys, new_states = _ret_fn(*(tuple(r[:, _head(h)] for h in heads) for r in (q_ref, k_ref, v_ref, g_ref)), states,
                                 cos_ref[...], sin_ref[...], *(tuple(r[h] for h in heads) for r in (dm_ref, kt_ref, qd_ref, cd_ref)))
        for h in heads:
            st_ref[h] = states[h]
            y_ref[:, _head(h)] = ys[h].astype(BF16)
            s_scr[h] = new_states[h]

    return pl.pallas_call(
        body, out_shape=(_sds((SEQ, GROUP), BF16), _sds((N_HEADS, nc, HEAD, HEAD), F32)), grid=(nc,),
        in_specs=[group(0), group(1), group(2), group(3), tab, tab, const, const, const, const],
        out_specs=(group(0), state), scratch_shapes=[pltpu.VMEM((N_HEADS, HEAD, HEAD), F32)], compiler_params=_params(), name=name,
    )(p, p, p, p, *tables)


def ret_bwd(p, tables, states, dy, *, name):
    group, tab, const, state, nc = _ret_specs(True)

    def body(q_ref, k_ref, v_ref, g_ref, cos_ref, sin_ref, dm_ref, kt_ref, qd_ref, cd_ref, st_ref, dy_ref,
             dq_ref, dk_ref, dv_ref, dg_ref, ds_scr):
        @pl.when(pl.program_id(0) == 0)
        def _():
            ds_scr[...] = jnp.zeros_like(ds_scr)

        heads = range(N_HEADS)
        consts = (cos_ref[...], sin_ref[...], *(tuple(r[h] for h in heads) for r in (dm_ref, kt_ref, qd_ref, cd_ref)))
        _, vjp = jax.vjp(lambda *a: _ret_fn(*a, *consts), *(tuple(r[:, _head(h)] for h in heads) for r in (q_ref, k_ref, v_ref, g_ref)),
                         tuple(st_ref[h] for h in heads))
        dqs, dks, dvs, dgs, dss = vjp((tuple(dy_ref[:, _head(h)] for h in heads), tuple(ds_scr[h] for h in heads)))
        for h in heads:
            dq_ref[:, _head(h)] = dqs[h].astype(BF16)
            dk_ref[:, _head(h)] = dks[h].astype(BF16)
            dv_ref[:, _head(h)] = dvs[h].astype(BF16)
            dg_ref[:, _head(h)] = dgs[h].astype(BF16)
            ds_scr[h] = dss[h]

    out = _sds((SEQ, GROUP), BF16)
    return pl.pallas_call(
        body, out_shape=(out, out, out, out), grid=(nc,),
        in_specs=[group(0), group(1), group(2), group(3), tab, tab, const, const, const, const, state, group(0)],
        out_specs=(group(0), group(0), group(0), group(0)), scratch_shapes=[pltpu.VMEM((N_HEADS, HEAD, HEAD), F32)],
        compiler_params=_params(), name=name,
    )(p, p, p, p, *tables, states, dy)


def _gdn_specs(rev):
    c = GDN_CHUNK
    nc = SEQ // c

    def n_of(n):
        return nc - 1 - n if rev else n

    def group(off):
        return pl.BlockSpec((c, GROUP), lambda n: (n_of(n), off))

    small = pl.BlockSpec((c, LANES), lambda n: (n_of(n), 0))
    vec = pl.BlockSpec((1, LANES), lambda n: (0, 0))
    state = pl.BlockSpec((N_HEADS, None, HEAD, HEAD), lambda n: (0, n_of(n), 0, 0))
    return group, small, vec, state, nc


GDN_GATE_GROUP = 7


def gdn_fwd(conv, p, small, a_log, dt_bias, gain, *, name):
    group, sm, vec, state, nc = _gdn_specs(False)

    def body(q_ref, k_ref, v_ref, g_ref, sm_ref, al_ref, dt_ref, gn_ref, y_ref, st_ref, s_scr):
        @pl.when(pl.program_id(0) == 0)
        def _():
            s_scr[...] = jnp.zeros_like(s_scr)

        states = tuple(s_scr[h] for h in range(N_HEADS))
        ys, new_states = _gdn_fn(*(tuple(r[:, _head(h)] for h in range(N_HEADS)) for r in (q_ref, k_ref, v_ref, g_ref)),
                                 sm_ref[...], al_ref[...], dt_ref[...], gn_ref[...], states)
        for h in range(N_HEADS):
            st_ref[h] = states[h]
            y_ref[:, _head(h)] = ys[h].astype(BF16)
            s_scr[h] = new_states[h]

    return pl.pallas_call(
        body, out_shape=(_sds((SEQ, GROUP), BF16), _sds((N_HEADS, nc, HEAD, HEAD), F32)), grid=(nc,),
        in_specs=[group(0), group(1), group(2), group(GDN_GATE_GROUP), sm, vec, vec, vec], out_specs=(group(0), state),
        scratch_shapes=[pltpu.VMEM((N_HEADS, HEAD, HEAD), F32)], compiler_params=_params(), name=name,
    )(conv, conv, conv, p, small, a_log, dt_bias, gain)


def gdn_bwd(conv, p, small, a_log, dt_bias, gain, states, dy, *, name):
    group, sm, vec, state, nc = _gdn_specs(True)

    def body(q_ref, k_ref, v_ref, g_ref, sm_ref, al_ref, dt_ref, gn_ref, st_ref, dy_ref,
             dq_ref, dk_ref, dv_ref, dg_ref, dsm_ref, dal_ref, ddt_ref, dgn_ref, ds_scr):
        @pl.when(pl.program_id(0) == 0)
        def _():
            ds_scr[...] = jnp.zeros_like(ds_scr)
            dal_ref[...] = jnp.zeros_like(dal_ref)
            ddt_ref[...] = jnp.zeros_like(ddt_ref)
            dgn_ref[...] = jnp.zeros_like(dgn_ref)

        per_head = tuple(tuple(r[:, _head(h)] for h in range(N_HEADS)) for r in (q_ref, k_ref, v_ref, g_ref))
        _, vjp = jax.vjp(_gdn_fn, *per_head, sm_ref[...], al_ref[...], dt_ref[...], gn_ref[...],
                         tuple(st_ref[h] for h in range(N_HEADS)))
        cts = (tuple(dy_ref[:, _head(h)] for h in range(N_HEADS)), tuple(ds_scr[h] for h in range(N_HEADS)))
        dqs, dks, dvs, dgs, dsm, dal, ddt, dgn, dss = vjp(cts)
        for h in range(N_HEADS):
            dq_ref[:, _head(h)] = dqs[h]
            dk_ref[:, _head(h)] = dks[h]
            dv_ref[:, _head(h)] = dvs[h]
            dg_ref[:, _head(h)] = dgs[h].astype(BF16)
            ds_scr[h] = dss[h]
        dsm_ref[...] = dsm
        dal_ref[...] += dal
        ddt_ref[...] += ddt
        dgn_ref[...] += dgn

    f = _sds((SEQ, GROUP), F32)
    pv = _sds((1, LANES), F32)
    return pl.pallas_call(
        body, out_shape=(f, f, f, _sds((SEQ, GROUP), BF16), _sds((SEQ, LANES), F32), pv, pv, pv), grid=(nc,),
        in_specs=[group(0), group(1), group(2), group(GDN_GATE_GROUP), sm, vec, vec, vec, state, group(1)],
        out_specs=(group(0), group(0), group(0), group(0), sm, vec, vec, vec), scratch_shapes=[pltpu.VMEM((N_HEADS, HEAD, HEAD), F32)],
        compiler_params=_params(), name=name,
    )(conv, conv, conv, p, small, a_log, dt_bias, gain, states, dy)


PACK_ROW_TILE = 1024


def adamw(w, g, m, v, *, name):
    r = w.shape[0]
    tr = _tile(r, (PACK_ROW_TILE, 256, 128, 64, 32, 16, 8))

    def body(w_ref, g_ref, m_ref, v_ref, d_ref, nm_ref, nv_ref):
        gg = g_ref[...]
        nm = ADAM_B1 * m_ref[...] + (1.0 - ADAM_B1) * gg
        nv = ADAM_B2 * v_ref[...] + (1.0 - ADAM_B2) * jnp.square(gg)
        m_hat = nm / (1.0 - ADAM_B1 ** ADAM_STEP)
        v_hat = nv / (1.0 - ADAM_B2 ** ADAM_STEP)
        d_ref[...] = -ADAM_LR * (m_hat / (jnp.sqrt(v_hat) + ADAM_EPS) + ADAM_WD * w_ref[...])
        nm_ref[...] = nm
        nv_ref[...] = nv

    blk = pl.BlockSpec((tr, LANES), lambda i: (i, 0))
    o = _sds((r, LANES), F32)
    return pl.pallas_call(body, out_shape=(o, o, o), grid=(r // tr,), in_specs=[blk] * 4, out_specs=(blk, blk, blk),
                          compiler_params=_params(), name=name)(w, g, m, v)


ELEMENTWISE_BLOCK_BYTES = 2 * 1024 * 1024


def _row_tile(r, c):
    best = None
    for tr in range(8, r + 1, 8):
        if r % tr == 0 and tr * c * 4 <= ELEMENTWISE_BLOCK_BYTES:
            best = tr
    if best is None:
        raise ValueError(f"no row tile for ({r}, {c})")
    return best


def _core_index():
    return lax.axis_index("c").astype(jnp.int32).reshape(1)


def _chip_index():
    return (2 * lax.axis_index("x") + lax.axis_index("y")).astype(jnp.int32).reshape(1)


def adamw_halves(w, m, v, g_own, g_sib, *, name):
    rows, c = w.shape
    r = rows // 2
    tr = _row_tile(r, c)
    nb = r // tr

    def body(c_ref, w_ref, m_ref, v_ref, own_ref, sib_ref, g_ref, d_ref, nm_ref, nv_ref):
        gg = jnp.where(pl.program_id(0) == c_ref[0], own_ref[...], sib_ref[...])
        nm = ADAM_B1 * m_ref[...] + (1.0 - ADAM_B1) * gg
        nv = ADAM_B2 * v_ref[...] + (1.0 - ADAM_B2) * jnp.square(gg)
        m_hat = nm / (1.0 - ADAM_B1 ** ADAM_STEP)
        v_hat = nv / (1.0 - ADAM_B2 ** ADAM_STEP)
        g_ref[...] = gg
        d_ref[...] = -ADAM_LR * (m_hat / (jnp.sqrt(v_hat) + ADAM_EPS) + ADAM_WD * w_ref[...])
        nm_ref[...] = nm
        nv_ref[...] = nv

    full = pl.BlockSpec((tr, c), lambda h, i, cr: (h * nb + i, 0))
    half = pl.BlockSpec((tr, c), lambda h, i, cr: (i, 0))
    o = _sds((rows, c), F32)
    gs = pltpu.PrefetchScalarGridSpec(num_scalar_prefetch=1, grid=(2, nb), in_specs=[full, full, full, half, half],
                                      out_specs=(full, full, full, full))
    return pl.pallas_call(body, out_shape=(o, o, o, o), grid_spec=gs, compiler_params=_params(), name=name)(
        _core_index(), w, m, v, g_own, g_sib)


def add_core_halves(g2, land, *, out_dtype, name):
    _, ns, r, cols = g2.shape
    tr = _row_tile(r, cols)

    def body(c_ref, a_ref, b_ref, o_ref):
        o_ref[...] = (a_ref[...] + b_ref[...]).astype(out_dtype)

    gs = pltpu.PrefetchScalarGridSpec(
        num_scalar_prefetch=1, grid=(ns, r // tr),
        in_specs=[pl.BlockSpec((None, None, tr, cols), lambda s, i, cr: (cr[0], s, i, 0)),
                  pl.BlockSpec((None, tr, cols), lambda s, i, cr: (s, i, 0))],
        out_specs=pl.BlockSpec((None, tr, cols), lambda s, i, cr: (s, i, 0)))
    return pl.pallas_call(body, out_shape=_sds((ns, r, cols), out_dtype), grid_spec=gs, compiler_params=_params(), name=name)(
        _core_index(), g2, land)


def sum_over_chips(own, land, *, scatter, name):
    _, r, cols = own.shape
    tr = _row_tile(r, cols)

    def body(mine_ref, own_ref, l0, l1, l2, l3, o_ref):
        mine = mine_ref[0]
        mine_val = own_ref[...]
        acc = None
        for s, l_ref in enumerate((l0, l1, l2, l3)):
            val = jnp.where(mine == s, mine_val, l_ref[...]).astype(F32)
            acc = val if acc is None else acc + val
        o_ref[...] = acc

    def slot(s):
        return pl.BlockSpec((None, tr, cols), lambda i, mr: (jnp.where(mr[0] == s, (s + 1) % N_SHARD, s), i, 0))

    own_spec = pl.BlockSpec((None, tr, cols), lambda i, mr: (mr[0] if scatter else 0, i, 0))
    gs = pltpu.PrefetchScalarGridSpec(num_scalar_prefetch=1, grid=(r // tr,), in_specs=[own_spec] + [slot(s) for s in range(N_SHARD)],
                                      out_specs=pl.BlockSpec((tr, cols), lambda i, mr: (i, 0)))
    return pl.pallas_call(body, out_shape=_sds((r, cols), F32), grid_spec=gs, compiler_params=_params(), name=name)(
        _chip_index(), own, land, land, land, land)


_ANY = pl.BlockSpec(memory_space=pl.ANY)


def xy_exchange(src, *, scatter, name):
    rh = src.shape[1]

    def body(src_ref, land_ref, send_sems, recv_sems, loc_sem):
        x, y, c = lax.axis_index("x"), lax.axis_index("y"), lax.axis_index("c")
        mine = 2 * x + y
        peers = [(1 - x, y), (x, 1 - y), (1 - x, 1 - y)]

        def piece(shard):
            return src_ref.at[shard] if scatter else src_ref.at[c]

        def copy(k, px, py, dst_slot):
            return pltpu.make_async_remote_copy(src_ref=piece(2 * px + py), dst_ref=land_ref.at[dst_slot], send_sem=send_sems.at[k],
                                                recv_sem=recv_sems.at[k], device_id=(px, py, c), device_id_type=MESH)

        keep = pltpu.make_async_copy(piece(mine), land_ref.at[mine], loc_sem)
        keep.start()
        sends = [copy(k, px, py, mine) for k, (px, py) in enumerate(peers)]
        for cp in sends:
            cp.start()
        for cp in sends:
            cp.wait_send()
        for k, (px, py) in enumerate(peers):
            copy(k, px, py, 2 * px + py).wait_recv()
        keep.wait()

    return pl.pallas_call(body, out_shape=_sds((N_SHARD, rh, LANES), src.dtype), in_specs=[_ANY], out_specs=_ANY,
                          scratch_shapes=[pltpu.SemaphoreType.DMA((3,)), pltpu.SemaphoreType.DMA((3,)), pltpu.SemaphoreType.DMA(())],
                          name=name)(src)


def core_exchange(src, *, send_other_half, name):
    def body(src_ref, out_ref, send_sem, recv_sem, loc_sem):
        x, y, c = lax.axis_index("x"), lax.axis_index("y"), lax.axis_index("c")
        if send_other_half:
            cp = pltpu.make_async_remote_copy(src_ref=src_ref.at[1 - c], dst_ref=out_ref, send_sem=send_sem, recv_sem=recv_sem,
                                              device_id=(x, y, 1 - c), device_id_type=MESH)
            cp.start()
            cp.wait_send()
            cp.wait_recv()
        else:
            keep = pltpu.make_async_copy(src_ref, out_ref.at[c], loc_sem)
            keep.start()
            cp = pltpu.make_async_remote_copy(src_ref=src_ref, dst_ref=out_ref.at[c], send_sem=send_sem, recv_sem=recv_sem,
                                              device_id=(x, y, 1 - c), device_id_type=MESH)
            cp.start()
            cp.wait_send()
            pltpu.make_async_remote_copy(src_ref=src_ref, dst_ref=out_ref.at[1 - c], send_sem=send_sem, recv_sem=recv_sem,
                                         device_id=(x, y, 1 - c), device_id_type=MESH).wait_recv()
            keep.wait()

    out_shape = _sds(src.shape[1:], src.dtype) if send_other_half else _sds((2,) + src.shape, src.dtype)
    return pl.pallas_call(body, out_shape=out_shape, in_specs=[_ANY], out_specs=_ANY,
                          scratch_shapes=[pltpu.SemaphoreType.DMA(()), pltpu.SemaphoreType.DMA(()), pltpu.SemaphoreType.DMA(())],
                          name=name)(src)


def _comm_call(body, ins, out_shapes, sem_counts, name):
    return pl.pallas_call(body, out_shape=tuple(out_shapes), in_specs=[_ANY] * len(ins), out_specs=tuple([_ANY] * len(out_shapes)),
                          scratch_shapes=[pltpu.SemaphoreType.DMA((k,)) for k in sem_counts], name=name)(*ins)


def _xy_peers(x, y):
    return [(1 - x, y), (x, 1 - y), (1 - x, 1 - y)]


def gather_halves(halves, *, name):
    n = len(halves)

    def body(*refs):
        ins, lands, sibs = refs[:n], refs[n:2 * n], refs[2 * n:3 * n]
        ici_send, ici_recv, d2d_send, d2d_recv = refs[3 * n:]
        x, y, c = lax.axis_index("x"), lax.axis_index("y"), lax.axis_index("c")
        mine = 2 * x + y
        peers = _xy_peers(x, y)

        def ici(i, k, slot):
            px, py = peers[k]
            return pltpu.make_async_remote_copy(src_ref=ins[i].at[c], dst_ref=lands[i].at[slot], send_sem=ici_send.at[3 * i + k],
                                                recv_sem=ici_recv.at[3 * i + k], device_id=(px, py, c), device_id_type=MESH)

        def pass_on(i, k):
            px, py = peers[k]
            slot = 2 * px + py
            return pltpu.make_async_remote_copy(src_ref=lands[i].at[slot], dst_ref=sibs[i].at[slot], send_sem=d2d_send.at[3 * i + k],
                                                recv_sem=d2d_recv.at[3 * i + k], device_id=(x, y, 1 - c), device_id_type=MESH)

        sends = [ici(i, k, mine) for i in range(n) for k in range(3)]
        for cp in sends:
            cp.start()
        passed = []
        for i in range(n):
            for k in range(3):
                px, py = peers[k]
                ici(i, k, 2 * px + py).wait_recv()
                cp = pass_on(i, k)
                cp.start()
                passed.append(cp)
        for cp in passed:
            cp.wait_recv()
        for cp in sends + passed:
            cp.wait_send()

    outs = [_sds((N_SHARD,) + h.shape[1:], h.dtype) for h in halves]
    res = _comm_call(body, halves, outs + outs, [3 * n] * 4, name)
    return res[:n], res[n:]


def send_other_half(arrays, *, name):
    n = len(arrays)

    def body(*refs):
        ins, lands = refs[:n], refs[n:2 * n]
        send_sems, recv_sems = refs[2 * n:]
        x, y, c = lax.axis_index("x"), lax.axis_index("y"), lax.axis_index("c")
        copies = [pltpu.make_async_remote_copy(src_ref=ins[i].at[1 - c], dst_ref=lands[i], send_sem=send_sems.at[i],
                                               recv_sem=recv_sems.at[i], device_id=(x, y, 1 - c), device_id_type=MESH) for i in range(n)]
        for cp in copies:
            cp.start()
        for cp in copies:
            cp.wait_recv()
        for cp in copies:
            cp.wait_send()

    return _comm_call(body, arrays, [_sds(a.shape[1:], a.dtype) for a in arrays], [n, n], name)


def send_to_chips(arrays, scatter, *, name):
    n = len(arrays)

    def body(*refs):
        ins, lands = refs[:n], refs[n:2 * n]
        send_sems, recv_sems = refs[2 * n:]
        x, y, c = lax.axis_index("x"), lax.axis_index("y"), lax.axis_index("c")
        mine = 2 * x + y
        peers = _xy_peers(x, y)

        def copy(i, k, dst_slot):
            px, py = peers[k]
            src = ins[i].at[2 * px + py] if scatter[i] else ins[i].at[0]
            return pltpu.make_async_remote_copy(src_ref=src, dst_ref=lands[i].at[dst_slot], send_sem=send_sems.at[3 * i + k],
                                                recv_sem=recv_sems.at[3 * i + k], device_id=(px, py, c), device_id_type=MESH)

        sends = [copy(i, k, mine) for i in range(n) for k in range(3)]
        for cp in sends:
            cp.start()
        for i in range(n):
            for k in range(3):
                px, py = peers[k]
                copy(i, k, 2 * px + py).wait_recv()
        for cp in sends:
            cp.wait_send()

    return _comm_call(body, arrays, [_sds((N_SHARD,) + a.shape[1:], a.dtype) for a in arrays], [3 * n, 3 * n], name)


def swap_with_other_core(arrays, *, name):
    n = len(arrays)

    def body(*refs):
        ins, lands = refs[:n], refs[n:2 * n]
        send_sems, recv_sems = refs[2 * n:]
        x, y, c = lax.axis_index("x"), lax.axis_index("y"), lax.axis_index("c")
        copies = [pltpu.make_async_remote_copy(src_ref=ins[i], dst_ref=lands[i], send_sem=send_sems.at[i], recv_sem=recv_sems.at[i],
                                               device_id=(x, y, 1 - c), device_id_type=MESH) for i in range(n)]
        for cp in copies:
            cp.start()
        for cp in copies:
            cp.wait_recv()
        for cp in copies:
            cp.wait_send()

    return _comm_call(body, arrays, [_sds(a.shape, a.dtype) for a in arrays], [n, n], name)


def _pack_rows(n_elems, row_multiple):
    rows = -(-n_elems // LANES)
    return -(-rows // row_multiple) * row_multiple


def _pack(arrays, rows, dtype):
    flat = jnp.concatenate([a.reshape(-1).astype(dtype) for a in arrays])
    return jnp.pad(flat, (0, rows * LANES - flat.shape[0])).reshape(rows, LANES)


def _unpack(packed, shapes):
    flat = packed.reshape(-1)
    out, off = [], 0
    for s in shapes:
        n = int(np.prod(s))
        out.append(flat[off:off + n].reshape(s))
        off += n
    return out


def all_gather_shards(shards, axes, dtype, row_multiple, tag):
    shapes = [s.shape for s in shards]
    rows = _pack_rows(sum(int(np.prod(s)) for s in shapes), row_multiple)
    packed = _pack(shards, rows, dtype).reshape(2, rows // 2, LANES)
    land = xy_exchange(packed, scatter=False, name=f"gather_xy_{tag}")
    both = core_exchange(land, send_other_half=False, name=f"gather_c_{tag}")
    per_shard = jnp.swapaxes(both, 0, 1).reshape(N_SHARD, rows, LANES)
    pieces = [_unpack(per_shard[s], shapes) for s in range(N_SHARD)]
    return [jnp.concatenate([pieces[s][i] for s in range(N_SHARD)], axis=ax) for i, ax in enumerate(axes)]


def reduce_over_devices(arrays, scatter):
    land = send_other_half(arrays, name="reduce_core_send")
    chip = [add_core_halves(a, l, out_dtype=BF16 if sc else F32, name=f"reduce_core_add_{i}")
            for i, (a, l, sc) in enumerate(zip(arrays, land, scatter))]
    land = send_to_chips(chip, scatter, name="reduce_chip_send")
    own = [sum_over_chips(ch, l, scatter=sc, name=f"reduce_chip_add_{i}") for i, (ch, l, sc) in enumerate(zip(chip, land, scatter))]
    sib = swap_with_other_core(own, name="reduce_core_swap")
    return own, sib


def _ffn_layer_fwd(h, norm_g, w_up, cw, cb, w_down, tag):
    hn = norm_fwd(h, norm_g, name=f"ffn_norm_{tag}")
    u = matmul(hn, w_up, name=f"ffn_up_{tag}")
    act = ffn_act_fwd(u, cw, cb, name=f"ffn_act_{tag}")
    out = matmul(act, w_down, add=h, name=f"ffn_down_{tag}")
    return out, (h, hn, u, act)


def _ffn_layer_bwd(saved, dout, norm_g, w_up, cw, cb, w_down, tag):
    h, hn, u, act = saved
    dact = matmul(dout, w_down, tb=True, name=f"ffn_down_dx_{tag}")
    d_w_down = matmul(act, dout, ta=True, name=f"ffn_down_dw_{tag}")
    dug, duv, dcw, dcb = ffn_act_bwd(u, cw, cb, dact, name=f"ffn_act_bwd_{tag}")
    du = jnp.concatenate([dug, duv], axis=1)
    dhn = matmul(du, w_up, tb=True, name=f"ffn_up_dx_{tag}")
    d_w_up = matmul(hn, du, ta=True, name=f"ffn_up_dw_{tag}")
    dh, dg = norm_bwd(h, norm_g, dhn, dout, name=f"ffn_norm_bwd_{tag}")
    return dh, dg, d_w_up, dcw, dcb, d_w_down


def local_step(x, target, w):
    g = {}
    tables = _ret_tables()
    w_in = w["ret_gdn_w_in"]
    w_main = w_in[:, :MIX_MAIN]
    w_small = jnp.pad(w_in[:, MIX_MAIN:], ((0, 0), (0, LANES - 2 * N_HEADS)))
    a_log = jnp.pad(w["gdn_a_log"], ((0, 0), (0, LANES - N_HEADS)))
    dt_bias = jnp.pad(w["gdn_dt_bias"], ((0, 0), (0, LANES - N_HEADS)))

    hn0 = norm_fwd(x, w["norm_mix"][0:1], name="mix0_norm")
    p = matmul(hn0, w_main, name="mix0_in")
    small = matmul(hn0, w_small, name="mix0_in_small")
    y_ret, s_ret = ret_fwd(p, tables, name="ret_fwd")
    conv = gdn_conv_fwd(p, w["gdn_conv_w"], name="gdn_conv")
    y_gdn, s_gdn = gdn_fwd(conv, p, small, a_log, dt_bias, w["gdn_out_gain"], name="gdn_fwd")
    y0 = jnp.concatenate([y_ret, y_gdn], axis=1)
    h1 = matmul(y0, w["ret_gdn_w_out"], add=x, name="mix0_out")
    h2, ffn0 = _ffn_layer_fwd(h1, w["norm_ffn"][0:1], w["ffn_w_up"][0], w["ffn_conv_w"][0], w["ffn_conv_b"][0:1], w["ffn_w_down"][0], "0")

    hn1 = norm_fwd(h2, w["norm_mix"][1:2], name="mix1_norm")
    gx = matmul(hn1, w["lru_w_in"], name="mix1_in")
    lru_p = (w["lru_conv_w"], w["lru_conv_b"], w["lru_w_a"], w["lru_b_a"], w["lru_w_x"], w["lru_b_x"], w["lru_lambda"])
    y1 = lru_fwd(gx, *lru_p, name="lru_fwd")
    h3 = matmul(y1, w["lru_w_out"], add=h2, name="mix1_out")
    h4, ffn1 = _ffn_layer_fwd(h3, w["norm_ffn"][1:2], w["ffn_w_up"][1], w["ffn_conv_w"][1], w["ffn_conv_b"][1:2], w["ffn_w_down"][1], "1")

    loss, dh4, g["norm_final"] = final_fwd_bwd(h4, w["norm_final"], target, name="final")

    dh3, dgf1, dwu1, dcw1, dcb1, dwd1 = _ffn_layer_bwd(ffn1, dh4, w["norm_ffn"][1:2], w["ffn_w_up"][1], w["ffn_conv_w"][1],
                                                     w["ffn_conv_b"][1:2], w["ffn_w_down"][1], "1")
    dy1 = matmul(dh3, w["lru_w_out"], tb=True, name="mix1_out_dx")
    g["lru_w_out"] = matmul(y1, dh3, ta=True, name="mix1_out_dw")
    dgate, dxr, g["lru_conv_w"], g["lru_conv_b"], g["lru_w_a"], g["lru_b_a"], g["lru_w_x"], g["lru_b_x"], g["lru_lambda"] = lru_bwd(
        gx, *lru_p, dy1, name="lru_bwd")
    dgx = jnp.concatenate([dgate, dxr], axis=1)
    dhn1 = matmul(dgx, w["lru_w_in"], tb=True, name="mix1_in_dx")
    g["lru_w_in"] = matmul(hn1, dgx, ta=True, name="mix1_in_dw")
    dh2, dgm1 = norm_bwd(h2, w["norm_mix"][1:2], dhn1, dh3, name="mix1_norm_bwd")

    dh1, dgf0, dwu0, dcw0, dcb0, dwd0 = _ffn_layer_bwd(ffn0, dh2, w["norm_ffn"][0:1], w["ffn_w_up"][0], w["ffn_conv_w"][0],
                                                     w["ffn_conv_b"][0:1], w["ffn_w_down"][0], "0")
    dy0 = matmul(dh1, w["ret_gdn_w_out"], tb=True, name="mix0_out_dx")
    g["ret_gdn_w_out"] = matmul(y0, dh1, ta=True, name="mix0_out_dw")
    dq_r, dk_r, dv_r, dg_r = ret_bwd(p, tables, s_ret, dy0, name="ret_bwd")
    dcq, dck, dcv, dg_d, dsmall, dal, ddt, dgain = gdn_bwd(conv, p, small, a_log, dt_bias, w["gdn_out_gain"], s_gdn, dy0, name="gdn_bwd")
    dconv = jnp.concatenate([dcq, dck, dcv], axis=1)
    dp_conv, g["gdn_conv_w"] = gdn_conv_bwd(p, w["gdn_conv_w"], dconv, name="gdn_conv_bwd")
    dp = jnp.concatenate([dq_r, dk_r, dv_r, dg_r, dp_conv, dg_d], axis=1)
    dhn0 = matmul(dp, w_main, tb=True, name="mix0_in_dx")
    dhn0 = matmul(dsmall, w_small, tb=True, add=dhn0, name="mix0_in_small_dx")
    d_w_main = matmul(hn0, dp, ta=True, name="mix0_in_dw")
    d_w_small = matmul(hn0, dsmall, ta=True, name="mix0_in_small_dw")
    g["ret_gdn_w_in"] = jnp.concatenate([d_w_main, d_w_small[:, :2 * N_HEADS]], axis=1)
    dx, dgm0 = norm_bwd(x, w["norm_mix"][0:1], dhn0, dh1, name="mix0_norm_bwd")

    g["gdn_a_log"] = dal[:, :N_HEADS]
    g["gdn_dt_bias"] = ddt[:, :N_HEADS]
    g["gdn_out_gain"] = dgain
    g["norm_mix"] = jnp.concatenate([dgm0, dgm1], axis=0)
    g["norm_ffn"] = jnp.concatenate([dgf0, dgf1], axis=0)
    g["ffn_w_up"] = jnp.stack([dwu0, dwu1])
    g["ffn_conv_w"] = jnp.stack([dcw0, dcw1])
    g["ffn_conv_b"] = jnp.concatenate([dcb0, dcb1], axis=0)
    g["ffn_w_down"] = jnp.stack([dwd0, dwd1])
    return loss, dx, g


WEIGHTS = ("norm_mix", "norm_ffn", "ret_gdn_w_in", "gdn_conv_w", "gdn_a_log", "gdn_dt_bias", "gdn_out_gain", "ret_gdn_w_out",
           "lru_w_in", "lru_conv_w", "lru_conv_b", "lru_w_a", "lru_b_a", "lru_w_x", "lru_b_x", "lru_lambda", "lru_w_out",
           "ffn_w_up", "ffn_conv_w", "ffn_conv_b", "ffn_w_down", "norm_final")
MATMUL_SHARDED = {"ret_gdn_w_in": 1, "ret_gdn_w_out": 0, "lru_w_in": 1, "lru_w_out": 0, "ffn_w_up": 2, "ffn_w_down": 1}
VECTOR_SHARDED = {"gdn_conv_w": 1, "lru_conv_w": 1, "lru_conv_b": 1, "lru_b_a": 1, "lru_b_x": 1, "lru_lambda": 1, "ffn_conv_w": 2}
SHARDED = {**MATMUL_SHARDED, **VECTOR_SHARDED}
REPLICATED = tuple(n for n in WEIGHTS if n not in SHARDED)
SQUEEZE = {"ret_gdn_w_in", "gdn_conv_w", "ret_gdn_w_out", "lru_w_in", "lru_conv_w", "lru_w_a", "lru_w_x", "lru_w_out"}
MIX_IN = MIX_MAIN + 2 * N_HEADS
BIG_LAYOUT = {
    "ret_gdn_w_in": ((D_MODEL, MIX_IN), (2, D_MODEL // 2, N_SHARD, MIX_IN // N_SHARD), (0, 2, 1, 3)),
    "ret_gdn_w_out": ((2 * GROUP, D_MODEL), (N_SHARD, 2, GROUP // N_SHARD, D_MODEL), (1, 0, 2, 3)),
    "lru_w_in": ((D_MODEL, 2 * D_MODEL), (2, D_MODEL // 2, N_SHARD, 2 * D_MODEL // N_SHARD), (0, 2, 1, 3)),
    "lru_w_out": ((D_MODEL, D_MODEL), (N_SHARD, 2, D_MODEL // (2 * N_SHARD), D_MODEL), (1, 0, 2, 3)),
    "ffn_w_up": ((2, D_MODEL, 2 * D_FF), (2, D_MODEL, N_SHARD, 2 * D_FF // N_SHARD), (0, 2, 1, 3)),
    "ffn_w_down": ((2, D_FF, D_MODEL), (2, N_SHARD, D_FF // N_SHARD, D_MODEL), (0, 1, 2, 3)),
}


def _local_view(name, a):
    if name in SQUEEZE:
        return a[0]
    if a.ndim == 1:
        return a[None, :]
    return a


def kernel(x, norm_mix, norm_ffn, ret_gdn_w_in, gdn_conv_w, gdn_a_log, gdn_dt_bias, gdn_out_gain, ret_gdn_w_out, lru_w_in, lru_conv_w, lru_conv_b, lru_w_a, lru_b_a, lru_w_x, lru_b_x, lru_lambda, lru_w_out, ffn_w_up, ffn_conv_w, ffn_conv_b, ffn_w_down, norm_final, loss_target, m_norm_mix, m_norm_ffn, m_ret_gdn_w_in, m_gdn_conv_w, m_gdn_a_log, m_gdn_dt_bias, m_gdn_out_gain, m_ret_gdn_w_out, m_lru_w_in, m_lru_conv_w, m_lru_conv_b, m_lru_w_a, m_lru_b_a, m_lru_w_x, m_lru_b_x, m_lru_lambda, m_lru_w_out, m_ffn_w_up, m_ffn_conv_w, m_ffn_conv_b, m_ffn_w_down, m_norm_final, v_norm_mix, v_norm_ffn, v_ret_gdn_w_in, v_gdn_conv_w, v_gdn_a_log, v_gdn_dt_bias, v_gdn_out_gain, v_ret_gdn_w_out, v_lru_w_in, v_lru_conv_w, v_lru_conv_b, v_lru_w_a, v_lru_b_a, v_lru_w_x, v_lru_b_x, v_lru_lambda, v_lru_w_out, v_ffn_w_up, v_ffn_conv_w, v_ffn_conv_b, v_ffn_w_down, v_norm_final):
    given = dict(norm_mix=norm_mix, norm_ffn=norm_ffn, ret_gdn_w_in=ret_gdn_w_in, gdn_conv_w=gdn_conv_w, gdn_a_log=gdn_a_log, gdn_dt_bias=gdn_dt_bias, gdn_out_gain=gdn_out_gain, ret_gdn_w_out=ret_gdn_w_out, lru_w_in=lru_w_in, lru_conv_w=lru_conv_w, lru_conv_b=lru_conv_b, lru_w_a=lru_w_a, lru_b_a=lru_b_a, lru_w_x=lru_w_x, lru_b_x=lru_b_x, lru_lambda=lru_lambda, lru_w_out=lru_w_out, ffn_w_up=ffn_w_up, ffn_conv_w=ffn_conv_w, ffn_conv_b=ffn_conv_b, ffn_w_down=ffn_w_down, norm_final=norm_final)
    mom1 = dict(norm_mix=m_norm_mix, norm_ffn=m_norm_ffn, ret_gdn_w_in=m_ret_gdn_w_in, gdn_conv_w=m_gdn_conv_w, gdn_a_log=m_gdn_a_log, gdn_dt_bias=m_gdn_dt_bias, gdn_out_gain=m_gdn_out_gain, ret_gdn_w_out=m_ret_gdn_w_out, lru_w_in=m_lru_w_in, lru_conv_w=m_lru_conv_w, lru_conv_b=m_lru_conv_b, lru_w_a=m_lru_w_a, lru_b_a=m_lru_b_a, lru_w_x=m_lru_w_x, lru_b_x=m_lru_b_x, lru_lambda=m_lru_lambda, lru_w_out=m_lru_w_out, ffn_w_up=m_ffn_w_up, ffn_conv_w=m_ffn_conv_w, ffn_conv_b=m_ffn_conv_b, ffn_w_down=m_ffn_w_down, norm_final=m_norm_final)
    mom2 = dict(norm_mix=v_norm_mix, norm_ffn=v_norm_ffn, ret_gdn_w_in=v_ret_gdn_w_in, gdn_conv_w=v_gdn_conv_w, gdn_a_log=v_gdn_a_log, gdn_dt_bias=v_gdn_dt_bias, gdn_out_gain=v_gdn_out_gain, ret_gdn_w_out=v_ret_gdn_w_out, lru_w_in=v_lru_w_in, lru_conv_w=v_lru_conv_w, lru_conv_b=v_lru_conv_b, lru_w_a=v_lru_w_a, lru_b_a=v_lru_b_a, lru_w_x=v_lru_w_x, lru_b_x=v_lru_b_x, lru_lambda=v_lru_lambda, lru_w_out=v_lru_w_out, ffn_w_up=v_ffn_w_up, ffn_conv_w=v_ffn_conv_w, ffn_conv_b=v_ffn_conv_b, ffn_w_down=v_ffn_w_down, norm_final=v_norm_final)

    local = {n: _local_view(n, a) for n, a in given.items()}

    core = lax.axis_index("c")
    chip = 2 * lax.axis_index("x") + lax.axis_index("y")
    is_my_chip = lax.broadcasted_iota(jnp.int32, (N_SHARD, 1, 1), 0) == chip

    def by_core(mine, other):
        return jnp.where(core == 0, jnp.stack([mine, other]), jnp.stack([other, mine]))

    mm_names, vec_names, rp_names = list(BIG_LAYOUT), list(VECTOR_SHARDED), list(REPLICATED)
    halves = []
    for n in mm_names:
        _, split, perm = BIG_LAYOUT[n]
        halves.append(local[n].astype(BF16).reshape((2,) + tuple(split[p] for p in perm)[2:]))
    lands, sibs = gather_halves(halves, name="gather_weights")
    full = {}
    for n, mine, land, sib in zip(mm_names, halves, lands, sibs):
        full_shape, split, perm = BIG_LAYOUT[n]
        half_mine = jnp.where(is_my_chip, jnp.where(core == 0, mine[0], mine[1])[None], land)
        half_other = jnp.where(is_my_chip, jnp.where(core == 0, mine[1], mine[0])[None], sib)
        full[n] = by_core(half_mine, half_other).transpose(perm).reshape(full_shape)
    full.update(zip(vec_names, all_gather_shards([local[n] for n in vec_names], [SHARDED[n] for n in vec_names], F32, 32, "p")))
    for n in rp_names:
        full[n] = local[n]

    loss_part, dx, grads = local_step(x[0], loss_target[0], full)
    loss = lax.psum(loss_part[0, 0], ("x", "y", "c"))

    small_names = rp_names + vec_names
    small_shapes = [grads[n].shape for n in small_names]
    small_rows = _pack_rows(sum(int(np.prod(s)) for s in small_shapes), 16)
    small = _pack([grads[n] for n in small_names], small_rows, F32).reshape(2, 1, small_rows // 2, LANES)
    arrays = [grads[n].reshape(BIG_LAYOUT[n][1]).transpose(BIG_LAYOUT[n][2]) for n in mm_names] + [small]
    g_own, g_sib = reduce_over_devices(arrays, [True] * len(mm_names) + [False])

    result = {}
    for i, n in enumerate(mm_names):
        r, cols = g_own[i].shape
        w2, m2, v2 = (_local_view(n, a).reshape(2 * r, cols) for a in (given[n], mom1[n], mom2[n]))
        result[n] = adamw_halves(w2, m2, v2, g_own[i], g_sib[i], name=f"adamw_{n}")

    g_small = dict(zip(small_names, _unpack(by_core(g_own[-1], g_sib[-1]).reshape(small_rows, LANES), small_shapes)))
    for n in vec_names:
        size = local[n].shape[SHARDED[n]]
        g_small[n] = lax.dynamic_slice_in_dim(g_small[n], chip * size, size, axis=SHARDED[n])
    loc_shapes = [local[n].shape for n in small_names]
    loc_rows = _pack_rows(sum(int(np.prod(s)) for s in loc_shapes), 256)
    packs = [_pack([src[n] for n in small_names], loc_rows, F32) for src in (given, g_small, mom1, mom2)]
    d_s, m_s, v_s = adamw(*packs, name="adamw_small")
    for n, d, nm, nv in zip(small_names, _unpack(d_s, loc_shapes), _unpack(m_s, loc_shapes), _unpack(v_s, loc_shapes)):
        result[n] = (g_small[n], d, nm, nv)

    outs = [[result[n][k].reshape(given[n].shape) for n in WEIGHTS] for k in range(4)]
    return (loss, dx[None], *outs[0], *outs[1], *outs[2], *outs[3])
```

```python
import functools

import numpy as np
import jax
import jax.numpy as jnp
from jax import lax
from jax.experimental import pallas as pl
from jax.experimental.pallas import tpu as pltpu
from jax.experimental.pallas import tpu_sc as plsc

F32 = jnp.float32
BF16 = jnp.bfloat16
HI = lax.Precision.HIGHEST
MESH = pl.DeviceIdType.MESH

SEQ = 2048
D_MODEL = 1024
N_HEADS = 4
HEAD = 128
RET_CHUNK = 128
GDN_CHUNK = 64
GROUP = N_HEADS * HEAD
MIX_MAIN = 8 * GROUP
D_FF = 2816
LRU_BLOCKS = 8
LRU_C = 8.0
ROPE_BASE = 10000.0
EPS = 1e-6
N_SHARD = 4
LANES = 128

ADAM_LR, ADAM_B1, ADAM_B2, ADAM_EPS, ADAM_WD, ADAM_STEP = 0.001, 0.9, 0.999, 1e-08, 0.01, 10

VMEM_LIMIT_BYTES = 56 * 1024 * 1024

_roll = pltpu.roll


def _params(**kw):
    return pltpu.CompilerParams(vmem_limit_bytes=VMEM_LIMIT_BYTES, **kw)


def _sds(shape, dtype):
    return jax.ShapeDtypeStruct(tuple(shape), dtype)


def _shift_raw(x, d):
    n = x.shape[0]
    t = lax.broadcasted_iota(jnp.int32, x.shape, 0)
    if d > 0:
        return jnp.where(t >= d, _roll(x, d, 0), 0.0)
    return jnp.where(t < n + d, _roll(x, n + d, 0), 0.0)


@functools.partial(jax.custom_vjp, nondiff_argnums=(1,))
def shift_rows(x, d):
    return _shift_raw(x, d)


def _shift_fwd(x, d):
    return _shift_raw(x, d), None


def _shift_bwd(d, _, g):
    return (_shift_raw(g, -d),)


shift_rows.defvjp(_shift_fwd, _shift_bwd)


@jax.custom_vjp
def swap_halves(x):
    return _roll(x, HEAD // 2, 1)


def _swap_fwd(x):
    return _roll(x, HEAD // 2, 1), None


def _swap_bwd(_, g):
    return (_roll(g, HEAD // 2, 1),)


swap_halves.defvjp(_swap_fwd, _swap_bwd)


def _scan_raw(a, u, reverse):
    n = a.shape[0]
    t = lax.broadcasted_iota(jnp.int32, a.shape, 0)
    d = 1
    while d < n:
        if reverse:
            m = t < n - d
            a_s, u_s = _roll(a, n - d, 0), _roll(u, n - d, 0)
        else:
            m = t >= d
            a_s, u_s = _roll(a, d, 0), _roll(u, d, 0)
        u = a * jnp.where(m, u_s, 0.0) + u
        a = a * jnp.where(m, a_s, 1.0)
        d *= 2
    return u


@jax.custom_vjp
def lin_scan(a, u):
    return _scan_raw(a, u, False)


def _lin_scan_fwd(a, u):
    hs = _scan_raw(a, u, False)
    return hs, (a, hs)


def _lin_scan_bwd(res, g):
    a, hs = res
    lam = _scan_raw(_shift_raw(a, -1), g, True)
    return lam * _shift_raw(hs, 1), lam


lin_scan.defvjp(_lin_scan_fwd, _lin_scan_bwd)


def _bdot(a, b, dims=(((1,), (0,)), ((), ()))):
    return lax.dot_general(a.astype(BF16), b.astype(BF16), dims, preferred_element_type=F32)


def _each(f, *seqs):
    return tuple(f(*a) for a in zip(*seqs))


def _split_bf16(a):
    hi = a.astype(BF16)
    return hi, (a - hi.astype(F32)).astype(BF16)


def _dot3_raw(a_s, b_s):
    a_hl = _each(_split_bf16, a_s)
    b_hl = _each(_split_bf16, b_s)
    hh = _each(lambda a, b: _bdot(a[0], b[0]), a_hl, b_hl)
    hl = _each(lambda a, b: _bdot(a[0], b[1]), a_hl, b_hl)
    lh = _each(lambda a, b: _bdot(a[1], b[0]), a_hl, b_hl)
    return _each(lambda x, y, z: x + (y + z), hh, hl, lh)


@jax.custom_vjp
def dot3(a_s, b_s):
    return _dot3_raw(a_s, b_s)


def _dot3_fwd(a_s, b_s):
    return _dot3_raw(a_s, b_s), (a_s, b_s)


def _dot3_bwd(res, g_s):
    a_s, b_s = res
    return (_each(lambda g, b: _bdot(g, b, (((1,), (1,)), ((), ()))), g_s, b_s),
            _each(lambda a, g: _bdot(a, g, (((0,), (0,)), ((), ()))), a_s, g_s))


dot3.defvjp(_dot3_fwd, _dot3_bwd)


def _eye(n):
    i = lax.broadcasted_iota(jnp.int32, (n, n), 0)
    j = lax.broadcasted_iota(jnp.int32, (n, n), 1)
    return (i == j).astype(F32)


def _unit_lower_inverse_raw(lmats):
    n = lmats[0].shape[0]
    eye = _eye(n)
    ps = _each(lambda l: -l, lmats)
    invs = _each(lambda x: eye + x, ps)
    k = 1
    while 2 * k < n:
        ps = _each(lambda p: _bdot(p, p), ps)
        invs = _each(lambda inv, p: inv + _bdot(inv, p), invs, ps)
        k *= 2
    prods = _dot3_raw(lmats, invs)
    resids = _each(lambda inv, pr: eye - inv - pr, invs, prods)
    return _each(lambda inv, r: inv + _bdot(inv, r), invs, resids)


@jax.custom_vjp
def unit_lower_inverse(lmats):
    return _unit_lower_inverse_raw(lmats)


def _uli_fwd(lmats):
    invs = _unit_lower_inverse_raw(lmats)
    return invs, invs


def _uli_bwd(invs, g_s):
    ms = _each(lambda inv, g: _bdot(inv, g, (((0,), (0,)), ((), ()))), invs, g_s)
    return (_each(lambda m, inv: -_bdot(m, inv, (((1,), (1,)), ((), ()))), ms, invs),)


unit_lower_inverse.defvjp(_uli_fwd, _uli_bwd)


def _cumsum_raw(x, reverse):
    n = x.shape[0]
    t = lax.broadcasted_iota(jnp.int32, x.shape, 0)
    d = 1
    while d < n:
        if reverse:
            x = x + jnp.where(t < n - d, _roll(x, n - d, 0), 0.0)
        else:
            x = x + jnp.where(t >= d, _roll(x, d, 0), 0.0)
        d *= 2
    return x


@jax.custom_vjp
def cumsum_rows(x):
    return _cumsum_raw(x, False)


def _cumsum_fwd(x):
    return _cumsum_raw(x, False), None


def _cumsum_bwd(_, g):
    return (_cumsum_raw(g, True),)


cumsum_rows.defvjp(_cumsum_fwd, _cumsum_bwd)


_NT = (((1,), (1,)), ((), ()))
_TN = (((0,), (0,)), ((), ()))


def _softplus(x):
    return jnp.maximum(x, 0.0) + jnp.log1p(jnp.exp(-jnp.abs(x)))


def _expm1_nonpos(x):
    poly = x * (1.0 + x * (0.5 + x * (1.0 / 6 + x * (1.0 / 24 + x * (1.0 / 120 + x * (1.0 / 720))))))
    return jnp.where(x > -0.25, poly, jnp.exp(x) - 1.0)


def _rms(x):
    return x * lax.rsqrt(jnp.mean(x * x, axis=-1, keepdims=True) + EPS)


def _causal_conv(x, w, width):
    y = w[width - 1:width, :] * x
    for j in range(width - 1):
        y = y + w[j:j + 1, :] * shift_rows(x, width - 1 - j)
    return y


def _norm_fn(x, g):
    return _rms(x) * g


def _ffn_act_fn(ug, uv, wg, wv, bg, bv):
    return jax.nn.silu(_causal_conv(ug, wg, 3) + bg) * (_causal_conv(uv, wv, 3) + bv)


def _gdn_conv_fn(x, w):
    return jax.nn.silu(_causal_conv(x, w, 4))


def _lru_fn(gate, x, cw, cb, wa, ba, wx, bx, lam):
    xr = _causal_conv(x, cw, 4) + cb
    r = jax.nn.sigmoid(_bdot(xr, wa) + ba)
    i = jax.nn.sigmoid(_bdot(xr, wx) + bx)
    log_a = -LRU_C * r * _softplus(-lam)
    a = jnp.exp(log_a)
    u = jnp.sqrt(-_expm1_nonpos(2.0 * log_a)) * (i * xr)
    hs = lin_scan(a, u)
    return jax.nn.gelu(gate) * hs


def _ret_fn(qs, ks, vs, gates, states, cos2, sin2, dmasks, ktails, qdecs, cdecs):
    qrs = _each(lambda q: q * cos2 + swap_halves(q) * sin2, qs)
    krs = _each(lambda k: (k * cos2 + swap_halves(k) * sin2) * (HEAD ** -0.5), ks)
    scores = _each(lambda q, k, m: _bdot(q, k, _NT) * m, qrs, krs, dmasks)
    inter = _each(lambda q, d, s: _bdot(q * d, s), qrs, qdecs, states)
    os_ = _each(lambda sc, v, x: _bdot(sc, v) + x, scores, vs, inter)
    new_states = _each(lambda s, cd, k, kt, v: s * cd + _bdot(k * kt, v, _TN), states, cdecs, krs, ktails, vs)
    ys = _each(lambda o, g: _rms(o) * jax.nn.silu(g), os_, gates)
    return ys, new_states


def _pick_lane(x, lane_idx):
    lane = lax.broadcasted_iota(jnp.int32, x.shape, 1)
    return jnp.sum(jnp.where(lane == lane_idx, x, 0.0), axis=1, keepdims=True)


def _l2norm(x):
    return x * lax.rsqrt(jnp.sum(x * x, axis=-1, keepdims=True) + EPS)


def _gdn_fn(qcs, kcs, vcs, gates, small, a_log, dt_bias, gain, states):
    c = GDN_CHUNK
    heads = tuple(range(len(qcs)))
    qs = _each(lambda x: _l2norm(x) * (HEAD ** -0.5), qcs)
    ks = _each(_l2norm, kcs)
    betas = _each(lambda h: jax.nn.sigmoid(_pick_lane(small, h)), heads)
    gs = _each(lambda h: -jnp.exp(_pick_lane(a_log, h)) * _softplus(_pick_lane(small, h + N_HEADS) + _pick_lane(dt_bias, h)), heads)
    i = lax.broadcasted_iota(jnp.int32, (c, c), 0)
    j = lax.broadcasted_iota(jnp.int32, (c, c), 1)
    tril = i >= j
    gcs = _each(lambda g: cumsum_rows(jnp.broadcast_to(g, (c, LANES)))[:, :1], gs)
    gc_rows = _each(lambda gc: jnp.broadcast_to(gc, (c, c)), gcs)
    decays = _each(lambda r: jnp.where(tril, jnp.exp(jnp.where(tril, r - r.T, 0.0)), 0.0), gc_rows)
    kbs = _each(lambda k, b: k * b, ks, betas)
    lmats = _each(lambda kb, k, d: jnp.where(i > j, _bdot(kb, k, _NT) * d, 0.0), kbs, ks, decays)
    attns = _each(lambda q, k, d: jnp.where(tril, _bdot(q, k, _NT) * d, 0.0), qs, ks, decays)
    invs = unit_lower_inverse(lmats)
    us = dot3(invs, _each(lambda v, b: v * b, vcs, betas))
    ws = dot3(invs, _each(lambda kb, gc: kb * jnp.exp(gc), kbs, gcs))
    g_lasts = _each(lambda g: jnp.sum(g, axis=0, keepdims=True), gs)
    v_news = _each(lambda u, w, s: u - _bdot(w, s), us, ws, states)
    inter = _each(lambda q, gc, s: _bdot(q * jnp.exp(gc), s), qs, gcs, states)
    os_ = _each(lambda x, a, v: x + _bdot(a, v), inter, attns, v_news)
    new_states = _each(lambda s, gl, k, gc, v: s * jnp.exp(gl) + _bdot(k * jnp.exp(gl - gc), v, _TN), states, g_lasts, ks, gcs, v_news)
    ys = _each(lambda o, gate: _rms(o) * gain * jax.nn.silu(gate), os_, gates)
    return ys, new_states


def _final_fn(h, g, target):
    y = _rms(h) * g
    return 0.5 * jnp.sum(jnp.mean(jnp.square(y - target), axis=-1, keepdims=True), axis=0, keepdims=True)


def _tile(n, candidates):
    for t in candidates:
        if n % t == 0:
            return t
    raise ValueError(f"no tile for {n}")


def matmul(a, b, *, ta=False, tb=False, add=None, out_dtype=F32, tm=None, tn=None, name):
    m = a.shape[1] if ta else a.shape[0]
    k = a.shape[0] if ta else a.shape[1]
    n = b.shape[0] if tb else b.shape[1]
    assert k == (b.shape[1] if tb else b.shape[0])
    tm = tm or _tile(m, (1024, 512, 1408, 256, 128))
    tn = tn or _tile(n, (512, 1408, 256, 128))
    dims = (((0 if ta else 1,), (1 if tb else 0,)), ((), ()))

    def body(*refs):
        if add is None:
            a_ref, b_ref, o_ref = refs
        else:
            a_ref, b_ref, r_ref, o_ref = refs
        acc = lax.dot_general(a_ref[...].astype(BF16), b_ref[...].astype(BF16), dims, preferred_element_type=F32)
        if add is not None:
            acc = acc + r_ref[...]
        o_ref[...] = acc.astype(out_dtype)

    a_spec = pl.BlockSpec((k, tm), lambda i, j: (0, i)) if ta else pl.BlockSpec((tm, k), lambda i, j: (i, 0))
    b_spec = pl.BlockSpec((tn, k), lambda i, j: (j, 0)) if tb else pl.BlockSpec((k, tn), lambda i, j: (0, j))
    o_spec = pl.BlockSpec((tm, tn), lambda i, j: (i, j))
    in_specs, args = [a_spec, b_spec], [a, b]
    if add is not None:
        in_specs.append(o_spec)
        args.append(add)
    return pl.pallas_call(body, out_shape=_sds((m, n), out_dtype), grid=(m // tm, n // tn), in_specs=in_specs,
                          out_specs=o_spec, compiler_params=_params(), name=name)(*args)


ROW_TILE = 256


def norm_fwd(x, g, *, name):
    t, d = x.shape

    def body(x_ref, g_ref, o_ref):
        o_ref[...] = _norm_fn(x_ref[...], g_ref[...]).astype(BF16)

    return pl.pallas_call(body, out_shape=_sds((t, d), BF16), grid=(t // ROW_TILE,),
                          in_specs=[pl.BlockSpec((ROW_TILE, d), lambda i: (i, 0)), pl.BlockSpec((1, d), lambda i: (0, 0))],
                          out_specs=pl.BlockSpec((ROW_TILE, d), lambda i: (i, 0)), compiler_params=_params(), name=name)(x, g)


def norm_bwd(x, g, dy, dres, *, name):
    t, d = x.shape

    def body(x_ref, g_ref, dy_ref, dres_ref, dx_ref, dg_ref):
        _, vjp = jax.vjp(_norm_fn, x_ref[...], g_ref[...])
        dx, dg = vjp(dy_ref[...])
        dx_ref[...] = dx + dres_ref[...]

        @pl.when(pl.program_id(0) == 0)
        def _():
            dg_ref[...] = jnp.zeros_like(dg_ref)

        dg_ref[...] += dg

    row = pl.BlockSpec((ROW_TILE, d), lambda i: (i, 0))
    vec = pl.BlockSpec((1, d), lambda i: (0, 0))
    return pl.pallas_call(body, out_shape=(_sds((t, d), F32), _sds((1, d), F32)), grid=(t // ROW_TILE,),
                          in_specs=[row, vec, row, row], out_specs=(row, vec), compiler_params=_params(), name=name)(x, g, dy, dres)


def final_fwd_bwd(h, g, target, *, name):
    t, d = h.shape

    def body(h_ref, g_ref, t_ref, loss_ref, dh_ref, dg_ref):
        tgt = t_ref[...]
        loss, vjp = jax.vjp(lambda hh, gg: _final_fn(hh, gg, tgt), h_ref[...], g_ref[...])
        dh, dg = vjp(jnp.ones((1, 1), F32))
        dh_ref[...] = dh

        @pl.when(pl.program_id(0) == 0)
        def _():
            dg_ref[...] = jnp.zeros_like(dg_ref)
            loss_ref[...] = jnp.zeros_like(loss_ref)

        dg_ref[...] += dg
        loss_ref[...] += jnp.broadcast_to(loss, loss_ref.shape)

    row = pl.BlockSpec((ROW_TILE, d), lambda i: (i, 0))
    vec = pl.BlockSpec((1, d), lambda i: (0, 0))
    return pl.pallas_call(body, out_shape=(_sds((1, LANES), F32), _sds((t, d), F32), _sds((1, d), F32)), grid=(t // ROW_TILE,),
                          in_specs=[row, vec, row], out_specs=(pl.BlockSpec((1, LANES), lambda i: (0, 0)), row, vec),
                          compiler_params=_params(), name=name)(h, g, target)


FFN_FWD_COLS = 256
FFN_BWD_COLS = 128


def ffn_act_fwd(u, cw, cb, *, name):
    t = u.shape[0]
    w = FFN_FWD_COLS
    nb = D_FF // w

    def body(ug_ref, uv_ref, wg_ref, wv_ref, bg_ref, bv_ref, o_ref):
        o_ref[...] = _ffn_act_fn(ug_ref[...], uv_ref[...], wg_ref[...], wv_ref[...], bg_ref[...], bv_ref[...]).astype(BF16)

    def col(rows, off):
        return pl.BlockSpec((rows, w), lambda j: (0, j + off))

    return pl.pallas_call(body, out_shape=_sds((t, D_FF), BF16), grid=(nb,),
                          in_specs=[col(t, 0), col(t, nb), col(3, 0), col(3, nb), col(1, 0), col(1, nb)],
                          out_specs=col(t, 0), compiler_params=_params(), name=name)(u, u, cw, cw, cb, cb)


def ffn_act_bwd(u, cw, cb, da, *, name):
    t = u.shape[0]
    w = FFN_BWD_COLS
    nb = D_FF // w

    def body(ug_ref, uv_ref, wg_ref, wv_ref, bg_ref, bv_ref, da_ref, dug_ref, duv_ref, dwg_ref, dwv_ref, dbg_ref, dbv_ref):
        _, vjp = jax.vjp(_ffn_act_fn, ug_ref[...], uv_ref[...], wg_ref[...], wv_ref[...], bg_ref[...], bv_ref[...])
        dug, duv, dwg, dwv, dbg, dbv = vjp(da_ref[...])
        dug_ref[...] = dug.astype(BF16)
        duv_ref[...] = duv.astype(BF16)
        dwg_ref[...] = dwg
        dwv_ref[...] = dwv
        dbg_ref[...] = dbg
        dbv_ref[...] = dbv

    def col(rows, off):
        return pl.BlockSpec((rows, w), lambda j: (0, j + off))

    outs = pl.pallas_call(
        body, out_shape=(_sds((t, D_FF), BF16), _sds((t, D_FF), BF16), _sds((3, D_FF), F32), _sds((3, D_FF), F32),
                         _sds((1, D_FF), F32), _sds((1, D_FF), F32)),
        grid=(nb,), in_specs=[col(t, 0), col(t, nb), col(3, 0), col(3, nb), col(1, 0), col(1, nb), col(t, 0)],
        out_specs=(col(t, 0), col(t, 0), col(3, 0), col(3, 0), col(1, 0), col(1, 0)), compiler_params=_params(), name=name,
    )(u, u, cw, cw, cb, cb, da)
    dug, duv, dwg, dwv, dbg, dbv = outs
    return dug, duv, jnp.concatenate([dwg, dwv], axis=1), jnp.concatenate([dbg, dbv], axis=1)


GDN_CONV_COLS = 256
GDN_CONV_OFF = 4 * GROUP


def gdn_conv_fwd(p, cw, *, name):
    t = p.shape[0]
    w = GDN_CONV_COLS
    nb = 3 * GROUP // w
    off = GDN_CONV_OFF // w

    def body(x_ref, w_ref, o_ref):
        o_ref[...] = _gdn_conv_fn(x_ref[...], w_ref[...])

    return pl.pallas_call(body, out_shape=_sds((t, 3 * GROUP), F32), grid=(nb,),
                          in_specs=[pl.BlockSpec((t, w), lambda j: (0, j + off)), pl.BlockSpec((4, w), lambda j: (0, j))],
                          out_specs=pl.BlockSpec((t, w), lambda j: (0, j)), compiler_params=_params(), name=name)(p, cw)


def gdn_conv_bwd(p, cw, dc, *, name):
    t = p.shape[0]
    w = GDN_CONV_COLS
    nb = 3 * GROUP // w
    off = GDN_CONV_OFF // w

    def body(x_ref, w_ref, dc_ref, dx_ref, dw_ref):
        _, vjp = jax.vjp(_gdn_conv_fn, x_ref[...], w_ref[...])
        dx, dw = vjp(dc_ref[...])
        dx_ref[...] = dx.astype(BF16)
        dw_ref[...] = dw

    blk = pl.BlockSpec((t, w), lambda j: (0, j))
    wblk = pl.BlockSpec((4, w), lambda j: (0, j))
    return pl.pallas_call(body, out_shape=(_sds((t, 3 * GROUP), BF16), _sds((4, 3 * GROUP), F32)), grid=(nb,),
                          in_specs=[pl.BlockSpec((t, w), lambda j: (0, j + off)), wblk, blk], out_specs=(blk, wblk),
                          compiler_params=_params(), name=name)(p, cw, dc)


def _lru_specs(t):
    w = D_MODEL // LRU_BLOCKS
    gate = pl.BlockSpec((t, w), lambda j: (0, j))
    xin = pl.BlockSpec((t, w), lambda j: (0, j + LRU_BLOCKS))
    cw = pl.BlockSpec((4, w), lambda j: (0, j))
    vec = pl.BlockSpec((1, w), lambda j: (0, j))
    mat = pl.BlockSpec((None, w, w), lambda j: (j, 0, 0))
    return gate, xin, cw, vec, mat


def lru_fwd(gx, cw, cb, wa, ba, wx, bx, lam, *, name):
    t = gx.shape[0]
    gate, xin, cws, vec, mat = _lru_specs(t)

    def body(g_ref, x_ref, cw_ref, cb_ref, wa_ref, ba_ref, wx_ref, bx_ref, lam_ref, o_ref):
        o_ref[...] = _lru_fn(g_ref[...], x_ref[...], cw_ref[...], cb_ref[...], wa_ref[...], ba_ref[...], wx_ref[...],
                             bx_ref[...], lam_ref[...]).astype(BF16)

    return pl.pallas_call(body, out_shape=_sds((t, D_MODEL), BF16), grid=(LRU_BLOCKS,),
                          in_specs=[gate, xin, cws, vec, mat, vec, mat, vec, vec], out_specs=gate,
                          compiler_params=_params(), name=name)(gx, gx, cw, cb, wa, ba, wx, bx, lam)


def lru_bwd(gx, cw, cb, wa, ba, wx, bx, lam, dy, *, name):
    t = gx.shape[0]
    gate, xin, cws, vec, mat = _lru_specs(t)

    def body(g_ref, x_ref, cw_ref, cb_ref, wa_ref, ba_ref, wx_ref, bx_ref, lam_ref, dy_ref,
             dg_ref, dx_ref, dcw_ref, dcb_ref, dwa_ref, dba_ref, dwx_ref, dbx_ref, dlam_ref):
        _, vjp = jax.vjp(_lru_fn, g_ref[...], x_ref[...], cw_ref[...], cb_ref[...], wa_ref[...], ba_ref[...], wx_ref[...],
                         bx_ref[...], lam_ref[...])
        dg, dx, dcw, dcb, dwa, dba, dwx, dbx, dlam = vjp(dy_ref[...])
        dg_ref[...] = dg.astype(BF16)
        dx_ref[...] = dx.astype(BF16)
        dcw_ref[...] = dcw
        dcb_ref[...] = dcb
        dwa_ref[...] = dwa
        dba_ref[...] = dba
        dwx_ref[...] = dwx
        dbx_ref[...] = dbx
        dlam_ref[...] = dlam

    d = D_MODEL
    w = d // LRU_BLOCKS
    out_shape = (_sds((t, d), BF16), _sds((t, d), BF16), _sds((4, d), F32), _sds((1, d), F32), _sds((LRU_BLOCKS, w, w), F32),
                 _sds((1, d), F32), _sds((LRU_BLOCKS, w, w), F32), _sds((1, d), F32), _sds((1, d), F32))
    return pl.pallas_call(body, out_shape=out_shape, grid=(LRU_BLOCKS,),
                          in_specs=[gate, xin, cws, vec, mat, vec, mat, vec, vec, gate],
                          out_specs=(gate, gate, cws, vec, mat, vec, mat, vec, vec), compiler_params=_params(), name=name,
                          )(gx, gx, cw, cb, wa, ba, wx, bx, lam, dy)


def _ret_tables():
    half = HEAD // 2
    inv_freq = (np.float32(ROPE_BASE) ** (-np.arange(half, dtype=np.float32) / np.float32(half))).astype(np.float32)
    ang = (np.arange(SEQ, dtype=np.float32)[:, None] * inv_freq[None, :]).astype(np.float64)
    cos2 = np.concatenate([np.cos(ang), np.cos(ang)], axis=1).astype(np.float32)
    sin2 = np.concatenate([-np.sin(ang), np.sin(ang)], axis=1).astype(np.float32)
    c = RET_CHUNK
    log_gamma = np.log1p(-np.exp2(-5.0 - np.arange(N_HEADS, dtype=np.float64)))
    idx = np.arange(c, dtype=np.float64)
    rel = idx[:, None] - idx[None, :]
    dmask = np.where(rel >= 0, np.exp(log_gamma[:, None, None] * np.maximum(rel, 0.0)), 0.0)
    ones = np.ones((N_HEADS, c, HEAD))
    ktail = np.exp(log_gamma[:, None] * (c - 1 - idx))[:, :, None] * ones
    qdec = np.exp(log_gamma[:, None] * (idx + 1.0))[:, :, None] * ones
    cdec = np.exp(log_gamma * c)[:, None, None] * ones
    return tuple(jnp.asarray(a, F32) for a in (cos2, sin2, dmask, ktail, qdec, cdec))


def _ret_specs(rev):
    c = RET_CHUNK
    nc = SEQ // c

    def n_of(n):
        return nc - 1 - n if rev else n

    def group(off):
        return pl.BlockSpec((c, GROUP), lambda n: (n_of(n), off))

    tab = pl.BlockSpec((c, HEAD), lambda n: (n_of(n), 0))
    const = pl.BlockSpec((N_HEADS, c, HEAD), lambda n: (0, 0, 0))
    state = pl.BlockSpec((N_HEADS, None, HEAD, HEAD), lambda n: (0, n_of(n), 0, 0))
    return group, tab, const, state, nc


def _head(h):
    return slice(h * HEAD, (h + 1) * HEAD)


def ret_fwd(p, tables, *, name):
    group, tab, const, state, nc = _ret_specs(False)

    def body(q_ref, k_ref, v_ref, g_ref, cos_ref, sin_ref, dm_ref, kt_ref, qd_ref, cd_ref, y_ref, st_ref, s_scr):
        @pl.when(pl.program_id(0) == 0)
        def _():
            s_scr[...] = jnp.zeros_like(s_scr)

        heads = range(N_HEADS)
        states = tuple(s_scr[h] for h in heads)
        ys, new_states = _ret_fn(*(tuple(r[:, _head(h)] for h in heads) for r in (q_ref, k_ref, v_ref, g_ref)), states,
                                 cos_ref[...], sin_ref[...], *(tuple(r[h] for h in heads) for r in (dm_ref, kt_ref, qd_ref, cd_ref)))
        for h in heads:
            st_ref[h] = states[h]
            y_ref[:, _head(h)] = ys[h].astype(BF16)
            s_scr[h] = new_states[h]

    return pl.pallas_call(
        body, out_shape=(_sds((SEQ, GROUP), BF16), _sds((N_HEADS, nc, HEAD, HEAD), F32)), grid=(nc,),
        in_specs=[group(0), group(1), group(2), group(3), tab, tab, const, const, const, const],
        out_specs=(group(0), state), scratch_shapes=[pltpu.VMEM((N_HEADS, HEAD, HEAD), F32)], compiler_params=_params(), name=name,
    )(p, p, p, p, *tables)


def ret_bwd(p, tables, states, dy, *, name):
    group, tab, const, state, nc = _ret_specs(True)

    def body(q_ref, k_ref, v_ref, g_ref, cos_ref, sin_ref, dm_ref, kt_ref, qd_ref, cd_ref, st_ref, dy_ref,
             dq_ref, dk_ref, dv_ref, dg_ref, ds_scr):
        @pl.when(pl.program_id(0) == 0)
        def _():
            ds_scr[...] = jnp.zeros_like(ds_scr)

        heads = range(N_HEADS)
        consts = (cos_ref[...], sin_ref[...], *(tuple(r[h] for h in heads) for r in (dm_ref, kt_ref, qd_ref, cd_ref)))
        _, vjp = jax.vjp(lambda *a: _ret_fn(*a, *consts), *(tuple(r[:, _head(h)] for h in heads) for r in (q_ref, k_ref, v_ref, g_ref)),
                         tuple(st_ref[h] for h in heads))
        dqs, dks, dvs, dgs, dss = vjp((tuple(dy_ref[:, _head(h)] for h in heads), tuple(ds_scr[h] for h in heads)))
        for h in heads:
            dq_ref[:, _head(h)] = dqs[h].astype(BF16)
            dk_ref[:, _head(h)] = dks[h].astype(BF16)
            dv_ref[:, _head(h)] = dvs[h].astype(BF16)
            dg_ref[:, _head(h)] = dgs[h].astype(BF16)
            ds_scr[h] = dss[h]

    out = _sds((SEQ, GROUP), BF16)
    return pl.pallas_call(
        body, out_shape=(out, out, out, out), grid=(nc,),
        in_specs=[group(0), group(1), group(2), group(3), tab, tab, const, const, const, const, state, group(0)],
        out_specs=(group(0), group(0), group(0), group(0)), scratch_shapes=[pltpu.VMEM((N_HEADS, HEAD, HEAD), F32)],
        compiler_params=_params(), name=name,
    )(p, p, p, p, *tables, states, dy)


def _gdn_specs(rev):
    c = GDN_CHUNK
    nc = SEQ // c

    def n_of(n):
        return nc - 1 - n if rev else n

    def group(off):
        return pl.BlockSpec((c, GROUP), lambda n: (n_of(n), off))

    small = pl.BlockSpec((c, LANES), lambda n: (n_of(n), 0))
    vec = pl.BlockSpec((1, LANES), lambda n: (0, 0))
    state = pl.BlockSpec((N_HEADS, None, HEAD, HEAD), lambda n: (0, n_of(n), 0, 0))
    return group, small, vec, state, nc


GDN_GATE_GROUP = 7


def gdn_fwd(conv, p, small, a_log, dt_bias, gain, *, name):
    group, sm, vec, state, nc = _gdn_specs(False)

    def body(q_ref, k_ref, v_ref, g_ref, sm_ref, al_ref, dt_ref, gn_ref, y_ref, st_ref, s_scr):
        @pl.when(pl.program_id(0) == 0)
        def _():
            s_scr[...] = jnp.zeros_like(s_scr)

        states = tuple(s_scr[h] for h in range(N_HEADS))
        ys, new_states = _gdn_fn(*(tuple(r[:, _head(h)] for h in range(N_HEADS)) for r in (q_ref, k_ref, v_ref, g_ref)),
                                 sm_ref[...], al_ref[...], dt_ref[...], gn_ref[...], states)
        for h in range(N_HEADS):
            st_ref[h] = states[h]
            y_ref[:, _head(h)] = ys[h].astype(BF16)
            s_scr[h] = new_states[h]

    return pl.pallas_call(
        body, out_shape=(_sds((SEQ, GROUP), BF16), _sds((N_HEADS, nc, HEAD, HEAD), F32)), grid=(nc,),
        in_specs=[group(0), group(1), group(2), group(GDN_GATE_GROUP), sm, vec, vec, vec], out_specs=(group(0), state),
        scratch_shapes=[pltpu.VMEM((N_HEADS, HEAD, HEAD), F32)], compiler_params=_params(), name=name,
    )(conv, conv, conv, p, small, a_log, dt_bias, gain)


def gdn_bwd(conv, p, small, a_log, dt_bias, gain, states, dy, *, name):
    group, sm, vec, state, nc = _gdn_specs(True)

    def body(q_ref, k_ref, v_ref, g_ref, sm_ref, al_ref, dt_ref, gn_ref, st_ref, dy_ref,
             dq_ref, dk_ref, dv_ref, dg_ref, dsm_ref, dal_ref, ddt_ref, dgn_ref, ds_scr):
        @pl.when(pl.program_id(0) == 0)
        def _():
            ds_scr[...] = jnp.zeros_like(ds_scr)
            dal_ref[...] = jnp.zeros_like(dal_ref)
            ddt_ref[...] = jnp.zeros_like(ddt_ref)
            dgn_ref[...] = jnp.zeros_like(dgn_ref)

        per_head = tuple(tuple(r[:, _head(h)] for h in range(N_HEADS)) for r in (q_ref, k_ref, v_ref, g_ref))
        _, vjp = jax.vjp(_gdn_fn, *per_head, sm_ref[...], al_ref[...], dt_ref[...], gn_ref[...],
                         tuple(st_ref[h] for h in range(N_HEADS)))
        cts = (tuple(dy_ref[:, _head(h)] for h in range(N_HEADS)), tuple(ds_scr[h] for h in range(N_HEADS)))
        dqs, dks, dvs, dgs, dsm, dal, ddt, dgn, dss = vjp(cts)
        for h in range(N_HEADS):
            dq_ref[:, _head(h)] = dqs[h]
            dk_ref[:, _head(h)] = dks[h]
            dv_ref[:, _head(h)] = dvs[h]
            dg_ref[:, _head(h)] = dgs[h].astype(BF16)
            ds_scr[h] = dss[h]
        dsm_ref[...] = dsm
        dal_ref[...] += dal
        ddt_ref[...] += ddt
        dgn_ref[...] += dgn

    f = _sds((SEQ, GROUP), F32)
    pv = _sds((1, LANES), F32)
    return pl.pallas_call(
        body, out_shape=(f, f, f, _sds((SEQ, GROUP), BF16), _sds((SEQ, LANES), F32), pv, pv, pv), grid=(nc,),
        in_specs=[group(0), group(1), group(2), group(GDN_GATE_GROUP), sm, vec, vec, vec, state, group(1)],
        out_specs=(group(0), group(0), group(0), group(0), sm, vec, vec, vec), scratch_shapes=[pltpu.VMEM((N_HEADS, HEAD, HEAD), F32)],
        compiler_params=_params(), name=name,
    )(conv, conv, conv, p, small, a_log, dt_bias, gain, states, dy)


PACK_ROW_TILE = 1024


def adamw(w, g, m, v, *, name):
    r = w.shape[0]
    tr = _tile(r, (PACK_ROW_TILE, 256, 128, 64, 32, 16, 8))

    def body(w_ref, g_ref, m_ref, v_ref, d_ref, nm_ref, nv_ref):
        gg = g_ref[...]
        nm = ADAM_B1 * m_ref[...] + (1.0 - ADAM_B1) * gg
        nv = ADAM_B2 * v_ref[...] + (1.0 - ADAM_B2) * jnp.square(gg)
        m_hat = nm / (1.0 - ADAM_B1 ** ADAM_STEP)
        v_hat = nv / (1.0 - ADAM_B2 ** ADAM_STEP)
        d_ref[...] = -ADAM_LR * (m_hat / (jnp.sqrt(v_hat) + ADAM_EPS) + ADAM_WD * w_ref[...])
        nm_ref[...] = nm
        nv_ref[...] = nv

    blk = pl.BlockSpec((tr, LANES), lambda i: (i, 0))
    o = _sds((r, LANES), F32)
    return pl.pallas_call(body, out_shape=(o, o, o), grid=(r // tr,), in_specs=[blk] * 4, out_specs=(blk, blk, blk),
                          compiler_params=_params(), name=name)(w, g, m, v)


ELEMENTWISE_BLOCK_BYTES = 2 * 1024 * 1024


def _row_tile(r, c):
    best = None
    for tr in range(8, r + 1, 8):
        if r % tr == 0 and tr * c * 4 <= ELEMENTWISE_BLOCK_BYTES:
            best = tr
    if best is None:
        raise ValueError(f"no row tile for ({r}, {c})")
    return best


def _core_index():
    return lax.axis_index("c").astype(jnp.int32).reshape(1)


def _chip_index():
    return (2 * lax.axis_index("x") + lax.axis_index("y")).astype(jnp.int32).reshape(1)


def adamw_halves(w, m, v, g_own, g_sib, *, layer=0, prev=None, name):
    n_layers, rows, c = w.shape
    r = rows // 2
    tr = _row_tile(r, c)
    nb = r // tr

    def body(c_ref, w_ref, m_ref, v_ref, own_ref, sib_ref, *rest):
        g_ref, d_ref, nm_ref, nv_ref = rest[-4:]
        gg = jnp.where(pl.program_id(0) == c_ref[0], own_ref[...], sib_ref[...])
        nm = ADAM_B1 * m_ref[...] + (1.0 - ADAM_B1) * gg
        nv = ADAM_B2 * v_ref[...] + (1.0 - ADAM_B2) * jnp.square(gg)
        m_hat = nm / (1.0 - ADAM_B1 ** ADAM_STEP)
        v_hat = nv / (1.0 - ADAM_B2 ** ADAM_STEP)
        g_ref[...] = gg
        d_ref[...] = -ADAM_LR * (m_hat / (jnp.sqrt(v_hat) + ADAM_EPS) + ADAM_WD * w_ref[...])
        nm_ref[...] = nm
        nv_ref[...] = nv

    full = pl.BlockSpec((None, tr, c), lambda h, i, cr: (layer, h * nb + i, 0))
    half = pl.BlockSpec((tr, c), lambda h, i, cr: (i, 0))
    o = _sds((n_layers, rows, c), F32)
    prev = list(prev or ())
    gs = pltpu.PrefetchScalarGridSpec(num_scalar_prefetch=1, grid=(2, nb), in_specs=[full, full, full, half, half] + [_ANY] * len(prev),
                                      out_specs=(full, full, full, full))
    n_fixed = 6
    return pl.pallas_call(body, out_shape=(o, o, o, o), grid_spec=gs, compiler_params=_params(), name=name,
                          input_output_aliases={n_fixed + k: k for k in range(len(prev))})(
        _core_index(), w, m, v, g_own, g_sib, *prev)


def add_core_halves(g2, land, *, out_dtype, name):
    _, ns, r, cols = g2.shape
    tr = _row_tile(r, cols)

    def body(c_ref, a_ref, b_ref, o_ref):
        o_ref[...] = (a_ref[...] + b_ref[...]).astype(out_dtype)

    gs = pltpu.PrefetchScalarGridSpec(
        num_scalar_prefetch=1, grid=(ns, r // tr),
        in_specs=[pl.BlockSpec((None, None, tr, cols), lambda s, i, cr: (cr[0], s, i, 0)),
                  pl.BlockSpec((None, tr, cols), lambda s, i, cr: (s, i, 0))],
        out_specs=pl.BlockSpec((None, tr, cols), lambda s, i, cr: (s, i, 0)))
    return pl.pallas_call(body, out_shape=_sds((ns, r, cols), out_dtype), grid_spec=gs, compiler_params=_params(), name=name)(
        _core_index(), g2, land)


def sum_over_chips(own, land, *, scatter, name):
    _, r, cols = own.shape
    tr = _row_tile(r, cols)

    def body(mine_ref, own_ref, l0, l1, l2, l3, o_ref):
        mine = mine_ref[0]
        mine_val = own_ref[...]
        acc = None
        for s, l_ref in enumerate((l0, l1, l2, l3)):
            val = jnp.where(mine == s, mine_val, l_ref[...]).astype(F32)
            acc = val if acc is None else acc + val
        o_ref[...] = acc

    def slot(s):
        return pl.BlockSpec((None, tr, cols), lambda i, mr: (jnp.where(mr[0] == s, (s + 1) % N_SHARD, s), i, 0))

    own_spec = pl.BlockSpec((None, tr, cols), lambda i, mr: (mr[0] if scatter else 0, i, 0))
    gs = pltpu.PrefetchScalarGridSpec(num_scalar_prefetch=1, grid=(r // tr,), in_specs=[own_spec] + [slot(s) for s in range(N_SHARD)],
                                      out_specs=pl.BlockSpec((tr, cols), lambda i, mr: (i, 0)))
    return pl.pallas_call(body, out_shape=_sds((r, cols), F32), grid_spec=gs, compiler_params=_params(), name=name)(
        _chip_index(), own, land, land, land, land)


_ANY = pl.BlockSpec(memory_space=pl.ANY)


def xy_exchange(src, *, scatter, name):
    rh = src.shape[1]

    def body(src_ref, land_ref, send_sems, recv_sems, loc_sem):
        x, y, c = lax.axis_index("x"), lax.axis_index("y"), lax.axis_index("c")
        mine = 2 * x + y
        peers = [(1 - x, y), (x, 1 - y), (1 - x, 1 - y)]

        def piece(shard):
            return src_ref.at[shard] if scatter else src_ref.at[c]

        def copy(k, px, py, dst_slot):
            return pltpu.make_async_remote_copy(src_ref=piece(2 * px + py), dst_ref=land_ref.at[dst_slot], send_sem=send_sems.at[k],
                                                recv_sem=recv_sems.at[k], device_id=(px, py, c), device_id_type=MESH)

        keep = pltpu.make_async_copy(piece(mine), land_ref.at[mine], loc_sem)
        keep.start()
        sends = [copy(k, px, py, mine) for k, (px, py) in enumerate(peers)]
        for cp in sends:
            cp.start()
        for cp in sends:
            cp.wait_send()
        for k, (px, py) in enumerate(peers):
            copy(k, px, py, 2 * px + py).wait_recv()
        keep.wait()

    return pl.pallas_call(body, out_shape=_sds((N_SHARD, rh, LANES), src.dtype), in_specs=[_ANY], out_specs=_ANY,
                          scratch_shapes=[pltpu.SemaphoreType.DMA((3,)), pltpu.SemaphoreType.DMA((3,)), pltpu.SemaphoreType.DMA(())],
                          name=name)(src)


def core_exchange(src, *, send_other_half, name):
    def body(src_ref, out_ref, send_sem, recv_sem, loc_sem):
        x, y, c = lax.axis_index("x"), lax.axis_index("y"), lax.axis_index("c")
        if send_other_half:
            cp = pltpu.make_async_remote_copy(src_ref=src_ref.at[1 - c], dst_ref=out_ref, send_sem=send_sem, recv_sem=recv_sem,
                                              device_id=(x, y, 1 - c), device_id_type=MESH)
            cp.start()
            cp.wait_send()
            cp.wait_recv()
        else:
            keep = pltpu.make_async_copy(src_ref, out_ref.at[c], loc_sem)
            keep.start()
            cp = pltpu.make_async_remote_copy(src_ref=src_ref, dst_ref=out_ref.at[c], send_sem=send_sem, recv_sem=recv_sem,
                                              device_id=(x, y, 1 - c), device_id_type=MESH)
            cp.start()
            cp.wait_send()
            pltpu.make_async_remote_copy(src_ref=src_ref, dst_ref=out_ref.at[1 - c], send_sem=send_sem, recv_sem=recv_sem,
                                         device_id=(x, y, 1 - c), device_id_type=MESH).wait_recv()
            keep.wait()

    out_shape = _sds(src.shape[1:], src.dtype) if send_other_half else _sds((2,) + src.shape, src.dtype)
    return pl.pallas_call(body, out_shape=out_shape, in_specs=[_ANY], out_specs=_ANY,
                          scratch_shapes=[pltpu.SemaphoreType.DMA(()), pltpu.SemaphoreType.DMA(()), pltpu.SemaphoreType.DMA(())],
                          name=name)(src)


def _comm_call(body, ins, out_shapes, sem_counts, name):
    return pl.pallas_call(body, out_shape=tuple(out_shapes), in_specs=[_ANY] * len(ins), out_specs=tuple([_ANY] * len(out_shapes)),
                          scratch_shapes=[pltpu.SemaphoreType.DMA((k,)) for k in sem_counts], name=name)(*ins)


def _sequencer_call(body, ins, out_shapes, sem_counts, name, collective_id):
    return pl.kernel(body, out_type=list(out_shapes), mesh=plsc.ScalarSubcoreMesh(axis_name="sequencer", num_cores=1), name=name,
                     scratch_types=[pltpu.SemaphoreType.DMA((k,)) for k in sem_counts],
                     compiler_params=pltpu.CompilerParams(collective_id=collective_id))(*ins)


def _handshake(peers):
    barrier = pltpu.get_barrier_semaphore()
    for peer in peers:
        pl.semaphore_signal(barrier, inc=1, device_id=peer, device_id_type=MESH)
    pl.semaphore_wait(barrier, len(peers))


def _xy_peers(x, y):
    return [(1 - x, y), (x, 1 - y), (1 - x, 1 - y)]


def gather_halves(halves, *, name, collective_id):
    n = len(halves)

    def body(*refs):
        ins, lands, sibs = refs[:n], refs[n:2 * n], refs[2 * n:3 * n]
        ici_send, ici_recv, d2d_send, d2d_recv = refs[3 * n:]
        x, y, c = lax.axis_index("x"), lax.axis_index("y"), lax.axis_index("c")
        mine = 2 * x + y
        peers = _xy_peers(x, y)
        _handshake([(px, py, c) for px, py in peers] + [(x, y, 1 - c)])

        def ici(i, k, slot):
            px, py = peers[k]
            return pltpu.make_async_remote_copy(src_ref=ins[i].at[c], dst_ref=lands[i].at[slot], send_sem=ici_send.at[3 * i + k],
                                                recv_sem=ici_recv.at[3 * i + k], device_id=(px, py, c), device_id_type=MESH)

        def pass_on(i, k):
            px, py = peers[k]
            slot = 2 * px + py
            return pltpu.make_async_remote_copy(src_ref=lands[i].at[slot], dst_ref=sibs[i].at[slot], send_sem=d2d_send.at[3 * i + k],
                                                recv_sem=d2d_recv.at[3 * i + k], device_id=(x, y, 1 - c), device_id_type=MESH)

        sends = [ici(i, k, mine) for i in range(n) for k in range(3)]
        for cp in sends:
            cp.start()
        passed = []
        for i in range(n):
            for k in range(3):
                px, py = peers[k]
                ici(i, k, 2 * px + py).wait_recv()
                cp = pass_on(i, k)
                cp.start()
                passed.append(cp)
        for cp in passed:
            cp.wait_recv()
        for cp in sends + passed:
            cp.wait_send()

    outs = [_sds((N_SHARD,) + h.shape[1:], h.dtype) for h in halves]
    res = _sequencer_call(body, halves, outs + outs, [3 * n] * 4, name, collective_id)
    return res[:n], res[n:]


def send_other_half(arrays, *, name):
    n = len(arrays)

    def body(*refs):
        ins, lands = refs[:n], refs[n:2 * n]
        send_sems, recv_sems = refs[2 * n:]
        x, y, c = lax.axis_index("x"), lax.axis_index("y"), lax.axis_index("c")
        copies = [pltpu.make_async_remote_copy(src_ref=ins[i].at[1 - c], dst_ref=lands[i], send_sem=send_sems.at[i],
                                               recv_sem=recv_sems.at[i], device_id=(x, y, 1 - c), device_id_type=MESH) for i in range(n)]
        for cp in copies:
            cp.start()
        for cp in copies:
            cp.wait_recv()
        for cp in copies:
            cp.wait_send()

    return _comm_call(body, arrays, [_sds(a.shape[1:], a.dtype) for a in arrays], [n, n], name)


def send_to_chips(arrays, scatter, *, name, collective_id):
    n = len(arrays)

    def body(*refs):
        ins, lands = refs[:n], refs[n:2 * n]
        send_sems, recv_sems = refs[2 * n:]
        x, y, c = lax.axis_index("x"), lax.axis_index("y"), lax.axis_index("c")
        mine = 2 * x + y
        peers = _xy_peers(x, y)
        _handshake([(px, py, c) for px, py in peers])

        def copy(i, k, dst_slot):
            px, py = peers[k]
            src = ins[i].at[2 * px + py] if scatter[i] else ins[i].at[0]
            return pltpu.make_async_remote_copy(src_ref=src, dst_ref=lands[i].at[dst_slot], send_sem=send_sems.at[3 * i + k],
                                                recv_sem=recv_sems.at[3 * i + k], device_id=(px, py, c), device_id_type=MESH)

        sends = [copy(i, k, mine) for i in range(n) for k in range(3)]
        for cp in sends:
            cp.start()
        for i in range(n):
            for k in range(3):
                px, py = peers[k]
                copy(i, k, 2 * px + py).wait_recv()
        for cp in sends:
            cp.wait_send()

    return _sequencer_call(body, arrays, [_sds((N_SHARD,) + a.shape[1:], a.dtype) for a in arrays], [3 * n, 3 * n], name, collective_id)


def swap_with_other_core(arrays, *, name):
    n = len(arrays)

    def body(*refs):
        ins, lands = refs[:n], refs[n:2 * n]
        send_sems, recv_sems = refs[2 * n:]
        x, y, c = lax.axis_index("x"), lax.axis_index("y"), lax.axis_index("c")
        copies = [pltpu.make_async_remote_copy(src_ref=ins[i], dst_ref=lands[i], send_sem=send_sems.at[i], recv_sem=recv_sems.at[i],
                                               device_id=(x, y, 1 - c), device_id_type=MESH) for i in range(n)]
        for cp in copies:
            cp.start()
        for cp in copies:
            cp.wait_recv()
        for cp in copies:
            cp.wait_send()

    return _comm_call(body, arrays, [_sds(a.shape, a.dtype) for a in arrays], [n, n], name)


def _pack_rows(n_elems, row_multiple):
    rows = -(-n_elems // LANES)
    return -(-rows // row_multiple) * row_multiple


def _pack(arrays, rows, dtype):
    flat = jnp.concatenate([a.reshape(-1).astype(dtype) for a in arrays])
    return jnp.pad(flat, (0, rows * LANES - flat.shape[0])).reshape(rows, LANES)


def _unpack(packed, shapes):
    flat = packed.reshape(-1)
    out, off = [], 0
    for s in shapes:
        n = int(np.prod(s))
        out.append(flat[off:off + n].reshape(s))
        off += n
    return out


def all_gather_shards(shards, axes, dtype, row_multiple, tag):
    shapes = [s.shape for s in shards]
    rows = _pack_rows(sum(int(np.prod(s)) for s in shapes), row_multiple)
    packed = _pack(shards, rows, dtype).reshape(2, rows // 2, LANES)
    land = xy_exchange(packed, scatter=False, name=f"gather_xy_{tag}")
    both = core_exchange(land, send_other_half=False, name=f"gather_c_{tag}")
    per_shard = jnp.swapaxes(both, 0, 1).reshape(N_SHARD, rows, LANES)
    pieces = [_unpack(per_shard[s], shapes) for s in range(N_SHARD)]
    return [jnp.concatenate([pieces[s][i] for s in range(N_SHARD)], axis=ax) for i, ax in enumerate(axes)]


def reduce_over_devices(arrays, scatter, *, tag, collective_id):
    land = send_other_half(arrays, name=f"reduce_core_send_{tag}")
    chip = [add_core_halves(a, l, out_dtype=BF16 if sc else F32, name=f"reduce_core_add_{tag}_{i}")
            for i, (a, l, sc) in enumerate(zip(arrays, land, scatter))]
    land = send_to_chips(chip, scatter, name=f"reduce_chip_send_{tag}", collective_id=collective_id)
    own = [sum_over_chips(ch, l, scatter=sc, name=f"reduce_chip_add_{tag}_{i}") for i, (ch, l, sc) in enumerate(zip(chip, land, scatter))]
    sib = swap_with_other_core(own, name=f"reduce_core_swap_{tag}")
    return own, sib


def _ffn_layer_fwd(h, norm_g, w_up, cw, cb, w_down, tag):
    hn = norm_fwd(h, norm_g, name=f"ffn_norm_{tag}")
    u = matmul(hn, w_up, name=f"ffn_up_{tag}")
    act = ffn_act_fwd(u, cw, cb, name=f"ffn_act_{tag}")
    out = matmul(act, w_down, add=h, name=f"ffn_down_{tag}")
    return out, (h, hn, u, act)


def _ffn_layer_bwd(saved, dout, norm_g, w_up, cw, cb, w_down, tag):
    h, hn, u, act = saved
    dact = matmul(dout, w_down, tb=True, name=f"ffn_down_dx_{tag}")
    d_w_down = matmul(act, dout, ta=True, name=f"ffn_down_dw_{tag}")
    dug, duv, dcw, dcb = ffn_act_bwd(u, cw, cb, dact, name=f"ffn_act_bwd_{tag}")
    du = jnp.concatenate([dug, duv], axis=1)
    dhn = matmul(du, w_up, tb=True, name=f"ffn_up_dx_{tag}")
    d_w_up = matmul(hn, du, ta=True, name=f"ffn_up_dw_{tag}")
    dh, dg = norm_bwd(h, norm_g, dhn, dout, name=f"ffn_norm_bwd_{tag}")
    return dh, dg, d_w_up, dcw, dcb, d_w_down


def local_step(x, target, w):
    g = {}
    tables = _ret_tables()
    w_in = w["ret_gdn_w_in"]
    w_main = w_in[:, :MIX_MAIN]
    w_small = jnp.pad(w_in[:, MIX_MAIN:], ((0, 0), (0, LANES - 2 * N_HEADS)))
    a_log = jnp.pad(w["gdn_a_log"], ((0, 0), (0, LANES - N_HEADS)))
    dt_bias = jnp.pad(w["gdn_dt_bias"], ((0, 0), (0, LANES - N_HEADS)))

    hn0 = norm_fwd(x, w["norm_mix"][0:1], name="mix0_norm")
    p = matmul(hn0, w_main, name="mix0_in")
    small = matmul(hn0, w_small, name="mix0_in_small")
    y_ret, s_ret = ret_fwd(p, tables, name="ret_fwd")
    conv = gdn_conv_fwd(p, w["gdn_conv_w"], name="gdn_conv")
    y_gdn, s_gdn = gdn_fwd(conv, p, small, a_log, dt_bias, w["gdn_out_gain"], name="gdn_fwd")
    y0 = jnp.concatenate([y_ret, y_gdn], axis=1)
    h1 = matmul(y0, w["ret_gdn_w_out"], add=x, name="mix0_out")
    h2, ffn0 = _ffn_layer_fwd(h1, w["norm_ffn"][0:1], w["ffn_w_up"][0], w["ffn_conv_w"][0], w["ffn_conv_b"][0:1], w["ffn_w_down"][0], "0")

    hn1 = norm_fwd(h2, w["norm_mix"][1:2], name="mix1_norm")
    gx = matmul(hn1, w["lru_w_in"], name="mix1_in")
    lru_p = (w["lru_conv_w"], w["lru_conv_b"], w["lru_w_a"], w["lru_b_a"], w["lru_w_x"], w["lru_b_x"], w["lru_lambda"])
    y1 = lru_fwd(gx, *lru_p, name="lru_fwd")
    h3 = matmul(y1, w["lru_w_out"], add=h2, name="mix1_out")
    h4, ffn1 = _ffn_layer_fwd(h3, w["norm_ffn"][1:2], w["ffn_w_up"][1], w["ffn_conv_w"][1], w["ffn_conv_b"][1:2], w["ffn_w_down"][1], "1")

    loss, dh4, g["norm_final"] = final_fwd_bwd(h4, w["norm_final"], target, name="final")

    dh3, dgf1, dwu1, dcw1, dcb1, dwd1 = _ffn_layer_bwd(ffn1, dh4, w["norm_ffn"][1:2], w["ffn_w_up"][1], w["ffn_conv_w"][1],
                                                     w["ffn_conv_b"][1:2], w["ffn_w_down"][1], "1")
    dy1 = matmul(dh3, w["lru_w_out"], tb=True, name="mix1_out_dx")
    g["lru_w_out"] = matmul(y1, dh3, ta=True, name="mix1_out_dw")
    dgate, dxr, g["lru_conv_w"], g["lru_conv_b"], g["lru_w_a"], g["lru_b_a"], g["lru_w_x"], g["lru_b_x"], g["lru_lambda"] = lru_bwd(
        gx, *lru_p, dy1, name="lru_bwd")
    dgx = jnp.concatenate([dgate, dxr], axis=1)
    dhn1 = matmul(dgx, w["lru_w_in"], tb=True, name="mix1_in_dx")
    g["lru_w_in"] = matmul(hn1, dgx, ta=True, name="mix1_in_dw")
    dh2, dgm1 = norm_bwd(h2, w["norm_mix"][1:2], dhn1, dh3, name="mix1_norm_bwd")

    dh1, dgf0, dwu0, dcw0, dcb0, dwd0 = _ffn_layer_bwd(ffn0, dh2, w["norm_ffn"][0:1], w["ffn_w_up"][0], w["ffn_conv_w"][0],
                                                     w["ffn_conv_b"][0:1], w["ffn_w_down"][0], "0")
    dy0 = matmul(dh1, w["ret_gdn_w_out"], tb=True, name="mix0_out_dx")
    g["ret_gdn_w_out"] = matmul(y0, dh1, ta=True, name="mix0_out_dw")
    dq_r, dk_r, dv_r, dg_r = ret_bwd(p, tables, s_ret, dy0, name="ret_bwd")
    dcq, dck, dcv, dg_d, dsmall, dal, ddt, dgain = gdn_bwd(conv, p, small, a_log, dt_bias, w["gdn_out_gain"], s_gdn, dy0, name="gdn_bwd")
    dconv = jnp.concatenate([dcq, dck, dcv], axis=1)
    dp_conv, g["gdn_conv_w"] = gdn_conv_bwd(p, w["gdn_conv_w"], dconv, name="gdn_conv_bwd")
    dp = jnp.concatenate([dq_r, dk_r, dv_r, dg_r, dp_conv, dg_d], axis=1)
    dhn0 = matmul(dp, w_main, tb=True, name="mix0_in_dx")
    dhn0 = matmul(dsmall, w_small, tb=True, add=dhn0, name="mix0_in_small_dx")
    d_w_main = matmul(hn0, dp, ta=True, name="mix0_in_dw")
    d_w_small = matmul(hn0, dsmall, ta=True, name="mix0_in_small_dw")
    g["ret_gdn_w_in"] = jnp.concatenate([d_w_main, d_w_small[:, :2 * N_HEADS]], axis=1)
    dx, dgm0 = norm_bwd(x, w["norm_mix"][0:1], dhn0, dh1, name="mix0_norm_bwd")

    g["gdn_a_log"] = dal[:, :N_HEADS]
    g["gdn_dt_bias"] = ddt[:, :N_HEADS]
    g["gdn_out_gain"] = dgain
    g["norm_mix"] = jnp.concatenate([dgm0, dgm1], axis=0)
    g["norm_ffn"] = jnp.concatenate([dgf0, dgf1], axis=0)
    g["ffn_w_up_0"], g["ffn_w_up_1"] = dwu0, dwu1
    g["ffn_conv_w"] = jnp.stack([dcw0, dcw1])
    g["ffn_conv_b"] = jnp.concatenate([dcb0, dcb1], axis=0)
    g["ffn_w_down"] = jnp.stack([dwd0, dwd1])
    return loss, dx, g


WEIGHTS = ("norm_mix", "norm_ffn", "ret_gdn_w_in", "gdn_conv_w", "gdn_a_log", "gdn_dt_bias", "gdn_out_gain", "ret_gdn_w_out",
           "lru_w_in", "lru_conv_w", "lru_conv_b", "lru_w_a", "lru_b_a", "lru_w_x", "lru_b_x", "lru_lambda", "lru_w_out",
           "ffn_w_up", "ffn_conv_w", "ffn_conv_b", "ffn_w_down", "norm_final")
MATMUL_SHARDED = {"ret_gdn_w_in": 1, "ret_gdn_w_out": 0, "lru_w_in": 1, "lru_w_out": 0, "ffn_w_up": 2, "ffn_w_down": 1}
VECTOR_SHARDED = {"gdn_conv_w": 1, "lru_conv_w": 1, "lru_conv_b": 1, "lru_b_a": 1, "lru_b_x": 1, "lru_lambda": 1, "ffn_conv_w": 2}
SHARDED = {**MATMUL_SHARDED, **VECTOR_SHARDED}
REPLICATED = tuple(n for n in WEIGHTS if n not in SHARDED)
SQUEEZE = {"ret_gdn_w_in", "gdn_conv_w", "ret_gdn_w_out", "lru_w_in", "lru_conv_w", "lru_w_a", "lru_w_x", "lru_w_out"}
MIX_IN = MIX_MAIN + 2 * N_HEADS
BIG_ARRAYS = {
    "ret_gdn_w_in": ("ret_gdn_w_in", None, (D_MODEL, MIX_IN), (2, D_MODEL // 2, N_SHARD, MIX_IN // N_SHARD), (0, 2, 1, 3)),
    "ret_gdn_w_out": ("ret_gdn_w_out", None, (2 * GROUP, D_MODEL), (N_SHARD, 2, GROUP // N_SHARD, D_MODEL), (1, 0, 2, 3)),
    "lru_w_in": ("lru_w_in", None, (D_MODEL, 2 * D_MODEL), (2, D_MODEL // 2, N_SHARD, 2 * D_MODEL // N_SHARD), (0, 2, 1, 3)),
    "lru_w_out": ("lru_w_out", None, (D_MODEL, D_MODEL), (N_SHARD, 2, D_MODEL // (2 * N_SHARD), D_MODEL), (1, 0, 2, 3)),
    "ffn_w_up_0": ("ffn_w_up", 0, (D_MODEL, 2 * D_FF), (2, D_MODEL // 2, N_SHARD, 2 * D_FF // N_SHARD), (0, 2, 1, 3)),
    "ffn_w_up_1": ("ffn_w_up", 1, (D_MODEL, 2 * D_FF), (2, D_MODEL // 2, N_SHARD, 2 * D_FF // N_SHARD), (0, 2, 1, 3)),
    "ffn_w_down": ("ffn_w_down", None, (2, D_FF, D_MODEL), (2, N_SHARD, D_FF // N_SHARD, D_MODEL), (0, 1, 2, 3)),
}
GATHER_GROUPS = (("ret_gdn_w_in",), ("ret_gdn_w_out", "ffn_w_up_0", "ffn_w_down"), ("lru_w_in", "lru_w_out", "ffn_w_up_1"))
REDUCE_GROUPS = (("ffn_w_up_1",), ("lru_w_in", "lru_w_out"), ("ffn_w_up_0", "ffn_w_down"), ("ret_gdn_w_out", "ret_gdn_w_in"))
GATHER_COLLECTIVE_ID = 1
REDUCE_COLLECTIVE_ID = GATHER_COLLECTIVE_ID + len(GATHER_GROUPS)


def _local_view(name, a):
    if name in SQUEEZE:
        return a[0]
    if a.ndim == 1:
        return a[None, :]
    return a


def kernel(x, norm_mix, norm_ffn, ret_gdn_w_in, gdn_conv_w, gdn_a_log, gdn_dt_bias, gdn_out_gain, ret_gdn_w_out, lru_w_in, lru_conv_w, lru_conv_b, lru_w_a, lru_b_a, lru_w_x, lru_b_x, lru_lambda, lru_w_out, ffn_w_up, ffn_conv_w, ffn_conv_b, ffn_w_down, norm_final, loss_target, m_norm_mix, m_norm_ffn, m_ret_gdn_w_in, m_gdn_conv_w, m_gdn_a_log, m_gdn_dt_bias, m_gdn_out_gain, m_ret_gdn_w_out, m_lru_w_in, m_lru_conv_w, m_lru_conv_b, m_lru_w_a, m_lru_b_a, m_lru_w_x, m_lru_b_x, m_lru_lambda, m_lru_w_out, m_ffn_w_up, m_ffn_conv_w, m_ffn_conv_b, m_ffn_w_down, m_norm_final, v_norm_mix, v_norm_ffn, v_ret_gdn_w_in, v_gdn_conv_w, v_gdn_a_log, v_gdn_dt_bias, v_gdn_out_gain, v_ret_gdn_w_out, v_lru_w_in, v_lru_conv_w, v_lru_conv_b, v_lru_w_a, v_lru_b_a, v_lru_w_x, v_lru_b_x, v_lru_lambda, v_lru_w_out, v_ffn_w_up, v_ffn_conv_w, v_ffn_conv_b, v_ffn_w_down, v_norm_final):
    given = dict(norm_mix=norm_mix, norm_ffn=norm_ffn, ret_gdn_w_in=ret_gdn_w_in, gdn_conv_w=gdn_conv_w, gdn_a_log=gdn_a_log, gdn_dt_bias=gdn_dt_bias, gdn_out_gain=gdn_out_gain, ret_gdn_w_out=ret_gdn_w_out, lru_w_in=lru_w_in, lru_conv_w=lru_conv_w, lru_conv_b=lru_conv_b, lru_w_a=lru_w_a, lru_b_a=lru_b_a, lru_w_x=lru_w_x, lru_b_x=lru_b_x, lru_lambda=lru_lambda, lru_w_out=lru_w_out, ffn_w_up=ffn_w_up, ffn_conv_w=ffn_conv_w, ffn_conv_b=ffn_conv_b, ffn_w_down=ffn_w_down, norm_final=norm_final)
    mom1 = dict(norm_mix=m_norm_mix, norm_ffn=m_norm_ffn, ret_gdn_w_in=m_ret_gdn_w_in, gdn_conv_w=m_gdn_conv_w, gdn_a_log=m_gdn_a_log, gdn_dt_bias=m_gdn_dt_bias, gdn_out_gain=m_gdn_out_gain, ret_gdn_w_out=m_ret_gdn_w_out, lru_w_in=m_lru_w_in, lru_conv_w=m_lru_conv_w, lru_conv_b=m_lru_conv_b, lru_w_a=m_lru_w_a, lru_b_a=m_lru_b_a, lru_w_x=m_lru_w_x, lru_b_x=m_lru_b_x, lru_lambda=m_lru_lambda, lru_w_out=m_lru_w_out, ffn_w_up=m_ffn_w_up, ffn_conv_w=m_ffn_conv_w, ffn_conv_b=m_ffn_conv_b, ffn_w_down=m_ffn_w_down, norm_final=m_norm_final)
    mom2 = dict(norm_mix=v_norm_mix, norm_ffn=v_norm_ffn, ret_gdn_w_in=v_ret_gdn_w_in, gdn_conv_w=v_gdn_conv_w, gdn_a_log=v_gdn_a_log, gdn_dt_bias=v_gdn_dt_bias, gdn_out_gain=v_gdn_out_gain, ret_gdn_w_out=v_ret_gdn_w_out, lru_w_in=v_lru_w_in, lru_conv_w=v_lru_conv_w, lru_conv_b=v_lru_conv_b, lru_w_a=v_lru_w_a, lru_b_a=v_lru_b_a, lru_w_x=v_lru_w_x, lru_b_x=v_lru_b_x, lru_lambda=v_lru_lambda, lru_w_out=v_lru_w_out, ffn_w_up=v_ffn_w_up, ffn_conv_w=v_ffn_conv_w, ffn_conv_b=v_ffn_conv_b, ffn_w_down=v_ffn_w_down, norm_final=v_norm_final)

    local = {n: _local_view(n, a) for n, a in given.items()}

    core = lax.axis_index("c")
    chip = 2 * lax.axis_index("x") + lax.axis_index("y")
    is_my_chip = lax.broadcasted_iota(jnp.int32, (N_SHARD, 1, 1), 0) == chip

    def by_core(mine, other):
        return jnp.where(core == 0, jnp.stack([mine, other]), jnp.stack([other, mine]))

    vec_names, rp_names = list(VECTOR_SHARDED), list(REPLICATED)
    arrays_full = {}
    for gi, group in enumerate(GATHER_GROUPS):
        halves = []
        for a in group:
            weight, layer, _, split, perm = BIG_ARRAYS[a]
            shard = local[weight] if layer is None else local[weight][layer]
            halves.append(shard.astype(BF16).reshape((2,) + tuple(split[p] for p in perm)[2:]))
        lands, sibs = gather_halves(halves, name=f"gather_weights_{gi}", collective_id=GATHER_COLLECTIVE_ID + gi)
        for a, mine, land, sib in zip(group, halves, lands, sibs):
            _, _, full_shape, split, perm = BIG_ARRAYS[a]
            half_mine = jnp.where(is_my_chip, jnp.where(core == 0, mine[0], mine[1])[None], land)
            half_other = jnp.where(is_my_chip, jnp.where(core == 0, mine[1], mine[0])[None], sib)
            arrays_full[a] = by_core(half_mine, half_other).transpose(perm).reshape(full_shape)
    full = {n: arrays_full[n] for n in ("ret_gdn_w_in", "ret_gdn_w_out", "lru_w_in", "lru_w_out", "ffn_w_down")}
    full["ffn_w_up"] = (arrays_full["ffn_w_up_0"], arrays_full["ffn_w_up_1"])
    full.update(zip(vec_names, all_gather_shards([local[n] for n in vec_names], [SHARDED[n] for n in vec_names], F32, 32, "p")))
    for n in rp_names:
        full[n] = local[n]

    loss_part, dx, grads = local_step(x[0], loss_target[0], full)
    loss = lax.psum(loss_part[0, 0], ("x", "y", "c"))

    small_names = rp_names + vec_names
    small_shapes = [grads[n].shape for n in small_names]
    small_rows = _pack_rows(sum(int(np.prod(s)) for s in small_shapes), 16)
    small = _pack([grads[n] for n in small_names], small_rows, F32).reshape(2, 1, small_rows // 2, LANES)
    reduced = {}
    for gi, group in enumerate(REDUCE_GROUPS):
        arrays = [grads[a].reshape(BIG_ARRAYS[a][3]).transpose(BIG_ARRAYS[a][4]) for a in group]
        scatter = [True] * len(group)
        if gi == len(REDUCE_GROUPS) - 1:
            arrays.append(small)
            scatter.append(False)
        g_own, g_sib = reduce_over_devices(arrays, scatter, tag=str(gi), collective_id=REDUCE_COLLECTIVE_ID + gi)
        reduced.update(zip(list(group) + ["small"], zip(g_own, g_sib)))

    result = {}
    for n in MATMUL_SHARDED:
        done = None
        for a in (k for k, spec in BIG_ARRAYS.items() if spec[0] == n):
            r, cols = reduced[a][0].shape
            layer = BIG_ARRAYS[a][1] or 0
            w3, m3, v3 = (t if BIG_ARRAYS[a][1] is not None else t.reshape(1, 2 * r, cols) for t in (given[n], mom1[n], mom2[n]))
            done = adamw_halves(w3, m3, v3, *reduced[a], layer=layer, prev=done, name=f"adamw_{a}")
        result[n] = done

    g_small = dict(zip(small_names, _unpack(by_core(*reduced["small"]).reshape(small_rows, LANES), small_shapes)))
    for n in vec_names:
        size = local[n].shape[SHARDED[n]]
        g_small[n] = lax.dynamic_slice_in_dim(g_small[n], chip * size, size, axis=SHARDED[n])
    loc_shapes = [local[n].shape for n in small_names]
    loc_rows = _pack_rows(sum(int(np.prod(s)) for s in loc_shapes), 256)
    packs = [_pack([src[n] for n in small_names], loc_rows, F32) for src in (given, g_small, mom1, mom2)]
    d_s, m_s, v_s = adamw(*packs, name="adamw_small")
    for n, d, nm, nv in zip(small_names, _unpack(d_s, loc_shapes), _unpack(m_s, loc_shapes), _unpack(v_s, loc_shapes)):
        result[n] = (g_small[n], d, nm, nv)

    outs = [[result[n][k].reshape(given[n].shape) for n in WEIGHTS] for k in range(4)]
    return (loss, dx[None], *outs[0], *outs[1], *outs[2], *outs[3])
```

```python
import functools

import numpy as np
import jax
import jax.numpy as jnp
from jax import lax
from jax.experimental import pallas as pl
from jax.experimental.pallas import tpu as pltpu
from jax.experimental.pallas import tpu_sc as plsc

F32 = jnp.float32
BF16 = jnp.bfloat16
HI = lax.Precision.HIGHEST
MESH = pl.DeviceIdType.MESH

SEQ = 2048
D_MODEL = 1024
N_HEADS = 4
HEAD = 128
RET_CHUNK = 128
GDN_CHUNK = 64
GROUP = N_HEADS * HEAD
MIX_MAIN = 8 * GROUP
D_FF = 2816
LRU_BLOCKS = 8
LRU_C = 8.0
ROPE_BASE = 10000.0
EPS = 1e-6
N_SHARD = 4
LANES = 128

ADAM_LR, ADAM_B1, ADAM_B2, ADAM_EPS, ADAM_WD, ADAM_STEP = 0.001, 0.9, 0.999, 1e-08, 0.01, 10

VMEM_LIMIT_BYTES = 56 * 1024 * 1024

_roll = pltpu.roll


def _params(**kw):
    return pltpu.CompilerParams(vmem_limit_bytes=VMEM_LIMIT_BYTES, **kw)


def _sds(shape, dtype):
    return jax.ShapeDtypeStruct(tuple(shape), dtype)


def _shift_raw(x, d):
    n = x.shape[0]
    t = lax.broadcasted_iota(jnp.int32, x.shape, 0)
    if d > 0:
        return jnp.where(t >= d, _roll(x, d, 0), 0.0)
    return jnp.where(t < n + d, _roll(x, n + d, 0), 0.0)


@functools.partial(jax.custom_vjp, nondiff_argnums=(1,))
def shift_rows(x, d):
    return _shift_raw(x, d)


def _shift_fwd(x, d):
    return _shift_raw(x, d), None


def _shift_bwd(d, _, g):
    return (_shift_raw(g, -d),)


shift_rows.defvjp(_shift_fwd, _shift_bwd)


@jax.custom_vjp
def swap_halves(x):
    return _roll(x, HEAD // 2, 1)


def _swap_fwd(x):
    return _roll(x, HEAD // 2, 1), None


def _swap_bwd(_, g):
    return (_roll(g, HEAD // 2, 1),)


swap_halves.defvjp(_swap_fwd, _swap_bwd)


def _scan_raw(a, u, reverse):
    n = a.shape[0]
    t = lax.broadcasted_iota(jnp.int32, a.shape, 0)
    d = 1
    while d < n:
        if reverse:
            m = t < n - d
            a_s, u_s = _roll(a, n - d, 0), _roll(u, n - d, 0)
        else:
            m = t >= d
            a_s, u_s = _roll(a, d, 0), _roll(u, d, 0)
        u = a * jnp.where(m, u_s, 0.0) + u
        a = a * jnp.where(m, a_s, 1.0)
        d *= 2
    return u


@jax.custom_vjp
def lin_scan(a, u):
    return _scan_raw(a, u, False)


def _lin_scan_fwd(a, u):
    hs = _scan_raw(a, u, False)
    return hs, (a, hs)


def _lin_scan_bwd(res, g):
    a, hs = res
    lam = _scan_raw(_shift_raw(a, -1), g, True)
    return lam * _shift_raw(hs, 1), lam


lin_scan.defvjp(_lin_scan_fwd, _lin_scan_bwd)


def _bdot(a, b, dims=(((1,), (0,)), ((), ()))):
    return lax.dot_general(a.astype(BF16), b.astype(BF16), dims, preferred_element_type=F32)


def _each(f, *seqs):
    return tuple(f(*a) for a in zip(*seqs))


def _split_bf16(a):
    hi = a.astype(BF16)
    return hi, (a - hi.astype(F32)).astype(BF16)


def _dot3_raw(a_s, b_s):
    a_hl = _each(_split_bf16, a_s)
    b_hl = _each(_split_bf16, b_s)
    hh = _each(lambda a, b: _bdot(a[0], b[0]), a_hl, b_hl)
    hl = _each(lambda a, b: _bdot(a[0], b[1]), a_hl, b_hl)
    lh = _each(lambda a, b: _bdot(a[1], b[0]), a_hl, b_hl)
    return _each(lambda x, y, z: x + (y + z), hh, hl, lh)


@jax.custom_vjp
def dot3(a_s, b_s):
    return _dot3_raw(a_s, b_s)


def _dot3_fwd(a_s, b_s):
    return _dot3_raw(a_s, b_s), (a_s, b_s)


def _dot3_bwd(res, g_s):
    a_s, b_s = res
    return (_each(lambda g, b: _bdot(g, b, (((1,), (1,)), ((), ()))), g_s, b_s),
            _each(lambda a, g: _bdot(a, g, (((0,), (0,)), ((), ()))), a_s, g_s))


dot3.defvjp(_dot3_fwd, _dot3_bwd)


def _eye(n):
    i = lax.broadcasted_iota(jnp.int32, (n, n), 0)
    j = lax.broadcasted_iota(jnp.int32, (n, n), 1)
    return (i == j).astype(F32)


def _unit_lower_inverse_raw(lmats):
    n = lmats[0].shape[0]
    eye = _eye(n)
    ps = _each(lambda l: -l, lmats)
    invs = _each(lambda x: eye + x, ps)
    k = 1
    while 2 * k < n:
        ps = _each(lambda p: _bdot(p, p), ps)
        invs = _each(lambda inv, p: inv + _bdot(inv, p), invs, ps)
        k *= 2
    prods = _dot3_raw(lmats, invs)
    resids = _each(lambda inv, pr: eye - inv - pr, invs, prods)
    return _each(lambda inv, r: inv + _bdot(inv, r), invs, resids)


@jax.custom_vjp
def unit_lower_inverse(lmats):
    return _unit_lower_inverse_raw(lmats)


def _uli_fwd(lmats):
    invs = _unit_lower_inverse_raw(lmats)
    return invs, invs


def _uli_bwd(invs, g_s):
    ms = _each(lambda inv, g: _bdot(inv, g, (((0,), (0,)), ((), ()))), invs, g_s)
    return (_each(lambda m, inv: -_bdot(m, inv, (((1,), (1,)), ((), ()))), ms, invs),)


unit_lower_inverse.defvjp(_uli_fwd, _uli_bwd)


def _cumsum_raw(x, reverse):
    n = x.shape[0]
    t = lax.broadcasted_iota(jnp.int32, x.shape, 0)
    d = 1
    while d < n:
        if reverse:
            x = x + jnp.where(t < n - d, _roll(x, n - d, 0), 0.0)
        else:
            x = x + jnp.where(t >= d, _roll(x, d, 0), 0.0)
        d *= 2
    return x


@jax.custom_vjp
def cumsum_rows(x):
    return _cumsum_raw(x, False)


def _cumsum_fwd(x):
    return _cumsum_raw(x, False), None


def _cumsum_bwd(_, g):
    return (_cumsum_raw(g, True),)


cumsum_rows.defvjp(_cumsum_fwd, _cumsum_bwd)


_NT = (((1,), (1,)), ((), ()))
_TN = (((0,), (0,)), ((), ()))


def _softplus(x):
    return jnp.maximum(x, 0.0) + jnp.log1p(jnp.exp(-jnp.abs(x)))


def _expm1_nonpos(x):
    poly = x * (1.0 + x * (0.5 + x * (1.0 / 6 + x * (1.0 / 24 + x * (1.0 / 120 + x * (1.0 / 720))))))
    return jnp.where(x > -0.25, poly, jnp.exp(x) - 1.0)


def _rms(x):
    return x * lax.rsqrt(jnp.mean(x * x, axis=-1, keepdims=True) + EPS)


def _causal_conv(x, w, width):
    y = w[width - 1:width, :] * x
    for j in range(width - 1):
        y = y + w[j:j + 1, :] * shift_rows(x, width - 1 - j)
    return y


def _norm_fn(x, g):
    return _rms(x) * g


def _ffn_act_fn(ug, uv, wg, wv, bg, bv):
    return jax.nn.silu(_causal_conv(ug, wg, 3) + bg) * (_causal_conv(uv, wv, 3) + bv)


def _gdn_conv_fn(x, w):
    return jax.nn.silu(_causal_conv(x, w, 4))


def _lru_fn(gate, x, cw, cb, wa, ba, wx, bx, lam):
    xr = _causal_conv(x, cw, 4) + cb
    r = jax.nn.sigmoid(_bdot(xr, wa) + ba)
    i = jax.nn.sigmoid(_bdot(xr, wx) + bx)
    log_a = -LRU_C * r * _softplus(-lam)
    a = jnp.exp(log_a)
    u = jnp.sqrt(-_expm1_nonpos(2.0 * log_a)) * (i * xr)
    hs = lin_scan(a, u)
    return jax.nn.gelu(gate) * hs


def _ret_fn(qs, ks, vs, gates, states, cos2, sin2, dmasks, ktails, qdecs, cdecs):
    qrs = _each(lambda q: q * cos2 + swap_halves(q) * sin2, qs)
    krs = _each(lambda k: (k * cos2 + swap_halves(k) * sin2) * (HEAD ** -0.5), ks)
    scores = _each(lambda q, k, m: _bdot(q, k, _NT) * m, qrs, krs, dmasks)
    inter = _each(lambda q, d, s: _bdot(q * d, s), qrs, qdecs, states)
    os_ = _each(lambda sc, v, x: _bdot(sc, v) + x, scores, vs, inter)
    new_states = _each(lambda s, cd, k, kt, v: s * cd + _bdot(k * kt, v, _TN), states, cdecs, krs, ktails, vs)
    ys = _each(lambda o, g: _rms(o) * jax.nn.silu(g), os_, gates)
    return ys, new_states


def _pick_lane(x, lane_idx):
    lane = lax.broadcasted_iota(jnp.int32, x.shape, 1)
    return jnp.sum(jnp.where(lane == lane_idx, x, 0.0), axis=1, keepdims=True)


def _l2norm(x):
    return x * lax.rsqrt(jnp.sum(x * x, axis=-1, keepdims=True) + EPS)


def _gdn_fn(qcs, kcs, vcs, gates, small, a_log, dt_bias, gain, states):
    c = GDN_CHUNK
    heads = tuple(range(len(qcs)))
    qs = _each(lambda x: _l2norm(x) * (HEAD ** -0.5), qcs)
    ks = _each(_l2norm, kcs)
    betas = _each(lambda h: jax.nn.sigmoid(_pick_lane(small, h)), heads)
    gs = _each(lambda h: -jnp.exp(_pick_lane(a_log, h)) * _softplus(_pick_lane(small, h + N_HEADS) + _pick_lane(dt_bias, h)), heads)
    i = lax.broadcasted_iota(jnp.int32, (c, c), 0)
    j = lax.broadcasted_iota(jnp.int32, (c, c), 1)
    tril = i >= j
    gcs = _each(lambda g: cumsum_rows(jnp.broadcast_to(g, (c, LANES)))[:, :1], gs)
    gc_rows = _each(lambda gc: jnp.broadcast_to(gc, (c, c)), gcs)
    decays = _each(lambda r: jnp.where(tril, jnp.exp(jnp.where(tril, r - r.T, 0.0)), 0.0), gc_rows)
    kbs = _each(lambda k, b: k * b, ks, betas)
    lmats = _each(lambda kb, k, d: jnp.where(i > j, _bdot(kb, k, _NT) * d, 0.0), kbs, ks, decays)
    attns = _each(lambda q, k, d: jnp.where(tril, _bdot(q, k, _NT) * d, 0.0), qs, ks, decays)
    invs = unit_lower_inverse(lmats)
    us = dot3(invs, _each(lambda v, b: v * b, vcs, betas))
    ws = dot3(invs, _each(lambda kb, gc: kb * jnp.exp(gc), kbs, gcs))
    g_lasts = _each(lambda g: jnp.sum(g, axis=0, keepdims=True), gs)
    v_news = _each(lambda u, w, s: u - _bdot(w, s), us, ws, states)
    inter = _each(lambda q, gc, s: _bdot(q * jnp.exp(gc), s), qs, gcs, states)
    os_ = _each(lambda x, a, v: x + _bdot(a, v), inter, attns, v_news)
    new_states = _each(lambda s, gl, k, gc, v: s * jnp.exp(gl) + _bdot(k * jnp.exp(gl - gc), v, _TN), states, g_lasts, ks, gcs, v_news)
    ys = _each(lambda o, gate: _rms(o) * gain * jax.nn.silu(gate), os_, gates)
    return ys, new_states


def _final_fn(h, g, target):
    y = _rms(h) * g
    return 0.5 * jnp.sum(jnp.mean(jnp.square(y - target), axis=-1, keepdims=True), axis=0, keepdims=True)


def _tile(n, candidates):
    for t in candidates:
        if n % t == 0:
            return t
    raise ValueError(f"no tile for {n}")


def matmul(a, b, *, ta=False, tb=False, add=None, out_dtype=F32, tm=None, tn=None, name):
    m = a.shape[1] if ta else a.shape[0]
    k = a.shape[0] if ta else a.shape[1]
    n = b.shape[0] if tb else b.shape[1]
    assert k == (b.shape[1] if tb else b.shape[0])
    tm = tm or _tile(m, (1024, 512, 1408, 256, 128))
    tn = tn or _tile(n, (512, 1408, 256, 128))
    dims = (((0 if ta else 1,), (1 if tb else 0,)), ((), ()))

    def body(*refs):
        if add is None:
            a_ref, b_ref, o_ref = refs
        else:
            a_ref, b_ref, r_ref, o_ref = refs
        acc = lax.dot_general(a_ref[...].astype(BF16), b_ref[...].astype(BF16), dims, preferred_element_type=F32)
        if add is not None:
            acc = acc + r_ref[...]
        o_ref[...] = acc.astype(out_dtype)

    a_spec = pl.BlockSpec((k, tm), lambda i, j: (0, i)) if ta else pl.BlockSpec((tm, k), lambda i, j: (i, 0))
    b_spec = pl.BlockSpec((tn, k), lambda i, j: (j, 0)) if tb else pl.BlockSpec((k, tn), lambda i, j: (0, j))
    o_spec = pl.BlockSpec((tm, tn), lambda i, j: (i, j))
    in_specs, args = [a_spec, b_spec], [a, b]
    if add is not None:
        in_specs.append(o_spec)
        args.append(add)
    return pl.pallas_call(body, out_shape=_sds((m, n), out_dtype), grid=(m // tm, n // tn), in_specs=in_specs,
                          out_specs=o_spec, compiler_params=_params(), name=name)(*args)


ROW_TILE = 256


def norm_fwd(x, g, *, name):
    t, d = x.shape

    def body(x_ref, g_ref, o_ref):
        o_ref[...] = _norm_fn(x_ref[...], g_ref[...]).astype(BF16)

    return pl.pallas_call(body, out_shape=_sds((t, d), BF16), grid=(t // ROW_TILE,),
                          in_specs=[pl.BlockSpec((ROW_TILE, d), lambda i: (i, 0)), pl.BlockSpec((1, d), lambda i: (0, 0))],
                          out_specs=pl.BlockSpec((ROW_TILE, d), lambda i: (i, 0)), compiler_params=_params(), name=name)(x, g)


def norm_bwd(x, g, dy, dres, *, name):
    t, d = x.shape

    def body(x_ref, g_ref, dy_ref, dres_ref, dx_ref, dg_ref):
        _, vjp = jax.vjp(_norm_fn, x_ref[...], g_ref[...])
        dx, dg = vjp(dy_ref[...])
        dx_ref[...] = dx + dres_ref[...]

        @pl.when(pl.program_id(0) == 0)
        def _():
            dg_ref[...] = jnp.zeros_like(dg_ref)

        dg_ref[...] += dg

    row = pl.BlockSpec((ROW_TILE, d), lambda i: (i, 0))
    vec = pl.BlockSpec((1, d), lambda i: (0, 0))
    return pl.pallas_call(body, out_shape=(_sds((t, d), F32), _sds((1, d), F32)), grid=(t // ROW_TILE,),
                          in_specs=[row, vec, row, row], out_specs=(row, vec), compiler_params=_params(), name=name)(x, g, dy, dres)


def final_fwd_bwd(h, g, target, *, name):
    t, d = h.shape

    def body(h_ref, g_ref, t_ref, loss_ref, dh_ref, dg_ref):
        tgt = t_ref[...]
        loss, vjp = jax.vjp(lambda hh, gg: _final_fn(hh, gg, tgt), h_ref[...], g_ref[...])
        dh, dg = vjp(jnp.ones((1, 1), F32))
        dh_ref[...] = dh

        @pl.when(pl.program_id(0) == 0)
        def _():
            dg_ref[...] = jnp.zeros_like(dg_ref)
            loss_ref[...] = jnp.zeros_like(loss_ref)

        dg_ref[...] += dg
        loss_ref[...] += jnp.broadcast_to(loss, loss_ref.shape)

    row = pl.BlockSpec((ROW_TILE, d), lambda i: (i, 0))
    vec = pl.BlockSpec((1, d), lambda i: (0, 0))
    return pl.pallas_call(body, out_shape=(_sds((1, LANES), F32), _sds((t, d), F32), _sds((1, d), F32)), grid=(t // ROW_TILE,),
                          in_specs=[row, vec, row], out_specs=(pl.BlockSpec((1, LANES), lambda i: (0, 0)), row, vec),
                          compiler_params=_params(), name=name)(h, g, target)


FFN_FWD_COLS = 256
FFN_BWD_COLS = 128


def ffn_act_fwd(u, cw, cb, *, name):
    t = u.shape[0]
    w = FFN_FWD_COLS
    nb = D_FF // w

    def body(ug_ref, uv_ref, wg_ref, wv_ref, bg_ref, bv_ref, o_ref):
        o_ref[...] = _ffn_act_fn(ug_ref[...], uv_ref[...], wg_ref[...], wv_ref[...], bg_ref[...], bv_ref[...]).astype(BF16)

    def col(rows, off):
        return pl.BlockSpec((rows, w), lambda j: (0, j + off))

    return pl.pallas_call(body, out_shape=_sds((t, D_FF), BF16), grid=(nb,),
                          in_specs=[col(t, 0), col(t, nb), col(3, 0), col(3, nb), col(1, 0), col(1, nb)],
                          out_specs=col(t, 0), compiler_params=_params(), name=name)(u, u, cw, cw, cb, cb)


def ffn_act_bwd(u, cw, cb, da, *, name):
    t = u.shape[0]
    w = FFN_BWD_COLS
    nb = D_FF // w

    def body(ug_ref, uv_ref, wg_ref, wv_ref, bg_ref, bv_ref, da_ref, dug_ref, duv_ref, dwg_ref, dwv_ref, dbg_ref, dbv_ref):
        _, vjp = jax.vjp(_ffn_act_fn, ug_ref[...], uv_ref[...], wg_ref[...], wv_ref[...], bg_ref[...], bv_ref[...])
        dug, duv, dwg, dwv, dbg, dbv = vjp(da_ref[...])
        dug_ref[...] = dug.astype(BF16)
        duv_ref[...] = duv.astype(BF16)
        dwg_ref[...] = dwg
        dwv_ref[...] = dwv
        dbg_ref[...] = dbg
        dbv_ref[...] = dbv

    def col(rows, off):
        return pl.BlockSpec((rows, w), lambda j: (0, j + off))

    outs = pl.pallas_call(
        body, out_shape=(_sds((t, D_FF), BF16), _sds((t, D_FF), BF16), _sds((3, D_FF), F32), _sds((3, D_FF), F32),
                         _sds((1, D_FF), F32), _sds((1, D_FF), F32)),
        grid=(nb,), in_specs=[col(t, 0), col(t, nb), col(3, 0), col(3, nb), col(1, 0), col(1, nb), col(t, 0)],
        out_specs=(col(t, 0), col(t, 0), col(3, 0), col(3, 0), col(1, 0), col(1, 0)), compiler_params=_params(), name=name,
    )(u, u, cw, cw, cb, cb, da)
    dug, duv, dwg, dwv, dbg, dbv = outs
    return dug, duv, jnp.concatenate([dwg, dwv], axis=1), jnp.concatenate([dbg, dbv], axis=1)


GDN_CONV_COLS = 256
GDN_CONV_OFF = 4 * GROUP


def gdn_conv_fwd(p, cw, *, name):
    t = p.shape[0]
    w = GDN_CONV_COLS
    nb = 3 * GROUP // w
    off = GDN_CONV_OFF // w

    def body(x_ref, w_ref, o_ref):
        o_ref[...] = _gdn_conv_fn(x_ref[...], w_ref[...])

    return pl.pallas_call(body, out_shape=_sds((t, 3 * GROUP), F32), grid=(nb,),
                          in_specs=[pl.BlockSpec((t, w), lambda j: (0, j + off)), pl.BlockSpec((4, w), lambda j: (0, j))],
                          out_specs=pl.BlockSpec((t, w), lambda j: (0, j)), compiler_params=_params(), name=name)(p, cw)


def gdn_conv_bwd(p, cw, dc, *, name):
    t = p.shape[0]
    w = GDN_CONV_COLS
    nb = 3 * GROUP // w
    off = GDN_CONV_OFF // w

    def body(x_ref, w_ref, dc_ref, dx_ref, dw_ref):
        _, vjp = jax.vjp(_gdn_conv_fn, x_ref[...], w_ref[...])
        dx, dw = vjp(dc_ref[...])
        dx_ref[...] = dx.astype(BF16)
        dw_ref[...] = dw

    blk = pl.BlockSpec((t, w), lambda j: (0, j))
    wblk = pl.BlockSpec((4, w), lambda j: (0, j))
    return pl.pallas_call(body, out_shape=(_sds((t, 3 * GROUP), BF16), _sds((4, 3 * GROUP), F32)), grid=(nb,),
                          in_specs=[pl.BlockSpec((t, w), lambda j: (0, j + off)), wblk, blk], out_specs=(blk, wblk),
                          compiler_params=_params(), name=name)(p, cw, dc)


def _lru_specs(t):
    w = D_MODEL // LRU_BLOCKS
    gate = pl.BlockSpec((t, w), lambda j: (0, j))
    xin = pl.BlockSpec((t, w), lambda j: (0, j + LRU_BLOCKS))
    cw = pl.BlockSpec((4, w), lambda j: (0, j))
    vec = pl.BlockSpec((1, w), lambda j: (0, j))
    mat = pl.BlockSpec((None, w, w), lambda j: (j, 0, 0))
    return gate, xin, cw, vec, mat


def lru_fwd(gx, cw, cb, wa, ba, wx, bx, lam, *, name):
    t = gx.shape[0]
    gate, xin, cws, vec, mat = _lru_specs(t)

    def body(g_ref, x_ref, cw_ref, cb_ref, wa_ref, ba_ref, wx_ref, bx_ref, lam_ref, o_ref):
        o_ref[...] = _lru_fn(g_ref[...], x_ref[...], cw_ref[...], cb_ref[...], wa_ref[...], ba_ref[...], wx_ref[...],
                             bx_ref[...], lam_ref[...]).astype(BF16)

    return pl.pallas_call(body, out_shape=_sds((t, D_MODEL), BF16), grid=(LRU_BLOCKS,),
                          in_specs=[gate, xin, cws, vec, mat, vec, mat, vec, vec], out_specs=gate,
                          compiler_params=_params(), name=name)(gx, gx, cw, cb, wa, ba, wx, bx, lam)


def lru_bwd(gx, cw, cb, wa, ba, wx, bx, lam, dy, *, name):
    t = gx.shape[0]
    gate, xin, cws, vec, mat = _lru_specs(t)

    def body(g_ref, x_ref, cw_ref, cb_ref, wa_ref, ba_ref, wx_ref, bx_ref, lam_ref, dy_ref,
             dg_ref, dx_ref, dcw_ref, dcb_ref, dwa_ref, dba_ref, dwx_ref, dbx_ref, dlam_ref):
        _, vjp = jax.vjp(_lru_fn, g_ref[...], x_ref[...], cw_ref[...], cb_ref[...], wa_ref[...], ba_ref[...], wx_ref[...],
                         bx_ref[...], lam_ref[...])
        dg, dx, dcw, dcb, dwa, dba, dwx, dbx, dlam = vjp(dy_ref[...])
        dg_ref[...] = dg.astype(BF16)
        dx_ref[...] = dx.astype(BF16)
        dcw_ref[...] = dcw
        dcb_ref[...] = dcb
        dwa_ref[...] = dwa
        dba_ref[...] = dba
        dwx_ref[...] = dwx
        dbx_ref[...] = dbx
        dlam_ref[...] = dlam

    d = D_MODEL
    w = d // LRU_BLOCKS
    out_shape = (_sds((t, d), BF16), _sds((t, d), BF16), _sds((4, d), F32), _sds((1, d), F32), _sds((LRU_BLOCKS, w, w), F32),
                 _sds((1, d), F32), _sds((LRU_BLOCKS, w, w), F32), _sds((1, d), F32), _sds((1, d), F32))
    return pl.pallas_call(body, out_shape=out_shape, grid=(LRU_BLOCKS,),
                          in_specs=[gate, xin, cws, vec, mat, vec, mat, vec, vec, gate],
                          out_specs=(gate, gate, cws, vec, mat, vec, mat, vec, vec), compiler_params=_params(), name=name,
                          )(gx, gx, cw, cb, wa, ba, wx, bx, lam, dy)


def _ret_tables():
    half = HEAD // 2
    inv_freq = (np.float32(ROPE_BASE) ** (-np.arange(half, dtype=np.float32) / np.float32(half))).astype(np.float32)
    ang = (np.arange(SEQ, dtype=np.float32)[:, None] * inv_freq[None, :]).astype(np.float64)
    cos2 = np.concatenate([np.cos(ang), np.cos(ang)], axis=1).astype(np.float32)
    sin2 = np.concatenate([-np.sin(ang), np.sin(ang)], axis=1).astype(np.float32)
    c = RET_CHUNK
    log_gamma = np.log1p(-np.exp2(-5.0 - np.arange(N_HEADS, dtype=np.float64)))
    idx = np.arange(c, dtype=np.float64)
    rel = idx[:, None] - idx[None, :]
    dmask = np.where(rel >= 0, np.exp(log_gamma[:, None, None] * np.maximum(rel, 0.0)), 0.0)
    ones = np.ones((N_HEADS, c, HEAD))
    ktail = np.exp(log_gamma[:, None] * (c - 1 - idx))[:, :, None] * ones
    qdec = np.exp(log_gamma[:, None] * (idx + 1.0))[:, :, None] * ones
    cdec = np.exp(log_gamma * c)[:, None, None] * ones
    return tuple(jnp.asarray(a, F32) for a in (cos2, sin2, dmask, ktail, qdec, cdec))


def _ret_specs(rev):
    c = RET_CHUNK
    nc = SEQ // c

    def n_of(n):
        return nc - 1 - n if rev else n

    def group(off):
        return pl.BlockSpec((c, GROUP), lambda n: (n_of(n), off))

    tab = pl.BlockSpec((c, HEAD), lambda n: (n_of(n), 0))
    const = pl.BlockSpec((N_HEADS, c, HEAD), lambda n: (0, 0, 0))
    state = pl.BlockSpec((N_HEADS, None, HEAD, HEAD), lambda n: (0, n_of(n), 0, 0))
    return group, tab, const, state, nc


def _head(h):
    return slice(h * HEAD, (h + 1) * HEAD)


def ret_fwd(p, tables, *, name):
    group, tab, const, state, nc = _ret_specs(False)

    def body(q_ref, k_ref, v_ref, g_ref, cos_ref, sin_ref, dm_ref, kt_ref, qd_ref, cd_ref, y_ref, st_ref, s_scr):
        @pl.when(pl.program_id(0) == 0)
        def _():
            s_scr[...] = jnp.zeros_like(s_scr)

        heads = range(N_HEADS)
        states = tuple(s_scr[h] for h in heads)
        ys, new_states = _ret_fn(*(tuple(r[:, _head(h)] for h in heads) for r in (q_ref, k_ref, v_ref, g_ref)), states,
                                 cos_ref[...], sin_ref[...], *(tuple(r[h] for h in heads) for r in (dm_ref, kt_ref, qd_ref, cd_ref)))
        for h in heads:
            st_ref[h] = states[h]
            y_ref[:, _head(h)] = ys[h].astype(BF16)
            s_scr[h] = new_states[h]

    return pl.pallas_call(
        body, out_shape=(_sds((SEQ, GROUP), BF16), _sds((N_HEADS, nc, HEAD, HEAD), F32)), grid=(nc,),
        in_specs=[group(0), group(1), group(2), group(3), tab, tab, const, const, const, const],
        out_specs=(group(0), state), scratch_shapes=[pltpu.VMEM((N_HEADS, HEAD, HEAD), F32)], compiler_params=_params(), name=name,
    )(p, p, p, p, *tables)


def ret_bwd(p, tables, states, dy, *, name):
    group, tab, const, state, nc = _ret_specs(True)

    def body(q_ref, k_ref, v_ref, g_ref, cos_ref, sin_ref, dm_ref, kt_ref, qd_ref, cd_ref, st_ref, dy_ref,
             dq_ref, dk_ref, dv_ref, dg_ref, ds_scr):
        @pl.when(pl.program_id(0) == 0)
        def _():
            ds_scr[...] = jnp.zeros_like(ds_scr)

        heads = range(N_HEADS)
        consts = (cos_ref[...], sin_ref[...], *(tuple(r[h] for h in heads) for r in (dm_ref, kt_ref, qd_ref, cd_ref)))
        _, vjp = jax.vjp(lambda *a: _ret_fn(*a, *consts), *(tuple(r[:, _head(h)] for h in heads) for r in (q_ref, k_ref, v_ref, g_ref)),
                         tuple(st_ref[h] for h in heads))
        dqs, dks, dvs, dgs, dss = vjp((tuple(dy_ref[:, _head(h)] for h in heads), tuple(ds_scr[h] for h in heads)))
        for h in heads:
            dq_ref[:, _head(h)] = dqs[h].astype(BF16)
            dk_ref[:, _head(h)] = dks[h].astype(BF16)
            dv_ref[:, _head(h)] = dvs[h].astype(BF16)
            dg_ref[:, _head(h)] = dgs[h].astype(BF16)
            ds_scr[h] = dss[h]

    out = _sds((SEQ, GROUP), BF16)
    return pl.pallas_call(
        body, out_shape=(out, out, out, out), grid=(nc,),
        in_specs=[group(0), group(1), group(2), group(3), tab, tab, const, const, const, const, state, group(0)],
        out_specs=(group(0), group(0), group(0), group(0)), scratch_shapes=[pltpu.VMEM((N_HEADS, HEAD, HEAD), F32)],
        compiler_params=_params(), name=name,
    )(p, p, p, p, *tables, states, dy)


def _gdn_specs(rev):
    c = GDN_CHUNK
    nc = SEQ // c

    def n_of(n):
        return nc - 1 - n if rev else n

    def group(off):
        return pl.BlockSpec((c, GROUP), lambda n: (n_of(n), off))

    small = pl.BlockSpec((c, LANES), lambda n: (n_of(n), 0))
    vec = pl.BlockSpec((1, LANES), lambda n: (0, 0))
    state = pl.BlockSpec((N_HEADS, None, HEAD, HEAD), lambda n: (0, n_of(n), 0, 0))
    return group, small, vec, state, nc


GDN_GATE_GROUP = 7


def gdn_fwd(conv, p, small, a_log, dt_bias, gain, *, name):
    group, sm, vec, state, nc = _gdn_specs(False)

    def body(q_ref, k_ref, v_ref, g_ref, sm_ref, al_ref, dt_ref, gn_ref, y_ref, st_ref, s_scr):
        @pl.when(pl.program_id(0) == 0)
        def _():
            s_scr[...] = jnp.zeros_like(s_scr)

        states = tuple(s_scr[h] for h in range(N_HEADS))
        ys, new_states = _gdn_fn(*(tuple(r[:, _head(h)] for h in range(N_HEADS)) for r in (q_ref, k_ref, v_ref, g_ref)),
                                 sm_ref[...], al_ref[...], dt_ref[...], gn_ref[...], states)
        for h in range(N_HEADS):
            st_ref[h] = states[h]
            y_ref[:, _head(h)] = ys[h].astype(BF16)
            s_scr[h] = new_states[h]

    return pl.pallas_call(
        body, out_shape=(_sds((SEQ, GROUP), BF16), _sds((N_HEADS, nc, HEAD, HEAD), F32)), grid=(nc,),
        in_specs=[group(0), group(1), group(2), group(GDN_GATE_GROUP), sm, vec, vec, vec], out_specs=(group(0), state),
        scratch_shapes=[pltpu.VMEM((N_HEADS, HEAD, HEAD), F32)], compiler_params=_params(), name=name,
    )(conv, conv, conv, p, small, a_log, dt_bias, gain)


def gdn_bwd(conv, p, small, a_log, dt_bias, gain, states, dy, *, name):
    group, sm, vec, state, nc = _gdn_specs(True)

    def body(q_ref, k_ref, v_ref, g_ref, sm_ref, al_ref, dt_ref, gn_ref, st_ref, dy_ref,
             dq_ref, dk_ref, dv_ref, dg_ref, dsm_ref, dal_ref, ddt_ref, dgn_ref, ds_scr):
        @pl.when(pl.program_id(0) == 0)
        def _():
            ds_scr[...] = jnp.zeros_like(ds_scr)
            dal_ref[...] = jnp.zeros_like(dal_ref)
            ddt_ref[...] = jnp.zeros_like(ddt_ref)
            dgn_ref[...] = jnp.zeros_like(dgn_ref)

        per_head = tuple(tuple(r[:, _head(h)] for h in range(N_HEADS)) for r in (q_ref, k_ref, v_ref, g_ref))
        _, vjp = jax.vjp(_gdn_fn, *per_head, sm_ref[...], al_ref[...], dt_ref[...], gn_ref[...],
                         tuple(st_ref[h] for h in range(N_HEADS)))
        cts = (tuple(dy_ref[:, _head(h)] for h in range(N_HEADS)), tuple(ds_scr[h] for h in range(N_HEADS)))
        dqs, dks, dvs, dgs, dsm, dal, ddt, dgn, dss = vjp(cts)
        for h in range(N_HEADS):
            dq_ref[:, _head(h)] = dqs[h]
            dk_ref[:, _head(h)] = dks[h]
            dv_ref[:, _head(h)] = dvs[h]
            dg_ref[:, _head(h)] = dgs[h].astype(BF16)
            ds_scr[h] = dss[h]
        dsm_ref[...] = dsm
        dal_ref[...] += dal
        ddt_ref[...] += ddt
        dgn_ref[...] += dgn

    f = _sds((SEQ, GROUP), F32)
    pv = _sds((1, LANES), F32)
    return pl.pallas_call(
        body, out_shape=(f, f, f, _sds((SEQ, GROUP), BF16), _sds((SEQ, LANES), F32), pv, pv, pv), grid=(nc,),
        in_specs=[group(0), group(1), group(2), group(GDN_GATE_GROUP), sm, vec, vec, vec, state, group(1)],
        out_specs=(group(0), group(0), group(0), group(0), sm, vec, vec, vec), scratch_shapes=[pltpu.VMEM((N_HEADS, HEAD, HEAD), F32)],
        compiler_params=_params(), name=name,
    )(conv, conv, conv, p, small, a_log, dt_bias, gain, states, dy)


PACK_ROW_TILE = 1024


def adamw(w, g, m, v, *, name):
    r = w.shape[0]
    tr = _tile(r, (PACK_ROW_TILE, 256, 128, 64, 32, 16, 8))

    def body(w_ref, g_ref, m_ref, v_ref, d_ref, nm_ref, nv_ref):
        gg = g_ref[...]
        nm = ADAM_B1 * m_ref[...] + (1.0 - ADAM_B1) * gg
        nv = ADAM_B2 * v_ref[...] + (1.0 - ADAM_B2) * jnp.square(gg)
        m_hat = nm / (1.0 - ADAM_B1 ** ADAM_STEP)
        v_hat = nv / (1.0 - ADAM_B2 ** ADAM_STEP)
        d_ref[...] = -ADAM_LR * (m_hat / (jnp.sqrt(v_hat) + ADAM_EPS) + ADAM_WD * w_ref[...])
        nm_ref[...] = nm
        nv_ref[...] = nv

    blk = pl.BlockSpec((tr, LANES), lambda i: (i, 0))
    o = _sds((r, LANES), F32)
    return pl.pallas_call(body, out_shape=(o, o, o), grid=(r // tr,), in_specs=[blk] * 4, out_specs=(blk, blk, blk),
                          compiler_params=_params(), name=name)(w, g, m, v)


ELEMENTWISE_BLOCK_BYTES = 2 * 1024 * 1024


def _row_tile(r, c):
    best = None
    for tr in range(8, r + 1, 8):
        if r % tr == 0 and tr * c * 4 <= ELEMENTWISE_BLOCK_BYTES:
            best = tr
    if best is None:
        raise ValueError(f"no row tile for ({r}, {c})")
    return best


def _core_index():
    return lax.axis_index("c").astype(jnp.int32).reshape(1)


def _chip_index():
    return (2 * lax.axis_index("x") + lax.axis_index("y")).astype(jnp.int32).reshape(1)


def adamw_halves(w, m, v, g_own, g_sib, *, layer=0, prev=None, name):
    n_layers, rows, c = w.shape
    r = rows // 2
    tr = _row_tile(r, c)
    nb = r // tr

    def body(c_ref, w_ref, m_ref, v_ref, own_ref, sib_ref, *rest):
        g_ref, d_ref, nm_ref, nv_ref = rest[-4:]
        gg = jnp.where(pl.program_id(0) == c_ref[0], own_ref[...], sib_ref[...])
        nm = ADAM_B1 * m_ref[...] + (1.0 - ADAM_B1) * gg
        nv = ADAM_B2 * v_ref[...] + (1.0 - ADAM_B2) * jnp.square(gg)
        m_hat = nm / (1.0 - ADAM_B1 ** ADAM_STEP)
        v_hat = nv / (1.0 - ADAM_B2 ** ADAM_STEP)
        g_ref[...] = gg
        d_ref[...] = -ADAM_LR * (m_hat / (jnp.sqrt(v_hat) + ADAM_EPS) + ADAM_WD * w_ref[...])
        nm_ref[...] = nm
        nv_ref[...] = nv

    full = pl.BlockSpec((None, tr, c), lambda h, i, cr: (layer, h * nb + i, 0))
    half = pl.BlockSpec((tr, c), lambda h, i, cr: (i, 0))
    o = _sds((n_layers, rows, c), F32)
    prev = list(prev or ())
    gs = pltpu.PrefetchScalarGridSpec(num_scalar_prefetch=1, grid=(2, nb), in_specs=[full, full, full, half, half] + [_ANY] * len(prev),
                                      out_specs=(full, full, full, full))
    n_fixed = 6
    return pl.pallas_call(body, out_shape=(o, o, o, o), grid_spec=gs, compiler_params=_params(), name=name,
                          input_output_aliases={n_fixed + k: k for k in range(len(prev))})(
        _core_index(), w, m, v, g_own, g_sib, *prev)


def add_core_halves(g2, land, *, out_dtype, name):
    _, ns, r, cols = g2.shape
    tr = _row_tile(r, cols)

    def body(c_ref, a_ref, b_ref, o_ref):
        o_ref[...] = (a_ref[...] + b_ref[...]).astype(out_dtype)

    gs = pltpu.PrefetchScalarGridSpec(
        num_scalar_prefetch=1, grid=(ns, r // tr),
        in_specs=[pl.BlockSpec((None, None, tr, cols), lambda s, i, cr: (cr[0], s, i, 0)),
                  pl.BlockSpec((None, tr, cols), lambda s, i, cr: (s, i, 0))],
        out_specs=pl.BlockSpec((None, tr, cols), lambda s, i, cr: (s, i, 0)))
    return pl.pallas_call(body, out_shape=_sds((ns, r, cols), out_dtype), grid_spec=gs, compiler_params=_params(), name=name)(
        _core_index(), g2, land)


def sum_over_chips(own, land, *, scatter, name):
    _, r, cols = own.shape
    tr = _row_tile(r, cols)

    def body(mine_ref, own_ref, l0, l1, l2, l3, o_ref):
        mine = mine_ref[0]
        mine_val = own_ref[...]
        acc = None
        for s, l_ref in enumerate((l0, l1, l2, l3)):
            val = jnp.where(mine == s, mine_val, l_ref[...]).astype(F32)
            acc = val if acc is None else acc + val
        o_ref[...] = acc

    def slot(s):
        return pl.BlockSpec((None, tr, cols), lambda i, mr: (jnp.where(mr[0] == s, (s + 1) % N_SHARD, s), i, 0))

    own_spec = pl.BlockSpec((None, tr, cols), lambda i, mr: (mr[0] if scatter else 0, i, 0))
    gs = pltpu.PrefetchScalarGridSpec(num_scalar_prefetch=1, grid=(r // tr,), in_specs=[own_spec] + [slot(s) for s in range(N_SHARD)],
                                      out_specs=pl.BlockSpec((tr, cols), lambda i, mr: (i, 0)))
    return pl.pallas_call(body, out_shape=_sds((r, cols), F32), grid_spec=gs, compiler_params=_params(), name=name)(
        _chip_index(), own, land, land, land, land)


_ANY = pl.BlockSpec(memory_space=pl.ANY)


def xy_exchange(src, *, scatter, name):
    rh = src.shape[1]

    def body(src_ref, land_ref, send_sems, recv_sems, loc_sem):
        x, y, c = lax.axis_index("x"), lax.axis_index("y"), lax.axis_index("c")
        mine = 2 * x + y
        peers = [(1 - x, y), (x, 1 - y), (1 - x, 1 - y)]

        def piece(shard):
            return src_ref.at[shard] if scatter else src_ref.at[c]

        def copy(k, px, py, dst_slot):
            return pltpu.make_async_remote_copy(src_ref=piece(2 * px + py), dst_ref=land_ref.at[dst_slot], send_sem=send_sems.at[k],
                                                recv_sem=recv_sems.at[k], device_id=(px, py, c), device_id_type=MESH)

        keep = pltpu.make_async_copy(piece(mine), land_ref.at[mine], loc_sem)
        keep.start()
        sends = [copy(k, px, py, mine) for k, (px, py) in enumerate(peers)]
        for cp in sends:
            cp.start()
        for cp in sends:
            cp.wait_send()
        for k, (px, py) in enumerate(peers):
            copy(k, px, py, 2 * px + py).wait_recv()
        keep.wait()

    return pl.pallas_call(body, out_shape=_sds((N_SHARD, rh, LANES), src.dtype), in_specs=[_ANY], out_specs=_ANY,
                          scratch_shapes=[pltpu.SemaphoreType.DMA((3,)), pltpu.SemaphoreType.DMA((3,)), pltpu.SemaphoreType.DMA(())],
                          name=name)(src)


def core_exchange(src, *, send_other_half, name):
    def body(src_ref, out_ref, send_sem, recv_sem, loc_sem):
        x, y, c = lax.axis_index("x"), lax.axis_index("y"), lax.axis_index("c")
        if send_other_half:
            cp = pltpu.make_async_remote_copy(src_ref=src_ref.at[1 - c], dst_ref=out_ref, send_sem=send_sem, recv_sem=recv_sem,
                                              device_id=(x, y, 1 - c), device_id_type=MESH)
            cp.start()
            cp.wait_send()
            cp.wait_recv()
        else:
            keep = pltpu.make_async_copy(src_ref, out_ref.at[c], loc_sem)
            keep.start()
            cp = pltpu.make_async_remote_copy(src_ref=src_ref, dst_ref=out_ref.at[c], send_sem=send_sem, recv_sem=recv_sem,
                                              device_id=(x, y, 1 - c), device_id_type=MESH)
            cp.start()
            cp.wait_send()
            pltpu.make_async_remote_copy(src_ref=src_ref, dst_ref=out_ref.at[1 - c], send_sem=send_sem, recv_sem=recv_sem,
                                         device_id=(x, y, 1 - c), device_id_type=MESH).wait_recv()
            keep.wait()

    out_shape = _sds(src.shape[1:], src.dtype) if send_other_half else _sds((2,) + src.shape, src.dtype)
    return pl.pallas_call(body, out_shape=out_shape, in_specs=[_ANY], out_specs=_ANY,
                          scratch_shapes=[pltpu.SemaphoreType.DMA(()), pltpu.SemaphoreType.DMA(()), pltpu.SemaphoreType.DMA(())],
                          name=name)(src)


def _comm_call(body, ins, out_shapes, sem_counts, name):
    return pl.pallas_call(body, out_shape=tuple(out_shapes), in_specs=[_ANY] * len(ins), out_specs=tuple([_ANY] * len(out_shapes)),
                          scratch_shapes=[pltpu.SemaphoreType.DMA((k,)) for k in sem_counts], name=name)(*ins)


def _sequencer_call(body, ins, out_shapes, sem_counts, name, collective_id):
    return pl.kernel(body, out_type=list(out_shapes), mesh=plsc.ScalarSubcoreMesh(axis_name="sequencer", num_cores=1), name=name,
                     scratch_types=[pltpu.SemaphoreType.DMA((k,)) for k in sem_counts],
                     compiler_params=pltpu.CompilerParams(collective_id=collective_id))(*ins)


def _handshake(peers):
    barrier = pltpu.get_barrier_semaphore()
    for peer in peers:
        pl.semaphore_signal(barrier, inc=1, device_id=peer, device_id_type=MESH)
    pl.semaphore_wait(barrier, len(peers))


def _xy_peers(x, y):
    return [(1 - x, y), (x, 1 - y), (1 - x, 1 - y)]


def gather_halves(halves, *, name, collective_id):
    n = len(halves)

    def body(*refs):
        ins, lands, sibs = refs[:n], refs[n:2 * n], refs[2 * n:3 * n]
        ici_send, ici_recv, d2d_send, d2d_recv = refs[3 * n:]
        x, y, c = lax.axis_index("x"), lax.axis_index("y"), lax.axis_index("c")
        mine = 2 * x + y
        peers = _xy_peers(x, y)
        _handshake([(px, py, c) for px, py in peers] + [(x, y, 1 - c)])

        def ici(i, k, slot):
            px, py = peers[k]
            return pltpu.make_async_remote_copy(src_ref=ins[i].at[c], dst_ref=lands[i].at[slot], send_sem=ici_send.at[3 * i + k],
                                                recv_sem=ici_recv.at[3 * i + k], device_id=(px, py, c), device_id_type=MESH)

        def pass_on(i, k):
            px, py = peers[k]
            slot = 2 * px + py
            return pltpu.make_async_remote_copy(src_ref=lands[i].at[slot], dst_ref=sibs[i].at[slot], send_sem=d2d_send.at[3 * i + k],
                                                recv_sem=d2d_recv.at[3 * i + k], device_id=(x, y, 1 - c), device_id_type=MESH)

        sends = [ici(i, k, mine) for i in range(n) for k in range(3)]
        for cp in sends:
            cp.start()
        passed = []
        for i in range(n):
            for k in range(3):
                px, py = peers[k]
                ici(i, k, 2 * px + py).wait_recv()
                cp = pass_on(i, k)
                cp.start()
                passed.append(cp)
        for cp in passed:
            cp.wait_recv()
        for cp in sends + passed:
            cp.wait_send()

    outs = [_sds((N_SHARD,) + h.shape[1:], h.dtype) for h in halves]
    res = _sequencer_call(body, halves, outs + outs, [3 * n] * 4, name, collective_id)
    return res[:n], res[n:]


def send_other_half(arrays, *, name, collective_id):
    n = len(arrays)

    def body(*refs):
        ins, lands = refs[:n], refs[n:2 * n]
        send_sems, recv_sems = refs[2 * n:]
        x, y, c = lax.axis_index("x"), lax.axis_index("y"), lax.axis_index("c")
        _handshake([(x, y, 1 - c)])
        copies = [pltpu.make_async_remote_copy(src_ref=ins[i].at[1 - c], dst_ref=lands[i], send_sem=send_sems.at[i],
                                               recv_sem=recv_sems.at[i], device_id=(x, y, 1 - c), device_id_type=MESH) for i in range(n)]
        for cp in copies:
            cp.start()
        for cp in copies:
            cp.wait_recv()
        for cp in copies:
            cp.wait_send()

    return _sequencer_call(body, arrays, [_sds(a.shape[1:], a.dtype) for a in arrays], [n, n], name, collective_id)


def send_to_chips(arrays, scatter, *, name, collective_id):
    n = len(arrays)

    def body(*refs):
        ins, lands = refs[:n], refs[n:2 * n]
        send_sems, recv_sems = refs[2 * n:]
        x, y, c = lax.axis_index("x"), lax.axis_index("y"), lax.axis_index("c")
        mine = 2 * x + y
        peers = _xy_peers(x, y)
        _handshake([(px, py, c) for px, py in peers])

        def copy(i, k, dst_slot):
            px, py = peers[k]
            src = ins[i].at[2 * px + py] if scatter[i] else ins[i].at[0]
            return pltpu.make_async_remote_copy(src_ref=src, dst_ref=lands[i].at[dst_slot], send_sem=send_sems.at[3 * i + k],
                                                recv_sem=recv_sems.at[3 * i + k], device_id=(px, py, c), device_id_type=MESH)

        sends = [copy(i, k, mine) for i in range(n) for k in range(3)]
        for cp in sends:
            cp.start()
        for i in range(n):
            for k in range(3):
                px, py = peers[k]
                copy(i, k, 2 * px + py).wait_recv()
        for cp in sends:
            cp.wait_send()

    return _sequencer_call(body, arrays, [_sds((N_SHARD,) + a.shape[1:], a.dtype) for a in arrays], [3 * n, 3 * n], name, collective_id)


def swap_with_other_core(arrays, *, name, collective_id):
    n = len(arrays)

    def body(*refs):
        ins, lands = refs[:n], refs[n:2 * n]
        send_sems, recv_sems = refs[2 * n:]
        x, y, c = lax.axis_index("x"), lax.axis_index("y"), lax.axis_index("c")
        _handshake([(x, y, 1 - c)])
        copies = [pltpu.make_async_remote_copy(src_ref=ins[i], dst_ref=lands[i], send_sem=send_sems.at[i], recv_sem=recv_sems.at[i],
                                               device_id=(x, y, 1 - c), device_id_type=MESH) for i in range(n)]
        for cp in copies:
            cp.start()
        for cp in copies:
            cp.wait_recv()
        for cp in copies:
            cp.wait_send()

    return _sequencer_call(body, arrays, [_sds(a.shape, a.dtype) for a in arrays], [n, n], name, collective_id)


def _pack_rows(n_elems, row_multiple):
    rows = -(-n_elems // LANES)
    return -(-rows // row_multiple) * row_multiple


def _pack(arrays, rows, dtype):
    flat = jnp.concatenate([a.reshape(-1).astype(dtype) for a in arrays])
    return jnp.pad(flat, (0, rows * LANES - flat.shape[0])).reshape(rows, LANES)


def _unpack(packed, shapes):
    flat = packed.reshape(-1)
    out, off = [], 0
    for s in shapes:
        n = int(np.prod(s))
        out.append(flat[off:off + n].reshape(s))
        off += n
    return out


def all_gather_shards(shards, axes, dtype, row_multiple, tag):
    shapes = [s.shape for s in shards]
    rows = _pack_rows(sum(int(np.prod(s)) for s in shapes), row_multiple)
    packed = _pack(shards, rows, dtype).reshape(2, rows // 2, LANES)
    land = xy_exchange(packed, scatter=False, name=f"gather_xy_{tag}")
    both = core_exchange(land, send_other_half=False, name=f"gather_c_{tag}")
    per_shard = jnp.swapaxes(both, 0, 1).reshape(N_SHARD, rows, LANES)
    pieces = [_unpack(per_shard[s], shapes) for s in range(N_SHARD)]
    return [jnp.concatenate([pieces[s][i] for s in range(N_SHARD)], axis=ax) for i, ax in enumerate(axes)]


def reduce_over_devices(arrays, scatter, *, tag, collective_id):
    land = send_other_half(arrays, name=f"reduce_core_send_{tag}", collective_id=collective_id)
    chip = [add_core_halves(a, l, out_dtype=BF16 if sc else F32, name=f"reduce_core_add_{tag}_{i}")
            for i, (a, l, sc) in enumerate(zip(arrays, land, scatter))]
    land = send_to_chips(chip, scatter, name=f"reduce_chip_send_{tag}", collective_id=collective_id + 1)
    own = [sum_over_chips(ch, l, scatter=sc, name=f"reduce_chip_add_{tag}_{i}") for i, (ch, l, sc) in enumerate(zip(chip, land, scatter))]
    sib = swap_with_other_core(own, name=f"reduce_core_swap_{tag}", collective_id=collective_id + 2)
    return own, sib


def _ffn_layer_fwd(h, norm_g, w_up, cw, cb, w_down, tag):
    hn = norm_fwd(h, norm_g, name=f"ffn_norm_{tag}")
    u = matmul(hn, w_up, name=f"ffn_up_{tag}")
    act = ffn_act_fwd(u, cw, cb, name=f"ffn_act_{tag}")
    out = matmul(act, w_down, add=h, name=f"ffn_down_{tag}")
    return out, (h, hn, u, act)


def _ffn_layer_bwd(saved, dout, norm_g, w_up, cw, cb, w_down, tag):
    h, hn, u, act = saved
    dact = matmul(dout, w_down, tb=True, name=f"ffn_down_dx_{tag}")
    d_w_down = matmul(act, dout, ta=True, name=f"ffn_down_dw_{tag}")
    dug, duv, dcw, dcb = ffn_act_bwd(u, cw, cb, dact, name=f"ffn_act_bwd_{tag}")
    du = jnp.concatenate([dug, duv], axis=1)
    dhn = matmul(du, w_up, tb=True, name=f"ffn_up_dx_{tag}")
    d_w_up = matmul(hn, du, ta=True, name=f"ffn_up_dw_{tag}")
    dh, dg = norm_bwd(h, norm_g, dhn, dout, name=f"ffn_norm_bwd_{tag}")
    return dh, dg, d_w_up, dcw, dcb, d_w_down


def local_step(x, target, w, stage=lambda name, dep: dep):
    g = {}
    tables = _ret_tables()
    x = stage("start", x)
    w_in = w["ret_gdn_w_in"]
    w_main = w_in[:, :MIX_MAIN]
    w_small = jnp.pad(w_in[:, MIX_MAIN:], ((0, 0), (0, LANES - 2 * N_HEADS)))
    a_log = jnp.pad(w["gdn_a_log"], ((0, 0), (0, LANES - N_HEADS)))
    dt_bias = jnp.pad(w["gdn_dt_bias"], ((0, 0), (0, LANES - N_HEADS)))

    hn0 = stage("normed", norm_fwd(x, w["norm_mix"][0:1], name="mix0_norm"))
    p = matmul(hn0, w_main, name="mix0_in")
    small = matmul(hn0, w_small, name="mix0_in_small")
    y_ret, s_ret = ret_fwd(p, tables, name="ret_fwd")
    conv = gdn_conv_fwd(p, w["gdn_conv_w"], name="gdn_conv")
    y_gdn, s_gdn = gdn_fwd(conv, p, small, a_log, dt_bias, w["gdn_out_gain"], name="gdn_fwd")
    y0 = stage("mixed", jnp.concatenate([y_ret, y_gdn], axis=1))
    h1 = matmul(y0, w["ret_gdn_w_out"], add=x, name="mix0_out")
    h2, ffn0 = _ffn_layer_fwd(h1, w["norm_ffn"][0:1], w["ffn_w_up"][0], w["ffn_conv_w"][0], w["ffn_conv_b"][0:1], w["ffn_w_down"][0], "0")
    h2 = stage("layer0", h2)

    hn1 = norm_fwd(h2, w["norm_mix"][1:2], name="mix1_norm")
    gx = matmul(hn1, w["lru_w_in"], name="mix1_in")
    lru_p = (w["lru_conv_w"], w["lru_conv_b"], w["lru_w_a"], w["lru_b_a"], w["lru_w_x"], w["lru_b_x"], w["lru_lambda"])
    y1 = lru_fwd(gx, *lru_p, name="lru_fwd")
    h3 = matmul(y1, w["lru_w_out"], add=h2, name="mix1_out")
    h4, ffn1 = _ffn_layer_fwd(h3, w["norm_ffn"][1:2], w["ffn_w_up"][1], w["ffn_conv_w"][1], w["ffn_conv_b"][1:2], w["ffn_w_down"][1], "1")

    loss, dh4, g["norm_final"] = final_fwd_bwd(h4, w["norm_final"], target, name="final")

    dh3, dgf1, dwu1, dcw1, dcb1, dwd1 = _ffn_layer_bwd(ffn1, dh4, w["norm_ffn"][1:2], w["ffn_w_up"][1], w["ffn_conv_w"][1],
                                                     w["ffn_conv_b"][1:2], w["ffn_w_down"][1], "1")
    dy1 = matmul(dh3, w["lru_w_out"], tb=True, name="mix1_out_dx")
    g["lru_w_out"] = matmul(y1, dh3, ta=True, name="mix1_out_dw")
    dgate, dxr, g["lru_conv_w"], g["lru_conv_b"], g["lru_w_a"], g["lru_b_a"], g["lru_w_x"], g["lru_b_x"], g["lru_lambda"] = lru_bwd(
        gx, *lru_p, dy1, name="lru_bwd")
    dgx = jnp.concatenate([dgate, dxr], axis=1)
    dhn1 = matmul(dgx, w["lru_w_in"], tb=True, name="mix1_in_dx")
    g["lru_w_in"] = matmul(hn1, dgx, ta=True, name="mix1_in_dw")
    dh2, dgm1 = norm_bwd(h2, w["norm_mix"][1:2], dhn1, dh3, name="mix1_norm_bwd")

    dh1, dgf0, dwu0, dcw0, dcb0, dwd0 = _ffn_layer_bwd(ffn0, dh2, w["norm_ffn"][0:1], w["ffn_w_up"][0], w["ffn_conv_w"][0],
                                                     w["ffn_conv_b"][0:1], w["ffn_w_down"][0], "0")
    dy0 = matmul(dh1, w["ret_gdn_w_out"], tb=True, name="mix0_out_dx")
    g["ret_gdn_w_out"] = matmul(y0, dh1, ta=True, name="mix0_out_dw")
    dq_r, dk_r, dv_r, dg_r = ret_bwd(p, tables, s_ret, dy0, name="ret_bwd")
    dcq, dck, dcv, dg_d, dsmall, dal, ddt, dgain = gdn_bwd(conv, p, small, a_log, dt_bias, w["gdn_out_gain"], s_gdn, dy0, name="gdn_bwd")
    dconv = jnp.concatenate([dcq, dck, dcv], axis=1)
    dp_conv, g["gdn_conv_w"] = gdn_conv_bwd(p, w["gdn_conv_w"], dconv, name="gdn_conv_bwd")
    dp = jnp.concatenate([dq_r, dk_r, dv_r, dg_r, dp_conv, dg_d], axis=1)
    dhn0 = matmul(dp, w_main, tb=True, name="mix0_in_dx")
    dhn0 = matmul(dsmall, w_small, tb=True, add=dhn0, name="mix0_in_small_dx")
    d_w_main = matmul(hn0, dp, ta=True, name="mix0_in_dw")
    d_w_small = matmul(hn0, dsmall, ta=True, name="mix0_in_small_dw")
    g["ret_gdn_w_in"] = jnp.concatenate([d_w_main, d_w_small[:, :2 * N_HEADS]], axis=1)
    dx, dgm0 = norm_bwd(x, w["norm_mix"][0:1], dhn0, dh1, name="mix0_norm_bwd")

    g["gdn_a_log"] = dal[:, :N_HEADS]
    g["gdn_dt_bias"] = ddt[:, :N_HEADS]
    g["gdn_out_gain"] = dgain
    g["norm_mix"] = jnp.concatenate([dgm0, dgm1], axis=0)
    g["norm_ffn"] = jnp.concatenate([dgf0, dgf1], axis=0)
    g["ffn_w_up_0"], g["ffn_w_up_1"] = dwu0, dwu1
    g["ffn_conv_w"] = jnp.stack([dcw0, dcw1])
    g["ffn_conv_b"] = jnp.concatenate([dcb0, dcb1], axis=0)
    g["ffn_w_down"] = jnp.stack([dwd0, dwd1])
    return loss, dx, g


WEIGHTS = ("norm_mix", "norm_ffn", "ret_gdn_w_in", "gdn_conv_w", "gdn_a_log", "gdn_dt_bias", "gdn_out_gain", "ret_gdn_w_out",
           "lru_w_in", "lru_conv_w", "lru_conv_b", "lru_w_a", "lru_b_a", "lru_w_x", "lru_b_x", "lru_lambda", "lru_w_out",
           "ffn_w_up", "ffn_conv_w", "ffn_conv_b", "ffn_w_down", "norm_final")
MATMUL_SHARDED = {"ret_gdn_w_in": 1, "ret_gdn_w_out": 0, "lru_w_in": 1, "lru_w_out": 0, "ffn_w_up": 2, "ffn_w_down": 1}
VECTOR_SHARDED = {"gdn_conv_w": 1, "lru_conv_w": 1, "lru_conv_b": 1, "lru_b_a": 1, "lru_b_x": 1, "lru_lambda": 1, "ffn_conv_w": 2}
SHARDED = {**MATMUL_SHARDED, **VECTOR_SHARDED}
REPLICATED = tuple(n for n in WEIGHTS if n not in SHARDED)
SQUEEZE = {"ret_gdn_w_in", "gdn_conv_w", "ret_gdn_w_out", "lru_w_in", "lru_conv_w", "lru_w_a", "lru_w_x", "lru_w_out"}
MIX_IN = MIX_MAIN + 2 * N_HEADS
BIG_ARRAYS = {
    "ret_gdn_w_in": ("ret_gdn_w_in", None, (D_MODEL, MIX_IN), (2, D_MODEL // 2, N_SHARD, MIX_IN // N_SHARD), (0, 2, 1, 3)),
    "ret_gdn_w_out": ("ret_gdn_w_out", None, (2 * GROUP, D_MODEL), (N_SHARD, 2, GROUP // N_SHARD, D_MODEL), (1, 0, 2, 3)),
    "lru_w_in": ("lru_w_in", None, (D_MODEL, 2 * D_MODEL), (2, D_MODEL // 2, N_SHARD, 2 * D_MODEL // N_SHARD), (0, 2, 1, 3)),
    "lru_w_out": ("lru_w_out", None, (D_MODEL, D_MODEL), (N_SHARD, 2, D_MODEL // (2 * N_SHARD), D_MODEL), (1, 0, 2, 3)),
    "ffn_w_up_0": ("ffn_w_up", 0, (D_MODEL, 2 * D_FF), (2, D_MODEL // 2, N_SHARD, 2 * D_FF // N_SHARD), (0, 2, 1, 3)),
    "ffn_w_up_1": ("ffn_w_up", 1, (D_MODEL, 2 * D_FF), (2, D_MODEL // 2, N_SHARD, 2 * D_FF // N_SHARD), (0, 2, 1, 3)),
    "ffn_w_down": ("ffn_w_down", None, (2, D_FF, D_MODEL), (2, N_SHARD, D_FF // N_SHARD, D_MODEL), (0, 1, 2, 3)),
}
GATHER_GROUPS = (("ret_gdn_w_in",), ("ret_gdn_w_out", "ffn_w_up_0", "ffn_w_down"), ("lru_w_in", "lru_w_out", "ffn_w_up_1"))
REDUCE_GROUPS = (("ffn_w_up_1",), ("lru_w_in", "lru_w_out"), ("ffn_w_up_0", "ffn_w_down"), ("ret_gdn_w_out", "ret_gdn_w_in"))
GATHER_COLLECTIVE_ID = 1
REDUCE_COLLECTIVE_ID = GATHER_COLLECTIVE_ID + len(GATHER_GROUPS)


def _local_view(name, a):
    if name in SQUEEZE:
        return a[0]
    if a.ndim == 1:
        return a[None, :]
    return a


def kernel(x, norm_mix, norm_ffn, ret_gdn_w_in, gdn_conv_w, gdn_a_log, gdn_dt_bias, gdn_out_gain, ret_gdn_w_out, lru_w_in, lru_conv_w, lru_conv_b, lru_w_a, lru_b_a, lru_w_x, lru_b_x, lru_lambda, lru_w_out, ffn_w_up, ffn_conv_w, ffn_conv_b, ffn_w_down, norm_final, loss_target, m_norm_mix, m_norm_ffn, m_ret_gdn_w_in, m_gdn_conv_w, m_gdn_a_log, m_gdn_dt_bias, m_gdn_out_gain, m_ret_gdn_w_out, m_lru_w_in, m_lru_conv_w, m_lru_conv_b, m_lru_w_a, m_lru_b_a, m_lru_w_x, m_lru_b_x, m_lru_lambda, m_lru_w_out, m_ffn_w_up, m_ffn_conv_w, m_ffn_conv_b, m_ffn_w_down, m_norm_final, v_norm_mix, v_norm_ffn, v_ret_gdn_w_in, v_gdn_conv_w, v_gdn_a_log, v_gdn_dt_bias, v_gdn_out_gain, v_ret_gdn_w_out, v_lru_w_in, v_lru_conv_w, v_lru_conv_b, v_lru_w_a, v_lru_b_a, v_lru_w_x, v_lru_b_x, v_lru_lambda, v_lru_w_out, v_ffn_w_up, v_ffn_conv_w, v_ffn_conv_b, v_ffn_w_down, v_norm_final):
    given = dict(norm_mix=norm_mix, norm_ffn=norm_ffn, ret_gdn_w_in=ret_gdn_w_in, gdn_conv_w=gdn_conv_w, gdn_a_log=gdn_a_log, gdn_dt_bias=gdn_dt_bias, gdn_out_gain=gdn_out_gain, ret_gdn_w_out=ret_gdn_w_out, lru_w_in=lru_w_in, lru_conv_w=lru_conv_w, lru_conv_b=lru_conv_b, lru_w_a=lru_w_a, lru_b_a=lru_b_a, lru_w_x=lru_w_x, lru_b_x=lru_b_x, lru_lambda=lru_lambda, lru_w_out=lru_w_out, ffn_w_up=ffn_w_up, ffn_conv_w=ffn_conv_w, ffn_conv_b=ffn_conv_b, ffn_w_down=ffn_w_down, norm_final=norm_final)
    mom1 = dict(norm_mix=m_norm_mix, norm_ffn=m_norm_ffn, ret_gdn_w_in=m_ret_gdn_w_in, gdn_conv_w=m_gdn_conv_w, gdn_a_log=m_gdn_a_log, gdn_dt_bias=m_gdn_dt_bias, gdn_out_gain=m_gdn_out_gain, ret_gdn_w_out=m_ret_gdn_w_out, lru_w_in=m_lru_w_in, lru_conv_w=m_lru_conv_w, lru_conv_b=m_lru_conv_b, lru_w_a=m_lru_w_a, lru_b_a=m_lru_b_a, lru_w_x=m_lru_w_x, lru_b_x=m_lru_b_x, lru_lambda=m_lru_lambda, lru_w_out=m_lru_w_out, ffn_w_up=m_ffn_w_up, ffn_conv_w=m_ffn_conv_w, ffn_conv_b=m_ffn_conv_b, ffn_w_down=m_ffn_w_down, norm_final=m_norm_final)
    mom2 = dict(norm_mix=v_norm_mix, norm_ffn=v_norm_ffn, ret_gdn_w_in=v_ret_gdn_w_in, gdn_conv_w=v_gdn_conv_w, gdn_a_log=v_gdn_a_log, gdn_dt_bias=v_gdn_dt_bias, gdn_out_gain=v_gdn_out_gain, ret_gdn_w_out=v_ret_gdn_w_out, lru_w_in=v_lru_w_in, lru_conv_w=v_lru_conv_w, lru_conv_b=v_lru_conv_b, lru_w_a=v_lru_w_a, lru_b_a=v_lru_b_a, lru_w_x=v_lru_w_x, lru_b_x=v_lru_b_x, lru_lambda=v_lru_lambda, lru_w_out=v_lru_w_out, ffn_w_up=v_ffn_w_up, ffn_conv_w=v_ffn_conv_w, ffn_conv_b=v_ffn_conv_b, ffn_w_down=v_ffn_w_down, norm_final=v_norm_final)

    local = {n: _local_view(n, a) for n, a in given.items()}

    core = lax.axis_index("c")
    chip = 2 * lax.axis_index("x") + lax.axis_index("y")
    is_my_chip = lax.broadcasted_iota(jnp.int32, (N_SHARD, 1, 1), 0) == chip

    def by_core(mine, other):
        return jnp.where(core == 0, jnp.stack([mine, other]), jnp.stack([other, mine]))

    vec_names, rp_names = list(VECTOR_SHARDED), list(REPLICATED)
    full = dict(zip(vec_names, all_gather_shards([local[n] for n in vec_names], [SHARDED[n] for n in vec_names], F32, 32, "p")))
    for n in rp_names:
        full[n] = local[n]
    in_flight = {}

    def launch(gi, after=None):
        halves = []
        for a in GATHER_GROUPS[gi]:
            weight, layer, _, split, perm = BIG_ARRAYS[a]
            shard = local[weight] if layer is None else local[weight][layer]
            halves.append(shard.astype(BF16).reshape((2,) + tuple(split[p] for p in perm)[2:]))
        if after is not None:
            halves, after = lax.optimization_barrier((halves, after))
        in_flight[gi] = (halves,) + gather_halves(halves, name=f"gather_weights_{gi}", collective_id=GATHER_COLLECTIVE_ID + gi)
        return after

    def land(gi, after):
        halves, lands, sibs = in_flight[gi]
        (lands, sibs), after = lax.optimization_barrier(((lands, sibs), after))
        for a, mine, got, passed in zip(GATHER_GROUPS[gi], halves, lands, sibs):
            weight, layer, full_shape, split, perm = BIG_ARRAYS[a]
            half_mine = jnp.where(is_my_chip, jnp.where(core == 0, mine[0], mine[1])[None], got)
            half_other = jnp.where(is_my_chip, jnp.where(core == 0, mine[1], mine[0])[None], passed)
            value = by_core(half_mine, half_other).transpose(perm).reshape(full_shape)
            if layer is None:
                full[weight] = value
            else:
                full.setdefault(weight, [None, None])[layer] = value
        return after

    def stage(name, tensor):
        if name == "start":
            launch(0)
            launch(1)
            return land(0, tensor)
        if name == "normed":
            return launch(2, tensor)
        return land({"mixed": 1, "layer0": 2}[name], tensor)

    loss_part, dx, grads = local_step(x[0], loss_target[0], full, stage)
    loss = lax.psum(loss_part[0, 0], ("x", "y", "c"))

    small_names = rp_names + vec_names
    small_shapes = [grads[n].shape for n in small_names]
    small_rows = _pack_rows(sum(int(np.prod(s)) for s in small_shapes), 16)
    small = _pack([grads[n] for n in small_names], small_rows, F32).reshape(2, 1, small_rows // 2, LANES)
    reduced = {}
    for gi, group in enumerate(REDUCE_GROUPS):
        arrays = [grads[a].reshape(BIG_ARRAYS[a][3]).transpose(BIG_ARRAYS[a][4]) for a in group]
        scatter = [True] * len(group)
        if gi == len(REDUCE_GROUPS) - 1:
            arrays.append(small)
            scatter.append(False)
        g_own, g_sib = reduce_over_devices(arrays, scatter, tag=str(gi), collective_id=REDUCE_COLLECTIVE_ID + 3 * gi)
        reduced.update(zip(list(group) + ["small"], zip(g_own, g_sib)))

    result = {}
    for n in MATMUL_SHARDED:
        done = None
        for a in (k for k, spec in BIG_ARRAYS.items() if spec[0] == n):
            r, cols = reduced[a][0].shape
            layer = BIG_ARRAYS[a][1] or 0
            w3, m3, v3 = (t if BIG_ARRAYS[a][1] is not None else t.reshape(1, 2 * r, cols) for t in (given[n], mom1[n], mom2[n]))
            done = adamw_halves(w3, m3, v3, *reduced[a], layer=layer, prev=done, name=f"adamw_{a}")
        result[n] = done

    g_small = dict(zip(small_names, _unpack(by_core(*reduced["small"]).reshape(small_rows, LANES), small_shapes)))
    for n in vec_names:
        size = local[n].shape[SHARDED[n]]
        g_small[n] = lax.dynamic_slice_in_dim(g_small[n], chip * size, size, axis=SHARDED[n])
    loc_shapes = [local[n].shape for n in small_names]
    loc_rows = _pack_rows(sum(int(np.prod(s)) for s in loc_shapes), 256)
    packs = [_pack([src[n] for n in small_names], loc_rows, F32) for src in (given, g_small, mom1, mom2)]
    d_s, m_s, v_s = adamw(*packs, name="adamw_small")
    for n, d, nm, nv in zip(small_names, _unpack(d_s, loc_shapes), _unpack(m_s, loc_shapes), _unpack(v_s, loc_shapes)):
        result[n] = (g_small[n], d, nm, nv)

    outs = [[result[n][k].reshape(given[n].shape) for n in WEIGHTS] for k in range(4)]
    return (loss, dx[None], *outs[0], *outs[1], *outs[2], *outs[3])
```

```python
import functools

import numpy as np
import jax
import jax.numpy as jnp
from jax import lax
from jax.experimental import pallas as pl
from jax.experimental.pallas import tpu as pltpu
from jax.experimental.pallas import tpu_sc as plsc

F32 = jnp.float32
BF16 = jnp.bfloat16
HI = lax.Precision.HIGHEST
MESH = pl.DeviceIdType.MESH

SEQ = 2048
D_MODEL = 1024
N_HEADS = 4
HEAD = 128
RET_CHUNK = 128
GDN_CHUNK = 64
GROUP = N_HEADS * HEAD
MIX_MAIN = 8 * GROUP
D_FF = 2816
LRU_BLOCKS = 8
LRU_C = 8.0
ROPE_BASE = 10000.0
EPS = 1e-6
N_SHARD = 4
LANES = 128

ADAM_LR, ADAM_B1, ADAM_B2, ADAM_EPS, ADAM_WD, ADAM_STEP = 0.001, 0.9, 0.999, 1e-08, 0.01, 10

VMEM_LIMIT_BYTES = 56 * 1024 * 1024

_roll = pltpu.roll


def _params(**kw):
    return pltpu.CompilerParams(vmem_limit_bytes=VMEM_LIMIT_BYTES, **kw)


def _sds(shape, dtype):
    return jax.ShapeDtypeStruct(tuple(shape), dtype)


def _shift_raw(x, d):
    n = x.shape[0]
    t = lax.broadcasted_iota(jnp.int32, x.shape, 0)
    if d > 0:
        return jnp.where(t >= d, _roll(x, d, 0), 0.0)
    return jnp.where(t < n + d, _roll(x, n + d, 0), 0.0)


@functools.partial(jax.custom_vjp, nondiff_argnums=(1,))
def shift_rows(x, d):
    return _shift_raw(x, d)


def _shift_fwd(x, d):
    return _shift_raw(x, d), None


def _shift_bwd(d, _, g):
    return (_shift_raw(g, -d),)


shift_rows.defvjp(_shift_fwd, _shift_bwd)


@jax.custom_vjp
def swap_halves(x):
    return _roll(x, HEAD // 2, 1)


def _swap_fwd(x):
    return _roll(x, HEAD // 2, 1), None


def _swap_bwd(_, g):
    return (_roll(g, HEAD // 2, 1),)


swap_halves.defvjp(_swap_fwd, _swap_bwd)


def _scan_raw(a, u, reverse):
    n = a.shape[0]
    t = lax.broadcasted_iota(jnp.int32, a.shape, 0)
    d = 1
    while d < n:
        if reverse:
            m = t < n - d
            a_s, u_s = _roll(a, n - d, 0), _roll(u, n - d, 0)
        else:
            m = t >= d
            a_s, u_s = _roll(a, d, 0), _roll(u, d, 0)
        u = a * jnp.where(m, u_s, 0.0) + u
        a = a * jnp.where(m, a_s, 1.0)
        d *= 2
    return u


@jax.custom_vjp
def lin_scan(a, u):
    return _scan_raw(a, u, False)


def _lin_scan_fwd(a, u):
    hs = _scan_raw(a, u, False)
    return hs, (a, hs)


def _lin_scan_bwd(res, g):
    a, hs = res
    lam = _scan_raw(_shift_raw(a, -1), g, True)
    return lam * _shift_raw(hs, 1), lam


lin_scan.defvjp(_lin_scan_fwd, _lin_scan_bwd)


def _bdot(a, b, dims=(((1,), (0,)), ((), ()))):
    return lax.dot_general(a.astype(BF16), b.astype(BF16), dims, preferred_element_type=F32)


def _each(f, *seqs):
    return tuple(f(*a) for a in zip(*seqs))


def _split_bf16(a):
    hi = a.astype(BF16)
    return hi, (a - hi.astype(F32)).astype(BF16)


def _dot3_raw(a_s, b_s):
    a_hl = _each(_split_bf16, a_s)
    b_hl = _each(_split_bf16, b_s)
    hh = _each(lambda a, b: _bdot(a[0], b[0]), a_hl, b_hl)
    hl = _each(lambda a, b: _bdot(a[0], b[1]), a_hl, b_hl)
    lh = _each(lambda a, b: _bdot(a[1], b[0]), a_hl, b_hl)
    return _each(lambda x, y, z: x + (y + z), hh, hl, lh)


@jax.custom_vjp
def dot3(a_s, b_s):
    return _dot3_raw(a_s, b_s)


def _dot3_fwd(a_s, b_s):
    return _dot3_raw(a_s, b_s), (a_s, b_s)


def _dot3_bwd(res, g_s):
    a_s, b_s = res
    return (_each(lambda g, b: _bdot(g, b, (((1,), (1,)), ((), ()))), g_s, b_s),
            _each(lambda a, g: _bdot(a, g, (((0,), (0,)), ((), ()))), a_s, g_s))


dot3.defvjp(_dot3_fwd, _dot3_bwd)


def _eye(n):
    i = lax.broadcasted_iota(jnp.int32, (n, n), 0)
    j = lax.broadcasted_iota(jnp.int32, (n, n), 1)
    return (i == j).astype(F32)


def _unit_lower_inverse_raw(lmats):
    n = lmats[0].shape[0]
    eye = _eye(n)
    ps = _each(lambda l: -l, lmats)
    invs = _each(lambda x: eye + x, ps)
    k = 1
    while 2 * k < n:
        ps = _each(lambda p: _bdot(p, p), ps)
        invs = _each(lambda inv, p: inv + _bdot(inv, p), invs, ps)
        k *= 2
    prods = _dot3_raw(lmats, invs)
    resids = _each(lambda inv, pr: eye - inv - pr, invs, prods)
    return _each(lambda inv, r: inv + _bdot(inv, r), invs, resids)


@jax.custom_vjp
def unit_lower_inverse(lmats):
    return _unit_lower_inverse_raw(lmats)


def _uli_fwd(lmats):
    invs = _unit_lower_inverse_raw(lmats)
    return invs, invs


def _uli_bwd(invs, g_s):
    ms = _each(lambda inv, g: _bdot(inv, g, (((0,), (0,)), ((), ()))), invs, g_s)
    return (_each(lambda m, inv: -_bdot(m, inv, (((1,), (1,)), ((), ()))), ms, invs),)


unit_lower_inverse.defvjp(_uli_fwd, _uli_bwd)


def _cumsum_raw(x, reverse):
    n = x.shape[0]
    t = lax.broadcasted_iota(jnp.int32, x.shape, 0)
    d = 1
    while d < n:
        if reverse:
            x = x + jnp.where(t < n - d, _roll(x, n - d, 0), 0.0)
        else:
            x = x + jnp.where(t >= d, _roll(x, d, 0), 0.0)
        d *= 2
    return x


@jax.custom_vjp
def cumsum_rows(x):
    return _cumsum_raw(x, False)


def _cumsum_fwd(x):
    return _cumsum_raw(x, False), None


def _cumsum_bwd(_, g):
    return (_cumsum_raw(g, True),)


cumsum_rows.defvjp(_cumsum_fwd, _cumsum_bwd)


_NT = (((1,), (1,)), ((), ()))
_TN = (((0,), (0,)), ((), ()))


def _softplus(x):
    return jnp.maximum(x, 0.0) + jnp.log1p(jnp.exp(-jnp.abs(x)))


def _expm1_nonpos(x):
    poly = x * (1.0 + x * (0.5 + x * (1.0 / 6 + x * (1.0 / 24 + x * (1.0 / 120 + x * (1.0 / 720))))))
    return jnp.where(x > -0.25, poly, jnp.exp(x) - 1.0)


def _rms(x):
    return x * lax.rsqrt(jnp.mean(x * x, axis=-1, keepdims=True) + EPS)


def _causal_conv(x, w, width):
    y = w[width - 1:width, :] * x
    for j in range(width - 1):
        y = y + w[j:j + 1, :] * shift_rows(x, width - 1 - j)
    return y


def _norm_fn(x, g):
    return _rms(x) * g


def _ffn_act_fn(ug, uv, wg, wv, bg, bv):
    return jax.nn.silu(_causal_conv(ug, wg, 3) + bg) * (_causal_conv(uv, wv, 3) + bv)


def _gdn_conv_fn(x, w):
    return jax.nn.silu(_causal_conv(x, w, 4))


def _lru_fn(gate, x, cw, cb, wa, ba, wx, bx, lam):
    xr = _causal_conv(x, cw, 4) + cb
    r = jax.nn.sigmoid(_bdot(xr, wa) + ba)
    i = jax.nn.sigmoid(_bdot(xr, wx) + bx)
    log_a = -LRU_C * r * _softplus(-lam)
    a = jnp.exp(log_a)
    u = jnp.sqrt(-_expm1_nonpos(2.0 * log_a)) * (i * xr)
    hs = lin_scan(a, u)
    return jax.nn.gelu(gate) * hs


def _ret_fn(qs, ks, vs, gates, states, cos2, sin2, dmasks, ktails, qdecs, cdecs):
    qrs = _each(lambda q: q * cos2 + swap_halves(q) * sin2, qs)
    krs = _each(lambda k: (k * cos2 + swap_halves(k) * sin2) * (HEAD ** -0.5), ks)
    scores = _each(lambda q, k, m: _bdot(q, k, _NT) * m, qrs, krs, dmasks)
    inter = _each(lambda q, d, s: _bdot(q * d, s), qrs, qdecs, states)
    os_ = _each(lambda sc, v, x: _bdot(sc, v) + x, scores, vs, inter)
    new_states = _each(lambda s, cd, k, kt, v: s * cd + _bdot(k * kt, v, _TN), states, cdecs, krs, ktails, vs)
    ys = _each(lambda o, g: _rms(o) * jax.nn.silu(g), os_, gates)
    return ys, new_states


def _pick_lane(x, lane_idx):
    lane = lax.broadcasted_iota(jnp.int32, x.shape, 1)
    return jnp.sum(jnp.where(lane == lane_idx, x, 0.0), axis=1, keepdims=True)


def _l2norm(x):
    return x * lax.rsqrt(jnp.sum(x * x, axis=-1, keepdims=True) + EPS)


def _gdn_fn(qcs, kcs, vcs, gates, small, a_log, dt_bias, gain, states):
    c = GDN_CHUNK
    heads = tuple(range(len(qcs)))
    qs = _each(lambda x: _l2norm(x) * (HEAD ** -0.5), qcs)
    ks = _each(_l2norm, kcs)
    betas = _each(lambda h: jax.nn.sigmoid(_pick_lane(small, h)), heads)
    gs = _each(lambda h: -jnp.exp(_pick_lane(a_log, h)) * _softplus(_pick_lane(small, h + N_HEADS) + _pick_lane(dt_bias, h)), heads)
    i = lax.broadcasted_iota(jnp.int32, (c, c), 0)
    j = lax.broadcasted_iota(jnp.int32, (c, c), 1)
    tril = i >= j
    gcs = _each(lambda g: cumsum_rows(jnp.broadcast_to(g, (c, LANES)))[:, :1], gs)
    gc_rows = _each(lambda gc: jnp.broadcast_to(gc, (c, c)), gcs)
    decays = _each(lambda r: jnp.where(tril, jnp.exp(jnp.where(tril, r - r.T, 0.0)), 0.0), gc_rows)
    kbs = _each(lambda k, b: k * b, ks, betas)
    lmats = _each(lambda kb, k, d: jnp.where(i > j, _bdot(kb, k, _NT) * d, 0.0), kbs, ks, decays)
    attns = _each(lambda q, k, d: jnp.where(tril, _bdot(q, k, _NT) * d, 0.0), qs, ks, decays)
    invs = unit_lower_inverse(lmats)
    us = dot3(invs, _each(lambda v, b: v * b, vcs, betas))
    ws = dot3(invs, _each(lambda kb, gc: kb * jnp.exp(gc), kbs, gcs))
    g_lasts = _each(lambda g: jnp.sum(g, axis=0, keepdims=True), gs)
    v_news = _each(lambda u, w, s: u - _bdot(w, s), us, ws, states)
    inter = _each(lambda q, gc, s: _bdot(q * jnp.exp(gc), s), qs, gcs, states)
    os_ = _each(lambda x, a, v: x + _bdot(a, v), inter, attns, v_news)
    new_states = _each(lambda s, gl, k, gc, v: s * jnp.exp(gl) + _bdot(k * jnp.exp(gl - gc), v, _TN), states, g_lasts, ks, gcs, v_news)
    ys = _each(lambda o, gate: _rms(o) * gain * jax.nn.silu(gate), os_, gates)
    return ys, new_states


def _final_fn(h, g, target):
    y = _rms(h) * g
    return 0.5 * jnp.sum(jnp.mean(jnp.square(y - target), axis=-1, keepdims=True), axis=0, keepdims=True)


def _tile(n, candidates):
    for t in candidates:
        if n % t == 0:
            return t
    raise ValueError(f"no tile for {n}")


def matmul(a, b, *, ta=False, tb=False, add=None, out_dtype=F32, tm=None, tn=None, name):
    m = a.shape[1] if ta else a.shape[0]
    k = a.shape[0] if ta else a.shape[1]
    n = b.shape[0] if tb else b.shape[1]
    assert k == (b.shape[1] if tb else b.shape[0])
    tm = tm or _tile(m, (1024, 512, 1408, 256, 128))
    tn = tn or _tile(n, (512, 1408, 256, 128))
    dims = (((0 if ta else 1,), (1 if tb else 0,)), ((), ()))

    def body(*refs):
        if add is None:
            a_ref, b_ref, o_ref = refs
        else:
            a_ref, b_ref, r_ref, o_ref = refs
        acc = lax.dot_general(a_ref[...].astype(BF16), b_ref[...].astype(BF16), dims, preferred_element_type=F32)
        if add is not None:
            acc = acc + r_ref[...]
        o_ref[...] = acc.astype(out_dtype)

    a_spec = pl.BlockSpec((k, tm), lambda i, j: (0, i)) if ta else pl.BlockSpec((tm, k), lambda i, j: (i, 0))
    b_spec = pl.BlockSpec((tn, k), lambda i, j: (j, 0)) if tb else pl.BlockSpec((k, tn), lambda i, j: (0, j))
    o_spec = pl.BlockSpec((tm, tn), lambda i, j: (i, j))
    in_specs, args = [a_spec, b_spec], [a, b]
    if add is not None:
        in_specs.append(o_spec)
        args.append(add)
    return pl.pallas_call(body, out_shape=_sds((m, n), out_dtype), grid=(m // tm, n // tn), in_specs=in_specs,
                          out_specs=o_spec, compiler_params=_params(), name=name)(*args)


ROW_TILE = 256


def norm_fwd(x, g, *, name):
    t, d = x.shape

    def body(x_ref, g_ref, o_ref):
        o_ref[...] = _norm_fn(x_ref[...], g_ref[...]).astype(BF16)

    return pl.pallas_call(body, out_shape=_sds((t, d), BF16), grid=(t // ROW_TILE,),
                          in_specs=[pl.BlockSpec((ROW_TILE, d), lambda i: (i, 0)), pl.BlockSpec((1, d), lambda i: (0, 0))],
                          out_specs=pl.BlockSpec((ROW_TILE, d), lambda i: (i, 0)), compiler_params=_params(), name=name)(x, g)


def norm_bwd(x, g, dy, dres, *, name):
    t, d = x.shape

    def body(x_ref, g_ref, dy_ref, dres_ref, dx_ref, dg_ref):
        _, vjp = jax.vjp(_norm_fn, x_ref[...], g_ref[...])
        dx, dg = vjp(dy_ref[...])
        dx_ref[...] = dx + dres_ref[...]

        @pl.when(pl.program_id(0) == 0)
        def _():
            dg_ref[...] = jnp.zeros_like(dg_ref)

        dg_ref[...] += dg

    row = pl.BlockSpec((ROW_TILE, d), lambda i: (i, 0))
    vec = pl.BlockSpec((1, d), lambda i: (0, 0))
    return pl.pallas_call(body, out_shape=(_sds((t, d), F32), _sds((1, d), F32)), grid=(t // ROW_TILE,),
                          in_specs=[row, vec, row, row], out_specs=(row, vec), compiler_params=_params(), name=name)(x, g, dy, dres)


def final_fwd_bwd(h, g, target, *, name):
    t, d = h.shape

    def body(h_ref, g_ref, t_ref, loss_ref, dh_ref, dg_ref):
        tgt = t_ref[...]
        loss, vjp = jax.vjp(lambda hh, gg: _final_fn(hh, gg, tgt), h_ref[...], g_ref[...])
        dh, dg = vjp(jnp.ones((1, 1), F32))
        dh_ref[...] = dh

        @pl.when(pl.program_id(0) == 0)
        def _():
            dg_ref[...] = jnp.zeros_like(dg_ref)
            loss_ref[...] = jnp.zeros_like(loss_ref)

        dg_ref[...] += dg
        loss_ref[...] += jnp.broadcast_to(loss, loss_ref.shape)

    row = pl.BlockSpec((ROW_TILE, d), lambda i: (i, 0))
    vec = pl.BlockSpec((1, d), lambda i: (0, 0))
    return pl.pallas_call(body, out_shape=(_sds((1, LANES), F32), _sds((t, d), F32), _sds((1, d), F32)), grid=(t // ROW_TILE,),
                          in_specs=[row, vec, row], out_specs=(pl.BlockSpec((1, LANES), lambda i: (0, 0)), row, vec),
                          compiler_params=_params(), name=name)(h, g, target)


FFN_FWD_COLS = 256
FFN_BWD_COLS = 128


def ffn_act_fwd(u, cw, cb, *, name):
    t = u.shape[0]
    w = FFN_FWD_COLS
    nb = D_FF // w

    def body(ug_ref, uv_ref, wg_ref, wv_ref, bg_ref, bv_ref, o_ref):
        o_ref[...] = _ffn_act_fn(ug_ref[...], uv_ref[...], wg_ref[...], wv_ref[...], bg_ref[...], bv_ref[...]).astype(BF16)

    def col(rows, off):
        return pl.BlockSpec((rows, w), lambda j: (0, j + off))

    return pl.pallas_call(body, out_shape=_sds((t, D_FF), BF16), grid=(nb,),
                          in_specs=[col(t, 0), col(t, nb), col(3, 0), col(3, nb), col(1, 0), col(1, nb)],
                          out_specs=col(t, 0), compiler_params=_params(), name=name)(u, u, cw, cw, cb, cb)


def ffn_act_bwd(u, cw, cb, da, *, name):
    t = u.shape[0]
    w = FFN_BWD_COLS
    nb = D_FF // w

    def body(ug_ref, uv_ref, wg_ref, wv_ref, bg_ref, bv_ref, da_ref, dug_ref, duv_ref, dwg_ref, dwv_ref, dbg_ref, dbv_ref):
        _, vjp = jax.vjp(_ffn_act_fn, ug_ref[...], uv_ref[...], wg_ref[...], wv_ref[...], bg_ref[...], bv_ref[...])
        dug, duv, dwg, dwv, dbg, dbv = vjp(da_ref[...])
        dug_ref[...] = dug.astype(BF16)
        duv_ref[...] = duv.astype(BF16)
        dwg_ref[...] = dwg
        dwv_ref[...] = dwv
        dbg_ref[...] = dbg
        dbv_ref[...] = dbv

    def col(rows, off):
        return pl.BlockSpec((rows, w), lambda j: (0, j + off))

    outs = pl.pallas_call(
        body, out_shape=(_sds((t, D_FF), BF16), _sds((t, D_FF), BF16), _sds((3, D_FF), F32), _sds((3, D_FF), F32),
                         _sds((1, D_FF), F32), _sds((1, D_FF), F32)),
        grid=(nb,), in_specs=[col(t, 0), col(t, nb), col(3, 0), col(3, nb), col(1, 0), col(1, nb), col(t, 0)],
        out_specs=(col(t, 0), col(t, 0), col(3, 0), col(3, 0), col(1, 0), col(1, 0)), compiler_params=_params(), name=name,
    )(u, u, cw, cw, cb, cb, da)
    dug, duv, dwg, dwv, dbg, dbv = outs
    return dug, duv, jnp.concatenate([dwg, dwv], axis=1), jnp.concatenate([dbg, dbv], axis=1)


GDN_CONV_COLS = 256
GDN_CONV_OFF = 4 * GROUP


def gdn_conv_fwd(p, cw, *, name):
    t = p.shape[0]
    w = GDN_CONV_COLS
    nb = 3 * GROUP // w
    off = GDN_CONV_OFF // w

    def body(x_ref, w_ref, o_ref):
        o_ref[...] = _gdn_conv_fn(x_ref[...], w_ref[...])

    return pl.pallas_call(body, out_shape=_sds((t, 3 * GROUP), F32), grid=(nb,),
                          in_specs=[pl.BlockSpec((t, w), lambda j: (0, j + off)), pl.BlockSpec((4, w), lambda j: (0, j))],
                          out_specs=pl.BlockSpec((t, w), lambda j: (0, j)), compiler_params=_params(), name=name)(p, cw)


def gdn_conv_bwd(p, cw, dc, *, name):
    t = p.shape[0]
    w = GDN_CONV_COLS
    nb = 3 * GROUP // w
    off = GDN_CONV_OFF // w

    def body(x_ref, w_ref, dc_ref, dx_ref, dw_ref):
        _, vjp = jax.vjp(_gdn_conv_fn, x_ref[...], w_ref[...])
        dx, dw = vjp(dc_ref[...])
        dx_ref[...] = dx.astype(BF16)
        dw_ref[...] = dw

    blk = pl.BlockSpec((t, w), lambda j: (0, j))
    wblk = pl.BlockSpec((4, w), lambda j: (0, j))
    return pl.pallas_call(body, out_shape=(_sds((t, 3 * GROUP), BF16), _sds((4, 3 * GROUP), F32)), grid=(nb,),
                          in_specs=[pl.BlockSpec((t, w), lambda j: (0, j + off)), wblk, blk], out_specs=(blk, wblk),
                          compiler_params=_params(), name=name)(p, cw, dc)


def _lru_specs(t):
    w = D_MODEL // LRU_BLOCKS
    gate = pl.BlockSpec((t, w), lambda j: (0, j))
    xin = pl.BlockSpec((t, w), lambda j: (0, j + LRU_BLOCKS))
    cw = pl.BlockSpec((4, w), lambda j: (0, j))
    vec = pl.BlockSpec((1, w), lambda j: (0, j))
    mat = pl.BlockSpec((None, w, w), lambda j: (j, 0, 0))
    return gate, xin, cw, vec, mat


def lru_fwd(gx, cw, cb, wa, ba, wx, bx, lam, *, name):
    t = gx.shape[0]
    gate, xin, cws, vec, mat = _lru_specs(t)

    def body(g_ref, x_ref, cw_ref, cb_ref, wa_ref, ba_ref, wx_ref, bx_ref, lam_ref, o_ref):
        o_ref[...] = _lru_fn(g_ref[...], x_ref[...], cw_ref[...], cb_ref[...], wa_ref[...], ba_ref[...], wx_ref[...],
                             bx_ref[...], lam_ref[...]).astype(BF16)

    return pl.pallas_call(body, out_shape=_sds((t, D_MODEL), BF16), grid=(LRU_BLOCKS,),
                          in_specs=[gate, xin, cws, vec, mat, vec, mat, vec, vec], out_specs=gate,
                          compiler_params=_params(), name=name)(gx, gx, cw, cb, wa, ba, wx, bx, lam)


def lru_bwd(gx, cw, cb, wa, ba, wx, bx, lam, dy, *, name):
    t = gx.shape[0]
    gate, xin, cws, vec, mat = _lru_specs(t)

    def body(g_ref, x_ref, cw_ref, cb_ref, wa_ref, ba_ref, wx_ref, bx_ref, lam_ref, dy_ref,
             dg_ref, dx_ref, dcw_ref, dcb_ref, dwa_ref, dba_ref, dwx_ref, dbx_ref, dlam_ref):
        _, vjp = jax.vjp(_lru_fn, g_ref[...], x_ref[...], cw_ref[...], cb_ref[...], wa_ref[...], ba_ref[...], wx_ref[...],
                         bx_ref[...], lam_ref[...])
        dg, dx, dcw, dcb, dwa, dba, dwx, dbx, dlam = vjp(dy_ref[...])
        dg_ref[...] = dg.astype(BF16)
        dx_ref[...] = dx.astype(BF16)
        dcw_ref[...] = dcw
        dcb_ref[...] = dcb
        dwa_ref[...] = dwa
        dba_ref[...] = dba
        dwx_ref[...] = dwx
        dbx_ref[...] = dbx
        dlam_ref[...] = dlam

    d = D_MODEL
    w = d // LRU_BLOCKS
    out_shape = (_sds((t, d), BF16), _sds((t, d), BF16), _sds((4, d), F32), _sds((1, d), F32), _sds((LRU_BLOCKS, w, w), F32),
                 _sds((1, d), F32), _sds((LRU_BLOCKS, w, w), F32), _sds((1, d), F32), _sds((1, d), F32))
    return pl.pallas_call(body, out_shape=out_shape, grid=(LRU_BLOCKS,),
                          in_specs=[gate, xin, cws, vec, mat, vec, mat, vec, vec, gate],
                          out_specs=(gate, gate, cws, vec, mat, vec, mat, vec, vec), compiler_params=_params(), name=name,
                          )(gx, gx, cw, cb, wa, ba, wx, bx, lam, dy)


def _ret_tables():
    half = HEAD // 2
    inv_freq = (np.float32(ROPE_BASE) ** (-np.arange(half, dtype=np.float32) / np.float32(half))).astype(np.float32)
    ang = (np.arange(SEQ, dtype=np.float32)[:, None] * inv_freq[None, :]).astype(np.float64)
    cos2 = np.concatenate([np.cos(ang), np.cos(ang)], axis=1).astype(np.float32)
    sin2 = np.concatenate([-np.sin(ang), np.sin(ang)], axis=1).astype(np.float32)
    c = RET_CHUNK
    log_gamma = np.log1p(-np.exp2(-5.0 - np.arange(N_HEADS, dtype=np.float64)))
    idx = np.arange(c, dtype=np.float64)
    rel = idx[:, None] - idx[None, :]
    dmask = np.where(rel >= 0, np.exp(log_gamma[:, None, None] * np.maximum(rel, 0.0)), 0.0)
    ones = np.ones((N_HEADS, c, HEAD))
    ktail = np.exp(log_gamma[:, None] * (c - 1 - idx))[:, :, None] * ones
    qdec = np.exp(log_gamma[:, None] * (idx + 1.0))[:, :, None] * ones
    cdec = np.exp(log_gamma * c)[:, None, None] * ones
    return tuple(jnp.asarray(a, F32) for a in (cos2, sin2, dmask, ktail, qdec, cdec))


def _ret_specs(rev):
    c = RET_CHUNK
    nc = SEQ // c

    def n_of(n):
        return nc - 1 - n if rev else n

    def group(off):
        return pl.BlockSpec((c, GROUP), lambda n: (n_of(n), off))

    tab = pl.BlockSpec((c, HEAD), lambda n: (n_of(n), 0))
    const = pl.BlockSpec((N_HEADS, c, HEAD), lambda n: (0, 0, 0))
    state = pl.BlockSpec((N_HEADS, None, HEAD, HEAD), lambda n: (0, n_of(n), 0, 0))
    return group, tab, const, state, nc


def _head(h):
    return slice(h * HEAD, (h + 1) * HEAD)


def ret_fwd(p, tables, *, name):
    group, tab, const, state, nc = _ret_specs(False)

    def body(q_ref, k_ref, v_ref, g_ref, cos_ref, sin_ref, dm_ref, kt_ref, qd_ref, cd_ref, y_ref, st_ref, s_scr):
        @pl.when(pl.program_id(0) == 0)
        def _():
            s_scr[...] = jnp.zeros_like(s_scr)

        heads = range(N_HEADS)
        states = tuple(s_scr[h] for h in heads)
        ys, new_states = _ret_fn(*(tuple(r[:, _head(h)] for h in heads) for r in (q_ref, k_ref, v_ref, g_ref)), states,
                                 cos_ref[...], sin_ref[...], *(tuple(r[h] for h in heads) for r in (dm_ref, kt_ref, qd_ref, cd_ref)))
        for h in heads:
            st_ref[h] = states[h]
            y_ref[:, _head(h)] = ys[h].astype(BF16)
            s_scr[h] = new_states[h]

    return pl.pallas_call(
        body, out_shape=(_sds((SEQ, GROUP), BF16), _sds((N_HEADS, nc, HEAD, HEAD), F32)), grid=(nc,),
        in_specs=[group(0), group(1), group(2), group(3), tab, tab, const, const, const, const],
        out_specs=(group(0), state), scratch_shapes=[pltpu.VMEM((N_HEADS, HEAD, HEAD), F32)], compiler_params=_params(), name=name,
    )(p, p, p, p, *tables)


def ret_bwd(p, tables, states, dy, *, name):
    group, tab, const, state, nc = _ret_specs(True)

    def body(q_ref, k_ref, v_ref, g_ref, cos_ref, sin_ref, dm_ref, kt_ref, qd_ref, cd_ref, st_ref, dy_ref,
             dq_ref, dk_ref, dv_ref, dg_ref, ds_scr):
        @pl.when(pl.program_id(0) == 0)
        def _():
            ds_scr[...] = jnp.zeros_like(ds_scr)

        heads = range(N_HEADS)
        consts = (cos_ref[...], sin_ref[...], *(tuple(r[h] for h in heads) for r in (dm_ref, kt_ref, qd_ref, cd_ref)))
        _, vjp = jax.vjp(lambda *a: _ret_fn(*a, *consts), *(tuple(r[:, _head(h)] for h in heads) for r in (q_ref, k_ref, v_ref, g_ref)),
                         tuple(st_ref[h] for h in heads))
        dqs, dks, dvs, dgs, dss = vjp((tuple(dy_ref[:, _head(h)] for h in heads), tuple(ds_scr[h] for h in heads)))
        for h in heads:
            dq_ref[:, _head(h)] = dqs[h].astype(BF16)
            dk_ref[:, _head(h)] = dks[h].astype(BF16)
            dv_ref[:, _head(h)] = dvs[h].astype(BF16)
            dg_ref[:, _head(h)] = dgs[h].astype(BF16)
            ds_scr[h] = dss[h]

    out = _sds((SEQ, GROUP), BF16)
    return pl.pallas_call(
        body, out_shape=(out, out, out, out), grid=(nc,),
        in_specs=[group(0), group(1), group(2), group(3), tab, tab, const, const, const, const, state, group(0)],
        out_specs=(group(0), group(0), group(0), group(0)), scratch_shapes=[pltpu.VMEM((N_HEADS, HEAD, HEAD), F32)],
        compiler_params=_params(), name=name,
    )(p, p, p, p, *tables, states, dy)


def _gdn_specs(rev):
    c = GDN_CHUNK
    nc = SEQ // c

    def n_of(n):
        return nc - 1 - n if rev else n

    def group(off):
        return pl.BlockSpec((c, GROUP), lambda n: (n_of(n), off))

    small = pl.BlockSpec((c, LANES), lambda n: (n_of(n), 0))
    vec = pl.BlockSpec((1, LANES), lambda n: (0, 0))
    state = pl.BlockSpec((N_HEADS, None, HEAD, HEAD), lambda n: (0, n_of(n), 0, 0))
    return group, small, vec, state, nc


GDN_GATE_GROUP = 7


def gdn_fwd(conv, p, small, a_log, dt_bias, gain, *, name):
    group, sm, vec, state, nc = _gdn_specs(False)

    def body(q_ref, k_ref, v_ref, g_ref, sm_ref, al_ref, dt_ref, gn_ref, y_ref, st_ref, s_scr):
        @pl.when(pl.program_id(0) == 0)
        def _():
            s_scr[...] = jnp.zeros_like(s_scr)

        states = tuple(s_scr[h] for h in range(N_HEADS))
        ys, new_states = _gdn_fn(*(tuple(r[:, _head(h)] for h in range(N_HEADS)) for r in (q_ref, k_ref, v_ref, g_ref)),
                                 sm_ref[...], al_ref[...], dt_ref[...], gn_ref[...], states)
        for h in range(N_HEADS):
            st_ref[h] = states[h]
            y_ref[:, _head(h)] = ys[h].astype(BF16)
            s_scr[h] = new_states[h]

    return pl.pallas_call(
        body, out_shape=(_sds((SEQ, GROUP), BF16), _sds((N_HEADS, nc, HEAD, HEAD), F32)), grid=(nc,),
        in_specs=[group(0), group(1), group(2), group(GDN_GATE_GROUP), sm, vec, vec, vec], out_specs=(group(0), state),
        scratch_shapes=[pltpu.VMEM((N_HEADS, HEAD, HEAD), F32)], compiler_params=_params(), name=name,
    )(conv, conv, conv, p, small, a_log, dt_bias, gain)


def gdn_bwd(conv, p, small, a_log, dt_bias, gain, states, dy, *, name):
    group, sm, vec, state, nc = _gdn_specs(True)

    def body(q_ref, k_ref, v_ref, g_ref, sm_ref, al_ref, dt_ref, gn_ref, st_ref, dy_ref,
             dq_ref, dk_ref, dv_ref, dg_ref, dsm_ref, dal_ref, ddt_ref, dgn_ref, ds_scr):
        @pl.when(pl.program_id(0) == 0)
        def _():
            ds_scr[...] = jnp.zeros_like(ds_scr)
            dal_ref[...] = jnp.zeros_like(dal_ref)
            ddt_ref[...] = jnp.zeros_like(ddt_ref)
            dgn_ref[...] = jnp.zeros_like(dgn_ref)

        per_head = tuple(tuple(r[:, _head(h)] for h in range(N_HEADS)) for r in (q_ref, k_ref, v_ref, g_ref))
        _, vjp = jax.vjp(_gdn_fn, *per_head, sm_ref[...], al_ref[...], dt_ref[...], gn_ref[...],
                         tuple(st_ref[h] for h in range(N_HEADS)))
        cts = (tuple(dy_ref[:, _head(h)] for h in range(N_HEADS)), tuple(ds_scr[h] for h in range(N_HEADS)))
        dqs, dks, dvs, dgs, dsm, dal, ddt, dgn, dss = vjp(cts)
        for h in range(N_HEADS):
            dq_ref[:, _head(h)] = dqs[h]
            dk_ref[:, _head(h)] = dks[h]
            dv_ref[:, _head(h)] = dvs[h]
            dg_ref[:, _head(h)] = dgs[h].astype(BF16)
            ds_scr[h] = dss[h]
        dsm_ref[...] = dsm
        dal_ref[...] += dal
        ddt_ref[...] += ddt
        dgn_ref[...] += dgn

    f = _sds((SEQ, GROUP), F32)
    pv = _sds((1, LANES), F32)
    return pl.pallas_call(
        body, out_shape=(f, f, f, _sds((SEQ, GROUP), BF16), _sds((SEQ, LANES), F32), pv, pv, pv), grid=(nc,),
        in_specs=[group(0), group(1), group(2), group(GDN_GATE_GROUP), sm, vec, vec, vec, state, group(1)],
        out_specs=(group(0), group(0), group(0), group(0), sm, vec, vec, vec), scratch_shapes=[pltpu.VMEM((N_HEADS, HEAD, HEAD), F32)],
        compiler_params=_params(), name=name,
    )(conv, conv, conv, p, small, a_log, dt_bias, gain, states, dy)


PACK_ROW_TILE = 1024


def adamw(w, g, m, v, *, name):
    r = w.shape[0]
    tr = _tile(r, (PACK_ROW_TILE, 256, 128, 64, 32, 16, 8))

    def body(w_ref, g_ref, m_ref, v_ref, d_ref, nm_ref, nv_ref):
        gg = g_ref[...]
        nm = ADAM_B1 * m_ref[...] + (1.0 - ADAM_B1) * gg
        nv = ADAM_B2 * v_ref[...] + (1.0 - ADAM_B2) * jnp.square(gg)
        m_hat = nm / (1.0 - ADAM_B1 ** ADAM_STEP)
        v_hat = nv / (1.0 - ADAM_B2 ** ADAM_STEP)
        d_ref[...] = -ADAM_LR * (m_hat / (jnp.sqrt(v_hat) + ADAM_EPS) + ADAM_WD * w_ref[...])
        nm_ref[...] = nm
        nv_ref[...] = nv

    blk = pl.BlockSpec((tr, LANES), lambda i: (i, 0))
    o = _sds((r, LANES), F32)
    return pl.pallas_call(body, out_shape=(o, o, o), grid=(r // tr,), in_specs=[blk] * 4, out_specs=(blk, blk, blk),
                          compiler_params=_params(), name=name)(w, g, m, v)


ELEMENTWISE_BLOCK_BYTES = 2 * 1024 * 1024


def _row_tile(r, c):
    best = None
    for tr in range(8, r + 1, 8):
        if r % tr == 0 and tr * c * 4 <= ELEMENTWISE_BLOCK_BYTES:
            best = tr
    if best is None:
        raise ValueError(f"no row tile for ({r}, {c})")
    return best


def _core_index():
    return lax.axis_index("c").astype(jnp.int32).reshape(1)


def _chip_index():
    return (2 * lax.axis_index("x") + lax.axis_index("y")).astype(jnp.int32).reshape(1)


def adamw_halves(w, m, v, g_own, g_sib, *, layer=0, prev=None, name):
    n_layers, rows, c = w.shape
    r = rows // 2
    tr = _row_tile(r, c)
    nb = r // tr

    def body(c_ref, w_ref, m_ref, v_ref, own_ref, sib_ref, *rest):
        g_ref, d_ref, nm_ref, nv_ref = rest[-4:]
        gg = jnp.where(pl.program_id(0) == c_ref[0], own_ref[...], sib_ref[...])
        nm = ADAM_B1 * m_ref[...] + (1.0 - ADAM_B1) * gg
        nv = ADAM_B2 * v_ref[...] + (1.0 - ADAM_B2) * jnp.square(gg)
        m_hat = nm / (1.0 - ADAM_B1 ** ADAM_STEP)
        v_hat = nv / (1.0 - ADAM_B2 ** ADAM_STEP)
        g_ref[...] = gg
        d_ref[...] = -ADAM_LR * (m_hat / (jnp.sqrt(v_hat) + ADAM_EPS) + ADAM_WD * w_ref[...])
        nm_ref[...] = nm
        nv_ref[...] = nv

    full = pl.BlockSpec((None, tr, c), lambda h, i, cr: (layer, h * nb + i, 0))
    half = pl.BlockSpec((tr, c), lambda h, i, cr: (i, 0))
    o = _sds((n_layers, rows, c), F32)
    prev = list(prev or ())
    gs = pltpu.PrefetchScalarGridSpec(num_scalar_prefetch=1, grid=(2, nb), in_specs=[full, full, full, half, half] + [_ANY] * len(prev),
                                      out_specs=(full, full, full, full))
    n_fixed = 6
    return pl.pallas_call(body, out_shape=(o, o, o, o), grid_spec=gs, compiler_params=_params(), name=name,
                          input_output_aliases={n_fixed + k: k for k in range(len(prev))})(
        _core_index(), w, m, v, g_own, g_sib, *prev)


def add_core_halves(g2, land, *, out_dtype, name):
    _, ns, r, cols = g2.shape
    tr = _row_tile(r, cols)

    def body(c_ref, a_ref, b_ref, o_ref):
        o_ref[...] = (a_ref[...] + b_ref[...]).astype(out_dtype)

    gs = pltpu.PrefetchScalarGridSpec(
        num_scalar_prefetch=1, grid=(ns, r // tr),
        in_specs=[pl.BlockSpec((None, None, tr, cols), lambda s, i, cr: (cr[0], s, i, 0)),
                  pl.BlockSpec((None, tr, cols), lambda s, i, cr: (s, i, 0))],
        out_specs=pl.BlockSpec((None, tr, cols), lambda s, i, cr: (s, i, 0)))
    return pl.pallas_call(body, out_shape=_sds((ns, r, cols), out_dtype), grid_spec=gs, compiler_params=_params(), name=name)(
        _core_index(), g2, land)


def sum_over_chips(own, land, *, scatter, name):
    _, r, cols = own.shape
    tr = _row_tile(r, cols)

    def body(mine_ref, own_ref, l0, l1, l2, l3, o_ref):
        mine = mine_ref[0]
        mine_val = own_ref[...]
        acc = None
        for s, l_ref in enumerate((l0, l1, l2, l3)):
            val = jnp.where(mine == s, mine_val, l_ref[...]).astype(F32)
            acc = val if acc is None else acc + val
        o_ref[...] = acc

    def slot(s):
        return pl.BlockSpec((None, tr, cols), lambda i, mr: (jnp.where(mr[0] == s, (s + 1) % N_SHARD, s), i, 0))

    own_spec = pl.BlockSpec((None, tr, cols), lambda i, mr: (mr[0] if scatter else 0, i, 0))
    gs = pltpu.PrefetchScalarGridSpec(num_scalar_prefetch=1, grid=(r // tr,), in_specs=[own_spec] + [slot(s) for s in range(N_SHARD)],
                                      out_specs=pl.BlockSpec((tr, cols), lambda i, mr: (i, 0)))
    return pl.pallas_call(body, out_shape=_sds((r, cols), F32), grid_spec=gs, compiler_params=_params(), name=name)(
        _chip_index(), own, land, land, land, land)


_ANY = pl.BlockSpec(memory_space=pl.ANY)


def xy_exchange(src, *, scatter, name):
    rh = src.shape[1]

    def body(src_ref, land_ref, send_sems, recv_sems, loc_sem):
        x, y, c = lax.axis_index("x"), lax.axis_index("y"), lax.axis_index("c")
        mine = 2 * x + y
        peers = [(1 - x, y), (x, 1 - y), (1 - x, 1 - y)]

        def piece(shard):
            return src_ref.at[shard] if scatter else src_ref.at[c]

        def copy(k, px, py, dst_slot):
            return pltpu.make_async_remote_copy(src_ref=piece(2 * px + py), dst_ref=land_ref.at[dst_slot], send_sem=send_sems.at[k],
                                                recv_sem=recv_sems.at[k], device_id=(px, py, c), device_id_type=MESH)

        keep = pltpu.make_async_copy(piece(mine), land_ref.at[mine], loc_sem)
        keep.start()
        sends = [copy(k, px, py, mine) for k, (px, py) in enumerate(peers)]
        for cp in sends:
            cp.start()
        for cp in sends:
            cp.wait_send()
        for k, (px, py) in enumerate(peers):
            copy(k, px, py, 2 * px + py).wait_recv()
        keep.wait()

    return pl.pallas_call(body, out_shape=_sds((N_SHARD, rh, LANES), src.dtype), in_specs=[_ANY], out_specs=_ANY,
                          scratch_shapes=[pltpu.SemaphoreType.DMA((3,)), pltpu.SemaphoreType.DMA((3,)), pltpu.SemaphoreType.DMA(())],
                          name=name)(src)


def core_exchange(src, *, send_other_half, name):
    def body(src_ref, out_ref, send_sem, recv_sem, loc_sem):
        x, y, c = lax.axis_index("x"), lax.axis_index("y"), lax.axis_index("c")
        if send_other_half:
            cp = pltpu.make_async_remote_copy(src_ref=src_ref.at[1 - c], dst_ref=out_ref, send_sem=send_sem, recv_sem=recv_sem,
                                              device_id=(x, y, 1 - c), device_id_type=MESH)
            cp.start()
            cp.wait_send()
            cp.wait_recv()
        else:
            keep = pltpu.make_async_copy(src_ref, out_ref.at[c], loc_sem)
            keep.start()
            cp = pltpu.make_async_remote_copy(src_ref=src_ref, dst_ref=out_ref.at[c], send_sem=send_sem, recv_sem=recv_sem,
                                              device_id=(x, y, 1 - c), device_id_type=MESH)
            cp.start()
            cp.wait_send()
            pltpu.make_async_remote_copy(src_ref=src_ref, dst_ref=out_ref.at[1 - c], send_sem=send_sem, recv_sem=recv_sem,
                                         device_id=(x, y, 1 - c), device_id_type=MESH).wait_recv()
            keep.wait()

    out_shape = _sds(src.shape[1:], src.dtype) if send_other_half else _sds((2,) + src.shape, src.dtype)
    return pl.pallas_call(body, out_shape=out_shape, in_specs=[_ANY], out_specs=_ANY,
                          scratch_shapes=[pltpu.SemaphoreType.DMA(()), pltpu.SemaphoreType.DMA(()), pltpu.SemaphoreType.DMA(())],
                          name=name)(src)


def _comm_call(body, ins, out_shapes, sem_counts, name):
    return pl.pallas_call(body, out_shape=tuple(out_shapes), in_specs=[_ANY] * len(ins), out_specs=tuple([_ANY] * len(out_shapes)),
                          scratch_shapes=[pltpu.SemaphoreType.DMA((k,)) for k in sem_counts], name=name)(*ins)


def _sequencer_call(body, ins, out_shapes, sem_counts, name, collective_id):
    return pl.kernel(body, out_type=list(out_shapes), mesh=plsc.ScalarSubcoreMesh(axis_name="sequencer", num_cores=1), name=name,
                     scratch_types=[pltpu.SemaphoreType.DMA((k,)) for k in sem_counts],
                     compiler_params=pltpu.CompilerParams(collective_id=collective_id))(*ins)


def _handshake(peers):
    barrier = pltpu.get_barrier_semaphore()
    for peer in peers:
        pl.semaphore_signal(barrier, inc=1, device_id=peer, device_id_type=MESH)
    pl.semaphore_wait(barrier, len(peers))


def _xy_peers(x, y):
    return [(1 - x, y), (x, 1 - y), (1 - x, 1 - y)]


def gather_halves(halves, *, name, collective_id):
    n = len(halves)

    def body(*refs):
        ins, lands, sibs = refs[:n], refs[n:2 * n], refs[2 * n:3 * n]
        ici_send, ici_recv, d2d_send, d2d_recv = refs[3 * n:]
        x, y, c = lax.axis_index("x"), lax.axis_index("y"), lax.axis_index("c")
        mine = 2 * x + y
        peers = _xy_peers(x, y)
        _handshake([(px, py, c) for px, py in peers] + [(x, y, 1 - c)])

        def ici(i, k, slot):
            px, py = peers[k]
            return pltpu.make_async_remote_copy(src_ref=ins[i].at[c], dst_ref=lands[i].at[slot], send_sem=ici_send.at[3 * i + k],
                                                recv_sem=ici_recv.at[3 * i + k], device_id=(px, py, c), device_id_type=MESH)

        def pass_on(i, k):
            px, py = peers[k]
            slot = 2 * px + py
            return pltpu.make_async_remote_copy(src_ref=lands[i].at[slot], dst_ref=sibs[i].at[slot], send_sem=d2d_send.at[3 * i + k],
                                                recv_sem=d2d_recv.at[3 * i + k], device_id=(x, y, 1 - c), device_id_type=MESH)

        sends = [ici(i, k, mine) for i in range(n) for k in range(3)]
        for cp in sends:
            cp.start()
        passed = []
        for i in range(n):
            for k in range(3):
                px, py = peers[k]
                ici(i, k, 2 * px + py).wait_recv()
                cp = pass_on(i, k)
                cp.start()
                passed.append(cp)
        for cp in passed:
            cp.wait_recv()
        for cp in sends + passed:
            cp.wait_send()

    outs = [_sds((N_SHARD,) + h.shape[1:], h.dtype) for h in halves]
    res = _sequencer_call(body, halves, outs + outs, [3 * n] * 4, name, collective_id)
    return res[:n], res[n:]


def send_other_half(arrays, *, name, collective_id):
    n = len(arrays)

    def body(*refs):
        ins, lands = refs[:n], refs[n:2 * n]
        send_sems, recv_sems = refs[2 * n:]
        x, y, c = lax.axis_index("x"), lax.axis_index("y"), lax.axis_index("c")
        _handshake([(x, y, 1 - c)])
        copies = [pltpu.make_async_remote_copy(src_ref=ins[i].at[1 - c], dst_ref=lands[i], send_sem=send_sems.at[i],
                                               recv_sem=recv_sems.at[i], device_id=(x, y, 1 - c), device_id_type=MESH) for i in range(n)]
        for cp in copies:
            cp.start()
        for cp in copies:
            cp.wait_recv()
        for cp in copies:
            cp.wait_send()

    return _sequencer_call(body, arrays, [_sds(a.shape[1:], a.dtype) for a in arrays], [n, n], name, collective_id)


def send_to_chips(arrays, scatter, *, name, collective_id):
    n = len(arrays)

    def body(*refs):
        ins, lands = refs[:n], refs[n:2 * n]
        send_sems, recv_sems = refs[2 * n:]
        x, y, c = lax.axis_index("x"), lax.axis_index("y"), lax.axis_index("c")
        mine = 2 * x + y
        peers = _xy_peers(x, y)
        _handshake([(px, py, c) for px, py in peers])

        def copy(i, k, dst_slot):
            px, py = peers[k]
            src = ins[i].at[2 * px + py] if scatter[i] else ins[i].at[0]
            return pltpu.make_async_remote_copy(src_ref=src, dst_ref=lands[i].at[dst_slot], send_sem=send_sems.at[3 * i + k],
                                                recv_sem=recv_sems.at[3 * i + k], device_id=(px, py, c), device_id_type=MESH)

        sends = [copy(i, k, mine) for i in range(n) for k in range(3)]
        for cp in sends:
            cp.start()
        for i in range(n):
            for k in range(3):
                px, py = peers[k]
                copy(i, k, 2 * px + py).wait_recv()
        for cp in sends:
            cp.wait_send()

    return _sequencer_call(body, arrays, [_sds((N_SHARD,) + a.shape[1:], a.dtype) for a in arrays], [3 * n, 3 * n], name, collective_id)


def swap_with_other_core(arrays, *, name, collective_id):
    n = len(arrays)

    def body(*refs):
        ins, lands = refs[:n], refs[n:2 * n]
        send_sems, recv_sems = refs[2 * n:]
        x, y, c = lax.axis_index("x"), lax.axis_index("y"), lax.axis_index("c")
        _handshake([(x, y, 1 - c)])
        copies = [pltpu.make_async_remote_copy(src_ref=ins[i], dst_ref=lands[i], send_sem=send_sems.at[i], recv_sem=recv_sems.at[i],
                                               device_id=(x, y, 1 - c), device_id_type=MESH) for i in range(n)]
        for cp in copies:
            cp.start()
        for cp in copies:
            cp.wait_recv()
        for cp in copies:
            cp.wait_send()

    return _sequencer_call(body, arrays, [_sds(a.shape, a.dtype) for a in arrays], [n, n], name, collective_id)


def _pack_rows(n_elems, row_multiple):
    rows = -(-n_elems // LANES)
    return -(-rows // row_multiple) * row_multiple


def _pack(arrays, rows, dtype):
    flat = jnp.concatenate([a.reshape(-1).astype(dtype) for a in arrays])
    return jnp.pad(flat, (0, rows * LANES - flat.shape[0])).reshape(rows, LANES)


def _unpack(packed, shapes):
    flat = packed.reshape(-1)
    out, off = [], 0
    for s in shapes:
        n = int(np.prod(s))
        out.append(flat[off:off + n].reshape(s))
        off += n
    return out


def all_gather_shards(shards, axes, dtype, row_multiple, tag):
    shapes = [s.shape for s in shards]
    rows = _pack_rows(sum(int(np.prod(s)) for s in shapes), row_multiple)
    packed = _pack(shards, rows, dtype).reshape(2, rows // 2, LANES)
    land = xy_exchange(packed, scatter=False, name=f"gather_xy_{tag}")
    both = core_exchange(land, send_other_half=False, name=f"gather_c_{tag}")
    per_shard = jnp.swapaxes(both, 0, 1).reshape(N_SHARD, rows, LANES)
    pieces = [_unpack(per_shard[s], shapes) for s in range(N_SHARD)]
    return [jnp.concatenate([pieces[s][i] for s in range(N_SHARD)], axis=ax) for i, ax in enumerate(axes)]


def _ordered_before(first, then):
    if then is None:
        return first, None
    return lax.optimization_barrier((first, then))


def reduce_between_cores(arrays, scatter, *, tag, collective_id, before=None):
    arrays, before = _ordered_before(arrays, before)
    land = send_other_half(arrays, name=f"reduce_core_send_{tag}", collective_id=collective_id)
    return (arrays, land, scatter, tag, collective_id), before


def reduce_between_chips(state, before=None):
    arrays, land, scatter, tag, collective_id = state
    chip = [add_core_halves(a, l, out_dtype=BF16 if sc else F32, name=f"reduce_core_add_{tag}_{i}")
            for i, (a, l, sc) in enumerate(zip(arrays, land, scatter))]
    chip, before = _ordered_before(chip, before)
    land = send_to_chips(chip, scatter, name=f"reduce_chip_send_{tag}", collective_id=collective_id + 1)
    return (chip, land, scatter, tag, collective_id), before


def reduce_finish(state):
    chip, land, scatter, tag, collective_id = state
    own = [sum_over_chips(ch, l, scatter=sc, name=f"reduce_chip_add_{tag}_{i}") for i, (ch, l, sc) in enumerate(zip(chip, land, scatter))]
    sib = swap_with_other_core(own, name=f"reduce_core_swap_{tag}", collective_id=collective_id + 2)
    return own, sib


def _ffn_layer_fwd(h, norm_g, w_up, cw, cb, w_down, tag):
    hn = norm_fwd(h, norm_g, name=f"ffn_norm_{tag}")
    u = matmul(hn, w_up, name=f"ffn_up_{tag}")
    act = ffn_act_fwd(u, cw, cb, name=f"ffn_act_{tag}")
    out = matmul(act, w_down, add=h, name=f"ffn_down_{tag}")
    return out, (h, hn, u, act)


def _ffn_layer_bwd(saved, dout, norm_g, w_up, cw, cb, w_down, tag):
    h, hn, u, act = saved
    dact = matmul(dout, w_down, tb=True, name=f"ffn_down_dx_{tag}")
    d_w_down = matmul(act, dout, ta=True, name=f"ffn_down_dw_{tag}")
    dug, duv, dcw, dcb = ffn_act_bwd(u, cw, cb, dact, name=f"ffn_act_bwd_{tag}")
    du = jnp.concatenate([dug, duv], axis=1)
    dhn = matmul(du, w_up, tb=True, name=f"ffn_up_dx_{tag}")
    d_w_up = matmul(hn, du, ta=True, name=f"ffn_up_dw_{tag}")
    dh, dg = norm_bwd(h, norm_g, dhn, dout, name=f"ffn_norm_bwd_{tag}")
    return dh, dg, d_w_up, dcw, dcb, d_w_down


def local_step(x, target, w, stage=lambda name, tensors, grads=None: tensors):
    g = {}
    tables = _ret_tables()
    x = stage("start", x)
    w_in = w["ret_gdn_w_in"]
    w_main = w_in[:, :MIX_MAIN]
    w_small = jnp.pad(w_in[:, MIX_MAIN:], ((0, 0), (0, LANES - 2 * N_HEADS)))
    a_log = jnp.pad(w["gdn_a_log"], ((0, 0), (0, LANES - N_HEADS)))
    dt_bias = jnp.pad(w["gdn_dt_bias"], ((0, 0), (0, LANES - N_HEADS)))

    hn0 = stage("normed", norm_fwd(x, w["norm_mix"][0:1], name="mix0_norm"))
    p = matmul(hn0, w_main, name="mix0_in")
    small = matmul(hn0, w_small, name="mix0_in_small")
    y_ret, s_ret = ret_fwd(p, tables, name="ret_fwd")
    conv = gdn_conv_fwd(p, w["gdn_conv_w"], name="gdn_conv")
    y_gdn, s_gdn = gdn_fwd(conv, p, small, a_log, dt_bias, w["gdn_out_gain"], name="gdn_fwd")
    y0 = stage("mixed", jnp.concatenate([y_ret, y_gdn], axis=1))
    h1 = matmul(y0, w["ret_gdn_w_out"], add=x, name="mix0_out")
    h2, ffn0 = _ffn_layer_fwd(h1, w["norm_ffn"][0:1], w["ffn_w_up"][0], w["ffn_conv_w"][0], w["ffn_conv_b"][0:1], w["ffn_w_down"][0], "0")
    h2 = stage("layer0", h2)

    hn1 = norm_fwd(h2, w["norm_mix"][1:2], name="mix1_norm")
    gx = matmul(hn1, w["lru_w_in"], name="mix1_in")
    lru_p = (w["lru_conv_w"], w["lru_conv_b"], w["lru_w_a"], w["lru_b_a"], w["lru_w_x"], w["lru_b_x"], w["lru_lambda"])
    y1 = lru_fwd(gx, *lru_p, name="lru_fwd")
    h3 = matmul(y1, w["lru_w_out"], add=h2, name="mix1_out")
    h4, ffn1 = _ffn_layer_fwd(h3, w["norm_ffn"][1:2], w["ffn_w_up"][1], w["ffn_conv_w"][1], w["ffn_conv_b"][1:2], w["ffn_w_down"][1], "1")

    loss, dh4, g["norm_final"] = final_fwd_bwd(h4, w["norm_final"], target, name="final")

    dh3, dgf1, dwu1, dcw1, dcb1, dwd1 = _ffn_layer_bwd(ffn1, dh4, w["norm_ffn"][1:2], w["ffn_w_up"][1], w["ffn_conv_w"][1],
                                                     w["ffn_conv_b"][1:2], w["ffn_w_down"][1], "1")
    g["ffn_w_up_1"] = dwu1
    dh3 = stage("grads0_ready", dh3, g)
    dy1 = matmul(dh3, w["lru_w_out"], tb=True, name="mix1_out_dx")
    g["lru_w_out"] = matmul(y1, dh3, ta=True, name="mix1_out_dw")
    dgate, dxr, g["lru_conv_w"], g["lru_conv_b"], g["lru_w_a"], g["lru_b_a"], g["lru_w_x"], g["lru_b_x"], g["lru_lambda"] = lru_bwd(
        gx, *lru_p, dy1, name="lru_bwd")
    dgx = stage("grads0_send", jnp.concatenate([dgate, dxr], axis=1), g)
    dhn1 = matmul(dgx, w["lru_w_in"], tb=True, name="mix1_in_dx")
    g["lru_w_in"] = matmul(hn1, dgx, ta=True, name="mix1_in_dw")
    dh2, dgm1 = norm_bwd(h2, w["norm_mix"][1:2], dhn1, dh3, name="mix1_norm_bwd")
    dh2 = stage("grads1_ready", dh2, g)

    dh1, dgf0, dwu0, dcw0, dcb0, dwd0 = _ffn_layer_bwd(ffn0, dh2, w["norm_ffn"][0:1], w["ffn_w_up"][0], w["ffn_conv_w"][0],
                                                     w["ffn_conv_b"][0:1], w["ffn_w_down"][0], "0")
    g["ffn_w_up_0"] = dwu0
    g["ffn_w_down"] = jnp.stack([dwd0, dwd1])
    dh1 = stage("grads2_ready", stage("grads1_send", dh1, g), g)
    dy0 = matmul(dh1, w["ret_gdn_w_out"], tb=True, name="mix0_out_dx")
    g["ret_gdn_w_out"] = matmul(y0, dh1, ta=True, name="mix0_out_dw")
    dq_r, dk_r, dv_r, dg_r = ret_bwd(p, tables, s_ret, dy0, name="ret_bwd")
    dy0, dq_r = stage("grads2_send", (dy0, dq_r), g)
    dcq, dck, dcv, dg_d, dsmall, dal, ddt, dgain = gdn_bwd(conv, p, small, a_log, dt_bias, w["gdn_out_gain"], s_gdn, dy0, name="gdn_bwd")
    dconv = jnp.concatenate([dcq, dck, dcv], axis=1)
    dp_conv, g["gdn_conv_w"] = gdn_conv_bwd(p, w["gdn_conv_w"], dconv, name="gdn_conv_bwd")
    dp = jnp.concatenate([dq_r, dk_r, dv_r, dg_r, dp_conv, dg_d], axis=1)
    dhn0 = matmul(dp, w_main, tb=True, name="mix0_in_dx")
    dhn0 = matmul(dsmall, w_small, tb=True, add=dhn0, name="mix0_in_small_dx")
    d_w_main = matmul(hn0, dp, ta=True, name="mix0_in_dw")
    d_w_small = matmul(hn0, dsmall, ta=True, name="mix0_in_small_dw")
    g["ret_gdn_w_in"] = jnp.concatenate([d_w_main, d_w_small[:, :2 * N_HEADS]], axis=1)
    dx, dgm0 = norm_bwd(x, w["norm_mix"][0:1], dhn0, dh1, name="mix0_norm_bwd")

    g["gdn_a_log"] = dal[:, :N_HEADS]
    g["gdn_dt_bias"] = ddt[:, :N_HEADS]
    g["gdn_out_gain"] = dgain
    g["norm_mix"] = jnp.concatenate([dgm0, dgm1], axis=0)
    g["norm_ffn"] = jnp.concatenate([dgf0, dgf1], axis=0)
    g["ffn_conv_w"] = jnp.stack([dcw0, dcw1])
    g["ffn_conv_b"] = jnp.concatenate([dcb0, dcb1], axis=0)
    return loss, dx, g


WEIGHTS = ("norm_mix", "norm_ffn", "ret_gdn_w_in", "gdn_conv_w", "gdn_a_log", "gdn_dt_bias", "gdn_out_gain", "ret_gdn_w_out",
           "lru_w_in", "lru_conv_w", "lru_conv_b", "lru_w_a", "lru_b_a", "lru_w_x", "lru_b_x", "lru_lambda", "lru_w_out",
           "ffn_w_up", "ffn_conv_w", "ffn_conv_b", "ffn_w_down", "norm_final")
MATMUL_SHARDED = {"ret_gdn_w_in": 1, "ret_gdn_w_out": 0, "lru_w_in": 1, "lru_w_out": 0, "ffn_w_up": 2, "ffn_w_down": 1}
VECTOR_SHARDED = {"gdn_conv_w": 1, "lru_conv_w": 1, "lru_conv_b": 1, "lru_b_a": 1, "lru_b_x": 1, "lru_lambda": 1, "ffn_conv_w": 2}
SHARDED = {**MATMUL_SHARDED, **VECTOR_SHARDED}
REPLICATED = tuple(n for n in WEIGHTS if n not in SHARDED)
SQUEEZE = {"ret_gdn_w_in", "gdn_conv_w", "ret_gdn_w_out", "lru_w_in", "lru_conv_w", "lru_w_a", "lru_w_x", "lru_w_out"}
MIX_IN = MIX_MAIN + 2 * N_HEADS
BIG_ARRAYS = {
    "ret_gdn_w_in": ("ret_gdn_w_in", None, (D_MODEL, MIX_IN), (2, D_MODEL // 2, N_SHARD, MIX_IN // N_SHARD), (0, 2, 1, 3)),
    "ret_gdn_w_out": ("ret_gdn_w_out", None, (2 * GROUP, D_MODEL), (N_SHARD, 2, GROUP // N_SHARD, D_MODEL), (1, 0, 2, 3)),
    "lru_w_in": ("lru_w_in", None, (D_MODEL, 2 * D_MODEL), (2, D_MODEL // 2, N_SHARD, 2 * D_MODEL // N_SHARD), (0, 2, 1, 3)),
    "lru_w_out": ("lru_w_out", None, (D_MODEL, D_MODEL), (N_SHARD, 2, D_MODEL // (2 * N_SHARD), D_MODEL), (1, 0, 2, 3)),
    "ffn_w_up_0": ("ffn_w_up", 0, (D_MODEL, 2 * D_FF), (2, D_MODEL // 2, N_SHARD, 2 * D_FF // N_SHARD), (0, 2, 1, 3)),
    "ffn_w_up_1": ("ffn_w_up", 1, (D_MODEL, 2 * D_FF), (2, D_MODEL // 2, N_SHARD, 2 * D_FF // N_SHARD), (0, 2, 1, 3)),
    "ffn_w_down": ("ffn_w_down", None, (2, D_FF, D_MODEL), (2, N_SHARD, D_FF // N_SHARD, D_MODEL), (0, 1, 2, 3)),
}
GATHER_GROUPS = (("ret_gdn_w_in",), ("ret_gdn_w_out", "ffn_w_up_0", "ffn_w_down"), ("lru_w_in", "lru_w_out", "ffn_w_up_1"))
REDUCE_GROUPS = (("ffn_w_up_1",), ("lru_w_in", "lru_w_out"), ("ffn_w_up_0", "ffn_w_down"), ("ret_gdn_w_out", "ret_gdn_w_in"))
GATHER_COLLECTIVE_ID = 1
REDUCE_COLLECTIVE_ID = GATHER_COLLECTIVE_ID + len(GATHER_GROUPS)


def _local_view(name, a):
    if name in SQUEEZE:
        return a[0]
    if a.ndim == 1:
        return a[None, :]
    return a


def kernel(x, norm_mix, norm_ffn, ret_gdn_w_in, gdn_conv_w, gdn_a_log, gdn_dt_bias, gdn_out_gain, ret_gdn_w_out, lru_w_in, lru_conv_w, lru_conv_b, lru_w_a, lru_b_a, lru_w_x, lru_b_x, lru_lambda, lru_w_out, ffn_w_up, ffn_conv_w, ffn_conv_b, ffn_w_down, norm_final, loss_target, m_norm_mix, m_norm_ffn, m_ret_gdn_w_in, m_gdn_conv_w, m_gdn_a_log, m_gdn_dt_bias, m_gdn_out_gain, m_ret_gdn_w_out, m_lru_w_in, m_lru_conv_w, m_lru_conv_b, m_lru_w_a, m_lru_b_a, m_lru_w_x, m_lru_b_x, m_lru_lambda, m_lru_w_out, m_ffn_w_up, m_ffn_conv_w, m_ffn_conv_b, m_ffn_w_down, m_norm_final, v_norm_mix, v_norm_ffn, v_ret_gdn_w_in, v_gdn_conv_w, v_gdn_a_log, v_gdn_dt_bias, v_gdn_out_gain, v_ret_gdn_w_out, v_lru_w_in, v_lru_conv_w, v_lru_conv_b, v_lru_w_a, v_lru_b_a, v_lru_w_x, v_lru_b_x, v_lru_lambda, v_lru_w_out, v_ffn_w_up, v_ffn_conv_w, v_ffn_conv_b, v_ffn_w_down, v_norm_final):
    given = dict(norm_mix=norm_mix, norm_ffn=norm_ffn, ret_gdn_w_in=ret_gdn_w_in, gdn_conv_w=gdn_conv_w, gdn_a_log=gdn_a_log, gdn_dt_bias=gdn_dt_bias, gdn_out_gain=gdn_out_gain, ret_gdn_w_out=ret_gdn_w_out, lru_w_in=lru_w_in, lru_conv_w=lru_conv_w, lru_conv_b=lru_conv_b, lru_w_a=lru_w_a, lru_b_a=lru_b_a, lru_w_x=lru_w_x, lru_b_x=lru_b_x, lru_lambda=lru_lambda, lru_w_out=lru_w_out, ffn_w_up=ffn_w_up, ffn_conv_w=ffn_conv_w, ffn_conv_b=ffn_conv_b, ffn_w_down=ffn_w_down, norm_final=norm_final)
    mom1 = dict(norm_mix=m_norm_mix, norm_ffn=m_norm_ffn, ret_gdn_w_in=m_ret_gdn_w_in, gdn_conv_w=m_gdn_conv_w, gdn_a_log=m_gdn_a_log, gdn_dt_bias=m_gdn_dt_bias, gdn_out_gain=m_gdn_out_gain, ret_gdn_w_out=m_ret_gdn_w_out, lru_w_in=m_lru_w_in, lru_conv_w=m_lru_conv_w, lru_conv_b=m_lru_conv_b, lru_w_a=m_lru_w_a, lru_b_a=m_lru_b_a, lru_w_x=m_lru_w_x, lru_b_x=m_lru_b_x, lru_lambda=m_lru_lambda, lru_w_out=m_lru_w_out, ffn_w_up=m_ffn_w_up, ffn_conv_w=m_ffn_conv_w, ffn_conv_b=m_ffn_conv_b, ffn_w_down=m_ffn_w_down, norm_final=m_norm_final)
    mom2 = dict(norm_mix=v_norm_mix, norm_ffn=v_norm_ffn, ret_gdn_w_in=v_ret_gdn_w_in, gdn_conv_w=v_gdn_conv_w, gdn_a_log=v_gdn_a_log, gdn_dt_bias=v_gdn_dt_bias, gdn_out_gain=v_gdn_out_gain, ret_gdn_w_out=v_ret_gdn_w_out, lru_w_in=v_lru_w_in, lru_conv_w=v_lru_conv_w, lru_conv_b=v_lru_conv_b, lru_w_a=v_lru_w_a, lru_b_a=v_lru_b_a, lru_w_x=v_lru_w_x, lru_b_x=v_lru_b_x, lru_lambda=v_lru_lambda, lru_w_out=v_lru_w_out, ffn_w_up=v_ffn_w_up, ffn_conv_w=v_ffn_conv_w, ffn_conv_b=v_ffn_conv_b, ffn_w_down=v_ffn_w_down, norm_final=v_norm_final)

    local = {n: _local_view(n, a) for n, a in given.items()}

    core = lax.axis_index("c")
    chip = 2 * lax.axis_index("x") + lax.axis_index("y")
    is_my_chip = lax.broadcasted_iota(jnp.int32, (N_SHARD, 1, 1), 0) == chip

    def by_core(mine, other):
        return jnp.where(core == 0, jnp.stack([mine, other]), jnp.stack([other, mine]))

    vec_names, rp_names = list(VECTOR_SHARDED), list(REPLICATED)
    full = dict(zip(vec_names, all_gather_shards([local[n] for n in vec_names], [SHARDED[n] for n in vec_names], F32, 32, "p")))
    for n in rp_names:
        full[n] = local[n]
    in_flight = {}

    def launch(gi, after=None):
        halves = []
        for a in GATHER_GROUPS[gi]:
            weight, layer, _, split, perm = BIG_ARRAYS[a]
            shard = local[weight] if layer is None else local[weight][layer]
            halves.append(shard.astype(BF16).reshape((2,) + tuple(split[p] for p in perm)[2:]))
        if after is not None:
            halves, after = lax.optimization_barrier((halves, after))
        in_flight[gi] = (halves,) + gather_halves(halves, name=f"gather_weights_{gi}", collective_id=GATHER_COLLECTIVE_ID + gi)
        return after

    def land(gi, after):
        halves, lands, sibs = in_flight[gi]
        (lands, sibs), after = lax.optimization_barrier(((lands, sibs), after))
        for a, mine, got, passed in zip(GATHER_GROUPS[gi], halves, lands, sibs):
            weight, layer, full_shape, split, perm = BIG_ARRAYS[a]
            half_mine = jnp.where(is_my_chip, jnp.where(core == 0, mine[0], mine[1])[None], got)
            half_other = jnp.where(is_my_chip, jnp.where(core == 0, mine[1], mine[0])[None], passed)
            value = by_core(half_mine, half_other).transpose(perm).reshape(full_shape)
            if layer is None:
                full[weight] = value
            else:
                full.setdefault(weight, [None, None])[layer] = value
        return after

    reducing = {}

    def reduce_ready(gi, grads, then=None, extra=()):
        arrays = [grads[a].reshape(BIG_ARRAYS[a][3]).transpose(BIG_ARRAYS[a][4]) for a in REDUCE_GROUPS[gi]] + list(extra)
        scatter = [True] * len(REDUCE_GROUPS[gi]) + [False] * len(extra)
        reducing[gi], then = reduce_between_cores(arrays, scatter, tag=str(gi), collective_id=REDUCE_COLLECTIVE_ID + 3 * gi, before=then)
        return then

    def reduce_send(gi, then=None):
        reducing[gi], then = reduce_between_chips(reducing[gi], before=then)
        return then

    def stage(name, tensors, grads=None):
        if name == "start":
            launch(0)
            launch(1)
            return land(0, tensors)
        if name == "normed":
            return launch(2, tensors)
        if name in ("mixed", "layer0"):
            return land({"mixed": 1, "layer0": 2}[name], tensors)
        gi = int(name[len("grads")])
        return reduce_ready(gi, grads, tensors) if name.endswith("_ready") else reduce_send(gi, tensors)

    loss_part, dx, grads = local_step(x[0], loss_target[0], full, stage)
    loss = lax.psum(loss_part[0, 0], ("x", "y", "c"))

    small_names = rp_names + vec_names
    small_shapes = [grads[n].shape for n in small_names]
    small_rows = _pack_rows(sum(int(np.prod(s)) for s in small_shapes), 16)
    small = _pack([grads[n] for n in small_names], small_rows, F32).reshape(2, 1, small_rows // 2, LANES)
    last = len(REDUCE_GROUPS) - 1
    reduce_ready(last, grads, extra=[small])
    reduce_send(last)
    reduced = {}
    for gi, group in enumerate(REDUCE_GROUPS):
        g_own, g_sib = reduce_finish(reducing[gi])
        reduced.update(zip(list(group) + ["small"], zip(g_own, g_sib)))

    result = {}
    for n in MATMUL_SHARDED:
        done = None
        for a in (k for k, spec in BIG_ARRAYS.items() if spec[0] == n):
            r, cols = reduced[a][0].shape
            layer = BIG_ARRAYS[a][1] or 0
            w3, m3, v3 = (t if BIG_ARRAYS[a][1] is not None else t.reshape(1, 2 * r, cols) for t in (given[n], mom1[n], mom2[n]))
            done = adamw_halves(w3, m3, v3, *reduced[a], layer=layer, prev=done, name=f"adamw_{a}")
        result[n] = done

    g_small = dict(zip(small_names, _unpack(by_core(*reduced["small"]).reshape(small_rows, LANES), small_shapes)))
    for n in vec_names:
        size = local[n].shape[SHARDED[n]]
        g_small[n] = lax.dynamic_slice_in_dim(g_small[n], chip * size, size, axis=SHARDED[n])
    loc_shapes = [local[n].shape for n in small_names]
    loc_rows = _pack_rows(sum(int(np.prod(s)) for s in loc_shapes), 256)
    packs = [_pack([src[n] for n in small_names], loc_rows, F32) for src in (given, g_small, mom1, mom2)]
    d_s, m_s, v_s = adamw(*packs, name="adamw_small")
    for n, d, nm, nv in zip(small_names, _unpack(d_s, loc_shapes), _unpack(m_s, loc_shapes), _unpack(v_s, loc_shapes)):
        result[n] = (g_small[n], d, nm, nv)

    outs = [[result[n][k].reshape(given[n].shape) for n in WEIGHTS] for k in range(4)]
    return (loss, dx[None], *outs[0], *outs[1], *outs[2], *outs[3])
```

```python
import functools

import numpy as np
import jax
import jax.numpy as jnp
from jax import lax
from jax.experimental import pallas as pl
from jax.experimental.pallas import tpu as pltpu
from jax.experimental.pallas import tpu_sc as plsc

F32 = jnp.float32
BF16 = jnp.bfloat16
HI = lax.Precision.HIGHEST
MESH = pl.DeviceIdType.MESH

SEQ = 2048
D_MODEL = 1024
N_HEADS = 4
HEAD = 128
RET_CHUNK = 128
GDN_CHUNK = 64
GROUP = N_HEADS * HEAD
MIX_MAIN = 8 * GROUP
D_FF = 2816
LRU_BLOCKS = 8
LRU_C = 8.0
ROPE_BASE = 10000.0
EPS = 1e-6
N_SHARD = 4
LANES = 128

ADAM_LR, ADAM_B1, ADAM_B2, ADAM_EPS, ADAM_WD, ADAM_STEP = 0.001, 0.9, 0.999, 1e-08, 0.01, 10

VMEM_LIMIT_BYTES = 56 * 1024 * 1024

_roll = pltpu.roll


def _params(**kw):
    return pltpu.CompilerParams(vmem_limit_bytes=VMEM_LIMIT_BYTES, **kw)


def _sds(shape, dtype):
    return jax.ShapeDtypeStruct(tuple(shape), dtype)


def _shift_raw(x, d):
    n = x.shape[0]
    t = lax.broadcasted_iota(jnp.int32, x.shape, 0)
    if d > 0:
        return jnp.where(t >= d, _roll(x, d, 0), 0.0)
    return jnp.where(t < n + d, _roll(x, n + d, 0), 0.0)


@functools.partial(jax.custom_vjp, nondiff_argnums=(1,))
def shift_rows(x, d):
    return _shift_raw(x, d)


def _shift_fwd(x, d):
    return _shift_raw(x, d), None


def _shift_bwd(d, _, g):
    return (_shift_raw(g, -d),)


shift_rows.defvjp(_shift_fwd, _shift_bwd)


@jax.custom_vjp
def swap_halves(x):
    return _roll(x, HEAD // 2, 1)


def _swap_fwd(x):
    return _roll(x, HEAD // 2, 1), None


def _swap_bwd(_, g):
    return (_roll(g, HEAD // 2, 1),)


swap_halves.defvjp(_swap_fwd, _swap_bwd)


def _scan_raw(a, u, reverse):
    n = a.shape[0]
    t = lax.broadcasted_iota(jnp.int32, a.shape, 0)
    d = 1
    while d < n:
        if reverse:
            m = t < n - d
            a_s, u_s = _roll(a, n - d, 0), _roll(u, n - d, 0)
        else:
            m = t >= d
            a_s, u_s = _roll(a, d, 0), _roll(u, d, 0)
        u = a * jnp.where(m, u_s, 0.0) + u
        a = a * jnp.where(m, a_s, 1.0)
        d *= 2
    return u


@jax.custom_vjp
def lin_scan(a, u):
    return _scan_raw(a, u, False)


def _lin_scan_fwd(a, u):
    hs = _scan_raw(a, u, False)
    return hs, (a, hs)


def _lin_scan_bwd(res, g):
    a, hs = res
    lam = _scan_raw(_shift_raw(a, -1), g, True)
    return lam * _shift_raw(hs, 1), lam


lin_scan.defvjp(_lin_scan_fwd, _lin_scan_bwd)


def _bdot(a, b, dims=(((1,), (0,)), ((), ()))):
    return lax.dot_general(a.astype(BF16), b.astype(BF16), dims, preferred_element_type=F32)


def _each(f, *seqs):
    return tuple(f(*a) for a in zip(*seqs))


def _split_bf16(a):
    hi = a.astype(BF16)
    return hi, (a - hi.astype(F32)).astype(BF16)


def _dot3_raw(a_s, b_s):
    a_hl = _each(_split_bf16, a_s)
    b_hl = _each(_split_bf16, b_s)
    hh = _each(lambda a, b: _bdot(a[0], b[0]), a_hl, b_hl)
    hl = _each(lambda a, b: _bdot(a[0], b[1]), a_hl, b_hl)
    lh = _each(lambda a, b: _bdot(a[1], b[0]), a_hl, b_hl)
    return _each(lambda x, y, z: x + (y + z), hh, hl, lh)


@jax.custom_vjp
def dot3(a_s, b_s):
    return _dot3_raw(a_s, b_s)


def _dot3_fwd(a_s, b_s):
    return _dot3_raw(a_s, b_s), (a_s, b_s)


def _dot3_bwd(res, g_s):
    a_s, b_s = res
    return (_each(lambda g, b: _bdot(g, b, (((1,), (1,)), ((), ()))), g_s, b_s),
            _each(lambda a, g: _bdot(a, g, (((0,), (0,)), ((), ()))), a_s, g_s))


dot3.defvjp(_dot3_fwd, _dot3_bwd)


def _eye(n):
    i = lax.broadcasted_iota(jnp.int32, (n, n), 0)
    j = lax.broadcasted_iota(jnp.int32, (n, n), 1)
    return (i == j).astype(F32)


def _unit_lower_inverse_raw(lmats):
    n = lmats[0].shape[0]
    eye = _eye(n)
    ps = _each(lambda l: -l, lmats)
    invs = _each(lambda x: eye + x, ps)
    k = 1
    while 2 * k < n:
        ps = _each(lambda p: _bdot(p, p), ps)
        invs = _each(lambda inv, p: inv + _bdot(inv, p), invs, ps)
        k *= 2
    prods = _dot3_raw(lmats, invs)
    resids = _each(lambda inv, pr: eye - inv - pr, invs, prods)
    return _each(lambda inv, r: inv + _bdot(inv, r), invs, resids)


@jax.custom_vjp
def unit_lower_inverse(lmats):
    return _unit_lower_inverse_raw(lmats)


def _uli_fwd(lmats):
    invs = _unit_lower_inverse_raw(lmats)
    return invs, invs


def _uli_bwd(invs, g_s):
    ms = _each(lambda inv, g: _bdot(inv, g, (((0,), (0,)), ((), ()))), invs, g_s)
    return (_each(lambda m, inv: -_bdot(m, inv, (((1,), (1,)), ((), ()))), ms, invs),)


unit_lower_inverse.defvjp(_uli_fwd, _uli_bwd)


def _cumsum_raw(x, reverse):
    n = x.shape[0]
    t = lax.broadcasted_iota(jnp.int32, x.shape, 0)
    d = 1
    while d < n:
        if reverse:
            x = x + jnp.where(t < n - d, _roll(x, n - d, 0), 0.0)
        else:
            x = x + jnp.where(t >= d, _roll(x, d, 0), 0.0)
        d *= 2
    return x


@jax.custom_vjp
def cumsum_rows(x):
    return _cumsum_raw(x, False)


def _cumsum_fwd(x):
    return _cumsum_raw(x, False), None


def _cumsum_bwd(_, g):
    return (_cumsum_raw(g, True),)


cumsum_rows.defvjp(_cumsum_fwd, _cumsum_bwd)


_NT = (((1,), (1,)), ((), ()))
_TN = (((0,), (0,)), ((), ()))


def _softplus(x):
    return jnp.maximum(x, 0.0) + jnp.log1p(jnp.exp(-jnp.abs(x)))


def _expm1_nonpos(x):
    poly = x * (1.0 + x * (0.5 + x * (1.0 / 6 + x * (1.0 / 24 + x * (1.0 / 120 + x * (1.0 / 720))))))
    return jnp.where(x > -0.25, poly, jnp.exp(x) - 1.0)


def _rms(x):
    return x * lax.rsqrt(jnp.mean(x * x, axis=-1, keepdims=True) + EPS)


def _causal_conv(x, w, width):
    y = w[width - 1:width, :] * x
    for j in range(width - 1):
        y = y + w[j:j + 1, :] * shift_rows(x, width - 1 - j)
    return y


def _norm_fn(x, g):
    return _rms(x) * g


def _ffn_act_fn(ug, uv, wg, wv, bg, bv):
    return jax.nn.silu(_causal_conv(ug, wg, 3) + bg) * (_causal_conv(uv, wv, 3) + bv)


def _gdn_conv_fn(x, w):
    return jax.nn.silu(_causal_conv(x, w, 4))


def _lru_fn(gate, x, cw, cb, wa, ba, wx, bx, lam):
    xr = _causal_conv(x, cw, 4) + cb
    r = jax.nn.sigmoid(_bdot(xr, wa) + ba)
    i = jax.nn.sigmoid(_bdot(xr, wx) + bx)
    log_a = -LRU_C * r * _softplus(-lam)
    a = jnp.exp(log_a)
    u = jnp.sqrt(-_expm1_nonpos(2.0 * log_a)) * (i * xr)
    hs = lin_scan(a, u)
    return jax.nn.gelu(gate) * hs


def _ret_fn(qs, ks, vs, gates, states, cos2, sin2, dmasks, ktails, qdecs, cdecs):
    qrs = _each(lambda q: q * cos2 + swap_halves(q) * sin2, qs)
    krs = _each(lambda k: (k * cos2 + swap_halves(k) * sin2) * (HEAD ** -0.5), ks)
    scores = _each(lambda q, k, m: _bdot(q, k, _NT) * m, qrs, krs, dmasks)
    inter = _each(lambda q, d, s: _bdot(q * d, s), qrs, qdecs, states)
    os_ = _each(lambda sc, v, x: _bdot(sc, v) + x, scores, vs, inter)
    new_states = _each(lambda s, cd, k, kt, v: s * cd + _bdot(k * kt, v, _TN), states, cdecs, krs, ktails, vs)
    ys = _each(lambda o, g: _rms(o) * jax.nn.silu(g), os_, gates)
    return ys, new_states


def _pick_lane(x, lane_idx):
    lane = lax.broadcasted_iota(jnp.int32, x.shape, 1)
    return jnp.sum(jnp.where(lane == lane_idx, x, 0.0), axis=1, keepdims=True)


def _l2norm(x):
    return x * lax.rsqrt(jnp.sum(x * x, axis=-1, keepdims=True) + EPS)


def _gdn_fn(qcs, kcs, vcs, gates, small, a_log, dt_bias, gain, states):
    c = GDN_CHUNK
    heads = tuple(range(len(qcs)))
    qs = _each(lambda x: _l2norm(x) * (HEAD ** -0.5), qcs)
    ks = _each(_l2norm, kcs)
    betas = _each(lambda h: jax.nn.sigmoid(_pick_lane(small, h)), heads)
    gs = _each(lambda h: -jnp.exp(_pick_lane(a_log, h)) * _softplus(_pick_lane(small, h + N_HEADS) + _pick_lane(dt_bias, h)), heads)
    i = lax.broadcasted_iota(jnp.int32, (c, c), 0)
    j = lax.broadcasted_iota(jnp.int32, (c, c), 1)
    tril = i >= j
    gcs = _each(lambda g: cumsum_rows(jnp.broadcast_to(g, (c, LANES)))[:, :1], gs)
    gc_rows = _each(lambda gc: jnp.broadcast_to(gc, (c, c)), gcs)
    decays = _each(lambda r: jnp.where(tril, jnp.exp(jnp.where(tril, r - r.T, 0.0)), 0.0), gc_rows)
    kbs = _each(lambda k, b: k * b, ks, betas)
    lmats = _each(lambda kb, k, d: jnp.where(i > j, _bdot(kb, k, _NT) * d, 0.0), kbs, ks, decays)
    attns = _each(lambda q, k, d: jnp.where(tril, _bdot(q, k, _NT) * d, 0.0), qs, ks, decays)
    invs = unit_lower_inverse(lmats)
    us = dot3(invs, _each(lambda v, b: v * b, vcs, betas))
    ws = dot3(invs, _each(lambda kb, gc: kb * jnp.exp(gc), kbs, gcs))
    g_lasts = _each(lambda g: jnp.sum(g, axis=0, keepdims=True), gs)
    v_news = _each(lambda u, w, s: u - _bdot(w, s), us, ws, states)
    inter = _each(lambda q, gc, s: _bdot(q * jnp.exp(gc), s), qs, gcs, states)
    os_ = _each(lambda x, a, v: x + _bdot(a, v), inter, attns, v_news)
    new_states = _each(lambda s, gl, k, gc, v: s * jnp.exp(gl) + _bdot(k * jnp.exp(gl - gc), v, _TN), states, g_lasts, ks, gcs, v_news)
    ys = _each(lambda o, gate: _rms(o) * gain * jax.nn.silu(gate), os_, gates)
    return ys, new_states


def _final_fn(h, g, target):
    y = _rms(h) * g
    return 0.5 * jnp.sum(jnp.mean(jnp.square(y - target), axis=-1, keepdims=True), axis=0, keepdims=True)


def _tile(n, candidates):
    for t in candidates:
        if n % t == 0:
            return t
    raise ValueError(f"no tile for {n}")


def matmul(a, b, *, ta=False, tb=False, add=None, out_dtype=F32, tm=None, tn=None, name):
    m = a.shape[1] if ta else a.shape[0]
    k = a.shape[0] if ta else a.shape[1]
    n = b.shape[0] if tb else b.shape[1]
    assert k == (b.shape[1] if tb else b.shape[0])
    tm = tm or _tile(m, (1024, 512, 1408, 256, 128))
    tn = tn or _tile(n, (512, 1408, 256, 128))
    dims = (((0 if ta else 1,), (1 if tb else 0,)), ((), ()))

    def body(*refs):
        if add is None:
            a_ref, b_ref, o_ref = refs
        else:
            a_ref, b_ref, r_ref, o_ref = refs
        acc = lax.dot_general(a_ref[...].astype(BF16), b_ref[...].astype(BF16), dims, preferred_element_type=F32)
        if add is not None:
            acc = acc + r_ref[...]
        o_ref[...] = acc.astype(out_dtype)

    a_spec = pl.BlockSpec((k, tm), lambda i, j: (0, i)) if ta else pl.BlockSpec((tm, k), lambda i, j: (i, 0))
    b_spec = pl.BlockSpec((tn, k), lambda i, j: (j, 0)) if tb else pl.BlockSpec((k, tn), lambda i, j: (0, j))
    o_spec = pl.BlockSpec((tm, tn), lambda i, j: (i, j))
    in_specs, args = [a_spec, b_spec], [a, b]
    if add is not None:
        in_specs.append(o_spec)
        args.append(add)
    return pl.pallas_call(body, out_shape=_sds((m, n), out_dtype), grid=(m // tm, n // tn), in_specs=in_specs,
                          out_specs=o_spec, compiler_params=_params(), name=name)(*args)


ROW_TILE = 256


def norm_fwd(x, g, *, name):
    t, d = x.shape

    def body(x_ref, g_ref, o_ref):
        o_ref[...] = _norm_fn(x_ref[...], g_ref[...]).astype(BF16)

    return pl.pallas_call(body, out_shape=_sds((t, d), BF16), grid=(t // ROW_TILE,),
                          in_specs=[pl.BlockSpec((ROW_TILE, d), lambda i: (i, 0)), pl.BlockSpec((1, d), lambda i: (0, 0))],
                          out_specs=pl.BlockSpec((ROW_TILE, d), lambda i: (i, 0)), compiler_params=_params(), name=name)(x, g)


def norm_bwd(x, g, dy, dres, *, name):
    t, d = x.shape

    def body(x_ref, g_ref, dy_ref, dres_ref, dx_ref, dg_ref):
        _, vjp = jax.vjp(_norm_fn, x_ref[...], g_ref[...])
        dx, dg = vjp(dy_ref[...])
        dx_ref[...] = dx + dres_ref[...]

        @pl.when(pl.program_id(0) == 0)
        def _():
            dg_ref[...] = jnp.zeros_like(dg_ref)

        dg_ref[...] += dg

    row = pl.BlockSpec((ROW_TILE, d), lambda i: (i, 0))
    vec = pl.BlockSpec((1, d), lambda i: (0, 0))
    return pl.pallas_call(body, out_shape=(_sds((t, d), F32), _sds((1, d), F32)), grid=(t // ROW_TILE,),
                          in_specs=[row, vec, row, row], out_specs=(row, vec), compiler_params=_params(), name=name)(x, g, dy, dres)


def final_fwd_bwd(h, g, target, *, name):
    t, d = h.shape

    def body(h_ref, g_ref, t_ref, loss_ref, dh_ref, dg_ref):
        tgt = t_ref[...]
        loss, vjp = jax.vjp(lambda hh, gg: _final_fn(hh, gg, tgt), h_ref[...], g_ref[...])
        dh, dg = vjp(jnp.ones((1, 1), F32))
        dh_ref[...] = dh

        @pl.when(pl.program_id(0) == 0)
        def _():
            dg_ref[...] = jnp.zeros_like(dg_ref)
            loss_ref[...] = jnp.zeros_like(loss_ref)

        dg_ref[...] += dg
        loss_ref[...] += jnp.broadcast_to(loss, loss_ref.shape)

    row = pl.BlockSpec((ROW_TILE, d), lambda i: (i, 0))
    vec = pl.BlockSpec((1, d), lambda i: (0, 0))
    return pl.pallas_call(body, out_shape=(_sds((1, LANES), F32), _sds((t, d), F32), _sds((1, d), F32)), grid=(t // ROW_TILE,),
                          in_specs=[row, vec, row], out_specs=(pl.BlockSpec((1, LANES), lambda i: (0, 0)), row, vec),
                          compiler_params=_params(), name=name)(h, g, target)


FFN_FWD_COLS = 256
FFN_BWD_COLS = 128


def ffn_act_fwd(u, cw, cb, *, name):
    t = u.shape[0]
    w = FFN_FWD_COLS
    nb = D_FF // w

    def body(ug_ref, uv_ref, wg_ref, wv_ref, bg_ref, bv_ref, o_ref):
        o_ref[...] = _ffn_act_fn(ug_ref[...], uv_ref[...], wg_ref[...], wv_ref[...], bg_ref[...], bv_ref[...]).astype(BF16)

    def col(rows, off):
        return pl.BlockSpec((rows, w), lambda j: (0, j + off))

    return pl.pallas_call(body, out_shape=_sds((t, D_FF), BF16), grid=(nb,),
                          in_specs=[col(t, 0), col(t, nb), col(3, 0), col(3, nb), col(1, 0), col(1, nb)],
                          out_specs=col(t, 0), compiler_params=_params(), name=name)(u, u, cw, cw, cb, cb)


def ffn_act_bwd(u, cw, cb, da, *, name):
    t = u.shape[0]
    w = FFN_BWD_COLS
    nb = D_FF // w

    def body(ug_ref, uv_ref, wg_ref, wv_ref, bg_ref, bv_ref, da_ref, dug_ref, duv_ref, dwg_ref, dwv_ref, dbg_ref, dbv_ref):
        _, vjp = jax.vjp(_ffn_act_fn, ug_ref[...], uv_ref[...], wg_ref[...], wv_ref[...], bg_ref[...], bv_ref[...])
        dug, duv, dwg, dwv, dbg, dbv = vjp(da_ref[...])
        dug_ref[...] = dug.astype(BF16)
        duv_ref[...] = duv.astype(BF16)
        dwg_ref[...] = dwg
        dwv_ref[...] = dwv
        dbg_ref[...] = dbg
        dbv_ref[...] = dbv

    def col(rows, off):
        return pl.BlockSpec((rows, w), lambda j: (0, j + off))

    outs = pl.pallas_call(
        body, out_shape=(_sds((t, D_FF), BF16), _sds((t, D_FF), BF16), _sds((3, D_FF), F32), _sds((3, D_FF), F32),
                         _sds((1, D_FF), F32), _sds((1, D_FF), F32)),
        grid=(nb,), in_specs=[col(t, 0), col(t, nb), col(3, 0), col(3, nb), col(1, 0), col(1, nb), col(t, 0)],
        out_specs=(col(t, 0), col(t, 0), col(3, 0), col(3, 0), col(1, 0), col(1, 0)), compiler_params=_params(), name=name,
    )(u, u, cw, cw, cb, cb, da)
    dug, duv, dwg, dwv, dbg, dbv = outs
    return dug, duv, jnp.concatenate([dwg, dwv], axis=1), jnp.concatenate([dbg, dbv], axis=1)


GDN_CONV_COLS = 256
GDN_CONV_OFF = 4 * GROUP


def gdn_conv_fwd(p, cw, *, name):
    t = p.shape[0]
    w = GDN_CONV_COLS
    nb = 3 * GROUP // w
    off = GDN_CONV_OFF // w

    def body(x_ref, w_ref, o_ref):
        o_ref[...] = _gdn_conv_fn(x_ref[...], w_ref[...])

    return pl.pallas_call(body, out_shape=_sds((t, 3 * GROUP), F32), grid=(nb,),
                          in_specs=[pl.BlockSpec((t, w), lambda j: (0, j + off)), pl.BlockSpec((4, w), lambda j: (0, j))],
                          out_specs=pl.BlockSpec((t, w), lambda j: (0, j)), compiler_params=_params(), name=name)(p, cw)


def gdn_conv_bwd(p, cw, dc, *, name):
    t = p.shape[0]
    w = GDN_CONV_COLS
    nb = 3 * GROUP // w
    off = GDN_CONV_OFF // w

    def body(x_ref, w_ref, dc_ref, dx_ref, dw_ref):
        _, vjp = jax.vjp(_gdn_conv_fn, x_ref[...], w_ref[...])
        dx, dw = vjp(dc_ref[...])
        dx_ref[...] = dx.astype(BF16)
        dw_ref[...] = dw

    blk = pl.BlockSpec((t, w), lambda j: (0, j))
    wblk = pl.BlockSpec((4, w), lambda j: (0, j))
    return pl.pallas_call(body, out_shape=(_sds((t, 3 * GROUP), BF16), _sds((4, 3 * GROUP), F32)), grid=(nb,),
                          in_specs=[pl.BlockSpec((t, w), lambda j: (0, j + off)), wblk, blk], out_specs=(blk, wblk),
                          compiler_params=_params(), name=name)(p, cw, dc)


def _lru_specs(t):
    w = D_MODEL // LRU_BLOCKS
    gate = pl.BlockSpec((t, w), lambda j: (0, j))
    xin = pl.BlockSpec((t, w), lambda j: (0, j + LRU_BLOCKS))
    cw = pl.BlockSpec((4, w), lambda j: (0, j))
    vec = pl.BlockSpec((1, w), lambda j: (0, j))
    mat = pl.BlockSpec((None, w, w), lambda j: (j, 0, 0))
    return gate, xin, cw, vec, mat


def lru_fwd(gx, cw, cb, wa, ba, wx, bx, lam, *, name):
    t = gx.shape[0]
    gate, xin, cws, vec, mat = _lru_specs(t)

    def body(g_ref, x_ref, cw_ref, cb_ref, wa_ref, ba_ref, wx_ref, bx_ref, lam_ref, o_ref):
        o_ref[...] = _lru_fn(g_ref[...], x_ref[...], cw_ref[...], cb_ref[...], wa_ref[...], ba_ref[...], wx_ref[...],
                             bx_ref[...], lam_ref[...]).astype(BF16)

    return pl.pallas_call(body, out_shape=_sds((t, D_MODEL), BF16), grid=(LRU_BLOCKS,),
                          in_specs=[gate, xin, cws, vec, mat, vec, mat, vec, vec], out_specs=gate,
                          compiler_params=_params(), name=name)(gx, gx, cw, cb, wa, ba, wx, bx, lam)


def lru_bwd(gx, cw, cb, wa, ba, wx, bx, lam, dy, *, name):
    t = gx.shape[0]
    gate, xin, cws, vec, mat = _lru_specs(t)

    def body(g_ref, x_ref, cw_ref, cb_ref, wa_ref, ba_ref, wx_ref, bx_ref, lam_ref, dy_ref,
             dg_ref, dx_ref, dcw_ref, dcb_ref, dwa_ref, dba_ref, dwx_ref, dbx_ref, dlam_ref):
        _, vjp = jax.vjp(_lru_fn, g_ref[...], x_ref[...], cw_ref[...], cb_ref[...], wa_ref[...], ba_ref[...], wx_ref[...],
                         bx_ref[...], lam_ref[...])
        dg, dx, dcw, dcb, dwa, dba, dwx, dbx, dlam = vjp(dy_ref[...])
        dg_ref[...] = dg.astype(BF16)
        dx_ref[...] = dx.astype(BF16)
        dcw_ref[...] = dcw
        dcb_ref[...] = dcb
        dwa_ref[...] = dwa
        dba_ref[...] = dba
        dwx_ref[...] = dwx
        dbx_ref[...] = dbx
        dlam_ref[...] = dlam

    d = D_MODEL
    w = d // LRU_BLOCKS
    out_shape = (_sds((t, d), BF16), _sds((t, d), BF16), _sds((4, d), F32), _sds((1, d), F32), _sds((LRU_BLOCKS, w, w), F32),
                 _sds((1, d), F32), _sds((LRU_BLOCKS, w, w), F32), _sds((1, d), F32), _sds((1, d), F32))
    return pl.pallas_call(body, out_shape=out_shape, grid=(LRU_BLOCKS,),
                          in_specs=[gate, xin, cws, vec, mat, vec, mat, vec, vec, gate],
                          out_specs=(gate, gate, cws, vec, mat, vec, mat, vec, vec), compiler_params=_params(), name=name,
                          )(gx, gx, cw, cb, wa, ba, wx, bx, lam, dy)


def _ret_tables():
    half = HEAD // 2
    inv_freq = (np.float32(ROPE_BASE) ** (-np.arange(half, dtype=np.float32) / np.float32(half))).astype(np.float32)
    ang = (np.arange(SEQ, dtype=np.float32)[:, None] * inv_freq[None, :]).astype(np.float64)
    cos2 = np.concatenate([np.cos(ang), np.cos(ang)], axis=1).astype(np.float32)
    sin2 = np.concatenate([-np.sin(ang), np.sin(ang)], axis=1).astype(np.float32)
    c = RET_CHUNK
    log_gamma = np.log1p(-np.exp2(-5.0 - np.arange(N_HEADS, dtype=np.float64)))
    idx = np.arange(c, dtype=np.float64)
    rel = idx[:, None] - idx[None, :]
    dmask = np.where(rel >= 0, np.exp(log_gamma[:, None, None] * np.maximum(rel, 0.0)), 0.0)
    ones = np.ones((N_HEADS, c, HEAD))
    ktail = np.exp(log_gamma[:, None] * (c - 1 - idx))[:, :, None] * ones
    qdec = np.exp(log_gamma[:, None] * (idx + 1.0))[:, :, None] * ones
    cdec = np.exp(log_gamma * c)[:, None, None] * ones
    return tuple(jnp.asarray(a, F32) for a in (cos2, sin2, dmask, ktail, qdec, cdec))


def _ret_specs(rev):
    c = RET_CHUNK
    nc = SEQ // c

    def n_of(n):
        return nc - 1 - n if rev else n

    def group(off):
        return pl.BlockSpec((c, GROUP), lambda n: (n_of(n), off))

    tab = pl.BlockSpec((c, HEAD), lambda n: (n_of(n), 0))
    const = pl.BlockSpec((N_HEADS, c, HEAD), lambda n: (0, 0, 0))
    state = pl.BlockSpec((N_HEADS, None, HEAD, HEAD), lambda n: (0, n_of(n), 0, 0))
    return group, tab, const, state, nc


def _head(h):
    return slice(h * HEAD, (h + 1) * HEAD)


def ret_fwd(p, tables, *, name):
    group, tab, const, state, nc = _ret_specs(False)

    def body(q_ref, k_ref, v_ref, g_ref, cos_ref, sin_ref, dm_ref, kt_ref, qd_ref, cd_ref, y_ref, st_ref, s_scr):
        @pl.when(pl.program_id(0) == 0)
        def _():
            s_scr[...] = jnp.zeros_like(s_scr)

        heads = range(N_HEADS)
        states = tuple(s_scr[h] for h in heads)
        ys, new_states = _ret_fn(*(tuple(r[:, _head(h)] for h in heads) for r in (q_ref, k_ref, v_ref, g_ref)), states,
                                 cos_ref[...], sin_ref[...], *(tuple(r[h] for h in heads) for r in (dm_ref, kt_ref, qd_ref, cd_ref)))
        for h in heads:
            st_ref[h] = states[h]
            y_ref[:, _head(h)] = ys[h].astype(BF16)
            s_scr[h] = new_states[h]

    return pl.pallas_call(
        body, out_shape=(_sds((SEQ, GROUP), BF16), _sds((N_HEADS, nc, HEAD, HEAD), F32)), grid=(nc,),
        in_specs=[group(0), group(1), group(2), group(3), tab, tab, const, const, const, const],
        out_specs=(group(0), state), scratch_shapes=[pltpu.VMEM((N_HEADS, HEAD, HEAD), F32)], compiler_params=_params(), name=name,
    )(p, p, p, p, *tables)


def ret_bwd(p, tables, states, dy, *, name):
    group, tab, const, state, nc = _ret_specs(True)

    def body(q_ref, k_ref, v_ref, g_ref, cos_ref, sin_ref, dm_ref, kt_ref, qd_ref, cd_ref, st_ref, dy_ref,
             dq_ref, dk_ref, dv_ref, dg_ref, ds_scr):
        @pl.when(pl.program_id(0) == 0)
        def _():
            ds_scr[...] = jnp.zeros_like(ds_scr)

        heads = range(N_HEADS)
        consts = (cos_ref[...], sin_ref[...], *(tuple(r[h] for h in heads) for r in (dm_ref, kt_ref, qd_ref, cd_ref)))
        _, vjp = jax.vjp(lambda *a: _ret_fn(*a, *consts), *(tuple(r[:, _head(h)] for h in heads) for r in (q_ref, k_ref, v_ref, g_ref)),
                         tuple(st_ref[h] for h in heads))
        dqs, dks, dvs, dgs, dss = vjp((tuple(dy_ref[:, _head(h)] for h in heads), tuple(ds_scr[h] for h in heads)))
        for h in heads:
            dq_ref[:, _head(h)] = dqs[h].astype(BF16)
            dk_ref[:, _head(h)] = dks[h].astype(BF16)
            dv_ref[:, _head(h)] = dvs[h].astype(BF16)
            dg_ref[:, _head(h)] = dgs[h].astype(BF16)
            ds_scr[h] = dss[h]

    out = _sds((SEQ, GROUP), BF16)
    return pl.pallas_call(
        body, out_shape=(out, out, out, out), grid=(nc,),
        in_specs=[group(0), group(1), group(2), group(3), tab, tab, const, const, const, const, state, group(0)],
        out_specs=(group(0), group(0), group(0), group(0)), scratch_shapes=[pltpu.VMEM((N_HEADS, HEAD, HEAD), F32)],
        compiler_params=_params(), name=name,
    )(p, p, p, p, *tables, states, dy)


def _gdn_specs(rev):
    c = GDN_CHUNK
    nc = SEQ // c

    def n_of(n):
        return nc - 1 - n if rev else n

    def group(off):
        return pl.BlockSpec((c, GROUP), lambda n: (n_of(n), off))

    small = pl.BlockSpec((c, LANES), lambda n: (n_of(n), 0))
    vec = pl.BlockSpec((1, LANES), lambda n: (0, 0))
    state = pl.BlockSpec((N_HEADS, None, HEAD, HEAD), lambda n: (0, n_of(n), 0, 0))
    return group, small, vec, state, nc


GDN_GATE_GROUP = 7


def gdn_fwd(conv, p, small, a_log, dt_bias, gain, *, name):
    group, sm, vec, state, nc = _gdn_specs(False)

    def body(q_ref, k_ref, v_ref, g_ref, sm_ref, al_ref, dt_ref, gn_ref, y_ref, st_ref, s_scr):
        @pl.when(pl.program_id(0) == 0)
        def _():
            s_scr[...] = jnp.zeros_like(s_scr)

        states = tuple(s_scr[h] for h in range(N_HEADS))
        ys, new_states = _gdn_fn(*(tuple(r[:, _head(h)] for h in range(N_HEADS)) for r in (q_ref, k_ref, v_ref, g_ref)),
                                 sm_ref[...], al_ref[...], dt_ref[...], gn_ref[...], states)
        for h in range(N_HEADS):
            st_ref[h] = states[h]
            y_ref[:, _head(h)] = ys[h].astype(BF16)
            s_scr[h] = new_states[h]

    return pl.pallas_call(
        body, out_shape=(_sds((SEQ, GROUP), BF16), _sds((N_HEADS, nc, HEAD, HEAD), F32)), grid=(nc,),
        in_specs=[group(0), group(1), group(2), group(GDN_GATE_GROUP), sm, vec, vec, vec], out_specs=(group(0), state),
        scratch_shapes=[pltpu.VMEM((N_HEADS, HEAD, HEAD), F32)], compiler_params=_params(), name=name,
    )(conv, conv, conv, p, small, a_log, dt_bias, gain)


def gdn_bwd(conv, p, small, a_log, dt_bias, gain, states, dy, *, name):
    group, sm, vec, state, nc = _gdn_specs(True)

    def body(q_ref, k_ref, v_ref, g_ref, sm_ref, al_ref, dt_ref, gn_ref, st_ref, dy_ref,
             dq_ref, dk_ref, dv_ref, dg_ref, dsm_ref, dal_ref, ddt_ref, dgn_ref, ds_scr):
        @pl.when(pl.program_id(0) == 0)
        def _():
            ds_scr[...] = jnp.zeros_like(ds_scr)
            dal_ref[...] = jnp.zeros_like(dal_ref)
            ddt_ref[...] = jnp.zeros_like(ddt_ref)
            dgn_ref[...] = jnp.zeros_like(dgn_ref)

        per_head = tuple(tuple(r[:, _head(h)] for h in range(N_HEADS)) for r in (q_ref, k_ref, v_ref, g_ref))
        _, vjp = jax.vjp(_gdn_fn, *per_head, sm_ref[...], al_ref[...], dt_ref[...], gn_ref[...],
                         tuple(st_ref[h] for h in range(N_HEADS)))
        cts = (tuple(dy_ref[:, _head(h)] for h in range(N_HEADS)), tuple(ds_scr[h] for h in range(N_HEADS)))
        dqs, dks, dvs, dgs, dsm, dal, ddt, dgn, dss = vjp(cts)
        for h in range(N_HEADS):
            dq_ref[:, _head(h)] = dqs[h]
            dk_ref[:, _head(h)] = dks[h]
            dv_ref[:, _head(h)] = dvs[h]
            dg_ref[:, _head(h)] = dgs[h].astype(BF16)
            ds_scr[h] = dss[h]
        dsm_ref[...] = dsm
        dal_ref[...] += dal
        ddt_ref[...] += ddt
        dgn_ref[...] += dgn

    f = _sds((SEQ, GROUP), F32)
    pv = _sds((1, LANES), F32)
    return pl.pallas_call(
        body, out_shape=(f, f, f, _sds((SEQ, GROUP), BF16), _sds((SEQ, LANES), F32), pv, pv, pv), grid=(nc,),
        in_specs=[group(0), group(1), group(2), group(GDN_GATE_GROUP), sm, vec, vec, vec, state, group(1)],
        out_specs=(group(0), group(0), group(0), group(0), sm, vec, vec, vec), scratch_shapes=[pltpu.VMEM((N_HEADS, HEAD, HEAD), F32)],
        compiler_params=_params(), name=name,
    )(conv, conv, conv, p, small, a_log, dt_bias, gain, states, dy)


PACK_ROW_TILE = 1024


def adamw(w, g, m, v, *, name):
    r = w.shape[0]
    tr = _tile(r, (PACK_ROW_TILE, 256, 128, 64, 32, 16, 8))

    def body(w_ref, g_ref, m_ref, v_ref, d_ref, nm_ref, nv_ref):
        gg = g_ref[...]
        nm = ADAM_B1 * m_ref[...] + (1.0 - ADAM_B1) * gg
        nv = ADAM_B2 * v_ref[...] + (1.0 - ADAM_B2) * jnp.square(gg)
        m_hat = nm / (1.0 - ADAM_B1 ** ADAM_STEP)
        v_hat = nv / (1.0 - ADAM_B2 ** ADAM_STEP)
        d_ref[...] = -ADAM_LR * (m_hat / (jnp.sqrt(v_hat) + ADAM_EPS) + ADAM_WD * w_ref[...])
        nm_ref[...] = nm
        nv_ref[...] = nv

    blk = pl.BlockSpec((tr, LANES), lambda i: (i, 0))
    o = _sds((r, LANES), F32)
    return pl.pallas_call(body, out_shape=(o, o, o), grid=(r // tr,), in_specs=[blk] * 4, out_specs=(blk, blk, blk),
                          compiler_params=_params(), name=name)(w, g, m, v)


ELEMENTWISE_BLOCK_BYTES = 2 * 1024 * 1024


def _row_tile(r, c):
    best = None
    for tr in range(8, r + 1, 8):
        if r % tr == 0 and tr * c * 4 <= ELEMENTWISE_BLOCK_BYTES:
            best = tr
    if best is None:
        raise ValueError(f"no row tile for ({r}, {c})")
    return best


def _core_index():
    return lax.axis_index("c").astype(jnp.int32).reshape(1)


def _chip_index():
    return (2 * lax.axis_index("x") + lax.axis_index("y")).astype(jnp.int32).reshape(1)


def adamw_halves(w, m, v, g_own, g_sib, *, layer=0, prev=None, name):
    n_layers, rows, c = w.shape
    r = rows // 2
    tr = _row_tile(r, c)
    nb = r // tr

    def body(c_ref, w_ref, m_ref, v_ref, own_ref, sib_ref, *rest):
        g_ref, d_ref, nm_ref, nv_ref = rest[-4:]
        gg = jnp.where(pl.program_id(0) == c_ref[0], own_ref[...], sib_ref[...])
        nm = ADAM_B1 * m_ref[...] + (1.0 - ADAM_B1) * gg
        nv = ADAM_B2 * v_ref[...] + (1.0 - ADAM_B2) * jnp.square(gg)
        m_hat = nm / (1.0 - ADAM_B1 ** ADAM_STEP)
        v_hat = nv / (1.0 - ADAM_B2 ** ADAM_STEP)
        g_ref[...] = gg
        d_ref[...] = -ADAM_LR * (m_hat / (jnp.sqrt(v_hat) + ADAM_EPS) + ADAM_WD * w_ref[...])
        nm_ref[...] = nm
        nv_ref[...] = nv

    full = pl.BlockSpec((None, tr, c), lambda h, i, cr: (layer, h * nb + i, 0))
    half = pl.BlockSpec((tr, c), lambda h, i, cr: (i, 0))
    o = _sds((n_layers, rows, c), F32)
    prev = list(prev or ())
    gs = pltpu.PrefetchScalarGridSpec(num_scalar_prefetch=1, grid=(2, nb), in_specs=[full, full, full, half, half] + [_ANY] * len(prev),
                                      out_specs=(full, full, full, full))
    n_fixed = 6
    return pl.pallas_call(body, out_shape=(o, o, o, o), grid_spec=gs, compiler_params=_params(), name=name,
                          input_output_aliases={n_fixed + k: k for k in range(len(prev))})(
        _core_index(), w, m, v, g_own, g_sib, *prev)


def add_core_halves(g2, land, *, out_dtype, name):
    _, ns, r, cols = g2.shape
    tr = _row_tile(r, cols)

    def body(c_ref, a_ref, b_ref, o_ref):
        o_ref[...] = (a_ref[...] + b_ref[...]).astype(out_dtype)

    gs = pltpu.PrefetchScalarGridSpec(
        num_scalar_prefetch=1, grid=(ns, r // tr),
        in_specs=[pl.BlockSpec((None, None, tr, cols), lambda s, i, cr: (cr[0], s, i, 0)),
                  pl.BlockSpec((None, tr, cols), lambda s, i, cr: (s, i, 0))],
        out_specs=pl.BlockSpec((None, tr, cols), lambda s, i, cr: (s, i, 0)))
    return pl.pallas_call(body, out_shape=_sds((ns, r, cols), out_dtype), grid_spec=gs, compiler_params=_params(), name=name)(
        _core_index(), g2, land)


def sum_over_chips(own, land, *, scatter, name):
    _, r, cols = own.shape
    tr = _row_tile(r, cols)

    def body(mine_ref, own_ref, l0, l1, l2, l3, o_ref):
        mine = mine_ref[0]
        mine_val = own_ref[...]
        acc = None
        for s, l_ref in enumerate((l0, l1, l2, l3)):
            val = jnp.where(mine == s, mine_val, l_ref[...]).astype(F32)
            acc = val if acc is None else acc + val
        o_ref[...] = acc

    def slot(s):
        return pl.BlockSpec((None, tr, cols), lambda i, mr: (jnp.where(mr[0] == s, (s + 1) % N_SHARD, s), i, 0))

    own_spec = pl.BlockSpec((None, tr, cols), lambda i, mr: (mr[0] if scatter else 0, i, 0))
    gs = pltpu.PrefetchScalarGridSpec(num_scalar_prefetch=1, grid=(r // tr,), in_specs=[own_spec] + [slot(s) for s in range(N_SHARD)],
                                      out_specs=pl.BlockSpec((tr, cols), lambda i, mr: (i, 0)))
    return pl.pallas_call(body, out_shape=_sds((r, cols), F32), grid_spec=gs, compiler_params=_params(), name=name)(
        _chip_index(), own, land, land, land, land)


_ANY = pl.BlockSpec(memory_space=pl.ANY)


def xy_exchange(src, *, scatter, name):
    rh = src.shape[1]

    def body(src_ref, land_ref, send_sems, recv_sems, loc_sem):
        x, y, c = lax.axis_index("x"), lax.axis_index("y"), lax.axis_index("c")
        mine = 2 * x + y
        peers = [(1 - x, y), (x, 1 - y), (1 - x, 1 - y)]

        def piece(shard):
            return src_ref.at[shard] if scatter else src_ref.at[c]

        def copy(k, px, py, dst_slot):
            return pltpu.make_async_remote_copy(src_ref=piece(2 * px + py), dst_ref=land_ref.at[dst_slot], send_sem=send_sems.at[k],
                                                recv_sem=recv_sems.at[k], device_id=(px, py, c), device_id_type=MESH)

        keep = pltpu.make_async_copy(piece(mine), land_ref.at[mine], loc_sem)
        keep.start()
        sends = [copy(k, px, py, mine) for k, (px, py) in enumerate(peers)]
        for cp in sends:
            cp.start()
        for cp in sends:
            cp.wait_send()
        for k, (px, py) in enumerate(peers):
            copy(k, px, py, 2 * px + py).wait_recv()
        keep.wait()

    return pl.pallas_call(body, out_shape=_sds((N_SHARD, rh, LANES), src.dtype), in_specs=[_ANY], out_specs=_ANY,
                          scratch_shapes=[pltpu.SemaphoreType.DMA((3,)), pltpu.SemaphoreType.DMA((3,)), pltpu.SemaphoreType.DMA(())],
                          name=name)(src)


def core_exchange(src, *, send_other_half, name):
    def body(src_ref, out_ref, send_sem, recv_sem, loc_sem):
        x, y, c = lax.axis_index("x"), lax.axis_index("y"), lax.axis_index("c")
        if send_other_half:
            cp = pltpu.make_async_remote_copy(src_ref=src_ref.at[1 - c], dst_ref=out_ref, send_sem=send_sem, recv_sem=recv_sem,
                                              device_id=(x, y, 1 - c), device_id_type=MESH)
            cp.start()
            cp.wait_send()
            cp.wait_recv()
        else:
            keep = pltpu.make_async_copy(src_ref, out_ref.at[c], loc_sem)
            keep.start()
            cp = pltpu.make_async_remote_copy(src_ref=src_ref, dst_ref=out_ref.at[c], send_sem=send_sem, recv_sem=recv_sem,
                                              device_id=(x, y, 1 - c), device_id_type=MESH)
            cp.start()
            cp.wait_send()
            pltpu.make_async_remote_copy(src_ref=src_ref, dst_ref=out_ref.at[1 - c], send_sem=send_sem, recv_sem=recv_sem,
                                         device_id=(x, y, 1 - c), device_id_type=MESH).wait_recv()
            keep.wait()

    out_shape = _sds(src.shape[1:], src.dtype) if send_other_half else _sds((2,) + src.shape, src.dtype)
    return pl.pallas_call(body, out_shape=out_shape, in_specs=[_ANY], out_specs=_ANY,
                          scratch_shapes=[pltpu.SemaphoreType.DMA(()), pltpu.SemaphoreType.DMA(()), pltpu.SemaphoreType.DMA(())],
                          name=name)(src)


def _comm_call(body, ins, out_shapes, sem_counts, name):
    return pl.pallas_call(body, out_shape=tuple(out_shapes), in_specs=[_ANY] * len(ins), out_specs=tuple([_ANY] * len(out_shapes)),
                          scratch_shapes=[pltpu.SemaphoreType.DMA((k,)) for k in sem_counts], name=name)(*ins)


def _sequencer_call(body, ins, out_shapes, sem_counts, name, collective_id):
    return pl.kernel(body, out_type=list(out_shapes), mesh=plsc.ScalarSubcoreMesh(axis_name="sequencer", num_cores=1), name=name,
                     scratch_types=[pltpu.SemaphoreType.DMA((k,)) for k in sem_counts],
                     compiler_params=pltpu.CompilerParams(collective_id=collective_id))(*ins)


def _handshake(peers):
    barrier = pltpu.get_barrier_semaphore()
    for peer in peers:
        pl.semaphore_signal(barrier, inc=1, device_id=peer, device_id_type=MESH)
    pl.semaphore_wait(barrier, len(peers))


def _xy_peers(x, y):
    return [(1 - x, y), (x, 1 - y), (1 - x, 1 - y)]


def gather_halves(halves, *, name, collective_id):
    n = len(halves)

    def body(*refs):
        ins, lands, sibs = refs[:n], refs[n:2 * n], refs[2 * n:3 * n]
        ici_send, ici_recv, d2d_send, d2d_recv = refs[3 * n:]
        x, y, c = lax.axis_index("x"), lax.axis_index("y"), lax.axis_index("c")
        mine = 2 * x + y
        peers = _xy_peers(x, y)
        _handshake([(px, py, c) for px, py in peers] + [(x, y, 1 - c)])

        def ici(i, k, slot):
            px, py = peers[k]
            return pltpu.make_async_remote_copy(src_ref=ins[i].at[c], dst_ref=lands[i].at[slot], send_sem=ici_send.at[3 * i + k],
                                                recv_sem=ici_recv.at[3 * i + k], device_id=(px, py, c), device_id_type=MESH)

        def pass_on(i, k):
            px, py = peers[k]
            slot = 2 * px + py
            return pltpu.make_async_remote_copy(src_ref=lands[i].at[slot], dst_ref=sibs[i].at[slot], send_sem=d2d_send.at[3 * i + k],
                                                recv_sem=d2d_recv.at[3 * i + k], device_id=(x, y, 1 - c), device_id_type=MESH)

        sends = [ici(i, k, mine) for i in range(n) for k in range(3)]
        for cp in sends:
            cp.start()
        passed = []
        for i in range(n):
            for k in range(3):
                px, py = peers[k]
                ici(i, k, 2 * px + py).wait_recv()
                cp = pass_on(i, k)
                cp.start()
                passed.append(cp)
        for cp in passed:
            cp.wait_recv()
        for cp in sends + passed:
            cp.wait_send()

    outs = [_sds((N_SHARD,) + h.shape[1:], h.dtype) for h in halves]
    res = _sequencer_call(body, halves, outs + outs, [3 * n] * 4, name, collective_id)
    return res[:n], res[n:]


def send_other_half(arrays, *, name, collective_id):
    n = len(arrays)

    def body(*refs):
        ins, lands = refs[:n], refs[n:2 * n]
        send_sems, recv_sems = refs[2 * n:]
        x, y, c = lax.axis_index("x"), lax.axis_index("y"), lax.axis_index("c")
        _handshake([(x, y, 1 - c)])
        copies = [pltpu.make_async_remote_copy(src_ref=ins[i].at[1 - c], dst_ref=lands[i], send_sem=send_sems.at[i],
                                               recv_sem=recv_sems.at[i], device_id=(x, y, 1 - c), device_id_type=MESH) for i in range(n)]
        for cp in copies:
            cp.start()
        for cp in copies:
            cp.wait_recv()
        for cp in copies:
            cp.wait_send()

    return _sequencer_call(body, arrays, [_sds(a.shape[1:], a.dtype) for a in arrays], [n, n], name, collective_id)


_HBM = pl.BlockSpec(memory_space=pltpu.HBM)
_SEM = pl.BlockSpec(memory_space=pltpu.SEMAPHORE)
_SPLIT_COPY = dict(has_side_effects=pltpu.SideEffectType.DATAFLOW_SIDE_EFFECTING)


def _chip_copy(ins, lands, send_sems, recv_sems, scatter, i, k, receive):
    x, y, c = lax.axis_index("x"), lax.axis_index("y"), lax.axis_index("c")
    px, py = _xy_peers(x, y)[k]
    theirs, mine = 2 * px + py, 2 * x + y
    src = ins[i].at[theirs] if scatter[i] else ins[i].at[0]
    return pltpu.make_async_remote_copy(src_ref=src, dst_ref=lands[i].at[theirs if receive else mine], send_sem=send_sems.at[3 * i + k],
                                        recv_sem=recv_sems.at[3 * i + k], device_id=(px, py, c), device_id_type=MESH)


def send_to_chips_start(arrays, scatter, *, name):
    n = len(arrays)

    def body(*refs):
        send_sems, recv_sems = refs[2 * n], refs[2 * n + 1]
        ins, lands = refs[2 * n + 2:3 * n + 2], refs[3 * n + 2:4 * n + 2]
        token = refs[4 * n + 2]
        for i in range(n):
            for k in range(3):
                _chip_copy(ins, lands, send_sems, recv_sems, scatter, i, k, receive=False).start()
        token[...] = jnp.zeros_like(token)

    land_shapes = [(N_SHARD,) + a.shape[1:] for a in arrays]
    operands = [pltpu.with_memory_space_constraint(a, pltpu.HBM) for a in arrays]
    operands += [pltpu.with_memory_space_constraint(lax.empty(s, a.dtype), pltpu.HBM) for s, a in zip(land_shapes, arrays)]
    out_shape = ([pltpu.SemaphoreType.DMA((3 * n,)), pltpu.SemaphoreType.DMA((3 * n,))] + [pltpu.HBM(a.shape, a.dtype) for a in arrays]
                 + [pltpu.HBM(s, a.dtype) for s, a in zip(land_shapes, arrays)] + [_sds((8, LANES), F32)])
    res = pl.pallas_call(body, name=name, out_shape=out_shape, in_specs=[_HBM] * (2 * n),
                         out_specs=[_SEM, _SEM] + [_HBM] * (2 * n) + [pl.BlockSpec(memory_space=pltpu.VMEM)],
                         input_output_aliases={i: 2 + i for i in range(2 * n)}, compiler_params=pltpu.CompilerParams(**_SPLIT_COPY))(*operands)
    return (res[0], res[1], res[2:2 + n], res[2 + n:2 + 2 * n], scatter), res[-1]


def send_to_chips_wait(state, after, *, name):
    send_sems, recv_sems, arrays, lands, scatter = state
    n = len(arrays)

    def body(*refs):
        ins, landing = refs[:n], refs[n:2 * n]
        send_sems, recv_sems = refs[2 * n], refs[2 * n + 1]
        for i in range(n):
            for k in range(3):
                _chip_copy(ins, landing, send_sems, recv_sems, scatter, i, k, receive=True).wait_recv()
        for i in range(n):
            for k in range(3):
                _chip_copy(ins, landing, send_sems, recv_sems, scatter, i, k, receive=False).wait_send()

    out_shape = [pltpu.HBM(a.shape, a.dtype) for a in list(arrays) + list(lands)]
    res = pl.pallas_call(body, name=name, out_shape=out_shape, in_specs=[_HBM] * (2 * n) + [_SEM, _SEM, _ANY], out_specs=[_HBM] * (2 * n),
                         input_output_aliases={i: i for i in range(2 * n)}, compiler_params=pltpu.CompilerParams(**_SPLIT_COPY))(
        *arrays, *lands, send_sems, recv_sems, after)
    return res[:n], res[n:]


def swap_with_other_core(arrays, *, name, collective_id):
    n = len(arrays)

    def body(*refs):
        ins, lands = refs[:n], refs[n:2 * n]
        send_sems, recv_sems = refs[2 * n:]
        x, y, c = lax.axis_index("x"), lax.axis_index("y"), lax.axis_index("c")
        _handshake([(x, y, 1 - c)])
        copies = [pltpu.make_async_remote_copy(src_ref=ins[i], dst_ref=lands[i], send_sem=send_sems.at[i], recv_sem=recv_sems.at[i],
                                               device_id=(x, y, 1 - c), device_id_type=MESH) for i in range(n)]
        for cp in copies:
            cp.start()
        for cp in copies:
            cp.wait_recv()
        for cp in copies:
            cp.wait_send()

    return _sequencer_call(body, arrays, [_sds(a.shape, a.dtype) for a in arrays], [n, n], name, collective_id)


def _pack_rows(n_elems, row_multiple):
    rows = -(-n_elems // LANES)
    return -(-rows // row_multiple) * row_multiple


def _pack(arrays, rows, dtype):
    flat = jnp.concatenate([a.reshape(-1).astype(dtype) for a in arrays])
    return jnp.pad(flat, (0, rows * LANES - flat.shape[0])).reshape(rows, LANES)


def _unpack(packed, shapes):
    flat = packed.reshape(-1)
    out, off = [], 0
    for s in shapes:
        n = int(np.prod(s))
        out.append(flat[off:off + n].reshape(s))
        off += n
    return out


def all_gather_shards(shards, axes, dtype, row_multiple, tag):
    shapes = [s.shape for s in shards]
    rows = _pack_rows(sum(int(np.prod(s)) for s in shapes), row_multiple)
    packed = _pack(shards, rows, dtype).reshape(2, rows // 2, LANES)
    land = xy_exchange(packed, scatter=False, name=f"gather_xy_{tag}")
    both = core_exchange(land, send_other_half=False, name=f"gather_c_{tag}")
    per_shard = jnp.swapaxes(both, 0, 1).reshape(N_SHARD, rows, LANES)
    pieces = [_unpack(per_shard[s], shapes) for s in range(N_SHARD)]
    return [jnp.concatenate([pieces[s][i] for s in range(N_SHARD)], axis=ax) for i, ax in enumerate(axes)]


def _ordered_before(first, then):
    if then is None:
        return first, None
    return lax.optimization_barrier((first, then))


def reduce_between_cores(arrays, scatter, *, tag, collective_id, before=None):
    arrays, before = _ordered_before(arrays, before)
    land = send_other_half(arrays, name=f"reduce_core_send_{tag}", collective_id=collective_id)
    return (arrays, land, scatter, tag, collective_id), before


def reduce_between_chips(state, before=None):
    arrays, land, scatter, tag, collective_id = state
    chip = [add_core_halves(a, l, out_dtype=BF16 if sc else F32, name=f"reduce_core_add_{tag}_{i}")
            for i, (a, l, sc) in enumerate(zip(arrays, land, scatter))]
    sending, token = send_to_chips_start(chip, scatter, name=f"reduce_chip_start_{tag}")
    _, before = _ordered_before(token, before)
    return (sending, scatter, tag, collective_id), before


def reduce_finish(state, after):
    sending, scatter, tag, collective_id = state
    chip, land = send_to_chips_wait(sending, after, name=f"reduce_chip_wait_{tag}")
    own = [sum_over_chips(ch, l, scatter=sc, name=f"reduce_chip_add_{tag}_{i}") for i, (ch, l, sc) in enumerate(zip(chip, land, scatter))]
    sib = swap_with_other_core(own, name=f"reduce_core_swap_{tag}", collective_id=collective_id + 2)
    return own, sib


def _ffn_layer_fwd(h, norm_g, w_up, cw, cb, w_down, tag):
    hn = norm_fwd(h, norm_g, name=f"ffn_norm_{tag}")
    u = matmul(hn, w_up, name=f"ffn_up_{tag}")
    act = ffn_act_fwd(u, cw, cb, name=f"ffn_act_{tag}")
    out = matmul(act, w_down, add=h, name=f"ffn_down_{tag}")
    return out, (h, hn, u, act)


def _ffn_layer_bwd(saved, dout, norm_g, w_up, cw, cb, w_down, tag):
    h, hn, u, act = saved
    dact = matmul(dout, w_down, tb=True, name=f"ffn_down_dx_{tag}")
    d_w_down = matmul(act, dout, ta=True, name=f"ffn_down_dw_{tag}")
    dug, duv, dcw, dcb = ffn_act_bwd(u, cw, cb, dact, name=f"ffn_act_bwd_{tag}")
    du = jnp.concatenate([dug, duv], axis=1)
    dhn = matmul(du, w_up, tb=True, name=f"ffn_up_dx_{tag}")
    d_w_up = matmul(hn, du, ta=True, name=f"ffn_up_dw_{tag}")
    dh, dg = norm_bwd(h, norm_g, dhn, dout, name=f"ffn_norm_bwd_{tag}")
    return dh, dg, d_w_up, dcw, dcb, d_w_down


def local_step(x, target, w, stage=lambda name, tensors, grads=None: tensors):
    g = {}
    tables = _ret_tables()
    x = stage("start", x)
    w_in = w["ret_gdn_w_in"]
    w_main = w_in[:, :MIX_MAIN]
    w_small = jnp.pad(w_in[:, MIX_MAIN:], ((0, 0), (0, LANES - 2 * N_HEADS)))
    a_log = jnp.pad(w["gdn_a_log"], ((0, 0), (0, LANES - N_HEADS)))
    dt_bias = jnp.pad(w["gdn_dt_bias"], ((0, 0), (0, LANES - N_HEADS)))

    hn0 = stage("normed", norm_fwd(x, w["norm_mix"][0:1], name="mix0_norm"))
    p = matmul(hn0, w_main, name="mix0_in")
    small = matmul(hn0, w_small, name="mix0_in_small")
    y_ret, s_ret = ret_fwd(p, tables, name="ret_fwd")
    conv = gdn_conv_fwd(p, w["gdn_conv_w"], name="gdn_conv")
    y_gdn, s_gdn = gdn_fwd(conv, p, small, a_log, dt_bias, w["gdn_out_gain"], name="gdn_fwd")
    y0 = stage("mixed", jnp.concatenate([y_ret, y_gdn], axis=1))
    h1 = matmul(y0, w["ret_gdn_w_out"], add=x, name="mix0_out")
    h2, ffn0 = _ffn_layer_fwd(h1, w["norm_ffn"][0:1], w["ffn_w_up"][0], w["ffn_conv_w"][0], w["ffn_conv_b"][0:1], w["ffn_w_down"][0], "0")
    h2 = stage("layer0", h2)

    hn1 = norm_fwd(h2, w["norm_mix"][1:2], name="mix1_norm")
    gx = matmul(hn1, w["lru_w_in"], name="mix1_in")
    lru_p = (w["lru_conv_w"], w["lru_conv_b"], w["lru_w_a"], w["lru_b_a"], w["lru_w_x"], w["lru_b_x"], w["lru_lambda"])
    y1 = lru_fwd(gx, *lru_p, name="lru_fwd")
    h3 = matmul(y1, w["lru_w_out"], add=h2, name="mix1_out")
    h4, ffn1 = _ffn_layer_fwd(h3, w["norm_ffn"][1:2], w["ffn_w_up"][1], w["ffn_conv_w"][1], w["ffn_conv_b"][1:2], w["ffn_w_down"][1], "1")

    loss, dh4, g["norm_final"] = final_fwd_bwd(h4, w["norm_final"], target, name="final")

    dh3, dgf1, dwu1, dcw1, dcb1, dwd1 = _ffn_layer_bwd(ffn1, dh4, w["norm_ffn"][1:2], w["ffn_w_up"][1], w["ffn_conv_w"][1],
                                                     w["ffn_conv_b"][1:2], w["ffn_w_down"][1], "1")
    g["ffn_w_up_1"] = dwu1
    dh3 = stage("grads0_ready", dh3, g)
    dy1 = matmul(dh3, w["lru_w_out"], tb=True, name="mix1_out_dx")
    g["lru_w_out"] = matmul(y1, dh3, ta=True, name="mix1_out_dw")
    dgate, dxr, g["lru_conv_w"], g["lru_conv_b"], g["lru_w_a"], g["lru_b_a"], g["lru_w_x"], g["lru_b_x"], g["lru_lambda"] = lru_bwd(
        gx, *lru_p, dy1, name="lru_bwd")
    dgx = stage("grads0_send", jnp.concatenate([dgate, dxr], axis=1), g)
    dhn1 = matmul(dgx, w["lru_w_in"], tb=True, name="mix1_in_dx")
    g["lru_w_in"] = matmul(hn1, dgx, ta=True, name="mix1_in_dw")
    dh2, dgm1 = norm_bwd(h2, w["norm_mix"][1:2], dhn1, dh3, name="mix1_norm_bwd")
    dh2 = stage("grads1_ready", dh2, g)

    dh1, dgf0, dwu0, dcw0, dcb0, dwd0 = _ffn_layer_bwd(ffn0, dh2, w["norm_ffn"][0:1], w["ffn_w_up"][0], w["ffn_conv_w"][0],
                                                     w["ffn_conv_b"][0:1], w["ffn_w_down"][0], "0")
    g["ffn_w_up_0"] = dwu0
    g["ffn_w_down"] = jnp.stack([dwd0, dwd1])
    dh1 = stage("grads2_ready", stage("grads1_send", dh1, g), g)
    dy0 = matmul(dh1, w["ret_gdn_w_out"], tb=True, name="mix0_out_dx")
    g["ret_gdn_w_out"] = matmul(y0, dh1, ta=True, name="mix0_out_dw")
    dq_r, dk_r, dv_r, dg_r = ret_bwd(p, tables, s_ret, dy0, name="ret_bwd")
    dy0, dq_r = stage("grads2_send", (dy0, dq_r), g)
    dcq, dck, dcv, dg_d, dsmall, dal, ddt, dgain = gdn_bwd(conv, p, small, a_log, dt_bias, w["gdn_out_gain"], s_gdn, dy0, name="gdn_bwd")
    dconv = jnp.concatenate([dcq, dck, dcv], axis=1)
    dp_conv, g["gdn_conv_w"] = gdn_conv_bwd(p, w["gdn_conv_w"], dconv, name="gdn_conv_bwd")
    dp = jnp.concatenate([dq_r, dk_r, dv_r, dg_r, dp_conv, dg_d], axis=1)
    dhn0 = matmul(dp, w_main, tb=True, name="mix0_in_dx")
    dhn0 = matmul(dsmall, w_small, tb=True, add=dhn0, name="mix0_in_small_dx")
    d_w_main = matmul(hn0, dp, ta=True, name="mix0_in_dw")
    d_w_small = matmul(hn0, dsmall, ta=True, name="mix0_in_small_dw")
    g["ret_gdn_w_in"] = jnp.concatenate([d_w_main, d_w_small[:, :2 * N_HEADS]], axis=1)
    dx, dgm0 = norm_bwd(x, w["norm_mix"][0:1], dhn0, dh1, name="mix0_norm_bwd")

    g["gdn_a_log"] = dal[:, :N_HEADS]
    g["gdn_dt_bias"] = ddt[:, :N_HEADS]
    g["gdn_out_gain"] = dgain
    g["norm_mix"] = jnp.concatenate([dgm0, dgm1], axis=0)
    g["norm_ffn"] = jnp.concatenate([dgf0, dgf1], axis=0)
    g["ffn_conv_w"] = jnp.stack([dcw0, dcw1])
    g["ffn_conv_b"] = jnp.concatenate([dcb0, dcb1], axis=0)
    return loss, dx, g


WEIGHTS = ("norm_mix", "norm_ffn", "ret_gdn_w_in", "gdn_conv_w", "gdn_a_log", "gdn_dt_bias", "gdn_out_gain", "ret_gdn_w_out",
           "lru_w_in", "lru_conv_w", "lru_conv_b", "lru_w_a", "lru_b_a", "lru_w_x", "lru_b_x", "lru_lambda", "lru_w_out",
           "ffn_w_up", "ffn_conv_w", "ffn_conv_b", "ffn_w_down", "norm_final")
MATMUL_SHARDED = {"ret_gdn_w_in": 1, "ret_gdn_w_out": 0, "lru_w_in": 1, "lru_w_out": 0, "ffn_w_up": 2, "ffn_w_down": 1}
VECTOR_SHARDED = {"gdn_conv_w": 1, "lru_conv_w": 1, "lru_conv_b": 1, "lru_b_a": 1, "lru_b_x": 1, "lru_lambda": 1, "ffn_conv_w": 2}
SHARDED = {**MATMUL_SHARDED, **VECTOR_SHARDED}
REPLICATED = tuple(n for n in WEIGHTS if n not in SHARDED)
SQUEEZE = {"ret_gdn_w_in", "gdn_conv_w", "ret_gdn_w_out", "lru_w_in", "lru_conv_w", "lru_w_a", "lru_w_x", "lru_w_out"}
MIX_IN = MIX_MAIN + 2 * N_HEADS
BIG_ARRAYS = {
    "ret_gdn_w_in": ("ret_gdn_w_in", None, (D_MODEL, MIX_IN), (2, D_MODEL // 2, N_SHARD, MIX_IN // N_SHARD), (0, 2, 1, 3)),
    "ret_gdn_w_out": ("ret_gdn_w_out", None, (2 * GROUP, D_MODEL), (N_SHARD, 2, GROUP // N_SHARD, D_MODEL), (1, 0, 2, 3)),
    "lru_w_in": ("lru_w_in", None, (D_MODEL, 2 * D_MODEL), (2, D_MODEL // 2, N_SHARD, 2 * D_MODEL // N_SHARD), (0, 2, 1, 3)),
    "lru_w_out": ("lru_w_out", None, (D_MODEL, D_MODEL), (N_SHARD, 2, D_MODEL // (2 * N_SHARD), D_MODEL), (1, 0, 2, 3)),
    "ffn_w_up_0": ("ffn_w_up", 0, (D_MODEL, 2 * D_FF), (2, D_MODEL // 2, N_SHARD, 2 * D_FF // N_SHARD), (0, 2, 1, 3)),
    "ffn_w_up_1": ("ffn_w_up", 1, (D_MODEL, 2 * D_FF), (2, D_MODEL // 2, N_SHARD, 2 * D_FF // N_SHARD), (0, 2, 1, 3)),
    "ffn_w_down": ("ffn_w_down", None, (2, D_FF, D_MODEL), (2, N_SHARD, D_FF // N_SHARD, D_MODEL), (0, 1, 2, 3)),
}
GATHER_GROUPS = (("ret_gdn_w_in",), ("ret_gdn_w_out", "ffn_w_up_0", "ffn_w_down"), ("lru_w_in", "lru_w_out", "ffn_w_up_1"))
REDUCE_GROUPS = (("ffn_w_up_1",), ("lru_w_in", "lru_w_out"), ("ffn_w_up_0", "ffn_w_down"), ("ret_gdn_w_out", "ret_gdn_w_in"))
GATHER_COLLECTIVE_ID = 1
REDUCE_COLLECTIVE_ID = GATHER_COLLECTIVE_ID + len(GATHER_GROUPS)


def _local_view(name, a):
    if name in SQUEEZE:
        return a[0]
    if a.ndim == 1:
        return a[None, :]
    return a


def kernel(x, norm_mix, norm_ffn, ret_gdn_w_in, gdn_conv_w, gdn_a_log, gdn_dt_bias, gdn_out_gain, ret_gdn_w_out, lru_w_in, lru_conv_w, lru_conv_b, lru_w_a, lru_b_a, lru_w_x, lru_b_x, lru_lambda, lru_w_out, ffn_w_up, ffn_conv_w, ffn_conv_b, ffn_w_down, norm_final, loss_target, m_norm_mix, m_norm_ffn, m_ret_gdn_w_in, m_gdn_conv_w, m_gdn_a_log, m_gdn_dt_bias, m_gdn_out_gain, m_ret_gdn_w_out, m_lru_w_in, m_lru_conv_w, m_lru_conv_b, m_lru_w_a, m_lru_b_a, m_lru_w_x, m_lru_b_x, m_lru_lambda, m_lru_w_out, m_ffn_w_up, m_ffn_conv_w, m_ffn_conv_b, m_ffn_w_down, m_norm_final, v_norm_mix, v_norm_ffn, v_ret_gdn_w_in, v_gdn_conv_w, v_gdn_a_log, v_gdn_dt_bias, v_gdn_out_gain, v_ret_gdn_w_out, v_lru_w_in, v_lru_conv_w, v_lru_conv_b, v_lru_w_a, v_lru_b_a, v_lru_w_x, v_lru_b_x, v_lru_lambda, v_lru_w_out, v_ffn_w_up, v_ffn_conv_w, v_ffn_conv_b, v_ffn_w_down, v_norm_final):
    given = dict(norm_mix=norm_mix, norm_ffn=norm_ffn, ret_gdn_w_in=ret_gdn_w_in, gdn_conv_w=gdn_conv_w, gdn_a_log=gdn_a_log, gdn_dt_bias=gdn_dt_bias, gdn_out_gain=gdn_out_gain, ret_gdn_w_out=ret_gdn_w_out, lru_w_in=lru_w_in, lru_conv_w=lru_conv_w, lru_conv_b=lru_conv_b, lru_w_a=lru_w_a, lru_b_a=lru_b_a, lru_w_x=lru_w_x, lru_b_x=lru_b_x, lru_lambda=lru_lambda, lru_w_out=lru_w_out, ffn_w_up=ffn_w_up, ffn_conv_w=ffn_conv_w, ffn_conv_b=ffn_conv_b, ffn_w_down=ffn_w_down, norm_final=norm_final)
    mom1 = dict(norm_mix=m_norm_mix, norm_ffn=m_norm_ffn, ret_gdn_w_in=m_ret_gdn_w_in, gdn_conv_w=m_gdn_conv_w, gdn_a_log=m_gdn_a_log, gdn_dt_bias=m_gdn_dt_bias, gdn_out_gain=m_gdn_out_gain, ret_gdn_w_out=m_ret_gdn_w_out, lru_w_in=m_lru_w_in, lru_conv_w=m_lru_conv_w, lru_conv_b=m_lru_conv_b, lru_w_a=m_lru_w_a, lru_b_a=m_lru_b_a, lru_w_x=m_lru_w_x, lru_b_x=m_lru_b_x, lru_lambda=m_lru_lambda, lru_w_out=m_lru_w_out, ffn_w_up=m_ffn_w_up, ffn_conv_w=m_ffn_conv_w, ffn_conv_b=m_ffn_conv_b, ffn_w_down=m_ffn_w_down, norm_final=m_norm_final)
    mom2 = dict(norm_mix=v_norm_mix, norm_ffn=v_norm_ffn, ret_gdn_w_in=v_ret_gdn_w_in, gdn_conv_w=v_gdn_conv_w, gdn_a_log=v_gdn_a_log, gdn_dt_bias=v_gdn_dt_bias, gdn_out_gain=v_gdn_out_gain, ret_gdn_w_out=v_ret_gdn_w_out, lru_w_in=v_lru_w_in, lru_conv_w=v_lru_conv_w, lru_conv_b=v_lru_conv_b, lru_w_a=v_lru_w_a, lru_b_a=v_lru_b_a, lru_w_x=v_lru_w_x, lru_b_x=v_lru_b_x, lru_lambda=v_lru_lambda, lru_w_out=v_lru_w_out, ffn_w_up=v_ffn_w_up, ffn_conv_w=v_ffn_conv_w, ffn_conv_b=v_ffn_conv_b, ffn_w_down=v_ffn_w_down, norm_final=v_norm_final)

    local = {n: _local_view(n, a) for n, a in given.items()}

    core = lax.axis_index("c")
    chip = 2 * lax.axis_index("x") + lax.axis_index("y")
    is_my_chip = lax.broadcasted_iota(jnp.int32, (N_SHARD, 1, 1), 0) == chip

    def by_core(mine, other):
        return jnp.where(core == 0, jnp.stack([mine, other]), jnp.stack([other, mine]))

    vec_names, rp_names = list(VECTOR_SHARDED), list(REPLICATED)
    full = dict(zip(vec_names, all_gather_shards([local[n] for n in vec_names], [SHARDED[n] for n in vec_names], F32, 32, "p")))
    for n in rp_names:
        full[n] = local[n]
    in_flight = {}

    def launch(gi, after=None):
        halves = []
        for a in GATHER_GROUPS[gi]:
            weight, layer, _, split, perm = BIG_ARRAYS[a]
            shard = local[weight] if layer is None else local[weight][layer]
            halves.append(shard.astype(BF16).reshape((2,) + tuple(split[p] for p in perm)[2:]))
        if after is not None:
            halves, after = lax.optimization_barrier((halves, after))
        in_flight[gi] = (halves,) + gather_halves(halves, name=f"gather_weights_{gi}", collective_id=GATHER_COLLECTIVE_ID + gi)
        return after

    def land(gi, after):
        halves, lands, sibs = in_flight[gi]
        (lands, sibs), after = lax.optimization_barrier(((lands, sibs), after))
        for a, mine, got, passed in zip(GATHER_GROUPS[gi], halves, lands, sibs):
            weight, layer, full_shape, split, perm = BIG_ARRAYS[a]
            half_mine = jnp.where(is_my_chip, jnp.where(core == 0, mine[0], mine[1])[None], got)
            half_other = jnp.where(is_my_chip, jnp.where(core == 0, mine[1], mine[0])[None], passed)
            value = by_core(half_mine, half_other).transpose(perm).reshape(full_shape)
            if layer is None:
                full[weight] = value
            else:
                full.setdefault(weight, [None, None])[layer] = value
        return after

    reducing = {}

    def reduce_ready(gi, grads, then=None, extra=()):
        arrays = [grads[a].reshape(BIG_ARRAYS[a][3]).transpose(BIG_ARRAYS[a][4]) for a in REDUCE_GROUPS[gi]] + list(extra)
        scatter = [True] * len(REDUCE_GROUPS[gi]) + [False] * len(extra)
        reducing[gi], then = reduce_between_cores(arrays, scatter, tag=str(gi), collective_id=REDUCE_COLLECTIVE_ID + 3 * gi, before=then)
        return then

    def reduce_send(gi, then=None):
        reducing[gi], then = reduce_between_chips(reducing[gi], before=then)
        return then

    def stage(name, tensors, grads=None):
        if name == "start":
            launch(0)
            launch(1)
            return land(0, tensors)
        if name == "normed":
            return launch(2, tensors)
        if name in ("mixed", "layer0"):
            return land({"mixed": 1, "layer0": 2}[name], tensors)
        gi = int(name[len("grads")])
        return reduce_ready(gi, grads, tensors) if name.endswith("_ready") else reduce_send(gi, tensors)

    loss_part, dx, grads = local_step(x[0], loss_target[0], full, stage)
    loss = lax.psum(loss_part[0, 0], ("x", "y", "c"))

    small_names = rp_names + vec_names
    small_shapes = [grads[n].shape for n in small_names]
    small_rows = _pack_rows(sum(int(np.prod(s)) for s in small_shapes), 16)
    small = _pack([grads[n] for n in small_names], small_rows, F32).reshape(2, 1, small_rows // 2, LANES)
    last = len(REDUCE_GROUPS) - 1
    reduce_ready(last, grads, extra=[small])
    reduce_send(last)
    reduced, result = {}, {}

    def finish(gi, after):
        g_own, g_sib = reduce_finish(reducing[gi], after)
        reduced.update(zip(list(REDUCE_GROUPS[gi]) + ["small"], zip(g_own, g_sib)))

    def update(n):
        done = None
        for a in (k for k, spec in BIG_ARRAYS.items() if spec[0] == n):
            r, cols = reduced[a][0].shape
            layer = BIG_ARRAYS[a][1] or 0
            w3, m3, v3 = (t if BIG_ARRAYS[a][1] is not None else t.reshape(1, 2 * r, cols) for t in (given[n], mom1[n], mom2[n]))
            done = adamw_halves(w3, m3, v3, *reduced[a], layer=layer, prev=done, name=f"adamw_{a}")
        result[n] = done

    for gi in range(last):
        finish(gi, dx)
    late = {BIG_ARRAYS[a][0] for a in REDUCE_GROUPS[last]}
    for n in MATMUL_SHARDED:
        if n not in late:
            update(n)
    finish(last, result[next(n for n in reversed(MATMUL_SHARDED) if n not in late)][0])
    for n in MATMUL_SHARDED:
        if n in late:
            update(n)

    g_small = dict(zip(small_names, _unpack(by_core(*reduced["small"]).reshape(small_rows, LANES), small_shapes)))
    for n in vec_names:
        size = local[n].shape[SHARDED[n]]
        g_small[n] = lax.dynamic_slice_in_dim(g_small[n], chip * size, size, axis=SHARDED[n])
    loc_shapes = [local[n].shape for n in small_names]
    loc_rows = _pack_rows(sum(int(np.prod(s)) for s in loc_shapes), 256)
    packs = [_pack([src[n] for n in small_names], loc_rows, F32) for src in (given, g_small, mom1, mom2)]
    d_s, m_s, v_s = adamw(*packs, name="adamw_small")
    for n, d, nm, nv in zip(small_names, _unpack(d_s, loc_shapes), _unpack(m_s, loc_shapes), _unpack(v_s, loc_shapes)):
        result[n] = (g_small[n], d, nm, nv)

    outs = [[result[n][k].reshape(given[n].shape) for n in WEIGHTS] for k in range(4)]
    return (loss, dx[None], *outs[0], *outs[1], *outs[2], *outs[3])
```

```python
import functools

import numpy as np
import jax
import jax.numpy as jnp
from jax import lax
from jax.experimental import pallas as pl
from jax.experimental.pallas import tpu as pltpu
from jax.experimental.pallas import tpu_sc as plsc

F32 = jnp.float32
BF16 = jnp.bfloat16
HI = lax.Precision.HIGHEST
MESH = pl.DeviceIdType.MESH

SEQ = 2048
D_MODEL = 1024
N_HEADS = 4
HEAD = 128
RET_CHUNK = 128
GDN_CHUNK = 64
GROUP = N_HEADS * HEAD
MIX_MAIN = 8 * GROUP
D_FF = 2816
LRU_BLOCKS = 8
LRU_C = 8.0
ROPE_BASE = 10000.0
EPS = 1e-6
N_SHARD = 4
LANES = 128

ADAM_LR, ADAM_B1, ADAM_B2, ADAM_EPS, ADAM_WD, ADAM_STEP = 0.001, 0.9, 0.999, 1e-08, 0.01, 10

VMEM_LIMIT_BYTES = 56 * 1024 * 1024

_roll = pltpu.roll


def _params(**kw):
    return pltpu.CompilerParams(vmem_limit_bytes=VMEM_LIMIT_BYTES, **kw)


def _sds(shape, dtype):
    return jax.ShapeDtypeStruct(tuple(shape), dtype)


def _shift_raw(x, d):
    n = x.shape[0]
    t = lax.broadcasted_iota(jnp.int32, x.shape, 0)
    if d > 0:
        return jnp.where(t >= d, _roll(x, d, 0), 0.0)
    return jnp.where(t < n + d, _roll(x, n + d, 0), 0.0)


@functools.partial(jax.custom_vjp, nondiff_argnums=(1,))
def shift_rows(x, d):
    return _shift_raw(x, d)


def _shift_fwd(x, d):
    return _shift_raw(x, d), None


def _shift_bwd(d, _, g):
    return (_shift_raw(g, -d),)


shift_rows.defvjp(_shift_fwd, _shift_bwd)


@jax.custom_vjp
def swap_halves(x):
    return _roll(x, HEAD // 2, 1)


def _swap_fwd(x):
    return _roll(x, HEAD // 2, 1), None


def _swap_bwd(_, g):
    return (_roll(g, HEAD // 2, 1),)


swap_halves.defvjp(_swap_fwd, _swap_bwd)


def _scan_raw(a, u, reverse):
    n = a.shape[0]
    t = lax.broadcasted_iota(jnp.int32, a.shape, 0)
    d = 1
    while d < n:
        if reverse:
            m = t < n - d
            a_s, u_s = _roll(a, n - d, 0), _roll(u, n - d, 0)
        else:
            m = t >= d
            a_s, u_s = _roll(a, d, 0), _roll(u, d, 0)
        u = a * jnp.where(m, u_s, 0.0) + u
        a = a * jnp.where(m, a_s, 1.0)
        d *= 2
    return u


@jax.custom_vjp
def lin_scan(a, u):
    return _scan_raw(a, u, False)


def _lin_scan_fwd(a, u):
    hs = _scan_raw(a, u, False)
    return hs, (a, hs)


def _lin_scan_bwd(res, g):
    a, hs = res
    lam = _scan_raw(_shift_raw(a, -1), g, True)
    return lam * _shift_raw(hs, 1), lam


lin_scan.defvjp(_lin_scan_fwd, _lin_scan_bwd)


def _bdot(a, b, dims=(((1,), (0,)), ((), ()))):
    return lax.dot_general(a.astype(BF16), b.astype(BF16), dims, preferred_element_type=F32)


def _each(f, *seqs):
    return tuple(f(*a) for a in zip(*seqs))


def _split_bf16(a):
    hi = a.astype(BF16)
    return hi, (a - hi.astype(F32)).astype(BF16)


def _dot3_raw(a_s, b_s):
    a_hl = _each(_split_bf16, a_s)
    b_hl = _each(_split_bf16, b_s)
    hh = _each(lambda a, b: _bdot(a[0], b[0]), a_hl, b_hl)
    hl = _each(lambda a, b: _bdot(a[0], b[1]), a_hl, b_hl)
    lh = _each(lambda a, b: _bdot(a[1], b[0]), a_hl, b_hl)
    return _each(lambda x, y, z: x + (y + z), hh, hl, lh)


@jax.custom_vjp
def dot3(a_s, b_s):
    return _dot3_raw(a_s, b_s)


def _dot3_fwd(a_s, b_s):
    return _dot3_raw(a_s, b_s), (a_s, b_s)


def _dot3_bwd(res, g_s):
    a_s, b_s = res
    return (_each(lambda g, b: _bdot(g, b, (((1,), (1,)), ((), ()))), g_s, b_s),
            _each(lambda a, g: _bdot(a, g, (((0,), (0,)), ((), ()))), a_s, g_s))


dot3.defvjp(_dot3_fwd, _dot3_bwd)


def _eye(n):
    i = lax.broadcasted_iota(jnp.int32, (n, n), 0)
    j = lax.broadcasted_iota(jnp.int32, (n, n), 1)
    return (i == j).astype(F32)


def _unit_lower_inverse_raw(lmats):
    n = lmats[0].shape[0]
    eye = _eye(n)
    ps = _each(lambda l: -l, lmats)
    invs = _each(lambda x: eye + x, ps)
    k = 1
    while 2 * k < n:
        ps = _each(lambda p: _bdot(p, p), ps)
        invs = _each(lambda inv, p: inv + _bdot(inv, p), invs, ps)
        k *= 2
    prods = _dot3_raw(lmats, invs)
    resids = _each(lambda inv, pr: eye - inv - pr, invs, prods)
    return _each(lambda inv, r: inv + _bdot(inv, r), invs, resids)


@jax.custom_vjp
def unit_lower_inverse(lmats):
    return _unit_lower_inverse_raw(lmats)


def _uli_fwd(lmats):
    invs = _unit_lower_inverse_raw(lmats)
    return invs, invs


def _uli_bwd(invs, g_s):
    ms = _each(lambda inv, g: _bdot(inv, g, (((0,), (0,)), ((), ()))), invs, g_s)
    return (_each(lambda m, inv: -_bdot(m, inv, (((1,), (1,)), ((), ()))), ms, invs),)


unit_lower_inverse.defvjp(_uli_fwd, _uli_bwd)


def _cumsum_raw(x, reverse):
    n = x.shape[0]
    t = lax.broadcasted_iota(jnp.int32, x.shape, 0)
    d = 1
    while d < n:
        if reverse:
            x = x + jnp.where(t < n - d, _roll(x, n - d, 0), 0.0)
        else:
            x = x + jnp.where(t >= d, _roll(x, d, 0), 0.0)
        d *= 2
    return x


@jax.custom_vjp
def cumsum_rows(x):
    return _cumsum_raw(x, False)


def _cumsum_fwd(x):
    return _cumsum_raw(x, False), None


def _cumsum_bwd(_, g):
    return (_cumsum_raw(g, True),)


cumsum_rows.defvjp(_cumsum_fwd, _cumsum_bwd)


_NT = (((1,), (1,)), ((), ()))
_TN = (((0,), (0,)), ((), ()))


def _softplus(x):
    return jnp.maximum(x, 0.0) + jnp.log1p(jnp.exp(-jnp.abs(x)))


def _expm1_nonpos(x):
    poly = x * (1.0 + x * (0.5 + x * (1.0 / 6 + x * (1.0 / 24 + x * (1.0 / 120 + x * (1.0 / 720))))))
    return jnp.where(x > -0.25, poly, jnp.exp(x) - 1.0)


def _rms(x):
    return x * lax.rsqrt(jnp.mean(x * x, axis=-1, keepdims=True) + EPS)


def _causal_conv(x, w, width):
    y = w[width - 1:width, :] * x
    for j in range(width - 1):
        y = y + w[j:j + 1, :] * shift_rows(x, width - 1 - j)
    return y


def _norm_fn(x, g):
    return _rms(x) * g


def _ffn_act_fn(ug, uv, wg, wv, bg, bv):
    return jax.nn.silu(_causal_conv(ug, wg, 3) + bg) * (_causal_conv(uv, wv, 3) + bv)


def _gdn_conv_fn(x, w):
    return jax.nn.silu(_causal_conv(x, w, 4))


def _lru_fn(gate, x, cw, cb, wa, ba, wx, bx, lam):
    xr = _causal_conv(x, cw, 4) + cb
    r = jax.nn.sigmoid(_bdot(xr, wa) + ba)
    i = jax.nn.sigmoid(_bdot(xr, wx) + bx)
    log_a = -LRU_C * r * _softplus(-lam)
    a = jnp.exp(log_a)
    u = jnp.sqrt(-_expm1_nonpos(2.0 * log_a)) * (i * xr)
    hs = lin_scan(a, u)
    return jax.nn.gelu(gate) * hs


def _ret_fn(qs, ks, vs, gates, states, cos2, sin2, dmasks, ktails, qdecs, cdecs):
    qrs = _each(lambda q: q * cos2 + swap_halves(q) * sin2, qs)
    krs = _each(lambda k: (k * cos2 + swap_halves(k) * sin2) * (HEAD ** -0.5), ks)
    scores = _each(lambda q, k, m: _bdot(q, k, _NT) * m, qrs, krs, dmasks)
    inter = _each(lambda q, d, s: _bdot(q * d, s), qrs, qdecs, states)
    os_ = _each(lambda sc, v, x: _bdot(sc, v) + x, scores, vs, inter)
    new_states = _each(lambda s, cd, k, kt, v: s * cd + _bdot(k * kt, v, _TN), states, cdecs, krs, ktails, vs)
    ys = _each(lambda o, g: _rms(o) * jax.nn.silu(g), os_, gates)
    return ys, new_states


def _pick_lane(x, lane_idx):
    lane = lax.broadcasted_iota(jnp.int32, x.shape, 1)
    return jnp.sum(jnp.where(lane == lane_idx, x, 0.0), axis=1, keepdims=True)


def _l2norm(x):
    return x * lax.rsqrt(jnp.sum(x * x, axis=-1, keepdims=True) + EPS)


def _gdn_fn(qcs, kcs, vcs, gates, small, a_log, dt_bias, gain, states):
    c = GDN_CHUNK
    heads = tuple(range(len(qcs)))
    qs = _each(lambda x: _l2norm(x) * (HEAD ** -0.5), qcs)
    ks = _each(_l2norm, kcs)
    betas = _each(lambda h: jax.nn.sigmoid(_pick_lane(small, h)), heads)
    gs = _each(lambda h: -jnp.exp(_pick_lane(a_log, h)) * _softplus(_pick_lane(small, h + N_HEADS) + _pick_lane(dt_bias, h)), heads)
    i = lax.broadcasted_iota(jnp.int32, (c, c), 0)
    j = lax.broadcasted_iota(jnp.int32, (c, c), 1)
    tril = i >= j
    gcs = _each(lambda g: cumsum_rows(jnp.broadcast_to(g, (c, LANES)))[:, :1], gs)
    gc_rows = _each(lambda gc: jnp.broadcast_to(gc, (c, c)), gcs)
    decays = _each(lambda r: jnp.where(tril, jnp.exp(jnp.where(tril, r - r.T, 0.0)), 0.0), gc_rows)
    kbs = _each(lambda k, b: k * b, ks, betas)
    lmats = _each(lambda kb, k, d: jnp.where(i > j, _bdot(kb, k, _NT) * d, 0.0), kbs, ks, decays)
    attns = _each(lambda q, k, d: jnp.where(tril, _bdot(q, k, _NT) * d, 0.0), qs, ks, decays)
    invs = unit_lower_inverse(lmats)
    us = dot3(invs, _each(lambda v, b: v * b, vcs, betas))
    ws = dot3(invs, _each(lambda kb, gc: kb * jnp.exp(gc), kbs, gcs))
    g_lasts = _each(lambda g: jnp.sum(g, axis=0, keepdims=True), gs)
    v_news = _each(lambda u, w, s: u - _bdot(w, s), us, ws, states)
    inter = _each(lambda q, gc, s: _bdot(q * jnp.exp(gc), s), qs, gcs, states)
    os_ = _each(lambda x, a, v: x + _bdot(a, v), inter, attns, v_news)
    new_states = _each(lambda s, gl, k, gc, v: s * jnp.exp(gl) + _bdot(k * jnp.exp(gl - gc), v, _TN), states, g_lasts, ks, gcs, v_news)
    ys = _each(lambda o, gate: _rms(o) * gain * jax.nn.silu(gate), os_, gates)
    return ys, new_states


def _final_fn(h, g, target):
    y = _rms(h) * g
    return 0.5 * jnp.sum(jnp.mean(jnp.square(y - target), axis=-1, keepdims=True), axis=0, keepdims=True)


def _tile(n, candidates):
    for t in candidates:
        if n % t == 0:
            return t
    raise ValueError(f"no tile for {n}")


def matmul(a, b, *, ta=False, tb=False, add=None, out_dtype=F32, tm=None, tn=None, name):
    m = a.shape[1] if ta else a.shape[0]
    k = a.shape[0] if ta else a.shape[1]
    n = b.shape[0] if tb else b.shape[1]
    assert k == (b.shape[1] if tb else b.shape[0])
    tm = tm or _tile(m, (1024, 512, 1408, 256, 128))
    tn = tn or _tile(n, (512, 1408, 256, 128))
    dims = (((0 if ta else 1,), (1 if tb else 0,)), ((), ()))

    def body(*refs):
        if add is None:
            a_ref, b_ref, o_ref = refs
        else:
            a_ref, b_ref, r_ref, o_ref = refs
        acc = lax.dot_general(a_ref[...].astype(BF16), b_ref[...].astype(BF16), dims, preferred_element_type=F32)
        if add is not None:
            acc = acc + r_ref[...]
        o_ref[...] = acc.astype(out_dtype)

    a_spec = pl.BlockSpec((k, tm), lambda i, j: (0, i)) if ta else pl.BlockSpec((tm, k), lambda i, j: (i, 0))
    b_spec = pl.BlockSpec((tn, k), lambda i, j: (j, 0)) if tb else pl.BlockSpec((k, tn), lambda i, j: (0, j))
    o_spec = pl.BlockSpec((tm, tn), lambda i, j: (i, j))
    in_specs, args = [a_spec, b_spec], [a, b]
    if add is not None:
        in_specs.append(o_spec)
        args.append(add)
    return pl.pallas_call(body, out_shape=_sds((m, n), out_dtype), grid=(m // tm, n // tn), in_specs=in_specs,
                          out_specs=o_spec, compiler_params=_params(), name=name)(*args)


ROW_TILE = 256


def norm_fwd(x, g, *, name):
    t, d = x.shape

    def body(x_ref, g_ref, o_ref):
        o_ref[...] = _norm_fn(x_ref[...], g_ref[...]).astype(BF16)

    return pl.pallas_call(body, out_shape=_sds((t, d), BF16), grid=(t // ROW_TILE,),
                          in_specs=[pl.BlockSpec((ROW_TILE, d), lambda i: (i, 0)), pl.BlockSpec((1, d), lambda i: (0, 0))],
                          out_specs=pl.BlockSpec((ROW_TILE, d), lambda i: (i, 0)), compiler_params=_params(), name=name)(x, g)


def norm_bwd(x, g, dy, dres, *, name):
    t, d = x.shape

    def body(x_ref, g_ref, dy_ref, dres_ref, dx_ref, dg_ref):
        _, vjp = jax.vjp(_norm_fn, x_ref[...], g_ref[...])
        dx, dg = vjp(dy_ref[...])
        dx_ref[...] = dx + dres_ref[...]

        @pl.when(pl.program_id(0) == 0)
        def _():
            dg_ref[...] = jnp.zeros_like(dg_ref)

        dg_ref[...] += dg

    row = pl.BlockSpec((ROW_TILE, d), lambda i: (i, 0))
    vec = pl.BlockSpec((1, d), lambda i: (0, 0))
    return pl.pallas_call(body, out_shape=(_sds((t, d), F32), _sds((1, d), F32)), grid=(t // ROW_TILE,),
                          in_specs=[row, vec, row, row], out_specs=(row, vec), compiler_params=_params(), name=name)(x, g, dy, dres)


def final_fwd_bwd(h, g, target, *, name):
    t, d = h.shape

    def body(h_ref, g_ref, t_ref, loss_ref, dh_ref, dg_ref):
        tgt = t_ref[...]
        loss, vjp = jax.vjp(lambda hh, gg: _final_fn(hh, gg, tgt), h_ref[...], g_ref[...])
        dh, dg = vjp(jnp.ones((1, 1), F32))
        dh_ref[...] = dh

        @pl.when(pl.program_id(0) == 0)
        def _():
            dg_ref[...] = jnp.zeros_like(dg_ref)
            loss_ref[...] = jnp.zeros_like(loss_ref)

        dg_ref[...] += dg
        loss_ref[...] += jnp.broadcast_to(loss, loss_ref.shape)

    row = pl.BlockSpec((ROW_TILE, d), lambda i: (i, 0))
    vec = pl.BlockSpec((1, d), lambda i: (0, 0))
    return pl.pallas_call(body, out_shape=(_sds((1, LANES), F32), _sds((t, d), F32), _sds((1, d), F32)), grid=(t // ROW_TILE,),
                          in_specs=[row, vec, row], out_specs=(pl.BlockSpec((1, LANES), lambda i: (0, 0)), row, vec),
                          compiler_params=_params(), name=name)(h, g, target)


FFN_FWD_COLS = 256
FFN_BWD_COLS = 128


def ffn_act_fwd(u, cw, cb, *, name):
    t = u.shape[0]
    w = FFN_FWD_COLS
    nb = D_FF // w

    def body(ug_ref, uv_ref, wg_ref, wv_ref, bg_ref, bv_ref, o_ref):
        o_ref[...] = _ffn_act_fn(ug_ref[...], uv_ref[...], wg_ref[...], wv_ref[...], bg_ref[...], bv_ref[...]).astype(BF16)

    def col(rows, off):
        return pl.BlockSpec((rows, w), lambda j: (0, j + off))

    return pl.pallas_call(body, out_shape=_sds((t, D_FF), BF16), grid=(nb,),
                          in_specs=[col(t, 0), col(t, nb), col(3, 0), col(3, nb), col(1, 0), col(1, nb)],
                          out_specs=col(t, 0), compiler_params=_params(), name=name)(u, u, cw, cw, cb, cb)


def ffn_act_bwd(u, cw, cb, da, *, name):
    t = u.shape[0]
    w = FFN_BWD_COLS
    nb = D_FF // w

    def body(ug_ref, uv_ref, wg_ref, wv_ref, bg_ref, bv_ref, da_ref, dug_ref, duv_ref, dwg_ref, dwv_ref, dbg_ref, dbv_ref):
        _, vjp = jax.vjp(_ffn_act_fn, ug_ref[...], uv_ref[...], wg_ref[...], wv_ref[...], bg_ref[...], bv_ref[...])
        dug, duv, dwg, dwv, dbg, dbv = vjp(da_ref[...])
        dug_ref[...] = dug.astype(BF16)
        duv_ref[...] = duv.astype(BF16)
        dwg_ref[...] = dwg
        dwv_ref[...] = dwv
        dbg_ref[...] = dbg
        dbv_ref[...] = dbv

    def col(rows, off):
        return pl.BlockSpec((rows, w), lambda j: (0, j + off))

    outs = pl.pallas_call(
        body, out_shape=(_sds((t, D_FF), BF16), _sds((t, D_FF), BF16), _sds((3, D_FF), F32), _sds((3, D_FF), F32),
                         _sds((1, D_FF), F32), _sds((1, D_FF), F32)),
        grid=(nb,), in_specs=[col(t, 0), col(t, nb), col(3, 0), col(3, nb), col(1, 0), col(1, nb), col(t, 0)],
        out_specs=(col(t, 0), col(t, 0), col(3, 0), col(3, 0), col(1, 0), col(1, 0)), compiler_params=_params(), name=name,
    )(u, u, cw, cw, cb, cb, da)
    dug, duv, dwg, dwv, dbg, dbv = outs
    return dug, duv, jnp.concatenate([dwg, dwv], axis=1), jnp.concatenate([dbg, dbv], axis=1)


GDN_CONV_COLS = 256
GDN_CONV_OFF = 4 * GROUP


def gdn_conv_fwd(p, cw, *, name):
    t = p.shape[0]
    w = GDN_CONV_COLS
    nb = 3 * GROUP // w
    off = GDN_CONV_OFF // w

    def body(x_ref, w_ref, o_ref):
        o_ref[...] = _gdn_conv_fn(x_ref[...], w_ref[...])

    return pl.pallas_call(body, out_shape=_sds((t, 3 * GROUP), F32), grid=(nb,),
                          in_specs=[pl.BlockSpec((t, w), lambda j: (0, j + off)), pl.BlockSpec((4, w), lambda j: (0, j))],
                          out_specs=pl.BlockSpec((t, w), lambda j: (0, j)), compiler_params=_params(), name=name)(p, cw)


def gdn_conv_bwd(p, cw, dc, *, name):
    t = p.shape[0]
    w = GDN_CONV_COLS
    nb = 3 * GROUP // w
    off = GDN_CONV_OFF // w

    def body(x_ref, w_ref, dc_ref, dx_ref, dw_ref):
        _, vjp = jax.vjp(_gdn_conv_fn, x_ref[...], w_ref[...])
        dx, dw = vjp(dc_ref[...])
        dx_ref[...] = dx.astype(BF16)
        dw_ref[...] = dw

    blk = pl.BlockSpec((t, w), lambda j: (0, j))
    wblk = pl.BlockSpec((4, w), lambda j: (0, j))
    return pl.pallas_call(body, out_shape=(_sds((t, 3 * GROUP), BF16), _sds((4, 3 * GROUP), F32)), grid=(nb,),
                          in_specs=[pl.BlockSpec((t, w), lambda j: (0, j + off)), wblk, blk], out_specs=(blk, wblk),
                          compiler_params=_params(), name=name)(p, cw, dc)


def _lru_specs(t):
    w = D_MODEL // LRU_BLOCKS
    gate = pl.BlockSpec((t, w), lambda j: (0, j))
    xin = pl.BlockSpec((t, w), lambda j: (0, j + LRU_BLOCKS))
    cw = pl.BlockSpec((4, w), lambda j: (0, j))
    vec = pl.BlockSpec((1, w), lambda j: (0, j))
    mat = pl.BlockSpec((None, w, w), lambda j: (j, 0, 0))
    return gate, xin, cw, vec, mat


def lru_fwd(gx, cw, cb, wa, ba, wx, bx, lam, *, name):
    t = gx.shape[0]
    gate, xin, cws, vec, mat = _lru_specs(t)

    def body(g_ref, x_ref, cw_ref, cb_ref, wa_ref, ba_ref, wx_ref, bx_ref, lam_ref, o_ref):
        o_ref[...] = _lru_fn(g_ref[...], x_ref[...], cw_ref[...], cb_ref[...], wa_ref[...], ba_ref[...], wx_ref[...],
                             bx_ref[...], lam_ref[...]).astype(BF16)

    return pl.pallas_call(body, out_shape=_sds((t, D_MODEL), BF16), grid=(LRU_BLOCKS,),
                          in_specs=[gate, xin, cws, vec, mat, vec, mat, vec, vec], out_specs=gate,
                          compiler_params=_params(), name=name)(gx, gx, cw, cb, wa, ba, wx, bx, lam)


def lru_bwd(gx, cw, cb, wa, ba, wx, bx, lam, dy, *, name):
    t = gx.shape[0]
    gate, xin, cws, vec, mat = _lru_specs(t)

    def body(g_ref, x_ref, cw_ref, cb_ref, wa_ref, ba_ref, wx_ref, bx_ref, lam_ref, dy_ref,
             dg_ref, dx_ref, dcw_ref, dcb_ref, dwa_ref, dba_ref, dwx_ref, dbx_ref, dlam_ref):
        _, vjp = jax.vjp(_lru_fn, g_ref[...], x_ref[...], cw_ref[...], cb_ref[...], wa_ref[...], ba_ref[...], wx_ref[...],
                         bx_ref[...], lam_ref[...])
        dg, dx, dcw, dcb, dwa, dba, dwx, dbx, dlam = vjp(dy_ref[...])
        dg_ref[...] = dg.astype(BF16)
        dx_ref[...] = dx.astype(BF16)
        dcw_ref[...] = dcw
        dcb_ref[...] = dcb
        dwa_ref[...] = dwa
        dba_ref[...] = dba
        dwx_ref[...] = dwx
        dbx_ref[...] = dbx
        dlam_ref[...] = dlam

    d = D_MODEL
    w = d // LRU_BLOCKS
    out_shape = (_sds((t, d), BF16), _sds((t, d), BF16), _sds((4, d), F32), _sds((1, d), F32), _sds((LRU_BLOCKS, w, w), F32),
                 _sds((1, d), F32), _sds((LRU_BLOCKS, w, w), F32), _sds((1, d), F32), _sds((1, d), F32))
    return pl.pallas_call(body, out_shape=out_shape, grid=(LRU_BLOCKS,),
                          in_specs=[gate, xin, cws, vec, mat, vec, mat, vec, vec, gate],
                          out_specs=(gate, gate, cws, vec, mat, vec, mat, vec, vec), compiler_params=_params(), name=name,
                          )(gx, gx, cw, cb, wa, ba, wx, bx, lam, dy)


def _ret_tables():
    half = HEAD // 2
    inv_freq = (np.float32(ROPE_BASE) ** (-np.arange(half, dtype=np.float32) / np.float32(half))).astype(np.float32)
    ang = (np.arange(SEQ, dtype=np.float32)[:, None] * inv_freq[None, :]).astype(np.float64)
    cos2 = np.concatenate([np.cos(ang), np.cos(ang)], axis=1).astype(np.float32)
    sin2 = np.concatenate([-np.sin(ang), np.sin(ang)], axis=1).astype(np.float32)
    c = RET_CHUNK
    log_gamma = np.log1p(-np.exp2(-5.0 - np.arange(N_HEADS, dtype=np.float64)))
    idx = np.arange(c, dtype=np.float64)
    rel = idx[:, None] - idx[None, :]
    dmask = np.where(rel >= 0, np.exp(log_gamma[:, None, None] * np.maximum(rel, 0.0)), 0.0)
    ones = np.ones((N_HEADS, c, HEAD))
    ktail = np.exp(log_gamma[:, None] * (c - 1 - idx))[:, :, None] * ones
    qdec = np.exp(log_gamma[:, None] * (idx + 1.0))[:, :, None] * ones
    cdec = np.exp(log_gamma * c)[:, None, None] * ones
    return tuple(jnp.asarray(a, F32) for a in (cos2, sin2, dmask, ktail, qdec, cdec))


def _ret_specs(rev):
    c = RET_CHUNK
    nc = SEQ // c

    def n_of(n):
        return nc - 1 - n if rev else n

    def group(off):
        return pl.BlockSpec((c, GROUP), lambda n: (n_of(n), off))

    tab = pl.BlockSpec((c, HEAD), lambda n: (n_of(n), 0))
    const = pl.BlockSpec((N_HEADS, c, HEAD), lambda n: (0, 0, 0))
    state = pl.BlockSpec((N_HEADS, None, HEAD, HEAD), lambda n: (0, n_of(n), 0, 0))
    return group, tab, const, state, nc


def _head(h):
    return slice(h * HEAD, (h + 1) * HEAD)


def ret_fwd(p, tables, *, name):
    group, tab, const, state, nc = _ret_specs(False)

    def body(q_ref, k_ref, v_ref, g_ref, cos_ref, sin_ref, dm_ref, kt_ref, qd_ref, cd_ref, y_ref, st_ref, s_scr):
        @pl.when(pl.program_id(0) == 0)
        def _():
            s_scr[...] = jnp.zeros_like(s_scr)

        heads = range(N_HEADS)
        states = tuple(s_scr[h] for h in heads)
        ys, new_states = _ret_fn(*(tuple(r[:, _head(h)] for h in heads) for r in (q_ref, k_ref, v_ref, g_ref)), states,
                                 cos_ref[...], sin_ref[...], *(tuple(r[h] for h in heads) for r in (dm_ref, kt_ref, qd_ref, cd_ref)))
        for h in heads:
            st_ref[h] = states[h]
            y_ref[:, _head(h)] = ys[h].astype(BF16)
            s_scr[h] = new_states[h]

    return pl.pallas_call(
        body, out_shape=(_sds((SEQ, GROUP), BF16), _sds((N_HEADS, nc, HEAD, HEAD), F32)), grid=(nc,),
        in_specs=[group(0), group(1), group(2), group(3), tab, tab, const, const, const, const],
        out_specs=(group(0), state), scratch_shapes=[pltpu.VMEM((N_HEADS, HEAD, HEAD), F32)], compiler_params=_params(), name=name,
    )(p, p, p, p, *tables)


def ret_bwd(p, tables, states, dy, *, name):
    group, tab, const, state, nc = _ret_specs(True)

    def body(q_ref, k_ref, v_ref, g_ref, cos_ref, sin_ref, dm_ref, kt_ref, qd_ref, cd_ref, st_ref, dy_ref,
             dq_ref, dk_ref, dv_ref, dg_ref, ds_scr):
        @pl.when(pl.program_id(0) == 0)
        def _():
            ds_scr[...] = jnp.zeros_like(ds_scr)

        heads = range(N_HEADS)
        consts = (cos_ref[...], sin_ref[...], *(tuple(r[h] for h in heads) for r in (dm_ref, kt_ref, qd_ref, cd_ref)))
        _, vjp = jax.vjp(lambda *a: _ret_fn(*a, *consts), *(tuple(r[:, _head(h)] for h in heads) for r in (q_ref, k_ref, v_ref, g_ref)),
                         tuple(st_ref[h] for h in heads))
        dqs, dks, dvs, dgs, dss = vjp((tuple(dy_ref[:, _head(h)] for h in heads), tuple(ds_scr[h] for h in heads)))
        for h in heads:
            dq_ref[:, _head(h)] = dqs[h].astype(BF16)
            dk_ref[:, _head(h)] = dks[h].astype(BF16)
            dv_ref[:, _head(h)] = dvs[h].astype(BF16)
            dg_ref[:, _head(h)] = dgs[h].astype(BF16)
            ds_scr[h] = dss[h]

    out = _sds((SEQ, GROUP), BF16)
    return pl.pallas_call(
        body, out_shape=(out, out, out, out), grid=(nc,),
        in_specs=[group(0), group(1), group(2), group(3), tab, tab, const, const, const, const, state, group(0)],
        out_specs=(group(0), group(0), group(0), group(0)), scratch_shapes=[pltpu.VMEM((N_HEADS, HEAD, HEAD), F32)],
        compiler_params=_params(), name=name,
    )(p, p, p, p, *tables, states, dy)


def _gdn_specs(rev):
    c = GDN_CHUNK
    nc = SEQ // c

    def n_of(n):
        return nc - 1 - n if rev else n

    def group(off):
        return pl.BlockSpec((c, GROUP), lambda n: (n_of(n), off))

    small = pl.BlockSpec((c, LANES), lambda n: (n_of(n), 0))
    vec = pl.BlockSpec((1, LANES), lambda n: (0, 0))
    state = pl.BlockSpec((N_HEADS, None, HEAD, HEAD), lambda n: (0, n_of(n), 0, 0))
    return group, small, vec, state, nc


GDN_GATE_GROUP = 7


def gdn_fwd(conv, p, small, a_log, dt_bias, gain, *, name):
    group, sm, vec, state, nc = _gdn_specs(False)

    def body(q_ref, k_ref, v_ref, g_ref, sm_ref, al_ref, dt_ref, gn_ref, y_ref, st_ref, s_scr):
        @pl.when(pl.program_id(0) == 0)
        def _():
            s_scr[...] = jnp.zeros_like(s_scr)

        states = tuple(s_scr[h] for h in range(N_HEADS))
        ys, new_states = _gdn_fn(*(tuple(r[:, _head(h)] for h in range(N_HEADS)) for r in (q_ref, k_ref, v_ref, g_ref)),
                                 sm_ref[...], al_ref[...], dt_ref[...], gn_ref[...], states)
        for h in range(N_HEADS):
            st_ref[h] = states[h]
            y_ref[:, _head(h)] = ys[h].astype(BF16)
            s_scr[h] = new_states[h]

    return pl.pallas_call(
        body, out_shape=(_sds((SEQ, GROUP), BF16), _sds((N_HEADS, nc, HEAD, HEAD), F32)), grid=(nc,),
        in_specs=[group(0), group(1), group(2), group(GDN_GATE_GROUP), sm, vec, vec, vec], out_specs=(group(0), state),
        scratch_shapes=[pltpu.VMEM((N_HEADS, HEAD, HEAD), F32)], compiler_params=_params(), name=name,
    )(conv, conv, conv, p, small, a_log, dt_bias, gain)


def gdn_bwd(conv, p, small, a_log, dt_bias, gain, states, dy, *, name):
    group, sm, vec, state, nc = _gdn_specs(True)

    def body(q_ref, k_ref, v_ref, g_ref, sm_ref, al_ref, dt_ref, gn_ref, st_ref, dy_ref,
             dq_ref, dk_ref, dv_ref, dg_ref, dsm_ref, dal_ref, ddt_ref, dgn_ref, ds_scr):
        @pl.when(pl.program_id(0) == 0)
        def _():
            ds_scr[...] = jnp.zeros_like(ds_scr)
            dal_ref[...] = jnp.zeros_like(dal_ref)
            ddt_ref[...] = jnp.zeros_like(ddt_ref)
            dgn_ref[...] = jnp.zeros_like(dgn_ref)

        per_head = tuple(tuple(r[:, _head(h)] for h in range(N_HEADS)) for r in (q_ref, k_ref, v_ref, g_ref))
        _, vjp = jax.vjp(_gdn_fn, *per_head, sm_ref[...], al_ref[...], dt_ref[...], gn_ref[...],
                         tuple(st_ref[h] for h in range(N_HEADS)))
        cts = (tuple(dy_ref[:, _head(h)] for h in range(N_HEADS)), tuple(ds_scr[h] for h in range(N_HEADS)))
        dqs, dks, dvs, dgs, dsm, dal, ddt, dgn, dss = vjp(cts)
        for h in range(N_HEADS):
            dq_ref[:, _head(h)] = dqs[h]
            dk_ref[:, _head(h)] = dks[h]
            dv_ref[:, _head(h)] = dvs[h]
            dg_ref[:, _head(h)] = dgs[h].astype(BF16)
            ds_scr[h] = dss[h]
        dsm_ref[...] = dsm
        dal_ref[...] += dal
        ddt_ref[...] += ddt
        dgn_ref[...] += dgn

    f = _sds((SEQ, GROUP), F32)
    pv = _sds((1, LANES), F32)
    return pl.pallas_call(
        body, out_shape=(f, f, f, _sds((SEQ, GROUP), BF16), _sds((SEQ, LANES), F32), pv, pv, pv), grid=(nc,),
        in_specs=[group(0), group(1), group(2), group(GDN_GATE_GROUP), sm, vec, vec, vec, state, group(1)],
        out_specs=(group(0), group(0), group(0), group(0), sm, vec, vec, vec), scratch_shapes=[pltpu.VMEM((N_HEADS, HEAD, HEAD), F32)],
        compiler_params=_params(), name=name,
    )(conv, conv, conv, p, small, a_log, dt_bias, gain, states, dy)


PACK_ROW_TILE = 1024


def adamw(w, g, m, v, *, name):
    r = w.shape[0]
    tr = _tile(r, (PACK_ROW_TILE, 256, 128, 64, 32, 16, 8))

    def body(w_ref, g_ref, m_ref, v_ref, d_ref, nm_ref, nv_ref):
        gg = g_ref[...]
        nm = ADAM_B1 * m_ref[...] + (1.0 - ADAM_B1) * gg
        nv = ADAM_B2 * v_ref[...] + (1.0 - ADAM_B2) * jnp.square(gg)
        m_hat = nm / (1.0 - ADAM_B1 ** ADAM_STEP)
        v_hat = nv / (1.0 - ADAM_B2 ** ADAM_STEP)
        d_ref[...] = -ADAM_LR * (m_hat / (jnp.sqrt(v_hat) + ADAM_EPS) + ADAM_WD * w_ref[...])
        nm_ref[...] = nm
        nv_ref[...] = nv

    blk = pl.BlockSpec((tr, LANES), lambda i: (i, 0))
    o = _sds((r, LANES), F32)
    return pl.pallas_call(body, out_shape=(o, o, o), grid=(r // tr,), in_specs=[blk] * 4, out_specs=(blk, blk, blk),
                          compiler_params=_params(), name=name)(w, g, m, v)


ELEMENTWISE_BLOCK_BYTES = 2 * 1024 * 1024


def _row_tile(r, c):
    best = None
    for tr in range(8, r + 1, 8):
        if r % tr == 0 and tr * c * 4 <= ELEMENTWISE_BLOCK_BYTES:
            best = tr
    if best is None:
        raise ValueError(f"no row tile for ({r}, {c})")
    return best


def _core_index():
    return lax.axis_index("c").astype(jnp.int32).reshape(1)


def _chip_index():
    return (2 * lax.axis_index("x") + lax.axis_index("y")).astype(jnp.int32).reshape(1)


def adamw_halves(w, m, v, g_own, g_sib, *, layer=0, prev=None, name):
    n_layers, rows, c = w.shape
    r = rows // 2
    tr = _row_tile(r, c)
    nb = r // tr

    def body(c_ref, w_ref, m_ref, v_ref, own_ref, sib_ref, *rest):
        g_ref, d_ref, nm_ref, nv_ref = rest[-4:]
        gg = jnp.where(pl.program_id(0) == c_ref[0], own_ref[...], sib_ref[...])
        nm = ADAM_B1 * m_ref[...] + (1.0 - ADAM_B1) * gg
        nv = ADAM_B2 * v_ref[...] + (1.0 - ADAM_B2) * jnp.square(gg)
        m_hat = nm / (1.0 - ADAM_B1 ** ADAM_STEP)
        v_hat = nv / (1.0 - ADAM_B2 ** ADAM_STEP)
        g_ref[...] = gg
        d_ref[...] = -ADAM_LR * (m_hat / (jnp.sqrt(v_hat) + ADAM_EPS) + ADAM_WD * w_ref[...])
        nm_ref[...] = nm
        nv_ref[...] = nv

    full = pl.BlockSpec((None, tr, c), lambda h, i, cr: (layer, h * nb + i, 0))
    half = pl.BlockSpec((tr, c), lambda h, i, cr: (i, 0))
    o = _sds((n_layers, rows, c), F32)
    prev = list(prev or ())
    gs = pltpu.PrefetchScalarGridSpec(num_scalar_prefetch=1, grid=(2, nb), in_specs=[full, full, full, half, half] + [_ANY] * len(prev),
                                      out_specs=(full, full, full, full))
    n_fixed = 6
    return pl.pallas_call(body, out_shape=(o, o, o, o), grid_spec=gs, compiler_params=_params(), name=name,
                          input_output_aliases={n_fixed + k: k for k in range(len(prev))})(
        _core_index(), w, m, v, g_own, g_sib, *prev)


def add_core_halves(g2, land, *, out_dtype, name):
    _, ns, r, cols = g2.shape
    tr = _row_tile(r, cols)

    def body(c_ref, a_ref, b_ref, o_ref):
        o_ref[...] = (a_ref[...] + b_ref[...]).astype(out_dtype)

    gs = pltpu.PrefetchScalarGridSpec(
        num_scalar_prefetch=1, grid=(ns, r // tr),
        in_specs=[pl.BlockSpec((None, None, tr, cols), lambda s, i, cr: (cr[0], s, i, 0)),
                  pl.BlockSpec((None, tr, cols), lambda s, i, cr: (s, i, 0))],
        out_specs=pl.BlockSpec((None, tr, cols), lambda s, i, cr: (s, i, 0)))
    return pl.pallas_call(body, out_shape=_sds((ns, r, cols), out_dtype), grid_spec=gs, compiler_params=_params(), name=name)(
        _core_index(), g2, land)


def sum_over_chips(own, land, *, scatter, name):
    _, r, cols = own.shape
    tr = _row_tile(r, cols)

    def body(mine_ref, own_ref, l0, l1, l2, l3, o_ref):
        mine = mine_ref[0]
        mine_val = own_ref[...]
        acc = None
        for s, l_ref in enumerate((l0, l1, l2, l3)):
            val = jnp.where(mine == s, mine_val, l_ref[...]).astype(F32)
            acc = val if acc is None else acc + val
        o_ref[...] = acc

    def slot(s):
        return pl.BlockSpec((None, tr, cols), lambda i, mr: (jnp.where(mr[0] == s, (s + 1) % N_SHARD, s), i, 0))

    own_spec = pl.BlockSpec((None, tr, cols), lambda i, mr: (mr[0] if scatter else 0, i, 0))
    gs = pltpu.PrefetchScalarGridSpec(num_scalar_prefetch=1, grid=(r // tr,), in_specs=[own_spec] + [slot(s) for s in range(N_SHARD)],
                                      out_specs=pl.BlockSpec((tr, cols), lambda i, mr: (i, 0)))
    return pl.pallas_call(body, out_shape=_sds((r, cols), F32), grid_spec=gs, compiler_params=_params(), name=name)(
        _chip_index(), own, land, land, land, land)


_ANY = pl.BlockSpec(memory_space=pl.ANY)


def xy_exchange(src, *, scatter, name):
    rh = src.shape[1]

    def body(src_ref, land_ref, send_sems, recv_sems, loc_sem):
        x, y, c = lax.axis_index("x"), lax.axis_index("y"), lax.axis_index("c")
        mine = 2 * x + y
        peers = [(1 - x, y), (x, 1 - y), (1 - x, 1 - y)]

        def piece(shard):
            return src_ref.at[shard] if scatter else src_ref.at[c]

        def copy(k, px, py, dst_slot):
            return pltpu.make_async_remote_copy(src_ref=piece(2 * px + py), dst_ref=land_ref.at[dst_slot], send_sem=send_sems.at[k],
                                                recv_sem=recv_sems.at[k], device_id=(px, py, c), device_id_type=MESH)

        keep = pltpu.make_async_copy(piece(mine), land_ref.at[mine], loc_sem)
        keep.start()
        sends = [copy(k, px, py, mine) for k, (px, py) in enumerate(peers)]
        for cp in sends:
            cp.start()
        for cp in sends:
            cp.wait_send()
        for k, (px, py) in enumerate(peers):
            copy(k, px, py, 2 * px + py).wait_recv()
        keep.wait()

    return pl.pallas_call(body, out_shape=_sds((N_SHARD, rh, LANES), src.dtype), in_specs=[_ANY], out_specs=_ANY,
                          scratch_shapes=[pltpu.SemaphoreType.DMA((3,)), pltpu.SemaphoreType.DMA((3,)), pltpu.SemaphoreType.DMA(())],
                          name=name)(src)


def core_exchange(src, *, send_other_half, name):
    def body(src_ref, out_ref, send_sem, recv_sem, loc_sem):
        x, y, c = lax.axis_index("x"), lax.axis_index("y"), lax.axis_index("c")
        if send_other_half:
            cp = pltpu.make_async_remote_copy(src_ref=src_ref.at[1 - c], dst_ref=out_ref, send_sem=send_sem, recv_sem=recv_sem,
                                              device_id=(x, y, 1 - c), device_id_type=MESH)
            cp.start()
            cp.wait_send()
            cp.wait_recv()
        else:
            keep = pltpu.make_async_copy(src_ref, out_ref.at[c], loc_sem)
            keep.start()
            cp = pltpu.make_async_remote_copy(src_ref=src_ref, dst_ref=out_ref.at[c], send_sem=send_sem, recv_sem=recv_sem,
                                              device_id=(x, y, 1 - c), device_id_type=MESH)
            cp.start()
            cp.wait_send()
            pltpu.make_async_remote_copy(src_ref=src_ref, dst_ref=out_ref.at[1 - c], send_sem=send_sem, recv_sem=recv_sem,
                                         device_id=(x, y, 1 - c), device_id_type=MESH).wait_recv()
            keep.wait()

    out_shape = _sds(src.shape[1:], src.dtype) if send_other_half else _sds((2,) + src.shape, src.dtype)
    return pl.pallas_call(body, out_shape=out_shape, in_specs=[_ANY], out_specs=_ANY,
                          scratch_shapes=[pltpu.SemaphoreType.DMA(()), pltpu.SemaphoreType.DMA(()), pltpu.SemaphoreType.DMA(())],
                          name=name)(src)


def _comm_call(body, ins, out_shapes, sem_counts, name):
    return pl.pallas_call(body, out_shape=tuple(out_shapes), in_specs=[_ANY] * len(ins), out_specs=tuple([_ANY] * len(out_shapes)),
                          scratch_shapes=[pltpu.SemaphoreType.DMA((k,)) for k in sem_counts], name=name)(*ins)


def _sequencer_call(body, ins, out_shapes, sem_counts, name, collective_id):
    return pl.kernel(body, out_type=list(out_shapes), mesh=plsc.ScalarSubcoreMesh(axis_name="sequencer", num_cores=1), name=name,
                     scratch_types=[pltpu.SemaphoreType.DMA((k,)) for k in sem_counts],
                     compiler_params=pltpu.CompilerParams(collective_id=collective_id))(*ins)


def _handshake(peers):
    barrier = pltpu.get_barrier_semaphore()
    for peer in peers:
        pl.semaphore_signal(barrier, inc=1, device_id=peer, device_id_type=MESH)
    pl.semaphore_wait(barrier, len(peers))


def _xy_peers(x, y):
    return [(1 - x, y), (x, 1 - y), (1 - x, 1 - y)]


def gather_halves(halves, *, name, collective_id):
    n = len(halves)

    def body(*refs):
        ins, lands, sibs = refs[:n], refs[n:2 * n], refs[2 * n:3 * n]
        ici_send, ici_recv, d2d_send, d2d_recv = refs[3 * n:]
        x, y, c = lax.axis_index("x"), lax.axis_index("y"), lax.axis_index("c")
        mine = 2 * x + y
        peers = _xy_peers(x, y)
        _handshake([(px, py, c) for px, py in peers] + [(x, y, 1 - c)])

        def ici(i, k, slot):
            px, py = peers[k]
            return pltpu.make_async_remote_copy(src_ref=ins[i].at[c], dst_ref=lands[i].at[slot], send_sem=ici_send.at[3 * i + k],
                                                recv_sem=ici_recv.at[3 * i + k], device_id=(px, py, c), device_id_type=MESH)

        def pass_on(i, k):
            px, py = peers[k]
            slot = 2 * px + py
            return pltpu.make_async_remote_copy(src_ref=lands[i].at[slot], dst_ref=sibs[i].at[slot], send_sem=d2d_send.at[3 * i + k],
                                                recv_sem=d2d_recv.at[3 * i + k], device_id=(x, y, 1 - c), device_id_type=MESH)

        sends = [ici(i, k, mine) for i in range(n) for k in range(3)]
        for cp in sends:
            cp.start()
        passed = []
        for i in range(n):
            for k in range(3):
                px, py = peers[k]
                ici(i, k, 2 * px + py).wait_recv()
                cp = pass_on(i, k)
                cp.start()
                passed.append(cp)
        for cp in passed:
            cp.wait_recv()
        for cp in sends + passed:
            cp.wait_send()

    outs = [_sds((N_SHARD,) + h.shape[1:], h.dtype) for h in halves]
    res = _sequencer_call(body, halves, outs + outs, [3 * n] * 4, name, collective_id)
    return res[:n], res[n:]


def send_other_half(arrays, *, name, collective_id):
    n = len(arrays)

    def body(*refs):
        ins, lands = refs[:n], refs[n:2 * n]
        send_sems, recv_sems = refs[2 * n:]
        x, y, c = lax.axis_index("x"), lax.axis_index("y"), lax.axis_index("c")
        _handshake([(x, y, 1 - c)])
        copies = [pltpu.make_async_remote_copy(src_ref=ins[i].at[1 - c], dst_ref=lands[i], send_sem=send_sems.at[i],
                                               recv_sem=recv_sems.at[i], device_id=(x, y, 1 - c), device_id_type=MESH) for i in range(n)]
        for cp in copies:
            cp.start()
        for cp in copies:
            cp.wait_recv()
        for cp in copies:
            cp.wait_send()

    return _sequencer_call(body, arrays, [_sds(a.shape[1:], a.dtype) for a in arrays], [n, n], name, collective_id)


_HBM = pl.BlockSpec(memory_space=pltpu.HBM)
_SEM = pl.BlockSpec(memory_space=pltpu.SEMAPHORE)
_SPLIT_COPY = dict(has_side_effects=pltpu.SideEffectType.DATAFLOW_SIDE_EFFECTING)


def _chip_copy(ins, lands, send_sems, recv_sems, scatter, i, k, receive):
    x, y, c = lax.axis_index("x"), lax.axis_index("y"), lax.axis_index("c")
    px, py = _xy_peers(x, y)[k]
    theirs, mine = 2 * px + py, 2 * x + y
    src = ins[i].at[theirs] if scatter[i] else ins[i].at[0]
    return pltpu.make_async_remote_copy(src_ref=src, dst_ref=lands[i].at[theirs if receive else mine], send_sem=send_sems.at[3 * i + k],
                                        recv_sem=recv_sems.at[3 * i + k], device_id=(px, py, c), device_id_type=MESH)


def send_to_chips_start(arrays, scatter, *, name):
    n = len(arrays)

    def body(*refs):
        send_sems, recv_sems = refs[2 * n], refs[2 * n + 1]
        ins, lands = refs[2 * n + 2:3 * n + 2], refs[3 * n + 2:4 * n + 2]
        token = refs[4 * n + 2]
        for i in range(n):
            for k in range(3):
                _chip_copy(ins, lands, send_sems, recv_sems, scatter, i, k, receive=False).start()
        token[...] = jnp.zeros_like(token)

    land_shapes = [(N_SHARD,) + a.shape[1:] for a in arrays]
    operands = [pltpu.with_memory_space_constraint(a, pltpu.HBM) for a in arrays]
    operands += [pltpu.with_memory_space_constraint(lax.empty(s, a.dtype), pltpu.HBM) for s, a in zip(land_shapes, arrays)]
    out_shape = ([pltpu.SemaphoreType.DMA((3 * n,)), pltpu.SemaphoreType.DMA((3 * n,))] + [pltpu.HBM(a.shape, a.dtype) for a in arrays]
                 + [pltpu.HBM(s, a.dtype) for s, a in zip(land_shapes, arrays)] + [_sds((8, LANES), F32)])
    res = pl.pallas_call(body, name=name, out_shape=out_shape, in_specs=[_HBM] * (2 * n),
                         out_specs=[_SEM, _SEM] + [_HBM] * (2 * n) + [pl.BlockSpec(memory_space=pltpu.VMEM)],
                         input_output_aliases={i: 2 + i for i in range(2 * n)}, compiler_params=pltpu.CompilerParams(**_SPLIT_COPY))(*operands)
    return (res[0], res[1], res[2:2 + n], res[2 + n:2 + 2 * n], scatter), res[-1]


def send_to_chips_wait(state, after, *, name):
    send_sems, recv_sems, arrays, lands, scatter = state
    n = len(arrays)

    def body(*refs):
        ins, landing = refs[:n], refs[n:2 * n]
        send_sems, recv_sems = refs[2 * n], refs[2 * n + 1]
        for i in range(n):
            for k in range(3):
                _chip_copy(ins, landing, send_sems, recv_sems, scatter, i, k, receive=True).wait_recv()
        for i in range(n):
            for k in range(3):
                _chip_copy(ins, landing, send_sems, recv_sems, scatter, i, k, receive=False).wait_send()

    out_shape = [pltpu.HBM(a.shape, a.dtype) for a in list(arrays) + list(lands)]
    res = pl.pallas_call(body, name=name, out_shape=out_shape, in_specs=[_HBM] * (2 * n) + [_SEM, _SEM] + [_ANY] * len(after),
                         out_specs=[_HBM] * (2 * n), input_output_aliases={i: i for i in range(2 * n)},
                         compiler_params=pltpu.CompilerParams(**_SPLIT_COPY))(*arrays, *lands, send_sems, recv_sems, *after)
    return res[:n], res[n:]


def swap_with_other_core(arrays, *, name, collective_id):
    n = len(arrays)

    def body(*refs):
        ins, lands = refs[:n], refs[n:2 * n]
        send_sems, recv_sems = refs[2 * n:]
        x, y, c = lax.axis_index("x"), lax.axis_index("y"), lax.axis_index("c")
        _handshake([(x, y, 1 - c)])
        copies = [pltpu.make_async_remote_copy(src_ref=ins[i], dst_ref=lands[i], send_sem=send_sems.at[i], recv_sem=recv_sems.at[i],
                                               device_id=(x, y, 1 - c), device_id_type=MESH) for i in range(n)]
        for cp in copies:
            cp.start()
        for cp in copies:
            cp.wait_recv()
        for cp in copies:
            cp.wait_send()

    return _sequencer_call(body, arrays, [_sds(a.shape, a.dtype) for a in arrays], [n, n], name, collective_id)


def _pack_rows(n_elems, row_multiple):
    rows = -(-n_elems // LANES)
    return -(-rows // row_multiple) * row_multiple


def _pack(arrays, rows, dtype):
    flat = jnp.concatenate([a.reshape(-1).astype(dtype) for a in arrays])
    return jnp.pad(flat, (0, rows * LANES - flat.shape[0])).reshape(rows, LANES)


def _unpack(packed, shapes):
    flat = packed.reshape(-1)
    out, off = [], 0
    for s in shapes:
        n = int(np.prod(s))
        out.append(flat[off:off + n].reshape(s))
        off += n
    return out


def all_gather_shards(shards, axes, dtype, row_multiple, tag):
    shapes = [s.shape for s in shards]
    rows = _pack_rows(sum(int(np.prod(s)) for s in shapes), row_multiple)
    packed = _pack(shards, rows, dtype).reshape(2, rows // 2, LANES)
    land = xy_exchange(packed, scatter=False, name=f"gather_xy_{tag}")
    both = core_exchange(land, send_other_half=False, name=f"gather_c_{tag}")
    per_shard = jnp.swapaxes(both, 0, 1).reshape(N_SHARD, rows, LANES)
    pieces = [_unpack(per_shard[s], shapes) for s in range(N_SHARD)]
    return [jnp.concatenate([pieces[s][i] for s in range(N_SHARD)], axis=ax) for i, ax in enumerate(axes)]


def _ordered_before(first, then):
    if then is None:
        return first, None
    return lax.optimization_barrier((first, then))


def reduce_between_cores(arrays, scatter, *, tag, collective_id, before=None):
    arrays, before = _ordered_before(arrays, before)
    land = send_other_half(arrays, name=f"reduce_core_send_{tag}", collective_id=collective_id)
    return (arrays, land, scatter, tag, collective_id), before


def reduce_between_chips(state, before=None):
    arrays, land, scatter, tag, collective_id = state
    chip = [add_core_halves(a, l, out_dtype=BF16 if sc else F32, name=f"reduce_core_add_{tag}_{i}")
            for i, (a, l, sc) in enumerate(zip(arrays, land, scatter))]
    sending, token = send_to_chips_start(chip, scatter, name=f"reduce_chip_start_{tag}")
    token, before = _ordered_before(token, before)
    return (sending, token, scatter, tag, collective_id), before


def reduce_finish(state, after):
    sending, token, scatter, tag, collective_id = state
    chip, land = send_to_chips_wait(sending, tuple(after) + (token,), name=f"reduce_chip_wait_{tag}")
    own = [sum_over_chips(ch, l, scatter=sc, name=f"reduce_chip_add_{tag}_{i}") for i, (ch, l, sc) in enumerate(zip(chip, land, scatter))]
    sib = swap_with_other_core(own, name=f"reduce_core_swap_{tag}", collective_id=collective_id + 2)
    return own, sib


def _ffn_layer_fwd(h, norm_g, w_up, cw, cb, w_down, tag):
    hn = norm_fwd(h, norm_g, name=f"ffn_norm_{tag}")
    u = matmul(hn, w_up, name=f"ffn_up_{tag}")
    act = ffn_act_fwd(u, cw, cb, name=f"ffn_act_{tag}")
    out = matmul(act, w_down, add=h, name=f"ffn_down_{tag}")
    return out, (h, hn, u, act)


def _ffn_layer_bwd(saved, dout, norm_g, w_up, cw, cb, w_down, tag):
    h, hn, u, act = saved
    dact = matmul(dout, w_down, tb=True, name=f"ffn_down_dx_{tag}")
    d_w_down = matmul(act, dout, ta=True, name=f"ffn_down_dw_{tag}")
    dug, duv, dcw, dcb = ffn_act_bwd(u, cw, cb, dact, name=f"ffn_act_bwd_{tag}")
    du = jnp.concatenate([dug, duv], axis=1)
    dhn = matmul(du, w_up, tb=True, name=f"ffn_up_dx_{tag}")
    d_w_up = matmul(hn, du, ta=True, name=f"ffn_up_dw_{tag}")
    dh, dg = norm_bwd(h, norm_g, dhn, dout, name=f"ffn_norm_bwd_{tag}")
    return dh, dg, d_w_up, dcw, dcb, d_w_down


def local_step(x, target, w, stage=lambda name, tensors, grads=None: tensors):
    g = {}
    tables = _ret_tables()
    x = stage("start", x)
    w_in = w["ret_gdn_w_in"]
    w_main = w_in[:, :MIX_MAIN]
    w_small = jnp.pad(w_in[:, MIX_MAIN:], ((0, 0), (0, LANES - 2 * N_HEADS)))
    a_log = jnp.pad(w["gdn_a_log"], ((0, 0), (0, LANES - N_HEADS)))
    dt_bias = jnp.pad(w["gdn_dt_bias"], ((0, 0), (0, LANES - N_HEADS)))

    hn0 = stage("normed", norm_fwd(x, w["norm_mix"][0:1], name="mix0_norm"))
    p = matmul(hn0, w_main, name="mix0_in")
    small = matmul(hn0, w_small, name="mix0_in_small")
    y_ret, s_ret = ret_fwd(p, tables, name="ret_fwd")
    conv = gdn_conv_fwd(p, w["gdn_conv_w"], name="gdn_conv")
    y_gdn, s_gdn = gdn_fwd(conv, p, small, a_log, dt_bias, w["gdn_out_gain"], name="gdn_fwd")
    y0 = stage("mixed", jnp.concatenate([y_ret, y_gdn], axis=1))
    h1 = matmul(y0, w["ret_gdn_w_out"], add=x, name="mix0_out")
    h2, ffn0 = _ffn_layer_fwd(h1, w["norm_ffn"][0:1], w["ffn_w_up"][0], w["ffn_conv_w"][0], w["ffn_conv_b"][0:1], w["ffn_w_down"][0], "0")
    h2 = stage("layer0", h2)

    hn1 = norm_fwd(h2, w["norm_mix"][1:2], name="mix1_norm")
    gx = matmul(hn1, w["lru_w_in"], name="mix1_in")
    lru_p = (w["lru_conv_w"], w["lru_conv_b"], w["lru_w_a"], w["lru_b_a"], w["lru_w_x"], w["lru_b_x"], w["lru_lambda"])
    y1 = lru_fwd(gx, *lru_p, name="lru_fwd")
    h3 = matmul(y1, w["lru_w_out"], add=h2, name="mix1_out")
    h4, ffn1 = _ffn_layer_fwd(h3, w["norm_ffn"][1:2], w["ffn_w_up"][1], w["ffn_conv_w"][1], w["ffn_conv_b"][1:2], w["ffn_w_down"][1], "1")

    loss, dh4, g["norm_final"] = final_fwd_bwd(h4, w["norm_final"], target, name="final")

    dh3, dgf1, dwu1, dcw1, dcb1, dwd1 = _ffn_layer_bwd(ffn1, dh4, w["norm_ffn"][1:2], w["ffn_w_up"][1], w["ffn_conv_w"][1],
                                                     w["ffn_conv_b"][1:2], w["ffn_w_down"][1], "1")
    g["ffn_w_up_1"] = dwu1
    dh3 = stage("grads0_ready", dh3, g)
    dy1 = matmul(dh3, w["lru_w_out"], tb=True, name="mix1_out_dx")
    g["lru_w_out"] = matmul(y1, dh3, ta=True, name="mix1_out_dw")
    dgate, dxr, g["lru_conv_w"], g["lru_conv_b"], g["lru_w_a"], g["lru_b_a"], g["lru_w_x"], g["lru_b_x"], g["lru_lambda"] = lru_bwd(
        gx, *lru_p, dy1, name="lru_bwd")
    dgx = stage("grads0_send", jnp.concatenate([dgate, dxr], axis=1), g)
    dhn1 = matmul(dgx, w["lru_w_in"], tb=True, name="mix1_in_dx")
    g["lru_w_in"] = matmul(hn1, dgx, ta=True, name="mix1_in_dw")
    dh2, dgm1 = norm_bwd(h2, w["norm_mix"][1:2], dhn1, dh3, name="mix1_norm_bwd")
    dh2 = stage("grads1_ready", dh2, g)

    dh1, dgf0, dwu0, dcw0, dcb0, dwd0 = _ffn_layer_bwd(ffn0, dh2, w["norm_ffn"][0:1], w["ffn_w_up"][0], w["ffn_conv_w"][0],
                                                     w["ffn_conv_b"][0:1], w["ffn_w_down"][0], "0")
    g["ffn_w_up_0"] = dwu0
    g["ffn_w_down"] = jnp.stack([dwd0, dwd1])
    dh1 = stage("grads2_ready", stage("grads1_send", dh1, g), g)
    dy0 = matmul(dh1, w["ret_gdn_w_out"], tb=True, name="mix0_out_dx")
    g["ret_gdn_w_out"] = matmul(y0, dh1, ta=True, name="mix0_out_dw")
    dq_r, dk_r, dv_r, dg_r = ret_bwd(p, tables, s_ret, dy0, name="ret_bwd")
    dy0, dq_r = stage("grads2_send", (dy0, dq_r), g)
    dcq, dck, dcv, dg_d, dsmall, dal, ddt, dgain = gdn_bwd(conv, p, small, a_log, dt_bias, w["gdn_out_gain"], s_gdn, dy0, name="gdn_bwd")
    dconv = jnp.concatenate([dcq, dck, dcv], axis=1)
    dp_conv, g["gdn_conv_w"] = gdn_conv_bwd(p, w["gdn_conv_w"], dconv, name="gdn_conv_bwd")
    dp = jnp.concatenate([dq_r, dk_r, dv_r, dg_r, dp_conv, dg_d], axis=1)
    dhn0 = matmul(dp, w_main, tb=True, name="mix0_in_dx")
    dhn0 = matmul(dsmall, w_small, tb=True, add=dhn0, name="mix0_in_small_dx")
    d_w_main = matmul(hn0, dp, ta=True, name="mix0_in_dw")
    d_w_small = matmul(hn0, dsmall, ta=True, name="mix0_in_small_dw")
    g["ret_gdn_w_in"] = jnp.concatenate([d_w_main, d_w_small[:, :2 * N_HEADS]], axis=1)
    dx, dgm0 = norm_bwd(x, w["norm_mix"][0:1], dhn0, dh1, name="mix0_norm_bwd")

    g["gdn_a_log"] = dal[:, :N_HEADS]
    g["gdn_dt_bias"] = ddt[:, :N_HEADS]
    g["gdn_out_gain"] = dgain
    g["norm_mix"] = jnp.concatenate([dgm0, dgm1], axis=0)
    g["norm_ffn"] = jnp.concatenate([dgf0, dgf1], axis=0)
    g["ffn_conv_w"] = jnp.stack([dcw0, dcw1])
    g["ffn_conv_b"] = jnp.concatenate([dcb0, dcb1], axis=0)
    return loss, dx, g


WEIGHTS = ("norm_mix", "norm_ffn", "ret_gdn_w_in", "gdn_conv_w", "gdn_a_log", "gdn_dt_bias", "gdn_out_gain", "ret_gdn_w_out",
           "lru_w_in", "lru_conv_w", "lru_conv_b", "lru_w_a", "lru_b_a", "lru_w_x", "lru_b_x", "lru_lambda", "lru_w_out",
           "ffn_w_up", "ffn_conv_w", "ffn_conv_b", "ffn_w_down", "norm_final")
MATMUL_SHARDED = {"ret_gdn_w_in": 1, "ret_gdn_w_out": 0, "lru_w_in": 1, "lru_w_out": 0, "ffn_w_up": 2, "ffn_w_down": 1}
VECTOR_SHARDED = {"gdn_conv_w": 1, "lru_conv_w": 1, "lru_conv_b": 1, "lru_b_a": 1, "lru_b_x": 1, "lru_lambda": 1, "ffn_conv_w": 2}
SHARDED = {**MATMUL_SHARDED, **VECTOR_SHARDED}
REPLICATED = tuple(n for n in WEIGHTS if n not in SHARDED)
SQUEEZE = {"ret_gdn_w_in", "gdn_conv_w", "ret_gdn_w_out", "lru_w_in", "lru_conv_w", "lru_w_a", "lru_w_x", "lru_w_out"}
MIX_IN = MIX_MAIN + 2 * N_HEADS
BIG_ARRAYS = {
    "ret_gdn_w_in": ("ret_gdn_w_in", None, (D_MODEL, MIX_IN), (2, D_MODEL // 2, N_SHARD, MIX_IN // N_SHARD), (0, 2, 1, 3)),
    "ret_gdn_w_out": ("ret_gdn_w_out", None, (2 * GROUP, D_MODEL), (N_SHARD, 2, GROUP // N_SHARD, D_MODEL), (1, 0, 2, 3)),
    "lru_w_in": ("lru_w_in", None, (D_MODEL, 2 * D_MODEL), (2, D_MODEL // 2, N_SHARD, 2 * D_MODEL // N_SHARD), (0, 2, 1, 3)),
    "lru_w_out": ("lru_w_out", None, (D_MODEL, D_MODEL), (N_SHARD, 2, D_MODEL // (2 * N_SHARD), D_MODEL), (1, 0, 2, 3)),
    "ffn_w_up_0": ("ffn_w_up", 0, (D_MODEL, 2 * D_FF), (2, D_MODEL // 2, N_SHARD, 2 * D_FF // N_SHARD), (0, 2, 1, 3)),
    "ffn_w_up_1": ("ffn_w_up", 1, (D_MODEL, 2 * D_FF), (2, D_MODEL // 2, N_SHARD, 2 * D_FF // N_SHARD), (0, 2, 1, 3)),
    "ffn_w_down": ("ffn_w_down", None, (2, D_FF, D_MODEL), (2, N_SHARD, D_FF // N_SHARD, D_MODEL), (0, 1, 2, 3)),
}
GATHER_GROUPS = (("ret_gdn_w_in",), ("ret_gdn_w_out", "ffn_w_up_0", "ffn_w_down"), ("lru_w_in", "lru_w_out", "ffn_w_up_1"))
REDUCE_GROUPS = (("ffn_w_up_1",), ("lru_w_in", "lru_w_out"), ("ffn_w_up_0", "ffn_w_down"), ("ret_gdn_w_out", "ret_gdn_w_in"))
GATHER_COLLECTIVE_ID = 1
REDUCE_COLLECTIVE_ID = GATHER_COLLECTIVE_ID + len(GATHER_GROUPS)


def _local_view(name, a):
    if name in SQUEEZE:
        return a[0]
    if a.ndim == 1:
        return a[None, :]
    return a


def kernel(x, norm_mix, norm_ffn, ret_gdn_w_in, gdn_conv_w, gdn_a_log, gdn_dt_bias, gdn_out_gain, ret_gdn_w_out, lru_w_in, lru_conv_w, lru_conv_b, lru_w_a, lru_b_a, lru_w_x, lru_b_x, lru_lambda, lru_w_out, ffn_w_up, ffn_conv_w, ffn_conv_b, ffn_w_down, norm_final, loss_target, m_norm_mix, m_norm_ffn, m_ret_gdn_w_in, m_gdn_conv_w, m_gdn_a_log, m_gdn_dt_bias, m_gdn_out_gain, m_ret_gdn_w_out, m_lru_w_in, m_lru_conv_w, m_lru_conv_b, m_lru_w_a, m_lru_b_a, m_lru_w_x, m_lru_b_x, m_lru_lambda, m_lru_w_out, m_ffn_w_up, m_ffn_conv_w, m_ffn_conv_b, m_ffn_w_down, m_norm_final, v_norm_mix, v_norm_ffn, v_ret_gdn_w_in, v_gdn_conv_w, v_gdn_a_log, v_gdn_dt_bias, v_gdn_out_gain, v_ret_gdn_w_out, v_lru_w_in, v_lru_conv_w, v_lru_conv_b, v_lru_w_a, v_lru_b_a, v_lru_w_x, v_lru_b_x, v_lru_lambda, v_lru_w_out, v_ffn_w_up, v_ffn_conv_w, v_ffn_conv_b, v_ffn_w_down, v_norm_final):
    given = dict(norm_mix=norm_mix, norm_ffn=norm_ffn, ret_gdn_w_in=ret_gdn_w_in, gdn_conv_w=gdn_conv_w, gdn_a_log=gdn_a_log, gdn_dt_bias=gdn_dt_bias, gdn_out_gain=gdn_out_gain, ret_gdn_w_out=ret_gdn_w_out, lru_w_in=lru_w_in, lru_conv_w=lru_conv_w, lru_conv_b=lru_conv_b, lru_w_a=lru_w_a, lru_b_a=lru_b_a, lru_w_x=lru_w_x, lru_b_x=lru_b_x, lru_lambda=lru_lambda, lru_w_out=lru_w_out, ffn_w_up=ffn_w_up, ffn_conv_w=ffn_conv_w, ffn_conv_b=ffn_conv_b, ffn_w_down=ffn_w_down, norm_final=norm_final)
    mom1 = dict(norm_mix=m_norm_mix, norm_ffn=m_norm_ffn, ret_gdn_w_in=m_ret_gdn_w_in, gdn_conv_w=m_gdn_conv_w, gdn_a_log=m_gdn_a_log, gdn_dt_bias=m_gdn_dt_bias, gdn_out_gain=m_gdn_out_gain, ret_gdn_w_out=m_ret_gdn_w_out, lru_w_in=m_lru_w_in, lru_conv_w=m_lru_conv_w, lru_conv_b=m_lru_conv_b, lru_w_a=m_lru_w_a, lru_b_a=m_lru_b_a, lru_w_x=m_lru_w_x, lru_b_x=m_lru_b_x, lru_lambda=m_lru_lambda, lru_w_out=m_lru_w_out, ffn_w_up=m_ffn_w_up, ffn_conv_w=m_ffn_conv_w, ffn_conv_b=m_ffn_conv_b, ffn_w_down=m_ffn_w_down, norm_final=m_norm_final)
    mom2 = dict(norm_mix=v_norm_mix, norm_ffn=v_norm_ffn, ret_gdn_w_in=v_ret_gdn_w_in, gdn_conv_w=v_gdn_conv_w, gdn_a_log=v_gdn_a_log, gdn_dt_bias=v_gdn_dt_bias, gdn_out_gain=v_gdn_out_gain, ret_gdn_w_out=v_ret_gdn_w_out, lru_w_in=v_lru_w_in, lru_conv_w=v_lru_conv_w, lru_conv_b=v_lru_conv_b, lru_w_a=v_lru_w_a, lru_b_a=v_lru_b_a, lru_w_x=v_lru_w_x, lru_b_x=v_lru_b_x, lru_lambda=v_lru_lambda, lru_w_out=v_lru_w_out, ffn_w_up=v_ffn_w_up, ffn_conv_w=v_ffn_conv_w, ffn_conv_b=v_ffn_conv_b, ffn_w_down=v_ffn_w_down, norm_final=v_norm_final)

    local = {n: _local_view(n, a) for n, a in given.items()}

    core = lax.axis_index("c")
    chip = 2 * lax.axis_index("x") + lax.axis_index("y")
    is_my_chip = lax.broadcasted_iota(jnp.int32, (N_SHARD, 1, 1), 0) == chip

    def by_core(mine, other):
        return jnp.where(core == 0, jnp.stack([mine, other]), jnp.stack([other, mine]))

    vec_names, rp_names = list(VECTOR_SHARDED), list(REPLICATED)
    full = dict(zip(vec_names, all_gather_shards([local[n] for n in vec_names], [SHARDED[n] for n in vec_names], F32, 32, "p")))
    for n in rp_names:
        full[n] = local[n]
    in_flight = {}

    def launch(gi, after=None):
        halves = []
        for a in GATHER_GROUPS[gi]:
            weight, layer, _, split, perm = BIG_ARRAYS[a]
            shard = local[weight] if layer is None else local[weight][layer]
            halves.append(shard.astype(BF16).reshape((2,) + tuple(split[p] for p in perm)[2:]))
        if after is not None:
            halves, after = lax.optimization_barrier((halves, after))
        in_flight[gi] = (halves,) + gather_halves(halves, name=f"gather_weights_{gi}", collective_id=GATHER_COLLECTIVE_ID + gi)
        return after

    def land(gi, after):
        halves, lands, sibs = in_flight[gi]
        (lands, sibs), after = lax.optimization_barrier(((lands, sibs), after))
        for a, mine, got, passed in zip(GATHER_GROUPS[gi], halves, lands, sibs):
            weight, layer, full_shape, split, perm = BIG_ARRAYS[a]
            half_mine = jnp.where(is_my_chip, jnp.where(core == 0, mine[0], mine[1])[None], got)
            half_other = jnp.where(is_my_chip, jnp.where(core == 0, mine[1], mine[0])[None], passed)
            value = by_core(half_mine, half_other).transpose(perm).reshape(full_shape)
            if layer is None:
                full[weight] = value
            else:
                full.setdefault(weight, [None, None])[layer] = value
        return after

    reducing = {}

    def reduce_ready(gi, grads, then=None, extra=()):
        arrays = [grads[a].reshape(BIG_ARRAYS[a][3]).transpose(BIG_ARRAYS[a][4]) for a in REDUCE_GROUPS[gi]] + list(extra)
        scatter = [True] * len(REDUCE_GROUPS[gi]) + [False] * len(extra)
        reducing[gi], then = reduce_between_cores(arrays, scatter, tag=str(gi), collective_id=REDUCE_COLLECTIVE_ID + 3 * gi, before=then)
        return then

    def reduce_send(gi, then=None):
        reducing[gi], then = reduce_between_chips(reducing[gi], before=then)
        return then

    def stage(name, tensors, grads=None):
        if name == "start":
            launch(0)
            launch(1)
            return land(0, tensors)
        if name == "normed":
            return launch(2, tensors)
        if name in ("mixed", "layer0"):
            return land({"mixed": 1, "layer0": 2}[name], tensors)
        gi = int(name[len("grads")])
        return reduce_ready(gi, grads, tensors) if name.endswith("_ready") else reduce_send(gi, tensors)

    loss_part, dx, grads = local_step(x[0], loss_target[0], full, stage)
    loss = lax.psum(loss_part[0, 0], ("x", "y", "c"))

    small_names = rp_names + vec_names
    small_shapes = [grads[n].shape for n in small_names]
    small_rows = _pack_rows(sum(int(np.prod(s)) for s in small_shapes), 16)
    small = _pack([grads[n] for n in small_names], small_rows, F32).reshape(2, 1, small_rows // 2, LANES)
    last = len(REDUCE_GROUPS) - 1
    reduce_ready(last, grads, extra=[small])
    reduce_send(last)
    reduced, result = {}, {}

    def finish(gi, after):
        g_own, g_sib = reduce_finish(reducing[gi], after)
        reduced.update(zip(list(REDUCE_GROUPS[gi]) + ["small"], zip(g_own, g_sib)))

    def update(n):
        done = None
        for a in (k for k, spec in BIG_ARRAYS.items() if spec[0] == n):
            r, cols = reduced[a][0].shape
            layer = BIG_ARRAYS[a][1] or 0
            w3, m3, v3 = (t if BIG_ARRAYS[a][1] is not None else t.reshape(1, 2 * r, cols) for t in (given[n], mom1[n], mom2[n]))
            done = adamw_halves(w3, m3, v3, *reduced[a], layer=layer, prev=done, name=f"adamw_{a}")
        result[n] = done

    for gi in range(last):
        finish(gi, (dx, reducing[last][1]))
    late = {BIG_ARRAYS[a][0] for a in REDUCE_GROUPS[last]}
    for n in MATMUL_SHARDED:
        if n not in late:
            update(n)
    finish(last, tuple(result[n][0] for n in MATMUL_SHARDED if n not in late))
    for n in MATMUL_SHARDED:
        if n in late:
            update(n)

    g_small = dict(zip(small_names, _unpack(by_core(*reduced["small"]).reshape(small_rows, LANES), small_shapes)))
    for n in vec_names:
        size = local[n].shape[SHARDED[n]]
        g_small[n] = lax.dynamic_slice_in_dim(g_small[n], chip * size, size, axis=SHARDED[n])
    loc_shapes = [local[n].shape for n in small_names]
    loc_rows = _pack_rows(sum(int(np.prod(s)) for s in loc_shapes), 256)
    packs = [_pack([src[n] for n in small_names], loc_rows, F32) for src in (given, g_small, mom1, mom2)]
    d_s, m_s, v_s = adamw(*packs, name="adamw_small")
    for n, d, nm, nv in zip(small_names, _unpack(d_s, loc_shapes), _unpack(m_s, loc_shapes), _unpack(v_s, loc_shapes)):
        result[n] = (g_small[n], d, nm, nv)

    outs = [[result[n][k].reshape(given[n].shape) for n in WEIGHTS] for k in range(4)]
    return (loss, dx[None], *outs[0], *outs[1], *outs[2], *outs[3])
```

```python
import functools

import numpy as np
import jax
import jax.numpy as jnp
from jax import lax
from jax.experimental import pallas as pl
from jax.experimental.pallas import tpu as pltpu
from jax.experimental.pallas import tpu_sc as plsc

F32 = jnp.float32
BF16 = jnp.bfloat16
HI = lax.Precision.HIGHEST
MESH = pl.DeviceIdType.MESH

SEQ = 2048
D_MODEL = 1024
N_HEADS = 4
HEAD = 128
RET_CHUNK = 128
GDN_CHUNK = 64
GROUP = N_HEADS * HEAD
MIX_MAIN = 8 * GROUP
D_FF = 2816
LRU_BLOCKS = 8
LRU_C = 8.0
ROPE_BASE = 10000.0
EPS = 1e-6
N_SHARD = 4
LANES = 128

ADAM_LR, ADAM_B1, ADAM_B2, ADAM_EPS, ADAM_WD, ADAM_STEP = 0.001, 0.9, 0.999, 1e-08, 0.01, 10

VMEM_LIMIT_BYTES = 56 * 1024 * 1024

_roll = pltpu.roll


def _params(**kw):
    return pltpu.CompilerParams(vmem_limit_bytes=VMEM_LIMIT_BYTES, **kw)


def _sds(shape, dtype):
    return jax.ShapeDtypeStruct(tuple(shape), dtype)


def _shift_raw(x, d):
    n = x.shape[0]
    t = lax.broadcasted_iota(jnp.int32, x.shape, 0)
    if d > 0:
        return jnp.where(t >= d, _roll(x, d, 0), 0.0)
    return jnp.where(t < n + d, _roll(x, n + d, 0), 0.0)


@functools.partial(jax.custom_vjp, nondiff_argnums=(1,))
def shift_rows(x, d):
    return _shift_raw(x, d)


def _shift_fwd(x, d):
    return _shift_raw(x, d), None


def _shift_bwd(d, _, g):
    return (_shift_raw(g, -d),)


shift_rows.defvjp(_shift_fwd, _shift_bwd)


@jax.custom_vjp
def swap_halves(x):
    return _roll(x, HEAD // 2, 1)


def _swap_fwd(x):
    return _roll(x, HEAD // 2, 1), None


def _swap_bwd(_, g):
    return (_roll(g, HEAD // 2, 1),)


swap_halves.defvjp(_swap_fwd, _swap_bwd)


def _scan_raw(a, u, reverse):
    n = a.shape[0]
    t = lax.broadcasted_iota(jnp.int32, a.shape, 0)
    d = 1
    while d < n:
        if reverse:
            m = t < n - d
            a_s, u_s = _roll(a, n - d, 0), _roll(u, n - d, 0)
        else:
            m = t >= d
            a_s, u_s = _roll(a, d, 0), _roll(u, d, 0)
        u = a * jnp.where(m, u_s, 0.0) + u
        a = a * jnp.where(m, a_s, 1.0)
        d *= 2
    return u


@jax.custom_vjp
def lin_scan(a, u):
    return _scan_raw(a, u, False)


def _lin_scan_fwd(a, u):
    hs = _scan_raw(a, u, False)
    return hs, (a, hs)


def _lin_scan_bwd(res, g):
    a, hs = res
    lam = _scan_raw(_shift_raw(a, -1), g, True)
    return lam * _shift_raw(hs, 1), lam


lin_scan.defvjp(_lin_scan_fwd, _lin_scan_bwd)


def _bdot(a, b, dims=(((1,), (0,)), ((), ()))):
    return lax.dot_general(a.astype(BF16), b.astype(BF16), dims, preferred_element_type=F32)


def _each(f, *seqs):
    return tuple(f(*a) for a in zip(*seqs))


def _split_bf16(a):
    hi = a.astype(BF16)
    return hi, (a - hi.astype(F32)).astype(BF16)


def _dot3_raw(a_s, b_s):
    a_hl = _each(_split_bf16, a_s)
    b_hl = _each(_split_bf16, b_s)
    hh = _each(lambda a, b: _bdot(a[0], b[0]), a_hl, b_hl)
    hl = _each(lambda a, b: _bdot(a[0], b[1]), a_hl, b_hl)
    lh = _each(lambda a, b: _bdot(a[1], b[0]), a_hl, b_hl)
    return _each(lambda x, y, z: x + (y + z), hh, hl, lh)


@jax.custom_vjp
def dot3(a_s, b_s):
    return _dot3_raw(a_s, b_s)


def _dot3_fwd(a_s, b_s):
    return _dot3_raw(a_s, b_s), (a_s, b_s)


def _dot3_bwd(res, g_s):
    a_s, b_s = res
    return (_each(lambda g, b: _bdot(g, b, (((1,), (1,)), ((), ()))), g_s, b_s),
            _each(lambda a, g: _bdot(a, g, (((0,), (0,)), ((), ()))), a_s, g_s))


dot3.defvjp(_dot3_fwd, _dot3_bwd)


def _eye(n):
    i = lax.broadcasted_iota(jnp.int32, (n, n), 0)
    j = lax.broadcasted_iota(jnp.int32, (n, n), 1)
    return (i == j).astype(F32)


def _unit_lower_inverse_raw(lmats):
    n = lmats[0].shape[0]
    eye = _eye(n)
    ps = _each(lambda l: -l, lmats)
    invs = _each(lambda x: eye + x, ps)
    k = 1
    while 2 * k < n:
        ps = _each(lambda p: _bdot(p, p), ps)
        invs = _each(lambda inv, p: inv + _bdot(inv, p), invs, ps)
        k *= 2
    prods = _dot3_raw(lmats, invs)
    resids = _each(lambda inv, pr: eye - inv - pr, invs, prods)
    return _each(lambda inv, r: inv + _bdot(inv, r), invs, resids)


@jax.custom_vjp
def unit_lower_inverse(lmats):
    return _unit_lower_inverse_raw(lmats)


def _uli_fwd(lmats):
    invs = _unit_lower_inverse_raw(lmats)
    return invs, invs


def _uli_bwd(invs, g_s):
    ms = _each(lambda inv, g: _bdot(inv, g, (((0,), (0,)), ((), ()))), invs, g_s)
    return (_each(lambda m, inv: -_bdot(m, inv, (((1,), (1,)), ((), ()))), ms, invs),)


unit_lower_inverse.defvjp(_uli_fwd, _uli_bwd)


def _cumsum_raw(x, reverse):
    n = x.shape[0]
    t = lax.broadcasted_iota(jnp.int32, x.shape, 0)
    d = 1
    while d < n:
        if reverse:
            x = x + jnp.where(t < n - d, _roll(x, n - d, 0), 0.0)
        else:
            x = x + jnp.where(t >= d, _roll(x, d, 0), 0.0)
        d *= 2
    return x


@jax.custom_vjp
def cumsum_rows(x):
    return _cumsum_raw(x, False)


def _cumsum_fwd(x):
    return _cumsum_raw(x, False), None


def _cumsum_bwd(_, g):
    return (_cumsum_raw(g, True),)


cumsum_rows.defvjp(_cumsum_fwd, _cumsum_bwd)


_NT = (((1,), (1,)), ((), ()))
_TN = (((0,), (0,)), ((), ()))


def _softplus(x):
    return jnp.maximum(x, 0.0) + jnp.log1p(jnp.exp(-jnp.abs(x)))


def _expm1_nonpos(x):
    poly = x * (1.0 + x * (0.5 + x * (1.0 / 6 + x * (1.0 / 24 + x * (1.0 / 120 + x * (1.0 / 720))))))
    return jnp.where(x > -0.25, poly, jnp.exp(x) - 1.0)


def _rms(x):
    return x * lax.rsqrt(jnp.mean(x * x, axis=-1, keepdims=True) + EPS)


def _causal_conv(x, w, width):
    y = w[width - 1:width, :] * x
    for j in range(width - 1):
        y = y + w[j:j + 1, :] * shift_rows(x, width - 1 - j)
    return y


def _norm_fn(x, g):
    return _rms(x) * g


def _ffn_act_fn(ug, uv, wg, wv, bg, bv):
    return jax.nn.silu(_causal_conv(ug, wg, 3) + bg) * (_causal_conv(uv, wv, 3) + bv)


def _gdn_conv_fn(x, w):
    return jax.nn.silu(_causal_conv(x, w, 4))


def _lru_fn(gate, x, cw, cb, wa, ba, wx, bx, lam):
    xr = _causal_conv(x, cw, 4) + cb
    r = jax.nn.sigmoid(_bdot(xr, wa) + ba)
    i = jax.nn.sigmoid(_bdot(xr, wx) + bx)
    log_a = -LRU_C * r * _softplus(-lam)
    a = jnp.exp(log_a)
    u = jnp.sqrt(-_expm1_nonpos(2.0 * log_a)) * (i * xr)
    hs = lin_scan(a, u)
    return jax.nn.gelu(gate) * hs


def _ret_fn(qs, ks, vs, gates, states, cos2, sin2, dmasks, ktails, qdecs, cdecs):
    qrs = _each(lambda q: q * cos2 + swap_halves(q) * sin2, qs)
    krs = _each(lambda k: (k * cos2 + swap_halves(k) * sin2) * (HEAD ** -0.5), ks)
    scores = _each(lambda q, k, m: _bdot(q, k, _NT) * m, qrs, krs, dmasks)
    inter = _each(lambda q, d, s: _bdot(q * d, s), qrs, qdecs, states)
    os_ = _each(lambda sc, v, x: _bdot(sc, v) + x, scores, vs, inter)
    new_states = _each(lambda s, cd, k, kt, v: s * cd + _bdot(k * kt, v, _TN), states, cdecs, krs, ktails, vs)
    ys = _each(lambda o, g: _rms(o) * jax.nn.silu(g), os_, gates)
    return ys, new_states


def _pick_lane(x, lane_idx):
    lane = lax.broadcasted_iota(jnp.int32, x.shape, 1)
    return jnp.sum(jnp.where(lane == lane_idx, x, 0.0), axis=1, keepdims=True)


def _l2norm(x):
    return x * lax.rsqrt(jnp.sum(x * x, axis=-1, keepdims=True) + EPS)


def _gdn_fn(qcs, kcs, vcs, gates, small, a_log, dt_bias, gain, states):
    c = GDN_CHUNK
    heads = tuple(range(len(qcs)))
    qs = _each(lambda x: _l2norm(x) * (HEAD ** -0.5), qcs)
    ks = _each(_l2norm, kcs)
    betas = _each(lambda h: jax.nn.sigmoid(_pick_lane(small, h)), heads)
    gs = _each(lambda h: -jnp.exp(_pick_lane(a_log, h)) * _softplus(_pick_lane(small, h + N_HEADS) + _pick_lane(dt_bias, h)), heads)
    i = lax.broadcasted_iota(jnp.int32, (c, c), 0)
    j = lax.broadcasted_iota(jnp.int32, (c, c), 1)
    tril = i >= j
    gcs = _each(lambda g: cumsum_rows(jnp.broadcast_to(g, (c, LANES)))[:, :1], gs)
    gc_rows = _each(lambda gc: jnp.broadcast_to(gc, (c, c)), gcs)
    decays = _each(lambda r: jnp.where(tril, jnp.exp(jnp.where(tril, r - r.T, 0.0)), 0.0), gc_rows)
    kbs = _each(lambda k, b: k * b, ks, betas)
    lmats = _each(lambda kb, k, d: jnp.where(i > j, _bdot(kb, k, _NT) * d, 0.0), kbs, ks, decays)
    attns = _each(lambda q, k, d: jnp.where(tril, _bdot(q, k, _NT) * d, 0.0), qs, ks, decays)
    invs = unit_lower_inverse(lmats)
    us = dot3(invs, _each(lambda v, b: v * b, vcs, betas))
    ws = dot3(invs, _each(lambda kb, gc: kb * jnp.exp(gc), kbs, gcs))
    g_lasts = _each(lambda g: jnp.sum(g, axis=0, keepdims=True), gs)
    v_news = _each(lambda u, w, s: u - _bdot(w, s), us, ws, states)
    inter = _each(lambda q, gc, s: _bdot(q * jnp.exp(gc), s), qs, gcs, states)
    os_ = _each(lambda x, a, v: x + _bdot(a, v), inter, attns, v_news)
    new_states = _each(lambda s, gl, k, gc, v: s * jnp.exp(gl) + _bdot(k * jnp.exp(gl - gc), v, _TN), states, g_lasts, ks, gcs, v_news)
    ys = _each(lambda o, gate: _rms(o) * gain * jax.nn.silu(gate), os_, gates)
    return ys, new_states


def _final_fn(h, g, target):
    y = _rms(h) * g
    return 0.5 * jnp.sum(jnp.mean(jnp.square(y - target), axis=-1, keepdims=True), axis=0, keepdims=True)


def _tile(n, candidates):
    for t in candidates:
        if n % t == 0:
            return t
    raise ValueError(f"no tile for {n}")


def matmul(a, b, *, ta=False, tb=False, add=None, out_dtype=F32, tm=None, tn=None, split=None, layer=None, name):
    m = a.shape[1] if ta else a.shape[0]
    k = a.shape[0] if ta else a.shape[1]
    n = b.shape[0] if tb else b.shape[1]
    assert k == (b.shape[1] if tb else b.shape[0])
    out_shape, out_block, out_index = (m, n), None, lambda i, j: (i, j)
    if split is not None:
        dims4, perm = split
        out_shape = tuple(dims4[p] for p in perm)
        r, cols = out_shape[2:]
        tm, tn = tm or _tile(r, (512, 256, 128)), tn or _tile(cols, (1408, 1024, 512))
        rb, cb = r // tm, cols // tn
        out_block = (None, None, tm, tn)
        if perm == (0, 2, 1, 3):
            out_index = lambda i, j: (i // rb, j // cb, i % rb, j % cb)
        elif perm == (1, 0, 2, 3):
            out_index = lambda i, j: ((i // rb) % 2, i // (2 * rb), i % rb, j)
        else:
            raise ValueError(perm)
    tm = tm or _tile(m, (1024, 512, 1408, 256, 128))
    tn = tn or _tile(n, (512, 1408, 256, 128))
    aliases, prev = {}, None
    if layer is not None:
        index, count, prev = layer
        out_shape, out_block, out_index = (count, m, n), (None, tm, tn), lambda i, j: (index, i, j)
    dims = (((0 if ta else 1,), (1 if tb else 0,)), ((), ()))

    def body(a_ref, b_ref, *rest):
        acc = lax.dot_general(a_ref[...].astype(BF16), b_ref[...].astype(BF16), dims, preferred_element_type=F32)
        if add is not None:
            acc = acc + rest[0][...]
        rest[-1][...] = acc.astype(out_dtype)

    a_spec = pl.BlockSpec((k, tm), lambda i, j: (0, i)) if ta else pl.BlockSpec((tm, k), lambda i, j: (i, 0))
    b_spec = pl.BlockSpec((tn, k), lambda i, j: (j, 0)) if tb else pl.BlockSpec((k, tn), lambda i, j: (0, j))
    o_spec = pl.BlockSpec(out_block or (tm, tn), out_index)
    in_specs, args = [a_spec, b_spec], [a, b]
    if add is not None:
        in_specs.append(o_spec)
        args.append(add)
    if prev is not None:
        aliases = {len(args): 0}
        in_specs.append(pl.BlockSpec(memory_space=pl.ANY))
        args.append(prev)
    return pl.pallas_call(body, out_shape=_sds(out_shape, out_dtype), grid=(m // tm, n // tn), in_specs=in_specs,
                          out_specs=o_spec, input_output_aliases=aliases, compiler_params=_params(), name=name)(*args)


ROW_TILE = 256


def norm_fwd(x, g, *, name):
    t, d = x.shape

    def body(x_ref, g_ref, o_ref):
        o_ref[...] = _norm_fn(x_ref[...], g_ref[...]).astype(BF16)

    return pl.pallas_call(body, out_shape=_sds((t, d), BF16), grid=(t // ROW_TILE,),
                          in_specs=[pl.BlockSpec((ROW_TILE, d), lambda i: (i, 0)), pl.BlockSpec((1, d), lambda i: (0, 0))],
                          out_specs=pl.BlockSpec((ROW_TILE, d), lambda i: (i, 0)), compiler_params=_params(), name=name)(x, g)


def norm_bwd(x, g, dy, dres, *, name):
    t, d = x.shape

    def body(x_ref, g_ref, dy_ref, dres_ref, dx_ref, dg_ref):
        _, vjp = jax.vjp(_norm_fn, x_ref[...], g_ref[...])
        dx, dg = vjp(dy_ref[...])
        dx_ref[...] = dx + dres_ref[...]

        @pl.when(pl.program_id(0) == 0)
        def _():
            dg_ref[...] = jnp.zeros_like(dg_ref)

        dg_ref[...] += dg

    row = pl.BlockSpec((ROW_TILE, d), lambda i: (i, 0))
    vec = pl.BlockSpec((1, d), lambda i: (0, 0))
    return pl.pallas_call(body, out_shape=(_sds((t, d), F32), _sds((1, d), F32)), grid=(t // ROW_TILE,),
                          in_specs=[row, vec, row, row], out_specs=(row, vec), compiler_params=_params(), name=name)(x, g, dy, dres)


def final_fwd_bwd(h, g, target, *, name):
    t, d = h.shape

    def body(h_ref, g_ref, t_ref, loss_ref, dh_ref, dg_ref):
        tgt = t_ref[...]
        loss, vjp = jax.vjp(lambda hh, gg: _final_fn(hh, gg, tgt), h_ref[...], g_ref[...])
        dh, dg = vjp(jnp.ones((1, 1), F32))
        dh_ref[...] = dh

        @pl.when(pl.program_id(0) == 0)
        def _():
            dg_ref[...] = jnp.zeros_like(dg_ref)
            loss_ref[...] = jnp.zeros_like(loss_ref)

        dg_ref[...] += dg
        loss_ref[...] += jnp.broadcast_to(loss, loss_ref.shape)

    row = pl.BlockSpec((ROW_TILE, d), lambda i: (i, 0))
    vec = pl.BlockSpec((1, d), lambda i: (0, 0))
    return pl.pallas_call(body, out_shape=(_sds((1, LANES), F32), _sds((t, d), F32), _sds((1, d), F32)), grid=(t // ROW_TILE,),
                          in_specs=[row, vec, row], out_specs=(pl.BlockSpec((1, LANES), lambda i: (0, 0)), row, vec),
                          compiler_params=_params(), name=name)(h, g, target)


FFN_FWD_COLS = 256
FFN_BWD_COLS = 128


def ffn_act_fwd(u, cw, cb, *, name):
    t = u.shape[0]
    w = FFN_FWD_COLS
    nb = D_FF // w

    def body(ug_ref, uv_ref, wg_ref, wv_ref, bg_ref, bv_ref, o_ref):
        o_ref[...] = _ffn_act_fn(ug_ref[...], uv_ref[...], wg_ref[...], wv_ref[...], bg_ref[...], bv_ref[...]).astype(BF16)

    def col(rows, off):
        return pl.BlockSpec((rows, w), lambda j: (0, j + off))

    return pl.pallas_call(body, out_shape=_sds((t, D_FF), BF16), grid=(nb,),
                          in_specs=[col(t, 0), col(t, nb), col(3, 0), col(3, nb), col(1, 0), col(1, nb)],
                          out_specs=col(t, 0), compiler_params=_params(), name=name)(u, u, cw, cw, cb, cb)


def ffn_act_bwd(u, cw, cb, da, *, name):
    t = u.shape[0]
    w = FFN_BWD_COLS
    nb = D_FF // w

    def body(ug_ref, uv_ref, wg_ref, wv_ref, bg_ref, bv_ref, da_ref, dug_ref, duv_ref, dwg_ref, dwv_ref, dbg_ref, dbv_ref):
        _, vjp = jax.vjp(_ffn_act_fn, ug_ref[...], uv_ref[...], wg_ref[...], wv_ref[...], bg_ref[...], bv_ref[...])
        dug, duv, dwg, dwv, dbg, dbv = vjp(da_ref[...])
        dug_ref[...] = dug.astype(BF16)
        duv_ref[...] = duv.astype(BF16)
        dwg_ref[...] = dwg
        dwv_ref[...] = dwv
        dbg_ref[...] = dbg
        dbv_ref[...] = dbv

    def col(rows, off):
        return pl.BlockSpec((rows, w), lambda j: (0, j + off))

    outs = pl.pallas_call(
        body, out_shape=(_sds((t, D_FF), BF16), _sds((t, D_FF), BF16), _sds((3, D_FF), F32), _sds((3, D_FF), F32),
                         _sds((1, D_FF), F32), _sds((1, D_FF), F32)),
        grid=(nb,), in_specs=[col(t, 0), col(t, nb), col(3, 0), col(3, nb), col(1, 0), col(1, nb), col(t, 0)],
        out_specs=(col(t, 0), col(t, 0), col(3, 0), col(3, 0), col(1, 0), col(1, 0)), compiler_params=_params(), name=name,
    )(u, u, cw, cw, cb, cb, da)
    dug, duv, dwg, dwv, dbg, dbv = outs
    return dug, duv, jnp.concatenate([dwg, dwv], axis=1), jnp.concatenate([dbg, dbv], axis=1)


GDN_CONV_COLS = 256
GDN_CONV_OFF = 4 * GROUP


def gdn_conv_fwd(p, cw, *, name):
    t = p.shape[0]
    w = GDN_CONV_COLS
    nb = 3 * GROUP // w
    off = GDN_CONV_OFF // w

    def body(x_ref, w_ref, o_ref):
        o_ref[...] = _gdn_conv_fn(x_ref[...], w_ref[...])

    return pl.pallas_call(body, out_shape=_sds((t, 3 * GROUP), F32), grid=(nb,),
                          in_specs=[pl.BlockSpec((t, w), lambda j: (0, j + off)), pl.BlockSpec((4, w), lambda j: (0, j))],
                          out_specs=pl.BlockSpec((t, w), lambda j: (0, j)), compiler_params=_params(), name=name)(p, cw)


def gdn_conv_bwd(p, cw, dc, *, name):
    t = p.shape[0]
    w = GDN_CONV_COLS
    nb = 3 * GROUP // w
    off = GDN_CONV_OFF // w

    def body(x_ref, w_ref, dc_ref, dx_ref, dw_ref):
        _, vjp = jax.vjp(_gdn_conv_fn, x_ref[...], w_ref[...])
        dx, dw = vjp(dc_ref[...])
        dx_ref[...] = dx.astype(BF16)
        dw_ref[...] = dw

    blk = pl.BlockSpec((t, w), lambda j: (0, j))
    wblk = pl.BlockSpec((4, w), lambda j: (0, j))
    return pl.pallas_call(body, out_shape=(_sds((t, 3 * GROUP), BF16), _sds((4, 3 * GROUP), F32)), grid=(nb,),
                          in_specs=[pl.BlockSpec((t, w), lambda j: (0, j + off)), wblk, blk], out_specs=(blk, wblk),
                          compiler_params=_params(), name=name)(p, cw, dc)


def _lru_specs(t):
    w = D_MODEL // LRU_BLOCKS
    gate = pl.BlockSpec((t, w), lambda j: (0, j))
    xin = pl.BlockSpec((t, w), lambda j: (0, j + LRU_BLOCKS))
    cw = pl.BlockSpec((4, w), lambda j: (0, j))
    vec = pl.BlockSpec((1, w), lambda j: (0, j))
    mat = pl.BlockSpec((None, w, w), lambda j: (j, 0, 0))
    return gate, xin, cw, vec, mat


def lru_fwd(gx, cw, cb, wa, ba, wx, bx, lam, *, name):
    t = gx.shape[0]
    gate, xin, cws, vec, mat = _lru_specs(t)

    def body(g_ref, x_ref, cw_ref, cb_ref, wa_ref, ba_ref, wx_ref, bx_ref, lam_ref, o_ref):
        o_ref[...] = _lru_fn(g_ref[...], x_ref[...], cw_ref[...], cb_ref[...], wa_ref[...], ba_ref[...], wx_ref[...],
                             bx_ref[...], lam_ref[...]).astype(BF16)

    return pl.pallas_call(body, out_shape=_sds((t, D_MODEL), BF16), grid=(LRU_BLOCKS,),
                          in_specs=[gate, xin, cws, vec, mat, vec, mat, vec, vec], out_specs=gate,
                          compiler_params=_params(), name=name)(gx, gx, cw, cb, wa, ba, wx, bx, lam)


def lru_bwd(gx, cw, cb, wa, ba, wx, bx, lam, dy, *, name):
    t = gx.shape[0]
    gate, xin, cws, vec, mat = _lru_specs(t)

    def body(g_ref, x_ref, cw_ref, cb_ref, wa_ref, ba_ref, wx_ref, bx_ref, lam_ref, dy_ref,
             dg_ref, dx_ref, dcw_ref, dcb_ref, dwa_ref, dba_ref, dwx_ref, dbx_ref, dlam_ref):
        _, vjp = jax.vjp(_lru_fn, g_ref[...], x_ref[...], cw_ref[...], cb_ref[...], wa_ref[...], ba_ref[...], wx_ref[...],
                         bx_ref[...], lam_ref[...])
        dg, dx, dcw, dcb, dwa, dba, dwx, dbx, dlam = vjp(dy_ref[...])
        dg_ref[...] = dg.astype(BF16)
        dx_ref[...] = dx.astype(BF16)
        dcw_ref[...] = dcw
        dcb_ref[...] = dcb
        dwa_ref[...] = dwa
        dba_ref[...] = dba
        dwx_ref[...] = dwx
        dbx_ref[...] = dbx
        dlam_ref[...] = dlam

    d = D_MODEL
    w = d // LRU_BLOCKS
    out_shape = (_sds((t, d), BF16), _sds((t, d), BF16), _sds((4, d), F32), _sds((1, d), F32), _sds((LRU_BLOCKS, w, w), F32),
                 _sds((1, d), F32), _sds((LRU_BLOCKS, w, w), F32), _sds((1, d), F32), _sds((1, d), F32))
    return pl.pallas_call(body, out_shape=out_shape, grid=(LRU_BLOCKS,),
                          in_specs=[gate, xin, cws, vec, mat, vec, mat, vec, vec, gate],
                          out_specs=(gate, gate, cws, vec, mat, vec, mat, vec, vec), compiler_params=_params(), name=name,
                          )(gx, gx, cw, cb, wa, ba, wx, bx, lam, dy)


def _ret_tables():
    half = HEAD // 2
    inv_freq = (np.float32(ROPE_BASE) ** (-np.arange(half, dtype=np.float32) / np.float32(half))).astype(np.float32)
    ang = (np.arange(SEQ, dtype=np.float32)[:, None] * inv_freq[None, :]).astype(np.float64)
    cos2 = np.concatenate([np.cos(ang), np.cos(ang)], axis=1).astype(np.float32)
    sin2 = np.concatenate([-np.sin(ang), np.sin(ang)], axis=1).astype(np.float32)
    c = RET_CHUNK
    log_gamma = np.log1p(-np.exp2(-5.0 - np.arange(N_HEADS, dtype=np.float64)))
    idx = np.arange(c, dtype=np.float64)
    rel = idx[:, None] - idx[None, :]
    dmask = np.where(rel >= 0, np.exp(log_gamma[:, None, None] * np.maximum(rel, 0.0)), 0.0)
    ones = np.ones((N_HEADS, c, HEAD))
    ktail = np.exp(log_gamma[:, None] * (c - 1 - idx))[:, :, None] * ones
    qdec = np.exp(log_gamma[:, None] * (idx + 1.0))[:, :, None] * ones
    cdec = np.exp(log_gamma * c)[:, None, None] * ones
    return tuple(jnp.asarray(a, F32) for a in (cos2, sin2, dmask, ktail, qdec, cdec))


def _ret_specs(rev):
    c = RET_CHUNK
    nc = SEQ // c

    def n_of(n):
        return nc - 1 - n if rev else n

    def group(off):
        return pl.BlockSpec((c, GROUP), lambda n: (n_of(n), off))

    tab = pl.BlockSpec((c, HEAD), lambda n: (n_of(n), 0))
    const = pl.BlockSpec((N_HEADS, c, HEAD), lambda n: (0, 0, 0))
    state = pl.BlockSpec((N_HEADS, None, HEAD, HEAD), lambda n: (0, n_of(n), 0, 0))
    return group, tab, const, state, nc


def _head(h):
    return slice(h * HEAD, (h + 1) * HEAD)


def ret_fwd(p, tables, *, name):
    group, tab, const, state, nc = _ret_specs(False)

    def body(q_ref, k_ref, v_ref, g_ref, cos_ref, sin_ref, dm_ref, kt_ref, qd_ref, cd_ref, y_ref, st_ref, s_scr):
        @pl.when(pl.program_id(0) == 0)
        def _():
            s_scr[...] = jnp.zeros_like(s_scr)

        heads = range(N_HEADS)
        states = tuple(s_scr[h] for h in heads)
        ys, new_states = _ret_fn(*(tuple(r[:, _head(h)] for h in heads) for r in (q_ref, k_ref, v_ref, g_ref)), states,
                                 cos_ref[...], sin_ref[...], *(tuple(r[h] for h in heads) for r in (dm_ref, kt_ref, qd_ref, cd_ref)))
        for h in heads:
            st_ref[h] = states[h]
            y_ref[:, _head(h)] = ys[h].astype(BF16)
            s_scr[h] = new_states[h]

    return pl.pallas_call(
        body, out_shape=(_sds((SEQ, GROUP), BF16), _sds((N_HEADS, nc, HEAD, HEAD), F32)), grid=(nc,),
        in_specs=[group(0), group(1), group(2), group(3), tab, tab, const, const, const, const],
        out_specs=(group(0), state), scratch_shapes=[pltpu.VMEM((N_HEADS, HEAD, HEAD), F32)], compiler_params=_params(), name=name,
    )(p, p, p, p, *tables)


def ret_bwd(p, tables, states, dy, *, name):
    group, tab, const, state, nc = _ret_specs(True)

    def body(q_ref, k_ref, v_ref, g_ref, cos_ref, sin_ref, dm_ref, kt_ref, qd_ref, cd_ref, st_ref, dy_ref,
             dq_ref, dk_ref, dv_ref, dg_ref, ds_scr):
        @pl.when(pl.program_id(0) == 0)
        def _():
            ds_scr[...] = jnp.zeros_like(ds_scr)

        heads = range(N_HEADS)
        consts = (cos_ref[...], sin_ref[...], *(tuple(r[h] for h in heads) for r in (dm_ref, kt_ref, qd_ref, cd_ref)))
        _, vjp = jax.vjp(lambda *a: _ret_fn(*a, *consts), *(tuple(r[:, _head(h)] for h in heads) for r in (q_ref, k_ref, v_ref, g_ref)),
                         tuple(st_ref[h] for h in heads))
        dqs, dks, dvs, dgs, dss = vjp((tuple(dy_ref[:, _head(h)] for h in heads), tuple(ds_scr[h] for h in heads)))
        for h in heads:
            dq_ref[:, _head(h)] = dqs[h].astype(BF16)
            dk_ref[:, _head(h)] = dks[h].astype(BF16)
            dv_ref[:, _head(h)] = dvs[h].astype(BF16)
            dg_ref[:, _head(h)] = dgs[h].astype(BF16)
            ds_scr[h] = dss[h]

    out = _sds((SEQ, GROUP), BF16)
    return pl.pallas_call(
        body, out_shape=(out, out, out, out), grid=(nc,),
        in_specs=[group(0), group(1), group(2), group(3), tab, tab, const, const, const, const, state, group(0)],
        out_specs=(group(0), group(0), group(0), group(0)), scratch_shapes=[pltpu.VMEM((N_HEADS, HEAD, HEAD), F32)],
        compiler_params=_params(), name=name,
    )(p, p, p, p, *tables, states, dy)


def _gdn_specs(rev):
    c = GDN_CHUNK
    nc = SEQ // c

    def n_of(n):
        return nc - 1 - n if rev else n

    def group(off):
        return pl.BlockSpec((c, GROUP), lambda n: (n_of(n), off))

    small = pl.BlockSpec((c, LANES), lambda n: (n_of(n), 0))
    vec = pl.BlockSpec((1, LANES), lambda n: (0, 0))
    state = pl.BlockSpec((N_HEADS, None, HEAD, HEAD), lambda n: (0, n_of(n), 0, 0))
    return group, small, vec, state, nc


GDN_GATE_GROUP = 7


def gdn_fwd(conv, p, small, a_log, dt_bias, gain, *, name):
    group, sm, vec, state, nc = _gdn_specs(False)

    def body(q_ref, k_ref, v_ref, g_ref, sm_ref, al_ref, dt_ref, gn_ref, y_ref, st_ref, s_scr):
        @pl.when(pl.program_id(0) == 0)
        def _():
            s_scr[...] = jnp.zeros_like(s_scr)

        states = tuple(s_scr[h] for h in range(N_HEADS))
        ys, new_states = _gdn_fn(*(tuple(r[:, _head(h)] for h in range(N_HEADS)) for r in (q_ref, k_ref, v_ref, g_ref)),
                                 sm_ref[...], al_ref[...], dt_ref[...], gn_ref[...], states)
        for h in range(N_HEADS):
            st_ref[h] = states[h]
            y_ref[:, _head(h)] = ys[h].astype(BF16)
            s_scr[h] = new_states[h]

    return pl.pallas_call(
        body, out_shape=(_sds((SEQ, GROUP), BF16), _sds((N_HEADS, nc, HEAD, HEAD), F32)), grid=(nc,),
        in_specs=[group(0), group(1), group(2), group(GDN_GATE_GROUP), sm, vec, vec, vec], out_specs=(group(0), state),
        scratch_shapes=[pltpu.VMEM((N_HEADS, HEAD, HEAD), F32)], compiler_params=_params(), name=name,
    )(conv, conv, conv, p, small, a_log, dt_bias, gain)


def gdn_bwd(conv, p, small, a_log, dt_bias, gain, states, dy, *, name):
    group, sm, vec, state, nc = _gdn_specs(True)

    def body(q_ref, k_ref, v_ref, g_ref, sm_ref, al_ref, dt_ref, gn_ref, st_ref, dy_ref,
             dq_ref, dk_ref, dv_ref, dg_ref, dsm_ref, dal_ref, ddt_ref, dgn_ref, ds_scr):
        @pl.when(pl.program_id(0) == 0)
        def _():
            ds_scr[...] = jnp.zeros_like(ds_scr)
            dal_ref[...] = jnp.zeros_like(dal_ref)
            ddt_ref[...] = jnp.zeros_like(ddt_ref)
            dgn_ref[...] = jnp.zeros_like(dgn_ref)

        per_head = tuple(tuple(r[:, _head(h)] for h in range(N_HEADS)) for r in (q_ref, k_ref, v_ref, g_ref))
        _, vjp = jax.vjp(_gdn_fn, *per_head, sm_ref[...], al_ref[...], dt_ref[...], gn_ref[...],
                         tuple(st_ref[h] for h in range(N_HEADS)))
        cts = (tuple(dy_ref[:, _head(h)] for h in range(N_HEADS)), tuple(ds_scr[h] for h in range(N_HEADS)))
        dqs, dks, dvs, dgs, dsm, dal, ddt, dgn, dss = vjp(cts)
        for h in range(N_HEADS):
            dq_ref[:, _head(h)] = dqs[h]
            dk_ref[:, _head(h)] = dks[h]
            dv_ref[:, _head(h)] = dvs[h]
            dg_ref[:, _head(h)] = dgs[h].astype(BF16)
            ds_scr[h] = dss[h]
        dsm_ref[...] = dsm
        dal_ref[...] += dal
        ddt_ref[...] += ddt
        dgn_ref[...] += dgn

    f = _sds((SEQ, GROUP), F32)
    pv = _sds((1, LANES), F32)
    return pl.pallas_call(
        body, out_shape=(f, f, f, _sds((SEQ, GROUP), BF16), _sds((SEQ, LANES), F32), pv, pv, pv), grid=(nc,),
        in_specs=[group(0), group(1), group(2), group(GDN_GATE_GROUP), sm, vec, vec, vec, state, group(1)],
        out_specs=(group(0), group(0), group(0), group(0), sm, vec, vec, vec), scratch_shapes=[pltpu.VMEM((N_HEADS, HEAD, HEAD), F32)],
        compiler_params=_params(), name=name,
    )(conv, conv, conv, p, small, a_log, dt_bias, gain, states, dy)


PACK_ROW_TILE = 1024


def adamw(w, g, m, v, *, name):
    r = w.shape[0]
    tr = _tile(r, (PACK_ROW_TILE, 256, 128, 64, 32, 16, 8))

    def body(w_ref, g_ref, m_ref, v_ref, d_ref, nm_ref, nv_ref):
        gg = g_ref[...]
        nm = ADAM_B1 * m_ref[...] + (1.0 - ADAM_B1) * gg
        nv = ADAM_B2 * v_ref[...] + (1.0 - ADAM_B2) * jnp.square(gg)
        m_hat = nm / (1.0 - ADAM_B1 ** ADAM_STEP)
        v_hat = nv / (1.0 - ADAM_B2 ** ADAM_STEP)
        d_ref[...] = -ADAM_LR * (m_hat / (jnp.sqrt(v_hat) + ADAM_EPS) + ADAM_WD * w_ref[...])
        nm_ref[...] = nm
        nv_ref[...] = nv

    blk = pl.BlockSpec((tr, LANES), lambda i: (i, 0))
    o = _sds((r, LANES), F32)
    return pl.pallas_call(body, out_shape=(o, o, o), grid=(r // tr,), in_specs=[blk] * 4, out_specs=(blk, blk, blk),
                          compiler_params=_params(), name=name)(w, g, m, v)


ELEMENTWISE_BLOCK_BYTES = 2 * 1024 * 1024


def _row_tile(r, c):
    best = None
    for tr in range(8, r + 1, 8):
        if r % tr == 0 and tr * c * 4 <= ELEMENTWISE_BLOCK_BYTES:
            best = tr
    if best is None:
        raise ValueError(f"no row tile for ({r}, {c})")
    return best


def _core_index():
    return lax.axis_index("c").astype(jnp.int32).reshape(1)


def _chip_index():
    return (2 * lax.axis_index("x") + lax.axis_index("y")).astype(jnp.int32).reshape(1)


def adamw_halves(w, m, v, g_own, g_sib, *, layer=0, prev=None, name):
    n_layers, rows, c = w.shape
    r = rows // 2
    tr = _row_tile(r, c)
    nb = r // tr

    def body(c_ref, w_ref, m_ref, v_ref, own_ref, sib_ref, *rest):
        g_ref, d_ref, nm_ref, nv_ref = rest[-4:]
        gg = jnp.where(pl.program_id(0) == c_ref[0], own_ref[...], sib_ref[...])
        nm = ADAM_B1 * m_ref[...] + (1.0 - ADAM_B1) * gg
        nv = ADAM_B2 * v_ref[...] + (1.0 - ADAM_B2) * jnp.square(gg)
        m_hat = nm / (1.0 - ADAM_B1 ** ADAM_STEP)
        v_hat = nv / (1.0 - ADAM_B2 ** ADAM_STEP)
        g_ref[...] = gg
        d_ref[...] = -ADAM_LR * (m_hat / (jnp.sqrt(v_hat) + ADAM_EPS) + ADAM_WD * w_ref[...])
        nm_ref[...] = nm
        nv_ref[...] = nv

    full = pl.BlockSpec((None, tr, c), lambda h, i, cr: (layer, h * nb + i, 0))
    half = pl.BlockSpec((tr, c), lambda h, i, cr: (i, 0))
    o = _sds((n_layers, rows, c), F32)
    prev = list(prev or ())
    gs = pltpu.PrefetchScalarGridSpec(num_scalar_prefetch=1, grid=(2, nb), in_specs=[full, full, full, half, half] + [_ANY] * len(prev),
                                      out_specs=(full, full, full, full))
    n_fixed = 6
    return pl.pallas_call(body, out_shape=(o, o, o, o), grid_spec=gs, compiler_params=_params(), name=name,
                          input_output_aliases={n_fixed + k: k for k in range(len(prev))})(
        _core_index(), w, m, v, g_own, g_sib, *prev)


def add_core_halves(g2, land, *, out_dtype, name):
    _, ns, r, cols = g2.shape
    tr = _row_tile(r, cols)

    def body(c_ref, a_ref, b_ref, o_ref):
        o_ref[...] = (a_ref[...] + b_ref[...]).astype(out_dtype)

    gs = pltpu.PrefetchScalarGridSpec(
        num_scalar_prefetch=1, grid=(ns, r // tr),
        in_specs=[pl.BlockSpec((None, None, tr, cols), lambda s, i, cr: (cr[0], s, i, 0)),
                  pl.BlockSpec((None, tr, cols), lambda s, i, cr: (s, i, 0))],
        out_specs=pl.BlockSpec((None, tr, cols), lambda s, i, cr: (s, i, 0)))
    return pl.pallas_call(body, out_shape=_sds((ns, r, cols), out_dtype), grid_spec=gs, compiler_params=_params(), name=name)(
        _core_index(), g2, land)


def sum_over_chips(own, land, *, scatter, name):
    _, r, cols = own.shape
    tr = _row_tile(r, cols)

    def body(mine_ref, own_ref, l0, l1, l2, l3, o_ref):
        mine = mine_ref[0]
        mine_val = own_ref[...]
        acc = None
        for s, l_ref in enumerate((l0, l1, l2, l3)):
            val = jnp.where(mine == s, mine_val, l_ref[...]).astype(F32)
            acc = val if acc is None else acc + val
        o_ref[...] = acc

    def slot(s):
        return pl.BlockSpec((None, tr, cols), lambda i, mr: (jnp.where(mr[0] == s, (s + 1) % N_SHARD, s), i, 0))

    own_spec = pl.BlockSpec((None, tr, cols), lambda i, mr: (mr[0] if scatter else 0, i, 0))
    gs = pltpu.PrefetchScalarGridSpec(num_scalar_prefetch=1, grid=(r // tr,), in_specs=[own_spec] + [slot(s) for s in range(N_SHARD)],
                                      out_specs=pl.BlockSpec((tr, cols), lambda i, mr: (i, 0)))
    return pl.pallas_call(body, out_shape=_sds((r, cols), F32), grid_spec=gs, compiler_params=_params(), name=name)(
        _chip_index(), own, land, land, land, land)


_ANY = pl.BlockSpec(memory_space=pl.ANY)


def xy_exchange(src, *, scatter, name):
    rh = src.shape[1]

    def body(src_ref, land_ref, send_sems, recv_sems, loc_sem):
        x, y, c = lax.axis_index("x"), lax.axis_index("y"), lax.axis_index("c")
        mine = 2 * x + y
        peers = [(1 - x, y), (x, 1 - y), (1 - x, 1 - y)]

        def piece(shard):
            return src_ref.at[shard] if scatter else src_ref.at[c]

        def copy(k, px, py, dst_slot):
            return pltpu.make_async_remote_copy(src_ref=piece(2 * px + py), dst_ref=land_ref.at[dst_slot], send_sem=send_sems.at[k],
                                                recv_sem=recv_sems.at[k], device_id=(px, py, c), device_id_type=MESH)

        keep = pltpu.make_async_copy(piece(mine), land_ref.at[mine], loc_sem)
        keep.start()
        sends = [copy(k, px, py, mine) for k, (px, py) in enumerate(peers)]
        for cp in sends:
            cp.start()
        for cp in sends:
            cp.wait_send()
        for k, (px, py) in enumerate(peers):
            copy(k, px, py, 2 * px + py).wait_recv()
        keep.wait()

    return pl.pallas_call(body, out_shape=_sds((N_SHARD, rh, LANES), src.dtype), in_specs=[_ANY], out_specs=_ANY,
                          scratch_shapes=[pltpu.SemaphoreType.DMA((3,)), pltpu.SemaphoreType.DMA((3,)), pltpu.SemaphoreType.DMA(())],
                          name=name)(src)


def core_exchange(src, *, send_other_half, name):
    def body(src_ref, out_ref, send_sem, recv_sem, loc_sem):
        x, y, c = lax.axis_index("x"), lax.axis_index("y"), lax.axis_index("c")
        if send_other_half:
            cp = pltpu.make_async_remote_copy(src_ref=src_ref.at[1 - c], dst_ref=out_ref, send_sem=send_sem, recv_sem=recv_sem,
                                              device_id=(x, y, 1 - c), device_id_type=MESH)
            cp.start()
            cp.wait_send()
            cp.wait_recv()
        else:
            keep = pltpu.make_async_copy(src_ref, out_ref.at[c], loc_sem)
            keep.start()
            cp = pltpu.make_async_remote_copy(src_ref=src_ref, dst_ref=out_ref.at[c], send_sem=send_sem, recv_sem=recv_sem,
                                              device_id=(x, y, 1 - c), device_id_type=MESH)
            cp.start()
            cp.wait_send()
            pltpu.make_async_remote_copy(src_ref=src_ref, dst_ref=out_ref.at[1 - c], send_sem=send_sem, recv_sem=recv_sem,
                                         device_id=(x, y, 1 - c), device_id_type=MESH).wait_recv()
            keep.wait()

    out_shape = _sds(src.shape[1:], src.dtype) if send_other_half else _sds((2,) + src.shape, src.dtype)
    return pl.pallas_call(body, out_shape=out_shape, in_specs=[_ANY], out_specs=_ANY,
                          scratch_shapes=[pltpu.SemaphoreType.DMA(()), pltpu.SemaphoreType.DMA(()), pltpu.SemaphoreType.DMA(())],
                          name=name)(src)


def _comm_call(body, ins, out_shapes, sem_counts, name):
    return pl.pallas_call(body, out_shape=tuple(out_shapes), in_specs=[_ANY] * len(ins), out_specs=tuple([_ANY] * len(out_shapes)),
                          scratch_shapes=[pltpu.SemaphoreType.DMA((k,)) for k in sem_counts], name=name)(*ins)


def _sequencer_call(body, ins, out_shapes, sem_counts, name, collective_id):
    return pl.kernel(body, out_type=list(out_shapes), mesh=plsc.ScalarSubcoreMesh(axis_name="sequencer", num_cores=1), name=name,
                     scratch_types=[pltpu.SemaphoreType.DMA((k,)) for k in sem_counts],
                     compiler_params=pltpu.CompilerParams(collective_id=collective_id))(*ins)


def _handshake(peers):
    barrier = pltpu.get_barrier_semaphore()
    for peer in peers:
        pl.semaphore_signal(barrier, inc=1, device_id=peer, device_id_type=MESH)
    pl.semaphore_wait(barrier, len(peers))


def _xy_peers(x, y):
    return [(1 - x, y), (x, 1 - y), (1 - x, 1 - y)]


def gather_halves(halves, *, name, collective_id):
    n = len(halves)

    def body(*refs):
        ins, lands, sibs = refs[:n], refs[n:2 * n], refs[2 * n:3 * n]
        ici_send, ici_recv, d2d_send, d2d_recv = refs[3 * n:]
        x, y, c = lax.axis_index("x"), lax.axis_index("y"), lax.axis_index("c")
        mine = 2 * x + y
        peers = _xy_peers(x, y)
        _handshake([(px, py, c) for px, py in peers] + [(x, y, 1 - c)])

        def ici(i, k, slot):
            px, py = peers[k]
            return pltpu.make_async_remote_copy(src_ref=ins[i].at[c], dst_ref=lands[i].at[slot], send_sem=ici_send.at[3 * i + k],
                                                recv_sem=ici_recv.at[3 * i + k], device_id=(px, py, c), device_id_type=MESH)

        def pass_on(i, k):
            px, py = peers[k]
            slot = 2 * px + py
            return pltpu.make_async_remote_copy(src_ref=lands[i].at[slot], dst_ref=sibs[i].at[slot], send_sem=d2d_send.at[3 * i + k],
                                                recv_sem=d2d_recv.at[3 * i + k], device_id=(x, y, 1 - c), device_id_type=MESH)

        sends = [ici(i, k, mine) for i in range(n) for k in range(3)]
        for cp in sends:
            cp.start()
        passed = []
        for i in range(n):
            for k in range(3):
                px, py = peers[k]
                ici(i, k, 2 * px + py).wait_recv()
                cp = pass_on(i, k)
                cp.start()
                passed.append(cp)
        for cp in passed:
            cp.wait_recv()
        for cp in sends + passed:
            cp.wait_send()

    outs = [_sds((N_SHARD,) + h.shape[1:], h.dtype) for h in halves]
    res = _sequencer_call(body, halves, outs + outs, [3 * n] * 4, name, collective_id)
    return res[:n], res[n:]


def send_other_half(arrays, *, name, collective_id):
    n = len(arrays)

    def body(*refs):
        ins, lands = refs[:n], refs[n:2 * n]
        send_sems, recv_sems = refs[2 * n:]
        x, y, c = lax.axis_index("x"), lax.axis_index("y"), lax.axis_index("c")
        _handshake([(x, y, 1 - c)])
        copies = [pltpu.make_async_remote_copy(src_ref=ins[i].at[1 - c], dst_ref=lands[i], send_sem=send_sems.at[i],
                                               recv_sem=recv_sems.at[i], device_id=(x, y, 1 - c), device_id_type=MESH) for i in range(n)]
        for cp in copies:
            cp.start()
        for cp in copies:
            cp.wait_recv()
        for cp in copies:
            cp.wait_send()

    return _sequencer_call(body, arrays, [_sds(a.shape[1:], a.dtype) for a in arrays], [n, n], name, collective_id)


_HBM = pl.BlockSpec(memory_space=pltpu.HBM)
_SEM = pl.BlockSpec(memory_space=pltpu.SEMAPHORE)
_SPLIT_COPY = dict(has_side_effects=pltpu.SideEffectType.DATAFLOW_SIDE_EFFECTING)


def _chip_copy(ins, lands, send_sems, recv_sems, scatter, i, k, receive):
    x, y, c = lax.axis_index("x"), lax.axis_index("y"), lax.axis_index("c")
    px, py = _xy_peers(x, y)[k]
    theirs, mine = 2 * px + py, 2 * x + y
    src = ins[i].at[theirs] if scatter[i] else ins[i].at[0]
    return pltpu.make_async_remote_copy(src_ref=src, dst_ref=lands[i].at[theirs if receive else mine], send_sem=send_sems.at[3 * i + k],
                                        recv_sem=recv_sems.at[3 * i + k], device_id=(px, py, c), device_id_type=MESH)


def send_to_chips_start(arrays, scatter, *, name):
    n = len(arrays)

    def body(*refs):
        send_sems, recv_sems = refs[2 * n], refs[2 * n + 1]
        ins, lands = refs[2 * n + 2:3 * n + 2], refs[3 * n + 2:4 * n + 2]
        token = refs[4 * n + 2]
        for i in range(n):
            for k in range(3):
                _chip_copy(ins, lands, send_sems, recv_sems, scatter, i, k, receive=False).start()
        token[...] = jnp.zeros_like(token)

    land_shapes = [(N_SHARD,) + a.shape[1:] for a in arrays]
    operands = [pltpu.with_memory_space_constraint(a, pltpu.HBM) for a in arrays]
    operands += [pltpu.with_memory_space_constraint(lax.empty(s, a.dtype), pltpu.HBM) for s, a in zip(land_shapes, arrays)]
    out_shape = ([pltpu.SemaphoreType.DMA((3 * n,)), pltpu.SemaphoreType.DMA((3 * n,))] + [pltpu.HBM(a.shape, a.dtype) for a in arrays]
                 + [pltpu.HBM(s, a.dtype) for s, a in zip(land_shapes, arrays)] + [_sds((8, LANES), F32)])
    res = pl.pallas_call(body, name=name, out_shape=out_shape, in_specs=[_HBM] * (2 * n),
                         out_specs=[_SEM, _SEM] + [_HBM] * (2 * n) + [pl.BlockSpec(memory_space=pltpu.VMEM)],
                         input_output_aliases={i: 2 + i for i in range(2 * n)}, compiler_params=pltpu.CompilerParams(**_SPLIT_COPY))(*operands)
    return (res[0], res[1], res[2:2 + n], res[2 + n:2 + 2 * n], scatter), res[-1]


def send_to_chips_wait(state, after, *, name):
    send_sems, recv_sems, arrays, lands, scatter = state
    n = len(arrays)

    def body(*refs):
        ins, landing = refs[:n], refs[n:2 * n]
        send_sems, recv_sems = refs[2 * n], refs[2 * n + 1]
        for i in range(n):
            for k in range(3):
                _chip_copy(ins, landing, send_sems, recv_sems, scatter, i, k, receive=True).wait_recv()
        for i in range(n):
            for k in range(3):
                _chip_copy(ins, landing, send_sems, recv_sems, scatter, i, k, receive=False).wait_send()

    out_shape = [pltpu.HBM(a.shape, a.dtype) for a in list(arrays) + list(lands)]
    res = pl.pallas_call(body, name=name, out_shape=out_shape, in_specs=[_HBM] * (2 * n) + [_SEM, _SEM] + [_ANY] * len(after),
                         out_specs=[_HBM] * (2 * n), input_output_aliases={i: i for i in range(2 * n)},
                         compiler_params=pltpu.CompilerParams(**_SPLIT_COPY))(*arrays, *lands, send_sems, recv_sems, *after)
    return res[:n], res[n:]


def swap_with_other_core(arrays, *, name, collective_id):
    n = len(arrays)

    def body(*refs):
        ins, lands = refs[:n], refs[n:2 * n]
        send_sems, recv_sems = refs[2 * n:]
        x, y, c = lax.axis_index("x"), lax.axis_index("y"), lax.axis_index("c")
        _handshake([(x, y, 1 - c)])
        copies = [pltpu.make_async_remote_copy(src_ref=ins[i], dst_ref=lands[i], send_sem=send_sems.at[i], recv_sem=recv_sems.at[i],
                                               device_id=(x, y, 1 - c), device_id_type=MESH) for i in range(n)]
        for cp in copies:
            cp.start()
        for cp in copies:
            cp.wait_recv()
        for cp in copies:
            cp.wait_send()

    return _sequencer_call(body, arrays, [_sds(a.shape, a.dtype) for a in arrays], [n, n], name, collective_id)


def _pack_rows(n_elems, row_multiple):
    rows = -(-n_elems // LANES)
    return -(-rows // row_multiple) * row_multiple


def _pack(arrays, rows, dtype):
    flat = jnp.concatenate([a.reshape(-1).astype(dtype) for a in arrays])
    return jnp.pad(flat, (0, rows * LANES - flat.shape[0])).reshape(rows, LANES)


def _unpack(packed, shapes):
    flat = packed.reshape(-1)
    out, off = [], 0
    for s in shapes:
        n = int(np.prod(s))
        out.append(flat[off:off + n].reshape(s))
        off += n
    return out


def all_gather_shards(shards, axes, dtype, row_multiple, tag):
    shapes = [s.shape for s in shards]
    rows = _pack_rows(sum(int(np.prod(s)) for s in shapes), row_multiple)
    packed = _pack(shards, rows, dtype).reshape(2, rows // 2, LANES)
    land = xy_exchange(packed, scatter=False, name=f"gather_xy_{tag}")
    both = core_exchange(land, send_other_half=False, name=f"gather_c_{tag}")
    per_shard = jnp.swapaxes(both, 0, 1).reshape(N_SHARD, rows, LANES)
    pieces = [_unpack(per_shard[s], shapes) for s in range(N_SHARD)]
    return [jnp.concatenate([pieces[s][i] for s in range(N_SHARD)], axis=ax) for i, ax in enumerate(axes)]


def _ordered_before(first, then):
    if then is None:
        return first, None
    return lax.optimization_barrier((first, then))


def reduce_between_cores(arrays, scatter, *, tag, collective_id, before=None):
    arrays, before = _ordered_before(arrays, before)
    land = send_other_half(arrays, name=f"reduce_core_send_{tag}", collective_id=collective_id)
    return (arrays, land, scatter, tag, collective_id), before


def reduce_between_chips(state, before=None):
    arrays, land, scatter, tag, collective_id = state
    chip = [add_core_halves(a, l, out_dtype=BF16 if sc else F32, name=f"reduce_core_add_{tag}_{i}")
            for i, (a, l, sc) in enumerate(zip(arrays, land, scatter))]
    sending, token = send_to_chips_start(chip, scatter, name=f"reduce_chip_start_{tag}")
    token, before = _ordered_before(token, before)
    return (sending, token, scatter, tag, collective_id), before


def reduce_finish(state, after):
    sending, token, scatter, tag, collective_id = state
    chip, land = send_to_chips_wait(sending, tuple(after) + (token,), name=f"reduce_chip_wait_{tag}")
    own = [sum_over_chips(ch, l, scatter=sc, name=f"reduce_chip_add_{tag}_{i}") for i, (ch, l, sc) in enumerate(zip(chip, land, scatter))]
    sib = swap_with_other_core(own, name=f"reduce_core_swap_{tag}", collective_id=collective_id + 2)
    return own, sib


def _ffn_layer_fwd(h, norm_g, w_up, cw, cb, w_down, tag):
    hn = norm_fwd(h, norm_g, name=f"ffn_norm_{tag}")
    u = matmul(hn, w_up, name=f"ffn_up_{tag}")
    act = ffn_act_fwd(u, cw, cb, name=f"ffn_act_{tag}")
    out = matmul(act, w_down, add=h, name=f"ffn_down_{tag}")
    return out, (h, hn, u, act)


def _travel_layout(array):
    return BIG_ARRAYS[array][3], BIG_ARRAYS[array][4]


def _ffn_layer_bwd(saved, dout, norm_g, w_up, cw, cb, w_down, tag, d_w_down_other=None):
    h, hn, u, act = saved
    dact = matmul(dout, w_down, tb=True, name=f"ffn_down_dx_{tag}")
    d_w_down = matmul(act, dout, ta=True, layer=(int(tag), 2, d_w_down_other), name=f"ffn_down_dw_{tag}")
    dug, duv, dcw, dcb = ffn_act_bwd(u, cw, cb, dact, name=f"ffn_act_bwd_{tag}")
    du = jnp.concatenate([dug, duv], axis=1)
    dhn = matmul(du, w_up, tb=True, name=f"ffn_up_dx_{tag}")
    d_w_up = matmul(hn, du, ta=True, split=_travel_layout(f"ffn_w_up_{tag}"), name=f"ffn_up_dw_{tag}")
    dh, dg = norm_bwd(h, norm_g, dhn, dout, name=f"ffn_norm_bwd_{tag}")
    return dh, dg, d_w_up, dcw, dcb, d_w_down


def local_step(x, target, w, stage=lambda name, tensors, grads=None: tensors):
    g = {}
    tables = _ret_tables()
    x = stage("start", x)
    w_in = w["ret_gdn_w_in"]
    w_main = w_in[:, :MIX_MAIN]
    w_small = jnp.pad(w_in[:, MIX_MAIN:], ((0, 0), (0, LANES - 2 * N_HEADS)))
    a_log = jnp.pad(w["gdn_a_log"], ((0, 0), (0, LANES - N_HEADS)))
    dt_bias = jnp.pad(w["gdn_dt_bias"], ((0, 0), (0, LANES - N_HEADS)))

    hn0 = stage("normed", norm_fwd(x, w["norm_mix"][0:1], name="mix0_norm"))
    p = matmul(hn0, w_main, name="mix0_in")
    small = matmul(hn0, w_small, name="mix0_in_small")
    y_ret, s_ret = ret_fwd(p, tables, name="ret_fwd")
    conv = gdn_conv_fwd(p, w["gdn_conv_w"], name="gdn_conv")
    y_gdn, s_gdn = gdn_fwd(conv, p, small, a_log, dt_bias, w["gdn_out_gain"], name="gdn_fwd")
    y0 = stage("mixed", jnp.concatenate([y_ret, y_gdn], axis=1))
    h1 = matmul(y0, w["ret_gdn_w_out"], add=x, name="mix0_out")
    h2, ffn0 = _ffn_layer_fwd(h1, w["norm_ffn"][0:1], w["ffn_w_up"][0], w["ffn_conv_w"][0], w["ffn_conv_b"][0:1], w["ffn_w_down"][0], "0")
    h2 = stage("layer0", h2)

    hn1 = norm_fwd(h2, w["norm_mix"][1:2], name="mix1_norm")
    gx = matmul(hn1, w["lru_w_in"], name="mix1_in")
    lru_p = (w["lru_conv_w"], w["lru_conv_b"], w["lru_w_a"], w["lru_b_a"], w["lru_w_x"], w["lru_b_x"], w["lru_lambda"])
    y1 = lru_fwd(gx, *lru_p, name="lru_fwd")
    h3 = matmul(y1, w["lru_w_out"], add=h2, name="mix1_out")
    h4, ffn1 = _ffn_layer_fwd(h3, w["norm_ffn"][1:2], w["ffn_w_up"][1], w["ffn_conv_w"][1], w["ffn_conv_b"][1:2], w["ffn_w_down"][1], "1")

    loss, dh4, g["norm_final"] = final_fwd_bwd(h4, w["norm_final"], target, name="final")

    dh3, dgf1, dwu1, dcw1, dcb1, dwd1 = _ffn_layer_bwd(ffn1, dh4, w["norm_ffn"][1:2], w["ffn_w_up"][1], w["ffn_conv_w"][1],
                                                     w["ffn_conv_b"][1:2], w["ffn_w_down"][1], "1")
    g["ffn_w_up_1"] = dwu1
    dh3 = stage("grads0_ready", dh3, g)
    dy1 = matmul(dh3, w["lru_w_out"], tb=True, name="mix1_out_dx")
    g["lru_w_out"] = matmul(y1, dh3, ta=True, split=_travel_layout("lru_w_out"), name="mix1_out_dw")
    dgate, dxr, g["lru_conv_w"], g["lru_conv_b"], g["lru_w_a"], g["lru_b_a"], g["lru_w_x"], g["lru_b_x"], g["lru_lambda"] = lru_bwd(
        gx, *lru_p, dy1, name="lru_bwd")
    dgx = stage("grads0_send", jnp.concatenate([dgate, dxr], axis=1), g)
    dhn1 = matmul(dgx, w["lru_w_in"], tb=True, name="mix1_in_dx")
    g["lru_w_in"] = matmul(hn1, dgx, ta=True, split=_travel_layout("lru_w_in"), name="mix1_in_dw")
    dh2, dgm1 = norm_bwd(h2, w["norm_mix"][1:2], dhn1, dh3, name="mix1_norm_bwd")
    dh2 = stage("grads1_ready", dh2, g)

    dh1, dgf0, dwu0, dcw0, dcb0, dwd0 = _ffn_layer_bwd(ffn0, dh2, w["norm_ffn"][0:1], w["ffn_w_up"][0], w["ffn_conv_w"][0],
                                                     w["ffn_conv_b"][0:1], w["ffn_w_down"][0], "0", dwd1)
    g["ffn_w_up_0"] = dwu0
    g["ffn_w_down"] = dwd0
    dh1 = stage("grads2_ready", stage("grads1_send", dh1, g), g)
    dy0 = matmul(dh1, w["ret_gdn_w_out"], tb=True, name="mix0_out_dx")
    g["ret_gdn_w_out"] = matmul(y0, dh1, ta=True, split=_travel_layout("ret_gdn_w_out"), name="mix0_out_dw")
    dq_r, dk_r, dv_r, dg_r = ret_bwd(p, tables, s_ret, dy0, name="ret_bwd")
    dy0, dq_r = stage("grads2_send", (dy0, dq_r), g)
    dcq, dck, dcv, dg_d, dsmall, dal, ddt, dgain = gdn_bwd(conv, p, small, a_log, dt_bias, w["gdn_out_gain"], s_gdn, dy0, name="gdn_bwd")
    dconv = jnp.concatenate([dcq, dck, dcv], axis=1)
    dp_conv, g["gdn_conv_w"] = gdn_conv_bwd(p, w["gdn_conv_w"], dconv, name="gdn_conv_bwd")
    dp = jnp.concatenate([dq_r, dk_r, dv_r, dg_r, dp_conv, dg_d], axis=1)
    dhn0 = matmul(dp, w_main, tb=True, name="mix0_in_dx")
    dhn0 = matmul(dsmall, w_small, tb=True, add=dhn0, name="mix0_in_small_dx")
    d_w_main = matmul(hn0, dp, ta=True, name="mix0_in_dw")
    d_w_small = matmul(hn0, dsmall, ta=True, name="mix0_in_small_dw")
    g["ret_gdn_w_in"] = jnp.concatenate([d_w_main, d_w_small[:, :2 * N_HEADS]], axis=1)
    dx, dgm0 = norm_bwd(x, w["norm_mix"][0:1], dhn0, dh1, name="mix0_norm_bwd")

    g["gdn_a_log"] = dal[:, :N_HEADS]
    g["gdn_dt_bias"] = ddt[:, :N_HEADS]
    g["gdn_out_gain"] = dgain
    g["norm_mix"] = jnp.concatenate([dgm0, dgm1], axis=0)
    g["norm_ffn"] = jnp.concatenate([dgf0, dgf1], axis=0)
    g["ffn_conv_w"] = jnp.stack([dcw0, dcw1])
    g["ffn_conv_b"] = jnp.concatenate([dcb0, dcb1], axis=0)
    return loss, dx, g


WEIGHTS = ("norm_mix", "norm_ffn", "ret_gdn_w_in", "gdn_conv_w", "gdn_a_log", "gdn_dt_bias", "gdn_out_gain", "ret_gdn_w_out",
           "lru_w_in", "lru_conv_w", "lru_conv_b", "lru_w_a", "lru_b_a", "lru_w_x", "lru_b_x", "lru_lambda", "lru_w_out",
           "ffn_w_up", "ffn_conv_w", "ffn_conv_b", "ffn_w_down", "norm_final")
MATMUL_SHARDED = {"ret_gdn_w_in": 1, "ret_gdn_w_out": 0, "lru_w_in": 1, "lru_w_out": 0, "ffn_w_up": 2, "ffn_w_down": 1}
VECTOR_SHARDED = {"gdn_conv_w": 1, "lru_conv_w": 1, "lru_conv_b": 1, "lru_b_a": 1, "lru_b_x": 1, "lru_lambda": 1, "ffn_conv_w": 2}
SHARDED = {**MATMUL_SHARDED, **VECTOR_SHARDED}
REPLICATED = tuple(n for n in WEIGHTS if n not in SHARDED)
SQUEEZE = {"ret_gdn_w_in", "gdn_conv_w", "ret_gdn_w_out", "lru_w_in", "lru_conv_w", "lru_w_a", "lru_w_x", "lru_w_out"}
MIX_IN = MIX_MAIN + 2 * N_HEADS
BIG_ARRAYS = {
    "ret_gdn_w_in": ("ret_gdn_w_in", None, (D_MODEL, MIX_IN), (2, D_MODEL // 2, N_SHARD, MIX_IN // N_SHARD), (0, 2, 1, 3)),
    "ret_gdn_w_out": ("ret_gdn_w_out", None, (2 * GROUP, D_MODEL), (N_SHARD, 2, GROUP // N_SHARD, D_MODEL), (1, 0, 2, 3)),
    "lru_w_in": ("lru_w_in", None, (D_MODEL, 2 * D_MODEL), (2, D_MODEL // 2, N_SHARD, 2 * D_MODEL // N_SHARD), (0, 2, 1, 3)),
    "lru_w_out": ("lru_w_out", None, (D_MODEL, D_MODEL), (N_SHARD, 2, D_MODEL // (2 * N_SHARD), D_MODEL), (1, 0, 2, 3)),
    "ffn_w_up_0": ("ffn_w_up", 0, (D_MODEL, 2 * D_FF), (2, D_MODEL // 2, N_SHARD, 2 * D_FF // N_SHARD), (0, 2, 1, 3)),
    "ffn_w_up_1": ("ffn_w_up", 1, (D_MODEL, 2 * D_FF), (2, D_MODEL // 2, N_SHARD, 2 * D_FF // N_SHARD), (0, 2, 1, 3)),
    "ffn_w_down": ("ffn_w_down", None, (2, D_FF, D_MODEL), (2, N_SHARD, D_FF // N_SHARD, D_MODEL), (0, 1, 2, 3)),
}
GATHER_GROUPS = (("ret_gdn_w_in",), ("ret_gdn_w_out", "ffn_w_up_0", "ffn_w_down"), ("lru_w_in", "lru_w_out", "ffn_w_up_1"))
REDUCE_GROUPS = (("ffn_w_up_1",), ("lru_w_in", "lru_w_out"), ("ffn_w_up_0", "ffn_w_down"), ("ret_gdn_w_out", "ret_gdn_w_in"))
BLOCK_WEIGHTS = ("lru_w_a", "lru_w_x")
GATHER_COLLECTIVE_ID = 1
REDUCE_COLLECTIVE_ID = GATHER_COLLECTIVE_ID + len(GATHER_GROUPS)


def _local_view(name, a):
    if name in SQUEEZE:
        return a[0]
    if a.ndim == 1:
        return a[None, :]
    return a


def kernel(x, norm_mix, norm_ffn, ret_gdn_w_in, gdn_conv_w, gdn_a_log, gdn_dt_bias, gdn_out_gain, ret_gdn_w_out, lru_w_in, lru_conv_w, lru_conv_b, lru_w_a, lru_b_a, lru_w_x, lru_b_x, lru_lambda, lru_w_out, ffn_w_up, ffn_conv_w, ffn_conv_b, ffn_w_down, norm_final, loss_target, m_norm_mix, m_norm_ffn, m_ret_gdn_w_in, m_gdn_conv_w, m_gdn_a_log, m_gdn_dt_bias, m_gdn_out_gain, m_ret_gdn_w_out, m_lru_w_in, m_lru_conv_w, m_lru_conv_b, m_lru_w_a, m_lru_b_a, m_lru_w_x, m_lru_b_x, m_lru_lambda, m_lru_w_out, m_ffn_w_up, m_ffn_conv_w, m_ffn_conv_b, m_ffn_w_down, m_norm_final, v_norm_mix, v_norm_ffn, v_ret_gdn_w_in, v_gdn_conv_w, v_gdn_a_log, v_gdn_dt_bias, v_gdn_out_gain, v_ret_gdn_w_out, v_lru_w_in, v_lru_conv_w, v_lru_conv_b, v_lru_w_a, v_lru_b_a, v_lru_w_x, v_lru_b_x, v_lru_lambda, v_lru_w_out, v_ffn_w_up, v_ffn_conv_w, v_ffn_conv_b, v_ffn_w_down, v_norm_final):
    given = dict(norm_mix=norm_mix, norm_ffn=norm_ffn, ret_gdn_w_in=ret_gdn_w_in, gdn_conv_w=gdn_conv_w, gdn_a_log=gdn_a_log, gdn_dt_bias=gdn_dt_bias, gdn_out_gain=gdn_out_gain, ret_gdn_w_out=ret_gdn_w_out, lru_w_in=lru_w_in, lru_conv_w=lru_conv_w, lru_conv_b=lru_conv_b, lru_w_a=lru_w_a, lru_b_a=lru_b_a, lru_w_x=lru_w_x, lru_b_x=lru_b_x, lru_lambda=lru_lambda, lru_w_out=lru_w_out, ffn_w_up=ffn_w_up, ffn_conv_w=ffn_conv_w, ffn_conv_b=ffn_conv_b, ffn_w_down=ffn_w_down, norm_final=norm_final)
    mom1 = dict(norm_mix=m_norm_mix, norm_ffn=m_norm_ffn, ret_gdn_w_in=m_ret_gdn_w_in, gdn_conv_w=m_gdn_conv_w, gdn_a_log=m_gdn_a_log, gdn_dt_bias=m_gdn_dt_bias, gdn_out_gain=m_gdn_out_gain, ret_gdn_w_out=m_ret_gdn_w_out, lru_w_in=m_lru_w_in, lru_conv_w=m_lru_conv_w, lru_conv_b=m_lru_conv_b, lru_w_a=m_lru_w_a, lru_b_a=m_lru_b_a, lru_w_x=m_lru_w_x, lru_b_x=m_lru_b_x, lru_lambda=m_lru_lambda, lru_w_out=m_lru_w_out, ffn_w_up=m_ffn_w_up, ffn_conv_w=m_ffn_conv_w, ffn_conv_b=m_ffn_conv_b, ffn_w_down=m_ffn_w_down, norm_final=m_norm_final)
    mom2 = dict(norm_mix=v_norm_mix, norm_ffn=v_norm_ffn, ret_gdn_w_in=v_ret_gdn_w_in, gdn_conv_w=v_gdn_conv_w, gdn_a_log=v_gdn_a_log, gdn_dt_bias=v_gdn_dt_bias, gdn_out_gain=v_gdn_out_gain, ret_gdn_w_out=v_ret_gdn_w_out, lru_w_in=v_lru_w_in, lru_conv_w=v_lru_conv_w, lru_conv_b=v_lru_conv_b, lru_w_a=v_lru_w_a, lru_b_a=v_lru_b_a, lru_w_x=v_lru_w_x, lru_b_x=v_lru_b_x, lru_lambda=v_lru_lambda, lru_w_out=v_lru_w_out, ffn_w_up=v_ffn_w_up, ffn_conv_w=v_ffn_conv_w, ffn_conv_b=v_ffn_conv_b, ffn_w_down=v_ffn_w_down, norm_final=v_norm_final)

    local = {n: _local_view(n, a) for n, a in given.items()}

    core = lax.axis_index("c")
    chip = 2 * lax.axis_index("x") + lax.axis_index("y")
    is_my_chip = lax.broadcasted_iota(jnp.int32, (N_SHARD, 1, 1), 0) == chip

    def by_core(mine, other):
        return jnp.where(core == 0, jnp.stack([mine, other]), jnp.stack([other, mine]))

    vec_names, rp_names = list(VECTOR_SHARDED), list(REPLICATED)
    full = dict(zip(vec_names, all_gather_shards([local[n] for n in vec_names], [SHARDED[n] for n in vec_names], F32, 32, "p")))
    for n in rp_names:
        full[n] = local[n]
    in_flight = {}

    def launch(gi, after=None):
        halves = []
        for a in GATHER_GROUPS[gi]:
            weight, layer, _, split, perm = BIG_ARRAYS[a]
            shard = local[weight] if layer is None else local[weight][layer]
            halves.append(shard.astype(BF16).reshape((2,) + tuple(split[p] for p in perm)[2:]))
        if after is not None:
            halves, after = lax.optimization_barrier((halves, after))
        in_flight[gi] = (halves,) + gather_halves(halves, name=f"gather_weights_{gi}", collective_id=GATHER_COLLECTIVE_ID + gi)
        return after

    def land(gi, after):
        halves, lands, sibs = in_flight[gi]
        (lands, sibs), after = lax.optimization_barrier(((lands, sibs), after))
        for a, mine, got, passed in zip(GATHER_GROUPS[gi], halves, lands, sibs):
            weight, layer, full_shape, split, perm = BIG_ARRAYS[a]
            half_mine = jnp.where(is_my_chip, jnp.where(core == 0, mine[0], mine[1])[None], got)
            half_other = jnp.where(is_my_chip, jnp.where(core == 0, mine[1], mine[0])[None], passed)
            value = by_core(half_mine, half_other).transpose(perm).reshape(full_shape)
            if layer is None:
                full[weight] = value
            else:
                full.setdefault(weight, [None, None])[layer] = value
        return after

    reducing = {}

    def reduce_ready(gi, grads, then=None, extra=()):
        def travelling(a):
            split, perm = _travel_layout(a)
            return grads[a] if grads[a].ndim == 4 else grads[a].reshape(split).transpose(perm)

        arrays = [travelling(a) for a in REDUCE_GROUPS[gi]] + list(extra)
        scatter = [True] * len(REDUCE_GROUPS[gi]) + [False] * len(extra)
        reducing[gi], then = reduce_between_cores(arrays, scatter, tag=str(gi), collective_id=REDUCE_COLLECTIVE_ID + 3 * gi, before=then)
        return then

    def reduce_send(gi, then=None):
        reducing[gi], then = reduce_between_chips(reducing[gi], before=then)
        return then

    def stage(name, tensors, grads=None):
        if name == "start":
            launch(0)
            launch(1)
            return land(0, tensors)
        if name == "normed":
            return launch(2, tensors)
        if name in ("mixed", "layer0"):
            return land({"mixed": 1, "layer0": 2}[name], tensors)
        gi = int(name[len("grads")])
        return reduce_ready(gi, grads, tensors) if name.endswith("_ready") else reduce_send(gi, tensors)

    loss_part, dx, grads = local_step(x[0], loss_target[0], full, stage)
    loss = lax.psum(loss_part[0, 0], ("x", "y", "c"))

    small_names = [n for n in rp_names if n not in BLOCK_WEIGHTS] + vec_names
    small_shapes = [grads[n].shape for n in small_names]
    small_rows = _pack_rows(sum(int(np.prod(s)) for s in small_shapes), 16)
    small = _pack([grads[n] for n in small_names], small_rows, F32).reshape(2, 1, small_rows // 2, LANES)
    last = len(REDUCE_GROUPS) - 1
    halves_of_blocks = [grads[n].reshape(2, 1, LRU_BLOCKS * HEAD // 2, HEAD) for n in BLOCK_WEIGHTS]
    reduce_ready(last, grads, extra=[small] + halves_of_blocks)
    reduce_send(last)
    reduced, result = {}, {}

    def finish(gi, after):
        g_own, g_sib = reduce_finish(reducing[gi], after)
        reduced.update(zip(list(REDUCE_GROUPS[gi]) + ["small"] + list(BLOCK_WEIGHTS), zip(g_own, g_sib)))

    def update(n):
        done = None
        for a in (k for k, spec in BIG_ARRAYS.items() if spec[0] == n):
            r, cols = reduced[a][0].shape
            layer = BIG_ARRAYS[a][1] or 0
            w3, m3, v3 = (t if BIG_ARRAYS[a][1] is not None else t.reshape(1, 2 * r, cols) for t in (given[n], mom1[n], mom2[n]))
            done = adamw_halves(w3, m3, v3, *reduced[a], layer=layer, prev=done, name=f"adamw_{a}")
        result[n] = done

    for gi in range(last):
        finish(gi, (dx, reducing[last][1]))
    late = {BIG_ARRAYS[a][0] for a in REDUCE_GROUPS[last]}
    for n in MATMUL_SHARDED:
        if n not in late:
            update(n)
    finish(last, tuple(result[n][0] for n in MATMUL_SHARDED if n not in late))
    for n in MATMUL_SHARDED:
        if n in late:
            update(n)

    for n in BLOCK_WEIGHTS:
        w3, m3, v3 = (t.reshape(1, LRU_BLOCKS * HEAD, HEAD) for t in (given[n], mom1[n], mom2[n]))
        result[n] = adamw_halves(w3, m3, v3, *reduced[n], name=f"adamw_{n}")

    g_small = dict(zip(small_names, _unpack(by_core(*reduced["small"]).reshape(small_rows, LANES), small_shapes)))
    for n in vec_names:
        size = local[n].shape[SHARDED[n]]
        g_small[n] = lax.dynamic_slice_in_dim(g_small[n], chip * size, size, axis=SHARDED[n])
    loc_shapes = [local[n].shape for n in small_names]
    loc_rows = _pack_rows(sum(int(np.prod(s)) for s in loc_shapes), 256)
    packs = [_pack([src[n] for n in small_names], loc_rows, F32) for src in (given, g_small, mom1, mom2)]
    d_s, m_s, v_s = adamw(*packs, name="adamw_small")
    for n, d, nm, nv in zip(small_names, _unpack(d_s, loc_shapes), _unpack(m_s, loc_shapes), _unpack(v_s, loc_shapes)):
        result[n] = (g_small[n], d, nm, nv)

    outs = [[result[n][k].reshape(given[n].shape) for n in WEIGHTS] for k in range(4)]
    return (loss, dx[None], *outs[0], *outs[1], *outs[2], *outs[3])
```

```python
import functools

import numpy as np
import jax
import jax.numpy as jnp
from jax import lax
from jax.experimental import pallas as pl
from jax.experimental.pallas import tpu as pltpu
from jax.experimental.pallas import tpu_sc as plsc

F32 = jnp.float32
BF16 = jnp.bfloat16
HI = lax.Precision.HIGHEST
MESH = pl.DeviceIdType.MESH

SEQ = 2048
D_MODEL = 1024
N_HEADS = 4
HEAD = 128
RET_CHUNK = 128
GDN_CHUNK = 64
GROUP = N_HEADS * HEAD
MIX_MAIN = 8 * GROUP
D_FF = 2816
LRU_BLOCKS = 8
LRU_C = 8.0
ROPE_BASE = 10000.0
EPS = 1e-6
N_SHARD = 4
LANES = 128

ADAM_LR, ADAM_B1, ADAM_B2, ADAM_EPS, ADAM_WD, ADAM_STEP = 0.001, 0.9, 0.999, 1e-08, 0.01, 10

VMEM_LIMIT_BYTES = 56 * 1024 * 1024

_roll = pltpu.roll


def _params(**kw):
    return pltpu.CompilerParams(vmem_limit_bytes=VMEM_LIMIT_BYTES, **kw)


def _sds(shape, dtype):
    return jax.ShapeDtypeStruct(tuple(shape), dtype)


def _shift_raw(x, d):
    n = x.shape[0]
    t = lax.broadcasted_iota(jnp.int32, x.shape, 0)
    if d > 0:
        return jnp.where(t >= d, _roll(x, d, 0), 0.0)
    return jnp.where(t < n + d, _roll(x, n + d, 0), 0.0)


@functools.partial(jax.custom_vjp, nondiff_argnums=(1,))
def shift_rows(x, d):
    return _shift_raw(x, d)


def _shift_fwd(x, d):
    return _shift_raw(x, d), None


def _shift_bwd(d, _, g):
    return (_shift_raw(g, -d),)


shift_rows.defvjp(_shift_fwd, _shift_bwd)


@jax.custom_vjp
def swap_halves(x):
    return _roll(x, HEAD // 2, 1)


def _swap_fwd(x):
    return _roll(x, HEAD // 2, 1), None


def _swap_bwd(_, g):
    return (_roll(g, HEAD // 2, 1),)


swap_halves.defvjp(_swap_fwd, _swap_bwd)


def _scan_raw(a, u, reverse):
    n = a.shape[0]
    t = lax.broadcasted_iota(jnp.int32, a.shape, 0)
    d = 1
    while d < n:
        if reverse:
            m = t < n - d
            a_s, u_s = _roll(a, n - d, 0), _roll(u, n - d, 0)
        else:
            m = t >= d
            a_s, u_s = _roll(a, d, 0), _roll(u, d, 0)
        u = a * jnp.where(m, u_s, 0.0) + u
        a = a * jnp.where(m, a_s, 1.0)
        d *= 2
    return u


@jax.custom_vjp
def lin_scan(a, u):
    return _scan_raw(a, u, False)


def _lin_scan_fwd(a, u):
    hs = _scan_raw(a, u, False)
    return hs, (a, hs)


def _lin_scan_bwd(res, g):
    a, hs = res
    lam = _scan_raw(_shift_raw(a, -1), g, True)
    return lam * _shift_raw(hs, 1), lam


lin_scan.defvjp(_lin_scan_fwd, _lin_scan_bwd)


def _bdot(a, b, dims=(((1,), (0,)), ((), ()))):
    return lax.dot_general(a.astype(BF16), b.astype(BF16), dims, preferred_element_type=F32)


def _each(f, *seqs):
    return tuple(f(*a) for a in zip(*seqs))


def _split_bf16(a):
    hi = a.astype(BF16)
    return hi, (a - hi.astype(F32)).astype(BF16)


def _dot3_raw(a_s, b_s):
    a_hl = _each(_split_bf16, a_s)
    b_hl = _each(_split_bf16, b_s)
    hh = _each(lambda a, b: _bdot(a[0], b[0]), a_hl, b_hl)
    hl = _each(lambda a, b: _bdot(a[0], b[1]), a_hl, b_hl)
    lh = _each(lambda a, b: _bdot(a[1], b[0]), a_hl, b_hl)
    return _each(lambda x, y, z: x + (y + z), hh, hl, lh)


@jax.custom_vjp
def dot3(a_s, b_s):
    return _dot3_raw(a_s, b_s)


def _dot3_fwd(a_s, b_s):
    return _dot3_raw(a_s, b_s), (a_s, b_s)


def _dot3_bwd(res, g_s):
    a_s, b_s = res
    return (_each(lambda g, b: _bdot(g, b, (((1,), (1,)), ((), ()))), g_s, b_s),
            _each(lambda a, g: _bdot(a, g, (((0,), (0,)), ((), ()))), a_s, g_s))


dot3.defvjp(_dot3_fwd, _dot3_bwd)


def _eye(n):
    i = lax.broadcasted_iota(jnp.int32, (n, n), 0)
    j = lax.broadcasted_iota(jnp.int32, (n, n), 1)
    return (i == j).astype(F32)


def _unit_lower_inverse_raw(lmats):
    n = lmats[0].shape[0]
    eye = _eye(n)
    ps = _each(lambda l: -l, lmats)
    invs = _each(lambda x: eye + x, ps)
    k = 1
    while 2 * k < n:
        ps = _each(lambda p: _bdot(p, p), ps)
        invs = _each(lambda inv, p: inv + _bdot(inv, p), invs, ps)
        k *= 2
    prods = _dot3_raw(lmats, invs)
    resids = _each(lambda inv, pr: eye - inv - pr, invs, prods)
    return _each(lambda inv, r: inv + _bdot(inv, r), invs, resids)


@jax.custom_vjp
def unit_lower_inverse(lmats):
    return _unit_lower_inverse_raw(lmats)


def _uli_fwd(lmats):
    invs = _unit_lower_inverse_raw(lmats)
    return invs, invs


def _uli_bwd(invs, g_s):
    ms = _each(lambda inv, g: _bdot(inv, g, (((0,), (0,)), ((), ()))), invs, g_s)
    return (_each(lambda m, inv: -_bdot(m, inv, (((1,), (1,)), ((), ()))), ms, invs),)


unit_lower_inverse.defvjp(_uli_fwd, _uli_bwd)


def _cumsum_raw(x, reverse):
    n = x.shape[0]
    t = lax.broadcasted_iota(jnp.int32, x.shape, 0)
    d = 1
    while d < n:
        if reverse:
            x = x + jnp.where(t < n - d, _roll(x, n - d, 0), 0.0)
        else:
            x = x + jnp.where(t >= d, _roll(x, d, 0), 0.0)
        d *= 2
    return x


@jax.custom_vjp
def cumsum_rows(x):
    return _cumsum_raw(x, False)


def _cumsum_fwd(x):
    return _cumsum_raw(x, False), None


def _cumsum_bwd(_, g):
    return (_cumsum_raw(g, True),)


cumsum_rows.defvjp(_cumsum_fwd, _cumsum_bwd)


_NT = (((1,), (1,)), ((), ()))
_TN = (((0,), (0,)), ((), ()))


def _softplus(x):
    return jnp.maximum(x, 0.0) + jnp.log1p(jnp.exp(-jnp.abs(x)))


def _expm1_nonpos(x):
    poly = x * (1.0 + x * (0.5 + x * (1.0 / 6 + x * (1.0 / 24 + x * (1.0 / 120 + x * (1.0 / 720))))))
    return jnp.where(x > -0.25, poly, jnp.exp(x) - 1.0)


def _rms(x):
    return x * lax.rsqrt(jnp.mean(x * x, axis=-1, keepdims=True) + EPS)


def _causal_conv(x, w, width):
    y = w[width - 1:width, :] * x
    for j in range(width - 1):
        y = y + w[j:j + 1, :] * shift_rows(x, width - 1 - j)
    return y


def _norm_fn(x, g):
    return _rms(x) * g


def _ffn_act_fn(ug, uv, wg, wv, bg, bv):
    return jax.nn.silu(_causal_conv(ug, wg, 3) + bg) * (_causal_conv(uv, wv, 3) + bv)


def _gdn_conv_fn(x, w):
    return jax.nn.silu(_causal_conv(x, w, 4))


def _lru_fn(gate, x, cw, cb, wa, ba, wx, bx, lam):
    xr = _causal_conv(x, cw, 4) + cb
    r = jax.nn.sigmoid(_bdot(xr, wa) + ba)
    i = jax.nn.sigmoid(_bdot(xr, wx) + bx)
    log_a = -LRU_C * r * _softplus(-lam)
    a = jnp.exp(log_a)
    u = jnp.sqrt(-_expm1_nonpos(2.0 * log_a)) * (i * xr)
    hs = lin_scan(a, u)
    return jax.nn.gelu(gate) * hs


def _ret_fn(qs, ks, vs, gates, states, cos2, sin2, dmasks, ktails, qdecs, cdecs):
    qrs = _each(lambda q: q * cos2 + swap_halves(q) * sin2, qs)
    krs = _each(lambda k: (k * cos2 + swap_halves(k) * sin2) * (HEAD ** -0.5), ks)
    scores = _each(lambda q, k, m: _bdot(q, k, _NT) * m, qrs, krs, dmasks)
    inter = _each(lambda q, d, s: _bdot(q * d, s), qrs, qdecs, states)
    os_ = _each(lambda sc, v, x: _bdot(sc, v) + x, scores, vs, inter)
    new_states = _each(lambda s, cd, k, kt, v: s * cd + _bdot(k * kt, v, _TN), states, cdecs, krs, ktails, vs)
    ys = _each(lambda o, g: _rms(o) * jax.nn.silu(g), os_, gates)
    return ys, new_states


def _pick_lane(x, lane_idx):
    lane = lax.broadcasted_iota(jnp.int32, x.shape, 1)
    return jnp.sum(jnp.where(lane == lane_idx, x, 0.0), axis=1, keepdims=True)


def _l2norm(x):
    return x * lax.rsqrt(jnp.sum(x * x, axis=-1, keepdims=True) + EPS)


def _gdn_fn(qcs, kcs, vcs, gates, small, a_log, dt_bias, gain, states):
    c = GDN_CHUNK
    heads = tuple(range(len(qcs)))
    qs = _each(lambda x: _l2norm(x) * (HEAD ** -0.5), qcs)
    ks = _each(_l2norm, kcs)
    betas = _each(lambda h: jax.nn.sigmoid(_pick_lane(small, h)), heads)
    gs = _each(lambda h: -jnp.exp(_pick_lane(a_log, h)) * _softplus(_pick_lane(small, h + N_HEADS) + _pick_lane(dt_bias, h)), heads)
    i = lax.broadcasted_iota(jnp.int32, (c, c), 0)
    j = lax.broadcasted_iota(jnp.int32, (c, c), 1)
    tril = i >= j
    gcs = _each(lambda g: cumsum_rows(jnp.broadcast_to(g, (c, LANES)))[:, :1], gs)
    gc_rows = _each(lambda gc: jnp.broadcast_to(gc, (c, c)), gcs)
    decays = _each(lambda r: jnp.where(tril, jnp.exp(jnp.where(tril, r - r.T, 0.0)), 0.0), gc_rows)
    kbs = _each(lambda k, b: k * b, ks, betas)
    lmats = _each(lambda kb, k, d: jnp.where(i > j, _bdot(kb, k, _NT) * d, 0.0), kbs, ks, decays)
    attns = _each(lambda q, k, d: jnp.where(tril, _bdot(q, k, _NT) * d, 0.0), qs, ks, decays)
    invs = unit_lower_inverse(lmats)
    us = dot3(invs, _each(lambda v, b: v * b, vcs, betas))
    ws = dot3(invs, _each(lambda kb, gc: kb * jnp.exp(gc), kbs, gcs))
    g_lasts = _each(lambda g: jnp.sum(g, axis=0, keepdims=True), gs)
    v_news = _each(lambda u, w, s: u - _bdot(w, s), us, ws, states)
    inter = _each(lambda q, gc, s: _bdot(q * jnp.exp(gc), s), qs, gcs, states)
    os_ = _each(lambda x, a, v: x + _bdot(a, v), inter, attns, v_news)
    new_states = _each(lambda s, gl, k, gc, v: s * jnp.exp(gl) + _bdot(k * jnp.exp(gl - gc), v, _TN), states, g_lasts, ks, gcs, v_news)
    ys = _each(lambda o, gate: _rms(o) * gain * jax.nn.silu(gate), os_, gates)
    return ys, new_states


def _final_fn(h, g, target):
    y = _rms(h) * g
    return 0.5 * jnp.sum(jnp.mean(jnp.square(y - target), axis=-1, keepdims=True), axis=0, keepdims=True)


def _tile(n, candidates):
    for t in candidates:
        if n % t == 0:
            return t
    raise ValueError(f"no tile for {n}")


def matmul(a, b, *, ta=False, tb=False, add=None, out_dtype=F32, tm=None, tn=None, split=None, layer=None, name):
    m = a.shape[1] if ta else a.shape[0]
    k = a.shape[0] if ta else a.shape[1]
    n = b.shape[0] if tb else b.shape[1]
    assert k == (b.shape[1] if tb else b.shape[0])
    out_shape, out_block, out_index = (m, n), None, lambda i, j: (i, j)
    if split is not None:
        dims4, perm = split
        out_shape = tuple(dims4[p] for p in perm)
        r, cols = out_shape[2:]
        tm, tn = tm or _tile(r, (512, 256, 128)), tn or _tile(cols, (1408, 1024, 512))
        rb, cb = r // tm, cols // tn
        out_block = (None, None, tm, tn)
        if perm == (0, 2, 1, 3):
            out_index = lambda i, j: (i // rb, j // cb, i % rb, j % cb)
        elif perm == (1, 0, 2, 3):
            out_index = lambda i, j: ((i // rb) % 2, i // (2 * rb), i % rb, j)
        else:
            raise ValueError(perm)
    tm = tm or _tile(m, (1024, 512, 1408, 256, 128))
    tn = tn or _tile(n, (512, 1408, 256, 128))
    aliases, prev = {}, None
    if layer is not None:
        index, count, prev = layer
        out_shape, out_block, out_index = (count, m, n), (None, tm, tn), lambda i, j: (index, i, j)
    dims = (((0 if ta else 1,), (1 if tb else 0,)), ((), ()))

    def body(a_ref, b_ref, *rest):
        acc = lax.dot_general(a_ref[...].astype(BF16), b_ref[...].astype(BF16), dims, preferred_element_type=F32)
        if add is not None:
            acc = acc + rest[0][...]
        rest[-1][...] = acc.astype(out_dtype)

    a_spec = pl.BlockSpec((k, tm), lambda i, j: (0, i)) if ta else pl.BlockSpec((tm, k), lambda i, j: (i, 0))
    b_spec = pl.BlockSpec((tn, k), lambda i, j: (j, 0)) if tb else pl.BlockSpec((k, tn), lambda i, j: (0, j))
    o_spec = pl.BlockSpec(out_block or (tm, tn), out_index)
    in_specs, args = [a_spec, b_spec], [a, b]
    if add is not None:
        in_specs.append(o_spec)
        args.append(add)
    if prev is not None:
        aliases = {len(args): 0}
        in_specs.append(pl.BlockSpec(memory_space=pl.ANY))
        args.append(prev)
    return pl.pallas_call(body, out_shape=_sds(out_shape, out_dtype), grid=(m // tm, n // tn), in_specs=in_specs,
                          out_specs=o_spec, input_output_aliases=aliases, compiler_params=_params(), name=name)(*args)


ROW_TILE = 256


def norm_fwd(x, g, *, name):
    t, d = x.shape

    def body(x_ref, g_ref, o_ref):
        o_ref[...] = _norm_fn(x_ref[...], g_ref[...]).astype(BF16)

    return pl.pallas_call(body, out_shape=_sds((t, d), BF16), grid=(t // ROW_TILE,),
                          in_specs=[pl.BlockSpec((ROW_TILE, d), lambda i: (i, 0)), pl.BlockSpec((1, d), lambda i: (0, 0))],
                          out_specs=pl.BlockSpec((ROW_TILE, d), lambda i: (i, 0)), compiler_params=_params(), name=name)(x, g)


def norm_bwd(x, g, dy, dres, *, name):
    t, d = x.shape

    def body(x_ref, g_ref, dy_ref, dres_ref, dx_ref, dg_ref):
        _, vjp = jax.vjp(_norm_fn, x_ref[...], g_ref[...])
        dx, dg = vjp(dy_ref[...])
        dx_ref[...] = dx + dres_ref[...]

        @pl.when(pl.program_id(0) == 0)
        def _():
            dg_ref[...] = jnp.zeros_like(dg_ref)

        dg_ref[...] += dg

    row = pl.BlockSpec((ROW_TILE, d), lambda i: (i, 0))
    vec = pl.BlockSpec((1, d), lambda i: (0, 0))
    return pl.pallas_call(body, out_shape=(_sds((t, d), F32), _sds((1, d), F32)), grid=(t // ROW_TILE,),
                          in_specs=[row, vec, row, row], out_specs=(row, vec), compiler_params=_params(), name=name)(x, g, dy, dres)


def final_fwd_bwd(h, g, target, *, name):
    t, d = h.shape

    def body(h_ref, g_ref, t_ref, loss_ref, dh_ref, dg_ref):
        tgt = t_ref[...]
        loss, vjp = jax.vjp(lambda hh, gg: _final_fn(hh, gg, tgt), h_ref[...], g_ref[...])
        dh, dg = vjp(jnp.ones((1, 1), F32))
        dh_ref[...] = dh

        @pl.when(pl.program_id(0) == 0)
        def _():
            dg_ref[...] = jnp.zeros_like(dg_ref)
            loss_ref[...] = jnp.zeros_like(loss_ref)

        dg_ref[...] += dg
        loss_ref[...] += jnp.broadcast_to(loss, loss_ref.shape)

    row = pl.BlockSpec((ROW_TILE, d), lambda i: (i, 0))
    vec = pl.BlockSpec((1, d), lambda i: (0, 0))
    return pl.pallas_call(body, out_shape=(_sds((1, LANES), F32), _sds((t, d), F32), _sds((1, d), F32)), grid=(t // ROW_TILE,),
                          in_specs=[row, vec, row], out_specs=(pl.BlockSpec((1, LANES), lambda i: (0, 0)), row, vec),
                          compiler_params=_params(), name=name)(h, g, target)


FFN_FWD_COLS = 256
FFN_BWD_COLS = 128


def ffn_act_fwd(u, cw, cb, *, name):
    t = u.shape[0]
    w = FFN_FWD_COLS
    nb = D_FF // w

    def body(ug_ref, uv_ref, wg_ref, wv_ref, bg_ref, bv_ref, o_ref):
        o_ref[...] = _ffn_act_fn(ug_ref[...], uv_ref[...], wg_ref[...], wv_ref[...], bg_ref[...], bv_ref[...]).astype(BF16)

    def col(rows, off):
        return pl.BlockSpec((rows, w), lambda j: (0, j + off))

    return pl.pallas_call(body, out_shape=_sds((t, D_FF), BF16), grid=(nb,),
                          in_specs=[col(t, 0), col(t, nb), col(3, 0), col(3, nb), col(1, 0), col(1, nb)],
                          out_specs=col(t, 0), compiler_params=_params(), name=name)(u, u, cw, cw, cb, cb)


def ffn_act_bwd(u, cw, cb, da, *, name):
    t = u.shape[0]
    w = FFN_BWD_COLS
    nb = D_FF // w

    def body(ug_ref, uv_ref, wg_ref, wv_ref, bg_ref, bv_ref, da_ref, dug_ref, duv_ref, dwg_ref, dwv_ref, dbg_ref, dbv_ref):
        _, vjp = jax.vjp(_ffn_act_fn, ug_ref[...], uv_ref[...], wg_ref[...], wv_ref[...], bg_ref[...], bv_ref[...])
        dug, duv, dwg, dwv, dbg, dbv = vjp(da_ref[...])
        dug_ref[...] = dug.astype(BF16)
        duv_ref[...] = duv.astype(BF16)
        dwg_ref[...] = dwg
        dwv_ref[...] = dwv
        dbg_ref[...] = dbg
        dbv_ref[...] = dbv

    def col(rows, off):
        return pl.BlockSpec((rows, w), lambda j: (0, j + off))

    outs = pl.pallas_call(
        body, out_shape=(_sds((t, D_FF), BF16), _sds((t, D_FF), BF16), _sds((3, D_FF), F32), _sds((3, D_FF), F32),
                         _sds((1, D_FF), F32), _sds((1, D_FF), F32)),
        grid=(nb,), in_specs=[col(t, 0), col(t, nb), col(3, 0), col(3, nb), col(1, 0), col(1, nb), col(t, 0)],
        out_specs=(col(t, 0), col(t, 0), col(3, 0), col(3, 0), col(1, 0), col(1, 0)), compiler_params=_params(), name=name,
    )(u, u, cw, cw, cb, cb, da)
    dug, duv, dwg, dwv, dbg, dbv = outs
    return dug, duv, jnp.concatenate([dwg, dwv], axis=1), jnp.concatenate([dbg, dbv], axis=1)


GDN_CONV_COLS = 256
GDN_CONV_OFF = 4 * GROUP


def gdn_conv_fwd(p, cw, *, name):
    t = p.shape[0]
    w = GDN_CONV_COLS
    nb = 3 * GROUP // w
    off = GDN_CONV_OFF // w

    def body(x_ref, w_ref, o_ref):
        o_ref[...] = _gdn_conv_fn(x_ref[...], w_ref[...])

    return pl.pallas_call(body, out_shape=_sds((t, 3 * GROUP), F32), grid=(nb,),
                          in_specs=[pl.BlockSpec((t, w), lambda j: (0, j + off)), pl.BlockSpec((4, w), lambda j: (0, j))],
                          out_specs=pl.BlockSpec((t, w), lambda j: (0, j)), compiler_params=_params(), name=name)(p, cw)


def gdn_conv_bwd(p, cw, dc, *, name):
    t = p.shape[0]
    w = GDN_CONV_COLS
    nb = 3 * GROUP // w
    off = GDN_CONV_OFF // w

    def body(x_ref, w_ref, dc_ref, dx_ref, dw_ref):
        _, vjp = jax.vjp(_gdn_conv_fn, x_ref[...], w_ref[...])
        dx, dw = vjp(dc_ref[...])
        dx_ref[...] = dx.astype(BF16)
        dw_ref[...] = dw

    blk = pl.BlockSpec((t, w), lambda j: (0, j))
    wblk = pl.BlockSpec((4, w), lambda j: (0, j))
    return pl.pallas_call(body, out_shape=(_sds((t, 3 * GROUP), BF16), _sds((4, 3 * GROUP), F32)), grid=(nb,),
                          in_specs=[pl.BlockSpec((t, w), lambda j: (0, j + off)), wblk, blk], out_specs=(blk, wblk),
                          compiler_params=_params(), name=name)(p, cw, dc)


def _lru_specs(t):
    w = D_MODEL // LRU_BLOCKS
    gate = pl.BlockSpec((t, w), lambda j: (0, j))
    xin = pl.BlockSpec((t, w), lambda j: (0, j + LRU_BLOCKS))
    cw = pl.BlockSpec((4, w), lambda j: (0, j))
    vec = pl.BlockSpec((1, w), lambda j: (0, j))
    mat = pl.BlockSpec((None, w, w), lambda j: (j, 0, 0))
    return gate, xin, cw, vec, mat


def lru_fwd(gx, cw, cb, wa, ba, wx, bx, lam, *, name):
    t = gx.shape[0]
    gate, xin, cws, vec, mat = _lru_specs(t)

    def body(g_ref, x_ref, cw_ref, cb_ref, wa_ref, ba_ref, wx_ref, bx_ref, lam_ref, o_ref):
        o_ref[...] = _lru_fn(g_ref[...], x_ref[...], cw_ref[...], cb_ref[...], wa_ref[...], ba_ref[...], wx_ref[...],
                             bx_ref[...], lam_ref[...]).astype(BF16)

    return pl.pallas_call(body, out_shape=_sds((t, D_MODEL), BF16), grid=(LRU_BLOCKS,),
                          in_specs=[gate, xin, cws, vec, mat, vec, mat, vec, vec], out_specs=gate,
                          compiler_params=_params(), name=name)(gx, gx, cw, cb, wa, ba, wx, bx, lam)


def lru_bwd(gx, cw, cb, wa, ba, wx, bx, lam, dy, *, name):
    t = gx.shape[0]
    gate, xin, cws, vec, mat = _lru_specs(t)

    def body(g_ref, x_ref, cw_ref, cb_ref, wa_ref, ba_ref, wx_ref, bx_ref, lam_ref, dy_ref,
             dg_ref, dx_ref, dcw_ref, dcb_ref, dwa_ref, dba_ref, dwx_ref, dbx_ref, dlam_ref):
        _, vjp = jax.vjp(_lru_fn, g_ref[...], x_ref[...], cw_ref[...], cb_ref[...], wa_ref[...], ba_ref[...], wx_ref[...],
                         bx_ref[...], lam_ref[...])
        dg, dx, dcw, dcb, dwa, dba, dwx, dbx, dlam = vjp(dy_ref[...])
        dg_ref[...] = dg.astype(BF16)
        dx_ref[...] = dx.astype(BF16)
        dcw_ref[...] = dcw
        dcb_ref[...] = dcb
        dwa_ref[...] = dwa
        dba_ref[...] = dba
        dwx_ref[...] = dwx
        dbx_ref[...] = dbx
        dlam_ref[...] = dlam

    d = D_MODEL
    w = d // LRU_BLOCKS
    out_shape = (_sds((t, d), BF16), _sds((t, d), BF16), _sds((4, d), F32), _sds((1, d), F32), _sds((LRU_BLOCKS, w, w), F32),
                 _sds((1, d), F32), _sds((LRU_BLOCKS, w, w), F32), _sds((1, d), F32), _sds((1, d), F32))
    return pl.pallas_call(body, out_shape=out_shape, grid=(LRU_BLOCKS,),
                          in_specs=[gate, xin, cws, vec, mat, vec, mat, vec, vec, gate],
                          out_specs=(gate, gate, cws, vec, mat, vec, mat, vec, vec), compiler_params=_params(), name=name,
                          )(gx, gx, cw, cb, wa, ba, wx, bx, lam, dy)


def _ret_tables():
    half = HEAD // 2
    inv_freq = (np.float32(ROPE_BASE) ** (-np.arange(half, dtype=np.float32) / np.float32(half))).astype(np.float32)
    ang = (np.arange(SEQ, dtype=np.float32)[:, None] * inv_freq[None, :]).astype(np.float64)
    cos2 = np.concatenate([np.cos(ang), np.cos(ang)], axis=1).astype(np.float32)
    sin2 = np.concatenate([-np.sin(ang), np.sin(ang)], axis=1).astype(np.float32)
    c = RET_CHUNK
    log_gamma = np.log1p(-np.exp2(-5.0 - np.arange(N_HEADS, dtype=np.float64)))
    idx = np.arange(c, dtype=np.float64)
    rel = idx[:, None] - idx[None, :]
    dmask = np.where(rel >= 0, np.exp(log_gamma[:, None, None] * np.maximum(rel, 0.0)), 0.0)
    ones = np.ones((N_HEADS, c, HEAD))
    ktail = np.exp(log_gamma[:, None] * (c - 1 - idx))[:, :, None] * ones
    qdec = np.exp(log_gamma[:, None] * (idx + 1.0))[:, :, None] * ones
    cdec = np.exp(log_gamma * c)[:, None, None] * ones
    return tuple(jnp.asarray(a, F32) for a in (cos2, sin2, dmask, ktail, qdec, cdec))


def _ret_specs(rev):
    c = RET_CHUNK
    nc = SEQ // c

    def n_of(n):
        return nc - 1 - n if rev else n

    def group(off):
        return pl.BlockSpec((c, GROUP), lambda n: (n_of(n), off))

    tab = pl.BlockSpec((c, HEAD), lambda n: (n_of(n), 0))
    const = pl.BlockSpec((N_HEADS, c, HEAD), lambda n: (0, 0, 0))
    state = pl.BlockSpec((N_HEADS, None, HEAD, HEAD), lambda n: (0, n_of(n), 0, 0))
    return group, tab, const, state, nc


def _head(h):
    return slice(h * HEAD, (h + 1) * HEAD)


def ret_fwd(p, tables, *, name):
    group, tab, const, state, nc = _ret_specs(False)

    def body(q_ref, k_ref, v_ref, g_ref, cos_ref, sin_ref, dm_ref, kt_ref, qd_ref, cd_ref, y_ref, st_ref, s_scr):
        @pl.when(pl.program_id(0) == 0)
        def _():
            s_scr[...] = jnp.zeros_like(s_scr)

        heads = range(N_HEADS)
        states = tuple(s_scr[h] for h in heads)
        ys, new_states = _ret_fn(*(tuple(r[:, _head(h)] for h in heads) for r in (q_ref, k_ref, v_ref, g_ref)), states,
                                 cos_ref[...], sin_ref[...], *(tuple(r[h] for h in heads) for r in (dm_ref, kt_ref, qd_ref, cd_ref)))
        for h in heads:
            st_ref[h] = states[h]
            y_ref[:, _head(h)] = ys[h].astype(BF16)
            s_scr[h] = new_states[h]

    return pl.pallas_call(
        body, out_shape=(_sds((SEQ, GROUP), BF16), _sds((N_HEADS, nc, HEAD, HEAD), F32)), grid=(nc,),
        in_specs=[group(0), group(1), group(2), group(3), tab, tab, const, const, const, const],
        out_specs=(group(0), state), scratch_shapes=[pltpu.VMEM((N_HEADS, HEAD, HEAD), F32)], compiler_params=_params(), name=name,
    )(p, p, p, p, *tables)


def ret_bwd(p, tables, states, dy, *, name):
    group, tab, const, state, nc = _ret_specs(True)

    def body(q_ref, k_ref, v_ref, g_ref, cos_ref, sin_ref, dm_ref, kt_ref, qd_ref, cd_ref, st_ref, dy_ref,
             dq_ref, dk_ref, dv_ref, dg_ref, ds_scr):
        @pl.when(pl.program_id(0) == 0)
        def _():
            ds_scr[...] = jnp.zeros_like(ds_scr)

        heads = range(N_HEADS)
        consts = (cos_ref[...], sin_ref[...], *(tuple(r[h] for h in heads) for r in (dm_ref, kt_ref, qd_ref, cd_ref)))
        _, vjp = jax.vjp(lambda *a: _ret_fn(*a, *consts), *(tuple(r[:, _head(h)] for h in heads) for r in (q_ref, k_ref, v_ref, g_ref)),
                         tuple(st_ref[h] for h in heads))
        dqs, dks, dvs, dgs, dss = vjp((tuple(dy_ref[:, _head(h)] for h in heads), tuple(ds_scr[h] for h in heads)))
        for h in heads:
            dq_ref[:, _head(h)] = dqs[h].astype(BF16)
            dk_ref[:, _head(h)] = dks[h].astype(BF16)
            dv_ref[:, _head(h)] = dvs[h].astype(BF16)
            dg_ref[:, _head(h)] = dgs[h].astype(BF16)
            ds_scr[h] = dss[h]

    out = _sds((SEQ, GROUP), BF16)
    return pl.pallas_call(
        body, out_shape=(out, out, out, out), grid=(nc,),
        in_specs=[group(0), group(1), group(2), group(3), tab, tab, const, const, const, const, state, group(0)],
        out_specs=(group(0), group(0), group(0), group(0)), scratch_shapes=[pltpu.VMEM((N_HEADS, HEAD, HEAD), F32)],
        compiler_params=_params(), name=name,
    )(p, p, p, p, *tables, states, dy)


def _gdn_specs(rev):
    c = GDN_CHUNK
    nc = SEQ // c

    def n_of(n):
        return nc - 1 - n if rev else n

    def group(off):
        return pl.BlockSpec((c, GROUP), lambda n: (n_of(n), off))

    small = pl.BlockSpec((c, LANES), lambda n: (n_of(n), 0))
    vec = pl.BlockSpec((1, LANES), lambda n: (0, 0))
    state = pl.BlockSpec((N_HEADS, None, HEAD, HEAD), lambda n: (0, n_of(n), 0, 0))
    return group, small, vec, state, nc


GDN_GATE_GROUP = 7


def gdn_fwd(conv, p, small, a_log, dt_bias, gain, *, name):
    group, sm, vec, state, nc = _gdn_specs(False)

    def body(q_ref, k_ref, v_ref, g_ref, sm_ref, al_ref, dt_ref, gn_ref, y_ref, st_ref, s_scr):
        @pl.when(pl.program_id(0) == 0)
        def _():
            s_scr[...] = jnp.zeros_like(s_scr)

        states = tuple(s_scr[h] for h in range(N_HEADS))
        ys, new_states = _gdn_fn(*(tuple(r[:, _head(h)] for h in range(N_HEADS)) for r in (q_ref, k_ref, v_ref, g_ref)),
                                 sm_ref[...], al_ref[...], dt_ref[...], gn_ref[...], states)
        for h in range(N_HEADS):
            st_ref[h] = states[h]
            y_ref[:, _head(h)] = ys[h].astype(BF16)
            s_scr[h] = new_states[h]

    return pl.pallas_call(
        body, out_shape=(_sds((SEQ, GROUP), BF16), _sds((N_HEADS, nc, HEAD, HEAD), F32)), grid=(nc,),
        in_specs=[group(0), group(1), group(2), group(GDN_GATE_GROUP), sm, vec, vec, vec], out_specs=(group(0), state),
        scratch_shapes=[pltpu.VMEM((N_HEADS, HEAD, HEAD), F32)], compiler_params=_params(), name=name,
    )(conv, conv, conv, p, small, a_log, dt_bias, gain)


def gdn_bwd(conv, p, small, a_log, dt_bias, gain, states, dy, *, name):
    group, sm, vec, state, nc = _gdn_specs(True)

    def body(q_ref, k_ref, v_ref, g_ref, sm_ref, al_ref, dt_ref, gn_ref, st_ref, dy_ref,
             dq_ref, dk_ref, dv_ref, dg_ref, dsm_ref, dal_ref, ddt_ref, dgn_ref, ds_scr):
        @pl.when(pl.program_id(0) == 0)
        def _():
            ds_scr[...] = jnp.zeros_like(ds_scr)
            dal_ref[...] = jnp.zeros_like(dal_ref)
            ddt_ref[...] = jnp.zeros_like(ddt_ref)
            dgn_ref[...] = jnp.zeros_like(dgn_ref)

        per_head = tuple(tuple(r[:, _head(h)] for h in range(N_HEADS)) for r in (q_ref, k_ref, v_ref, g_ref))
        _, vjp = jax.vjp(_gdn_fn, *per_head, sm_ref[...], al_ref[...], dt_ref[...], gn_ref[...],
                         tuple(st_ref[h] for h in range(N_HEADS)))
        cts = (tuple(dy_ref[:, _head(h)] for h in range(N_HEADS)), tuple(ds_scr[h] for h in range(N_HEADS)))
        dqs, dks, dvs, dgs, dsm, dal, ddt, dgn, dss = vjp(cts)
        for h in range(N_HEADS):
            dq_ref[:, _head(h)] = dqs[h]
            dk_ref[:, _head(h)] = dks[h]
            dv_ref[:, _head(h)] = dvs[h]
            dg_ref[:, _head(h)] = dgs[h].astype(BF16)
            ds_scr[h] = dss[h]
        dsm_ref[...] = dsm
        dal_ref[...] += dal
        ddt_ref[...] += ddt
        dgn_ref[...] += dgn

    f = _sds((SEQ, GROUP), F32)
    pv = _sds((1, LANES), F32)
    return pl.pallas_call(
        body, out_shape=(f, f, f, _sds((SEQ, GROUP), BF16), _sds((SEQ, LANES), F32), pv, pv, pv), grid=(nc,),
        in_specs=[group(0), group(1), group(2), group(GDN_GATE_GROUP), sm, vec, vec, vec, state, group(1)],
        out_specs=(group(0), group(0), group(0), group(0), sm, vec, vec, vec), scratch_shapes=[pltpu.VMEM((N_HEADS, HEAD, HEAD), F32)],
        compiler_params=_params(), name=name,
    )(conv, conv, conv, p, small, a_log, dt_bias, gain, states, dy)


PACK_ROW_TILE = 1024


def adamw(w, g, m, v, *, name):
    r = w.shape[0]
    tr = _tile(r, (PACK_ROW_TILE, 256, 128, 64, 32, 16, 8))

    def body(w_ref, g_ref, m_ref, v_ref, d_ref, nm_ref, nv_ref):
        gg = g_ref[...]
        nm = ADAM_B1 * m_ref[...] + (1.0 - ADAM_B1) * gg
        nv = ADAM_B2 * v_ref[...] + (1.0 - ADAM_B2) * jnp.square(gg)
        m_hat = nm / (1.0 - ADAM_B1 ** ADAM_STEP)
        v_hat = nv / (1.0 - ADAM_B2 ** ADAM_STEP)
        d_ref[...] = -ADAM_LR * (m_hat / (jnp.sqrt(v_hat) + ADAM_EPS) + ADAM_WD * w_ref[...])
        nm_ref[...] = nm
        nv_ref[...] = nv

    blk = pl.BlockSpec((tr, LANES), lambda i: (i, 0))
    o = _sds((r, LANES), F32)
    return pl.pallas_call(body, out_shape=(o, o, o), grid=(r // tr,), in_specs=[blk] * 4, out_specs=(blk, blk, blk),
                          compiler_params=_params(), name=name)(w, g, m, v)


ELEMENTWISE_BLOCK_BYTES = 2 * 1024 * 1024


def _row_tile(r, c):
    best = None
    for tr in range(8, r + 1, 8):
        if r % tr == 0 and tr * c * 4 <= ELEMENTWISE_BLOCK_BYTES:
            best = tr
    if best is None:
        raise ValueError(f"no row tile for ({r}, {c})")
    return best


def _tile_2d(r, c):
    if any(r % tr == 0 for tr in range(8, r + 1, 8)):
        return _row_tile(r, c), c
    tc = max(t for t in range(LANES, c + 1, LANES) if c % t == 0 and r * t * 4 <= ELEMENTWISE_BLOCK_BYTES)
    return r, tc


def _core_index():
    return lax.axis_index("c").astype(jnp.int32).reshape(1)


def _chip_index():
    return (2 * lax.axis_index("x") + lax.axis_index("y")).astype(jnp.int32).reshape(1)


def adamw_halves(w, m, v, g_own, g_sib, *, layer=0, prev=None, name):
    n_layers, rows, c = w.shape
    r = rows // 2
    tr = _row_tile(r, c)
    nb = r // tr

    def body(c_ref, w_ref, m_ref, v_ref, own_ref, sib_ref, *rest):
        g_ref, d_ref, nm_ref, nv_ref = rest[-4:]
        gg = jnp.where(pl.program_id(0) == c_ref[0], own_ref[...], sib_ref[...])
        nm = ADAM_B1 * m_ref[...] + (1.0 - ADAM_B1) * gg
        nv = ADAM_B2 * v_ref[...] + (1.0 - ADAM_B2) * jnp.square(gg)
        m_hat = nm / (1.0 - ADAM_B1 ** ADAM_STEP)
        v_hat = nv / (1.0 - ADAM_B2 ** ADAM_STEP)
        g_ref[...] = gg
        d_ref[...] = -ADAM_LR * (m_hat / (jnp.sqrt(v_hat) + ADAM_EPS) + ADAM_WD * w_ref[...])
        nm_ref[...] = nm
        nv_ref[...] = nv

    full = pl.BlockSpec((None, tr, c), lambda h, i, cr: (layer, h * nb + i, 0))
    half = pl.BlockSpec((tr, c), lambda h, i, cr: (i, 0))
    o = _sds((n_layers, rows, c), F32)
    prev = list(prev or ())
    gs = pltpu.PrefetchScalarGridSpec(num_scalar_prefetch=1, grid=(2, nb), in_specs=[full, full, full, half, half] + [_ANY] * len(prev),
                                      out_specs=(full, full, full, full))
    n_fixed = 6
    return pl.pallas_call(body, out_shape=(o, o, o, o), grid_spec=gs, compiler_params=_params(), name=name,
                          input_output_aliases={n_fixed + k: k for k in range(len(prev))})(
        _core_index(), w, m, v, g_own, g_sib, *prev)


ADAMW_COLUMN_TILE = 256


def adamw_column_halves(w, m, v, g_own, g_sib, *, name):
    _, rows, cols = w.shape
    tc = ADAMW_COLUMN_TILE
    per_half = cols // 2 // tc

    def body(c_ref, w_ref, m_ref, v_ref, own_ref, sib_ref, g_ref, d_ref, nm_ref, nv_ref):
        gg = jnp.where(pl.program_id(0) // per_half == c_ref[0], own_ref[...], sib_ref[...])
        nm = ADAM_B1 * m_ref[...] + (1.0 - ADAM_B1) * gg
        nv = ADAM_B2 * v_ref[...] + (1.0 - ADAM_B2) * jnp.square(gg)
        m_hat = nm / (1.0 - ADAM_B1 ** ADAM_STEP)
        v_hat = nv / (1.0 - ADAM_B2 ** ADAM_STEP)
        g_ref[...] = gg
        d_ref[...] = -ADAM_LR * (m_hat / (jnp.sqrt(v_hat) + ADAM_EPS) + ADAM_WD * w_ref[...])
        nm_ref[...] = nm
        nv_ref[...] = nv

    full = pl.BlockSpec((None, rows, tc), lambda j, cr: (0, 0, j))
    half = pl.BlockSpec((rows, tc), lambda j, cr: (0, j % per_half))
    o = _sds(w.shape, F32)
    gs = pltpu.PrefetchScalarGridSpec(num_scalar_prefetch=1, grid=(cols // tc,), in_specs=[full, full, full, half, half],
                                      out_specs=(full, full, full, full))
    return pl.pallas_call(body, out_shape=(o, o, o, o), grid_spec=gs, compiler_params=_params(), name=name)(
        _core_index(), w, m, v, g_own, g_sib)


def add_core_halves(g2, land, *, out_dtype, name):
    _, ns, r, cols = g2.shape
    tr, tc = _tile_2d(r, cols)

    def body(c_ref, a_ref, b_ref, o_ref):
        o_ref[...] = (a_ref[...] + b_ref[...]).astype(out_dtype)

    gs = pltpu.PrefetchScalarGridSpec(
        num_scalar_prefetch=1, grid=(ns, r // tr, cols // tc),
        in_specs=[pl.BlockSpec((None, None, tr, tc), lambda s, i, j, cr: (cr[0], s, i, j)),
                  pl.BlockSpec((None, tr, tc), lambda s, i, j, cr: (s, i, j))],
        out_specs=pl.BlockSpec((None, tr, tc), lambda s, i, j, cr: (s, i, j)))
    return pl.pallas_call(body, out_shape=_sds((ns, r, cols), out_dtype), grid_spec=gs, compiler_params=_params(), name=name)(
        _core_index(), g2, land)


def sum_over_chips(own, land, *, scatter, name):
    _, r, cols = own.shape
    tr, tc = _tile_2d(r, cols)

    def body(mine_ref, own_ref, l0, l1, l2, l3, o_ref):
        mine = mine_ref[0]
        mine_val = own_ref[...]
        acc = None
        for s, l_ref in enumerate((l0, l1, l2, l3)):
            val = jnp.where(mine == s, mine_val, l_ref[...]).astype(F32)
            acc = val if acc is None else acc + val
        o_ref[...] = acc

    def slot(s):
        return pl.BlockSpec((None, tr, tc), lambda i, j, mr: (jnp.where(mr[0] == s, (s + 1) % N_SHARD, s), i, j))

    own_spec = pl.BlockSpec((None, tr, tc), lambda i, j, mr: (mr[0] if scatter else 0, i, j))
    gs = pltpu.PrefetchScalarGridSpec(num_scalar_prefetch=1, grid=(r // tr, cols // tc), in_specs=[own_spec] + [slot(s) for s in range(N_SHARD)],
                                      out_specs=pl.BlockSpec((tr, tc), lambda i, j, mr: (i, j)))
    return pl.pallas_call(body, out_shape=_sds((r, cols), F32), grid_spec=gs, compiler_params=_params(), name=name)(
        _chip_index(), own, land, land, land, land)


_ANY = pl.BlockSpec(memory_space=pl.ANY)


def xy_exchange(src, *, scatter, name):
    rh = src.shape[1]

    def body(src_ref, land_ref, send_sems, recv_sems, loc_sem):
        x, y, c = lax.axis_index("x"), lax.axis_index("y"), lax.axis_index("c")
        mine = 2 * x + y
        peers = [(1 - x, y), (x, 1 - y), (1 - x, 1 - y)]

        def piece(shard):
            return src_ref.at[shard] if scatter else src_ref.at[c]

        def copy(k, px, py, dst_slot):
            return pltpu.make_async_remote_copy(src_ref=piece(2 * px + py), dst_ref=land_ref.at[dst_slot], send_sem=send_sems.at[k],
                                                recv_sem=recv_sems.at[k], device_id=(px, py, c), device_id_type=MESH)

        keep = pltpu.make_async_copy(piece(mine), land_ref.at[mine], loc_sem)
        keep.start()
        sends = [copy(k, px, py, mine) for k, (px, py) in enumerate(peers)]
        for cp in sends:
            cp.start()
        for cp in sends:
            cp.wait_send()
        for k, (px, py) in enumerate(peers):
            copy(k, px, py, 2 * px + py).wait_recv()
        keep.wait()

    return pl.pallas_call(body, out_shape=_sds((N_SHARD, rh, LANES), src.dtype), in_specs=[_ANY], out_specs=_ANY,
                          scratch_shapes=[pltpu.SemaphoreType.DMA((3,)), pltpu.SemaphoreType.DMA((3,)), pltpu.SemaphoreType.DMA(())],
                          name=name)(src)


def core_exchange(src, *, send_other_half, name):
    def body(src_ref, out_ref, send_sem, recv_sem, loc_sem):
        x, y, c = lax.axis_index("x"), lax.axis_index("y"), lax.axis_index("c")
        if send_other_half:
            cp = pltpu.make_async_remote_copy(src_ref=src_ref.at[1 - c], dst_ref=out_ref, send_sem=send_sem, recv_sem=recv_sem,
                                              device_id=(x, y, 1 - c), device_id_type=MESH)
            cp.start()
            cp.wait_send()
            cp.wait_recv()
        else:
            keep = pltpu.make_async_copy(src_ref, out_ref.at[c], loc_sem)
            keep.start()
            cp = pltpu.make_async_remote_copy(src_ref=src_ref, dst_ref=out_ref.at[c], send_sem=send_sem, recv_sem=recv_sem,
                                              device_id=(x, y, 1 - c), device_id_type=MESH)
            cp.start()
            cp.wait_send()
            pltpu.make_async_remote_copy(src_ref=src_ref, dst_ref=out_ref.at[1 - c], send_sem=send_sem, recv_sem=recv_sem,
                                         device_id=(x, y, 1 - c), device_id_type=MESH).wait_recv()
            keep.wait()

    out_shape = _sds(src.shape[1:], src.dtype) if send_other_half else _sds((2,) + src.shape, src.dtype)
    return pl.pallas_call(body, out_shape=out_shape, in_specs=[_ANY], out_specs=_ANY,
                          scratch_shapes=[pltpu.SemaphoreType.DMA(()), pltpu.SemaphoreType.DMA(()), pltpu.SemaphoreType.DMA(())],
                          name=name)(src)


def _comm_call(body, ins, out_shapes, sem_counts, name):
    return pl.pallas_call(body, out_shape=tuple(out_shapes), in_specs=[_ANY] * len(ins), out_specs=tuple([_ANY] * len(out_shapes)),
                          scratch_shapes=[pltpu.SemaphoreType.DMA((k,)) for k in sem_counts], name=name)(*ins)


def _sequencer_call(body, ins, out_shapes, sem_counts, name, collective_id):
    return pl.kernel(body, out_type=list(out_shapes), mesh=plsc.ScalarSubcoreMesh(axis_name="sequencer", num_cores=1), name=name,
                     scratch_types=[pltpu.SemaphoreType.DMA((k,)) for k in sem_counts],
                     compiler_params=pltpu.CompilerParams(collective_id=collective_id))(*ins)


def _handshake(peers):
    barrier = pltpu.get_barrier_semaphore()
    for peer in peers:
        pl.semaphore_signal(barrier, inc=1, device_id=peer, device_id_type=MESH)
    pl.semaphore_wait(barrier, len(peers))


def _xy_peers(x, y):
    return [(1 - x, y), (x, 1 - y), (1 - x, 1 - y)]


def gather_halves(halves, *, name, collective_id):
    n = len(halves)

    def body(*refs):
        ins, lands, sibs = refs[:n], refs[n:2 * n], refs[2 * n:3 * n]
        ici_send, ici_recv, d2d_send, d2d_recv = refs[3 * n:]
        x, y, c = lax.axis_index("x"), lax.axis_index("y"), lax.axis_index("c")
        mine = 2 * x + y
        peers = _xy_peers(x, y)
        _handshake([(px, py, c) for px, py in peers] + [(x, y, 1 - c)])

        def ici(i, k, slot):
            px, py = peers[k]
            return pltpu.make_async_remote_copy(src_ref=ins[i].at[c], dst_ref=lands[i].at[slot], send_sem=ici_send.at[3 * i + k],
                                                recv_sem=ici_recv.at[3 * i + k], device_id=(px, py, c), device_id_type=MESH)

        def pass_on(i, k):
            px, py = peers[k]
            slot = 2 * px + py
            return pltpu.make_async_remote_copy(src_ref=lands[i].at[slot], dst_ref=sibs[i].at[slot], send_sem=d2d_send.at[3 * i + k],
                                                recv_sem=d2d_recv.at[3 * i + k], device_id=(x, y, 1 - c), device_id_type=MESH)

        sends = [ici(i, k, mine) for i in range(n) for k in range(3)]
        for cp in sends:
            cp.start()
        passed = []
        for i in range(n):
            for k in range(3):
                px, py = peers[k]
                ici(i, k, 2 * px + py).wait_recv()
                cp = pass_on(i, k)
                cp.start()
                passed.append(cp)
        for cp in passed:
            cp.wait_recv()
        for cp in sends + passed:
            cp.wait_send()

    outs = [_sds((N_SHARD,) + h.shape[1:], h.dtype) for h in halves]
    res = _sequencer_call(body, halves, outs + outs, [3 * n] * 4, name, collective_id)
    return res[:n], res[n:]


def send_other_half(arrays, *, name, collective_id):
    n = len(arrays)

    def body(*refs):
        ins, lands = refs[:n], refs[n:2 * n]
        send_sems, recv_sems = refs[2 * n:]
        x, y, c = lax.axis_index("x"), lax.axis_index("y"), lax.axis_index("c")
        _handshake([(x, y, 1 - c)])
        copies = [pltpu.make_async_remote_copy(src_ref=ins[i].at[1 - c], dst_ref=lands[i], send_sem=send_sems.at[i],
                                               recv_sem=recv_sems.at[i], device_id=(x, y, 1 - c), device_id_type=MESH) for i in range(n)]
        for cp in copies:
            cp.start()
        for cp in copies:
            cp.wait_recv()
        for cp in copies:
            cp.wait_send()

    return _sequencer_call(body, arrays, [_sds(a.shape[1:], a.dtype) for a in arrays], [n, n], name, collective_id)


_HBM = pl.BlockSpec(memory_space=pltpu.HBM)
_SEM = pl.BlockSpec(memory_space=pltpu.SEMAPHORE)
_SPLIT_COPY = dict(has_side_effects=pltpu.SideEffectType.DATAFLOW_SIDE_EFFECTING)


def _chip_copy(ins, lands, send_sems, recv_sems, scatter, i, k, receive):
    x, y, c = lax.axis_index("x"), lax.axis_index("y"), lax.axis_index("c")
    px, py = _xy_peers(x, y)[k]
    theirs, mine = 2 * px + py, 2 * x + y
    src = ins[i].at[theirs] if scatter[i] else ins[i].at[0]
    return pltpu.make_async_remote_copy(src_ref=src, dst_ref=lands[i].at[theirs if receive else mine], send_sem=send_sems.at[3 * i + k],
                                        recv_sem=recv_sems.at[3 * i + k], device_id=(px, py, c), device_id_type=MESH)


def send_to_chips_start(arrays, scatter, *, name):
    n = len(arrays)

    def body(*refs):
        send_sems, recv_sems = refs[2 * n], refs[2 * n + 1]
        ins, lands = refs[2 * n + 2:3 * n + 2], refs[3 * n + 2:4 * n + 2]
        token = refs[4 * n + 2]
        for i in range(n):
            for k in range(3):
                _chip_copy(ins, lands, send_sems, recv_sems, scatter, i, k, receive=False).start()
        token[...] = jnp.zeros_like(token)

    land_shapes = [(N_SHARD,) + a.shape[1:] for a in arrays]
    operands = [pltpu.with_memory_space_constraint(a, pltpu.HBM) for a in arrays]
    operands += [pltpu.with_memory_space_constraint(lax.empty(s, a.dtype), pltpu.HBM) for s, a in zip(land_shapes, arrays)]
    out_shape = ([pltpu.SemaphoreType.DMA((3 * n,)), pltpu.SemaphoreType.DMA((3 * n,))] + [pltpu.HBM(a.shape, a.dtype) for a in arrays]
                 + [pltpu.HBM(s, a.dtype) for s, a in zip(land_shapes, arrays)] + [_sds((8, LANES), F32)])
    res = pl.pallas_call(body, name=name, out_shape=out_shape, in_specs=[_HBM] * (2 * n),
                         out_specs=[_SEM, _SEM] + [_HBM] * (2 * n) + [pl.BlockSpec(memory_space=pltpu.VMEM)],
                         input_output_aliases={i: 2 + i for i in range(2 * n)}, compiler_params=pltpu.CompilerParams(**_SPLIT_COPY))(*operands)
    return (res[0], res[1], res[2:2 + n], res[2 + n:2 + 2 * n], scatter), res[-1]


def send_to_chips_wait(state, after, *, name):
    send_sems, recv_sems, arrays, lands, scatter = state
    n = len(arrays)

    def body(*refs):
        ins, landing = refs[:n], refs[n:2 * n]
        send_sems, recv_sems = refs[2 * n], refs[2 * n + 1]
        for i in range(n):
            for k in range(3):
                _chip_copy(ins, landing, send_sems, recv_sems, scatter, i, k, receive=True).wait_recv()
        for i in range(n):
            for k in range(3):
                _chip_copy(ins, landing, send_sems, recv_sems, scatter, i, k, receive=False).wait_send()

    out_shape = [pltpu.HBM(a.shape, a.dtype) for a in list(arrays) + list(lands)]
    res = pl.pallas_call(body, name=name, out_shape=out_shape, in_specs=[_HBM] * (2 * n) + [_SEM, _SEM] + [_ANY] * len(after),
                         out_specs=[_HBM] * (2 * n), input_output_aliases={i: i for i in range(2 * n)},
                         compiler_params=pltpu.CompilerParams(**_SPLIT_COPY))(*arrays, *lands, send_sems, recv_sems, *after)
    return res[:n], res[n:]


def swap_with_other_core(arrays, *, name, collective_id):
    n = len(arrays)

    def body(*refs):
        ins, lands = refs[:n], refs[n:2 * n]
        send_sems, recv_sems = refs[2 * n:]
        x, y, c = lax.axis_index("x"), lax.axis_index("y"), lax.axis_index("c")
        _handshake([(x, y, 1 - c)])
        copies = [pltpu.make_async_remote_copy(src_ref=ins[i], dst_ref=lands[i], send_sem=send_sems.at[i], recv_sem=recv_sems.at[i],
                                               device_id=(x, y, 1 - c), device_id_type=MESH) for i in range(n)]
        for cp in copies:
            cp.start()
        for cp in copies:
            cp.wait_recv()
        for cp in copies:
            cp.wait_send()

    return _sequencer_call(body, arrays, [_sds(a.shape, a.dtype) for a in arrays], [n, n], name, collective_id)


def _pack_rows(n_elems, row_multiple):
    rows = -(-n_elems // LANES)
    return -(-rows // row_multiple) * row_multiple


def _pack(arrays, rows, dtype):
    flat = jnp.concatenate([a.reshape(-1).astype(dtype) for a in arrays])
    return jnp.pad(flat, (0, rows * LANES - flat.shape[0])).reshape(rows, LANES)


def _unpack(packed, shapes):
    flat = packed.reshape(-1)
    out, off = [], 0
    for s in shapes:
        n = int(np.prod(s))
        out.append(flat[off:off + n].reshape(s))
        off += n
    return out


def all_gather_shards(shards, axes, dtype, row_multiple, tag):
    shapes = [s.shape for s in shards]
    rows = _pack_rows(sum(int(np.prod(s)) for s in shapes), row_multiple)
    packed = _pack(shards, rows, dtype).reshape(2, rows // 2, LANES)
    land = xy_exchange(packed, scatter=False, name=f"gather_xy_{tag}")
    both = core_exchange(land, send_other_half=False, name=f"gather_c_{tag}")
    per_shard = jnp.swapaxes(both, 0, 1).reshape(N_SHARD, rows, LANES)
    pieces = [_unpack(per_shard[s], shapes) for s in range(N_SHARD)]
    return [jnp.concatenate([pieces[s][i] for s in range(N_SHARD)], axis=ax) for i, ax in enumerate(axes)]


def _ordered_before(first, then):
    if then is None:
        return first, None
    return lax.optimization_barrier((first, then))


def reduce_between_cores(arrays, scatter, *, tag, collective_id, before=None):
    arrays, before = _ordered_before(arrays, before)
    land = send_other_half(arrays, name=f"reduce_core_send_{tag}", collective_id=collective_id)
    return (arrays, land, scatter, tag, collective_id), before


def reduce_between_chips(state, before=None):
    arrays, land, scatter, tag, collective_id = state
    chip = [add_core_halves(a, l, out_dtype=BF16 if sc else F32, name=f"reduce_core_add_{tag}_{i}")
            for i, (a, l, sc) in enumerate(zip(arrays, land, scatter))]
    sending, token = send_to_chips_start(chip, scatter, name=f"reduce_chip_start_{tag}")
    token, before = _ordered_before(token, before)
    return (sending, token, scatter, tag, collective_id), before


def reduce_finish(state, after):
    sending, token, scatter, tag, collective_id = state
    chip, land = send_to_chips_wait(sending, tuple(after) + (token,), name=f"reduce_chip_wait_{tag}")
    own = [sum_over_chips(ch, l, scatter=sc, name=f"reduce_chip_add_{tag}_{i}") for i, (ch, l, sc) in enumerate(zip(chip, land, scatter))]
    sib = swap_with_other_core(own, name=f"reduce_core_swap_{tag}", collective_id=collective_id + 2)
    return own, sib


def _ffn_layer_fwd(h, norm_g, w_up, cw, cb, w_down, tag):
    hn = norm_fwd(h, norm_g, name=f"ffn_norm_{tag}")
    u = matmul(hn, w_up, name=f"ffn_up_{tag}")
    act = ffn_act_fwd(u, cw, cb, name=f"ffn_act_{tag}")
    out = matmul(act, w_down, add=h, name=f"ffn_down_{tag}")
    return out, (h, hn, u, act)


def _travel_layout(array):
    return BIG_ARRAYS[array][3], BIG_ARRAYS[array][4]


def _ffn_layer_bwd(saved, dout, norm_g, w_up, cw, cb, w_down, tag, d_w_down_other=None):
    h, hn, u, act = saved
    dact = matmul(dout, w_down, tb=True, name=f"ffn_down_dx_{tag}")
    d_w_down = matmul(act, dout, ta=True, layer=(int(tag), 2, d_w_down_other), name=f"ffn_down_dw_{tag}")
    dug, duv, dcw, dcb = ffn_act_bwd(u, cw, cb, dact, name=f"ffn_act_bwd_{tag}")
    du = jnp.concatenate([dug, duv], axis=1)
    dhn = matmul(du, w_up, tb=True, name=f"ffn_up_dx_{tag}")
    d_w_up = matmul(hn, du, ta=True, split=_travel_layout(f"ffn_w_up_{tag}"), name=f"ffn_up_dw_{tag}")
    dh, dg = norm_bwd(h, norm_g, dhn, dout, name=f"ffn_norm_bwd_{tag}")
    return dh, dg, d_w_up, dcw, dcb, d_w_down


def local_step(x, target, w, stage=lambda name, tensors, grads=None: tensors):
    g = {}
    tables = _ret_tables()
    x = stage("start", x)
    w_in_t = w["ret_gdn_w_in"]
    w_main = w_in_t[:MIX_MAIN]
    w_small = jnp.pad(w_in_t[MIX_MAIN:], ((0, LANES - 2 * N_HEADS), (0, 0)))
    a_log = jnp.pad(w["gdn_a_log"], ((0, 0), (0, LANES - N_HEADS)))
    dt_bias = jnp.pad(w["gdn_dt_bias"], ((0, 0), (0, LANES - N_HEADS)))

    hn0 = stage("normed", norm_fwd(x, w["norm_mix"][0:1], name="mix0_norm"))
    p = matmul(hn0, w_main, tb=True, name="mix0_in")
    small = matmul(hn0, w_small, tb=True, name="mix0_in_small")
    y_ret, s_ret = ret_fwd(p, tables, name="ret_fwd")
    conv = gdn_conv_fwd(p, w["gdn_conv_w"], name="gdn_conv")
    y_gdn, s_gdn = gdn_fwd(conv, p, small, a_log, dt_bias, w["gdn_out_gain"], name="gdn_fwd")
    y0 = stage("mixed", jnp.concatenate([y_ret, y_gdn], axis=1))
    h1 = matmul(y0, w["ret_gdn_w_out"], add=x, name="mix0_out")
    h2, ffn0 = _ffn_layer_fwd(h1, w["norm_ffn"][0:1], w["ffn_w_up"][0], w["ffn_conv_w"][0], w["ffn_conv_b"][0:1], w["ffn_w_down"][0], "0")
    h2 = stage("layer0", h2)

    hn1 = norm_fwd(h2, w["norm_mix"][1:2], name="mix1_norm")
    gx = matmul(hn1, w["lru_w_in"], name="mix1_in")
    lru_p = (w["lru_conv_w"], w["lru_conv_b"], w["lru_w_a"], w["lru_b_a"], w["lru_w_x"], w["lru_b_x"], w["lru_lambda"])
    y1 = lru_fwd(gx, *lru_p, name="lru_fwd")
    h3 = matmul(y1, w["lru_w_out"], add=h2, name="mix1_out")
    h4, ffn1 = _ffn_layer_fwd(h3, w["norm_ffn"][1:2], w["ffn_w_up"][1], w["ffn_conv_w"][1], w["ffn_conv_b"][1:2], w["ffn_w_down"][1], "1")

    loss, dh4, g["norm_final"] = final_fwd_bwd(h4, w["norm_final"], target, name="final")

    dh3, dgf1, dwu1, dcw1, dcb1, dwd1 = _ffn_layer_bwd(ffn1, dh4, w["norm_ffn"][1:2], w["ffn_w_up"][1], w["ffn_conv_w"][1],
                                                     w["ffn_conv_b"][1:2], w["ffn_w_down"][1], "1")
    g["ffn_w_up_1"] = dwu1
    dh3 = stage("grads0_ready", dh3, g)
    dy1 = matmul(dh3, w["lru_w_out"], tb=True, name="mix1_out_dx")
    g["lru_w_out"] = matmul(y1, dh3, ta=True, split=_travel_layout("lru_w_out"), name="mix1_out_dw")
    dgate, dxr, g["lru_conv_w"], g["lru_conv_b"], g["lru_w_a"], g["lru_b_a"], g["lru_w_x"], g["lru_b_x"], g["lru_lambda"] = lru_bwd(
        gx, *lru_p, dy1, name="lru_bwd")
    dgx = stage("grads0_send", jnp.concatenate([dgate, dxr], axis=1), g)
    dhn1 = matmul(dgx, w["lru_w_in"], tb=True, name="mix1_in_dx")
    g["lru_w_in"] = matmul(hn1, dgx, ta=True, split=_travel_layout("lru_w_in"), name="mix1_in_dw")
    dh2, dgm1 = norm_bwd(h2, w["norm_mix"][1:2], dhn1, dh3, name="mix1_norm_bwd")
    dh2 = stage("grads1_ready", dh2, g)

    dh1, dgf0, dwu0, dcw0, dcb0, dwd0 = _ffn_layer_bwd(ffn0, dh2, w["norm_ffn"][0:1], w["ffn_w_up"][0], w["ffn_conv_w"][0],
                                                     w["ffn_conv_b"][0:1], w["ffn_w_down"][0], "0", dwd1)
    g["ffn_w_up_0"] = dwu0
    g["ffn_w_down"] = dwd0
    dh1 = stage("grads2_ready", stage("grads1_send", dh1, g), g)
    dy0 = matmul(dh1, w["ret_gdn_w_out"], tb=True, name="mix0_out_dx")
    g["ret_gdn_w_out"] = matmul(y0, dh1, ta=True, split=_travel_layout("ret_gdn_w_out"), name="mix0_out_dw")
    dq_r, dk_r, dv_r, dg_r = ret_bwd(p, tables, s_ret, dy0, name="ret_bwd")
    dy0, dq_r = stage("grads2_send", (dy0, dq_r), g)
    dcq, dck, dcv, dg_d, dsmall, dal, ddt, dgain = gdn_bwd(conv, p, small, a_log, dt_bias, w["gdn_out_gain"], s_gdn, dy0, name="gdn_bwd")
    dconv = jnp.concatenate([dcq, dck, dcv], axis=1)
    dp_conv, g["gdn_conv_w"] = gdn_conv_bwd(p, w["gdn_conv_w"], dconv, name="gdn_conv_bwd")
    dp = jnp.concatenate([dq_r, dk_r, dv_r, dg_r, dp_conv, dg_d], axis=1)
    dhn0 = matmul(dp, w_main, name="mix0_in_dx")
    dhn0 = matmul(dsmall, w_small, add=dhn0, name="mix0_in_small_dx")
    d_w_main = matmul(dp, hn0, ta=True, name="mix0_in_dw")
    d_w_small = matmul(dsmall, hn0, ta=True, name="mix0_in_small_dw")
    g["ret_gdn_w_in"] = jnp.concatenate([d_w_main, d_w_small[:2 * N_HEADS]], axis=0)
    dx, dgm0 = norm_bwd(x, w["norm_mix"][0:1], dhn0, dh1, name="mix0_norm_bwd")

    g["gdn_a_log"] = dal[:, :N_HEADS]
    g["gdn_dt_bias"] = ddt[:, :N_HEADS]
    g["gdn_out_gain"] = dgain
    g["norm_mix"] = jnp.concatenate([dgm0, dgm1], axis=0)
    g["norm_ffn"] = jnp.concatenate([dgf0, dgf1], axis=0)
    g["ffn_conv_w"] = jnp.stack([dcw0, dcw1])
    g["ffn_conv_b"] = jnp.concatenate([dcb0, dcb1], axis=0)
    return loss, dx, g


WEIGHTS = ("norm_mix", "norm_ffn", "ret_gdn_w_in", "gdn_conv_w", "gdn_a_log", "gdn_dt_bias", "gdn_out_gain", "ret_gdn_w_out",
           "lru_w_in", "lru_conv_w", "lru_conv_b", "lru_w_a", "lru_b_a", "lru_w_x", "lru_b_x", "lru_lambda", "lru_w_out",
           "ffn_w_up", "ffn_conv_w", "ffn_conv_b", "ffn_w_down", "norm_final")
MATMUL_SHARDED = {"ret_gdn_w_in": 1, "ret_gdn_w_out": 0, "lru_w_in": 1, "lru_w_out": 0, "ffn_w_up": 2, "ffn_w_down": 1}
VECTOR_SHARDED = {"gdn_conv_w": 1, "lru_conv_w": 1, "lru_conv_b": 1, "lru_b_a": 1, "lru_b_x": 1, "lru_lambda": 1, "ffn_conv_w": 2}
SHARDED = {**MATMUL_SHARDED, **VECTOR_SHARDED}
REPLICATED = tuple(n for n in WEIGHTS if n not in SHARDED)
SQUEEZE = {"ret_gdn_w_in", "gdn_conv_w", "ret_gdn_w_out", "lru_w_in", "lru_conv_w", "lru_w_a", "lru_w_x", "lru_w_out"}
MIX_IN = MIX_MAIN + 2 * N_HEADS
BIG_ARRAYS = {
    "ret_gdn_w_in": ("ret_gdn_w_in", None, (MIX_IN, D_MODEL), (N_SHARD, MIX_IN // N_SHARD, 2, D_MODEL // 2), (2, 0, 1, 3)),
    "ret_gdn_w_out": ("ret_gdn_w_out", None, (2 * GROUP, D_MODEL), (N_SHARD, 2, GROUP // N_SHARD, D_MODEL), (1, 0, 2, 3)),
    "lru_w_in": ("lru_w_in", None, (D_MODEL, 2 * D_MODEL), (2, D_MODEL // 2, N_SHARD, 2 * D_MODEL // N_SHARD), (0, 2, 1, 3)),
    "lru_w_out": ("lru_w_out", None, (D_MODEL, D_MODEL), (N_SHARD, 2, D_MODEL // (2 * N_SHARD), D_MODEL), (1, 0, 2, 3)),
    "ffn_w_up_0": ("ffn_w_up", 0, (D_MODEL, 2 * D_FF), (2, D_MODEL // 2, N_SHARD, 2 * D_FF // N_SHARD), (0, 2, 1, 3)),
    "ffn_w_up_1": ("ffn_w_up", 1, (D_MODEL, 2 * D_FF), (2, D_MODEL // 2, N_SHARD, 2 * D_FF // N_SHARD), (0, 2, 1, 3)),
    "ffn_w_down": ("ffn_w_down", None, (2, D_FF, D_MODEL), (2, N_SHARD, D_FF // N_SHARD, D_MODEL), (0, 1, 2, 3)),
}
GATHER_GROUPS = (("ret_gdn_w_in",), ("ret_gdn_w_out", "ffn_w_up_0", "ffn_w_down"), ("lru_w_in", "lru_w_out", "ffn_w_up_1"))
REDUCE_GROUPS = (("ffn_w_up_1",), ("lru_w_in", "lru_w_out"), ("ffn_w_up_0", "ffn_w_down"), ("ret_gdn_w_out", "ret_gdn_w_in"))
BLOCK_WEIGHTS = ("lru_w_a", "lru_w_x")
GATHER_COLLECTIVE_ID = 1
REDUCE_COLLECTIVE_ID = GATHER_COLLECTIVE_ID + len(GATHER_GROUPS)


TRANSPOSED = ("ret_gdn_w_in",)


def _shard_of(array, tensors):
    weight, layer = BIG_ARRAYS[array][:2]
    t = tensors[weight]
    if weight in TRANSPOSED:
        return jnp.swapaxes(t, 1, 2)[0]
    return _local_view(weight, t) if layer is None else t[layer]


def _core_halves(array, shard):
    _, _, _, split, perm = BIG_ARRAYS[array]
    kept = [k for k in range(4) if k != perm[1]]
    order = [kept.index(perm[0]), kept.index(perm[2]), kept.index(perm[3])]
    return shard.reshape([split[k] for k in kept]).transpose(order)


def _local_view(name, a):
    if name in SQUEEZE:
        return a[0]
    if a.ndim == 1:
        return a[None, :]
    return a


def kernel(x, norm_mix, norm_ffn, ret_gdn_w_in, gdn_conv_w, gdn_a_log, gdn_dt_bias, gdn_out_gain, ret_gdn_w_out, lru_w_in, lru_conv_w, lru_conv_b, lru_w_a, lru_b_a, lru_w_x, lru_b_x, lru_lambda, lru_w_out, ffn_w_up, ffn_conv_w, ffn_conv_b, ffn_w_down, norm_final, loss_target, m_norm_mix, m_norm_ffn, m_ret_gdn_w_in, m_gdn_conv_w, m_gdn_a_log, m_gdn_dt_bias, m_gdn_out_gain, m_ret_gdn_w_out, m_lru_w_in, m_lru_conv_w, m_lru_conv_b, m_lru_w_a, m_lru_b_a, m_lru_w_x, m_lru_b_x, m_lru_lambda, m_lru_w_out, m_ffn_w_up, m_ffn_conv_w, m_ffn_conv_b, m_ffn_w_down, m_norm_final, v_norm_mix, v_norm_ffn, v_ret_gdn_w_in, v_gdn_conv_w, v_gdn_a_log, v_gdn_dt_bias, v_gdn_out_gain, v_ret_gdn_w_out, v_lru_w_in, v_lru_conv_w, v_lru_conv_b, v_lru_w_a, v_lru_b_a, v_lru_w_x, v_lru_b_x, v_lru_lambda, v_lru_w_out, v_ffn_w_up, v_ffn_conv_w, v_ffn_conv_b, v_ffn_w_down, v_norm_final):
    given = dict(norm_mix=norm_mix, norm_ffn=norm_ffn, ret_gdn_w_in=ret_gdn_w_in, gdn_conv_w=gdn_conv_w, gdn_a_log=gdn_a_log, gdn_dt_bias=gdn_dt_bias, gdn_out_gain=gdn_out_gain, ret_gdn_w_out=ret_gdn_w_out, lru_w_in=lru_w_in, lru_conv_w=lru_conv_w, lru_conv_b=lru_conv_b, lru_w_a=lru_w_a, lru_b_a=lru_b_a, lru_w_x=lru_w_x, lru_b_x=lru_b_x, lru_lambda=lru_lambda, lru_w_out=lru_w_out, ffn_w_up=ffn_w_up, ffn_conv_w=ffn_conv_w, ffn_conv_b=ffn_conv_b, ffn_w_down=ffn_w_down, norm_final=norm_final)
    mom1 = dict(norm_mix=m_norm_mix, norm_ffn=m_norm_ffn, ret_gdn_w_in=m_ret_gdn_w_in, gdn_conv_w=m_gdn_conv_w, gdn_a_log=m_gdn_a_log, gdn_dt_bias=m_gdn_dt_bias, gdn_out_gain=m_gdn_out_gain, ret_gdn_w_out=m_ret_gdn_w_out, lru_w_in=m_lru_w_in, lru_conv_w=m_lru_conv_w, lru_conv_b=m_lru_conv_b, lru_w_a=m_lru_w_a, lru_b_a=m_lru_b_a, lru_w_x=m_lru_w_x, lru_b_x=m_lru_b_x, lru_lambda=m_lru_lambda, lru_w_out=m_lru_w_out, ffn_w_up=m_ffn_w_up, ffn_conv_w=m_ffn_conv_w, ffn_conv_b=m_ffn_conv_b, ffn_w_down=m_ffn_w_down, norm_final=m_norm_final)
    mom2 = dict(norm_mix=v_norm_mix, norm_ffn=v_norm_ffn, ret_gdn_w_in=v_ret_gdn_w_in, gdn_conv_w=v_gdn_conv_w, gdn_a_log=v_gdn_a_log, gdn_dt_bias=v_gdn_dt_bias, gdn_out_gain=v_gdn_out_gain, ret_gdn_w_out=v_ret_gdn_w_out, lru_w_in=v_lru_w_in, lru_conv_w=v_lru_conv_w, lru_conv_b=v_lru_conv_b, lru_w_a=v_lru_w_a, lru_b_a=v_lru_b_a, lru_w_x=v_lru_w_x, lru_b_x=v_lru_b_x, lru_lambda=v_lru_lambda, lru_w_out=v_lru_w_out, ffn_w_up=v_ffn_w_up, ffn_conv_w=v_ffn_conv_w, ffn_conv_b=v_ffn_conv_b, ffn_w_down=v_ffn_w_down, norm_final=v_norm_final)

    local = {n: _local_view(n, a) for n, a in given.items()}

    core = lax.axis_index("c")
    chip = 2 * lax.axis_index("x") + lax.axis_index("y")
    is_my_chip = lax.broadcasted_iota(jnp.int32, (N_SHARD, 1, 1), 0) == chip

    def by_core(mine, other):
        return jnp.where(core == 0, jnp.stack([mine, other]), jnp.stack([other, mine]))

    vec_names, rp_names = list(VECTOR_SHARDED), list(REPLICATED)
    full = dict(zip(vec_names, all_gather_shards([local[n] for n in vec_names], [SHARDED[n] for n in vec_names], F32, 32, "p")))
    for n in rp_names:
        full[n] = local[n]
    in_flight = {}

    def launch(gi, after=None):
        halves = []
        for a in GATHER_GROUPS[gi]:
            halves.append(_core_halves(a, _shard_of(a, given).astype(BF16)))
        if after is not None:
            halves, after = lax.optimization_barrier((halves, after))
        in_flight[gi] = (halves,) + gather_halves(halves, name=f"gather_weights_{gi}", collective_id=GATHER_COLLECTIVE_ID + gi)
        return after

    def land(gi, after):
        halves, lands, sibs = in_flight[gi]
        (lands, sibs), after = lax.optimization_barrier(((lands, sibs), after))
        for a, mine, got, passed in zip(GATHER_GROUPS[gi], halves, lands, sibs):
            weight, layer, full_shape, split, perm = BIG_ARRAYS[a]
            half_mine = jnp.where(is_my_chip, jnp.where(core == 0, mine[0], mine[1])[None], got)
            half_other = jnp.where(is_my_chip, jnp.where(core == 0, mine[1], mine[0])[None], passed)
            value = by_core(half_mine, half_other).transpose(tuple(np.argsort(perm))).reshape(full_shape)
            if layer is None:
                full[weight] = value
            else:
                full.setdefault(weight, [None, None])[layer] = value
        return after

    reducing = {}

    def reduce_ready(gi, grads, then=None, extra=()):
        def travelling(a):
            split, perm = _travel_layout(a)
            return grads[a] if grads[a].ndim == 4 else grads[a].reshape(split).transpose(perm)

        arrays = [travelling(a) for a in REDUCE_GROUPS[gi]] + list(extra)
        scatter = [True] * len(REDUCE_GROUPS[gi]) + [False] * len(extra)
        reducing[gi], then = reduce_between_cores(arrays, scatter, tag=str(gi), collective_id=REDUCE_COLLECTIVE_ID + 3 * gi, before=then)
        return then

    def reduce_send(gi, then=None):
        reducing[gi], then = reduce_between_chips(reducing[gi], before=then)
        return then

    def stage(name, tensors, grads=None):
        if name == "start":
            launch(0)
            launch(1)
            return land(0, tensors)
        if name == "normed":
            return launch(2, tensors)
        if name in ("mixed", "layer0"):
            return land({"mixed": 1, "layer0": 2}[name], tensors)
        gi = int(name[len("grads")])
        return reduce_ready(gi, grads, tensors) if name.endswith("_ready") else reduce_send(gi, tensors)

    loss_part, dx, grads = local_step(x[0], loss_target[0], full, stage)
    loss = lax.psum(loss_part[0, 0], ("x", "y", "c"))

    small_names = [n for n in rp_names if n not in BLOCK_WEIGHTS] + vec_names
    small_shapes = [grads[n].shape for n in small_names]
    small_rows = _pack_rows(sum(int(np.prod(s)) for s in small_shapes), 16)
    small = _pack([grads[n] for n in small_names], small_rows, F32).reshape(2, 1, small_rows // 2, LANES)
    last = len(REDUCE_GROUPS) - 1
    halves_of_blocks = [grads[n].reshape(2, 1, LRU_BLOCKS * HEAD // 2, HEAD) for n in BLOCK_WEIGHTS]
    reduce_ready(last, grads, extra=[small] + halves_of_blocks)
    reduce_send(last)
    reduced, result = {}, {}

    def finish(gi, after):
        g_own, g_sib = reduce_finish(reducing[gi], after)
        reduced.update(zip(list(REDUCE_GROUPS[gi]) + ["small"] + list(BLOCK_WEIGHTS), zip(g_own, g_sib)))

    def update(n):
        if n in TRANSPOSED:
            w3, m3, v3 = (jnp.swapaxes(t, 1, 2) for t in (given[n], mom1[n], mom2[n]))
            result[n] = tuple(jnp.swapaxes(t, 1, 2) for t in adamw_column_halves(w3, m3, v3, *reduced[n], name=f"adamw_{n}"))
            return
        done = None
        for a in (k for k, spec in BIG_ARRAYS.items() if spec[0] == n):
            r, cols = reduced[a][0].shape
            layer = BIG_ARRAYS[a][1] or 0
            w3, m3, v3 = (t if BIG_ARRAYS[a][1] is not None else t.reshape(1, 2 * r, cols) for t in (given[n], mom1[n], mom2[n]))
            done = adamw_halves(w3, m3, v3, *reduced[a], layer=layer, prev=done, name=f"adamw_{a}")
        result[n] = done

    for gi in range(last):
        finish(gi, (dx, reducing[last][1]))
    late = {BIG_ARRAYS[a][0] for a in REDUCE_GROUPS[last]}
    for n in MATMUL_SHARDED:
        if n not in late:
            update(n)
    finish(last, tuple(result[n][0] for n in MATMUL_SHARDED if n not in late))
    for n in MATMUL_SHARDED:
        if n in late:
            update(n)

    for n in BLOCK_WEIGHTS:
        w3, m3, v3 = (t.reshape(1, LRU_BLOCKS * HEAD, HEAD) for t in (given[n], mom1[n], mom2[n]))
        result[n] = adamw_halves(w3, m3, v3, *reduced[n], name=f"adamw_{n}")

    g_small = dict(zip(small_names, _unpack(by_core(*reduced["small"]).reshape(small_rows, LANES), small_shapes)))
    for n in vec_names:
        size = local[n].shape[SHARDED[n]]
        g_small[n] = lax.dynamic_slice_in_dim(g_small[n], chip * size, size, axis=SHARDED[n])
    loc_shapes = [local[n].shape for n in small_names]
    loc_rows = _pack_rows(sum(int(np.prod(s)) for s in loc_shapes), 256)
    packs = [_pack([src[n] for n in small_names], loc_rows, F32) for src in (given, g_small, mom1, mom2)]
    d_s, m_s, v_s = adamw(*packs, name="adamw_small")
    for n, d, nm, nv in zip(small_names, _unpack(d_s, loc_shapes), _unpack(m_s, loc_shapes), _unpack(v_s, loc_shapes)):
        result[n] = (g_small[n], d, nm, nv)

    outs = [[result[n][k].reshape(given[n].shape) for n in WEIGHTS] for k in range(4)]
    return (loss, dx[None], *outs[0], *outs[1], *outs[2], *outs[3])
```

```python
import functools

import numpy as np
import jax
import jax.numpy as jnp
from jax import lax
from jax.experimental import pallas as pl
from jax.experimental.pallas import tpu as pltpu
from jax.experimental.pallas import tpu_sc as plsc

F32 = jnp.float32
BF16 = jnp.bfloat16
HI = lax.Precision.HIGHEST
MESH = pl.DeviceIdType.MESH

SEQ = 2048
D_MODEL = 1024
N_HEADS = 4
HEAD = 128
RET_CHUNK = 128
GDN_CHUNK = 64
GROUP = N_HEADS * HEAD
MIX_MAIN = 8 * GROUP
D_FF = 2816
LRU_BLOCKS = 8
LRU_C = 8.0
ROPE_BASE = 10000.0
EPS = 1e-6
N_SHARD = 4
LANES = 128

ADAM_LR, ADAM_B1, ADAM_B2, ADAM_EPS, ADAM_WD, ADAM_STEP = 0.001, 0.9, 0.999, 1e-08, 0.01, 10

VMEM_LIMIT_BYTES = 56 * 1024 * 1024

_roll = pltpu.roll


def _params(**kw):
    return pltpu.CompilerParams(vmem_limit_bytes=VMEM_LIMIT_BYTES, **kw)


def _sds(shape, dtype):
    return jax.ShapeDtypeStruct(tuple(shape), dtype)


def _shift_raw(x, d):
    n = x.shape[0]
    t = lax.broadcasted_iota(jnp.int32, x.shape, 0)
    if d > 0:
        return jnp.where(t >= d, _roll(x, d, 0), 0.0)
    return jnp.where(t < n + d, _roll(x, n + d, 0), 0.0)


@functools.partial(jax.custom_vjp, nondiff_argnums=(1,))
def shift_rows(x, d):
    return _shift_raw(x, d)


def _shift_fwd(x, d):
    return _shift_raw(x, d), None


def _shift_bwd(d, _, g):
    return (_shift_raw(g, -d),)


shift_rows.defvjp(_shift_fwd, _shift_bwd)


@jax.custom_vjp
def swap_halves(x):
    return _roll(x, HEAD // 2, 1)


def _swap_fwd(x):
    return _roll(x, HEAD // 2, 1), None


def _swap_bwd(_, g):
    return (_roll(g, HEAD // 2, 1),)


swap_halves.defvjp(_swap_fwd, _swap_bwd)


def _scan_raw(a, u, reverse):
    n = a.shape[0]
    t = lax.broadcasted_iota(jnp.int32, a.shape, 0)
    d = 1
    while d < n:
        if reverse:
            m = t < n - d
            a_s, u_s = _roll(a, n - d, 0), _roll(u, n - d, 0)
        else:
            m = t >= d
            a_s, u_s = _roll(a, d, 0), _roll(u, d, 0)
        u = a * jnp.where(m, u_s, 0.0) + u
        a = a * jnp.where(m, a_s, 1.0)
        d *= 2
    return u


@jax.custom_vjp
def lin_scan(a, u):
    return _scan_raw(a, u, False)


def _lin_scan_fwd(a, u):
    hs = _scan_raw(a, u, False)
    return hs, (a, hs)


def _lin_scan_bwd(res, g):
    a, hs = res
    lam = _scan_raw(_shift_raw(a, -1), g, True)
    return lam * _shift_raw(hs, 1), lam


lin_scan.defvjp(_lin_scan_fwd, _lin_scan_bwd)


def _bdot(a, b, dims=(((1,), (0,)), ((), ()))):
    return lax.dot_general(a.astype(BF16), b.astype(BF16), dims, preferred_element_type=F32)


def _each(f, *seqs):
    return tuple(f(*a) for a in zip(*seqs))


def _split_bf16(a):
    hi = a.astype(BF16)
    return hi, (a - hi.astype(F32)).astype(BF16)


def _dot3_raw(a_s, b_s):
    a_hl = _each(_split_bf16, a_s)
    b_hl = _each(_split_bf16, b_s)
    hh = _each(lambda a, b: _bdot(a[0], b[0]), a_hl, b_hl)
    hl = _each(lambda a, b: _bdot(a[0], b[1]), a_hl, b_hl)
    lh = _each(lambda a, b: _bdot(a[1], b[0]), a_hl, b_hl)
    return _each(lambda x, y, z: x + (y + z), hh, hl, lh)


@jax.custom_vjp
def dot3(a_s, b_s):
    return _dot3_raw(a_s, b_s)


def _dot3_fwd(a_s, b_s):
    return _dot3_raw(a_s, b_s), (a_s, b_s)


def _dot3_bwd(res, g_s):
    a_s, b_s = res
    return (_each(lambda g, b: _bdot(g, b, (((1,), (1,)), ((), ()))), g_s, b_s),
            _each(lambda a, g: _bdot(a, g, (((0,), (0,)), ((), ()))), a_s, g_s))


dot3.defvjp(_dot3_fwd, _dot3_bwd)


def _eye(n):
    i = lax.broadcasted_iota(jnp.int32, (n, n), 0)
    j = lax.broadcasted_iota(jnp.int32, (n, n), 1)
    return (i == j).astype(F32)


def _unit_lower_inverse_raw(lmats):
    n = lmats[0].shape[0]
    eye = _eye(n)
    ps = _each(lambda l: -l, lmats)
    invs = _each(lambda x: eye + x, ps)
    k = 1
    while 2 * k < n:
        ps = _each(lambda p: _bdot(p, p), ps)
        invs = _each(lambda inv, p: inv + _bdot(inv, p), invs, ps)
        k *= 2
    prods = _dot3_raw(lmats, invs)
    resids = _each(lambda inv, pr: eye - inv - pr, invs, prods)
    return _each(lambda inv, r: inv + _bdot(inv, r), invs, resids)


@jax.custom_vjp
def unit_lower_inverse(lmats):
    return _unit_lower_inverse_raw(lmats)


def _uli_fwd(lmats):
    invs = _unit_lower_inverse_raw(lmats)
    return invs, invs


def _uli_bwd(invs, g_s):
    ms = _each(lambda inv, g: _bdot(inv, g, (((0,), (0,)), ((), ()))), invs, g_s)
    return (_each(lambda m, inv: -_bdot(m, inv, (((1,), (1,)), ((), ()))), ms, invs),)


unit_lower_inverse.defvjp(_uli_fwd, _uli_bwd)


def _cumsum_raw(x, reverse):
    n = x.shape[0]
    t = lax.broadcasted_iota(jnp.int32, x.shape, 0)
    d = 1
    while d < n:
        if reverse:
            x = x + jnp.where(t < n - d, _roll(x, n - d, 0), 0.0)
        else:
            x = x + jnp.where(t >= d, _roll(x, d, 0), 0.0)
        d *= 2
    return x


@jax.custom_vjp
def cumsum_rows(x):
    return _cumsum_raw(x, False)


def _cumsum_fwd(x):
    return _cumsum_raw(x, False), None


def _cumsum_bwd(_, g):
    return (_cumsum_raw(g, True),)


cumsum_rows.defvjp(_cumsum_fwd, _cumsum_bwd)


_NT = (((1,), (1,)), ((), ()))
_TN = (((0,), (0,)), ((), ()))


def _softplus(x):
    return jnp.maximum(x, 0.0) + jnp.log1p(jnp.exp(-jnp.abs(x)))


def _expm1_nonpos(x):
    poly = x * (1.0 + x * (0.5 + x * (1.0 / 6 + x * (1.0 / 24 + x * (1.0 / 120 + x * (1.0 / 720))))))
    return jnp.where(x > -0.25, poly, jnp.exp(x) - 1.0)


def _rms(x):
    return x * lax.rsqrt(jnp.mean(x * x, axis=-1, keepdims=True) + EPS)


def _causal_conv(x, w, width):
    y = w[width - 1:width, :] * x
    for j in range(width - 1):
        y = y + w[j:j + 1, :] * shift_rows(x, width - 1 - j)
    return y


def _norm_fn(x, g):
    return _rms(x) * g


def _ffn_act_fn(ug, uv, wg, wv, bg, bv):
    return jax.nn.silu(_causal_conv(ug, wg, 3) + bg) * (_causal_conv(uv, wv, 3) + bv)


def _gdn_conv_fn(x, w):
    return jax.nn.silu(_causal_conv(x, w, 4))


def _lru_fn(gate, x, cw, cb, wa, ba, wx, bx, lam):
    xr = _causal_conv(x, cw, 4) + cb
    r = jax.nn.sigmoid(_bdot(xr, wa) + ba)
    i = jax.nn.sigmoid(_bdot(xr, wx) + bx)
    log_a = -LRU_C * r * _softplus(-lam)
    a = jnp.exp(log_a)
    u = jnp.sqrt(-_expm1_nonpos(2.0 * log_a)) * (i * xr)
    hs = lin_scan(a, u)
    return jax.nn.gelu(gate) * hs


def _ret_fn(qs, ks, vs, gates, states, cos2, sin2, dmasks, ktails, qdecs, cdecs):
    qrs = _each(lambda q: q * cos2 + swap_halves(q) * sin2, qs)
    krs = _each(lambda k: (k * cos2 + swap_halves(k) * sin2) * (HEAD ** -0.5), ks)
    scores = _each(lambda q, k, m: _bdot(q, k, _NT) * m, qrs, krs, dmasks)
    inter = _each(lambda q, d, s: _bdot(q * d, s), qrs, qdecs, states)
    os_ = _each(lambda sc, v, x: _bdot(sc, v) + x, scores, vs, inter)
    new_states = _each(lambda s, cd, k, kt, v: s * cd + _bdot(k * kt, v, _TN), states, cdecs, krs, ktails, vs)
    ys = _each(lambda o, g: _rms(o) * jax.nn.silu(g), os_, gates)
    return ys, new_states


def _pick_lane(x, lane_idx):
    lane = lax.broadcasted_iota(jnp.int32, x.shape, 1)
    return jnp.sum(jnp.where(lane == lane_idx, x, 0.0), axis=1, keepdims=True)


def _l2norm(x):
    return x * lax.rsqrt(jnp.sum(x * x, axis=-1, keepdims=True) + EPS)


def _gdn_fn(qcs, kcs, vcs, gates, small, a_log, dt_bias, gain, states):
    c = GDN_CHUNK
    heads = tuple(range(len(qcs)))
    qs = _each(lambda x: _l2norm(x) * (HEAD ** -0.5), qcs)
    ks = _each(_l2norm, kcs)
    betas = _each(lambda h: jax.nn.sigmoid(_pick_lane(small, h)), heads)
    gs = _each(lambda h: -jnp.exp(_pick_lane(a_log, h)) * _softplus(_pick_lane(small, h + N_HEADS) + _pick_lane(dt_bias, h)), heads)
    i = lax.broadcasted_iota(jnp.int32, (c, c), 0)
    j = lax.broadcasted_iota(jnp.int32, (c, c), 1)
    tril = i >= j
    gcs = _each(lambda g: cumsum_rows(jnp.broadcast_to(g, (c, LANES)))[:, :1], gs)
    gc_rows = _each(lambda gc: jnp.broadcast_to(gc, (c, c)), gcs)
    decays = _each(lambda r: jnp.where(tril, jnp.exp(jnp.where(tril, r - r.T, 0.0)), 0.0), gc_rows)
    kbs = _each(lambda k, b: k * b, ks, betas)
    lmats = _each(lambda kb, k, d: jnp.where(i > j, _bdot(kb, k, _NT) * d, 0.0), kbs, ks, decays)
    attns = _each(lambda q, k, d: jnp.where(tril, _bdot(q, k, _NT) * d, 0.0), qs, ks, decays)
    invs = unit_lower_inverse(lmats)
    us = dot3(invs, _each(lambda v, b: v * b, vcs, betas))
    ws = dot3(invs, _each(lambda kb, gc: kb * jnp.exp(gc), kbs, gcs))
    g_lasts = _each(lambda g: jnp.sum(g, axis=0, keepdims=True), gs)
    v_news = _each(lambda u, w, s: u - _bdot(w, s), us, ws, states)
    inter = _each(lambda q, gc, s: _bdot(q * jnp.exp(gc), s), qs, gcs, states)
    os_ = _each(lambda x, a, v: x + _bdot(a, v), inter, attns, v_news)
    new_states = _each(lambda s, gl, k, gc, v: s * jnp.exp(gl) + _bdot(k * jnp.exp(gl - gc), v, _TN), states, g_lasts, ks, gcs, v_news)
    ys = _each(lambda o, gate: _rms(o) * gain * jax.nn.silu(gate), os_, gates)
    return ys, new_states


def _final_fn(h, g, target):
    y = _rms(h) * g
    return 0.5 * jnp.sum(jnp.mean(jnp.square(y - target), axis=-1, keepdims=True), axis=0, keepdims=True)


def _tile(n, candidates):
    for t in candidates:
        if n % t == 0:
            return t
    raise ValueError(f"no tile for {n}")


def matmul(a, b, *, ta=False, tb=False, add=None, out_dtype=F32, tm=None, tn=None, split=None, layer=None, name):
    m = a.shape[1] if ta else a.shape[0]
    k = a.shape[0] if ta else a.shape[1]
    n = b.shape[0] if tb else b.shape[1]
    assert k == (b.shape[1] if tb else b.shape[0])
    out_shape, out_block, out_index = (m, n), None, lambda i, j: (i, j)
    if split is not None:
        dims4, perm = split
        out_shape = tuple(dims4[p] for p in perm)
        r, cols = out_shape[2:]
        tm, tn = tm or _tile(r, (512, 256, 128)), tn or _tile(cols, (1408, 1024, 512))
        rb, cb = r // tm, cols // tn
        out_block = (None, None, tm, tn)
        if perm == (0, 2, 1, 3):
            out_index = lambda i, j: (i // rb, j // cb, i % rb, j % cb)
        elif perm == (1, 0, 2, 3):
            tm, out_block = 2 * r, (2, None, r, tn)
            out_index = lambda i, j: (0, i, 0, j)
        else:
            raise ValueError(perm)
    tm = tm or _tile(m, (1024, 512, 1408, 256, 128))
    tn = tn or _tile(n, (512, 1408, 256, 128))
    aliases, prev = {}, None
    if layer is not None:
        index, count, prev = layer
        out_shape, out_block, out_index = (count, m, n), (None, tm, tn), lambda i, j: (index, i, j)
    dims = (((0 if ta else 1,), (1 if tb else 0,)), ((), ()))

    def body(a_ref, b_ref, *rest):
        acc = lax.dot_general(a_ref[...].astype(BF16), b_ref[...].astype(BF16), dims, preferred_element_type=F32)
        if add is not None:
            acc = acc + rest[0][...]
        rest[-1][...] = acc.astype(out_dtype).reshape(rest[-1].shape)

    a_spec = pl.BlockSpec((k, tm), lambda i, j: (0, i)) if ta else pl.BlockSpec((tm, k), lambda i, j: (i, 0))
    b_spec = pl.BlockSpec((tn, k), lambda i, j: (j, 0)) if tb else pl.BlockSpec((k, tn), lambda i, j: (0, j))
    o_spec = pl.BlockSpec(out_block or (tm, tn), out_index)
    in_specs, args = [a_spec, b_spec], [a, b]
    if add is not None:
        in_specs.append(o_spec)
        args.append(add)
    if prev is not None:
        aliases = {len(args): 0}
        in_specs.append(pl.BlockSpec(memory_space=pl.ANY))
        args.append(prev)
    return pl.pallas_call(body, out_shape=_sds(out_shape, out_dtype), grid=(m // tm, n // tn), in_specs=in_specs,
                          out_specs=o_spec, input_output_aliases=aliases, compiler_params=_params(), name=name)(*args)


ROW_TILE = 256


def norm_fwd(x, g, *, name):
    t, d = x.shape

    def body(x_ref, g_ref, o_ref):
        o_ref[...] = _norm_fn(x_ref[...], g_ref[...]).astype(BF16)

    return pl.pallas_call(body, out_shape=_sds((t, d), BF16), grid=(t // ROW_TILE,),
                          in_specs=[pl.BlockSpec((ROW_TILE, d), lambda i: (i, 0)), pl.BlockSpec((1, d), lambda i: (0, 0))],
                          out_specs=pl.BlockSpec((ROW_TILE, d), lambda i: (i, 0)), compiler_params=_params(), name=name)(x, g)


def norm_bwd(x, g, dy, dres, *, name):
    t, d = x.shape

    def body(x_ref, g_ref, dy_ref, dres_ref, dx_ref, dg_ref):
        _, vjp = jax.vjp(_norm_fn, x_ref[...], g_ref[...])
        dx, dg = vjp(dy_ref[...])
        dx_ref[...] = dx + dres_ref[...]

        @pl.when(pl.program_id(0) == 0)
        def _():
            dg_ref[...] = jnp.zeros_like(dg_ref)

        dg_ref[...] += dg

    row = pl.BlockSpec((ROW_TILE, d), lambda i: (i, 0))
    vec = pl.BlockSpec((1, d), lambda i: (0, 0))
    return pl.pallas_call(body, out_shape=(_sds((t, d), F32), _sds((1, d), F32)), grid=(t // ROW_TILE,),
                          in_specs=[row, vec, row, row], out_specs=(row, vec), compiler_params=_params(), name=name)(x, g, dy, dres)


def final_fwd_bwd(h, g, target, *, name):
    t, d = h.shape

    def body(h_ref, g_ref, t_ref, loss_ref, dh_ref, dg_ref):
        tgt = t_ref[...]
        loss, vjp = jax.vjp(lambda hh, gg: _final_fn(hh, gg, tgt), h_ref[...], g_ref[...])
        dh, dg = vjp(jnp.ones((1, 1), F32))
        dh_ref[...] = dh

        @pl.when(pl.program_id(0) == 0)
        def _():
            dg_ref[...] = jnp.zeros_like(dg_ref)
            loss_ref[...] = jnp.zeros_like(loss_ref)

        dg_ref[...] += dg
        loss_ref[...] += jnp.broadcast_to(loss, loss_ref.shape)

    row = pl.BlockSpec((ROW_TILE, d), lambda i: (i, 0))
    vec = pl.BlockSpec((1, d), lambda i: (0, 0))
    return pl.pallas_call(body, out_shape=(_sds((1, LANES), F32), _sds((t, d), F32), _sds((1, d), F32)), grid=(t // ROW_TILE,),
                          in_specs=[row, vec, row], out_specs=(pl.BlockSpec((1, LANES), lambda i: (0, 0)), row, vec),
                          compiler_params=_params(), name=name)(h, g, target)


FFN_FWD_COLS = 256
FFN_BWD_COLS = 128


def ffn_act_fwd(u, cw, cb, *, name):
    t = u.shape[0]
    w = FFN_FWD_COLS
    nb = D_FF // w

    def body(ug_ref, uv_ref, wg_ref, wv_ref, bg_ref, bv_ref, o_ref):
        o_ref[...] = _ffn_act_fn(ug_ref[...], uv_ref[...], wg_ref[...], wv_ref[...], bg_ref[...], bv_ref[...]).astype(BF16)

    def col(rows, off):
        return pl.BlockSpec((rows, w), lambda j: (0, j + off))

    return pl.pallas_call(body, out_shape=_sds((t, D_FF), BF16), grid=(nb,),
                          in_specs=[col(t, 0), col(t, nb), col(3, 0), col(3, nb), col(1, 0), col(1, nb)],
                          out_specs=col(t, 0), compiler_params=_params(), name=name)(u, u, cw, cw, cb, cb)


def ffn_act_bwd(u, cw, cb, da, *, name):
    t = u.shape[0]
    w = FFN_BWD_COLS
    nb = D_FF // w

    def body(ug_ref, uv_ref, wg_ref, wv_ref, bg_ref, bv_ref, da_ref, dug_ref, duv_ref, dwg_ref, dwv_ref, dbg_ref, dbv_ref):
        _, vjp = jax.vjp(_ffn_act_fn, ug_ref[...], uv_ref[...], wg_ref[...], wv_ref[...], bg_ref[...], bv_ref[...])
        dug, duv, dwg, dwv, dbg, dbv = vjp(da_ref[...])
        dug_ref[...] = dug.astype(BF16)
        duv_ref[...] = duv.astype(BF16)
        dwg_ref[...] = dwg
        dwv_ref[...] = dwv
        dbg_ref[...] = dbg
        dbv_ref[...] = dbv

    def col(rows, off):
        return pl.BlockSpec((rows, w), lambda j: (0, j + off))

    outs = pl.pallas_call(
        body, out_shape=(_sds((t, D_FF), BF16), _sds((t, D_FF), BF16), _sds((3, D_FF), F32), _sds((3, D_FF), F32),
                         _sds((1, D_FF), F32), _sds((1, D_FF), F32)),
        grid=(nb,), in_specs=[col(t, 0), col(t, nb), col(3, 0), col(3, nb), col(1, 0), col(1, nb), col(t, 0)],
        out_specs=(col(t, 0), col(t, 0), col(3, 0), col(3, 0), col(1, 0), col(1, 0)), compiler_params=_params(), name=name,
    )(u, u, cw, cw, cb, cb, da)
    dug, duv, dwg, dwv, dbg, dbv = outs
    return dug, duv, jnp.concatenate([dwg, dwv], axis=1), jnp.concatenate([dbg, dbv], axis=1)


GDN_CONV_COLS = 256
GDN_CONV_OFF = 4 * GROUP


def gdn_conv_fwd(p, cw, *, name):
    t = p.shape[0]
    w = GDN_CONV_COLS
    nb = 3 * GROUP // w
    off = GDN_CONV_OFF // w

    def body(x_ref, w_ref, o_ref):
        o_ref[...] = _gdn_conv_fn(x_ref[...], w_ref[...])

    return pl.pallas_call(body, out_shape=_sds((t, 3 * GROUP), F32), grid=(nb,),
                          in_specs=[pl.BlockSpec((t, w), lambda j: (0, j + off)), pl.BlockSpec((4, w), lambda j: (0, j))],
                          out_specs=pl.BlockSpec((t, w), lambda j: (0, j)), compiler_params=_params(), name=name)(p, cw)


def gdn_conv_bwd(p, cw, dc, *, name):
    t = p.shape[0]
    w = GDN_CONV_COLS
    nb = 3 * GROUP // w
    off = GDN_CONV_OFF // w

    def body(x_ref, w_ref, dc_ref, dx_ref, dw_ref):
        _, vjp = jax.vjp(_gdn_conv_fn, x_ref[...], w_ref[...])
        dx, dw = vjp(dc_ref[...])
        dx_ref[...] = dx.astype(BF16)
        dw_ref[...] = dw

    blk = pl.BlockSpec((t, w), lambda j: (0, j))
    wblk = pl.BlockSpec((4, w), lambda j: (0, j))
    return pl.pallas_call(body, out_shape=(_sds((t, 3 * GROUP), BF16), _sds((4, 3 * GROUP), F32)), grid=(nb,),
                          in_specs=[pl.BlockSpec((t, w), lambda j: (0, j + off)), wblk, blk], out_specs=(blk, wblk),
                          compiler_params=_params(), name=name)(p, cw, dc)


def _lru_specs(t):
    w = D_MODEL // LRU_BLOCKS
    gate = pl.BlockSpec((t, w), lambda j: (0, j))
    xin = pl.BlockSpec((t, w), lambda j: (0, j + LRU_BLOCKS))
    cw = pl.BlockSpec((4, w), lambda j: (0, j))
    vec = pl.BlockSpec((1, w), lambda j: (0, j))
    mat = pl.BlockSpec((None, w, w), lambda j: (j, 0, 0))
    return gate, xin, cw, vec, mat


def lru_fwd(gx, cw, cb, wa, ba, wx, bx, lam, *, name):
    t = gx.shape[0]
    gate, xin, cws, vec, mat = _lru_specs(t)

    def body(g_ref, x_ref, cw_ref, cb_ref, wa_ref, ba_ref, wx_ref, bx_ref, lam_ref, o_ref):
        o_ref[...] = _lru_fn(g_ref[...], x_ref[...], cw_ref[...], cb_ref[...], wa_ref[...], ba_ref[...], wx_ref[...],
                             bx_ref[...], lam_ref[...]).astype(BF16)

    return pl.pallas_call(body, out_shape=_sds((t, D_MODEL), BF16), grid=(LRU_BLOCKS,),
                          in_specs=[gate, xin, cws, vec, mat, vec, mat, vec, vec], out_specs=gate,
                          compiler_params=_params(), name=name)(gx, gx, cw, cb, wa, ba, wx, bx, lam)


def lru_bwd(gx, cw, cb, wa, ba, wx, bx, lam, dy, *, name):
    t = gx.shape[0]
    gate, xin, cws, vec, mat = _lru_specs(t)

    def body(g_ref, x_ref, cw_ref, cb_ref, wa_ref, ba_ref, wx_ref, bx_ref, lam_ref, dy_ref,
             dg_ref, dx_ref, dcw_ref, dcb_ref, dwa_ref, dba_ref, dwx_ref, dbx_ref, dlam_ref):
        _, vjp = jax.vjp(_lru_fn, g_ref[...], x_ref[...], cw_ref[...], cb_ref[...], wa_ref[...], ba_ref[...], wx_ref[...],
                         bx_ref[...], lam_ref[...])
        dg, dx, dcw, dcb, dwa, dba, dwx, dbx, dlam = vjp(dy_ref[...])
        dg_ref[...] = dg.astype(BF16)
        dx_ref[...] = dx.astype(BF16)
        dcw_ref[...] = dcw
        dcb_ref[...] = dcb
        dwa_ref[...] = dwa
        dba_ref[...] = dba
        dwx_ref[...] = dwx
        dbx_ref[...] = dbx
        dlam_ref[...] = dlam

    d = D_MODEL
    w = d // LRU_BLOCKS
    out_shape = (_sds((t, d), BF16), _sds((t, d), BF16), _sds((4, d), F32), _sds((1, d), F32), _sds((LRU_BLOCKS, w, w), F32),
                 _sds((1, d), F32), _sds((LRU_BLOCKS, w, w), F32), _sds((1, d), F32), _sds((1, d), F32))
    return pl.pallas_call(body, out_shape=out_shape, grid=(LRU_BLOCKS,),
                          in_specs=[gate, xin, cws, vec, mat, vec, mat, vec, vec, gate],
                          out_specs=(gate, gate, cws, vec, mat, vec, mat, vec, vec), compiler_params=_params(), name=name,
                          )(gx, gx, cw, cb, wa, ba, wx, bx, lam, dy)


def _ret_tables():
    half = HEAD // 2
    inv_freq = (np.float32(ROPE_BASE) ** (-np.arange(half, dtype=np.float32) / np.float32(half))).astype(np.float32)
    ang = (np.arange(SEQ, dtype=np.float32)[:, None] * inv_freq[None, :]).astype(np.float64)
    cos2 = np.concatenate([np.cos(ang), np.cos(ang)], axis=1).astype(np.float32)
    sin2 = np.concatenate([-np.sin(ang), np.sin(ang)], axis=1).astype(np.float32)
    c = RET_CHUNK
    log_gamma = np.log1p(-np.exp2(-5.0 - np.arange(N_HEADS, dtype=np.float64)))
    idx = np.arange(c, dtype=np.float64)
    rel = idx[:, None] - idx[None, :]
    dmask = np.where(rel >= 0, np.exp(log_gamma[:, None, None] * np.maximum(rel, 0.0)), 0.0)
    ones = np.ones((N_HEADS, c, HEAD))
    ktail = np.exp(log_gamma[:, None] * (c - 1 - idx))[:, :, None] * ones
    qdec = np.exp(log_gamma[:, None] * (idx + 1.0))[:, :, None] * ones
    cdec = np.exp(log_gamma * c)[:, None, None] * ones
    return tuple(jnp.asarray(a, F32) for a in (cos2, sin2, dmask, ktail, qdec, cdec))


def _ret_specs(rev):
    c = RET_CHUNK
    nc = SEQ // c

    def n_of(n):
        return nc - 1 - n if rev else n

    def group(off):
        return pl.BlockSpec((c, GROUP), lambda n: (n_of(n), off))

    tab = pl.BlockSpec((c, HEAD), lambda n: (n_of(n), 0))
    const = pl.BlockSpec((N_HEADS, c, HEAD), lambda n: (0, 0, 0))
    state = pl.BlockSpec((N_HEADS, None, HEAD, HEAD), lambda n: (0, n_of(n), 0, 0))
    return group, tab, const, state, nc


def _head(h):
    return slice(h * HEAD, (h + 1) * HEAD)


def ret_fwd(p, tables, *, name):
    group, tab, const, state, nc = _ret_specs(False)

    def body(q_ref, k_ref, v_ref, g_ref, cos_ref, sin_ref, dm_ref, kt_ref, qd_ref, cd_ref, y_ref, st_ref, s_scr):
        @pl.when(pl.program_id(0) == 0)
        def _():
            s_scr[...] = jnp.zeros_like(s_scr)

        heads = range(N_HEADS)
        states = tuple(s_scr[h] for h in heads)
        ys, new_states = _ret_fn(*(tuple(r[:, _head(h)] for h in heads) for r in (q_ref, k_ref, v_ref, g_ref)), states,
                                 cos_ref[...], sin_ref[...], *(tuple(r[h] for h in heads) for r in (dm_ref, kt_ref, qd_ref, cd_ref)))
        for h in heads:
            st_ref[h] = states[h]
            y_ref[:, _head(h)] = ys[h].astype(BF16)
            s_scr[h] = new_states[h]

    return pl.pallas_call(
        body, out_shape=(_sds((SEQ, GROUP), BF16), _sds((N_HEADS, nc, HEAD, HEAD), F32)), grid=(nc,),
        in_specs=[group(0), group(1), group(2), group(3), tab, tab, const, const, const, const],
        out_specs=(group(0), state), scratch_shapes=[pltpu.VMEM((N_HEADS, HEAD, HEAD), F32)], compiler_params=_params(), name=name,
    )(p, p, p, p, *tables)


def ret_bwd(p, tables, states, dy, *, name):
    group, tab, const, state, nc = _ret_specs(True)

    def body(q_ref, k_ref, v_ref, g_ref, cos_ref, sin_ref, dm_ref, kt_ref, qd_ref, cd_ref, st_ref, dy_ref,
             dq_ref, dk_ref, dv_ref, dg_ref, ds_scr):
        @pl.when(pl.program_id(0) == 0)
        def _():
            ds_scr[...] = jnp.zeros_like(ds_scr)

        heads = range(N_HEADS)
        consts = (cos_ref[...], sin_ref[...], *(tuple(r[h] for h in heads) for r in (dm_ref, kt_ref, qd_ref, cd_ref)))
        _, vjp = jax.vjp(lambda *a: _ret_fn(*a, *consts), *(tuple(r[:, _head(h)] for h in heads) for r in (q_ref, k_ref, v_ref, g_ref)),
                         tuple(st_ref[h] for h in heads))
        dqs, dks, dvs, dgs, dss = vjp((tuple(dy_ref[:, _head(h)] for h in heads), tuple(ds_scr[h] for h in heads)))
        for h in heads:
            dq_ref[:, _head(h)] = dqs[h].astype(BF16)
            dk_ref[:, _head(h)] = dks[h].astype(BF16)
            dv_ref[:, _head(h)] = dvs[h].astype(BF16)
            dg_ref[:, _head(h)] = dgs[h].astype(BF16)
            ds_scr[h] = dss[h]

    out = _sds((SEQ, GROUP), BF16)
    return pl.pallas_call(
        body, out_shape=(out, out, out, out), grid=(nc,),
        in_specs=[group(0), group(1), group(2), group(3), tab, tab, const, const, const, const, state, group(0)],
        out_specs=(group(0), group(0), group(0), group(0)), scratch_shapes=[pltpu.VMEM((N_HEADS, HEAD, HEAD), F32)],
        compiler_params=_params(), name=name,
    )(p, p, p, p, *tables, states, dy)


def _gdn_specs(rev):
    c = GDN_CHUNK
    nc = SEQ // c

    def n_of(n):
        return nc - 1 - n if rev else n

    def group(off):
        return pl.BlockSpec((c, GROUP), lambda n: (n_of(n), off))

    small = pl.BlockSpec((c, LANES), lambda n: (n_of(n), 0))
    vec = pl.BlockSpec((1, LANES), lambda n: (0, 0))
    state = pl.BlockSpec((N_HEADS, None, HEAD, HEAD), lambda n: (0, n_of(n), 0, 0))
    return group, small, vec, state, nc


GDN_GATE_GROUP = 7


def gdn_fwd(conv, p, small, a_log, dt_bias, gain, *, name):
    group, sm, vec, state, nc = _gdn_specs(False)

    def body(q_ref, k_ref, v_ref, g_ref, sm_ref, al_ref, dt_ref, gn_ref, y_ref, st_ref, s_scr):
        @pl.when(pl.program_id(0) == 0)
        def _():
            s_scr[...] = jnp.zeros_like(s_scr)

        states = tuple(s_scr[h] for h in range(N_HEADS))
        ys, new_states = _gdn_fn(*(tuple(r[:, _head(h)] for h in range(N_HEADS)) for r in (q_ref, k_ref, v_ref, g_ref)),
                                 sm_ref[...], al_ref[...], dt_ref[...], gn_ref[...], states)
        for h in range(N_HEADS):
            st_ref[h] = states[h]
            y_ref[:, _head(h)] = ys[h].astype(BF16)
            s_scr[h] = new_states[h]

    return pl.pallas_call(
        body, out_shape=(_sds((SEQ, GROUP), BF16), _sds((N_HEADS, nc, HEAD, HEAD), F32)), grid=(nc,),
        in_specs=[group(0), group(1), group(2), group(GDN_GATE_GROUP), sm, vec, vec, vec], out_specs=(group(0), state),
        scratch_shapes=[pltpu.VMEM((N_HEADS, HEAD, HEAD), F32)], compiler_params=_params(), name=name,
    )(conv, conv, conv, p, small, a_log, dt_bias, gain)


def gdn_bwd(conv, p, small, a_log, dt_bias, gain, states, dy, *, name):
    group, sm, vec, state, nc = _gdn_specs(True)

    def body(q_ref, k_ref, v_ref, g_ref, sm_ref, al_ref, dt_ref, gn_ref, st_ref, dy_ref,
             dq_ref, dk_ref, dv_ref, dg_ref, dsm_ref, dal_ref, ddt_ref, dgn_ref, ds_scr):
        @pl.when(pl.program_id(0) == 0)
        def _():
            ds_scr[...] = jnp.zeros_like(ds_scr)
            dal_ref[...] = jnp.zeros_like(dal_ref)
            ddt_ref[...] = jnp.zeros_like(ddt_ref)
            dgn_ref[...] = jnp.zeros_like(dgn_ref)

        per_head = tuple(tuple(r[:, _head(h)] for h in range(N_HEADS)) for r in (q_ref, k_ref, v_ref, g_ref))
        _, vjp = jax.vjp(_gdn_fn, *per_head, sm_ref[...], al_ref[...], dt_ref[...], gn_ref[...],
                         tuple(st_ref[h] for h in range(N_HEADS)))
        cts = (tuple(dy_ref[:, _head(h)] for h in range(N_HEADS)), tuple(ds_scr[h] for h in range(N_HEADS)))
        dqs, dks, dvs, dgs, dsm, dal, ddt, dgn, dss = vjp(cts)
        for h in range(N_HEADS):
            dq_ref[:, _head(h)] = dqs[h]
            dk_ref[:, _head(h)] = dks[h]
            dv_ref[:, _head(h)] = dvs[h]
            dg_ref[:, _head(h)] = dgs[h].astype(BF16)
            ds_scr[h] = dss[h]
        dsm_ref[...] = dsm
        dal_ref[...] += dal
        ddt_ref[...] += ddt
        dgn_ref[...] += dgn

    f = _sds((SEQ, GROUP), F32)
    pv = _sds((1, LANES), F32)
    return pl.pallas_call(
        body, out_shape=(f, f, f, _sds((SEQ, GROUP), BF16), _sds((SEQ, LANES), F32), pv, pv, pv), grid=(nc,),
        in_specs=[group(0), group(1), group(2), group(GDN_GATE_GROUP), sm, vec, vec, vec, state, group(1)],
        out_specs=(group(0), group(0), group(0), group(0), sm, vec, vec, vec), scratch_shapes=[pltpu.VMEM((N_HEADS, HEAD, HEAD), F32)],
        compiler_params=_params(), name=name,
    )(conv, conv, conv, p, small, a_log, dt_bias, gain, states, dy)


PACK_ROW_TILE = 1024


def adamw(w, g, m, v, *, name):
    r = w.shape[0]
    tr = _row_tile(r, LANES)

    def body(w_ref, g_ref, m_ref, v_ref, d_ref, nm_ref, nv_ref):
        gg = g_ref[...]
        nm = ADAM_B1 * m_ref[...] + (1.0 - ADAM_B1) * gg
        nv = ADAM_B2 * v_ref[...] + (1.0 - ADAM_B2) * jnp.square(gg)
        m_hat = nm / (1.0 - ADAM_B1 ** ADAM_STEP)
        v_hat = nv / (1.0 - ADAM_B2 ** ADAM_STEP)
        d_ref[...] = -ADAM_LR * (m_hat / (jnp.sqrt(v_hat) + ADAM_EPS) + ADAM_WD * w_ref[...])
        nm_ref[...] = nm
        nv_ref[...] = nv

    blk = pl.BlockSpec((tr, LANES), lambda i: (i, 0))
    o = _sds((r, LANES), F32)
    return pl.pallas_call(body, out_shape=(o, o, o), grid=(r // tr,), in_specs=[blk] * 4, out_specs=(blk, blk, blk),
                          compiler_params=_params(), name=name)(w, g, m, v)


ELEMENTWISE_BLOCK_BYTES = 2 * 1024 * 1024


def _row_tile(r, c):
    best = None
    for tr in range(8, r + 1, 8):
        if r % tr == 0 and tr * c * 4 <= ELEMENTWISE_BLOCK_BYTES:
            best = tr
    if best is None:
        raise ValueError(f"no row tile for ({r}, {c})")
    return best


def _tile_2d(r, c):
    if any(r % tr == 0 for tr in range(8, r + 1, 8)):
        return _row_tile(r, c), c
    tc = max(t for t in range(LANES, c + 1, LANES) if c % t == 0 and r * t * 4 <= ELEMENTWISE_BLOCK_BYTES)
    return r, tc


def _core_index():
    return lax.axis_index("c").astype(jnp.int32).reshape(1)


def _chip_index():
    return (2 * lax.axis_index("x") + lax.axis_index("y")).astype(jnp.int32).reshape(1)


def adamw_halves(w, m, v, g_own, g_sib, *, layer=0, prev=None, name):
    n_layers, rows, c = w.shape
    r = rows // 2
    tr = _row_tile(r, c)
    nb = r // tr

    def body(c_ref, w_ref, m_ref, v_ref, own_ref, sib_ref, *rest):
        g_ref, d_ref, nm_ref, nv_ref = rest[-4:]
        gg = jnp.where(pl.program_id(0) == c_ref[0], own_ref[...], sib_ref[...])
        nm = ADAM_B1 * m_ref[...] + (1.0 - ADAM_B1) * gg
        nv = ADAM_B2 * v_ref[...] + (1.0 - ADAM_B2) * jnp.square(gg)
        m_hat = nm / (1.0 - ADAM_B1 ** ADAM_STEP)
        v_hat = nv / (1.0 - ADAM_B2 ** ADAM_STEP)
        g_ref[...] = gg
        d_ref[...] = -ADAM_LR * (m_hat / (jnp.sqrt(v_hat) + ADAM_EPS) + ADAM_WD * w_ref[...])
        nm_ref[...] = nm
        nv_ref[...] = nv

    full = pl.BlockSpec((None, tr, c), lambda h, i, cr: (layer, h * nb + i, 0))
    half = pl.BlockSpec((tr, c), lambda h, i, cr: (i, 0))
    o = _sds((n_layers, rows, c), F32)
    prev = list(prev or ())
    gs = pltpu.PrefetchScalarGridSpec(num_scalar_prefetch=1, grid=(2, nb), in_specs=[full, full, full, half, half] + [_ANY] * len(prev),
                                      out_specs=(full, full, full, full))
    n_fixed = 6
    return pl.pallas_call(body, out_shape=(o, o, o, o), grid_spec=gs, compiler_params=_params(), name=name,
                          input_output_aliases={n_fixed + k: k for k in range(len(prev))})(
        _core_index(), w, m, v, g_own, g_sib, *prev)


ADAMW_COLUMN_TILE = 256


def adamw_column_halves(w, m, v, g_own, g_sib, *, name):
    _, rows, cols = w.shape
    tc = ADAMW_COLUMN_TILE
    per_half = cols // 2 // tc

    def body(c_ref, w_ref, m_ref, v_ref, own_ref, sib_ref, g_ref, d_ref, nm_ref, nv_ref):
        gg = jnp.where(pl.program_id(0) // per_half == c_ref[0], own_ref[...], sib_ref[...])
        nm = ADAM_B1 * m_ref[...] + (1.0 - ADAM_B1) * gg
        nv = ADAM_B2 * v_ref[...] + (1.0 - ADAM_B2) * jnp.square(gg)
        m_hat = nm / (1.0 - ADAM_B1 ** ADAM_STEP)
        v_hat = nv / (1.0 - ADAM_B2 ** ADAM_STEP)
        g_ref[...] = gg
        d_ref[...] = -ADAM_LR * (m_hat / (jnp.sqrt(v_hat) + ADAM_EPS) + ADAM_WD * w_ref[...])
        nm_ref[...] = nm
        nv_ref[...] = nv

    full = pl.BlockSpec((None, rows, tc), lambda j, cr: (0, 0, j))
    half = pl.BlockSpec((rows, tc), lambda j, cr: (0, j % per_half))
    o = _sds(w.shape, F32)
    gs = pltpu.PrefetchScalarGridSpec(num_scalar_prefetch=1, grid=(cols // tc,), in_specs=[full, full, full, half, half],
                                      out_specs=(full, full, full, full))
    return pl.pallas_call(body, out_shape=(o, o, o, o), grid_spec=gs, compiler_params=_params(), name=name)(
        _core_index(), w, m, v, g_own, g_sib)


def add_core_halves(g2, land, *, out_dtype, name):
    _, ns, r, cols = g2.shape
    tr, tc = _tile_2d(r, cols)

    def body(c_ref, a_ref, b_ref, o_ref):
        o_ref[...] = (a_ref[...] + b_ref[...]).astype(out_dtype)

    gs = pltpu.PrefetchScalarGridSpec(
        num_scalar_prefetch=1, grid=(ns, r // tr, cols // tc),
        in_specs=[pl.BlockSpec((None, None, tr, tc), lambda s, i, j, cr: (cr[0], s, i, j)),
                  pl.BlockSpec((None, tr, tc), lambda s, i, j, cr: (s, i, j))],
        out_specs=pl.BlockSpec((None, tr, tc), lambda s, i, j, cr: (s, i, j)))
    return pl.pallas_call(body, out_shape=_sds((ns, r, cols), out_dtype), grid_spec=gs, compiler_params=_params(), name=name)(
        _core_index(), g2, land)


def sum_over_chips(own, land, *, scatter, name):
    _, r, cols = own.shape
    tr, tc = _tile_2d(r, cols)

    def body(mine_ref, own_ref, l0, l1, l2, l3, o_ref):
        mine = mine_ref[0]
        mine_val = own_ref[...]
        acc = None
        for s, l_ref in enumerate((l0, l1, l2, l3)):
            val = jnp.where(mine == s, mine_val, l_ref[...]).astype(F32)
            acc = val if acc is None else acc + val
        o_ref[...] = acc

    def slot(s):
        return pl.BlockSpec((None, tr, tc), lambda i, j, mr: (jnp.where(mr[0] == s, (s + 1) % N_SHARD, s), i, j))

    own_spec = pl.BlockSpec((None, tr, tc), lambda i, j, mr: (mr[0] if scatter else 0, i, j))
    gs = pltpu.PrefetchScalarGridSpec(num_scalar_prefetch=1, grid=(r // tr, cols // tc), in_specs=[own_spec] + [slot(s) for s in range(N_SHARD)],
                                      out_specs=pl.BlockSpec((tr, tc), lambda i, j, mr: (i, j)))
    return pl.pallas_call(body, out_shape=_sds((r, cols), F32), grid_spec=gs, compiler_params=_params(), name=name)(
        _chip_index(), own, land, land, land, land)


_ANY = pl.BlockSpec(memory_space=pl.ANY)


def xy_exchange(src, *, scatter, name):
    rh = src.shape[1]

    def body(src_ref, land_ref, send_sems, recv_sems, loc_sem):
        x, y, c = lax.axis_index("x"), lax.axis_index("y"), lax.axis_index("c")
        mine = 2 * x + y
        peers = [(1 - x, y), (x, 1 - y), (1 - x, 1 - y)]

        def piece(shard):
            return src_ref.at[shard] if scatter else src_ref.at[c]

        def copy(k, px, py, dst_slot):
            return pltpu.make_async_remote_copy(src_ref=piece(2 * px + py), dst_ref=land_ref.at[dst_slot], send_sem=send_sems.at[k],
                                                recv_sem=recv_sems.at[k], device_id=(px, py, c), device_id_type=MESH)

        keep = pltpu.make_async_copy(piece(mine), land_ref.at[mine], loc_sem)
        keep.start()
        sends = [copy(k, px, py, mine) for k, (px, py) in enumerate(peers)]
        for cp in sends:
            cp.start()
        for cp in sends:
            cp.wait_send()
        for k, (px, py) in enumerate(peers):
            copy(k, px, py, 2 * px + py).wait_recv()
        keep.wait()

    return pl.pallas_call(body, out_shape=_sds((N_SHARD, rh, LANES), src.dtype), in_specs=[_ANY], out_specs=_ANY,
                          scratch_shapes=[pltpu.SemaphoreType.DMA((3,)), pltpu.SemaphoreType.DMA((3,)), pltpu.SemaphoreType.DMA(())],
                          name=name)(src)


def core_exchange(src, *, send_other_half, name):
    def body(src_ref, out_ref, send_sem, recv_sem, loc_sem):
        x, y, c = lax.axis_index("x"), lax.axis_index("y"), lax.axis_index("c")
        if send_other_half:
            cp = pltpu.make_async_remote_copy(src_ref=src_ref.at[1 - c], dst_ref=out_ref, send_sem=send_sem, recv_sem=recv_sem,
                                              device_id=(x, y, 1 - c), device_id_type=MESH)
            cp.start()
            cp.wait_send()
            cp.wait_recv()
        else:
            keep = pltpu.make_async_copy(src_ref, out_ref.at[c], loc_sem)
            keep.start()
            cp = pltpu.make_async_remote_copy(src_ref=src_ref, dst_ref=out_ref.at[c], send_sem=send_sem, recv_sem=recv_sem,
                                              device_id=(x, y, 1 - c), device_id_type=MESH)
            cp.start()
            cp.wait_send()
            pltpu.make_async_remote_copy(src_ref=src_ref, dst_ref=out_ref.at[1 - c], send_sem=send_sem, recv_sem=recv_sem,
                                         device_id=(x, y, 1 - c), device_id_type=MESH).wait_recv()
            keep.wait()

    out_shape = _sds(src.shape[1:], src.dtype) if send_other_half else _sds((2,) + src.shape, src.dtype)
    return pl.pallas_call(body, out_shape=out_shape, in_specs=[_ANY], out_specs=_ANY,
                          scratch_shapes=[pltpu.SemaphoreType.DMA(()), pltpu.SemaphoreType.DMA(()), pltpu.SemaphoreType.DMA(())],
                          name=name)(src)


def _comm_call(body, ins, out_shapes, sem_counts, name):
    return pl.pallas_call(body, out_shape=tuple(out_shapes), in_specs=[_ANY] * len(ins), out_specs=tuple([_ANY] * len(out_shapes)),
                          scratch_shapes=[pltpu.SemaphoreType.DMA((k,)) for k in sem_counts], name=name)(*ins)


def _sequencer_call(body, ins, out_shapes, sem_counts, name, collective_id):
    return pl.kernel(body, out_type=list(out_shapes), mesh=plsc.ScalarSubcoreMesh(axis_name="sequencer", num_cores=1), name=name,
                     scratch_types=[pltpu.SemaphoreType.DMA((k,)) for k in sem_counts],
                     compiler_params=pltpu.CompilerParams(collective_id=collective_id))(*ins)


def _handshake(peers):
    barrier = pltpu.get_barrier_semaphore()
    for peer in peers:
        pl.semaphore_signal(barrier, inc=1, device_id=peer, device_id_type=MESH)
    pl.semaphore_wait(barrier, len(peers))


def _xy_peers(x, y):
    return [(1 - x, y), (x, 1 - y), (1 - x, 1 - y)]


def gather_halves(halves, *, name, collective_id):
    n = len(halves)

    def body(*refs):
        ins, lands, sibs = refs[:n], refs[n:2 * n], refs[2 * n:3 * n]
        ici_send, ici_recv, d2d_send, d2d_recv = refs[3 * n:]
        x, y, c = lax.axis_index("x"), lax.axis_index("y"), lax.axis_index("c")
        mine = 2 * x + y
        peers = _xy_peers(x, y)
        _handshake([(px, py, c) for px, py in peers] + [(x, y, 1 - c)])

        def ici(i, k, slot):
            px, py = peers[k]
            return pltpu.make_async_remote_copy(src_ref=ins[i].at[c], dst_ref=lands[i].at[slot], send_sem=ici_send.at[3 * i + k],
                                                recv_sem=ici_recv.at[3 * i + k], device_id=(px, py, c), device_id_type=MESH)

        def pass_on(i, k):
            px, py = peers[k]
            slot = 2 * px + py
            return pltpu.make_async_remote_copy(src_ref=lands[i].at[slot], dst_ref=sibs[i].at[slot], send_sem=d2d_send.at[3 * i + k],
                                                recv_sem=d2d_recv.at[3 * i + k], device_id=(x, y, 1 - c), device_id_type=MESH)

        sends = [ici(i, k, mine) for i in range(n) for k in range(3)]
        for cp in sends:
            cp.start()
        passed = []
        for i in range(n):
            for k in range(3):
                px, py = peers[k]
                ici(i, k, 2 * px + py).wait_recv()
                cp = pass_on(i, k)
                cp.start()
                passed.append(cp)
        for cp in passed:
            cp.wait_recv()
        for cp in sends + passed:
            cp.wait_send()

    outs = [_sds((N_SHARD,) + h.shape[1:], h.dtype) for h in halves]
    res = _sequencer_call(body, halves, outs + outs, [3 * n] * 4, name, collective_id)
    return res[:n], res[n:]


def send_other_half(arrays, *, name, collective_id):
    n = len(arrays)

    def body(*refs):
        ins, lands = refs[:n], refs[n:2 * n]
        send_sems, recv_sems = refs[2 * n:]
        x, y, c = lax.axis_index("x"), lax.axis_index("y"), lax.axis_index("c")
        _handshake([(x, y, 1 - c)])
        copies = [pltpu.make_async_remote_copy(src_ref=ins[i].at[1 - c], dst_ref=lands[i], send_sem=send_sems.at[i],
                                               recv_sem=recv_sems.at[i], device_id=(x, y, 1 - c), device_id_type=MESH) for i in range(n)]
        for cp in copies:
            cp.start()
        for cp in copies:
            cp.wait_recv()
        for cp in copies:
            cp.wait_send()

    return _sequencer_call(body, arrays, [_sds(a.shape[1:], a.dtype) for a in arrays], [n, n], name, collective_id)


_HBM = pl.BlockSpec(memory_space=pltpu.HBM)
_SEM = pl.BlockSpec(memory_space=pltpu.SEMAPHORE)
_SPLIT_COPY = dict(has_side_effects=pltpu.SideEffectType.DATAFLOW_SIDE_EFFECTING)


def _chip_copy(ins, lands, send_sems, recv_sems, scatter, i, k, receive):
    x, y, c = lax.axis_index("x"), lax.axis_index("y"), lax.axis_index("c")
    px, py = _xy_peers(x, y)[k]
    theirs, mine = 2 * px + py, 2 * x + y
    src = ins[i].at[theirs] if scatter[i] else ins[i].at[0]
    return pltpu.make_async_remote_copy(src_ref=src, dst_ref=lands[i].at[theirs if receive else mine], send_sem=send_sems.at[3 * i + k],
                                        recv_sem=recv_sems.at[3 * i + k], device_id=(px, py, c), device_id_type=MESH)


def send_to_chips_start(arrays, scatter, *, name):
    n = len(arrays)

    def body(*refs):
        send_sems, recv_sems = refs[2 * n], refs[2 * n + 1]
        ins, lands = refs[2 * n + 2:3 * n + 2], refs[3 * n + 2:4 * n + 2]
        token = refs[4 * n + 2]
        for i in range(n):
            for k in range(3):
                _chip_copy(ins, lands, send_sems, recv_sems, scatter, i, k, receive=False).start()
        token[...] = jnp.zeros_like(token)

    land_shapes = [(N_SHARD,) + a.shape[1:] for a in arrays]
    operands = [pltpu.with_memory_space_constraint(a, pltpu.HBM) for a in arrays]
    operands += [pltpu.with_memory_space_constraint(lax.empty(s, a.dtype), pltpu.HBM) for s, a in zip(land_shapes, arrays)]
    out_shape = ([pltpu.SemaphoreType.DMA((3 * n,)), pltpu.SemaphoreType.DMA((3 * n,))] + [pltpu.HBM(a.shape, a.dtype) for a in arrays]
                 + [pltpu.HBM(s, a.dtype) for s, a in zip(land_shapes, arrays)] + [_sds((8, LANES), F32)])
    res = pl.pallas_call(body, name=name, out_shape=out_shape, in_specs=[_HBM] * (2 * n),
                         out_specs=[_SEM, _SEM] + [_HBM] * (2 * n) + [pl.BlockSpec(memory_space=pltpu.VMEM)],
                         input_output_aliases={i: 2 + i for i in range(2 * n)}, compiler_params=pltpu.CompilerParams(**_SPLIT_COPY))(*operands)
    return (res[0], res[1], res[2:2 + n], res[2 + n:2 + 2 * n], scatter), res[-1]


def send_to_chips_wait(state, after, *, name):
    send_sems, recv_sems, arrays, lands, scatter = state
    n = len(arrays)

    def body(*refs):
        ins, landing = refs[:n], refs[n:2 * n]
        send_sems, recv_sems = refs[2 * n], refs[2 * n + 1]
        for i in range(n):
            for k in range(3):
                _chip_copy(ins, landing, send_sems, recv_sems, scatter, i, k, receive=True).wait_recv()
        for i in range(n):
            for k in range(3):
                _chip_copy(ins, landing, send_sems, recv_sems, scatter, i, k, receive=False).wait_send()

    out_shape = [pltpu.HBM(a.shape, a.dtype) for a in list(arrays) + list(lands)]
    res = pl.pallas_call(body, name=name, out_shape=out_shape, in_specs=[_HBM] * (2 * n) + [_SEM, _SEM] + [_ANY] * len(after),
                         out_specs=[_HBM] * (2 * n), input_output_aliases={i: i for i in range(2 * n)},
                         compiler_params=pltpu.CompilerParams(**_SPLIT_COPY))(*arrays, *lands, send_sems, recv_sems, *after)
    return res[:n], res[n:]


def swap_with_other_core(arrays, *, name, collective_id):
    n = len(arrays)

    def body(*refs):
        ins, lands = refs[:n], refs[n:2 * n]
        send_sems, recv_sems = refs[2 * n:]
        x, y, c = lax.axis_index("x"), lax.axis_index("y"), lax.axis_index("c")
        _handshake([(x, y, 1 - c)])
        copies = [pltpu.make_async_remote_copy(src_ref=ins[i], dst_ref=lands[i], send_sem=send_sems.at[i], recv_sem=recv_sems.at[i],
                                               device_id=(x, y, 1 - c), device_id_type=MESH) for i in range(n)]
        for cp in copies:
            cp.start()
        for cp in copies:
            cp.wait_recv()
        for cp in copies:
            cp.wait_send()

    return _sequencer_call(body, arrays, [_sds(a.shape, a.dtype) for a in arrays], [n, n], name, collective_id)


def _pack_rows(n_elems, row_multiple):
    rows = -(-n_elems // LANES)
    return -(-rows // row_multiple) * row_multiple


def _pack(arrays, rows, dtype):
    flat = jnp.concatenate([a.reshape(-1).astype(dtype) for a in arrays])
    return jnp.pad(flat, (0, rows * LANES - flat.shape[0])).reshape(rows, LANES)


def _unpack(packed, shapes):
    flat = packed.reshape(-1)
    out, off = [], 0
    for s in shapes:
        n = int(np.prod(s))
        out.append(flat[off:off + n].reshape(s))
        off += n
    return out


def all_gather_shards(shards, axes, dtype, row_multiple, tag):
    shapes = [s.shape for s in shards]
    rows = _pack_rows(sum(int(np.prod(s)) for s in shapes), row_multiple)
    packed = _pack(shards, rows, dtype).reshape(2, rows // 2, LANES)
    land = xy_exchange(packed, scatter=False, name=f"gather_xy_{tag}")
    both = core_exchange(land, send_other_half=False, name=f"gather_c_{tag}")
    per_shard = jnp.swapaxes(both, 0, 1).reshape(N_SHARD, rows, LANES)
    pieces = [_unpack(per_shard[s], shapes) for s in range(N_SHARD)]
    return [jnp.concatenate([pieces[s][i] for s in range(N_SHARD)], axis=ax) for i, ax in enumerate(axes)]


def _ordered_before(first, then):
    if then is None:
        return first, None
    return lax.optimization_barrier((first, then))


def reduce_between_cores(arrays, scatter, *, tag, collective_id, before=None):
    arrays, before = _ordered_before(arrays, before)
    land = send_other_half(arrays, name=f"reduce_core_send_{tag}", collective_id=collective_id)
    return (arrays, land, scatter, tag, collective_id), before


def reduce_between_chips(state, before=None):
    arrays, land, scatter, tag, collective_id = state
    chip = [add_core_halves(a, l, out_dtype=BF16 if sc else F32, name=f"reduce_core_add_{tag}_{i}")
            for i, (a, l, sc) in enumerate(zip(arrays, land, scatter))]
    sending, token = send_to_chips_start(chip, scatter, name=f"reduce_chip_start_{tag}")
    token, before = _ordered_before(token, before)
    return (sending, token, scatter, tag, collective_id), before


def reduce_finish(state, after):
    sending, token, scatter, tag, collective_id = state
    chip, land = send_to_chips_wait(sending, tuple(after) + (token,), name=f"reduce_chip_wait_{tag}")
    own = [sum_over_chips(ch, l, scatter=sc, name=f"reduce_chip_add_{tag}_{i}") for i, (ch, l, sc) in enumerate(zip(chip, land, scatter))]
    sib = swap_with_other_core(own, name=f"reduce_core_swap_{tag}", collective_id=collective_id + 2)
    return own, sib


def _ffn_layer_fwd(h, norm_g, w_up, cw, cb, w_down, tag):
    hn = norm_fwd(h, norm_g, name=f"ffn_norm_{tag}")
    u = matmul(hn, w_up, name=f"ffn_up_{tag}")
    act = ffn_act_fwd(u, cw, cb, name=f"ffn_act_{tag}")
    out = matmul(act, w_down, add=h, name=f"ffn_down_{tag}")
    return out, (h, hn, u, act)


def _travel_layout(array):
    return BIG_ARRAYS[array][3], BIG_ARRAYS[array][4]


def _ffn_layer_bwd(saved, dout, norm_g, w_up, cw, cb, w_down, tag, d_w_down_other=None):
    h, hn, u, act = saved
    dact = matmul(dout, w_down, tb=True, name=f"ffn_down_dx_{tag}")
    d_w_down = matmul(act, dout, ta=True, layer=(int(tag), 2, d_w_down_other), name=f"ffn_down_dw_{tag}")
    dug, duv, dcw, dcb = ffn_act_bwd(u, cw, cb, dact, name=f"ffn_act_bwd_{tag}")
    du = jnp.concatenate([dug, duv], axis=1)
    dhn = matmul(du, w_up, tb=True, name=f"ffn_up_dx_{tag}")
    d_w_up = matmul(hn, du, ta=True, split=_travel_layout(f"ffn_w_up_{tag}"), name=f"ffn_up_dw_{tag}")
    dh, dg = norm_bwd(h, norm_g, dhn, dout, name=f"ffn_norm_bwd_{tag}")
    return dh, dg, d_w_up, dcw, dcb, d_w_down


def local_step(x, target, w, stage=lambda name, tensors, grads=None: tensors):
    g = {}
    tables = _ret_tables()
    x = stage("start", x)
    w_in_t = w["ret_gdn_w_in"]
    w_main = w_in_t[:MIX_MAIN]
    w_small = jnp.pad(w_in_t[MIX_MAIN:], ((0, LANES - 2 * N_HEADS), (0, 0)))
    a_log = jnp.pad(w["gdn_a_log"], ((0, 0), (0, LANES - N_HEADS)))
    dt_bias = jnp.pad(w["gdn_dt_bias"], ((0, 0), (0, LANES - N_HEADS)))

    hn0 = stage("normed", norm_fwd(x, w["norm_mix"][0:1], name="mix0_norm"))
    p = matmul(hn0, w_main, tb=True, name="mix0_in")
    small = matmul(hn0, w_small, tb=True, name="mix0_in_small")
    y_ret, s_ret = ret_fwd(p, tables, name="ret_fwd")
    conv = gdn_conv_fwd(p, w["gdn_conv_w"], name="gdn_conv")
    y_gdn, s_gdn = gdn_fwd(conv, p, small, a_log, dt_bias, w["gdn_out_gain"], name="gdn_fwd")
    y0 = stage("mixed", jnp.concatenate([y_ret, y_gdn], axis=1))
    h1 = matmul(y0, w["ret_gdn_w_out"], add=x, name="mix0_out")
    h2, ffn0 = _ffn_layer_fwd(h1, w["norm_ffn"][0:1], w["ffn_w_up"][0], w["ffn_conv_w"][0], w["ffn_conv_b"][0:1], w["ffn_w_down"][0], "0")
    h2 = stage("layer0", h2)

    hn1 = norm_fwd(h2, w["norm_mix"][1:2], name="mix1_norm")
    gx = matmul(hn1, w["lru_w_in"], name="mix1_in")
    lru_p = (w["lru_conv_w"], w["lru_conv_b"], w["lru_w_a"], w["lru_b_a"], w["lru_w_x"], w["lru_b_x"], w["lru_lambda"])
    y1 = lru_fwd(gx, *lru_p, name="lru_fwd")
    h3 = matmul(y1, w["lru_w_out"], add=h2, name="mix1_out")
    h4, ffn1 = _ffn_layer_fwd(h3, w["norm_ffn"][1:2], w["ffn_w_up"][1], w["ffn_conv_w"][1], w["ffn_conv_b"][1:2], w["ffn_w_down"][1], "1")

    loss, dh4, g["norm_final"] = final_fwd_bwd(h4, w["norm_final"], target, name="final")

    dh3, dgf1, dwu1, dcw1, dcb1, dwd1 = _ffn_layer_bwd(ffn1, dh4, w["norm_ffn"][1:2], w["ffn_w_up"][1], w["ffn_conv_w"][1],
                                                     w["ffn_conv_b"][1:2], w["ffn_w_down"][1], "1")
    g["ffn_w_up_1"] = dwu1
    dh3 = stage("grads0_ready", dh3, g)
    dy1 = matmul(dh3, w["lru_w_out"], tb=True, name="mix1_out_dx")
    g["lru_w_out"] = matmul(y1, dh3, ta=True, split=_travel_layout("lru_w_out"), name="mix1_out_dw")
    dgate, dxr, g["lru_conv_w"], g["lru_conv_b"], g["lru_w_a"], g["lru_b_a"], g["lru_w_x"], g["lru_b_x"], g["lru_lambda"] = lru_bwd(
        gx, *lru_p, dy1, name="lru_bwd")
    dgx = stage("grads0_send", jnp.concatenate([dgate, dxr], axis=1), g)
    dhn1 = matmul(dgx, w["lru_w_in"], tb=True, name="mix1_in_dx")
    g["lru_w_in"] = matmul(hn1, dgx, ta=True, split=_travel_layout("lru_w_in"), name="mix1_in_dw")
    dh2, dgm1 = norm_bwd(h2, w["norm_mix"][1:2], dhn1, dh3, name="mix1_norm_bwd")
    dh2 = stage("grads1_ready", dh2, g)

    dh1, dgf0, dwu0, dcw0, dcb0, dwd0 = _ffn_layer_bwd(ffn0, dh2, w["norm_ffn"][0:1], w["ffn_w_up"][0], w["ffn_conv_w"][0],
                                                     w["ffn_conv_b"][0:1], w["ffn_w_down"][0], "0", dwd1)
    g["ffn_w_up_0"] = dwu0
    g["ffn_w_down"] = dwd0
    dh1 = stage("grads2_ready", stage("grads1_send", dh1, g), g)
    dy0 = matmul(dh1, w["ret_gdn_w_out"], tb=True, name="mix0_out_dx")
    g["ret_gdn_w_out"] = matmul(y0, dh1, ta=True, split=_travel_layout("ret_gdn_w_out"), name="mix0_out_dw")
    dq_r, dk_r, dv_r, dg_r = ret_bwd(p, tables, s_ret, dy0, name="ret_bwd")
    dy0, dq_r = stage("grads2_send", (dy0, dq_r), g)
    dcq, dck, dcv, dg_d, dsmall, dal, ddt, dgain = gdn_bwd(conv, p, small, a_log, dt_bias, w["gdn_out_gain"], s_gdn, dy0, name="gdn_bwd")
    dconv = jnp.concatenate([dcq, dck, dcv], axis=1)
    dp_conv, g["gdn_conv_w"] = gdn_conv_bwd(p, w["gdn_conv_w"], dconv, name="gdn_conv_bwd")
    dp = jnp.concatenate([dq_r, dk_r, dv_r, dg_r, dp_conv, dg_d], axis=1)
    dhn0 = matmul(dp, w_main, name="mix0_in_dx")
    dhn0 = matmul(dsmall, w_small, add=dhn0, name="mix0_in_small_dx")
    d_w_main = matmul(dp, hn0, ta=True, name="mix0_in_dw")
    d_w_small = matmul(dsmall, hn0, ta=True, name="mix0_in_small_dw")
    g["ret_gdn_w_in"] = jnp.concatenate([d_w_main, d_w_small[:2 * N_HEADS]], axis=0)
    dx, dgm0 = norm_bwd(x, w["norm_mix"][0:1], dhn0, dh1, name="mix0_norm_bwd")

    g["gdn_a_log"] = dal[:, :N_HEADS]
    g["gdn_dt_bias"] = ddt[:, :N_HEADS]
    g["gdn_out_gain"] = dgain
    g["norm_mix"] = jnp.concatenate([dgm0, dgm1], axis=0)
    g["norm_ffn"] = jnp.concatenate([dgf0, dgf1], axis=0)
    g["ffn_conv_w"] = jnp.stack([dcw0, dcw1])
    g["ffn_conv_b"] = jnp.concatenate([dcb0, dcb1], axis=0)
    return loss, dx, g


WEIGHTS = ("norm_mix", "norm_ffn", "ret_gdn_w_in", "gdn_conv_w", "gdn_a_log", "gdn_dt_bias", "gdn_out_gain", "ret_gdn_w_out",
           "lru_w_in", "lru_conv_w", "lru_conv_b", "lru_w_a", "lru_b_a", "lru_w_x", "lru_b_x", "lru_lambda", "lru_w_out",
           "ffn_w_up", "ffn_conv_w", "ffn_conv_b", "ffn_w_down", "norm_final")
MATMUL_SHARDED = {"ret_gdn_w_in": 1, "ret_gdn_w_out": 0, "lru_w_in": 1, "lru_w_out": 0, "ffn_w_up": 2, "ffn_w_down": 1}
VECTOR_SHARDED = {"gdn_conv_w": 1, "lru_conv_w": 1, "lru_conv_b": 1, "lru_b_a": 1, "lru_b_x": 1, "lru_lambda": 1, "ffn_conv_w": 2}
SHARDED = {**MATMUL_SHARDED, **VECTOR_SHARDED}
REPLICATED = tuple(n for n in WEIGHTS if n not in SHARDED)
SQUEEZE = {"ret_gdn_w_in", "gdn_conv_w", "ret_gdn_w_out", "lru_w_in", "lru_conv_w", "lru_w_a", "lru_w_x", "lru_w_out"}
MIX_IN = MIX_MAIN + 2 * N_HEADS
BIG_ARRAYS = {
    "ret_gdn_w_in": ("ret_gdn_w_in", None, (MIX_IN, D_MODEL), (N_SHARD, MIX_IN // N_SHARD, 2, D_MODEL // 2), (2, 0, 1, 3)),
    "ret_gdn_w_out": ("ret_gdn_w_out", None, (2 * GROUP, D_MODEL), (N_SHARD, 2, GROUP // N_SHARD, D_MODEL), (1, 0, 2, 3)),
    "lru_w_in": ("lru_w_in", None, (D_MODEL, 2 * D_MODEL), (2, D_MODEL // 2, N_SHARD, 2 * D_MODEL // N_SHARD), (0, 2, 1, 3)),
    "lru_w_out": ("lru_w_out", None, (D_MODEL, D_MODEL), (N_SHARD, 2, D_MODEL // (2 * N_SHARD), D_MODEL), (1, 0, 2, 3)),
    "ffn_w_up_0": ("ffn_w_up", 0, (D_MODEL, 2 * D_FF), (2, D_MODEL // 2, N_SHARD, 2 * D_FF // N_SHARD), (0, 2, 1, 3)),
    "ffn_w_up_1": ("ffn_w_up", 1, (D_MODEL, 2 * D_FF), (2, D_MODEL // 2, N_SHARD, 2 * D_FF // N_SHARD), (0, 2, 1, 3)),
    "ffn_w_down": ("ffn_w_down", None, (2, D_FF, D_MODEL), (2, N_SHARD, D_FF // N_SHARD, D_MODEL), (0, 1, 2, 3)),
}
GATHER_GROUPS = (("ret_gdn_w_in",), ("ret_gdn_w_out", "ffn_w_up_0", "ffn_w_down"), ("lru_w_in", "lru_w_out", "ffn_w_up_1"))
REDUCE_GROUPS = (("ffn_w_up_1",), ("lru_w_in", "lru_w_out"), ("ffn_w_up_0", "ffn_w_down"), ("ret_gdn_w_out", "ret_gdn_w_in"))
BLOCK_WEIGHTS = ("lru_w_a", "lru_w_x")
GATHER_COLLECTIVE_ID = 1
REDUCE_COLLECTIVE_ID = GATHER_COLLECTIVE_ID + len(GATHER_GROUPS)


TRANSPOSED = ("ret_gdn_w_in",)


def _shard_of(array, tensors):
    weight, layer = BIG_ARRAYS[array][:2]
    t = tensors[weight]
    if weight in TRANSPOSED:
        return jnp.swapaxes(t, 1, 2)[0]
    return _local_view(weight, t) if layer is None else t[layer]


def _core_halves(array, shard):
    _, _, _, split, perm = BIG_ARRAYS[array]
    kept = [k for k in range(4) if k != perm[1]]
    order = [kept.index(perm[0]), kept.index(perm[2]), kept.index(perm[3])]
    return shard.reshape([split[k] for k in kept]).transpose(order)


def _local_view(name, a):
    if name in SQUEEZE:
        return a[0]
    if a.ndim == 1:
        return a[None, :]
    return a


def kernel(x, norm_mix, norm_ffn, ret_gdn_w_in, gdn_conv_w, gdn_a_log, gdn_dt_bias, gdn_out_gain, ret_gdn_w_out, lru_w_in, lru_conv_w, lru_conv_b, lru_w_a, lru_b_a, lru_w_x, lru_b_x, lru_lambda, lru_w_out, ffn_w_up, ffn_conv_w, ffn_conv_b, ffn_w_down, norm_final, loss_target, m_norm_mix, m_norm_ffn, m_ret_gdn_w_in, m_gdn_conv_w, m_gdn_a_log, m_gdn_dt_bias, m_gdn_out_gain, m_ret_gdn_w_out, m_lru_w_in, m_lru_conv_w, m_lru_conv_b, m_lru_w_a, m_lru_b_a, m_lru_w_x, m_lru_b_x, m_lru_lambda, m_lru_w_out, m_ffn_w_up, m_ffn_conv_w, m_ffn_conv_b, m_ffn_w_down, m_norm_final, v_norm_mix, v_norm_ffn, v_ret_gdn_w_in, v_gdn_conv_w, v_gdn_a_log, v_gdn_dt_bias, v_gdn_out_gain, v_ret_gdn_w_out, v_lru_w_in, v_lru_conv_w, v_lru_conv_b, v_lru_w_a, v_lru_b_a, v_lru_w_x, v_lru_b_x, v_lru_lambda, v_lru_w_out, v_ffn_w_up, v_ffn_conv_w, v_ffn_conv_b, v_ffn_w_down, v_norm_final):
    given = dict(norm_mix=norm_mix, norm_ffn=norm_ffn, ret_gdn_w_in=ret_gdn_w_in, gdn_conv_w=gdn_conv_w, gdn_a_log=gdn_a_log, gdn_dt_bias=gdn_dt_bias, gdn_out_gain=gdn_out_gain, ret_gdn_w_out=ret_gdn_w_out, lru_w_in=lru_w_in, lru_conv_w=lru_conv_w, lru_conv_b=lru_conv_b, lru_w_a=lru_w_a, lru_b_a=lru_b_a, lru_w_x=lru_w_x, lru_b_x=lru_b_x, lru_lambda=lru_lambda, lru_w_out=lru_w_out, ffn_w_up=ffn_w_up, ffn_conv_w=ffn_conv_w, ffn_conv_b=ffn_conv_b, ffn_w_down=ffn_w_down, norm_final=norm_final)
    mom1 = dict(norm_mix=m_norm_mix, norm_ffn=m_norm_ffn, ret_gdn_w_in=m_ret_gdn_w_in, gdn_conv_w=m_gdn_conv_w, gdn_a_log=m_gdn_a_log, gdn_dt_bias=m_gdn_dt_bias, gdn_out_gain=m_gdn_out_gain, ret_gdn_w_out=m_ret_gdn_w_out, lru_w_in=m_lru_w_in, lru_conv_w=m_lru_conv_w, lru_conv_b=m_lru_conv_b, lru_w_a=m_lru_w_a, lru_b_a=m_lru_b_a, lru_w_x=m_lru_w_x, lru_b_x=m_lru_b_x, lru_lambda=m_lru_lambda, lru_w_out=m_lru_w_out, ffn_w_up=m_ffn_w_up, ffn_conv_w=m_ffn_conv_w, ffn_conv_b=m_ffn_conv_b, ffn_w_down=m_ffn_w_down, norm_final=m_norm_final)
    mom2 = dict(norm_mix=v_norm_mix, norm_ffn=v_norm_ffn, ret_gdn_w_in=v_ret_gdn_w_in, gdn_conv_w=v_gdn_conv_w, gdn_a_log=v_gdn_a_log, gdn_dt_bias=v_gdn_dt_bias, gdn_out_gain=v_gdn_out_gain, ret_gdn_w_out=v_ret_gdn_w_out, lru_w_in=v_lru_w_in, lru_conv_w=v_lru_conv_w, lru_conv_b=v_lru_conv_b, lru_w_a=v_lru_w_a, lru_b_a=v_lru_b_a, lru_w_x=v_lru_w_x, lru_b_x=v_lru_b_x, lru_lambda=v_lru_lambda, lru_w_out=v_lru_w_out, ffn_w_up=v_ffn_w_up, ffn_conv_w=v_ffn_conv_w, ffn_conv_b=v_ffn_conv_b, ffn_w_down=v_ffn_w_down, norm_final=v_norm_final)

    local = {n: _local_view(n, a) for n, a in given.items()}

    core = lax.axis_index("c")
    chip = 2 * lax.axis_index("x") + lax.axis_index("y")
    is_my_chip = lax.broadcasted_iota(jnp.int32, (N_SHARD, 1, 1), 0) == chip

    def by_core(mine, other):
        return jnp.where(core == 0, jnp.stack([mine, other]), jnp.stack([other, mine]))

    vec_names, rp_names = list(VECTOR_SHARDED), list(REPLICATED)
    full = dict(zip(vec_names, all_gather_shards([local[n] for n in vec_names], [SHARDED[n] for n in vec_names], F32, 32, "p")))
    for n in rp_names:
        full[n] = local[n]
    in_flight = {}

    def launch(gi, after=None):
        halves = []
        for a in GATHER_GROUPS[gi]:
            halves.append(_core_halves(a, _shard_of(a, given).astype(BF16)))
        if after is not None:
            halves, after = lax.optimization_barrier((halves, after))
        in_flight[gi] = (halves,) + gather_halves(halves, name=f"gather_weights_{gi}", collective_id=GATHER_COLLECTIVE_ID + gi)
        return after

    def land(gi, after):
        halves, lands, sibs = in_flight[gi]
        (lands, sibs), after = lax.optimization_barrier(((lands, sibs), after))
        for a, mine, got, passed in zip(GATHER_GROUPS[gi], halves, lands, sibs):
            weight, layer, full_shape, split, perm = BIG_ARRAYS[a]
            half_mine = jnp.where(is_my_chip, jnp.where(core == 0, mine[0], mine[1])[None], got)
            half_other = jnp.where(is_my_chip, jnp.where(core == 0, mine[1], mine[0])[None], passed)
            value = by_core(half_mine, half_other).transpose(tuple(np.argsort(perm))).reshape(full_shape)
            if layer is None:
                full[weight] = value
            else:
                full.setdefault(weight, [None, None])[layer] = value
        return after

    reducing = {}

    def reduce_ready(gi, grads, then=None, extra=()):
        def travelling(a):
            split, perm = _travel_layout(a)
            return grads[a] if grads[a].ndim == 4 else grads[a].reshape(split).transpose(perm)

        arrays = [travelling(a) for a in REDUCE_GROUPS[gi]] + list(extra)
        scatter = [True] * len(REDUCE_GROUPS[gi]) + [False] * len(extra)
        reducing[gi], then = reduce_between_cores(arrays, scatter, tag=str(gi), collective_id=REDUCE_COLLECTIVE_ID + 3 * gi, before=then)
        return then

    def reduce_send(gi, then=None):
        reducing[gi], then = reduce_between_chips(reducing[gi], before=then)
        return then

    def stage(name, tensors, grads=None):
        if name == "start":
            launch(0)
            launch(1)
            packed["wmv"], tensors = lax.optimization_barrier((packed["wmv"], tensors))
            return land(0, tensors)
        if name == "normed":
            return launch(2, tensors)
        if name in ("mixed", "layer0"):
            return land({"mixed": 1, "layer0": 2}[name], tensors)
        gi = int(name[len("grads")])
        return reduce_ready(gi, grads, tensors) if name.endswith("_ready") else reduce_send(gi, tensors)

    small_names = [n for n in rp_names if n not in BLOCK_WEIGHTS] + vec_names
    loc_shapes = [local[n].shape for n in small_names]
    loc_rows = _pack_rows(sum(int(np.prod(s)) for s in loc_shapes), 256)
    packed = {"wmv": [_pack([src[n] for n in small_names], loc_rows, F32) for src in (given, mom1, mom2)]}

    loss_part, dx, grads = local_step(x[0], loss_target[0], full, stage)
    loss = lax.psum(loss_part[0, 0], ("x", "y", "c"))

    small_shapes = [grads[n].shape for n in small_names]
    small_rows = _pack_rows(sum(int(np.prod(s)) for s in small_shapes), 16)
    small = _pack([grads[n] for n in small_names], small_rows, F32).reshape(2, 1, small_rows // 2, LANES)
    last = len(REDUCE_GROUPS) - 1
    halves_of_blocks = [grads[n].reshape(2, 1, LRU_BLOCKS * HEAD // 2, HEAD) for n in BLOCK_WEIGHTS]
    reduce_ready(last, grads, extra=[small] + halves_of_blocks)
    reduce_send(last)
    reduced, result = {}, {}

    def finish(gi, after):
        g_own, g_sib = reduce_finish(reducing[gi], after)
        reduced.update(zip(list(REDUCE_GROUPS[gi]) + ["small"] + list(BLOCK_WEIGHTS), zip(g_own, g_sib)))

    def update(n):
        if n in TRANSPOSED:
            w3, m3, v3 = (jnp.swapaxes(t, 1, 2) for t in (given[n], mom1[n], mom2[n]))
            result[n] = tuple(jnp.swapaxes(t, 1, 2) for t in adamw_column_halves(w3, m3, v3, *reduced[n], name=f"adamw_{n}"))
            return
        done = None
        for a in (k for k, spec in BIG_ARRAYS.items() if spec[0] == n):
            r, cols = reduced[a][0].shape
            layer = BIG_ARRAYS[a][1] or 0
            w3, m3, v3 = (t if BIG_ARRAYS[a][1] is not None else t.reshape(1, 2 * r, cols) for t in (given[n], mom1[n], mom2[n]))
            done = adamw_halves(w3, m3, v3, *reduced[a], layer=layer, prev=done, name=f"adamw_{a}")
        result[n] = done

    for gi in range(last):
        finish(gi, (dx, reducing[last][1]))
    late = {BIG_ARRAYS[a][0] for a in REDUCE_GROUPS[last]}
    for n in MATMUL_SHARDED:
        if n not in late:
            update(n)
    finish(last, tuple(result[n][0] for n in MATMUL_SHARDED if n not in late))
    for n in MATMUL_SHARDED:
        if n in late:
            update(n)

    for n in BLOCK_WEIGHTS:
        w3, m3, v3 = (t.reshape(1, LRU_BLOCKS * HEAD, HEAD) for t in (given[n], mom1[n], mom2[n]))
        result[n] = adamw_halves(w3, m3, v3, *reduced[n], name=f"adamw_{n}")

    g_small = dict(zip(small_names, _unpack(by_core(*reduced["small"]).reshape(small_rows, LANES), small_shapes)))
    for n in vec_names:
        size = local[n].shape[SHARDED[n]]
        g_small[n] = lax.dynamic_slice_in_dim(g_small[n], chip * size, size, axis=SHARDED[n])
    w_pack, m_pack, v_pack = packed["wmv"]
    d_s, m_s, v_s = adamw(w_pack, _pack([g_small[n] for n in small_names], loc_rows, F32), m_pack, v_pack, name="adamw_small")
    for n, d, nm, nv in zip(small_names, _unpack(d_s, loc_shapes), _unpack(m_s, loc_shapes), _unpack(v_s, loc_shapes)):
        result[n] = (g_small[n], d, nm, nv)

    outs = [[result[n][k].reshape(given[n].shape) for n in WEIGHTS] for k in range(4)]
    return (loss, dx[None], *outs[0], *outs[1], *outs[2], *outs[3])
```

```python
import functools

import numpy as np
import jax
import jax.numpy as jnp
from jax import lax
from jax.experimental import pallas as pl
from jax.experimental.pallas import tpu as pltpu
from jax.experimental.pallas import tpu_sc as plsc

F32 = jnp.float32
BF16 = jnp.bfloat16
HI = lax.Precision.HIGHEST
MESH = pl.DeviceIdType.MESH

SEQ = 2048
D_MODEL = 1024
N_HEADS = 4
HEAD = 128
RET_CHUNK = 128
GDN_CHUNK = 64
GDN_CHUNKS_PER_STEP = 4
GROUP = N_HEADS * HEAD
MIX_MAIN = 8 * GROUP
D_FF = 2816
LRU_BLOCKS = 8
LRU_C = 8.0
ROPE_BASE = 10000.0
EPS = 1e-6
N_SHARD = 4
LANES = 128

ADAM_LR, ADAM_B1, ADAM_B2, ADAM_EPS, ADAM_WD, ADAM_STEP = 0.001, 0.9, 0.999, 1e-08, 0.01, 10

VMEM_LIMIT_BYTES = 56 * 1024 * 1024

_roll = pltpu.roll


def _params(**kw):
    return pltpu.CompilerParams(vmem_limit_bytes=VMEM_LIMIT_BYTES, **kw)


def _sds(shape, dtype):
    return jax.ShapeDtypeStruct(tuple(shape), dtype)


def _shift_raw(x, d):
    n = x.shape[0]
    t = lax.broadcasted_iota(jnp.int32, x.shape, 0)
    if d > 0:
        return jnp.where(t >= d, _roll(x, d, 0), 0.0)
    return jnp.where(t < n + d, _roll(x, n + d, 0), 0.0)


@functools.partial(jax.custom_vjp, nondiff_argnums=(1,))
def shift_rows(x, d):
    return _shift_raw(x, d)


def _shift_fwd(x, d):
    return _shift_raw(x, d), None


def _shift_bwd(d, _, g):
    return (_shift_raw(g, -d),)


shift_rows.defvjp(_shift_fwd, _shift_bwd)


@jax.custom_vjp
def swap_halves(x):
    return _roll(x, HEAD // 2, 1)


def _swap_fwd(x):
    return _roll(x, HEAD // 2, 1), None


def _swap_bwd(_, g):
    return (_roll(g, HEAD // 2, 1),)


swap_halves.defvjp(_swap_fwd, _swap_bwd)


def _scan_raw(a, u, reverse):
    n = a.shape[0]
    t = lax.broadcasted_iota(jnp.int32, a.shape, 0)
    d = 1
    while d < n:
        if reverse:
            m = t < n - d
            a_s, u_s = _roll(a, n - d, 0), _roll(u, n - d, 0)
        else:
            m = t >= d
            a_s, u_s = _roll(a, d, 0), _roll(u, d, 0)
        u = a * jnp.where(m, u_s, 0.0) + u
        a = a * jnp.where(m, a_s, 1.0)
        d *= 2
    return u


@jax.custom_vjp
def lin_scan(a, u):
    return _scan_raw(a, u, False)


def _lin_scan_fwd(a, u):
    hs = _scan_raw(a, u, False)
    return hs, (a, hs)


def _lin_scan_bwd(res, g):
    a, hs = res
    lam = _scan_raw(_shift_raw(a, -1), g, True)
    return lam * _shift_raw(hs, 1), lam


lin_scan.defvjp(_lin_scan_fwd, _lin_scan_bwd)


def _bdot(a, b, dims=(((1,), (0,)), ((), ()))):
    return lax.dot_general(a.astype(BF16), b.astype(BF16), dims, preferred_element_type=F32)


def _each(f, *seqs):
    return tuple(f(*a) for a in zip(*seqs))


def _split_bf16(a):
    hi = a.astype(BF16)
    return hi, (a - hi.astype(F32)).astype(BF16)


def _dot3_raw(a_s, b_s):
    a_hl = _each(_split_bf16, a_s)
    b_hl = _each(_split_bf16, b_s)
    hh = _each(lambda a, b: _bdot(a[0], b[0]), a_hl, b_hl)
    hl = _each(lambda a, b: _bdot(a[0], b[1]), a_hl, b_hl)
    lh = _each(lambda a, b: _bdot(a[1], b[0]), a_hl, b_hl)
    return _each(lambda x, y, z: x + (y + z), hh, hl, lh)


@jax.custom_vjp
def dot3(a_s, b_s):
    return _dot3_raw(a_s, b_s)


def _dot3_fwd(a_s, b_s):
    return _dot3_raw(a_s, b_s), (a_s, b_s)


def _dot3_bwd(res, g_s):
    a_s, b_s = res
    return (_each(lambda g, b: _bdot(g, b, (((1,), (1,)), ((), ()))), g_s, b_s),
            _each(lambda a, g: _bdot(a, g, (((0,), (0,)), ((), ()))), a_s, g_s))


dot3.defvjp(_dot3_fwd, _dot3_bwd)


def _eye(n):
    i = lax.broadcasted_iota(jnp.int32, (n, n), 0)
    j = lax.broadcasted_iota(jnp.int32, (n, n), 1)
    return (i == j).astype(F32)


def _unit_lower_inverse_raw(lmats):
    n = lmats[0].shape[0]
    eye = _eye(n)
    ps = _each(lambda l: -l, lmats)
    invs = _each(lambda x: eye + x, ps)
    k = 1
    while 2 * k < n:
        ps = _each(lambda p: _bdot(p, p), ps)
        invs = _each(lambda inv, p: inv + _bdot(inv, p), invs, ps)
        k *= 2
    prods = _dot3_raw(lmats, invs)
    resids = _each(lambda inv, pr: eye - inv - pr, invs, prods)
    return _each(lambda inv, r: inv + _bdot(inv, r), invs, resids)


@jax.custom_vjp
def unit_lower_inverse(lmats):
    return _unit_lower_inverse_raw(lmats)


def _uli_fwd(lmats):
    invs = _unit_lower_inverse_raw(lmats)
    return invs, invs


def _uli_bwd(invs, g_s):
    ms = _each(lambda inv, g: _bdot(inv, g, (((0,), (0,)), ((), ()))), invs, g_s)
    return (_each(lambda m, inv: -_bdot(m, inv, (((1,), (1,)), ((), ()))), ms, invs),)


unit_lower_inverse.defvjp(_uli_fwd, _uli_bwd)


def _cumsum_raw(x, reverse):
    n = x.shape[0]
    t = lax.broadcasted_iota(jnp.int32, x.shape, 0)
    d = 1
    while d < n:
        if reverse:
            x = x + jnp.where(t < n - d, _roll(x, n - d, 0), 0.0)
        else:
            x = x + jnp.where(t >= d, _roll(x, d, 0), 0.0)
        d *= 2
    return x


@jax.custom_vjp
def cumsum_rows(x):
    return _cumsum_raw(x, False)


def _cumsum_fwd(x):
    return _cumsum_raw(x, False), None


def _cumsum_bwd(_, g):
    return (_cumsum_raw(g, True),)


cumsum_rows.defvjp(_cumsum_fwd, _cumsum_bwd)


_NT = (((1,), (1,)), ((), ()))
_TN = (((0,), (0,)), ((), ()))


def _softplus(x):
    return jnp.maximum(x, 0.0) + jnp.log1p(jnp.exp(-jnp.abs(x)))


def _expm1_nonpos(x):
    poly = x * (1.0 + x * (0.5 + x * (1.0 / 6 + x * (1.0 / 24 + x * (1.0 / 120 + x * (1.0 / 720))))))
    return jnp.where(x > -0.25, poly, jnp.exp(x) - 1.0)


def _rms(x):
    return x * lax.rsqrt(jnp.mean(x * x, axis=-1, keepdims=True) + EPS)


def _causal_conv(x, w, width):
    y = w[width - 1:width, :] * x
    for j in range(width - 1):
        y = y + w[j:j + 1, :] * shift_rows(x, width - 1 - j)
    return y


def _norm_fn(x, g):
    return _rms(x) * g


def _ffn_act_fn(ug, uv, wg, wv, bg, bv):
    return jax.nn.silu(_causal_conv(ug, wg, 3) + bg) * (_causal_conv(uv, wv, 3) + bv)


def _gdn_conv_fn(x, w):
    return jax.nn.silu(_causal_conv(x, w, 4))


def _lru_fn(gate, x, cw, cb, wa, ba, wx, bx, lam):
    xr = _causal_conv(x, cw, 4) + cb
    r = jax.nn.sigmoid(_bdot(xr, wa) + ba)
    i = jax.nn.sigmoid(_bdot(xr, wx) + bx)
    log_a = -LRU_C * r * _softplus(-lam)
    a = jnp.exp(log_a)
    u = jnp.sqrt(-_expm1_nonpos(2.0 * log_a)) * (i * xr)
    hs = lin_scan(a, u)
    return jax.nn.gelu(gate) * hs


def _ret_fn(qs, ks, vs, gates, states, cos2, sin2, dmasks, ktails, qdecs, cdecs):
    qrs = _each(lambda q: q * cos2 + swap_halves(q) * sin2, qs)
    krs = _each(lambda k: (k * cos2 + swap_halves(k) * sin2) * (HEAD ** -0.5), ks)
    scores = _each(lambda q, k, m: _bdot(q, k, _NT) * m, qrs, krs, dmasks)
    inter = _each(lambda q, d, s: _bdot(q * d, s), qrs, qdecs, states)
    os_ = _each(lambda sc, v, x: _bdot(sc, v) + x, scores, vs, inter)
    new_states = _each(lambda s, cd, k, kt, v: s * cd + _bdot(k * kt, v, _TN), states, cdecs, krs, ktails, vs)
    ys = _each(lambda o, g: _rms(o) * jax.nn.silu(g), os_, gates)
    return ys, new_states


def _pick_lane(x, lane_idx):
    lane = lax.broadcasted_iota(jnp.int32, x.shape, 1)
    return jnp.sum(jnp.where(lane == lane_idx, x, 0.0), axis=1, keepdims=True)


def _l2norm(x):
    return x * lax.rsqrt(jnp.sum(x * x, axis=-1, keepdims=True) + EPS)


def _gdn_fn(qcs, kcs, vcs, gates, small, a_log, dt_bias, gain, states):
    c = GDN_CHUNK
    n_heads = len(qcs)
    n_chunks = qcs[0].shape[0] // c
    units = tuple((ci, h) for ci in range(n_chunks) for h in range(n_heads))

    def unit_rows(per_head):
        return tuple(per_head[h][ci * c:(ci + 1) * c] for ci, h in units)

    smalls = tuple(small[ci * c:(ci + 1) * c] for ci, _ in units)
    heads = tuple(h for _, h in units)
    intra = _gdn_intra(unit_rows(qcs), unit_rows(kcs), unit_rows(vcs), smalls, heads, a_log, dt_bias)
    outs = []
    for ci in range(n_chunks):
        mine = slice(ci * n_heads, (ci + 1) * n_heads)
        os_, states = _gdn_inter(*(part[mine] for part in intra), states)
        outs.append(os_)
    ys = tuple(_rms(jnp.concatenate([outs[ci][h] for ci in range(n_chunks)], axis=0)) * gain * jax.nn.silu(gates[h])
               for h in range(n_heads))
    return ys, states


def _gdn_inter(qs, ks, us, ws, attns, gcs, g_lasts, states):
    v_news = _each(lambda u, w, s: u - _bdot(w, s), us, ws, states)
    inter = _each(lambda q, gc, s: _bdot(q * jnp.exp(gc), s), qs, gcs, states)
    os_ = _each(lambda x, a, v: x + _bdot(a, v), inter, attns, v_news)
    new_states = _each(lambda s, gl, k, gc, v: s * jnp.exp(gl) + _bdot(k * jnp.exp(gl - gc), v, _TN), states, g_lasts, ks, gcs, v_news)
    return os_, new_states


def _gdn_intra(qcs, kcs, vcs, smalls, heads, a_log, dt_bias):
    c = GDN_CHUNK
    qs = _each(lambda x: _l2norm(x) * (HEAD ** -0.5), qcs)
    ks = _each(_l2norm, kcs)
    betas = _each(lambda sm, h: jax.nn.sigmoid(_pick_lane(sm, h)), smalls, heads)
    gs = _each(lambda sm, h: -jnp.exp(_pick_lane(a_log, h)) * _softplus(_pick_lane(sm, h + N_HEADS) + _pick_lane(dt_bias, h)),
               smalls, heads)
    i = lax.broadcasted_iota(jnp.int32, (c, c), 0)
    j = lax.broadcasted_iota(jnp.int32, (c, c), 1)
    tril = i >= j
    gcs = _each(lambda g: cumsum_rows(jnp.broadcast_to(g, (c, LANES)))[:, :1], gs)
    gc_rows = _each(lambda gc: jnp.broadcast_to(gc, (c, c)), gcs)
    decays = _each(lambda r: jnp.where(tril, jnp.exp(jnp.where(tril, r - r.T, 0.0)), 0.0), gc_rows)
    kbs = _each(lambda k, b: k * b, ks, betas)
    lmats = _each(lambda kb, k, d: jnp.where(i > j, _bdot(kb, k, _NT) * d, 0.0), kbs, ks, decays)
    attns = _each(lambda q, k, d: jnp.where(tril, _bdot(q, k, _NT) * d, 0.0), qs, ks, decays)
    invs = unit_lower_inverse(lmats)
    us = dot3(invs, _each(lambda v, b: v * b, vcs, betas))
    ws = dot3(invs, _each(lambda kb, gc: kb * jnp.exp(gc), kbs, gcs))
    g_lasts = _each(lambda g: jnp.sum(g, axis=0, keepdims=True), gs)
    return qs, ks, us, ws, attns, gcs, g_lasts


def _final_fn(h, g, target):
    y = _rms(h) * g
    return 0.5 * jnp.sum(jnp.mean(jnp.square(y - target), axis=-1, keepdims=True), axis=0, keepdims=True)


def _tile(n, candidates):
    for t in candidates:
        if n % t == 0:
            return t
    raise ValueError(f"no tile for {n}")


def matmul(a, b, *, ta=False, tb=False, add=None, out_dtype=F32, tm=None, tn=None, split=None, layer=None, name):
    m = a.shape[1] if ta else a.shape[0]
    k = a.shape[0] if ta else a.shape[1]
    n = b.shape[0] if tb else b.shape[1]
    assert k == (b.shape[1] if tb else b.shape[0])
    out_shape, out_block, out_index = (m, n), None, lambda i, j: (i, j)
    if split is not None:
        dims4, perm = split
        out_shape = tuple(dims4[p] for p in perm)
        r, cols = out_shape[2:]
        tm, tn = tm or _tile(r, (512, 256, 128)), tn or _tile(cols, (1408, 1024, 512))
        rb, cb = r // tm, cols // tn
        out_block = (None, None, tm, tn)
        if perm == (0, 2, 1, 3):
            out_index = lambda i, j: (i // rb, j // cb, i % rb, j % cb)
        elif perm == (1, 0, 2, 3):
            tm, out_block = 2 * r, (2, None, r, tn)
            out_index = lambda i, j: (0, i, 0, j)
        else:
            raise ValueError(perm)
    tm = tm or _tile(m, (1024, 512, 1408, 256, 128))
    tn = tn or _tile(n, (512, 1408, 256, 128))
    aliases, prev = {}, None
    if layer is not None:
        index, count, prev = layer
        out_shape, out_block, out_index = (count, m, n), (None, tm, tn), lambda i, j: (index, i, j)
    dims = (((0 if ta else 1,), (1 if tb else 0,)), ((), ()))

    def body(a_ref, b_ref, *rest):
        acc = lax.dot_general(a_ref[...].astype(BF16), b_ref[...].astype(BF16), dims, preferred_element_type=F32)
        if add is not None:
            acc = acc + rest[0][...]
        rest[-1][...] = acc.astype(out_dtype).reshape(rest[-1].shape)

    a_spec = pl.BlockSpec((k, tm), lambda i, j: (0, i)) if ta else pl.BlockSpec((tm, k), lambda i, j: (i, 0))
    b_spec = pl.BlockSpec((tn, k), lambda i, j: (j, 0)) if tb else pl.BlockSpec((k, tn), lambda i, j: (0, j))
    o_spec = pl.BlockSpec(out_block or (tm, tn), out_index)
    in_specs, args = [a_spec, b_spec], [a, b]
    if add is not None:
        in_specs.append(o_spec)
        args.append(add)
    if prev is not None:
        aliases = {len(args): 0}
        in_specs.append(pl.BlockSpec(memory_space=pl.ANY))
        args.append(prev)
    return pl.pallas_call(body, out_shape=_sds(out_shape, out_dtype), grid=(m // tm, n // tn), in_specs=in_specs,
                          out_specs=o_spec, input_output_aliases=aliases, compiler_params=_params(), name=name)(*args)


ROW_TILE = 256


def norm_fwd(x, g, *, name):
    t, d = x.shape

    def body(x_ref, g_ref, o_ref):
        o_ref[...] = _norm_fn(x_ref[...], g_ref[...]).astype(BF16)

    return pl.pallas_call(body, out_shape=_sds((t, d), BF16), grid=(t // ROW_TILE,),
                          in_specs=[pl.BlockSpec((ROW_TILE, d), lambda i: (i, 0)), pl.BlockSpec((1, d), lambda i: (0, 0))],
                          out_specs=pl.BlockSpec((ROW_TILE, d), lambda i: (i, 0)), compiler_params=_params(), name=name)(x, g)


def norm_bwd(x, g, dy, dres, *, name):
    t, d = x.shape

    def body(x_ref, g_ref, dy_ref, dres_ref, dx_ref, dg_ref):
        _, vjp = jax.vjp(_norm_fn, x_ref[...], g_ref[...])
        dx, dg = vjp(dy_ref[...])
        dx_ref[...] = dx + dres_ref[...]

        @pl.when(pl.program_id(0) == 0)
        def _():
            dg_ref[...] = jnp.zeros_like(dg_ref)

        dg_ref[...] += dg

    row = pl.BlockSpec((ROW_TILE, d), lambda i: (i, 0))
    vec = pl.BlockSpec((1, d), lambda i: (0, 0))
    return pl.pallas_call(body, out_shape=(_sds((t, d), F32), _sds((1, d), F32)), grid=(t // ROW_TILE,),
                          in_specs=[row, vec, row, row], out_specs=(row, vec), compiler_params=_params(), name=name)(x, g, dy, dres)


def final_fwd_bwd(h, g, target, *, name):
    t, d = h.shape

    def body(h_ref, g_ref, t_ref, loss_ref, dh_ref, dg_ref):
        tgt = t_ref[...]
        loss, vjp = jax.vjp(lambda hh, gg: _final_fn(hh, gg, tgt), h_ref[...], g_ref[...])
        dh, dg = vjp(jnp.ones((1, 1), F32))
        dh_ref[...] = dh

        @pl.when(pl.program_id(0) == 0)
        def _():
            dg_ref[...] = jnp.zeros_like(dg_ref)
            loss_ref[...] = jnp.zeros_like(loss_ref)

        dg_ref[...] += dg
        loss_ref[...] += jnp.broadcast_to(loss, loss_ref.shape)

    row = pl.BlockSpec((ROW_TILE, d), lambda i: (i, 0))
    vec = pl.BlockSpec((1, d), lambda i: (0, 0))
    return pl.pallas_call(body, out_shape=(_sds((1, LANES), F32), _sds((t, d), F32), _sds((1, d), F32)), grid=(t // ROW_TILE,),
                          in_specs=[row, vec, row], out_specs=(pl.BlockSpec((1, LANES), lambda i: (0, 0)), row, vec),
                          compiler_params=_params(), name=name)(h, g, target)


FFN_FWD_COLS = 256
FFN_BWD_COLS = 128


def ffn_act_fwd(u, cw, cb, *, name):
    t = u.shape[0]
    w = FFN_FWD_COLS
    nb = D_FF // w

    def body(ug_ref, uv_ref, wg_ref, wv_ref, bg_ref, bv_ref, o_ref):
        o_ref[...] = _ffn_act_fn(ug_ref[...], uv_ref[...], wg_ref[...], wv_ref[...], bg_ref[...], bv_ref[...]).astype(BF16)

    def col(rows, off):
        return pl.BlockSpec((rows, w), lambda j: (0, j + off))

    return pl.pallas_call(body, out_shape=_sds((t, D_FF), BF16), grid=(nb,),
                          in_specs=[col(t, 0), col(t, nb), col(3, 0), col(3, nb), col(1, 0), col(1, nb)],
                          out_specs=col(t, 0), compiler_params=_params(), name=name)(u, u, cw, cw, cb, cb)


def ffn_act_bwd(u, cw, cb, da, *, name):
    t = u.shape[0]
    w = FFN_BWD_COLS
    nb = D_FF // w

    def body(ug_ref, uv_ref, wg_ref, wv_ref, bg_ref, bv_ref, da_ref, dug_ref, duv_ref, dwg_ref, dwv_ref, dbg_ref, dbv_ref):
        _, vjp = jax.vjp(_ffn_act_fn, ug_ref[...], uv_ref[...], wg_ref[...], wv_ref[...], bg_ref[...], bv_ref[...])
        dug, duv, dwg, dwv, dbg, dbv = vjp(da_ref[...])
        dug_ref[...] = dug.astype(BF16)
        duv_ref[...] = duv.astype(BF16)
        dwg_ref[...] = dwg
        dwv_ref[...] = dwv
        dbg_ref[...] = dbg
        dbv_ref[...] = dbv

    def col(rows, off):
        return pl.BlockSpec((rows, w), lambda j: (0, j + off))

    outs = pl.pallas_call(
        body, out_shape=(_sds((t, D_FF), BF16), _sds((t, D_FF), BF16), _sds((3, D_FF), F32), _sds((3, D_FF), F32),
                         _sds((1, D_FF), F32), _sds((1, D_FF), F32)),
        grid=(nb,), in_specs=[col(t, 0), col(t, nb), col(3, 0), col(3, nb), col(1, 0), col(1, nb), col(t, 0)],
        out_specs=(col(t, 0), col(t, 0), col(3, 0), col(3, 0), col(1, 0), col(1, 0)), compiler_params=_params(), name=name,
    )(u, u, cw, cw, cb, cb, da)
    dug, duv, dwg, dwv, dbg, dbv = outs
    return dug, duv, jnp.concatenate([dwg, dwv], axis=1), jnp.concatenate([dbg, dbv], axis=1)


GDN_CONV_COLS = 256
GDN_CONV_OFF = 4 * GROUP


def gdn_conv_fwd(p, cw, *, name):
    t = p.shape[0]
    w = GDN_CONV_COLS
    nb = 3 * GROUP // w
    off = GDN_CONV_OFF // w

    def body(x_ref, w_ref, o_ref):
        o_ref[...] = _gdn_conv_fn(x_ref[...], w_ref[...])

    return pl.pallas_call(body, out_shape=_sds((t, 3 * GROUP), F32), grid=(nb,),
                          in_specs=[pl.BlockSpec((t, w), lambda j: (0, j + off)), pl.BlockSpec((4, w), lambda j: (0, j))],
                          out_specs=pl.BlockSpec((t, w), lambda j: (0, j)), compiler_params=_params(), name=name)(p, cw)


def gdn_conv_bwd(p, cw, dc, *, name):
    t = p.shape[0]
    w = GDN_CONV_COLS
    nb = 3 * GROUP // w
    off = GDN_CONV_OFF // w

    def body(x_ref, w_ref, dc_ref, dx_ref, dw_ref):
        _, vjp = jax.vjp(_gdn_conv_fn, x_ref[...], w_ref[...])
        dx, dw = vjp(dc_ref[...])
        dx_ref[...] = dx.astype(BF16)
        dw_ref[...] = dw

    blk = pl.BlockSpec((t, w), lambda j: (0, j))
    wblk = pl.BlockSpec((4, w), lambda j: (0, j))
    return pl.pallas_call(body, out_shape=(_sds((t, 3 * GROUP), BF16), _sds((4, 3 * GROUP), F32)), grid=(nb,),
                          in_specs=[pl.BlockSpec((t, w), lambda j: (0, j + off)), wblk, blk], out_specs=(blk, wblk),
                          compiler_params=_params(), name=name)(p, cw, dc)


def _lru_specs(t):
    w = D_MODEL // LRU_BLOCKS
    gate = pl.BlockSpec((t, w), lambda j: (0, j))
    xin = pl.BlockSpec((t, w), lambda j: (0, j + LRU_BLOCKS))
    cw = pl.BlockSpec((4, w), lambda j: (0, j))
    vec = pl.BlockSpec((1, w), lambda j: (0, j))
    mat = pl.BlockSpec((None, w, w), lambda j: (j, 0, 0))
    return gate, xin, cw, vec, mat


def lru_fwd(gx, cw, cb, wa, ba, wx, bx, lam, *, name):
    t = gx.shape[0]
    gate, xin, cws, vec, mat = _lru_specs(t)

    def body(g_ref, x_ref, cw_ref, cb_ref, wa_ref, ba_ref, wx_ref, bx_ref, lam_ref, o_ref):
        o_ref[...] = _lru_fn(g_ref[...], x_ref[...], cw_ref[...], cb_ref[...], wa_ref[...], ba_ref[...], wx_ref[...],
                             bx_ref[...], lam_ref[...]).astype(BF16)

    return pl.pallas_call(body, out_shape=_sds((t, D_MODEL), BF16), grid=(LRU_BLOCKS,),
                          in_specs=[gate, xin, cws, vec, mat, vec, mat, vec, vec], out_specs=gate,
                          compiler_params=_params(), name=name)(gx, gx, cw, cb, wa, ba, wx, bx, lam)


def lru_bwd(gx, cw, cb, wa, ba, wx, bx, lam, dy, *, name):
    t = gx.shape[0]
    gate, xin, cws, vec, mat = _lru_specs(t)

    def body(g_ref, x_ref, cw_ref, cb_ref, wa_ref, ba_ref, wx_ref, bx_ref, lam_ref, dy_ref,
             dg_ref, dx_ref, dcw_ref, dcb_ref, dwa_ref, dba_ref, dwx_ref, dbx_ref, dlam_ref):
        _, vjp = jax.vjp(_lru_fn, g_ref[...], x_ref[...], cw_ref[...], cb_ref[...], wa_ref[...], ba_ref[...], wx_ref[...],
                         bx_ref[...], lam_ref[...])
        dg, dx, dcw, dcb, dwa, dba, dwx, dbx, dlam = vjp(dy_ref[...])
        dg_ref[...] = dg.astype(BF16)
        dx_ref[...] = dx.astype(BF16)
        dcw_ref[...] = dcw
        dcb_ref[...] = dcb
        dwa_ref[...] = dwa
        dba_ref[...] = dba
        dwx_ref[...] = dwx
        dbx_ref[...] = dbx
        dlam_ref[...] = dlam

    d = D_MODEL
    w = d // LRU_BLOCKS
    out_shape = (_sds((t, d), BF16), _sds((t, d), BF16), _sds((4, d), F32), _sds((1, d), F32), _sds((LRU_BLOCKS, w, w), F32),
                 _sds((1, d), F32), _sds((LRU_BLOCKS, w, w), F32), _sds((1, d), F32), _sds((1, d), F32))
    return pl.pallas_call(body, out_shape=out_shape, grid=(LRU_BLOCKS,),
                          in_specs=[gate, xin, cws, vec, mat, vec, mat, vec, vec, gate],
                          out_specs=(gate, gate, cws, vec, mat, vec, mat, vec, vec), compiler_params=_params(), name=name,
                          )(gx, gx, cw, cb, wa, ba, wx, bx, lam, dy)


def _ret_tables():
    half = HEAD // 2
    inv_freq = (np.float32(ROPE_BASE) ** (-np.arange(half, dtype=np.float32) / np.float32(half))).astype(np.float32)
    ang = (np.arange(SEQ, dtype=np.float32)[:, None] * inv_freq[None, :]).astype(np.float64)
    cos2 = np.concatenate([np.cos(ang), np.cos(ang)], axis=1).astype(np.float32)
    sin2 = np.concatenate([-np.sin(ang), np.sin(ang)], axis=1).astype(np.float32)
    c = RET_CHUNK
    log_gamma = np.log1p(-np.exp2(-5.0 - np.arange(N_HEADS, dtype=np.float64)))
    idx = np.arange(c, dtype=np.float64)
    rel = idx[:, None] - idx[None, :]
    dmask = np.where(rel >= 0, np.exp(log_gamma[:, None, None] * np.maximum(rel, 0.0)), 0.0)
    ones = np.ones((N_HEADS, c, HEAD))
    ktail = np.exp(log_gamma[:, None] * (c - 1 - idx))[:, :, None] * ones
    qdec = np.exp(log_gamma[:, None] * (idx + 1.0))[:, :, None] * ones
    cdec = np.exp(log_gamma * c)[:, None, None] * ones
    return tuple(jnp.asarray(a, F32) for a in (cos2, sin2, dmask, ktail, qdec, cdec))


def _ret_specs(rev):
    c = RET_CHUNK
    nc = SEQ // c

    def n_of(n):
        return nc - 1 - n if rev else n

    def group(off):
        return pl.BlockSpec((c, GROUP), lambda n: (n_of(n), off))

    tab = pl.BlockSpec((c, HEAD), lambda n: (n_of(n), 0))
    const = pl.BlockSpec((N_HEADS, c, HEAD), lambda n: (0, 0, 0))
    state = pl.BlockSpec((N_HEADS, None, HEAD, HEAD), lambda n: (0, n_of(n), 0, 0))
    return group, tab, const, state, nc


def _head(h):
    return slice(h * HEAD, (h + 1) * HEAD)


def ret_fwd(p, tables, *, name):
    group, tab, const, state, nc = _ret_specs(False)

    def body(q_ref, k_ref, v_ref, g_ref, cos_ref, sin_ref, dm_ref, kt_ref, qd_ref, cd_ref, y_ref, st_ref, s_scr):
        @pl.when(pl.program_id(0) == 0)
        def _():
            s_scr[...] = jnp.zeros_like(s_scr)

        heads = range(N_HEADS)
        states = tuple(s_scr[h] for h in heads)
        ys, new_states = _ret_fn(*(tuple(r[:, _head(h)] for h in heads) for r in (q_ref, k_ref, v_ref, g_ref)), states,
                                 cos_ref[...], sin_ref[...], *(tuple(r[h] for h in heads) for r in (dm_ref, kt_ref, qd_ref, cd_ref)))
        for h in heads:
            st_ref[h] = states[h]
            y_ref[:, _head(h)] = ys[h].astype(BF16)
            s_scr[h] = new_states[h]

    return pl.pallas_call(
        body, out_shape=(_sds((SEQ, GROUP), BF16), _sds((N_HEADS, nc, HEAD, HEAD), F32)), grid=(nc,),
        in_specs=[group(0), group(1), group(2), group(3), tab, tab, const, const, const, const],
        out_specs=(group(0), state), scratch_shapes=[pltpu.VMEM((N_HEADS, HEAD, HEAD), F32)], compiler_params=_params(), name=name,
    )(p, p, p, p, *tables)


def ret_bwd(p, tables, states, dy, *, name):
    group, tab, const, state, nc = _ret_specs(True)

    def body(q_ref, k_ref, v_ref, g_ref, cos_ref, sin_ref, dm_ref, kt_ref, qd_ref, cd_ref, st_ref, dy_ref,
             dq_ref, dk_ref, dv_ref, dg_ref, ds_scr):
        @pl.when(pl.program_id(0) == 0)
        def _():
            ds_scr[...] = jnp.zeros_like(ds_scr)

        heads = range(N_HEADS)
        consts = (cos_ref[...], sin_ref[...], *(tuple(r[h] for h in heads) for r in (dm_ref, kt_ref, qd_ref, cd_ref)))
        _, vjp = jax.vjp(lambda *a: _ret_fn(*a, *consts), *(tuple(r[:, _head(h)] for h in heads) for r in (q_ref, k_ref, v_ref, g_ref)),
                         tuple(st_ref[h] for h in heads))
        dqs, dks, dvs, dgs, dss = vjp((tuple(dy_ref[:, _head(h)] for h in heads), tuple(ds_scr[h] for h in heads)))
        for h in heads:
            dq_ref[:, _head(h)] = dqs[h].astype(BF16)
            dk_ref[:, _head(h)] = dks[h].astype(BF16)
            dv_ref[:, _head(h)] = dvs[h].astype(BF16)
            dg_ref[:, _head(h)] = dgs[h].astype(BF16)
            ds_scr[h] = dss[h]

    out = _sds((SEQ, GROUP), BF16)
    return pl.pallas_call(
        body, out_shape=(out, out, out, out), grid=(nc,),
        in_specs=[group(0), group(1), group(2), group(3), tab, tab, const, const, const, const, state, group(0)],
        out_specs=(group(0), group(0), group(0), group(0)), scratch_shapes=[pltpu.VMEM((N_HEADS, HEAD, HEAD), F32)],
        compiler_params=_params(), name=name,
    )(p, p, p, p, *tables, states, dy)


def _gdn_specs(rev):
    c = GDN_CHUNK * GDN_CHUNKS_PER_STEP
    nc = SEQ // c

    def n_of(n):
        return nc - 1 - n if rev else n

    def group(off):
        return pl.BlockSpec((c, GROUP), lambda n: (n_of(n), off))

    small = pl.BlockSpec((c, LANES), lambda n: (n_of(n), 0))
    vec = pl.BlockSpec((1, LANES), lambda n: (0, 0))
    state = pl.BlockSpec((N_HEADS, None, HEAD, HEAD), lambda n: (0, n_of(n), 0, 0))
    return group, small, vec, state, nc


GDN_GATE_GROUP = 7


def gdn_fwd(conv, p, small, a_log, dt_bias, gain, *, name):
    group, sm, vec, state, nc = _gdn_specs(False)

    def body(q_ref, k_ref, v_ref, g_ref, sm_ref, al_ref, dt_ref, gn_ref, y_ref, st_ref, s_scr):
        @pl.when(pl.program_id(0) == 0)
        def _():
            s_scr[...] = jnp.zeros_like(s_scr)

        states = tuple(s_scr[h] for h in range(N_HEADS))
        ys, new_states = _gdn_fn(*(tuple(r[:, _head(h)] for h in range(N_HEADS)) for r in (q_ref, k_ref, v_ref, g_ref)),
                                 sm_ref[...], al_ref[...], dt_ref[...], gn_ref[...], states)
        for h in range(N_HEADS):
            st_ref[h] = states[h]
            y_ref[:, _head(h)] = ys[h].astype(BF16)
            s_scr[h] = new_states[h]

    return pl.pallas_call(
        body, out_shape=(_sds((SEQ, GROUP), BF16), _sds((N_HEADS, nc, HEAD, HEAD), F32)), grid=(nc,),
        in_specs=[group(0), group(1), group(2), group(GDN_GATE_GROUP), sm, vec, vec, vec], out_specs=(group(0), state),
        scratch_shapes=[pltpu.VMEM((N_HEADS, HEAD, HEAD), F32)], compiler_params=_params(), name=name,
    )(conv, conv, conv, p, small, a_log, dt_bias, gain)


def gdn_bwd(conv, p, small, a_log, dt_bias, gain, states, dy, *, name):
    group, sm, vec, state, nc = _gdn_specs(True)

    def body(q_ref, k_ref, v_ref, g_ref, sm_ref, al_ref, dt_ref, gn_ref, st_ref, dy_ref,
             dq_ref, dk_ref, dv_ref, dg_ref, dsm_ref, dal_ref, ddt_ref, dgn_ref, ds_scr):
        @pl.when(pl.program_id(0) == 0)
        def _():
            ds_scr[...] = jnp.zeros_like(ds_scr)
            dal_ref[...] = jnp.zeros_like(dal_ref)
            ddt_ref[...] = jnp.zeros_like(ddt_ref)
            dgn_ref[...] = jnp.zeros_like(dgn_ref)

        per_head = tuple(tuple(r[:, _head(h)] for h in range(N_HEADS)) for r in (q_ref, k_ref, v_ref, g_ref))
        _, vjp = jax.vjp(_gdn_fn, *per_head, sm_ref[...], al_ref[...], dt_ref[...], gn_ref[...],
                         tuple(st_ref[h] for h in range(N_HEADS)))
        cts = (tuple(dy_ref[:, _head(h)] for h in range(N_HEADS)), tuple(ds_scr[h] for h in range(N_HEADS)))
        dqs, dks, dvs, dgs, dsm, dal, ddt, dgn, dss = vjp(cts)
        for h in range(N_HEADS):
            dq_ref[:, _head(h)] = dqs[h]
            dk_ref[:, _head(h)] = dks[h]
            dv_ref[:, _head(h)] = dvs[h]
            dg_ref[:, _head(h)] = dgs[h].astype(BF16)
            ds_scr[h] = dss[h]
        dsm_ref[...] = dsm
        dal_ref[...] += dal
        ddt_ref[...] += ddt
        dgn_ref[...] += dgn

    f = _sds((SEQ, GROUP), F32)
    pv = _sds((1, LANES), F32)
    return pl.pallas_call(
        body, out_shape=(f, f, f, _sds((SEQ, GROUP), BF16), _sds((SEQ, LANES), F32), pv, pv, pv), grid=(nc,),
        in_specs=[group(0), group(1), group(2), group(GDN_GATE_GROUP), sm, vec, vec, vec, state, group(1)],
        out_specs=(group(0), group(0), group(0), group(0), sm, vec, vec, vec), scratch_shapes=[pltpu.VMEM((N_HEADS, HEAD, HEAD), F32)],
        compiler_params=_params(), name=name,
    )(conv, conv, conv, p, small, a_log, dt_bias, gain, states, dy)


PACK_ROW_TILE = 1024


def adamw(w, g, m, v, *, name):
    r = w.shape[0]
    tr = _row_tile(r, LANES)

    def body(w_ref, g_ref, m_ref, v_ref, d_ref, nm_ref, nv_ref):
        gg = g_ref[...]
        nm = ADAM_B1 * m_ref[...] + (1.0 - ADAM_B1) * gg
        nv = ADAM_B2 * v_ref[...] + (1.0 - ADAM_B2) * jnp.square(gg)
        m_hat = nm / (1.0 - ADAM_B1 ** ADAM_STEP)
        v_hat = nv / (1.0 - ADAM_B2 ** ADAM_STEP)
        d_ref[...] = -ADAM_LR * (m_hat / (jnp.sqrt(v_hat) + ADAM_EPS) + ADAM_WD * w_ref[...])
        nm_ref[...] = nm
        nv_ref[...] = nv

    blk = pl.BlockSpec((tr, LANES), lambda i: (i, 0))
    o = _sds((r, LANES), F32)
    return pl.pallas_call(body, out_shape=(o, o, o), grid=(r // tr,), in_specs=[blk] * 4, out_specs=(blk, blk, blk),
                          compiler_params=_params(), name=name)(w, g, m, v)


ELEMENTWISE_BLOCK_BYTES = 2 * 1024 * 1024


def _row_tile(r, c):
    best = None
    for tr in range(8, r + 1, 8):
        if r % tr == 0 and tr * c * 4 <= ELEMENTWISE_BLOCK_BYTES:
            best = tr
    if best is None:
        raise ValueError(f"no row tile for ({r}, {c})")
    return best


def _tile_2d(r, c):
    if any(r % tr == 0 for tr in range(8, r + 1, 8)):
        return _row_tile(r, c), c
    tc = max(t for t in range(LANES, c + 1, LANES) if c % t == 0 and r * t * 4 <= ELEMENTWISE_BLOCK_BYTES)
    return r, tc


def _core_index():
    return lax.axis_index("c").astype(jnp.int32).reshape(1)


def _chip_index():
    return (2 * lax.axis_index("x") + lax.axis_index("y")).astype(jnp.int32).reshape(1)


def adamw_halves(w, m, v, g_own, g_sib, *, layer=0, prev=None, name):
    n_layers, rows, c = w.shape
    r = rows // 2
    tr = _row_tile(r, c)
    nb = r // tr

    def body(c_ref, w_ref, m_ref, v_ref, own_ref, sib_ref, *rest):
        g_ref, d_ref, nm_ref, nv_ref = rest[-4:]
        gg = jnp.where(pl.program_id(0) == c_ref[0], own_ref[...], sib_ref[...])
        nm = ADAM_B1 * m_ref[...] + (1.0 - ADAM_B1) * gg
        nv = ADAM_B2 * v_ref[...] + (1.0 - ADAM_B2) * jnp.square(gg)
        m_hat = nm / (1.0 - ADAM_B1 ** ADAM_STEP)
        v_hat = nv / (1.0 - ADAM_B2 ** ADAM_STEP)
        g_ref[...] = gg
        d_ref[...] = -ADAM_LR * (m_hat / (jnp.sqrt(v_hat) + ADAM_EPS) + ADAM_WD * w_ref[...])
        nm_ref[...] = nm
        nv_ref[...] = nv

    full = pl.BlockSpec((None, tr, c), lambda h, i, cr: (layer, h * nb + i, 0))
    half = pl.BlockSpec((tr, c), lambda h, i, cr: (i, 0))
    o = _sds((n_layers, rows, c), F32)
    prev = list(prev or ())
    gs = pltpu.PrefetchScalarGridSpec(num_scalar_prefetch=1, grid=(2, nb), in_specs=[full, full, full, half, half] + [_ANY] * len(prev),
                                      out_specs=(full, full, full, full))
    n_fixed = 6
    return pl.pallas_call(body, out_shape=(o, o, o, o), grid_spec=gs, compiler_params=_params(), name=name,
                          input_output_aliases={n_fixed + k: k for k in range(len(prev))})(
        _core_index(), w, m, v, g_own, g_sib, *prev)


ADAMW_COLUMN_TILE = 256


def adamw_column_halves(w, m, v, g_own, g_sib, *, name):
    _, rows, cols = w.shape
    tc = ADAMW_COLUMN_TILE
    per_half = cols // 2 // tc

    def body(c_ref, w_ref, m_ref, v_ref, own_ref, sib_ref, g_ref, d_ref, nm_ref, nv_ref):
        gg = jnp.where(pl.program_id(0) // per_half == c_ref[0], own_ref[...], sib_ref[...])
        nm = ADAM_B1 * m_ref[...] + (1.0 - ADAM_B1) * gg
        nv = ADAM_B2 * v_ref[...] + (1.0 - ADAM_B2) * jnp.square(gg)
        m_hat = nm / (1.0 - ADAM_B1 ** ADAM_STEP)
        v_hat = nv / (1.0 - ADAM_B2 ** ADAM_STEP)
        g_ref[...] = gg
        d_ref[...] = -ADAM_LR * (m_hat / (jnp.sqrt(v_hat) + ADAM_EPS) + ADAM_WD * w_ref[...])
        nm_ref[...] = nm
        nv_ref[...] = nv

    full = pl.BlockSpec((None, rows, tc), lambda j, cr: (0, 0, j))
    half = pl.BlockSpec((rows, tc), lambda j, cr: (0, j % per_half))
    o = _sds(w.shape, F32)
    gs = pltpu.PrefetchScalarGridSpec(num_scalar_prefetch=1, grid=(cols // tc,), in_specs=[full, full, full, half, half],
                                      out_specs=(full, full, full, full))
    return pl.pallas_call(body, out_shape=(o, o, o, o), grid_spec=gs, compiler_params=_params(), name=name)(
        _core_index(), w, m, v, g_own, g_sib)


def add_core_halves(g2, land, *, out_dtype, name):
    _, ns, r, cols = g2.shape
    tr, tc = _tile_2d(r, cols)

    def body(c_ref, a_ref, b_ref, o_ref):
        o_ref[...] = (a_ref[...] + b_ref[...]).astype(out_dtype)

    gs = pltpu.PrefetchScalarGridSpec(
        num_scalar_prefetch=1, grid=(ns, r // tr, cols // tc),
        in_specs=[pl.BlockSpec((None, None, tr, tc), lambda s, i, j, cr: (cr[0], s, i, j)),
                  pl.BlockSpec((None, tr, tc), lambda s, i, j, cr: (s, i, j))],
        out_specs=pl.BlockSpec((None, tr, tc), lambda s, i, j, cr: (s, i, j)))
    return pl.pallas_call(body, out_shape=_sds((ns, r, cols), out_dtype), grid_spec=gs, compiler_params=_params(), name=name)(
        _core_index(), g2, land)


def sum_over_chips(own, land, *, scatter, name):
    _, r, cols = own.shape
    tr, tc = _tile_2d(r, cols)

    def body(mine_ref, own_ref, l0, l1, l2, l3, o_ref):
        mine = mine_ref[0]
        mine_val = own_ref[...]
        acc = None
        for s, l_ref in enumerate((l0, l1, l2, l3)):
            val = jnp.where(mine == s, mine_val, l_ref[...]).astype(F32)
            acc = val if acc is None else acc + val
        o_ref[...] = acc

    def slot(s):
        return pl.BlockSpec((None, tr, tc), lambda i, j, mr: (jnp.where(mr[0] == s, (s + 1) % N_SHARD, s), i, j))

    own_spec = pl.BlockSpec((None, tr, tc), lambda i, j, mr: (mr[0] if scatter else 0, i, j))
    gs = pltpu.PrefetchScalarGridSpec(num_scalar_prefetch=1, grid=(r // tr, cols // tc), in_specs=[own_spec] + [slot(s) for s in range(N_SHARD)],
                                      out_specs=pl.BlockSpec((tr, tc), lambda i, j, mr: (i, j)))
    return pl.pallas_call(body, out_shape=_sds((r, cols), F32), grid_spec=gs, compiler_params=_params(), name=name)(
        _chip_index(), own, land, land, land, land)


_ANY = pl.BlockSpec(memory_space=pl.ANY)


def xy_exchange(src, *, scatter, name):
    rh = src.shape[1]

    def body(src_ref, land_ref, send_sems, recv_sems, loc_sem):
        x, y, c = lax.axis_index("x"), lax.axis_index("y"), lax.axis_index("c")
        mine = 2 * x + y
        peers = [(1 - x, y), (x, 1 - y), (1 - x, 1 - y)]

        def piece(shard):
            return src_ref.at[shard] if scatter else src_ref.at[c]

        def copy(k, px, py, dst_slot):
            return pltpu.make_async_remote_copy(src_ref=piece(2 * px + py), dst_ref=land_ref.at[dst_slot], send_sem=send_sems.at[k],
                                                recv_sem=recv_sems.at[k], device_id=(px, py, c), device_id_type=MESH)

        keep = pltpu.make_async_copy(piece(mine), land_ref.at[mine], loc_sem)
        keep.start()
        sends = [copy(k, px, py, mine) for k, (px, py) in enumerate(peers)]
        for cp in sends:
            cp.start()
        for cp in sends:
            cp.wait_send()
        for k, (px, py) in enumerate(peers):
            copy(k, px, py, 2 * px + py).wait_recv()
        keep.wait()

    return pl.pallas_call(body, out_shape=_sds((N_SHARD, rh, LANES), src.dtype), in_specs=[_ANY], out_specs=_ANY,
                          scratch_shapes=[pltpu.SemaphoreType.DMA((3,)), pltpu.SemaphoreType.DMA((3,)), pltpu.SemaphoreType.DMA(())],
                          name=name)(src)


def core_exchange(src, *, send_other_half, name):
    def body(src_ref, out_ref, send_sem, recv_sem, loc_sem):
        x, y, c = lax.axis_index("x"), lax.axis_index("y"), lax.axis_index("c")
        if send_other_half:
            cp = pltpu.make_async_remote_copy(src_ref=src_ref.at[1 - c], dst_ref=out_ref, send_sem=send_sem, recv_sem=recv_sem,
                                              device_id=(x, y, 1 - c), device_id_type=MESH)
            cp.start()
            cp.wait_send()
            cp.wait_recv()
        else:
            keep = pltpu.make_async_copy(src_ref, out_ref.at[c], loc_sem)
            keep.start()
            cp = pltpu.make_async_remote_copy(src_ref=src_ref, dst_ref=out_ref.at[c], send_sem=send_sem, recv_sem=recv_sem,
                                              device_id=(x, y, 1 - c), device_id_type=MESH)
            cp.start()
            cp.wait_send()
            pltpu.make_async_remote_copy(src_ref=src_ref, dst_ref=out_ref.at[1 - c], send_sem=send_sem, recv_sem=recv_sem,
                                         device_id=(x, y, 1 - c), device_id_type=MESH).wait_recv()
            keep.wait()

    out_shape = _sds(src.shape[1:], src.dtype) if send_other_half else _sds((2,) + src.shape, src.dtype)
    return pl.pallas_call(body, out_shape=out_shape, in_specs=[_ANY], out_specs=_ANY,
                          scratch_shapes=[pltpu.SemaphoreType.DMA(()), pltpu.SemaphoreType.DMA(()), pltpu.SemaphoreType.DMA(())],
                          name=name)(src)


def _comm_call(body, ins, out_shapes, sem_counts, name):
    return pl.pallas_call(body, out_shape=tuple(out_shapes), in_specs=[_ANY] * len(ins), out_specs=tuple([_ANY] * len(out_shapes)),
                          scratch_shapes=[pltpu.SemaphoreType.DMA((k,)) for k in sem_counts], name=name)(*ins)


def _sequencer_call(body, ins, out_shapes, sem_counts, name, collective_id):
    return pl.kernel(body, out_type=list(out_shapes), mesh=plsc.ScalarSubcoreMesh(axis_name="sequencer", num_cores=1), name=name,
                     scratch_types=[pltpu.SemaphoreType.DMA((k,)) for k in sem_counts],
                     compiler_params=pltpu.CompilerParams(collective_id=collective_id))(*ins)


def _handshake(peers):
    barrier = pltpu.get_barrier_semaphore()
    for peer in peers:
        pl.semaphore_signal(barrier, inc=1, device_id=peer, device_id_type=MESH)
    pl.semaphore_wait(barrier, len(peers))


def _xy_peers(x, y):
    return [(1 - x, y), (x, 1 - y), (1 - x, 1 - y)]


def gather_halves(halves, *, name, collective_id):
    n = len(halves)

    def body(*refs):
        ins, lands, sibs = refs[:n], refs[n:2 * n], refs[2 * n:3 * n]
        ici_send, ici_recv, d2d_send, d2d_recv = refs[3 * n:]
        x, y, c = lax.axis_index("x"), lax.axis_index("y"), lax.axis_index("c")
        mine = 2 * x + y
        peers = _xy_peers(x, y)
        _handshake([(px, py, c) for px, py in peers] + [(x, y, 1 - c)])

        def ici(i, k, slot):
            px, py = peers[k]
            return pltpu.make_async_remote_copy(src_ref=ins[i].at[c], dst_ref=lands[i].at[slot], send_sem=ici_send.at[3 * i + k],
                                                recv_sem=ici_recv.at[3 * i + k], device_id=(px, py, c), device_id_type=MESH)

        def pass_on(i, k):
            px, py = peers[k]
            slot = 2 * px + py
            return pltpu.make_async_remote_copy(src_ref=lands[i].at[slot], dst_ref=sibs[i].at[slot], send_sem=d2d_send.at[3 * i + k],
                                                recv_sem=d2d_recv.at[3 * i + k], device_id=(x, y, 1 - c), device_id_type=MESH)

        sends = [ici(i, k, mine) for i in range(n) for k in range(3)]
        for cp in sends:
            cp.start()
        passed = []
        for i in range(n):
            for k in range(3):
                px, py = peers[k]
                ici(i, k, 2 * px + py).wait_recv()
                cp = pass_on(i, k)
                cp.start()
                passed.append(cp)
        for cp in passed:
            cp.wait_recv()
        for cp in sends + passed:
            cp.wait_send()

    outs = [_sds((N_SHARD,) + h.shape[1:], h.dtype) for h in halves]
    res = _sequencer_call(body, halves, outs + outs, [3 * n] * 4, name, collective_id)
    return res[:n], res[n:]


def send_other_half(arrays, *, name, collective_id):
    n = len(arrays)

    def body(*refs):
        ins, lands = refs[:n], refs[n:2 * n]
        send_sems, recv_sems = refs[2 * n:]
        x, y, c = lax.axis_index("x"), lax.axis_index("y"), lax.axis_index("c")
        _handshake([(x, y, 1 - c)])
        copies = [pltpu.make_async_remote_copy(src_ref=ins[i].at[1 - c], dst_ref=lands[i], send_sem=send_sems.at[i],
                                               recv_sem=recv_sems.at[i], device_id=(x, y, 1 - c), device_id_type=MESH) for i in range(n)]
        for cp in copies:
            cp.start()
        for cp in copies:
            cp.wait_recv()
        for cp in copies:
            cp.wait_send()

    return _sequencer_call(body, arrays, [_sds(a.shape[1:], a.dtype) for a in arrays], [n, n], name, collective_id)


_HBM = pl.BlockSpec(memory_space=pltpu.HBM)
_SEM = pl.BlockSpec(memory_space=pltpu.SEMAPHORE)
_SPLIT_COPY = dict(has_side_effects=pltpu.SideEffectType.DATAFLOW_SIDE_EFFECTING)


def _chip_copy(ins, lands, send_sems, recv_sems, scatter, i, k, receive):
    x, y, c = lax.axis_index("x"), lax.axis_index("y"), lax.axis_index("c")
    px, py = _xy_peers(x, y)[k]
    theirs, mine = 2 * px + py, 2 * x + y
    src = ins[i].at[theirs] if scatter[i] else ins[i].at[0]
    return pltpu.make_async_remote_copy(src_ref=src, dst_ref=lands[i].at[theirs if receive else mine], send_sem=send_sems.at[3 * i + k],
                                        recv_sem=recv_sems.at[3 * i + k], device_id=(px, py, c), device_id_type=MESH)


def send_to_chips_start(arrays, scatter, *, name):
    n = len(arrays)

    def body(*refs):
        send_sems, recv_sems = refs[2 * n], refs[2 * n + 1]
        ins, lands = refs[2 * n + 2:3 * n + 2], refs[3 * n + 2:4 * n + 2]
        token = refs[4 * n + 2]
        for i in range(n):
            for k in range(3):
                _chip_copy(ins, lands, send_sems, recv_sems, scatter, i, k, receive=False).start()
        token[...] = jnp.zeros_like(token)

    land_shapes = [(N_SHARD,) + a.shape[1:] for a in arrays]
    operands = [pltpu.with_memory_space_constraint(a, pltpu.HBM) for a in arrays]
    operands += [pltpu.with_memory_space_constraint(lax.empty(s, a.dtype), pltpu.HBM) for s, a in zip(land_shapes, arrays)]
    out_shape = ([pltpu.SemaphoreType.DMA((3 * n,)), pltpu.SemaphoreType.DMA((3 * n,))] + [pltpu.HBM(a.shape, a.dtype) for a in arrays]
                 + [pltpu.HBM(s, a.dtype) for s, a in zip(land_shapes, arrays)] + [_sds((8, LANES), F32)])
    res = pl.pallas_call(body, name=name, out_shape=out_shape, in_specs=[_HBM] * (2 * n),
                         out_specs=[_SEM, _SEM] + [_HBM] * (2 * n) + [pl.BlockSpec(memory_space=pltpu.VMEM)],
                         input_output_aliases={i: 2 + i for i in range(2 * n)}, compiler_params=pltpu.CompilerParams(**_SPLIT_COPY))(*operands)
    return (res[0], res[1], res[2:2 + n], res[2 + n:2 + 2 * n], scatter), res[-1]


def send_to_chips_wait(state, after, *, name):
    send_sems, recv_sems, arrays, lands, scatter = state
    n = len(arrays)

    def body(*refs):
        ins, landing = refs[:n], refs[n:2 * n]
        send_sems, recv_sems = refs[2 * n], refs[2 * n + 1]
        for i in range(n):
            for k in range(3):
                _chip_copy(ins, landing, send_sems, recv_sems, scatter, i, k, receive=True).wait_recv()
        for i in range(n):
            for k in range(3):
                _chip_copy(ins, landing, send_sems, recv_sems, scatter, i, k, receive=False).wait_send()

    out_shape = [pltpu.HBM(a.shape, a.dtype) for a in list(arrays) + list(lands)]
    res = pl.pallas_call(body, name=name, out_shape=out_shape, in_specs=[_HBM] * (2 * n) + [_SEM, _SEM] + [_ANY] * len(after),
                         out_specs=[_HBM] * (2 * n), input_output_aliases={i: i for i in range(2 * n)},
                         compiler_params=pltpu.CompilerParams(**_SPLIT_COPY))(*arrays, *lands, send_sems, recv_sems, *after)
    return res[:n], res[n:]


def swap_with_other_core(arrays, *, name, collective_id):
    n = len(arrays)

    def body(*refs):
        ins, lands = refs[:n], refs[n:2 * n]
        send_sems, recv_sems = refs[2 * n:]
        x, y, c = lax.axis_index("x"), lax.axis_index("y"), lax.axis_index("c")
        _handshake([(x, y, 1 - c)])
        copies = [pltpu.make_async_remote_copy(src_ref=ins[i], dst_ref=lands[i], send_sem=send_sems.at[i], recv_sem=recv_sems.at[i],
                                               device_id=(x, y, 1 - c), device_id_type=MESH) for i in range(n)]
        for cp in copies:
            cp.start()
        for cp in copies:
            cp.wait_recv()
        for cp in copies:
            cp.wait_send()

    return _sequencer_call(body, arrays, [_sds(a.shape, a.dtype) for a in arrays], [n, n], name, collective_id)


def _pack_rows(n_elems, row_multiple):
    rows = -(-n_elems // LANES)
    return -(-rows // row_multiple) * row_multiple


def _pack(arrays, rows, dtype):
    flat = jnp.concatenate([a.reshape(-1).astype(dtype) for a in arrays])
    return jnp.pad(flat, (0, rows * LANES - flat.shape[0])).reshape(rows, LANES)


def _unpack(packed, shapes):
    flat = packed.reshape(-1)
    out, off = [], 0
    for s in shapes:
        n = int(np.prod(s))
        out.append(flat[off:off + n].reshape(s))
        off += n
    return out


def all_gather_shards(shards, axes, dtype, row_multiple, tag):
    shapes = [s.shape for s in shards]
    rows = _pack_rows(sum(int(np.prod(s)) for s in shapes), row_multiple)
    packed = _pack(shards, rows, dtype).reshape(2, rows // 2, LANES)
    land = xy_exchange(packed, scatter=False, name=f"gather_xy_{tag}")
    both = core_exchange(land, send_other_half=False, name=f"gather_c_{tag}")
    per_shard = jnp.swapaxes(both, 0, 1).reshape(N_SHARD, rows, LANES)
    pieces = [_unpack(per_shard[s], shapes) for s in range(N_SHARD)]
    return [jnp.concatenate([pieces[s][i] for s in range(N_SHARD)], axis=ax) for i, ax in enumerate(axes)]


def _ordered_before(first, then):
    if then is None:
        return first, None
    return lax.optimization_barrier((first, then))


def reduce_between_cores(arrays, scatter, *, tag, collective_id, before=None):
    arrays, before = _ordered_before(arrays, before)
    land = send_other_half(arrays, name=f"reduce_core_send_{tag}", collective_id=collective_id)
    return (arrays, land, scatter, tag, collective_id), before


def reduce_between_chips(state, before=None):
    arrays, land, scatter, tag, collective_id = state
    chip = [add_core_halves(a, l, out_dtype=BF16 if sc else F32, name=f"reduce_core_add_{tag}_{i}")
            for i, (a, l, sc) in enumerate(zip(arrays, land, scatter))]
    sending, token = send_to_chips_start(chip, scatter, name=f"reduce_chip_start_{tag}")
    token, before = _ordered_before(token, before)
    return (sending, token, scatter, tag, collective_id), before


def reduce_finish(state, after):
    sending, token, scatter, tag, collective_id = state
    chip, land = send_to_chips_wait(sending, tuple(after) + (token,), name=f"reduce_chip_wait_{tag}")
    own = [sum_over_chips(ch, l, scatter=sc, name=f"reduce_chip_add_{tag}_{i}") for i, (ch, l, sc) in enumerate(zip(chip, land, scatter))]
    sib = swap_with_other_core(own, name=f"reduce_core_swap_{tag}", collective_id=collective_id + 2)
    return own, sib


def _ffn_layer_fwd(h, norm_g, w_up, cw, cb, w_down, tag):
    hn = norm_fwd(h, norm_g, name=f"ffn_norm_{tag}")
    u = matmul(hn, w_up, name=f"ffn_up_{tag}")
    act = ffn_act_fwd(u, cw, cb, name=f"ffn_act_{tag}")
    out = matmul(act, w_down, add=h, name=f"ffn_down_{tag}")
    return out, (h, hn, u, act)


def _travel_layout(array):
    return BIG_ARRAYS[array][3], BIG_ARRAYS[array][4]


def _ffn_layer_bwd(saved, dout, norm_g, w_up, cw, cb, w_down, tag, d_w_down_other=None):
    h, hn, u, act = saved
    dact = matmul(dout, w_down, tb=True, name=f"ffn_down_dx_{tag}")
    d_w_down = matmul(act, dout, ta=True, layer=(int(tag), 2, d_w_down_other), name=f"ffn_down_dw_{tag}")
    dug, duv, dcw, dcb = ffn_act_bwd(u, cw, cb, dact, name=f"ffn_act_bwd_{tag}")
    du = jnp.concatenate([dug, duv], axis=1)
    dhn = matmul(du, w_up, tb=True, name=f"ffn_up_dx_{tag}")
    d_w_up = matmul(hn, du, ta=True, split=_travel_layout(f"ffn_w_up_{tag}"), name=f"ffn_up_dw_{tag}")
    dh, dg = norm_bwd(h, norm_g, dhn, dout, name=f"ffn_norm_bwd_{tag}")
    return dh, dg, d_w_up, dcw, dcb, d_w_down


def local_step(x, target, w, stage=lambda name, tensors, grads=None: tensors):
    g = {}
    tables = _ret_tables()
    x = stage("start", x)
    w_in_t = w["ret_gdn_w_in"]
    w_main = w_in_t[:MIX_MAIN]
    w_small = jnp.pad(w_in_t[MIX_MAIN:], ((0, LANES - 2 * N_HEADS), (0, 0)))
    a_log = jnp.pad(w["gdn_a_log"], ((0, 0), (0, LANES - N_HEADS)))
    dt_bias = jnp.pad(w["gdn_dt_bias"], ((0, 0), (0, LANES - N_HEADS)))

    hn0 = stage("normed", norm_fwd(x, w["norm_mix"][0:1], name="mix0_norm"))
    p = matmul(hn0, w_main, tb=True, name="mix0_in")
    small = matmul(hn0, w_small, tb=True, name="mix0_in_small")
    y_ret, s_ret = ret_fwd(p, tables, name="ret_fwd")
    conv = gdn_conv_fwd(p, w["gdn_conv_w"], name="gdn_conv")
    y_gdn, s_gdn = gdn_fwd(conv, p, small, a_log, dt_bias, w["gdn_out_gain"], name="gdn_fwd")
    y0 = stage("mixed", jnp.concatenate([y_ret, y_gdn], axis=1))
    h1 = matmul(y0, w["ret_gdn_w_out"], add=x, name="mix0_out")
    h2, ffn0 = _ffn_layer_fwd(h1, w["norm_ffn"][0:1], w["ffn_w_up"][0], w["ffn_conv_w"][0], w["ffn_conv_b"][0:1], w["ffn_w_down"][0], "0")
    h2 = stage("layer0", h2)

    hn1 = norm_fwd(h2, w["norm_mix"][1:2], name="mix1_norm")
    gx = matmul(hn1, w["lru_w_in"], name="mix1_in")
    lru_p = (w["lru_conv_w"], w["lru_conv_b"], w["lru_w_a"], w["lru_b_a"], w["lru_w_x"], w["lru_b_x"], w["lru_lambda"])
    y1 = lru_fwd(gx, *lru_p, name="lru_fwd")
    h3 = matmul(y1, w["lru_w_out"], add=h2, name="mix1_out")
    h4, ffn1 = _ffn_layer_fwd(h3, w["norm_ffn"][1:2], w["ffn_w_up"][1], w["ffn_conv_w"][1], w["ffn_conv_b"][1:2], w["ffn_w_down"][1], "1")

    loss, dh4, g["norm_final"] = final_fwd_bwd(h4, w["norm_final"], target, name="final")

    dh3, dgf1, dwu1, dcw1, dcb1, dwd1 = _ffn_layer_bwd(ffn1, dh4, w["norm_ffn"][1:2], w["ffn_w_up"][1], w["ffn_conv_w"][1],
                                                     w["ffn_conv_b"][1:2], w["ffn_w_down"][1], "1")
    g["ffn_w_up_1"] = dwu1
    dh3 = stage("grads0_ready", dh3, g)
    dy1 = matmul(dh3, w["lru_w_out"], tb=True, name="mix1_out_dx")
    g["lru_w_out"] = matmul(y1, dh3, ta=True, split=_travel_layout("lru_w_out"), name="mix1_out_dw")
    dgate, dxr, g["lru_conv_w"], g["lru_conv_b"], g["lru_w_a"], g["lru_b_a"], g["lru_w_x"], g["lru_b_x"], g["lru_lambda"] = lru_bwd(
        gx, *lru_p, dy1, name="lru_bwd")
    dgx = stage("grads0_send", jnp.concatenate([dgate, dxr], axis=1), g)
    dhn1 = matmul(dgx, w["lru_w_in"], tb=True, name="mix1_in_dx")
    g["lru_w_in"] = matmul(hn1, dgx, ta=True, split=_travel_layout("lru_w_in"), name="mix1_in_dw")
    dh2, dgm1 = norm_bwd(h2, w["norm_mix"][1:2], dhn1, dh3, name="mix1_norm_bwd")
    dh2 = stage("grads1_ready", dh2, g)

    dh1, dgf0, dwu0, dcw0, dcb0, dwd0 = _ffn_layer_bwd(ffn0, dh2, w["norm_ffn"][0:1], w["ffn_w_up"][0], w["ffn_conv_w"][0],
                                                     w["ffn_conv_b"][0:1], w["ffn_w_down"][0], "0", dwd1)
    g["ffn_w_up_0"] = dwu0
    g["ffn_w_down"] = dwd0
    dh1 = stage("grads2_ready", stage("grads1_send", dh1, g), g)
    dy0 = matmul(dh1, w["ret_gdn_w_out"], tb=True, name="mix0_out_dx")
    g["ret_gdn_w_out"] = matmul(y0, dh1, ta=True, split=_travel_layout("ret_gdn_w_out"), name="mix0_out_dw")
    dq_r, dk_r, dv_r, dg_r = ret_bwd(p, tables, s_ret, dy0, name="ret_bwd")
    dy0, dq_r = stage("grads2_send", (dy0, dq_r), g)
    dcq, dck, dcv, dg_d, dsmall, dal, ddt, dgain = gdn_bwd(conv, p, small, a_log, dt_bias, w["gdn_out_gain"], s_gdn, dy0, name="gdn_bwd")
    dconv = jnp.concatenate([dcq, dck, dcv], axis=1)
    dp_conv, g["gdn_conv_w"] = gdn_conv_bwd(p, w["gdn_conv_w"], dconv, name="gdn_conv_bwd")
    dp = jnp.concatenate([dq_r, dk_r, dv_r, dg_r, dp_conv, dg_d], axis=1)
    dhn0 = matmul(dp, w_main, name="mix0_in_dx")
    dhn0 = matmul(dsmall, w_small, add=dhn0, name="mix0_in_small_dx")
    d_w_main = matmul(dp, hn0, ta=True, name="mix0_in_dw")
    d_w_small = matmul(dsmall, hn0, ta=True, name="mix0_in_small_dw")
    g["ret_gdn_w_in"] = jnp.concatenate([d_w_main, d_w_small[:2 * N_HEADS]], axis=0)
    dx, dgm0 = norm_bwd(x, w["norm_mix"][0:1], dhn0, dh1, name="mix0_norm_bwd")

    g["gdn_a_log"] = dal[:, :N_HEADS]
    g["gdn_dt_bias"] = ddt[:, :N_HEADS]
    g["gdn_out_gain"] = dgain
    g["norm_mix"] = jnp.concatenate([dgm0, dgm1], axis=0)
    g["norm_ffn"] = jnp.concatenate([dgf0, dgf1], axis=0)
    g["ffn_conv_w"] = jnp.stack([dcw0, dcw1])
    g["ffn_conv_b"] = jnp.concatenate([dcb0, dcb1], axis=0)
    return loss, dx, g


WEIGHTS = ("norm_mix", "norm_ffn", "ret_gdn_w_in", "gdn_conv_w", "gdn_a_log", "gdn_dt_bias", "gdn_out_gain", "ret_gdn_w_out",
           "lru_w_in", "lru_conv_w", "lru_conv_b", "lru_w_a", "lru_b_a", "lru_w_x", "lru_b_x", "lru_lambda", "lru_w_out",
           "ffn_w_up", "ffn_conv_w", "ffn_conv_b", "ffn_w_down", "norm_final")
MATMUL_SHARDED = {"ret_gdn_w_in": 1, "ret_gdn_w_out": 0, "lru_w_in": 1, "lru_w_out": 0, "ffn_w_up": 2, "ffn_w_down": 1}
VECTOR_SHARDED = {"gdn_conv_w": 1, "lru_conv_w": 1, "lru_conv_b": 1, "lru_b_a": 1, "lru_b_x": 1, "lru_lambda": 1, "ffn_conv_w": 2}
SHARDED = {**MATMUL_SHARDED, **VECTOR_SHARDED}
REPLICATED = tuple(n for n in WEIGHTS if n not in SHARDED)
SQUEEZE = {"ret_gdn_w_in", "gdn_conv_w", "ret_gdn_w_out", "lru_w_in", "lru_conv_w", "lru_w_a", "lru_w_x", "lru_w_out"}
MIX_IN = MIX_MAIN + 2 * N_HEADS
BIG_ARRAYS = {
    "ret_gdn_w_in": ("ret_gdn_w_in", None, (MIX_IN, D_MODEL), (N_SHARD, MIX_IN // N_SHARD, 2, D_MODEL // 2), (2, 0, 1, 3)),
    "ret_gdn_w_out": ("ret_gdn_w_out", None, (2 * GROUP, D_MODEL), (N_SHARD, 2, GROUP // N_SHARD, D_MODEL), (1, 0, 2, 3)),
    "lru_w_in": ("lru_w_in", None, (D_MODEL, 2 * D_MODEL), (2, D_MODEL // 2, N_SHARD, 2 * D_MODEL // N_SHARD), (0, 2, 1, 3)),
    "lru_w_out": ("lru_w_out", None, (D_MODEL, D_MODEL), (N_SHARD, 2, D_MODEL // (2 * N_SHARD), D_MODEL), (1, 0, 2, 3)),
    "ffn_w_up_0": ("ffn_w_up", 0, (D_MODEL, 2 * D_FF), (2, D_MODEL // 2, N_SHARD, 2 * D_FF // N_SHARD), (0, 2, 1, 3)),
    "ffn_w_up_1": ("ffn_w_up", 1, (D_MODEL, 2 * D_FF), (2, D_MODEL // 2, N_SHARD, 2 * D_FF // N_SHARD), (0, 2, 1, 3)),
    "ffn_w_down": ("ffn_w_down", None, (2, D_FF, D_MODEL), (2, N_SHARD, D_FF // N_SHARD, D_MODEL), (0, 1, 2, 3)),
}
GATHER_GROUPS = (("ret_gdn_w_in",), ("ret_gdn_w_out", "ffn_w_up_0", "ffn_w_down"), ("lru_w_in", "lru_w_out", "ffn_w_up_1"))
REDUCE_GROUPS = (("ffn_w_up_1",), ("lru_w_in", "lru_w_out"), ("ffn_w_up_0", "ffn_w_down"), ("ret_gdn_w_out", "ret_gdn_w_in"))
BLOCK_WEIGHTS = ("lru_w_a", "lru_w_x")
GATHER_COLLECTIVE_ID = 1
REDUCE_COLLECTIVE_ID = GATHER_COLLECTIVE_ID + len(GATHER_GROUPS)


TRANSPOSED = ("ret_gdn_w_in",)


def _shard_of(array, tensors):
    weight, layer = BIG_ARRAYS[array][:2]
    t = tensors[weight]
    if weight in TRANSPOSED:
        return jnp.swapaxes(t, 1, 2)[0]
    return _local_view(weight, t) if layer is None else t[layer]


def _core_halves(array, shard):
    _, _, _, split, perm = BIG_ARRAYS[array]
    kept = [k for k in range(4) if k != perm[1]]
    order = [kept.index(perm[0]), kept.index(perm[2]), kept.index(perm[3])]
    return shard.reshape([split[k] for k in kept]).transpose(order)


def _local_view(name, a):
    if name in SQUEEZE:
        return a[0]
    if a.ndim == 1:
        return a[None, :]
    return a


def kernel(x, norm_mix, norm_ffn, ret_gdn_w_in, gdn_conv_w, gdn_a_log, gdn_dt_bias, gdn_out_gain, ret_gdn_w_out, lru_w_in, lru_conv_w, lru_conv_b, lru_w_a, lru_b_a, lru_w_x, lru_b_x, lru_lambda, lru_w_out, ffn_w_up, ffn_conv_w, ffn_conv_b, ffn_w_down, norm_final, loss_target, m_norm_mix, m_norm_ffn, m_ret_gdn_w_in, m_gdn_conv_w, m_gdn_a_log, m_gdn_dt_bias, m_gdn_out_gain, m_ret_gdn_w_out, m_lru_w_in, m_lru_conv_w, m_lru_conv_b, m_lru_w_a, m_lru_b_a, m_lru_w_x, m_lru_b_x, m_lru_lambda, m_lru_w_out, m_ffn_w_up, m_ffn_conv_w, m_ffn_conv_b, m_ffn_w_down, m_norm_final, v_norm_mix, v_norm_ffn, v_ret_gdn_w_in, v_gdn_conv_w, v_gdn_a_log, v_gdn_dt_bias, v_gdn_out_gain, v_ret_gdn_w_out, v_lru_w_in, v_lru_conv_w, v_lru_conv_b, v_lru_w_a, v_lru_b_a, v_lru_w_x, v_lru_b_x, v_lru_lambda, v_lru_w_out, v_ffn_w_up, v_ffn_conv_w, v_ffn_conv_b, v_ffn_w_down, v_norm_final):
    given = dict(norm_mix=norm_mix, norm_ffn=norm_ffn, ret_gdn_w_in=ret_gdn_w_in, gdn_conv_w=gdn_conv_w, gdn_a_log=gdn_a_log, gdn_dt_bias=gdn_dt_bias, gdn_out_gain=gdn_out_gain, ret_gdn_w_out=ret_gdn_w_out, lru_w_in=lru_w_in, lru_conv_w=lru_conv_w, lru_conv_b=lru_conv_b, lru_w_a=lru_w_a, lru_b_a=lru_b_a, lru_w_x=lru_w_x, lru_b_x=lru_b_x, lru_lambda=lru_lambda, lru_w_out=lru_w_out, ffn_w_up=ffn_w_up, ffn_conv_w=ffn_conv_w, ffn_conv_b=ffn_conv_b, ffn_w_down=ffn_w_down, norm_final=norm_final)
    mom1 = dict(norm_mix=m_norm_mix, norm_ffn=m_norm_ffn, ret_gdn_w_in=m_ret_gdn_w_in, gdn_conv_w=m_gdn_conv_w, gdn_a_log=m_gdn_a_log, gdn_dt_bias=m_gdn_dt_bias, gdn_out_gain=m_gdn_out_gain, ret_gdn_w_out=m_ret_gdn_w_out, lru_w_in=m_lru_w_in, lru_conv_w=m_lru_conv_w, lru_conv_b=m_lru_conv_b, lru_w_a=m_lru_w_a, lru_b_a=m_lru_b_a, lru_w_x=m_lru_w_x, lru_b_x=m_lru_b_x, lru_lambda=m_lru_lambda, lru_w_out=m_lru_w_out, ffn_w_up=m_ffn_w_up, ffn_conv_w=m_ffn_conv_w, ffn_conv_b=m_ffn_conv_b, ffn_w_down=m_ffn_w_down, norm_final=m_norm_final)
    mom2 = dict(norm_mix=v_norm_mix, norm_ffn=v_norm_ffn, ret_gdn_w_in=v_ret_gdn_w_in, gdn_conv_w=v_gdn_conv_w, gdn_a_log=v_gdn_a_log, gdn_dt_bias=v_gdn_dt_bias, gdn_out_gain=v_gdn_out_gain, ret_gdn_w_out=v_ret_gdn_w_out, lru_w_in=v_lru_w_in, lru_conv_w=v_lru_conv_w, lru_conv_b=v_lru_conv_b, lru_w_a=v_lru_w_a, lru_b_a=v_lru_b_a, lru_w_x=v_lru_w_x, lru_b_x=v_lru_b_x, lru_lambda=v_lru_lambda, lru_w_out=v_lru_w_out, ffn_w_up=v_ffn_w_up, ffn_conv_w=v_ffn_conv_w, ffn_conv_b=v_ffn_conv_b, ffn_w_down=v_ffn_w_down, norm_final=v_norm_final)

    local = {n: _local_view(n, a) for n, a in given.items()}

    core = lax.axis_index("c")
    chip = 2 * lax.axis_index("x") + lax.axis_index("y")
    is_my_chip = lax.broadcasted_iota(jnp.int32, (N_SHARD, 1, 1), 0) == chip

    def by_core(mine, other):
        return jnp.where(core == 0, jnp.stack([mine, other]), jnp.stack([other, mine]))

    vec_names, rp_names = list(VECTOR_SHARDED), list(REPLICATED)
    full = dict(zip(vec_names, all_gather_shards([local[n] for n in vec_names], [SHARDED[n] for n in vec_names], F32, 32, "p")))
    for n in rp_names:
        full[n] = local[n]
    in_flight = {}

    def launch(gi, after=None):
        halves = []
        for a in GATHER_GROUPS[gi]:
            halves.append(_core_halves(a, _shard_of(a, given).astype(BF16)))
        if after is not None:
            halves, after = lax.optimization_barrier((halves, after))
        in_flight[gi] = (halves,) + gather_halves(halves, name=f"gather_weights_{gi}", collective_id=GATHER_COLLECTIVE_ID + gi)
        return after

    def land(gi, after):
        halves, lands, sibs = in_flight[gi]
        (lands, sibs), after = lax.optimization_barrier(((lands, sibs), after))
        for a, mine, got, passed in zip(GATHER_GROUPS[gi], halves, lands, sibs):
            weight, layer, full_shape, split, perm = BIG_ARRAYS[a]
            half_mine = jnp.where(is_my_chip, jnp.where(core == 0, mine[0], mine[1])[None], got)
            half_other = jnp.where(is_my_chip, jnp.where(core == 0, mine[1], mine[0])[None], passed)
            value = by_core(half_mine, half_other).transpose(tuple(np.argsort(perm))).reshape(full_shape)
            if layer is None:
                full[weight] = value
            else:
                full.setdefault(weight, [None, None])[layer] = value
        return after

    reducing = {}

    def reduce_ready(gi, grads, then=None, extra=()):
        def travelling(a):
            split, perm = _travel_layout(a)
            return grads[a] if grads[a].ndim == 4 else grads[a].reshape(split).transpose(perm)

        arrays = [travelling(a) for a in REDUCE_GROUPS[gi]] + list(extra)
        scatter = [True] * len(REDUCE_GROUPS[gi]) + [False] * len(extra)
        reducing[gi], then = reduce_between_cores(arrays, scatter, tag=str(gi), collective_id=REDUCE_COLLECTIVE_ID + 3 * gi, before=then)
        return then

    def reduce_send(gi, then=None):
        reducing[gi], then = reduce_between_chips(reducing[gi], before=then)
        return then

    def stage(name, tensors, grads=None):
        if name == "start":
            launch(0)
            launch(1)
            packed["wmv"], tensors = lax.optimization_barrier((packed["wmv"], tensors))
            return land(0, tensors)
        if name == "normed":
            return launch(2, tensors)
        if name in ("mixed", "layer0"):
            return land({"mixed": 1, "layer0": 2}[name], tensors)
        gi = int(name[len("grads")])
        return reduce_ready(gi, grads, tensors) if name.endswith("_ready") else reduce_send(gi, tensors)

    small_names = [n for n in rp_names if n not in BLOCK_WEIGHTS] + vec_names
    loc_shapes = [local[n].shape for n in small_names]
    loc_rows = _pack_rows(sum(int(np.prod(s)) for s in loc_shapes), 256)
    packed = {"wmv": [_pack([src[n] for n in small_names], loc_rows, F32) for src in (given, mom1, mom2)]}

    loss_part, dx, grads = local_step(x[0], loss_target[0], full, stage)
    small_shapes = [grads[n].shape for n in small_names] + [(1, 1)]
    small_rows = _pack_rows(sum(int(np.prod(s)) for s in small_shapes), 16)
    small = _pack([grads[n] for n in small_names] + [loss_part[:, :1]], small_rows, F32).reshape(2, 1, small_rows // 2, LANES)
    last = len(REDUCE_GROUPS) - 1
    halves_of_blocks = [grads[n].reshape(2, 1, LRU_BLOCKS * HEAD // 2, HEAD) for n in BLOCK_WEIGHTS]
    reduce_ready(last, grads, extra=[small] + halves_of_blocks)
    reduce_send(last)
    reduced, result = {}, {}

    def finish(gi, after):
        g_own, g_sib = reduce_finish(reducing[gi], after)
        reduced.update(zip(list(REDUCE_GROUPS[gi]) + ["small"] + list(BLOCK_WEIGHTS), zip(g_own, g_sib)))

    def update(n):
        if n in TRANSPOSED:
            w3, m3, v3 = (jnp.swapaxes(t, 1, 2) for t in (given[n], mom1[n], mom2[n]))
            result[n] = tuple(jnp.swapaxes(t, 1, 2) for t in adamw_column_halves(w3, m3, v3, *reduced[n], name=f"adamw_{n}"))
            return
        done = None
        for a in (k for k, spec in BIG_ARRAYS.items() if spec[0] == n):
            r, cols = reduced[a][0].shape
            layer = BIG_ARRAYS[a][1] or 0
            w3, m3, v3 = (t if BIG_ARRAYS[a][1] is not None else t.reshape(1, 2 * r, cols) for t in (given[n], mom1[n], mom2[n]))
            done = adamw_halves(w3, m3, v3, *reduced[a], layer=layer, prev=done, name=f"adamw_{a}")
        result[n] = done

    for gi in range(last):
        finish(gi, (dx, reducing[last][1]))
    late = {BIG_ARRAYS[a][0] for a in REDUCE_GROUPS[last]}
    for n in MATMUL_SHARDED:
        if n not in late:
            update(n)
    finish(last, tuple(result[n][0] for n in MATMUL_SHARDED if n not in late))
    for n in MATMUL_SHARDED:
        if n in late:
            update(n)

    for n in BLOCK_WEIGHTS:
        w3, m3, v3 = (t.reshape(1, LRU_BLOCKS * HEAD, HEAD) for t in (given[n], mom1[n], mom2[n]))
        result[n] = adamw_halves(w3, m3, v3, *reduced[n], name=f"adamw_{n}")

    *small_sums, loss_sum = _unpack(by_core(*reduced["small"]).reshape(small_rows, LANES), small_shapes)
    loss = loss_sum[0, 0]
    g_small = dict(zip(small_names, small_sums))
    for n in vec_names:
        size = local[n].shape[SHARDED[n]]
        g_small[n] = lax.dynamic_slice_in_dim(g_small[n], chip * size, size, axis=SHARDED[n])
    w_pack, m_pack, v_pack = packed["wmv"]
    d_s, m_s, v_s = adamw(w_pack, _pack([g_small[n] for n in small_names], loc_rows, F32), m_pack, v_pack, name="adamw_small")
    for n, d, nm, nv in zip(small_names, _unpack(d_s, loc_shapes), _unpack(m_s, loc_shapes), _unpack(v_s, loc_shapes)):
        result[n] = (g_small[n], d, nm, nv)

    outs = [[result[n][k].reshape(given[n].shape) for n in WEIGHTS] for k in range(4)]
    return (loss, dx[None], *outs[0], *outs[1], *outs[2], *outs[3])
```

```python
import functools

import numpy as np
import jax
import jax.numpy as jnp
from jax import lax
from jax.experimental import pallas as pl
from jax.experimental.pallas import tpu as pltpu
from jax.experimental.pallas import tpu_sc as plsc

F32 = jnp.float32
BF16 = jnp.bfloat16
HI = lax.Precision.HIGHEST
MESH = pl.DeviceIdType.MESH

SEQ = 2048
D_MODEL = 1024
N_HEADS = 4
HEAD = 128
RET_CHUNK = 128
GDN_CHUNK = 64
GDN_CHUNKS_PER_STEP = 4
GROUP = N_HEADS * HEAD
MIX_MAIN = 8 * GROUP
D_FF = 2816
LRU_BLOCKS = 8
LRU_C = 8.0
ROPE_BASE = 10000.0
EPS = 1e-6
N_SHARD = 4
LANES = 128

ADAM_LR, ADAM_B1, ADAM_B2, ADAM_EPS, ADAM_WD, ADAM_STEP = 0.001, 0.9, 0.999, 1e-08, 0.01, 10

VMEM_LIMIT_BYTES = 56 * 1024 * 1024

_roll = pltpu.roll


def _params(**kw):
    return pltpu.CompilerParams(vmem_limit_bytes=VMEM_LIMIT_BYTES, **kw)


def _sds(shape, dtype):
    return jax.ShapeDtypeStruct(tuple(shape), dtype)


def _shift_raw(x, d):
    n = x.shape[0]
    t = lax.broadcasted_iota(jnp.int32, x.shape, 0)
    if d > 0:
        return jnp.where(t >= d, _roll(x, d, 0), 0.0)
    return jnp.where(t < n + d, _roll(x, n + d, 0), 0.0)


@functools.partial(jax.custom_vjp, nondiff_argnums=(1,))
def shift_rows(x, d):
    return _shift_raw(x, d)


def _shift_fwd(x, d):
    return _shift_raw(x, d), None


def _shift_bwd(d, _, g):
    return (_shift_raw(g, -d),)


shift_rows.defvjp(_shift_fwd, _shift_bwd)


@jax.custom_vjp
def swap_halves(x):
    return _roll(x, HEAD // 2, 1)


def _swap_fwd(x):
    return _roll(x, HEAD // 2, 1), None


def _swap_bwd(_, g):
    return (_roll(g, HEAD // 2, 1),)


swap_halves.defvjp(_swap_fwd, _swap_bwd)


def _scan_raw(a, u, reverse):
    n = a.shape[0]
    t = lax.broadcasted_iota(jnp.int32, a.shape, 0)
    d = 1
    while d < n:
        if reverse:
            m = t < n - d
            a_s, u_s = _roll(a, n - d, 0), _roll(u, n - d, 0)
        else:
            m = t >= d
            a_s, u_s = _roll(a, d, 0), _roll(u, d, 0)
        u = a * jnp.where(m, u_s, 0.0) + u
        a = a * jnp.where(m, a_s, 1.0)
        d *= 2
    return u


@jax.custom_vjp
def lin_scan(a, u):
    return _scan_raw(a, u, False)


def _lin_scan_fwd(a, u):
    hs = _scan_raw(a, u, False)
    return hs, (a, hs)


def _lin_scan_bwd(res, g):
    a, hs = res
    lam = _scan_raw(_shift_raw(a, -1), g, True)
    return lam * _shift_raw(hs, 1), lam


lin_scan.defvjp(_lin_scan_fwd, _lin_scan_bwd)


def _bdot(a, b, dims=(((1,), (0,)), ((), ()))):
    return lax.dot_general(a.astype(BF16), b.astype(BF16), dims, preferred_element_type=F32)


def _each(f, *seqs):
    return tuple(f(*a) for a in zip(*seqs))


def _split_bf16(a):
    hi = a.astype(BF16)
    return hi, (a - hi.astype(F32)).astype(BF16)


def _dot3_raw(a_s, b_s):
    a_hl = _each(_split_bf16, a_s)
    b_hl = _each(_split_bf16, b_s)
    hh = _each(lambda a, b: _bdot(a[0], b[0]), a_hl, b_hl)
    hl = _each(lambda a, b: _bdot(a[0], b[1]), a_hl, b_hl)
    lh = _each(lambda a, b: _bdot(a[1], b[0]), a_hl, b_hl)
    return _each(lambda x, y, z: x + (y + z), hh, hl, lh)


@jax.custom_vjp
def dot3(a_s, b_s):
    return _dot3_raw(a_s, b_s)


def _dot3_fwd(a_s, b_s):
    return _dot3_raw(a_s, b_s), (a_s, b_s)


def _dot3_bwd(res, g_s):
    a_s, b_s = res
    return (_each(lambda g, b: _bdot(g, b, (((1,), (1,)), ((), ()))), g_s, b_s),
            _each(lambda a, g: _bdot(a, g, (((0,), (0,)), ((), ()))), a_s, g_s))


dot3.defvjp(_dot3_fwd, _dot3_bwd)


def _eye(n):
    i = lax.broadcasted_iota(jnp.int32, (n, n), 0)
    j = lax.broadcasted_iota(jnp.int32, (n, n), 1)
    return (i == j).astype(F32)


def _unit_lower_inverse_raw(lmats):
    n = lmats[0].shape[0]
    eye = _eye(n)
    ps = _each(lambda l: -l, lmats)
    invs = _each(lambda x: eye + x, ps)
    k = 1
    while 2 * k < n:
        ps = _each(lambda p: _bdot(p, p), ps)
        invs = _each(lambda inv, p: inv + _bdot(inv, p), invs, ps)
        k *= 2
    prods = _dot3_raw(lmats, invs)
    resids = _each(lambda inv, pr: eye - inv - pr, invs, prods)
    return _each(lambda inv, r: inv + _bdot(inv, r), invs, resids)


@jax.custom_vjp
def unit_lower_inverse(lmats):
    return _unit_lower_inverse_raw(lmats)


def _uli_fwd(lmats):
    invs = _unit_lower_inverse_raw(lmats)
    return invs, invs


def _uli_bwd(invs, g_s):
    ms = _each(lambda inv, g: _bdot(inv, g, (((0,), (0,)), ((), ()))), invs, g_s)
    return (_each(lambda m, inv: -_bdot(m, inv, (((1,), (1,)), ((), ()))), ms, invs),)


unit_lower_inverse.defvjp(_uli_fwd, _uli_bwd)


def _cumsum_raw(x, reverse):
    n = x.shape[0]
    t = lax.broadcasted_iota(jnp.int32, x.shape, 0)
    d = 1
    while d < n:
        if reverse:
            x = x + jnp.where(t < n - d, _roll(x, n - d, 0), 0.0)
        else:
            x = x + jnp.where(t >= d, _roll(x, d, 0), 0.0)
        d *= 2
    return x


@jax.custom_vjp
def cumsum_rows(x):
    return _cumsum_raw(x, False)


def _cumsum_fwd(x):
    return _cumsum_raw(x, False), None


def _cumsum_bwd(_, g):
    return (_cumsum_raw(g, True),)


cumsum_rows.defvjp(_cumsum_fwd, _cumsum_bwd)


_NT = (((1,), (1,)), ((), ()))
_TN = (((0,), (0,)), ((), ()))


def _softplus(x):
    return jnp.maximum(x, 0.0) + jnp.log1p(jnp.exp(-jnp.abs(x)))


def _expm1_nonpos(x):
    poly = x * (1.0 + x * (0.5 + x * (1.0 / 6 + x * (1.0 / 24 + x * (1.0 / 120 + x * (1.0 / 720))))))
    return jnp.where(x > -0.25, poly, jnp.exp(x) - 1.0)


def _rms(x):
    return x * lax.rsqrt(jnp.mean(x * x, axis=-1, keepdims=True) + EPS)


def _causal_conv(x, w, width):
    y = w[width - 1:width, :] * x
    for j in range(width - 1):
        y = y + w[j:j + 1, :] * shift_rows(x, width - 1 - j)
    return y


def _norm_fn(x, g):
    return _rms(x) * g


def _ffn_act_fn(ug, uv, wg, wv, bg, bv):
    return jax.nn.silu(_causal_conv(ug, wg, 3) + bg) * (_causal_conv(uv, wv, 3) + bv)


def _gdn_conv_fn(x, w):
    return jax.nn.silu(_causal_conv(x, w, 4))


def _lru_fn(gate, x, cw, cb, wa, ba, wx, bx, lam):
    xr = _causal_conv(x, cw, 4) + cb
    r = jax.nn.sigmoid(_bdot(xr, wa) + ba)
    i = jax.nn.sigmoid(_bdot(xr, wx) + bx)
    log_a = -LRU_C * r * _softplus(-lam)
    a = jnp.exp(log_a)
    u = jnp.sqrt(-_expm1_nonpos(2.0 * log_a)) * (i * xr)
    hs = lin_scan(a, u)
    return jax.nn.gelu(gate) * hs


def _ret_fn(qs, ks, vs, gates, states, cos2, sin2, dmasks, ktails, qdecs, cdecs):
    qrs = _each(lambda q: q * cos2 + swap_halves(q) * sin2, qs)
    krs = _each(lambda k: (k * cos2 + swap_halves(k) * sin2) * (HEAD ** -0.5), ks)
    scores = _each(lambda q, k, m: _bdot(q, k, _NT) * m, qrs, krs, dmasks)
    inter = _each(lambda q, d, s: _bdot(q * d, s), qrs, qdecs, states)
    os_ = _each(lambda sc, v, x: _bdot(sc, v) + x, scores, vs, inter)
    new_states = _each(lambda s, cd, k, kt, v: s * cd + _bdot(k * kt, v, _TN), states, cdecs, krs, ktails, vs)
    ys = _each(lambda o, g: _rms(o) * jax.nn.silu(g), os_, gates)
    return ys, new_states


def _pick_lane(x, lane_idx):
    lane = lax.broadcasted_iota(jnp.int32, x.shape, 1)
    return jnp.sum(jnp.where(lane == lane_idx, x, 0.0), axis=1, keepdims=True)


def _l2norm(x):
    return x * lax.rsqrt(jnp.sum(x * x, axis=-1, keepdims=True) + EPS)


def _gdn_fn(qcs, kcs, vcs, gates, small, a_log, dt_bias, gain, states):
    c = GDN_CHUNK
    n_heads = len(qcs)
    n_chunks = qcs[0].shape[0] // c
    units = tuple((ci, h) for ci in range(n_chunks) for h in range(n_heads))

    def unit_rows(per_head):
        return tuple(per_head[h][ci * c:(ci + 1) * c] for ci, h in units)

    smalls = tuple(small[ci * c:(ci + 1) * c] for ci, _ in units)
    heads = tuple(h for _, h in units)
    intra = _gdn_intra(unit_rows(qcs), unit_rows(kcs), unit_rows(vcs), smalls, heads, a_log, dt_bias)
    outs = []
    for ci in range(n_chunks):
        mine = slice(ci * n_heads, (ci + 1) * n_heads)
        os_, states = _gdn_inter(*(part[mine] for part in intra), states)
        outs.append(os_)
    ys = tuple(_rms(jnp.concatenate([outs[ci][h] for ci in range(n_chunks)], axis=0)) * gain * jax.nn.silu(gates[h])
               for h in range(n_heads))
    return ys, states


def _gdn_inter(qs, ks, us, ws, attns, gcs, g_lasts, states):
    v_news = _each(lambda u, w, s: u - _bdot(w, s), us, ws, states)
    inter = _each(lambda q, gc, s: _bdot(q * jnp.exp(gc), s), qs, gcs, states)
    os_ = _each(lambda x, a, v: x + _bdot(a, v), inter, attns, v_news)
    new_states = _each(lambda s, gl, k, gc, v: s * jnp.exp(gl) + _bdot(k * jnp.exp(gl - gc), v, _TN), states, g_lasts, ks, gcs, v_news)
    return os_, new_states


def _gdn_intra(qcs, kcs, vcs, smalls, heads, a_log, dt_bias):
    c = GDN_CHUNK
    qs = _each(lambda x: _l2norm(x) * (HEAD ** -0.5), qcs)
    ks = _each(_l2norm, kcs)
    betas = _each(lambda sm, h: jax.nn.sigmoid(_pick_lane(sm, h)), smalls, heads)
    gs = _each(lambda sm, h: -jnp.exp(_pick_lane(a_log, h)) * _softplus(_pick_lane(sm, h + N_HEADS) + _pick_lane(dt_bias, h)),
               smalls, heads)
    i = lax.broadcasted_iota(jnp.int32, (c, c), 0)
    j = lax.broadcasted_iota(jnp.int32, (c, c), 1)
    tril = i >= j
    gcs = _each(lambda g: cumsum_rows(jnp.broadcast_to(g, (c, LANES)))[:, :1], gs)
    gc_rows = _each(lambda gc: jnp.broadcast_to(gc, (c, c)), gcs)
    decays = _each(lambda r: jnp.where(tril, jnp.exp(jnp.where(tril, r - r.T, 0.0)), 0.0), gc_rows)
    kbs = _each(lambda k, b: k * b, ks, betas)
    lmats = _each(lambda kb, k, d: jnp.where(i > j, _bdot(kb, k, _NT) * d, 0.0), kbs, ks, decays)
    attns = _each(lambda q, k, d: jnp.where(tril, _bdot(q, k, _NT) * d, 0.0), qs, ks, decays)
    invs = unit_lower_inverse(lmats)
    us = dot3(invs, _each(lambda v, b: v * b, vcs, betas))
    ws = dot3(invs, _each(lambda kb, gc: kb * jnp.exp(gc), kbs, gcs))
    g_lasts = _each(lambda g: jnp.sum(g, axis=0, keepdims=True), gs)
    return qs, ks, us, ws, attns, gcs, g_lasts


def _final_fn(h, g, target):
    y = _rms(h) * g
    return 0.5 * jnp.sum(jnp.mean(jnp.square(y - target), axis=-1, keepdims=True), axis=0, keepdims=True)


def _tile(n, candidates):
    for t in candidates:
        if n % t == 0:
            return t
    raise ValueError(f"no tile for {n}")


def matmul(a, b, *, ta=False, tb=False, add=None, out_dtype=F32, tm=None, tn=None, split=None, layer=None, name):
    m = a.shape[1] if ta else a.shape[0]
    k = a.shape[0] if ta else a.shape[1]
    n = b.shape[0] if tb else b.shape[1]
    assert k == (b.shape[1] if tb else b.shape[0])
    out_shape, out_block, out_index = (m, n), None, lambda i, j: (i, j)
    if split is not None:
        dims4, perm = split
        out_shape = tuple(dims4[p] for p in perm)
        r, cols = out_shape[2:]
        tm, tn = m, tn or _tile(cols, (1408, 512))
        cb = cols // tn
        if perm == (0, 2, 1, 3):
            out_block, out_index = (2, None, r, tn), lambda i, j: (0, j // cb, 0, j % cb)
        elif perm == (1, 0, 2, 3):
            out_block, out_index = (2, N_SHARD, r, tn), lambda i, j: (0, 0, 0, j)
        else:
            raise ValueError(perm)
    tm = tm or _tile(m, (1024, 512, 1408, 256, 128))
    tn = tn or _tile(n, (512, 1408, 256, 128))
    aliases, prev = {}, None
    if layer is not None:
        index, count, prev = layer
        out_shape, out_block, out_index = (count, m, n), (None, tm, tn), lambda i, j: (index, i, j)
    dims = (((0 if ta else 1,), (1 if tb else 0,)), ((), ()))

    def body(a_ref, b_ref, *rest):
        acc = lax.dot_general(a_ref[...].astype(BF16), b_ref[...].astype(BF16), dims, preferred_element_type=F32)
        if add is not None:
            acc = acc + rest[0][...]
        o_ref = rest[-1]
        acc = acc.astype(out_dtype)
        if split is not None and split[1] == (1, 0, 2, 3):
            rows = o_ref.shape[2]
            for s in range(N_SHARD):
                for h in range(2):
                    o_ref[h, s] = acc[(2 * s + h) * rows:(2 * s + h + 1) * rows]
        else:
            o_ref[...] = acc.reshape(o_ref.shape)

    a_spec = pl.BlockSpec((k, tm), lambda i, j: (0, i)) if ta else pl.BlockSpec((tm, k), lambda i, j: (i, 0))
    b_spec = pl.BlockSpec((tn, k), lambda i, j: (j, 0)) if tb else pl.BlockSpec((k, tn), lambda i, j: (0, j))
    o_spec = pl.BlockSpec(out_block or (tm, tn), out_index)
    in_specs, args = [a_spec, b_spec], [a, b]
    if add is not None:
        in_specs.append(o_spec)
        args.append(add)
    if prev is not None:
        aliases = {len(args): 0}
        in_specs.append(pl.BlockSpec(memory_space=pl.ANY))
        args.append(prev)
    return pl.pallas_call(body, out_shape=_sds(out_shape, out_dtype), grid=(m // tm, n // tn), in_specs=in_specs,
                          out_specs=o_spec, input_output_aliases=aliases, compiler_params=_params(), name=name)(*args)


ROW_TILE = 256


def norm_fwd(x, g, *, name):
    t, d = x.shape

    def body(x_ref, g_ref, o_ref):
        o_ref[...] = _norm_fn(x_ref[...], g_ref[...]).astype(BF16)

    return pl.pallas_call(body, out_shape=_sds((t, d), BF16), grid=(t // ROW_TILE,),
                          in_specs=[pl.BlockSpec((ROW_TILE, d), lambda i: (i, 0)), pl.BlockSpec((1, d), lambda i: (0, 0))],
                          out_specs=pl.BlockSpec((ROW_TILE, d), lambda i: (i, 0)), compiler_params=_params(), name=name)(x, g)


def norm_bwd(x, g, dy, dres, *, name):
    t, d = x.shape

    def body(x_ref, g_ref, dy_ref, dres_ref, dx_ref, dg_ref):
        _, vjp = jax.vjp(_norm_fn, x_ref[...], g_ref[...])
        dx, dg = vjp(dy_ref[...])
        dx_ref[...] = dx + dres_ref[...]

        @pl.when(pl.program_id(0) == 0)
        def _():
            dg_ref[...] = jnp.zeros_like(dg_ref)

        dg_ref[...] += dg

    row = pl.BlockSpec((ROW_TILE, d), lambda i: (i, 0))
    vec = pl.BlockSpec((1, d), lambda i: (0, 0))
    return pl.pallas_call(body, out_shape=(_sds((t, d), F32), _sds((1, d), F32)), grid=(t // ROW_TILE,),
                          in_specs=[row, vec, row, row], out_specs=(row, vec), compiler_params=_params(), name=name)(x, g, dy, dres)


def final_fwd_bwd(h, g, target, *, name):
    t, d = h.shape

    def body(h_ref, g_ref, t_ref, loss_ref, dh_ref, dg_ref):
        tgt = t_ref[...]
        loss, vjp = jax.vjp(lambda hh, gg: _final_fn(hh, gg, tgt), h_ref[...], g_ref[...])
        dh, dg = vjp(jnp.ones((1, 1), F32))
        dh_ref[...] = dh

        @pl.when(pl.program_id(0) == 0)
        def _():
            dg_ref[...] = jnp.zeros_like(dg_ref)
            loss_ref[...] = jnp.zeros_like(loss_ref)

        dg_ref[...] += dg
        loss_ref[...] += jnp.broadcast_to(loss, loss_ref.shape)

    row = pl.BlockSpec((ROW_TILE, d), lambda i: (i, 0))
    vec = pl.BlockSpec((1, d), lambda i: (0, 0))
    return pl.pallas_call(body, out_shape=(_sds((1, LANES), F32), _sds((t, d), F32), _sds((1, d), F32)), grid=(t // ROW_TILE,),
                          in_specs=[row, vec, row], out_specs=(pl.BlockSpec((1, LANES), lambda i: (0, 0)), row, vec),
                          compiler_params=_params(), name=name)(h, g, target)


FFN_FWD_COLS = 256
FFN_BWD_COLS = 128


def ffn_act_fwd(u, cw, cb, *, name):
    t = u.shape[0]
    w = FFN_FWD_COLS
    nb = D_FF // w

    def body(ug_ref, uv_ref, wg_ref, wv_ref, bg_ref, bv_ref, o_ref):
        o_ref[...] = _ffn_act_fn(ug_ref[...], uv_ref[...], wg_ref[...], wv_ref[...], bg_ref[...], bv_ref[...]).astype(BF16)

    def col(rows, off):
        return pl.BlockSpec((rows, w), lambda j: (0, j + off))

    return pl.pallas_call(body, out_shape=_sds((t, D_FF), BF16), grid=(nb,),
                          in_specs=[col(t, 0), col(t, nb), col(3, 0), col(3, nb), col(1, 0), col(1, nb)],
                          out_specs=col(t, 0), compiler_params=_params(), name=name)(u, u, cw, cw, cb, cb)


def _put_column_blocks(step, n_steps, blocks, dst_ref, width, stage_ref, sems):
    def copies(at):
        slot = at % 2
        return [pltpu.make_async_copy(stage_ref.at[slot, p], dst_ref.at[:, pl.ds(pl.multiple_of((p * n_steps + at) * width, LANES), width)],
                                      sems.at[slot, p]) for p in range(len(blocks))]

    @pl.when(step >= 2)
    def _():
        for cp in copies(step - 2):
            cp.wait()

    for p, value in enumerate(blocks):
        stage_ref[step % 2, p] = value
    for cp in copies(step):
        cp.start()

    @pl.when(step == n_steps - 1)
    def _():
        for cp in copies(step - 1) + copies(step):
            cp.wait()


def ffn_act_bwd(u, cw, cb, da, *, name):
    t = u.shape[0]
    w = FFN_BWD_COLS
    nb = D_FF // w

    def body(ug_ref, uv_ref, wg_ref, wv_ref, bg_ref, bv_ref, da_ref, du_ref, dwg_ref, dwv_ref, dbg_ref, dbv_ref, stage_ref, sems):
        _, vjp = jax.vjp(_ffn_act_fn, ug_ref[...], uv_ref[...], wg_ref[...], wv_ref[...], bg_ref[...], bv_ref[...])
        dug, duv, dwg, dwv, dbg, dbv = vjp(da_ref[...])
        _put_column_blocks(pl.program_id(0), nb, (dug.astype(BF16), duv.astype(BF16)), du_ref, w, stage_ref, sems)
        dwg_ref[...] = dwg
        dwv_ref[...] = dwv
        dbg_ref[...] = dbg
        dbv_ref[...] = dbv

    def col(rows, off):
        return pl.BlockSpec((rows, w), lambda j: (0, j + off))

    outs = pl.pallas_call(
        body, out_shape=(_sds((t, 2 * D_FF), BF16), _sds((3, D_FF), F32), _sds((3, D_FF), F32), _sds((1, D_FF), F32), _sds((1, D_FF), F32)),
        grid=(nb,), in_specs=[col(t, 0), col(t, nb), col(3, 0), col(3, nb), col(1, 0), col(1, nb), col(t, 0)],
        out_specs=(pl.BlockSpec(memory_space=pl.ANY), col(3, 0), col(3, 0), col(1, 0), col(1, 0)),
        scratch_shapes=[pltpu.VMEM((2, 2, t, w), BF16), pltpu.SemaphoreType.DMA((2, 2))], compiler_params=_params(), name=name,
    )(u, u, cw, cw, cb, cb, da)
    du, dwg, dwv, dbg, dbv = outs
    return du, jnp.concatenate([dwg, dwv], axis=1), jnp.concatenate([dbg, dbv], axis=1)


GDN_CONV_COLS = 256
GDN_CONV_OFF = 4 * GROUP


def gdn_conv_fwd(p, cw, *, name):
    t = p.shape[0]
    w = GDN_CONV_COLS
    nb = 3 * GROUP // w
    off = GDN_CONV_OFF // w

    def body(x_ref, w_ref, o_ref):
        o_ref[...] = _gdn_conv_fn(x_ref[...], w_ref[...])

    return pl.pallas_call(body, out_shape=_sds((t, 3 * GROUP), F32), grid=(nb,),
                          in_specs=[pl.BlockSpec((t, w), lambda j: (0, j + off)), pl.BlockSpec((4, w), lambda j: (0, j))],
                          out_specs=pl.BlockSpec((t, w), lambda j: (0, j)), compiler_params=_params(), name=name)(p, cw)


def gdn_conv_bwd(p, cw, dc, *, name):
    t = p.shape[0]
    w = GDN_CONV_COLS
    nb = 3 * GROUP // w
    off = GDN_CONV_OFF // w

    def body(x_ref, w_ref, dc_ref, dx_ref, dw_ref):
        _, vjp = jax.vjp(_gdn_conv_fn, x_ref[...], w_ref[...])
        dx, dw = vjp(dc_ref[...])
        dx_ref[...] = dx.astype(BF16)
        dw_ref[...] = dw

    blk = pl.BlockSpec((t, w), lambda j: (0, j))
    wblk = pl.BlockSpec((4, w), lambda j: (0, j))
    return pl.pallas_call(body, out_shape=(_sds((t, 3 * GROUP), BF16), _sds((4, 3 * GROUP), F32)), grid=(nb,),
                          in_specs=[pl.BlockSpec((t, w), lambda j: (0, j + off)), wblk, blk], out_specs=(blk, wblk),
                          compiler_params=_params(), name=name)(p, cw, dc)


def _lru_specs(t):
    w = D_MODEL // LRU_BLOCKS
    gate = pl.BlockSpec((t, w), lambda j: (0, j))
    xin = pl.BlockSpec((t, w), lambda j: (0, j + LRU_BLOCKS))
    cw = pl.BlockSpec((4, w), lambda j: (0, j))
    vec = pl.BlockSpec((1, w), lambda j: (0, j))
    mat = pl.BlockSpec((None, w, w), lambda j: (j, 0, 0))
    return gate, xin, cw, vec, mat


def lru_fwd(gx, cw, cb, wa, ba, wx, bx, lam, *, name):
    t = gx.shape[0]
    gate, xin, cws, vec, mat = _lru_specs(t)

    def body(g_ref, x_ref, cw_ref, cb_ref, wa_ref, ba_ref, wx_ref, bx_ref, lam_ref, o_ref):
        o_ref[...] = _lru_fn(g_ref[...], x_ref[...], cw_ref[...], cb_ref[...], wa_ref[...], ba_ref[...], wx_ref[...],
                             bx_ref[...], lam_ref[...]).astype(BF16)

    return pl.pallas_call(body, out_shape=_sds((t, D_MODEL), BF16), grid=(LRU_BLOCKS,),
                          in_specs=[gate, xin, cws, vec, mat, vec, mat, vec, vec], out_specs=gate,
                          compiler_params=_params(), name=name)(gx, gx, cw, cb, wa, ba, wx, bx, lam)


def lru_bwd(gx, cw, cb, wa, ba, wx, bx, lam, dy, *, name):
    t = gx.shape[0]
    gate, xin, cws, vec, mat = _lru_specs(t)

    def body(g_ref, x_ref, cw_ref, cb_ref, wa_ref, ba_ref, wx_ref, bx_ref, lam_ref, dy_ref,
             dgx_ref, dcw_ref, dcb_ref, dwa_ref, dba_ref, dwx_ref, dbx_ref, dlam_ref, stage_ref, sems):
        _, vjp = jax.vjp(_lru_fn, g_ref[...], x_ref[...], cw_ref[...], cb_ref[...], wa_ref[...], ba_ref[...], wx_ref[...],
                         bx_ref[...], lam_ref[...])
        dg, dx, dcw, dcb, dwa, dba, dwx, dbx, dlam = vjp(dy_ref[...])
        _put_column_blocks(pl.program_id(0), LRU_BLOCKS, (dg.astype(BF16), dx.astype(BF16)), dgx_ref, D_MODEL // LRU_BLOCKS, stage_ref, sems)
        dcw_ref[...] = dcw
        dcb_ref[...] = dcb
        dwa_ref[...] = dwa
        dba_ref[...] = dba
        dwx_ref[...] = dwx
        dbx_ref[...] = dbx
        dlam_ref[...] = dlam

    d = D_MODEL
    w = d // LRU_BLOCKS
    out_shape = (_sds((t, 2 * d), BF16), _sds((4, d), F32), _sds((1, d), F32), _sds((LRU_BLOCKS, w, w), F32),
                 _sds((1, d), F32), _sds((LRU_BLOCKS, w, w), F32), _sds((1, d), F32), _sds((1, d), F32))
    return pl.pallas_call(body, out_shape=out_shape, grid=(LRU_BLOCKS,),
                          in_specs=[gate, xin, cws, vec, mat, vec, mat, vec, vec, gate],
                          out_specs=(pl.BlockSpec(memory_space=pl.ANY), cws, vec, mat, vec, mat, vec, vec),
                          scratch_shapes=[pltpu.VMEM((2, 2, t, w), BF16), pltpu.SemaphoreType.DMA((2, 2))],
                          compiler_params=_params(), name=name)(gx, gx, cw, cb, wa, ba, wx, bx, lam, dy)


def _ret_tables():
    half = HEAD // 2
    inv_freq = (np.float32(ROPE_BASE) ** (-np.arange(half, dtype=np.float32) / np.float32(half))).astype(np.float32)
    ang = (np.arange(SEQ, dtype=np.float32)[:, None] * inv_freq[None, :]).astype(np.float64)
    cos2 = np.concatenate([np.cos(ang), np.cos(ang)], axis=1).astype(np.float32)
    sin2 = np.concatenate([-np.sin(ang), np.sin(ang)], axis=1).astype(np.float32)
    c = RET_CHUNK
    log_gamma = np.log1p(-np.exp2(-5.0 - np.arange(N_HEADS, dtype=np.float64)))
    idx = np.arange(c, dtype=np.float64)
    rel = idx[:, None] - idx[None, :]
    dmask = np.where(rel >= 0, np.exp(log_gamma[:, None, None] * np.maximum(rel, 0.0)), 0.0)
    ones = np.ones((N_HEADS, c, HEAD))
    ktail = np.exp(log_gamma[:, None] * (c - 1 - idx))[:, :, None] * ones
    qdec = np.exp(log_gamma[:, None] * (idx + 1.0))[:, :, None] * ones
    cdec = np.exp(log_gamma * c)[:, None, None] * ones
    return tuple(jnp.asarray(a, F32) for a in (cos2, sin2, dmask, ktail, qdec, cdec))


def _ret_specs(rev):
    c = RET_CHUNK
    nc = SEQ // c

    def n_of(n):
        return nc - 1 - n if rev else n

    def group(off):
        return pl.BlockSpec((c, GROUP), lambda n: (n_of(n), off))

    tab = pl.BlockSpec((c, HEAD), lambda n: (n_of(n), 0))
    const = pl.BlockSpec((N_HEADS, c, HEAD), lambda n: (0, 0, 0))
    state = pl.BlockSpec((N_HEADS, None, HEAD, HEAD), lambda n: (0, n_of(n), 0, 0))
    return group, tab, const, state, nc


def _head(h):
    return slice(h * HEAD, (h + 1) * HEAD)


def ret_fwd(p, tables, *, name):
    group, tab, const, state, nc = _ret_specs(False)

    def body(q_ref, k_ref, v_ref, g_ref, cos_ref, sin_ref, dm_ref, kt_ref, qd_ref, cd_ref, y_ref, st_ref, s_scr):
        @pl.when(pl.program_id(0) == 0)
        def _():
            s_scr[...] = jnp.zeros_like(s_scr)

        heads = range(N_HEADS)
        states = tuple(s_scr[h] for h in heads)
        ys, new_states = _ret_fn(*(tuple(r[:, _head(h)] for h in heads) for r in (q_ref, k_ref, v_ref, g_ref)), states,
                                 cos_ref[...], sin_ref[...], *(tuple(r[h] for h in heads) for r in (dm_ref, kt_ref, qd_ref, cd_ref)))
        for h in heads:
            st_ref[h] = states[h]
            y_ref[:, _head(h)] = ys[h].astype(BF16)
            s_scr[h] = new_states[h]

    return pl.pallas_call(
        body, out_shape=(_sds((SEQ, GROUP), BF16), _sds((N_HEADS, nc, HEAD, HEAD), F32)), grid=(nc,),
        in_specs=[group(0), group(1), group(2), group(3), tab, tab, const, const, const, const],
        out_specs=(group(0), state), scratch_shapes=[pltpu.VMEM((N_HEADS, HEAD, HEAD), F32)], compiler_params=_params(), name=name,
    )(p, p, p, p, *tables)


def ret_bwd(p, tables, states, dy, *, name):
    group, tab, const, state, nc = _ret_specs(True)

    def body(q_ref, k_ref, v_ref, g_ref, cos_ref, sin_ref, dm_ref, kt_ref, qd_ref, cd_ref, st_ref, dy_ref,
             dq_ref, dk_ref, dv_ref, dg_ref, ds_scr):
        @pl.when(pl.program_id(0) == 0)
        def _():
            ds_scr[...] = jnp.zeros_like(ds_scr)

        heads = range(N_HEADS)
        consts = (cos_ref[...], sin_ref[...], *(tuple(r[h] for h in heads) for r in (dm_ref, kt_ref, qd_ref, cd_ref)))
        _, vjp = jax.vjp(lambda *a: _ret_fn(*a, *consts), *(tuple(r[:, _head(h)] for h in heads) for r in (q_ref, k_ref, v_ref, g_ref)),
                         tuple(st_ref[h] for h in heads))
        dqs, dks, dvs, dgs, dss = vjp((tuple(dy_ref[:, _head(h)] for h in heads), tuple(ds_scr[h] for h in heads)))
        for h in heads:
            dq_ref[:, _head(h)] = dqs[h].astype(BF16)
            dk_ref[:, _head(h)] = dks[h].astype(BF16)
            dv_ref[:, _head(h)] = dvs[h].astype(BF16)
            dg_ref[:, _head(h)] = dgs[h].astype(BF16)
            ds_scr[h] = dss[h]

    out = _sds((SEQ, GROUP), BF16)
    return pl.pallas_call(
        body, out_shape=(out, out, out, out), grid=(nc,),
        in_specs=[group(0), group(1), group(2), group(3), tab, tab, const, const, const, const, state, group(0)],
        out_specs=(group(0), group(0), group(0), group(0)), scratch_shapes=[pltpu.VMEM((N_HEADS, HEAD, HEAD), F32)],
        compiler_params=_params(), name=name,
    )(p, p, p, p, *tables, states, dy)


def _gdn_specs(rev):
    c = GDN_CHUNK * GDN_CHUNKS_PER_STEP
    nc = SEQ // c

    def n_of(n):
        return nc - 1 - n if rev else n

    def group(off):
        return pl.BlockSpec((c, GROUP), lambda n: (n_of(n), off))

    small = pl.BlockSpec((c, LANES), lambda n: (n_of(n), 0))
    vec = pl.BlockSpec((1, LANES), lambda n: (0, 0))
    state = pl.BlockSpec((N_HEADS, None, HEAD, HEAD), lambda n: (0, n_of(n), 0, 0))
    return group, small, vec, state, nc


GDN_GATE_GROUP = 7


def gdn_fwd(conv, p, small, a_log, dt_bias, gain, *, name):
    group, sm, vec, state, nc = _gdn_specs(False)

    def body(q_ref, k_ref, v_ref, g_ref, sm_ref, al_ref, dt_ref, gn_ref, y_ref, st_ref, s_scr):
        @pl.when(pl.program_id(0) == 0)
        def _():
            s_scr[...] = jnp.zeros_like(s_scr)

        states = tuple(s_scr[h] for h in range(N_HEADS))
        ys, new_states = _gdn_fn(*(tuple(r[:, _head(h)] for h in range(N_HEADS)) for r in (q_ref, k_ref, v_ref, g_ref)),
                                 sm_ref[...], al_ref[...], dt_ref[...], gn_ref[...], states)
        for h in range(N_HEADS):
            st_ref[h] = states[h]
            y_ref[:, _head(h)] = ys[h].astype(BF16)
            s_scr[h] = new_states[h]

    return pl.pallas_call(
        body, out_shape=(_sds((SEQ, GROUP), BF16), _sds((N_HEADS, nc, HEAD, HEAD), F32)), grid=(nc,),
        in_specs=[group(0), group(1), group(2), group(GDN_GATE_GROUP), sm, vec, vec, vec], out_specs=(group(0), state),
        scratch_shapes=[pltpu.VMEM((N_HEADS, HEAD, HEAD), F32)], compiler_params=_params(), name=name,
    )(conv, conv, conv, p, small, a_log, dt_bias, gain)


def gdn_bwd(conv, p, small, a_log, dt_bias, gain, states, dy, *, name):
    group, sm, vec, state, nc = _gdn_specs(True)

    def body(q_ref, k_ref, v_ref, g_ref, sm_ref, al_ref, dt_ref, gn_ref, st_ref, dy_ref,
             dq_ref, dk_ref, dv_ref, dg_ref, dsm_ref, dal_ref, ddt_ref, dgn_ref, ds_scr):
        @pl.when(pl.program_id(0) == 0)
        def _():
            ds_scr[...] = jnp.zeros_like(ds_scr)
            dal_ref[...] = jnp.zeros_like(dal_ref)
            ddt_ref[...] = jnp.zeros_like(ddt_ref)
            dgn_ref[...] = jnp.zeros_like(dgn_ref)

        per_head = tuple(tuple(r[:, _head(h)] for h in range(N_HEADS)) for r in (q_ref, k_ref, v_ref, g_ref))
        _, vjp = jax.vjp(_gdn_fn, *per_head, sm_ref[...], al_ref[...], dt_ref[...], gn_ref[...],
                         tuple(st_ref[h] for h in range(N_HEADS)))
        cts = (tuple(dy_ref[:, _head(h)] for h in range(N_HEADS)), tuple(ds_scr[h] for h in range(N_HEADS)))
        dqs, dks, dvs, dgs, dsm, dal, ddt, dgn, dss = vjp(cts)
        for h in range(N_HEADS):
            dq_ref[:, _head(h)] = dqs[h]
            dk_ref[:, _head(h)] = dks[h]
            dv_ref[:, _head(h)] = dvs[h]
            dg_ref[:, _head(h)] = dgs[h].astype(BF16)
            ds_scr[h] = dss[h]
        dsm_ref[...] = dsm
        dal_ref[...] += dal
        ddt_ref[...] += ddt
        dgn_ref[...] += dgn

    f = _sds((SEQ, GROUP), F32)
    pv = _sds((1, LANES), F32)
    return pl.pallas_call(
        body, out_shape=(f, f, f, _sds((SEQ, GROUP), BF16), _sds((SEQ, LANES), F32), pv, pv, pv), grid=(nc,),
        in_specs=[group(0), group(1), group(2), group(GDN_GATE_GROUP), sm, vec, vec, vec, state, group(1)],
        out_specs=(group(0), group(0), group(0), group(0), sm, vec, vec, vec), scratch_shapes=[pltpu.VMEM((N_HEADS, HEAD, HEAD), F32)],
        compiler_params=_params(), name=name,
    )(conv, conv, conv, p, small, a_log, dt_bias, gain, states, dy)


PACK_ROW_TILE = 1024


def adamw(w, g, m, v, *, name):
    r = w.shape[0]
    tr = _row_tile(r, LANES)

    def body(w_ref, g_ref, m_ref, v_ref, d_ref, nm_ref, nv_ref):
        gg = g_ref[...]
        nm = ADAM_B1 * m_ref[...] + (1.0 - ADAM_B1) * gg
        nv = ADAM_B2 * v_ref[...] + (1.0 - ADAM_B2) * jnp.square(gg)
        m_hat = nm / (1.0 - ADAM_B1 ** ADAM_STEP)
        v_hat = nv / (1.0 - ADAM_B2 ** ADAM_STEP)
        d_ref[...] = -ADAM_LR * (m_hat / (jnp.sqrt(v_hat) + ADAM_EPS) + ADAM_WD * w_ref[...])
        nm_ref[...] = nm
        nv_ref[...] = nv

    blk = pl.BlockSpec((tr, LANES), lambda i: (i, 0))
    o = _sds((r, LANES), F32)
    return pl.pallas_call(body, out_shape=(o, o, o), grid=(r // tr,), in_specs=[blk] * 4, out_specs=(blk, blk, blk),
                          compiler_params=_params(), name=name)(w, g, m, v)


ELEMENTWISE_BLOCK_BYTES = 2 * 1024 * 1024


def _row_tile(r, c):
    best = None
    for tr in range(8, r + 1, 8):
        if r % tr == 0 and tr * c * 4 <= ELEMENTWISE_BLOCK_BYTES:
            best = tr
    if best is None:
        raise ValueError(f"no row tile for ({r}, {c})")
    return best


def _tile_2d(r, c):
    if any(r % tr == 0 for tr in range(8, r + 1, 8)):
        return _row_tile(r, c), c
    tc = max(t for t in range(LANES, c + 1, LANES) if c % t == 0 and r * t * 4 <= ELEMENTWISE_BLOCK_BYTES)
    return r, tc


def _core_index():
    return lax.axis_index("c").astype(jnp.int32).reshape(1)


def _chip_index():
    return (2 * lax.axis_index("x") + lax.axis_index("y")).astype(jnp.int32).reshape(1)


def adamw_halves(w, m, v, g_own, g_sib, *, layer=0, prev=None, name):
    n_layers, rows, c = w.shape
    r = rows // 2
    tr = _row_tile(r, c)
    nb = r // tr

    def body(c_ref, w_ref, m_ref, v_ref, own_ref, sib_ref, *rest):
        g_ref, d_ref, nm_ref, nv_ref = rest[-4:]
        gg = jnp.where(pl.program_id(0) == c_ref[0], own_ref[...], sib_ref[...])
        nm = ADAM_B1 * m_ref[...] + (1.0 - ADAM_B1) * gg
        nv = ADAM_B2 * v_ref[...] + (1.0 - ADAM_B2) * jnp.square(gg)
        m_hat = nm / (1.0 - ADAM_B1 ** ADAM_STEP)
        v_hat = nv / (1.0 - ADAM_B2 ** ADAM_STEP)
        g_ref[...] = gg
        d_ref[...] = -ADAM_LR * (m_hat / (jnp.sqrt(v_hat) + ADAM_EPS) + ADAM_WD * w_ref[...])
        nm_ref[...] = nm
        nv_ref[...] = nv

    full = pl.BlockSpec((None, tr, c), lambda h, i, cr: (layer, h * nb + i, 0))
    half = pl.BlockSpec((tr, c), lambda h, i, cr: (i, 0))
    o = _sds((n_layers, rows, c), F32)
    prev = list(prev or ())
    gs = pltpu.PrefetchScalarGridSpec(num_scalar_prefetch=1, grid=(2, nb), in_specs=[full, full, full, half, half] + [_ANY] * len(prev),
                                      out_specs=(full, full, full, full))
    n_fixed = 6
    return pl.pallas_call(body, out_shape=(o, o, o, o), grid_spec=gs, compiler_params=_params(), name=name,
                          input_output_aliases={n_fixed + k: k for k in range(len(prev))})(
        _core_index(), w, m, v, g_own, g_sib, *prev)


ADAMW_COLUMN_TILE = 256


def adamw_column_halves(w, m, v, g_own, g_sib, *, name):
    _, rows, cols = w.shape
    tc = ADAMW_COLUMN_TILE
    per_half = cols // 2 // tc

    def body(c_ref, w_ref, m_ref, v_ref, own_ref, sib_ref, g_ref, d_ref, nm_ref, nv_ref):
        gg = jnp.where(pl.program_id(0) // per_half == c_ref[0], own_ref[...], sib_ref[...])
        nm = ADAM_B1 * m_ref[...] + (1.0 - ADAM_B1) * gg
        nv = ADAM_B2 * v_ref[...] + (1.0 - ADAM_B2) * jnp.square(gg)
        m_hat = nm / (1.0 - ADAM_B1 ** ADAM_STEP)
        v_hat = nv / (1.0 - ADAM_B2 ** ADAM_STEP)
        g_ref[...] = gg
        d_ref[...] = -ADAM_LR * (m_hat / (jnp.sqrt(v_hat) + ADAM_EPS) + ADAM_WD * w_ref[...])
        nm_ref[...] = nm
        nv_ref[...] = nv

    full = pl.BlockSpec((None, rows, tc), lambda j, cr: (0, 0, j))
    half = pl.BlockSpec((rows, tc), lambda j, cr: (0, j % per_half))
    o = _sds(w.shape, F32)
    gs = pltpu.PrefetchScalarGridSpec(num_scalar_prefetch=1, grid=(cols // tc,), in_specs=[full, full, full, half, half],
                                      out_specs=(full, full, full, full))
    return pl.pallas_call(body, out_shape=(o, o, o, o), grid_spec=gs, compiler_params=_params(), name=name)(
        _core_index(), w, m, v, g_own, g_sib)


def add_core_halves(g2, land, *, out_dtype, name):
    _, ns, r, cols = g2.shape
    tr, tc = _tile_2d(r, cols)

    def body(c_ref, a_ref, b_ref, o_ref):
        o_ref[...] = (a_ref[...] + b_ref[...]).astype(out_dtype)

    gs = pltpu.PrefetchScalarGridSpec(
        num_scalar_prefetch=1, grid=(ns, r // tr, cols // tc),
        in_specs=[pl.BlockSpec((None, None, tr, tc), lambda s, i, j, cr: (cr[0], s, i, j)),
                  pl.BlockSpec((None, tr, tc), lambda s, i, j, cr: (s, i, j))],
        out_specs=pl.BlockSpec((None, tr, tc), lambda s, i, j, cr: (s, i, j)))
    return pl.pallas_call(body, out_shape=_sds((ns, r, cols), out_dtype), grid_spec=gs, compiler_params=_params(), name=name)(
        _core_index(), g2, land)


def sum_over_chips(own, land, *, scatter, name):
    _, r, cols = own.shape
    tr, tc = _tile_2d(r, cols)

    def body(mine_ref, own_ref, l0, l1, l2, l3, o_ref):
        mine = mine_ref[0]
        mine_val = own_ref[...]
        acc = None
        for s, l_ref in enumerate((l0, l1, l2, l3)):
            val = jnp.where(mine == s, mine_val, l_ref[...]).astype(F32)
            acc = val if acc is None else acc + val
        o_ref[...] = acc

    def slot(s):
        return pl.BlockSpec((None, tr, tc), lambda i, j, mr: (jnp.where(mr[0] == s, (s + 1) % N_SHARD, s), i, j))

    own_spec = pl.BlockSpec((None, tr, tc), lambda i, j, mr: (mr[0] if scatter else 0, i, j))
    gs = pltpu.PrefetchScalarGridSpec(num_scalar_prefetch=1, grid=(r // tr, cols // tc), in_specs=[own_spec] + [slot(s) for s in range(N_SHARD)],
                                      out_specs=pl.BlockSpec((tr, tc), lambda i, j, mr: (i, j)))
    return pl.pallas_call(body, out_shape=_sds((r, cols), F32), grid_spec=gs, compiler_params=_params(), name=name)(
        _chip_index(), own, land, land, land, land)


_ANY = pl.BlockSpec(memory_space=pl.ANY)


def xy_exchange(src, *, scatter, name):
    rh = src.shape[1]

    def body(src_ref, land_ref, send_sems, recv_sems, loc_sem):
        x, y, c = lax.axis_index("x"), lax.axis_index("y"), lax.axis_index("c")
        mine = 2 * x + y
        peers = [(1 - x, y), (x, 1 - y), (1 - x, 1 - y)]

        def piece(shard):
            return src_ref.at[shard] if scatter else src_ref.at[c]

        def copy(k, px, py, dst_slot):
            return pltpu.make_async_remote_copy(src_ref=piece(2 * px + py), dst_ref=land_ref.at[dst_slot], send_sem=send_sems.at[k],
                                                recv_sem=recv_sems.at[k], device_id=(px, py, c), device_id_type=MESH)

        keep = pltpu.make_async_copy(piece(mine), land_ref.at[mine], loc_sem)
        keep.start()
        sends = [copy(k, px, py, mine) for k, (px, py) in enumerate(peers)]
        for cp in sends:
            cp.start()
        for cp in sends:
            cp.wait_send()
        for k, (px, py) in enumerate(peers):
            copy(k, px, py, 2 * px + py).wait_recv()
        keep.wait()

    return pl.pallas_call(body, out_shape=_sds((N_SHARD, rh, LANES), src.dtype), in_specs=[_ANY], out_specs=_ANY,
                          scratch_shapes=[pltpu.SemaphoreType.DMA((3,)), pltpu.SemaphoreType.DMA((3,)), pltpu.SemaphoreType.DMA(())],
                          name=name)(src)


def core_exchange(src, *, send_other_half, name):
    def body(src_ref, out_ref, send_sem, recv_sem, loc_sem):
        x, y, c = lax.axis_index("x"), lax.axis_index("y"), lax.axis_index("c")
        if send_other_half:
            cp = pltpu.make_async_remote_copy(src_ref=src_ref.at[1 - c], dst_ref=out_ref, send_sem=send_sem, recv_sem=recv_sem,
                                              device_id=(x, y, 1 - c), device_id_type=MESH)
            cp.start()
            cp.wait_send()
            cp.wait_recv()
        else:
            keep = pltpu.make_async_copy(src_ref, out_ref.at[c], loc_sem)
            keep.start()
            cp = pltpu.make_async_remote_copy(src_ref=src_ref, dst_ref=out_ref.at[c], send_sem=send_sem, recv_sem=recv_sem,
                                              device_id=(x, y, 1 - c), device_id_type=MESH)
            cp.start()
            cp.wait_send()
            pltpu.make_async_remote_copy(src_ref=src_ref, dst_ref=out_ref.at[1 - c], send_sem=send_sem, recv_sem=recv_sem,
                                         device_id=(x, y, 1 - c), device_id_type=MESH).wait_recv()
            keep.wait()

    out_shape = _sds(src.shape[1:], src.dtype) if send_other_half else _sds((2,) + src.shape, src.dtype)
    return pl.pallas_call(body, out_shape=out_shape, in_specs=[_ANY], out_specs=_ANY,
                          scratch_shapes=[pltpu.SemaphoreType.DMA(()), pltpu.SemaphoreType.DMA(()), pltpu.SemaphoreType.DMA(())],
                          name=name)(src)


def _comm_call(body, ins, out_shapes, sem_counts, name):
    return pl.pallas_call(body, out_shape=tuple(out_shapes), in_specs=[_ANY] * len(ins), out_specs=tuple([_ANY] * len(out_shapes)),
                          scratch_shapes=[pltpu.SemaphoreType.DMA((k,)) for k in sem_counts], name=name)(*ins)


def _sequencer_call(body, ins, out_shapes, sem_counts, name, collective_id):
    return pl.kernel(body, out_type=list(out_shapes), mesh=plsc.ScalarSubcoreMesh(axis_name="sequencer", num_cores=1), name=name,
                     scratch_types=[pltpu.SemaphoreType.DMA((k,)) for k in sem_counts],
                     compiler_params=pltpu.CompilerParams(collective_id=collective_id))(*ins)


def _handshake(peers):
    barrier = pltpu.get_barrier_semaphore()
    for peer in peers:
        pl.semaphore_signal(barrier, inc=1, device_id=peer, device_id_type=MESH)
    pl.semaphore_wait(barrier, len(peers))


def _xy_peers(x, y):
    return [(1 - x, y), (x, 1 - y), (1 - x, 1 - y)]


def gather_halves(halves, *, name, collective_id):
    n = len(halves)

    def body(*refs):
        ins, lands, sibs = refs[:n], refs[n:2 * n], refs[2 * n:3 * n]
        ici_send, ici_recv, d2d_send, d2d_recv = refs[3 * n:]
        x, y, c = lax.axis_index("x"), lax.axis_index("y"), lax.axis_index("c")
        mine = 2 * x + y
        peers = _xy_peers(x, y)
        _handshake([(px, py, c) for px, py in peers] + [(x, y, 1 - c)])

        def ici(i, k, slot):
            px, py = peers[k]
            return pltpu.make_async_remote_copy(src_ref=ins[i].at[c], dst_ref=lands[i].at[slot], send_sem=ici_send.at[3 * i + k],
                                                recv_sem=ici_recv.at[3 * i + k], device_id=(px, py, c), device_id_type=MESH)

        def pass_on(i, k):
            px, py = peers[k]
            slot = 2 * px + py
            return pltpu.make_async_remote_copy(src_ref=lands[i].at[slot], dst_ref=sibs[i].at[slot], send_sem=d2d_send.at[3 * i + k],
                                                recv_sem=d2d_recv.at[3 * i + k], device_id=(x, y, 1 - c), device_id_type=MESH)

        sends = [ici(i, k, mine) for i in range(n) for k in range(3)]
        for cp in sends:
            cp.start()
        passed = []
        for i in range(n):
            for k in range(3):
                px, py = peers[k]
                ici(i, k, 2 * px + py).wait_recv()
                cp = pass_on(i, k)
                cp.start()
                passed.append(cp)
        for cp in passed:
            cp.wait_recv()
        for cp in sends + passed:
            cp.wait_send()

    outs = [_sds((N_SHARD,) + h.shape[1:], h.dtype) for h in halves]
    res = _sequencer_call(body, halves, outs + outs, [3 * n] * 4, name, collective_id)
    return res[:n], res[n:]


def send_other_half(arrays, *, name, collective_id):
    n = len(arrays)

    def body(*refs):
        ins, lands = refs[:n], refs[n:2 * n]
        send_sems, recv_sems = refs[2 * n:]
        x, y, c = lax.axis_index("x"), lax.axis_index("y"), lax.axis_index("c")
        _handshake([(x, y, 1 - c)])
        copies = [pltpu.make_async_remote_copy(src_ref=ins[i].at[1 - c], dst_ref=lands[i], send_sem=send_sems.at[i],
                                               recv_sem=recv_sems.at[i], device_id=(x, y, 1 - c), device_id_type=MESH) for i in range(n)]
        for cp in copies:
            cp.start()
        for cp in copies:
            cp.wait_recv()
        for cp in copies:
            cp.wait_send()

    return _sequencer_call(body, arrays, [_sds(a.shape[1:], a.dtype) for a in arrays], [n, n], name, collective_id)


_HBM = pl.BlockSpec(memory_space=pltpu.HBM)
_SEM = pl.BlockSpec(memory_space=pltpu.SEMAPHORE)
_SPLIT_COPY = dict(has_side_effects=pltpu.SideEffectType.DATAFLOW_SIDE_EFFECTING)


def _chip_copy(ins, lands, send_sems, recv_sems, scatter, i, k, receive):
    x, y, c = lax.axis_index("x"), lax.axis_index("y"), lax.axis_index("c")
    px, py = _xy_peers(x, y)[k]
    theirs, mine = 2 * px + py, 2 * x + y
    src = ins[i].at[theirs] if scatter[i] else ins[i].at[0]
    return pltpu.make_async_remote_copy(src_ref=src, dst_ref=lands[i].at[theirs if receive else mine], send_sem=send_sems.at[3 * i + k],
                                        recv_sem=recv_sems.at[3 * i + k], device_id=(px, py, c), device_id_type=MESH)


def send_to_chips_start(arrays, scatter, *, name):
    n = len(arrays)

    def body(*refs):
        send_sems, recv_sems = refs[2 * n], refs[2 * n + 1]
        ins, lands = refs[2 * n + 2:3 * n + 2], refs[3 * n + 2:4 * n + 2]
        token = refs[4 * n + 2]
        for i in range(n):
            for k in range(3):
                _chip_copy(ins, lands, send_sems, recv_sems, scatter, i, k, receive=False).start()
        token[...] = jnp.zeros_like(token)

    land_shapes = [(N_SHARD,) + a.shape[1:] for a in arrays]
    operands = [pltpu.with_memory_space_constraint(a, pltpu.HBM) for a in arrays]
    operands += [pltpu.with_memory_space_constraint(lax.empty(s, a.dtype), pltpu.HBM) for s, a in zip(land_shapes, arrays)]
    out_shape = ([pltpu.SemaphoreType.DMA((3 * n,)), pltpu.SemaphoreType.DMA((3 * n,))] + [pltpu.HBM(a.shape, a.dtype) for a in arrays]
                 + [pltpu.HBM(s, a.dtype) for s, a in zip(land_shapes, arrays)] + [_sds((8, LANES), F32)])
    res = pl.pallas_call(body, name=name, out_shape=out_shape, in_specs=[_HBM] * (2 * n),
                         out_specs=[_SEM, _SEM] + [_HBM] * (2 * n) + [pl.BlockSpec(memory_space=pltpu.VMEM)],
                         input_output_aliases={i: 2 + i for i in range(2 * n)}, compiler_params=pltpu.CompilerParams(**_SPLIT_COPY))(*operands)
    return (res[0], res[1], res[2:2 + n], res[2 + n:2 + 2 * n], scatter), res[-1]


def send_to_chips_wait(state, after, *, name):
    send_sems, recv_sems, arrays, lands, scatter = state
    n = len(arrays)

    def body(*refs):
        ins, landing = refs[:n], refs[n:2 * n]
        send_sems, recv_sems = refs[2 * n], refs[2 * n + 1]
        for i in range(n):
            for k in range(3):
                _chip_copy(ins, landing, send_sems, recv_sems, scatter, i, k, receive=True).wait_recv()
        for i in range(n):
            for k in range(3):
                _chip_copy(ins, landing, send_sems, recv_sems, scatter, i, k, receive=False).wait_send()

    out_shape = [pltpu.HBM(a.shape, a.dtype) for a in list(arrays) + list(lands)]
    res = pl.pallas_call(body, name=name, out_shape=out_shape, in_specs=[_HBM] * (2 * n) + [_SEM, _SEM] + [_ANY] * len(after),
                         out_specs=[_HBM] * (2 * n), input_output_aliases={i: i for i in range(2 * n)},
                         compiler_params=pltpu.CompilerParams(**_SPLIT_COPY))(*arrays, *lands, send_sems, recv_sems, *after)
    return res[:n], res[n:]


def swap_with_other_core(arrays, *, name, collective_id):
    n = len(arrays)

    def body(*refs):
        ins, lands = refs[:n], refs[n:2 * n]
        send_sems, recv_sems = refs[2 * n:]
        x, y, c = lax.axis_index("x"), lax.axis_index("y"), lax.axis_index("c")
        _handshake([(x, y, 1 - c)])
        copies = [pltpu.make_async_remote_copy(src_ref=ins[i], dst_ref=lands[i], send_sem=send_sems.at[i], recv_sem=recv_sems.at[i],
                                               device_id=(x, y, 1 - c), device_id_type=MESH) for i in range(n)]
        for cp in copies:
            cp.start()
        for cp in copies:
            cp.wait_recv()
        for cp in copies:
            cp.wait_send()

    return _sequencer_call(body, arrays, [_sds(a.shape, a.dtype) for a in arrays], [n, n], name, collective_id)


def _pack_rows(n_elems, row_multiple):
    rows = -(-n_elems // LANES)
    return -(-rows // row_multiple) * row_multiple


def _pack(arrays, rows, dtype):
    flat = jnp.concatenate([a.reshape(-1).astype(dtype) for a in arrays])
    return jnp.pad(flat, (0, rows * LANES - flat.shape[0])).reshape(rows, LANES)


def _unpack(packed, shapes):
    flat = packed.reshape(-1)
    out, off = [], 0
    for s in shapes:
        n = int(np.prod(s))
        out.append(flat[off:off + n].reshape(s))
        off += n
    return out


def all_gather_shards(shards, axes, dtype, row_multiple, tag):
    shapes = [s.shape for s in shards]
    rows = _pack_rows(sum(int(np.prod(s)) for s in shapes), row_multiple)
    packed = _pack(shards, rows, dtype).reshape(2, rows // 2, LANES)
    land = xy_exchange(packed, scatter=False, name=f"gather_xy_{tag}")
    both = core_exchange(land, send_other_half=False, name=f"gather_c_{tag}")
    per_shard = jnp.swapaxes(both, 0, 1).reshape(N_SHARD, rows, LANES)
    pieces = [_unpack(per_shard[s], shapes) for s in range(N_SHARD)]
    return [jnp.concatenate([pieces[s][i] for s in range(N_SHARD)], axis=ax) for i, ax in enumerate(axes)]


def _ordered_before(first, then):
    if then is None:
        return first, None
    return lax.optimization_barrier((first, then))


def reduce_between_cores(arrays, scatter, *, tag, collective_id, before=None):
    arrays, before = _ordered_before(arrays, before)
    land = send_other_half(arrays, name=f"reduce_core_send_{tag}", collective_id=collective_id)
    return (arrays, land, scatter, tag, collective_id), before


def reduce_between_chips(state, before=None):
    arrays, land, scatter, tag, collective_id = state
    chip = [add_core_halves(a, l, out_dtype=BF16 if sc else F32, name=f"reduce_core_add_{tag}_{i}")
            for i, (a, l, sc) in enumerate(zip(arrays, land, scatter))]
    sending, token = send_to_chips_start(chip, scatter, name=f"reduce_chip_start_{tag}")
    token, before = _ordered_before(token, before)
    return (sending, token, scatter, tag, collective_id), before


def reduce_finish(state, after):
    sending, token, scatter, tag, collective_id = state
    chip, land = send_to_chips_wait(sending, tuple(after) + (token,), name=f"reduce_chip_wait_{tag}")
    own = [sum_over_chips(ch, l, scatter=sc, name=f"reduce_chip_add_{tag}_{i}") for i, (ch, l, sc) in enumerate(zip(chip, land, scatter))]
    sib = swap_with_other_core(own, name=f"reduce_core_swap_{tag}", collective_id=collective_id + 2)
    return own, sib


def _ffn_layer_fwd(h, norm_g, w_up, cw, cb, w_down, tag):
    hn = norm_fwd(h, norm_g, name=f"ffn_norm_{tag}")
    u = matmul(hn, w_up, name=f"ffn_up_{tag}")
    act = ffn_act_fwd(u, cw, cb, name=f"ffn_act_{tag}")
    out = matmul(act, w_down, add=h, name=f"ffn_down_{tag}")
    return out, (h, hn, u, act)


def _travel_layout(array):
    return BIG_ARRAYS[array][3], BIG_ARRAYS[array][4]


def _ffn_layer_bwd(saved, dout, norm_g, w_up, cw, cb, w_down, tag, d_w_down_other=None):
    h, hn, u, act = saved
    dact = matmul(dout, w_down, tb=True, name=f"ffn_down_dx_{tag}")
    d_w_down = matmul(act, dout, ta=True, layer=(int(tag), 2, d_w_down_other), name=f"ffn_down_dw_{tag}")
    du, dcw, dcb = ffn_act_bwd(u, cw, cb, dact, name=f"ffn_act_bwd_{tag}")
    dhn = matmul(du, w_up, tb=True, name=f"ffn_up_dx_{tag}")
    d_w_up = matmul(hn, du, ta=True, split=_travel_layout(f"ffn_w_up_{tag}"), name=f"ffn_up_dw_{tag}")
    dh, dg = norm_bwd(h, norm_g, dhn, dout, name=f"ffn_norm_bwd_{tag}")
    return dh, dg, d_w_up, dcw, dcb, d_w_down


def local_step(x, target, w, stage=lambda name, tensors, grads=None: tensors):
    g = {}
    tables = _ret_tables()
    x = stage("start", x)
    w_in_t = w["ret_gdn_w_in"]
    w_main = w_in_t[:MIX_MAIN]
    w_small = jnp.pad(w_in_t[MIX_MAIN:], ((0, LANES - 2 * N_HEADS), (0, 0)))
    a_log = jnp.pad(w["gdn_a_log"], ((0, 0), (0, LANES - N_HEADS)))
    dt_bias = jnp.pad(w["gdn_dt_bias"], ((0, 0), (0, LANES - N_HEADS)))

    hn0 = stage("normed", norm_fwd(x, w["norm_mix"][0:1], name="mix0_norm"))
    p = matmul(hn0, w_main, tb=True, name="mix0_in")
    small = matmul(hn0, w_small, tb=True, name="mix0_in_small")
    y_ret, s_ret = ret_fwd(p, tables, name="ret_fwd")
    conv = gdn_conv_fwd(p, w["gdn_conv_w"], name="gdn_conv")
    y_gdn, s_gdn = gdn_fwd(conv, p, small, a_log, dt_bias, w["gdn_out_gain"], name="gdn_fwd")
    y0 = stage("mixed", jnp.concatenate([y_ret, y_gdn], axis=1))
    h1 = matmul(y0, w["ret_gdn_w_out"], add=x, name="mix0_out")
    h2, ffn0 = _ffn_layer_fwd(h1, w["norm_ffn"][0:1], w["ffn_w_up"][0], w["ffn_conv_w"][0], w["ffn_conv_b"][0:1], w["ffn_w_down"][0], "0")
    h2 = stage("layer0", h2)

    hn1 = norm_fwd(h2, w["norm_mix"][1:2], name="mix1_norm")
    gx = matmul(hn1, w["lru_w_in"], name="mix1_in")
    lru_p = (w["lru_conv_w"], w["lru_conv_b"], w["lru_w_a"], w["lru_b_a"], w["lru_w_x"], w["lru_b_x"], w["lru_lambda"])
    y1 = lru_fwd(gx, *lru_p, name="lru_fwd")
    h3 = matmul(y1, w["lru_w_out"], add=h2, name="mix1_out")
    h4, ffn1 = _ffn_layer_fwd(h3, w["norm_ffn"][1:2], w["ffn_w_up"][1], w["ffn_conv_w"][1], w["ffn_conv_b"][1:2], w["ffn_w_down"][1], "1")

    loss, dh4, g["norm_final"] = final_fwd_bwd(h4, w["norm_final"], target, name="final")

    dh3, dgf1, dwu1, dcw1, dcb1, dwd1 = _ffn_layer_bwd(ffn1, dh4, w["norm_ffn"][1:2], w["ffn_w_up"][1], w["ffn_conv_w"][1],
                                                     w["ffn_conv_b"][1:2], w["ffn_w_down"][1], "1")
    g["ffn_w_up_1"] = dwu1
    dh3 = stage("grads0_ready", dh3, g)
    dy1 = matmul(dh3, w["lru_w_out"], tb=True, name="mix1_out_dx")
    g["lru_w_out"] = matmul(y1, dh3, ta=True, split=_travel_layout("lru_w_out"), name="mix1_out_dw")
    dgx, g["lru_conv_w"], g["lru_conv_b"], g["lru_w_a"], g["lru_b_a"], g["lru_w_x"], g["lru_b_x"], g["lru_lambda"] = lru_bwd(
        gx, *lru_p, dy1, name="lru_bwd")
    dgx = stage("grads0_send", dgx, g)
    dhn1 = matmul(dgx, w["lru_w_in"], tb=True, name="mix1_in_dx")
    g["lru_w_in"] = matmul(hn1, dgx, ta=True, split=_travel_layout("lru_w_in"), name="mix1_in_dw")
    dh2, dgm1 = norm_bwd(h2, w["norm_mix"][1:2], dhn1, dh3, name="mix1_norm_bwd")
    dh2 = stage("grads1_ready", dh2, g)

    dh1, dgf0, dwu0, dcw0, dcb0, dwd0 = _ffn_layer_bwd(ffn0, dh2, w["norm_ffn"][0:1], w["ffn_w_up"][0], w["ffn_conv_w"][0],
                                                     w["ffn_conv_b"][0:1], w["ffn_w_down"][0], "0", dwd1)
    g["ffn_w_up_0"] = dwu0
    g["ffn_w_down"] = dwd0
    dh1 = stage("grads2_ready", stage("grads1_send", dh1, g), g)
    dy0 = matmul(dh1, w["ret_gdn_w_out"], tb=True, name="mix0_out_dx")
    g["ret_gdn_w_out"] = matmul(y0, dh1, ta=True, split=_travel_layout("ret_gdn_w_out"), name="mix0_out_dw")
    dq_r, dk_r, dv_r, dg_r = ret_bwd(p, tables, s_ret, dy0, name="ret_bwd")
    dy0, dq_r = stage("grads2_send", (dy0, dq_r), g)
    dcq, dck, dcv, dg_d, dsmall, dal, ddt, dgain = gdn_bwd(conv, p, small, a_log, dt_bias, w["gdn_out_gain"], s_gdn, dy0, name="gdn_bwd")
    dconv = jnp.concatenate([dcq, dck, dcv], axis=1)
    dp_conv, g["gdn_conv_w"] = gdn_conv_bwd(p, w["gdn_conv_w"], dconv, name="gdn_conv_bwd")
    dp = jnp.concatenate([dq_r, dk_r, dv_r, dg_r, dp_conv, dg_d], axis=1)
    dhn0 = matmul(dp, w_main, name="mix0_in_dx")
    dhn0 = matmul(dsmall, w_small, add=dhn0, name="mix0_in_small_dx")
    d_w_main = matmul(dp, hn0, ta=True, name="mix0_in_dw")
    d_w_small = matmul(dsmall, hn0, ta=True, name="mix0_in_small_dw")
    g["ret_gdn_w_in"] = jnp.concatenate([d_w_main, d_w_small[:2 * N_HEADS]], axis=0)
    dx, dgm0 = norm_bwd(x, w["norm_mix"][0:1], dhn0, dh1, name="mix0_norm_bwd")

    g["gdn_a_log"] = dal[:, :N_HEADS]
    g["gdn_dt_bias"] = ddt[:, :N_HEADS]
    g["gdn_out_gain"] = dgain
    g["norm_mix"] = jnp.concatenate([dgm0, dgm1], axis=0)
    g["norm_ffn"] = jnp.concatenate([dgf0, dgf1], axis=0)
    g["ffn_conv_w"] = jnp.stack([dcw0, dcw1])
    g["ffn_conv_b"] = jnp.concatenate([dcb0, dcb1], axis=0)
    return loss, dx, g


WEIGHTS = ("norm_mix", "norm_ffn", "ret_gdn_w_in", "gdn_conv_w", "gdn_a_log", "gdn_dt_bias", "gdn_out_gain", "ret_gdn_w_out",
           "lru_w_in", "lru_conv_w", "lru_conv_b", "lru_w_a", "lru_b_a", "lru_w_x", "lru_b_x", "lru_lambda", "lru_w_out",
           "ffn_w_up", "ffn_conv_w", "ffn_conv_b", "ffn_w_down", "norm_final")
MATMUL_SHARDED = {"ret_gdn_w_in": 1, "ret_gdn_w_out": 0, "lru_w_in": 1, "lru_w_out": 0, "ffn_w_up": 2, "ffn_w_down": 1}
VECTOR_SHARDED = {"gdn_conv_w": 1, "lru_conv_w": 1, "lru_conv_b": 1, "lru_b_a": 1, "lru_b_x": 1, "lru_lambda": 1, "ffn_conv_w": 2}
SHARDED = {**MATMUL_SHARDED, **VECTOR_SHARDED}
REPLICATED = tuple(n for n in WEIGHTS if n not in SHARDED)
SQUEEZE = {"ret_gdn_w_in", "gdn_conv_w", "ret_gdn_w_out", "lru_w_in", "lru_conv_w", "lru_w_a", "lru_w_x", "lru_w_out"}
MIX_IN = MIX_MAIN + 2 * N_HEADS
BIG_ARRAYS = {
    "ret_gdn_w_in": ("ret_gdn_w_in", None, (MIX_IN, D_MODEL), (N_SHARD, MIX_IN // N_SHARD, 2, D_MODEL // 2), (2, 0, 1, 3)),
    "ret_gdn_w_out": ("ret_gdn_w_out", None, (2 * GROUP, D_MODEL), (N_SHARD, 2, GROUP // N_SHARD, D_MODEL), (1, 0, 2, 3)),
    "lru_w_in": ("lru_w_in", None, (D_MODEL, 2 * D_MODEL), (2, D_MODEL // 2, N_SHARD, 2 * D_MODEL // N_SHARD), (0, 2, 1, 3)),
    "lru_w_out": ("lru_w_out", None, (D_MODEL, D_MODEL), (N_SHARD, 2, D_MODEL // (2 * N_SHARD), D_MODEL), (1, 0, 2, 3)),
    "ffn_w_up_0": ("ffn_w_up", 0, (D_MODEL, 2 * D_FF), (2, D_MODEL // 2, N_SHARD, 2 * D_FF // N_SHARD), (0, 2, 1, 3)),
    "ffn_w_up_1": ("ffn_w_up", 1, (D_MODEL, 2 * D_FF), (2, D_MODEL // 2, N_SHARD, 2 * D_FF // N_SHARD), (0, 2, 1, 3)),
    "ffn_w_down": ("ffn_w_down", None, (2, D_FF, D_MODEL), (2, N_SHARD, D_FF // N_SHARD, D_MODEL), (0, 1, 2, 3)),
}
GATHER_GROUPS = (("ret_gdn_w_in",), ("ret_gdn_w_out", "ffn_w_up_0", "ffn_w_down"), ("lru_w_in", "lru_w_out", "ffn_w_up_1"))
REDUCE_GROUPS = (("ffn_w_up_1",), ("lru_w_in", "lru_w_out"), ("ffn_w_up_0", "ffn_w_down"), ("ret_gdn_w_out", "ret_gdn_w_in"))
BLOCK_WEIGHTS = ("lru_w_a", "lru_w_x")
GATHER_COLLECTIVE_ID = 1
REDUCE_COLLECTIVE_ID = GATHER_COLLECTIVE_ID + len(GATHER_GROUPS)


TRANSPOSED = ("ret_gdn_w_in",)


def _shard_of(array, tensors):
    weight, layer = BIG_ARRAYS[array][:2]
    t = tensors[weight]
    if weight in TRANSPOSED:
        return jnp.swapaxes(t, 1, 2)[0]
    return _local_view(weight, t) if layer is None else t[layer]


def _core_halves(array, shard):
    _, _, _, split, perm = BIG_ARRAYS[array]
    kept = [k for k in range(4) if k != perm[1]]
    order = [kept.index(perm[0]), kept.index(perm[2]), kept.index(perm[3])]
    return shard.reshape([split[k] for k in kept]).transpose(order)


def _local_view(name, a):
    if name in SQUEEZE:
        return a[0]
    if a.ndim == 1:
        return a[None, :]
    return a


def kernel(x, norm_mix, norm_ffn, ret_gdn_w_in, gdn_conv_w, gdn_a_log, gdn_dt_bias, gdn_out_gain, ret_gdn_w_out, lru_w_in, lru_conv_w, lru_conv_b, lru_w_a, lru_b_a, lru_w_x, lru_b_x, lru_lambda, lru_w_out, ffn_w_up, ffn_conv_w, ffn_conv_b, ffn_w_down, norm_final, loss_target, m_norm_mix, m_norm_ffn, m_ret_gdn_w_in, m_gdn_conv_w, m_gdn_a_log, m_gdn_dt_bias, m_gdn_out_gain, m_ret_gdn_w_out, m_lru_w_in, m_lru_conv_w, m_lru_conv_b, m_lru_w_a, m_lru_b_a, m_lru_w_x, m_lru_b_x, m_lru_lambda, m_lru_w_out, m_ffn_w_up, m_ffn_conv_w, m_ffn_conv_b, m_ffn_w_down, m_norm_final, v_norm_mix, v_norm_ffn, v_ret_gdn_w_in, v_gdn_conv_w, v_gdn_a_log, v_gdn_dt_bias, v_gdn_out_gain, v_ret_gdn_w_out, v_lru_w_in, v_lru_conv_w, v_lru_conv_b, v_lru_w_a, v_lru_b_a, v_lru_w_x, v_lru_b_x, v_lru_lambda, v_lru_w_out, v_ffn_w_up, v_ffn_conv_w, v_ffn_conv_b, v_ffn_w_down, v_norm_final):
    given = dict(norm_mix=norm_mix, norm_ffn=norm_ffn, ret_gdn_w_in=ret_gdn_w_in, gdn_conv_w=gdn_conv_w, gdn_a_log=gdn_a_log, gdn_dt_bias=gdn_dt_bias, gdn_out_gain=gdn_out_gain, ret_gdn_w_out=ret_gdn_w_out, lru_w_in=lru_w_in, lru_conv_w=lru_conv_w, lru_conv_b=lru_conv_b, lru_w_a=lru_w_a, lru_b_a=lru_b_a, lru_w_x=lru_w_x, lru_b_x=lru_b_x, lru_lambda=lru_lambda, lru_w_out=lru_w_out, ffn_w_up=ffn_w_up, ffn_conv_w=ffn_conv_w, ffn_conv_b=ffn_conv_b, ffn_w_down=ffn_w_down, norm_final=norm_final)
    mom1 = dict(norm_mix=m_norm_mix, norm_ffn=m_norm_ffn, ret_gdn_w_in=m_ret_gdn_w_in, gdn_conv_w=m_gdn_conv_w, gdn_a_log=m_gdn_a_log, gdn_dt_bias=m_gdn_dt_bias, gdn_out_gain=m_gdn_out_gain, ret_gdn_w_out=m_ret_gdn_w_out, lru_w_in=m_lru_w_in, lru_conv_w=m_lru_conv_w, lru_conv_b=m_lru_conv_b, lru_w_a=m_lru_w_a, lru_b_a=m_lru_b_a, lru_w_x=m_lru_w_x, lru_b_x=m_lru_b_x, lru_lambda=m_lru_lambda, lru_w_out=m_lru_w_out, ffn_w_up=m_ffn_w_up, ffn_conv_w=m_ffn_conv_w, ffn_conv_b=m_ffn_conv_b, ffn_w_down=m_ffn_w_down, norm_final=m_norm_final)
    mom2 = dict(norm_mix=v_norm_mix, norm_ffn=v_norm_ffn, ret_gdn_w_in=v_ret_gdn_w_in, gdn_conv_w=v_gdn_conv_w, gdn_a_log=v_gdn_a_log, gdn_dt_bias=v_gdn_dt_bias, gdn_out_gain=v_gdn_out_gain, ret_gdn_w_out=v_ret_gdn_w_out, lru_w_in=v_lru_w_in, lru_conv_w=v_lru_conv_w, lru_conv_b=v_lru_conv_b, lru_w_a=v_lru_w_a, lru_b_a=v_lru_b_a, lru_w_x=v_lru_w_x, lru_b_x=v_lru_b_x, lru_lambda=v_lru_lambda, lru_w_out=v_lru_w_out, ffn_w_up=v_ffn_w_up, ffn_conv_w=v_ffn_conv_w, ffn_conv_b=v_ffn_conv_b, ffn_w_down=v_ffn_w_down, norm_final=v_norm_final)

    local = {n: _local_view(n, a) for n, a in given.items()}

    core = lax.axis_index("c")
    chip = 2 * lax.axis_index("x") + lax.axis_index("y")
    is_my_chip = lax.broadcasted_iota(jnp.int32, (N_SHARD, 1, 1), 0) == chip

    def by_core(mine, other):
        return jnp.where(core == 0, jnp.stack([mine, other]), jnp.stack([other, mine]))

    vec_names, rp_names = list(VECTOR_SHARDED), list(REPLICATED)
    full = dict(zip(vec_names, all_gather_shards([local[n] for n in vec_names], [SHARDED[n] for n in vec_names], F32, 32, "p")))
    for n in rp_names:
        full[n] = local[n]
    in_flight = {}

    def launch(gi, after=None):
        halves = []
        for a in GATHER_GROUPS[gi]:
            halves.append(_core_halves(a, _shard_of(a, given).astype(BF16)))
        if after is not None:
            halves, after = lax.optimization_barrier((halves, after))
        in_flight[gi] = (halves,) + gather_halves(halves, name=f"gather_weights_{gi}", collective_id=GATHER_COLLECTIVE_ID + gi)
        return after

    def land(gi, after):
        halves, lands, sibs = in_flight[gi]
        (lands, sibs), after = lax.optimization_barrier(((lands, sibs), after))
        for a, mine, got, passed in zip(GATHER_GROUPS[gi], halves, lands, sibs):
            weight, layer, full_shape, split, perm = BIG_ARRAYS[a]
            half_mine = jnp.where(is_my_chip, jnp.where(core == 0, mine[0], mine[1])[None], got)
            half_other = jnp.where(is_my_chip, jnp.where(core == 0, mine[1], mine[0])[None], passed)
            value = by_core(half_mine, half_other).transpose(tuple(np.argsort(perm))).reshape(full_shape)
            if layer is None:
                full[weight] = value
            else:
                full.setdefault(weight, [None, None])[layer] = value
        return after

    reducing = {}

    def reduce_ready(gi, grads, then=None, extra=()):
        def travelling(a):
            split, perm = _travel_layout(a)
            return grads[a] if grads[a].ndim == 4 else grads[a].reshape(split).transpose(perm)

        arrays = [travelling(a) for a in REDUCE_GROUPS[gi]] + list(extra)
        scatter = [True] * len(REDUCE_GROUPS[gi]) + [False] * len(extra)
        reducing[gi], then = reduce_between_cores(arrays, scatter, tag=str(gi), collective_id=REDUCE_COLLECTIVE_ID + 3 * gi, before=then)
        return then

    def reduce_send(gi, then=None):
        reducing[gi], then = reduce_between_chips(reducing[gi], before=then)
        return then

    def stage(name, tensors, grads=None):
        if name == "start":
            launch(0)
            launch(1)
            packed["wmv"], tensors = lax.optimization_barrier((packed["wmv"], tensors))
            return land(0, tensors)
        if name == "normed":
            return launch(2, tensors)
        if name in ("mixed", "layer0"):
            return land({"mixed": 1, "layer0": 2}[name], tensors)
        gi = int(name[len("grads")])
        return reduce_ready(gi, grads, tensors) if name.endswith("_ready") else reduce_send(gi, tensors)

    small_names = [n for n in rp_names if n not in BLOCK_WEIGHTS] + vec_names
    loc_shapes = [local[n].shape for n in small_names]
    loc_rows = _pack_rows(sum(int(np.prod(s)) for s in loc_shapes), 256)
    packed = {"wmv": [_pack([src[n] for n in small_names], loc_rows, F32) for src in (given, mom1, mom2)]}

    loss_part, dx, grads = local_step(x[0], loss_target[0], full, stage)
    small_shapes = [grads[n].shape for n in small_names] + [(1, 1)]
    small_rows = _pack_rows(sum(int(np.prod(s)) for s in small_shapes), 16)
    small = _pack([grads[n] for n in small_names] + [loss_part[:, :1]], small_rows, F32).reshape(2, 1, small_rows // 2, LANES)
    last = len(REDUCE_GROUPS) - 1
    halves_of_blocks = [grads[n].reshape(2, 1, LRU_BLOCKS * HEAD // 2, HEAD) for n in BLOCK_WEIGHTS]
    reduce_ready(last, grads, extra=[small] + halves_of_blocks)
    reduce_send(last)
    reduced, result = {}, {}

    def finish(gi, after):
        g_own, g_sib = reduce_finish(reducing[gi], after)
        reduced.update(zip(list(REDUCE_GROUPS[gi]) + ["small"] + list(BLOCK_WEIGHTS), zip(g_own, g_sib)))

    def update(n):
        if n in TRANSPOSED:
            w3, m3, v3 = (jnp.swapaxes(t, 1, 2) for t in (given[n], mom1[n], mom2[n]))
            result[n] = tuple(jnp.swapaxes(t, 1, 2) for t in adamw_column_halves(w3, m3, v3, *reduced[n], name=f"adamw_{n}"))
            return
        done = None
        for a in (k for k, spec in BIG_ARRAYS.items() if spec[0] == n):
            r, cols = reduced[a][0].shape
            layer = BIG_ARRAYS[a][1] or 0
            w3, m3, v3 = (t if BIG_ARRAYS[a][1] is not None else t.reshape(1, 2 * r, cols) for t in (given[n], mom1[n], mom2[n]))
            done = adamw_halves(w3, m3, v3, *reduced[a], layer=layer, prev=done, name=f"adamw_{a}")
        result[n] = done

    for gi in range(last):
        finish(gi, (dx, reducing[last][1]))
    late = {BIG_ARRAYS[a][0] for a in REDUCE_GROUPS[last]}
    for n in MATMUL_SHARDED:
        if n not in late:
            update(n)
    finish(last, tuple(result[n][0] for n in MATMUL_SHARDED if n not in late))
    for n in MATMUL_SHARDED:
        if n in late:
            update(n)

    for n in BLOCK_WEIGHTS:
        w3, m3, v3 = (t.reshape(1, LRU_BLOCKS * HEAD, HEAD) for t in (given[n], mom1[n], mom2[n]))
        result[n] = adamw_halves(w3, m3, v3, *reduced[n], name=f"adamw_{n}")

    *small_sums, loss_sum = _unpack(by_core(*reduced["small"]).reshape(small_rows, LANES), small_shapes)
    loss = loss_sum[0, 0]
    g_small = dict(zip(small_names, small_sums))
    for n in vec_names:
        size = local[n].shape[SHARDED[n]]
        g_small[n] = lax.dynamic_slice_in_dim(g_small[n], chip * size, size, axis=SHARDED[n])
    w_pack, m_pack, v_pack = packed["wmv"]
    d_s, m_s, v_s = adamw(w_pack, _pack([g_small[n] for n in small_names], loc_rows, F32), m_pack, v_pack, name="adamw_small")
    for n, d, nm, nv in zip(small_names, _unpack(d_s, loc_shapes), _unpack(m_s, loc_shapes), _unpack(v_s, loc_shapes)):
        result[n] = (g_small[n], d, nm, nv)

    outs = [[result[n][k].reshape(given[n].shape) for n in WEIGHTS] for k in range(4)]
    return (loss, dx[None], *outs[0], *outs[1], *outs[2], *outs[3])
```

```python
import functools

import numpy as np
import jax
import jax.numpy as jnp
from jax import lax
from jax.experimental import pallas as pl
from jax.experimental.pallas import tpu as pltpu
from jax.experimental.pallas import tpu_sc as plsc

F32 = jnp.float32
BF16 = jnp.bfloat16
HI = lax.Precision.HIGHEST
MESH = pl.DeviceIdType.MESH

SEQ = 2048
D_MODEL = 1024
N_HEADS = 4
HEAD = 128
RET_CHUNK = 128
RET_CHUNKS_PER_STEP = 2
GDN_CHUNK = 64
GDN_CHUNKS_PER_STEP = 4
GROUP = N_HEADS * HEAD
MIX_MAIN = 8 * GROUP
D_FF = 2816
LRU_BLOCKS = 8
LRU_C = 8.0
ROPE_BASE = 10000.0
EPS = 1e-6
N_SHARD = 4
LANES = 128

ADAM_LR, ADAM_B1, ADAM_B2, ADAM_EPS, ADAM_WD, ADAM_STEP = 0.001, 0.9, 0.999, 1e-08, 0.01, 10

VMEM_LIMIT_BYTES = 56 * 1024 * 1024

_roll = pltpu.roll


def _params(**kw):
    return pltpu.CompilerParams(vmem_limit_bytes=VMEM_LIMIT_BYTES, **kw)


def _sds(shape, dtype):
    return jax.ShapeDtypeStruct(tuple(shape), dtype)


def _shift_raw(x, d):
    n = x.shape[0]
    t = lax.broadcasted_iota(jnp.int32, x.shape, 0)
    if d > 0:
        return jnp.where(t >= d, _roll(x, d, 0), 0.0)
    return jnp.where(t < n + d, _roll(x, n + d, 0), 0.0)


@functools.partial(jax.custom_vjp, nondiff_argnums=(1,))
def shift_rows(x, d):
    return _shift_raw(x, d)


def _shift_fwd(x, d):
    return _shift_raw(x, d), None


def _shift_bwd(d, _, g):
    return (_shift_raw(g, -d),)


shift_rows.defvjp(_shift_fwd, _shift_bwd)


@jax.custom_vjp
def swap_halves(x):
    return _roll(x, HEAD // 2, 1)


def _swap_fwd(x):
    return _roll(x, HEAD // 2, 1), None


def _swap_bwd(_, g):
    return (_roll(g, HEAD // 2, 1),)


swap_halves.defvjp(_swap_fwd, _swap_bwd)


def _scan_raw(a, u, reverse):
    n = a.shape[0]
    t = lax.broadcasted_iota(jnp.int32, a.shape, 0)
    d = 1
    while d < n:
        if reverse:
            m = t < n - d
            a_s, u_s = _roll(a, n - d, 0), _roll(u, n - d, 0)
        else:
            m = t >= d
            a_s, u_s = _roll(a, d, 0), _roll(u, d, 0)
        u = a * jnp.where(m, u_s, 0.0) + u
        a = a * jnp.where(m, a_s, 1.0)
        d *= 2
    return u


@jax.custom_vjp
def lin_scan(a, u):
    return _scan_raw(a, u, False)


def _lin_scan_fwd(a, u):
    hs = _scan_raw(a, u, False)
    return hs, (a, hs)


def _lin_scan_bwd(res, g):
    a, hs = res
    lam = _scan_raw(_shift_raw(a, -1), g, True)
    return lam * _shift_raw(hs, 1), lam


lin_scan.defvjp(_lin_scan_fwd, _lin_scan_bwd)


def _bdot(a, b, dims=(((1,), (0,)), ((), ()))):
    return lax.dot_general(a.astype(BF16), b.astype(BF16), dims, preferred_element_type=F32)


def _each(f, *seqs):
    return tuple(f(*a) for a in zip(*seqs))


def _split_bf16(a):
    hi = a.astype(BF16)
    return hi, (a - hi.astype(F32)).astype(BF16)


def _dot3_raw(a_s, b_s):
    a_hl = _each(_split_bf16, a_s)
    b_hl = _each(_split_bf16, b_s)
    hh = _each(lambda a, b: _bdot(a[0], b[0]), a_hl, b_hl)
    hl = _each(lambda a, b: _bdot(a[0], b[1]), a_hl, b_hl)
    lh = _each(lambda a, b: _bdot(a[1], b[0]), a_hl, b_hl)
    return _each(lambda x, y, z: x + (y + z), hh, hl, lh)


@jax.custom_vjp
def dot3(a_s, b_s):
    return _dot3_raw(a_s, b_s)


def _dot3_fwd(a_s, b_s):
    return _dot3_raw(a_s, b_s), (a_s, b_s)


def _dot3_bwd(res, g_s):
    a_s, b_s = res
    return (_each(lambda g, b: _bdot(g, b, (((1,), (1,)), ((), ()))), g_s, b_s),
            _each(lambda a, g: _bdot(a, g, (((0,), (0,)), ((), ()))), a_s, g_s))


dot3.defvjp(_dot3_fwd, _dot3_bwd)


def _eye(n):
    i = lax.broadcasted_iota(jnp.int32, (n, n), 0)
    j = lax.broadcasted_iota(jnp.int32, (n, n), 1)
    return (i == j).astype(F32)


def _unit_lower_inverse_raw(lmats):
    n = lmats[0].shape[0]
    eye = _eye(n)
    ps = _each(lambda l: -l, lmats)
    invs = _each(lambda x: eye + x, ps)
    k = 1
    while 2 * k < n:
        ps = _each(lambda p: _bdot(p, p), ps)
        invs = _each(lambda inv, p: inv + _bdot(inv, p), invs, ps)
        k *= 2
    prods = _dot3_raw(lmats, invs)
    resids = _each(lambda inv, pr: eye - inv - pr, invs, prods)
    return _each(lambda inv, r: inv + _bdot(inv, r), invs, resids)


@jax.custom_vjp
def unit_lower_inverse(lmats):
    return _unit_lower_inverse_raw(lmats)


def _uli_fwd(lmats):
    invs = _unit_lower_inverse_raw(lmats)
    return invs, invs


def _uli_bwd(invs, g_s):
    ms = _each(lambda inv, g: _bdot(inv, g, (((0,), (0,)), ((), ()))), invs, g_s)
    return (_each(lambda m, inv: -_bdot(m, inv, (((1,), (1,)), ((), ()))), ms, invs),)


unit_lower_inverse.defvjp(_uli_fwd, _uli_bwd)


def _cumsum_raw(x, reverse):
    n = x.shape[0]
    t = lax.broadcasted_iota(jnp.int32, x.shape, 0)
    d = 1
    while d < n:
        if reverse:
            x = x + jnp.where(t < n - d, _roll(x, n - d, 0), 0.0)
        else:
            x = x + jnp.where(t >= d, _roll(x, d, 0), 0.0)
        d *= 2
    return x


@jax.custom_vjp
def cumsum_rows(x):
    return _cumsum_raw(x, False)


def _cumsum_fwd(x):
    return _cumsum_raw(x, False), None


def _cumsum_bwd(_, g):
    return (_cumsum_raw(g, True),)


cumsum_rows.defvjp(_cumsum_fwd, _cumsum_bwd)


_NT = (((1,), (1,)), ((), ()))
_TN = (((0,), (0,)), ((), ()))


def _softplus(x):
    return jnp.maximum(x, 0.0) + jnp.log1p(jnp.exp(-jnp.abs(x)))


def _expm1_nonpos(x):
    poly = x * (1.0 + x * (0.5 + x * (1.0 / 6 + x * (1.0 / 24 + x * (1.0 / 120 + x * (1.0 / 720))))))
    return jnp.where(x > -0.25, poly, jnp.exp(x) - 1.0)


def _rms(x):
    return x * lax.rsqrt(jnp.mean(x * x, axis=-1, keepdims=True) + EPS)


def _causal_conv(x, w, width):
    y = w[width - 1:width, :] * x
    for j in range(width - 1):
        y = y + w[j:j + 1, :] * shift_rows(x, width - 1 - j)
    return y


def _norm_fn(x, g):
    return _rms(x) * g


def _ffn_act_fn(ug, uv, wg, wv, bg, bv):
    return jax.nn.silu(_causal_conv(ug, wg, 3) + bg) * (_causal_conv(uv, wv, 3) + bv)


def _gdn_conv_fn(x, w):
    return jax.nn.silu(_causal_conv(x, w, 4))


def _lru_fn(gate, x, cw, cb, wa, ba, wx, bx, lam):
    xr = _causal_conv(x, cw, 4) + cb
    r = jax.nn.sigmoid(_bdot(xr, wa) + ba)
    i = jax.nn.sigmoid(_bdot(xr, wx) + bx)
    log_a = -LRU_C * r * _softplus(-lam)
    a = jnp.exp(log_a)
    u = jnp.sqrt(-_expm1_nonpos(2.0 * log_a)) * (i * xr)
    hs = lin_scan(a, u)
    return jax.nn.gelu(gate) * hs


def _ret_fn(qs, ks, vs, gates, states, cos2, sin2, dmasks, ktails, qdecs, cdecs):
    c = RET_CHUNK
    n_heads = len(qs)
    n_chunks = qs[0].shape[0] // c
    units = tuple((ci, h) for ci in range(n_chunks) for h in range(n_heads))

    def rows(x, ci):
        return x[ci * c:(ci + 1) * c]

    qrs = tuple(rows(qs[h], ci) * rows(cos2, ci) + swap_halves(rows(qs[h], ci)) * rows(sin2, ci) for ci, h in units)
    krs = tuple((rows(ks[h], ci) * rows(cos2, ci) + swap_halves(rows(ks[h], ci)) * rows(sin2, ci)) * (HEAD ** -0.5) for ci, h in units)
    vus = tuple(rows(vs[h], ci) for ci, h in units)
    scores = tuple(_bdot(q, k, _NT) * dmasks[h] for q, k, (_, h) in zip(qrs, krs, units))
    intra = _each(lambda sc, v: _bdot(sc, v), scores, vus)
    outs = []
    for ci in range(n_chunks):
        mine = slice(ci * n_heads, (ci + 1) * n_heads)
        inter = _each(lambda q, d, s: _bdot(q * d, s), qrs[mine], qdecs, states)
        outs.append(_each(lambda a, b: a + b, intra[mine], inter))
        states = _each(lambda s, cd, k, kt, v: s * cd + _bdot(k * kt, v, _TN), states, cdecs, krs[mine], ktails, vus[mine])
    ys = tuple(_rms(jnp.concatenate([outs[ci][h] for ci in range(n_chunks)], axis=0)) * jax.nn.silu(gates[h]) for h in range(n_heads))
    return ys, states


def _pick_lane(x, lane_idx):
    lane = lax.broadcasted_iota(jnp.int32, x.shape, 1)
    return jnp.sum(jnp.where(lane == lane_idx, x, 0.0), axis=1, keepdims=True)


def _l2norm(x):
    return x * lax.rsqrt(jnp.sum(x * x, axis=-1, keepdims=True) + EPS)


def _gdn_fn(qcs, kcs, vcs, gates, small, a_log, dt_bias, gain, states):
    c = GDN_CHUNK
    n_heads = len(qcs)
    n_chunks = qcs[0].shape[0] // c
    units = tuple((ci, h) for ci in range(n_chunks) for h in range(n_heads))

    def unit_rows(per_head):
        return tuple(per_head[h][ci * c:(ci + 1) * c] for ci, h in units)

    smalls = tuple(small[ci * c:(ci + 1) * c] for ci, _ in units)
    heads = tuple(h for _, h in units)
    intra = _gdn_intra(unit_rows(qcs), unit_rows(kcs), unit_rows(vcs), smalls, heads, a_log, dt_bias)
    outs = []
    for ci in range(n_chunks):
        mine = slice(ci * n_heads, (ci + 1) * n_heads)
        os_, states = _gdn_inter(*(part[mine] for part in intra), states)
        outs.append(os_)
    ys = tuple(_rms(jnp.concatenate([outs[ci][h] for ci in range(n_chunks)], axis=0)) * gain * jax.nn.silu(gates[h])
               for h in range(n_heads))
    return ys, states


def _gdn_inter(qs, ks, us, ws, attns, gcs, g_lasts, states):
    v_news = _each(lambda u, w, s: u - _bdot(w, s), us, ws, states)
    inter = _each(lambda q, gc, s: _bdot(q * jnp.exp(gc), s), qs, gcs, states)
    os_ = _each(lambda x, a, v: x + _bdot(a, v), inter, attns, v_news)
    new_states = _each(lambda s, gl, k, gc, v: s * jnp.exp(gl) + _bdot(k * jnp.exp(gl - gc), v, _TN), states, g_lasts, ks, gcs, v_news)
    return os_, new_states


def _gdn_intra(qcs, kcs, vcs, smalls, heads, a_log, dt_bias):
    c = GDN_CHUNK
    qs = _each(lambda x: _l2norm(x) * (HEAD ** -0.5), qcs)
    ks = _each(_l2norm, kcs)
    betas = _each(lambda sm, h: jax.nn.sigmoid(_pick_lane(sm, h)), smalls, heads)
    gs = _each(lambda sm, h: -jnp.exp(_pick_lane(a_log, h)) * _softplus(_pick_lane(sm, h + N_HEADS) + _pick_lane(dt_bias, h)),
               smalls, heads)
    i = lax.broadcasted_iota(jnp.int32, (c, c), 0)
    j = lax.broadcasted_iota(jnp.int32, (c, c), 1)
    tril = i >= j
    gcs = _each(lambda g: cumsum_rows(jnp.broadcast_to(g, (c, LANES)))[:, :1], gs)
    gc_rows = _each(lambda gc: jnp.broadcast_to(gc, (c, c)), gcs)
    decays = _each(lambda r: jnp.where(tril, jnp.exp(jnp.where(tril, r - r.T, 0.0)), 0.0), gc_rows)
    kbs = _each(lambda k, b: k * b, ks, betas)
    lmats = _each(lambda kb, k, d: jnp.where(i > j, _bdot(kb, k, _NT) * d, 0.0), kbs, ks, decays)
    attns = _each(lambda q, k, d: jnp.where(tril, _bdot(q, k, _NT) * d, 0.0), qs, ks, decays)
    invs = unit_lower_inverse(lmats)
    us = dot3(invs, _each(lambda v, b: v * b, vcs, betas))
    ws = dot3(invs, _each(lambda kb, gc: kb * jnp.exp(gc), kbs, gcs))
    g_lasts = _each(lambda g: jnp.sum(g, axis=0, keepdims=True), gs)
    return qs, ks, us, ws, attns, gcs, g_lasts


def _final_fn(h, g, target):
    y = _rms(h) * g
    return 0.5 * jnp.sum(jnp.mean(jnp.square(y - target), axis=-1, keepdims=True), axis=0, keepdims=True)


def _tile(n, candidates):
    for t in candidates:
        if n % t == 0:
            return t
    raise ValueError(f"no tile for {n}")


def matmul(a, b, *, ta=False, tb=False, add=None, out_dtype=F32, tm=None, tn=None, split=None, layer=None, name):
    m = a.shape[1] if ta else a.shape[0]
    k = a.shape[0] if ta else a.shape[1]
    n = b.shape[0] if tb else b.shape[1]
    assert k == (b.shape[1] if tb else b.shape[0])
    out_shape, out_block, out_index = (m, n), None, lambda i, j: (i, j)
    if split is not None:
        dims4, perm = split
        out_shape = tuple(dims4[p] for p in perm)
        r, cols = out_shape[2:]
        tm, tn = m, tn or _tile(cols, (1408, 512))
        cb = cols // tn
        if perm == (0, 2, 1, 3):
            out_block, out_index = (2, None, r, tn), lambda i, j: (0, j // cb, 0, j % cb)
        elif perm == (1, 0, 2, 3):
            out_block, out_index = (2, N_SHARD, r, tn), lambda i, j: (0, 0, 0, j)
        else:
            raise ValueError(perm)
    tm = tm or _tile(m, (1024, 512, 1408, 256, 128))
    tn = tn or _tile(n, (512, 1408, 256, 128))
    aliases, prev = {}, None
    if layer is not None:
        index, count, prev = layer
        out_shape, out_block, out_index = (count, m, n), (None, tm, tn), lambda i, j: (index, i, j)
    dims = (((0 if ta else 1,), (1 if tb else 0,)), ((), ()))

    def body(a_ref, b_ref, *rest):
        acc = lax.dot_general(a_ref[...].astype(BF16), b_ref[...].astype(BF16), dims, preferred_element_type=F32)
        if add is not None:
            acc = acc + rest[0][...]
        o_ref = rest[-1]
        acc = acc.astype(out_dtype)
        if split is not None and split[1] == (1, 0, 2, 3):
            rows = o_ref.shape[2]
            for s in range(N_SHARD):
                for h in range(2):
                    o_ref[h, s] = acc[(2 * s + h) * rows:(2 * s + h + 1) * rows]
        else:
            o_ref[...] = acc.reshape(o_ref.shape)

    a_spec = pl.BlockSpec((k, tm), lambda i, j: (0, i)) if ta else pl.BlockSpec((tm, k), lambda i, j: (i, 0))
    b_spec = pl.BlockSpec((tn, k), lambda i, j: (j, 0)) if tb else pl.BlockSpec((k, tn), lambda i, j: (0, j))
    o_spec = pl.BlockSpec(out_block or (tm, tn), out_index)
    in_specs, args = [a_spec, b_spec], [a, b]
    if add is not None:
        in_specs.append(o_spec)
        args.append(add)
    if prev is not None:
        aliases = {len(args): 0}
        in_specs.append(pl.BlockSpec(memory_space=pl.ANY))
        args.append(prev)
    return pl.pallas_call(body, out_shape=_sds(out_shape, out_dtype), grid=(m // tm, n // tn), in_specs=in_specs,
                          out_specs=o_spec, input_output_aliases=aliases, compiler_params=_params(), name=name)(*args)


ROW_TILE = 256


def norm_fwd(x, g, *, name):
    t, d = x.shape

    def body(x_ref, g_ref, o_ref):
        o_ref[...] = _norm_fn(x_ref[...], g_ref[...]).astype(BF16)

    return pl.pallas_call(body, out_shape=_sds((t, d), BF16), grid=(t // ROW_TILE,),
                          in_specs=[pl.BlockSpec((ROW_TILE, d), lambda i: (i, 0)), pl.BlockSpec((1, d), lambda i: (0, 0))],
                          out_specs=pl.BlockSpec((ROW_TILE, d), lambda i: (i, 0)), compiler_params=_params(), name=name)(x, g)


def norm_bwd(x, g, dy, dres, *, name):
    t, d = x.shape

    def body(x_ref, g_ref, dy_ref, dres_ref, dx_ref, dg_ref):
        _, vjp = jax.vjp(_norm_fn, x_ref[...], g_ref[...])
        dx, dg = vjp(dy_ref[...])
        dx_ref[...] = dx + dres_ref[...]

        @pl.when(pl.program_id(0) == 0)
        def _():
            dg_ref[...] = jnp.zeros_like(dg_ref)

        dg_ref[...] += dg

    row = pl.BlockSpec((ROW_TILE, d), lambda i: (i, 0))
    vec = pl.BlockSpec((1, d), lambda i: (0, 0))
    return pl.pallas_call(body, out_shape=(_sds((t, d), F32), _sds((1, d), F32)), grid=(t // ROW_TILE,),
                          in_specs=[row, vec, row, row], out_specs=(row, vec), compiler_params=_params(), name=name)(x, g, dy, dres)


def final_fwd_bwd(h, g, target, *, name):
    t, d = h.shape

    def body(h_ref, g_ref, t_ref, loss_ref, dh_ref, dg_ref):
        tgt = t_ref[...]
        loss, vjp = jax.vjp(lambda hh, gg: _final_fn(hh, gg, tgt), h_ref[...], g_ref[...])
        dh, dg = vjp(jnp.ones((1, 1), F32))
        dh_ref[...] = dh

        @pl.when(pl.program_id(0) == 0)
        def _():
            dg_ref[...] = jnp.zeros_like(dg_ref)
            loss_ref[...] = jnp.zeros_like(loss_ref)

        dg_ref[...] += dg
        loss_ref[...] += jnp.broadcast_to(loss, loss_ref.shape)

    row = pl.BlockSpec((ROW_TILE, d), lambda i: (i, 0))
    vec = pl.BlockSpec((1, d), lambda i: (0, 0))
    return pl.pallas_call(body, out_shape=(_sds((1, LANES), F32), _sds((t, d), F32), _sds((1, d), F32)), grid=(t // ROW_TILE,),
                          in_specs=[row, vec, row], out_specs=(pl.BlockSpec((1, LANES), lambda i: (0, 0)), row, vec),
                          compiler_params=_params(), name=name)(h, g, target)


FFN_FWD_COLS = 256
FFN_BWD_COLS = 128


def ffn_act_fwd(u, cw, cb, *, name):
    t = u.shape[0]
    w = FFN_FWD_COLS
    nb = D_FF // w

    def body(ug_ref, uv_ref, wg_ref, wv_ref, bg_ref, bv_ref, o_ref):
        o_ref[...] = _ffn_act_fn(ug_ref[...], uv_ref[...], wg_ref[...], wv_ref[...], bg_ref[...], bv_ref[...]).astype(BF16)

    def col(rows, off):
        return pl.BlockSpec((rows, w), lambda j: (0, j + off))

    return pl.pallas_call(body, out_shape=_sds((t, D_FF), BF16), grid=(nb,),
                          in_specs=[col(t, 0), col(t, nb), col(3, 0), col(3, nb), col(1, 0), col(1, nb)],
                          out_specs=col(t, 0), compiler_params=_params(), name=name)(u, u, cw, cw, cb, cb)


def _put_column_blocks(step, n_steps, blocks, dst_ref, width, stage_ref, sems):
    def copies(at):
        slot = at % 2
        return [pltpu.make_async_copy(stage_ref.at[slot, p], dst_ref.at[:, pl.ds(pl.multiple_of((p * n_steps + at) * width, LANES), width)],
                                      sems.at[slot, p]) for p in range(len(blocks))]

    @pl.when(step >= 2)
    def _():
        for cp in copies(step - 2):
            cp.wait()

    for p, value in enumerate(blocks):
        stage_ref[step % 2, p] = value
    for cp in copies(step):
        cp.start()

    @pl.when(step == n_steps - 1)
    def _():
        for cp in copies(step - 1) + copies(step):
            cp.wait()


def ffn_act_bwd(u, cw, cb, da, *, name):
    t = u.shape[0]
    w = FFN_BWD_COLS
    nb = D_FF // w

    def body(ug_ref, uv_ref, wg_ref, wv_ref, bg_ref, bv_ref, da_ref, dug_ref, duv_ref, dwg_ref, dwv_ref, dbg_ref, dbv_ref):
        _, vjp = jax.vjp(_ffn_act_fn, ug_ref[...], uv_ref[...], wg_ref[...], wv_ref[...], bg_ref[...], bv_ref[...])
        dug, duv, dwg, dwv, dbg, dbv = vjp(da_ref[...])
        dug_ref[...] = dug.astype(BF16)
        duv_ref[...] = duv.astype(BF16)
        dwg_ref[...] = dwg
        dwv_ref[...] = dwv
        dbg_ref[...] = dbg
        dbv_ref[...] = dbv

    def col(rows, off):
        return pl.BlockSpec((rows, w), lambda j: (0, j + off))

    outs = pl.pallas_call(
        body, out_shape=(_sds((t, D_FF), BF16), _sds((t, D_FF), BF16), _sds((3, D_FF), F32), _sds((3, D_FF), F32),
                         _sds((1, D_FF), F32), _sds((1, D_FF), F32)),
        grid=(nb,), in_specs=[col(t, 0), col(t, nb), col(3, 0), col(3, nb), col(1, 0), col(1, nb), col(t, 0)],
        out_specs=(col(t, 0), col(t, 0), col(3, 0), col(3, 0), col(1, 0), col(1, 0)), compiler_params=_params(), name=name,
    )(u, u, cw, cw, cb, cb, da)
    dug, duv, dwg, dwv, dbg, dbv = outs
    return jnp.concatenate([dug, duv], axis=1), jnp.concatenate([dwg, dwv], axis=1), jnp.concatenate([dbg, dbv], axis=1)


GDN_CONV_COLS = 256
GDN_CONV_OFF = 4 * GROUP


def gdn_conv_fwd(p, cw, *, name):
    t = p.shape[0]
    w = GDN_CONV_COLS
    nb = 3 * GROUP // w
    off = GDN_CONV_OFF // w

    def body(x_ref, w_ref, o_ref):
        o_ref[...] = _gdn_conv_fn(x_ref[...], w_ref[...])

    return pl.pallas_call(body, out_shape=_sds((t, 3 * GROUP), F32), grid=(nb,),
                          in_specs=[pl.BlockSpec((t, w), lambda j: (0, j + off)), pl.BlockSpec((4, w), lambda j: (0, j))],
                          out_specs=pl.BlockSpec((t, w), lambda j: (0, j)), compiler_params=_params(), name=name)(p, cw)


def gdn_conv_bwd(p, cw, dc, *, name):
    t = p.shape[0]
    w = GDN_CONV_COLS
    nb = 3 * GROUP // w
    off = GDN_CONV_OFF // w

    def body(x_ref, w_ref, dc_ref, dx_ref, dw_ref):
        _, vjp = jax.vjp(_gdn_conv_fn, x_ref[...], w_ref[...])
        dx, dw = vjp(dc_ref[...])
        dx_ref[...] = dx.astype(BF16)
        dw_ref[...] = dw

    blk = pl.BlockSpec((t, w), lambda j: (0, j))
    wblk = pl.BlockSpec((4, w), lambda j: (0, j))
    return pl.pallas_call(body, out_shape=(_sds((t, 3 * GROUP), BF16), _sds((4, 3 * GROUP), F32)), grid=(nb,),
                          in_specs=[pl.BlockSpec((t, w), lambda j: (0, j + off)), wblk, blk], out_specs=(blk, wblk),
                          compiler_params=_params(), name=name)(p, cw, dc)


def _lru_specs(t):
    w = D_MODEL // LRU_BLOCKS
    gate = pl.BlockSpec((t, w), lambda j: (0, j))
    xin = pl.BlockSpec((t, w), lambda j: (0, j + LRU_BLOCKS))
    cw = pl.BlockSpec((4, w), lambda j: (0, j))
    vec = pl.BlockSpec((1, w), lambda j: (0, j))
    mat = pl.BlockSpec((None, w, w), lambda j: (j, 0, 0))
    return gate, xin, cw, vec, mat


def lru_fwd(gx, cw, cb, wa, ba, wx, bx, lam, *, name):
    t = gx.shape[0]
    gate, xin, cws, vec, mat = _lru_specs(t)

    def body(g_ref, x_ref, cw_ref, cb_ref, wa_ref, ba_ref, wx_ref, bx_ref, lam_ref, o_ref):
        o_ref[...] = _lru_fn(g_ref[...], x_ref[...], cw_ref[...], cb_ref[...], wa_ref[...], ba_ref[...], wx_ref[...],
                             bx_ref[...], lam_ref[...]).astype(BF16)

    return pl.pallas_call(body, out_shape=_sds((t, D_MODEL), BF16), grid=(LRU_BLOCKS,),
                          in_specs=[gate, xin, cws, vec, mat, vec, mat, vec, vec], out_specs=gate,
                          compiler_params=_params(), name=name)(gx, gx, cw, cb, wa, ba, wx, bx, lam)


def lru_bwd(gx, cw, cb, wa, ba, wx, bx, lam, dy, *, name):
    t = gx.shape[0]
    gate, xin, cws, vec, mat = _lru_specs(t)

    def body(g_ref, x_ref, cw_ref, cb_ref, wa_ref, ba_ref, wx_ref, bx_ref, lam_ref, dy_ref,
             dgx_ref, dcw_ref, dcb_ref, dwa_ref, dba_ref, dwx_ref, dbx_ref, dlam_ref, stage_ref, sems):
        _, vjp = jax.vjp(_lru_fn, g_ref[...], x_ref[...], cw_ref[...], cb_ref[...], wa_ref[...], ba_ref[...], wx_ref[...],
                         bx_ref[...], lam_ref[...])
        dg, dx, dcw, dcb, dwa, dba, dwx, dbx, dlam = vjp(dy_ref[...])
        _put_column_blocks(pl.program_id(0), LRU_BLOCKS, (dg.astype(BF16), dx.astype(BF16)), dgx_ref, D_MODEL // LRU_BLOCKS, stage_ref, sems)
        dcw_ref[...] = dcw
        dcb_ref[...] = dcb
        dwa_ref[...] = dwa
        dba_ref[...] = dba
        dwx_ref[...] = dwx
        dbx_ref[...] = dbx
        dlam_ref[...] = dlam

    d = D_MODEL
    w = d // LRU_BLOCKS
    out_shape = (_sds((t, 2 * d), BF16), _sds((4, d), F32), _sds((1, d), F32), _sds((LRU_BLOCKS, w, w), F32),
                 _sds((1, d), F32), _sds((LRU_BLOCKS, w, w), F32), _sds((1, d), F32), _sds((1, d), F32))
    return pl.pallas_call(body, out_shape=out_shape, grid=(LRU_BLOCKS,),
                          in_specs=[gate, xin, cws, vec, mat, vec, mat, vec, vec, gate],
                          out_specs=(pl.BlockSpec(memory_space=pl.ANY), cws, vec, mat, vec, mat, vec, vec),
                          scratch_shapes=[pltpu.VMEM((2, 2, t, w), BF16), pltpu.SemaphoreType.DMA((2, 2))],
                          compiler_params=_params(), name=name)(gx, gx, cw, cb, wa, ba, wx, bx, lam, dy)


def _ret_tables():
    half = HEAD // 2
    inv_freq = (np.float32(ROPE_BASE) ** (-np.arange(half, dtype=np.float32) / np.float32(half))).astype(np.float32)
    ang = (np.arange(SEQ, dtype=np.float32)[:, None] * inv_freq[None, :]).astype(np.float64)
    cos2 = np.concatenate([np.cos(ang), np.cos(ang)], axis=1).astype(np.float32)
    sin2 = np.concatenate([-np.sin(ang), np.sin(ang)], axis=1).astype(np.float32)
    c = RET_CHUNK
    log_gamma = np.log1p(-np.exp2(-5.0 - np.arange(N_HEADS, dtype=np.float64)))
    idx = np.arange(c, dtype=np.float64)
    rel = idx[:, None] - idx[None, :]
    dmask = np.where(rel >= 0, np.exp(log_gamma[:, None, None] * np.maximum(rel, 0.0)), 0.0)
    ones = np.ones((N_HEADS, c, HEAD))
    ktail = np.exp(log_gamma[:, None] * (c - 1 - idx))[:, :, None] * ones
    qdec = np.exp(log_gamma[:, None] * (idx + 1.0))[:, :, None] * ones
    cdec = np.exp(log_gamma * c)[:, None, None] * ones
    return tuple(jnp.asarray(a, F32) for a in (cos2, sin2, dmask, ktail, qdec, cdec))


def _ret_specs(rev):
    c = RET_CHUNK * RET_CHUNKS_PER_STEP
    nc = SEQ // c

    def n_of(n):
        return nc - 1 - n if rev else n

    def group(off):
        return pl.BlockSpec((c, GROUP), lambda n: (n_of(n), off))

    tab = pl.BlockSpec((c, HEAD), lambda n: (n_of(n), 0))
    const = pl.BlockSpec((N_HEADS, RET_CHUNK, HEAD), lambda n: (0, 0, 0))
    state = pl.BlockSpec((N_HEADS, None, HEAD, HEAD), lambda n: (0, n_of(n), 0, 0))
    return group, tab, const, state, nc


def _head(h):
    return slice(h * HEAD, (h + 1) * HEAD)


def ret_fwd(p, tables, *, name):
    group, tab, const, state, nc = _ret_specs(False)

    def body(q_ref, k_ref, v_ref, g_ref, cos_ref, sin_ref, dm_ref, kt_ref, qd_ref, cd_ref, y_ref, st_ref, s_scr):
        @pl.when(pl.program_id(0) == 0)
        def _():
            s_scr[...] = jnp.zeros_like(s_scr)

        heads = range(N_HEADS)
        states = tuple(s_scr[h] for h in heads)
        ys, new_states = _ret_fn(*(tuple(r[:, _head(h)] for h in heads) for r in (q_ref, k_ref, v_ref, g_ref)), states,
                                 cos_ref[...], sin_ref[...], *(tuple(r[h] for h in heads) for r in (dm_ref, kt_ref, qd_ref, cd_ref)))
        for h in heads:
            st_ref[h] = states[h]
            y_ref[:, _head(h)] = ys[h].astype(BF16)
            s_scr[h] = new_states[h]

    return pl.pallas_call(
        body, out_shape=(_sds((SEQ, GROUP), BF16), _sds((N_HEADS, nc, HEAD, HEAD), F32)), grid=(nc,),
        in_specs=[group(0), group(1), group(2), group(3), tab, tab, const, const, const, const],
        out_specs=(group(0), state), scratch_shapes=[pltpu.VMEM((N_HEADS, HEAD, HEAD), F32)], compiler_params=_params(), name=name,
    )(p, p, p, p, *tables)


def ret_bwd(p, tables, states, dy, *, name):
    group, tab, const, state, nc = _ret_specs(True)

    def body(q_ref, k_ref, v_ref, g_ref, cos_ref, sin_ref, dm_ref, kt_ref, qd_ref, cd_ref, st_ref, dy_ref,
             dq_ref, dk_ref, dv_ref, dg_ref, ds_scr):
        @pl.when(pl.program_id(0) == 0)
        def _():
            ds_scr[...] = jnp.zeros_like(ds_scr)

        heads = range(N_HEADS)
        consts = (cos_ref[...], sin_ref[...], *(tuple(r[h] for h in heads) for r in (dm_ref, kt_ref, qd_ref, cd_ref)))
        _, vjp = jax.vjp(lambda *a: _ret_fn(*a, *consts), *(tuple(r[:, _head(h)] for h in heads) for r in (q_ref, k_ref, v_ref, g_ref)),
                         tuple(st_ref[h] for h in heads))
        dqs, dks, dvs, dgs, dss = vjp((tuple(dy_ref[:, _head(h)] for h in heads), tuple(ds_scr[h] for h in heads)))
        for h in heads:
            dq_ref[:, _head(h)] = dqs[h].astype(BF16)
            dk_ref[:, _head(h)] = dks[h].astype(BF16)
            dv_ref[:, _head(h)] = dvs[h].astype(BF16)
            dg_ref[:, _head(h)] = dgs[h].astype(BF16)
            ds_scr[h] = dss[h]

    out = _sds((SEQ, GROUP), BF16)
    return pl.pallas_call(
        body, out_shape=(out, out, out, out), grid=(nc,),
        in_specs=[group(0), group(1), group(2), group(3), tab, tab, const, const, const, const, state, group(0)],
        out_specs=(group(0), group(0), group(0), group(0)), scratch_shapes=[pltpu.VMEM((N_HEADS, HEAD, HEAD), F32)],
        compiler_params=_params(), name=name,
    )(p, p, p, p, *tables, states, dy)


def _gdn_specs(rev):
    c = GDN_CHUNK * GDN_CHUNKS_PER_STEP
    nc = SEQ // c

    def n_of(n):
        return nc - 1 - n if rev else n

    def group(off):
        return pl.BlockSpec((c, GROUP), lambda n: (n_of(n), off))

    small = pl.BlockSpec((c, LANES), lambda n: (n_of(n), 0))
    vec = pl.BlockSpec((1, LANES), lambda n: (0, 0))
    state = pl.BlockSpec((N_HEADS, None, HEAD, HEAD), lambda n: (0, n_of(n), 0, 0))
    return group, small, vec, state, nc


GDN_GATE_GROUP = 7


def gdn_fwd(conv, p, small, a_log, dt_bias, gain, *, name):
    group, sm, vec, state, nc = _gdn_specs(False)

    def body(q_ref, k_ref, v_ref, g_ref, sm_ref, al_ref, dt_ref, gn_ref, y_ref, st_ref, s_scr):
        @pl.when(pl.program_id(0) == 0)
        def _():
            s_scr[...] = jnp.zeros_like(s_scr)

        states = tuple(s_scr[h] for h in range(N_HEADS))
        ys, new_states = _gdn_fn(*(tuple(r[:, _head(h)] for h in range(N_HEADS)) for r in (q_ref, k_ref, v_ref, g_ref)),
                                 sm_ref[...], al_ref[...], dt_ref[...], gn_ref[...], states)
        for h in range(N_HEADS):
            st_ref[h] = states[h]
            y_ref[:, _head(h)] = ys[h].astype(BF16)
            s_scr[h] = new_states[h]

    return pl.pallas_call(
        body, out_shape=(_sds((SEQ, GROUP), BF16), _sds((N_HEADS, nc, HEAD, HEAD), F32)), grid=(nc,),
        in_specs=[group(0), group(1), group(2), group(GDN_GATE_GROUP), sm, vec, vec, vec], out_specs=(group(0), state),
        scratch_shapes=[pltpu.VMEM((N_HEADS, HEAD, HEAD), F32)], compiler_params=_params(), name=name,
    )(conv, conv, conv, p, small, a_log, dt_bias, gain)


def gdn_bwd(conv, p, small, a_log, dt_bias, gain, states, dy, *, name):
    group, sm, vec, state, nc = _gdn_specs(True)

    def body(q_ref, k_ref, v_ref, g_ref, sm_ref, al_ref, dt_ref, gn_ref, st_ref, dy_ref,
             dq_ref, dk_ref, dv_ref, dg_ref, dsm_ref, dal_ref, ddt_ref, dgn_ref, ds_scr):
        @pl.when(pl.program_id(0) == 0)
        def _():
            ds_scr[...] = jnp.zeros_like(ds_scr)
            dal_ref[...] = jnp.zeros_like(dal_ref)
            ddt_ref[...] = jnp.zeros_like(ddt_ref)
            dgn_ref[...] = jnp.zeros_like(dgn_ref)

        per_head = tuple(tuple(r[:, _head(h)] for h in range(N_HEADS)) for r in (q_ref, k_ref, v_ref, g_ref))
        _, vjp = jax.vjp(_gdn_fn, *per_head, sm_ref[...], al_ref[...], dt_ref[...], gn_ref[...],
                         tuple(st_ref[h] for h in range(N_HEADS)))
        cts = (tuple(dy_ref[:, _head(h)] for h in range(N_HEADS)), tuple(ds_scr[h] for h in range(N_HEADS)))
        dqs, dks, dvs, dgs, dsm, dal, ddt, dgn, dss = vjp(cts)
        for h in range(N_HEADS):
            dq_ref[:, _head(h)] = dqs[h]
            dk_ref[:, _head(h)] = dks[h]
            dv_ref[:, _head(h)] = dvs[h]
            dg_ref[:, _head(h)] = dgs[h].astype(BF16)
            ds_scr[h] = dss[h]
        dsm_ref[...] = dsm
        dal_ref[...] += dal
        ddt_ref[...] += ddt
        dgn_ref[...] += dgn

    f = _sds((SEQ, GROUP), F32)
    pv = _sds((1, LANES), F32)
    return pl.pallas_call(
        body, out_shape=(f, f, f, _sds((SEQ, GROUP), BF16), _sds((SEQ, LANES), F32), pv, pv, pv), grid=(nc,),
        in_specs=[group(0), group(1), group(2), group(GDN_GATE_GROUP), sm, vec, vec, vec, state, group(1)],
        out_specs=(group(0), group(0), group(0), group(0), sm, vec, vec, vec), scratch_shapes=[pltpu.VMEM((N_HEADS, HEAD, HEAD), F32)],
        compiler_params=_params(), name=name,
    )(conv, conv, conv, p, small, a_log, dt_bias, gain, states, dy)


PACK_ROW_TILE = 1024


def adamw(w, g, m, v, *, name):
    r = w.shape[0]
    tr = _row_tile(r, LANES)

    def body(w_ref, g_ref, m_ref, v_ref, d_ref, nm_ref, nv_ref):
        gg = g_ref[...]
        nm = ADAM_B1 * m_ref[...] + (1.0 - ADAM_B1) * gg
        nv = ADAM_B2 * v_ref[...] + (1.0 - ADAM_B2) * jnp.square(gg)
        m_hat = nm / (1.0 - ADAM_B1 ** ADAM_STEP)
        v_hat = nv / (1.0 - ADAM_B2 ** ADAM_STEP)
        d_ref[...] = -ADAM_LR * (m_hat / (jnp.sqrt(v_hat) + ADAM_EPS) + ADAM_WD * w_ref[...])
        nm_ref[...] = nm
        nv_ref[...] = nv

    blk = pl.BlockSpec((tr, LANES), lambda i: (i, 0))
    o = _sds((r, LANES), F32)
    return pl.pallas_call(body, out_shape=(o, o, o), grid=(r // tr,), in_specs=[blk] * 4, out_specs=(blk, blk, blk),
                          compiler_params=_params(), name=name)(w, g, m, v)


ELEMENTWISE_BLOCK_BYTES = 2 * 1024 * 1024


def _row_tile(r, c):
    best = None
    for tr in range(8, r + 1, 8):
        if r % tr == 0 and tr * c * 4 <= ELEMENTWISE_BLOCK_BYTES:
            best = tr
    if best is None:
        raise ValueError(f"no row tile for ({r}, {c})")
    return best


def _tile_2d(r, c):
    if any(r % tr == 0 for tr in range(8, r + 1, 8)):
        return _row_tile(r, c), c
    tc = max(t for t in range(LANES, c + 1, LANES) if c % t == 0 and r * t * 4 <= ELEMENTWISE_BLOCK_BYTES)
    return r, tc


def _core_index():
    return lax.axis_index("c").astype(jnp.int32).reshape(1)


def _chip_index():
    return (2 * lax.axis_index("x") + lax.axis_index("y")).astype(jnp.int32).reshape(1)


def adamw_halves(w, m, v, g_own, g_sib, *, layer=0, prev=None, name):
    n_layers, rows, c = w.shape
    r = rows // 2
    tr = _row_tile(r, c)
    nb = r // tr

    def body(c_ref, w_ref, m_ref, v_ref, own_ref, sib_ref, *rest):
        g_ref, d_ref, nm_ref, nv_ref = rest[-4:]
        gg = jnp.where(pl.program_id(0) == c_ref[0], own_ref[...], sib_ref[...])
        nm = ADAM_B1 * m_ref[...] + (1.0 - ADAM_B1) * gg
        nv = ADAM_B2 * v_ref[...] + (1.0 - ADAM_B2) * jnp.square(gg)
        m_hat = nm / (1.0 - ADAM_B1 ** ADAM_STEP)
        v_hat = nv / (1.0 - ADAM_B2 ** ADAM_STEP)
        g_ref[...] = gg
        d_ref[...] = -ADAM_LR * (m_hat / (jnp.sqrt(v_hat) + ADAM_EPS) + ADAM_WD * w_ref[...])
        nm_ref[...] = nm
        nv_ref[...] = nv

    full = pl.BlockSpec((None, tr, c), lambda h, i, cr: (layer, h * nb + i, 0))
    half = pl.BlockSpec((tr, c), lambda h, i, cr: (i, 0))
    o = _sds((n_layers, rows, c), F32)
    prev = list(prev or ())
    gs = pltpu.PrefetchScalarGridSpec(num_scalar_prefetch=1, grid=(2, nb), in_specs=[full, full, full, half, half] + [_ANY] * len(prev),
                                      out_specs=(full, full, full, full))
    n_fixed = 6
    return pl.pallas_call(body, out_shape=(o, o, o, o), grid_spec=gs, compiler_params=_params(), name=name,
                          input_output_aliases={n_fixed + k: k for k in range(len(prev))})(
        _core_index(), w, m, v, g_own, g_sib, *prev)


ADAMW_COLUMN_TILE = 256


def adamw_column_halves(w, m, v, g_own, g_sib, *, name):
    _, rows, cols = w.shape
    tc = ADAMW_COLUMN_TILE
    per_half = cols // 2 // tc

    def body(c_ref, w_ref, m_ref, v_ref, own_ref, sib_ref, g_ref, d_ref, nm_ref, nv_ref):
        gg = jnp.where(pl.program_id(0) // per_half == c_ref[0], own_ref[...], sib_ref[...])
        nm = ADAM_B1 * m_ref[...] + (1.0 - ADAM_B1) * gg
        nv = ADAM_B2 * v_ref[...] + (1.0 - ADAM_B2) * jnp.square(gg)
        m_hat = nm / (1.0 - ADAM_B1 ** ADAM_STEP)
        v_hat = nv / (1.0 - ADAM_B2 ** ADAM_STEP)
        g_ref[...] = gg
        d_ref[...] = -ADAM_LR * (m_hat / (jnp.sqrt(v_hat) + ADAM_EPS) + ADAM_WD * w_ref[...])
        nm_ref[...] = nm
        nv_ref[...] = nv

    full = pl.BlockSpec((None, rows, tc), lambda j, cr: (0, 0, j))
    half = pl.BlockSpec((rows, tc), lambda j, cr: (0, j % per_half))
    o = _sds(w.shape, F32)
    gs = pltpu.PrefetchScalarGridSpec(num_scalar_prefetch=1, grid=(cols // tc,), in_specs=[full, full, full, half, half],
                                      out_specs=(full, full, full, full))
    return pl.pallas_call(body, out_shape=(o, o, o, o), grid_spec=gs, compiler_params=_params(), name=name)(
        _core_index(), w, m, v, g_own, g_sib)


def add_core_halves(g2, land, *, out_dtype, name):
    _, ns, r, cols = g2.shape
    tr, tc = _tile_2d(r, cols)

    def body(c_ref, a_ref, b_ref, o_ref):
        o_ref[...] = (a_ref[...] + b_ref[...]).astype(out_dtype)

    gs = pltpu.PrefetchScalarGridSpec(
        num_scalar_prefetch=1, grid=(ns, r // tr, cols // tc),
        in_specs=[pl.BlockSpec((None, None, tr, tc), lambda s, i, j, cr: (cr[0], s, i, j)),
                  pl.BlockSpec((None, tr, tc), lambda s, i, j, cr: (s, i, j))],
        out_specs=pl.BlockSpec((None, tr, tc), lambda s, i, j, cr: (s, i, j)))
    return pl.pallas_call(body, out_shape=_sds((ns, r, cols), out_dtype), grid_spec=gs, compiler_params=_params(), name=name)(
        _core_index(), g2, land)


def sum_over_chips(own, land, *, scatter, name):
    _, r, cols = own.shape
    tr, tc = _tile_2d(r, cols)

    def body(mine_ref, own_ref, l0, l1, l2, l3, o_ref):
        mine = mine_ref[0]
        mine_val = own_ref[...]
        acc = None
        for s, l_ref in enumerate((l0, l1, l2, l3)):
            val = jnp.where(mine == s, mine_val, l_ref[...]).astype(F32)
            acc = val if acc is None else acc + val
        o_ref[...] = acc

    def slot(s):
        return pl.BlockSpec((None, tr, tc), lambda i, j, mr: (jnp.where(mr[0] == s, (s + 1) % N_SHARD, s), i, j))

    own_spec = pl.BlockSpec((None, tr, tc), lambda i, j, mr: (mr[0] if scatter else 0, i, j))
    gs = pltpu.PrefetchScalarGridSpec(num_scalar_prefetch=1, grid=(r // tr, cols // tc), in_specs=[own_spec] + [slot(s) for s in range(N_SHARD)],
                                      out_specs=pl.BlockSpec((tr, tc), lambda i, j, mr: (i, j)))
    return pl.pallas_call(body, out_shape=_sds((r, cols), F32), grid_spec=gs, compiler_params=_params(), name=name)(
        _chip_index(), own, land, land, land, land)


_ANY = pl.BlockSpec(memory_space=pl.ANY)


def xy_exchange(src, *, scatter, name):
    rh = src.shape[1]

    def body(src_ref, land_ref, send_sems, recv_sems, loc_sem):
        x, y, c = lax.axis_index("x"), lax.axis_index("y"), lax.axis_index("c")
        mine = 2 * x + y
        peers = [(1 - x, y), (x, 1 - y), (1 - x, 1 - y)]

        def piece(shard):
            return src_ref.at[shard] if scatter else src_ref.at[c]

        def copy(k, px, py, dst_slot):
            return pltpu.make_async_remote_copy(src_ref=piece(2 * px + py), dst_ref=land_ref.at[dst_slot], send_sem=send_sems.at[k],
                                                recv_sem=recv_sems.at[k], device_id=(px, py, c), device_id_type=MESH)

        keep = pltpu.make_async_copy(piece(mine), land_ref.at[mine], loc_sem)
        keep.start()
        sends = [copy(k, px, py, mine) for k, (px, py) in enumerate(peers)]
        for cp in sends:
            cp.start()
        for cp in sends:
            cp.wait_send()
        for k, (px, py) in enumerate(peers):
            copy(k, px, py, 2 * px + py).wait_recv()
        keep.wait()

    return pl.pallas_call(body, out_shape=_sds((N_SHARD, rh, LANES), src.dtype), in_specs=[_ANY], out_specs=_ANY,
                          scratch_shapes=[pltpu.SemaphoreType.DMA((3,)), pltpu.SemaphoreType.DMA((3,)), pltpu.SemaphoreType.DMA(())],
                          name=name)(src)


def core_exchange(src, *, send_other_half, name):
    def body(src_ref, out_ref, send_sem, recv_sem, loc_sem):
        x, y, c = lax.axis_index("x"), lax.axis_index("y"), lax.axis_index("c")
        if send_other_half:
            cp = pltpu.make_async_remote_copy(src_ref=src_ref.at[1 - c], dst_ref=out_ref, send_sem=send_sem, recv_sem=recv_sem,
                                              device_id=(x, y, 1 - c), device_id_type=MESH)
            cp.start()
            cp.wait_send()
            cp.wait_recv()
        else:
            keep = pltpu.make_async_copy(src_ref, out_ref.at[c], loc_sem)
            keep.start()
            cp = pltpu.make_async_remote_copy(src_ref=src_ref, dst_ref=out_ref.at[c], send_sem=send_sem, recv_sem=recv_sem,
                                              device_id=(x, y, 1 - c), device_id_type=MESH)
            cp.start()
            cp.wait_send()
            pltpu.make_async_remote_copy(src_ref=src_ref, dst_ref=out_ref.at[1 - c], send_sem=send_sem, recv_sem=recv_sem,
                                         device_id=(x, y, 1 - c), device_id_type=MESH).wait_recv()
            keep.wait()

    out_shape = _sds(src.shape[1:], src.dtype) if send_other_half else _sds((2,) + src.shape, src.dtype)
    return pl.pallas_call(body, out_shape=out_shape, in_specs=[_ANY], out_specs=_ANY,
                          scratch_shapes=[pltpu.SemaphoreType.DMA(()), pltpu.SemaphoreType.DMA(()), pltpu.SemaphoreType.DMA(())],
                          name=name)(src)


def _comm_call(body, ins, out_shapes, sem_counts, name):
    return pl.pallas_call(body, out_shape=tuple(out_shapes), in_specs=[_ANY] * len(ins), out_specs=tuple([_ANY] * len(out_shapes)),
                          scratch_shapes=[pltpu.SemaphoreType.DMA((k,)) for k in sem_counts], name=name)(*ins)


def _sequencer_call(body, ins, out_shapes, sem_counts, name, collective_id):
    return pl.kernel(body, out_type=list(out_shapes), mesh=plsc.ScalarSubcoreMesh(axis_name="sequencer", num_cores=1), name=name,
                     scratch_types=[pltpu.SemaphoreType.DMA((k,)) for k in sem_counts],
                     compiler_params=pltpu.CompilerParams(collective_id=collective_id))(*ins)


def _handshake(peers):
    barrier = pltpu.get_barrier_semaphore()
    for peer in peers:
        pl.semaphore_signal(barrier, inc=1, device_id=peer, device_id_type=MESH)
    pl.semaphore_wait(barrier, len(peers))


def _xy_peers(x, y):
    return [(1 - x, y), (x, 1 - y), (1 - x, 1 - y)]


def gather_halves(halves, *, name, collective_id):
    n = len(halves)

    def body(*refs):
        ins, lands, sibs = refs[:n], refs[n:2 * n], refs[2 * n:3 * n]
        ici_send, ici_recv, d2d_send, d2d_recv = refs[3 * n:]
        x, y, c = lax.axis_index("x"), lax.axis_index("y"), lax.axis_index("c")
        mine = 2 * x + y
        peers = _xy_peers(x, y)
        _handshake([(px, py, c) for px, py in peers] + [(x, y, 1 - c)])

        def ici(i, k, slot):
            px, py = peers[k]
            return pltpu.make_async_remote_copy(src_ref=ins[i].at[c], dst_ref=lands[i].at[slot], send_sem=ici_send.at[3 * i + k],
                                                recv_sem=ici_recv.at[3 * i + k], device_id=(px, py, c), device_id_type=MESH)

        def pass_on(i, k):
            px, py = peers[k]
            slot = 2 * px + py
            return pltpu.make_async_remote_copy(src_ref=lands[i].at[slot], dst_ref=sibs[i].at[slot], send_sem=d2d_send.at[3 * i + k],
                                                recv_sem=d2d_recv.at[3 * i + k], device_id=(x, y, 1 - c), device_id_type=MESH)

        sends = [ici(i, k, mine) for i in range(n) for k in range(3)]
        for cp in sends:
            cp.start()
        passed = []
        for i in range(n):
            for k in range(3):
                px, py = peers[k]
                ici(i, k, 2 * px + py).wait_recv()
                cp = pass_on(i, k)
                cp.start()
                passed.append(cp)
        for cp in passed:
            cp.wait_recv()
        for cp in sends + passed:
            cp.wait_send()

    outs = [_sds((N_SHARD,) + h.shape[1:], h.dtype) for h in halves]
    res = _sequencer_call(body, halves, outs + outs, [3 * n] * 4, name, collective_id)
    return res[:n], res[n:]


def send_other_half(arrays, *, name, collective_id):
    n = len(arrays)

    def body(*refs):
        ins, lands = refs[:n], refs[n:2 * n]
        send_sems, recv_sems = refs[2 * n:]
        x, y, c = lax.axis_index("x"), lax.axis_index("y"), lax.axis_index("c")
        _handshake([(x, y, 1 - c)])
        copies = [pltpu.make_async_remote_copy(src_ref=ins[i].at[1 - c], dst_ref=lands[i], send_sem=send_sems.at[i],
                                               recv_sem=recv_sems.at[i], device_id=(x, y, 1 - c), device_id_type=MESH) for i in range(n)]
        for cp in copies:
            cp.start()
        for cp in copies:
            cp.wait_recv()
        for cp in copies:
            cp.wait_send()

    return _sequencer_call(body, arrays, [_sds(a.shape[1:], a.dtype) for a in arrays], [n, n], name, collective_id)


_HBM = pl.BlockSpec(memory_space=pltpu.HBM)
_SEM = pl.BlockSpec(memory_space=pltpu.SEMAPHORE)
_SPLIT_COPY = dict(has_side_effects=pltpu.SideEffectType.DATAFLOW_SIDE_EFFECTING)


def _chip_copy(ins, lands, send_sems, recv_sems, scatter, i, k, receive):
    x, y, c = lax.axis_index("x"), lax.axis_index("y"), lax.axis_index("c")
    px, py = _xy_peers(x, y)[k]
    theirs, mine = 2 * px + py, 2 * x + y
    src = ins[i].at[theirs] if scatter[i] else ins[i].at[0]
    return pltpu.make_async_remote_copy(src_ref=src, dst_ref=lands[i].at[theirs if receive else mine], send_sem=send_sems.at[3 * i + k],
                                        recv_sem=recv_sems.at[3 * i + k], device_id=(px, py, c), device_id_type=MESH)


def send_to_chips_start(arrays, scatter, *, name):
    n = len(arrays)

    def body(*refs):
        send_sems, recv_sems = refs[2 * n], refs[2 * n + 1]
        ins, lands = refs[2 * n + 2:3 * n + 2], refs[3 * n + 2:4 * n + 2]
        token = refs[4 * n + 2]
        for i in range(n):
            for k in range(3):
                _chip_copy(ins, lands, send_sems, recv_sems, scatter, i, k, receive=False).start()
        token[...] = jnp.zeros_like(token)

    land_shapes = [(N_SHARD,) + a.shape[1:] for a in arrays]
    operands = [pltpu.with_memory_space_constraint(a, pltpu.HBM) for a in arrays]
    operands += [pltpu.with_memory_space_constraint(lax.empty(s, a.dtype), pltpu.HBM) for s, a in zip(land_shapes, arrays)]
    out_shape = ([pltpu.SemaphoreType.DMA((3 * n,)), pltpu.SemaphoreType.DMA((3 * n,))] + [pltpu.HBM(a.shape, a.dtype) for a in arrays]
                 + [pltpu.HBM(s, a.dtype) for s, a in zip(land_shapes, arrays)] + [_sds((8, LANES), F32)])
    res = pl.pallas_call(body, name=name, out_shape=out_shape, in_specs=[_HBM] * (2 * n),
                         out_specs=[_SEM, _SEM] + [_HBM] * (2 * n) + [pl.BlockSpec(memory_space=pltpu.VMEM)],
                         input_output_aliases={i: 2 + i for i in range(2 * n)}, compiler_params=pltpu.CompilerParams(**_SPLIT_COPY))(*operands)
    return (res[0], res[1], res[2:2 + n], res[2 + n:2 + 2 * n], scatter), res[-1]


def send_to_chips_wait(state, after, *, name):
    send_sems, recv_sems, arrays, lands, scatter = state
    n = len(arrays)

    def body(*refs):
        ins, landing = refs[:n], refs[n:2 * n]
        send_sems, recv_sems = refs[2 * n], refs[2 * n + 1]
        for i in range(n):
            for k in range(3):
                _chip_copy(ins, landing, send_sems, recv_sems, scatter, i, k, receive=True).wait_recv()
        for i in range(n):
            for k in range(3):
                _chip_copy(ins, landing, send_sems, recv_sems, scatter, i, k, receive=False).wait_send()

    out_shape = [pltpu.HBM(a.shape, a.dtype) for a in list(arrays) + list(lands)]
    res = pl.pallas_call(body, name=name, out_shape=out_shape, in_specs=[_HBM] * (2 * n) + [_SEM, _SEM] + [_ANY] * len(after),
                         out_specs=[_HBM] * (2 * n), input_output_aliases={i: i for i in range(2 * n)},
                         compiler_params=pltpu.CompilerParams(**_SPLIT_COPY))(*arrays, *lands, send_sems, recv_sems, *after)
    return res[:n], res[n:]


def swap_with_other_core(arrays, *, name, collective_id):
    n = len(arrays)

    def body(*refs):
        ins, lands = refs[:n], refs[n:2 * n]
        send_sems, recv_sems = refs[2 * n:]
        x, y, c = lax.axis_index("x"), lax.axis_index("y"), lax.axis_index("c")
        _handshake([(x, y, 1 - c)])
        copies = [pltpu.make_async_remote_copy(src_ref=ins[i], dst_ref=lands[i], send_sem=send_sems.at[i], recv_sem=recv_sems.at[i],
                                               device_id=(x, y, 1 - c), device_id_type=MESH) for i in range(n)]
        for cp in copies:
            cp.start()
        for cp in copies:
            cp.wait_recv()
        for cp in copies:
            cp.wait_send()

    return _sequencer_call(body, arrays, [_sds(a.shape, a.dtype) for a in arrays], [n, n], name, collective_id)


def _pack_rows(n_elems, row_multiple):
    rows = -(-n_elems // LANES)
    return -(-rows // row_multiple) * row_multiple


def _pack(arrays, rows, dtype):
    flat = jnp.concatenate([a.reshape(-1).astype(dtype) for a in arrays])
    return jnp.pad(flat, (0, rows * LANES - flat.shape[0])).reshape(rows, LANES)


def _unpack(packed, shapes):
    flat = packed.reshape(-1)
    out, off = [], 0
    for s in shapes:
        n = int(np.prod(s))
        out.append(flat[off:off + n].reshape(s))
        off += n
    return out


def all_gather_shards(shards, axes, dtype, row_multiple, tag):
    shapes = [s.shape for s in shards]
    rows = _pack_rows(sum(int(np.prod(s)) for s in shapes), row_multiple)
    packed = _pack(shards, rows, dtype).reshape(2, rows // 2, LANES)
    land = xy_exchange(packed, scatter=False, name=f"gather_xy_{tag}")
    both = core_exchange(land, send_other_half=False, name=f"gather_c_{tag}")
    per_shard = jnp.swapaxes(both, 0, 1).reshape(N_SHARD, rows, LANES)
    pieces = [_unpack(per_shard[s], shapes) for s in range(N_SHARD)]
    return [jnp.concatenate([pieces[s][i] for s in range(N_SHARD)], axis=ax) for i, ax in enumerate(axes)]


def _ordered_before(first, then):
    if then is None:
        return first, None
    return lax.optimization_barrier((first, then))


def reduce_between_cores(arrays, scatter, *, tag, collective_id, before=None):
    arrays, before = _ordered_before(arrays, before)
    land = send_other_half(arrays, name=f"reduce_core_send_{tag}", collective_id=collective_id)
    return (arrays, land, scatter, tag, collective_id), before


def reduce_between_chips(state, before=None):
    arrays, land, scatter, tag, collective_id = state
    chip = [add_core_halves(a, l, out_dtype=BF16 if sc else F32, name=f"reduce_core_add_{tag}_{i}")
            for i, (a, l, sc) in enumerate(zip(arrays, land, scatter))]
    sending, token = send_to_chips_start(chip, scatter, name=f"reduce_chip_start_{tag}")
    token, before = _ordered_before(token, before)
    return (sending, token, scatter, tag, collective_id), before


def reduce_finish(state, after):
    sending, token, scatter, tag, collective_id = state
    chip, land = send_to_chips_wait(sending, tuple(after) + (token,), name=f"reduce_chip_wait_{tag}")
    own = [sum_over_chips(ch, l, scatter=sc, name=f"reduce_chip_add_{tag}_{i}") for i, (ch, l, sc) in enumerate(zip(chip, land, scatter))]
    sib = swap_with_other_core(own, name=f"reduce_core_swap_{tag}", collective_id=collective_id + 2)
    return own, sib


def _ffn_layer_fwd(h, norm_g, w_up, cw, cb, w_down, tag):
    hn = norm_fwd(h, norm_g, name=f"ffn_norm_{tag}")
    u = matmul(hn, w_up, name=f"ffn_up_{tag}")
    act = ffn_act_fwd(u, cw, cb, name=f"ffn_act_{tag}")
    out = matmul(act, w_down, add=h, name=f"ffn_down_{tag}")
    return out, (h, hn, u, act)


def _travel_layout(array):
    return BIG_ARRAYS[array][3], BIG_ARRAYS[array][4]


def _ffn_layer_bwd(saved, dout, norm_g, w_up, cw, cb, w_down, tag, d_w_down_other=None):
    h, hn, u, act = saved
    dact = matmul(dout, w_down, tb=True, name=f"ffn_down_dx_{tag}")
    d_w_down = matmul(act, dout, ta=True, layer=(int(tag), 2, d_w_down_other), name=f"ffn_down_dw_{tag}")
    du, dcw, dcb = ffn_act_bwd(u, cw, cb, dact, name=f"ffn_act_bwd_{tag}")
    dhn = matmul(du, w_up, tb=True, name=f"ffn_up_dx_{tag}")
    d_w_up = matmul(hn, du, ta=True, split=_travel_layout(f"ffn_w_up_{tag}"), name=f"ffn_up_dw_{tag}")
    dh, dg = norm_bwd(h, norm_g, dhn, dout, name=f"ffn_norm_bwd_{tag}")
    return dh, dg, d_w_up, dcw, dcb, d_w_down


def local_step(x, target, w, stage=lambda name, tensors, grads=None: tensors):
    g = {}
    tables = _ret_tables()
    x = stage("start", x)
    w_in_t = w["ret_gdn_w_in"]
    w_main = w_in_t[:MIX_MAIN]
    w_small = jnp.pad(w_in_t[MIX_MAIN:], ((0, LANES - 2 * N_HEADS), (0, 0)))
    a_log = jnp.pad(w["gdn_a_log"], ((0, 0), (0, LANES - N_HEADS)))
    dt_bias = jnp.pad(w["gdn_dt_bias"], ((0, 0), (0, LANES - N_HEADS)))

    hn0 = stage("normed", norm_fwd(x, w["norm_mix"][0:1], name="mix0_norm"))
    p = matmul(hn0, w_main, tb=True, name="mix0_in")
    small = matmul(hn0, w_small, tb=True, name="mix0_in_small")
    y_ret, s_ret = ret_fwd(p, tables, name="ret_fwd")
    conv = gdn_conv_fwd(p, w["gdn_conv_w"], name="gdn_conv")
    y_gdn, s_gdn = gdn_fwd(conv, p, small, a_log, dt_bias, w["gdn_out_gain"], name="gdn_fwd")
    y0 = stage("mixed", jnp.concatenate([y_ret, y_gdn], axis=1))
    h1 = matmul(y0, w["ret_gdn_w_out"], add=x, name="mix0_out")
    h2, ffn0 = _ffn_layer_fwd(h1, w["norm_ffn"][0:1], w["ffn_w_up"][0], w["ffn_conv_w"][0], w["ffn_conv_b"][0:1], w["ffn_w_down"][0], "0")
    h2 = stage("layer0", h2)

    hn1 = norm_fwd(h2, w["norm_mix"][1:2], name="mix1_norm")
    gx = matmul(hn1, w["lru_w_in"], name="mix1_in")
    lru_p = (w["lru_conv_w"], w["lru_conv_b"], w["lru_w_a"], w["lru_b_a"], w["lru_w_x"], w["lru_b_x"], w["lru_lambda"])
    y1 = lru_fwd(gx, *lru_p, name="lru_fwd")
    h3 = matmul(y1, w["lru_w_out"], add=h2, name="mix1_out")
    h4, ffn1 = _ffn_layer_fwd(h3, w["norm_ffn"][1:2], w["ffn_w_up"][1], w["ffn_conv_w"][1], w["ffn_conv_b"][1:2], w["ffn_w_down"][1], "1")

    loss, dh4, g["norm_final"] = final_fwd_bwd(h4, w["norm_final"], target, name="final")

    dh3, dgf1, dwu1, dcw1, dcb1, dwd1 = _ffn_layer_bwd(ffn1, dh4, w["norm_ffn"][1:2], w["ffn_w_up"][1], w["ffn_conv_w"][1],
                                                     w["ffn_conv_b"][1:2], w["ffn_w_down"][1], "1")
    g["ffn_w_up_1"] = dwu1
    dh3 = stage("grads0_ready", dh3, g)
    dy1 = matmul(dh3, w["lru_w_out"], tb=True, name="mix1_out_dx")
    g["lru_w_out"] = matmul(y1, dh3, ta=True, split=_travel_layout("lru_w_out"), name="mix1_out_dw")
    dgx, g["lru_conv_w"], g["lru_conv_b"], g["lru_w_a"], g["lru_b_a"], g["lru_w_x"], g["lru_b_x"], g["lru_lambda"] = lru_bwd(
        gx, *lru_p, dy1, name="lru_bwd")
    dgx = stage("grads0_send", dgx, g)
    dhn1 = matmul(dgx, w["lru_w_in"], tb=True, name="mix1_in_dx")
    g["lru_w_in"] = matmul(hn1, dgx, ta=True, split=_travel_layout("lru_w_in"), name="mix1_in_dw")
    dh2, dgm1 = norm_bwd(h2, w["norm_mix"][1:2], dhn1, dh3, name="mix1_norm_bwd")
    dh2 = stage("grads1_ready", dh2, g)

    dh1, dgf0, dwu0, dcw0, dcb0, dwd0 = _ffn_layer_bwd(ffn0, dh2, w["norm_ffn"][0:1], w["ffn_w_up"][0], w["ffn_conv_w"][0],
                                                     w["ffn_conv_b"][0:1], w["ffn_w_down"][0], "0", dwd1)
    g["ffn_w_up_0"] = dwu0
    g["ffn_w_down"] = dwd0
    dh1 = stage("grads2_ready", stage("grads1_send", dh1, g), g)
    dy0 = matmul(dh1, w["ret_gdn_w_out"], tb=True, name="mix0_out_dx")
    g["ret_gdn_w_out"] = matmul(y0, dh1, ta=True, split=_travel_layout("ret_gdn_w_out"), name="mix0_out_dw")
    dq_r, dk_r, dv_r, dg_r = ret_bwd(p, tables, s_ret, dy0, name="ret_bwd")
    dy0, dq_r = stage("grads2_send", (dy0, dq_r), g)
    dcq, dck, dcv, dg_d, dsmall, dal, ddt, dgain = gdn_bwd(conv, p, small, a_log, dt_bias, w["gdn_out_gain"], s_gdn, dy0, name="gdn_bwd")
    dconv = jnp.concatenate([dcq, dck, dcv], axis=1)
    dp_conv, g["gdn_conv_w"] = gdn_conv_bwd(p, w["gdn_conv_w"], dconv, name="gdn_conv_bwd")
    dp = jnp.concatenate([dq_r, dk_r, dv_r, dg_r, dp_conv, dg_d], axis=1)
    dhn0 = matmul(dp, w_main, name="mix0_in_dx")
    dhn0 = matmul(dsmall, w_small, add=dhn0, name="mix0_in_small_dx")
    d_w_main = matmul(dp, hn0, ta=True, name="mix0_in_dw")
    d_w_small = matmul(dsmall, hn0, ta=True, name="mix0_in_small_dw")
    g["ret_gdn_w_in"] = jnp.concatenate([d_w_main, d_w_small[:2 * N_HEADS]], axis=0)
    dx, dgm0 = norm_bwd(x, w["norm_mix"][0:1], dhn0, dh1, name="mix0_norm_bwd")

    g["gdn_a_log"] = dal[:, :N_HEADS]
    g["gdn_dt_bias"] = ddt[:, :N_HEADS]
    g["gdn_out_gain"] = dgain
    g["norm_mix"] = jnp.concatenate([dgm0, dgm1], axis=0)
    g["norm_ffn"] = jnp.concatenate([dgf0, dgf1], axis=0)
    g["ffn_conv_w"] = jnp.stack([dcw0, dcw1])
    g["ffn_conv_b"] = jnp.concatenate([dcb0, dcb1], axis=0)
    return loss, dx, g


WEIGHTS = ("norm_mix", "norm_ffn", "ret_gdn_w_in", "gdn_conv_w", "gdn_a_log", "gdn_dt_bias", "gdn_out_gain", "ret_gdn_w_out",
           "lru_w_in", "lru_conv_w", "lru_conv_b", "lru_w_a", "lru_b_a", "lru_w_x", "lru_b_x", "lru_lambda", "lru_w_out",
           "ffn_w_up", "ffn_conv_w", "ffn_conv_b", "ffn_w_down", "norm_final")
MATMUL_SHARDED = {"ret_gdn_w_in": 1, "ret_gdn_w_out": 0, "lru_w_in": 1, "lru_w_out": 0, "ffn_w_up": 2, "ffn_w_down": 1}
VECTOR_SHARDED = {"gdn_conv_w": 1, "lru_conv_w": 1, "lru_conv_b": 1, "lru_b_a": 1, "lru_b_x": 1, "lru_lambda": 1, "ffn_conv_w": 2}
SHARDED = {**MATMUL_SHARDED, **VECTOR_SHARDED}
REPLICATED = tuple(n for n in WEIGHTS if n not in SHARDED)
SQUEEZE = {"ret_gdn_w_in", "gdn_conv_w", "ret_gdn_w_out", "lru_w_in", "lru_conv_w", "lru_w_a", "lru_w_x", "lru_w_out"}
MIX_IN = MIX_MAIN + 2 * N_HEADS
BIG_ARRAYS = {
    "ret_gdn_w_in": ("ret_gdn_w_in", None, (MIX_IN, D_MODEL), (N_SHARD, MIX_IN // N_SHARD, 2, D_MODEL // 2), (2, 0, 1, 3)),
    "ret_gdn_w_out": ("ret_gdn_w_out", None, (2 * GROUP, D_MODEL), (N_SHARD, 2, GROUP // N_SHARD, D_MODEL), (1, 0, 2, 3)),
    "lru_w_in": ("lru_w_in", None, (D_MODEL, 2 * D_MODEL), (2, D_MODEL // 2, N_SHARD, 2 * D_MODEL // N_SHARD), (0, 2, 1, 3)),
    "lru_w_out": ("lru_w_out", None, (D_MODEL, D_MODEL), (N_SHARD, 2, D_MODEL // (2 * N_SHARD), D_MODEL), (1, 0, 2, 3)),
    "ffn_w_up_0": ("ffn_w_up", 0, (D_MODEL, 2 * D_FF), (2, D_MODEL // 2, N_SHARD, 2 * D_FF // N_SHARD), (0, 2, 1, 3)),
    "ffn_w_up_1": ("ffn_w_up", 1, (D_MODEL, 2 * D_FF), (2, D_MODEL // 2, N_SHARD, 2 * D_FF // N_SHARD), (0, 2, 1, 3)),
    "ffn_w_down": ("ffn_w_down", None, (2, D_FF, D_MODEL), (2, N_SHARD, D_FF // N_SHARD, D_MODEL), (0, 1, 2, 3)),
}
GATHER_GROUPS = (("ret_gdn_w_in",), ("ret_gdn_w_out", "ffn_w_up_0", "ffn_w_down"), ("lru_w_in", "lru_w_out", "ffn_w_up_1"))
REDUCE_GROUPS = (("ffn_w_up_1",), ("lru_w_in", "lru_w_out"), ("ffn_w_up_0", "ffn_w_down"), ("ret_gdn_w_out", "ret_gdn_w_in"))
BLOCK_WEIGHTS = ("lru_w_a", "lru_w_x")
GATHER_COLLECTIVE_ID = 1
REDUCE_COLLECTIVE_ID = GATHER_COLLECTIVE_ID + len(GATHER_GROUPS)


TRANSPOSED = ("ret_gdn_w_in",)


def _shard_of(array, tensors):
    weight, layer = BIG_ARRAYS[array][:2]
    t = tensors[weight]
    if weight in TRANSPOSED:
        return jnp.swapaxes(t, 1, 2)[0]
    return _local_view(weight, t) if layer is None else t[layer]


def _core_halves(array, shard):
    _, _, _, split, perm = BIG_ARRAYS[array]
    kept = [k for k in range(4) if k != perm[1]]
    order = [kept.index(perm[0]), kept.index(perm[2]), kept.index(perm[3])]
    return shard.reshape([split[k] for k in kept]).transpose(order)


def _local_view(name, a):
    if name in SQUEEZE:
        return a[0]
    if a.ndim == 1:
        return a[None, :]
    return a


def kernel(x, norm_mix, norm_ffn, ret_gdn_w_in, gdn_conv_w, gdn_a_log, gdn_dt_bias, gdn_out_gain, ret_gdn_w_out, lru_w_in, lru_conv_w, lru_conv_b, lru_w_a, lru_b_a, lru_w_x, lru_b_x, lru_lambda, lru_w_out, ffn_w_up, ffn_conv_w, ffn_conv_b, ffn_w_down, norm_final, loss_target, m_norm_mix, m_norm_ffn, m_ret_gdn_w_in, m_gdn_conv_w, m_gdn_a_log, m_gdn_dt_bias, m_gdn_out_gain, m_ret_gdn_w_out, m_lru_w_in, m_lru_conv_w, m_lru_conv_b, m_lru_w_a, m_lru_b_a, m_lru_w_x, m_lru_b_x, m_lru_lambda, m_lru_w_out, m_ffn_w_up, m_ffn_conv_w, m_ffn_conv_b, m_ffn_w_down, m_norm_final, v_norm_mix, v_norm_ffn, v_ret_gdn_w_in, v_gdn_conv_w, v_gdn_a_log, v_gdn_dt_bias, v_gdn_out_gain, v_ret_gdn_w_out, v_lru_w_in, v_lru_conv_w, v_lru_conv_b, v_lru_w_a, v_lru_b_a, v_lru_w_x, v_lru_b_x, v_lru_lambda, v_lru_w_out, v_ffn_w_up, v_ffn_conv_w, v_ffn_conv_b, v_ffn_w_down, v_norm_final):
    given = dict(norm_mix=norm_mix, norm_ffn=norm_ffn, ret_gdn_w_in=ret_gdn_w_in, gdn_conv_w=gdn_conv_w, gdn_a_log=gdn_a_log, gdn_dt_bias=gdn_dt_bias, gdn_out_gain=gdn_out_gain, ret_gdn_w_out=ret_gdn_w_out, lru_w_in=lru_w_in, lru_conv_w=lru_conv_w, lru_conv_b=lru_conv_b, lru_w_a=lru_w_a, lru_b_a=lru_b_a, lru_w_x=lru_w_x, lru_b_x=lru_b_x, lru_lambda=lru_lambda, lru_w_out=lru_w_out, ffn_w_up=ffn_w_up, ffn_conv_w=ffn_conv_w, ffn_conv_b=ffn_conv_b, ffn_w_down=ffn_w_down, norm_final=norm_final)
    mom1 = dict(norm_mix=m_norm_mix, norm_ffn=m_norm_ffn, ret_gdn_w_in=m_ret_gdn_w_in, gdn_conv_w=m_gdn_conv_w, gdn_a_log=m_gdn_a_log, gdn_dt_bias=m_gdn_dt_bias, gdn_out_gain=m_gdn_out_gain, ret_gdn_w_out=m_ret_gdn_w_out, lru_w_in=m_lru_w_in, lru_conv_w=m_lru_conv_w, lru_conv_b=m_lru_conv_b, lru_w_a=m_lru_w_a, lru_b_a=m_lru_b_a, lru_w_x=m_lru_w_x, lru_b_x=m_lru_b_x, lru_lambda=m_lru_lambda, lru_w_out=m_lru_w_out, ffn_w_up=m_ffn_w_up, ffn_conv_w=m_ffn_conv_w, ffn_conv_b=m_ffn_conv_b, ffn_w_down=m_ffn_w_down, norm_final=m_norm_final)
    mom2 = dict(norm_mix=v_norm_mix, norm_ffn=v_norm_ffn, ret_gdn_w_in=v_ret_gdn_w_in, gdn_conv_w=v_gdn_conv_w, gdn_a_log=v_gdn_a_log, gdn_dt_bias=v_gdn_dt_bias, gdn_out_gain=v_gdn_out_gain, ret_gdn_w_out=v_ret_gdn_w_out, lru_w_in=v_lru_w_in, lru_conv_w=v_lru_conv_w, lru_conv_b=v_lru_conv_b, lru_w_a=v_lru_w_a, lru_b_a=v_lru_b_a, lru_w_x=v_lru_w_x, lru_b_x=v_lru_b_x, lru_lambda=v_lru_lambda, lru_w_out=v_lru_w_out, ffn_w_up=v_ffn_w_up, ffn_conv_w=v_ffn_conv_w, ffn_conv_b=v_ffn_conv_b, ffn_w_down=v_ffn_w_down, norm_final=v_norm_final)

    local = {n: _local_view(n, a) for n, a in given.items()}

    core = lax.axis_index("c")
    chip = 2 * lax.axis_index("x") + lax.axis_index("y")
    is_my_chip = lax.broadcasted_iota(jnp.int32, (N_SHARD, 1, 1), 0) == chip

    def by_core(mine, other):
        return jnp.where(core == 0, jnp.stack([mine, other]), jnp.stack([other, mine]))

    vec_names, rp_names = list(VECTOR_SHARDED), list(REPLICATED)
    full = dict(zip(vec_names, all_gather_shards([local[n] for n in vec_names], [SHARDED[n] for n in vec_names], F32, 32, "p")))
    for n in rp_names:
        full[n] = local[n]
    in_flight = {}

    def launch(gi, after=None):
        halves = []
        for a in GATHER_GROUPS[gi]:
            halves.append(_core_halves(a, _shard_of(a, given).astype(BF16)))
        if after is not None:
            halves, after = lax.optimization_barrier((halves, after))
        in_flight[gi] = (halves,) + gather_halves(halves, name=f"gather_weights_{gi}", collective_id=GATHER_COLLECTIVE_ID + gi)
        return after

    def land(gi, after):
        halves, lands, sibs = in_flight[gi]
        (lands, sibs), after = lax.optimization_barrier(((lands, sibs), after))
        for a, mine, got, passed in zip(GATHER_GROUPS[gi], halves, lands, sibs):
            weight, layer, full_shape, split, perm = BIG_ARRAYS[a]
            half_mine = jnp.where(is_my_chip, jnp.where(core == 0, mine[0], mine[1])[None], got)
            half_other = jnp.where(is_my_chip, jnp.where(core == 0, mine[1], mine[0])[None], passed)
            value = by_core(half_mine, half_other).transpose(tuple(np.argsort(perm))).reshape(full_shape)
            if layer is None:
                full[weight] = value
            else:
                full.setdefault(weight, [None, None])[layer] = value
        return after

    reducing = {}

    def reduce_ready(gi, grads, then=None, extra=()):
        def travelling(a):
            split, perm = _travel_layout(a)
            return grads[a] if grads[a].ndim == 4 else grads[a].reshape(split).transpose(perm)

        arrays = [travelling(a) for a in REDUCE_GROUPS[gi]] + list(extra)
        scatter = [True] * len(REDUCE_GROUPS[gi]) + [False] * len(extra)
        reducing[gi], then = reduce_between_cores(arrays, scatter, tag=str(gi), collective_id=REDUCE_COLLECTIVE_ID + 3 * gi, before=then)
        return then

    def reduce_send(gi, then=None):
        reducing[gi], then = reduce_between_chips(reducing[gi], before=then)
        return then

    def stage(name, tensors, grads=None):
        if name == "start":
            launch(0)
            launch(1)
            packed["wmv"], tensors = lax.optimization_barrier((packed["wmv"], tensors))
            return land(0, tensors)
        if name == "normed":
            return launch(2, tensors)
        if name in ("mixed", "layer0"):
            return land({"mixed": 1, "layer0": 2}[name], tensors)
        gi = int(name[len("grads")])
        return reduce_ready(gi, grads, tensors) if name.endswith("_ready") else reduce_send(gi, tensors)

    small_names = [n for n in rp_names if n not in BLOCK_WEIGHTS] + vec_names
    loc_shapes = [local[n].shape for n in small_names]
    loc_rows = _pack_rows(sum(int(np.prod(s)) for s in loc_shapes), 256)
    packed = {"wmv": [_pack([src[n] for n in small_names], loc_rows, F32) for src in (given, mom1, mom2)]}

    loss_part, dx, grads = local_step(x[0], loss_target[0], full, stage)
    small_shapes = [grads[n].shape for n in small_names] + [(1, 1)]
    small_rows = _pack_rows(sum(int(np.prod(s)) for s in small_shapes), 16)
    small = _pack([grads[n] for n in small_names] + [loss_part[:, :1]], small_rows, F32).reshape(2, 1, small_rows // 2, LANES)
    last = len(REDUCE_GROUPS) - 1
    halves_of_blocks = [grads[n].reshape(2, 1, LRU_BLOCKS * HEAD // 2, HEAD) for n in BLOCK_WEIGHTS]
    reduce_ready(last, grads, extra=[small] + halves_of_blocks)
    reduce_send(last)
    reduced, result = {}, {}

    def finish(gi, after):
        g_own, g_sib = reduce_finish(reducing[gi], after)
        reduced.update(zip(list(REDUCE_GROUPS[gi]) + ["small"] + list(BLOCK_WEIGHTS), zip(g_own, g_sib)))

    def update(n):
        if n in TRANSPOSED:
            w3, m3, v3 = (jnp.swapaxes(t, 1, 2) for t in (given[n], mom1[n], mom2[n]))
            result[n] = tuple(jnp.swapaxes(t, 1, 2) for t in adamw_column_halves(w3, m3, v3, *reduced[n], name=f"adamw_{n}"))
            return
        done = None
        for a in (k for k, spec in BIG_ARRAYS.items() if spec[0] == n):
            r, cols = reduced[a][0].shape
            layer = BIG_ARRAYS[a][1] or 0
            w3, m3, v3 = (t if BIG_ARRAYS[a][1] is not None else t.reshape(1, 2 * r, cols) for t in (given[n], mom1[n], mom2[n]))
            done = adamw_halves(w3, m3, v3, *reduced[a], layer=layer, prev=done, name=f"adamw_{a}")
        result[n] = done

    for gi in range(last):
        finish(gi, (dx, reducing[last][1]))
    late = {BIG_ARRAYS[a][0] for a in REDUCE_GROUPS[last]}
    for n in MATMUL_SHARDED:
        if n not in late:
            update(n)
    finish(last, tuple(result[n][0] for n in MATMUL_SHARDED if n not in late))
    for n in MATMUL_SHARDED:
        if n in late:
            update(n)

    for n in BLOCK_WEIGHTS:
        w3, m3, v3 = (t.reshape(1, LRU_BLOCKS * HEAD, HEAD) for t in (given[n], mom1[n], mom2[n]))
        result[n] = adamw_halves(w3, m3, v3, *reduced[n], name=f"adamw_{n}")

    *small_sums, loss_sum = _unpack(by_core(*reduced["small"]).reshape(small_rows, LANES), small_shapes)
    loss = loss_sum[0, 0]
    g_small = dict(zip(small_names, small_sums))
    for n in vec_names:
        size = local[n].shape[SHARDED[n]]
        g_small[n] = lax.dynamic_slice_in_dim(g_small[n], chip * size, size, axis=SHARDED[n])
    w_pack, m_pack, v_pack = packed["wmv"]
    d_s, m_s, v_s = adamw(w_pack, _pack([g_small[n] for n in small_names], loc_rows, F32), m_pack, v_pack, name="adamw_small")
    for n, d, nm, nv in zip(small_names, _unpack(d_s, loc_shapes), _unpack(m_s, loc_shapes), _unpack(v_s, loc_shapes)):
        result[n] = (g_small[n], d, nm, nv)

    outs = [[result[n][k].reshape(given[n].shape) for n in WEIGHTS] for k in range(4)]
    return (loss, dx[None], *outs[0], *outs[1], *outs[2], *outs[3])
```

```python
import functools

import numpy as np
import jax
import jax.numpy as jnp
from jax import lax
from jax.experimental import pallas as pl
from jax.experimental.pallas import tpu as pltpu
from jax.experimental.pallas import tpu_sc as plsc

F32 = jnp.float32
BF16 = jnp.bfloat16
HI = lax.Precision.HIGHEST
MESH = pl.DeviceIdType.MESH

SEQ = 2048
D_MODEL = 1024
N_HEADS = 4
HEAD = 128
RET_CHUNK = 128
RET_CHUNKS_PER_STEP = 2
GDN_CHUNK = 64
GDN_CHUNKS_PER_STEP = 4
GROUP = N_HEADS * HEAD
MIX_MAIN = 8 * GROUP
D_FF = 2816
LRU_BLOCKS = 8
LRU_C = 8.0
ROPE_BASE = 10000.0
EPS = 1e-6
N_SHARD = 4
LANES = 128

ADAM_LR, ADAM_B1, ADAM_B2, ADAM_EPS, ADAM_WD, ADAM_STEP = 0.001, 0.9, 0.999, 1e-08, 0.01, 10

VMEM_LIMIT_BYTES = 56 * 1024 * 1024

_roll = pltpu.roll


def _params(**kw):
    return pltpu.CompilerParams(vmem_limit_bytes=VMEM_LIMIT_BYTES, **kw)


def _sds(shape, dtype):
    return jax.ShapeDtypeStruct(tuple(shape), dtype)


def _shift_raw(x, d):
    n = x.shape[0]
    t = lax.broadcasted_iota(jnp.int32, x.shape, 0)
    if d > 0:
        return jnp.where(t >= d, _roll(x, d, 0), 0.0)
    return jnp.where(t < n + d, _roll(x, n + d, 0), 0.0)


@functools.partial(jax.custom_vjp, nondiff_argnums=(1,))
def shift_rows(x, d):
    return _shift_raw(x, d)


def _shift_fwd(x, d):
    return _shift_raw(x, d), None


def _shift_bwd(d, _, g):
    return (_shift_raw(g, -d),)


shift_rows.defvjp(_shift_fwd, _shift_bwd)


@jax.custom_vjp
def swap_halves(x):
    return _roll(x, HEAD // 2, 1)


def _swap_fwd(x):
    return _roll(x, HEAD // 2, 1), None


def _swap_bwd(_, g):
    return (_roll(g, HEAD // 2, 1),)


swap_halves.defvjp(_swap_fwd, _swap_bwd)


SCAN_BLOCK_ROWS = 64


def _scan_block(a, u, reverse):
    n = a.shape[0]
    t = lax.broadcasted_iota(jnp.int32, a.shape, 0)
    d = 1
    while d < n:
        if reverse:
            m = t < n - d
            a_s, u_s = _roll(a, n - d, 0), _roll(u, n - d, 0)
        else:
            m = t >= d
            a_s, u_s = _roll(a, d, 0), _roll(u, d, 0)
        u = a * jnp.where(m, u_s, 0.0) + u
        a = a * jnp.where(m, a_s, 1.0)
        d *= 2
    return a, u


def _scan_raw(a, u, reverse):
    n = a.shape[0]
    blocks = range(n // SCAN_BLOCK_ROWS)
    out = [None] * len(blocks)
    entering = None
    for b in (reversed(blocks) if reverse else blocks):
        rows = slice(b * SCAN_BLOCK_ROWS, (b + 1) * SCAN_BLOCK_ROWS)
        a_run, h = _scan_block(a[rows], u[rows], reverse)
        if entering is not None:
            h = a_run * entering + h
        out[b] = h
        entering = h[:1] if reverse else h[SCAN_BLOCK_ROWS - 1:]
    return jnp.concatenate(out, axis=0)


@jax.custom_vjp
def lin_scan(a, u):
    return _scan_raw(a, u, False)


def _lin_scan_fwd(a, u):
    hs = _scan_raw(a, u, False)
    return hs, (a, hs)


def _lin_scan_bwd(res, g):
    a, hs = res
    lam = _scan_raw(_shift_raw(a, -1), g, True)
    return lam * _shift_raw(hs, 1), lam


lin_scan.defvjp(_lin_scan_fwd, _lin_scan_bwd)


def _bdot(a, b, dims=(((1,), (0,)), ((), ()))):
    return lax.dot_general(a.astype(BF16), b.astype(BF16), dims, preferred_element_type=F32)


def _each(f, *seqs):
    return tuple(f(*a) for a in zip(*seqs))


def _split_bf16(a):
    hi = a.astype(BF16)
    return hi, (a - hi.astype(F32)).astype(BF16)


def _dot3_raw(a_s, b_s):
    a_hl = _each(_split_bf16, a_s)
    b_hl = _each(_split_bf16, b_s)
    hh = _each(lambda a, b: _bdot(a[0], b[0]), a_hl, b_hl)
    hl = _each(lambda a, b: _bdot(a[0], b[1]), a_hl, b_hl)
    lh = _each(lambda a, b: _bdot(a[1], b[0]), a_hl, b_hl)
    return _each(lambda x, y, z: x + (y + z), hh, hl, lh)


@jax.custom_vjp
def dot3(a_s, b_s):
    return _dot3_raw(a_s, b_s)


def _dot3_fwd(a_s, b_s):
    return _dot3_raw(a_s, b_s), (a_s, b_s)


def _dot3_bwd(res, g_s):
    a_s, b_s = res
    return (_each(lambda g, b: _bdot(g, b, (((1,), (1,)), ((), ()))), g_s, b_s),
            _each(lambda a, g: _bdot(a, g, (((0,), (0,)), ((), ()))), a_s, g_s))


dot3.defvjp(_dot3_fwd, _dot3_bwd)


def _eye(n):
    i = lax.broadcasted_iota(jnp.int32, (n, n), 0)
    j = lax.broadcasted_iota(jnp.int32, (n, n), 1)
    return (i == j).astype(F32)


def _unit_lower_inverse_raw(lmats):
    n = lmats[0].shape[0]
    eye = _eye(n)
    ps = _each(lambda l: -l, lmats)
    invs = _each(lambda x: eye + x, ps)
    k = 1
    while 2 * k < n:
        ps = _each(lambda p: _bdot(p, p), ps)
        invs = _each(lambda inv, p: inv + _bdot(inv, p), invs, ps)
        k *= 2
    prods = _dot3_raw(lmats, invs)
    resids = _each(lambda inv, pr: eye - inv - pr, invs, prods)
    return _each(lambda inv, r: inv + _bdot(inv, r), invs, resids)


@jax.custom_vjp
def unit_lower_inverse(lmats):
    return _unit_lower_inverse_raw(lmats)


def _uli_fwd(lmats):
    invs = _unit_lower_inverse_raw(lmats)
    return invs, invs


def _uli_bwd(invs, g_s):
    ms = _each(lambda inv, g: _bdot(inv, g, (((0,), (0,)), ((), ()))), invs, g_s)
    return (_each(lambda m, inv: -_bdot(m, inv, (((1,), (1,)), ((), ()))), ms, invs),)


unit_lower_inverse.defvjp(_uli_fwd, _uli_bwd)


def _cumsum_raw(x, reverse):
    n = x.shape[0]
    t = lax.broadcasted_iota(jnp.int32, x.shape, 0)
    d = 1
    while d < n:
        if reverse:
            x = x + jnp.where(t < n - d, _roll(x, n - d, 0), 0.0)
        else:
            x = x + jnp.where(t >= d, _roll(x, d, 0), 0.0)
        d *= 2
    return x


@jax.custom_vjp
def cumsum_rows(x):
    return _cumsum_raw(x, False)


def _cumsum_fwd(x):
    return _cumsum_raw(x, False), None


def _cumsum_bwd(_, g):
    return (_cumsum_raw(g, True),)


cumsum_rows.defvjp(_cumsum_fwd, _cumsum_bwd)


_NT = (((1,), (1,)), ((), ()))
_TN = (((0,), (0,)), ((), ()))


def _softplus(x):
    return jnp.maximum(x, 0.0) + jnp.log1p(jnp.exp(-jnp.abs(x)))


def _expm1_nonpos(x):
    poly = x * (1.0 + x * (0.5 + x * (1.0 / 6 + x * (1.0 / 24 + x * (1.0 / 120 + x * (1.0 / 720))))))
    return jnp.where(x > -0.25, poly, jnp.exp(x) - 1.0)


def _rms(x):
    return x * lax.rsqrt(jnp.mean(x * x, axis=-1, keepdims=True) + EPS)


def _causal_conv(x, w, width):
    y = w[width - 1:width, :] * x
    for j in range(width - 1):
        y = y + w[j:j + 1, :] * shift_rows(x, width - 1 - j)
    return y


def _norm_fn(x, g):
    return _rms(x) * g


def _ffn_act_fn(ug, uv, wg, wv, bg, bv):
    return jax.nn.silu(_causal_conv(ug, wg, 3) + bg) * (_causal_conv(uv, wv, 3) + bv)


def _gdn_conv_fn(x, w):
    return jax.nn.silu(_causal_conv(x, w, 4))


def _lru_fn(gate, x, cw, cb, wa, ba, wx, bx, lam):
    xr = _causal_conv(x, cw, 4) + cb
    r = jax.nn.sigmoid(_bdot(xr, wa) + ba)
    i = jax.nn.sigmoid(_bdot(xr, wx) + bx)
    log_a = -LRU_C * r * _softplus(-lam)
    a = jnp.exp(log_a)
    u = jnp.sqrt(-_expm1_nonpos(2.0 * log_a)) * (i * xr)
    hs = lin_scan(a, u)
    return jax.nn.gelu(gate) * hs


def _ret_fn(qs, ks, vs, gates, states, cos2, sin2, dmasks, ktails, qdecs, cdecs):
    c = RET_CHUNK
    n_heads = len(qs)
    n_chunks = qs[0].shape[0] // c
    units = tuple((ci, h) for ci in range(n_chunks) for h in range(n_heads))

    def rows(x, ci):
        return x[ci * c:(ci + 1) * c]

    qrs = tuple(rows(qs[h], ci) * rows(cos2, ci) + swap_halves(rows(qs[h], ci)) * rows(sin2, ci) for ci, h in units)
    krs = tuple((rows(ks[h], ci) * rows(cos2, ci) + swap_halves(rows(ks[h], ci)) * rows(sin2, ci)) * (HEAD ** -0.5) for ci, h in units)
    vus = tuple(rows(vs[h], ci) for ci, h in units)
    scores = tuple(_bdot(q, k, _NT) * dmasks[h] for q, k, (_, h) in zip(qrs, krs, units))
    intra = _each(lambda sc, v: _bdot(sc, v), scores, vus)
    outs = []
    for ci in range(n_chunks):
        mine = slice(ci * n_heads, (ci + 1) * n_heads)
        inter = _each(lambda q, d, s: _bdot(q * d, s), qrs[mine], qdecs, states)
        outs.append(_each(lambda a, b: a + b, intra[mine], inter))
        states = _each(lambda s, cd, k, kt, v: s * cd + _bdot(k * kt, v, _TN), states, cdecs, krs[mine], ktails, vus[mine])
    ys = tuple(_rms(jnp.concatenate([outs[ci][h] for ci in range(n_chunks)], axis=0)) * jax.nn.silu(gates[h]) for h in range(n_heads))
    return ys, states


def _pick_lane(x, lane_idx):
    lane = lax.broadcasted_iota(jnp.int32, x.shape, 1)
    return jnp.sum(jnp.where(lane == lane_idx, x, 0.0), axis=1, keepdims=True)


def _l2norm(x):
    return x * lax.rsqrt(jnp.sum(x * x, axis=-1, keepdims=True) + EPS)


def _gdn_fn(qcs, kcs, vcs, gates, small, a_log, dt_bias, gain, states):
    c = GDN_CHUNK
    n_heads = len(qcs)
    n_chunks = qcs[0].shape[0] // c
    units = tuple((ci, h) for ci in range(n_chunks) for h in range(n_heads))

    def unit_rows(per_head):
        return tuple(per_head[h][ci * c:(ci + 1) * c] for ci, h in units)

    smalls = tuple(small[ci * c:(ci + 1) * c] for ci, _ in units)
    heads = tuple(h for _, h in units)
    intra = _gdn_intra(unit_rows(qcs), unit_rows(kcs), unit_rows(vcs), smalls, heads, a_log, dt_bias)
    outs = []
    for ci in range(n_chunks):
        mine = slice(ci * n_heads, (ci + 1) * n_heads)
        os_, states = _gdn_inter(*(part[mine] for part in intra), states)
        outs.append(os_)
    ys = tuple(_rms(jnp.concatenate([outs[ci][h] for ci in range(n_chunks)], axis=0)) * gain * jax.nn.silu(gates[h])
               for h in range(n_heads))
    return ys, states


def _gdn_inter(qs, ks, us, ws, attns, gcs, g_lasts, states):
    v_news = _each(lambda u, w, s: u - _bdot(w, s), us, ws, states)
    inter = _each(lambda q, gc, s: _bdot(q * jnp.exp(gc), s), qs, gcs, states)
    os_ = _each(lambda x, a, v: x + _bdot(a, v), inter, attns, v_news)
    new_states = _each(lambda s, gl, k, gc, v: s * jnp.exp(gl) + _bdot(k * jnp.exp(gl - gc), v, _TN), states, g_lasts, ks, gcs, v_news)
    return os_, new_states


def _gdn_intra(qcs, kcs, vcs, smalls, heads, a_log, dt_bias):
    c = GDN_CHUNK
    qs = _each(lambda x: _l2norm(x) * (HEAD ** -0.5), qcs)
    ks = _each(_l2norm, kcs)
    betas = _each(lambda sm, h: jax.nn.sigmoid(_pick_lane(sm, h)), smalls, heads)
    gs = _each(lambda sm, h: -jnp.exp(_pick_lane(a_log, h)) * _softplus(_pick_lane(sm, h + N_HEADS) + _pick_lane(dt_bias, h)),
               smalls, heads)
    i = lax.broadcasted_iota(jnp.int32, (c, c), 0)
    j = lax.broadcasted_iota(jnp.int32, (c, c), 1)
    tril = i >= j
    gcs = _each(lambda g: cumsum_rows(jnp.broadcast_to(g, (c, LANES)))[:, :1], gs)
    gc_rows = _each(lambda gc: jnp.broadcast_to(gc, (c, c)), gcs)
    decays = _each(lambda r: jnp.where(tril, jnp.exp(jnp.where(tril, r - r.T, 0.0)), 0.0), gc_rows)
    kbs = _each(lambda k, b: k * b, ks, betas)
    lmats = _each(lambda kb, k, d: jnp.where(i > j, _bdot(kb, k, _NT) * d, 0.0), kbs, ks, decays)
    attns = _each(lambda q, k, d: jnp.where(tril, _bdot(q, k, _NT) * d, 0.0), qs, ks, decays)
    invs = unit_lower_inverse(lmats)
    us = dot3(invs, _each(lambda v, b: v * b, vcs, betas))
    ws = dot3(invs, _each(lambda kb, gc: kb * jnp.exp(gc), kbs, gcs))
    g_lasts = _each(lambda g: jnp.sum(g, axis=0, keepdims=True), gs)
    return qs, ks, us, ws, attns, gcs, g_lasts


def _final_fn(h, g, target):
    y = _rms(h) * g
    return 0.5 * jnp.sum(jnp.mean(jnp.square(y - target), axis=-1, keepdims=True), axis=0, keepdims=True)


def _tile(n, candidates):
    for t in candidates:
        if n % t == 0:
            return t
    raise ValueError(f"no tile for {n}")


def matmul(a, b, *, ta=False, tb=False, add=None, out_dtype=F32, tm=None, tn=None, split=None, layer=None, name):
    m = a.shape[1] if ta else a.shape[0]
    k = a.shape[0] if ta else a.shape[1]
    n = b.shape[0] if tb else b.shape[1]
    assert k == (b.shape[1] if tb else b.shape[0])
    out_shape, out_block, out_index = (m, n), None, lambda i, j: (i, j)
    if split is not None:
        dims4, perm = split
        out_shape = tuple(dims4[p] for p in perm)
        r, cols = out_shape[2:]
        tm, tn = m, tn or _tile(cols, (1408, 512))
        cb = cols // tn
        if perm == (0, 2, 1, 3):
            out_block, out_index = (2, None, r, tn), lambda i, j: (0, j // cb, 0, j % cb)
        elif perm == (1, 0, 2, 3):
            out_block, out_index = (2, N_SHARD, r, tn), lambda i, j: (0, 0, 0, j)
        else:
            raise ValueError(perm)
    tm = tm or _tile(m, (1024, 512, 1408, 256, 128))
    tn = tn or _tile(n, (512, 1408, 256, 128))
    aliases, prev = {}, None
    if layer is not None:
        index, count, prev = layer
        out_shape, out_block, out_index = (count, m, n), (None, tm, tn), lambda i, j: (index, i, j)
    dims = (((0 if ta else 1,), (1 if tb else 0,)), ((), ()))

    def body(a_ref, b_ref, *rest):
        acc = lax.dot_general(a_ref[...].astype(BF16), b_ref[...].astype(BF16), dims, preferred_element_type=F32)
        if add is not None:
            acc = acc + rest[0][...]
        o_ref = rest[-1]
        acc = acc.astype(out_dtype)
        if split is not None and split[1] == (1, 0, 2, 3):
            rows = o_ref.shape[2]
            for s in range(N_SHARD):
                for h in range(2):
                    o_ref[h, s] = acc[(2 * s + h) * rows:(2 * s + h + 1) * rows]
        else:
            o_ref[...] = acc.reshape(o_ref.shape)

    a_spec = pl.BlockSpec((k, tm), lambda i, j: (0, i)) if ta else pl.BlockSpec((tm, k), lambda i, j: (i, 0))
    b_spec = pl.BlockSpec((tn, k), lambda i, j: (j, 0)) if tb else pl.BlockSpec((k, tn), lambda i, j: (0, j))
    o_spec = pl.BlockSpec(out_block or (tm, tn), out_index)
    in_specs, args = [a_spec, b_spec], [a, b]
    if add is not None:
        in_specs.append(o_spec)
        args.append(add)
    if prev is not None:
        aliases = {len(args): 0}
        in_specs.append(pl.BlockSpec(memory_space=pl.ANY))
        args.append(prev)
    return pl.pallas_call(body, out_shape=_sds(out_shape, out_dtype), grid=(m // tm, n // tn), in_specs=in_specs,
                          out_specs=o_spec, input_output_aliases=aliases, compiler_params=_params(), name=name)(*args)


ROW_TILE = 256


def norm_fwd(x, g, *, name):
    t, d = x.shape

    def body(x_ref, g_ref, o_ref):
        o_ref[...] = _norm_fn(x_ref[...], g_ref[...]).astype(BF16)

    return pl.pallas_call(body, out_shape=_sds((t, d), BF16), grid=(t // ROW_TILE,),
                          in_specs=[pl.BlockSpec((ROW_TILE, d), lambda i: (i, 0)), pl.BlockSpec((1, d), lambda i: (0, 0))],
                          out_specs=pl.BlockSpec((ROW_TILE, d), lambda i: (i, 0)), compiler_params=_params(), name=name)(x, g)


def norm_bwd(x, g, dy, dres, *, name):
    t, d = x.shape

    def body(x_ref, g_ref, dy_ref, dres_ref, dx_ref, dg_ref):
        _, vjp = jax.vjp(_norm_fn, x_ref[...], g_ref[...])
        dx, dg = vjp(dy_ref[...])
        dx_ref[...] = dx + dres_ref[...]

        @pl.when(pl.program_id(0) == 0)
        def _():
            dg_ref[...] = jnp.zeros_like(dg_ref)

        dg_ref[...] += dg

    row = pl.BlockSpec((ROW_TILE, d), lambda i: (i, 0))
    vec = pl.BlockSpec((1, d), lambda i: (0, 0))
    return pl.pallas_call(body, out_shape=(_sds((t, d), F32), _sds((1, d), F32)), grid=(t // ROW_TILE,),
                          in_specs=[row, vec, row, row], out_specs=(row, vec), compiler_params=_params(), name=name)(x, g, dy, dres)


def final_fwd_bwd(h, g, target, *, name):
    t, d = h.shape

    def body(h_ref, g_ref, t_ref, loss_ref, dh_ref, dg_ref):
        tgt = t_ref[...]
        loss, vjp = jax.vjp(lambda hh, gg: _final_fn(hh, gg, tgt), h_ref[...], g_ref[...])
        dh, dg = vjp(jnp.ones((1, 1), F32))
        dh_ref[...] = dh

        @pl.when(pl.program_id(0) == 0)
        def _():
            dg_ref[...] = jnp.zeros_like(dg_ref)
            loss_ref[...] = jnp.zeros_like(loss_ref)

        dg_ref[...] += dg
        loss_ref[...] += jnp.broadcast_to(loss, loss_ref.shape)

    row = pl.BlockSpec((ROW_TILE, d), lambda i: (i, 0))
    vec = pl.BlockSpec((1, d), lambda i: (0, 0))
    return pl.pallas_call(body, out_shape=(_sds((1, LANES), F32), _sds((t, d), F32), _sds((1, d), F32)), grid=(t // ROW_TILE,),
                          in_specs=[row, vec, row], out_specs=(pl.BlockSpec((1, LANES), lambda i: (0, 0)), row, vec),
                          compiler_params=_params(), name=name)(h, g, target)


FFN_FWD_COLS = 256
FFN_BWD_COLS = 128


def ffn_act_fwd(u, cw, cb, *, name):
    t = u.shape[0]
    w = FFN_FWD_COLS
    nb = D_FF // w

    def body(ug_ref, uv_ref, wg_ref, wv_ref, bg_ref, bv_ref, o_ref):
        o_ref[...] = _ffn_act_fn(ug_ref[...], uv_ref[...], wg_ref[...], wv_ref[...], bg_ref[...], bv_ref[...]).astype(BF16)

    def col(rows, off):
        return pl.BlockSpec((rows, w), lambda j: (0, j + off))

    return pl.pallas_call(body, out_shape=_sds((t, D_FF), BF16), grid=(nb,),
                          in_specs=[col(t, 0), col(t, nb), col(3, 0), col(3, nb), col(1, 0), col(1, nb)],
                          out_specs=col(t, 0), compiler_params=_params(), name=name)(u, u, cw, cw, cb, cb)


def _put_column_blocks(step, n_steps, blocks, dst_ref, width, stage_ref, sems):
    def copies(at):
        slot = at % 2
        return [pltpu.make_async_copy(stage_ref.at[slot, p], dst_ref.at[:, pl.ds(pl.multiple_of((p * n_steps + at) * width, LANES), width)],
                                      sems.at[slot, p]) for p in range(len(blocks))]

    @pl.when(step >= 2)
    def _():
        for cp in copies(step - 2):
            cp.wait()

    for p, value in enumerate(blocks):
        stage_ref[step % 2, p] = value
    for cp in copies(step):
        cp.start()

    @pl.when(step == n_steps - 1)
    def _():
        for cp in copies(step - 1) + copies(step):
            cp.wait()


def ffn_act_bwd(u, cw, cb, da, *, name):
    t = u.shape[0]
    w = FFN_BWD_COLS
    nb = D_FF // w

    def body(ug_ref, uv_ref, wg_ref, wv_ref, bg_ref, bv_ref, da_ref, dug_ref, duv_ref, dwg_ref, dwv_ref, dbg_ref, dbv_ref):
        _, vjp = jax.vjp(_ffn_act_fn, ug_ref[...], uv_ref[...], wg_ref[...], wv_ref[...], bg_ref[...], bv_ref[...])
        dug, duv, dwg, dwv, dbg, dbv = vjp(da_ref[...])
        dug_ref[...] = dug.astype(BF16)
        duv_ref[...] = duv.astype(BF16)
        dwg_ref[...] = dwg
        dwv_ref[...] = dwv
        dbg_ref[...] = dbg
        dbv_ref[...] = dbv

    def col(rows, off):
        return pl.BlockSpec((rows, w), lambda j: (0, j + off))

    outs = pl.pallas_call(
        body, out_shape=(_sds((t, D_FF), BF16), _sds((t, D_FF), BF16), _sds((3, D_FF), F32), _sds((3, D_FF), F32),
                         _sds((1, D_FF), F32), _sds((1, D_FF), F32)),
        grid=(nb,), in_specs=[col(t, 0), col(t, nb), col(3, 0), col(3, nb), col(1, 0), col(1, nb), col(t, 0)],
        out_specs=(col(t, 0), col(t, 0), col(3, 0), col(3, 0), col(1, 0), col(1, 0)), compiler_params=_params(), name=name,
    )(u, u, cw, cw, cb, cb, da)
    dug, duv, dwg, dwv, dbg, dbv = outs
    return jnp.concatenate([dug, duv], axis=1), jnp.concatenate([dwg, dwv], axis=1), jnp.concatenate([dbg, dbv], axis=1)


GDN_CONV_COLS = 256
GDN_CONV_OFF = 4 * GROUP


def gdn_conv_fwd(p, cw, *, name):
    t = p.shape[0]
    w = GDN_CONV_COLS
    nb = 3 * GROUP // w
    off = GDN_CONV_OFF // w

    def body(x_ref, w_ref, o_ref):
        o_ref[...] = _gdn_conv_fn(x_ref[...], w_ref[...])

    return pl.pallas_call(body, out_shape=_sds((t, 3 * GROUP), F32), grid=(nb,),
                          in_specs=[pl.BlockSpec((t, w), lambda j: (0, j + off)), pl.BlockSpec((4, w), lambda j: (0, j))],
                          out_specs=pl.BlockSpec((t, w), lambda j: (0, j)), compiler_params=_params(), name=name)(p, cw)


def gdn_conv_bwd(p, cw, dc, *, name):
    t = p.shape[0]
    w = GDN_CONV_COLS
    nb = 3 * GROUP // w
    off = GDN_CONV_OFF // w

    def body(x_ref, w_ref, dc_ref, dx_ref, dw_ref):
        _, vjp = jax.vjp(_gdn_conv_fn, x_ref[...], w_ref[...])
        dx, dw = vjp(dc_ref[...])
        dx_ref[...] = dx.astype(BF16)
        dw_ref[...] = dw

    blk = pl.BlockSpec((t, w), lambda j: (0, j))
    wblk = pl.BlockSpec((4, w), lambda j: (0, j))
    return pl.pallas_call(body, out_shape=(_sds((t, 3 * GROUP), BF16), _sds((4, 3 * GROUP), F32)), grid=(nb,),
                          in_specs=[pl.BlockSpec((t, w), lambda j: (0, j + off)), wblk, blk], out_specs=(blk, wblk),
                          compiler_params=_params(), name=name)(p, cw, dc)


def _lru_specs(t):
    w = D_MODEL // LRU_BLOCKS
    gate = pl.BlockSpec((t, w), lambda j: (0, j))
    xin = pl.BlockSpec((t, w), lambda j: (0, j + LRU_BLOCKS))
    cw = pl.BlockSpec((4, w), lambda j: (0, j))
    vec = pl.BlockSpec((1, w), lambda j: (0, j))
    mat = pl.BlockSpec((None, w, w), lambda j: (j, 0, 0))
    return gate, xin, cw, vec, mat


def lru_fwd(gx, cw, cb, wa, ba, wx, bx, lam, *, name):
    t = gx.shape[0]
    gate, xin, cws, vec, mat = _lru_specs(t)

    def body(g_ref, x_ref, cw_ref, cb_ref, wa_ref, ba_ref, wx_ref, bx_ref, lam_ref, o_ref):
        o_ref[...] = _lru_fn(g_ref[...], x_ref[...], cw_ref[...], cb_ref[...], wa_ref[...], ba_ref[...], wx_ref[...],
                             bx_ref[...], lam_ref[...]).astype(BF16)

    return pl.pallas_call(body, out_shape=_sds((t, D_MODEL), BF16), grid=(LRU_BLOCKS,),
                          in_specs=[gate, xin, cws, vec, mat, vec, mat, vec, vec], out_specs=gate,
                          compiler_params=_params(), name=name)(gx, gx, cw, cb, wa, ba, wx, bx, lam)


def lru_bwd(gx, cw, cb, wa, ba, wx, bx, lam, dy, *, name):
    t = gx.shape[0]
    gate, xin, cws, vec, mat = _lru_specs(t)

    def body(g_ref, x_ref, cw_ref, cb_ref, wa_ref, ba_ref, wx_ref, bx_ref, lam_ref, dy_ref,
             dgx_ref, dcw_ref, dcb_ref, dwa_ref, dba_ref, dwx_ref, dbx_ref, dlam_ref, stage_ref, sems):
        _, vjp = jax.vjp(_lru_fn, g_ref[...], x_ref[...], cw_ref[...], cb_ref[...], wa_ref[...], ba_ref[...], wx_ref[...],
                         bx_ref[...], lam_ref[...])
        dg, dx, dcw, dcb, dwa, dba, dwx, dbx, dlam = vjp(dy_ref[...])
        _put_column_blocks(pl.program_id(0), LRU_BLOCKS, (dg.astype(BF16), dx.astype(BF16)), dgx_ref, D_MODEL // LRU_BLOCKS, stage_ref, sems)
        dcw_ref[...] = dcw
        dcb_ref[...] = dcb
        dwa_ref[...] = dwa
        dba_ref[...] = dba
        dwx_ref[...] = dwx
        dbx_ref[...] = dbx
        dlam_ref[...] = dlam

    d = D_MODEL
    w = d // LRU_BLOCKS
    out_shape = (_sds((t, 2 * d), BF16), _sds((4, d), F32), _sds((1, d), F32), _sds((LRU_BLOCKS, w, w), F32),
                 _sds((1, d), F32), _sds((LRU_BLOCKS, w, w), F32), _sds((1, d), F32), _sds((1, d), F32))
    return pl.pallas_call(body, out_shape=out_shape, grid=(LRU_BLOCKS,),
                          in_specs=[gate, xin, cws, vec, mat, vec, mat, vec, vec, gate],
                          out_specs=(pl.BlockSpec(memory_space=pl.ANY), cws, vec, mat, vec, mat, vec, vec),
                          scratch_shapes=[pltpu.VMEM((2, 2, t, w), BF16), pltpu.SemaphoreType.DMA((2, 2))],
                          compiler_params=_params(), name=name)(gx, gx, cw, cb, wa, ba, wx, bx, lam, dy)


def _ret_tables():
    half = HEAD // 2
    inv_freq = (np.float32(ROPE_BASE) ** (-np.arange(half, dtype=np.float32) / np.float32(half))).astype(np.float32)
    ang = (np.arange(SEQ, dtype=np.float32)[:, None] * inv_freq[None, :]).astype(np.float64)
    cos2 = np.concatenate([np.cos(ang), np.cos(ang)], axis=1).astype(np.float32)
    sin2 = np.concatenate([-np.sin(ang), np.sin(ang)], axis=1).astype(np.float32)
    c = RET_CHUNK
    log_gamma = np.log1p(-np.exp2(-5.0 - np.arange(N_HEADS, dtype=np.float64)))
    idx = np.arange(c, dtype=np.float64)
    rel = idx[:, None] - idx[None, :]
    dmask = np.where(rel >= 0, np.exp(log_gamma[:, None, None] * np.maximum(rel, 0.0)), 0.0)
    ones = np.ones((N_HEADS, c, HEAD))
    ktail = np.exp(log_gamma[:, None] * (c - 1 - idx))[:, :, None] * ones
    qdec = np.exp(log_gamma[:, None] * (idx + 1.0))[:, :, None] * ones
    cdec = np.exp(log_gamma * c)[:, None, None] * ones
    return tuple(jnp.asarray(a, F32) for a in (cos2, sin2, dmask, ktail, qdec, cdec))


def _ret_specs(rev):
    c = RET_CHUNK * RET_CHUNKS_PER_STEP
    nc = SEQ // c

    def n_of(n):
        return nc - 1 - n if rev else n

    def group(off):
        return pl.BlockSpec((c, GROUP), lambda n: (n_of(n), off))

    tab = pl.BlockSpec((c, HEAD), lambda n: (n_of(n), 0))
    const = pl.BlockSpec((N_HEADS, RET_CHUNK, HEAD), lambda n: (0, 0, 0))
    state = pl.BlockSpec((N_HEADS, None, HEAD, HEAD), lambda n: (0, n_of(n), 0, 0))
    return group, tab, const, state, nc


def _head(h):
    return slice(h * HEAD, (h + 1) * HEAD)


def ret_fwd(p, tables, *, name):
    group, tab, const, state, nc = _ret_specs(False)

    def body(q_ref, k_ref, v_ref, g_ref, cos_ref, sin_ref, dm_ref, kt_ref, qd_ref, cd_ref, y_ref, st_ref, s_scr):
        @pl.when(pl.program_id(0) == 0)
        def _():
            s_scr[...] = jnp.zeros_like(s_scr)

        heads = range(N_HEADS)
        states = tuple(s_scr[h] for h in heads)
        ys, new_states = _ret_fn(*(tuple(r[:, _head(h)] for h in heads) for r in (q_ref, k_ref, v_ref, g_ref)), states,
                                 cos_ref[...], sin_ref[...], *(tuple(r[h] for h in heads) for r in (dm_ref, kt_ref, qd_ref, cd_ref)))
        for h in heads:
            st_ref[h] = states[h]
            y_ref[:, _head(h)] = ys[h].astype(BF16)
            s_scr[h] = new_states[h]

    return pl.pallas_call(
        body, out_shape=(_sds((SEQ, GROUP), BF16), _sds((N_HEADS, nc, HEAD, HEAD), F32)), grid=(nc,),
        in_specs=[group(0), group(1), group(2), group(3), tab, tab, const, const, const, const],
        out_specs=(group(0), state), scratch_shapes=[pltpu.VMEM((N_HEADS, HEAD, HEAD), F32)], compiler_params=_params(), name=name,
    )(p, p, p, p, *tables)


def ret_bwd(p, tables, states, dy, *, name):
    group, tab, const, state, nc = _ret_specs(True)

    def body(q_ref, k_ref, v_ref, g_ref, cos_ref, sin_ref, dm_ref, kt_ref, qd_ref, cd_ref, st_ref, dy_ref,
             dq_ref, dk_ref, dv_ref, dg_ref, ds_scr):
        @pl.when(pl.program_id(0) == 0)
        def _():
            ds_scr[...] = jnp.zeros_like(ds_scr)

        heads = range(N_HEADS)
        consts = (cos_ref[...], sin_ref[...], *(tuple(r[h] for h in heads) for r in (dm_ref, kt_ref, qd_ref, cd_ref)))
        _, vjp = jax.vjp(lambda *a: _ret_fn(*a, *consts), *(tuple(r[:, _head(h)] for h in heads) for r in (q_ref, k_ref, v_ref, g_ref)),
                         tuple(st_ref[h] for h in heads))
        dqs, dks, dvs, dgs, dss = vjp((tuple(dy_ref[:, _head(h)] for h in heads), tuple(ds_scr[h] for h in heads)))
        for h in heads:
            dq_ref[:, _head(h)] = dqs[h].astype(BF16)
            dk_ref[:, _head(h)] = dks[h].astype(BF16)
            dv_ref[:, _head(h)] = dvs[h].astype(BF16)
            dg_ref[:, _head(h)] = dgs[h].astype(BF16)
            ds_scr[h] = dss[h]

    out = _sds((SEQ, GROUP), BF16)
    return pl.pallas_call(
        body, out_shape=(out, out, out, out), grid=(nc,),
        in_specs=[group(0), group(1), group(2), group(3), tab, tab, const, const, const, const, state, group(0)],
        out_specs=(group(0), group(0), group(0), group(0)), scratch_shapes=[pltpu.VMEM((N_HEADS, HEAD, HEAD), F32)],
        compiler_params=_params(), name=name,
    )(p, p, p, p, *tables, states, dy)


def _gdn_specs(rev):
    c = GDN_CHUNK * GDN_CHUNKS_PER_STEP
    nc = SEQ // c

    def n_of(n):
        return nc - 1 - n if rev else n

    def group(off):
        return pl.BlockSpec((c, GROUP), lambda n: (n_of(n), off))

    small = pl.BlockSpec((c, LANES), lambda n: (n_of(n), 0))
    vec = pl.BlockSpec((1, LANES), lambda n: (0, 0))
    state = pl.BlockSpec((N_HEADS, None, HEAD, HEAD), lambda n: (0, n_of(n), 0, 0))
    return group, small, vec, state, nc


GDN_GATE_GROUP = 7


def gdn_fwd(conv, p, small, a_log, dt_bias, gain, *, name):
    group, sm, vec, state, nc = _gdn_specs(False)

    def body(q_ref, k_ref, v_ref, g_ref, sm_ref, al_ref, dt_ref, gn_ref, y_ref, st_ref, s_scr):
        @pl.when(pl.program_id(0) == 0)
        def _():
            s_scr[...] = jnp.zeros_like(s_scr)

        states = tuple(s_scr[h] for h in range(N_HEADS))
        ys, new_states = _gdn_fn(*(tuple(r[:, _head(h)] for h in range(N_HEADS)) for r in (q_ref, k_ref, v_ref, g_ref)),
                                 sm_ref[...], al_ref[...], dt_ref[...], gn_ref[...], states)
        for h in range(N_HEADS):
            st_ref[h] = states[h]
            y_ref[:, _head(h)] = ys[h].astype(BF16)
            s_scr[h] = new_states[h]

    return pl.pallas_call(
        body, out_shape=(_sds((SEQ, GROUP), BF16), _sds((N_HEADS, nc, HEAD, HEAD), F32)), grid=(nc,),
        in_specs=[group(0), group(1), group(2), group(GDN_GATE_GROUP), sm, vec, vec, vec], out_specs=(group(0), state),
        scratch_shapes=[pltpu.VMEM((N_HEADS, HEAD, HEAD), F32)], compiler_params=_params(), name=name,
    )(conv, conv, conv, p, small, a_log, dt_bias, gain)


def gdn_bwd(conv, p, small, a_log, dt_bias, gain, states, dy, *, name):
    group, sm, vec, state, nc = _gdn_specs(True)

    def body(q_ref, k_ref, v_ref, g_ref, sm_ref, al_ref, dt_ref, gn_ref, st_ref, dy_ref,
             dq_ref, dk_ref, dv_ref, dg_ref, dsm_ref, dal_ref, ddt_ref, dgn_ref, ds_scr):
        @pl.when(pl.program_id(0) == 0)
        def _():
            ds_scr[...] = jnp.zeros_like(ds_scr)
            dal_ref[...] = jnp.zeros_like(dal_ref)
            ddt_ref[...] = jnp.zeros_like(ddt_ref)
            dgn_ref[...] = jnp.zeros_like(dgn_ref)

        per_head = tuple(tuple(r[:, _head(h)] for h in range(N_HEADS)) for r in (q_ref, k_ref, v_ref, g_ref))
        _, vjp = jax.vjp(_gdn_fn, *per_head, sm_ref[...], al_ref[...], dt_ref[...], gn_ref[...],
                         tuple(st_ref[h] for h in range(N_HEADS)))
        cts = (tuple(dy_ref[:, _head(h)] for h in range(N_HEADS)), tuple(ds_scr[h] for h in range(N_HEADS)))
        dqs, dks, dvs, dgs, dsm, dal, ddt, dgn, dss = vjp(cts)
        for h in range(N_HEADS):
            dq_ref[:, _head(h)] = dqs[h]
            dk_ref[:, _head(h)] = dks[h]
            dv_ref[:, _head(h)] = dvs[h]
            dg_ref[:, _head(h)] = dgs[h].astype(BF16)
            ds_scr[h] = dss[h]
        dsm_ref[...] = dsm
        dal_ref[...] += dal
        ddt_ref[...] += ddt
        dgn_ref[...] += dgn

    f = _sds((SEQ, GROUP), F32)
    pv = _sds((1, LANES), F32)
    return pl.pallas_call(
        body, out_shape=(f, f, f, _sds((SEQ, GROUP), BF16), _sds((SEQ, LANES), F32), pv, pv, pv), grid=(nc,),
        in_specs=[group(0), group(1), group(2), group(GDN_GATE_GROUP), sm, vec, vec, vec, state, group(1)],
        out_specs=(group(0), group(0), group(0), group(0), sm, vec, vec, vec), scratch_shapes=[pltpu.VMEM((N_HEADS, HEAD, HEAD), F32)],
        compiler_params=_params(), name=name,
    )(conv, conv, conv, p, small, a_log, dt_bias, gain, states, dy)


PACK_ROW_TILE = 1024


def adamw(w, g, m, v, *, name):
    r = w.shape[0]
    tr = _row_tile(r, LANES)

    def body(w_ref, g_ref, m_ref, v_ref, d_ref, nm_ref, nv_ref):
        gg = g_ref[...]
        nm = ADAM_B1 * m_ref[...] + (1.0 - ADAM_B1) * gg
        nv = ADAM_B2 * v_ref[...] + (1.0 - ADAM_B2) * jnp.square(gg)
        m_hat = nm / (1.0 - ADAM_B1 ** ADAM_STEP)
        v_hat = nv / (1.0 - ADAM_B2 ** ADAM_STEP)
        d_ref[...] = -ADAM_LR * (m_hat / (jnp.sqrt(v_hat) + ADAM_EPS) + ADAM_WD * w_ref[...])
        nm_ref[...] = nm
        nv_ref[...] = nv

    blk = pl.BlockSpec((tr, LANES), lambda i: (i, 0))
    o = _sds((r, LANES), F32)
    return pl.pallas_call(body, out_shape=(o, o, o), grid=(r // tr,), in_specs=[blk] * 4, out_specs=(blk, blk, blk),
                          compiler_params=_params(), name=name)(w, g, m, v)


ELEMENTWISE_BLOCK_BYTES = 2 * 1024 * 1024


def _row_tile(r, c):
    best = None
    for tr in range(8, r + 1, 8):
        if r % tr == 0 and tr * c * 4 <= ELEMENTWISE_BLOCK_BYTES:
            best = tr
    if best is None:
        raise ValueError(f"no row tile for ({r}, {c})")
    return best


def _tile_2d(r, c):
    if any(r % tr == 0 for tr in range(8, r + 1, 8)):
        return _row_tile(r, c), c
    tc = max(t for t in range(LANES, c + 1, LANES) if c % t == 0 and r * t * 4 <= ELEMENTWISE_BLOCK_BYTES)
    return r, tc


def _core_index():
    return lax.axis_index("c").astype(jnp.int32).reshape(1)


def _chip_index():
    return (2 * lax.axis_index("x") + lax.axis_index("y")).astype(jnp.int32).reshape(1)


def adamw_halves(w, m, v, g_own, g_sib, *, layer=0, prev=None, name):
    n_layers, rows, c = w.shape
    r = rows // 2
    tr = _row_tile(r, c)
    nb = r // tr

    def body(c_ref, w_ref, m_ref, v_ref, own_ref, sib_ref, *rest):
        g_ref, d_ref, nm_ref, nv_ref = rest[-4:]
        gg = jnp.where(pl.program_id(0) == c_ref[0], own_ref[...], sib_ref[...])
        nm = ADAM_B1 * m_ref[...] + (1.0 - ADAM_B1) * gg
        nv = ADAM_B2 * v_ref[...] + (1.0 - ADAM_B2) * jnp.square(gg)
        m_hat = nm / (1.0 - ADAM_B1 ** ADAM_STEP)
        v_hat = nv / (1.0 - ADAM_B2 ** ADAM_STEP)
        g_ref[...] = gg
        d_ref[...] = -ADAM_LR * (m_hat / (jnp.sqrt(v_hat) + ADAM_EPS) + ADAM_WD * w_ref[...])
        nm_ref[...] = nm
        nv_ref[...] = nv

    full = pl.BlockSpec((None, tr, c), lambda h, i, cr: (layer, h * nb + i, 0))
    half = pl.BlockSpec((tr, c), lambda h, i, cr: (i, 0))
    o = _sds((n_layers, rows, c), F32)
    prev = list(prev or ())
    gs = pltpu.PrefetchScalarGridSpec(num_scalar_prefetch=1, grid=(2, nb), in_specs=[full, full, full, half, half] + [_ANY] * len(prev),
                                      out_specs=(full, full, full, full))
    n_fixed = 6
    return pl.pallas_call(body, out_shape=(o, o, o, o), grid_spec=gs, compiler_params=_params(), name=name,
                          input_output_aliases={n_fixed + k: k for k in range(len(prev))})(
        _core_index(), w, m, v, g_own, g_sib, *prev)


ADAMW_COLUMN_TILE = 256


def adamw_column_halves(w, m, v, g_own, g_sib, *, name):
    _, rows, cols = w.shape
    tc = ADAMW_COLUMN_TILE
    per_half = cols // 2 // tc

    def body(c_ref, w_ref, m_ref, v_ref, own_ref, sib_ref, g_ref, d_ref, nm_ref, nv_ref):
        gg = jnp.where(pl.program_id(0) // per_half == c_ref[0], own_ref[...], sib_ref[...])
        nm = ADAM_B1 * m_ref[...] + (1.0 - ADAM_B1) * gg
        nv = ADAM_B2 * v_ref[...] + (1.0 - ADAM_B2) * jnp.square(gg)
        m_hat = nm / (1.0 - ADAM_B1 ** ADAM_STEP)
        v_hat = nv / (1.0 - ADAM_B2 ** ADAM_STEP)
        g_ref[...] = gg
        d_ref[...] = -ADAM_LR * (m_hat / (jnp.sqrt(v_hat) + ADAM_EPS) + ADAM_WD * w_ref[...])
        nm_ref[...] = nm
        nv_ref[...] = nv

    full = pl.BlockSpec((None, rows, tc), lambda j, cr: (0, 0, j))
    half = pl.BlockSpec((rows, tc), lambda j, cr: (0, j % per_half))
    o = _sds(w.shape, F32)
    gs = pltpu.PrefetchScalarGridSpec(num_scalar_prefetch=1, grid=(cols // tc,), in_specs=[full, full, full, half, half],
                                      out_specs=(full, full, full, full))
    return pl.pallas_call(body, out_shape=(o, o, o, o), grid_spec=gs, compiler_params=_params(), name=name)(
        _core_index(), w, m, v, g_own, g_sib)


def add_core_halves(g2, land, *, out_dtype, name):
    _, ns, r, cols = g2.shape
    tr, tc = _tile_2d(r, cols)

    def body(c_ref, a_ref, b_ref, o_ref):
        o_ref[...] = (a_ref[...] + b_ref[...]).astype(out_dtype)

    gs = pltpu.PrefetchScalarGridSpec(
        num_scalar_prefetch=1, grid=(ns, r // tr, cols // tc),
        in_specs=[pl.BlockSpec((None, None, tr, tc), lambda s, i, j, cr: (cr[0], s, i, j)),
                  pl.BlockSpec((None, tr, tc), lambda s, i, j, cr: (s, i, j))],
        out_specs=pl.BlockSpec((None, tr, tc), lambda s, i, j, cr: (s, i, j)))
    return pl.pallas_call(body, out_shape=_sds((ns, r, cols), out_dtype), grid_spec=gs, compiler_params=_params(), name=name)(
        _core_index(), g2, land)


def sum_over_chips(own, land, *, scatter, name):
    _, r, cols = own.shape
    tr, tc = _tile_2d(r, cols)

    def body(mine_ref, own_ref, l0, l1, l2, l3, o_ref):
        mine = mine_ref[0]
        mine_val = own_ref[...]
        acc = None
        for s, l_ref in enumerate((l0, l1, l2, l3)):
            val = jnp.where(mine == s, mine_val, l_ref[...]).astype(F32)
            acc = val if acc is None else acc + val
        o_ref[...] = acc

    def slot(s):
        return pl.BlockSpec((None, tr, tc), lambda i, j, mr: (jnp.where(mr[0] == s, (s + 1) % N_SHARD, s), i, j))

    own_spec = pl.BlockSpec((None, tr, tc), lambda i, j, mr: (mr[0] if scatter else 0, i, j))
    gs = pltpu.PrefetchScalarGridSpec(num_scalar_prefetch=1, grid=(r // tr, cols // tc), in_specs=[own_spec] + [slot(s) for s in range(N_SHARD)],
                                      out_specs=pl.BlockSpec((tr, tc), lambda i, j, mr: (i, j)))
    return pl.pallas_call(body, out_shape=_sds((r, cols), F32), grid_spec=gs, compiler_params=_params(), name=name)(
        _chip_index(), own, land, land, land, land)


_ANY = pl.BlockSpec(memory_space=pl.ANY)


def xy_exchange(src, *, scatter, name):
    rh = src.shape[1]

    def body(src_ref, land_ref, send_sems, recv_sems, loc_sem):
        x, y, c = lax.axis_index("x"), lax.axis_index("y"), lax.axis_index("c")
        mine = 2 * x + y
        peers = [(1 - x, y), (x, 1 - y), (1 - x, 1 - y)]

        def piece(shard):
            return src_ref.at[shard] if scatter else src_ref.at[c]

        def copy(k, px, py, dst_slot):
            return pltpu.make_async_remote_copy(src_ref=piece(2 * px + py), dst_ref=land_ref.at[dst_slot], send_sem=send_sems.at[k],
                                                recv_sem=recv_sems.at[k], device_id=(px, py, c), device_id_type=MESH)

        keep = pltpu.make_async_copy(piece(mine), land_ref.at[mine], loc_sem)
        keep.start()
        sends = [copy(k, px, py, mine) for k, (px, py) in enumerate(peers)]
        for cp in sends:
            cp.start()
        for cp in sends:
            cp.wait_send()
        for k, (px, py) in enumerate(peers):
            copy(k, px, py, 2 * px + py).wait_recv()
        keep.wait()

    return pl.pallas_call(body, out_shape=_sds((N_SHARD, rh, LANES), src.dtype), in_specs=[_ANY], out_specs=_ANY,
                          scratch_shapes=[pltpu.SemaphoreType.DMA((3,)), pltpu.SemaphoreType.DMA((3,)), pltpu.SemaphoreType.DMA(())],
                          name=name)(src)


def core_exchange(src, *, send_other_half, name):
    def body(src_ref, out_ref, send_sem, recv_sem, loc_sem):
        x, y, c = lax.axis_index("x"), lax.axis_index("y"), lax.axis_index("c")
        if send_other_half:
            cp = pltpu.make_async_remote_copy(src_ref=src_ref.at[1 - c], dst_ref=out_ref, send_sem=send_sem, recv_sem=recv_sem,
                                              device_id=(x, y, 1 - c), device_id_type=MESH)
            cp.start()
            cp.wait_send()
            cp.wait_recv()
        else:
            keep = pltpu.make_async_copy(src_ref, out_ref.at[c], loc_sem)
            keep.start()
            cp = pltpu.make_async_remote_copy(src_ref=src_ref, dst_ref=out_ref.at[c], send_sem=send_sem, recv_sem=recv_sem,
                                              device_id=(x, y, 1 - c), device_id_type=MESH)
            cp.start()
            cp.wait_send()
            pltpu.make_async_remote_copy(src_ref=src_ref, dst_ref=out_ref.at[1 - c], send_sem=send_sem, recv_sem=recv_sem,
                                         device_id=(x, y, 1 - c), device_id_type=MESH).wait_recv()
            keep.wait()

    out_shape = _sds(src.shape[1:], src.dtype) if send_other_half else _sds((2,) + src.shape, src.dtype)
    return pl.pallas_call(body, out_shape=out_shape, in_specs=[_ANY], out_specs=_ANY,
                          scratch_shapes=[pltpu.SemaphoreType.DMA(()), pltpu.SemaphoreType.DMA(()), pltpu.SemaphoreType.DMA(())],
                          name=name)(src)


def _comm_call(body, ins, out_shapes, sem_counts, name):
    return pl.pallas_call(body, out_shape=tuple(out_shapes), in_specs=[_ANY] * len(ins), out_specs=tuple([_ANY] * len(out_shapes)),
                          scratch_shapes=[pltpu.SemaphoreType.DMA((k,)) for k in sem_counts], name=name)(*ins)


def _sequencer_call(body, ins, out_shapes, sem_counts, name, collective_id):
    return pl.kernel(body, out_type=list(out_shapes), mesh=plsc.ScalarSubcoreMesh(axis_name="sequencer", num_cores=1), name=name,
                     scratch_types=[pltpu.SemaphoreType.DMA((k,)) for k in sem_counts],
                     compiler_params=pltpu.CompilerParams(collective_id=collective_id))(*ins)


def _handshake(peers):
    barrier = pltpu.get_barrier_semaphore()
    for peer in peers:
        pl.semaphore_signal(barrier, inc=1, device_id=peer, device_id_type=MESH)
    pl.semaphore_wait(barrier, len(peers))


def _xy_peers(x, y):
    return [(1 - x, y), (x, 1 - y), (1 - x, 1 - y)]


def gather_halves(halves, *, name, collective_id):
    n = len(halves)

    def body(*refs):
        ins, lands, sibs = refs[:n], refs[n:2 * n], refs[2 * n:3 * n]
        ici_send, ici_recv, d2d_send, d2d_recv = refs[3 * n:]
        x, y, c = lax.axis_index("x"), lax.axis_index("y"), lax.axis_index("c")
        mine = 2 * x + y
        peers = _xy_peers(x, y)
        _handshake([(px, py, c) for px, py in peers] + [(x, y, 1 - c)])

        def ici(i, k, slot):
            px, py = peers[k]
            return pltpu.make_async_remote_copy(src_ref=ins[i].at[c], dst_ref=lands[i].at[slot], send_sem=ici_send.at[3 * i + k],
                                                recv_sem=ici_recv.at[3 * i + k], device_id=(px, py, c), device_id_type=MESH)

        def pass_on(i, k):
            px, py = peers[k]
            slot = 2 * px + py
            return pltpu.make_async_remote_copy(src_ref=lands[i].at[slot], dst_ref=sibs[i].at[slot], send_sem=d2d_send.at[3 * i + k],
                                                recv_sem=d2d_recv.at[3 * i + k], device_id=(x, y, 1 - c), device_id_type=MESH)

        sends = [ici(i, k, mine) for i in range(n) for k in range(3)]
        for cp in sends:
            cp.start()
        passed = []
        for i in range(n):
            for k in range(3):
                px, py = peers[k]
                ici(i, k, 2 * px + py).wait_recv()
                cp = pass_on(i, k)
                cp.start()
                passed.append(cp)
        for cp in passed:
            cp.wait_recv()
        for cp in sends + passed:
            cp.wait_send()

    outs = [_sds((N_SHARD,) + h.shape[1:], h.dtype) for h in halves]
    res = _sequencer_call(body, halves, outs + outs, [3 * n] * 4, name, collective_id)
    return res[:n], res[n:]


def send_other_half(arrays, *, name, collective_id):
    n = len(arrays)

    def body(*refs):
        ins, lands = refs[:n], refs[n:2 * n]
        send_sems, recv_sems = refs[2 * n:]
        x, y, c = lax.axis_index("x"), lax.axis_index("y"), lax.axis_index("c")
        _handshake([(x, y, 1 - c)])
        copies = [pltpu.make_async_remote_copy(src_ref=ins[i].at[1 - c], dst_ref=lands[i], send_sem=send_sems.at[i],
                                               recv_sem=recv_sems.at[i], device_id=(x, y, 1 - c), device_id_type=MESH) for i in range(n)]
        for cp in copies:
            cp.start()
        for cp in copies:
            cp.wait_recv()
        for cp in copies:
            cp.wait_send()

    return _sequencer_call(body, arrays, [_sds(a.shape[1:], a.dtype) for a in arrays], [n, n], name, collective_id)


_HBM = pl.BlockSpec(memory_space=pltpu.HBM)
_SEM = pl.BlockSpec(memory_space=pltpu.SEMAPHORE)
_SPLIT_COPY = dict(has_side_effects=pltpu.SideEffectType.DATAFLOW_SIDE_EFFECTING)


def _chip_copy(ins, lands, send_sems, recv_sems, scatter, i, k, receive):
    x, y, c = lax.axis_index("x"), lax.axis_index("y"), lax.axis_index("c")
    px, py = _xy_peers(x, y)[k]
    theirs, mine = 2 * px + py, 2 * x + y
    src = ins[i].at[theirs] if scatter[i] else ins[i].at[0]
    return pltpu.make_async_remote_copy(src_ref=src, dst_ref=lands[i].at[theirs if receive else mine], send_sem=send_sems.at[3 * i + k],
                                        recv_sem=recv_sems.at[3 * i + k], device_id=(px, py, c), device_id_type=MESH)


def send_to_chips_start(arrays, scatter, *, name):
    n = len(arrays)

    def body(*refs):
        send_sems, recv_sems = refs[2 * n], refs[2 * n + 1]
        ins, lands = refs[2 * n + 2:3 * n + 2], refs[3 * n + 2:4 * n + 2]
        token = refs[4 * n + 2]
        for i in range(n):
            for k in range(3):
                _chip_copy(ins, lands, send_sems, recv_sems, scatter, i, k, receive=False).start()
        token[...] = jnp.zeros_like(token)

    land_shapes = [(N_SHARD,) + a.shape[1:] for a in arrays]
    operands = [pltpu.with_memory_space_constraint(a, pltpu.HBM) for a in arrays]
    operands += [pltpu.with_memory_space_constraint(lax.empty(s, a.dtype), pltpu.HBM) for s, a in zip(land_shapes, arrays)]
    out_shape = ([pltpu.SemaphoreType.DMA((3 * n,)), pltpu.SemaphoreType.DMA((3 * n,))] + [pltpu.HBM(a.shape, a.dtype) for a in arrays]
                 + [pltpu.HBM(s, a.dtype) for s, a in zip(land_shapes, arrays)] + [_sds((8, LANES), F32)])
    res = pl.pallas_call(body, name=name, out_shape=out_shape, in_specs=[_HBM] * (2 * n),
                         out_specs=[_SEM, _SEM] + [_HBM] * (2 * n) + [pl.BlockSpec(memory_space=pltpu.VMEM)],
                         input_output_aliases={i: 2 + i for i in range(2 * n)}, compiler_params=pltpu.CompilerParams(**_SPLIT_COPY))(*operands)
    return (res[0], res[1], res[2:2 + n], res[2 + n:2 + 2 * n], scatter), res[-1]


def send_to_chips_wait(state, after, *, name):
    send_sems, recv_sems, arrays, lands, scatter = state
    n = len(arrays)

    def body(*refs):
        ins, landing = refs[:n], refs[n:2 * n]
        send_sems, recv_sems = refs[2 * n], refs[2 * n + 1]
        for i in range(n):
            for k in range(3):
                _chip_copy(ins, landing, send_sems, recv_sems, scatter, i, k, receive=True).wait_recv()
        for i in range(n):
            for k in range(3):
                _chip_copy(ins, landing, send_sems, recv_sems, scatter, i, k, receive=False).wait_send()

    out_shape = [pltpu.HBM(a.shape, a.dtype) for a in list(arrays) + list(lands)]
    res = pl.pallas_call(body, name=name, out_shape=out_shape, in_specs=[_HBM] * (2 * n) + [_SEM, _SEM] + [_ANY] * len(after),
                         out_specs=[_HBM] * (2 * n), input_output_aliases={i: i for i in range(2 * n)},
                         compiler_params=pltpu.CompilerParams(**_SPLIT_COPY))(*arrays, *lands, send_sems, recv_sems, *after)
    return res[:n], res[n:]


def swap_with_other_core(arrays, *, name, collective_id):
    n = len(arrays)

    def body(*refs):
        ins, lands = refs[:n], refs[n:2 * n]
        send_sems, recv_sems = refs[2 * n:]
        x, y, c = lax.axis_index("x"), lax.axis_index("y"), lax.axis_index("c")
        _handshake([(x, y, 1 - c)])
        copies = [pltpu.make_async_remote_copy(src_ref=ins[i], dst_ref=lands[i], send_sem=send_sems.at[i], recv_sem=recv_sems.at[i],
                                               device_id=(x, y, 1 - c), device_id_type=MESH) for i in range(n)]
        for cp in copies:
            cp.start()
        for cp in copies:
            cp.wait_recv()
        for cp in copies:
            cp.wait_send()

    return _sequencer_call(body, arrays, [_sds(a.shape, a.dtype) for a in arrays], [n, n], name, collective_id)


def _pack_rows(n_elems, row_multiple):
    rows = -(-n_elems // LANES)
    return -(-rows // row_multiple) * row_multiple


def _pack(arrays, rows, dtype):
    flat = jnp.concatenate([a.reshape(-1).astype(dtype) for a in arrays])
    return jnp.pad(flat, (0, rows * LANES - flat.shape[0])).reshape(rows, LANES)


def _unpack(packed, shapes):
    flat = packed.reshape(-1)
    out, off = [], 0
    for s in shapes:
        n = int(np.prod(s))
        out.append(flat[off:off + n].reshape(s))
        off += n
    return out


def all_gather_shards(shards, axes, dtype, row_multiple, tag):
    shapes = [s.shape for s in shards]
    rows = _pack_rows(sum(int(np.prod(s)) for s in shapes), row_multiple)
    packed = _pack(shards, rows, dtype).reshape(2, rows // 2, LANES)
    land = xy_exchange(packed, scatter=False, name=f"gather_xy_{tag}")
    both = core_exchange(land, send_other_half=False, name=f"gather_c_{tag}")
    per_shard = jnp.swapaxes(both, 0, 1).reshape(N_SHARD, rows, LANES)
    pieces = [_unpack(per_shard[s], shapes) for s in range(N_SHARD)]
    return [jnp.concatenate([pieces[s][i] for s in range(N_SHARD)], axis=ax) for i, ax in enumerate(axes)]


def _ordered_before(first, then):
    if then is None:
        return first, None
    return lax.optimization_barrier((first, then))


def reduce_between_cores(arrays, scatter, *, tag, collective_id, before=None):
    arrays, before = _ordered_before(arrays, before)
    land = send_other_half(arrays, name=f"reduce_core_send_{tag}", collective_id=collective_id)
    return (arrays, land, scatter, tag, collective_id), before


def reduce_between_chips(state, before=None):
    arrays, land, scatter, tag, collective_id = state
    chip = [add_core_halves(a, l, out_dtype=BF16 if sc else F32, name=f"reduce_core_add_{tag}_{i}")
            for i, (a, l, sc) in enumerate(zip(arrays, land, scatter))]
    sending, token = send_to_chips_start(chip, scatter, name=f"reduce_chip_start_{tag}")
    token, before = _ordered_before(token, before)
    return (sending, token, scatter, tag, collective_id), before


def reduce_finish(state, after):
    sending, token, scatter, tag, collective_id = state
    chip, land = send_to_chips_wait(sending, tuple(after) + (token,), name=f"reduce_chip_wait_{tag}")
    own = [sum_over_chips(ch, l, scatter=sc, name=f"reduce_chip_add_{tag}_{i}") for i, (ch, l, sc) in enumerate(zip(chip, land, scatter))]
    sib = swap_with_other_core(own, name=f"reduce_core_swap_{tag}", collective_id=collective_id + 2)
    return own, sib


def _ffn_layer_fwd(h, norm_g, w_up, cw, cb, w_down, tag):
    hn = norm_fwd(h, norm_g, name=f"ffn_norm_{tag}")
    u = matmul(hn, w_up, name=f"ffn_up_{tag}")
    act = ffn_act_fwd(u, cw, cb, name=f"ffn_act_{tag}")
    out = matmul(act, w_down, add=h, name=f"ffn_down_{tag}")
    return out, (h, hn, u, act)


def _travel_layout(array):
    return BIG_ARRAYS[array][3], BIG_ARRAYS[array][4]


def _ffn_layer_bwd(saved, dout, norm_g, w_up, cw, cb, w_down, tag, d_w_down_other=None):
    h, hn, u, act = saved
    dact = matmul(dout, w_down, tb=True, name=f"ffn_down_dx_{tag}")
    d_w_down = matmul(act, dout, ta=True, layer=(int(tag), 2, d_w_down_other), name=f"ffn_down_dw_{tag}")
    du, dcw, dcb = ffn_act_bwd(u, cw, cb, dact, name=f"ffn_act_bwd_{tag}")
    dhn = matmul(du, w_up, tb=True, name=f"ffn_up_dx_{tag}")
    d_w_up = matmul(hn, du, ta=True, split=_travel_layout(f"ffn_w_up_{tag}"), name=f"ffn_up_dw_{tag}")
    dh, dg = norm_bwd(h, norm_g, dhn, dout, name=f"ffn_norm_bwd_{tag}")
    return dh, dg, d_w_up, dcw, dcb, d_w_down


def local_step(x, target, w, stage=lambda name, tensors, grads=None: tensors):
    g = {}
    tables = _ret_tables()
    x = stage("start", x)
    w_in_t = w["ret_gdn_w_in"]
    w_main = w_in_t[:MIX_MAIN]
    w_small = jnp.pad(w_in_t[MIX_MAIN:], ((0, LANES - 2 * N_HEADS), (0, 0)))
    a_log = jnp.pad(w["gdn_a_log"], ((0, 0), (0, LANES - N_HEADS)))
    dt_bias = jnp.pad(w["gdn_dt_bias"], ((0, 0), (0, LANES - N_HEADS)))

    hn0 = stage("normed", norm_fwd(x, w["norm_mix"][0:1], name="mix0_norm"))
    p = matmul(hn0, w_main, tb=True, name="mix0_in")
    small = matmul(hn0, w_small, tb=True, name="mix0_in_small")
    y_ret, s_ret = ret_fwd(p, tables, name="ret_fwd")
    conv = gdn_conv_fwd(p, w["gdn_conv_w"], name="gdn_conv")
    y_gdn, s_gdn = gdn_fwd(conv, p, small, a_log, dt_bias, w["gdn_out_gain"], name="gdn_fwd")
    y0 = stage("mixed", jnp.concatenate([y_ret, y_gdn], axis=1))
    h1 = matmul(y0, w["ret_gdn_w_out"], add=x, name="mix0_out")
    h2, ffn0 = _ffn_layer_fwd(h1, w["norm_ffn"][0:1], w["ffn_w_up"][0], w["ffn_conv_w"][0], w["ffn_conv_b"][0:1], w["ffn_w_down"][0], "0")
    h2 = stage("layer0", h2)

    hn1 = norm_fwd(h2, w["norm_mix"][1:2], name="mix1_norm")
    gx = matmul(hn1, w["lru_w_in"], name="mix1_in")
    lru_p = (w["lru_conv_w"], w["lru_conv_b"], w["lru_w_a"], w["lru_b_a"], w["lru_w_x"], w["lru_b_x"], w["lru_lambda"])
    y1 = lru_fwd(gx, *lru_p, name="lru_fwd")
    h3 = matmul(y1, w["lru_w_out"], add=h2, name="mix1_out")
    h4, ffn1 = _ffn_layer_fwd(h3, w["norm_ffn"][1:2], w["ffn_w_up"][1], w["ffn_conv_w"][1], w["ffn_conv_b"][1:2], w["ffn_w_down"][1], "1")

    loss, dh4, g["norm_final"] = final_fwd_bwd(h4, w["norm_final"], target, name="final")

    dh3, dgf1, dwu1, dcw1, dcb1, dwd1 = _ffn_layer_bwd(ffn1, dh4, w["norm_ffn"][1:2], w["ffn_w_up"][1], w["ffn_conv_w"][1],
                                                     w["ffn_conv_b"][1:2], w["ffn_w_down"][1], "1")
    g["ffn_w_up_1"] = dwu1
    dh3 = stage("grads0_ready", dh3, g)
    dy1 = matmul(dh3, w["lru_w_out"], tb=True, name="mix1_out_dx")
    g["lru_w_out"] = matmul(y1, dh3, ta=True, split=_travel_layout("lru_w_out"), name="mix1_out_dw")
    dgx, g["lru_conv_w"], g["lru_conv_b"], g["lru_w_a"], g["lru_b_a"], g["lru_w_x"], g["lru_b_x"], g["lru_lambda"] = lru_bwd(
        gx, *lru_p, dy1, name="lru_bwd")
    dgx = stage("grads0_send", dgx, g)
    dhn1 = matmul(dgx, w["lru_w_in"], tb=True, name="mix1_in_dx")
    g["lru_w_in"] = matmul(hn1, dgx, ta=True, split=_travel_layout("lru_w_in"), name="mix1_in_dw")
    dh2, dgm1 = norm_bwd(h2, w["norm_mix"][1:2], dhn1, dh3, name="mix1_norm_bwd")
    dh2 = stage("grads1_ready", dh2, g)

    dh1, dgf0, dwu0, dcw0, dcb0, dwd0 = _ffn_layer_bwd(ffn0, dh2, w["norm_ffn"][0:1], w["ffn_w_up"][0], w["ffn_conv_w"][0],
                                                     w["ffn_conv_b"][0:1], w["ffn_w_down"][0], "0", dwd1)
    g["ffn_w_up_0"] = dwu0
    g["ffn_w_down"] = dwd0
    dh1 = stage("grads2_ready", stage("grads1_send", dh1, g), g)
    dy0 = matmul(dh1, w["ret_gdn_w_out"], tb=True, name="mix0_out_dx")
    g["ret_gdn_w_out"] = matmul(y0, dh1, ta=True, split=_travel_layout("ret_gdn_w_out"), name="mix0_out_dw")
    dq_r, dk_r, dv_r, dg_r = ret_bwd(p, tables, s_ret, dy0, name="ret_bwd")
    dy0, dq_r = stage("grads2_send", (dy0, dq_r), g)
    dcq, dck, dcv, dg_d, dsmall, dal, ddt, dgain = gdn_bwd(conv, p, small, a_log, dt_bias, w["gdn_out_gain"], s_gdn, dy0, name="gdn_bwd")
    dconv = jnp.concatenate([dcq, dck, dcv], axis=1)
    dp_conv, g["gdn_conv_w"] = gdn_conv_bwd(p, w["gdn_conv_w"], dconv, name="gdn_conv_bwd")
    dp = jnp.concatenate([dq_r, dk_r, dv_r, dg_r, dp_conv, dg_d], axis=1)
    dhn0 = matmul(dp, w_main, name="mix0_in_dx")
    dhn0 = matmul(dsmall, w_small, add=dhn0, name="mix0_in_small_dx")
    d_w_main = matmul(dp, hn0, ta=True, name="mix0_in_dw")
    d_w_small = matmul(dsmall, hn0, ta=True, name="mix0_in_small_dw")
    g["ret_gdn_w_in"] = jnp.concatenate([d_w_main, d_w_small[:2 * N_HEADS]], axis=0)
    dx, dgm0 = norm_bwd(x, w["norm_mix"][0:1], dhn0, dh1, name="mix0_norm_bwd")

    g["gdn_a_log"] = dal[:, :N_HEADS]
    g["gdn_dt_bias"] = ddt[:, :N_HEADS]
    g["gdn_out_gain"] = dgain
    g["norm_mix"] = jnp.concatenate([dgm0, dgm1], axis=0)
    g["norm_ffn"] = jnp.concatenate([dgf0, dgf1], axis=0)
    g["ffn_conv_w"] = jnp.stack([dcw0, dcw1])
    g["ffn_conv_b"] = jnp.concatenate([dcb0, dcb1], axis=0)
    return loss, dx, g


WEIGHTS = ("norm_mix", "norm_ffn", "ret_gdn_w_in", "gdn_conv_w", "gdn_a_log", "gdn_dt_bias", "gdn_out_gain", "ret_gdn_w_out",
           "lru_w_in", "lru_conv_w", "lru_conv_b", "lru_w_a", "lru_b_a", "lru_w_x", "lru_b_x", "lru_lambda", "lru_w_out",
           "ffn_w_up", "ffn_conv_w", "ffn_conv_b", "ffn_w_down", "norm_final")
MATMUL_SHARDED = {"ret_gdn_w_in": 1, "ret_gdn_w_out": 0, "lru_w_in": 1, "lru_w_out": 0, "ffn_w_up": 2, "ffn_w_down": 1}
VECTOR_SHARDED = {"gdn_conv_w": 1, "lru_conv_w": 1, "lru_conv_b": 1, "lru_b_a": 1, "lru_b_x": 1, "lru_lambda": 1, "ffn_conv_w": 2}
SHARDED = {**MATMUL_SHARDED, **VECTOR_SHARDED}
REPLICATED = tuple(n for n in WEIGHTS if n not in SHARDED)
SQUEEZE = {"ret_gdn_w_in", "gdn_conv_w", "ret_gdn_w_out", "lru_w_in", "lru_conv_w", "lru_w_a", "lru_w_x", "lru_w_out"}
MIX_IN = MIX_MAIN + 2 * N_HEADS
BIG_ARRAYS = {
    "ret_gdn_w_in": ("ret_gdn_w_in", None, (MIX_IN, D_MODEL), (N_SHARD, MIX_IN // N_SHARD, 2, D_MODEL // 2), (2, 0, 1, 3)),
    "ret_gdn_w_out": ("ret_gdn_w_out", None, (2 * GROUP, D_MODEL), (N_SHARD, 2, GROUP // N_SHARD, D_MODEL), (1, 0, 2, 3)),
    "lru_w_in": ("lru_w_in", None, (D_MODEL, 2 * D_MODEL), (2, D_MODEL // 2, N_SHARD, 2 * D_MODEL // N_SHARD), (0, 2, 1, 3)),
    "lru_w_out": ("lru_w_out", None, (D_MODEL, D_MODEL), (N_SHARD, 2, D_MODEL // (2 * N_SHARD), D_MODEL), (1, 0, 2, 3)),
    "ffn_w_up_0": ("ffn_w_up", 0, (D_MODEL, 2 * D_FF), (2, D_MODEL // 2, N_SHARD, 2 * D_FF // N_SHARD), (0, 2, 1, 3)),
    "ffn_w_up_1": ("ffn_w_up", 1, (D_MODEL, 2 * D_FF), (2, D_MODEL // 2, N_SHARD, 2 * D_FF // N_SHARD), (0, 2, 1, 3)),
    "ffn_w_down": ("ffn_w_down", None, (2, D_FF, D_MODEL), (2, N_SHARD, D_FF // N_SHARD, D_MODEL), (0, 1, 2, 3)),
}
GATHER_GROUPS = (("ret_gdn_w_in",), ("ret_gdn_w_out", "ffn_w_up_0", "ffn_w_down"), ("lru_w_in", "lru_w_out", "ffn_w_up_1"))
REDUCE_GROUPS = (("ffn_w_up_1",), ("lru_w_in", "lru_w_out"), ("ffn_w_up_0", "ffn_w_down"), ("ret_gdn_w_out", "ret_gdn_w_in"))
BLOCK_WEIGHTS = ("lru_w_a", "lru_w_x")
GATHER_COLLECTIVE_ID = 1
REDUCE_COLLECTIVE_ID = GATHER_COLLECTIVE_ID + len(GATHER_GROUPS)


TRANSPOSED = ("ret_gdn_w_in",)


def _shard_of(array, tensors):
    weight, layer = BIG_ARRAYS[array][:2]
    t = tensors[weight]
    if weight in TRANSPOSED:
        return jnp.swapaxes(t, 1, 2)[0]
    return _local_view(weight, t) if layer is None else t[layer]


def _core_halves(array, shard):
    _, _, _, split, perm = BIG_ARRAYS[array]
    kept = [k for k in range(4) if k != perm[1]]
    order = [kept.index(perm[0]), kept.index(perm[2]), kept.index(perm[3])]
    return shard.reshape([split[k] for k in kept]).transpose(order)


def _local_view(name, a):
    if name in SQUEEZE:
        return a[0]
    if a.ndim == 1:
        return a[None, :]
    return a


def kernel(x, norm_mix, norm_ffn, ret_gdn_w_in, gdn_conv_w, gdn_a_log, gdn_dt_bias, gdn_out_gain, ret_gdn_w_out, lru_w_in, lru_conv_w, lru_conv_b, lru_w_a, lru_b_a, lru_w_x, lru_b_x, lru_lambda, lru_w_out, ffn_w_up, ffn_conv_w, ffn_conv_b, ffn_w_down, norm_final, loss_target, m_norm_mix, m_norm_ffn, m_ret_gdn_w_in, m_gdn_conv_w, m_gdn_a_log, m_gdn_dt_bias, m_gdn_out_gain, m_ret_gdn_w_out, m_lru_w_in, m_lru_conv_w, m_lru_conv_b, m_lru_w_a, m_lru_b_a, m_lru_w_x, m_lru_b_x, m_lru_lambda, m_lru_w_out, m_ffn_w_up, m_ffn_conv_w, m_ffn_conv_b, m_ffn_w_down, m_norm_final, v_norm_mix, v_norm_ffn, v_ret_gdn_w_in, v_gdn_conv_w, v_gdn_a_log, v_gdn_dt_bias, v_gdn_out_gain, v_ret_gdn_w_out, v_lru_w_in, v_lru_conv_w, v_lru_conv_b, v_lru_w_a, v_lru_b_a, v_lru_w_x, v_lru_b_x, v_lru_lambda, v_lru_w_out, v_ffn_w_up, v_ffn_conv_w, v_ffn_conv_b, v_ffn_w_down, v_norm_final):
    given = dict(norm_mix=norm_mix, norm_ffn=norm_ffn, ret_gdn_w_in=ret_gdn_w_in, gdn_conv_w=gdn_conv_w, gdn_a_log=gdn_a_log, gdn_dt_bias=gdn_dt_bias, gdn_out_gain=gdn_out_gain, ret_gdn_w_out=ret_gdn_w_out, lru_w_in=lru_w_in, lru_conv_w=lru_conv_w, lru_conv_b=lru_conv_b, lru_w_a=lru_w_a, lru_b_a=lru_b_a, lru_w_x=lru_w_x, lru_b_x=lru_b_x, lru_lambda=lru_lambda, lru_w_out=lru_w_out, ffn_w_up=ffn_w_up, ffn_conv_w=ffn_conv_w, ffn_conv_b=ffn_conv_b, ffn_w_down=ffn_w_down, norm_final=norm_final)
    mom1 = dict(norm_mix=m_norm_mix, norm_ffn=m_norm_ffn, ret_gdn_w_in=m_ret_gdn_w_in, gdn_conv_w=m_gdn_conv_w, gdn_a_log=m_gdn_a_log, gdn_dt_bias=m_gdn_dt_bias, gdn_out_gain=m_gdn_out_gain, ret_gdn_w_out=m_ret_gdn_w_out, lru_w_in=m_lru_w_in, lru_conv_w=m_lru_conv_w, lru_conv_b=m_lru_conv_b, lru_w_a=m_lru_w_a, lru_b_a=m_lru_b_a, lru_w_x=m_lru_w_x, lru_b_x=m_lru_b_x, lru_lambda=m_lru_lambda, lru_w_out=m_lru_w_out, ffn_w_up=m_ffn_w_up, ffn_conv_w=m_ffn_conv_w, ffn_conv_b=m_ffn_conv_b, ffn_w_down=m_ffn_w_down, norm_final=m_norm_final)
    mom2 = dict(norm_mix=v_norm_mix, norm_ffn=v_norm_ffn, ret_gdn_w_in=v_ret_gdn_w_in, gdn_conv_w=v_gdn_conv_w, gdn_a_log=v_gdn_a_log, gdn_dt_bias=v_gdn_dt_bias, gdn_out_gain=v_gdn_out_gain, ret_gdn_w_out=v_ret_gdn_w_out, lru_w_in=v_lru_w_in, lru_conv_w=v_lru_conv_w, lru_conv_b=v_lru_conv_b, lru_w_a=v_lru_w_a, lru_b_a=v_lru_b_a, lru_w_x=v_lru_w_x, lru_b_x=v_lru_b_x, lru_lambda=v_lru_lambda, lru_w_out=v_lru_w_out, ffn_w_up=v_ffn_w_up, ffn_conv_w=v_ffn_conv_w, ffn_conv_b=v_ffn_conv_b, ffn_w_down=v_ffn_w_down, norm_final=v_norm_final)

    local = {n: _local_view(n, a) for n, a in given.items()}

    core = lax.axis_index("c")
    chip = 2 * lax.axis_index("x") + lax.axis_index("y")
    is_my_chip = lax.broadcasted_iota(jnp.int32, (N_SHARD, 1, 1), 0) == chip

    def by_core(mine, other):
        return jnp.where(core == 0, jnp.stack([mine, other]), jnp.stack([other, mine]))

    vec_names, rp_names = list(VECTOR_SHARDED), list(REPLICATED)
    full = dict(zip(vec_names, all_gather_shards([local[n] for n in vec_names], [SHARDED[n] for n in vec_names], F32, 32, "p")))
    for n in rp_names:
        full[n] = local[n]
    in_flight = {}

    def launch(gi, after=None):
        halves = []
        for a in GATHER_GROUPS[gi]:
            halves.append(_core_halves(a, _shard_of(a, given).astype(BF16)))
        if after is not None:
            halves, after = lax.optimization_barrier((halves, after))
        in_flight[gi] = (halves,) + gather_halves(halves, name=f"gather_weights_{gi}", collective_id=GATHER_COLLECTIVE_ID + gi)
        return after

    def land(gi, after):
        halves, lands, sibs = in_flight[gi]
        (lands, sibs), after = lax.optimization_barrier(((lands, sibs), after))
        for a, mine, got, passed in zip(GATHER_GROUPS[gi], halves, lands, sibs):
            weight, layer, full_shape, split, perm = BIG_ARRAYS[a]
            half_mine = jnp.where(is_my_chip, jnp.where(core == 0, mine[0], mine[1])[None], got)
            half_other = jnp.where(is_my_chip, jnp.where(core == 0, mine[1], mine[0])[None], passed)
            value = by_core(half_mine, half_other).transpose(tuple(np.argsort(perm))).reshape(full_shape)
            if layer is None:
                full[weight] = value
            else:
                full.setdefault(weight, [None, None])[layer] = value
        return after

    reducing = {}

    def reduce_ready(gi, grads, then=None, extra=()):
        def travelling(a):
            split, perm = _travel_layout(a)
            return grads[a] if grads[a].ndim == 4 else grads[a].reshape(split).transpose(perm)

        arrays = [travelling(a) for a in REDUCE_GROUPS[gi]] + list(extra)
        scatter = [True] * len(REDUCE_GROUPS[gi]) + [False] * len(extra)
        reducing[gi], then = reduce_between_cores(arrays, scatter, tag=str(gi), collective_id=REDUCE_COLLECTIVE_ID + 3 * gi, before=then)
        return then

    def reduce_send(gi, then=None):
        reducing[gi], then = reduce_between_chips(reducing[gi], before=then)
        return then

    def stage(name, tensors, grads=None):
        if name == "start":
            launch(0)
            launch(1)
            packed["wmv"], tensors = lax.optimization_barrier((packed["wmv"], tensors))
            return land(0, tensors)
        if name == "normed":
            return launch(2, tensors)
        if name in ("mixed", "layer0"):
            return land({"mixed": 1, "layer0": 2}[name], tensors)
        gi = int(name[len("grads")])
        return reduce_ready(gi, grads, tensors) if name.endswith("_ready") else reduce_send(gi, tensors)

    small_names = [n for n in rp_names if n not in BLOCK_WEIGHTS] + vec_names
    loc_shapes = [local[n].shape for n in small_names]
    loc_rows = _pack_rows(sum(int(np.prod(s)) for s in loc_shapes), 256)
    packed = {"wmv": [_pack([src[n] for n in small_names], loc_rows, F32) for src in (given, mom1, mom2)]}

    loss_part, dx, grads = local_step(x[0], loss_target[0], full, stage)
    small_shapes = [grads[n].shape for n in small_names] + [(1, 1)]
    small_rows = _pack_rows(sum(int(np.prod(s)) for s in small_shapes), 16)
    small = _pack([grads[n] for n in small_names] + [loss_part[:, :1]], small_rows, F32).reshape(2, 1, small_rows // 2, LANES)
    last = len(REDUCE_GROUPS) - 1
    halves_of_blocks = [grads[n].reshape(2, 1, LRU_BLOCKS * HEAD // 2, HEAD) for n in BLOCK_WEIGHTS]
    reduce_ready(last, grads, extra=[small] + halves_of_blocks)
    reduce_send(last)
    reduced, result = {}, {}

    def finish(gi, after):
        g_own, g_sib = reduce_finish(reducing[gi], after)
        reduced.update(zip(list(REDUCE_GROUPS[gi]) + ["small"] + list(BLOCK_WEIGHTS), zip(g_own, g_sib)))

    def update(n):
        if n in TRANSPOSED:
            w3, m3, v3 = (jnp.swapaxes(t, 1, 2) for t in (given[n], mom1[n], mom2[n]))
            result[n] = tuple(jnp.swapaxes(t, 1, 2) for t in adamw_column_halves(w3, m3, v3, *reduced[n], name=f"adamw_{n}"))
            return
        done = None
        for a in (k for k, spec in BIG_ARRAYS.items() if spec[0] == n):
            r, cols = reduced[a][0].shape
            layer = BIG_ARRAYS[a][1] or 0
            w3, m3, v3 = (t if BIG_ARRAYS[a][1] is not None else t.reshape(1, 2 * r, cols) for t in (given[n], mom1[n], mom2[n]))
            done = adamw_halves(w3, m3, v3, *reduced[a], layer=layer, prev=done, name=f"adamw_{a}")
        result[n] = done

    for gi in range(last):
        finish(gi, (dx, reducing[last][1]))
    late = {BIG_ARRAYS[a][0] for a in REDUCE_GROUPS[last]}
    for n in MATMUL_SHARDED:
        if n not in late:
            update(n)
    finish(last, tuple(result[n][0] for n in MATMUL_SHARDED if n not in late))
    for n in MATMUL_SHARDED:
        if n in late:
            update(n)

    for n in BLOCK_WEIGHTS:
        w3, m3, v3 = (t.reshape(1, LRU_BLOCKS * HEAD, HEAD) for t in (given[n], mom1[n], mom2[n]))
        result[n] = adamw_halves(w3, m3, v3, *reduced[n], name=f"adamw_{n}")

    *small_sums, loss_sum = _unpack(by_core(*reduced["small"]).reshape(small_rows, LANES), small_shapes)
    loss = loss_sum[0, 0]
    g_small = dict(zip(small_names, small_sums))
    for n in vec_names:
        size = local[n].shape[SHARDED[n]]
        g_small[n] = lax.dynamic_slice_in_dim(g_small[n], chip * size, size, axis=SHARDED[n])
    w_pack, m_pack, v_pack = packed["wmv"]
    d_s, m_s, v_s = adamw(w_pack, _pack([g_small[n] for n in small_names], loc_rows, F32), m_pack, v_pack, name="adamw_small")
    for n, d, nm, nv in zip(small_names, _unpack(d_s, loc_shapes), _unpack(m_s, loc_shapes), _unpack(v_s, loc_shapes)):
        result[n] = (g_small[n], d, nm, nv)

    outs = [[result[n][k].reshape(given[n].shape) for n in WEIGHTS] for k in range(4)]
    return (loss, dx[None], *outs[0], *outs[1], *outs[2], *outs[3])
```

```python
import functools

import numpy as np
import jax
import jax.numpy as jnp
from jax import lax
from jax.experimental import pallas as pl
from jax.experimental.pallas import tpu as pltpu
from jax.experimental.pallas import tpu_sc as plsc

F32 = jnp.float32
BF16 = jnp.bfloat16
HI = lax.Precision.HIGHEST
MESH = pl.DeviceIdType.MESH

SEQ = 2048
D_MODEL = 1024
N_HEADS = 4
HEAD = 128
RET_CHUNK = 128
RET_CHUNKS_PER_STEP = 2
GDN_CHUNK = 64
GDN_CHUNKS_PER_STEP = 4
GROUP = N_HEADS * HEAD
MIX_MAIN = 8 * GROUP
D_FF = 2816
LRU_BLOCKS = 8
LRU_C = 8.0
ROPE_BASE = 10000.0
EPS = 1e-6
N_SHARD = 4
LANES = 128

ADAM_LR, ADAM_B1, ADAM_B2, ADAM_EPS, ADAM_WD, ADAM_STEP = 0.001, 0.9, 0.999, 1e-08, 0.01, 10

VMEM_LIMIT_BYTES = 56 * 1024 * 1024

_roll = pltpu.roll


def _params(**kw):
    return pltpu.CompilerParams(vmem_limit_bytes=VMEM_LIMIT_BYTES, **kw)


def _sds(shape, dtype):
    return jax.ShapeDtypeStruct(tuple(shape), dtype)


def _shift_raw(x, d):
    n = x.shape[0]
    t = lax.broadcasted_iota(jnp.int32, x.shape, 0)
    if d > 0:
        return jnp.where(t >= d, _roll(x, d, 0), 0.0)
    return jnp.where(t < n + d, _roll(x, n + d, 0), 0.0)


@functools.partial(jax.custom_vjp, nondiff_argnums=(1,))
def shift_rows(x, d):
    return _shift_raw(x, d)


def _shift_fwd(x, d):
    return _shift_raw(x, d), None


def _shift_bwd(d, _, g):
    return (_shift_raw(g, -d),)


shift_rows.defvjp(_shift_fwd, _shift_bwd)


@jax.custom_vjp
def swap_halves(x):
    return _roll(x, HEAD // 2, 1)


def _swap_fwd(x):
    return _roll(x, HEAD // 2, 1), None


def _swap_bwd(_, g):
    return (_roll(g, HEAD // 2, 1),)


swap_halves.defvjp(_swap_fwd, _swap_bwd)


SCAN_BLOCK_ROWS = 64


def _scan_block(a, u, reverse):
    n = a.shape[0]
    t = lax.broadcasted_iota(jnp.int32, a.shape, 0)
    d = 1
    while d < n:
        if reverse:
            m = t < n - d
            a_s, u_s = _roll(a, n - d, 0), _roll(u, n - d, 0)
        else:
            m = t >= d
            a_s, u_s = _roll(a, d, 0), _roll(u, d, 0)
        u = a * jnp.where(m, u_s, 0.0) + u
        a = a * jnp.where(m, a_s, 1.0)
        d *= 2
    return a, u


def _scan_raw(a, u, reverse):
    n = a.shape[0]
    blocks = range(n // SCAN_BLOCK_ROWS)
    out = [None] * len(blocks)
    entering = None
    for b in (reversed(blocks) if reverse else blocks):
        rows = slice(b * SCAN_BLOCK_ROWS, (b + 1) * SCAN_BLOCK_ROWS)
        a_run, h = _scan_block(a[rows], u[rows], reverse)
        if entering is not None:
            h = a_run * entering + h
        out[b] = h
        entering = h[:1] if reverse else h[SCAN_BLOCK_ROWS - 1:]
    return jnp.concatenate(out, axis=0)


@jax.custom_vjp
def lin_scan(a, u):
    return _scan_raw(a, u, False)


def _lin_scan_fwd(a, u):
    hs = _scan_raw(a, u, False)
    return hs, (a, hs)


def _lin_scan_bwd(res, g):
    a, hs = res
    lam = _scan_raw(_shift_raw(a, -1), g, True)
    return lam * _shift_raw(hs, 1), lam


lin_scan.defvjp(_lin_scan_fwd, _lin_scan_bwd)


def _bdot(a, b, dims=(((1,), (0,)), ((), ()))):
    return lax.dot_general(a.astype(BF16), b.astype(BF16), dims, preferred_element_type=F32)


def _each(f, *seqs):
    return tuple(f(*a) for a in zip(*seqs))


def _split_bf16(a):
    hi = a.astype(BF16)
    return hi, (a - hi.astype(F32)).astype(BF16)


def _dot3_raw(a_s, b_s):
    a_hl = _each(_split_bf16, a_s)
    b_hl = _each(_split_bf16, b_s)
    hh = _each(lambda a, b: _bdot(a[0], b[0]), a_hl, b_hl)
    hl = _each(lambda a, b: _bdot(a[0], b[1]), a_hl, b_hl)
    lh = _each(lambda a, b: _bdot(a[1], b[0]), a_hl, b_hl)
    return _each(lambda x, y, z: x + (y + z), hh, hl, lh)


@jax.custom_vjp
def dot3(a_s, b_s):
    return _dot3_raw(a_s, b_s)


def _dot3_fwd(a_s, b_s):
    return _dot3_raw(a_s, b_s), (a_s, b_s)


def _dot3_bwd(res, g_s):
    a_s, b_s = res
    return (_each(lambda g, b: _bdot(g, b, (((1,), (1,)), ((), ()))), g_s, b_s),
            _each(lambda a, g: _bdot(a, g, (((0,), (0,)), ((), ()))), a_s, g_s))


dot3.defvjp(_dot3_fwd, _dot3_bwd)


def _eye(n):
    i = lax.broadcasted_iota(jnp.int32, (n, n), 0)
    j = lax.broadcasted_iota(jnp.int32, (n, n), 1)
    return (i == j).astype(F32)


def _unit_lower_inverse_raw(lmats):
    n = lmats[0].shape[0]
    eye = _eye(n)
    ps = _each(lambda l: -l, lmats)
    invs = _each(lambda x: eye + x, ps)
    k = 1
    while 2 * k < n:
        ps = _each(lambda p: _bdot(p, p), ps)
        invs = _each(lambda inv, p: inv + _bdot(inv, p), invs, ps)
        k *= 2
    prods = _dot3_raw(lmats, invs)
    resids = _each(lambda inv, pr: eye - inv - pr, invs, prods)
    return _each(lambda inv, r: inv + _bdot(inv, r), invs, resids)


@jax.custom_vjp
def unit_lower_inverse(lmats):
    return _unit_lower_inverse_raw(lmats)


def _uli_fwd(lmats):
    invs = _unit_lower_inverse_raw(lmats)
    return invs, invs


def _uli_bwd(invs, g_s):
    ms = _each(lambda inv, g: _bdot(inv, g, (((0,), (0,)), ((), ()))), invs, g_s)
    return (_each(lambda m, inv: -_bdot(m, inv, (((1,), (1,)), ((), ()))), ms, invs),)


unit_lower_inverse.defvjp(_uli_fwd, _uli_bwd)


def _cumsum_raw(x, reverse):
    n = x.shape[0]
    t = lax.broadcasted_iota(jnp.int32, x.shape, 0)
    d = 1
    while d < n:
        if reverse:
            x = x + jnp.where(t < n - d, _roll(x, n - d, 0), 0.0)
        else:
            x = x + jnp.where(t >= d, _roll(x, d, 0), 0.0)
        d *= 2
    return x


@jax.custom_vjp
def cumsum_rows(x):
    return _cumsum_raw(x, False)


def _cumsum_fwd(x):
    return _cumsum_raw(x, False), None


def _cumsum_bwd(_, g):
    return (_cumsum_raw(g, True),)


cumsum_rows.defvjp(_cumsum_fwd, _cumsum_bwd)


_NT = (((1,), (1,)), ((), ()))
_TN = (((0,), (0,)), ((), ()))


def _softplus(x):
    return jnp.maximum(x, 0.0) + jnp.log1p(jnp.exp(-jnp.abs(x)))


def _expm1_nonpos(x):
    poly = x * (1.0 + x * (0.5 + x * (1.0 / 6 + x * (1.0 / 24 + x * (1.0 / 120 + x * (1.0 / 720))))))
    return jnp.where(x > -0.25, poly, jnp.exp(x) - 1.0)


def _rms(x):
    return x * lax.rsqrt(jnp.mean(x * x, axis=-1, keepdims=True) + EPS)


def _causal_conv(x, w, width):
    y = w[width - 1:width, :] * x
    for j in range(width - 1):
        y = y + w[j:j + 1, :] * shift_rows(x, width - 1 - j)
    return y


def _norm_fn(x, g):
    return _rms(x) * g


def _ffn_act_fn(ug, uv, wg, wv, bg, bv):
    return jax.nn.silu(_causal_conv(ug, wg, 3) + bg) * (_causal_conv(uv, wv, 3) + bv)


def _gdn_conv_fn(x, w):
    return jax.nn.silu(_causal_conv(x, w, 4))


def _lru_fn(gate, x, cw, cb, wa, ba, wx, bx, lam):
    xr = _causal_conv(x, cw, 4) + cb
    r = jax.nn.sigmoid(_bdot(xr, wa) + ba)
    i = jax.nn.sigmoid(_bdot(xr, wx) + bx)
    log_a = -LRU_C * r * _softplus(-lam)
    a = jnp.exp(log_a)
    u = jnp.sqrt(-_expm1_nonpos(2.0 * log_a)) * (i * xr)
    hs = lin_scan(a, u)
    return jax.nn.gelu(gate) * hs


def _ret_fn(qs, ks, vs, gates, states, cos2, sin2, dmasks, ktails, qdecs, cdecs):
    c = RET_CHUNK
    n_heads = len(qs)
    n_chunks = qs[0].shape[0] // c
    units = tuple((ci, h) for ci in range(n_chunks) for h in range(n_heads))

    def rows(x, ci):
        return x[ci * c:(ci + 1) * c]

    qrs = tuple(rows(qs[h], ci) * rows(cos2, ci) + swap_halves(rows(qs[h], ci)) * rows(sin2, ci) for ci, h in units)
    krs = tuple((rows(ks[h], ci) * rows(cos2, ci) + swap_halves(rows(ks[h], ci)) * rows(sin2, ci)) * (HEAD ** -0.5) for ci, h in units)
    vus = tuple(rows(vs[h], ci) for ci, h in units)
    scores = tuple(_bdot(q, k, _NT) * dmasks[h] for q, k, (_, h) in zip(qrs, krs, units))
    intra = _each(lambda sc, v: _bdot(sc, v), scores, vus)
    outs = []
    for ci in range(n_chunks):
        mine = slice(ci * n_heads, (ci + 1) * n_heads)
        inter = _each(lambda q, d, s: _bdot(q * d, s), qrs[mine], qdecs, states)
        outs.append(_each(lambda a, b: a + b, intra[mine], inter))
        states = _each(lambda s, cd, k, kt, v: s * cd + _bdot(k * kt, v, _TN), states, cdecs, krs[mine], ktails, vus[mine])
    ys = tuple(_rms(jnp.concatenate([outs[ci][h] for ci in range(n_chunks)], axis=0)) * jax.nn.silu(gates[h]) for h in range(n_heads))
    return ys, states


def _pick_lane(x, lane_idx):
    lane = lax.broadcasted_iota(jnp.int32, x.shape, 1)
    return jnp.sum(jnp.where(lane == lane_idx, x, 0.0), axis=1, keepdims=True)


def _l2norm(x):
    return x * lax.rsqrt(jnp.sum(x * x, axis=-1, keepdims=True) + EPS)


def _gdn_fn(qcs, kcs, vcs, gates, small, a_log, dt_bias, gain, states):
    c = GDN_CHUNK
    n_heads = len(qcs)
    n_chunks = qcs[0].shape[0] // c
    units = tuple((ci, h) for ci in range(n_chunks) for h in range(n_heads))

    def unit_rows(per_head):
        return tuple(per_head[h][ci * c:(ci + 1) * c] for ci, h in units)

    smalls = tuple(small[ci * c:(ci + 1) * c] for ci, _ in units)
    heads = tuple(h for _, h in units)
    intra = _gdn_intra(unit_rows(qcs), unit_rows(kcs), unit_rows(vcs), smalls, heads, a_log, dt_bias)
    outs = []
    for ci in range(n_chunks):
        mine = slice(ci * n_heads, (ci + 1) * n_heads)
        os_, states = _gdn_inter(*(part[mine] for part in intra), states)
        outs.append(os_)
    ys = tuple(_rms(jnp.concatenate([outs[ci][h] for ci in range(n_chunks)], axis=0)) * gain * jax.nn.silu(gates[h])
               for h in range(n_heads))
    return ys, states


def _gdn_inter(qs, ks, us, ws, attns, gcs, g_lasts, states):
    v_news = _each(lambda u, w, s: u - _bdot(w, s), us, ws, states)
    inter = _each(lambda q, gc, s: _bdot(q * jnp.exp(gc), s), qs, gcs, states)
    os_ = _each(lambda x, a, v: x + _bdot(a, v), inter, attns, v_news)
    new_states = _each(lambda s, gl, k, gc, v: s * jnp.exp(gl) + _bdot(k * jnp.exp(gl - gc), v, _TN), states, g_lasts, ks, gcs, v_news)
    return os_, new_states


def _gdn_intra(qcs, kcs, vcs, smalls, heads, a_log, dt_bias):
    c = GDN_CHUNK
    qs = _each(lambda x: _l2norm(x) * (HEAD ** -0.5), qcs)
    ks = _each(_l2norm, kcs)
    betas = _each(lambda sm, h: jax.nn.sigmoid(_pick_lane(sm, h)), smalls, heads)
    gs = _each(lambda sm, h: -jnp.exp(_pick_lane(a_log, h)) * _softplus(_pick_lane(sm, h + N_HEADS) + _pick_lane(dt_bias, h)),
               smalls, heads)
    i = lax.broadcasted_iota(jnp.int32, (c, c), 0)
    j = lax.broadcasted_iota(jnp.int32, (c, c), 1)
    tril = i >= j
    gcs = _each(lambda g: cumsum_rows(jnp.broadcast_to(g, (c, LANES)))[:, :1], gs)
    gc_rows = _each(lambda gc: jnp.broadcast_to(gc, (c, c)), gcs)
    decays = _each(lambda r: jnp.where(tril, jnp.exp(jnp.where(tril, r - r.T, 0.0)), 0.0), gc_rows)
    kbs = _each(lambda k, b: k * b, ks, betas)
    lmats = _each(lambda kb, k, d: jnp.where(i > j, _bdot(kb, k, _NT) * d, 0.0), kbs, ks, decays)
    attns = _each(lambda q, k, d: jnp.where(tril, _bdot(q, k, _NT) * d, 0.0), qs, ks, decays)
    invs = unit_lower_inverse(lmats)
    us = dot3(invs, _each(lambda v, b: v * b, vcs, betas))
    ws = dot3(invs, _each(lambda kb, gc: kb * jnp.exp(gc), kbs, gcs))
    g_lasts = _each(lambda g: jnp.sum(g, axis=0, keepdims=True), gs)
    return qs, ks, us, ws, attns, gcs, g_lasts


def _final_fn(h, g, target):
    y = _rms(h) * g
    return 0.5 * jnp.sum(jnp.mean(jnp.square(y - target), axis=-1, keepdims=True), axis=0, keepdims=True)


def _tile(n, candidates):
    for t in candidates:
        if n % t == 0:
            return t
    raise ValueError(f"no tile for {n}")


MATMUL_RESIDENT_LHS_BYTES = 4 * 1024 * 1024


def matmul(a, b, *, ta=False, tb=False, add=None, out_dtype=F32, tm=None, tn=None, split=None, layer=None, name):
    m = a.shape[1] if ta else a.shape[0]
    k = a.shape[0] if ta else a.shape[1]
    n = b.shape[0] if tb else b.shape[1]
    assert k == (b.shape[1] if tb else b.shape[0])
    out_shape, out_block, out_index = (m, n), None, lambda i, j: (i, j)
    if split is not None:
        dims4, perm = split
        out_shape = tuple(dims4[p] for p in perm)
        r, cols = out_shape[2:]
        tm, tn = m, tn or _tile(cols, (1408, 512))
        cb = cols // tn
        if perm == (0, 2, 1, 3):
            out_block, out_index = (2, None, r, tn), lambda i, j: (0, j // cb, 0, j % cb)
        elif perm == (1, 0, 2, 3):
            out_block, out_index = (2, N_SHARD, r, tn), lambda i, j: (0, 0, 0, j)
        else:
            raise ValueError(perm)
    if tm is None and not ta and m * k * a.dtype.itemsize <= MATMUL_RESIDENT_LHS_BYTES:
        tm = m
    tm = tm or _tile(m, (1024, 512, 1408, 256, 128))
    tn = tn or _tile(n, (512, 1408, 256, 128))
    aliases, prev = {}, None
    if layer is not None:
        index, count, prev = layer
        out_shape, out_block, out_index = (count, m, n), (None, tm, tn), lambda i, j: (index, i, j)
    dims = (((0 if ta else 1,), (1 if tb else 0,)), ((), ()))

    def body(a_ref, b_ref, *rest):
        acc = lax.dot_general(a_ref[...].astype(BF16), b_ref[...].astype(BF16), dims, preferred_element_type=F32)
        if add is not None:
            acc = acc + rest[0][...]
        o_ref = rest[-1]
        acc = acc.astype(out_dtype)
        if split is not None and split[1] == (1, 0, 2, 3):
            rows = o_ref.shape[2]
            for s in range(N_SHARD):
                for h in range(2):
                    o_ref[h, s] = acc[(2 * s + h) * rows:(2 * s + h + 1) * rows]
        else:
            o_ref[...] = acc.reshape(o_ref.shape)

    a_spec = pl.BlockSpec((k, tm), lambda i, j: (0, i)) if ta else pl.BlockSpec((tm, k), lambda i, j: (i, 0))
    b_spec = pl.BlockSpec((tn, k), lambda i, j: (j, 0)) if tb else pl.BlockSpec((k, tn), lambda i, j: (0, j))
    o_spec = pl.BlockSpec(out_block or (tm, tn), out_index)
    in_specs, args = [a_spec, b_spec], [a, b]
    if add is not None:
        in_specs.append(o_spec)
        args.append(add)
    if prev is not None:
        aliases = {len(args): 0}
        in_specs.append(pl.BlockSpec(memory_space=pl.ANY))
        args.append(prev)
    return pl.pallas_call(body, out_shape=_sds(out_shape, out_dtype), grid=(m // tm, n // tn), in_specs=in_specs,
                          out_specs=o_spec, input_output_aliases=aliases, compiler_params=_params(), name=name)(*args)


ROW_TILE = 256


def norm_fwd(x, g, *, name):
    t, d = x.shape

    def body(x_ref, g_ref, o_ref):
        o_ref[...] = _norm_fn(x_ref[...], g_ref[...]).astype(BF16)

    return pl.pallas_call(body, out_shape=_sds((t, d), BF16), grid=(t // ROW_TILE,),
                          in_specs=[pl.BlockSpec((ROW_TILE, d), lambda i: (i, 0)), pl.BlockSpec((1, d), lambda i: (0, 0))],
                          out_specs=pl.BlockSpec((ROW_TILE, d), lambda i: (i, 0)), compiler_params=_params(), name=name)(x, g)


def norm_bwd(x, g, dy, dres, *, name):
    t, d = x.shape

    def body(x_ref, g_ref, dy_ref, dres_ref, dx_ref, dg_ref):
        _, vjp = jax.vjp(_norm_fn, x_ref[...], g_ref[...])
        dx, dg = vjp(dy_ref[...])
        dx_ref[...] = dx + dres_ref[...]

        @pl.when(pl.program_id(0) == 0)
        def _():
            dg_ref[...] = jnp.zeros_like(dg_ref)

        dg_ref[...] += dg

    row = pl.BlockSpec((ROW_TILE, d), lambda i: (i, 0))
    vec = pl.BlockSpec((1, d), lambda i: (0, 0))
    return pl.pallas_call(body, out_shape=(_sds((t, d), F32), _sds((1, d), F32)), grid=(t // ROW_TILE,),
                          in_specs=[row, vec, row, row], out_specs=(row, vec), compiler_params=_params(), name=name)(x, g, dy, dres)


def final_fwd_bwd(h, g, target, *, name):
    t, d = h.shape

    def body(h_ref, g_ref, t_ref, loss_ref, dh_ref, dg_ref):
        tgt = t_ref[...]
        loss, vjp = jax.vjp(lambda hh, gg: _final_fn(hh, gg, tgt), h_ref[...], g_ref[...])
        dh, dg = vjp(jnp.ones((1, 1), F32))
        dh_ref[...] = dh

        @pl.when(pl.program_id(0) == 0)
        def _():
            dg_ref[...] = jnp.zeros_like(dg_ref)
            loss_ref[...] = jnp.zeros_like(loss_ref)

        dg_ref[...] += dg
        loss_ref[...] += jnp.broadcast_to(loss, loss_ref.shape)

    row = pl.BlockSpec((ROW_TILE, d), lambda i: (i, 0))
    vec = pl.BlockSpec((1, d), lambda i: (0, 0))
    return pl.pallas_call(body, out_shape=(_sds((1, LANES), F32), _sds((t, d), F32), _sds((1, d), F32)), grid=(t // ROW_TILE,),
                          in_specs=[row, vec, row], out_specs=(pl.BlockSpec((1, LANES), lambda i: (0, 0)), row, vec),
                          compiler_params=_params(), name=name)(h, g, target)


FFN_FWD_COLS = 256
FFN_BWD_COLS = 128


def ffn_act_fwd(u, cw, cb, *, name):
    t = u.shape[0]
    w = FFN_FWD_COLS
    nb = D_FF // w

    def body(ug_ref, uv_ref, wg_ref, wv_ref, bg_ref, bv_ref, o_ref):
        o_ref[...] = _ffn_act_fn(ug_ref[...], uv_ref[...], wg_ref[...], wv_ref[...], bg_ref[...], bv_ref[...]).astype(BF16)

    def col(rows, off):
        return pl.BlockSpec((rows, w), lambda j: (0, j + off))

    return pl.pallas_call(body, out_shape=_sds((t, D_FF), BF16), grid=(nb,),
                          in_specs=[col(t, 0), col(t, nb), col(3, 0), col(3, nb), col(1, 0), col(1, nb)],
                          out_specs=col(t, 0), compiler_params=_params(), name=name)(u, u, cw, cw, cb, cb)


def _put_column_blocks(step, n_steps, blocks, dst_ref, width, stage_ref, sems):
    def copies(at):
        slot = at % 2
        return [pltpu.make_async_copy(stage_ref.at[slot, p], dst_ref.at[:, pl.ds(pl.multiple_of((p * n_steps + at) * width, LANES), width)],
                                      sems.at[slot, p]) for p in range(len(blocks))]

    @pl.when(step >= 2)
    def _():
        for cp in copies(step - 2):
            cp.wait()

    for p, value in enumerate(blocks):
        stage_ref[step % 2, p] = value
    for cp in copies(step):
        cp.start()

    @pl.when(step == n_steps - 1)
    def _():
        for cp in copies(step - 1) + copies(step):
            cp.wait()


def ffn_act_bwd(u, cw, cb, da, *, name):
    t = u.shape[0]
    w = FFN_BWD_COLS
    nb = D_FF // w

    def body(ug_ref, uv_ref, wg_ref, wv_ref, bg_ref, bv_ref, da_ref, dug_ref, duv_ref, dwg_ref, dwv_ref, dbg_ref, dbv_ref):
        _, vjp = jax.vjp(_ffn_act_fn, ug_ref[...], uv_ref[...], wg_ref[...], wv_ref[...], bg_ref[...], bv_ref[...])
        dug, duv, dwg, dwv, dbg, dbv = vjp(da_ref[...])
        dug_ref[...] = dug.astype(BF16)
        duv_ref[...] = duv.astype(BF16)
        dwg_ref[...] = dwg
        dwv_ref[...] = dwv
        dbg_ref[...] = dbg
        dbv_ref[...] = dbv

    def col(rows, off):
        return pl.BlockSpec((rows, w), lambda j: (0, j + off))

    outs = pl.pallas_call(
        body, out_shape=(_sds((t, D_FF), BF16), _sds((t, D_FF), BF16), _sds((3, D_FF), F32), _sds((3, D_FF), F32),
                         _sds((1, D_FF), F32), _sds((1, D_FF), F32)),
        grid=(nb,), in_specs=[col(t, 0), col(t, nb), col(3, 0), col(3, nb), col(1, 0), col(1, nb), col(t, 0)],
        out_specs=(col(t, 0), col(t, 0), col(3, 0), col(3, 0), col(1, 0), col(1, 0)), compiler_params=_params(), name=name,
    )(u, u, cw, cw, cb, cb, da)
    dug, duv, dwg, dwv, dbg, dbv = outs
    return jnp.concatenate([dug, duv], axis=1), jnp.concatenate([dwg, dwv], axis=1), jnp.concatenate([dbg, dbv], axis=1)


GDN_CONV_COLS = 256
GDN_CONV_OFF = 4 * GROUP


def gdn_conv_fwd(p, cw, *, name):
    t = p.shape[0]
    w = GDN_CONV_COLS
    nb = 3 * GROUP // w
    off = GDN_CONV_OFF // w

    def body(x_ref, w_ref, o_ref):
        o_ref[...] = _gdn_conv_fn(x_ref[...], w_ref[...])

    return pl.pallas_call(body, out_shape=_sds((t, 3 * GROUP), F32), grid=(nb,),
                          in_specs=[pl.BlockSpec((t, w), lambda j: (0, j + off)), pl.BlockSpec((4, w), lambda j: (0, j))],
                          out_specs=pl.BlockSpec((t, w), lambda j: (0, j)), compiler_params=_params(), name=name)(p, cw)


def gdn_conv_bwd(p, cw, dc, *, name):
    t = p.shape[0]
    w = GDN_CONV_COLS
    nb = 3 * GROUP // w
    off = GDN_CONV_OFF // w

    def body(x_ref, w_ref, dc_ref, dx_ref, dw_ref):
        _, vjp = jax.vjp(_gdn_conv_fn, x_ref[...], w_ref[...])
        dx, dw = vjp(dc_ref[...])
        dx_ref[...] = dx.astype(BF16)
        dw_ref[...] = dw

    blk = pl.BlockSpec((t, w), lambda j: (0, j))
    wblk = pl.BlockSpec((4, w), lambda j: (0, j))
    return pl.pallas_call(body, out_shape=(_sds((t, 3 * GROUP), BF16), _sds((4, 3 * GROUP), F32)), grid=(nb,),
                          in_specs=[pl.BlockSpec((t, w), lambda j: (0, j + off)), wblk, blk], out_specs=(blk, wblk),
                          compiler_params=_params(), name=name)(p, cw, dc)


def _lru_specs(t):
    w = D_MODEL // LRU_BLOCKS
    gate = pl.BlockSpec((t, w), lambda j: (0, j))
    xin = pl.BlockSpec((t, w), lambda j: (0, j + LRU_BLOCKS))
    cw = pl.BlockSpec((4, w), lambda j: (0, j))
    vec = pl.BlockSpec((1, w), lambda j: (0, j))
    mat = pl.BlockSpec((None, w, w), lambda j: (j, 0, 0))
    return gate, xin, cw, vec, mat


def lru_fwd(gx, cw, cb, wa, ba, wx, bx, lam, *, name):
    t = gx.shape[0]
    gate, xin, cws, vec, mat = _lru_specs(t)

    def body(g_ref, x_ref, cw_ref, cb_ref, wa_ref, ba_ref, wx_ref, bx_ref, lam_ref, o_ref):
        o_ref[...] = _lru_fn(g_ref[...], x_ref[...], cw_ref[...], cb_ref[...], wa_ref[...], ba_ref[...], wx_ref[...],
                             bx_ref[...], lam_ref[...]).astype(BF16)

    return pl.pallas_call(body, out_shape=_sds((t, D_MODEL), BF16), grid=(LRU_BLOCKS,),
                          in_specs=[gate, xin, cws, vec, mat, vec, mat, vec, vec], out_specs=gate,
                          compiler_params=_params(), name=name)(gx, gx, cw, cb, wa, ba, wx, bx, lam)


def lru_bwd(gx, cw, cb, wa, ba, wx, bx, lam, dy, *, name):
    t = gx.shape[0]
    gate, xin, cws, vec, mat = _lru_specs(t)

    def body(g_ref, x_ref, cw_ref, cb_ref, wa_ref, ba_ref, wx_ref, bx_ref, lam_ref, dy_ref,
             dgx_ref, dcw_ref, dcb_ref, dwa_ref, dba_ref, dwx_ref, dbx_ref, dlam_ref, stage_ref, sems):
        _, vjp = jax.vjp(_lru_fn, g_ref[...], x_ref[...], cw_ref[...], cb_ref[...], wa_ref[...], ba_ref[...], wx_ref[...],
                         bx_ref[...], lam_ref[...])
        dg, dx, dcw, dcb, dwa, dba, dwx, dbx, dlam = vjp(dy_ref[...])
        _put_column_blocks(pl.program_id(0), LRU_BLOCKS, (dg.astype(BF16), dx.astype(BF16)), dgx_ref, D_MODEL // LRU_BLOCKS, stage_ref, sems)
        dcw_ref[...] = dcw
        dcb_ref[...] = dcb
        dwa_ref[...] = dwa
        dba_ref[...] = dba
        dwx_ref[...] = dwx
        dbx_ref[...] = dbx
        dlam_ref[...] = dlam

    d = D_MODEL
    w = d // LRU_BLOCKS
    out_shape = (_sds((t, 2 * d), BF16), _sds((4, d), F32), _sds((1, d), F32), _sds((LRU_BLOCKS, w, w), F32),
                 _sds((1, d), F32), _sds((LRU_BLOCKS, w, w), F32), _sds((1, d), F32), _sds((1, d), F32))
    return pl.pallas_call(body, out_shape=out_shape, grid=(LRU_BLOCKS,),
                          in_specs=[gate, xin, cws, vec, mat, vec, mat, vec, vec, gate],
                          out_specs=(pl.BlockSpec(memory_space=pl.ANY), cws, vec, mat, vec, mat, vec, vec),
                          scratch_shapes=[pltpu.VMEM((2, 2, t, w), BF16), pltpu.SemaphoreType.DMA((2, 2))],
                          compiler_params=_params(), name=name)(gx, gx, cw, cb, wa, ba, wx, bx, lam, dy)


def _ret_tables():
    half = HEAD // 2
    inv_freq = (np.float32(ROPE_BASE) ** (-np.arange(half, dtype=np.float32) / np.float32(half))).astype(np.float32)
    ang = (np.arange(SEQ, dtype=np.float32)[:, None] * inv_freq[None, :]).astype(np.float64)
    cos2 = np.concatenate([np.cos(ang), np.cos(ang)], axis=1).astype(np.float32)
    sin2 = np.concatenate([-np.sin(ang), np.sin(ang)], axis=1).astype(np.float32)
    c = RET_CHUNK
    log_gamma = np.log1p(-np.exp2(-5.0 - np.arange(N_HEADS, dtype=np.float64)))
    idx = np.arange(c, dtype=np.float64)
    rel = idx[:, None] - idx[None, :]
    dmask = np.where(rel >= 0, np.exp(log_gamma[:, None, None] * np.maximum(rel, 0.0)), 0.0)
    ones = np.ones((N_HEADS, c, HEAD))
    ktail = np.exp(log_gamma[:, None] * (c - 1 - idx))[:, :, None] * ones
    qdec = np.exp(log_gamma[:, None] * (idx + 1.0))[:, :, None] * ones
    cdec = np.exp(log_gamma * c)[:, None, None] * ones
    return tuple(jnp.asarray(a, F32) for a in (cos2, sin2, dmask, ktail, qdec, cdec))


def _ret_specs(rev):
    c = RET_CHUNK * RET_CHUNKS_PER_STEP
    nc = SEQ // c

    def n_of(n):
        return nc - 1 - n if rev else n

    def group(off):
        return pl.BlockSpec((c, GROUP), lambda n: (n_of(n), off))

    tab = pl.BlockSpec((c, HEAD), lambda n: (n_of(n), 0))
    const = pl.BlockSpec((N_HEADS, RET_CHUNK, HEAD), lambda n: (0, 0, 0))
    state = pl.BlockSpec((N_HEADS, None, HEAD, HEAD), lambda n: (0, n_of(n), 0, 0))
    return group, tab, const, state, nc


def _head(h):
    return slice(h * HEAD, (h + 1) * HEAD)


def ret_fwd(p, tables, *, name):
    group, tab, const, state, nc = _ret_specs(False)

    def body(q_ref, k_ref, v_ref, g_ref, cos_ref, sin_ref, dm_ref, kt_ref, qd_ref, cd_ref, y_ref, st_ref, s_scr):
        @pl.when(pl.program_id(0) == 0)
        def _():
            s_scr[...] = jnp.zeros_like(s_scr)

        heads = range(N_HEADS)
        states = tuple(s_scr[h] for h in heads)
        ys, new_states = _ret_fn(*(tuple(r[:, _head(h)] for h in heads) for r in (q_ref, k_ref, v_ref, g_ref)), states,
                                 cos_ref[...], sin_ref[...], *(tuple(r[h] for h in heads) for r in (dm_ref, kt_ref, qd_ref, cd_ref)))
        for h in heads:
            st_ref[h] = states[h]
            y_ref[:, _head(h)] = ys[h].astype(BF16)
            s_scr[h] = new_states[h]

    return pl.pallas_call(
        body, out_shape=(_sds((SEQ, GROUP), BF16), _sds((N_HEADS, nc, HEAD, HEAD), F32)), grid=(nc,),
        in_specs=[group(0), group(1), group(2), group(3), tab, tab, const, const, const, const],
        out_specs=(group(0), state), scratch_shapes=[pltpu.VMEM((N_HEADS, HEAD, HEAD), F32)], compiler_params=_params(), name=name,
    )(p, p, p, p, *tables)


def ret_bwd(p, tables, states, dy, *, name):
    group, tab, const, state, nc = _ret_specs(True)

    def body(q_ref, k_ref, v_ref, g_ref, cos_ref, sin_ref, dm_ref, kt_ref, qd_ref, cd_ref, st_ref, dy_ref,
             dq_ref, dk_ref, dv_ref, dg_ref, ds_scr):
        @pl.when(pl.program_id(0) == 0)
        def _():
            ds_scr[...] = jnp.zeros_like(ds_scr)

        heads = range(N_HEADS)
        consts = (cos_ref[...], sin_ref[...], *(tuple(r[h] for h in heads) for r in (dm_ref, kt_ref, qd_ref, cd_ref)))
        _, vjp = jax.vjp(lambda *a: _ret_fn(*a, *consts), *(tuple(r[:, _head(h)] for h in heads) for r in (q_ref, k_ref, v_ref, g_ref)),
                         tuple(st_ref[h] for h in heads))
        dqs, dks, dvs, dgs, dss = vjp((tuple(dy_ref[:, _head(h)] for h in heads), tuple(ds_scr[h] for h in heads)))
        for h in heads:
            dq_ref[:, _head(h)] = dqs[h].astype(BF16)
            dk_ref[:, _head(h)] = dks[h].astype(BF16)
            dv_ref[:, _head(h)] = dvs[h].astype(BF16)
            dg_ref[:, _head(h)] = dgs[h].astype(BF16)
            ds_scr[h] = dss[h]

    out = _sds((SEQ, GROUP), BF16)
    return pl.pallas_call(
        body, out_shape=(out, out, out, out), grid=(nc,),
        in_specs=[group(0), group(1), group(2), group(3), tab, tab, const, const, const, const, state, group(0)],
        out_specs=(group(0), group(0), group(0), group(0)), scratch_shapes=[pltpu.VMEM((N_HEADS, HEAD, HEAD), F32)],
        compiler_params=_params(), name=name,
    )(p, p, p, p, *tables, states, dy)


def _gdn_specs(rev):
    c = GDN_CHUNK * GDN_CHUNKS_PER_STEP
    nc = SEQ // c

    def n_of(n):
        return nc - 1 - n if rev else n

    def group(off):
        return pl.BlockSpec((c, GROUP), lambda n: (n_of(n), off))

    small = pl.BlockSpec((c, LANES), lambda n: (n_of(n), 0))
    vec = pl.BlockSpec((1, LANES), lambda n: (0, 0))
    state = pl.BlockSpec((N_HEADS, None, HEAD, HEAD), lambda n: (0, n_of(n), 0, 0))
    return group, small, vec, state, nc


GDN_GATE_GROUP = 7


def gdn_fwd(conv, p, small, a_log, dt_bias, gain, *, name):
    group, sm, vec, state, nc = _gdn_specs(False)

    def body(q_ref, k_ref, v_ref, g_ref, sm_ref, al_ref, dt_ref, gn_ref, y_ref, st_ref, s_scr):
        @pl.when(pl.program_id(0) == 0)
        def _():
            s_scr[...] = jnp.zeros_like(s_scr)

        states = tuple(s_scr[h] for h in range(N_HEADS))
        ys, new_states = _gdn_fn(*(tuple(r[:, _head(h)] for h in range(N_HEADS)) for r in (q_ref, k_ref, v_ref, g_ref)),
                                 sm_ref[...], al_ref[...], dt_ref[...], gn_ref[...], states)
        for h in range(N_HEADS):
            st_ref[h] = states[h]
            y_ref[:, _head(h)] = ys[h].astype(BF16)
            s_scr[h] = new_states[h]

    return pl.pallas_call(
        body, out_shape=(_sds((SEQ, GROUP), BF16), _sds((N_HEADS, nc, HEAD, HEAD), F32)), grid=(nc,),
        in_specs=[group(0), group(1), group(2), group(GDN_GATE_GROUP), sm, vec, vec, vec], out_specs=(group(0), state),
        scratch_shapes=[pltpu.VMEM((N_HEADS, HEAD, HEAD), F32)], compiler_params=_params(), name=name,
    )(conv, conv, conv, p, small, a_log, dt_bias, gain)


def gdn_bwd(conv, p, small, a_log, dt_bias, gain, states, dy, *, name):
    group, sm, vec, state, nc = _gdn_specs(True)

    def body(q_ref, k_ref, v_ref, g_ref, sm_ref, al_ref, dt_ref, gn_ref, st_ref, dy_ref,
             dq_ref, dk_ref, dv_ref, dg_ref, dsm_ref, dal_ref, ddt_ref, dgn_ref, ds_scr):
        @pl.when(pl.program_id(0) == 0)
        def _():
            ds_scr[...] = jnp.zeros_like(ds_scr)
            dal_ref[...] = jnp.zeros_like(dal_ref)
            ddt_ref[...] = jnp.zeros_like(ddt_ref)
            dgn_ref[...] = jnp.zeros_like(dgn_ref)

        per_head = tuple(tuple(r[:, _head(h)] for h in range(N_HEADS)) for r in (q_ref, k_ref, v_ref, g_ref))
        _, vjp = jax.vjp(_gdn_fn, *per_head, sm_ref[...], al_ref[...], dt_ref[...], gn_ref[...],
                         tuple(st_ref[h] for h in range(N_HEADS)))
        cts = (tuple(dy_ref[:, _head(h)] for h in range(N_HEADS)), tuple(ds_scr[h] for h in range(N_HEADS)))
        dqs, dks, dvs, dgs, dsm, dal, ddt, dgn, dss = vjp(cts)
        for h in range(N_HEADS):
            dq_ref[:, _head(h)] = dqs[h]
            dk_ref[:, _head(h)] = dks[h]
            dv_ref[:, _head(h)] = dvs[h]
            dg_ref[:, _head(h)] = dgs[h].astype(BF16)
            ds_scr[h] = dss[h]
        dsm_ref[...] = dsm
        dal_ref[...] += dal
        ddt_ref[...] += ddt
        dgn_ref[...] += dgn

    f = _sds((SEQ, GROUP), F32)
    pv = _sds((1, LANES), F32)
    return pl.pallas_call(
        body, out_shape=(f, f, f, _sds((SEQ, GROUP), BF16), _sds((SEQ, LANES), F32), pv, pv, pv), grid=(nc,),
        in_specs=[group(0), group(1), group(2), group(GDN_GATE_GROUP), sm, vec, vec, vec, state, group(1)],
        out_specs=(group(0), group(0), group(0), group(0), sm, vec, vec, vec), scratch_shapes=[pltpu.VMEM((N_HEADS, HEAD, HEAD), F32)],
        compiler_params=_params(), name=name,
    )(conv, conv, conv, p, small, a_log, dt_bias, gain, states, dy)


PACK_ROW_TILE = 1024


def adamw(w, g, m, v, *, name):
    r = w.shape[0]
    tr = _row_tile(r, LANES)

    def body(w_ref, g_ref, m_ref, v_ref, d_ref, nm_ref, nv_ref):
        gg = g_ref[...]
        nm = ADAM_B1 * m_ref[...] + (1.0 - ADAM_B1) * gg
        nv = ADAM_B2 * v_ref[...] + (1.0 - ADAM_B2) * jnp.square(gg)
        m_hat = nm / (1.0 - ADAM_B1 ** ADAM_STEP)
        v_hat = nv / (1.0 - ADAM_B2 ** ADAM_STEP)
        d_ref[...] = -ADAM_LR * (m_hat / (jnp.sqrt(v_hat) + ADAM_EPS) + ADAM_WD * w_ref[...])
        nm_ref[...] = nm
        nv_ref[...] = nv

    blk = pl.BlockSpec((tr, LANES), lambda i: (i, 0))
    o = _sds((r, LANES), F32)
    return pl.pallas_call(body, out_shape=(o, o, o), grid=(r // tr,), in_specs=[blk] * 4, out_specs=(blk, blk, blk),
                          compiler_params=_params(), name=name)(w, g, m, v)


ELEMENTWISE_BLOCK_BYTES = 2 * 1024 * 1024


def _row_tile(r, c):
    best = None
    for tr in range(8, r + 1, 8):
        if r % tr == 0 and tr * c * 4 <= ELEMENTWISE_BLOCK_BYTES:
            best = tr
    if best is None:
        raise ValueError(f"no row tile for ({r}, {c})")
    return best


def _tile_2d(r, c):
    if any(r % tr == 0 for tr in range(8, r + 1, 8)):
        return _row_tile(r, c), c
    tc = max(t for t in range(LANES, c + 1, LANES) if c % t == 0 and r * t * 4 <= ELEMENTWISE_BLOCK_BYTES)
    return r, tc


def _core_index():
    return lax.axis_index("c").astype(jnp.int32).reshape(1)


def _chip_index():
    return (2 * lax.axis_index("x") + lax.axis_index("y")).astype(jnp.int32).reshape(1)


def adamw_halves(w, m, v, g_own, g_sib, *, layer=0, prev=None, name):
    n_layers, rows, c = w.shape
    r = rows // 2
    tr = _row_tile(r, c)
    nb = r // tr

    def body(c_ref, w_ref, m_ref, v_ref, own_ref, sib_ref, *rest):
        g_ref, d_ref, nm_ref, nv_ref = rest[-4:]
        gg = jnp.where(pl.program_id(0) == c_ref[0], own_ref[...], sib_ref[...])
        nm = ADAM_B1 * m_ref[...] + (1.0 - ADAM_B1) * gg
        nv = ADAM_B2 * v_ref[...] + (1.0 - ADAM_B2) * jnp.square(gg)
        m_hat = nm / (1.0 - ADAM_B1 ** ADAM_STEP)
        v_hat = nv / (1.0 - ADAM_B2 ** ADAM_STEP)
        g_ref[...] = gg
        d_ref[...] = -ADAM_LR * (m_hat / (jnp.sqrt(v_hat) + ADAM_EPS) + ADAM_WD * w_ref[...])
        nm_ref[...] = nm
        nv_ref[...] = nv

    full = pl.BlockSpec((None, tr, c), lambda h, i, cr: (layer, h * nb + i, 0))
    half = pl.BlockSpec((tr, c), lambda h, i, cr: (i, 0))
    o = _sds((n_layers, rows, c), F32)
    prev = list(prev or ())
    gs = pltpu.PrefetchScalarGridSpec(num_scalar_prefetch=1, grid=(2, nb), in_specs=[full, full, full, half, half] + [_ANY] * len(prev),
                                      out_specs=(full, full, full, full))
    n_fixed = 6
    return pl.pallas_call(body, out_shape=(o, o, o, o), grid_spec=gs, compiler_params=_params(), name=name,
                          input_output_aliases={n_fixed + k: k for k in range(len(prev))})(
        _core_index(), w, m, v, g_own, g_sib, *prev)


ADAMW_COLUMN_TILE = 256


def adamw_column_halves(w, m, v, g_own, g_sib, *, name):
    _, rows, cols = w.shape
    tc = ADAMW_COLUMN_TILE
    per_half = cols // 2 // tc

    def body(c_ref, w_ref, m_ref, v_ref, own_ref, sib_ref, g_ref, d_ref, nm_ref, nv_ref):
        gg = jnp.where(pl.program_id(0) // per_half == c_ref[0], own_ref[...], sib_ref[...])
        nm = ADAM_B1 * m_ref[...] + (1.0 - ADAM_B1) * gg
        nv = ADAM_B2 * v_ref[...] + (1.0 - ADAM_B2) * jnp.square(gg)
        m_hat = nm / (1.0 - ADAM_B1 ** ADAM_STEP)
        v_hat = nv / (1.0 - ADAM_B2 ** ADAM_STEP)
        g_ref[...] = gg
        d_ref[...] = -ADAM_LR * (m_hat / (jnp.sqrt(v_hat) + ADAM_EPS) + ADAM_WD * w_ref[...])
        nm_ref[...] = nm
        nv_ref[...] = nv

    full = pl.BlockSpec((None, rows, tc), lambda j, cr: (0, 0, j))
    half = pl.BlockSpec((rows, tc), lambda j, cr: (0, j % per_half))
    o = _sds(w.shape, F32)
    gs = pltpu.PrefetchScalarGridSpec(num_scalar_prefetch=1, grid=(cols // tc,), in_specs=[full, full, full, half, half],
                                      out_specs=(full, full, full, full))
    return pl.pallas_call(body, out_shape=(o, o, o, o), grid_spec=gs, compiler_params=_params(), name=name)(
        _core_index(), w, m, v, g_own, g_sib)


def add_core_halves(g2, land, *, out_dtype, name):
    _, ns, r, cols = g2.shape
    tr, tc = _tile_2d(r, cols)

    def body(c_ref, a_ref, b_ref, o_ref):
        o_ref[...] = (a_ref[...] + b_ref[...]).astype(out_dtype)

    gs = pltpu.PrefetchScalarGridSpec(
        num_scalar_prefetch=1, grid=(ns, r // tr, cols // tc),
        in_specs=[pl.BlockSpec((None, None, tr, tc), lambda s, i, j, cr: (cr[0], s, i, j)),
                  pl.BlockSpec((None, tr, tc), lambda s, i, j, cr: (s, i, j))],
        out_specs=pl.BlockSpec((None, tr, tc), lambda s, i, j, cr: (s, i, j)))
    return pl.pallas_call(body, out_shape=_sds((ns, r, cols), out_dtype), grid_spec=gs, compiler_params=_params(), name=name)(
        _core_index(), g2, land)


def sum_over_chips(own, land, *, scatter, name):
    _, r, cols = own.shape
    tr, tc = _tile_2d(r, cols)

    def body(mine_ref, own_ref, l0, l1, l2, l3, o_ref):
        mine = mine_ref[0]
        mine_val = own_ref[...]
        acc = None
        for s, l_ref in enumerate((l0, l1, l2, l3)):
            val = jnp.where(mine == s, mine_val, l_ref[...]).astype(F32)
            acc = val if acc is None else acc + val
        o_ref[...] = acc

    def slot(s):
        return pl.BlockSpec((None, tr, tc), lambda i, j, mr: (jnp.where(mr[0] == s, (s + 1) % N_SHARD, s), i, j))

    own_spec = pl.BlockSpec((None, tr, tc), lambda i, j, mr: (mr[0] if scatter else 0, i, j))
    gs = pltpu.PrefetchScalarGridSpec(num_scalar_prefetch=1, grid=(r // tr, cols // tc), in_specs=[own_spec] + [slot(s) for s in range(N_SHARD)],
                                      out_specs=pl.BlockSpec((tr, tc), lambda i, j, mr: (i, j)))
    return pl.pallas_call(body, out_shape=_sds((r, cols), F32), grid_spec=gs, compiler_params=_params(), name=name)(
        _chip_index(), own, land, land, land, land)


_ANY = pl.BlockSpec(memory_space=pl.ANY)


def xy_exchange(src, *, scatter, name):
    rh = src.shape[1]

    def body(src_ref, land_ref, send_sems, recv_sems, loc_sem):
        x, y, c = lax.axis_index("x"), lax.axis_index("y"), lax.axis_index("c")
        mine = 2 * x + y
        peers = [(1 - x, y), (x, 1 - y), (1 - x, 1 - y)]

        def piece(shard):
            return src_ref.at[shard] if scatter else src_ref.at[c]

        def copy(k, px, py, dst_slot):
            return pltpu.make_async_remote_copy(src_ref=piece(2 * px + py), dst_ref=land_ref.at[dst_slot], send_sem=send_sems.at[k],
                                                recv_sem=recv_sems.at[k], device_id=(px, py, c), device_id_type=MESH)

        keep = pltpu.make_async_copy(piece(mine), land_ref.at[mine], loc_sem)
        keep.start()
        sends = [copy(k, px, py, mine) for k, (px, py) in enumerate(peers)]
        for cp in sends:
            cp.start()
        for cp in sends:
            cp.wait_send()
        for k, (px, py) in enumerate(peers):
            copy(k, px, py, 2 * px + py).wait_recv()
        keep.wait()

    return pl.pallas_call(body, out_shape=_sds((N_SHARD, rh, LANES), src.dtype), in_specs=[_ANY], out_specs=_ANY,
                          scratch_shapes=[pltpu.SemaphoreType.DMA((3,)), pltpu.SemaphoreType.DMA((3,)), pltpu.SemaphoreType.DMA(())],
                          name=name)(src)


def core_exchange(src, *, send_other_half, name):
    def body(src_ref, out_ref, send_sem, recv_sem, loc_sem):
        x, y, c = lax.axis_index("x"), lax.axis_index("y"), lax.axis_index("c")
        if send_other_half:
            cp = pltpu.make_async_remote_copy(src_ref=src_ref.at[1 - c], dst_ref=out_ref, send_sem=send_sem, recv_sem=recv_sem,
                                              device_id=(x, y, 1 - c), device_id_type=MESH)
            cp.start()
            cp.wait_send()
            cp.wait_recv()
        else:
            keep = pltpu.make_async_copy(src_ref, out_ref.at[c], loc_sem)
            keep.start()
            cp = pltpu.make_async_remote_copy(src_ref=src_ref, dst_ref=out_ref.at[c], send_sem=send_sem, recv_sem=recv_sem,
                                              device_id=(x, y, 1 - c), device_id_type=MESH)
            cp.start()
            cp.wait_send()
            pltpu.make_async_remote_copy(src_ref=src_ref, dst_ref=out_ref.at[1 - c], send_sem=send_sem, recv_sem=recv_sem,
                                         device_id=(x, y, 1 - c), device_id_type=MESH).wait_recv()
            keep.wait()

    out_shape = _sds(src.shape[1:], src.dtype) if send_other_half else _sds((2,) + src.shape, src.dtype)
    return pl.pallas_call(body, out_shape=out_shape, in_specs=[_ANY], out_specs=_ANY,
                          scratch_shapes=[pltpu.SemaphoreType.DMA(()), pltpu.SemaphoreType.DMA(()), pltpu.SemaphoreType.DMA(())],
                          name=name)(src)


def _comm_call(body, ins, out_shapes, sem_counts, name):
    return pl.pallas_call(body, out_shape=tuple(out_shapes), in_specs=[_ANY] * len(ins), out_specs=tuple([_ANY] * len(out_shapes)),
                          scratch_shapes=[pltpu.SemaphoreType.DMA((k,)) for k in sem_counts], name=name)(*ins)


def _sequencer_call(body, ins, out_shapes, sem_counts, name, collective_id):
    return pl.kernel(body, out_type=list(out_shapes), mesh=plsc.ScalarSubcoreMesh(axis_name="sequencer", num_cores=1), name=name,
                     scratch_types=[pltpu.SemaphoreType.DMA((k,)) for k in sem_counts],
                     compiler_params=pltpu.CompilerParams(collective_id=collective_id))(*ins)


def _handshake(peers):
    barrier = pltpu.get_barrier_semaphore()
    for peer in peers:
        pl.semaphore_signal(barrier, inc=1, device_id=peer, device_id_type=MESH)
    pl.semaphore_wait(barrier, len(peers))


def _xy_peers(x, y):
    return [(1 - x, y), (x, 1 - y), (1 - x, 1 - y)]


def gather_halves(halves, *, name, collective_id):
    n = len(halves)

    def body(*refs):
        ins, lands, sibs = refs[:n], refs[n:2 * n], refs[2 * n:3 * n]
        ici_send, ici_recv, d2d_send, d2d_recv = refs[3 * n:]
        x, y, c = lax.axis_index("x"), lax.axis_index("y"), lax.axis_index("c")
        mine = 2 * x + y
        peers = _xy_peers(x, y)
        _handshake([(px, py, c) for px, py in peers] + [(x, y, 1 - c)])

        def ici(i, k, slot):
            px, py = peers[k]
            return pltpu.make_async_remote_copy(src_ref=ins[i].at[c], dst_ref=lands[i].at[slot], send_sem=ici_send.at[3 * i + k],
                                                recv_sem=ici_recv.at[3 * i + k], device_id=(px, py, c), device_id_type=MESH)

        def pass_on(i, k):
            px, py = peers[k]
            slot = 2 * px + py
            return pltpu.make_async_remote_copy(src_ref=lands[i].at[slot], dst_ref=sibs[i].at[slot], send_sem=d2d_send.at[3 * i + k],
                                                recv_sem=d2d_recv.at[3 * i + k], device_id=(x, y, 1 - c), device_id_type=MESH)

        sends = [ici(i, k, mine) for i in range(n) for k in range(3)]
        for cp in sends:
            cp.start()
        passed = []
        for i in range(n):
            for k in range(3):
                px, py = peers[k]
                ici(i, k, 2 * px + py).wait_recv()
                cp = pass_on(i, k)
                cp.start()
                passed.append(cp)
        for cp in passed:
            cp.wait_recv()
        for cp in sends + passed:
            cp.wait_send()

    outs = [_sds((N_SHARD,) + h.shape[1:], h.dtype) for h in halves]
    res = _sequencer_call(body, halves, outs + outs, [3 * n] * 4, name, collective_id)
    return res[:n], res[n:]


def send_other_half(arrays, *, name, collective_id):
    n = len(arrays)

    def body(*refs):
        ins, lands = refs[:n], refs[n:2 * n]
        send_sems, recv_sems = refs[2 * n:]
        x, y, c = lax.axis_index("x"), lax.axis_index("y"), lax.axis_index("c")
        _handshake([(x, y, 1 - c)])
        copies = [pltpu.make_async_remote_copy(src_ref=ins[i].at[1 - c], dst_ref=lands[i], send_sem=send_sems.at[i],
                                               recv_sem=recv_sems.at[i], device_id=(x, y, 1 - c), device_id_type=MESH) for i in range(n)]
        for cp in copies:
            cp.start()
        for cp in copies:
            cp.wait_recv()
        for cp in copies:
            cp.wait_send()

    return _sequencer_call(body, arrays, [_sds(a.shape[1:], a.dtype) for a in arrays], [n, n], name, collective_id)


_HBM = pl.BlockSpec(memory_space=pltpu.HBM)
_SEM = pl.BlockSpec(memory_space=pltpu.SEMAPHORE)
_SPLIT_COPY = dict(has_side_effects=pltpu.SideEffectType.DATAFLOW_SIDE_EFFECTING)


def _chip_copy(ins, lands, send_sems, recv_sems, scatter, i, k, receive):
    x, y, c = lax.axis_index("x"), lax.axis_index("y"), lax.axis_index("c")
    px, py = _xy_peers(x, y)[k]
    theirs, mine = 2 * px + py, 2 * x + y
    src = ins[i].at[theirs] if scatter[i] else ins[i].at[0]
    return pltpu.make_async_remote_copy(src_ref=src, dst_ref=lands[i].at[theirs if receive else mine], send_sem=send_sems.at[3 * i + k],
                                        recv_sem=recv_sems.at[3 * i + k], device_id=(px, py, c), device_id_type=MESH)


def send_to_chips_start(arrays, scatter, *, name):
    n = len(arrays)

    def body(*refs):
        send_sems, recv_sems = refs[2 * n], refs[2 * n + 1]
        ins, lands = refs[2 * n + 2:3 * n + 2], refs[3 * n + 2:4 * n + 2]
        token = refs[4 * n + 2]
        for i in range(n):
            for k in range(3):
                _chip_copy(ins, lands, send_sems, recv_sems, scatter, i, k, receive=False).start()
        token[...] = jnp.zeros_like(token)

    land_shapes = [(N_SHARD,) + a.shape[1:] for a in arrays]
    operands = [pltpu.with_memory_space_constraint(a, pltpu.HBM) for a in arrays]
    operands += [pltpu.with_memory_space_constraint(lax.empty(s, a.dtype), pltpu.HBM) for s, a in zip(land_shapes, arrays)]
    out_shape = ([pltpu.SemaphoreType.DMA((3 * n,)), pltpu.SemaphoreType.DMA((3 * n,))] + [pltpu.HBM(a.shape, a.dtype) for a in arrays]
                 + [pltpu.HBM(s, a.dtype) for s, a in zip(land_shapes, arrays)] + [_sds((8, LANES), F32)])
    res = pl.pallas_call(body, name=name, out_shape=out_shape, in_specs=[_HBM] * (2 * n),
                         out_specs=[_SEM, _SEM] + [_HBM] * (2 * n) + [pl.BlockSpec(memory_space=pltpu.VMEM)],
                         input_output_aliases={i: 2 + i for i in range(2 * n)}, compiler_params=pltpu.CompilerParams(**_SPLIT_COPY))(*operands)
    return (res[0], res[1], res[2:2 + n], res[2 + n:2 + 2 * n], scatter), res[-1]


def send_to_chips_wait(state, after, *, name):
    send_sems, recv_sems, arrays, lands, scatter = state
    n = len(arrays)

    def body(*refs):
        ins, landing = refs[:n], refs[n:2 * n]
        send_sems, recv_sems = refs[2 * n], refs[2 * n + 1]
        for i in range(n):
            for k in range(3):
                _chip_copy(ins, landing, send_sems, recv_sems, scatter, i, k, receive=True).wait_recv()
        for i in range(n):
            for k in range(3):
                _chip_copy(ins, landing, send_sems, recv_sems, scatter, i, k, receive=False).wait_send()

    out_shape = [pltpu.HBM(a.shape, a.dtype) for a in list(arrays) + list(lands)]
    res = pl.pallas_call(body, name=name, out_shape=out_shape, in_specs=[_HBM] * (2 * n) + [_SEM, _SEM] + [_ANY] * len(after),
                         out_specs=[_HBM] * (2 * n), input_output_aliases={i: i for i in range(2 * n)},
                         compiler_params=pltpu.CompilerParams(**_SPLIT_COPY))(*arrays, *lands, send_sems, recv_sems, *after)
    return res[:n], res[n:]


def swap_with_other_core(arrays, *, name, collective_id):
    n = len(arrays)

    def body(*refs):
        ins, lands = refs[:n], refs[n:2 * n]
        send_sems, recv_sems = refs[2 * n:]
        x, y, c = lax.axis_index("x"), lax.axis_index("y"), lax.axis_index("c")
        _handshake([(x, y, 1 - c)])
        copies = [pltpu.make_async_remote_copy(src_ref=ins[i], dst_ref=lands[i], send_sem=send_sems.at[i], recv_sem=recv_sems.at[i],
                                               device_id=(x, y, 1 - c), device_id_type=MESH) for i in range(n)]
        for cp in copies:
            cp.start()
        for cp in copies:
            cp.wait_recv()
        for cp in copies:
            cp.wait_send()

    return _sequencer_call(body, arrays, [_sds(a.shape, a.dtype) for a in arrays], [n, n], name, collective_id)


def _pack_rows(n_elems, row_multiple):
    rows = -(-n_elems // LANES)
    return -(-rows // row_multiple) * row_multiple


def _pack(arrays, rows, dtype):
    flat = jnp.concatenate([a.reshape(-1).astype(dtype) for a in arrays])
    return jnp.pad(flat, (0, rows * LANES - flat.shape[0])).reshape(rows, LANES)


def _unpack(packed, shapes):
    flat = packed.reshape(-1)
    out, off = [], 0
    for s in shapes:
        n = int(np.prod(s))
        out.append(flat[off:off + n].reshape(s))
        off += n
    return out


def all_gather_shards(shards, axes, dtype, row_multiple, tag):
    shapes = [s.shape for s in shards]
    rows = _pack_rows(sum(int(np.prod(s)) for s in shapes), row_multiple)
    packed = _pack(shards, rows, dtype).reshape(2, rows // 2, LANES)
    land = xy_exchange(packed, scatter=False, name=f"gather_xy_{tag}")
    both = core_exchange(land, send_other_half=False, name=f"gather_c_{tag}")
    per_shard = jnp.swapaxes(both, 0, 1).reshape(N_SHARD, rows, LANES)
    pieces = [_unpack(per_shard[s], shapes) for s in range(N_SHARD)]
    return [jnp.concatenate([pieces[s][i] for s in range(N_SHARD)], axis=ax) for i, ax in enumerate(axes)]


def _ordered_before(first, then):
    if then is None:
        return first, None
    return lax.optimization_barrier((first, then))


def reduce_between_cores(arrays, scatter, *, tag, collective_id, before=None):
    arrays, before = _ordered_before(arrays, before)
    land = send_other_half(arrays, name=f"reduce_core_send_{tag}", collective_id=collective_id)
    return (arrays, land, scatter, tag, collective_id), before


def reduce_between_chips(state, before=None):
    arrays, land, scatter, tag, collective_id = state
    chip = [add_core_halves(a, l, out_dtype=BF16 if sc else F32, name=f"reduce_core_add_{tag}_{i}")
            for i, (a, l, sc) in enumerate(zip(arrays, land, scatter))]
    sending, token = send_to_chips_start(chip, scatter, name=f"reduce_chip_start_{tag}")
    token, before = _ordered_before(token, before)
    return (sending, token, scatter, tag, collective_id), before


def reduce_finish(state, after):
    sending, token, scatter, tag, collective_id = state
    chip, land = send_to_chips_wait(sending, tuple(after) + (token,), name=f"reduce_chip_wait_{tag}")
    own = [sum_over_chips(ch, l, scatter=sc, name=f"reduce_chip_add_{tag}_{i}") for i, (ch, l, sc) in enumerate(zip(chip, land, scatter))]
    sib = swap_with_other_core(own, name=f"reduce_core_swap_{tag}", collective_id=collective_id + 2)
    return own, sib


def _ffn_layer_fwd(h, norm_g, w_up, cw, cb, w_down, tag):
    hn = norm_fwd(h, norm_g, name=f"ffn_norm_{tag}")
    u = matmul(hn, w_up, name=f"ffn_up_{tag}")
    act = ffn_act_fwd(u, cw, cb, name=f"ffn_act_{tag}")
    out = matmul(act, w_down, add=h, name=f"ffn_down_{tag}")
    return out, (h, hn, u, act)


def _travel_layout(array):
    return BIG_ARRAYS[array][3], BIG_ARRAYS[array][4]


def _ffn_layer_bwd(saved, dout, norm_g, w_up, cw, cb, w_down, tag, d_w_down_other=None):
    h, hn, u, act = saved
    dact = matmul(dout, w_down, tb=True, name=f"ffn_down_dx_{tag}")
    d_w_down = matmul(act, dout, ta=True, layer=(int(tag), 2, d_w_down_other), name=f"ffn_down_dw_{tag}")
    du, dcw, dcb = ffn_act_bwd(u, cw, cb, dact, name=f"ffn_act_bwd_{tag}")
    dhn = matmul(du, w_up, tb=True, name=f"ffn_up_dx_{tag}")
    d_w_up = matmul(hn, du, ta=True, split=_travel_layout(f"ffn_w_up_{tag}"), name=f"ffn_up_dw_{tag}")
    dh, dg = norm_bwd(h, norm_g, dhn, dout, name=f"ffn_norm_bwd_{tag}")
    return dh, dg, d_w_up, dcw, dcb, d_w_down


def local_step(x, target, w, stage=lambda name, tensors, grads=None: tensors):
    g = {}
    tables = _ret_tables()
    x = stage("start", x)
    w_in_t = w["ret_gdn_w_in"]
    w_main = w_in_t[:MIX_MAIN]
    w_small = jnp.pad(w_in_t[MIX_MAIN:], ((0, LANES - 2 * N_HEADS), (0, 0)))
    a_log = jnp.pad(w["gdn_a_log"], ((0, 0), (0, LANES - N_HEADS)))
    dt_bias = jnp.pad(w["gdn_dt_bias"], ((0, 0), (0, LANES - N_HEADS)))

    hn0 = stage("normed", norm_fwd(x, w["norm_mix"][0:1], name="mix0_norm"))
    p = matmul(hn0, w_main, tb=True, name="mix0_in")
    small = matmul(hn0, w_small, tb=True, name="mix0_in_small")
    y_ret, s_ret = ret_fwd(p, tables, name="ret_fwd")
    conv = gdn_conv_fwd(p, w["gdn_conv_w"], name="gdn_conv")
    y_gdn, s_gdn = gdn_fwd(conv, p, small, a_log, dt_bias, w["gdn_out_gain"], name="gdn_fwd")
    y0 = stage("mixed", jnp.concatenate([y_ret, y_gdn], axis=1))
    h1 = matmul(y0, w["ret_gdn_w_out"], add=x, name="mix0_out")
    h2, ffn0 = _ffn_layer_fwd(h1, w["norm_ffn"][0:1], w["ffn_w_up"][0], w["ffn_conv_w"][0], w["ffn_conv_b"][0:1], w["ffn_w_down"][0], "0")
    h2 = stage("layer0", h2)

    hn1 = norm_fwd(h2, w["norm_mix"][1:2], name="mix1_norm")
    gx = matmul(hn1, w["lru_w_in"], name="mix1_in")
    lru_p = (w["lru_conv_w"], w["lru_conv_b"], w["lru_w_a"], w["lru_b_a"], w["lru_w_x"], w["lru_b_x"], w["lru_lambda"])
    y1 = lru_fwd(gx, *lru_p, name="lru_fwd")
    h3 = matmul(y1, w["lru_w_out"], add=h2, name="mix1_out")
    h4, ffn1 = _ffn_layer_fwd(h3, w["norm_ffn"][1:2], w["ffn_w_up"][1], w["ffn_conv_w"][1], w["ffn_conv_b"][1:2], w["ffn_w_down"][1], "1")

    loss, dh4, g["norm_final"] = final_fwd_bwd(h4, w["norm_final"], target, name="final")

    dh3, dgf1, dwu1, dcw1, dcb1, dwd1 = _ffn_layer_bwd(ffn1, dh4, w["norm_ffn"][1:2], w["ffn_w_up"][1], w["ffn_conv_w"][1],
                                                     w["ffn_conv_b"][1:2], w["ffn_w_down"][1], "1")
    g["ffn_w_up_1"] = dwu1
    dh3 = stage("grads0_ready", dh3, g)
    dy1 = matmul(dh3, w["lru_w_out"], tb=True, name="mix1_out_dx")
    g["lru_w_out"] = matmul(y1, dh3, ta=True, split=_travel_layout("lru_w_out"), name="mix1_out_dw")
    dgx, g["lru_conv_w"], g["lru_conv_b"], g["lru_w_a"], g["lru_b_a"], g["lru_w_x"], g["lru_b_x"], g["lru_lambda"] = lru_bwd(
        gx, *lru_p, dy1, name="lru_bwd")
    dgx = stage("grads0_send", dgx, g)
    dhn1 = matmul(dgx, w["lru_w_in"], tb=True, name="mix1_in_dx")
    g["lru_w_in"] = matmul(hn1, dgx, ta=True, split=_travel_layout("lru_w_in"), name="mix1_in_dw")
    dh2, dgm1 = norm_bwd(h2, w["norm_mix"][1:2], dhn1, dh3, name="mix1_norm_bwd")
    dh2 = stage("grads1_ready", dh2, g)

    dh1, dgf0, dwu0, dcw0, dcb0, dwd0 = _ffn_layer_bwd(ffn0, dh2, w["norm_ffn"][0:1], w["ffn_w_up"][0], w["ffn_conv_w"][0],
                                                     w["ffn_conv_b"][0:1], w["ffn_w_down"][0], "0", dwd1)
    g["ffn_w_up_0"] = dwu0
    g["ffn_w_down"] = dwd0
    dh1 = stage("grads2_ready", stage("grads1_send", dh1, g), g)
    dy0 = matmul(dh1, w["ret_gdn_w_out"], tb=True, name="mix0_out_dx")
    g["ret_gdn_w_out"] = matmul(y0, dh1, ta=True, split=_travel_layout("ret_gdn_w_out"), name="mix0_out_dw")
    dq_r, dk_r, dv_r, dg_r = ret_bwd(p, tables, s_ret, dy0, name="ret_bwd")
    dy0, dq_r = stage("grads2_send", (dy0, dq_r), g)
    dcq, dck, dcv, dg_d, dsmall, dal, ddt, dgain = gdn_bwd(conv, p, small, a_log, dt_bias, w["gdn_out_gain"], s_gdn, dy0, name="gdn_bwd")
    dconv = jnp.concatenate([dcq, dck, dcv], axis=1)
    dp_conv, g["gdn_conv_w"] = gdn_conv_bwd(p, w["gdn_conv_w"], dconv, name="gdn_conv_bwd")
    dp = jnp.concatenate([dq_r, dk_r, dv_r, dg_r, dp_conv, dg_d], axis=1)
    dhn0 = matmul(dp, w_main, name="mix0_in_dx")
    dhn0 = matmul(dsmall, w_small, add=dhn0, name="mix0_in_small_dx")
    d_w_main = matmul(dp, hn0, ta=True, name="mix0_in_dw")
    d_w_small = matmul(dsmall, hn0, ta=True, name="mix0_in_small_dw")
    g["ret_gdn_w_in"] = jnp.concatenate([d_w_main, d_w_small[:2 * N_HEADS]], axis=0)
    dx, dgm0 = norm_bwd(x, w["norm_mix"][0:1], dhn0, dh1, name="mix0_norm_bwd")

    g["gdn_a_log"] = dal[:, :N_HEADS]
    g["gdn_dt_bias"] = ddt[:, :N_HEADS]
    g["gdn_out_gain"] = dgain
    g["norm_mix"] = jnp.concatenate([dgm0, dgm1], axis=0)
    g["norm_ffn"] = jnp.concatenate([dgf0, dgf1], axis=0)
    g["ffn_conv_w"] = jnp.stack([dcw0, dcw1])
    g["ffn_conv_b"] = jnp.concatenate([dcb0, dcb1], axis=0)
    return loss, dx, g


WEIGHTS = ("norm_mix", "norm_ffn", "ret_gdn_w_in", "gdn_conv_w", "gdn_a_log", "gdn_dt_bias", "gdn_out_gain", "ret_gdn_w_out",
           "lru_w_in", "lru_conv_w", "lru_conv_b", "lru_w_a", "lru_b_a", "lru_w_x", "lru_b_x", "lru_lambda", "lru_w_out",
           "ffn_w_up", "ffn_conv_w", "ffn_conv_b", "ffn_w_down", "norm_final")
MATMUL_SHARDED = {"ret_gdn_w_in": 1, "ret_gdn_w_out": 0, "lru_w_in": 1, "lru_w_out": 0, "ffn_w_up": 2, "ffn_w_down": 1}
VECTOR_SHARDED = {"gdn_conv_w": 1, "lru_conv_w": 1, "lru_conv_b": 1, "lru_b_a": 1, "lru_b_x": 1, "lru_lambda": 1, "ffn_conv_w": 2}
SHARDED = {**MATMUL_SHARDED, **VECTOR_SHARDED}
REPLICATED = tuple(n for n in WEIGHTS if n not in SHARDED)
SQUEEZE = {"ret_gdn_w_in", "gdn_conv_w", "ret_gdn_w_out", "lru_w_in", "lru_conv_w", "lru_w_a", "lru_w_x", "lru_w_out"}
MIX_IN = MIX_MAIN + 2 * N_HEADS
BIG_ARRAYS = {
    "ret_gdn_w_in": ("ret_gdn_w_in", None, (MIX_IN, D_MODEL), (N_SHARD, MIX_IN // N_SHARD, 2, D_MODEL // 2), (2, 0, 1, 3)),
    "ret_gdn_w_out": ("ret_gdn_w_out", None, (2 * GROUP, D_MODEL), (N_SHARD, 2, GROUP // N_SHARD, D_MODEL), (1, 0, 2, 3)),
    "lru_w_in": ("lru_w_in", None, (D_MODEL, 2 * D_MODEL), (2, D_MODEL // 2, N_SHARD, 2 * D_MODEL // N_SHARD), (0, 2, 1, 3)),
    "lru_w_out": ("lru_w_out", None, (D_MODEL, D_MODEL), (N_SHARD, 2, D_MODEL // (2 * N_SHARD), D_MODEL), (1, 0, 2, 3)),
    "ffn_w_up_0": ("ffn_w_up", 0, (D_MODEL, 2 * D_FF), (2, D_MODEL // 2, N_SHARD, 2 * D_FF // N_SHARD), (0, 2, 1, 3)),
    "ffn_w_up_1": ("ffn_w_up", 1, (D_MODEL, 2 * D_FF), (2, D_MODEL // 2, N_SHARD, 2 * D_FF // N_SHARD), (0, 2, 1, 3)),
    "ffn_w_down": ("ffn_w_down", None, (2, D_FF, D_MODEL), (2, N_SHARD, D_FF // N_SHARD, D_MODEL), (0, 1, 2, 3)),
}
GATHER_GROUPS = (("ret_gdn_w_in",), ("ret_gdn_w_out", "ffn_w_up_0", "ffn_w_down"), ("lru_w_in", "lru_w_out", "ffn_w_up_1"))
REDUCE_GROUPS = (("ffn_w_up_1",), ("lru_w_in", "lru_w_out"), ("ffn_w_up_0", "ffn_w_down"), ("ret_gdn_w_out", "ret_gdn_w_in"))
BLOCK_WEIGHTS = ("lru_w_a", "lru_w_x")
GATHER_COLLECTIVE_ID = 1
REDUCE_COLLECTIVE_ID = GATHER_COLLECTIVE_ID + len(GATHER_GROUPS)


TRANSPOSED = ("ret_gdn_w_in",)


def _shard_of(array, tensors):
    weight, layer = BIG_ARRAYS[array][:2]
    t = tensors[weight]
    if weight in TRANSPOSED:
        return jnp.swapaxes(t, 1, 2)[0]
    return _local_view(weight, t) if layer is None else t[layer]


def _core_halves(array, shard):
    _, _, _, split, perm = BIG_ARRAYS[array]
    kept = [k for k in range(4) if k != perm[1]]
    order = [kept.index(perm[0]), kept.index(perm[2]), kept.index(perm[3])]
    return shard.reshape([split[k] for k in kept]).transpose(order)


def _local_view(name, a):
    if name in SQUEEZE:
        return a[0]
    if a.ndim == 1:
        return a[None, :]
    return a


def kernel(x, norm_mix, norm_ffn, ret_gdn_w_in, gdn_conv_w, gdn_a_log, gdn_dt_bias, gdn_out_gain, ret_gdn_w_out, lru_w_in, lru_conv_w, lru_conv_b, lru_w_a, lru_b_a, lru_w_x, lru_b_x, lru_lambda, lru_w_out, ffn_w_up, ffn_conv_w, ffn_conv_b, ffn_w_down, norm_final, loss_target, m_norm_mix, m_norm_ffn, m_ret_gdn_w_in, m_gdn_conv_w, m_gdn_a_log, m_gdn_dt_bias, m_gdn_out_gain, m_ret_gdn_w_out, m_lru_w_in, m_lru_conv_w, m_lru_conv_b, m_lru_w_a, m_lru_b_a, m_lru_w_x, m_lru_b_x, m_lru_lambda, m_lru_w_out, m_ffn_w_up, m_ffn_conv_w, m_ffn_conv_b, m_ffn_w_down, m_norm_final, v_norm_mix, v_norm_ffn, v_ret_gdn_w_in, v_gdn_conv_w, v_gdn_a_log, v_gdn_dt_bias, v_gdn_out_gain, v_ret_gdn_w_out, v_lru_w_in, v_lru_conv_w, v_lru_conv_b, v_lru_w_a, v_lru_b_a, v_lru_w_x, v_lru_b_x, v_lru_lambda, v_lru_w_out, v_ffn_w_up, v_ffn_conv_w, v_ffn_conv_b, v_ffn_w_down, v_norm_final):
    given = dict(norm_mix=norm_mix, norm_ffn=norm_ffn, ret_gdn_w_in=ret_gdn_w_in, gdn_conv_w=gdn_conv_w, gdn_a_log=gdn_a_log, gdn_dt_bias=gdn_dt_bias, gdn_out_gain=gdn_out_gain, ret_gdn_w_out=ret_gdn_w_out, lru_w_in=lru_w_in, lru_conv_w=lru_conv_w, lru_conv_b=lru_conv_b, lru_w_a=lru_w_a, lru_b_a=lru_b_a, lru_w_x=lru_w_x, lru_b_x=lru_b_x, lru_lambda=lru_lambda, lru_w_out=lru_w_out, ffn_w_up=ffn_w_up, ffn_conv_w=ffn_conv_w, ffn_conv_b=ffn_conv_b, ffn_w_down=ffn_w_down, norm_final=norm_final)
    mom1 = dict(norm_mix=m_norm_mix, norm_ffn=m_norm_ffn, ret_gdn_w_in=m_ret_gdn_w_in, gdn_conv_w=m_gdn_conv_w, gdn_a_log=m_gdn_a_log, gdn_dt_bias=m_gdn_dt_bias, gdn_out_gain=m_gdn_out_gain, ret_gdn_w_out=m_ret_gdn_w_out, lru_w_in=m_lru_w_in, lru_conv_w=m_lru_conv_w, lru_conv_b=m_lru_conv_b, lru_w_a=m_lru_w_a, lru_b_a=m_lru_b_a, lru_w_x=m_lru_w_x, lru_b_x=m_lru_b_x, lru_lambda=m_lru_lambda, lru_w_out=m_lru_w_out, ffn_w_up=m_ffn_w_up, ffn_conv_w=m_ffn_conv_w, ffn_conv_b=m_ffn_conv_b, ffn_w_down=m_ffn_w_down, norm_final=m_norm_final)
    mom2 = dict(norm_mix=v_norm_mix, norm_ffn=v_norm_ffn, ret_gdn_w_in=v_ret_gdn_w_in, gdn_conv_w=v_gdn_conv_w, gdn_a_log=v_gdn_a_log, gdn_dt_bias=v_gdn_dt_bias, gdn_out_gain=v_gdn_out_gain, ret_gdn_w_out=v_ret_gdn_w_out, lru_w_in=v_lru_w_in, lru_conv_w=v_lru_conv_w, lru_conv_b=v_lru_conv_b, lru_w_a=v_lru_w_a, lru_b_a=v_lru_b_a, lru_w_x=v_lru_w_x, lru_b_x=v_lru_b_x, lru_lambda=v_lru_lambda, lru_w_out=v_lru_w_out, ffn_w_up=v_ffn_w_up, ffn_conv_w=v_ffn_conv_w, ffn_conv_b=v_ffn_conv_b, ffn_w_down=v_ffn_w_down, norm_final=v_norm_final)

    local = {n: _local_view(n, a) for n, a in given.items()}

    core = lax.axis_index("c")
    chip = 2 * lax.axis_index("x") + lax.axis_index("y")
    is_my_chip = lax.broadcasted_iota(jnp.int32, (N_SHARD, 1, 1), 0) == chip

    def by_core(mine, other):
        return jnp.where(core == 0, jnp.stack([mine, other]), jnp.stack([other, mine]))

    vec_names, rp_names = list(VECTOR_SHARDED), list(REPLICATED)
    full = dict(zip(vec_names, all_gather_shards([local[n] for n in vec_names], [SHARDED[n] for n in vec_names], F32, 32, "p")))
    for n in rp_names:
        full[n] = local[n]
    in_flight = {}

    bf16_halves = {}

    def cast_halves(gi):
        if gi not in bf16_halves:
            bf16_halves[gi] = [_core_halves(a, _shard_of(a, given).astype(BF16)) for a in GATHER_GROUPS[gi]]
        return bf16_halves[gi]

    def launch(gi, after=None):
        halves = cast_halves(gi)
        if after is not None:
            halves, after = lax.optimization_barrier((halves, after))
        in_flight[gi] = (halves,) + gather_halves(halves, name=f"gather_weights_{gi}", collective_id=GATHER_COLLECTIVE_ID + gi)
        return after

    def land(gi, after):
        halves, lands, sibs = in_flight[gi]
        (lands, sibs), after = lax.optimization_barrier(((lands, sibs), after))
        for a, mine, got, passed in zip(GATHER_GROUPS[gi], halves, lands, sibs):
            weight, layer, full_shape, split, perm = BIG_ARRAYS[a]
            half_mine = jnp.where(is_my_chip, jnp.where(core == 0, mine[0], mine[1])[None], got)
            half_other = jnp.where(is_my_chip, jnp.where(core == 0, mine[1], mine[0])[None], passed)
            value = by_core(half_mine, half_other).transpose(tuple(np.argsort(perm))).reshape(full_shape)
            if layer is None:
                full[weight] = value
            else:
                full.setdefault(weight, [None, None])[layer] = value
        return after

    reducing = {}

    def reduce_ready(gi, grads, then=None, extra=()):
        def travelling(a):
            split, perm = _travel_layout(a)
            return grads[a] if grads[a].ndim == 4 else grads[a].reshape(split).transpose(perm)

        arrays = [travelling(a) for a in REDUCE_GROUPS[gi]] + list(extra)
        scatter = [True] * len(REDUCE_GROUPS[gi]) + [False] * len(extra)
        reducing[gi], then = reduce_between_cores(arrays, scatter, tag=str(gi), collective_id=REDUCE_COLLECTIVE_ID + 3 * gi, before=then)
        return then

    def reduce_send(gi, then=None):
        reducing[gi], then = reduce_between_chips(reducing[gi], before=then)
        return then

    def stage(name, tensors, grads=None):
        if name == "start":
            launch(0)
            launch(1)
            fillers = (packed["wmv"], cast_halves(2), [full[n] for n in vec_names])
            (packed["wmv"], bf16_halves[2], gathered_small), tensors = lax.optimization_barrier((fillers, tensors))
            full.update(zip(vec_names, gathered_small))
            return land(0, tensors)
        if name == "normed":
            return launch(2, tensors)
        if name in ("mixed", "layer0"):
            return land({"mixed": 1, "layer0": 2}[name], tensors)
        gi = int(name[len("grads")])
        return reduce_ready(gi, grads, tensors) if name.endswith("_ready") else reduce_send(gi, tensors)

    small_names = [n for n in rp_names if n not in BLOCK_WEIGHTS] + vec_names
    loc_shapes = [local[n].shape for n in small_names]
    loc_rows = _pack_rows(sum(int(np.prod(s)) for s in loc_shapes), 256)
    packed = {"wmv": [_pack([src[n] for n in small_names], loc_rows, F32) for src in (given, mom1, mom2)]}

    loss_part, dx, grads = local_step(x[0], loss_target[0], full, stage)
    small_shapes = [grads[n].shape for n in small_names] + [(1, 1)]
    small_rows = _pack_rows(sum(int(np.prod(s)) for s in small_shapes), 16)
    small = _pack([grads[n] for n in small_names] + [loss_part[:, :1]], small_rows, F32).reshape(2, 1, small_rows // 2, LANES)
    last = len(REDUCE_GROUPS) - 1
    halves_of_blocks = [grads[n].reshape(2, 1, LRU_BLOCKS * HEAD // 2, HEAD) for n in BLOCK_WEIGHTS]
    reduce_ready(last, grads, extra=[small] + halves_of_blocks)
    reduce_send(last)
    reduced, result = {}, {}

    def finish(gi, after):
        g_own, g_sib = reduce_finish(reducing[gi], after)
        reduced.update(zip(list(REDUCE_GROUPS[gi]) + ["small"] + list(BLOCK_WEIGHTS), zip(g_own, g_sib)))

    def update(n):
        if n in TRANSPOSED:
            w3, m3, v3 = (jnp.swapaxes(t, 1, 2) for t in (given[n], mom1[n], mom2[n]))
            result[n] = tuple(jnp.swapaxes(t, 1, 2) for t in adamw_column_halves(w3, m3, v3, *reduced[n], name=f"adamw_{n}"))
            return
        done = None
        for a in (k for k, spec in BIG_ARRAYS.items() if spec[0] == n):
            r, cols = reduced[a][0].shape
            layer = BIG_ARRAYS[a][1] or 0
            w3, m3, v3 = (t if BIG_ARRAYS[a][1] is not None else t.reshape(1, 2 * r, cols) for t in (given[n], mom1[n], mom2[n]))
            done = adamw_halves(w3, m3, v3, *reduced[a], layer=layer, prev=done, name=f"adamw_{a}")
        result[n] = done

    for gi in range(last):
        finish(gi, (dx, reducing[last][1]))
    late = {BIG_ARRAYS[a][0] for a in REDUCE_GROUPS[last]}
    for n in MATMUL_SHARDED:
        if n not in late:
            update(n)
    finish(last, tuple(result[n][0] for n in MATMUL_SHARDED if n not in late))
    for n in MATMUL_SHARDED:
        if n in late:
            update(n)

    for n in BLOCK_WEIGHTS:
        w3, m3, v3 = (t.reshape(1, LRU_BLOCKS * HEAD, HEAD) for t in (given[n], mom1[n], mom2[n]))
        result[n] = adamw_halves(w3, m3, v3, *reduced[n], name=f"adamw_{n}")

    *small_sums, loss_sum = _unpack(by_core(*reduced["small"]).reshape(small_rows, LANES), small_shapes)
    loss = loss_sum[0, 0]
    g_small = dict(zip(small_names, small_sums))
    for n in vec_names:
        size = local[n].shape[SHARDED[n]]
        g_small[n] = lax.dynamic_slice_in_dim(g_small[n], chip * size, size, axis=SHARDED[n])
    w_pack, m_pack, v_pack = packed["wmv"]
    d_s, m_s, v_s = adamw(w_pack, _pack([g_small[n] for n in small_names], loc_rows, F32), m_pack, v_pack, name="adamw_small")
    for n, d, nm, nv in zip(small_names, _unpack(d_s, loc_shapes), _unpack(m_s, loc_shapes), _unpack(v_s, loc_shapes)):
        result[n] = (g_small[n], d, nm, nv)

    outs = [[result[n][k].reshape(given[n].shape) for n in WEIGHTS] for k in range(4)]
    return (loss, dx[None], *outs[0], *outs[1], *outs[2], *outs[3])
```

```python
import functools

import numpy as np
import jax
import jax.numpy as jnp
from jax import lax
from jax.experimental import pallas as pl
from jax.experimental.pallas import tpu as pltpu
from jax.experimental.pallas import tpu_sc as plsc

F32 = jnp.float32
BF16 = jnp.bfloat16
HI = lax.Precision.HIGHEST
MESH = pl.DeviceIdType.MESH

SEQ = 2048
D_MODEL = 1024
N_HEADS = 4
HEAD = 128
RET_CHUNK = 128
RET_CHUNKS_PER_STEP = 2
GDN_CHUNK = 64
GDN_CHUNKS_PER_STEP = 4
GROUP = N_HEADS * HEAD
MIX_MAIN = 8 * GROUP
D_FF = 2816
LRU_BLOCKS = 8
LRU_C = 8.0
ROPE_BASE = 10000.0
EPS = 1e-6
N_SHARD = 4
LANES = 128

ADAM_LR, ADAM_B1, ADAM_B2, ADAM_EPS, ADAM_WD, ADAM_STEP = 0.001, 0.9, 0.999, 1e-08, 0.01, 10

VMEM_LIMIT_BYTES = 56 * 1024 * 1024

_roll = pltpu.roll


def _params(**kw):
    return pltpu.CompilerParams(vmem_limit_bytes=VMEM_LIMIT_BYTES, **kw)


def _sds(shape, dtype):
    return jax.ShapeDtypeStruct(tuple(shape), dtype)


def _shift_raw(x, d):
    n = x.shape[0]
    t = lax.broadcasted_iota(jnp.int32, x.shape, 0)
    if d > 0:
        return jnp.where(t >= d, _roll(x, d, 0), 0.0)
    return jnp.where(t < n + d, _roll(x, n + d, 0), 0.0)


@functools.partial(jax.custom_vjp, nondiff_argnums=(1,))
def shift_rows(x, d):
    return _shift_raw(x, d)


def _shift_fwd(x, d):
    return _shift_raw(x, d), None


def _shift_bwd(d, _, g):
    return (_shift_raw(g, -d),)


shift_rows.defvjp(_shift_fwd, _shift_bwd)


@jax.custom_vjp
def swap_halves(x):
    return _roll(x, HEAD // 2, 1)


def _swap_fwd(x):
    return _roll(x, HEAD // 2, 1), None


def _swap_bwd(_, g):
    return (_roll(g, HEAD // 2, 1),)


swap_halves.defvjp(_swap_fwd, _swap_bwd)


SCAN_BLOCK_ROWS = 64


def _scan_block(a, u, reverse):
    n = a.shape[0]
    t = lax.broadcasted_iota(jnp.int32, a.shape, 0)
    d = 1
    while d < n:
        if reverse:
            m = t < n - d
            a_s, u_s = _roll(a, n - d, 0), _roll(u, n - d, 0)
        else:
            m = t >= d
            a_s, u_s = _roll(a, d, 0), _roll(u, d, 0)
        u = a * jnp.where(m, u_s, 0.0) + u
        a = a * jnp.where(m, a_s, 1.0)
        d *= 2
    return a, u


def _scan_raw(a, u, reverse):
    n = a.shape[0]
    blocks = range(n // SCAN_BLOCK_ROWS)
    out = [None] * len(blocks)
    entering = None
    for b in (reversed(blocks) if reverse else blocks):
        rows = slice(b * SCAN_BLOCK_ROWS, (b + 1) * SCAN_BLOCK_ROWS)
        a_run, h = _scan_block(a[rows], u[rows], reverse)
        if entering is not None:
            h = a_run * entering + h
        out[b] = h
        entering = h[:1] if reverse else h[SCAN_BLOCK_ROWS - 1:]
    return jnp.concatenate(out, axis=0)


@jax.custom_vjp
def lin_scan(a, u):
    return _scan_raw(a, u, False)


def _lin_scan_fwd(a, u):
    hs = _scan_raw(a, u, False)
    return hs, (a, hs)


def _lin_scan_bwd(res, g):
    a, hs = res
    lam = _scan_raw(_shift_raw(a, -1), g, True)
    return lam * _shift_raw(hs, 1), lam


lin_scan.defvjp(_lin_scan_fwd, _lin_scan_bwd)


def _bdot(a, b, dims=(((1,), (0,)), ((), ()))):
    return lax.dot_general(a.astype(BF16), b.astype(BF16), dims, preferred_element_type=F32)


def _each(f, *seqs):
    return tuple(f(*a) for a in zip(*seqs))


def _split_bf16(a):
    hi = a.astype(BF16)
    return hi, (a - hi.astype(F32)).astype(BF16)


def _dot3_raw(a_s, b_s):
    a_hl = _each(_split_bf16, a_s)
    b_hl = _each(_split_bf16, b_s)
    hh = _each(lambda a, b: _bdot(a[0], b[0]), a_hl, b_hl)
    hl = _each(lambda a, b: _bdot(a[0], b[1]), a_hl, b_hl)
    lh = _each(lambda a, b: _bdot(a[1], b[0]), a_hl, b_hl)
    return _each(lambda x, y, z: x + (y + z), hh, hl, lh)


@jax.custom_vjp
def dot3(a_s, b_s):
    return _dot3_raw(a_s, b_s)


def _dot3_fwd(a_s, b_s):
    return _dot3_raw(a_s, b_s), (a_s, b_s)


def _dot3_bwd(res, g_s):
    a_s, b_s = res
    return (_each(lambda g, b: _bdot(g, b, (((1,), (1,)), ((), ()))), g_s, b_s),
            _each(lambda a, g: _bdot(a, g, (((0,), (0,)), ((), ()))), a_s, g_s))


dot3.defvjp(_dot3_fwd, _dot3_bwd)


def _eye(n):
    i = lax.broadcasted_iota(jnp.int32, (n, n), 0)
    j = lax.broadcasted_iota(jnp.int32, (n, n), 1)
    return (i == j).astype(F32)


def _unit_lower_inverse_raw(lmats):
    n = lmats[0].shape[0]
    eye = _eye(n)
    ps = _each(lambda l: -l, lmats)
    invs = _each(lambda x: eye + x, ps)
    k = 1
    while 2 * k < n:
        ps = _each(lambda p: _bdot(p, p), ps)
        invs = _each(lambda inv, p: inv + _bdot(inv, p), invs, ps)
        k *= 2
    prods = _dot3_raw(lmats, invs)
    resids = _each(lambda inv, pr: eye - inv - pr, invs, prods)
    return _each(lambda inv, r: inv + _bdot(inv, r), invs, resids)


@jax.custom_vjp
def unit_lower_inverse(lmats):
    return _unit_lower_inverse_raw(lmats)


def _uli_fwd(lmats):
    invs = _unit_lower_inverse_raw(lmats)
    return invs, invs


def _uli_bwd(invs, g_s):
    ms = _each(lambda inv, g: _bdot(inv, g, (((0,), (0,)), ((), ()))), invs, g_s)
    return (_each(lambda m, inv: -_bdot(m, inv, (((1,), (1,)), ((), ()))), ms, invs),)


unit_lower_inverse.defvjp(_uli_fwd, _uli_bwd)


def _cumsum_raw(x, reverse):
    n = x.shape[0]
    t = lax.broadcasted_iota(jnp.int32, x.shape, 0)
    d = 1
    while d < n:
        if reverse:
            x = x + jnp.where(t < n - d, _roll(x, n - d, 0), 0.0)
        else:
            x = x + jnp.where(t >= d, _roll(x, d, 0), 0.0)
        d *= 2
    return x


@jax.custom_vjp
def cumsum_rows(x):
    return _cumsum_raw(x, False)


def _cumsum_fwd(x):
    return _cumsum_raw(x, False), None


def _cumsum_bwd(_, g):
    return (_cumsum_raw(g, True),)


cumsum_rows.defvjp(_cumsum_fwd, _cumsum_bwd)


_NT = (((1,), (1,)), ((), ()))
_TN = (((0,), (0,)), ((), ()))


def _softplus(x):
    return jnp.maximum(x, 0.0) + jnp.log1p(jnp.exp(-jnp.abs(x)))


def _expm1_nonpos(x):
    poly = x * (1.0 + x * (0.5 + x * (1.0 / 6 + x * (1.0 / 24 + x * (1.0 / 120 + x * (1.0 / 720))))))
    return jnp.where(x > -0.25, poly, jnp.exp(x) - 1.0)


def _rms(x):
    return x * lax.rsqrt(jnp.mean(x * x, axis=-1, keepdims=True) + EPS)


def _causal_conv(x, w, width):
    y = w[width - 1:width, :] * x
    for j in range(width - 1):
        y = y + w[j:j + 1, :] * shift_rows(x, width - 1 - j)
    return y


def _norm_fn(x, g):
    return _rms(x) * g


def _ffn_act_fn(ug, uv, wg, wv, bg, bv):
    return jax.nn.silu(_causal_conv(ug, wg, 3) + bg) * (_causal_conv(uv, wv, 3) + bv)


def _gdn_conv_fn(x, w):
    return jax.nn.silu(_causal_conv(x, w, 4))


def _lru_fn(gate, x, cw, cb, wa, ba, wx, bx, lam):
    xr = _causal_conv(x, cw, 4) + cb
    r = jax.nn.sigmoid(_bdot(xr, wa) + ba)
    i = jax.nn.sigmoid(_bdot(xr, wx) + bx)
    log_a = -LRU_C * r * _softplus(-lam)
    a = jnp.exp(log_a)
    u = jnp.sqrt(-_expm1_nonpos(2.0 * log_a)) * (i * xr)
    hs = lin_scan(a, u)
    return jax.nn.gelu(gate) * hs


def _ret_fn(qs, ks, vs, gates, states, cos2, sin2, dmasks, ktails, qdecs, cdecs):
    c = RET_CHUNK
    n_heads = len(qs)
    n_chunks = qs[0].shape[0] // c
    units = tuple((ci, h) for ci in range(n_chunks) for h in range(n_heads))

    def rows(x, ci):
        return x[ci * c:(ci + 1) * c]

    qrs = tuple(rows(qs[h], ci) * rows(cos2, ci) + swap_halves(rows(qs[h], ci)) * rows(sin2, ci) for ci, h in units)
    krs = tuple((rows(ks[h], ci) * rows(cos2, ci) + swap_halves(rows(ks[h], ci)) * rows(sin2, ci)) * (HEAD ** -0.5) for ci, h in units)
    vus = tuple(rows(vs[h], ci) for ci, h in units)
    scores = tuple(_bdot(q, k, _NT) * dmasks[h] for q, k, (_, h) in zip(qrs, krs, units))
    intra = _each(lambda sc, v: _bdot(sc, v), scores, vus)
    outs = []
    for ci in range(n_chunks):
        mine = slice(ci * n_heads, (ci + 1) * n_heads)
        inter = _each(lambda q, d, s: _bdot(q * d, s), qrs[mine], qdecs, states)
        outs.append(_each(lambda a, b: a + b, intra[mine], inter))
        states = _each(lambda s, cd, k, kt, v: s * cd + _bdot(k * kt, v, _TN), states, cdecs, krs[mine], ktails, vus[mine])
    ys = tuple(_rms(jnp.concatenate([outs[ci][h] for ci in range(n_chunks)], axis=0)) * jax.nn.silu(gates[h]) for h in range(n_heads))
    return ys, states


def _pick_lane(x, lane_idx):
    lane = lax.broadcasted_iota(jnp.int32, x.shape, 1)
    return jnp.sum(jnp.where(lane == lane_idx, x, 0.0), axis=1, keepdims=True)


def _l2norm(x):
    return x * lax.rsqrt(jnp.sum(x * x, axis=-1, keepdims=True) + EPS)


def _gdn_fn(qcs, kcs, vcs, gates, small, a_log, dt_bias, gain, states):
    c = GDN_CHUNK
    n_heads = len(qcs)
    n_chunks = qcs[0].shape[0] // c
    units = tuple((ci, h) for ci in range(n_chunks) for h in range(n_heads))

    def unit_rows(per_head):
        return tuple(per_head[h][ci * c:(ci + 1) * c] for ci, h in units)

    smalls = tuple(small[ci * c:(ci + 1) * c] for ci, _ in units)
    heads = tuple(h for _, h in units)
    intra = _gdn_intra(unit_rows(qcs), unit_rows(kcs), unit_rows(vcs), smalls, heads, a_log, dt_bias)
    outs = []
    for ci in range(n_chunks):
        mine = slice(ci * n_heads, (ci + 1) * n_heads)
        os_, states = _gdn_inter(*(part[mine] for part in intra), states)
        outs.append(os_)
    ys = tuple(_rms(jnp.concatenate([outs[ci][h] for ci in range(n_chunks)], axis=0)) * gain * jax.nn.silu(gates[h])
               for h in range(n_heads))
    return ys, states


def _gdn_inter(qs, ks, us, ws, attns, gcs, g_lasts, states):
    v_news = _each(lambda u, w, s: u - _bdot(w, s), us, ws, states)
    inter = _each(lambda q, gc, s: _bdot(q * jnp.exp(gc), s), qs, gcs, states)
    os_ = _each(lambda x, a, v: x + _bdot(a, v), inter, attns, v_news)
    new_states = _each(lambda s, gl, k, gc, v: s * jnp.exp(gl) + _bdot(k * jnp.exp(gl - gc), v, _TN), states, g_lasts, ks, gcs, v_news)
    return os_, new_states


def _gdn_intra(qcs, kcs, vcs, smalls, heads, a_log, dt_bias):
    c = GDN_CHUNK
    qs = _each(lambda x: _l2norm(x) * (HEAD ** -0.5), qcs)
    ks = _each(_l2norm, kcs)
    betas = _each(lambda sm, h: jax.nn.sigmoid(_pick_lane(sm, h)), smalls, heads)
    gs = _each(lambda sm, h: -jnp.exp(_pick_lane(a_log, h)) * _softplus(_pick_lane(sm, h + N_HEADS) + _pick_lane(dt_bias, h)),
               smalls, heads)
    i = lax.broadcasted_iota(jnp.int32, (c, c), 0)
    j = lax.broadcasted_iota(jnp.int32, (c, c), 1)
    tril = i >= j
    gcs = _each(lambda g: cumsum_rows(jnp.broadcast_to(g, (c, LANES)))[:, :1], gs)
    gc_rows = _each(lambda gc: jnp.broadcast_to(gc, (c, c)), gcs)
    decays = _each(lambda r: jnp.where(tril, jnp.exp(jnp.where(tril, r - r.T, 0.0)), 0.0), gc_rows)
    kbs = _each(lambda k, b: k * b, ks, betas)
    lmats = _each(lambda kb, k, d: jnp.where(i > j, _bdot(kb, k, _NT) * d, 0.0), kbs, ks, decays)
    attns = _each(lambda q, k, d: jnp.where(tril, _bdot(q, k, _NT) * d, 0.0), qs, ks, decays)
    invs = unit_lower_inverse(lmats)
    us = dot3(invs, _each(lambda v, b: v * b, vcs, betas))
    ws = dot3(invs, _each(lambda kb, gc: kb * jnp.exp(gc), kbs, gcs))
    g_lasts = _each(lambda g: jnp.sum(g, axis=0, keepdims=True), gs)
    return qs, ks, us, ws, attns, gcs, g_lasts


def _final_fn(h, g, target):
    y = _rms(h) * g
    return 0.5 * jnp.sum(jnp.mean(jnp.square(y - target), axis=-1, keepdims=True), axis=0, keepdims=True)


def _tile(n, candidates):
    for t in candidates:
        if n % t == 0:
            return t
    raise ValueError(f"no tile for {n}")


MATMUL_RESIDENT_LHS_BYTES = 8 * 1024 * 1024


def matmul(a, b, *, ta=False, tb=False, add=None, out_dtype=F32, tm=None, tn=None, split=None, layer=None, name):
    m = a.shape[1] if ta else a.shape[0]
    k = a.shape[0] if ta else a.shape[1]
    n = b.shape[0] if tb else b.shape[1]
    assert k == (b.shape[1] if tb else b.shape[0])
    out_shape, out_block, out_index = (m, n), None, lambda i, j: (i, j)
    if split is not None:
        dims4, perm = split
        out_shape = tuple(dims4[p] for p in perm)
        r, cols = out_shape[2:]
        tm, tn = m, tn or _tile(cols, (1408, 512))
        cb = cols // tn
        if perm == (0, 2, 1, 3):
            out_block, out_index = (2, None, r, tn), lambda i, j: (0, j // cb, 0, j % cb)
        elif perm == (1, 0, 2, 3):
            out_block, out_index = (2, N_SHARD, r, tn), lambda i, j: (0, 0, 0, j)
        else:
            raise ValueError(perm)
    if tm is None and not ta and m * k * a.dtype.itemsize <= MATMUL_RESIDENT_LHS_BYTES:
        tm = m
    tm = tm or _tile(m, (1024, 512, 1408, 256, 128))
    tn = tn or _tile(n, (512, 1408, 256, 128))
    aliases, prev = {}, None
    if layer is not None:
        index, count, prev = layer
        out_shape, out_block, out_index = (count, m, n), (None, tm, tn), lambda i, j: (index, i, j)
    dims = (((0 if ta else 1,), (1 if tb else 0,)), ((), ()))

    def body(a_ref, b_ref, *rest):
        acc = lax.dot_general(a_ref[...].astype(BF16), b_ref[...].astype(BF16), dims, preferred_element_type=F32)
        if add is not None:
            acc = acc + rest[0][...]
        o_ref = rest[-1]
        acc = acc.astype(out_dtype)
        if split is not None and split[1] == (1, 0, 2, 3):
            rows = o_ref.shape[2]
            for s in range(N_SHARD):
                for h in range(2):
                    o_ref[h, s] = acc[(2 * s + h) * rows:(2 * s + h + 1) * rows]
        else:
            o_ref[...] = acc.reshape(o_ref.shape)

    a_spec = pl.BlockSpec((k, tm), lambda i, j: (0, i)) if ta else pl.BlockSpec((tm, k), lambda i, j: (i, 0))
    b_spec = pl.BlockSpec((tn, k), lambda i, j: (j, 0)) if tb else pl.BlockSpec((k, tn), lambda i, j: (0, j))
    o_spec = pl.BlockSpec(out_block or (tm, tn), out_index)
    in_specs, args = [a_spec, b_spec], [a, b]
    if add is not None:
        in_specs.append(o_spec)
        args.append(add)
    if prev is not None:
        aliases = {len(args): 0}
        in_specs.append(pl.BlockSpec(memory_space=pl.ANY))
        args.append(prev)
    return pl.pallas_call(body, out_shape=_sds(out_shape, out_dtype), grid=(m // tm, n // tn), in_specs=in_specs,
                          out_specs=o_spec, input_output_aliases=aliases, compiler_params=_params(), name=name)(*args)


def norm_matmul(x, g, b, *, tb=False, name):
    t, k = x.shape
    n = b.shape[0] if tb else b.shape[1]
    tn = _tile(n, (512, 1408, 256, 128))
    dims = (((1,), (1 if tb else 0,)), ((), ()))

    def body(x_ref, g_ref, b_ref, o_ref, hn_ref):
        @pl.when(pl.program_id(0) == 0)
        def _():
            hn_ref[...] = _norm_fn(x_ref[...], g_ref[...]).astype(BF16)

        o_ref[...] = lax.dot_general(hn_ref[...], b_ref[...].astype(BF16), dims, preferred_element_type=F32)

    b_spec = pl.BlockSpec((tn, k), lambda j: (j, 0)) if tb else pl.BlockSpec((k, tn), lambda j: (0, j))
    whole = pl.BlockSpec((t, k), lambda j: (0, 0))
    return pl.pallas_call(body, out_shape=(_sds((t, n), F32), _sds((t, k), BF16)), grid=(n // tn,),
                          in_specs=[whole, pl.BlockSpec((1, k), lambda j: (0, 0)), b_spec],
                          out_specs=(pl.BlockSpec((t, tn), lambda j: (0, j)), whole), compiler_params=_params(), name=name)(x, g, b)


ROW_TILE = 256


def norm_bwd(x, g, dy, dres, *, name):
    t, d = x.shape

    def body(x_ref, g_ref, dy_ref, dres_ref, dx_ref, dg_ref):
        _, vjp = jax.vjp(_norm_fn, x_ref[...], g_ref[...])
        dx, dg = vjp(dy_ref[...])
        dx_ref[...] = dx + dres_ref[...]

        @pl.when(pl.program_id(0) == 0)
        def _():
            dg_ref[...] = jnp.zeros_like(dg_ref)

        dg_ref[...] += dg

    row = pl.BlockSpec((ROW_TILE, d), lambda i: (i, 0))
    vec = pl.BlockSpec((1, d), lambda i: (0, 0))
    return pl.pallas_call(body, out_shape=(_sds((t, d), F32), _sds((1, d), F32)), grid=(t // ROW_TILE,),
                          in_specs=[row, vec, row, row], out_specs=(row, vec), compiler_params=_params(), name=name)(x, g, dy, dres)


def final_fwd_bwd(h, g, target, *, name):
    t, d = h.shape

    def body(h_ref, g_ref, t_ref, loss_ref, dh_ref, dg_ref):
        tgt = t_ref[...]
        loss, vjp = jax.vjp(lambda hh, gg: _final_fn(hh, gg, tgt), h_ref[...], g_ref[...])
        dh, dg = vjp(jnp.ones((1, 1), F32))
        dh_ref[...] = dh

        @pl.when(pl.program_id(0) == 0)
        def _():
            dg_ref[...] = jnp.zeros_like(dg_ref)
            loss_ref[...] = jnp.zeros_like(loss_ref)

        dg_ref[...] += dg
        loss_ref[...] += jnp.broadcast_to(loss, loss_ref.shape)

    row = pl.BlockSpec((ROW_TILE, d), lambda i: (i, 0))
    vec = pl.BlockSpec((1, d), lambda i: (0, 0))
    return pl.pallas_call(body, out_shape=(_sds((1, LANES), F32), _sds((t, d), F32), _sds((1, d), F32)), grid=(t // ROW_TILE,),
                          in_specs=[row, vec, row], out_specs=(pl.BlockSpec((1, LANES), lambda i: (0, 0)), row, vec),
                          compiler_params=_params(), name=name)(h, g, target)


FFN_FWD_COLS = 256
FFN_BWD_COLS = 128


def ffn_act_fwd(u, cw, cb, *, name):
    t = u.shape[0]
    w = FFN_FWD_COLS
    nb = D_FF // w

    def body(ug_ref, uv_ref, wg_ref, wv_ref, bg_ref, bv_ref, o_ref):
        o_ref[...] = _ffn_act_fn(ug_ref[...], uv_ref[...], wg_ref[...], wv_ref[...], bg_ref[...], bv_ref[...]).astype(BF16)

    def col(rows, off):
        return pl.BlockSpec((rows, w), lambda j: (0, j + off))

    return pl.pallas_call(body, out_shape=_sds((t, D_FF), BF16), grid=(nb,),
                          in_specs=[col(t, 0), col(t, nb), col(3, 0), col(3, nb), col(1, 0), col(1, nb)],
                          out_specs=col(t, 0), compiler_params=_params(), name=name)(u, u, cw, cw, cb, cb)


def _put_column_blocks(step, n_steps, blocks, dst_ref, width, stage_ref, sems):
    def copies(at):
        slot = at % 2
        return [pltpu.make_async_copy(stage_ref.at[slot, p], dst_ref.at[:, pl.ds(pl.multiple_of((p * n_steps + at) * width, LANES), width)],
                                      sems.at[slot, p]) for p in range(len(blocks))]

    @pl.when(step >= 2)
    def _():
        for cp in copies(step - 2):
            cp.wait()

    for p, value in enumerate(blocks):
        stage_ref[step % 2, p] = value
    for cp in copies(step):
        cp.start()

    @pl.when(step == n_steps - 1)
    def _():
        for cp in copies(step - 1) + copies(step):
            cp.wait()


def ffn_act_bwd(u, cw, cb, da, *, name):
    t = u.shape[0]
    w = FFN_BWD_COLS
    nb = D_FF // w

    def body(ug_ref, uv_ref, wg_ref, wv_ref, bg_ref, bv_ref, da_ref, dug_ref, duv_ref, dwg_ref, dwv_ref, dbg_ref, dbv_ref):
        _, vjp = jax.vjp(_ffn_act_fn, ug_ref[...], uv_ref[...], wg_ref[...], wv_ref[...], bg_ref[...], bv_ref[...])
        dug, duv, dwg, dwv, dbg, dbv = vjp(da_ref[...])
        dug_ref[...] = dug.astype(BF16)
        duv_ref[...] = duv.astype(BF16)
        dwg_ref[...] = dwg
        dwv_ref[...] = dwv
        dbg_ref[...] = dbg
        dbv_ref[...] = dbv

    def col(rows, off):
        return pl.BlockSpec((rows, w), lambda j: (0, j + off))

    outs = pl.pallas_call(
        body, out_shape=(_sds((t, D_FF), BF16), _sds((t, D_FF), BF16), _sds((3, D_FF), F32), _sds((3, D_FF), F32),
                         _sds((1, D_FF), F32), _sds((1, D_FF), F32)),
        grid=(nb,), in_specs=[col(t, 0), col(t, nb), col(3, 0), col(3, nb), col(1, 0), col(1, nb), col(t, 0)],
        out_specs=(col(t, 0), col(t, 0), col(3, 0), col(3, 0), col(1, 0), col(1, 0)), compiler_params=_params(), name=name,
    )(u, u, cw, cw, cb, cb, da)
    dug, duv, dwg, dwv, dbg, dbv = outs
    return jnp.concatenate([dug, duv], axis=1), jnp.concatenate([dwg, dwv], axis=1), jnp.concatenate([dbg, dbv], axis=1)


GDN_CONV_COLS = 256
GDN_CONV_OFF = 4 * GROUP


def gdn_conv_fwd(p, cw, *, name):
    t = p.shape[0]
    w = GDN_CONV_COLS
    nb = 3 * GROUP // w
    off = GDN_CONV_OFF // w

    def body(x_ref, w_ref, o_ref):
        o_ref[...] = _gdn_conv_fn(x_ref[...], w_ref[...])

    return pl.pallas_call(body, out_shape=_sds((t, 3 * GROUP), F32), grid=(nb,),
                          in_specs=[pl.BlockSpec((t, w), lambda j: (0, j + off)), pl.BlockSpec((4, w), lambda j: (0, j))],
                          out_specs=pl.BlockSpec((t, w), lambda j: (0, j)), compiler_params=_params(), name=name)(p, cw)


def gdn_conv_bwd(p, cw, dc, *, name):
    t = p.shape[0]
    w = GDN_CONV_COLS
    nb = 3 * GROUP // w
    off = GDN_CONV_OFF // w

    def body(x_ref, w_ref, dc_ref, dx_ref, dw_ref):
        _, vjp = jax.vjp(_gdn_conv_fn, x_ref[...], w_ref[...])
        dx, dw = vjp(dc_ref[...])
        dx_ref[...] = dx.astype(BF16)
        dw_ref[...] = dw

    blk = pl.BlockSpec((t, w), lambda j: (0, j))
    wblk = pl.BlockSpec((4, w), lambda j: (0, j))
    return pl.pallas_call(body, out_shape=(_sds((t, 3 * GROUP), BF16), _sds((4, 3 * GROUP), F32)), grid=(nb,),
                          in_specs=[pl.BlockSpec((t, w), lambda j: (0, j + off)), wblk, blk], out_specs=(blk, wblk),
                          compiler_params=_params(), name=name)(p, cw, dc)


def _lru_specs(t):
    w = D_MODEL // LRU_BLOCKS
    gate = pl.BlockSpec((t, w), lambda j: (0, j))
    xin = pl.BlockSpec((t, w), lambda j: (0, j + LRU_BLOCKS))
    cw = pl.BlockSpec((4, w), lambda j: (0, j))
    vec = pl.BlockSpec((1, w), lambda j: (0, j))
    mat = pl.BlockSpec((None, w, w), lambda j: (j, 0, 0))
    return gate, xin, cw, vec, mat


def lru_fwd(gx, cw, cb, wa, ba, wx, bx, lam, *, name):
    t = gx.shape[0]
    gate, xin, cws, vec, mat = _lru_specs(t)

    def body(g_ref, x_ref, cw_ref, cb_ref, wa_ref, ba_ref, wx_ref, bx_ref, lam_ref, o_ref):
        o_ref[...] = _lru_fn(g_ref[...], x_ref[...], cw_ref[...], cb_ref[...], wa_ref[...], ba_ref[...], wx_ref[...],
                             bx_ref[...], lam_ref[...]).astype(BF16)

    return pl.pallas_call(body, out_shape=_sds((t, D_MODEL), BF16), grid=(LRU_BLOCKS,),
                          in_specs=[gate, xin, cws, vec, mat, vec, mat, vec, vec], out_specs=gate,
                          compiler_params=_params(), name=name)(gx, gx, cw, cb, wa, ba, wx, bx, lam)


def lru_bwd(gx, cw, cb, wa, ba, wx, bx, lam, dy, *, name):
    t = gx.shape[0]
    gate, xin, cws, vec, mat = _lru_specs(t)

    def body(g_ref, x_ref, cw_ref, cb_ref, wa_ref, ba_ref, wx_ref, bx_ref, lam_ref, dy_ref,
             dgx_ref, dcw_ref, dcb_ref, dwa_ref, dba_ref, dwx_ref, dbx_ref, dlam_ref, stage_ref, sems):
        _, vjp = jax.vjp(_lru_fn, g_ref[...], x_ref[...], cw_ref[...], cb_ref[...], wa_ref[...], ba_ref[...], wx_ref[...],
                         bx_ref[...], lam_ref[...])
        dg, dx, dcw, dcb, dwa, dba, dwx, dbx, dlam = vjp(dy_ref[...])
        _put_column_blocks(pl.program_id(0), LRU_BLOCKS, (dg.astype(BF16), dx.astype(BF16)), dgx_ref, D_MODEL // LRU_BLOCKS, stage_ref, sems)
        dcw_ref[...] = dcw
        dcb_ref[...] = dcb
        dwa_ref[...] = dwa
        dba_ref[...] = dba
        dwx_ref[...] = dwx
        dbx_ref[...] = dbx
        dlam_ref[...] = dlam

    d = D_MODEL
    w = d // LRU_BLOCKS
    out_shape = (_sds((t, 2 * d), BF16), _sds((4, d), F32), _sds((1, d), F32), _sds((LRU_BLOCKS, w, w), F32),
                 _sds((1, d), F32), _sds((LRU_BLOCKS, w, w), F32), _sds((1, d), F32), _sds((1, d), F32))
    return pl.pallas_call(body, out_shape=out_shape, grid=(LRU_BLOCKS,),
                          in_specs=[gate, xin, cws, vec, mat, vec, mat, vec, vec, gate],
                          out_specs=(pl.BlockSpec(memory_space=pl.ANY), cws, vec, mat, vec, mat, vec, vec),
                          scratch_shapes=[pltpu.VMEM((2, 2, t, w), BF16), pltpu.SemaphoreType.DMA((2, 2))],
                          compiler_params=_params(), name=name)(gx, gx, cw, cb, wa, ba, wx, bx, lam, dy)


def _ret_tables():
    half = HEAD // 2
    inv_freq = (np.float32(ROPE_BASE) ** (-np.arange(half, dtype=np.float32) / np.float32(half))).astype(np.float32)
    ang = (np.arange(SEQ, dtype=np.float32)[:, None] * inv_freq[None, :]).astype(np.float64)
    cos2 = np.concatenate([np.cos(ang), np.cos(ang)], axis=1).astype(np.float32)
    sin2 = np.concatenate([-np.sin(ang), np.sin(ang)], axis=1).astype(np.float32)
    c = RET_CHUNK
    log_gamma = np.log1p(-np.exp2(-5.0 - np.arange(N_HEADS, dtype=np.float64)))
    idx = np.arange(c, dtype=np.float64)
    rel = idx[:, None] - idx[None, :]
    dmask = np.where(rel >= 0, np.exp(log_gamma[:, None, None] * np.maximum(rel, 0.0)), 0.0)
    ones = np.ones((N_HEADS, c, HEAD))
    ktail = np.exp(log_gamma[:, None] * (c - 1 - idx))[:, :, None] * ones
    qdec = np.exp(log_gamma[:, None] * (idx + 1.0))[:, :, None] * ones
    cdec = np.exp(log_gamma * c)[:, None, None] * ones
    return tuple(jnp.asarray(a, F32) for a in (cos2, sin2, dmask, ktail, qdec, cdec))


def _ret_specs(rev):
    c = RET_CHUNK * RET_CHUNKS_PER_STEP
    nc = SEQ // c

    def n_of(n):
        return nc - 1 - n if rev else n

    def group(off):
        return pl.BlockSpec((c, GROUP), lambda n: (n_of(n), off))

    tab = pl.BlockSpec((c, HEAD), lambda n: (n_of(n), 0))
    const = pl.BlockSpec((N_HEADS, RET_CHUNK, HEAD), lambda n: (0, 0, 0))
    state = pl.BlockSpec((N_HEADS, None, HEAD, HEAD), lambda n: (0, n_of(n), 0, 0))
    return group, tab, const, state, nc


def _head(h):
    return slice(h * HEAD, (h + 1) * HEAD)


def ret_fwd(p, tables, *, name):
    group, tab, const, state, nc = _ret_specs(False)

    def body(q_ref, k_ref, v_ref, g_ref, cos_ref, sin_ref, dm_ref, kt_ref, qd_ref, cd_ref, y_ref, st_ref, s_scr):
        @pl.when(pl.program_id(0) == 0)
        def _():
            s_scr[...] = jnp.zeros_like(s_scr)

        heads = range(N_HEADS)
        states = tuple(s_scr[h] for h in heads)
        ys, new_states = _ret_fn(*(tuple(r[:, _head(h)] for h in heads) for r in (q_ref, k_ref, v_ref, g_ref)), states,
                                 cos_ref[...], sin_ref[...], *(tuple(r[h] for h in heads) for r in (dm_ref, kt_ref, qd_ref, cd_ref)))
        for h in heads:
            st_ref[h] = states[h]
            y_ref[:, _head(h)] = ys[h].astype(BF16)
            s_scr[h] = new_states[h]

    return pl.pallas_call(
        body, out_shape=(_sds((SEQ, GROUP), BF16), _sds((N_HEADS, nc, HEAD, HEAD), F32)), grid=(nc,),
        in_specs=[group(0), group(1), group(2), group(3), tab, tab, const, const, const, const],
        out_specs=(group(0), state), scratch_shapes=[pltpu.VMEM((N_HEADS, HEAD, HEAD), F32)], compiler_params=_params(), name=name,
    )(p, p, p, p, *tables)


def ret_bwd(p, tables, states, dy, *, name):
    group, tab, const, state, nc = _ret_specs(True)

    def body(q_ref, k_ref, v_ref, g_ref, cos_ref, sin_ref, dm_ref, kt_ref, qd_ref, cd_ref, st_ref, dy_ref,
             dq_ref, dk_ref, dv_ref, dg_ref, ds_scr):
        @pl.when(pl.program_id(0) == 0)
        def _():
            ds_scr[...] = jnp.zeros_like(ds_scr)

        heads = range(N_HEADS)
        consts = (cos_ref[...], sin_ref[...], *(tuple(r[h] for h in heads) for r in (dm_ref, kt_ref, qd_ref, cd_ref)))
        _, vjp = jax.vjp(lambda *a: _ret_fn(*a, *consts), *(tuple(r[:, _head(h)] for h in heads) for r in (q_ref, k_ref, v_ref, g_ref)),
                         tuple(st_ref[h] for h in heads))
        dqs, dks, dvs, dgs, dss = vjp((tuple(dy_ref[:, _head(h)] for h in heads), tuple(ds_scr[h] for h in heads)))
        for h in heads:
            dq_ref[:, _head(h)] = dqs[h].astype(BF16)
            dk_ref[:, _head(h)] = dks[h].astype(BF16)
            dv_ref[:, _head(h)] = dvs[h].astype(BF16)
            dg_ref[:, _head(h)] = dgs[h].astype(BF16)
            ds_scr[h] = dss[h]

    out = _sds((SEQ, GROUP), BF16)
    return pl.pallas_call(
        body, out_shape=(out, out, out, out), grid=(nc,),
        in_specs=[group(0), group(1), group(2), group(3), tab, tab, const, const, const, const, state, group(0)],
        out_specs=(group(0), group(0), group(0), group(0)), scratch_shapes=[pltpu.VMEM((N_HEADS, HEAD, HEAD), F32)],
        compiler_params=_params(), name=name,
    )(p, p, p, p, *tables, states, dy)


def _gdn_specs(rev):
    c = GDN_CHUNK * GDN_CHUNKS_PER_STEP
    nc = SEQ // c

    def n_of(n):
        return nc - 1 - n if rev else n

    def group(off):
        return pl.BlockSpec((c, GROUP), lambda n: (n_of(n), off))

    small = pl.BlockSpec((c, LANES), lambda n: (n_of(n), 0))
    vec = pl.BlockSpec((1, LANES), lambda n: (0, 0))
    state = pl.BlockSpec((N_HEADS, None, HEAD, HEAD), lambda n: (0, n_of(n), 0, 0))
    return group, small, vec, state, nc


GDN_GATE_GROUP = 7


def gdn_fwd(conv, p, small, a_log, dt_bias, gain, *, name):
    group, sm, vec, state, nc = _gdn_specs(False)

    def body(q_ref, k_ref, v_ref, g_ref, sm_ref, al_ref, dt_ref, gn_ref, y_ref, st_ref, s_scr):
        @pl.when(pl.program_id(0) == 0)
        def _():
            s_scr[...] = jnp.zeros_like(s_scr)

        states = tuple(s_scr[h] for h in range(N_HEADS))
        ys, new_states = _gdn_fn(*(tuple(r[:, _head(h)] for h in range(N_HEADS)) for r in (q_ref, k_ref, v_ref, g_ref)),
                                 sm_ref[...], al_ref[...], dt_ref[...], gn_ref[...], states)
        for h in range(N_HEADS):
            st_ref[h] = states[h]
            y_ref[:, _head(h)] = ys[h].astype(BF16)
            s_scr[h] = new_states[h]

    return pl.pallas_call(
        body, out_shape=(_sds((SEQ, GROUP), BF16), _sds((N_HEADS, nc, HEAD, HEAD), F32)), grid=(nc,),
        in_specs=[group(0), group(1), group(2), group(GDN_GATE_GROUP), sm, vec, vec, vec], out_specs=(group(0), state),
        scratch_shapes=[pltpu.VMEM((N_HEADS, HEAD, HEAD), F32)], compiler_params=_params(), name=name,
    )(conv, conv, conv, p, small, a_log, dt_bias, gain)


def gdn_bwd(conv, p, small, a_log, dt_bias, gain, states, dy, *, name):
    group, sm, vec, state, nc = _gdn_specs(True)

    def body(q_ref, k_ref, v_ref, g_ref, sm_ref, al_ref, dt_ref, gn_ref, st_ref, dy_ref,
             dq_ref, dk_ref, dv_ref, dg_ref, dsm_ref, dal_ref, ddt_ref, dgn_ref, ds_scr):
        @pl.when(pl.program_id(0) == 0)
        def _():
            ds_scr[...] = jnp.zeros_like(ds_scr)
            dal_ref[...] = jnp.zeros_like(dal_ref)
            ddt_ref[...] = jnp.zeros_like(ddt_ref)
            dgn_ref[...] = jnp.zeros_like(dgn_ref)

        per_head = tuple(tuple(r[:, _head(h)] for h in range(N_HEADS)) for r in (q_ref, k_ref, v_ref, g_ref))
        _, vjp = jax.vjp(_gdn_fn, *per_head, sm_ref[...], al_ref[...], dt_ref[...], gn_ref[...],
                         tuple(st_ref[h] for h in range(N_HEADS)))
        cts = (tuple(dy_ref[:, _head(h)] for h in range(N_HEADS)), tuple(ds_scr[h] for h in range(N_HEADS)))
        dqs, dks, dvs, dgs, dsm, dal, ddt, dgn, dss = vjp(cts)
        for h in range(N_HEADS):
            dq_ref[:, _head(h)] = dqs[h]
            dk_ref[:, _head(h)] = dks[h]
            dv_ref[:, _head(h)] = dvs[h]
            dg_ref[:, _head(h)] = dgs[h].astype(BF16)
            ds_scr[h] = dss[h]
        dsm_ref[...] = dsm
        dal_ref[...] += dal
        ddt_ref[...] += ddt
        dgn_ref[...] += dgn

    f = _sds((SEQ, GROUP), F32)
    pv = _sds((1, LANES), F32)
    return pl.pallas_call(
        body, out_shape=(f, f, f, _sds((SEQ, GROUP), BF16), _sds((SEQ, LANES), F32), pv, pv, pv), grid=(nc,),
        in_specs=[group(0), group(1), group(2), group(GDN_GATE_GROUP), sm, vec, vec, vec, state, group(1)],
        out_specs=(group(0), group(0), group(0), group(0), sm, vec, vec, vec), scratch_shapes=[pltpu.VMEM((N_HEADS, HEAD, HEAD), F32)],
        compiler_params=_params(), name=name,
    )(conv, conv, conv, p, small, a_log, dt_bias, gain, states, dy)


PACK_ROW_TILE = 1024


def adamw(w, g, m, v, *, name):
    r = w.shape[0]
    tr = _row_tile(r, LANES)

    def body(w_ref, g_ref, m_ref, v_ref, d_ref, nm_ref, nv_ref):
        gg = g_ref[...]
        nm = ADAM_B1 * m_ref[...] + (1.0 - ADAM_B1) * gg
        nv = ADAM_B2 * v_ref[...] + (1.0 - ADAM_B2) * jnp.square(gg)
        m_hat = nm / (1.0 - ADAM_B1 ** ADAM_STEP)
        v_hat = nv / (1.0 - ADAM_B2 ** ADAM_STEP)
        d_ref[...] = -ADAM_LR * (m_hat / (jnp.sqrt(v_hat) + ADAM_EPS) + ADAM_WD * w_ref[...])
        nm_ref[...] = nm
        nv_ref[...] = nv

    blk = pl.BlockSpec((tr, LANES), lambda i: (i, 0))
    o = _sds((r, LANES), F32)
    return pl.pallas_call(body, out_shape=(o, o, o), grid=(r // tr,), in_specs=[blk] * 4, out_specs=(blk, blk, blk),
                          compiler_params=_params(), name=name)(w, g, m, v)


ELEMENTWISE_BLOCK_BYTES = 2 * 1024 * 1024


def _row_tile(r, c):
    best = None
    for tr in range(8, r + 1, 8):
        if r % tr == 0 and tr * c * 4 <= ELEMENTWISE_BLOCK_BYTES:
            best = tr
    if best is None:
        raise ValueError(f"no row tile for ({r}, {c})")
    return best


def _tile_2d(r, c):
    if any(r % tr == 0 for tr in range(8, r + 1, 8)):
        return _row_tile(r, c), c
    tc = max(t for t in range(LANES, c + 1, LANES) if c % t == 0 and r * t * 4 <= ELEMENTWISE_BLOCK_BYTES)
    return r, tc


def _core_index():
    return lax.axis_index("c").astype(jnp.int32).reshape(1)


def _chip_index():
    return (2 * lax.axis_index("x") + lax.axis_index("y")).astype(jnp.int32).reshape(1)


def adamw_halves(w, m, v, g_own, g_sib, *, layer=0, prev=None, name):
    n_layers, rows, c = w.shape
    r = rows // 2
    tr = _row_tile(r, c)
    nb = r // tr

    def body(c_ref, w_ref, m_ref, v_ref, own_ref, sib_ref, *rest):
        g_ref, d_ref, nm_ref, nv_ref = rest[-4:]
        gg = jnp.where(pl.program_id(0) == c_ref[0], own_ref[...], sib_ref[...])
        nm = ADAM_B1 * m_ref[...] + (1.0 - ADAM_B1) * gg
        nv = ADAM_B2 * v_ref[...] + (1.0 - ADAM_B2) * jnp.square(gg)
        m_hat = nm / (1.0 - ADAM_B1 ** ADAM_STEP)
        v_hat = nv / (1.0 - ADAM_B2 ** ADAM_STEP)
        g_ref[...] = gg
        d_ref[...] = -ADAM_LR * (m_hat / (jnp.sqrt(v_hat) + ADAM_EPS) + ADAM_WD * w_ref[...])
        nm_ref[...] = nm
        nv_ref[...] = nv

    full = pl.BlockSpec((None, tr, c), lambda h, i, cr: (layer, h * nb + i, 0))
    half = pl.BlockSpec((tr, c), lambda h, i, cr: (i, 0))
    o = _sds((n_layers, rows, c), F32)
    prev = list(prev or ())
    gs = pltpu.PrefetchScalarGridSpec(num_scalar_prefetch=1, grid=(2, nb), in_specs=[full, full, full, half, half] + [_ANY] * len(prev),
                                      out_specs=(full, full, full, full))
    n_fixed = 6
    return pl.pallas_call(body, out_shape=(o, o, o, o), grid_spec=gs, compiler_params=_params(), name=name,
                          input_output_aliases={n_fixed + k: k for k in range(len(prev))})(
        _core_index(), w, m, v, g_own, g_sib, *prev)


ADAMW_COLUMN_TILE = 256


def adamw_column_halves(w, m, v, g_own, g_sib, *, name):
    _, rows, cols = w.shape
    tc = ADAMW_COLUMN_TILE
    per_half = cols // 2 // tc

    def body(c_ref, w_ref, m_ref, v_ref, own_ref, sib_ref, g_ref, d_ref, nm_ref, nv_ref):
        gg = jnp.where(pl.program_id(0) // per_half == c_ref[0], own_ref[...], sib_ref[...])
        nm = ADAM_B1 * m_ref[...] + (1.0 - ADAM_B1) * gg
        nv = ADAM_B2 * v_ref[...] + (1.0 - ADAM_B2) * jnp.square(gg)
        m_hat = nm / (1.0 - ADAM_B1 ** ADAM_STEP)
        v_hat = nv / (1.0 - ADAM_B2 ** ADAM_STEP)
        g_ref[...] = gg
        d_ref[...] = -ADAM_LR * (m_hat / (jnp.sqrt(v_hat) + ADAM_EPS) + ADAM_WD * w_ref[...])
        nm_ref[...] = nm
        nv_ref[...] = nv

    full = pl.BlockSpec((None, rows, tc), lambda j, cr: (0, 0, j))
    half = pl.BlockSpec((rows, tc), lambda j, cr: (0, j % per_half))
    o = _sds(w.shape, F32)
    gs = pltpu.PrefetchScalarGridSpec(num_scalar_prefetch=1, grid=(cols // tc,), in_specs=[full, full, full, half, half],
                                      out_specs=(full, full, full, full))
    return pl.pallas_call(body, out_shape=(o, o, o, o), grid_spec=gs, compiler_params=_params(), name=name)(
        _core_index(), w, m, v, g_own, g_sib)


def add_core_halves(g2, land, *, out_dtype, name):
    _, ns, r, cols = g2.shape
    tr, tc = _tile_2d(r, cols)

    def body(c_ref, a_ref, b_ref, o_ref):
        o_ref[...] = (a_ref[...] + b_ref[...]).astype(out_dtype)

    gs = pltpu.PrefetchScalarGridSpec(
        num_scalar_prefetch=1, grid=(ns, r // tr, cols // tc),
        in_specs=[pl.BlockSpec((None, None, tr, tc), lambda s, i, j, cr: (cr[0], s, i, j)),
                  pl.BlockSpec((None, tr, tc), lambda s, i, j, cr: (s, i, j))],
        out_specs=pl.BlockSpec((None, tr, tc), lambda s, i, j, cr: (s, i, j)))
    return pl.pallas_call(body, out_shape=_sds((ns, r, cols), out_dtype), grid_spec=gs, compiler_params=_params(), name=name)(
        _core_index(), g2, land)


def sum_over_chips(own, land, *, scatter, name):
    _, r, cols = own.shape
    tr, tc = _tile_2d(r, cols)

    def body(mine_ref, own_ref, l0, l1, l2, l3, o_ref):
        mine = mine_ref[0]
        mine_val = own_ref[...]
        acc = None
        for s, l_ref in enumerate((l0, l1, l2, l3)):
            val = jnp.where(mine == s, mine_val, l_ref[...]).astype(F32)
            acc = val if acc is None else acc + val
        o_ref[...] = acc

    def slot(s):
        return pl.BlockSpec((None, tr, tc), lambda i, j, mr: (jnp.where(mr[0] == s, (s + 1) % N_SHARD, s), i, j))

    own_spec = pl.BlockSpec((None, tr, tc), lambda i, j, mr: (mr[0] if scatter else 0, i, j))
    gs = pltpu.PrefetchScalarGridSpec(num_scalar_prefetch=1, grid=(r // tr, cols // tc), in_specs=[own_spec] + [slot(s) for s in range(N_SHARD)],
                                      out_specs=pl.BlockSpec((tr, tc), lambda i, j, mr: (i, j)))
    return pl.pallas_call(body, out_shape=_sds((r, cols), F32), grid_spec=gs, compiler_params=_params(), name=name)(
        _chip_index(), own, land, land, land, land)


_ANY = pl.BlockSpec(memory_space=pl.ANY)


def xy_exchange(src, *, scatter, name):
    rh = src.shape[1]

    def body(src_ref, land_ref, send_sems, recv_sems, loc_sem):
        x, y, c = lax.axis_index("x"), lax.axis_index("y"), lax.axis_index("c")
        mine = 2 * x + y
        peers = [(1 - x, y), (x, 1 - y), (1 - x, 1 - y)]

        def piece(shard):
            return src_ref.at[shard] if scatter else src_ref.at[c]

        def copy(k, px, py, dst_slot):
            return pltpu.make_async_remote_copy(src_ref=piece(2 * px + py), dst_ref=land_ref.at[dst_slot], send_sem=send_sems.at[k],
                                                recv_sem=recv_sems.at[k], device_id=(px, py, c), device_id_type=MESH)

        keep = pltpu.make_async_copy(piece(mine), land_ref.at[mine], loc_sem)
        keep.start()
        sends = [copy(k, px, py, mine) for k, (px, py) in enumerate(peers)]
        for cp in sends:
            cp.start()
        for cp in sends:
            cp.wait_send()
        for k, (px, py) in enumerate(peers):
            copy(k, px, py, 2 * px + py).wait_recv()
        keep.wait()

    return pl.pallas_call(body, out_shape=_sds((N_SHARD, rh, LANES), src.dtype), in_specs=[_ANY], out_specs=_ANY,
                          scratch_shapes=[pltpu.SemaphoreType.DMA((3,)), pltpu.SemaphoreType.DMA((3,)), pltpu.SemaphoreType.DMA(())],
                          name=name)(src)


def core_exchange(src, *, send_other_half, name):
    def body(src_ref, out_ref, send_sem, recv_sem, loc_sem):
        x, y, c = lax.axis_index("x"), lax.axis_index("y"), lax.axis_index("c")
        if send_other_half:
            cp = pltpu.make_async_remote_copy(src_ref=src_ref.at[1 - c], dst_ref=out_ref, send_sem=send_sem, recv_sem=recv_sem,
                                              device_id=(x, y, 1 - c), device_id_type=MESH)
            cp.start()
            cp.wait_send()
            cp.wait_recv()
        else:
            keep = pltpu.make_async_copy(src_ref, out_ref.at[c], loc_sem)
            keep.start()
            cp = pltpu.make_async_remote_copy(src_ref=src_ref, dst_ref=out_ref.at[c], send_sem=send_sem, recv_sem=recv_sem,
                                              device_id=(x, y, 1 - c), device_id_type=MESH)
            cp.start()
            cp.wait_send()
            pltpu.make_async_remote_copy(src_ref=src_ref, dst_ref=out_ref.at[1 - c], send_sem=send_sem, recv_sem=recv_sem,
                                         device_id=(x, y, 1 - c), device_id_type=MESH).wait_recv()
            keep.wait()

    out_shape = _sds(src.shape[1:], src.dtype) if send_other_half else _sds((2,) + src.shape, src.dtype)
    return pl.pallas_call(body, out_shape=out_shape, in_specs=[_ANY], out_specs=_ANY,
                          scratch_shapes=[pltpu.SemaphoreType.DMA(()), pltpu.SemaphoreType.DMA(()), pltpu.SemaphoreType.DMA(())],
                          name=name)(src)


def _comm_call(body, ins, out_shapes, sem_counts, name):
    return pl.pallas_call(body, out_shape=tuple(out_shapes), in_specs=[_ANY] * len(ins), out_specs=tuple([_ANY] * len(out_shapes)),
                          scratch_shapes=[pltpu.SemaphoreType.DMA((k,)) for k in sem_counts], name=name)(*ins)


def _sequencer_call(body, ins, out_shapes, sem_counts, name, collective_id):
    return pl.kernel(body, out_type=list(out_shapes), mesh=plsc.ScalarSubcoreMesh(axis_name="sequencer", num_cores=1), name=name,
                     scratch_types=[pltpu.SemaphoreType.DMA((k,)) for k in sem_counts],
                     compiler_params=pltpu.CompilerParams(collective_id=collective_id))(*ins)


def _handshake(peers):
    barrier = pltpu.get_barrier_semaphore()
    for peer in peers:
        pl.semaphore_signal(barrier, inc=1, device_id=peer, device_id_type=MESH)
    pl.semaphore_wait(barrier, len(peers))


def _xy_peers(x, y):
    return [(1 - x, y), (x, 1 - y), (1 - x, 1 - y)]


def gather_halves(halves, *, name, collective_id):
    n = len(halves)

    def body(*refs):
        ins, lands, sibs = refs[:n], refs[n:2 * n], refs[2 * n:3 * n]
        ici_send, ici_recv, d2d_send, d2d_recv = refs[3 * n:]
        x, y, c = lax.axis_index("x"), lax.axis_index("y"), lax.axis_index("c")
        mine = 2 * x + y
        peers = _xy_peers(x, y)
        _handshake([(px, py, c) for px, py in peers] + [(x, y, 1 - c)])

        def ici(i, k, slot):
            px, py = peers[k]
            return pltpu.make_async_remote_copy(src_ref=ins[i].at[c], dst_ref=lands[i].at[slot], send_sem=ici_send.at[3 * i + k],
                                                recv_sem=ici_recv.at[3 * i + k], device_id=(px, py, c), device_id_type=MESH)

        def pass_on(i, k):
            px, py = peers[k]
            slot = 2 * px + py
            return pltpu.make_async_remote_copy(src_ref=lands[i].at[slot], dst_ref=sibs[i].at[slot], send_sem=d2d_send.at[3 * i + k],
                                                recv_sem=d2d_recv.at[3 * i + k], device_id=(x, y, 1 - c), device_id_type=MESH)

        sends = [ici(i, k, mine) for i in range(n) for k in range(3)]
        for cp in sends:
            cp.start()
        passed = []
        for i in range(n):
            for k in range(3):
                px, py = peers[k]
                ici(i, k, 2 * px + py).wait_recv()
                cp = pass_on(i, k)
                cp.start()
                passed.append(cp)
        for cp in passed:
            cp.wait_recv()
        for cp in sends + passed:
            cp.wait_send()

    outs = [_sds((N_SHARD,) + h.shape[1:], h.dtype) for h in halves]
    res = _sequencer_call(body, halves, outs + outs, [3 * n] * 4, name, collective_id)
    return res[:n], res[n:]


def send_other_half(arrays, *, name, collective_id):
    n = len(arrays)

    def body(*refs):
        ins, lands = refs[:n], refs[n:2 * n]
        send_sems, recv_sems = refs[2 * n:]
        x, y, c = lax.axis_index("x"), lax.axis_index("y"), lax.axis_index("c")
        _handshake([(x, y, 1 - c)])
        copies = [pltpu.make_async_remote_copy(src_ref=ins[i].at[1 - c], dst_ref=lands[i], send_sem=send_sems.at[i],
                                               recv_sem=recv_sems.at[i], device_id=(x, y, 1 - c), device_id_type=MESH) for i in range(n)]
        for cp in copies:
            cp.start()
        for cp in copies:
            cp.wait_recv()
        for cp in copies:
            cp.wait_send()

    return _sequencer_call(body, arrays, [_sds(a.shape[1:], a.dtype) for a in arrays], [n, n], name, collective_id)


_HBM = pl.BlockSpec(memory_space=pltpu.HBM)
_SEM = pl.BlockSpec(memory_space=pltpu.SEMAPHORE)
_SPLIT_COPY = dict(has_side_effects=pltpu.SideEffectType.DATAFLOW_SIDE_EFFECTING)


def _chip_copy(ins, lands, send_sems, recv_sems, scatter, i, k, receive):
    x, y, c = lax.axis_index("x"), lax.axis_index("y"), lax.axis_index("c")
    px, py = _xy_peers(x, y)[k]
    theirs, mine = 2 * px + py, 2 * x + y
    src = ins[i].at[theirs] if scatter[i] else ins[i].at[0]
    return pltpu.make_async_remote_copy(src_ref=src, dst_ref=lands[i].at[theirs if receive else mine], send_sem=send_sems.at[3 * i + k],
                                        recv_sem=recv_sems.at[3 * i + k], device_id=(px, py, c), device_id_type=MESH)


def send_to_chips_start(arrays, scatter, *, name):
    n = len(arrays)

    def body(*refs):
        send_sems, recv_sems = refs[2 * n], refs[2 * n + 1]
        ins, lands = refs[2 * n + 2:3 * n + 2], refs[3 * n + 2:4 * n + 2]
        token = refs[4 * n + 2]
        for i in range(n):
            for k in range(3):
                _chip_copy(ins, lands, send_sems, recv_sems, scatter, i, k, receive=False).start()
        token[...] = jnp.zeros_like(token)

    land_shapes = [(N_SHARD,) + a.shape[1:] for a in arrays]
    operands = [pltpu.with_memory_space_constraint(a, pltpu.HBM) for a in arrays]
    operands += [pltpu.with_memory_space_constraint(lax.empty(s, a.dtype), pltpu.HBM) for s, a in zip(land_shapes, arrays)]
    out_shape = ([pltpu.SemaphoreType.DMA((3 * n,)), pltpu.SemaphoreType.DMA((3 * n,))] + [pltpu.HBM(a.shape, a.dtype) for a in arrays]
                 + [pltpu.HBM(s, a.dtype) for s, a in zip(land_shapes, arrays)] + [_sds((8, LANES), F32)])
    res = pl.pallas_call(body, name=name, out_shape=out_shape, in_specs=[_HBM] * (2 * n),
                         out_specs=[_SEM, _SEM] + [_HBM] * (2 * n) + [pl.BlockSpec(memory_space=pltpu.VMEM)],
                         input_output_aliases={i: 2 + i for i in range(2 * n)}, compiler_params=pltpu.CompilerParams(**_SPLIT_COPY))(*operands)
    return (res[0], res[1], res[2:2 + n], res[2 + n:2 + 2 * n], scatter), res[-1]


def send_to_chips_wait(state, after, *, name):
    send_sems, recv_sems, arrays, lands, scatter = state
    n = len(arrays)

    def body(*refs):
        ins, landing = refs[:n], refs[n:2 * n]
        send_sems, recv_sems = refs[2 * n], refs[2 * n + 1]
        for i in range(n):
            for k in range(3):
                _chip_copy(ins, landing, send_sems, recv_sems, scatter, i, k, receive=True).wait_recv()
        for i in range(n):
            for k in range(3):
                _chip_copy(ins, landing, send_sems, recv_sems, scatter, i, k, receive=False).wait_send()

    out_shape = [pltpu.HBM(a.shape, a.dtype) for a in list(arrays) + list(lands)]
    res = pl.pallas_call(body, name=name, out_shape=out_shape, in_specs=[_HBM] * (2 * n) + [_SEM, _SEM] + [_ANY] * len(after),
                         out_specs=[_HBM] * (2 * n), input_output_aliases={i: i for i in range(2 * n)},
                         compiler_params=pltpu.CompilerParams(**_SPLIT_COPY))(*arrays, *lands, send_sems, recv_sems, *after)
    return res[:n], res[n:]


def swap_with_other_core(arrays, *, name, collective_id):
    n = len(arrays)

    def body(*refs):
        ins, lands = refs[:n], refs[n:2 * n]
        send_sems, recv_sems = refs[2 * n:]
        x, y, c = lax.axis_index("x"), lax.axis_index("y"), lax.axis_index("c")
        _handshake([(x, y, 1 - c)])
        copies = [pltpu.make_async_remote_copy(src_ref=ins[i], dst_ref=lands[i], send_sem=send_sems.at[i], recv_sem=recv_sems.at[i],
                                               device_id=(x, y, 1 - c), device_id_type=MESH) for i in range(n)]
        for cp in copies:
            cp.start()
        for cp in copies:
            cp.wait_recv()
        for cp in copies:
            cp.wait_send()

    return _sequencer_call(body, arrays, [_sds(a.shape, a.dtype) for a in arrays], [n, n], name, collective_id)


def _pack_rows(n_elems, row_multiple):
    rows = -(-n_elems // LANES)
    return -(-rows // row_multiple) * row_multiple


def _pack(arrays, rows, dtype):
    flat = jnp.concatenate([a.reshape(-1).astype(dtype) for a in arrays])
    return jnp.pad(flat, (0, rows * LANES - flat.shape[0])).reshape(rows, LANES)


def _unpack(packed, shapes):
    flat = packed.reshape(-1)
    out, off = [], 0
    for s in shapes:
        n = int(np.prod(s))
        out.append(flat[off:off + n].reshape(s))
        off += n
    return out


def all_gather_shards(shards, axes, dtype, row_multiple, tag):
    shapes = [s.shape for s in shards]
    rows = _pack_rows(sum(int(np.prod(s)) for s in shapes), row_multiple)
    packed = _pack(shards, rows, dtype).reshape(2, rows // 2, LANES)
    land = xy_exchange(packed, scatter=False, name=f"gather_xy_{tag}")
    both = core_exchange(land, send_other_half=False, name=f"gather_c_{tag}")
    per_shard = jnp.swapaxes(both, 0, 1).reshape(N_SHARD, rows, LANES)
    pieces = [_unpack(per_shard[s], shapes) for s in range(N_SHARD)]
    return [jnp.concatenate([pieces[s][i] for s in range(N_SHARD)], axis=ax) for i, ax in enumerate(axes)]


def _ordered_before(first, then):
    if then is None:
        return first, None
    return lax.optimization_barrier((first, then))


def reduce_between_cores(arrays, scatter, *, tag, collective_id, before=None):
    arrays, before = _ordered_before(arrays, before)
    land = send_other_half(arrays, name=f"reduce_core_send_{tag}", collective_id=collective_id)
    return (arrays, land, scatter, tag, collective_id), before


def reduce_between_chips(state, before=None):
    arrays, land, scatter, tag, collective_id = state
    chip = [add_core_halves(a, l, out_dtype=BF16 if sc else F32, name=f"reduce_core_add_{tag}_{i}")
            for i, (a, l, sc) in enumerate(zip(arrays, land, scatter))]
    sending, token = send_to_chips_start(chip, scatter, name=f"reduce_chip_start_{tag}")
    token, before = _ordered_before(token, before)
    return (sending, token, scatter, tag, collective_id), before


def reduce_finish(state, after):
    sending, token, scatter, tag, collective_id = state
    chip, land = send_to_chips_wait(sending, tuple(after) + (token,), name=f"reduce_chip_wait_{tag}")
    own = [sum_over_chips(ch, l, scatter=sc, name=f"reduce_chip_add_{tag}_{i}") for i, (ch, l, sc) in enumerate(zip(chip, land, scatter))]
    sib = swap_with_other_core(own, name=f"reduce_core_swap_{tag}", collective_id=collective_id + 2)
    return own, sib


def _ffn_layer_fwd(h, norm_g, w_up, cw, cb, w_down, tag):
    u, hn = norm_matmul(h, norm_g, w_up, name=f"ffn_up_{tag}")
    act = ffn_act_fwd(u, cw, cb, name=f"ffn_act_{tag}")
    out = matmul(act, w_down, add=h, name=f"ffn_down_{tag}")
    return out, (h, hn, u, act)


def _travel_layout(array):
    return BIG_ARRAYS[array][3], BIG_ARRAYS[array][4]


def _ffn_layer_bwd(saved, dout, norm_g, w_up, cw, cb, w_down, tag, d_w_down_other=None):
    h, hn, u, act = saved
    dact = matmul(dout, w_down, tb=True, name=f"ffn_down_dx_{tag}")
    d_w_down = matmul(act, dout, ta=True, layer=(int(tag), 2, d_w_down_other), name=f"ffn_down_dw_{tag}")
    du, dcw, dcb = ffn_act_bwd(u, cw, cb, dact, name=f"ffn_act_bwd_{tag}")
    dhn = matmul(du, w_up, tb=True, name=f"ffn_up_dx_{tag}")
    d_w_up = matmul(hn, du, ta=True, split=_travel_layout(f"ffn_w_up_{tag}"), name=f"ffn_up_dw_{tag}")
    dh, dg = norm_bwd(h, norm_g, dhn, dout, name=f"ffn_norm_bwd_{tag}")
    return dh, dg, d_w_up, dcw, dcb, d_w_down


def local_step(x, target, w, stage=lambda name, tensors, grads=None: tensors):
    g = {}
    tables = _ret_tables()
    x = stage("start", x)
    w_in_t = w["ret_gdn_w_in"]
    w_main = w_in_t[:MIX_MAIN]
    w_small = jnp.pad(w_in_t[MIX_MAIN:], ((0, LANES - 2 * N_HEADS), (0, 0)))
    a_log = jnp.pad(w["gdn_a_log"], ((0, 0), (0, LANES - N_HEADS)))
    dt_bias = jnp.pad(w["gdn_dt_bias"], ((0, 0), (0, LANES - N_HEADS)))

    p, hn0 = norm_matmul(x, w["norm_mix"][0:1], w_main, tb=True, name="mix0_in")
    hn0 = stage("normed", hn0)
    small = matmul(hn0, w_small, tb=True, name="mix0_in_small")
    y_ret, s_ret = ret_fwd(p, tables, name="ret_fwd")
    conv = gdn_conv_fwd(p, w["gdn_conv_w"], name="gdn_conv")
    y_gdn, s_gdn = gdn_fwd(conv, p, small, a_log, dt_bias, w["gdn_out_gain"], name="gdn_fwd")
    y0 = stage("mixed", jnp.concatenate([y_ret, y_gdn], axis=1))
    h1 = matmul(y0, w["ret_gdn_w_out"], add=x, name="mix0_out")
    h2, ffn0 = _ffn_layer_fwd(h1, w["norm_ffn"][0:1], w["ffn_w_up"][0], w["ffn_conv_w"][0], w["ffn_conv_b"][0:1], w["ffn_w_down"][0], "0")
    h2 = stage("layer0", h2)

    gx, hn1 = norm_matmul(h2, w["norm_mix"][1:2], w["lru_w_in"], name="mix1_in")
    lru_p = (w["lru_conv_w"], w["lru_conv_b"], w["lru_w_a"], w["lru_b_a"], w["lru_w_x"], w["lru_b_x"], w["lru_lambda"])
    y1 = lru_fwd(gx, *lru_p, name="lru_fwd")
    h3 = matmul(y1, w["lru_w_out"], add=h2, name="mix1_out")
    h4, ffn1 = _ffn_layer_fwd(h3, w["norm_ffn"][1:2], w["ffn_w_up"][1], w["ffn_conv_w"][1], w["ffn_conv_b"][1:2], w["ffn_w_down"][1], "1")

    loss, dh4, g["norm_final"] = final_fwd_bwd(h4, w["norm_final"], target, name="final")

    dh3, dgf1, dwu1, dcw1, dcb1, dwd1 = _ffn_layer_bwd(ffn1, dh4, w["norm_ffn"][1:2], w["ffn_w_up"][1], w["ffn_conv_w"][1],
                                                     w["ffn_conv_b"][1:2], w["ffn_w_down"][1], "1")
    g["ffn_w_up_1"] = dwu1
    dh3 = stage("grads0_ready", dh3, g)
    dy1 = matmul(dh3, w["lru_w_out"], tb=True, name="mix1_out_dx")
    g["lru_w_out"] = matmul(y1, dh3, ta=True, split=_travel_layout("lru_w_out"), name="mix1_out_dw")
    dgx, g["lru_conv_w"], g["lru_conv_b"], g["lru_w_a"], g["lru_b_a"], g["lru_w_x"], g["lru_b_x"], g["lru_lambda"] = lru_bwd(
        gx, *lru_p, dy1, name="lru_bwd")
    dgx = stage("grads0_send", dgx, g)
    dhn1 = matmul(dgx, w["lru_w_in"], tb=True, name="mix1_in_dx")
    g["lru_w_in"] = matmul(hn1, dgx, ta=True, split=_travel_layout("lru_w_in"), name="mix1_in_dw")
    dh2, dgm1 = norm_bwd(h2, w["norm_mix"][1:2], dhn1, dh3, name="mix1_norm_bwd")
    dh2 = stage("grads1_ready", dh2, g)

    dh1, dgf0, dwu0, dcw0, dcb0, dwd0 = _ffn_layer_bwd(ffn0, dh2, w["norm_ffn"][0:1], w["ffn_w_up"][0], w["ffn_conv_w"][0],
                                                     w["ffn_conv_b"][0:1], w["ffn_w_down"][0], "0", dwd1)
    g["ffn_w_up_0"] = dwu0
    g["ffn_w_down"] = dwd0
    dh1 = stage("grads2_ready", stage("grads1_send", dh1, g), g)
    dy0 = matmul(dh1, w["ret_gdn_w_out"], tb=True, name="mix0_out_dx")
    g["ret_gdn_w_out"] = matmul(y0, dh1, ta=True, split=_travel_layout("ret_gdn_w_out"), name="mix0_out_dw")
    dq_r, dk_r, dv_r, dg_r = ret_bwd(p, tables, s_ret, dy0, name="ret_bwd")
    dy0, dq_r = stage("grads2_send", (dy0, dq_r), g)
    dcq, dck, dcv, dg_d, dsmall, dal, ddt, dgain = gdn_bwd(conv, p, small, a_log, dt_bias, w["gdn_out_gain"], s_gdn, dy0, name="gdn_bwd")
    dconv = jnp.concatenate([dcq, dck, dcv], axis=1)
    dp_conv, g["gdn_conv_w"] = gdn_conv_bwd(p, w["gdn_conv_w"], dconv, name="gdn_conv_bwd")
    dp = jnp.concatenate([dq_r, dk_r, dv_r, dg_r, dp_conv, dg_d], axis=1)
    dhn0 = matmul(dp, w_main, name="mix0_in_dx")
    dhn0 = matmul(dsmall, w_small, add=dhn0, name="mix0_in_small_dx")
    d_w_main = matmul(dp, hn0, ta=True, name="mix0_in_dw")
    d_w_small = matmul(dsmall, hn0, ta=True, name="mix0_in_small_dw")
    g["ret_gdn_w_in"] = jnp.concatenate([d_w_main, d_w_small[:2 * N_HEADS]], axis=0)
    dx, dgm0 = norm_bwd(x, w["norm_mix"][0:1], dhn0, dh1, name="mix0_norm_bwd")

    g["gdn_a_log"] = dal[:, :N_HEADS]
    g["gdn_dt_bias"] = ddt[:, :N_HEADS]
    g["gdn_out_gain"] = dgain
    g["norm_mix"] = jnp.concatenate([dgm0, dgm1], axis=0)
    g["norm_ffn"] = jnp.concatenate([dgf0, dgf1], axis=0)
    g["ffn_conv_w"] = jnp.stack([dcw0, dcw1])
    g["ffn_conv_b"] = jnp.concatenate([dcb0, dcb1], axis=0)
    return loss, dx, g


WEIGHTS = ("norm_mix", "norm_ffn", "ret_gdn_w_in", "gdn_conv_w", "gdn_a_log", "gdn_dt_bias", "gdn_out_gain", "ret_gdn_w_out",
           "lru_w_in", "lru_conv_w", "lru_conv_b", "lru_w_a", "lru_b_a", "lru_w_x", "lru_b_x", "lru_lambda", "lru_w_out",
           "ffn_w_up", "ffn_conv_w", "ffn_conv_b", "ffn_w_down", "norm_final")
MATMUL_SHARDED = {"ret_gdn_w_in": 1, "ret_gdn_w_out": 0, "lru_w_in": 1, "lru_w_out": 0, "ffn_w_up": 2, "ffn_w_down": 1}
VECTOR_SHARDED = {"gdn_conv_w": 1, "lru_conv_w": 1, "lru_conv_b": 1, "lru_b_a": 1, "lru_b_x": 1, "lru_lambda": 1, "ffn_conv_w": 2}
SHARDED = {**MATMUL_SHARDED, **VECTOR_SHARDED}
REPLICATED = tuple(n for n in WEIGHTS if n not in SHARDED)
SQUEEZE = {"ret_gdn_w_in", "gdn_conv_w", "ret_gdn_w_out", "lru_w_in", "lru_conv_w", "lru_w_a", "lru_w_x", "lru_w_out"}
MIX_IN = MIX_MAIN + 2 * N_HEADS
BIG_ARRAYS = {
    "ret_gdn_w_in": ("ret_gdn_w_in", None, (MIX_IN, D_MODEL), (N_SHARD, MIX_IN // N_SHARD, 2, D_MODEL // 2), (2, 0, 1, 3)),
    "ret_gdn_w_out": ("ret_gdn_w_out", None, (2 * GROUP, D_MODEL), (N_SHARD, 2, GROUP // N_SHARD, D_MODEL), (1, 0, 2, 3)),
    "lru_w_in": ("lru_w_in", None, (D_MODEL, 2 * D_MODEL), (2, D_MODEL // 2, N_SHARD, 2 * D_MODEL // N_SHARD), (0, 2, 1, 3)),
    "lru_w_out": ("lru_w_out", None, (D_MODEL, D_MODEL), (N_SHARD, 2, D_MODEL // (2 * N_SHARD), D_MODEL), (1, 0, 2, 3)),
    "ffn_w_up_0": ("ffn_w_up", 0, (D_MODEL, 2 * D_FF), (2, D_MODEL // 2, N_SHARD, 2 * D_FF // N_SHARD), (0, 2, 1, 3)),
    "ffn_w_up_1": ("ffn_w_up", 1, (D_MODEL, 2 * D_FF), (2, D_MODEL // 2, N_SHARD, 2 * D_FF // N_SHARD), (0, 2, 1, 3)),
    "ffn_w_down": ("ffn_w_down", None, (2, D_FF, D_MODEL), (2, N_SHARD, D_FF // N_SHARD, D_MODEL), (0, 1, 2, 3)),
}
GATHER_GROUPS = (("ret_gdn_w_in",), ("ret_gdn_w_out", "ffn_w_up_0", "ffn_w_down"), ("lru_w_in", "lru_w_out", "ffn_w_up_1"))
REDUCE_GROUPS = (("ffn_w_up_1",), ("lru_w_in", "lru_w_out"), ("ffn_w_up_0", "ffn_w_down"), ("ret_gdn_w_out", "ret_gdn_w_in"))
BLOCK_WEIGHTS = ("lru_w_a", "lru_w_x")
GATHER_COLLECTIVE_ID = 1
REDUCE_COLLECTIVE_ID = GATHER_COLLECTIVE_ID + len(GATHER_GROUPS)


TRANSPOSED = ("ret_gdn_w_in",)


def _shard_of(array, tensors):
    weight, layer = BIG_ARRAYS[array][:2]
    t = tensors[weight]
    if weight in TRANSPOSED:
        return jnp.swapaxes(t, 1, 2)[0]
    return _local_view(weight, t) if layer is None else t[layer]


def _core_halves(array, shard):
    _, _, _, split, perm = BIG_ARRAYS[array]
    kept = [k for k in range(4) if k != perm[1]]
    order = [kept.index(perm[0]), kept.index(perm[2]), kept.index(perm[3])]
    return shard.reshape([split[k] for k in kept]).transpose(order)


def _local_view(name, a):
    if name in SQUEEZE:
        return a[0]
    if a.ndim == 1:
        return a[None, :]
    return a


def kernel(x, norm_mix, norm_ffn, ret_gdn_w_in, gdn_conv_w, gdn_a_log, gdn_dt_bias, gdn_out_gain, ret_gdn_w_out, lru_w_in, lru_conv_w, lru_conv_b, lru_w_a, lru_b_a, lru_w_x, lru_b_x, lru_lambda, lru_w_out, ffn_w_up, ffn_conv_w, ffn_conv_b, ffn_w_down, norm_final, loss_target, m_norm_mix, m_norm_ffn, m_ret_gdn_w_in, m_gdn_conv_w, m_gdn_a_log, m_gdn_dt_bias, m_gdn_out_gain, m_ret_gdn_w_out, m_lru_w_in, m_lru_conv_w, m_lru_conv_b, m_lru_w_a, m_lru_b_a, m_lru_w_x, m_lru_b_x, m_lru_lambda, m_lru_w_out, m_ffn_w_up, m_ffn_conv_w, m_ffn_conv_b, m_ffn_w_down, m_norm_final, v_norm_mix, v_norm_ffn, v_ret_gdn_w_in, v_gdn_conv_w, v_gdn_a_log, v_gdn_dt_bias, v_gdn_out_gain, v_ret_gdn_w_out, v_lru_w_in, v_lru_conv_w, v_lru_conv_b, v_lru_w_a, v_lru_b_a, v_lru_w_x, v_lru_b_x, v_lru_lambda, v_lru_w_out, v_ffn_w_up, v_ffn_conv_w, v_ffn_conv_b, v_ffn_w_down, v_norm_final):
    given = dict(norm_mix=norm_mix, norm_ffn=norm_ffn, ret_gdn_w_in=ret_gdn_w_in, gdn_conv_w=gdn_conv_w, gdn_a_log=gdn_a_log, gdn_dt_bias=gdn_dt_bias, gdn_out_gain=gdn_out_gain, ret_gdn_w_out=ret_gdn_w_out, lru_w_in=lru_w_in, lru_conv_w=lru_conv_w, lru_conv_b=lru_conv_b, lru_w_a=lru_w_a, lru_b_a=lru_b_a, lru_w_x=lru_w_x, lru_b_x=lru_b_x, lru_lambda=lru_lambda, lru_w_out=lru_w_out, ffn_w_up=ffn_w_up, ffn_conv_w=ffn_conv_w, ffn_conv_b=ffn_conv_b, ffn_w_down=ffn_w_down, norm_final=norm_final)
    mom1 = dict(norm_mix=m_norm_mix, norm_ffn=m_norm_ffn, ret_gdn_w_in=m_ret_gdn_w_in, gdn_conv_w=m_gdn_conv_w, gdn_a_log=m_gdn_a_log, gdn_dt_bias=m_gdn_dt_bias, gdn_out_gain=m_gdn_out_gain, ret_gdn_w_out=m_ret_gdn_w_out, lru_w_in=m_lru_w_in, lru_conv_w=m_lru_conv_w, lru_conv_b=m_lru_conv_b, lru_w_a=m_lru_w_a, lru_b_a=m_lru_b_a, lru_w_x=m_lru_w_x, lru_b_x=m_lru_b_x, lru_lambda=m_lru_lambda, lru_w_out=m_lru_w_out, ffn_w_up=m_ffn_w_up, ffn_conv_w=m_ffn_conv_w, ffn_conv_b=m_ffn_conv_b, ffn_w_down=m_ffn_w_down, norm_final=m_norm_final)
    mom2 = dict(norm_mix=v_norm_mix, norm_ffn=v_norm_ffn, ret_gdn_w_in=v_ret_gdn_w_in, gdn_conv_w=v_gdn_conv_w, gdn_a_log=v_gdn_a_log, gdn_dt_bias=v_gdn_dt_bias, gdn_out_gain=v_gdn_out_gain, ret_gdn_w_out=v_ret_gdn_w_out, lru_w_in=v_lru_w_in, lru_conv_w=v_lru_conv_w, lru_conv_b=v_lru_conv_b, lru_w_a=v_lru_w_a, lru_b_a=v_lru_b_a, lru_w_x=v_lru_w_x, lru_b_x=v_lru_b_x, lru_lambda=v_lru_lambda, lru_w_out=v_lru_w_out, ffn_w_up=v_ffn_w_up, ffn_conv_w=v_ffn_conv_w, ffn_conv_b=v_ffn_conv_b, ffn_w_down=v_ffn_w_down, norm_final=v_norm_final)

    local = {n: _local_view(n, a) for n, a in given.items()}

    core = lax.axis_index("c")
    chip = 2 * lax.axis_index("x") + lax.axis_index("y")
    is_my_chip = lax.broadcasted_iota(jnp.int32, (N_SHARD, 1, 1), 0) == chip

    def by_core(mine, other):
        return jnp.where(core == 0, jnp.stack([mine, other]), jnp.stack([other, mine]))

    vec_names, rp_names = list(VECTOR_SHARDED), list(REPLICATED)
    full = dict(zip(vec_names, all_gather_shards([local[n] for n in vec_names], [SHARDED[n] for n in vec_names], F32, 32, "p")))
    for n in rp_names:
        full[n] = local[n]
    in_flight = {}

    bf16_halves = {}

    def cast_halves(gi):
        if gi not in bf16_halves:
            bf16_halves[gi] = [_core_halves(a, _shard_of(a, given).astype(BF16)) for a in GATHER_GROUPS[gi]]
        return bf16_halves[gi]

    def launch(gi, after=None):
        halves = cast_halves(gi)
        if after is not None:
            halves, after = lax.optimization_barrier((halves, after))
        in_flight[gi] = (halves,) + gather_halves(halves, name=f"gather_weights_{gi}", collective_id=GATHER_COLLECTIVE_ID + gi)
        return after

    def land(gi, after):
        halves, lands, sibs = in_flight[gi]
        (lands, sibs), after = lax.optimization_barrier(((lands, sibs), after))
        for a, mine, got, passed in zip(GATHER_GROUPS[gi], halves, lands, sibs):
            weight, layer, full_shape, split, perm = BIG_ARRAYS[a]
            half_mine = jnp.where(is_my_chip, jnp.where(core == 0, mine[0], mine[1])[None], got)
            half_other = jnp.where(is_my_chip, jnp.where(core == 0, mine[1], mine[0])[None], passed)
            value = by_core(half_mine, half_other).transpose(tuple(np.argsort(perm))).reshape(full_shape)
            if layer is None:
                full[weight] = value
            else:
                full.setdefault(weight, [None, None])[layer] = value
        return after

    reducing = {}

    def reduce_ready(gi, grads, then=None, extra=()):
        def travelling(a):
            split, perm = _travel_layout(a)
            return grads[a] if grads[a].ndim == 4 else grads[a].reshape(split).transpose(perm)

        arrays = [travelling(a) for a in REDUCE_GROUPS[gi]] + list(extra)
        scatter = [True] * len(REDUCE_GROUPS[gi]) + [False] * len(extra)
        reducing[gi], then = reduce_between_cores(arrays, scatter, tag=str(gi), collective_id=REDUCE_COLLECTIVE_ID + 3 * gi, before=then)
        return then

    def reduce_send(gi, then=None):
        reducing[gi], then = reduce_between_chips(reducing[gi], before=then)
        return then

    def stage(name, tensors, grads=None):
        if name == "start":
            launch(0)
            launch(1)
            fillers = (packed["wmv"], cast_halves(2), [full[n] for n in vec_names])
            (packed["wmv"], bf16_halves[2], gathered_small), tensors = lax.optimization_barrier((fillers, tensors))
            full.update(zip(vec_names, gathered_small))
            return land(0, tensors)
        if name == "normed":
            return launch(2, tensors)
        if name in ("mixed", "layer0"):
            return land({"mixed": 1, "layer0": 2}[name], tensors)
        gi = int(name[len("grads")])
        return reduce_ready(gi, grads, tensors) if name.endswith("_ready") else reduce_send(gi, tensors)

    small_names = [n for n in rp_names if n not in BLOCK_WEIGHTS] + vec_names
    loc_shapes = [local[n].shape for n in small_names]
    loc_rows = _pack_rows(sum(int(np.prod(s)) for s in loc_shapes), 256)
    packed = {"wmv": [_pack([src[n] for n in small_names], loc_rows, F32) for src in (given, mom1, mom2)]}

    loss_part, dx, grads = local_step(x[0], loss_target[0], full, stage)
    small_shapes = [grads[n].shape for n in small_names] + [(1, 1)]
    small_rows = _pack_rows(sum(int(np.prod(s)) for s in small_shapes), 16)
    small = _pack([grads[n] for n in small_names] + [loss_part[:, :1]], small_rows, F32).reshape(2, 1, small_rows // 2, LANES)
    last = len(REDUCE_GROUPS) - 1
    halves_of_blocks = [grads[n].reshape(2, 1, LRU_BLOCKS * HEAD // 2, HEAD) for n in BLOCK_WEIGHTS]
    reduce_ready(last, grads, extra=[small] + halves_of_blocks)
    reduce_send(last)
    reduced, result = {}, {}

    def finish(gi, after):
        g_own, g_sib = reduce_finish(reducing[gi], after)
        reduced.update(zip(list(REDUCE_GROUPS[gi]) + ["small"] + list(BLOCK_WEIGHTS), zip(g_own, g_sib)))

    def update(n):
        if n in TRANSPOSED:
            w3, m3, v3 = (jnp.swapaxes(t, 1, 2) for t in (given[n], mom1[n], mom2[n]))
            result[n] = tuple(jnp.swapaxes(t, 1, 2) for t in adamw_column_halves(w3, m3, v3, *reduced[n], name=f"adamw_{n}"))
            return
        done = None
        for a in (k for k, spec in BIG_ARRAYS.items() if spec[0] == n):
            r, cols = reduced[a][0].shape
            layer = BIG_ARRAYS[a][1] or 0
            w3, m3, v3 = (t if BIG_ARRAYS[a][1] is not None else t.reshape(1, 2 * r, cols) for t in (given[n], mom1[n], mom2[n]))
            done = adamw_halves(w3, m3, v3, *reduced[a], layer=layer, prev=done, name=f"adamw_{a}")
        result[n] = done

    for gi in range(last):
        finish(gi, (dx, reducing[last][1]))
    late = {BIG_ARRAYS[a][0] for a in REDUCE_GROUPS[last]}
    for n in MATMUL_SHARDED:
        if n not in late:
            update(n)
    finish(last, tuple(result[n][0] for n in MATMUL_SHARDED if n not in late))
    for n in MATMUL_SHARDED:
        if n in late:
            update(n)

    for n in BLOCK_WEIGHTS:
        w3, m3, v3 = (t.reshape(1, LRU_BLOCKS * HEAD, HEAD) for t in (given[n], mom1[n], mom2[n]))
        result[n] = adamw_halves(w3, m3, v3, *reduced[n], name=f"adamw_{n}")

    *small_sums, loss_sum = _unpack(by_core(*reduced["small"]).reshape(small_rows, LANES), small_shapes)
    loss = loss_sum[0, 0]
    g_small = dict(zip(small_names, small_sums))
    for n in vec_names:
        size = local[n].shape[SHARDED[n]]
        g_small[n] = lax.dynamic_slice_in_dim(g_small[n], chip * size, size, axis=SHARDED[n])
    w_pack, m_pack, v_pack = packed["wmv"]
    d_s, m_s, v_s = adamw(w_pack, _pack([g_small[n] for n in small_names], loc_rows, F32), m_pack, v_pack, name="adamw_small")
    for n, d, nm, nv in zip(small_names, _unpack(d_s, loc_shapes), _unpack(m_s, loc_shapes), _unpack(v_s, loc_shapes)):
        result[n] = (g_small[n], d, nm, nv)

    outs = [[result[n][k].reshape(given[n].shape) for n in WEIGHTS] for k in range(4)]
    return (loss, dx[None], *outs[0], *outs[1], *outs[2], *outs[3])
```

```python
import functools

import numpy as np
import jax
import jax.numpy as jnp
from jax import lax
from jax.experimental import pallas as pl
from jax.experimental.pallas import tpu as pltpu
from jax.experimental.pallas import tpu_sc as plsc

F32 = jnp.float32
BF16 = jnp.bfloat16
HI = lax.Precision.HIGHEST
MESH = pl.DeviceIdType.MESH

SEQ = 2048
D_MODEL = 1024
N_HEADS = 4
HEAD = 128
RET_CHUNK = 128
RET_CHUNKS_PER_STEP = 2
GDN_CHUNK = 64
GDN_CHUNKS_PER_STEP = 8
GROUP = N_HEADS * HEAD
MIX_MAIN = 8 * GROUP
D_FF = 2816
LRU_BLOCKS = 8
LRU_C = 8.0
ROPE_BASE = 10000.0
EPS = 1e-6
N_SHARD = 4
LANES = 128

ADAM_LR, ADAM_B1, ADAM_B2, ADAM_EPS, ADAM_WD, ADAM_STEP = 0.001, 0.9, 0.999, 1e-08, 0.01, 10

VMEM_LIMIT_BYTES = 56 * 1024 * 1024

_roll = pltpu.roll


def _params(**kw):
    return pltpu.CompilerParams(vmem_limit_bytes=VMEM_LIMIT_BYTES, **kw)


def _sds(shape, dtype):
    return jax.ShapeDtypeStruct(tuple(shape), dtype)


def _shift_raw(x, d):
    n = x.shape[0]
    t = lax.broadcasted_iota(jnp.int32, x.shape, 0)
    if d > 0:
        return jnp.where(t >= d, _roll(x, d, 0), 0.0)
    return jnp.where(t < n + d, _roll(x, n + d, 0), 0.0)


@functools.partial(jax.custom_vjp, nondiff_argnums=(1,))
def shift_rows(x, d):
    return _shift_raw(x, d)


def _shift_fwd(x, d):
    return _shift_raw(x, d), None


def _shift_bwd(d, _, g):
    return (_shift_raw(g, -d),)


shift_rows.defvjp(_shift_fwd, _shift_bwd)


@jax.custom_vjp
def swap_halves(x):
    return _roll(x, HEAD // 2, 1)


def _swap_fwd(x):
    return _roll(x, HEAD // 2, 1), None


def _swap_bwd(_, g):
    return (_roll(g, HEAD // 2, 1),)


swap_halves.defvjp(_swap_fwd, _swap_bwd)


SCAN_BLOCK_ROWS = 64


def _scan_block(a, u, reverse):
    n = a.shape[0]
    t = lax.broadcasted_iota(jnp.int32, a.shape, 0)
    d = 1
    while d < n:
        if reverse:
            m = t < n - d
            a_s, u_s = _roll(a, n - d, 0), _roll(u, n - d, 0)
        else:
            m = t >= d
            a_s, u_s = _roll(a, d, 0), _roll(u, d, 0)
        u = a * jnp.where(m, u_s, 0.0) + u
        a = a * jnp.where(m, a_s, 1.0)
        d *= 2
    return a, u


def _scan_raw(a, u, reverse):
    n = a.shape[0]
    blocks = range(n // SCAN_BLOCK_ROWS)
    out = [None] * len(blocks)
    entering = None
    for b in (reversed(blocks) if reverse else blocks):
        rows = slice(b * SCAN_BLOCK_ROWS, (b + 1) * SCAN_BLOCK_ROWS)
        a_run, h = _scan_block(a[rows], u[rows], reverse)
        if entering is not None:
            h = a_run * entering + h
        out[b] = h
        entering = h[:1] if reverse else h[SCAN_BLOCK_ROWS - 1:]
    return jnp.concatenate(out, axis=0)


@jax.custom_vjp
def lin_scan(a, u):
    return _scan_raw(a, u, False)


def _lin_scan_fwd(a, u):
    hs = _scan_raw(a, u, False)
    return hs, (a, hs)


def _lin_scan_bwd(res, g):
    a, hs = res
    lam = _scan_raw(_shift_raw(a, -1), g, True)
    return lam * _shift_raw(hs, 1), lam


lin_scan.defvjp(_lin_scan_fwd, _lin_scan_bwd)


def _bdot(a, b, dims=(((1,), (0,)), ((), ()))):
    return lax.dot_general(a.astype(BF16), b.astype(BF16), dims, preferred_element_type=F32)


def _each(f, *seqs):
    return tuple(f(*a) for a in zip(*seqs))


def _split_bf16(a):
    hi = a.astype(BF16)
    return hi, (a - hi.astype(F32)).astype(BF16)


def _dot3_raw(a_s, b_s):
    a_hl = _each(_split_bf16, a_s)
    b_hl = _each(_split_bf16, b_s)
    hh = _each(lambda a, b: _bdot(a[0], b[0]), a_hl, b_hl)
    hl = _each(lambda a, b: _bdot(a[0], b[1]), a_hl, b_hl)
    lh = _each(lambda a, b: _bdot(a[1], b[0]), a_hl, b_hl)
    return _each(lambda x, y, z: x + (y + z), hh, hl, lh)


@jax.custom_vjp
def dot3(a_s, b_s):
    return _dot3_raw(a_s, b_s)


def _dot3_fwd(a_s, b_s):
    return _dot3_raw(a_s, b_s), (a_s, b_s)


def _dot3_bwd(res, g_s):
    a_s, b_s = res
    return (_each(lambda g, b: _bdot(g, b, (((1,), (1,)), ((), ()))), g_s, b_s),
            _each(lambda a, g: _bdot(a, g, (((0,), (0,)), ((), ()))), a_s, g_s))


dot3.defvjp(_dot3_fwd, _dot3_bwd)


def _eye(n):
    i = lax.broadcasted_iota(jnp.int32, (n, n), 0)
    j = lax.broadcasted_iota(jnp.int32, (n, n), 1)
    return (i == j).astype(F32)


def _unit_lower_inverse_raw(lmats):
    n = lmats[0].shape[0]
    eye = _eye(n)
    ps = _each(lambda l: -l, lmats)
    invs = _each(lambda x: eye + x, ps)
    k = 1
    while 2 * k < n:
        ps = _each(lambda p: _bdot(p, p), ps)
        invs = _each(lambda inv, p: inv + _bdot(inv, p), invs, ps)
        k *= 2
    prods = _dot3_raw(lmats, invs)
    resids = _each(lambda inv, pr: eye - inv - pr, invs, prods)
    return _each(lambda inv, r: inv + _bdot(inv, r), invs, resids)


@jax.custom_vjp
def unit_lower_inverse(lmats):
    return _unit_lower_inverse_raw(lmats)


def _uli_fwd(lmats):
    invs = _unit_lower_inverse_raw(lmats)
    return invs, invs


def _uli_bwd(invs, g_s):
    ms = _each(lambda inv, g: _bdot(inv, g, (((0,), (0,)), ((), ()))), invs, g_s)
    return (_each(lambda m, inv: -_bdot(m, inv, (((1,), (1,)), ((), ()))), ms, invs),)


unit_lower_inverse.defvjp(_uli_fwd, _uli_bwd)


def _cumsum_raw(x, reverse):
    n = x.shape[0]
    t = lax.broadcasted_iota(jnp.int32, x.shape, 0)
    d = 1
    while d < n:
        if reverse:
            x = x + jnp.where(t < n - d, _roll(x, n - d, 0), 0.0)
        else:
            x = x + jnp.where(t >= d, _roll(x, d, 0), 0.0)
        d *= 2
    return x


@jax.custom_vjp
def cumsum_rows(x):
    return _cumsum_raw(x, False)


def _cumsum_fwd(x):
    return _cumsum_raw(x, False), None


def _cumsum_bwd(_, g):
    return (_cumsum_raw(g, True),)


cumsum_rows.defvjp(_cumsum_fwd, _cumsum_bwd)


_NT = (((1,), (1,)), ((), ()))
_TN = (((0,), (0,)), ((), ()))


def _softplus(x):
    return jnp.maximum(x, 0.0) + jnp.log1p(jnp.exp(-jnp.abs(x)))


def _expm1_nonpos(x):
    poly = x * (1.0 + x * (0.5 + x * (1.0 / 6 + x * (1.0 / 24 + x * (1.0 / 120 + x * (1.0 / 720))))))
    return jnp.where(x > -0.25, poly, jnp.exp(x) - 1.0)


def _rms(x):
    return x * lax.rsqrt(jnp.mean(x * x, axis=-1, keepdims=True) + EPS)


def _causal_conv(x, w, width):
    y = w[width - 1:width, :] * x
    for j in range(width - 1):
        y = y + w[j:j + 1, :] * shift_rows(x, width - 1 - j)
    return y


def _norm_fn(x, g):
    return _rms(x) * g


def _ffn_act_fn(ug, uv, wg, wv, bg, bv):
    return jax.nn.silu(_causal_conv(ug, wg, 3) + bg) * (_causal_conv(uv, wv, 3) + bv)


def _gdn_conv_fn(x, w):
    return jax.nn.silu(_causal_conv(x, w, 4))


def _lru_fn(gate, x, cw, cb, wa, ba, wx, bx, lam):
    xr = _causal_conv(x, cw, 4) + cb
    r = jax.nn.sigmoid(_bdot(xr, wa) + ba)
    i = jax.nn.sigmoid(_bdot(xr, wx) + bx)
    log_a = -LRU_C * r * _softplus(-lam)
    a = jnp.exp(log_a)
    u = jnp.sqrt(-_expm1_nonpos(2.0 * log_a)) * (i * xr)
    hs = lin_scan(a, u)
    return jax.nn.gelu(gate) * hs


def _ret_fn(qs, ks, vs, gates, states, cos2, sin2, dmasks, ktails, qdecs, cdecs):
    c = RET_CHUNK
    n_heads = len(qs)
    n_chunks = qs[0].shape[0] // c
    units = tuple((ci, h) for ci in range(n_chunks) for h in range(n_heads))

    def rows(x, ci):
        return x[ci * c:(ci + 1) * c]

    qrs = tuple(rows(qs[h], ci) * rows(cos2, ci) + swap_halves(rows(qs[h], ci)) * rows(sin2, ci) for ci, h in units)
    krs = tuple((rows(ks[h], ci) * rows(cos2, ci) + swap_halves(rows(ks[h], ci)) * rows(sin2, ci)) * (HEAD ** -0.5) for ci, h in units)
    vus = tuple(rows(vs[h], ci) for ci, h in units)
    scores = tuple(_bdot(q, k, _NT) * dmasks[h] for q, k, (_, h) in zip(qrs, krs, units))
    intra = _each(lambda sc, v: _bdot(sc, v), scores, vus)
    outs = []
    for ci in range(n_chunks):
        mine = slice(ci * n_heads, (ci + 1) * n_heads)
        inter = _each(lambda q, d, s: _bdot(q * d, s), qrs[mine], qdecs, states)
        outs.append(_each(lambda a, b: a + b, intra[mine], inter))
        states = _each(lambda s, cd, k, kt, v: s * cd + _bdot(k * kt, v, _TN), states, cdecs, krs[mine], ktails, vus[mine])
    ys = tuple(_rms(jnp.concatenate([outs[ci][h] for ci in range(n_chunks)], axis=0)) * jax.nn.silu(gates[h]) for h in range(n_heads))
    return ys, states


def _pick_lane(x, lane_idx):
    lane = lax.broadcasted_iota(jnp.int32, x.shape, 1)
    return jnp.sum(jnp.where(lane == lane_idx, x, 0.0), axis=1, keepdims=True)


def _l2norm(x):
    return x * lax.rsqrt(jnp.sum(x * x, axis=-1, keepdims=True) + EPS)


def _gdn_fn(qcs, kcs, vcs, gates, small, a_log, dt_bias, gain, states):
    c = GDN_CHUNK
    n_heads = len(qcs)
    n_chunks = qcs[0].shape[0] // c
    units = tuple((ci, h) for ci in range(n_chunks) for h in range(n_heads))

    def unit_rows(per_head):
        return tuple(per_head[h][ci * c:(ci + 1) * c] for ci, h in units)

    smalls = tuple(small[ci * c:(ci + 1) * c] for ci, _ in units)
    heads = tuple(h for _, h in units)
    intra = _gdn_intra(unit_rows(qcs), unit_rows(kcs), unit_rows(vcs), smalls, heads, a_log, dt_bias)
    outs = []
    for ci in range(n_chunks):
        mine = slice(ci * n_heads, (ci + 1) * n_heads)
        os_, states = _gdn_inter(*(part[mine] for part in intra), states)
        outs.append(os_)
    ys = tuple(_rms(jnp.concatenate([outs[ci][h] for ci in range(n_chunks)], axis=0)) * gain * jax.nn.silu(gates[h])
               for h in range(n_heads))
    return ys, states


def _gdn_inter(qs, ks, us, ws, attns, gcs, g_lasts, states):
    v_news = _each(lambda u, w, s: u - _bdot(w, s), us, ws, states)
    inter = _each(lambda q, gc, s: _bdot(q * jnp.exp(gc), s), qs, gcs, states)
    os_ = _each(lambda x, a, v: x + _bdot(a, v), inter, attns, v_news)
    new_states = _each(lambda s, gl, k, gc, v: s * jnp.exp(gl) + _bdot(k * jnp.exp(gl - gc), v, _TN), states, g_lasts, ks, gcs, v_news)
    return os_, new_states


def _gdn_intra(qcs, kcs, vcs, smalls, heads, a_log, dt_bias):
    c = GDN_CHUNK
    qs = _each(lambda x: _l2norm(x) * (HEAD ** -0.5), qcs)
    ks = _each(_l2norm, kcs)
    betas = _each(lambda sm, h: jax.nn.sigmoid(_pick_lane(sm, h)), smalls, heads)
    gs = _each(lambda sm, h: -jnp.exp(_pick_lane(a_log, h)) * _softplus(_pick_lane(sm, h + N_HEADS) + _pick_lane(dt_bias, h)),
               smalls, heads)
    i = lax.broadcasted_iota(jnp.int32, (c, c), 0)
    j = lax.broadcasted_iota(jnp.int32, (c, c), 1)
    tril = i >= j
    gcs = _each(lambda g: cumsum_rows(jnp.broadcast_to(g, (c, LANES)))[:, :1], gs)
    gc_rows = _each(lambda gc: jnp.broadcast_to(gc, (c, c)), gcs)
    decays = _each(lambda r: jnp.where(tril, jnp.exp(jnp.where(tril, r - r.T, 0.0)), 0.0), gc_rows)
    kbs = _each(lambda k, b: k * b, ks, betas)
    lmats = _each(lambda kb, k, d: jnp.where(i > j, _bdot(kb, k, _NT) * d, 0.0), kbs, ks, decays)
    attns = _each(lambda q, k, d: jnp.where(tril, _bdot(q, k, _NT) * d, 0.0), qs, ks, decays)
    invs = unit_lower_inverse(lmats)
    us = dot3(invs, _each(lambda v, b: v * b, vcs, betas))
    ws = dot3(invs, _each(lambda kb, gc: kb * jnp.exp(gc), kbs, gcs))
    g_lasts = _each(lambda g: jnp.sum(g, axis=0, keepdims=True), gs)
    return qs, ks, us, ws, attns, gcs, g_lasts


def _final_fn(h, g, target):
    y = _rms(h) * g
    return 0.5 * jnp.sum(jnp.mean(jnp.square(y - target), axis=-1, keepdims=True), axis=0, keepdims=True)


def _tile(n, candidates):
    for t in candidates:
        if n % t == 0:
            return t
    raise ValueError(f"no tile for {n}")


MATMUL_RESIDENT_LHS_BYTES = 8 * 1024 * 1024


def matmul(a, b, *, ta=False, tb=False, add=None, out_dtype=F32, tm=None, tn=None, split=None, layer=None, name):
    m = a.shape[1] if ta else a.shape[0]
    k = a.shape[0] if ta else a.shape[1]
    n = b.shape[0] if tb else b.shape[1]
    assert k == (b.shape[1] if tb else b.shape[0])
    out_shape, out_block, out_index = (m, n), None, lambda i, j: (i, j)
    if split is not None:
        dims4, perm = split
        out_shape = tuple(dims4[p] for p in perm)
        r, cols = out_shape[2:]
        tm, tn = m, tn or _tile(cols, (1408, 512))
        cb = cols // tn
        if perm == (0, 2, 1, 3):
            out_block, out_index = (2, None, r, tn), lambda i, j: (0, j // cb, 0, j % cb)
        elif perm == (1, 0, 2, 3):
            out_block, out_index = (2, N_SHARD, r, tn), lambda i, j: (0, 0, 0, j)
        else:
            raise ValueError(perm)
    if tm is None and not ta and m * k * a.dtype.itemsize <= MATMUL_RESIDENT_LHS_BYTES:
        tm = m
    tm = tm or _tile(m, (1024, 512, 1408, 256, 128))
    tn = tn or _tile(n, (512, 1408, 256, 128))
    aliases, prev = {}, None
    if layer is not None:
        index, count, prev = layer
        out_shape, out_block, out_index = (count, m, n), (None, tm, tn), lambda i, j: (index, i, j)
    dims = (((0 if ta else 1,), (1 if tb else 0,)), ((), ()))

    def body(a_ref, b_ref, *rest):
        acc = lax.dot_general(a_ref[...].astype(BF16), b_ref[...].astype(BF16), dims, preferred_element_type=F32)
        if add is not None:
            acc = acc + rest[0][...]
        o_ref = rest[-1]
        acc = acc.astype(out_dtype)
        if split is not None and split[1] == (1, 0, 2, 3):
            rows = o_ref.shape[2]
            for s in range(N_SHARD):
                for h in range(2):
                    o_ref[h, s] = acc[(2 * s + h) * rows:(2 * s + h + 1) * rows]
        else:
            o_ref[...] = acc.reshape(o_ref.shape)

    a_spec = pl.BlockSpec((k, tm), lambda i, j: (0, i)) if ta else pl.BlockSpec((tm, k), lambda i, j: (i, 0))
    b_spec = pl.BlockSpec((tn, k), lambda i, j: (j, 0)) if tb else pl.BlockSpec((k, tn), lambda i, j: (0, j))
    o_spec = pl.BlockSpec(out_block or (tm, tn), out_index)
    in_specs, args = [a_spec, b_spec], [a, b]
    if add is not None:
        in_specs.append(o_spec)
        args.append(add)
    if prev is not None:
        aliases = {len(args): 0}
        in_specs.append(pl.BlockSpec(memory_space=pl.ANY))
        args.append(prev)
    return pl.pallas_call(body, out_shape=_sds(out_shape, out_dtype), grid=(m // tm, n // tn), in_specs=in_specs,
                          out_specs=o_spec, input_output_aliases=aliases, compiler_params=_params(), name=name)(*args)


def norm_matmul(x, g, b, *, tb=False, name):
    t, k = x.shape
    n = b.shape[0] if tb else b.shape[1]
    tn = _tile(n, (512, 1408, 256, 128))
    dims = (((1,), (1 if tb else 0,)), ((), ()))

    def body(x_ref, g_ref, b_ref, o_ref, hn_ref):
        @pl.when(pl.program_id(0) == 0)
        def _():
            hn_ref[...] = _norm_fn(x_ref[...], g_ref[...]).astype(BF16)

        o_ref[...] = lax.dot_general(hn_ref[...], b_ref[...].astype(BF16), dims, preferred_element_type=F32)

    b_spec = pl.BlockSpec((tn, k), lambda j: (j, 0)) if tb else pl.BlockSpec((k, tn), lambda j: (0, j))
    whole = pl.BlockSpec((t, k), lambda j: (0, 0))
    return pl.pallas_call(body, out_shape=(_sds((t, n), F32), _sds((t, k), BF16)), grid=(n // tn,),
                          in_specs=[whole, pl.BlockSpec((1, k), lambda j: (0, 0)), b_spec],
                          out_specs=(pl.BlockSpec((t, tn), lambda j: (0, j)), whole), compiler_params=_params(), name=name)(x, g, b)


ROW_TILE = 256


def norm_bwd(x, g, dy, dres, *, name):
    t, d = x.shape

    def body(x_ref, g_ref, dy_ref, dres_ref, dx_ref, dg_ref):
        _, vjp = jax.vjp(_norm_fn, x_ref[...], g_ref[...])
        dx, dg = vjp(dy_ref[...])
        dx_ref[...] = dx + dres_ref[...]

        @pl.when(pl.program_id(0) == 0)
        def _():
            dg_ref[...] = jnp.zeros_like(dg_ref)

        dg_ref[...] += dg

    row = pl.BlockSpec((ROW_TILE, d), lambda i: (i, 0))
    vec = pl.BlockSpec((1, d), lambda i: (0, 0))
    return pl.pallas_call(body, out_shape=(_sds((t, d), F32), _sds((1, d), F32)), grid=(t // ROW_TILE,),
                          in_specs=[row, vec, row, row], out_specs=(row, vec), compiler_params=_params(), name=name)(x, g, dy, dres)


def final_fwd_bwd(h, g, target, *, name):
    t, d = h.shape

    def body(h_ref, g_ref, t_ref, loss_ref, dh_ref, dg_ref):
        tgt = t_ref[...]
        loss, vjp = jax.vjp(lambda hh, gg: _final_fn(hh, gg, tgt), h_ref[...], g_ref[...])
        dh, dg = vjp(jnp.ones((1, 1), F32))
        dh_ref[...] = dh

        @pl.when(pl.program_id(0) == 0)
        def _():
            dg_ref[...] = jnp.zeros_like(dg_ref)
            loss_ref[...] = jnp.zeros_like(loss_ref)

        dg_ref[...] += dg
        loss_ref[...] += jnp.broadcast_to(loss, loss_ref.shape)

    row = pl.BlockSpec((ROW_TILE, d), lambda i: (i, 0))
    vec = pl.BlockSpec((1, d), lambda i: (0, 0))
    return pl.pallas_call(body, out_shape=(_sds((1, LANES), F32), _sds((t, d), F32), _sds((1, d), F32)), grid=(t // ROW_TILE,),
                          in_specs=[row, vec, row], out_specs=(pl.BlockSpec((1, LANES), lambda i: (0, 0)), row, vec),
                          compiler_params=_params(), name=name)(h, g, target)


FFN_FWD_COLS = 256
FFN_BWD_COLS = 128


def ffn_act_fwd(u, cw, cb, *, name):
    t = u.shape[0]
    w = FFN_FWD_COLS
    nb = D_FF // w

    def body(ug_ref, uv_ref, wg_ref, wv_ref, bg_ref, bv_ref, o_ref):
        o_ref[...] = _ffn_act_fn(ug_ref[...], uv_ref[...], wg_ref[...], wv_ref[...], bg_ref[...], bv_ref[...]).astype(BF16)

    def col(rows, off):
        return pl.BlockSpec((rows, w), lambda j: (0, j + off))

    return pl.pallas_call(body, out_shape=_sds((t, D_FF), BF16), grid=(nb,),
                          in_specs=[col(t, 0), col(t, nb), col(3, 0), col(3, nb), col(1, 0), col(1, nb)],
                          out_specs=col(t, 0), compiler_params=_params(), name=name)(u, u, cw, cw, cb, cb)


def _put_column_blocks(step, n_steps, blocks, dst_ref, width, stage_ref, sems):
    def copies(at):
        slot = at % 2
        return [pltpu.make_async_copy(stage_ref.at[slot, p], dst_ref.at[:, pl.ds(pl.multiple_of((p * n_steps + at) * width, LANES), width)],
                                      sems.at[slot, p]) for p in range(len(blocks))]

    @pl.when(step >= 2)
    def _():
        for cp in copies(step - 2):
            cp.wait()

    for p, value in enumerate(blocks):
        stage_ref[step % 2, p] = value
    for cp in copies(step):
        cp.start()

    @pl.when(step == n_steps - 1)
    def _():
        for cp in copies(step - 1) + copies(step):
            cp.wait()


def ffn_act_bwd(u, cw, cb, da, *, name):
    t = u.shape[0]
    w = FFN_BWD_COLS
    nb = D_FF // w

    def body(ug_ref, uv_ref, wg_ref, wv_ref, bg_ref, bv_ref, da_ref, dug_ref, duv_ref, dwg_ref, dwv_ref, dbg_ref, dbv_ref):
        _, vjp = jax.vjp(_ffn_act_fn, ug_ref[...], uv_ref[...], wg_ref[...], wv_ref[...], bg_ref[...], bv_ref[...])
        dug, duv, dwg, dwv, dbg, dbv = vjp(da_ref[...])
        dug_ref[...] = dug.astype(BF16)
        duv_ref[...] = duv.astype(BF16)
        dwg_ref[...] = dwg
        dwv_ref[...] = dwv
        dbg_ref[...] = dbg
        dbv_ref[...] = dbv

    def col(rows, off):
        return pl.BlockSpec((rows, w), lambda j: (0, j + off))

    outs = pl.pallas_call(
        body, out_shape=(_sds((t, D_FF), BF16), _sds((t, D_FF), BF16), _sds((3, D_FF), F32), _sds((3, D_FF), F32),
                         _sds((1, D_FF), F32), _sds((1, D_FF), F32)),
        grid=(nb,), in_specs=[col(t, 0), col(t, nb), col(3, 0), col(3, nb), col(1, 0), col(1, nb), col(t, 0)],
        out_specs=(col(t, 0), col(t, 0), col(3, 0), col(3, 0), col(1, 0), col(1, 0)), compiler_params=_params(), name=name,
    )(u, u, cw, cw, cb, cb, da)
    dug, duv, dwg, dwv, dbg, dbv = outs
    return jnp.concatenate([dug, duv], axis=1), jnp.concatenate([dwg, dwv], axis=1), jnp.concatenate([dbg, dbv], axis=1)


GDN_CONV_COLS = 256
GDN_CONV_OFF = 4 * GROUP


def gdn_conv_fwd(p, cw, *, name):
    t = p.shape[0]
    w = GDN_CONV_COLS
    nb = 3 * GROUP // w
    off = GDN_CONV_OFF // w

    def body(x_ref, w_ref, o_ref):
        o_ref[...] = _gdn_conv_fn(x_ref[...], w_ref[...])

    return pl.pallas_call(body, out_shape=_sds((t, 3 * GROUP), F32), grid=(nb,),
                          in_specs=[pl.BlockSpec((t, w), lambda j: (0, j + off)), pl.BlockSpec((4, w), lambda j: (0, j))],
                          out_specs=pl.BlockSpec((t, w), lambda j: (0, j)), compiler_params=_params(), name=name)(p, cw)


def gdn_conv_bwd(p, cw, dc, *, name):
    t = p.shape[0]
    w = GDN_CONV_COLS
    nb = 3 * GROUP // w
    off = GDN_CONV_OFF // w

    def body(x_ref, w_ref, dc_ref, dx_ref, dw_ref):
        _, vjp = jax.vjp(_gdn_conv_fn, x_ref[...], w_ref[...])
        dx, dw = vjp(dc_ref[...])
        dx_ref[...] = dx.astype(BF16)
        dw_ref[...] = dw

    blk = pl.BlockSpec((t, w), lambda j: (0, j))
    wblk = pl.BlockSpec((4, w), lambda j: (0, j))
    return pl.pallas_call(body, out_shape=(_sds((t, 3 * GROUP), BF16), _sds((4, 3 * GROUP), F32)), grid=(nb,),
                          in_specs=[pl.BlockSpec((t, w), lambda j: (0, j + off)), wblk, blk], out_specs=(blk, wblk),
                          compiler_params=_params(), name=name)(p, cw, dc)


def _lru_specs(t):
    w = D_MODEL // LRU_BLOCKS
    gate = pl.BlockSpec((t, w), lambda j: (0, j))
    xin = pl.BlockSpec((t, w), lambda j: (0, j + LRU_BLOCKS))
    cw = pl.BlockSpec((4, w), lambda j: (0, j))
    vec = pl.BlockSpec((1, w), lambda j: (0, j))
    mat = pl.BlockSpec((None, w, w), lambda j: (j, 0, 0))
    return gate, xin, cw, vec, mat


def lru_fwd(gx, cw, cb, wa, ba, wx, bx, lam, *, name):
    t = gx.shape[0]
    gate, xin, cws, vec, mat = _lru_specs(t)

    def body(g_ref, x_ref, cw_ref, cb_ref, wa_ref, ba_ref, wx_ref, bx_ref, lam_ref, o_ref):
        o_ref[...] = _lru_fn(g_ref[...], x_ref[...], cw_ref[...], cb_ref[...], wa_ref[...], ba_ref[...], wx_ref[...],
                             bx_ref[...], lam_ref[...]).astype(BF16)

    return pl.pallas_call(body, out_shape=_sds((t, D_MODEL), BF16), grid=(LRU_BLOCKS,),
                          in_specs=[gate, xin, cws, vec, mat, vec, mat, vec, vec], out_specs=gate,
                          compiler_params=_params(), name=name)(gx, gx, cw, cb, wa, ba, wx, bx, lam)


def lru_bwd(gx, cw, cb, wa, ba, wx, bx, lam, dy, *, name):
    t = gx.shape[0]
    gate, xin, cws, vec, mat = _lru_specs(t)

    def body(g_ref, x_ref, cw_ref, cb_ref, wa_ref, ba_ref, wx_ref, bx_ref, lam_ref, dy_ref,
             dgx_ref, dcw_ref, dcb_ref, dwa_ref, dba_ref, dwx_ref, dbx_ref, dlam_ref, stage_ref, sems):
        _, vjp = jax.vjp(_lru_fn, g_ref[...], x_ref[...], cw_ref[...], cb_ref[...], wa_ref[...], ba_ref[...], wx_ref[...],
                         bx_ref[...], lam_ref[...])
        dg, dx, dcw, dcb, dwa, dba, dwx, dbx, dlam = vjp(dy_ref[...])
        _put_column_blocks(pl.program_id(0), LRU_BLOCKS, (dg.astype(BF16), dx.astype(BF16)), dgx_ref, D_MODEL // LRU_BLOCKS, stage_ref, sems)
        dcw_ref[...] = dcw
        dcb_ref[...] = dcb
        dwa_ref[...] = dwa
        dba_ref[...] = dba
        dwx_ref[...] = dwx
        dbx_ref[...] = dbx
        dlam_ref[...] = dlam

    d = D_MODEL
    w = d // LRU_BLOCKS
    out_shape = (_sds((t, 2 * d), BF16), _sds((4, d), F32), _sds((1, d), F32), _sds((LRU_BLOCKS, w, w), F32),
                 _sds((1, d), F32), _sds((LRU_BLOCKS, w, w), F32), _sds((1, d), F32), _sds((1, d), F32))
    return pl.pallas_call(body, out_shape=out_shape, grid=(LRU_BLOCKS,),
                          in_specs=[gate, xin, cws, vec, mat, vec, mat, vec, vec, gate],
                          out_specs=(pl.BlockSpec(memory_space=pl.ANY), cws, vec, mat, vec, mat, vec, vec),
                          scratch_shapes=[pltpu.VMEM((2, 2, t, w), BF16), pltpu.SemaphoreType.DMA((2, 2))],
                          compiler_params=_params(), name=name)(gx, gx, cw, cb, wa, ba, wx, bx, lam, dy)


def _ret_tables():
    half = HEAD // 2
    inv_freq = (np.float32(ROPE_BASE) ** (-np.arange(half, dtype=np.float32) / np.float32(half))).astype(np.float32)
    ang = (np.arange(SEQ, dtype=np.float32)[:, None] * inv_freq[None, :]).astype(np.float64)
    cos2 = np.concatenate([np.cos(ang), np.cos(ang)], axis=1).astype(np.float32)
    sin2 = np.concatenate([-np.sin(ang), np.sin(ang)], axis=1).astype(np.float32)
    c = RET_CHUNK
    log_gamma = np.log1p(-np.exp2(-5.0 - np.arange(N_HEADS, dtype=np.float64)))
    idx = np.arange(c, dtype=np.float64)
    rel = idx[:, None] - idx[None, :]
    dmask = np.where(rel >= 0, np.exp(log_gamma[:, None, None] * np.maximum(rel, 0.0)), 0.0)
    ones = np.ones((N_HEADS, c, HEAD))
    ktail = np.exp(log_gamma[:, None] * (c - 1 - idx))[:, :, None] * ones
    qdec = np.exp(log_gamma[:, None] * (idx + 1.0))[:, :, None] * ones
    cdec = np.exp(log_gamma * c)[:, None, None] * ones
    return tuple(jnp.asarray(a, F32) for a in (cos2, sin2, dmask, ktail, qdec, cdec))


def _ret_specs(rev):
    c = RET_CHUNK * RET_CHUNKS_PER_STEP
    nc = SEQ // c

    def n_of(n):
        return nc - 1 - n if rev else n

    def group(off):
        return pl.BlockSpec((c, GROUP), lambda n: (n_of(n), off))

    tab = pl.BlockSpec((c, HEAD), lambda n: (n_of(n), 0))
    const = pl.BlockSpec((N_HEADS, RET_CHUNK, HEAD), lambda n: (0, 0, 0))
    state = pl.BlockSpec((N_HEADS, None, HEAD, HEAD), lambda n: (0, n_of(n), 0, 0))
    return group, tab, const, state, nc


def _head(h):
    return slice(h * HEAD, (h + 1) * HEAD)


def ret_fwd(p, tables, *, name):
    group, tab, const, state, nc = _ret_specs(False)

    def body(q_ref, k_ref, v_ref, g_ref, cos_ref, sin_ref, dm_ref, kt_ref, qd_ref, cd_ref, y_ref, st_ref, s_scr):
        @pl.when(pl.program_id(0) == 0)
        def _():
            s_scr[...] = jnp.zeros_like(s_scr)

        heads = range(N_HEADS)
        states = tuple(s_scr[h] for h in heads)
        ys, new_states = _ret_fn(*(tuple(r[:, _head(h)] for h in heads) for r in (q_ref, k_ref, v_ref, g_ref)), states,
                                 cos_ref[...], sin_ref[...], *(tuple(r[h] for h in heads) for r in (dm_ref, kt_ref, qd_ref, cd_ref)))
        for h in heads:
            st_ref[h] = states[h]
            y_ref[:, _head(h)] = ys[h].astype(BF16)
            s_scr[h] = new_states[h]

    return pl.pallas_call(
        body, out_shape=(_sds((SEQ, GROUP), BF16), _sds((N_HEADS, nc, HEAD, HEAD), F32)), grid=(nc,),
        in_specs=[group(0), group(1), group(2), group(3), tab, tab, const, const, const, const],
        out_specs=(group(0), state), scratch_shapes=[pltpu.VMEM((N_HEADS, HEAD, HEAD), F32)], compiler_params=_params(), name=name,
    )(p, p, p, p, *tables)


def ret_bwd(p, tables, states, dy, *, name):
    group, tab, const, state, nc = _ret_specs(True)

    def body(q_ref, k_ref, v_ref, g_ref, cos_ref, sin_ref, dm_ref, kt_ref, qd_ref, cd_ref, st_ref, dy_ref,
             dq_ref, dk_ref, dv_ref, dg_ref, ds_scr):
        @pl.when(pl.program_id(0) == 0)
        def _():
            ds_scr[...] = jnp.zeros_like(ds_scr)

        heads = range(N_HEADS)
        consts = (cos_ref[...], sin_ref[...], *(tuple(r[h] for h in heads) for r in (dm_ref, kt_ref, qd_ref, cd_ref)))
        _, vjp = jax.vjp(lambda *a: _ret_fn(*a, *consts), *(tuple(r[:, _head(h)] for h in heads) for r in (q_ref, k_ref, v_ref, g_ref)),
                         tuple(st_ref[h] for h in heads))
        dqs, dks, dvs, dgs, dss = vjp((tuple(dy_ref[:, _head(h)] for h in heads), tuple(ds_scr[h] for h in heads)))
        for h in heads:
            dq_ref[:, _head(h)] = dqs[h].astype(BF16)
            dk_ref[:, _head(h)] = dks[h].astype(BF16)
            dv_ref[:, _head(h)] = dvs[h].astype(BF16)
            dg_ref[:, _head(h)] = dgs[h].astype(BF16)
            ds_scr[h] = dss[h]

    out = _sds((SEQ, GROUP), BF16)
    return pl.pallas_call(
        body, out_shape=(out, out, out, out), grid=(nc,),
        in_specs=[group(0), group(1), group(2), group(3), tab, tab, const, const, const, const, state, group(0)],
        out_specs=(group(0), group(0), group(0), group(0)), scratch_shapes=[pltpu.VMEM((N_HEADS, HEAD, HEAD), F32)],
        compiler_params=_params(), name=name,
    )(p, p, p, p, *tables, states, dy)


def _gdn_specs(rev):
    c = GDN_CHUNK * GDN_CHUNKS_PER_STEP
    nc = SEQ // c

    def n_of(n):
        return nc - 1 - n if rev else n

    def group(off):
        return pl.BlockSpec((c, GROUP), lambda n: (n_of(n), off))

    small = pl.BlockSpec((c, LANES), lambda n: (n_of(n), 0))
    vec = pl.BlockSpec((1, LANES), lambda n: (0, 0))
    state = pl.BlockSpec((N_HEADS, None, HEAD, HEAD), lambda n: (0, n_of(n), 0, 0))
    return group, small, vec, state, nc


GDN_GATE_GROUP = 7


def gdn_fwd(conv, p, small, a_log, dt_bias, gain, *, name):
    group, sm, vec, state, nc = _gdn_specs(False)

    def body(q_ref, k_ref, v_ref, g_ref, sm_ref, al_ref, dt_ref, gn_ref, y_ref, st_ref, s_scr):
        @pl.when(pl.program_id(0) == 0)
        def _():
            s_scr[...] = jnp.zeros_like(s_scr)

        states = tuple(s_scr[h] for h in range(N_HEADS))
        ys, new_states = _gdn_fn(*(tuple(r[:, _head(h)] for h in range(N_HEADS)) for r in (q_ref, k_ref, v_ref, g_ref)),
                                 sm_ref[...], al_ref[...], dt_ref[...], gn_ref[...], states)
        for h in range(N_HEADS):
            st_ref[h] = states[h]
            y_ref[:, _head(h)] = ys[h].astype(BF16)
            s_scr[h] = new_states[h]

    return pl.pallas_call(
        body, out_shape=(_sds((SEQ, GROUP), BF16), _sds((N_HEADS, nc, HEAD, HEAD), F32)), grid=(nc,),
        in_specs=[group(0), group(1), group(2), group(GDN_GATE_GROUP), sm, vec, vec, vec], out_specs=(group(0), state),
        scratch_shapes=[pltpu.VMEM((N_HEADS, HEAD, HEAD), F32)], compiler_params=_params(), name=name,
    )(conv, conv, conv, p, small, a_log, dt_bias, gain)


def gdn_bwd(conv, p, small, a_log, dt_bias, gain, states, dy, *, name):
    group, sm, vec, state, nc = _gdn_specs(True)

    def body(q_ref, k_ref, v_ref, g_ref, sm_ref, al_ref, dt_ref, gn_ref, st_ref, dy_ref,
             dq_ref, dk_ref, dv_ref, dg_ref, dsm_ref, dal_ref, ddt_ref, dgn_ref, ds_scr):
        @pl.when(pl.program_id(0) == 0)
        def _():
            ds_scr[...] = jnp.zeros_like(ds_scr)
            dal_ref[...] = jnp.zeros_like(dal_ref)
            ddt_ref[...] = jnp.zeros_like(ddt_ref)
            dgn_ref[...] = jnp.zeros_like(dgn_ref)

        per_head = tuple(tuple(r[:, _head(h)] for h in range(N_HEADS)) for r in (q_ref, k_ref, v_ref, g_ref))
        _, vjp = jax.vjp(_gdn_fn, *per_head, sm_ref[...], al_ref[...], dt_ref[...], gn_ref[...],
                         tuple(st_ref[h] for h in range(N_HEADS)))
        cts = (tuple(dy_ref[:, _head(h)] for h in range(N_HEADS)), tuple(ds_scr[h] for h in range(N_HEADS)))
        dqs, dks, dvs, dgs, dsm, dal, ddt, dgn, dss = vjp(cts)
        for h in range(N_HEADS):
            dq_ref[:, _head(h)] = dqs[h]
            dk_ref[:, _head(h)] = dks[h]
            dv_ref[:, _head(h)] = dvs[h]
            dg_ref[:, _head(h)] = dgs[h].astype(BF16)
            ds_scr[h] = dss[h]
        dsm_ref[...] = dsm
        dal_ref[...] += dal
        ddt_ref[...] += ddt
        dgn_ref[...] += dgn

    f = _sds((SEQ, GROUP), F32)
    pv = _sds((1, LANES), F32)
    return pl.pallas_call(
        body, out_shape=(f, f, f, _sds((SEQ, GROUP), BF16), _sds((SEQ, LANES), F32), pv, pv, pv), grid=(nc,),
        in_specs=[group(0), group(1), group(2), group(GDN_GATE_GROUP), sm, vec, vec, vec, state, group(1)],
        out_specs=(group(0), group(0), group(0), group(0), sm, vec, vec, vec), scratch_shapes=[pltpu.VMEM((N_HEADS, HEAD, HEAD), F32)],
        compiler_params=_params(), name=name,
    )(conv, conv, conv, p, small, a_log, dt_bias, gain, states, dy)


ELEMENTWISE_BLOCK_BYTES = 2 * 1024 * 1024


def _row_tile(r, c):
    best = None
    for tr in range(8, r + 1, 8):
        if r % tr == 0 and tr * c * 4 <= ELEMENTWISE_BLOCK_BYTES:
            best = tr
    if best is None:
        raise ValueError(f"no row tile for ({r}, {c})")
    return best


def _tile_2d(r, c):
    if any(r % tr == 0 for tr in range(8, r + 1, 8)):
        return _row_tile(r, c), c
    tc = max(t for t in range(LANES, c + 1, LANES) if c % t == 0 and r * t * 4 <= ELEMENTWISE_BLOCK_BYTES)
    return r, tc


def _core_index():
    return lax.axis_index("c").astype(jnp.int32).reshape(1)


def _chip_index():
    return (2 * lax.axis_index("x") + lax.axis_index("y")).astype(jnp.int32).reshape(1)


def adamw_halves(w, m, v, g_own, g_sib, *, layer=0, prev=None, name):
    n_layers, rows, c = w.shape
    r = rows // 2
    tr = _row_tile(r, c)
    nb = r // tr

    def body(c_ref, w_ref, m_ref, v_ref, own_ref, sib_ref, *rest):
        g_ref, d_ref, nm_ref, nv_ref = rest[-4:]
        gg = jnp.where(pl.program_id(0) == c_ref[0], own_ref[...], sib_ref[...])
        nm = ADAM_B1 * m_ref[...] + (1.0 - ADAM_B1) * gg
        nv = ADAM_B2 * v_ref[...] + (1.0 - ADAM_B2) * jnp.square(gg)
        m_hat = nm / (1.0 - ADAM_B1 ** ADAM_STEP)
        v_hat = nv / (1.0 - ADAM_B2 ** ADAM_STEP)
        g_ref[...] = gg
        d_ref[...] = -ADAM_LR * (m_hat / (jnp.sqrt(v_hat) + ADAM_EPS) + ADAM_WD * w_ref[...])
        nm_ref[...] = nm
        nv_ref[...] = nv

    full = pl.BlockSpec((None, tr, c), lambda h, i, cr: (layer, h * nb + i, 0))
    half = pl.BlockSpec((tr, c), lambda h, i, cr: (i, 0))
    o = _sds((n_layers, rows, c), F32)
    prev = list(prev or ())
    gs = pltpu.PrefetchScalarGridSpec(num_scalar_prefetch=1, grid=(2, nb), in_specs=[full, full, full, half, half] + [_ANY] * len(prev),
                                      out_specs=(full, full, full, full))
    n_fixed = 6
    return pl.pallas_call(body, out_shape=(o, o, o, o), grid_spec=gs, compiler_params=_params(), name=name,
                          input_output_aliases={n_fixed + k: k for k in range(len(prev))})(
        _core_index(), w, m, v, g_own, g_sib, *prev)


ADAMW_COLUMN_TILE = 256


def adamw_column_halves(w, m, v, g_own, g_sib, *, name):
    _, rows, cols = w.shape
    tc = ADAMW_COLUMN_TILE
    per_half = cols // 2 // tc

    def body(c_ref, w_ref, m_ref, v_ref, own_ref, sib_ref, g_ref, d_ref, nm_ref, nv_ref):
        gg = jnp.where(pl.program_id(0) // per_half == c_ref[0], own_ref[...], sib_ref[...])
        nm = ADAM_B1 * m_ref[...] + (1.0 - ADAM_B1) * gg
        nv = ADAM_B2 * v_ref[...] + (1.0 - ADAM_B2) * jnp.square(gg)
        m_hat = nm / (1.0 - ADAM_B1 ** ADAM_STEP)
        v_hat = nv / (1.0 - ADAM_B2 ** ADAM_STEP)
        g_ref[...] = gg
        d_ref[...] = -ADAM_LR * (m_hat / (jnp.sqrt(v_hat) + ADAM_EPS) + ADAM_WD * w_ref[...])
        nm_ref[...] = nm
        nv_ref[...] = nv

    full = pl.BlockSpec((None, rows, tc), lambda j, cr: (0, 0, j))
    half = pl.BlockSpec((rows, tc), lambda j, cr: (0, j % per_half))
    o = _sds(w.shape, F32)
    gs = pltpu.PrefetchScalarGridSpec(num_scalar_prefetch=1, grid=(cols // tc,), in_specs=[full, full, full, half, half],
                                      out_specs=(full, full, full, full))
    return pl.pallas_call(body, out_shape=(o, o, o, o), grid_spec=gs, compiler_params=_params(), name=name)(
        _core_index(), w, m, v, g_own, g_sib)


def adamw_many(ws, gs, ms, vs, *, name):
    n = len(ws)

    def body(*refs):
        w_refs, g_refs, m_refs, v_refs, d_refs, nm_refs, nv_refs = (refs[k * n:(k + 1) * n] for k in range(7))
        for i in range(n):
            gg = g_refs[i][...]
            nm = ADAM_B1 * m_refs[i][...] + (1.0 - ADAM_B1) * gg
            nv = ADAM_B2 * v_refs[i][...] + (1.0 - ADAM_B2) * jnp.square(gg)
            m_hat = nm / (1.0 - ADAM_B1 ** ADAM_STEP)
            v_hat = nv / (1.0 - ADAM_B2 ** ADAM_STEP)
            d_refs[i][...] = -ADAM_LR * (m_hat / (jnp.sqrt(v_hat) + ADAM_EPS) + ADAM_WD * w_refs[i][...])
            nm_refs[i][...] = nm
            nv_refs[i][...] = nv

    outs = pl.pallas_call(body, out_shape=[_sds(w.shape, F32) for w in ws] * 3, compiler_params=_params(), name=name)(*ws, *gs, *ms, *vs)
    return outs[:n], outs[n:2 * n], outs[2 * n:]


def add_core_halves(g2, land, *, out_dtype, name):
    _, ns, r, cols = g2.shape
    tr, tc = _tile_2d(r, cols)

    def body(c_ref, a_ref, b_ref, o_ref):
        o_ref[...] = (a_ref[...] + b_ref[...]).astype(out_dtype)

    gs = pltpu.PrefetchScalarGridSpec(
        num_scalar_prefetch=1, grid=(ns, r // tr, cols // tc),
        in_specs=[pl.BlockSpec((None, None, tr, tc), lambda s, i, j, cr: (cr[0], s, i, j)),
                  pl.BlockSpec((None, tr, tc), lambda s, i, j, cr: (s, i, j))],
        out_specs=pl.BlockSpec((None, tr, tc), lambda s, i, j, cr: (s, i, j)))
    return pl.pallas_call(body, out_shape=_sds((ns, r, cols), out_dtype), grid_spec=gs, compiler_params=_params(), name=name)(
        _core_index(), g2, land)


def sum_over_chips(own, land, *, scatter, name):
    _, r, cols = own.shape
    tr, tc = _tile_2d(r, cols)

    def body(mine_ref, own_ref, l0, l1, l2, l3, o_ref):
        mine = mine_ref[0]
        mine_val = own_ref[...]
        acc = None
        for s, l_ref in enumerate((l0, l1, l2, l3)):
            val = jnp.where(mine == s, mine_val, l_ref[...]).astype(F32)
            acc = val if acc is None else acc + val
        o_ref[...] = acc

    def slot(s):
        return pl.BlockSpec((None, tr, tc), lambda i, j, mr: (jnp.where(mr[0] == s, (s + 1) % N_SHARD, s), i, j))

    own_spec = pl.BlockSpec((None, tr, tc), lambda i, j, mr: (mr[0] if scatter else 0, i, j))
    gs = pltpu.PrefetchScalarGridSpec(num_scalar_prefetch=1, grid=(r // tr, cols // tc), in_specs=[own_spec] + [slot(s) for s in range(N_SHARD)],
                                      out_specs=pl.BlockSpec((tr, tc), lambda i, j, mr: (i, j)))
    return pl.pallas_call(body, out_shape=_sds((r, cols), F32), grid_spec=gs, compiler_params=_params(), name=name)(
        _chip_index(), own, land, land, land, land)


_ANY = pl.BlockSpec(memory_space=pl.ANY)


def xy_exchange(src, *, scatter, name):
    rh = src.shape[1]

    def body(src_ref, land_ref, send_sems, recv_sems, loc_sem):
        x, y, c = lax.axis_index("x"), lax.axis_index("y"), lax.axis_index("c")
        mine = 2 * x + y
        peers = [(1 - x, y), (x, 1 - y), (1 - x, 1 - y)]

        def piece(shard):
            return src_ref.at[shard] if scatter else src_ref.at[c]

        def copy(k, px, py, dst_slot):
            return pltpu.make_async_remote_copy(src_ref=piece(2 * px + py), dst_ref=land_ref.at[dst_slot], send_sem=send_sems.at[k],
                                                recv_sem=recv_sems.at[k], device_id=(px, py, c), device_id_type=MESH)

        keep = pltpu.make_async_copy(piece(mine), land_ref.at[mine], loc_sem)
        keep.start()
        sends = [copy(k, px, py, mine) for k, (px, py) in enumerate(peers)]
        for cp in sends:
            cp.start()
        for cp in sends:
            cp.wait_send()
        for k, (px, py) in enumerate(peers):
            copy(k, px, py, 2 * px + py).wait_recv()
        keep.wait()

    return pl.pallas_call(body, out_shape=_sds((N_SHARD, rh, LANES), src.dtype), in_specs=[_ANY], out_specs=_ANY,
                          scratch_shapes=[pltpu.SemaphoreType.DMA((3,)), pltpu.SemaphoreType.DMA((3,)), pltpu.SemaphoreType.DMA(())],
                          name=name)(src)


def core_exchange(src, *, send_other_half, name):
    def body(src_ref, out_ref, send_sem, recv_sem, loc_sem):
        x, y, c = lax.axis_index("x"), lax.axis_index("y"), lax.axis_index("c")
        if send_other_half:
            cp = pltpu.make_async_remote_copy(src_ref=src_ref.at[1 - c], dst_ref=out_ref, send_sem=send_sem, recv_sem=recv_sem,
                                              device_id=(x, y, 1 - c), device_id_type=MESH)
            cp.start()
            cp.wait_send()
            cp.wait_recv()
        else:
            keep = pltpu.make_async_copy(src_ref, out_ref.at[c], loc_sem)
            keep.start()
            cp = pltpu.make_async_remote_copy(src_ref=src_ref, dst_ref=out_ref.at[c], send_sem=send_sem, recv_sem=recv_sem,
                                              device_id=(x, y, 1 - c), device_id_type=MESH)
            cp.start()
            cp.wait_send()
            pltpu.make_async_remote_copy(src_ref=src_ref, dst_ref=out_ref.at[1 - c], send_sem=send_sem, recv_sem=recv_sem,
                                         device_id=(x, y, 1 - c), device_id_type=MESH).wait_recv()
            keep.wait()

    out_shape = _sds(src.shape[1:], src.dtype) if send_other_half else _sds((2,) + src.shape, src.dtype)
    return pl.pallas_call(body, out_shape=out_shape, in_specs=[_ANY], out_specs=_ANY,
                          scratch_shapes=[pltpu.SemaphoreType.DMA(()), pltpu.SemaphoreType.DMA(()), pltpu.SemaphoreType.DMA(())],
                          name=name)(src)


def _comm_call(body, ins, out_shapes, sem_counts, name):
    return pl.pallas_call(body, out_shape=tuple(out_shapes), in_specs=[_ANY] * len(ins), out_specs=tuple([_ANY] * len(out_shapes)),
                          scratch_shapes=[pltpu.SemaphoreType.DMA((k,)) for k in sem_counts], name=name)(*ins)


def _sequencer_call(body, ins, out_shapes, sem_counts, name, collective_id):
    return pl.kernel(body, out_type=list(out_shapes), mesh=plsc.ScalarSubcoreMesh(axis_name="sequencer", num_cores=1), name=name,
                     scratch_types=[pltpu.SemaphoreType.DMA((k,)) for k in sem_counts],
                     compiler_params=pltpu.CompilerParams(collective_id=collective_id))(*ins)


def _handshake(peers):
    barrier = pltpu.get_barrier_semaphore()
    for peer in peers:
        pl.semaphore_signal(barrier, inc=1, device_id=peer, device_id_type=MESH)
    pl.semaphore_wait(barrier, len(peers))


def _xy_peers(x, y):
    return [(1 - x, y), (x, 1 - y), (1 - x, 1 - y)]


def gather_halves(halves, *, name, collective_id):
    n = len(halves)

    def body(*refs):
        ins, lands, sibs = refs[:n], refs[n:2 * n], refs[2 * n:3 * n]
        ici_send, ici_recv, d2d_send, d2d_recv = refs[3 * n:]
        x, y, c = lax.axis_index("x"), lax.axis_index("y"), lax.axis_index("c")
        mine = 2 * x + y
        peers = _xy_peers(x, y)
        _handshake([(px, py, c) for px, py in peers] + [(x, y, 1 - c)])

        def ici(i, k, slot):
            px, py = peers[k]
            return pltpu.make_async_remote_copy(src_ref=ins[i].at[c], dst_ref=lands[i].at[slot], send_sem=ici_send.at[3 * i + k],
                                                recv_sem=ici_recv.at[3 * i + k], device_id=(px, py, c), device_id_type=MESH)

        def pass_on(i, k):
            px, py = peers[k]
            slot = 2 * px + py
            return pltpu.make_async_remote_copy(src_ref=lands[i].at[slot], dst_ref=sibs[i].at[slot], send_sem=d2d_send.at[3 * i + k],
                                                recv_sem=d2d_recv.at[3 * i + k], device_id=(x, y, 1 - c), device_id_type=MESH)

        sends = [ici(i, k, mine) for i in range(n) for k in range(3)]
        for cp in sends:
            cp.start()
        passed = []
        for i in range(n):
            for k in range(3):
                px, py = peers[k]
                ici(i, k, 2 * px + py).wait_recv()
                cp = pass_on(i, k)
                cp.start()
                passed.append(cp)
        for cp in passed:
            cp.wait_recv()
        for cp in sends + passed:
            cp.wait_send()

    outs = [_sds((N_SHARD,) + h.shape[1:], h.dtype) for h in halves]
    res = _sequencer_call(body, halves, outs + outs, [3 * n] * 4, name, collective_id)
    return res[:n], res[n:]


def send_other_half(arrays, *, name, collective_id):
    n = len(arrays)

    def body(*refs):
        ins, lands = refs[:n], refs[n:2 * n]
        send_sems, recv_sems = refs[2 * n:]
        x, y, c = lax.axis_index("x"), lax.axis_index("y"), lax.axis_index("c")
        _handshake([(x, y, 1 - c)])
        copies = [pltpu.make_async_remote_copy(src_ref=ins[i].at[1 - c], dst_ref=lands[i], send_sem=send_sems.at[i],
                                               recv_sem=recv_sems.at[i], device_id=(x, y, 1 - c), device_id_type=MESH) for i in range(n)]
        for cp in copies:
            cp.start()
        for cp in copies:
            cp.wait_recv()
        for cp in copies:
            cp.wait_send()

    return _sequencer_call(body, arrays, [_sds(a.shape[1:], a.dtype) for a in arrays], [n, n], name, collective_id)


_HBM = pl.BlockSpec(memory_space=pltpu.HBM)
_SEM = pl.BlockSpec(memory_space=pltpu.SEMAPHORE)
_SPLIT_COPY = dict(has_side_effects=pltpu.SideEffectType.DATAFLOW_SIDE_EFFECTING)


def _chip_copy(ins, lands, send_sems, recv_sems, scatter, i, k, receive):
    x, y, c = lax.axis_index("x"), lax.axis_index("y"), lax.axis_index("c")
    px, py = _xy_peers(x, y)[k]
    theirs, mine = 2 * px + py, 2 * x + y
    src = ins[i].at[theirs] if scatter[i] else ins[i].at[0]
    return pltpu.make_async_remote_copy(src_ref=src, dst_ref=lands[i].at[theirs if receive else mine], send_sem=send_sems.at[3 * i + k],
                                        recv_sem=recv_sems.at[3 * i + k], device_id=(px, py, c), device_id_type=MESH)


def send_to_chips_start(arrays, scatter, *, name):
    n = len(arrays)

    def body(*refs):
        send_sems, recv_sems = refs[2 * n], refs[2 * n + 1]
        ins, lands = refs[2 * n + 2:3 * n + 2], refs[3 * n + 2:4 * n + 2]
        token = refs[4 * n + 2]
        for i in range(n):
            for k in range(3):
                _chip_copy(ins, lands, send_sems, recv_sems, scatter, i, k, receive=False).start()
        token[...] = jnp.zeros_like(token)

    land_shapes = [(N_SHARD,) + a.shape[1:] for a in arrays]
    operands = [pltpu.with_memory_space_constraint(a, pltpu.HBM) for a in arrays]
    operands += [pltpu.with_memory_space_constraint(lax.empty(s, a.dtype), pltpu.HBM) for s, a in zip(land_shapes, arrays)]
    out_shape = ([pltpu.SemaphoreType.DMA((3 * n,)), pltpu.SemaphoreType.DMA((3 * n,))] + [pltpu.HBM(a.shape, a.dtype) for a in arrays]
                 + [pltpu.HBM(s, a.dtype) for s, a in zip(land_shapes, arrays)] + [_sds((8, LANES), F32)])
    res = pl.pallas_call(body, name=name, out_shape=out_shape, in_specs=[_HBM] * (2 * n),
                         out_specs=[_SEM, _SEM] + [_HBM] * (2 * n) + [pl.BlockSpec(memory_space=pltpu.VMEM)],
                         input_output_aliases={i: 2 + i for i in range(2 * n)}, compiler_params=pltpu.CompilerParams(**_SPLIT_COPY))(*operands)
    return (res[0], res[1], res[2:2 + n], res[2 + n:2 + 2 * n], scatter), res[-1]


def send_to_chips_wait(state, after, *, name):
    send_sems, recv_sems, arrays, lands, scatter = state
    n = len(arrays)

    def body(*refs):
        ins, landing = refs[:n], refs[n:2 * n]
        send_sems, recv_sems = refs[2 * n], refs[2 * n + 1]
        for i in range(n):
            for k in range(3):
                _chip_copy(ins, landing, send_sems, recv_sems, scatter, i, k, receive=True).wait_recv()
        for i in range(n):
            for k in range(3):
                _chip_copy(ins, landing, send_sems, recv_sems, scatter, i, k, receive=False).wait_send()

    out_shape = [pltpu.HBM(a.shape, a.dtype) for a in list(arrays) + list(lands)]
    res = pl.pallas_call(body, name=name, out_shape=out_shape, in_specs=[_HBM] * (2 * n) + [_SEM, _SEM] + [_ANY] * len(after),
                         out_specs=[_HBM] * (2 * n), input_output_aliases={i: i for i in range(2 * n)},
                         compiler_params=pltpu.CompilerParams(**_SPLIT_COPY))(*arrays, *lands, send_sems, recv_sems, *after)
    return res[:n], res[n:]


def swap_with_other_core(arrays, *, name, collective_id):
    n = len(arrays)

    def body(*refs):
        ins, lands = refs[:n], refs[n:2 * n]
        send_sems, recv_sems = refs[2 * n:]
        x, y, c = lax.axis_index("x"), lax.axis_index("y"), lax.axis_index("c")
        _handshake([(x, y, 1 - c)])
        copies = [pltpu.make_async_remote_copy(src_ref=ins[i], dst_ref=lands[i], send_sem=send_sems.at[i], recv_sem=recv_sems.at[i],
                                               device_id=(x, y, 1 - c), device_id_type=MESH) for i in range(n)]
        for cp in copies:
            cp.start()
        for cp in copies:
            cp.wait_recv()
        for cp in copies:
            cp.wait_send()

    return _sequencer_call(body, arrays, [_sds(a.shape, a.dtype) for a in arrays], [n, n], name, collective_id)


def _pack_rows(n_elems, row_multiple):
    rows = -(-n_elems // LANES)
    return -(-rows // row_multiple) * row_multiple


def _pack(arrays, rows, dtype):
    flat = jnp.concatenate([a.reshape(-1).astype(dtype) for a in arrays])
    return jnp.pad(flat, (0, rows * LANES - flat.shape[0])).reshape(rows, LANES)


def _unpack(packed, shapes):
    flat = packed.reshape(-1)
    out, off = [], 0
    for s in shapes:
        n = int(np.prod(s))
        out.append(flat[off:off + n].reshape(s))
        off += n
    return out


def all_gather_shards(shards, axes, dtype, row_multiple, tag):
    shapes = [s.shape for s in shards]
    rows = _pack_rows(sum(int(np.prod(s)) for s in shapes), row_multiple)
    packed = _pack(shards, rows, dtype).reshape(2, rows // 2, LANES)
    land = xy_exchange(packed, scatter=False, name=f"gather_xy_{tag}")
    both = core_exchange(land, send_other_half=False, name=f"gather_c_{tag}")
    per_shard = jnp.swapaxes(both, 0, 1).reshape(N_SHARD, rows, LANES)
    pieces = [_unpack(per_shard[s], shapes) for s in range(N_SHARD)]
    return [jnp.concatenate([pieces[s][i] for s in range(N_SHARD)], axis=ax) for i, ax in enumerate(axes)]


def _ordered_before(first, then):
    if then is None:
        return first, None
    return lax.optimization_barrier((first, then))


def reduce_between_cores(arrays, scatter, *, tag, collective_id, before=None):
    arrays, before = _ordered_before(arrays, before)
    land = send_other_half(arrays, name=f"reduce_core_send_{tag}", collective_id=collective_id)
    return (arrays, land, scatter, tag, collective_id), before


def reduce_between_chips(state, before=None):
    arrays, land, scatter, tag, collective_id = state
    chip = [add_core_halves(a, l, out_dtype=BF16 if sc else F32, name=f"reduce_core_add_{tag}_{i}")
            for i, (a, l, sc) in enumerate(zip(arrays, land, scatter))]
    sending, token = send_to_chips_start(chip, scatter, name=f"reduce_chip_start_{tag}")
    token, before = _ordered_before(token, before)
    return (sending, token, scatter, tag, collective_id), before


def reduce_finish(state, after):
    sending, token, scatter, tag, collective_id = state
    chip, land = send_to_chips_wait(sending, tuple(after) + (token,), name=f"reduce_chip_wait_{tag}")
    own = [sum_over_chips(ch, l, scatter=sc, name=f"reduce_chip_add_{tag}_{i}") for i, (ch, l, sc) in enumerate(zip(chip, land, scatter))]
    sib = swap_with_other_core(own, name=f"reduce_core_swap_{tag}", collective_id=collective_id + 2)
    return own, sib


def _ffn_layer_fwd(h, norm_g, w_up, cw, cb, w_down, tag):
    u, hn = norm_matmul(h, norm_g, w_up, name=f"ffn_up_{tag}")
    act = ffn_act_fwd(u, cw, cb, name=f"ffn_act_{tag}")
    out = matmul(act, w_down, add=h, name=f"ffn_down_{tag}")
    return out, (h, hn, u, act)


def _travel_layout(array):
    return BIG_ARRAYS[array][3], BIG_ARRAYS[array][4]


def _ffn_layer_bwd(saved, dout, norm_g, w_up, cw, cb, w_down, tag, d_w_down_other=None):
    h, hn, u, act = saved
    dact = matmul(dout, w_down, tb=True, name=f"ffn_down_dx_{tag}")
    d_w_down = matmul(act, dout, ta=True, layer=(int(tag), 2, d_w_down_other), name=f"ffn_down_dw_{tag}")
    du, dcw, dcb = ffn_act_bwd(u, cw, cb, dact, name=f"ffn_act_bwd_{tag}")
    dhn = matmul(du, w_up, tb=True, name=f"ffn_up_dx_{tag}")
    d_w_up = matmul(hn, du, ta=True, split=_travel_layout(f"ffn_w_up_{tag}"), name=f"ffn_up_dw_{tag}")
    dh, dg = norm_bwd(h, norm_g, dhn, dout, name=f"ffn_norm_bwd_{tag}")
    return dh, dg, d_w_up, dcw, dcb, d_w_down


def local_step(x, target, w, stage=lambda name, tensors, grads=None: tensors):
    g = {}
    tables = _ret_tables()
    x = stage("start", x)
    w_in_t = w["ret_gdn_w_in"]
    w_main = w_in_t[:MIX_MAIN]
    w_small = jnp.pad(w_in_t[MIX_MAIN:], ((0, LANES - 2 * N_HEADS), (0, 0)))
    a_log = jnp.pad(w["gdn_a_log"], ((0, 0), (0, LANES - N_HEADS)))
    dt_bias = jnp.pad(w["gdn_dt_bias"], ((0, 0), (0, LANES - N_HEADS)))

    p, hn0 = norm_matmul(x, w["norm_mix"][0:1], w_main, tb=True, name="mix0_in")
    hn0 = stage("normed", hn0)
    small = matmul(hn0, w_small, tb=True, name="mix0_in_small")
    y_ret, s_ret = ret_fwd(p, tables, name="ret_fwd")
    conv = gdn_conv_fwd(p, w["gdn_conv_w"], name="gdn_conv")
    y_gdn, s_gdn = gdn_fwd(conv, p, small, a_log, dt_bias, w["gdn_out_gain"], name="gdn_fwd")
    y0 = stage("mixed", jnp.concatenate([y_ret, y_gdn], axis=1))
    h1 = matmul(y0, w["ret_gdn_w_out"], add=x, name="mix0_out")
    h2, ffn0 = _ffn_layer_fwd(h1, w["norm_ffn"][0:1], w["ffn_w_up"][0], w["ffn_conv_w"][0], w["ffn_conv_b"][0:1], w["ffn_w_down"][0], "0")
    h2 = stage("layer0", h2)

    gx, hn1 = norm_matmul(h2, w["norm_mix"][1:2], w["lru_w_in"], name="mix1_in")
    lru_p = (w["lru_conv_w"], w["lru_conv_b"], w["lru_w_a"], w["lru_b_a"], w["lru_w_x"], w["lru_b_x"], w["lru_lambda"])
    y1 = lru_fwd(gx, *lru_p, name="lru_fwd")
    h3 = matmul(y1, w["lru_w_out"], add=h2, name="mix1_out")
    h4, ffn1 = _ffn_layer_fwd(h3, w["norm_ffn"][1:2], w["ffn_w_up"][1], w["ffn_conv_w"][1], w["ffn_conv_b"][1:2], w["ffn_w_down"][1], "1")

    loss, dh4, g["norm_final"] = final_fwd_bwd(h4, w["norm_final"], target, name="final")

    dh3, dgf1, dwu1, dcw1, dcb1, dwd1 = _ffn_layer_bwd(ffn1, dh4, w["norm_ffn"][1:2], w["ffn_w_up"][1], w["ffn_conv_w"][1],
                                                     w["ffn_conv_b"][1:2], w["ffn_w_down"][1], "1")
    g["ffn_w_up_1"] = dwu1
    dh3 = stage("grads0_ready", dh3, g)
    dy1 = matmul(dh3, w["lru_w_out"], tb=True, name="mix1_out_dx")
    g["lru_w_out"] = matmul(y1, dh3, ta=True, split=_travel_layout("lru_w_out"), name="mix1_out_dw")
    dgx, g["lru_conv_w"], g["lru_conv_b"], g["lru_w_a"], g["lru_b_a"], g["lru_w_x"], g["lru_b_x"], g["lru_lambda"] = lru_bwd(
        gx, *lru_p, dy1, name="lru_bwd")
    dgx = stage("grads0_send", dgx, g)
    dhn1 = matmul(dgx, w["lru_w_in"], tb=True, name="mix1_in_dx")
    g["lru_w_in"] = matmul(hn1, dgx, ta=True, split=_travel_layout("lru_w_in"), name="mix1_in_dw")
    dh2, dgm1 = norm_bwd(h2, w["norm_mix"][1:2], dhn1, dh3, name="mix1_norm_bwd")
    dh2 = stage("grads1_ready", dh2, g)

    dh1, dgf0, dwu0, dcw0, dcb0, dwd0 = _ffn_layer_bwd(ffn0, dh2, w["norm_ffn"][0:1], w["ffn_w_up"][0], w["ffn_conv_w"][0],
                                                     w["ffn_conv_b"][0:1], w["ffn_w_down"][0], "0", dwd1)
    g["ffn_w_up_0"] = dwu0
    g["ffn_w_down"] = dwd0
    dh1 = stage("grads2_ready", stage("grads1_send", dh1, g), g)
    dy0 = matmul(dh1, w["ret_gdn_w_out"], tb=True, name="mix0_out_dx")
    g["ret_gdn_w_out"] = matmul(y0, dh1, ta=True, split=_travel_layout("ret_gdn_w_out"), name="mix0_out_dw")
    dq_r, dk_r, dv_r, dg_r = ret_bwd(p, tables, s_ret, dy0, name="ret_bwd")
    dy0, dq_r = stage("grads2_send", (dy0, dq_r), g)
    dcq, dck, dcv, dg_d, dsmall, dal, ddt, dgain = gdn_bwd(conv, p, small, a_log, dt_bias, w["gdn_out_gain"], s_gdn, dy0, name="gdn_bwd")
    dconv = jnp.concatenate([dcq, dck, dcv], axis=1)
    dp_conv, g["gdn_conv_w"] = gdn_conv_bwd(p, w["gdn_conv_w"], dconv, name="gdn_conv_bwd")
    dp = jnp.concatenate([dq_r, dk_r, dv_r, dg_r, dp_conv, dg_d], axis=1)
    dhn0 = matmul(dp, w_main, name="mix0_in_dx")
    dhn0 = matmul(dsmall, w_small, add=dhn0, name="mix0_in_small_dx")
    d_w_main = matmul(dp, hn0, ta=True, name="mix0_in_dw")
    d_w_small = matmul(dsmall, hn0, ta=True, name="mix0_in_small_dw")
    g["ret_gdn_w_in"] = jnp.concatenate([d_w_main, d_w_small[:2 * N_HEADS]], axis=0)
    dx, dgm0 = norm_bwd(x, w["norm_mix"][0:1], dhn0, dh1, name="mix0_norm_bwd")

    g["gdn_a_log"] = dal[:, :N_HEADS]
    g["gdn_dt_bias"] = ddt[:, :N_HEADS]
    g["gdn_out_gain"] = dgain
    g["norm_mix"] = jnp.concatenate([dgm0, dgm1], axis=0)
    g["norm_ffn"] = jnp.concatenate([dgf0, dgf1], axis=0)
    g["ffn_conv_w"] = jnp.stack([dcw0, dcw1])
    g["ffn_conv_b"] = jnp.concatenate([dcb0, dcb1], axis=0)
    return loss, dx, g


WEIGHTS = ("norm_mix", "norm_ffn", "ret_gdn_w_in", "gdn_conv_w", "gdn_a_log", "gdn_dt_bias", "gdn_out_gain", "ret_gdn_w_out",
           "lru_w_in", "lru_conv_w", "lru_conv_b", "lru_w_a", "lru_b_a", "lru_w_x", "lru_b_x", "lru_lambda", "lru_w_out",
           "ffn_w_up", "ffn_conv_w", "ffn_conv_b", "ffn_w_down", "norm_final")
MATMUL_SHARDED = {"ret_gdn_w_in": 1, "ret_gdn_w_out": 0, "lru_w_in": 1, "lru_w_out": 0, "ffn_w_up": 2, "ffn_w_down": 1}
VECTOR_SHARDED = {"gdn_conv_w": 1, "lru_conv_w": 1, "lru_conv_b": 1, "lru_b_a": 1, "lru_b_x": 1, "lru_lambda": 1, "ffn_conv_w": 2}
SHARDED = {**MATMUL_SHARDED, **VECTOR_SHARDED}
REPLICATED = tuple(n for n in WEIGHTS if n not in SHARDED)
SQUEEZE = {"ret_gdn_w_in", "gdn_conv_w", "ret_gdn_w_out", "lru_w_in", "lru_conv_w", "lru_w_a", "lru_w_x", "lru_w_out"}
MIX_IN = MIX_MAIN + 2 * N_HEADS
BIG_ARRAYS = {
    "ret_gdn_w_in": ("ret_gdn_w_in", None, (MIX_IN, D_MODEL), (N_SHARD, MIX_IN // N_SHARD, 2, D_MODEL // 2), (2, 0, 1, 3)),
    "ret_gdn_w_out": ("ret_gdn_w_out", None, (2 * GROUP, D_MODEL), (N_SHARD, 2, GROUP // N_SHARD, D_MODEL), (1, 0, 2, 3)),
    "lru_w_in": ("lru_w_in", None, (D_MODEL, 2 * D_MODEL), (2, D_MODEL // 2, N_SHARD, 2 * D_MODEL // N_SHARD), (0, 2, 1, 3)),
    "lru_w_out": ("lru_w_out", None, (D_MODEL, D_MODEL), (N_SHARD, 2, D_MODEL // (2 * N_SHARD), D_MODEL), (1, 0, 2, 3)),
    "ffn_w_up_0": ("ffn_w_up", 0, (D_MODEL, 2 * D_FF), (2, D_MODEL // 2, N_SHARD, 2 * D_FF // N_SHARD), (0, 2, 1, 3)),
    "ffn_w_up_1": ("ffn_w_up", 1, (D_MODEL, 2 * D_FF), (2, D_MODEL // 2, N_SHARD, 2 * D_FF // N_SHARD), (0, 2, 1, 3)),
    "ffn_w_down": ("ffn_w_down", None, (2, D_FF, D_MODEL), (2, N_SHARD, D_FF // N_SHARD, D_MODEL), (0, 1, 2, 3)),
}
GATHER_GROUPS = (("ret_gdn_w_in",), ("ret_gdn_w_out", "ffn_w_up_0", "ffn_w_down"), ("lru_w_in", "lru_w_out", "ffn_w_up_1"))
REDUCE_GROUPS = (("ffn_w_up_1",), ("lru_w_in", "lru_w_out"), ("ffn_w_up_0", "ffn_w_down"), ("ret_gdn_w_out", "ret_gdn_w_in"))
BLOCK_WEIGHTS = ("lru_w_a", "lru_w_x")
GATHER_COLLECTIVE_ID = 1
REDUCE_COLLECTIVE_ID = GATHER_COLLECTIVE_ID + len(GATHER_GROUPS)


TRANSPOSED = ("ret_gdn_w_in",)


def _shard_of(array, tensors):
    weight, layer = BIG_ARRAYS[array][:2]
    t = tensors[weight]
    if weight in TRANSPOSED:
        return jnp.swapaxes(t, 1, 2)[0]
    return _local_view(weight, t) if layer is None else t[layer]


def _core_halves(array, shard):
    _, _, _, split, perm = BIG_ARRAYS[array]
    kept = [k for k in range(4) if k != perm[1]]
    order = [kept.index(perm[0]), kept.index(perm[2]), kept.index(perm[3])]
    return shard.reshape([split[k] for k in kept]).transpose(order)


def _local_view(name, a):
    if name in SQUEEZE:
        return a[0]
    if a.ndim == 1:
        return a[None, :]
    return a


def kernel(x, norm_mix, norm_ffn, ret_gdn_w_in, gdn_conv_w, gdn_a_log, gdn_dt_bias, gdn_out_gain, ret_gdn_w_out, lru_w_in, lru_conv_w, lru_conv_b, lru_w_a, lru_b_a, lru_w_x, lru_b_x, lru_lambda, lru_w_out, ffn_w_up, ffn_conv_w, ffn_conv_b, ffn_w_down, norm_final, loss_target, m_norm_mix, m_norm_ffn, m_ret_gdn_w_in, m_gdn_conv_w, m_gdn_a_log, m_gdn_dt_bias, m_gdn_out_gain, m_ret_gdn_w_out, m_lru_w_in, m_lru_conv_w, m_lru_conv_b, m_lru_w_a, m_lru_b_a, m_lru_w_x, m_lru_b_x, m_lru_lambda, m_lru_w_out, m_ffn_w_up, m_ffn_conv_w, m_ffn_conv_b, m_ffn_w_down, m_norm_final, v_norm_mix, v_norm_ffn, v_ret_gdn_w_in, v_gdn_conv_w, v_gdn_a_log, v_gdn_dt_bias, v_gdn_out_gain, v_ret_gdn_w_out, v_lru_w_in, v_lru_conv_w, v_lru_conv_b, v_lru_w_a, v_lru_b_a, v_lru_w_x, v_lru_b_x, v_lru_lambda, v_lru_w_out, v_ffn_w_up, v_ffn_conv_w, v_ffn_conv_b, v_ffn_w_down, v_norm_final):
    given = dict(norm_mix=norm_mix, norm_ffn=norm_ffn, ret_gdn_w_in=ret_gdn_w_in, gdn_conv_w=gdn_conv_w, gdn_a_log=gdn_a_log, gdn_dt_bias=gdn_dt_bias, gdn_out_gain=gdn_out_gain, ret_gdn_w_out=ret_gdn_w_out, lru_w_in=lru_w_in, lru_conv_w=lru_conv_w, lru_conv_b=lru_conv_b, lru_w_a=lru_w_a, lru_b_a=lru_b_a, lru_w_x=lru_w_x, lru_b_x=lru_b_x, lru_lambda=lru_lambda, lru_w_out=lru_w_out, ffn_w_up=ffn_w_up, ffn_conv_w=ffn_conv_w, ffn_conv_b=ffn_conv_b, ffn_w_down=ffn_w_down, norm_final=norm_final)
    mom1 = dict(norm_mix=m_norm_mix, norm_ffn=m_norm_ffn, ret_gdn_w_in=m_ret_gdn_w_in, gdn_conv_w=m_gdn_conv_w, gdn_a_log=m_gdn_a_log, gdn_dt_bias=m_gdn_dt_bias, gdn_out_gain=m_gdn_out_gain, ret_gdn_w_out=m_ret_gdn_w_out, lru_w_in=m_lru_w_in, lru_conv_w=m_lru_conv_w, lru_conv_b=m_lru_conv_b, lru_w_a=m_lru_w_a, lru_b_a=m_lru_b_a, lru_w_x=m_lru_w_x, lru_b_x=m_lru_b_x, lru_lambda=m_lru_lambda, lru_w_out=m_lru_w_out, ffn_w_up=m_ffn_w_up, ffn_conv_w=m_ffn_conv_w, ffn_conv_b=m_ffn_conv_b, ffn_w_down=m_ffn_w_down, norm_final=m_norm_final)
    mom2 = dict(norm_mix=v_norm_mix, norm_ffn=v_norm_ffn, ret_gdn_w_in=v_ret_gdn_w_in, gdn_conv_w=v_gdn_conv_w, gdn_a_log=v_gdn_a_log, gdn_dt_bias=v_gdn_dt_bias, gdn_out_gain=v_gdn_out_gain, ret_gdn_w_out=v_ret_gdn_w_out, lru_w_in=v_lru_w_in, lru_conv_w=v_lru_conv_w, lru_conv_b=v_lru_conv_b, lru_w_a=v_lru_w_a, lru_b_a=v_lru_b_a, lru_w_x=v_lru_w_x, lru_b_x=v_lru_b_x, lru_lambda=v_lru_lambda, lru_w_out=v_lru_w_out, ffn_w_up=v_ffn_w_up, ffn_conv_w=v_ffn_conv_w, ffn_conv_b=v_ffn_conv_b, ffn_w_down=v_ffn_w_down, norm_final=v_norm_final)

    local = {n: _local_view(n, a) for n, a in given.items()}

    core = lax.axis_index("c")
    chip = 2 * lax.axis_index("x") + lax.axis_index("y")
    is_my_chip = lax.broadcasted_iota(jnp.int32, (N_SHARD, 1, 1), 0) == chip

    def by_core(mine, other):
        return jnp.where(core == 0, jnp.stack([mine, other]), jnp.stack([other, mine]))

    vec_names, rp_names = list(VECTOR_SHARDED), list(REPLICATED)
    full = dict(zip(vec_names, all_gather_shards([local[n] for n in vec_names], [SHARDED[n] for n in vec_names], F32, 32, "p")))
    for n in rp_names:
        full[n] = local[n]
    in_flight = {}

    bf16_halves = {}

    def cast_halves(gi):
        if gi not in bf16_halves:
            bf16_halves[gi] = [_core_halves(a, _shard_of(a, given).astype(BF16)) for a in GATHER_GROUPS[gi]]
        return bf16_halves[gi]

    def launch(gi, after=None):
        halves = cast_halves(gi)
        if after is not None:
            halves, after = lax.optimization_barrier((halves, after))
        in_flight[gi] = (halves,) + gather_halves(halves, name=f"gather_weights_{gi}", collective_id=GATHER_COLLECTIVE_ID + gi)
        return after

    def land(gi, after):
        halves, lands, sibs = in_flight[gi]
        (lands, sibs), after = lax.optimization_barrier(((lands, sibs), after))
        for a, mine, got, passed in zip(GATHER_GROUPS[gi], halves, lands, sibs):
            weight, layer, full_shape, split, perm = BIG_ARRAYS[a]
            half_mine = jnp.where(is_my_chip, jnp.where(core == 0, mine[0], mine[1])[None], got)
            half_other = jnp.where(is_my_chip, jnp.where(core == 0, mine[1], mine[0])[None], passed)
            value = by_core(half_mine, half_other).transpose(tuple(np.argsort(perm))).reshape(full_shape)
            if layer is None:
                full[weight] = value
            else:
                full.setdefault(weight, [None, None])[layer] = value
        return after

    reducing = {}

    def reduce_ready(gi, grads, then=None, extra=()):
        def travelling(a):
            split, perm = _travel_layout(a)
            return grads[a] if grads[a].ndim == 4 else grads[a].reshape(split).transpose(perm)

        arrays = [travelling(a) for a in REDUCE_GROUPS[gi]] + list(extra)
        scatter = [True] * len(REDUCE_GROUPS[gi]) + [False] * len(extra)
        reducing[gi], then = reduce_between_cores(arrays, scatter, tag=str(gi), collective_id=REDUCE_COLLECTIVE_ID + 3 * gi, before=then)
        return then

    def reduce_send(gi, then=None):
        reducing[gi], then = reduce_between_chips(reducing[gi], before=then)
        return then

    def stage(name, tensors, grads=None):
        if name == "start":
            launch(0)
            launch(1)
            fillers = (cast_halves(2), [full[n] for n in vec_names])
            (bf16_halves[2], gathered_small), tensors = lax.optimization_barrier((fillers, tensors))
            full.update(zip(vec_names, gathered_small))
            return land(0, tensors)
        if name == "normed":
            return launch(2, tensors)
        if name in ("mixed", "layer0"):
            return land({"mixed": 1, "layer0": 2}[name], tensors)
        gi = int(name[len("grads")])
        return reduce_ready(gi, grads, tensors) if name.endswith("_ready") else reduce_send(gi, tensors)

    small_names = [n for n in rp_names if n not in BLOCK_WEIGHTS] + vec_names

    loss_part, dx, grads = local_step(x[0], loss_target[0], full, stage)
    small_shapes = [grads[n].shape for n in small_names] + [(1, 1)]
    small_rows = _pack_rows(sum(int(np.prod(s)) for s in small_shapes), 16)
    small = _pack([grads[n] for n in small_names] + [loss_part[:, :1]], small_rows, F32).reshape(2, 1, small_rows // 2, LANES)
    last = len(REDUCE_GROUPS) - 1
    halves_of_blocks = [grads[n].reshape(2, 1, LRU_BLOCKS * HEAD // 2, HEAD) for n in BLOCK_WEIGHTS]
    reduce_ready(last, grads, extra=[small] + halves_of_blocks)
    reduce_send(last)
    reduced, result = {}, {}

    def finish(gi, after):
        g_own, g_sib = reduce_finish(reducing[gi], after)
        reduced.update(zip(list(REDUCE_GROUPS[gi]) + ["small"] + list(BLOCK_WEIGHTS), zip(g_own, g_sib)))

    def update(n):
        if n in TRANSPOSED:
            w3, m3, v3 = (jnp.swapaxes(t, 1, 2) for t in (given[n], mom1[n], mom2[n]))
            result[n] = tuple(jnp.swapaxes(t, 1, 2) for t in adamw_column_halves(w3, m3, v3, *reduced[n], name=f"adamw_{n}"))
            return
        done = None
        for a in (k for k, spec in BIG_ARRAYS.items() if spec[0] == n):
            r, cols = reduced[a][0].shape
            layer = BIG_ARRAYS[a][1] or 0
            w3, m3, v3 = (t if BIG_ARRAYS[a][1] is not None else t.reshape(1, 2 * r, cols) for t in (given[n], mom1[n], mom2[n]))
            done = adamw_halves(w3, m3, v3, *reduced[a], layer=layer, prev=done, name=f"adamw_{a}")
        result[n] = done

    for gi in range(last):
        finish(gi, (dx, reducing[last][1]))
    late = {BIG_ARRAYS[a][0] for a in REDUCE_GROUPS[last]}
    for n in MATMUL_SHARDED:
        if n not in late:
            update(n)
    finish(last, tuple(result[n][0] for n in MATMUL_SHARDED if n not in late))
    for n in MATMUL_SHARDED:
        if n in late:
            update(n)

    for n in BLOCK_WEIGHTS:
        w3, m3, v3 = (t.reshape(1, LRU_BLOCKS * HEAD, HEAD) for t in (given[n], mom1[n], mom2[n]))
        result[n] = adamw_halves(w3, m3, v3, *reduced[n], name=f"adamw_{n}")

    *small_sums, loss_sum = _unpack(by_core(*reduced["small"]).reshape(small_rows, LANES), small_shapes)
    loss = loss_sum[0, 0]
    g_small = dict(zip(small_names, small_sums))
    for n in vec_names:
        size = local[n].shape[SHARDED[n]]
        g_small[n] = lax.dynamic_slice_in_dim(g_small[n], chip * size, size, axis=SHARDED[n])
    views = [[_local_view(n, src[n]) for n in small_names] for src in (given, mom1, mom2)]
    d_s, m_s, v_s = adamw_many(views[0], [g_small[n] for n in small_names], views[1], views[2], name="adamw_small")
    for n, d, nm, nv in zip(small_names, d_s, m_s, v_s):
        result[n] = (g_small[n], d, nm, nv)

    outs = [[result[n][k].reshape(given[n].shape) for n in WEIGHTS] for k in range(4)]
    return (loss, dx[None], *outs[0], *outs[1], *outs[2], *outs[3])
```

```python
import functools

import numpy as np
import jax
import jax.numpy as jnp
from jax import lax
from jax.experimental import pallas as pl
from jax.experimental.pallas import tpu as pltpu
from jax.experimental.pallas import tpu_sc as plsc

F32 = jnp.float32
BF16 = jnp.bfloat16
HI = lax.Precision.HIGHEST
MESH = pl.DeviceIdType.MESH

SEQ = 2048
D_MODEL = 1024
N_HEADS = 4
HEAD = 128
RET_CHUNK = 128
RET_CHUNKS_PER_STEP = 2
GDN_CHUNK = 64
GDN_CHUNKS_PER_STEP = 8
GROUP = N_HEADS * HEAD
MIX_MAIN = 8 * GROUP
D_FF = 2816
LRU_BLOCKS = 8
LRU_C = 8.0
ROPE_BASE = 10000.0
EPS = 1e-6
N_SHARD = 4
LANES = 128

ADAM_LR, ADAM_B1, ADAM_B2, ADAM_EPS, ADAM_WD, ADAM_STEP = 0.001, 0.9, 0.999, 1e-08, 0.01, 10

VMEM_LIMIT_BYTES = 56 * 1024 * 1024

_roll = pltpu.roll


def _params(**kw):
    return pltpu.CompilerParams(vmem_limit_bytes=VMEM_LIMIT_BYTES, **kw)


def _sds(shape, dtype):
    return jax.ShapeDtypeStruct(tuple(shape), dtype)


def _shift_raw(x, d):
    n = x.shape[0]
    t = lax.broadcasted_iota(jnp.int32, x.shape, 0)
    if d > 0:
        return jnp.where(t >= d, _roll(x, d, 0), 0.0)
    return jnp.where(t < n + d, _roll(x, n + d, 0), 0.0)


@functools.partial(jax.custom_vjp, nondiff_argnums=(1,))
def shift_rows(x, d):
    return _shift_raw(x, d)


def _shift_fwd(x, d):
    return _shift_raw(x, d), None


def _shift_bwd(d, _, g):
    return (_shift_raw(g, -d),)


shift_rows.defvjp(_shift_fwd, _shift_bwd)


@jax.custom_vjp
def swap_halves(x):
    return _roll(x, HEAD // 2, 1)


def _swap_fwd(x):
    return _roll(x, HEAD // 2, 1), None


def _swap_bwd(_, g):
    return (_roll(g, HEAD // 2, 1),)


swap_halves.defvjp(_swap_fwd, _swap_bwd)


SCAN_BLOCK_ROWS = 64


def _scan_block(a, u, reverse):
    n = a.shape[0]
    t = lax.broadcasted_iota(jnp.int32, a.shape, 0)
    d = 1
    while d < n:
        if reverse:
            m = t < n - d
            a_s, u_s = _roll(a, n - d, 0), _roll(u, n - d, 0)
        else:
            m = t >= d
            a_s, u_s = _roll(a, d, 0), _roll(u, d, 0)
        u = a * jnp.where(m, u_s, 0.0) + u
        a = a * jnp.where(m, a_s, 1.0)
        d *= 2
    return a, u


def _scan_raw(a, u, reverse):
    n = a.shape[0]
    blocks = range(n // SCAN_BLOCK_ROWS)
    out = [None] * len(blocks)
    entering = None
    for b in (reversed(blocks) if reverse else blocks):
        rows = slice(b * SCAN_BLOCK_ROWS, (b + 1) * SCAN_BLOCK_ROWS)
        a_run, h = _scan_block(a[rows], u[rows], reverse)
        if entering is not None:
            h = a_run * entering + h
        out[b] = h
        entering = h[:1] if reverse else h[SCAN_BLOCK_ROWS - 1:]
    return jnp.concatenate(out, axis=0)


@jax.custom_vjp
def lin_scan(a, u):
    return _scan_raw(a, u, False)


def _lin_scan_fwd(a, u):
    hs = _scan_raw(a, u, False)
    return hs, (a, hs)


def _lin_scan_bwd(res, g):
    a, hs = res
    lam = _scan_raw(_shift_raw(a, -1), g, True)
    return lam * _shift_raw(hs, 1), lam


lin_scan.defvjp(_lin_scan_fwd, _lin_scan_bwd)


def _bdot(a, b, dims=(((1,), (0,)), ((), ()))):
    return lax.dot_general(a.astype(BF16), b.astype(BF16), dims, preferred_element_type=F32)


def _each(f, *seqs):
    return tuple(f(*a) for a in zip(*seqs))


def _split_bf16(a):
    hi = a.astype(BF16)
    return hi, (a - hi.astype(F32)).astype(BF16)


def _dot3_raw(a_s, b_s):
    a_hl = _each(_split_bf16, a_s)
    b_hl = _each(_split_bf16, b_s)
    hh = _each(lambda a, b: _bdot(a[0], b[0]), a_hl, b_hl)
    hl = _each(lambda a, b: _bdot(a[0], b[1]), a_hl, b_hl)
    lh = _each(lambda a, b: _bdot(a[1], b[0]), a_hl, b_hl)
    return _each(lambda x, y, z: x + (y + z), hh, hl, lh)


@jax.custom_vjp
def dot3(a_s, b_s):
    return _dot3_raw(a_s, b_s)


def _dot3_fwd(a_s, b_s):
    return _dot3_raw(a_s, b_s), (a_s, b_s)


def _dot3_bwd(res, g_s):
    a_s, b_s = res
    return (_each(lambda g, b: _bdot(g, b, (((1,), (1,)), ((), ()))), g_s, b_s),
            _each(lambda a, g: _bdot(a, g, (((0,), (0,)), ((), ()))), a_s, g_s))


dot3.defvjp(_dot3_fwd, _dot3_bwd)


def _eye(n):
    i = lax.broadcasted_iota(jnp.int32, (n, n), 0)
    j = lax.broadcasted_iota(jnp.int32, (n, n), 1)
    return (i == j).astype(F32)


def _unit_lower_inverse_raw(lmats):
    n = lmats[0].shape[0]
    eye = _eye(n)
    ps = _each(lambda l: -l, lmats)
    invs = _each(lambda x: eye + x, ps)
    k = 1
    while 2 * k < n:
        ps = _each(lambda p: _bdot(p, p), ps)
        invs = _each(lambda inv, p: inv + _bdot(inv, p), invs, ps)
        k *= 2
    prods = _dot3_raw(lmats, invs)
    resids = _each(lambda inv, pr: eye - inv - pr, invs, prods)
    return _each(lambda inv, r: inv + _bdot(inv, r), invs, resids)


@jax.custom_vjp
def unit_lower_inverse(lmats):
    return _unit_lower_inverse_raw(lmats)


def _uli_fwd(lmats):
    invs = _unit_lower_inverse_raw(lmats)
    return invs, invs


def _uli_bwd(invs, g_s):
    ms = _each(lambda inv, g: _bdot(inv, g, (((0,), (0,)), ((), ()))), invs, g_s)
    return (_each(lambda m, inv: -_bdot(m, inv, (((1,), (1,)), ((), ()))), ms, invs),)


unit_lower_inverse.defvjp(_uli_fwd, _uli_bwd)


def _cumsum_raw(x, reverse):
    n = x.shape[0]
    t = lax.broadcasted_iota(jnp.int32, x.shape, 0)
    d = 1
    while d < n:
        if reverse:
            x = x + jnp.where(t < n - d, _roll(x, n - d, 0), 0.0)
        else:
            x = x + jnp.where(t >= d, _roll(x, d, 0), 0.0)
        d *= 2
    return x


@jax.custom_vjp
def cumsum_rows(x):
    return _cumsum_raw(x, False)


def _cumsum_fwd(x):
    return _cumsum_raw(x, False), None


def _cumsum_bwd(_, g):
    return (_cumsum_raw(g, True),)


cumsum_rows.defvjp(_cumsum_fwd, _cumsum_bwd)


_NT = (((1,), (1,)), ((), ()))
_TN = (((0,), (0,)), ((), ()))


def _softplus(x):
    return jnp.maximum(x, 0.0) + jnp.log1p(jnp.exp(-jnp.abs(x)))


def _expm1_nonpos(x):
    poly = x * (1.0 + x * (0.5 + x * (1.0 / 6 + x * (1.0 / 24 + x * (1.0 / 120 + x * (1.0 / 720))))))
    return jnp.where(x > -0.25, poly, jnp.exp(x) - 1.0)


def _rms(x):
    return x * lax.rsqrt(jnp.mean(x * x, axis=-1, keepdims=True) + EPS)


def _causal_conv(x, w, width):
    y = w[width - 1:width, :] * x
    for j in range(width - 1):
        y = y + w[j:j + 1, :] * shift_rows(x, width - 1 - j)
    return y


def _norm_fn(x, g):
    return _rms(x) * g


def _ffn_act_fn(ug, uv, wg, wv, bg, bv):
    return jax.nn.silu(_causal_conv(ug, wg, 3) + bg) * (_causal_conv(uv, wv, 3) + bv)


def _gdn_conv_fn(x, w):
    return jax.nn.silu(_causal_conv(x, w, 4))


def _lru_fn(gate, x, cw, cb, wa, ba, wx, bx, lam):
    xr = _causal_conv(x, cw, 4) + cb
    r = jax.nn.sigmoid(_bdot(xr, wa) + ba)
    i = jax.nn.sigmoid(_bdot(xr, wx) + bx)
    log_a = -LRU_C * r * _softplus(-lam)
    a = jnp.exp(log_a)
    u = jnp.sqrt(-_expm1_nonpos(2.0 * log_a)) * (i * xr)
    hs = lin_scan(a, u)
    return jax.nn.gelu(gate) * hs


def _ret_fn(qs, ks, vs, gates, states, cos2, sin2, dmasks, ktails, qdecs, cdecs):
    c = RET_CHUNK
    n_heads = len(qs)
    n_chunks = qs[0].shape[0] // c
    units = tuple((ci, h) for ci in range(n_chunks) for h in range(n_heads))

    def rows(x, ci):
        return x[ci * c:(ci + 1) * c]

    qrs = tuple(rows(qs[h], ci) * rows(cos2, ci) + swap_halves(rows(qs[h], ci)) * rows(sin2, ci) for ci, h in units)
    krs = tuple((rows(ks[h], ci) * rows(cos2, ci) + swap_halves(rows(ks[h], ci)) * rows(sin2, ci)) * (HEAD ** -0.5) for ci, h in units)
    vus = tuple(rows(vs[h], ci) for ci, h in units)
    scores = tuple(_bdot(q, k, _NT) * dmasks[h] for q, k, (_, h) in zip(qrs, krs, units))
    intra = _each(lambda sc, v: _bdot(sc, v), scores, vus)
    outs = []
    for ci in range(n_chunks):
        mine = slice(ci * n_heads, (ci + 1) * n_heads)
        inter = _each(lambda q, d, s: _bdot(q * d, s), qrs[mine], qdecs, states)
        outs.append(_each(lambda a, b: a + b, intra[mine], inter))
        states = _each(lambda s, cd, k, kt, v: s * cd + _bdot(k * kt, v, _TN), states, cdecs, krs[mine], ktails, vus[mine])
    ys = tuple(_rms(jnp.concatenate([outs[ci][h] for ci in range(n_chunks)], axis=0)) * jax.nn.silu(gates[h]) for h in range(n_heads))
    return ys, states


def _pick_lane(x, lane_idx):
    lane = lax.broadcasted_iota(jnp.int32, x.shape, 1)
    return jnp.sum(jnp.where(lane == lane_idx, x, 0.0), axis=1, keepdims=True)


def _l2norm(x):
    return x * lax.rsqrt(jnp.sum(x * x, axis=-1, keepdims=True) + EPS)


def _gdn_fn(qcs, kcs, vcs, gates, small, a_log, dt_bias, gain, states):
    c = GDN_CHUNK
    n_heads = len(qcs)
    n_chunks = qcs[0].shape[0] // c
    units = tuple((ci, h) for ci in range(n_chunks) for h in range(n_heads))

    def unit_rows(per_head):
        return tuple(per_head[h][ci * c:(ci + 1) * c] for ci, h in units)

    smalls = tuple(small[ci * c:(ci + 1) * c] for ci, _ in units)
    heads = tuple(h for _, h in units)
    intra = _gdn_intra(unit_rows(qcs), unit_rows(kcs), unit_rows(vcs), smalls, heads, a_log, dt_bias)
    outs = []
    for ci in range(n_chunks):
        mine = slice(ci * n_heads, (ci + 1) * n_heads)
        os_, states = _gdn_inter(*(part[mine] for part in intra), states)
        outs.append(os_)
    ys = tuple(_rms(jnp.concatenate([outs[ci][h] for ci in range(n_chunks)], axis=0)) * gain * jax.nn.silu(gates[h])
               for h in range(n_heads))
    return ys, states


def _gdn_inter(qs, ks, us, ws, attns, gcs, g_lasts, states):
    v_news = _each(lambda u, w, s: u - _bdot(w, s), us, ws, states)
    inter = _each(lambda q, gc, s: _bdot(q * jnp.exp(gc), s), qs, gcs, states)
    os_ = _each(lambda x, a, v: x + _bdot(a, v), inter, attns, v_news)
    new_states = _each(lambda s, gl, k, gc, v: s * jnp.exp(gl) + _bdot(k * jnp.exp(gl - gc), v, _TN), states, g_lasts, ks, gcs, v_news)
    return os_, new_states


def _gdn_intra(qcs, kcs, vcs, smalls, heads, a_log, dt_bias):
    c = GDN_CHUNK
    qs = _each(lambda x: _l2norm(x) * (HEAD ** -0.5), qcs)
    ks = _each(_l2norm, kcs)
    betas = _each(lambda sm, h: jax.nn.sigmoid(_pick_lane(sm, h)), smalls, heads)
    gs = _each(lambda sm, h: -jnp.exp(_pick_lane(a_log, h)) * _softplus(_pick_lane(sm, h + N_HEADS) + _pick_lane(dt_bias, h)),
               smalls, heads)
    i = lax.broadcasted_iota(jnp.int32, (c, c), 0)
    j = lax.broadcasted_iota(jnp.int32, (c, c), 1)
    tril = i >= j
    gcs = _each(lambda g: cumsum_rows(jnp.broadcast_to(g, (c, LANES)))[:, :1], gs)
    gc_rows = _each(lambda gc: jnp.broadcast_to(gc, (c, c)), gcs)
    decays = _each(lambda r: jnp.where(tril, jnp.exp(jnp.where(tril, r - r.T, 0.0)), 0.0), gc_rows)
    kbs = _each(lambda k, b: k * b, ks, betas)
    lmats = _each(lambda kb, k, d: jnp.where(i > j, _bdot(kb, k, _NT) * d, 0.0), kbs, ks, decays)
    attns = _each(lambda q, k, d: jnp.where(tril, _bdot(q, k, _NT) * d, 0.0), qs, ks, decays)
    invs = unit_lower_inverse(lmats)
    us = dot3(invs, _each(lambda v, b: v * b, vcs, betas))
    ws = dot3(invs, _each(lambda kb, gc: kb * jnp.exp(gc), kbs, gcs))
    g_lasts = _each(lambda g: jnp.sum(g, axis=0, keepdims=True), gs)
    return qs, ks, us, ws, attns, gcs, g_lasts


def _final_fn(h, g, target):
    y = _rms(h) * g
    return 0.5 * jnp.sum(jnp.mean(jnp.square(y - target), axis=-1, keepdims=True), axis=0, keepdims=True)


def _tile(n, candidates):
    for t in candidates:
        if n % t == 0:
            return t
    raise ValueError(f"no tile for {n}")


MATMUL_RESIDENT_LHS_BYTES = 8 * 1024 * 1024


def matmul(a, b, *, ta=False, tb=False, add=None, out_dtype=F32, tm=None, tn=None, split=None, layer=None, column_halves=None, name):
    m = a.shape[1] if ta else a.shape[0]
    k = a.shape[0] if ta else a.shape[1]
    n = b.shape[0] if tb else b.shape[1]
    assert k == (b.shape[1] if tb else b.shape[0])
    out_shape, out_block, out_index = (m, n), None, lambda i, j: (i, j)
    if split is not None:
        dims4, perm = split
        out_shape = tuple(dims4[p] for p in perm)
        r, cols = out_shape[2:]
        tm, tn = m, tn or _tile(cols, (1408, 512))
        cb = cols // tn
        if perm == (0, 2, 1, 3):
            out_block, out_index = (2, None, r, tn), lambda i, j: (0, j // cb, 0, j % cb)
        elif perm == (1, 0, 2, 3):
            out_block, out_index = (2, N_SHARD, r, tn), lambda i, j: (0, 0, 0, j)
        else:
            raise ValueError(perm)
    if tm is None and not ta and m * k * a.dtype.itemsize <= MATMUL_RESIDENT_LHS_BYTES:
        tm = m
    tm = tm or _tile(m, (1024, 512, 1408, 256, 128))
    tn = tn or _tile(n, (512, 1408, 256, 128))
    aliases, prev, keep_rows = {}, None, None
    if layer is not None:
        index, count, prev = layer
        out_shape, out_block, out_index = (count, m, n), (None, tm, tn), lambda i, j: (index, i, j)
    if column_halves is not None:
        total_rows, first_row, keep_rows, prev = column_halves
        tn = n // 2
        rows_out = keep_rows or tm
        out_shape, out_block = (2, total_rows, tn), (None, rows_out, tn)
        out_index = lambda i, j: (j, first_row // rows_out + i, 0)
    dims = (((0 if ta else 1,), (1 if tb else 0,)), ((), ()))

    def body(a_ref, b_ref, *rest):
        acc = lax.dot_general(a_ref[...].astype(BF16), b_ref[...].astype(BF16), dims, preferred_element_type=F32)
        if add is not None:
            acc = acc + rest[0][...]
        o_ref = rest[-1]
        acc = acc.astype(out_dtype)
        if split is not None and split[1] == (1, 0, 2, 3):
            rows = o_ref.shape[2]
            for s in range(N_SHARD):
                for h in range(2):
                    o_ref[h, s] = acc[(2 * s + h) * rows:(2 * s + h + 1) * rows]
        elif keep_rows is not None:
            o_ref[...] = acc[:keep_rows]
        else:
            o_ref[...] = acc.reshape(o_ref.shape)

    a_spec = pl.BlockSpec((k, tm), lambda i, j: (0, i)) if ta else pl.BlockSpec((tm, k), lambda i, j: (i, 0))
    b_spec = pl.BlockSpec((tn, k), lambda i, j: (j, 0)) if tb else pl.BlockSpec((k, tn), lambda i, j: (0, j))
    o_spec = pl.BlockSpec(out_block or (tm, tn), out_index)
    in_specs, args = [a_spec, b_spec], [a, b]
    if add is not None:
        in_specs.append(o_spec)
        args.append(add)
    if prev is not None:
        aliases = {len(args): 0}
        in_specs.append(pl.BlockSpec(memory_space=pl.ANY))
        args.append(prev)
    return pl.pallas_call(body, out_shape=_sds(out_shape, out_dtype), grid=(m // tm, n // tn), in_specs=in_specs,
                          out_specs=o_spec, input_output_aliases=aliases, compiler_params=_params(), name=name)(*args)


def norm_matmul(x, g, b, *, tb=False, name):
    t, k = x.shape
    n = b.shape[0] if tb else b.shape[1]
    tn = _tile(n, (512, 1408, 256, 128))
    dims = (((1,), (1 if tb else 0,)), ((), ()))

    def body(x_ref, g_ref, b_ref, o_ref, hn_ref):
        @pl.when(pl.program_id(0) == 0)
        def _():
            hn_ref[...] = _norm_fn(x_ref[...], g_ref[...]).astype(BF16)

        o_ref[...] = lax.dot_general(hn_ref[...], b_ref[...].astype(BF16), dims, preferred_element_type=F32)

    b_spec = pl.BlockSpec((tn, k), lambda j: (j, 0)) if tb else pl.BlockSpec((k, tn), lambda j: (0, j))
    whole = pl.BlockSpec((t, k), lambda j: (0, 0))
    return pl.pallas_call(body, out_shape=(_sds((t, n), F32), _sds((t, k), BF16)), grid=(n // tn,),
                          in_specs=[whole, pl.BlockSpec((1, k), lambda j: (0, 0)), b_spec],
                          out_specs=(pl.BlockSpec((t, tn), lambda j: (0, j)), whole), compiler_params=_params(), name=name)(x, g, b)


ROW_TILE = 256


def norm_bwd(x, g, dy, dres, *, name):
    t, d = x.shape

    def body(x_ref, g_ref, dy_ref, dres_ref, dx_ref, dg_ref):
        _, vjp = jax.vjp(_norm_fn, x_ref[...], g_ref[...])
        dx, dg = vjp(dy_ref[...])
        dx_ref[...] = dx + dres_ref[...]

        @pl.when(pl.program_id(0) == 0)
        def _():
            dg_ref[...] = jnp.zeros_like(dg_ref)

        dg_ref[...] += dg

    row = pl.BlockSpec((ROW_TILE, d), lambda i: (i, 0))
    vec = pl.BlockSpec((1, d), lambda i: (0, 0))
    return pl.pallas_call(body, out_shape=(_sds((t, d), F32), _sds((1, d), F32)), grid=(t // ROW_TILE,),
                          in_specs=[row, vec, row, row], out_specs=(row, vec), compiler_params=_params(), name=name)(x, g, dy, dres)


def final_fwd_bwd(h, g, target, *, name):
    t, d = h.shape

    def body(h_ref, g_ref, t_ref, loss_ref, dh_ref, dg_ref):
        tgt = t_ref[...]
        loss, vjp = jax.vjp(lambda hh, gg: _final_fn(hh, gg, tgt), h_ref[...], g_ref[...])
        dh, dg = vjp(jnp.ones((1, 1), F32))
        dh_ref[...] = dh

        @pl.when(pl.program_id(0) == 0)
        def _():
            dg_ref[...] = jnp.zeros_like(dg_ref)
            loss_ref[...] = jnp.zeros_like(loss_ref)

        dg_ref[...] += dg
        loss_ref[...] += jnp.broadcast_to(loss, loss_ref.shape)

    row = pl.BlockSpec((ROW_TILE, d), lambda i: (i, 0))
    vec = pl.BlockSpec((1, d), lambda i: (0, 0))
    return pl.pallas_call(body, out_shape=(_sds((1, LANES), F32), _sds((t, d), F32), _sds((1, d), F32)), grid=(t // ROW_TILE,),
                          in_specs=[row, vec, row], out_specs=(pl.BlockSpec((1, LANES), lambda i: (0, 0)), row, vec),
                          compiler_params=_params(), name=name)(h, g, target)


FFN_FWD_COLS = 256
FFN_BWD_COLS = 128


def ffn_act_fwd(u, cw, cb, *, name):
    t = u.shape[0]
    w = FFN_FWD_COLS
    nb = D_FF // w

    def body(ug_ref, uv_ref, wg_ref, wv_ref, bg_ref, bv_ref, o_ref):
        o_ref[...] = _ffn_act_fn(ug_ref[...], uv_ref[...], wg_ref[...], wv_ref[...], bg_ref[...], bv_ref[...]).astype(BF16)

    def col(rows, off):
        return pl.BlockSpec((rows, w), lambda j: (0, j + off))

    return pl.pallas_call(body, out_shape=_sds((t, D_FF), BF16), grid=(nb,),
                          in_specs=[col(t, 0), col(t, nb), col(3, 0), col(3, nb), col(1, 0), col(1, nb)],
                          out_specs=col(t, 0), compiler_params=_params(), name=name)(u, u, cw, cw, cb, cb)


def _put_column_blocks(step, n_steps, blocks, dst_ref, width, stage_ref, sems):
    def copies(at):
        slot = at % 2
        return [pltpu.make_async_copy(stage_ref.at[slot, p], dst_ref.at[:, pl.ds(pl.multiple_of((p * n_steps + at) * width, LANES), width)],
                                      sems.at[slot, p]) for p in range(len(blocks))]

    @pl.when(step >= 2)
    def _():
        for cp in copies(step - 2):
            cp.wait()

    for p, value in enumerate(blocks):
        stage_ref[step % 2, p] = value
    for cp in copies(step):
        cp.start()

    @pl.when(step == n_steps - 1)
    def _():
        for cp in copies(step - 1) + copies(step):
            cp.wait()


def ffn_act_bwd(u, cw, cb, da, *, name):
    t = u.shape[0]
    w = FFN_BWD_COLS
    nb = D_FF // w

    def body(ug_ref, uv_ref, wg_ref, wv_ref, bg_ref, bv_ref, da_ref, dug_ref, duv_ref, dwg_ref, dwv_ref, dbg_ref, dbv_ref):
        _, vjp = jax.vjp(_ffn_act_fn, ug_ref[...], uv_ref[...], wg_ref[...], wv_ref[...], bg_ref[...], bv_ref[...])
        dug, duv, dwg, dwv, dbg, dbv = vjp(da_ref[...])
        dug_ref[...] = dug.astype(BF16)
        duv_ref[...] = duv.astype(BF16)
        dwg_ref[...] = dwg
        dwv_ref[...] = dwv
        dbg_ref[...] = dbg
        dbv_ref[...] = dbv

    def col(rows, off):
        return pl.BlockSpec((rows, w), lambda j: (0, j + off))

    outs = pl.pallas_call(
        body, out_shape=(_sds((t, D_FF), BF16), _sds((t, D_FF), BF16), _sds((3, D_FF), F32), _sds((3, D_FF), F32),
                         _sds((1, D_FF), F32), _sds((1, D_FF), F32)),
        grid=(nb,), in_specs=[col(t, 0), col(t, nb), col(3, 0), col(3, nb), col(1, 0), col(1, nb), col(t, 0)],
        out_specs=(col(t, 0), col(t, 0), col(3, 0), col(3, 0), col(1, 0), col(1, 0)), compiler_params=_params(), name=name,
    )(u, u, cw, cw, cb, cb, da)
    dug, duv, dwg, dwv, dbg, dbv = outs
    return jnp.concatenate([dug, duv], axis=1), jnp.concatenate([dwg, dwv], axis=1), jnp.concatenate([dbg, dbv], axis=1)


GDN_CONV_COLS = 256
GDN_CONV_OFF = 4 * GROUP


def gdn_conv_fwd(p, cw, *, name):
    t = p.shape[0]
    w = GDN_CONV_COLS
    nb = 3 * GROUP // w
    off = GDN_CONV_OFF // w

    def body(x_ref, w_ref, o_ref):
        o_ref[...] = _gdn_conv_fn(x_ref[...], w_ref[...])

    return pl.pallas_call(body, out_shape=_sds((t, 3 * GROUP), F32), grid=(nb,),
                          in_specs=[pl.BlockSpec((t, w), lambda j: (0, j + off)), pl.BlockSpec((4, w), lambda j: (0, j))],
                          out_specs=pl.BlockSpec((t, w), lambda j: (0, j)), compiler_params=_params(), name=name)(p, cw)


def gdn_conv_bwd(p, cw, dc, *, name):
    t = p.shape[0]
    w = GDN_CONV_COLS
    nb = 3 * GROUP // w
    off = GDN_CONV_OFF // w

    def body(x_ref, w_ref, dc_ref, dx_ref, dw_ref):
        _, vjp = jax.vjp(_gdn_conv_fn, x_ref[...], w_ref[...])
        dx, dw = vjp(dc_ref[...])
        dx_ref[...] = dx.astype(BF16)
        dw_ref[...] = dw

    blk = pl.BlockSpec((t, w), lambda j: (0, j))
    wblk = pl.BlockSpec((4, w), lambda j: (0, j))
    return pl.pallas_call(body, out_shape=(_sds((t, 3 * GROUP), BF16), _sds((4, 3 * GROUP), F32)), grid=(nb,),
                          in_specs=[pl.BlockSpec((t, w), lambda j: (0, j + off)), wblk, blk], out_specs=(blk, wblk),
                          compiler_params=_params(), name=name)(p, cw, dc)


def _lru_specs(t):
    w = D_MODEL // LRU_BLOCKS
    gate = pl.BlockSpec((t, w), lambda j: (0, j))
    xin = pl.BlockSpec((t, w), lambda j: (0, j + LRU_BLOCKS))
    cw = pl.BlockSpec((4, w), lambda j: (0, j))
    vec = pl.BlockSpec((1, w), lambda j: (0, j))
    mat = pl.BlockSpec((None, w, w), lambda j: (j, 0, 0))
    return gate, xin, cw, vec, mat


def lru_fwd(gx, cw, cb, wa, ba, wx, bx, lam, *, name):
    t = gx.shape[0]
    gate, xin, cws, vec, mat = _lru_specs(t)

    def body(g_ref, x_ref, cw_ref, cb_ref, wa_ref, ba_ref, wx_ref, bx_ref, lam_ref, o_ref):
        o_ref[...] = _lru_fn(g_ref[...], x_ref[...], cw_ref[...], cb_ref[...], wa_ref[...], ba_ref[...], wx_ref[...],
                             bx_ref[...], lam_ref[...]).astype(BF16)

    return pl.pallas_call(body, out_shape=_sds((t, D_MODEL), BF16), grid=(LRU_BLOCKS,),
                          in_specs=[gate, xin, cws, vec, mat, vec, mat, vec, vec], out_specs=gate,
                          compiler_params=_params(), name=name)(gx, gx, cw, cb, wa, ba, wx, bx, lam)


def lru_bwd(gx, cw, cb, wa, ba, wx, bx, lam, dy, *, name):
    t = gx.shape[0]
    gate, xin, cws, vec, mat = _lru_specs(t)

    def body(g_ref, x_ref, cw_ref, cb_ref, wa_ref, ba_ref, wx_ref, bx_ref, lam_ref, dy_ref,
             dgx_ref, dcw_ref, dcb_ref, dwa_ref, dba_ref, dwx_ref, dbx_ref, dlam_ref, stage_ref, sems):
        _, vjp = jax.vjp(_lru_fn, g_ref[...], x_ref[...], cw_ref[...], cb_ref[...], wa_ref[...], ba_ref[...], wx_ref[...],
                         bx_ref[...], lam_ref[...])
        dg, dx, dcw, dcb, dwa, dba, dwx, dbx, dlam = vjp(dy_ref[...])
        _put_column_blocks(pl.program_id(0), LRU_BLOCKS, (dg.astype(BF16), dx.astype(BF16)), dgx_ref, D_MODEL // LRU_BLOCKS, stage_ref, sems)
        dcw_ref[...] = dcw
        dcb_ref[...] = dcb
        dwa_ref[...] = dwa
        dba_ref[...] = dba
        dwx_ref[...] = dwx
        dbx_ref[...] = dbx
        dlam_ref[...] = dlam

    d = D_MODEL
    w = d // LRU_BLOCKS
    out_shape = (_sds((t, 2 * d), BF16), _sds((4, d), F32), _sds((1, d), F32), _sds((LRU_BLOCKS, w, w), F32),
                 _sds((1, d), F32), _sds((LRU_BLOCKS, w, w), F32), _sds((1, d), F32), _sds((1, d), F32))
    return pl.pallas_call(body, out_shape=out_shape, grid=(LRU_BLOCKS,),
                          in_specs=[gate, xin, cws, vec, mat, vec, mat, vec, vec, gate],
                          out_specs=(pl.BlockSpec(memory_space=pl.ANY), cws, vec, mat, vec, mat, vec, vec),
                          scratch_shapes=[pltpu.VMEM((2, 2, t, w), BF16), pltpu.SemaphoreType.DMA((2, 2))],
                          compiler_params=_params(), name=name)(gx, gx, cw, cb, wa, ba, wx, bx, lam, dy)


def _ret_tables():
    half = HEAD // 2
    inv_freq = (np.float32(ROPE_BASE) ** (-np.arange(half, dtype=np.float32) / np.float32(half))).astype(np.float32)
    ang = (np.arange(SEQ, dtype=np.float32)[:, None] * inv_freq[None, :]).astype(np.float64)
    cos2 = np.concatenate([np.cos(ang), np.cos(ang)], axis=1).astype(np.float32)
    sin2 = np.concatenate([-np.sin(ang), np.sin(ang)], axis=1).astype(np.float32)
    c = RET_CHUNK
    log_gamma = np.log1p(-np.exp2(-5.0 - np.arange(N_HEADS, dtype=np.float64)))
    idx = np.arange(c, dtype=np.float64)
    rel = idx[:, None] - idx[None, :]
    dmask = np.where(rel >= 0, np.exp(log_gamma[:, None, None] * np.maximum(rel, 0.0)), 0.0)
    ones = np.ones((N_HEADS, c, HEAD))
    ktail = np.exp(log_gamma[:, None] * (c - 1 - idx))[:, :, None] * ones
    qdec = np.exp(log_gamma[:, None] * (idx + 1.0))[:, :, None] * ones
    cdec = np.exp(log_gamma * c)[:, None, None] * ones
    return tuple(jnp.asarray(a, F32) for a in (cos2, sin2, dmask, ktail, qdec, cdec))


def _ret_specs(rev):
    c = RET_CHUNK * RET_CHUNKS_PER_STEP
    nc = SEQ // c

    def n_of(n):
        return nc - 1 - n if rev else n

    def group(off):
        return pl.BlockSpec((c, GROUP), lambda n: (n_of(n), off))

    tab = pl.BlockSpec((c, HEAD), lambda n: (n_of(n), 0))
    const = pl.BlockSpec((N_HEADS, RET_CHUNK, HEAD), lambda n: (0, 0, 0))
    state = pl.BlockSpec((N_HEADS, None, HEAD, HEAD), lambda n: (0, n_of(n), 0, 0))
    return group, tab, const, state, nc


def _head(h):
    return slice(h * HEAD, (h + 1) * HEAD)


def ret_fwd(p, tables, *, name):
    group, tab, const, state, nc = _ret_specs(False)

    def body(q_ref, k_ref, v_ref, g_ref, cos_ref, sin_ref, dm_ref, kt_ref, qd_ref, cd_ref, y_ref, st_ref, s_scr):
        @pl.when(pl.program_id(0) == 0)
        def _():
            s_scr[...] = jnp.zeros_like(s_scr)

        heads = range(N_HEADS)
        states = tuple(s_scr[h] for h in heads)
        ys, new_states = _ret_fn(*(tuple(r[:, _head(h)] for h in heads) for r in (q_ref, k_ref, v_ref, g_ref)), states,
                                 cos_ref[...], sin_ref[...], *(tuple(r[h] for h in heads) for r in (dm_ref, kt_ref, qd_ref, cd_ref)))
        for h in heads:
            st_ref[h] = states[h]
            y_ref[:, _head(h)] = ys[h].astype(BF16)
            s_scr[h] = new_states[h]

    return pl.pallas_call(
        body, out_shape=(_sds((SEQ, GROUP), BF16), _sds((N_HEADS, nc, HEAD, HEAD), F32)), grid=(nc,),
        in_specs=[group(0), group(1), group(2), group(3), tab, tab, const, const, const, const],
        out_specs=(group(0), state), scratch_shapes=[pltpu.VMEM((N_HEADS, HEAD, HEAD), F32)], compiler_params=_params(), name=name,
    )(p, p, p, p, *tables)


def ret_bwd(p, tables, states, dy, *, name):
    group, tab, const, state, nc = _ret_specs(True)

    def body(q_ref, k_ref, v_ref, g_ref, cos_ref, sin_ref, dm_ref, kt_ref, qd_ref, cd_ref, st_ref, dy_ref,
             dq_ref, dk_ref, dv_ref, dg_ref, ds_scr):
        @pl.when(pl.program_id(0) == 0)
        def _():
            ds_scr[...] = jnp.zeros_like(ds_scr)

        heads = range(N_HEADS)
        consts = (cos_ref[...], sin_ref[...], *(tuple(r[h] for h in heads) for r in (dm_ref, kt_ref, qd_ref, cd_ref)))
        _, vjp = jax.vjp(lambda *a: _ret_fn(*a, *consts), *(tuple(r[:, _head(h)] for h in heads) for r in (q_ref, k_ref, v_ref, g_ref)),
                         tuple(st_ref[h] for h in heads))
        dqs, dks, dvs, dgs, dss = vjp((tuple(dy_ref[:, _head(h)] for h in heads), tuple(ds_scr[h] for h in heads)))
        for h in heads:
            dq_ref[:, _head(h)] = dqs[h].astype(BF16)
            dk_ref[:, _head(h)] = dks[h].astype(BF16)
            dv_ref[:, _head(h)] = dvs[h].astype(BF16)
            dg_ref[:, _head(h)] = dgs[h].astype(BF16)
            ds_scr[h] = dss[h]

    out = _sds((SEQ, GROUP), BF16)
    return pl.pallas_call(
        body, out_shape=(out, out, out, out), grid=(nc,),
        in_specs=[group(0), group(1), group(2), group(3), tab, tab, const, const, const, const, state, group(0)],
        out_specs=(group(0), group(0), group(0), group(0)), scratch_shapes=[pltpu.VMEM((N_HEADS, HEAD, HEAD), F32)],
        compiler_params=_params(), name=name,
    )(p, p, p, p, *tables, states, dy)


def _gdn_specs(rev):
    c = GDN_CHUNK * GDN_CHUNKS_PER_STEP
    nc = SEQ // c

    def n_of(n):
        return nc - 1 - n if rev else n

    def group(off):
        return pl.BlockSpec((c, GROUP), lambda n: (n_of(n), off))

    small = pl.BlockSpec((c, LANES), lambda n: (n_of(n), 0))
    vec = pl.BlockSpec((1, LANES), lambda n: (0, 0))
    state = pl.BlockSpec((N_HEADS, None, HEAD, HEAD), lambda n: (0, n_of(n), 0, 0))
    return group, small, vec, state, nc


GDN_GATE_GROUP = 7


def gdn_fwd(conv, p, small, a_log, dt_bias, gain, *, name):
    group, sm, vec, state, nc = _gdn_specs(False)

    def body(q_ref, k_ref, v_ref, g_ref, sm_ref, al_ref, dt_ref, gn_ref, y_ref, st_ref, s_scr):
        @pl.when(pl.program_id(0) == 0)
        def _():
            s_scr[...] = jnp.zeros_like(s_scr)

        states = tuple(s_scr[h] for h in range(N_HEADS))
        ys, new_states = _gdn_fn(*(tuple(r[:, _head(h)] for h in range(N_HEADS)) for r in (q_ref, k_ref, v_ref, g_ref)),
                                 sm_ref[...], al_ref[...], dt_ref[...], gn_ref[...], states)
        for h in range(N_HEADS):
            st_ref[h] = states[h]
            y_ref[:, _head(h)] = ys[h].astype(BF16)
            s_scr[h] = new_states[h]

    return pl.pallas_call(
        body, out_shape=(_sds((SEQ, GROUP), BF16), _sds((N_HEADS, nc, HEAD, HEAD), F32)), grid=(nc,),
        in_specs=[group(0), group(1), group(2), group(GDN_GATE_GROUP), sm, vec, vec, vec], out_specs=(group(0), state),
        scratch_shapes=[pltpu.VMEM((N_HEADS, HEAD, HEAD), F32)], compiler_params=_params(), name=name,
    )(conv, conv, conv, p, small, a_log, dt_bias, gain)


def gdn_bwd(conv, p, small, a_log, dt_bias, gain, states, dy, *, name):
    group, sm, vec, state, nc = _gdn_specs(True)

    def body(q_ref, k_ref, v_ref, g_ref, sm_ref, al_ref, dt_ref, gn_ref, st_ref, dy_ref,
             dq_ref, dk_ref, dv_ref, dg_ref, dsm_ref, dal_ref, ddt_ref, dgn_ref, ds_scr):
        @pl.when(pl.program_id(0) == 0)
        def _():
            ds_scr[...] = jnp.zeros_like(ds_scr)
            dal_ref[...] = jnp.zeros_like(dal_ref)
            ddt_ref[...] = jnp.zeros_like(ddt_ref)
            dgn_ref[...] = jnp.zeros_like(dgn_ref)

        per_head = tuple(tuple(r[:, _head(h)] for h in range(N_HEADS)) for r in (q_ref, k_ref, v_ref, g_ref))
        _, vjp = jax.vjp(_gdn_fn, *per_head, sm_ref[...], al_ref[...], dt_ref[...], gn_ref[...],
                         tuple(st_ref[h] for h in range(N_HEADS)))
        cts = (tuple(dy_ref[:, _head(h)] for h in range(N_HEADS)), tuple(ds_scr[h] for h in range(N_HEADS)))
        dqs, dks, dvs, dgs, dsm, dal, ddt, dgn, dss = vjp(cts)
        for h in range(N_HEADS):
            dq_ref[:, _head(h)] = dqs[h]
            dk_ref[:, _head(h)] = dks[h]
            dv_ref[:, _head(h)] = dvs[h]
            dg_ref[:, _head(h)] = dgs[h].astype(BF16)
            ds_scr[h] = dss[h]
        dsm_ref[...] = dsm
        dal_ref[...] += dal
        ddt_ref[...] += ddt
        dgn_ref[...] += dgn

    f = _sds((SEQ, GROUP), F32)
    pv = _sds((1, LANES), F32)
    return pl.pallas_call(
        body, out_shape=(f, f, f, _sds((SEQ, GROUP), BF16), _sds((SEQ, LANES), F32), pv, pv, pv), grid=(nc,),
        in_specs=[group(0), group(1), group(2), group(GDN_GATE_GROUP), sm, vec, vec, vec, state, group(1)],
        out_specs=(group(0), group(0), group(0), group(0), sm, vec, vec, vec), scratch_shapes=[pltpu.VMEM((N_HEADS, HEAD, HEAD), F32)],
        compiler_params=_params(), name=name,
    )(conv, conv, conv, p, small, a_log, dt_bias, gain, states, dy)


ELEMENTWISE_BLOCK_BYTES = 2 * 1024 * 1024


def _row_tile(r, c):
    best = None
    for tr in range(8, r + 1, 8):
        if r % tr == 0 and tr * c * 4 <= ELEMENTWISE_BLOCK_BYTES:
            best = tr
    if best is None:
        raise ValueError(f"no row tile for ({r}, {c})")
    return best


def _tile_2d(r, c):
    if any(r % tr == 0 for tr in range(8, r + 1, 8)):
        return _row_tile(r, c), c
    tc = max(t for t in range(LANES, c + 1, LANES) if c % t == 0 and r * t * 4 <= ELEMENTWISE_BLOCK_BYTES)
    return r, tc


def _core_index():
    return lax.axis_index("c").astype(jnp.int32).reshape(1)


def _chip_index():
    return (2 * lax.axis_index("x") + lax.axis_index("y")).astype(jnp.int32).reshape(1)


def adamw_halves(w, m, v, g_own, g_sib, *, layer=0, prev=None, name):
    n_layers, rows, c = w.shape
    r = rows // 2
    tr = _row_tile(r, c)
    nb = r // tr

    def body(c_ref, w_ref, m_ref, v_ref, own_ref, sib_ref, *rest):
        g_ref, d_ref, nm_ref, nv_ref = rest[-4:]
        gg = jnp.where(pl.program_id(0) == c_ref[0], own_ref[...], sib_ref[...])
        nm = ADAM_B1 * m_ref[...] + (1.0 - ADAM_B1) * gg
        nv = ADAM_B2 * v_ref[...] + (1.0 - ADAM_B2) * jnp.square(gg)
        m_hat = nm / (1.0 - ADAM_B1 ** ADAM_STEP)
        v_hat = nv / (1.0 - ADAM_B2 ** ADAM_STEP)
        g_ref[...] = gg
        d_ref[...] = -ADAM_LR * (m_hat / (jnp.sqrt(v_hat) + ADAM_EPS) + ADAM_WD * w_ref[...])
        nm_ref[...] = nm
        nv_ref[...] = nv

    full = pl.BlockSpec((None, tr, c), lambda h, i, cr: (layer, h * nb + i, 0))
    half = pl.BlockSpec((tr, c), lambda h, i, cr: (i, 0))
    o = _sds((n_layers, rows, c), F32)
    prev = list(prev or ())
    gs = pltpu.PrefetchScalarGridSpec(num_scalar_prefetch=1, grid=(2, nb), in_specs=[full, full, full, half, half] + [_ANY] * len(prev),
                                      out_specs=(full, full, full, full))
    n_fixed = 6
    return pl.pallas_call(body, out_shape=(o, o, o, o), grid_spec=gs, compiler_params=_params(), name=name,
                          input_output_aliases={n_fixed + k: k for k in range(len(prev))})(
        _core_index(), w, m, v, g_own, g_sib, *prev)


ADAMW_COLUMN_TILE = 256


def adamw_column_halves(w, m, v, g_own, g_sib, *, name):
    _, rows, cols = w.shape
    tc = ADAMW_COLUMN_TILE
    per_half = cols // 2 // tc

    def body(c_ref, w_ref, m_ref, v_ref, own_ref, sib_ref, g_ref, d_ref, nm_ref, nv_ref):
        gg = jnp.where(pl.program_id(0) // per_half == c_ref[0], own_ref[...], sib_ref[...])
        nm = ADAM_B1 * m_ref[...] + (1.0 - ADAM_B1) * gg
        nv = ADAM_B2 * v_ref[...] + (1.0 - ADAM_B2) * jnp.square(gg)
        m_hat = nm / (1.0 - ADAM_B1 ** ADAM_STEP)
        v_hat = nv / (1.0 - ADAM_B2 ** ADAM_STEP)
        g_ref[...] = gg
        d_ref[...] = -ADAM_LR * (m_hat / (jnp.sqrt(v_hat) + ADAM_EPS) + ADAM_WD * w_ref[...])
        nm_ref[...] = nm
        nv_ref[...] = nv

    full = pl.BlockSpec((None, rows, tc), lambda j, cr: (0, 0, j))
    half = pl.BlockSpec((rows, tc), lambda j, cr: (0, j % per_half))
    o = _sds(w.shape, F32)
    gs = pltpu.PrefetchScalarGridSpec(num_scalar_prefetch=1, grid=(cols // tc,), in_specs=[full, full, full, half, half],
                                      out_specs=(full, full, full, full))
    return pl.pallas_call(body, out_shape=(o, o, o, o), grid_spec=gs, compiler_params=_params(), name=name)(
        _core_index(), w, m, v, g_own, g_sib)


def adamw_many(ws, gs, ms, vs, *, name):
    n = len(ws)

    def body(*refs):
        w_refs, g_refs, m_refs, v_refs, d_refs, nm_refs, nv_refs = (refs[k * n:(k + 1) * n] for k in range(7))
        for i in range(n):
            gg = g_refs[i][...]
            nm = ADAM_B1 * m_refs[i][...] + (1.0 - ADAM_B1) * gg
            nv = ADAM_B2 * v_refs[i][...] + (1.0 - ADAM_B2) * jnp.square(gg)
            m_hat = nm / (1.0 - ADAM_B1 ** ADAM_STEP)
            v_hat = nv / (1.0 - ADAM_B2 ** ADAM_STEP)
            d_refs[i][...] = -ADAM_LR * (m_hat / (jnp.sqrt(v_hat) + ADAM_EPS) + ADAM_WD * w_refs[i][...])
            nm_refs[i][...] = nm
            nv_refs[i][...] = nv

    outs = pl.pallas_call(body, out_shape=[_sds(w.shape, F32) for w in ws] * 3, compiler_params=_params(), name=name)(*ws, *gs, *ms, *vs)
    return outs[:n], outs[n:2 * n], outs[2 * n:]


def add_core_halves(g2, land, *, out_dtype, name):
    _, ns, r, cols = g2.shape
    tr, tc = _tile_2d(r, cols)

    def body(c_ref, a_ref, b_ref, o_ref):
        o_ref[...] = (a_ref[...] + b_ref[...]).astype(out_dtype)

    gs = pltpu.PrefetchScalarGridSpec(
        num_scalar_prefetch=1, grid=(ns, r // tr, cols // tc),
        in_specs=[pl.BlockSpec((None, None, tr, tc), lambda s, i, j, cr: (cr[0], s, i, j)),
                  pl.BlockSpec((None, tr, tc), lambda s, i, j, cr: (s, i, j))],
        out_specs=pl.BlockSpec((None, tr, tc), lambda s, i, j, cr: (s, i, j)))
    return pl.pallas_call(body, out_shape=_sds((ns, r, cols), out_dtype), grid_spec=gs, compiler_params=_params(), name=name)(
        _core_index(), g2, land)


def sum_over_chips(own, land, *, scatter, name):
    _, r, cols = own.shape
    tr, tc = _tile_2d(r, cols)

    def body(mine_ref, own_ref, l0, l1, l2, l3, o_ref):
        mine = mine_ref[0]
        mine_val = own_ref[...]
        acc = None
        for s, l_ref in enumerate((l0, l1, l2, l3)):
            val = jnp.where(mine == s, mine_val, l_ref[...]).astype(F32)
            acc = val if acc is None else acc + val
        o_ref[...] = acc

    def slot(s):
        return pl.BlockSpec((None, tr, tc), lambda i, j, mr: (jnp.where(mr[0] == s, (s + 1) % N_SHARD, s), i, j))

    own_spec = pl.BlockSpec((None, tr, tc), lambda i, j, mr: (mr[0] if scatter else 0, i, j))
    gs = pltpu.PrefetchScalarGridSpec(num_scalar_prefetch=1, grid=(r // tr, cols // tc), in_specs=[own_spec] + [slot(s) for s in range(N_SHARD)],
                                      out_specs=pl.BlockSpec((tr, tc), lambda i, j, mr: (i, j)))
    return pl.pallas_call(body, out_shape=_sds((r, cols), F32), grid_spec=gs, compiler_params=_params(), name=name)(
        _chip_index(), own, land, land, land, land)


_ANY = pl.BlockSpec(memory_space=pl.ANY)


def xy_exchange(src, *, scatter, name):
    rh = src.shape[1]

    def body(src_ref, land_ref, send_sems, recv_sems, loc_sem):
        x, y, c = lax.axis_index("x"), lax.axis_index("y"), lax.axis_index("c")
        mine = 2 * x + y
        peers = [(1 - x, y), (x, 1 - y), (1 - x, 1 - y)]

        def piece(shard):
            return src_ref.at[shard] if scatter else src_ref.at[c]

        def copy(k, px, py, dst_slot):
            return pltpu.make_async_remote_copy(src_ref=piece(2 * px + py), dst_ref=land_ref.at[dst_slot], send_sem=send_sems.at[k],
                                                recv_sem=recv_sems.at[k], device_id=(px, py, c), device_id_type=MESH)

        keep = pltpu.make_async_copy(piece(mine), land_ref.at[mine], loc_sem)
        keep.start()
        sends = [copy(k, px, py, mine) for k, (px, py) in enumerate(peers)]
        for cp in sends:
            cp.start()
        for cp in sends:
            cp.wait_send()
        for k, (px, py) in enumerate(peers):
            copy(k, px, py, 2 * px + py).wait_recv()
        keep.wait()

    return pl.pallas_call(body, out_shape=_sds((N_SHARD, rh, LANES), src.dtype), in_specs=[_ANY], out_specs=_ANY,
                          scratch_shapes=[pltpu.SemaphoreType.DMA((3,)), pltpu.SemaphoreType.DMA((3,)), pltpu.SemaphoreType.DMA(())],
                          name=name)(src)


def core_exchange(src, *, send_other_half, name):
    def body(src_ref, out_ref, send_sem, recv_sem, loc_sem):
        x, y, c = lax.axis_index("x"), lax.axis_index("y"), lax.axis_index("c")
        if send_other_half:
            cp = pltpu.make_async_remote_copy(src_ref=src_ref.at[1 - c], dst_ref=out_ref, send_sem=send_sem, recv_sem=recv_sem,
                                              device_id=(x, y, 1 - c), device_id_type=MESH)
            cp.start()
            cp.wait_send()
            cp.wait_recv()
        else:
            keep = pltpu.make_async_copy(src_ref, out_ref.at[c], loc_sem)
            keep.start()
            cp = pltpu.make_async_remote_copy(src_ref=src_ref, dst_ref=out_ref.at[c], send_sem=send_sem, recv_sem=recv_sem,
                                              device_id=(x, y, 1 - c), device_id_type=MESH)
            cp.start()
            cp.wait_send()
            pltpu.make_async_remote_copy(src_ref=src_ref, dst_ref=out_ref.at[1 - c], send_sem=send_sem, recv_sem=recv_sem,
                                         device_id=(x, y, 1 - c), device_id_type=MESH).wait_recv()
            keep.wait()

    out_shape = _sds(src.shape[1:], src.dtype) if send_other_half else _sds((2,) + src.shape, src.dtype)
    return pl.pallas_call(body, out_shape=out_shape, in_specs=[_ANY], out_specs=_ANY,
                          scratch_shapes=[pltpu.SemaphoreType.DMA(()), pltpu.SemaphoreType.DMA(()), pltpu.SemaphoreType.DMA(())],
                          name=name)(src)


def _comm_call(body, ins, out_shapes, sem_counts, name):
    return pl.pallas_call(body, out_shape=tuple(out_shapes), in_specs=[_ANY] * len(ins), out_specs=tuple([_ANY] * len(out_shapes)),
                          scratch_shapes=[pltpu.SemaphoreType.DMA((k,)) for k in sem_counts], name=name)(*ins)


def _sequencer_call(body, ins, out_shapes, sem_counts, name, collective_id):
    return pl.kernel(body, out_type=list(out_shapes), mesh=plsc.ScalarSubcoreMesh(axis_name="sequencer", num_cores=1), name=name,
                     scratch_types=[pltpu.SemaphoreType.DMA((k,)) for k in sem_counts],
                     compiler_params=pltpu.CompilerParams(collective_id=collective_id))(*ins)


def _handshake(peers):
    barrier = pltpu.get_barrier_semaphore()
    for peer in peers:
        pl.semaphore_signal(barrier, inc=1, device_id=peer, device_id_type=MESH)
    pl.semaphore_wait(barrier, len(peers))


def _xy_peers(x, y):
    return [(1 - x, y), (x, 1 - y), (1 - x, 1 - y)]


def gather_halves(halves, *, name, collective_id):
    n = len(halves)

    def body(*refs):
        ins, lands, sibs = refs[:n], refs[n:2 * n], refs[2 * n:3 * n]
        ici_send, ici_recv, d2d_send, d2d_recv = refs[3 * n:]
        x, y, c = lax.axis_index("x"), lax.axis_index("y"), lax.axis_index("c")
        mine = 2 * x + y
        peers = _xy_peers(x, y)
        _handshake([(px, py, c) for px, py in peers] + [(x, y, 1 - c)])

        def ici(i, k, slot):
            px, py = peers[k]
            return pltpu.make_async_remote_copy(src_ref=ins[i].at[c], dst_ref=lands[i].at[slot], send_sem=ici_send.at[3 * i + k],
                                                recv_sem=ici_recv.at[3 * i + k], device_id=(px, py, c), device_id_type=MESH)

        def pass_on(i, k):
            px, py = peers[k]
            slot = 2 * px + py
            return pltpu.make_async_remote_copy(src_ref=lands[i].at[slot], dst_ref=sibs[i].at[slot], send_sem=d2d_send.at[3 * i + k],
                                                recv_sem=d2d_recv.at[3 * i + k], device_id=(x, y, 1 - c), device_id_type=MESH)

        sends = [ici(i, k, mine) for i in range(n) for k in range(3)]
        for cp in sends:
            cp.start()
        passed = []
        for i in range(n):
            for k in range(3):
                px, py = peers[k]
                ici(i, k, 2 * px + py).wait_recv()
                cp = pass_on(i, k)
                cp.start()
                passed.append(cp)
        for cp in passed:
            cp.wait_recv()
        for cp in sends + passed:
            cp.wait_send()

    outs = [_sds((N_SHARD,) + h.shape[1:], h.dtype) for h in halves]
    res = _sequencer_call(body, halves, outs + outs, [3 * n] * 4, name, collective_id)
    return res[:n], res[n:]


def send_other_half(arrays, *, name, collective_id):
    n = len(arrays)

    def body(*refs):
        ins, lands = refs[:n], refs[n:2 * n]
        send_sems, recv_sems = refs[2 * n:]
        x, y, c = lax.axis_index("x"), lax.axis_index("y"), lax.axis_index("c")
        _handshake([(x, y, 1 - c)])
        copies = [pltpu.make_async_remote_copy(src_ref=ins[i].at[1 - c], dst_ref=lands[i], send_sem=send_sems.at[i],
                                               recv_sem=recv_sems.at[i], device_id=(x, y, 1 - c), device_id_type=MESH) for i in range(n)]
        for cp in copies:
            cp.start()
        for cp in copies:
            cp.wait_recv()
        for cp in copies:
            cp.wait_send()

    return _sequencer_call(body, arrays, [_sds(a.shape[1:], a.dtype) for a in arrays], [n, n], name, collective_id)


_HBM = pl.BlockSpec(memory_space=pltpu.HBM)
_SEM = pl.BlockSpec(memory_space=pltpu.SEMAPHORE)
_SPLIT_COPY = dict(has_side_effects=pltpu.SideEffectType.DATAFLOW_SIDE_EFFECTING)


def _chip_copy(ins, lands, send_sems, recv_sems, scatter, i, k, receive):
    x, y, c = lax.axis_index("x"), lax.axis_index("y"), lax.axis_index("c")
    px, py = _xy_peers(x, y)[k]
    theirs, mine = 2 * px + py, 2 * x + y
    src = ins[i].at[theirs] if scatter[i] else ins[i].at[0]
    return pltpu.make_async_remote_copy(src_ref=src, dst_ref=lands[i].at[theirs if receive else mine], send_sem=send_sems.at[3 * i + k],
                                        recv_sem=recv_sems.at[3 * i + k], device_id=(px, py, c), device_id_type=MESH)


def send_to_chips_start(arrays, scatter, *, name):
    n = len(arrays)

    def body(*refs):
        send_sems, recv_sems = refs[2 * n], refs[2 * n + 1]
        ins, lands = refs[2 * n + 2:3 * n + 2], refs[3 * n + 2:4 * n + 2]
        token = refs[4 * n + 2]
        for i in range(n):
            for k in range(3):
                _chip_copy(ins, lands, send_sems, recv_sems, scatter, i, k, receive=False).start()
        token[...] = jnp.zeros_like(token)

    land_shapes = [(N_SHARD,) + a.shape[1:] for a in arrays]
    operands = [pltpu.with_memory_space_constraint(a, pltpu.HBM) for a in arrays]
    operands += [pltpu.with_memory_space_constraint(lax.empty(s, a.dtype), pltpu.HBM) for s, a in zip(land_shapes, arrays)]
    out_shape = ([pltpu.SemaphoreType.DMA((3 * n,)), pltpu.SemaphoreType.DMA((3 * n,))] + [pltpu.HBM(a.shape, a.dtype) for a in arrays]
                 + [pltpu.HBM(s, a.dtype) for s, a in zip(land_shapes, arrays)] + [_sds((8, LANES), F32)])
    res = pl.pallas_call(body, name=name, out_shape=out_shape, in_specs=[_HBM] * (2 * n),
                         out_specs=[_SEM, _SEM] + [_HBM] * (2 * n) + [pl.BlockSpec(memory_space=pltpu.VMEM)],
                         input_output_aliases={i: 2 + i for i in range(2 * n)}, compiler_params=pltpu.CompilerParams(**_SPLIT_COPY))(*operands)
    return (res[0], res[1], res[2:2 + n], res[2 + n:2 + 2 * n], scatter), res[-1]


def send_to_chips_wait(state, after, *, name):
    send_sems, recv_sems, arrays, lands, scatter = state
    n = len(arrays)

    def body(*refs):
        ins, landing = refs[:n], refs[n:2 * n]
        send_sems, recv_sems = refs[2 * n], refs[2 * n + 1]
        for i in range(n):
            for k in range(3):
                _chip_copy(ins, landing, send_sems, recv_sems, scatter, i, k, receive=True).wait_recv()
        for i in range(n):
            for k in range(3):
                _chip_copy(ins, landing, send_sems, recv_sems, scatter, i, k, receive=False).wait_send()

    out_shape = [pltpu.HBM(a.shape, a.dtype) for a in list(arrays) + list(lands)]
    res = pl.pallas_call(body, name=name, out_shape=out_shape, in_specs=[_HBM] * (2 * n) + [_SEM, _SEM] + [_ANY] * len(after),
                         out_specs=[_HBM] * (2 * n), input_output_aliases={i: i for i in range(2 * n)},
                         compiler_params=pltpu.CompilerParams(**_SPLIT_COPY))(*arrays, *lands, send_sems, recv_sems, *after)
    return res[:n], res[n:]


def swap_with_other_core(arrays, *, name, collective_id):
    n = len(arrays)

    def body(*refs):
        ins, lands = refs[:n], refs[n:2 * n]
        send_sems, recv_sems = refs[2 * n:]
        x, y, c = lax.axis_index("x"), lax.axis_index("y"), lax.axis_index("c")
        _handshake([(x, y, 1 - c)])
        copies = [pltpu.make_async_remote_copy(src_ref=ins[i], dst_ref=lands[i], send_sem=send_sems.at[i], recv_sem=recv_sems.at[i],
                                               device_id=(x, y, 1 - c), device_id_type=MESH) for i in range(n)]
        for cp in copies:
            cp.start()
        for cp in copies:
            cp.wait_recv()
        for cp in copies:
            cp.wait_send()

    return _sequencer_call(body, arrays, [_sds(a.shape, a.dtype) for a in arrays], [n, n], name, collective_id)


def _pack_rows(n_elems, row_multiple):
    rows = -(-n_elems // LANES)
    return -(-rows // row_multiple) * row_multiple


def _pack(arrays, rows, dtype):
    flat = jnp.concatenate([a.reshape(-1).astype(dtype) for a in arrays])
    return jnp.pad(flat, (0, rows * LANES - flat.shape[0])).reshape(rows, LANES)


def _unpack(packed, shapes):
    flat = packed.reshape(-1)
    out, off = [], 0
    for s in shapes:
        n = int(np.prod(s))
        out.append(flat[off:off + n].reshape(s))
        off += n
    return out


def all_gather_shards(shards, axes, dtype, row_multiple, tag):
    shapes = [s.shape for s in shards]
    rows = _pack_rows(sum(int(np.prod(s)) for s in shapes), row_multiple)
    packed = _pack(shards, rows, dtype).reshape(2, rows // 2, LANES)
    land = xy_exchange(packed, scatter=False, name=f"gather_xy_{tag}")
    both = core_exchange(land, send_other_half=False, name=f"gather_c_{tag}")
    per_shard = jnp.swapaxes(both, 0, 1).reshape(N_SHARD, rows, LANES)
    pieces = [_unpack(per_shard[s], shapes) for s in range(N_SHARD)]
    return [jnp.concatenate([pieces[s][i] for s in range(N_SHARD)], axis=ax) for i, ax in enumerate(axes)]


def _ordered_before(first, then):
    if then is None:
        return first, None
    return lax.optimization_barrier((first, then))


def reduce_between_cores(arrays, scatter, *, tag, collective_id, before=None):
    arrays, before = _ordered_before(arrays, before)
    land = send_other_half(arrays, name=f"reduce_core_send_{tag}", collective_id=collective_id)
    return (arrays, land, scatter, tag, collective_id), before


def reduce_between_chips(state, before=None):
    arrays, land, scatter, tag, collective_id = state
    chip = [add_core_halves(a, l, out_dtype=BF16 if sc else F32, name=f"reduce_core_add_{tag}_{i}")
            for i, (a, l, sc) in enumerate(zip(arrays, land, scatter))]
    sending, token = send_to_chips_start(chip, scatter, name=f"reduce_chip_start_{tag}")
    token, before = _ordered_before(token, before)
    return (sending, token, scatter, tag, collective_id), before


def reduce_finish(state, after):
    sending, token, scatter, tag, collective_id = state
    chip, land = send_to_chips_wait(sending, tuple(after) + (token,), name=f"reduce_chip_wait_{tag}")
    own = [sum_over_chips(ch, l, scatter=sc, name=f"reduce_chip_add_{tag}_{i}") for i, (ch, l, sc) in enumerate(zip(chip, land, scatter))]
    sib = swap_with_other_core(own, name=f"reduce_core_swap_{tag}", collective_id=collective_id + 2)
    return own, sib


def _ffn_layer_fwd(h, norm_g, w_up, cw, cb, w_down, tag):
    u, hn = norm_matmul(h, norm_g, w_up, name=f"ffn_up_{tag}")
    act = ffn_act_fwd(u, cw, cb, name=f"ffn_act_{tag}")
    out = matmul(act, w_down, add=h, name=f"ffn_down_{tag}")
    return out, (h, hn, u, act)


def _travel_layout(array):
    return BIG_ARRAYS[array][3], BIG_ARRAYS[array][4]


def _ffn_layer_bwd(saved, dout, norm_g, w_up, cw, cb, w_down, tag, d_w_down_other=None):
    h, hn, u, act = saved
    dact = matmul(dout, w_down, tb=True, name=f"ffn_down_dx_{tag}")
    d_w_down = matmul(act, dout, ta=True, layer=(int(tag), 2, d_w_down_other), name=f"ffn_down_dw_{tag}")
    du, dcw, dcb = ffn_act_bwd(u, cw, cb, dact, name=f"ffn_act_bwd_{tag}")
    dhn = matmul(du, w_up, tb=True, name=f"ffn_up_dx_{tag}")
    d_w_up = matmul(hn, du, ta=True, split=_travel_layout(f"ffn_w_up_{tag}"), name=f"ffn_up_dw_{tag}")
    dh, dg = norm_bwd(h, norm_g, dhn, dout, name=f"ffn_norm_bwd_{tag}")
    return dh, dg, d_w_up, dcw, dcb, d_w_down


def local_step(x, target, w, stage=lambda name, tensors, grads=None: tensors):
    g = {}
    tables = _ret_tables()
    x = stage("start", x)
    w_in_t = w["ret_gdn_w_in"]
    w_main = w_in_t[:MIX_MAIN]
    w_small = jnp.pad(w_in_t[MIX_MAIN:], ((0, LANES - 2 * N_HEADS), (0, 0)))
    a_log = jnp.pad(w["gdn_a_log"], ((0, 0), (0, LANES - N_HEADS)))
    dt_bias = jnp.pad(w["gdn_dt_bias"], ((0, 0), (0, LANES - N_HEADS)))

    p, hn0 = norm_matmul(x, w["norm_mix"][0:1], w_main, tb=True, name="mix0_in")
    hn0 = stage("normed", hn0)
    small = matmul(hn0, w_small, tb=True, name="mix0_in_small")
    y_ret, s_ret = ret_fwd(p, tables, name="ret_fwd")
    conv = gdn_conv_fwd(p, w["gdn_conv_w"], name="gdn_conv")
    y_gdn, s_gdn = gdn_fwd(conv, p, small, a_log, dt_bias, w["gdn_out_gain"], name="gdn_fwd")
    y0 = stage("mixed", jnp.concatenate([y_ret, y_gdn], axis=1))
    h1 = matmul(y0, w["ret_gdn_w_out"], add=x, name="mix0_out")
    h2, ffn0 = _ffn_layer_fwd(h1, w["norm_ffn"][0:1], w["ffn_w_up"][0], w["ffn_conv_w"][0], w["ffn_conv_b"][0:1], w["ffn_w_down"][0], "0")
    h2 = stage("layer0", h2)

    gx, hn1 = norm_matmul(h2, w["norm_mix"][1:2], w["lru_w_in"], name="mix1_in")
    lru_p = (w["lru_conv_w"], w["lru_conv_b"], w["lru_w_a"], w["lru_b_a"], w["lru_w_x"], w["lru_b_x"], w["lru_lambda"])
    y1 = lru_fwd(gx, *lru_p, name="lru_fwd")
    h3 = matmul(y1, w["lru_w_out"], add=h2, name="mix1_out")
    h4, ffn1 = _ffn_layer_fwd(h3, w["norm_ffn"][1:2], w["ffn_w_up"][1], w["ffn_conv_w"][1], w["ffn_conv_b"][1:2], w["ffn_w_down"][1], "1")

    loss, dh4, g["norm_final"] = final_fwd_bwd(h4, w["norm_final"], target, name="final")

    dh3, dgf1, dwu1, dcw1, dcb1, dwd1 = _ffn_layer_bwd(ffn1, dh4, w["norm_ffn"][1:2], w["ffn_w_up"][1], w["ffn_conv_w"][1],
                                                     w["ffn_conv_b"][1:2], w["ffn_w_down"][1], "1")
    g["ffn_w_up_1"] = dwu1
    dh3 = stage("grads0_ready", dh3, g)
    dy1 = matmul(dh3, w["lru_w_out"], tb=True, name="mix1_out_dx")
    g["lru_w_out"] = matmul(y1, dh3, ta=True, split=_travel_layout("lru_w_out"), name="mix1_out_dw")
    dgx, g["lru_conv_w"], g["lru_conv_b"], g["lru_w_a"], g["lru_b_a"], g["lru_w_x"], g["lru_b_x"], g["lru_lambda"] = lru_bwd(
        gx, *lru_p, dy1, name="lru_bwd")
    dgx = stage("grads0_send", dgx, g)
    dhn1 = matmul(dgx, w["lru_w_in"], tb=True, name="mix1_in_dx")
    g["lru_w_in"] = matmul(hn1, dgx, ta=True, split=_travel_layout("lru_w_in"), name="mix1_in_dw")
    dh2, dgm1 = norm_bwd(h2, w["norm_mix"][1:2], dhn1, dh3, name="mix1_norm_bwd")
    dh2 = stage("grads1_ready", dh2, g)

    dh1, dgf0, dwu0, dcw0, dcb0, dwd0 = _ffn_layer_bwd(ffn0, dh2, w["norm_ffn"][0:1], w["ffn_w_up"][0], w["ffn_conv_w"][0],
                                                     w["ffn_conv_b"][0:1], w["ffn_w_down"][0], "0", dwd1)
    g["ffn_w_up_0"] = dwu0
    g["ffn_w_down"] = dwd0
    dh1 = stage("grads2_ready", stage("grads1_send", dh1, g), g)
    dy0 = matmul(dh1, w["ret_gdn_w_out"], tb=True, name="mix0_out_dx")
    g["ret_gdn_w_out"] = matmul(y0, dh1, ta=True, split=_travel_layout("ret_gdn_w_out"), name="mix0_out_dw")
    dq_r, dk_r, dv_r, dg_r = ret_bwd(p, tables, s_ret, dy0, name="ret_bwd")
    dy0, dq_r = stage("grads2_send", (dy0, dq_r), g)
    dcq, dck, dcv, dg_d, dsmall, dal, ddt, dgain = gdn_bwd(conv, p, small, a_log, dt_bias, w["gdn_out_gain"], s_gdn, dy0, name="gdn_bwd")
    dconv = jnp.concatenate([dcq, dck, dcv], axis=1)
    dp_conv, g["gdn_conv_w"] = gdn_conv_bwd(p, w["gdn_conv_w"], dconv, name="gdn_conv_bwd")
    dp = jnp.concatenate([dq_r, dk_r, dv_r, dg_r, dp_conv, dg_d], axis=1)
    dhn0 = matmul(dp, w_main, name="mix0_in_dx")
    dhn0 = matmul(dsmall, w_small, add=dhn0, name="mix0_in_small_dx")
    d_w_in = matmul(dp, hn0, ta=True, column_halves=(MIX_IN, 0, None, None), name="mix0_in_dw")
    d_w_in = matmul(dsmall, hn0, ta=True, column_halves=(MIX_IN, MIX_MAIN, 2 * N_HEADS, d_w_in), name="mix0_in_small_dw")
    g["ret_gdn_w_in"] = d_w_in.reshape(2, N_SHARD, MIX_IN // N_SHARD, D_MODEL // 2)
    dx, dgm0 = norm_bwd(x, w["norm_mix"][0:1], dhn0, dh1, name="mix0_norm_bwd")

    g["gdn_a_log"] = dal[:, :N_HEADS]
    g["gdn_dt_bias"] = ddt[:, :N_HEADS]
    g["gdn_out_gain"] = dgain
    g["norm_mix"] = jnp.concatenate([dgm0, dgm1], axis=0)
    g["norm_ffn"] = jnp.concatenate([dgf0, dgf1], axis=0)
    g["ffn_conv_w"] = jnp.stack([dcw0, dcw1])
    g["ffn_conv_b"] = jnp.concatenate([dcb0, dcb1], axis=0)
    return loss, dx, g


WEIGHTS = ("norm_mix", "norm_ffn", "ret_gdn_w_in", "gdn_conv_w", "gdn_a_log", "gdn_dt_bias", "gdn_out_gain", "ret_gdn_w_out",
           "lru_w_in", "lru_conv_w", "lru_conv_b", "lru_w_a", "lru_b_a", "lru_w_x", "lru_b_x", "lru_lambda", "lru_w_out",
           "ffn_w_up", "ffn_conv_w", "ffn_conv_b", "ffn_w_down", "norm_final")
MATMUL_SHARDED = {"ret_gdn_w_in": 1, "ret_gdn_w_out": 0, "lru_w_in": 1, "lru_w_out": 0, "ffn_w_up": 2, "ffn_w_down": 1}
VECTOR_SHARDED = {"gdn_conv_w": 1, "lru_conv_w": 1, "lru_conv_b": 1, "lru_b_a": 1, "lru_b_x": 1, "lru_lambda": 1, "ffn_conv_w": 2}
SHARDED = {**MATMUL_SHARDED, **VECTOR_SHARDED}
REPLICATED = tuple(n for n in WEIGHTS if n not in SHARDED)
SQUEEZE = {"ret_gdn_w_in", "gdn_conv_w", "ret_gdn_w_out", "lru_w_in", "lru_conv_w", "lru_w_a", "lru_w_x", "lru_w_out"}
MIX_IN = MIX_MAIN + 2 * N_HEADS
BIG_ARRAYS = {
    "ret_gdn_w_in": ("ret_gdn_w_in", None, (MIX_IN, D_MODEL), (N_SHARD, MIX_IN // N_SHARD, 2, D_MODEL // 2), (2, 0, 1, 3)),
    "ret_gdn_w_out": ("ret_gdn_w_out", None, (2 * GROUP, D_MODEL), (N_SHARD, 2, GROUP // N_SHARD, D_MODEL), (1, 0, 2, 3)),
    "lru_w_in": ("lru_w_in", None, (D_MODEL, 2 * D_MODEL), (2, D_MODEL // 2, N_SHARD, 2 * D_MODEL // N_SHARD), (0, 2, 1, 3)),
    "lru_w_out": ("lru_w_out", None, (D_MODEL, D_MODEL), (N_SHARD, 2, D_MODEL // (2 * N_SHARD), D_MODEL), (1, 0, 2, 3)),
    "ffn_w_up_0": ("ffn_w_up", 0, (D_MODEL, 2 * D_FF), (2, D_MODEL // 2, N_SHARD, 2 * D_FF // N_SHARD), (0, 2, 1, 3)),
    "ffn_w_up_1": ("ffn_w_up", 1, (D_MODEL, 2 * D_FF), (2, D_MODEL // 2, N_SHARD, 2 * D_FF // N_SHARD), (0, 2, 1, 3)),
    "ffn_w_down": ("ffn_w_down", None, (2, D_FF, D_MODEL), (2, N_SHARD, D_FF // N_SHARD, D_MODEL), (0, 1, 2, 3)),
}
GATHER_GROUPS = (("ret_gdn_w_in",), ("ret_gdn_w_out", "ffn_w_up_0", "ffn_w_down"), ("lru_w_in", "lru_w_out", "ffn_w_up_1"))
REDUCE_GROUPS = (("ffn_w_up_1",), ("lru_w_in", "lru_w_out"), ("ffn_w_up_0", "ffn_w_down"), ("ret_gdn_w_out", "ret_gdn_w_in"))
BLOCK_WEIGHTS = ("lru_w_a", "lru_w_x")
GATHER_COLLECTIVE_ID = 1
REDUCE_COLLECTIVE_ID = GATHER_COLLECTIVE_ID + len(GATHER_GROUPS)


TRANSPOSED = ("ret_gdn_w_in",)


def _shard_of(array, tensors):
    weight, layer = BIG_ARRAYS[array][:2]
    t = tensors[weight]
    if weight in TRANSPOSED:
        return jnp.swapaxes(t, 1, 2)[0]
    return _local_view(weight, t) if layer is None else t[layer]


def _core_halves(array, shard):
    _, _, _, split, perm = BIG_ARRAYS[array]
    kept = [k for k in range(4) if k != perm[1]]
    order = [kept.index(perm[0]), kept.index(perm[2]), kept.index(perm[3])]
    return shard.reshape([split[k] for k in kept]).transpose(order)


def _local_view(name, a):
    if name in SQUEEZE:
        return a[0]
    if a.ndim == 1:
        return a[None, :]
    return a


def kernel(x, norm_mix, norm_ffn, ret_gdn_w_in, gdn_conv_w, gdn_a_log, gdn_dt_bias, gdn_out_gain, ret_gdn_w_out, lru_w_in, lru_conv_w, lru_conv_b, lru_w_a, lru_b_a, lru_w_x, lru_b_x, lru_lambda, lru_w_out, ffn_w_up, ffn_conv_w, ffn_conv_b, ffn_w_down, norm_final, loss_target, m_norm_mix, m_norm_ffn, m_ret_gdn_w_in, m_gdn_conv_w, m_gdn_a_log, m_gdn_dt_bias, m_gdn_out_gain, m_ret_gdn_w_out, m_lru_w_in, m_lru_conv_w, m_lru_conv_b, m_lru_w_a, m_lru_b_a, m_lru_w_x, m_lru_b_x, m_lru_lambda, m_lru_w_out, m_ffn_w_up, m_ffn_conv_w, m_ffn_conv_b, m_ffn_w_down, m_norm_final, v_norm_mix, v_norm_ffn, v_ret_gdn_w_in, v_gdn_conv_w, v_gdn_a_log, v_gdn_dt_bias, v_gdn_out_gain, v_ret_gdn_w_out, v_lru_w_in, v_lru_conv_w, v_lru_conv_b, v_lru_w_a, v_lru_b_a, v_lru_w_x, v_lru_b_x, v_lru_lambda, v_lru_w_out, v_ffn_w_up, v_ffn_conv_w, v_ffn_conv_b, v_ffn_w_down, v_norm_final):
    given = dict(norm_mix=norm_mix, norm_ffn=norm_ffn, ret_gdn_w_in=ret_gdn_w_in, gdn_conv_w=gdn_conv_w, gdn_a_log=gdn_a_log, gdn_dt_bias=gdn_dt_bias, gdn_out_gain=gdn_out_gain, ret_gdn_w_out=ret_gdn_w_out, lru_w_in=lru_w_in, lru_conv_w=lru_conv_w, lru_conv_b=lru_conv_b, lru_w_a=lru_w_a, lru_b_a=lru_b_a, lru_w_x=lru_w_x, lru_b_x=lru_b_x, lru_lambda=lru_lambda, lru_w_out=lru_w_out, ffn_w_up=ffn_w_up, ffn_conv_w=ffn_conv_w, ffn_conv_b=ffn_conv_b, ffn_w_down=ffn_w_down, norm_final=norm_final)
    mom1 = dict(norm_mix=m_norm_mix, norm_ffn=m_norm_ffn, ret_gdn_w_in=m_ret_gdn_w_in, gdn_conv_w=m_gdn_conv_w, gdn_a_log=m_gdn_a_log, gdn_dt_bias=m_gdn_dt_bias, gdn_out_gain=m_gdn_out_gain, ret_gdn_w_out=m_ret_gdn_w_out, lru_w_in=m_lru_w_in, lru_conv_w=m_lru_conv_w, lru_conv_b=m_lru_conv_b, lru_w_a=m_lru_w_a, lru_b_a=m_lru_b_a, lru_w_x=m_lru_w_x, lru_b_x=m_lru_b_x, lru_lambda=m_lru_lambda, lru_w_out=m_lru_w_out, ffn_w_up=m_ffn_w_up, ffn_conv_w=m_ffn_conv_w, ffn_conv_b=m_ffn_conv_b, ffn_w_down=m_ffn_w_down, norm_final=m_norm_final)
    mom2 = dict(norm_mix=v_norm_mix, norm_ffn=v_norm_ffn, ret_gdn_w_in=v_ret_gdn_w_in, gdn_conv_w=v_gdn_conv_w, gdn_a_log=v_gdn_a_log, gdn_dt_bias=v_gdn_dt_bias, gdn_out_gain=v_gdn_out_gain, ret_gdn_w_out=v_ret_gdn_w_out, lru_w_in=v_lru_w_in, lru_conv_w=v_lru_conv_w, lru_conv_b=v_lru_conv_b, lru_w_a=v_lru_w_a, lru_b_a=v_lru_b_a, lru_w_x=v_lru_w_x, lru_b_x=v_lru_b_x, lru_lambda=v_lru_lambda, lru_w_out=v_lru_w_out, ffn_w_up=v_ffn_w_up, ffn_conv_w=v_ffn_conv_w, ffn_conv_b=v_ffn_conv_b, ffn_w_down=v_ffn_w_down, norm_final=v_norm_final)

    local = {n: _local_view(n, a) for n, a in given.items()}

    core = lax.axis_index("c")
    chip = 2 * lax.axis_index("x") + lax.axis_index("y")
    is_my_chip = lax.broadcasted_iota(jnp.int32, (N_SHARD, 1, 1), 0) == chip

    def by_core(mine, other):
        return jnp.where(core == 0, jnp.stack([mine, other]), jnp.stack([other, mine]))

    vec_names, rp_names = list(VECTOR_SHARDED), list(REPLICATED)
    full = dict(zip(vec_names, all_gather_shards([local[n] for n in vec_names], [SHARDED[n] for n in vec_names], F32, 32, "p")))
    for n in rp_names:
        full[n] = local[n]
    in_flight = {}

    bf16_halves = {}

    def cast_halves(gi):
        if gi not in bf16_halves:
            bf16_halves[gi] = [_core_halves(a, _shard_of(a, given).astype(BF16)) for a in GATHER_GROUPS[gi]]
        return bf16_halves[gi]

    def launch(gi, after=None):
        halves = cast_halves(gi)
        if after is not None:
            halves, after = lax.optimization_barrier((halves, after))
        in_flight[gi] = (halves,) + gather_halves(halves, name=f"gather_weights_{gi}", collective_id=GATHER_COLLECTIVE_ID + gi)
        return after

    def land(gi, after):
        halves, lands, sibs = in_flight[gi]
        (lands, sibs), after = lax.optimization_barrier(((lands, sibs), after))
        for a, mine, got, passed in zip(GATHER_GROUPS[gi], halves, lands, sibs):
            weight, layer, full_shape, split, perm = BIG_ARRAYS[a]
            half_mine = jnp.where(is_my_chip, jnp.where(core == 0, mine[0], mine[1])[None], got)
            half_other = jnp.where(is_my_chip, jnp.where(core == 0, mine[1], mine[0])[None], passed)
            value = by_core(half_mine, half_other).transpose(tuple(np.argsort(perm))).reshape(full_shape)
            if layer is None:
                full[weight] = value
            else:
                full.setdefault(weight, [None, None])[layer] = value
        return after

    reducing = {}

    def reduce_ready(gi, grads, then=None, extra=()):
        def travelling(a):
            split, perm = _travel_layout(a)
            return grads[a] if grads[a].ndim == 4 else grads[a].reshape(split).transpose(perm)

        arrays = [travelling(a) for a in REDUCE_GROUPS[gi]] + list(extra)
        scatter = [True] * len(REDUCE_GROUPS[gi]) + [False] * len(extra)
        reducing[gi], then = reduce_between_cores(arrays, scatter, tag=str(gi), collective_id=REDUCE_COLLECTIVE_ID + 3 * gi, before=then)
        return then

    def reduce_send(gi, then=None):
        reducing[gi], then = reduce_between_chips(reducing[gi], before=then)
        return then

    def stage(name, tensors, grads=None):
        if name == "start":
            launch(0)
            launch(1)
            fillers = (cast_halves(2), [full[n] for n in vec_names])
            (bf16_halves[2], gathered_small), tensors = lax.optimization_barrier((fillers, tensors))
            full.update(zip(vec_names, gathered_small))
            return land(0, tensors)
        if name == "normed":
            return launch(2, tensors)
        if name in ("mixed", "layer0"):
            return land({"mixed": 1, "layer0": 2}[name], tensors)
        gi = int(name[len("grads")])
        return reduce_ready(gi, grads, tensors) if name.endswith("_ready") else reduce_send(gi, tensors)

    small_names = [n for n in rp_names if n not in BLOCK_WEIGHTS] + vec_names

    loss_part, dx, grads = local_step(x[0], loss_target[0], full, stage)
    small_shapes = [grads[n].shape for n in small_names] + [(1, 1)]
    small_rows = _pack_rows(sum(int(np.prod(s)) for s in small_shapes), 16)
    small = _pack([grads[n] for n in small_names] + [loss_part[:, :1]], small_rows, F32).reshape(2, 1, small_rows // 2, LANES)
    last = len(REDUCE_GROUPS) - 1
    halves_of_blocks = [grads[n].reshape(2, 1, LRU_BLOCKS * HEAD // 2, HEAD) for n in BLOCK_WEIGHTS]
    reduce_ready(last, grads, extra=[small] + halves_of_blocks)
    reduce_send(last)
    reduced, result = {}, {}

    def finish(gi, after):
        g_own, g_sib = reduce_finish(reducing[gi], after)
        reduced.update(zip(list(REDUCE_GROUPS[gi]) + ["small"] + list(BLOCK_WEIGHTS), zip(g_own, g_sib)))

    def update(n):
        if n in TRANSPOSED:
            w3, m3, v3 = (jnp.swapaxes(t, 1, 2) for t in (given[n], mom1[n], mom2[n]))
            result[n] = tuple(jnp.swapaxes(t, 1, 2) for t in adamw_column_halves(w3, m3, v3, *reduced[n], name=f"adamw_{n}"))
            return
        done = None
        for a in (k for k, spec in BIG_ARRAYS.items() if spec[0] == n):
            r, cols = reduced[a][0].shape
            layer = BIG_ARRAYS[a][1] or 0
            w3, m3, v3 = (t if BIG_ARRAYS[a][1] is not None else t.reshape(1, 2 * r, cols) for t in (given[n], mom1[n], mom2[n]))
            done = adamw_halves(w3, m3, v3, *reduced[a], layer=layer, prev=done, name=f"adamw_{a}")
        result[n] = done

    for gi in range(last):
        finish(gi, (dx, reducing[last][1]))
    late = {BIG_ARRAYS[a][0] for a in REDUCE_GROUPS[last]}
    for n in MATMUL_SHARDED:
        if n not in late:
            update(n)
    finish(last, tuple(result[n][0] for n in MATMUL_SHARDED if n not in late))
    for n in MATMUL_SHARDED:
        if n in late:
            update(n)

    for n in BLOCK_WEIGHTS:
        w3, m3, v3 = (t.reshape(1, LRU_BLOCKS * HEAD, HEAD) for t in (given[n], mom1[n], mom2[n]))
        result[n] = adamw_halves(w3, m3, v3, *reduced[n], name=f"adamw_{n}")

    *small_sums, loss_sum = _unpack(by_core(*reduced["small"]).reshape(small_rows, LANES), small_shapes)
    loss = loss_sum[0, 0]
    g_small = dict(zip(small_names, small_sums))
    for n in vec_names:
        size = local[n].shape[SHARDED[n]]
        g_small[n] = lax.dynamic_slice_in_dim(g_small[n], chip * size, size, axis=SHARDED[n])
    views = [[_local_view(n, src[n]) for n in small_names] for src in (given, mom1, mom2)]
    d_s, m_s, v_s = adamw_many(views[0], [g_small[n] for n in small_names], views[1], views[2], name="adamw_small")
    for n, d, nm, nv in zip(small_names, d_s, m_s, v_s):
        result[n] = (g_small[n], d, nm, nv)

    outs = [[result[n][k].reshape(given[n].shape) for n in WEIGHTS] for k in range(4)]
    return (loss, dx[None], *outs[0], *outs[1], *outs[2], *outs[3])
```

```python
import functools

import numpy as np
import jax
import jax.numpy as jnp
from jax import lax
from jax.experimental import pallas as pl
from jax.experimental.pallas import tpu as pltpu
from jax.experimental.pallas import tpu_sc as plsc

F32 = jnp.float32
BF16 = jnp.bfloat16
HI = lax.Precision.HIGHEST
MESH = pl.DeviceIdType.MESH

SEQ = 2048
D_MODEL = 1024
N_HEADS = 4
HEAD = 128
RET_CHUNK = 128
RET_CHUNKS_PER_STEP = 2
GDN_CHUNK = 64
GDN_CHUNKS_PER_STEP = 8
GROUP = N_HEADS * HEAD
MIX_MAIN = 8 * GROUP
D_FF = 2816
LRU_BLOCKS = 8
LRU_C = 8.0
ROPE_BASE = 10000.0
EPS = 1e-6
N_SHARD = 4
LANES = 128

ADAM_LR, ADAM_B1, ADAM_B2, ADAM_EPS, ADAM_WD, ADAM_STEP = 0.001, 0.9, 0.999, 1e-08, 0.01, 10

VMEM_LIMIT_BYTES = 56 * 1024 * 1024

_roll = pltpu.roll


def _params(**kw):
    return pltpu.CompilerParams(vmem_limit_bytes=VMEM_LIMIT_BYTES, **kw)


def _sds(shape, dtype):
    return jax.ShapeDtypeStruct(tuple(shape), dtype)


def _shift_raw(x, d):
    n = x.shape[0]
    t = lax.broadcasted_iota(jnp.int32, x.shape, 0)
    if d > 0:
        return jnp.where(t >= d, _roll(x, d, 0), 0.0)
    return jnp.where(t < n + d, _roll(x, n + d, 0), 0.0)


@functools.partial(jax.custom_vjp, nondiff_argnums=(1,))
def shift_rows(x, d):
    return _shift_raw(x, d)


def _shift_fwd(x, d):
    return _shift_raw(x, d), None


def _shift_bwd(d, _, g):
    return (_shift_raw(g, -d),)


shift_rows.defvjp(_shift_fwd, _shift_bwd)


@jax.custom_vjp
def swap_halves(x):
    return _roll(x, HEAD // 2, 1)


def _swap_fwd(x):
    return _roll(x, HEAD // 2, 1), None


def _swap_bwd(_, g):
    return (_roll(g, HEAD // 2, 1),)


swap_halves.defvjp(_swap_fwd, _swap_bwd)


SCAN_BLOCK_ROWS = 64


def _scan_block(a, u, reverse):
    n = a.shape[0]
    t = lax.broadcasted_iota(jnp.int32, a.shape, 0)
    d = 1
    while d < n:
        if reverse:
            m = t < n - d
            a_s, u_s = _roll(a, n - d, 0), _roll(u, n - d, 0)
        else:
            m = t >= d
            a_s, u_s = _roll(a, d, 0), _roll(u, d, 0)
        u = a * jnp.where(m, u_s, 0.0) + u
        a = a * jnp.where(m, a_s, 1.0)
        d *= 2
    return a, u


def _scan_raw(a, u, reverse):
    n = a.shape[0]
    blocks = range(n // SCAN_BLOCK_ROWS)
    out = [None] * len(blocks)
    entering = None
    for b in (reversed(blocks) if reverse else blocks):
        rows = slice(b * SCAN_BLOCK_ROWS, (b + 1) * SCAN_BLOCK_ROWS)
        a_run, h = _scan_block(a[rows], u[rows], reverse)
        if entering is not None:
            h = a_run * entering + h
        out[b] = h
        entering = h[:1] if reverse else h[SCAN_BLOCK_ROWS - 1:]
    return jnp.concatenate(out, axis=0)


@jax.custom_vjp
def lin_scan(a, u):
    return _scan_raw(a, u, False)


def _lin_scan_fwd(a, u):
    hs = _scan_raw(a, u, False)
    return hs, (a, hs)


def _lin_scan_bwd(res, g):
    a, hs = res
    lam = _scan_raw(_shift_raw(a, -1), g, True)
    return lam * _shift_raw(hs, 1), lam


lin_scan.defvjp(_lin_scan_fwd, _lin_scan_bwd)


def _bdot(a, b, dims=(((1,), (0,)), ((), ()))):
    return lax.dot_general(a.astype(BF16), b.astype(BF16), dims, preferred_element_type=F32)


def _each(f, *seqs):
    return tuple(f(*a) for a in zip(*seqs))


def _split_bf16(a):
    hi = a.astype(BF16)
    return hi, (a - hi.astype(F32)).astype(BF16)


def _dot3_raw(a_s, b_s):
    a_hl = _each(_split_bf16, a_s)
    b_hl = _each(_split_bf16, b_s)
    hh = _each(lambda a, b: _bdot(a[0], b[0]), a_hl, b_hl)
    hl = _each(lambda a, b: _bdot(a[0], b[1]), a_hl, b_hl)
    lh = _each(lambda a, b: _bdot(a[1], b[0]), a_hl, b_hl)
    return _each(lambda x, y, z: x + (y + z), hh, hl, lh)


@jax.custom_vjp
def dot3(a_s, b_s):
    return _dot3_raw(a_s, b_s)


def _dot3_fwd(a_s, b_s):
    return _dot3_raw(a_s, b_s), (a_s, b_s)


def _dot3_bwd(res, g_s):
    a_s, b_s = res
    return (_each(lambda g, b: _bdot(g, b, (((1,), (1,)), ((), ()))), g_s, b_s),
            _each(lambda a, g: _bdot(a, g, (((0,), (0,)), ((), ()))), a_s, g_s))


dot3.defvjp(_dot3_fwd, _dot3_bwd)


def _eye(n):
    i = lax.broadcasted_iota(jnp.int32, (n, n), 0)
    j = lax.broadcasted_iota(jnp.int32, (n, n), 1)
    return (i == j).astype(F32)


def _unit_lower_inverse_raw(lmats):
    n = lmats[0].shape[0]
    eye = _eye(n)
    ps = _each(lambda l: -l, lmats)
    invs = _each(lambda x: eye + x, ps)
    k = 1
    while 2 * k < n:
        ps = _each(lambda p: _bdot(p, p), ps)
        invs = _each(lambda inv, p: inv + _bdot(inv, p), invs, ps)
        k *= 2
    prods = _dot3_raw(lmats, invs)
    resids = _each(lambda inv, pr: eye - inv - pr, invs, prods)
    return _each(lambda inv, r: inv + _bdot(inv, r), invs, resids)


@jax.custom_vjp
def unit_lower_inverse(lmats):
    return _unit_lower_inverse_raw(lmats)


def _uli_fwd(lmats):
    invs = _unit_lower_inverse_raw(lmats)
    return invs, invs


def _uli_bwd(invs, g_s):
    ms = _each(lambda inv, g: _bdot(inv, g, (((0,), (0,)), ((), ()))), invs, g_s)
    return (_each(lambda m, inv: -_bdot(m, inv, (((1,), (1,)), ((), ()))), ms, invs),)


unit_lower_inverse.defvjp(_uli_fwd, _uli_bwd)


def _cumsum_raw(x, reverse):
    n = x.shape[0]
    t = lax.broadcasted_iota(jnp.int32, x.shape, 0)
    d = 1
    while d < n:
        if reverse:
            x = x + jnp.where(t < n - d, _roll(x, n - d, 0), 0.0)
        else:
            x = x + jnp.where(t >= d, _roll(x, d, 0), 0.0)
        d *= 2
    return x


@jax.custom_vjp
def cumsum_rows(x):
    return _cumsum_raw(x, False)


def _cumsum_fwd(x):
    return _cumsum_raw(x, False), None


def _cumsum_bwd(_, g):
    return (_cumsum_raw(g, True),)


cumsum_rows.defvjp(_cumsum_fwd, _cumsum_bwd)


_NT = (((1,), (1,)), ((), ()))
_TN = (((0,), (0,)), ((), ()))


def _softplus(x):
    return jnp.maximum(x, 0.0) + jnp.log1p(jnp.exp(-jnp.abs(x)))


def _expm1_nonpos(x):
    poly = x * (1.0 + x * (0.5 + x * (1.0 / 6 + x * (1.0 / 24 + x * (1.0 / 120 + x * (1.0 / 720))))))
    return jnp.where(x > -0.25, poly, jnp.exp(x) - 1.0)


def _rms(x):
    return x * lax.rsqrt(jnp.mean(x * x, axis=-1, keepdims=True) + EPS)


def _causal_conv(x, w, width):
    y = w[width - 1:width, :] * x
    for j in range(width - 1):
        y = y + w[j:j + 1, :] * shift_rows(x, width - 1 - j)
    return y


def _norm_fn(x, g):
    return _rms(x) * g


def _ffn_act_fn(ug, uv, wg, wv, bg, bv):
    return jax.nn.silu(_causal_conv(ug, wg, 3) + bg) * (_causal_conv(uv, wv, 3) + bv)


def _gdn_conv_fn(x, w):
    return jax.nn.silu(_causal_conv(x, w, 4))


def _lru_fn(gate, x, cw, cb, wa, ba, wx, bx, lam):
    xr = _causal_conv(x, cw, 4) + cb
    r = jax.nn.sigmoid(_bdot(xr, wa) + ba)
    i = jax.nn.sigmoid(_bdot(xr, wx) + bx)
    log_a = -LRU_C * r * _softplus(-lam)
    a = jnp.exp(log_a)
    u = jnp.sqrt(-_expm1_nonpos(2.0 * log_a)) * (i * xr)
    hs = lin_scan(a, u)
    return jax.nn.gelu(gate) * hs


def _ret_fn(qs, ks, vs, gates, states, cos2, sin2, dmasks, ktails, qdecs, cdecs):
    c = RET_CHUNK
    n_heads = len(qs)
    n_chunks = qs[0].shape[0] // c
    units = tuple((ci, h) for ci in range(n_chunks) for h in range(n_heads))

    def rows(x, ci):
        return x[ci * c:(ci + 1) * c]

    qrs = tuple(rows(qs[h], ci) * rows(cos2, ci) + swap_halves(rows(qs[h], ci)) * rows(sin2, ci) for ci, h in units)
    krs = tuple((rows(ks[h], ci) * rows(cos2, ci) + swap_halves(rows(ks[h], ci)) * rows(sin2, ci)) * (HEAD ** -0.5) for ci, h in units)
    vus = tuple(rows(vs[h], ci) for ci, h in units)
    scores = tuple(_bdot(q, k, _NT) * dmasks[h] for q, k, (_, h) in zip(qrs, krs, units))
    intra = _each(lambda sc, v: _bdot(sc, v), scores, vus)
    outs = []
    for ci in range(n_chunks):
        mine = slice(ci * n_heads, (ci + 1) * n_heads)
        inter = _each(lambda q, d, s: _bdot(q * d, s), qrs[mine], qdecs, states)
        outs.append(_each(lambda a, b: a + b, intra[mine], inter))
        states = _each(lambda s, cd, k, kt, v: s * cd + _bdot(k * kt, v, _TN), states, cdecs, krs[mine], ktails, vus[mine])
    ys = tuple(_rms(jnp.concatenate([outs[ci][h] for ci in range(n_chunks)], axis=0)) * jax.nn.silu(gates[h]) for h in range(n_heads))
    return ys, states


def _pick_lane(x, lane_idx):
    lane = lax.broadcasted_iota(jnp.int32, x.shape, 1)
    return jnp.sum(jnp.where(lane == lane_idx, x, 0.0), axis=1, keepdims=True)


def _l2norm(x):
    return x * lax.rsqrt(jnp.sum(x * x, axis=-1, keepdims=True) + EPS)


def _gdn_fn(qcs, kcs, vcs, gates, small, a_log, dt_bias, gain, states):
    c = GDN_CHUNK
    n_heads = len(qcs)
    n_chunks = qcs[0].shape[0] // c
    units = tuple((ci, h) for ci in range(n_chunks) for h in range(n_heads))

    def unit_rows(per_head):
        return tuple(per_head[h][ci * c:(ci + 1) * c] for ci, h in units)

    smalls = tuple(small[ci * c:(ci + 1) * c] for ci, _ in units)
    heads = tuple(h for _, h in units)
    intra = _gdn_intra(unit_rows(qcs), unit_rows(kcs), unit_rows(vcs), smalls, heads, a_log, dt_bias)
    outs = []
    for ci in range(n_chunks):
        mine = slice(ci * n_heads, (ci + 1) * n_heads)
        os_, states = _gdn_inter(*(part[mine] for part in intra), states)
        outs.append(os_)
    ys = tuple(_rms(jnp.concatenate([outs[ci][h] for ci in range(n_chunks)], axis=0)) * gain * jax.nn.silu(gates[h])
               for h in range(n_heads))
    return ys, states


def _gdn_inter(qs, ks, us, ws, attns, gcs, g_lasts, states):
    v_news = _each(lambda u, w, s: u - _bdot(w, s), us, ws, states)
    inter = _each(lambda q, gc, s: _bdot(q * jnp.exp(gc), s), qs, gcs, states)
    os_ = _each(lambda x, a, v: x + _bdot(a, v), inter, attns, v_news)
    new_states = _each(lambda s, gl, k, gc, v: s * jnp.exp(gl) + _bdot(k * jnp.exp(gl - gc), v, _TN), states, g_lasts, ks, gcs, v_news)
    return os_, new_states


def _gdn_intra(qcs, kcs, vcs, smalls, heads, a_log, dt_bias):
    c = GDN_CHUNK
    qs = _each(lambda x: _l2norm(x) * (HEAD ** -0.5), qcs)
    ks = _each(_l2norm, kcs)
    betas = _each(lambda sm, h: jax.nn.sigmoid(_pick_lane(sm, h)), smalls, heads)
    gs = _each(lambda sm, h: -jnp.exp(_pick_lane(a_log, h)) * _softplus(_pick_lane(sm, h + N_HEADS) + _pick_lane(dt_bias, h)),
               smalls, heads)
    i = lax.broadcasted_iota(jnp.int32, (c, c), 0)
    j = lax.broadcasted_iota(jnp.int32, (c, c), 1)
    tril = i >= j
    gcs = _each(lambda g: cumsum_rows(jnp.broadcast_to(g, (c, LANES)))[:, :1], gs)
    gc_rows = _each(lambda gc: jnp.broadcast_to(gc, (c, c)), gcs)
    decays = _each(lambda r: jnp.where(tril, jnp.exp(jnp.where(tril, r - r.T, 0.0)), 0.0), gc_rows)
    kbs = _each(lambda k, b: k * b, ks, betas)
    lmats = _each(lambda kb, k, d: jnp.where(i > j, _bdot(kb, k, _NT) * d, 0.0), kbs, ks, decays)
    attns = _each(lambda q, k, d: jnp.where(tril, _bdot(q, k, _NT) * d, 0.0), qs, ks, decays)
    invs = unit_lower_inverse(lmats)
    us = dot3(invs, _each(lambda v, b: v * b, vcs, betas))
    ws = dot3(invs, _each(lambda kb, gc: kb * jnp.exp(gc), kbs, gcs))
    g_lasts = _each(lambda g: jnp.sum(g, axis=0, keepdims=True), gs)
    return qs, ks, us, ws, attns, gcs, g_lasts


def _final_fn(h, g, target):
    y = _rms(h) * g
    return 0.5 * jnp.sum(jnp.mean(jnp.square(y - target), axis=-1, keepdims=True), axis=0, keepdims=True)


def _tile(n, candidates):
    for t in candidates:
        if n % t == 0:
            return t
    raise ValueError(f"no tile for {n}")


MATMUL_RESIDENT_LHS_BYTES = 8 * 1024 * 1024


def matmul(a, b, *, ta=False, tb=False, add=None, out_dtype=F32, tm=None, tn=None, split=None, layer=None, column_halves=None, name):
    m = a.shape[1] if ta else a.shape[0]
    k = a.shape[0] if ta else a.shape[1]
    n = b.shape[0] if tb else b.shape[1]
    assert k == (b.shape[1] if tb else b.shape[0])
    out_shape, out_block, out_index = (m, n), None, lambda i, j: (i, j)
    if split is not None:
        dims4, perm = split
        out_shape = tuple(dims4[p] for p in perm)
        r, cols = out_shape[2:]
        tm, tn = m, tn or _tile(cols, (1408, 512))
        cb = cols // tn
        if perm == (0, 2, 1, 3):
            out_block, out_index = (2, None, r, tn), lambda i, j: (0, j // cb, 0, j % cb)
        elif perm == (1, 0, 2, 3):
            out_block, out_index = (2, N_SHARD, r, tn), lambda i, j: (0, 0, 0, j)
        else:
            raise ValueError(perm)
    if tm is None and not ta and m * k * a.dtype.itemsize <= MATMUL_RESIDENT_LHS_BYTES:
        tm = m
    tm = tm or _tile(m, (1024, 512, 1408, 256, 128))
    tn = tn or _tile(n, (512, 1408, 256, 128))
    aliases, prev, keep_rows = {}, None, None
    if layer is not None:
        index, count, prev = layer
        out_shape, out_block, out_index = (count, m, n), (None, tm, tn), lambda i, j: (index, i, j)
    if column_halves is not None:
        total_rows, first_row, keep_rows, prev = column_halves
        tn = n // 2
        rows_out = keep_rows or tm
        out_shape, out_block = (2, total_rows, tn), (None, rows_out, tn)
        out_index = lambda i, j: (j, first_row // rows_out + i, 0)
    dims = (((0 if ta else 1,), (1 if tb else 0,)), ((), ()))

    def body(a_ref, b_ref, *rest):
        acc = lax.dot_general(a_ref[...].astype(BF16), b_ref[...].astype(BF16), dims, preferred_element_type=F32)
        if add is not None:
            acc = acc + rest[0][...]
        o_ref = rest[-1]
        acc = acc.astype(out_dtype)
        if split is not None and split[1] == (1, 0, 2, 3):
            rows = o_ref.shape[2]
            for s in range(N_SHARD):
                for h in range(2):
                    o_ref[h, s] = acc[(2 * s + h) * rows:(2 * s + h + 1) * rows]
        elif keep_rows is not None:
            o_ref[...] = acc[:keep_rows]
        else:
            o_ref[...] = acc.reshape(o_ref.shape)

    a_spec = pl.BlockSpec((k, tm), lambda i, j: (0, i)) if ta else pl.BlockSpec((tm, k), lambda i, j: (i, 0))
    b_spec = pl.BlockSpec((tn, k), lambda i, j: (j, 0)) if tb else pl.BlockSpec((k, tn), lambda i, j: (0, j))
    o_spec = pl.BlockSpec(out_block or (tm, tn), out_index)
    in_specs, args = [a_spec, b_spec], [a, b]
    if add is not None:
        in_specs.append(o_spec)
        args.append(add)
    if prev is not None:
        aliases = {len(args): 0}
        in_specs.append(pl.BlockSpec(memory_space=pl.ANY))
        args.append(prev)
    return pl.pallas_call(body, out_shape=_sds(out_shape, out_dtype), grid=(m // tm, n // tn), in_specs=in_specs,
                          out_specs=o_spec, input_output_aliases=aliases, compiler_params=_params(), name=name)(*args)


def norm_matmul(x, g, b, *, tb=False, name):
    t, k = x.shape
    n = b.shape[0] if tb else b.shape[1]
    tn = _tile(n, (512, 1408, 256, 128))
    dims = (((1,), (1 if tb else 0,)), ((), ()))

    def body(x_ref, g_ref, b_ref, o_ref, hn_ref):
        @pl.when(pl.program_id(0) == 0)
        def _():
            hn_ref[...] = _norm_fn(x_ref[...], g_ref[...]).astype(BF16)

        o_ref[...] = lax.dot_general(hn_ref[...], b_ref[...].astype(BF16), dims, preferred_element_type=F32)

    b_spec = pl.BlockSpec((tn, k), lambda j: (j, 0)) if tb else pl.BlockSpec((k, tn), lambda j: (0, j))
    whole = pl.BlockSpec((t, k), lambda j: (0, 0))
    return pl.pallas_call(body, out_shape=(_sds((t, n), F32), _sds((t, k), BF16)), grid=(n // tn,),
                          in_specs=[whole, pl.BlockSpec((1, k), lambda j: (0, 0)), b_spec],
                          out_specs=(pl.BlockSpec((t, tn), lambda j: (0, j)), whole), compiler_params=_params(), name=name)(x, g, b)


ROW_TILE = 256


def norm_bwd(x, g, dy, dres, *, name):
    t, d = x.shape

    def body(x_ref, g_ref, dy_ref, dres_ref, dx_ref, dg_ref):
        _, vjp = jax.vjp(_norm_fn, x_ref[...], g_ref[...])
        dx, dg = vjp(dy_ref[...])
        dx_ref[...] = dx + dres_ref[...]

        @pl.when(pl.program_id(0) == 0)
        def _():
            dg_ref[...] = jnp.zeros_like(dg_ref)

        dg_ref[...] += dg

    row = pl.BlockSpec((ROW_TILE, d), lambda i: (i, 0))
    vec = pl.BlockSpec((1, d), lambda i: (0, 0))
    return pl.pallas_call(body, out_shape=(_sds((t, d), F32), _sds((1, d), F32)), grid=(t // ROW_TILE,),
                          in_specs=[row, vec, row, row], out_specs=(row, vec), compiler_params=_params(), name=name)(x, g, dy, dres)


def final_fwd_bwd(h, g, target, *, name):
    t, d = h.shape

    def body(h_ref, g_ref, t_ref, loss_ref, dh_ref, dg_ref):
        tgt = t_ref[...]
        loss, vjp = jax.vjp(lambda hh, gg: _final_fn(hh, gg, tgt), h_ref[...], g_ref[...])
        dh, dg = vjp(jnp.ones((1, 1), F32))
        dh_ref[...] = dh

        @pl.when(pl.program_id(0) == 0)
        def _():
            dg_ref[...] = jnp.zeros_like(dg_ref)
            loss_ref[...] = jnp.zeros_like(loss_ref)

        dg_ref[...] += dg
        loss_ref[...] += jnp.broadcast_to(loss, loss_ref.shape)

    row = pl.BlockSpec((ROW_TILE, d), lambda i: (i, 0))
    vec = pl.BlockSpec((1, d), lambda i: (0, 0))
    return pl.pallas_call(body, out_shape=(_sds((1, LANES), F32), _sds((t, d), F32), _sds((1, d), F32)), grid=(t // ROW_TILE,),
                          in_specs=[row, vec, row], out_specs=(pl.BlockSpec((1, LANES), lambda i: (0, 0)), row, vec),
                          compiler_params=_params(), name=name)(h, g, target)


FFN_FWD_COLS = 256
FFN_BWD_COLS = 128


def ffn_act_fwd(u, cw, cb, *, name):
    t = u.shape[0]
    w = FFN_FWD_COLS
    nb = D_FF // w

    def body(ug_ref, uv_ref, wg_ref, wv_ref, bg_ref, bv_ref, o_ref):
        o_ref[...] = _ffn_act_fn(ug_ref[...], uv_ref[...], wg_ref[...], wv_ref[...], bg_ref[...], bv_ref[...]).astype(BF16)

    def col(rows, off):
        return pl.BlockSpec((rows, w), lambda j: (0, j + off))

    return pl.pallas_call(body, out_shape=_sds((t, D_FF), BF16), grid=(nb,),
                          in_specs=[col(t, 0), col(t, nb), col(3, 0), col(3, nb), col(1, 0), col(1, nb)],
                          out_specs=col(t, 0), compiler_params=_params(), name=name)(u, u, cw, cw, cb, cb)


def _put_column_blocks(step, n_steps, blocks, dst_ref, width, stage_ref, sems):
    def copies(at):
        slot = at % 2
        return [pltpu.make_async_copy(stage_ref.at[slot, p], dst_ref.at[:, pl.ds(pl.multiple_of((p * n_steps + at) * width, LANES), width)],
                                      sems.at[slot, p]) for p in range(len(blocks))]

    @pl.when(step >= 2)
    def _():
        for cp in copies(step - 2):
            cp.wait()

    for p, value in enumerate(blocks):
        stage_ref[step % 2, p] = value
    for cp in copies(step):
        cp.start()

    @pl.when(step == n_steps - 1)
    def _():
        for cp in copies(step - 1) + copies(step):
            cp.wait()


def ffn_act_bwd(u, cw, cb, da, *, name):
    t = u.shape[0]
    w = FFN_BWD_COLS
    nb = D_FF // w

    def body(ug_ref, uv_ref, wg_ref, wv_ref, bg_ref, bv_ref, da_ref, dug_ref, duv_ref, dwg_ref, dwv_ref, dbg_ref, dbv_ref):
        _, vjp = jax.vjp(_ffn_act_fn, ug_ref[...], uv_ref[...], wg_ref[...], wv_ref[...], bg_ref[...], bv_ref[...])
        dug, duv, dwg, dwv, dbg, dbv = vjp(da_ref[...])
        dug_ref[...] = dug.astype(BF16)
        duv_ref[...] = duv.astype(BF16)
        dwg_ref[...] = dwg
        dwv_ref[...] = dwv
        dbg_ref[...] = dbg
        dbv_ref[...] = dbv

    def col(rows, off):
        return pl.BlockSpec((rows, w), lambda j: (0, j + off))

    outs = pl.pallas_call(
        body, out_shape=(_sds((t, D_FF), BF16), _sds((t, D_FF), BF16), _sds((3, D_FF), F32), _sds((3, D_FF), F32),
                         _sds((1, D_FF), F32), _sds((1, D_FF), F32)),
        grid=(nb,), in_specs=[col(t, 0), col(t, nb), col(3, 0), col(3, nb), col(1, 0), col(1, nb), col(t, 0)],
        out_specs=(col(t, 0), col(t, 0), col(3, 0), col(3, 0), col(1, 0), col(1, 0)), compiler_params=_params(), name=name,
    )(u, u, cw, cw, cb, cb, da)
    dug, duv, dwg, dwv, dbg, dbv = outs
    return jnp.concatenate([dug, duv], axis=1), jnp.concatenate([dwg, dwv], axis=1), jnp.concatenate([dbg, dbv], axis=1)


GDN_CONV_COLS = 256
GDN_CONV_OFF = 4 * GROUP


def gdn_conv_fwd(p, cw, *, name):
    t = p.shape[0]
    w = GDN_CONV_COLS
    nb = 3 * GROUP // w
    off = GDN_CONV_OFF // w

    def body(x_ref, w_ref, o_ref):
        o_ref[...] = _gdn_conv_fn(x_ref[...], w_ref[...])

    return pl.pallas_call(body, out_shape=_sds((t, 3 * GROUP), F32), grid=(nb,),
                          in_specs=[pl.BlockSpec((t, w), lambda j: (0, j + off)), pl.BlockSpec((4, w), lambda j: (0, j))],
                          out_specs=pl.BlockSpec((t, w), lambda j: (0, j)), compiler_params=_params(), name=name)(p, cw)


def gdn_conv_bwd(p, cw, dc, *, name):
    t = p.shape[0]
    w = GDN_CONV_COLS
    nb = 3 * GROUP // w
    off = GDN_CONV_OFF // w

    def body(x_ref, w_ref, dc_ref, dx_ref, dw_ref):
        _, vjp = jax.vjp(_gdn_conv_fn, x_ref[...], w_ref[...])
        dx, dw = vjp(dc_ref[...])
        dx_ref[...] = dx.astype(BF16)
        dw_ref[...] = dw

    blk = pl.BlockSpec((t, w), lambda j: (0, j))
    wblk = pl.BlockSpec((4, w), lambda j: (0, j))
    return pl.pallas_call(body, out_shape=(_sds((t, 3 * GROUP), BF16), _sds((4, 3 * GROUP), F32)), grid=(nb,),
                          in_specs=[pl.BlockSpec((t, w), lambda j: (0, j + off)), wblk, blk], out_specs=(blk, wblk),
                          compiler_params=_params(), name=name)(p, cw, dc)


def _lru_specs(t):
    w = D_MODEL // LRU_BLOCKS
    gate = pl.BlockSpec((t, w), lambda j: (0, j))
    xin = pl.BlockSpec((t, w), lambda j: (0, j + LRU_BLOCKS))
    cw = pl.BlockSpec((4, w), lambda j: (0, j))
    vec = pl.BlockSpec((1, w), lambda j: (0, j))
    mat = pl.BlockSpec((None, w, w), lambda j: (j, 0, 0))
    return gate, xin, cw, vec, mat


def lru_fwd(gx, cw, cb, wa, ba, wx, bx, lam, *, name):
    t = gx.shape[0]
    gate, xin, cws, vec, mat = _lru_specs(t)

    def body(g_ref, x_ref, cw_ref, cb_ref, wa_ref, ba_ref, wx_ref, bx_ref, lam_ref, o_ref):
        o_ref[...] = _lru_fn(g_ref[...], x_ref[...], cw_ref[...], cb_ref[...], wa_ref[...], ba_ref[...], wx_ref[...],
                             bx_ref[...], lam_ref[...]).astype(BF16)

    return pl.pallas_call(body, out_shape=_sds((t, D_MODEL), BF16), grid=(LRU_BLOCKS,),
                          in_specs=[gate, xin, cws, vec, mat, vec, mat, vec, vec], out_specs=gate,
                          compiler_params=_params(), name=name)(gx, gx, cw, cb, wa, ba, wx, bx, lam)


def lru_bwd(gx, cw, cb, wa, ba, wx, bx, lam, dy, *, name):
    t = gx.shape[0]
    gate, xin, cws, vec, mat = _lru_specs(t)

    def body(g_ref, x_ref, cw_ref, cb_ref, wa_ref, ba_ref, wx_ref, bx_ref, lam_ref, dy_ref,
             dgx_ref, dcw_ref, dcb_ref, dwa_ref, dba_ref, dwx_ref, dbx_ref, dlam_ref, stage_ref, sems):
        _, vjp = jax.vjp(_lru_fn, g_ref[...], x_ref[...], cw_ref[...], cb_ref[...], wa_ref[...], ba_ref[...], wx_ref[...],
                         bx_ref[...], lam_ref[...])
        dg, dx, dcw, dcb, dwa, dba, dwx, dbx, dlam = vjp(dy_ref[...])
        _put_column_blocks(pl.program_id(0), LRU_BLOCKS, (dg.astype(BF16), dx.astype(BF16)), dgx_ref, D_MODEL // LRU_BLOCKS, stage_ref, sems)
        dcw_ref[...] = dcw
        dcb_ref[...] = dcb
        dwa_ref[...] = dwa
        dba_ref[...] = dba
        dwx_ref[...] = dwx
        dbx_ref[...] = dbx
        dlam_ref[...] = dlam

    d = D_MODEL
    w = d // LRU_BLOCKS
    out_shape = (_sds((t, 2 * d), BF16), _sds((4, d), F32), _sds((1, d), F32), _sds((LRU_BLOCKS, w, w), F32),
                 _sds((1, d), F32), _sds((LRU_BLOCKS, w, w), F32), _sds((1, d), F32), _sds((1, d), F32))
    return pl.pallas_call(body, out_shape=out_shape, grid=(LRU_BLOCKS,),
                          in_specs=[gate, xin, cws, vec, mat, vec, mat, vec, vec, gate],
                          out_specs=(pl.BlockSpec(memory_space=pl.ANY), cws, vec, mat, vec, mat, vec, vec),
                          scratch_shapes=[pltpu.VMEM((2, 2, t, w), BF16), pltpu.SemaphoreType.DMA((2, 2))],
                          compiler_params=_params(), name=name)(gx, gx, cw, cb, wa, ba, wx, bx, lam, dy)


def _ret_tables():
    half = HEAD // 2
    inv_freq = (np.float32(ROPE_BASE) ** (-np.arange(half, dtype=np.float32) / np.float32(half))).astype(np.float32)
    ang = (np.arange(SEQ, dtype=np.float32)[:, None] * inv_freq[None, :]).astype(np.float64)
    cos2 = np.concatenate([np.cos(ang), np.cos(ang)], axis=1).astype(np.float32)
    sin2 = np.concatenate([-np.sin(ang), np.sin(ang)], axis=1).astype(np.float32)
    c = RET_CHUNK
    log_gamma = np.log1p(-np.exp2(-5.0 - np.arange(N_HEADS, dtype=np.float64)))
    idx = np.arange(c, dtype=np.float64)
    rel = idx[:, None] - idx[None, :]
    dmask = np.where(rel >= 0, np.exp(log_gamma[:, None, None] * np.maximum(rel, 0.0)), 0.0)
    ones = np.ones((N_HEADS, c, HEAD))
    ktail = np.exp(log_gamma[:, None] * (c - 1 - idx))[:, :, None] * ones
    qdec = np.exp(log_gamma[:, None] * (idx + 1.0))[:, :, None] * ones
    cdec = np.exp(log_gamma * c)[:, None, None] * ones
    return tuple(jnp.asarray(a, F32) for a in (cos2, sin2, dmask, ktail, qdec, cdec))


def _ret_specs(rev):
    c = RET_CHUNK * RET_CHUNKS_PER_STEP
    nc = SEQ // c

    def n_of(n):
        return nc - 1 - n if rev else n

    def group(off):
        return pl.BlockSpec((c, GROUP), lambda n: (n_of(n), off))

    tab = pl.BlockSpec((c, HEAD), lambda n: (n_of(n), 0))
    const = pl.BlockSpec((N_HEADS, RET_CHUNK, HEAD), lambda n: (0, 0, 0))
    state = pl.BlockSpec((N_HEADS, None, HEAD, HEAD), lambda n: (0, n_of(n), 0, 0))
    return group, tab, const, state, nc


def _head(h):
    return slice(h * HEAD, (h + 1) * HEAD)


def ret_fwd(p, tables, *, name):
    group, tab, const, state, nc = _ret_specs(False)

    def body(q_ref, k_ref, v_ref, g_ref, cos_ref, sin_ref, dm_ref, kt_ref, qd_ref, cd_ref, y_ref, st_ref, s_scr):
        @pl.when(pl.program_id(0) == 0)
        def _():
            s_scr[...] = jnp.zeros_like(s_scr)

        heads = range(N_HEADS)
        states = tuple(s_scr[h] for h in heads)
        ys, new_states = _ret_fn(*(tuple(r[:, _head(h)] for h in heads) for r in (q_ref, k_ref, v_ref, g_ref)), states,
                                 cos_ref[...], sin_ref[...], *(tuple(r[h] for h in heads) for r in (dm_ref, kt_ref, qd_ref, cd_ref)))
        for h in heads:
            st_ref[h] = states[h]
            y_ref[:, _head(h)] = ys[h].astype(BF16)
            s_scr[h] = new_states[h]

    return pl.pallas_call(
        body, out_shape=(_sds((SEQ, GROUP), BF16), _sds((N_HEADS, nc, HEAD, HEAD), F32)), grid=(nc,),
        in_specs=[group(0), group(1), group(2), group(3), tab, tab, const, const, const, const],
        out_specs=(group(0), state), scratch_shapes=[pltpu.VMEM((N_HEADS, HEAD, HEAD), F32)], compiler_params=_params(), name=name,
    )(p, p, p, p, *tables)


def ret_bwd(p, tables, states, dy, *, name):
    group, tab, const, state, nc = _ret_specs(True)

    def body(q_ref, k_ref, v_ref, g_ref, cos_ref, sin_ref, dm_ref, kt_ref, qd_ref, cd_ref, st_ref, dy_ref,
             dq_ref, dk_ref, dv_ref, dg_ref, ds_scr):
        @pl.when(pl.program_id(0) == 0)
        def _():
            ds_scr[...] = jnp.zeros_like(ds_scr)

        heads = range(N_HEADS)
        consts = (cos_ref[...], sin_ref[...], *(tuple(r[h] for h in heads) for r in (dm_ref, kt_ref, qd_ref, cd_ref)))
        _, vjp = jax.vjp(lambda *a: _ret_fn(*a, *consts), *(tuple(r[:, _head(h)] for h in heads) for r in (q_ref, k_ref, v_ref, g_ref)),
                         tuple(st_ref[h] for h in heads))
        dqs, dks, dvs, dgs, dss = vjp((tuple(dy_ref[:, _head(h)] for h in heads), tuple(ds_scr[h] for h in heads)))
        for h in heads:
            dq_ref[:, _head(h)] = dqs[h].astype(BF16)
            dk_ref[:, _head(h)] = dks[h].astype(BF16)
            dv_ref[:, _head(h)] = dvs[h].astype(BF16)
            dg_ref[:, _head(h)] = dgs[h].astype(BF16)
            ds_scr[h] = dss[h]

    out = _sds((SEQ, GROUP), BF16)
    return pl.pallas_call(
        body, out_shape=(out, out, out, out), grid=(nc,),
        in_specs=[group(0), group(1), group(2), group(3), tab, tab, const, const, const, const, state, group(0)],
        out_specs=(group(0), group(0), group(0), group(0)), scratch_shapes=[pltpu.VMEM((N_HEADS, HEAD, HEAD), F32)],
        compiler_params=_params(), name=name,
    )(p, p, p, p, *tables, states, dy)


def _gdn_specs(rev):
    c = GDN_CHUNK * GDN_CHUNKS_PER_STEP
    nc = SEQ // c

    def n_of(n):
        return nc - 1 - n if rev else n

    def group(off):
        return pl.BlockSpec((c, GROUP), lambda n: (n_of(n), off))

    small = pl.BlockSpec((c, LANES), lambda n: (n_of(n), 0))
    vec = pl.BlockSpec((1, LANES), lambda n: (0, 0))
    state = pl.BlockSpec((N_HEADS, None, HEAD, HEAD), lambda n: (0, n_of(n), 0, 0))
    return group, small, vec, state, nc


GDN_GATE_GROUP = 7


def gdn_fwd(conv, p, small, a_log, dt_bias, gain, *, name):
    group, sm, vec, state, nc = _gdn_specs(False)

    def body(q_ref, k_ref, v_ref, g_ref, sm_ref, al_ref, dt_ref, gn_ref, y_ref, st_ref, s_scr):
        @pl.when(pl.program_id(0) == 0)
        def _():
            s_scr[...] = jnp.zeros_like(s_scr)

        states = tuple(s_scr[h] for h in range(N_HEADS))
        ys, new_states = _gdn_fn(*(tuple(r[:, _head(h)] for h in range(N_HEADS)) for r in (q_ref, k_ref, v_ref, g_ref)),
                                 sm_ref[...], al_ref[...], dt_ref[...], gn_ref[...], states)
        for h in range(N_HEADS):
            st_ref[h] = states[h]
            y_ref[:, _head(h)] = ys[h].astype(BF16)
            s_scr[h] = new_states[h]

    return pl.pallas_call(
        body, out_shape=(_sds((SEQ, GROUP), BF16), _sds((N_HEADS, nc, HEAD, HEAD), F32)), grid=(nc,),
        in_specs=[group(0), group(1), group(2), group(GDN_GATE_GROUP), sm, vec, vec, vec], out_specs=(group(0), state),
        scratch_shapes=[pltpu.VMEM((N_HEADS, HEAD, HEAD), F32)], compiler_params=_params(), name=name,
    )(conv, conv, conv, p, small, a_log, dt_bias, gain)


def gdn_bwd(conv, p, small, a_log, dt_bias, gain, states, dy, *, name):
    group, sm, vec, state, nc = _gdn_specs(True)

    def body(q_ref, k_ref, v_ref, g_ref, sm_ref, al_ref, dt_ref, gn_ref, st_ref, dy_ref,
             dq_ref, dk_ref, dv_ref, dg_ref, dsm_ref, dal_ref, ddt_ref, dgn_ref, ds_scr):
        @pl.when(pl.program_id(0) == 0)
        def _():
            ds_scr[...] = jnp.zeros_like(ds_scr)
            dal_ref[...] = jnp.zeros_like(dal_ref)
            ddt_ref[...] = jnp.zeros_like(ddt_ref)
            dgn_ref[...] = jnp.zeros_like(dgn_ref)

        per_head = tuple(tuple(r[:, _head(h)] for h in range(N_HEADS)) for r in (q_ref, k_ref, v_ref, g_ref))
        _, vjp = jax.vjp(_gdn_fn, *per_head, sm_ref[...], al_ref[...], dt_ref[...], gn_ref[...],
                         tuple(st_ref[h] for h in range(N_HEADS)))
        cts = (tuple(dy_ref[:, _head(h)] for h in range(N_HEADS)), tuple(ds_scr[h] for h in range(N_HEADS)))
        dqs, dks, dvs, dgs, dsm, dal, ddt, dgn, dss = vjp(cts)
        for h in range(N_HEADS):
            dq_ref[:, _head(h)] = dqs[h]
            dk_ref[:, _head(h)] = dks[h]
            dv_ref[:, _head(h)] = dvs[h]
            dg_ref[:, _head(h)] = dgs[h].astype(BF16)
            ds_scr[h] = dss[h]
        dsm_ref[...] = dsm
        dal_ref[...] += dal
        ddt_ref[...] += ddt
        dgn_ref[...] += dgn

    f = _sds((SEQ, GROUP), F32)
    pv = _sds((1, LANES), F32)
    return pl.pallas_call(
        body, out_shape=(f, f, f, _sds((SEQ, GROUP), BF16), _sds((SEQ, LANES), F32), pv, pv, pv), grid=(nc,),
        in_specs=[group(0), group(1), group(2), group(GDN_GATE_GROUP), sm, vec, vec, vec, state, group(1)],
        out_specs=(group(0), group(0), group(0), group(0), sm, vec, vec, vec), scratch_shapes=[pltpu.VMEM((N_HEADS, HEAD, HEAD), F32)],
        compiler_params=_params(), name=name,
    )(conv, conv, conv, p, small, a_log, dt_bias, gain, states, dy)


ELEMENTWISE_BLOCK_BYTES = 2 * 1024 * 1024


def _row_tile(r, c):
    best = None
    for tr in range(8, r + 1, 8):
        if r % tr == 0 and tr * c * 4 <= ELEMENTWISE_BLOCK_BYTES:
            best = tr
    if best is None:
        raise ValueError(f"no row tile for ({r}, {c})")
    return best


def _tile_2d(r, c):
    if any(r % tr == 0 for tr in range(8, r + 1, 8)):
        return _row_tile(r, c), c
    tc = max(t for t in range(LANES, c + 1, LANES) if c % t == 0 and r * t * 4 <= ELEMENTWISE_BLOCK_BYTES)
    return r, tc


def _core_index():
    return lax.axis_index("c").astype(jnp.int32).reshape(1)


def _chip_index():
    return (2 * lax.axis_index("x") + lax.axis_index("y")).astype(jnp.int32).reshape(1)


def adamw_halves(w, m, v, g_own, g_sib, *, layer=0, prev=None, name):
    n_layers, rows, c = w.shape
    r = rows // 2
    tr = _row_tile(r, c)
    nb = r // tr

    def body(c_ref, w_ref, m_ref, v_ref, own_ref, sib_ref, *rest):
        g_ref, d_ref, nm_ref, nv_ref = rest[-4:]
        gg = jnp.where(pl.program_id(0) == c_ref[0], own_ref[...], sib_ref[...])
        nm = ADAM_B1 * m_ref[...] + (1.0 - ADAM_B1) * gg
        nv = ADAM_B2 * v_ref[...] + (1.0 - ADAM_B2) * jnp.square(gg)
        m_hat = nm / (1.0 - ADAM_B1 ** ADAM_STEP)
        v_hat = nv / (1.0 - ADAM_B2 ** ADAM_STEP)
        g_ref[...] = gg
        d_ref[...] = -ADAM_LR * (m_hat / (jnp.sqrt(v_hat) + ADAM_EPS) + ADAM_WD * w_ref[...])
        nm_ref[...] = nm
        nv_ref[...] = nv

    full = pl.BlockSpec((None, tr, c), lambda h, i, cr: (layer, h * nb + i, 0))
    half = pl.BlockSpec((tr, c), lambda h, i, cr: (i, 0))
    o = _sds((n_layers, rows, c), F32)
    prev = list(prev or ())
    gs = pltpu.PrefetchScalarGridSpec(num_scalar_prefetch=1, grid=(2, nb), in_specs=[full, full, full, half, half] + [_ANY] * len(prev),
                                      out_specs=(full, full, full, full))
    n_fixed = 6
    return pl.pallas_call(body, out_shape=(o, o, o, o), grid_spec=gs, compiler_params=_params(), name=name,
                          input_output_aliases={n_fixed + k: k for k in range(len(prev))})(
        _core_index(), w, m, v, g_own, g_sib, *prev)


ADAMW_ROW_STEPS = 6


def adamw_rows(w, g, m, v, *, name):
    rows, _, cols = w.shape
    tr = rows // ADAMW_ROW_STEPS

    def body(w_ref, g_ref, m_ref, v_ref, g_out_ref, d_ref, nm_ref, nv_ref):
        gg = g_ref[...]
        nm = ADAM_B1 * m_ref[...] + (1.0 - ADAM_B1) * gg
        nv = ADAM_B2 * v_ref[...] + (1.0 - ADAM_B2) * jnp.square(gg)
        m_hat = nm / (1.0 - ADAM_B1 ** ADAM_STEP)
        v_hat = nv / (1.0 - ADAM_B2 ** ADAM_STEP)
        g_out_ref[...] = gg
        d_ref[...] = -ADAM_LR * (m_hat / (jnp.sqrt(v_hat) + ADAM_EPS) + ADAM_WD * w_ref[...])
        nm_ref[...] = nm
        nv_ref[...] = nv

    blk = pl.BlockSpec((tr, 1, cols), lambda i: (i, 0, 0))
    o = _sds(w.shape, F32)
    return pl.pallas_call(body, out_shape=(o, o, o, o), grid=(ADAMW_ROW_STEPS,), in_specs=[blk] * 4, out_specs=(blk, blk, blk, blk),
                          compiler_params=_params(), name=name)(w, g, m, v)


def adamw_many(ws, gs, ms, vs, *, name):
    n = len(ws)

    def body(*refs):
        w_refs, g_refs, m_refs, v_refs, d_refs, nm_refs, nv_refs = (refs[k * n:(k + 1) * n] for k in range(7))
        for i in range(n):
            gg = g_refs[i][...]
            nm = ADAM_B1 * m_refs[i][...] + (1.0 - ADAM_B1) * gg
            nv = ADAM_B2 * v_refs[i][...] + (1.0 - ADAM_B2) * jnp.square(gg)
            m_hat = nm / (1.0 - ADAM_B1 ** ADAM_STEP)
            v_hat = nv / (1.0 - ADAM_B2 ** ADAM_STEP)
            d_refs[i][...] = -ADAM_LR * (m_hat / (jnp.sqrt(v_hat) + ADAM_EPS) + ADAM_WD * w_refs[i][...])
            nm_refs[i][...] = nm
            nv_refs[i][...] = nv

    outs = pl.pallas_call(body, out_shape=[_sds(w.shape, F32) for w in ws] * 3, compiler_params=_params(), name=name)(*ws, *gs, *ms, *vs)
    return outs[:n], outs[n:2 * n], outs[2 * n:]


def add_core_halves(g2, land, *, out_dtype, name):
    _, ns, r, cols = g2.shape
    tr, tc = _tile_2d(r, cols)

    def body(c_ref, a_ref, b_ref, o_ref):
        o_ref[...] = (a_ref[...] + b_ref[...]).astype(out_dtype)

    gs = pltpu.PrefetchScalarGridSpec(
        num_scalar_prefetch=1, grid=(ns, r // tr, cols // tc),
        in_specs=[pl.BlockSpec((None, None, tr, tc), lambda s, i, j, cr: (cr[0], s, i, j)),
                  pl.BlockSpec((None, tr, tc), lambda s, i, j, cr: (s, i, j))],
        out_specs=pl.BlockSpec((None, tr, tc), lambda s, i, j, cr: (s, i, j)))
    return pl.pallas_call(body, out_shape=_sds((ns, r, cols), out_dtype), grid_spec=gs, compiler_params=_params(), name=name)(
        _core_index(), g2, land)


def sum_over_chips(own, land, *, scatter, name):
    _, r, cols = own.shape
    tr, tc = _tile_2d(r, cols)

    def body(mine_ref, own_ref, l0, l1, l2, l3, o_ref):
        mine = mine_ref[0]
        mine_val = own_ref[...]
        acc = None
        for s, l_ref in enumerate((l0, l1, l2, l3)):
            val = jnp.where(mine == s, mine_val, l_ref[...]).astype(F32)
            acc = val if acc is None else acc + val
        o_ref[...] = acc

    def slot(s):
        return pl.BlockSpec((None, tr, tc), lambda i, j, mr: (jnp.where(mr[0] == s, (s + 1) % N_SHARD, s), i, j))

    own_spec = pl.BlockSpec((None, tr, tc), lambda i, j, mr: (mr[0] if scatter else 0, i, j))
    gs = pltpu.PrefetchScalarGridSpec(num_scalar_prefetch=1, grid=(r // tr, cols // tc), in_specs=[own_spec] + [slot(s) for s in range(N_SHARD)],
                                      out_specs=pl.BlockSpec((tr, tc), lambda i, j, mr: (i, j)))
    return pl.pallas_call(body, out_shape=_sds((r, cols), F32), grid_spec=gs, compiler_params=_params(), name=name)(
        _chip_index(), own, land, land, land, land)


_ANY = pl.BlockSpec(memory_space=pl.ANY)


def xy_exchange(src, *, scatter, name):
    rh = src.shape[1]

    def body(src_ref, land_ref, send_sems, recv_sems, loc_sem):
        x, y, c = lax.axis_index("x"), lax.axis_index("y"), lax.axis_index("c")
        mine = 2 * x + y
        peers = [(1 - x, y), (x, 1 - y), (1 - x, 1 - y)]

        def piece(shard):
            return src_ref.at[shard] if scatter else src_ref.at[c]

        def copy(k, px, py, dst_slot):
            return pltpu.make_async_remote_copy(src_ref=piece(2 * px + py), dst_ref=land_ref.at[dst_slot], send_sem=send_sems.at[k],
                                                recv_sem=recv_sems.at[k], device_id=(px, py, c), device_id_type=MESH)

        keep = pltpu.make_async_copy(piece(mine), land_ref.at[mine], loc_sem)
        keep.start()
        sends = [copy(k, px, py, mine) for k, (px, py) in enumerate(peers)]
        for cp in sends:
            cp.start()
        for cp in sends:
            cp.wait_send()
        for k, (px, py) in enumerate(peers):
            copy(k, px, py, 2 * px + py).wait_recv()
        keep.wait()

    return pl.pallas_call(body, out_shape=_sds((N_SHARD, rh, LANES), src.dtype), in_specs=[_ANY], out_specs=_ANY,
                          scratch_shapes=[pltpu.SemaphoreType.DMA((3,)), pltpu.SemaphoreType.DMA((3,)), pltpu.SemaphoreType.DMA(())],
                          name=name)(src)


def core_exchange(src, *, send_other_half, name):
    def body(src_ref, out_ref, send_sem, recv_sem, loc_sem):
        x, y, c = lax.axis_index("x"), lax.axis_index("y"), lax.axis_index("c")
        if send_other_half:
            cp = pltpu.make_async_remote_copy(src_ref=src_ref.at[1 - c], dst_ref=out_ref, send_sem=send_sem, recv_sem=recv_sem,
                                              device_id=(x, y, 1 - c), device_id_type=MESH)
            cp.start()
            cp.wait_send()
            cp.wait_recv()
        else:
            keep = pltpu.make_async_copy(src_ref, out_ref.at[c], loc_sem)
            keep.start()
            cp = pltpu.make_async_remote_copy(src_ref=src_ref, dst_ref=out_ref.at[c], send_sem=send_sem, recv_sem=recv_sem,
                                              device_id=(x, y, 1 - c), device_id_type=MESH)
            cp.start()
            cp.wait_send()
            pltpu.make_async_remote_copy(src_ref=src_ref, dst_ref=out_ref.at[1 - c], send_sem=send_sem, recv_sem=recv_sem,
                                         device_id=(x, y, 1 - c), device_id_type=MESH).wait_recv()
            keep.wait()

    out_shape = _sds(src.shape[1:], src.dtype) if send_other_half else _sds((2,) + src.shape, src.dtype)
    return pl.pallas_call(body, out_shape=out_shape, in_specs=[_ANY], out_specs=_ANY,
                          scratch_shapes=[pltpu.SemaphoreType.DMA(()), pltpu.SemaphoreType.DMA(()), pltpu.SemaphoreType.DMA(())],
                          name=name)(src)


def _comm_call(body, ins, out_shapes, sem_counts, name):
    return pl.pallas_call(body, out_shape=tuple(out_shapes), in_specs=[_ANY] * len(ins), out_specs=tuple([_ANY] * len(out_shapes)),
                          scratch_shapes=[pltpu.SemaphoreType.DMA((k,)) for k in sem_counts], name=name)(*ins)


def _sequencer_call(body, ins, out_shapes, sem_counts, name, collective_id):
    return pl.kernel(body, out_type=list(out_shapes), mesh=plsc.ScalarSubcoreMesh(axis_name="sequencer", num_cores=1), name=name,
                     scratch_types=[pltpu.SemaphoreType.DMA((k,)) for k in sem_counts],
                     compiler_params=pltpu.CompilerParams(collective_id=collective_id))(*ins)


def _handshake(peers):
    barrier = pltpu.get_barrier_semaphore()
    for peer in peers:
        pl.semaphore_signal(barrier, inc=1, device_id=peer, device_id_type=MESH)
    pl.semaphore_wait(barrier, len(peers))


def _xy_peers(x, y):
    return [(1 - x, y), (x, 1 - y), (1 - x, 1 - y)]


def gather_halves(halves, *, name, collective_id):
    n = len(halves)

    def body(*refs):
        ins, lands, sibs = refs[:n], refs[n:2 * n], refs[2 * n:3 * n]
        ici_send, ici_recv, d2d_send, d2d_recv = refs[3 * n:]
        x, y, c = lax.axis_index("x"), lax.axis_index("y"), lax.axis_index("c")
        mine = 2 * x + y
        peers = _xy_peers(x, y)
        _handshake([(px, py, c) for px, py in peers] + [(x, y, 1 - c)])

        def ici(i, k, slot):
            px, py = peers[k]
            return pltpu.make_async_remote_copy(src_ref=ins[i].at[c], dst_ref=lands[i].at[slot], send_sem=ici_send.at[3 * i + k],
                                                recv_sem=ici_recv.at[3 * i + k], device_id=(px, py, c), device_id_type=MESH)

        def pass_on(i, k):
            px, py = peers[k]
            slot = 2 * px + py
            return pltpu.make_async_remote_copy(src_ref=lands[i].at[slot], dst_ref=sibs[i].at[slot], send_sem=d2d_send.at[3 * i + k],
                                                recv_sem=d2d_recv.at[3 * i + k], device_id=(x, y, 1 - c), device_id_type=MESH)

        sends = [ici(i, k, mine) for i in range(n) for k in range(3)]
        for cp in sends:
            cp.start()
        passed = []
        for i in range(n):
            for k in range(3):
                px, py = peers[k]
                ici(i, k, 2 * px + py).wait_recv()
                cp = pass_on(i, k)
                cp.start()
                passed.append(cp)
        for cp in passed:
            cp.wait_recv()
        for cp in sends + passed:
            cp.wait_send()

    outs = [_sds((N_SHARD,) + h.shape[1:], h.dtype) for h in halves]
    res = _sequencer_call(body, halves, outs + outs, [3 * n] * 4, name, collective_id)
    return res[:n], res[n:]


def send_other_half(arrays, *, name, collective_id):
    n = len(arrays)

    def body(*refs):
        ins, lands = refs[:n], refs[n:2 * n]
        send_sems, recv_sems = refs[2 * n:]
        x, y, c = lax.axis_index("x"), lax.axis_index("y"), lax.axis_index("c")
        _handshake([(x, y, 1 - c)])
        copies = [pltpu.make_async_remote_copy(src_ref=ins[i].at[1 - c], dst_ref=lands[i], send_sem=send_sems.at[i],
                                               recv_sem=recv_sems.at[i], device_id=(x, y, 1 - c), device_id_type=MESH) for i in range(n)]
        for cp in copies:
            cp.start()
        for cp in copies:
            cp.wait_recv()
        for cp in copies:
            cp.wait_send()

    return _sequencer_call(body, arrays, [_sds(a.shape[1:], a.dtype) for a in arrays], [n, n], name, collective_id)


_HBM = pl.BlockSpec(memory_space=pltpu.HBM)
_SEM = pl.BlockSpec(memory_space=pltpu.SEMAPHORE)
_SPLIT_COPY = dict(has_side_effects=pltpu.SideEffectType.DATAFLOW_SIDE_EFFECTING)


def _chip_copy(ins, lands, send_sems, recv_sems, scatter, i, k, receive):
    x, y, c = lax.axis_index("x"), lax.axis_index("y"), lax.axis_index("c")
    px, py = _xy_peers(x, y)[k]
    theirs, mine = 2 * px + py, 2 * x + y
    src = ins[i].at[theirs] if scatter[i] else ins[i].at[0]
    return pltpu.make_async_remote_copy(src_ref=src, dst_ref=lands[i].at[theirs if receive else mine], send_sem=send_sems.at[3 * i + k],
                                        recv_sem=recv_sems.at[3 * i + k], device_id=(px, py, c), device_id_type=MESH)


def send_to_chips_start(arrays, scatter, *, name):
    n = len(arrays)

    def body(*refs):
        send_sems, recv_sems = refs[2 * n], refs[2 * n + 1]
        ins, lands = refs[2 * n + 2:3 * n + 2], refs[3 * n + 2:4 * n + 2]
        token = refs[4 * n + 2]
        for i in range(n):
            for k in range(3):
                _chip_copy(ins, lands, send_sems, recv_sems, scatter, i, k, receive=False).start()
        token[...] = jnp.zeros_like(token)

    land_shapes = [(N_SHARD,) + a.shape[1:] for a in arrays]
    operands = [pltpu.with_memory_space_constraint(a, pltpu.HBM) for a in arrays]
    operands += [pltpu.with_memory_space_constraint(lax.empty(s, a.dtype), pltpu.HBM) for s, a in zip(land_shapes, arrays)]
    out_shape = ([pltpu.SemaphoreType.DMA((3 * n,)), pltpu.SemaphoreType.DMA((3 * n,))] + [pltpu.HBM(a.shape, a.dtype) for a in arrays]
                 + [pltpu.HBM(s, a.dtype) for s, a in zip(land_shapes, arrays)] + [_sds((8, LANES), F32)])
    res = pl.pallas_call(body, name=name, out_shape=out_shape, in_specs=[_HBM] * (2 * n),
                         out_specs=[_SEM, _SEM] + [_HBM] * (2 * n) + [pl.BlockSpec(memory_space=pltpu.VMEM)],
                         input_output_aliases={i: 2 + i for i in range(2 * n)}, compiler_params=pltpu.CompilerParams(**_SPLIT_COPY))(*operands)
    return (res[0], res[1], res[2:2 + n], res[2 + n:2 + 2 * n], scatter), res[-1]


def send_to_chips_wait(state, after, *, name):
    send_sems, recv_sems, arrays, lands, scatter = state
    n = len(arrays)

    def body(*refs):
        ins, landing = refs[:n], refs[n:2 * n]
        send_sems, recv_sems = refs[2 * n], refs[2 * n + 1]
        for i in range(n):
            for k in range(3):
                _chip_copy(ins, landing, send_sems, recv_sems, scatter, i, k, receive=True).wait_recv()
        for i in range(n):
            for k in range(3):
                _chip_copy(ins, landing, send_sems, recv_sems, scatter, i, k, receive=False).wait_send()

    out_shape = [pltpu.HBM(a.shape, a.dtype) for a in list(arrays) + list(lands)]
    res = pl.pallas_call(body, name=name, out_shape=out_shape, in_specs=[_HBM] * (2 * n) + [_SEM, _SEM] + [_ANY] * len(after),
                         out_specs=[_HBM] * (2 * n), input_output_aliases={i: i for i in range(2 * n)},
                         compiler_params=pltpu.CompilerParams(**_SPLIT_COPY))(*arrays, *lands, send_sems, recv_sems, *after)
    return res[:n], res[n:]


def swap_with_other_core(arrays, *, name, collective_id):
    n = len(arrays)

    def body(*refs):
        ins, lands = refs[:n], refs[n:2 * n]
        send_sems, recv_sems = refs[2 * n:]
        x, y, c = lax.axis_index("x"), lax.axis_index("y"), lax.axis_index("c")
        _handshake([(x, y, 1 - c)])
        copies = [pltpu.make_async_remote_copy(src_ref=ins[i], dst_ref=lands[i], send_sem=send_sems.at[i], recv_sem=recv_sems.at[i],
                                               device_id=(x, y, 1 - c), device_id_type=MESH) for i in range(n)]
        for cp in copies:
            cp.start()
        for cp in copies:
            cp.wait_recv()
        for cp in copies:
            cp.wait_send()

    return _sequencer_call(body, arrays, [_sds(a.shape, a.dtype) for a in arrays], [n, n], name, collective_id)


def _pack_rows(n_elems, row_multiple):
    rows = -(-n_elems // LANES)
    return -(-rows // row_multiple) * row_multiple


def _pack(arrays, rows, dtype):
    flat = jnp.concatenate([a.reshape(-1).astype(dtype) for a in arrays])
    return jnp.pad(flat, (0, rows * LANES - flat.shape[0])).reshape(rows, LANES)


def _unpack(packed, shapes):
    flat = packed.reshape(-1)
    out, off = [], 0
    for s in shapes:
        n = int(np.prod(s))
        out.append(flat[off:off + n].reshape(s))
        off += n
    return out


def all_gather_shards(shards, axes, dtype, row_multiple, tag):
    shapes = [s.shape for s in shards]
    rows = _pack_rows(sum(int(np.prod(s)) for s in shapes), row_multiple)
    packed = _pack(shards, rows, dtype).reshape(2, rows // 2, LANES)
    land = xy_exchange(packed, scatter=False, name=f"gather_xy_{tag}")
    both = core_exchange(land, send_other_half=False, name=f"gather_c_{tag}")
    per_shard = jnp.swapaxes(both, 0, 1).reshape(N_SHARD, rows, LANES)
    pieces = [_unpack(per_shard[s], shapes) for s in range(N_SHARD)]
    return [jnp.concatenate([pieces[s][i] for s in range(N_SHARD)], axis=ax) for i, ax in enumerate(axes)]


def _ordered_before(first, then):
    if then is None:
        return first, None
    return lax.optimization_barrier((first, then))


def reduce_between_cores(arrays, scatter, *, tag, collective_id, before=None):
    arrays, before = _ordered_before(arrays, before)
    land = send_other_half(arrays, name=f"reduce_core_send_{tag}", collective_id=collective_id)
    return (arrays, land, scatter, tag, collective_id), before


def reduce_between_chips(state, before=None):
    arrays, land, scatter, tag, collective_id = state
    chip = [add_core_halves(a, l, out_dtype=BF16 if sc else F32, name=f"reduce_core_add_{tag}_{i}")
            for i, (a, l, sc) in enumerate(zip(arrays, land, scatter))]
    sending, token = send_to_chips_start(chip, scatter, name=f"reduce_chip_start_{tag}")
    token, before = _ordered_before(token, before)
    return (sending, token, scatter, tag, collective_id), before


def reduce_finish(state, after):
    sending, token, scatter, tag, collective_id = state
    chip, land = send_to_chips_wait(sending, tuple(after) + (token,), name=f"reduce_chip_wait_{tag}")
    own = [sum_over_chips(ch, l, scatter=sc, name=f"reduce_chip_add_{tag}_{i}") for i, (ch, l, sc) in enumerate(zip(chip, land, scatter))]
    sib = swap_with_other_core(own, name=f"reduce_core_swap_{tag}", collective_id=collective_id + 2)
    return own, sib


def _ffn_layer_fwd(h, norm_g, w_up, cw, cb, w_down, tag):
    u, hn = norm_matmul(h, norm_g, w_up, name=f"ffn_up_{tag}")
    act = ffn_act_fwd(u, cw, cb, name=f"ffn_act_{tag}")
    out = matmul(act, w_down, add=h, name=f"ffn_down_{tag}")
    return out, (h, hn, u, act)


def _travel_layout(array):
    return BIG_ARRAYS[array][3], BIG_ARRAYS[array][4]


def _ffn_layer_bwd(saved, dout, norm_g, w_up, cw, cb, w_down, tag, d_w_down_other=None):
    h, hn, u, act = saved
    dact = matmul(dout, w_down, tb=True, name=f"ffn_down_dx_{tag}")
    d_w_down = matmul(act, dout, ta=True, layer=(int(tag), 2, d_w_down_other), name=f"ffn_down_dw_{tag}")
    du, dcw, dcb = ffn_act_bwd(u, cw, cb, dact, name=f"ffn_act_bwd_{tag}")
    dhn = matmul(du, w_up, tb=True, name=f"ffn_up_dx_{tag}")
    d_w_up = matmul(hn, du, ta=True, split=_travel_layout(f"ffn_w_up_{tag}"), name=f"ffn_up_dw_{tag}")
    dh, dg = norm_bwd(h, norm_g, dhn, dout, name=f"ffn_norm_bwd_{tag}")
    return dh, dg, d_w_up, dcw, dcb, d_w_down


def local_step(x, target, w, stage=lambda name, tensors, grads=None: tensors):
    g = {}
    tables = _ret_tables()
    x = stage("start", x)
    w_in_t = w["ret_gdn_w_in"]
    w_main = w_in_t[:MIX_MAIN]
    w_small = jnp.pad(w_in_t[MIX_MAIN:], ((0, LANES - 2 * N_HEADS), (0, 0)))
    a_log = jnp.pad(w["gdn_a_log"], ((0, 0), (0, LANES - N_HEADS)))
    dt_bias = jnp.pad(w["gdn_dt_bias"], ((0, 0), (0, LANES - N_HEADS)))

    p, hn0 = norm_matmul(x, w["norm_mix"][0:1], w_main, tb=True, name="mix0_in")
    hn0 = stage("normed", hn0)
    small = matmul(hn0, w_small, tb=True, name="mix0_in_small")
    y_ret, s_ret = ret_fwd(p, tables, name="ret_fwd")
    conv = gdn_conv_fwd(p, w["gdn_conv_w"], name="gdn_conv")
    y_gdn, s_gdn = gdn_fwd(conv, p, small, a_log, dt_bias, w["gdn_out_gain"], name="gdn_fwd")
    y0 = stage("mixed", jnp.concatenate([y_ret, y_gdn], axis=1))
    h1 = matmul(y0, w["ret_gdn_w_out"], add=x, name="mix0_out")
    h2, ffn0 = _ffn_layer_fwd(h1, w["norm_ffn"][0:1], w["ffn_w_up"][0], w["ffn_conv_w"][0], w["ffn_conv_b"][0:1], w["ffn_w_down"][0], "0")
    h2 = stage("layer0", h2)

    gx, hn1 = norm_matmul(h2, w["norm_mix"][1:2], w["lru_w_in"], name="mix1_in")
    lru_p = (w["lru_conv_w"], w["lru_conv_b"], w["lru_w_a"], w["lru_b_a"], w["lru_w_x"], w["lru_b_x"], w["lru_lambda"])
    y1 = lru_fwd(gx, *lru_p, name="lru_fwd")
    h3 = matmul(y1, w["lru_w_out"], add=h2, name="mix1_out")
    h4, ffn1 = _ffn_layer_fwd(h3, w["norm_ffn"][1:2], w["ffn_w_up"][1], w["ffn_conv_w"][1], w["ffn_conv_b"][1:2], w["ffn_w_down"][1], "1")

    loss, dh4, g["norm_final"] = final_fwd_bwd(h4, w["norm_final"], target, name="final")

    dh3, dgf1, dwu1, dcw1, dcb1, dwd1 = _ffn_layer_bwd(ffn1, dh4, w["norm_ffn"][1:2], w["ffn_w_up"][1], w["ffn_conv_w"][1],
                                                     w["ffn_conv_b"][1:2], w["ffn_w_down"][1], "1")
    g["ffn_w_up_1"] = dwu1
    dh3 = stage("grads0_ready", dh3, g)
    dy1 = matmul(dh3, w["lru_w_out"], tb=True, name="mix1_out_dx")
    g["lru_w_out"] = matmul(y1, dh3, ta=True, split=_travel_layout("lru_w_out"), name="mix1_out_dw")
    dgx, g["lru_conv_w"], g["lru_conv_b"], g["lru_w_a"], g["lru_b_a"], g["lru_w_x"], g["lru_b_x"], g["lru_lambda"] = lru_bwd(
        gx, *lru_p, dy1, name="lru_bwd")
    dgx = stage("grads0_send", dgx, g)
    dhn1 = matmul(dgx, w["lru_w_in"], tb=True, name="mix1_in_dx")
    g["lru_w_in"] = matmul(hn1, dgx, ta=True, split=_travel_layout("lru_w_in"), name="mix1_in_dw")
    dh2, dgm1 = norm_bwd(h2, w["norm_mix"][1:2], dhn1, dh3, name="mix1_norm_bwd")
    dh2 = stage("grads1_ready", dh2, g)

    dh1, dgf0, dwu0, dcw0, dcb0, dwd0 = _ffn_layer_bwd(ffn0, dh2, w["norm_ffn"][0:1], w["ffn_w_up"][0], w["ffn_conv_w"][0],
                                                     w["ffn_conv_b"][0:1], w["ffn_w_down"][0], "0", dwd1)
    g["ffn_w_up_0"] = dwu0
    g["ffn_w_down"] = dwd0
    dh1 = stage("grads2_ready", stage("grads1_send", dh1, g), g)
    dy0 = matmul(dh1, w["ret_gdn_w_out"], tb=True, name="mix0_out_dx")
    g["ret_gdn_w_out"] = matmul(y0, dh1, ta=True, split=_travel_layout("ret_gdn_w_out"), name="mix0_out_dw")
    dq_r, dk_r, dv_r, dg_r = ret_bwd(p, tables, s_ret, dy0, name="ret_bwd")
    dy0, dq_r = stage("grads2_send", (dy0, dq_r), g)
    dcq, dck, dcv, dg_d, dsmall, dal, ddt, dgain = gdn_bwd(conv, p, small, a_log, dt_bias, w["gdn_out_gain"], s_gdn, dy0, name="gdn_bwd")
    dconv = jnp.concatenate([dcq, dck, dcv], axis=1)
    dp_conv, g["gdn_conv_w"] = gdn_conv_bwd(p, w["gdn_conv_w"], dconv, name="gdn_conv_bwd")
    dp = jnp.concatenate([dq_r, dk_r, dv_r, dg_r, dp_conv, dg_d], axis=1)
    dhn0 = matmul(dp, w_main, name="mix0_in_dx")
    dhn0 = matmul(dsmall, w_small, add=dhn0, name="mix0_in_small_dx")
    d_w_in = matmul(dp, hn0, ta=True, column_halves=(MIX_IN, 0, None, None), name="mix0_in_dw")
    d_w_in = matmul(dsmall, hn0, ta=True, column_halves=(MIX_IN, MIX_MAIN, 2 * N_HEADS, d_w_in), name="mix0_in_small_dw")
    g["ret_gdn_w_in"] = d_w_in.reshape(2, N_SHARD, MIX_IN // N_SHARD, D_MODEL // 2)
    dx, dgm0 = norm_bwd(x, w["norm_mix"][0:1], dhn0, dh1, name="mix0_norm_bwd")

    g["gdn_a_log"] = dal[:, :N_HEADS]
    g["gdn_dt_bias"] = ddt[:, :N_HEADS]
    g["gdn_out_gain"] = dgain
    g["norm_mix"] = jnp.concatenate([dgm0, dgm1], axis=0)
    g["norm_ffn"] = jnp.concatenate([dgf0, dgf1], axis=0)
    g["ffn_conv_w"] = jnp.stack([dcw0, dcw1])
    g["ffn_conv_b"] = jnp.concatenate([dcb0, dcb1], axis=0)
    return loss, dx, g


WEIGHTS = ("norm_mix", "norm_ffn", "ret_gdn_w_in", "gdn_conv_w", "gdn_a_log", "gdn_dt_bias", "gdn_out_gain", "ret_gdn_w_out",
           "lru_w_in", "lru_conv_w", "lru_conv_b", "lru_w_a", "lru_b_a", "lru_w_x", "lru_b_x", "lru_lambda", "lru_w_out",
           "ffn_w_up", "ffn_conv_w", "ffn_conv_b", "ffn_w_down", "norm_final")
MATMUL_SHARDED = {"ret_gdn_w_in": 1, "ret_gdn_w_out": 0, "lru_w_in": 1, "lru_w_out": 0, "ffn_w_up": 2, "ffn_w_down": 1}
VECTOR_SHARDED = {"gdn_conv_w": 1, "lru_conv_w": 1, "lru_conv_b": 1, "lru_b_a": 1, "lru_b_x": 1, "lru_lambda": 1, "ffn_conv_w": 2}
SHARDED = {**MATMUL_SHARDED, **VECTOR_SHARDED}
REPLICATED = tuple(n for n in WEIGHTS if n not in SHARDED)
SQUEEZE = {"ret_gdn_w_in", "gdn_conv_w", "ret_gdn_w_out", "lru_w_in", "lru_conv_w", "lru_w_a", "lru_w_x", "lru_w_out"}
MIX_IN = MIX_MAIN + 2 * N_HEADS
BIG_ARRAYS = {
    "ret_gdn_w_in": ("ret_gdn_w_in", None, (MIX_IN, D_MODEL), (N_SHARD, MIX_IN // N_SHARD, 2, D_MODEL // 2), (2, 0, 1, 3)),
    "ret_gdn_w_out": ("ret_gdn_w_out", None, (2 * GROUP, D_MODEL), (N_SHARD, 2, GROUP // N_SHARD, D_MODEL), (1, 0, 2, 3)),
    "lru_w_in": ("lru_w_in", None, (D_MODEL, 2 * D_MODEL), (2, D_MODEL // 2, N_SHARD, 2 * D_MODEL // N_SHARD), (0, 2, 1, 3)),
    "lru_w_out": ("lru_w_out", None, (D_MODEL, D_MODEL), (N_SHARD, 2, D_MODEL // (2 * N_SHARD), D_MODEL), (1, 0, 2, 3)),
    "ffn_w_up_0": ("ffn_w_up", 0, (D_MODEL, 2 * D_FF), (2, D_MODEL // 2, N_SHARD, 2 * D_FF // N_SHARD), (0, 2, 1, 3)),
    "ffn_w_up_1": ("ffn_w_up", 1, (D_MODEL, 2 * D_FF), (2, D_MODEL // 2, N_SHARD, 2 * D_FF // N_SHARD), (0, 2, 1, 3)),
    "ffn_w_down": ("ffn_w_down", None, (2, D_FF, D_MODEL), (2, N_SHARD, D_FF // N_SHARD, D_MODEL), (0, 1, 2, 3)),
}
GATHER_GROUPS = (("ret_gdn_w_in",), ("ret_gdn_w_out", "ffn_w_up_0", "ffn_w_down"), ("lru_w_in", "lru_w_out", "ffn_w_up_1"))
REDUCE_GROUPS = (("ffn_w_up_1",), ("lru_w_in", "lru_w_out"), ("ffn_w_up_0", "ffn_w_down"), ("ret_gdn_w_out", "ret_gdn_w_in"))
BLOCK_WEIGHTS = ("lru_w_a", "lru_w_x")
GATHER_COLLECTIVE_ID = 1
REDUCE_COLLECTIVE_ID = GATHER_COLLECTIVE_ID + len(GATHER_GROUPS)


TRANSPOSED = ("ret_gdn_w_in",)


def _shard_of(array, tensors):
    weight, layer = BIG_ARRAYS[array][:2]
    t = tensors[weight]
    if weight in TRANSPOSED:
        return jnp.swapaxes(t, 1, 2)[0]
    return _local_view(weight, t) if layer is None else t[layer]


def _core_halves(array, shard):
    _, _, _, split, perm = BIG_ARRAYS[array]
    kept = [k for k in range(4) if k != perm[1]]
    order = [kept.index(perm[0]), kept.index(perm[2]), kept.index(perm[3])]
    return shard.reshape([split[k] for k in kept]).transpose(order)


def _local_view(name, a):
    if name in SQUEEZE:
        return a[0]
    if a.ndim == 1:
        return a[None, :]
    return a


def kernel(x, norm_mix, norm_ffn, ret_gdn_w_in, gdn_conv_w, gdn_a_log, gdn_dt_bias, gdn_out_gain, ret_gdn_w_out, lru_w_in, lru_conv_w, lru_conv_b, lru_w_a, lru_b_a, lru_w_x, lru_b_x, lru_lambda, lru_w_out, ffn_w_up, ffn_conv_w, ffn_conv_b, ffn_w_down, norm_final, loss_target, m_norm_mix, m_norm_ffn, m_ret_gdn_w_in, m_gdn_conv_w, m_gdn_a_log, m_gdn_dt_bias, m_gdn_out_gain, m_ret_gdn_w_out, m_lru_w_in, m_lru_conv_w, m_lru_conv_b, m_lru_w_a, m_lru_b_a, m_lru_w_x, m_lru_b_x, m_lru_lambda, m_lru_w_out, m_ffn_w_up, m_ffn_conv_w, m_ffn_conv_b, m_ffn_w_down, m_norm_final, v_norm_mix, v_norm_ffn, v_ret_gdn_w_in, v_gdn_conv_w, v_gdn_a_log, v_gdn_dt_bias, v_gdn_out_gain, v_ret_gdn_w_out, v_lru_w_in, v_lru_conv_w, v_lru_conv_b, v_lru_w_a, v_lru_b_a, v_lru_w_x, v_lru_b_x, v_lru_lambda, v_lru_w_out, v_ffn_w_up, v_ffn_conv_w, v_ffn_conv_b, v_ffn_w_down, v_norm_final):
    given = dict(norm_mix=norm_mix, norm_ffn=norm_ffn, ret_gdn_w_in=ret_gdn_w_in, gdn_conv_w=gdn_conv_w, gdn_a_log=gdn_a_log, gdn_dt_bias=gdn_dt_bias, gdn_out_gain=gdn_out_gain, ret_gdn_w_out=ret_gdn_w_out, lru_w_in=lru_w_in, lru_conv_w=lru_conv_w, lru_conv_b=lru_conv_b, lru_w_a=lru_w_a, lru_b_a=lru_b_a, lru_w_x=lru_w_x, lru_b_x=lru_b_x, lru_lambda=lru_lambda, lru_w_out=lru_w_out, ffn_w_up=ffn_w_up, ffn_conv_w=ffn_conv_w, ffn_conv_b=ffn_conv_b, ffn_w_down=ffn_w_down, norm_final=norm_final)
    mom1 = dict(norm_mix=m_norm_mix, norm_ffn=m_norm_ffn, ret_gdn_w_in=m_ret_gdn_w_in, gdn_conv_w=m_gdn_conv_w, gdn_a_log=m_gdn_a_log, gdn_dt_bias=m_gdn_dt_bias, gdn_out_gain=m_gdn_out_gain, ret_gdn_w_out=m_ret_gdn_w_out, lru_w_in=m_lru_w_in, lru_conv_w=m_lru_conv_w, lru_conv_b=m_lru_conv_b, lru_w_a=m_lru_w_a, lru_b_a=m_lru_b_a, lru_w_x=m_lru_w_x, lru_b_x=m_lru_b_x, lru_lambda=m_lru_lambda, lru_w_out=m_lru_w_out, ffn_w_up=m_ffn_w_up, ffn_conv_w=m_ffn_conv_w, ffn_conv_b=m_ffn_conv_b, ffn_w_down=m_ffn_w_down, norm_final=m_norm_final)
    mom2 = dict(norm_mix=v_norm_mix, norm_ffn=v_norm_ffn, ret_gdn_w_in=v_ret_gdn_w_in, gdn_conv_w=v_gdn_conv_w, gdn_a_log=v_gdn_a_log, gdn_dt_bias=v_gdn_dt_bias, gdn_out_gain=v_gdn_out_gain, ret_gdn_w_out=v_ret_gdn_w_out, lru_w_in=v_lru_w_in, lru_conv_w=v_lru_conv_w, lru_conv_b=v_lru_conv_b, lru_w_a=v_lru_w_a, lru_b_a=v_lru_b_a, lru_w_x=v_lru_w_x, lru_b_x=v_lru_b_x, lru_lambda=v_lru_lambda, lru_w_out=v_lru_w_out, ffn_w_up=v_ffn_w_up, ffn_conv_w=v_ffn_conv_w, ffn_conv_b=v_ffn_conv_b, ffn_w_down=v_ffn_w_down, norm_final=v_norm_final)

    local = {n: _local_view(n, a) for n, a in given.items()}

    core = lax.axis_index("c")
    chip = 2 * lax.axis_index("x") + lax.axis_index("y")
    is_my_chip = lax.broadcasted_iota(jnp.int32, (N_SHARD, 1, 1), 0) == chip

    def by_core(mine, other):
        return jnp.where(core == 0, jnp.stack([mine, other]), jnp.stack([other, mine]))

    vec_names, rp_names = list(VECTOR_SHARDED), list(REPLICATED)
    full = dict(zip(vec_names, all_gather_shards([local[n] for n in vec_names], [SHARDED[n] for n in vec_names], F32, 32, "p")))
    for n in rp_names:
        full[n] = local[n]
    in_flight = {}

    bf16_halves = {}

    def cast_halves(gi):
        if gi not in bf16_halves:
            bf16_halves[gi] = [_core_halves(a, _shard_of(a, given).astype(BF16)) for a in GATHER_GROUPS[gi]]
        return bf16_halves[gi]

    def launch(gi, after=None):
        halves = cast_halves(gi)
        if after is not None:
            halves, after = lax.optimization_barrier((halves, after))
        in_flight[gi] = (halves,) + gather_halves(halves, name=f"gather_weights_{gi}", collective_id=GATHER_COLLECTIVE_ID + gi)
        return after

    def land(gi, after):
        halves, lands, sibs = in_flight[gi]
        (lands, sibs), after = lax.optimization_barrier(((lands, sibs), after))
        for a, mine, got, passed in zip(GATHER_GROUPS[gi], halves, lands, sibs):
            weight, layer, full_shape, split, perm = BIG_ARRAYS[a]
            half_mine = jnp.where(is_my_chip, jnp.where(core == 0, mine[0], mine[1])[None], got)
            half_other = jnp.where(is_my_chip, jnp.where(core == 0, mine[1], mine[0])[None], passed)
            value = by_core(half_mine, half_other).transpose(tuple(np.argsort(perm))).reshape(full_shape)
            if layer is None:
                full[weight] = value
            else:
                full.setdefault(weight, [None, None])[layer] = value
        return after

    reducing = {}

    def reduce_ready(gi, grads, then=None, extra=()):
        def travelling(a):
            split, perm = _travel_layout(a)
            return grads[a] if grads[a].ndim == 4 else grads[a].reshape(split).transpose(perm)

        arrays = [travelling(a) for a in REDUCE_GROUPS[gi]] + list(extra)
        scatter = [True] * len(REDUCE_GROUPS[gi]) + [False] * len(extra)
        reducing[gi], then = reduce_between_cores(arrays, scatter, tag=str(gi), collective_id=REDUCE_COLLECTIVE_ID + 3 * gi, before=then)
        return then

    def reduce_send(gi, then=None):
        reducing[gi], then = reduce_between_chips(reducing[gi], before=then)
        return then

    def stage(name, tensors, grads=None):
        if name == "start":
            launch(0)
            launch(1)
            fillers = (cast_halves(2), [full[n] for n in vec_names])
            (bf16_halves[2], gathered_small), tensors = lax.optimization_barrier((fillers, tensors))
            full.update(zip(vec_names, gathered_small))
            return land(0, tensors)
        if name == "normed":
            return launch(2, tensors)
        if name in ("mixed", "layer0"):
            return land({"mixed": 1, "layer0": 2}[name], tensors)
        gi = int(name[len("grads")])
        return reduce_ready(gi, grads, tensors) if name.endswith("_ready") else reduce_send(gi, tensors)

    small_names = [n for n in rp_names if n not in BLOCK_WEIGHTS] + vec_names

    loss_part, dx, grads = local_step(x[0], loss_target[0], full, stage)
    small_shapes = [grads[n].shape for n in small_names] + [(1, 1)]
    small_rows = _pack_rows(sum(int(np.prod(s)) for s in small_shapes), 16)
    small = _pack([grads[n] for n in small_names] + [loss_part[:, :1]], small_rows, F32).reshape(2, 1, small_rows // 2, LANES)
    last = len(REDUCE_GROUPS) - 1
    halves_of_blocks = [grads[n].reshape(2, 1, LRU_BLOCKS * HEAD // 2, HEAD) for n in BLOCK_WEIGHTS]
    reduce_ready(last, grads, extra=[small] + halves_of_blocks)
    reduce_send(last)
    reduced, result = {}, {}

    def finish(gi, after):
        g_own, g_sib = reduce_finish(reducing[gi], after)
        reduced.update(zip(list(REDUCE_GROUPS[gi]) + ["small"] + list(BLOCK_WEIGHTS), zip(g_own, g_sib)))

    def update(n):
        if n in TRANSPOSED:
            n_rows, n_cols = given[n].shape[2], given[n].shape[1]

            def rows(t):
                return jnp.swapaxes(t, 1, 2).reshape(n_rows, 1, n_cols)

            def back(t):
                return jnp.swapaxes(t.reshape(1, n_rows, n_cols), 1, 2)

            g_rows = jnp.swapaxes(by_core(*reduced[n]), 0, 1).reshape(n_rows, 1, n_cols)
            result[n] = tuple(back(t) for t in adamw_rows(rows(given[n]), g_rows, rows(mom1[n]), rows(mom2[n]), name=f"adamw_{n}"))
            return
        done = None
        for a in (k for k, spec in BIG_ARRAYS.items() if spec[0] == n):
            r, cols = reduced[a][0].shape
            layer = BIG_ARRAYS[a][1] or 0
            w3, m3, v3 = (t if BIG_ARRAYS[a][1] is not None else t.reshape(1, 2 * r, cols) for t in (given[n], mom1[n], mom2[n]))
            done = adamw_halves(w3, m3, v3, *reduced[a], layer=layer, prev=done, name=f"adamw_{a}")
        result[n] = done

    for gi in range(last):
        finish(gi, (dx, reducing[last][1]))
    late = {BIG_ARRAYS[a][0] for a in REDUCE_GROUPS[last]}
    for n in MATMUL_SHARDED:
        if n not in late:
            update(n)
    finish(last, tuple(result[n][0] for n in MATMUL_SHARDED if n not in late))
    for n in MATMUL_SHARDED:
        if n in late:
            update(n)

    for n in BLOCK_WEIGHTS:
        w3, m3, v3 = (t.reshape(1, LRU_BLOCKS * HEAD, HEAD) for t in (given[n], mom1[n], mom2[n]))
        result[n] = adamw_halves(w3, m3, v3, *reduced[n], name=f"adamw_{n}")

    *small_sums, loss_sum = _unpack(by_core(*reduced["small"]).reshape(small_rows, LANES), small_shapes)
    loss = loss_sum[0, 0]
    g_small = dict(zip(small_names, small_sums))
    for n in vec_names:
        size = local[n].shape[SHARDED[n]]
        g_small[n] = lax.dynamic_slice_in_dim(g_small[n], chip * size, size, axis=SHARDED[n])
    views = [[_local_view(n, src[n]) for n in small_names] for src in (given, mom1, mom2)]
    d_s, m_s, v_s = adamw_many(views[0], [g_small[n] for n in small_names], views[1], views[2], name="adamw_small")
    for n, d, nm, nv in zip(small_names, d_s, m_s, v_s):
        result[n] = (g_small[n], d, nm, nv)

    outs = [[result[n][k].reshape(given[n].shape) for n in WEIGHTS] for k in range(4)]
    return (loss, dx[None], *outs[0], *outs[1], *outs[2], *outs[3])
```

```python
import functools

import numpy as np
import jax
import jax.numpy as jnp
from jax import lax
from jax.experimental import pallas as pl
from jax.experimental.pallas import tpu as pltpu
from jax.experimental.pallas import tpu_sc as plsc

F32 = jnp.float32
BF16 = jnp.bfloat16
HI = lax.Precision.HIGHEST
MESH = pl.DeviceIdType.MESH

SEQ = 2048
D_MODEL = 1024
N_HEADS = 4
HEAD = 128
RET_CHUNK = 128
RET_CHUNKS_PER_STEP = 2
GDN_CHUNK = 64
GDN_CHUNKS_PER_STEP = 8
GROUP = N_HEADS * HEAD
MIX_MAIN = 8 * GROUP
D_FF = 2816
LRU_BLOCKS = 8
LRU_C = 8.0
ROPE_BASE = 10000.0
EPS = 1e-6
N_SHARD = 4
LANES = 128

ADAM_LR, ADAM_B1, ADAM_B2, ADAM_EPS, ADAM_WD, ADAM_STEP = 0.001, 0.9, 0.999, 1e-08, 0.01, 10

VMEM_LIMIT_BYTES = 56 * 1024 * 1024

_roll = pltpu.roll


def _params(**kw):
    return pltpu.CompilerParams(vmem_limit_bytes=VMEM_LIMIT_BYTES, **kw)


def _sds(shape, dtype):
    return jax.ShapeDtypeStruct(tuple(shape), dtype)


def _shift_raw(x, d):
    n = x.shape[0]
    t = lax.broadcasted_iota(jnp.int32, x.shape, 0)
    if d > 0:
        return jnp.where(t >= d, _roll(x, d, 0), 0.0)
    return jnp.where(t < n + d, _roll(x, n + d, 0), 0.0)


@functools.partial(jax.custom_vjp, nondiff_argnums=(1,))
def shift_rows(x, d):
    return _shift_raw(x, d)


def _shift_fwd(x, d):
    return _shift_raw(x, d), None


def _shift_bwd(d, _, g):
    return (_shift_raw(g, -d),)


shift_rows.defvjp(_shift_fwd, _shift_bwd)


@jax.custom_vjp
def swap_halves(x):
    return _roll(x, HEAD // 2, 1)


def _swap_fwd(x):
    return _roll(x, HEAD // 2, 1), None


def _swap_bwd(_, g):
    return (_roll(g, HEAD // 2, 1),)


swap_halves.defvjp(_swap_fwd, _swap_bwd)


SCAN_BLOCK_ROWS = 64


def _scan_block(a, u, reverse):
    n = a.shape[0]
    t = lax.broadcasted_iota(jnp.int32, a.shape, 0)
    d = 1
    while d < n:
        if reverse:
            m = t < n - d
            a_s, u_s = _roll(a, n - d, 0), _roll(u, n - d, 0)
        else:
            m = t >= d
            a_s, u_s = _roll(a, d, 0), _roll(u, d, 0)
        u = a * jnp.where(m, u_s, 0.0) + u
        a = a * jnp.where(m, a_s, 1.0)
        d *= 2
    return a, u


def _scan_raw(a, u, reverse):
    n = a.shape[0]
    blocks = range(n // SCAN_BLOCK_ROWS)
    out = [None] * len(blocks)
    entering = None
    for b in (reversed(blocks) if reverse else blocks):
        rows = slice(b * SCAN_BLOCK_ROWS, (b + 1) * SCAN_BLOCK_ROWS)
        a_run, h = _scan_block(a[rows], u[rows], reverse)
        if entering is not None:
            h = a_run * entering + h
        out[b] = h
        entering = h[:1] if reverse else h[SCAN_BLOCK_ROWS - 1:]
    return jnp.concatenate(out, axis=0)


@jax.custom_vjp
def lin_scan(a, u):
    return _scan_raw(a, u, False)


def _lin_scan_fwd(a, u):
    hs = _scan_raw(a, u, False)
    return hs, (a, hs)


def _lin_scan_bwd(res, g):
    a, hs = res
    lam = _scan_raw(_shift_raw(a, -1), g, True)
    return lam * _shift_raw(hs, 1), lam


lin_scan.defvjp(_lin_scan_fwd, _lin_scan_bwd)


def _bdot(a, b, dims=(((1,), (0,)), ((), ()))):
    return lax.dot_general(a.astype(BF16), b.astype(BF16), dims, preferred_element_type=F32)


def _each(f, *seqs):
    return tuple(f(*a) for a in zip(*seqs))


def _split_bf16(a):
    hi = a.astype(BF16)
    return hi, (a - hi.astype(F32)).astype(BF16)


def _dot3_raw(a_s, b_s):
    a_hl = _each(_split_bf16, a_s)
    b_hl = _each(_split_bf16, b_s)
    hh = _each(lambda a, b: _bdot(a[0], b[0]), a_hl, b_hl)
    hl = _each(lambda a, b: _bdot(a[0], b[1]), a_hl, b_hl)
    lh = _each(lambda a, b: _bdot(a[1], b[0]), a_hl, b_hl)
    return _each(lambda x, y, z: x + (y + z), hh, hl, lh)


@jax.custom_vjp
def dot3(a_s, b_s):
    return _dot3_raw(a_s, b_s)


def _dot3_fwd(a_s, b_s):
    return _dot3_raw(a_s, b_s), (a_s, b_s)


def _dot3_bwd(res, g_s):
    a_s, b_s = res
    return (_each(lambda g, b: _bdot(g, b, (((1,), (1,)), ((), ()))), g_s, b_s),
            _each(lambda a, g: _bdot(a, g, (((0,), (0,)), ((), ()))), a_s, g_s))


dot3.defvjp(_dot3_fwd, _dot3_bwd)


def _eye(n):
    i = lax.broadcasted_iota(jnp.int32, (n, n), 0)
    j = lax.broadcasted_iota(jnp.int32, (n, n), 1)
    return (i == j).astype(F32)


def _unit_lower_inverse_raw(lmats):
    n = lmats[0].shape[0]
    eye = _eye(n)
    ps = _each(lambda l: -l, lmats)
    invs = _each(lambda x: eye + x, ps)
    k = 1
    while 2 * k < n:
        ps = _each(lambda p: _bdot(p, p), ps)
        invs = _each(lambda inv, p: inv + _bdot(inv, p), invs, ps)
        k *= 2
    prods = _dot3_raw(lmats, invs)
    resids = _each(lambda inv, pr: eye - inv - pr, invs, prods)
    return _each(lambda inv, r: inv + _bdot(inv, r), invs, resids)


@jax.custom_vjp
def unit_lower_inverse(lmats):
    return _unit_lower_inverse_raw(lmats)


def _uli_fwd(lmats):
    invs = _unit_lower_inverse_raw(lmats)
    return invs, invs


def _uli_bwd(invs, g_s):
    ms = _each(lambda inv, g: _bdot(inv, g, (((0,), (0,)), ((), ()))), invs, g_s)
    return (_each(lambda m, inv: -_bdot(m, inv, (((1,), (1,)), ((), ()))), ms, invs),)


unit_lower_inverse.defvjp(_uli_fwd, _uli_bwd)


def _cumsum_raw(x, reverse):
    n = x.shape[0]
    t = lax.broadcasted_iota(jnp.int32, x.shape, 0)
    d = 1
    while d < n:
        if reverse:
            x = x + jnp.where(t < n - d, _roll(x, n - d, 0), 0.0)
        else:
            x = x + jnp.where(t >= d, _roll(x, d, 0), 0.0)
        d *= 2
    return x


@jax.custom_vjp
def cumsum_rows(x):
    return _cumsum_raw(x, False)


def _cumsum_fwd(x):
    return _cumsum_raw(x, False), None


def _cumsum_bwd(_, g):
    return (_cumsum_raw(g, True),)


cumsum_rows.defvjp(_cumsum_fwd, _cumsum_bwd)


_NT = (((1,), (1,)), ((), ()))
_TN = (((0,), (0,)), ((), ()))


def _softplus(x):
    return jnp.maximum(x, 0.0) + jnp.log1p(jnp.exp(-jnp.abs(x)))


def _expm1_nonpos(x):
    poly = x * (1.0 + x * (0.5 + x * (1.0 / 6 + x * (1.0 / 24 + x * (1.0 / 120 + x * (1.0 / 720))))))
    return jnp.where(x > -0.25, poly, jnp.exp(x) - 1.0)


def _rms(x):
    return x * lax.rsqrt(jnp.mean(x * x, axis=-1, keepdims=True) + EPS)


def _causal_conv(x, w, width):
    y = w[width - 1:width, :] * x
    for j in range(width - 1):
        y = y + w[j:j + 1, :] * shift_rows(x, width - 1 - j)
    return y


def _norm_fn(x, g):
    return _rms(x) * g


def _ffn_act_fn(ug, uv, wg, wv, bg, bv):
    return jax.nn.silu(_causal_conv(ug, wg, 3) + bg) * (_causal_conv(uv, wv, 3) + bv)


def _gdn_conv_fn(x, w):
    return jax.nn.silu(_causal_conv(x, w, 4))


def _lru_fn(gate, x, cw, cb, wa, ba, wx, bx, lam):
    xr = _causal_conv(x, cw, 4) + cb
    r = jax.nn.sigmoid(_bdot(xr, wa) + ba)
    i = jax.nn.sigmoid(_bdot(xr, wx) + bx)
    log_a = -LRU_C * r * _softplus(-lam)
    a = jnp.exp(log_a)
    u = jnp.sqrt(-_expm1_nonpos(2.0 * log_a)) * (i * xr)
    hs = lin_scan(a, u)
    return jax.nn.gelu(gate) * hs


def _ret_fn(qs, ks, vs, gates, states, cos2, sin2, dmasks, ktails, qdecs, cdecs):
    c = RET_CHUNK
    n_heads = len(qs)
    n_chunks = qs[0].shape[0] // c
    units = tuple((ci, h) for ci in range(n_chunks) for h in range(n_heads))

    def rows(x, ci):
        return x[ci * c:(ci + 1) * c]

    qrs = tuple(rows(qs[h], ci) * rows(cos2, ci) + swap_halves(rows(qs[h], ci)) * rows(sin2, ci) for ci, h in units)
    krs = tuple((rows(ks[h], ci) * rows(cos2, ci) + swap_halves(rows(ks[h], ci)) * rows(sin2, ci)) * (HEAD ** -0.5) for ci, h in units)
    vus = tuple(rows(vs[h], ci) for ci, h in units)
    scores = tuple(_bdot(q, k, _NT) * dmasks[h] for q, k, (_, h) in zip(qrs, krs, units))
    intra = _each(lambda sc, v: _bdot(sc, v), scores, vus)
    outs = []
    for ci in range(n_chunks):
        mine = slice(ci * n_heads, (ci + 1) * n_heads)
        inter = _each(lambda q, d, s: _bdot(q * d, s), qrs[mine], qdecs, states)
        outs.append(_each(lambda a, b: a + b, intra[mine], inter))
        states = _each(lambda s, cd, k, kt, v: s * cd + _bdot(k * kt, v, _TN), states, cdecs, krs[mine], ktails, vus[mine])
    ys = tuple(_rms(jnp.concatenate([outs[ci][h] for ci in range(n_chunks)], axis=0)) * jax.nn.silu(gates[h]) for h in range(n_heads))
    return ys, states


def _pick_lane(x, lane_idx):
    lane = lax.broadcasted_iota(jnp.int32, x.shape, 1)
    return jnp.sum(jnp.where(lane == lane_idx, x, 0.0), axis=1, keepdims=True)


def _l2norm(x):
    return x * lax.rsqrt(jnp.sum(x * x, axis=-1, keepdims=True) + EPS)


def _gdn_fn(qcs, kcs, vcs, gates, small, a_log, dt_bias, gain, states):
    c = GDN_CHUNK
    n_heads = len(qcs)
    n_chunks = qcs[0].shape[0] // c
    units = tuple((ci, h) for ci in range(n_chunks) for h in range(n_heads))

    def unit_rows(per_head):
        return tuple(per_head[h][ci * c:(ci + 1) * c] for ci, h in units)

    smalls = tuple(small[ci * c:(ci + 1) * c] for ci, _ in units)
    heads = tuple(h for _, h in units)
    intra = _gdn_intra(unit_rows(qcs), unit_rows(kcs), unit_rows(vcs), smalls, heads, a_log, dt_bias)
    outs = []
    for ci in range(n_chunks):
        mine = slice(ci * n_heads, (ci + 1) * n_heads)
        os_, states = _gdn_inter(*(part[mine] for part in intra), states)
        outs.append(os_)
    ys = tuple(_rms(jnp.concatenate([outs[ci][h] for ci in range(n_chunks)], axis=0)) * gain * jax.nn.silu(gates[h])
               for h in range(n_heads))
    return ys, states


def _gdn_inter(qs, ks, us, ws, attns, gcs, g_lasts, states):
    v_news = _each(lambda u, w, s: u - _bdot(w, s), us, ws, states)
    inter = _each(lambda q, gc, s: _bdot(q * jnp.exp(gc), s), qs, gcs, states)
    os_ = _each(lambda x, a, v: x + _bdot(a, v), inter, attns, v_news)
    new_states = _each(lambda s, gl, k, gc, v: s * jnp.exp(gl) + _bdot(k * jnp.exp(gl - gc), v, _TN), states, g_lasts, ks, gcs, v_news)
    return os_, new_states


def _gdn_intra(qcs, kcs, vcs, smalls, heads, a_log, dt_bias):
    c = GDN_CHUNK
    qs = _each(lambda x: _l2norm(x) * (HEAD ** -0.5), qcs)
    ks = _each(_l2norm, kcs)
    betas = _each(lambda sm, h: jax.nn.sigmoid(_pick_lane(sm, h)), smalls, heads)
    gs = _each(lambda sm, h: -jnp.exp(_pick_lane(a_log, h)) * _softplus(_pick_lane(sm, h + N_HEADS) + _pick_lane(dt_bias, h)),
               smalls, heads)
    i = lax.broadcasted_iota(jnp.int32, (c, c), 0)
    j = lax.broadcasted_iota(jnp.int32, (c, c), 1)
    tril = i >= j
    gcs = _each(lambda g: cumsum_rows(jnp.broadcast_to(g, (c, LANES)))[:, :1], gs)
    gc_rows = _each(lambda gc: jnp.broadcast_to(gc, (c, c)), gcs)
    decays = _each(lambda r: jnp.where(tril, jnp.exp(jnp.where(tril, r - r.T, 0.0)), 0.0), gc_rows)
    kbs = _each(lambda k, b: k * b, ks, betas)
    lmats = _each(lambda kb, k, d: jnp.where(i > j, _bdot(kb, k, _NT) * d, 0.0), kbs, ks, decays)
    attns = _each(lambda q, k, d: jnp.where(tril, _bdot(q, k, _NT) * d, 0.0), qs, ks, decays)
    invs = unit_lower_inverse(lmats)
    us = dot3(invs, _each(lambda v, b: v * b, vcs, betas))
    ws = dot3(invs, _each(lambda kb, gc: kb * jnp.exp(gc), kbs, gcs))
    g_lasts = _each(lambda g: jnp.sum(g, axis=0, keepdims=True), gs)
    return qs, ks, us, ws, attns, gcs, g_lasts


def _final_fn(h, g, target):
    y = _rms(h) * g
    return 0.5 * jnp.sum(jnp.mean(jnp.square(y - target), axis=-1, keepdims=True), axis=0, keepdims=True)


def _tile(n, candidates):
    for t in candidates:
        if n % t == 0:
            return t
    raise ValueError(f"no tile for {n}")


MATMUL_RESIDENT_LHS_BYTES = 8 * 1024 * 1024


def matmul(a, b, *, ta=False, tb=False, add=None, out_dtype=F32, tm=None, tn=None, split=None, layer=None, column_halves=None, name):
    m = a.shape[1] if ta else a.shape[0]
    k = a.shape[0] if ta else a.shape[1]
    n = b.shape[0] if tb else b.shape[1]
    assert k == (b.shape[1] if tb else b.shape[0])
    out_shape, out_block, out_index = (m, n), None, lambda i, j: (i, j)
    if split is not None:
        dims4, perm = split
        out_shape = tuple(dims4[p] for p in perm)
        r, cols = out_shape[2:]
        tm, tn = m, tn or _tile(cols, (1408, 512))
        cb = cols // tn
        if perm == (0, 2, 1, 3):
            out_block, out_index = (2, None, r, tn), lambda i, j: (0, j // cb, 0, j % cb)
        elif perm == (1, 0, 2, 3):
            out_block, out_index = (2, N_SHARD, r, tn), lambda i, j: (0, 0, 0, j)
        else:
            raise ValueError(perm)
    if tm is None and not ta and m * k * a.dtype.itemsize <= MATMUL_RESIDENT_LHS_BYTES:
        tm = m
    tm = tm or _tile(m, (1024, 512, 1408, 256, 128))
    tn = tn or _tile(n, (512, 1408, 256, 128))
    aliases, prev, keep_rows = {}, None, None
    if layer is not None:
        index, count, prev = layer
        out_shape, out_block, out_index = (count, m, n), (None, tm, tn), lambda i, j: (index, i, j)
    if column_halves is not None:
        total_rows, first_row, keep_rows, prev = column_halves
        tn = n // 2
        rows_out = keep_rows or tm
        out_shape, out_block = (2, total_rows, tn), (None, rows_out, tn)
        out_index = lambda i, j: (j, first_row // rows_out + i, 0)
    dims = (((0 if ta else 1,), (1 if tb else 0,)), ((), ()))

    def body(a_ref, b_ref, *rest):
        acc = lax.dot_general(a_ref[...].astype(BF16), b_ref[...].astype(BF16), dims, preferred_element_type=F32)
        if add is not None:
            acc = acc + rest[0][...]
        o_ref = rest[-1]
        acc = acc.astype(out_dtype)
        if split is not None and split[1] == (1, 0, 2, 3):
            rows = o_ref.shape[2]
            for s in range(N_SHARD):
                for h in range(2):
                    o_ref[h, s] = acc[(2 * s + h) * rows:(2 * s + h + 1) * rows]
        elif keep_rows is not None:
            o_ref[...] = acc[:keep_rows]
        else:
            o_ref[...] = acc.reshape(o_ref.shape)

    a_spec = pl.BlockSpec((k, tm), lambda i, j: (0, i)) if ta else pl.BlockSpec((tm, k), lambda i, j: (i, 0))
    b_spec = pl.BlockSpec((tn, k), lambda i, j: (j, 0)) if tb else pl.BlockSpec((k, tn), lambda i, j: (0, j))
    o_spec = pl.BlockSpec(out_block or (tm, tn), out_index)
    in_specs, args = [a_spec, b_spec], [a, b]
    if add is not None:
        in_specs.append(o_spec)
        args.append(add)
    if prev is not None:
        aliases = {len(args): 0}
        in_specs.append(pl.BlockSpec(memory_space=pl.ANY))
        args.append(prev)
    return pl.pallas_call(body, out_shape=_sds(out_shape, out_dtype), grid=(m // tm, n // tn), in_specs=in_specs,
                          out_specs=o_spec, input_output_aliases=aliases, compiler_params=_params(), name=name)(*args)


def norm_matmul(x, g, b, *, tb=False, name):
    t, k = x.shape
    n = b.shape[0] if tb else b.shape[1]
    tn = _tile(n, (512, 1408, 256, 128))
    dims = (((1,), (1 if tb else 0,)), ((), ()))

    def body(x_ref, g_ref, b_ref, o_ref, hn_ref):
        @pl.when(pl.program_id(0) == 0)
        def _():
            hn_ref[...] = _norm_fn(x_ref[...], g_ref[...]).astype(BF16)

        o_ref[...] = lax.dot_general(hn_ref[...], b_ref[...].astype(BF16), dims, preferred_element_type=F32)

    b_spec = pl.BlockSpec((tn, k), lambda j: (j, 0)) if tb else pl.BlockSpec((k, tn), lambda j: (0, j))
    whole = pl.BlockSpec((t, k), lambda j: (0, 0))
    return pl.pallas_call(body, out_shape=(_sds((t, n), F32), _sds((t, k), BF16)), grid=(n // tn,),
                          in_specs=[whole, pl.BlockSpec((1, k), lambda j: (0, 0)), b_spec],
                          out_specs=(pl.BlockSpec((t, tn), lambda j: (0, j)), whole), compiler_params=_params(), name=name)(x, g, b)


ROW_TILE = 256


def norm_bwd(x, g, dy, dres, *, name):
    t, d = x.shape

    def body(x_ref, g_ref, dy_ref, dres_ref, dx_ref, dg_ref):
        _, vjp = jax.vjp(_norm_fn, x_ref[...], g_ref[...])
        dx, dg = vjp(dy_ref[...])
        dx_ref[...] = dx + dres_ref[...]

        @pl.when(pl.program_id(0) == 0)
        def _():
            dg_ref[...] = jnp.zeros_like(dg_ref)

        dg_ref[...] += dg

    row = pl.BlockSpec((ROW_TILE, d), lambda i: (i, 0))
    vec = pl.BlockSpec((1, d), lambda i: (0, 0))
    return pl.pallas_call(body, out_shape=(_sds((t, d), F32), _sds((1, d), F32)), grid=(t // ROW_TILE,),
                          in_specs=[row, vec, row, row], out_specs=(row, vec), compiler_params=_params(), name=name)(x, g, dy, dres)


def final_fwd_bwd(h, g, target, *, name):
    t, d = h.shape

    def body(h_ref, g_ref, t_ref, loss_ref, dh_ref, dg_ref):
        tgt = t_ref[...]
        loss, vjp = jax.vjp(lambda hh, gg: _final_fn(hh, gg, tgt), h_ref[...], g_ref[...])
        dh, dg = vjp(jnp.ones((1, 1), F32))
        dh_ref[...] = dh

        @pl.when(pl.program_id(0) == 0)
        def _():
            dg_ref[...] = jnp.zeros_like(dg_ref)
            loss_ref[...] = jnp.zeros_like(loss_ref)

        dg_ref[...] += dg
        loss_ref[...] += jnp.broadcast_to(loss, loss_ref.shape)

    row = pl.BlockSpec((ROW_TILE, d), lambda i: (i, 0))
    vec = pl.BlockSpec((1, d), lambda i: (0, 0))
    return pl.pallas_call(body, out_shape=(_sds((1, LANES), F32), _sds((t, d), F32), _sds((1, d), F32)), grid=(t // ROW_TILE,),
                          in_specs=[row, vec, row], out_specs=(pl.BlockSpec((1, LANES), lambda i: (0, 0)), row, vec),
                          compiler_params=_params(), name=name)(h, g, target)


FFN_FWD_COLS = 256
FFN_BWD_COLS = 128


def ffn_act_fwd(u, cw, cb, *, name):
    t = u.shape[0]
    w = FFN_FWD_COLS
    nb = D_FF // w

    def body(ug_ref, uv_ref, wg_ref, wv_ref, bg_ref, bv_ref, o_ref):
        o_ref[...] = _ffn_act_fn(ug_ref[...], uv_ref[...], wg_ref[...], wv_ref[...], bg_ref[...], bv_ref[...]).astype(BF16)

    def col(rows, off):
        return pl.BlockSpec((rows, w), lambda j: (0, j + off))

    return pl.pallas_call(body, out_shape=_sds((t, D_FF), BF16), grid=(nb,),
                          in_specs=[col(t, 0), col(t, nb), col(3, 0), col(3, nb), col(1, 0), col(1, nb)],
                          out_specs=col(t, 0), compiler_params=_params(), name=name)(u, u, cw, cw, cb, cb)


def _put_column_blocks(step, n_steps, blocks, dst_ref, width, stage_ref, sems):
    def copies(at):
        slot = at % 2
        return [pltpu.make_async_copy(stage_ref.at[slot, p], dst_ref.at[:, pl.ds(pl.multiple_of((p * n_steps + at) * width, LANES), width)],
                                      sems.at[slot, p]) for p in range(len(blocks))]

    @pl.when(step >= 2)
    def _():
        for cp in copies(step - 2):
            cp.wait()

    for p, value in enumerate(blocks):
        stage_ref[step % 2, p] = value
    for cp in copies(step):
        cp.start()

    @pl.when(step == n_steps - 1)
    def _():
        for cp in copies(step - 1) + copies(step):
            cp.wait()


def ffn_act_bwd(u, cw, cb, da, *, name):
    t = u.shape[0]
    w = FFN_BWD_COLS
    nb = D_FF // w

    def body(ug_ref, uv_ref, wg_ref, wv_ref, bg_ref, bv_ref, da_ref, dug_ref, duv_ref, dwg_ref, dwv_ref, dbg_ref, dbv_ref):
        _, vjp = jax.vjp(_ffn_act_fn, ug_ref[...], uv_ref[...], wg_ref[...], wv_ref[...], bg_ref[...], bv_ref[...])
        dug, duv, dwg, dwv, dbg, dbv = vjp(da_ref[...])
        dug_ref[...] = dug.astype(BF16)
        duv_ref[...] = duv.astype(BF16)
        dwg_ref[...] = dwg
        dwv_ref[...] = dwv
        dbg_ref[...] = dbg
        dbv_ref[...] = dbv

    def col(rows, off):
        return pl.BlockSpec((rows, w), lambda j: (0, j + off))

    outs = pl.pallas_call(
        body, out_shape=(_sds((t, D_FF), BF16), _sds((t, D_FF), BF16), _sds((3, D_FF), F32), _sds((3, D_FF), F32),
                         _sds((1, D_FF), F32), _sds((1, D_FF), F32)),
        grid=(nb,), in_specs=[col(t, 0), col(t, nb), col(3, 0), col(3, nb), col(1, 0), col(1, nb), col(t, 0)],
        out_specs=(col(t, 0), col(t, 0), col(3, 0), col(3, 0), col(1, 0), col(1, 0)), compiler_params=_params(), name=name,
    )(u, u, cw, cw, cb, cb, da)
    dug, duv, dwg, dwv, dbg, dbv = outs
    return jnp.concatenate([dug, duv], axis=1), jnp.concatenate([dwg, dwv], axis=1), jnp.concatenate([dbg, dbv], axis=1)


GDN_CONV_COLS = 256
GDN_CONV_OFF = 4 * GROUP


def gdn_conv_fwd(p, cw, *, name):
    t = p.shape[0]
    w = GDN_CONV_COLS
    nb = 3 * GROUP // w
    off = GDN_CONV_OFF // w

    def body(x_ref, w_ref, o_ref):
        o_ref[...] = _gdn_conv_fn(x_ref[...], w_ref[...])

    return pl.pallas_call(body, out_shape=_sds((t, 3 * GROUP), F32), grid=(nb,),
                          in_specs=[pl.BlockSpec((t, w), lambda j: (0, j + off)), pl.BlockSpec((4, w), lambda j: (0, j))],
                          out_specs=pl.BlockSpec((t, w), lambda j: (0, j)), compiler_params=_params(), name=name)(p, cw)


def gdn_conv_bwd(p, cw, dc, *, name):
    t = p.shape[0]
    w = GDN_CONV_COLS
    nb = 3 * GROUP // w
    off = GDN_CONV_OFF // w

    def body(x_ref, w_ref, dc_ref, dx_ref, dw_ref):
        _, vjp = jax.vjp(_gdn_conv_fn, x_ref[...], w_ref[...])
        dx, dw = vjp(dc_ref[...])
        dx_ref[...] = dx.astype(BF16)
        dw_ref[...] = dw

    blk = pl.BlockSpec((t, w), lambda j: (0, j))
    wblk = pl.BlockSpec((4, w), lambda j: (0, j))
    return pl.pallas_call(body, out_shape=(_sds((t, 3 * GROUP), BF16), _sds((4, 3 * GROUP), F32)), grid=(nb,),
                          in_specs=[pl.BlockSpec((t, w), lambda j: (0, j + off)), wblk, blk], out_specs=(blk, wblk),
                          compiler_params=_params(), name=name)(p, cw, dc)


def _lru_specs(t):
    w = D_MODEL // LRU_BLOCKS
    gate = pl.BlockSpec((t, w), lambda j: (0, j))
    xin = pl.BlockSpec((t, w), lambda j: (0, j + LRU_BLOCKS))
    cw = pl.BlockSpec((4, w), lambda j: (0, j))
    vec = pl.BlockSpec((1, w), lambda j: (0, j))
    mat = pl.BlockSpec((None, w, w), lambda j: (j, 0, 0))
    return gate, xin, cw, vec, mat


def lru_fwd(gx, cw, cb, wa, ba, wx, bx, lam, *, name):
    t = gx.shape[0]
    gate, xin, cws, vec, mat = _lru_specs(t)

    def body(g_ref, x_ref, cw_ref, cb_ref, wa_ref, ba_ref, wx_ref, bx_ref, lam_ref, o_ref):
        o_ref[...] = _lru_fn(g_ref[...], x_ref[...], cw_ref[...], cb_ref[...], wa_ref[...], ba_ref[...], wx_ref[...],
                             bx_ref[...], lam_ref[...]).astype(BF16)

    return pl.pallas_call(body, out_shape=_sds((t, D_MODEL), BF16), grid=(LRU_BLOCKS,),
                          in_specs=[gate, xin, cws, vec, mat, vec, mat, vec, vec], out_specs=gate,
                          compiler_params=_params(), name=name)(gx, gx, cw, cb, wa, ba, wx, bx, lam)


def lru_bwd(gx, cw, cb, wa, ba, wx, bx, lam, dy, *, name):
    t = gx.shape[0]
    gate, xin, cws, vec, mat = _lru_specs(t)

    def body(g_ref, x_ref, cw_ref, cb_ref, wa_ref, ba_ref, wx_ref, bx_ref, lam_ref, dy_ref,
             dgx_ref, dcw_ref, dcb_ref, dwa_ref, dba_ref, dwx_ref, dbx_ref, dlam_ref, stage_ref, sems):
        _, vjp = jax.vjp(_lru_fn, g_ref[...], x_ref[...], cw_ref[...], cb_ref[...], wa_ref[...], ba_ref[...], wx_ref[...],
                         bx_ref[...], lam_ref[...])
        dg, dx, dcw, dcb, dwa, dba, dwx, dbx, dlam = vjp(dy_ref[...])
        _put_column_blocks(pl.program_id(0), LRU_BLOCKS, (dg.astype(BF16), dx.astype(BF16)), dgx_ref, D_MODEL // LRU_BLOCKS, stage_ref, sems)
        dcw_ref[...] = dcw
        dcb_ref[...] = dcb
        dwa_ref[...] = dwa
        dba_ref[...] = dba
        dwx_ref[...] = dwx
        dbx_ref[...] = dbx
        dlam_ref[...] = dlam

    d = D_MODEL
    w = d // LRU_BLOCKS
    out_shape = (_sds((t, 2 * d), BF16), _sds((4, d), F32), _sds((1, d), F32), _sds((LRU_BLOCKS, w, w), F32),
                 _sds((1, d), F32), _sds((LRU_BLOCKS, w, w), F32), _sds((1, d), F32), _sds((1, d), F32))
    return pl.pallas_call(body, out_shape=out_shape, grid=(LRU_BLOCKS,),
                          in_specs=[gate, xin, cws, vec, mat, vec, mat, vec, vec, gate],
                          out_specs=(pl.BlockSpec(memory_space=pl.ANY), cws, vec, mat, vec, mat, vec, vec),
                          scratch_shapes=[pltpu.VMEM((2, 2, t, w), BF16), pltpu.SemaphoreType.DMA((2, 2))],
                          compiler_params=_params(), name=name)(gx, gx, cw, cb, wa, ba, wx, bx, lam, dy)


def _ret_tables():
    half = HEAD // 2
    inv_freq = (np.float32(ROPE_BASE) ** (-np.arange(half, dtype=np.float32) / np.float32(half))).astype(np.float32)
    ang = (np.arange(SEQ, dtype=np.float32)[:, None] * inv_freq[None, :]).astype(np.float64)
    cos2 = np.concatenate([np.cos(ang), np.cos(ang)], axis=1).astype(np.float32)
    sin2 = np.concatenate([-np.sin(ang), np.sin(ang)], axis=1).astype(np.float32)
    c = RET_CHUNK
    log_gamma = np.log1p(-np.exp2(-5.0 - np.arange(N_HEADS, dtype=np.float64)))
    idx = np.arange(c, dtype=np.float64)
    rel = idx[:, None] - idx[None, :]
    dmask = np.where(rel >= 0, np.exp(log_gamma[:, None, None] * np.maximum(rel, 0.0)), 0.0)
    ones = np.ones((N_HEADS, c, HEAD))
    ktail = np.exp(log_gamma[:, None] * (c - 1 - idx))[:, :, None] * ones
    qdec = np.exp(log_gamma[:, None] * (idx + 1.0))[:, :, None] * ones
    cdec = np.exp(log_gamma * c)[:, None, None] * ones
    return tuple(jnp.asarray(a, F32) for a in (cos2, sin2, dmask, ktail, qdec, cdec))


def _ret_specs(rev):
    c = RET_CHUNK * RET_CHUNKS_PER_STEP
    nc = SEQ // c

    def n_of(n):
        return nc - 1 - n if rev else n

    def group(off):
        return pl.BlockSpec((c, GROUP), lambda n: (n_of(n), off))

    tab = pl.BlockSpec((c, HEAD), lambda n: (n_of(n), 0))
    const = pl.BlockSpec((N_HEADS, RET_CHUNK, HEAD), lambda n: (0, 0, 0))
    state = pl.BlockSpec((N_HEADS, None, HEAD, HEAD), lambda n: (0, n_of(n), 0, 0))
    return group, tab, const, state, nc


def _head(h):
    return slice(h * HEAD, (h + 1) * HEAD)


def ret_fwd(p, tables, *, name):
    group, tab, const, state, nc = _ret_specs(False)

    def body(q_ref, k_ref, v_ref, g_ref, cos_ref, sin_ref, dm_ref, kt_ref, qd_ref, cd_ref, y_ref, st_ref, s_scr):
        @pl.when(pl.program_id(0) == 0)
        def _():
            s_scr[...] = jnp.zeros_like(s_scr)

        heads = range(N_HEADS)
        states = tuple(s_scr[h] for h in heads)
        ys, new_states = _ret_fn(*(tuple(r[:, _head(h)] for h in heads) for r in (q_ref, k_ref, v_ref, g_ref)), states,
                                 cos_ref[...], sin_ref[...], *(tuple(r[h] for h in heads) for r in (dm_ref, kt_ref, qd_ref, cd_ref)))
        for h in heads:
            st_ref[h] = states[h]
            y_ref[:, _head(h)] = ys[h].astype(BF16)
            s_scr[h] = new_states[h]

    return pl.pallas_call(
        body, out_shape=(_sds((SEQ, 2 * GROUP), BF16), _sds((N_HEADS, nc, HEAD, HEAD), F32)), grid=(nc,),
        in_specs=[group(0), group(1), group(2), group(3), tab, tab, const, const, const, const],
        out_specs=(group(0), state), scratch_shapes=[pltpu.VMEM((N_HEADS, HEAD, HEAD), F32)], compiler_params=_params(), name=name,
    )(p, p, p, p, *tables)


def ret_bwd(p, tables, states, dy, *, name):
    group, tab, const, state, nc = _ret_specs(True)

    def body(q_ref, k_ref, v_ref, g_ref, cos_ref, sin_ref, dm_ref, kt_ref, qd_ref, cd_ref, st_ref, dy_ref,
             dq_ref, dk_ref, dv_ref, dg_ref, ds_scr):
        @pl.when(pl.program_id(0) == 0)
        def _():
            ds_scr[...] = jnp.zeros_like(ds_scr)

        heads = range(N_HEADS)
        consts = (cos_ref[...], sin_ref[...], *(tuple(r[h] for h in heads) for r in (dm_ref, kt_ref, qd_ref, cd_ref)))
        _, vjp = jax.vjp(lambda *a: _ret_fn(*a, *consts), *(tuple(r[:, _head(h)] for h in heads) for r in (q_ref, k_ref, v_ref, g_ref)),
                         tuple(st_ref[h] for h in heads))
        dqs, dks, dvs, dgs, dss = vjp((tuple(dy_ref[:, _head(h)] for h in heads), tuple(ds_scr[h] for h in heads)))
        for h in heads:
            dq_ref[:, _head(h)] = dqs[h].astype(BF16)
            dk_ref[:, _head(h)] = dks[h].astype(BF16)
            dv_ref[:, _head(h)] = dvs[h].astype(BF16)
            dg_ref[:, _head(h)] = dgs[h].astype(BF16)
            ds_scr[h] = dss[h]

    out = _sds((SEQ, GROUP), BF16)
    return pl.pallas_call(
        body, out_shape=(out, out, out, out), grid=(nc,),
        in_specs=[group(0), group(1), group(2), group(3), tab, tab, const, const, const, const, state, group(0)],
        out_specs=(group(0), group(0), group(0), group(0)), scratch_shapes=[pltpu.VMEM((N_HEADS, HEAD, HEAD), F32)],
        compiler_params=_params(), name=name,
    )(p, p, p, p, *tables, states, dy)


def _gdn_specs(rev):
    c = GDN_CHUNK * GDN_CHUNKS_PER_STEP
    nc = SEQ // c

    def n_of(n):
        return nc - 1 - n if rev else n

    def group(off):
        return pl.BlockSpec((c, GROUP), lambda n: (n_of(n), off))

    small = pl.BlockSpec((c, LANES), lambda n: (n_of(n), 0))
    vec = pl.BlockSpec((1, LANES), lambda n: (0, 0))
    state = pl.BlockSpec((N_HEADS, None, HEAD, HEAD), lambda n: (0, n_of(n), 0, 0))
    qkv = pl.BlockSpec((c, 3 * GROUP), lambda n: (n_of(n), 0))
    return group, small, vec, state, qkv, nc


GDN_GATE_GROUP = 7


def gdn_fwd(conv, p, small, a_log, dt_bias, gain, y_started, *, name):
    group, sm, vec, state, _, nc = _gdn_specs(False)

    def body(q_ref, k_ref, v_ref, g_ref, sm_ref, al_ref, dt_ref, gn_ref, _, y_ref, st_ref, s_scr):
        @pl.when(pl.program_id(0) == 0)
        def _():
            s_scr[...] = jnp.zeros_like(s_scr)

        states = tuple(s_scr[h] for h in range(N_HEADS))
        ys, new_states = _gdn_fn(*(tuple(r[:, _head(h)] for h in range(N_HEADS)) for r in (q_ref, k_ref, v_ref, g_ref)),
                                 sm_ref[...], al_ref[...], dt_ref[...], gn_ref[...], states)
        for h in range(N_HEADS):
            st_ref[h] = states[h]
            y_ref[:, _head(h)] = ys[h].astype(BF16)
            s_scr[h] = new_states[h]

    return pl.pallas_call(
        body, out_shape=(_sds((SEQ, 2 * GROUP), BF16), _sds((N_HEADS, nc, HEAD, HEAD), F32)), grid=(nc,),
        in_specs=[group(0), group(1), group(2), group(GDN_GATE_GROUP), sm, vec, vec, vec, pl.BlockSpec(memory_space=pl.ANY)],
        out_specs=(group(1), state), input_output_aliases={8: 0},
        scratch_shapes=[pltpu.VMEM((N_HEADS, HEAD, HEAD), F32)], compiler_params=_params(), name=name,
    )(conv, conv, conv, p, small, a_log, dt_bias, gain, y_started)


def gdn_bwd(conv, p, small, a_log, dt_bias, gain, states, dy, *, name):
    group, sm, vec, state, qkv, nc = _gdn_specs(True)

    def body(q_ref, k_ref, v_ref, g_ref, sm_ref, al_ref, dt_ref, gn_ref, st_ref, dy_ref,
             dqkv_ref, dg_ref, dsm_ref, dal_ref, ddt_ref, dgn_ref, ds_scr):
        @pl.when(pl.program_id(0) == 0)
        def _():
            ds_scr[...] = jnp.zeros_like(ds_scr)
            dal_ref[...] = jnp.zeros_like(dal_ref)
            ddt_ref[...] = jnp.zeros_like(ddt_ref)
            dgn_ref[...] = jnp.zeros_like(dgn_ref)

        per_head = tuple(tuple(r[:, _head(h)] for h in range(N_HEADS)) for r in (q_ref, k_ref, v_ref, g_ref))
        _, vjp = jax.vjp(_gdn_fn, *per_head, sm_ref[...], al_ref[...], dt_ref[...], gn_ref[...],
                         tuple(st_ref[h] for h in range(N_HEADS)))
        cts = (tuple(dy_ref[:, _head(h)] for h in range(N_HEADS)), tuple(ds_scr[h] for h in range(N_HEADS)))
        dqs, dks, dvs, dgs, dsm, dal, ddt, dgn, dss = vjp(cts)
        for h in range(N_HEADS):
            for part, blocks in enumerate((dqs, dks, dvs)):
                dqkv_ref[:, part * GROUP + h * HEAD:part * GROUP + (h + 1) * HEAD] = blocks[h]
            dg_ref[:, _head(h)] = dgs[h].astype(BF16)
            ds_scr[h] = dss[h]
        dsm_ref[...] = dsm
        dal_ref[...] += dal
        ddt_ref[...] += ddt
        dgn_ref[...] += dgn

    pv = _sds((1, LANES), F32)
    return pl.pallas_call(
        body, out_shape=(_sds((SEQ, 3 * GROUP), F32), _sds((SEQ, GROUP), BF16), _sds((SEQ, LANES), F32), pv, pv, pv), grid=(nc,),
        in_specs=[group(0), group(1), group(2), group(GDN_GATE_GROUP), sm, vec, vec, vec, state, group(1)],
        out_specs=(qkv, group(0), sm, vec, vec, vec), scratch_shapes=[pltpu.VMEM((N_HEADS, HEAD, HEAD), F32)],
        compiler_params=_params(), name=name,
    )(conv, conv, conv, p, small, a_log, dt_bias, gain, states, dy)


ELEMENTWISE_BLOCK_BYTES = 2 * 1024 * 1024


def _row_tile(r, c):
    best = None
    for tr in range(8, r + 1, 8):
        if r % tr == 0 and tr * c * 4 <= ELEMENTWISE_BLOCK_BYTES:
            best = tr
    if best is None:
        raise ValueError(f"no row tile for ({r}, {c})")
    return best


def _tile_2d(r, c):
    if any(r % tr == 0 for tr in range(8, r + 1, 8)):
        return _row_tile(r, c), c
    tc = max(t for t in range(LANES, c + 1, LANES) if c % t == 0 and r * t * 4 <= ELEMENTWISE_BLOCK_BYTES)
    return r, tc


def _core_index():
    return lax.axis_index("c").astype(jnp.int32).reshape(1)


def _chip_index():
    return (2 * lax.axis_index("x") + lax.axis_index("y")).astype(jnp.int32).reshape(1)


def adamw_halves(w, m, v, g_own, g_sib, *, layer=0, prev=None, name):
    n_layers, rows, c = w.shape
    r = rows // 2
    tr = _row_tile(r, c)
    nb = r // tr

    def body(c_ref, w_ref, m_ref, v_ref, own_ref, sib_ref, *rest):
        g_ref, d_ref, nm_ref, nv_ref = rest[-4:]
        gg = jnp.where(pl.program_id(0) == c_ref[0], own_ref[...], sib_ref[...])
        nm = ADAM_B1 * m_ref[...] + (1.0 - ADAM_B1) * gg
        nv = ADAM_B2 * v_ref[...] + (1.0 - ADAM_B2) * jnp.square(gg)
        m_hat = nm / (1.0 - ADAM_B1 ** ADAM_STEP)
        v_hat = nv / (1.0 - ADAM_B2 ** ADAM_STEP)
        g_ref[...] = gg
        d_ref[...] = -ADAM_LR * (m_hat / (jnp.sqrt(v_hat) + ADAM_EPS) + ADAM_WD * w_ref[...])
        nm_ref[...] = nm
        nv_ref[...] = nv

    full = pl.BlockSpec((None, tr, c), lambda h, i, cr: (layer, h * nb + i, 0))
    half = pl.BlockSpec((tr, c), lambda h, i, cr: (i, 0))
    o = _sds((n_layers, rows, c), F32)
    prev = list(prev or ())
    gs = pltpu.PrefetchScalarGridSpec(num_scalar_prefetch=1, grid=(2, nb), in_specs=[full, full, full, half, half] + [_ANY] * len(prev),
                                      out_specs=(full, full, full, full))
    n_fixed = 6
    return pl.pallas_call(body, out_shape=(o, o, o, o), grid_spec=gs, compiler_params=_params(), name=name,
                          input_output_aliases={n_fixed + k: k for k in range(len(prev))})(
        _core_index(), w, m, v, g_own, g_sib, *prev)


ADAMW_ROW_STEPS = 6


def adamw_rows(w, g, m, v, *, name):
    rows, _, cols = w.shape
    tr = rows // ADAMW_ROW_STEPS

    def body(w_ref, g_ref, m_ref, v_ref, g_out_ref, d_ref, nm_ref, nv_ref):
        gg = g_ref[...]
        nm = ADAM_B1 * m_ref[...] + (1.0 - ADAM_B1) * gg
        nv = ADAM_B2 * v_ref[...] + (1.0 - ADAM_B2) * jnp.square(gg)
        m_hat = nm / (1.0 - ADAM_B1 ** ADAM_STEP)
        v_hat = nv / (1.0 - ADAM_B2 ** ADAM_STEP)
        g_out_ref[...] = gg
        d_ref[...] = -ADAM_LR * (m_hat / (jnp.sqrt(v_hat) + ADAM_EPS) + ADAM_WD * w_ref[...])
        nm_ref[...] = nm
        nv_ref[...] = nv

    blk = pl.BlockSpec((tr, 1, cols), lambda i: (i, 0, 0))
    o = _sds(w.shape, F32)
    return pl.pallas_call(body, out_shape=(o, o, o, o), grid=(ADAMW_ROW_STEPS,), in_specs=[blk] * 4, out_specs=(blk, blk, blk, blk),
                          compiler_params=_params(), name=name)(w, g, m, v)


def adamw_many(ws, gs, ms, vs, *, name):
    n = len(ws)

    def body(*refs):
        w_refs, g_refs, m_refs, v_refs, d_refs, nm_refs, nv_refs = (refs[k * n:(k + 1) * n] for k in range(7))
        for i in range(n):
            gg = g_refs[i][...]
            nm = ADAM_B1 * m_refs[i][...] + (1.0 - ADAM_B1) * gg
            nv = ADAM_B2 * v_refs[i][...] + (1.0 - ADAM_B2) * jnp.square(gg)
            m_hat = nm / (1.0 - ADAM_B1 ** ADAM_STEP)
            v_hat = nv / (1.0 - ADAM_B2 ** ADAM_STEP)
            d_refs[i][...] = -ADAM_LR * (m_hat / (jnp.sqrt(v_hat) + ADAM_EPS) + ADAM_WD * w_refs[i][...])
            nm_refs[i][...] = nm
            nv_refs[i][...] = nv

    outs = pl.pallas_call(body, out_shape=[_sds(w.shape, F32) for w in ws] * 3, compiler_params=_params(), name=name)(*ws, *gs, *ms, *vs)
    return outs[:n], outs[n:2 * n], outs[2 * n:]


def add_core_halves(g2, land, *, out_dtype, name):
    _, ns, r, cols = g2.shape
    tr, tc = _tile_2d(r, cols)

    def body(c_ref, a_ref, b_ref, o_ref):
        o_ref[...] = (a_ref[...] + b_ref[...]).astype(out_dtype)

    gs = pltpu.PrefetchScalarGridSpec(
        num_scalar_prefetch=1, grid=(ns, r // tr, cols // tc),
        in_specs=[pl.BlockSpec((None, None, tr, tc), lambda s, i, j, cr: (cr[0], s, i, j)),
                  pl.BlockSpec((None, tr, tc), lambda s, i, j, cr: (s, i, j))],
        out_specs=pl.BlockSpec((None, tr, tc), lambda s, i, j, cr: (s, i, j)))
    return pl.pallas_call(body, out_shape=_sds((ns, r, cols), out_dtype), grid_spec=gs, compiler_params=_params(), name=name)(
        _core_index(), g2, land)


def sum_over_chips(own, land, *, scatter, name):
    _, r, cols = own.shape
    tr, tc = _tile_2d(r, cols)

    def body(mine_ref, own_ref, l0, l1, l2, l3, o_ref):
        mine = mine_ref[0]
        mine_val = own_ref[...]
        acc = None
        for s, l_ref in enumerate((l0, l1, l2, l3)):
            val = jnp.where(mine == s, mine_val, l_ref[...]).astype(F32)
            acc = val if acc is None else acc + val
        o_ref[...] = acc

    def slot(s):
        return pl.BlockSpec((None, tr, tc), lambda i, j, mr: (jnp.where(mr[0] == s, (s + 1) % N_SHARD, s), i, j))

    own_spec = pl.BlockSpec((None, tr, tc), lambda i, j, mr: (mr[0] if scatter else 0, i, j))
    gs = pltpu.PrefetchScalarGridSpec(num_scalar_prefetch=1, grid=(r // tr, cols // tc), in_specs=[own_spec] + [slot(s) for s in range(N_SHARD)],
                                      out_specs=pl.BlockSpec((tr, tc), lambda i, j, mr: (i, j)))
    return pl.pallas_call(body, out_shape=_sds((r, cols), F32), grid_spec=gs, compiler_params=_params(), name=name)(
        _chip_index(), own, land, land, land, land)


_ANY = pl.BlockSpec(memory_space=pl.ANY)


def xy_exchange(src, *, scatter, name):
    rh = src.shape[1]

    def body(src_ref, land_ref, send_sems, recv_sems, loc_sem):
        x, y, c = lax.axis_index("x"), lax.axis_index("y"), lax.axis_index("c")
        mine = 2 * x + y
        peers = [(1 - x, y), (x, 1 - y), (1 - x, 1 - y)]

        def piece(shard):
            return src_ref.at[shard] if scatter else src_ref.at[c]

        def copy(k, px, py, dst_slot):
            return pltpu.make_async_remote_copy(src_ref=piece(2 * px + py), dst_ref=land_ref.at[dst_slot], send_sem=send_sems.at[k],
                                                recv_sem=recv_sems.at[k], device_id=(px, py, c), device_id_type=MESH)

        keep = pltpu.make_async_copy(piece(mine), land_ref.at[mine], loc_sem)
        keep.start()
        sends = [copy(k, px, py, mine) for k, (px, py) in enumerate(peers)]
        for cp in sends:
            cp.start()
        for cp in sends:
            cp.wait_send()
        for k, (px, py) in enumerate(peers):
            copy(k, px, py, 2 * px + py).wait_recv()
        keep.wait()

    return pl.pallas_call(body, out_shape=_sds((N_SHARD, rh, LANES), src.dtype), in_specs=[_ANY], out_specs=_ANY,
                          scratch_shapes=[pltpu.SemaphoreType.DMA((3,)), pltpu.SemaphoreType.DMA((3,)), pltpu.SemaphoreType.DMA(())],
                          name=name)(src)


def core_exchange(src, *, send_other_half, name):
    def body(src_ref, out_ref, send_sem, recv_sem, loc_sem):
        x, y, c = lax.axis_index("x"), lax.axis_index("y"), lax.axis_index("c")
        if send_other_half:
            cp = pltpu.make_async_remote_copy(src_ref=src_ref.at[1 - c], dst_ref=out_ref, send_sem=send_sem, recv_sem=recv_sem,
                                              device_id=(x, y, 1 - c), device_id_type=MESH)
            cp.start()
            cp.wait_send()
            cp.wait_recv()
        else:
            keep = pltpu.make_async_copy(src_ref, out_ref.at[c], loc_sem)
            keep.start()
            cp = pltpu.make_async_remote_copy(src_ref=src_ref, dst_ref=out_ref.at[c], send_sem=send_sem, recv_sem=recv_sem,
                                              device_id=(x, y, 1 - c), device_id_type=MESH)
            cp.start()
            cp.wait_send()
            pltpu.make_async_remote_copy(src_ref=src_ref, dst_ref=out_ref.at[1 - c], send_sem=send_sem, recv_sem=recv_sem,
                                         device_id=(x, y, 1 - c), device_id_type=MESH).wait_recv()
            keep.wait()

    out_shape = _sds(src.shape[1:], src.dtype) if send_other_half else _sds((2,) + src.shape, src.dtype)
    return pl.pallas_call(body, out_shape=out_shape, in_specs=[_ANY], out_specs=_ANY,
                          scratch_shapes=[pltpu.SemaphoreType.DMA(()), pltpu.SemaphoreType.DMA(()), pltpu.SemaphoreType.DMA(())],
                          name=name)(src)


def _comm_call(body, ins, out_shapes, sem_counts, name):
    return pl.pallas_call(body, out_shape=tuple(out_shapes), in_specs=[_ANY] * len(ins), out_specs=tuple([_ANY] * len(out_shapes)),
                          scratch_shapes=[pltpu.SemaphoreType.DMA((k,)) for k in sem_counts], name=name)(*ins)


def _sequencer_call(body, ins, out_shapes, sem_counts, name, collective_id):
    return pl.kernel(body, out_type=list(out_shapes), mesh=plsc.ScalarSubcoreMesh(axis_name="sequencer", num_cores=1), name=name,
                     scratch_types=[pltpu.SemaphoreType.DMA((k,)) for k in sem_counts],
                     compiler_params=pltpu.CompilerParams(collective_id=collective_id))(*ins)


def _handshake(peers):
    barrier = pltpu.get_barrier_semaphore()
    for peer in peers:
        pl.semaphore_signal(barrier, inc=1, device_id=peer, device_id_type=MESH)
    pl.semaphore_wait(barrier, len(peers))


def _xy_peers(x, y):
    return [(1 - x, y), (x, 1 - y), (1 - x, 1 - y)]


def gather_halves(halves, *, name, collective_id):
    n = len(halves)

    def body(*refs):
        ins, lands, sibs = refs[:n], refs[n:2 * n], refs[2 * n:3 * n]
        ici_send, ici_recv, d2d_send, d2d_recv = refs[3 * n:]
        x, y, c = lax.axis_index("x"), lax.axis_index("y"), lax.axis_index("c")
        mine = 2 * x + y
        peers = _xy_peers(x, y)
        _handshake([(px, py, c) for px, py in peers] + [(x, y, 1 - c)])

        def ici(i, k, slot):
            px, py = peers[k]
            return pltpu.make_async_remote_copy(src_ref=ins[i].at[c], dst_ref=lands[i].at[slot], send_sem=ici_send.at[3 * i + k],
                                                recv_sem=ici_recv.at[3 * i + k], device_id=(px, py, c), device_id_type=MESH)

        def pass_on(i, k):
            px, py = peers[k]
            slot = 2 * px + py
            return pltpu.make_async_remote_copy(src_ref=lands[i].at[slot], dst_ref=sibs[i].at[slot], send_sem=d2d_send.at[3 * i + k],
                                                recv_sem=d2d_recv.at[3 * i + k], device_id=(x, y, 1 - c), device_id_type=MESH)

        sends = [ici(i, k, mine) for i in range(n) for k in range(3)]
        for cp in sends:
            cp.start()
        passed = []
        for i in range(n):
            for k in range(3):
                px, py = peers[k]
                ici(i, k, 2 * px + py).wait_recv()
                cp = pass_on(i, k)
                cp.start()
                passed.append(cp)
        for cp in passed:
            cp.wait_recv()
        for cp in sends + passed:
            cp.wait_send()

    outs = [_sds((N_SHARD,) + h.shape[1:], h.dtype) for h in halves]
    res = _sequencer_call(body, halves, outs + outs, [3 * n] * 4, name, collective_id)
    return res[:n], res[n:]


def send_other_half(arrays, *, name, collective_id):
    n = len(arrays)

    def body(*refs):
        ins, lands = refs[:n], refs[n:2 * n]
        send_sems, recv_sems = refs[2 * n:]
        x, y, c = lax.axis_index("x"), lax.axis_index("y"), lax.axis_index("c")
        _handshake([(x, y, 1 - c)])
        copies = [pltpu.make_async_remote_copy(src_ref=ins[i].at[1 - c], dst_ref=lands[i], send_sem=send_sems.at[i],
                                               recv_sem=recv_sems.at[i], device_id=(x, y, 1 - c), device_id_type=MESH) for i in range(n)]
        for cp in copies:
            cp.start()
        for cp in copies:
            cp.wait_recv()
        for cp in copies:
            cp.wait_send()

    return _sequencer_call(body, arrays, [_sds(a.shape[1:], a.dtype) for a in arrays], [n, n], name, collective_id)


_HBM = pl.BlockSpec(memory_space=pltpu.HBM)
_SEM = pl.BlockSpec(memory_space=pltpu.SEMAPHORE)
_SPLIT_COPY = dict(has_side_effects=pltpu.SideEffectType.DATAFLOW_SIDE_EFFECTING)


def _chip_copy(ins, lands, send_sems, recv_sems, scatter, i, k, receive):
    x, y, c = lax.axis_index("x"), lax.axis_index("y"), lax.axis_index("c")
    px, py = _xy_peers(x, y)[k]
    theirs, mine = 2 * px + py, 2 * x + y
    src = ins[i].at[theirs] if scatter[i] else ins[i].at[0]
    return pltpu.make_async_remote_copy(src_ref=src, dst_ref=lands[i].at[theirs if receive else mine], send_sem=send_sems.at[3 * i + k],
                                        recv_sem=recv_sems.at[3 * i + k], device_id=(px, py, c), device_id_type=MESH)


def send_to_chips_start(arrays, scatter, *, name):
    n = len(arrays)

    def body(*refs):
        send_sems, recv_sems = refs[2 * n], refs[2 * n + 1]
        ins, lands = refs[2 * n + 2:3 * n + 2], refs[3 * n + 2:4 * n + 2]
        token = refs[4 * n + 2]
        for i in range(n):
            for k in range(3):
                _chip_copy(ins, lands, send_sems, recv_sems, scatter, i, k, receive=False).start()
        token[...] = jnp.zeros_like(token)

    land_shapes = [(N_SHARD,) + a.shape[1:] for a in arrays]
    operands = [pltpu.with_memory_space_constraint(a, pltpu.HBM) for a in arrays]
    operands += [pltpu.with_memory_space_constraint(lax.empty(s, a.dtype), pltpu.HBM) for s, a in zip(land_shapes, arrays)]
    out_shape = ([pltpu.SemaphoreType.DMA((3 * n,)), pltpu.SemaphoreType.DMA((3 * n,))] + [pltpu.HBM(a.shape, a.dtype) for a in arrays]
                 + [pltpu.HBM(s, a.dtype) for s, a in zip(land_shapes, arrays)] + [_sds((8, LANES), F32)])
    res = pl.pallas_call(body, name=name, out_shape=out_shape, in_specs=[_HBM] * (2 * n),
                         out_specs=[_SEM, _SEM] + [_HBM] * (2 * n) + [pl.BlockSpec(memory_space=pltpu.VMEM)],
                         input_output_aliases={i: 2 + i for i in range(2 * n)}, compiler_params=pltpu.CompilerParams(**_SPLIT_COPY))(*operands)
    return (res[0], res[1], res[2:2 + n], res[2 + n:2 + 2 * n], scatter), res[-1]


def send_to_chips_wait(state, after, *, name):
    send_sems, recv_sems, arrays, lands, scatter = state
    n = len(arrays)

    def body(*refs):
        ins, landing = refs[:n], refs[n:2 * n]
        send_sems, recv_sems = refs[2 * n], refs[2 * n + 1]
        for i in range(n):
            for k in range(3):
                _chip_copy(ins, landing, send_sems, recv_sems, scatter, i, k, receive=True).wait_recv()
        for i in range(n):
            for k in range(3):
                _chip_copy(ins, landing, send_sems, recv_sems, scatter, i, k, receive=False).wait_send()

    out_shape = [pltpu.HBM(a.shape, a.dtype) for a in list(arrays) + list(lands)]
    res = pl.pallas_call(body, name=name, out_shape=out_shape, in_specs=[_HBM] * (2 * n) + [_SEM, _SEM] + [_ANY] * len(after),
                         out_specs=[_HBM] * (2 * n), input_output_aliases={i: i for i in range(2 * n)},
                         compiler_params=pltpu.CompilerParams(**_SPLIT_COPY))(*arrays, *lands, send_sems, recv_sems, *after)
    return res[:n], res[n:]


def swap_with_other_core(arrays, *, name, collective_id):
    n = len(arrays)

    def body(*refs):
        ins, lands = refs[:n], refs[n:2 * n]
        send_sems, recv_sems = refs[2 * n:]
        x, y, c = lax.axis_index("x"), lax.axis_index("y"), lax.axis_index("c")
        _handshake([(x, y, 1 - c)])
        copies = [pltpu.make_async_remote_copy(src_ref=ins[i], dst_ref=lands[i], send_sem=send_sems.at[i], recv_sem=recv_sems.at[i],
                                               device_id=(x, y, 1 - c), device_id_type=MESH) for i in range(n)]
        for cp in copies:
            cp.start()
        for cp in copies:
            cp.wait_recv()
        for cp in copies:
            cp.wait_send()

    return _sequencer_call(body, arrays, [_sds(a.shape, a.dtype) for a in arrays], [n, n], name, collective_id)


def _pack_rows(n_elems, row_multiple):
    rows = -(-n_elems // LANES)
    return -(-rows // row_multiple) * row_multiple


def _pack(arrays, rows, dtype):
    flat = jnp.concatenate([a.reshape(-1).astype(dtype) for a in arrays])
    return jnp.pad(flat, (0, rows * LANES - flat.shape[0])).reshape(rows, LANES)


def _unpack(packed, shapes):
    flat = packed.reshape(-1)
    out, off = [], 0
    for s in shapes:
        n = int(np.prod(s))
        out.append(flat[off:off + n].reshape(s))
        off += n
    return out


def all_gather_shards(shards, axes, dtype, row_multiple, tag):
    shapes = [s.shape for s in shards]
    rows = _pack_rows(sum(int(np.prod(s)) for s in shapes), row_multiple)
    packed = _pack(shards, rows, dtype).reshape(2, rows // 2, LANES)
    land = xy_exchange(packed, scatter=False, name=f"gather_xy_{tag}")
    both = core_exchange(land, send_other_half=False, name=f"gather_c_{tag}")
    per_shard = jnp.swapaxes(both, 0, 1).reshape(N_SHARD, rows, LANES)
    pieces = [_unpack(per_shard[s], shapes) for s in range(N_SHARD)]
    return [jnp.concatenate([pieces[s][i] for s in range(N_SHARD)], axis=ax) for i, ax in enumerate(axes)]


def _ordered_before(first, then):
    if then is None:
        return first, None
    return lax.optimization_barrier((first, then))


def reduce_between_cores(arrays, scatter, *, tag, collective_id, before=None):
    arrays, before = _ordered_before(arrays, before)
    land = send_other_half(arrays, name=f"reduce_core_send_{tag}", collective_id=collective_id)
    return (arrays, land, scatter, tag, collective_id), before


def reduce_between_chips(state, before=None):
    arrays, land, scatter, tag, collective_id = state
    chip = [add_core_halves(a, l, out_dtype=BF16 if sc else F32, name=f"reduce_core_add_{tag}_{i}")
            for i, (a, l, sc) in enumerate(zip(arrays, land, scatter))]
    sending, token = send_to_chips_start(chip, scatter, name=f"reduce_chip_start_{tag}")
    token, before = _ordered_before(token, before)
    return (sending, token, scatter, tag, collective_id), before


def reduce_finish(state, after):
    sending, token, scatter, tag, collective_id = state
    chip, land = send_to_chips_wait(sending, tuple(after) + (token,), name=f"reduce_chip_wait_{tag}")
    own = [sum_over_chips(ch, l, scatter=sc, name=f"reduce_chip_add_{tag}_{i}") for i, (ch, l, sc) in enumerate(zip(chip, land, scatter))]
    sib = swap_with_other_core(own, name=f"reduce_core_swap_{tag}", collective_id=collective_id + 2)
    return own, sib


def _ffn_layer_fwd(h, norm_g, w_up, cw, cb, w_down, tag):
    u, hn = norm_matmul(h, norm_g, w_up, name=f"ffn_up_{tag}")
    act = ffn_act_fwd(u, cw, cb, name=f"ffn_act_{tag}")
    out = matmul(act, w_down, add=h, name=f"ffn_down_{tag}")
    return out, (h, hn, u, act)


def _travel_layout(array):
    return BIG_ARRAYS[array][3], BIG_ARRAYS[array][4]


def _ffn_layer_bwd(saved, dout, norm_g, w_up, cw, cb, w_down, tag, d_w_down_other=None):
    h, hn, u, act = saved
    dact = matmul(dout, w_down, tb=True, name=f"ffn_down_dx_{tag}")
    d_w_down = matmul(act, dout, ta=True, layer=(int(tag), 2, d_w_down_other), name=f"ffn_down_dw_{tag}")
    du, dcw, dcb = ffn_act_bwd(u, cw, cb, dact, name=f"ffn_act_bwd_{tag}")
    dhn = matmul(du, w_up, tb=True, name=f"ffn_up_dx_{tag}")
    d_w_up = matmul(hn, du, ta=True, split=_travel_layout(f"ffn_w_up_{tag}"), name=f"ffn_up_dw_{tag}")
    dh, dg = norm_bwd(h, norm_g, dhn, dout, name=f"ffn_norm_bwd_{tag}")
    return dh, dg, d_w_up, dcw, dcb, d_w_down


def local_step(x, target, w, stage=lambda name, tensors, grads=None: tensors):
    g = {}
    tables = _ret_tables()
    x = stage("start", x)
    w_in_t = w["ret_gdn_w_in"]
    w_main = w_in_t[:MIX_MAIN]
    w_small = jnp.pad(w_in_t[MIX_MAIN:], ((0, LANES - 2 * N_HEADS), (0, 0)))
    a_log = jnp.pad(w["gdn_a_log"], ((0, 0), (0, LANES - N_HEADS)))
    dt_bias = jnp.pad(w["gdn_dt_bias"], ((0, 0), (0, LANES - N_HEADS)))

    p, hn0 = norm_matmul(x, w["norm_mix"][0:1], w_main, tb=True, name="mix0_in")
    hn0 = stage("normed", hn0)
    small = matmul(hn0, w_small, tb=True, name="mix0_in_small")
    y_ret, s_ret = ret_fwd(p, tables, name="ret_fwd")
    conv = gdn_conv_fwd(p, w["gdn_conv_w"], name="gdn_conv")
    y0, s_gdn = gdn_fwd(conv, p, small, a_log, dt_bias, w["gdn_out_gain"], y_ret, name="gdn_fwd")
    y0 = stage("mixed", y0)
    h1 = matmul(y0, w["ret_gdn_w_out"], add=x, name="mix0_out")
    h2, ffn0 = _ffn_layer_fwd(h1, w["norm_ffn"][0:1], w["ffn_w_up"][0], w["ffn_conv_w"][0], w["ffn_conv_b"][0:1], w["ffn_w_down"][0], "0")
    h2 = stage("layer0", h2)

    gx, hn1 = norm_matmul(h2, w["norm_mix"][1:2], w["lru_w_in"], name="mix1_in")
    lru_p = (w["lru_conv_w"], w["lru_conv_b"], w["lru_w_a"], w["lru_b_a"], w["lru_w_x"], w["lru_b_x"], w["lru_lambda"])
    y1 = lru_fwd(gx, *lru_p, name="lru_fwd")
    h3 = matmul(y1, w["lru_w_out"], add=h2, name="mix1_out")
    h4, ffn1 = _ffn_layer_fwd(h3, w["norm_ffn"][1:2], w["ffn_w_up"][1], w["ffn_conv_w"][1], w["ffn_conv_b"][1:2], w["ffn_w_down"][1], "1")

    loss, dh4, g["norm_final"] = final_fwd_bwd(h4, w["norm_final"], target, name="final")

    dh3, dgf1, dwu1, dcw1, dcb1, dwd1 = _ffn_layer_bwd(ffn1, dh4, w["norm_ffn"][1:2], w["ffn_w_up"][1], w["ffn_conv_w"][1],
                                                     w["ffn_conv_b"][1:2], w["ffn_w_down"][1], "1")
    g["ffn_w_up_1"] = dwu1
    dh3 = stage("grads0_ready", dh3, g)
    dy1 = matmul(dh3, w["lru_w_out"], tb=True, name="mix1_out_dx")
    g["lru_w_out"] = matmul(y1, dh3, ta=True, split=_travel_layout("lru_w_out"), name="mix1_out_dw")
    dgx, g["lru_conv_w"], g["lru_conv_b"], g["lru_w_a"], g["lru_b_a"], g["lru_w_x"], g["lru_b_x"], g["lru_lambda"] = lru_bwd(
        gx, *lru_p, dy1, name="lru_bwd")
    dgx = stage("grads0_send", dgx, g)
    dhn1 = matmul(dgx, w["lru_w_in"], tb=True, name="mix1_in_dx")
    g["lru_w_in"] = matmul(hn1, dgx, ta=True, split=_travel_layout("lru_w_in"), name="mix1_in_dw")
    dh2, dgm1 = norm_bwd(h2, w["norm_mix"][1:2], dhn1, dh3, name="mix1_norm_bwd")
    dh2 = stage("grads1_ready", dh2, g)

    dh1, dgf0, dwu0, dcw0, dcb0, dwd0 = _ffn_layer_bwd(ffn0, dh2, w["norm_ffn"][0:1], w["ffn_w_up"][0], w["ffn_conv_w"][0],
                                                     w["ffn_conv_b"][0:1], w["ffn_w_down"][0], "0", dwd1)
    g["ffn_w_up_0"] = dwu0
    g["ffn_w_down"] = dwd0
    dh1 = stage("grads2_ready", stage("grads1_send", dh1, g), g)
    dy0 = matmul(dh1, w["ret_gdn_w_out"], tb=True, name="mix0_out_dx")
    g["ret_gdn_w_out"] = matmul(y0, dh1, ta=True, split=_travel_layout("ret_gdn_w_out"), name="mix0_out_dw")
    dq_r, dk_r, dv_r, dg_r = ret_bwd(p, tables, s_ret, dy0, name="ret_bwd")
    dy0, dq_r = stage("grads2_send", (dy0, dq_r), g)
    dconv, dg_d, dsmall, dal, ddt, dgain = gdn_bwd(conv, p, small, a_log, dt_bias, w["gdn_out_gain"], s_gdn, dy0, name="gdn_bwd")
    dp_conv, g["gdn_conv_w"] = gdn_conv_bwd(p, w["gdn_conv_w"], dconv, name="gdn_conv_bwd")
    dp = jnp.concatenate([dq_r, dk_r, dv_r, dg_r, dp_conv, dg_d], axis=1)
    dhn0 = matmul(dp, w_main, name="mix0_in_dx")
    dhn0 = matmul(dsmall, w_small, add=dhn0, name="mix0_in_small_dx")
    d_w_in = matmul(dp, hn0, ta=True, column_halves=(MIX_IN, 0, None, None), name="mix0_in_dw")
    d_w_in = matmul(dsmall, hn0, ta=True, column_halves=(MIX_IN, MIX_MAIN, 2 * N_HEADS, d_w_in), name="mix0_in_small_dw")
    g["ret_gdn_w_in"] = d_w_in.reshape(2, N_SHARD, MIX_IN // N_SHARD, D_MODEL // 2)
    dx, dgm0 = norm_bwd(x, w["norm_mix"][0:1], dhn0, dh1, name="mix0_norm_bwd")

    g["gdn_a_log"] = dal[:, :N_HEADS]
    g["gdn_dt_bias"] = ddt[:, :N_HEADS]
    g["gdn_out_gain"] = dgain
    g["norm_mix"] = jnp.concatenate([dgm0, dgm1], axis=0)
    g["norm_ffn"] = jnp.concatenate([dgf0, dgf1], axis=0)
    g["ffn_conv_w"] = jnp.stack([dcw0, dcw1])
    g["ffn_conv_b"] = jnp.concatenate([dcb0, dcb1], axis=0)
    return loss, dx, g


WEIGHTS = ("norm_mix", "norm_ffn", "ret_gdn_w_in", "gdn_conv_w", "gdn_a_log", "gdn_dt_bias", "gdn_out_gain", "ret_gdn_w_out",
           "lru_w_in", "lru_conv_w", "lru_conv_b", "lru_w_a", "lru_b_a", "lru_w_x", "lru_b_x", "lru_lambda", "lru_w_out",
           "ffn_w_up", "ffn_conv_w", "ffn_conv_b", "ffn_w_down", "norm_final")
MATMUL_SHARDED = {"ret_gdn_w_in": 1, "ret_gdn_w_out": 0, "lru_w_in": 1, "lru_w_out": 0, "ffn_w_up": 2, "ffn_w_down": 1}
VECTOR_SHARDED = {"gdn_conv_w": 1, "lru_conv_w": 1, "lru_conv_b": 1, "lru_b_a": 1, "lru_b_x": 1, "lru_lambda": 1, "ffn_conv_w": 2}
SHARDED = {**MATMUL_SHARDED, **VECTOR_SHARDED}
REPLICATED = tuple(n for n in WEIGHTS if n not in SHARDED)
SQUEEZE = {"ret_gdn_w_in", "gdn_conv_w", "ret_gdn_w_out", "lru_w_in", "lru_conv_w", "lru_w_a", "lru_w_x", "lru_w_out"}
MIX_IN = MIX_MAIN + 2 * N_HEADS
BIG_ARRAYS = {
    "ret_gdn_w_in": ("ret_gdn_w_in", None, (MIX_IN, D_MODEL), (N_SHARD, MIX_IN // N_SHARD, 2, D_MODEL // 2), (2, 0, 1, 3)),
    "ret_gdn_w_out": ("ret_gdn_w_out", None, (2 * GROUP, D_MODEL), (N_SHARD, 2, GROUP // N_SHARD, D_MODEL), (1, 0, 2, 3)),
    "lru_w_in": ("lru_w_in", None, (D_MODEL, 2 * D_MODEL), (2, D_MODEL // 2, N_SHARD, 2 * D_MODEL // N_SHARD), (0, 2, 1, 3)),
    "lru_w_out": ("lru_w_out", None, (D_MODEL, D_MODEL), (N_SHARD, 2, D_MODEL // (2 * N_SHARD), D_MODEL), (1, 0, 2, 3)),
    "ffn_w_up_0": ("ffn_w_up", 0, (D_MODEL, 2 * D_FF), (2, D_MODEL // 2, N_SHARD, 2 * D_FF // N_SHARD), (0, 2, 1, 3)),
    "ffn_w_up_1": ("ffn_w_up", 1, (D_MODEL, 2 * D_FF), (2, D_MODEL // 2, N_SHARD, 2 * D_FF // N_SHARD), (0, 2, 1, 3)),
    "ffn_w_down": ("ffn_w_down", None, (2, D_FF, D_MODEL), (2, N_SHARD, D_FF // N_SHARD, D_MODEL), (0, 1, 2, 3)),
}
GATHER_GROUPS = (("ret_gdn_w_in",), ("ret_gdn_w_out", "ffn_w_up_0", "ffn_w_down"), ("lru_w_in", "lru_w_out", "ffn_w_up_1"))
REDUCE_GROUPS = (("ffn_w_up_1",), ("lru_w_in", "lru_w_out"), ("ffn_w_up_0", "ffn_w_down"), ("ret_gdn_w_out", "ret_gdn_w_in"))
BLOCK_WEIGHTS = ("lru_w_a", "lru_w_x")
GATHER_COLLECTIVE_ID = 1
REDUCE_COLLECTIVE_ID = GATHER_COLLECTIVE_ID + len(GATHER_GROUPS)


TRANSPOSED = ("ret_gdn_w_in",)


def _shard_of(array, tensors):
    weight, layer = BIG_ARRAYS[array][:2]
    t = tensors[weight]
    if weight in TRANSPOSED:
        return jnp.swapaxes(t, 1, 2)[0]
    return _local_view(weight, t) if layer is None else t[layer]


def _core_halves(array, shard):
    _, _, _, split, perm = BIG_ARRAYS[array]
    kept = [k for k in range(4) if k != perm[1]]
    order = [kept.index(perm[0]), kept.index(perm[2]), kept.index(perm[3])]
    return shard.reshape([split[k] for k in kept]).transpose(order)


def _local_view(name, a):
    if name in SQUEEZE:
        return a[0]
    if a.ndim == 1:
        return a[None, :]
    return a


def kernel(x, norm_mix, norm_ffn, ret_gdn_w_in, gdn_conv_w, gdn_a_log, gdn_dt_bias, gdn_out_gain, ret_gdn_w_out, lru_w_in, lru_conv_w, lru_conv_b, lru_w_a, lru_b_a, lru_w_x, lru_b_x, lru_lambda, lru_w_out, ffn_w_up, ffn_conv_w, ffn_conv_b, ffn_w_down, norm_final, loss_target, m_norm_mix, m_norm_ffn, m_ret_gdn_w_in, m_gdn_conv_w, m_gdn_a_log, m_gdn_dt_bias, m_gdn_out_gain, m_ret_gdn_w_out, m_lru_w_in, m_lru_conv_w, m_lru_conv_b, m_lru_w_a, m_lru_b_a, m_lru_w_x, m_lru_b_x, m_lru_lambda, m_lru_w_out, m_ffn_w_up, m_ffn_conv_w, m_ffn_conv_b, m_ffn_w_down, m_norm_final, v_norm_mix, v_norm_ffn, v_ret_gdn_w_in, v_gdn_conv_w, v_gdn_a_log, v_gdn_dt_bias, v_gdn_out_gain, v_ret_gdn_w_out, v_lru_w_in, v_lru_conv_w, v_lru_conv_b, v_lru_w_a, v_lru_b_a, v_lru_w_x, v_lru_b_x, v_lru_lambda, v_lru_w_out, v_ffn_w_up, v_ffn_conv_w, v_ffn_conv_b, v_ffn_w_down, v_norm_final):
    given = dict(norm_mix=norm_mix, norm_ffn=norm_ffn, ret_gdn_w_in=ret_gdn_w_in, gdn_conv_w=gdn_conv_w, gdn_a_log=gdn_a_log, gdn_dt_bias=gdn_dt_bias, gdn_out_gain=gdn_out_gain, ret_gdn_w_out=ret_gdn_w_out, lru_w_in=lru_w_in, lru_conv_w=lru_conv_w, lru_conv_b=lru_conv_b, lru_w_a=lru_w_a, lru_b_a=lru_b_a, lru_w_x=lru_w_x, lru_b_x=lru_b_x, lru_lambda=lru_lambda, lru_w_out=lru_w_out, ffn_w_up=ffn_w_up, ffn_conv_w=ffn_conv_w, ffn_conv_b=ffn_conv_b, ffn_w_down=ffn_w_down, norm_final=norm_final)
    mom1 = dict(norm_mix=m_norm_mix, norm_ffn=m_norm_ffn, ret_gdn_w_in=m_ret_gdn_w_in, gdn_conv_w=m_gdn_conv_w, gdn_a_log=m_gdn_a_log, gdn_dt_bias=m_gdn_dt_bias, gdn_out_gain=m_gdn_out_gain, ret_gdn_w_out=m_ret_gdn_w_out, lru_w_in=m_lru_w_in, lru_conv_w=m_lru_conv_w, lru_conv_b=m_lru_conv_b, lru_w_a=m_lru_w_a, lru_b_a=m_lru_b_a, lru_w_x=m_lru_w_x, lru_b_x=m_lru_b_x, lru_lambda=m_lru_lambda, lru_w_out=m_lru_w_out, ffn_w_up=m_ffn_w_up, ffn_conv_w=m_ffn_conv_w, ffn_conv_b=m_ffn_conv_b, ffn_w_down=m_ffn_w_down, norm_final=m_norm_final)
    mom2 = dict(norm_mix=v_norm_mix, norm_ffn=v_norm_ffn, ret_gdn_w_in=v_ret_gdn_w_in, gdn_conv_w=v_gdn_conv_w, gdn_a_log=v_gdn_a_log, gdn_dt_bias=v_gdn_dt_bias, gdn_out_gain=v_gdn_out_gain, ret_gdn_w_out=v_ret_gdn_w_out, lru_w_in=v_lru_w_in, lru_conv_w=v_lru_conv_w, lru_conv_b=v_lru_conv_b, lru_w_a=v_lru_w_a, lru_b_a=v_lru_b_a, lru_w_x=v_lru_w_x, lru_b_x=v_lru_b_x, lru_lambda=v_lru_lambda, lru_w_out=v_lru_w_out, ffn_w_up=v_ffn_w_up, ffn_conv_w=v_ffn_conv_w, ffn_conv_b=v_ffn_conv_b, ffn_w_down=v_ffn_w_down, norm_final=v_norm_final)

    local = {n: _local_view(n, a) for n, a in given.items()}

    core = lax.axis_index("c")
    chip = 2 * lax.axis_index("x") + lax.axis_index("y")
    is_my_chip = lax.broadcasted_iota(jnp.int32, (N_SHARD, 1, 1), 0) == chip

    def by_core(mine, other):
        return jnp.where(core == 0, jnp.stack([mine, other]), jnp.stack([other, mine]))

    vec_names, rp_names = list(VECTOR_SHARDED), list(REPLICATED)
    full = dict(zip(vec_names, all_gather_shards([local[n] for n in vec_names], [SHARDED[n] for n in vec_names], F32, 32, "p")))
    for n in rp_names:
        full[n] = local[n]
    in_flight = {}

    bf16_halves = {}

    def cast_halves(gi):
        if gi not in bf16_halves:
            bf16_halves[gi] = [_core_halves(a, _shard_of(a, given).astype(BF16)) for a in GATHER_GROUPS[gi]]
        return bf16_halves[gi]

    def launch(gi, after=None):
        halves = cast_halves(gi)
        if after is not None:
            halves, after = lax.optimization_barrier((halves, after))
        in_flight[gi] = (halves,) + gather_halves(halves, name=f"gather_weights_{gi}", collective_id=GATHER_COLLECTIVE_ID + gi)
        return after

    def land(gi, after):
        halves, lands, sibs = in_flight[gi]
        (lands, sibs), after = lax.optimization_barrier(((lands, sibs), after))
        for a, mine, got, passed in zip(GATHER_GROUPS[gi], halves, lands, sibs):
            weight, layer, full_shape, split, perm = BIG_ARRAYS[a]
            half_mine = jnp.where(is_my_chip, jnp.where(core == 0, mine[0], mine[1])[None], got)
            half_other = jnp.where(is_my_chip, jnp.where(core == 0, mine[1], mine[0])[None], passed)
            value = by_core(half_mine, half_other).transpose(tuple(np.argsort(perm))).reshape(full_shape)
            if layer is None:
                full[weight] = value
            else:
                full.setdefault(weight, [None, None])[layer] = value
        return after

    reducing = {}

    def reduce_ready(gi, grads, then=None, extra=()):
        def travelling(a):
            split, perm = _travel_layout(a)
            return grads[a] if grads[a].ndim == 4 else grads[a].reshape(split).transpose(perm)

        arrays = [travelling(a) for a in REDUCE_GROUPS[gi]] + list(extra)
        scatter = [True] * len(REDUCE_GROUPS[gi]) + [False] * len(extra)
        reducing[gi], then = reduce_between_cores(arrays, scatter, tag=str(gi), collective_id=REDUCE_COLLECTIVE_ID + 3 * gi, before=then)
        return then

    def reduce_send(gi, then=None):
        reducing[gi], then = reduce_between_chips(reducing[gi], before=then)
        return then

    def stage(name, tensors, grads=None):
        if name == "start":
            launch(0)
            launch(1)
            fillers = (cast_halves(2), [full[n] for n in vec_names])
            (bf16_halves[2], gathered_small), tensors = lax.optimization_barrier((fillers, tensors))
            full.update(zip(vec_names, gathered_small))
            return land(0, tensors)
        if name == "normed":
            return launch(2, tensors)
        if name in ("mixed", "layer0"):
            return land({"mixed": 1, "layer0": 2}[name], tensors)
        gi = int(name[len("grads")])
        return reduce_ready(gi, grads, tensors) if name.endswith("_ready") else reduce_send(gi, tensors)

    small_names = [n for n in rp_names if n not in BLOCK_WEIGHTS] + vec_names

    loss_part, dx, grads = local_step(x[0], loss_target[0], full, stage)
    small_shapes = [grads[n].shape for n in small_names] + [(1, 1)]
    small_rows = _pack_rows(sum(int(np.prod(s)) for s in small_shapes), 16)
    small = _pack([grads[n] for n in small_names] + [loss_part[:, :1]], small_rows, F32).reshape(2, 1, small_rows // 2, LANES)
    last = len(REDUCE_GROUPS) - 1
    halves_of_blocks = [grads[n].reshape(2, 1, LRU_BLOCKS * HEAD // 2, HEAD) for n in BLOCK_WEIGHTS]
    reduce_ready(last, grads, extra=[small] + halves_of_blocks)
    reduce_send(last)
    reduced, result = {}, {}

    def finish(gi, after):
        g_own, g_sib = reduce_finish(reducing[gi], after)
        reduced.update(zip(list(REDUCE_GROUPS[gi]) + ["small"] + list(BLOCK_WEIGHTS), zip(g_own, g_sib)))

    def update(n):
        if n in TRANSPOSED:
            n_rows, n_cols = given[n].shape[2], given[n].shape[1]

            def rows(t):
                return jnp.swapaxes(t, 1, 2).reshape(n_rows, 1, n_cols)

            def back(t):
                return jnp.swapaxes(t.reshape(1, n_rows, n_cols), 1, 2)

            g_rows = jnp.swapaxes(by_core(*reduced[n]), 0, 1).reshape(n_rows, 1, n_cols)
            result[n] = tuple(back(t) for t in adamw_rows(rows(given[n]), g_rows, rows(mom1[n]), rows(mom2[n]), name=f"adamw_{n}"))
            return
        done = None
        for a in (k for k, spec in BIG_ARRAYS.items() if spec[0] == n):
            r, cols = reduced[a][0].shape
            layer = BIG_ARRAYS[a][1] or 0
            w3, m3, v3 = (t if BIG_ARRAYS[a][1] is not None else t.reshape(1, 2 * r, cols) for t in (given[n], mom1[n], mom2[n]))
            done = adamw_halves(w3, m3, v3, *reduced[a], layer=layer, prev=done, name=f"adamw_{a}")
        result[n] = done

    for gi in range(last):
        finish(gi, (dx, reducing[last][1]))
    late = {BIG_ARRAYS[a][0] for a in REDUCE_GROUPS[last]}
    for n in MATMUL_SHARDED:
        if n not in late:
            update(n)
    finish(last, tuple(result[n][0] for n in MATMUL_SHARDED if n not in late))
    for n in MATMUL_SHARDED:
        if n in late:
            update(n)

    for n in BLOCK_WEIGHTS:
        w3, m3, v3 = (t.reshape(1, LRU_BLOCKS * HEAD, HEAD) for t in (given[n], mom1[n], mom2[n]))
        result[n] = adamw_halves(w3, m3, v3, *reduced[n], name=f"adamw_{n}")

    *small_sums, loss_sum = _unpack(by_core(*reduced["small"]).reshape(small_rows, LANES), small_shapes)
    loss = loss_sum[0, 0]
    g_small = dict(zip(small_names, small_sums))
    for n in vec_names:
        size = local[n].shape[SHARDED[n]]
        g_small[n] = lax.dynamic_slice_in_dim(g_small[n], chip * size, size, axis=SHARDED[n])
    views = [[_local_view(n, src[n]) for n in small_names] for src in (given, mom1, mom2)]
    d_s, m_s, v_s = adamw_many(views[0], [g_small[n] for n in small_names], views[1], views[2], name="adamw_small")
    for n, d, nm, nv in zip(small_names, d_s, m_s, v_s):
        result[n] = (g_small[n], d, nm, nv)

    outs = [[result[n][k].reshape(given[n].shape) for n in WEIGHTS] for k in range(4)]
    return (loss, dx[None], *outs[0], *outs[1], *outs[2], *outs[3])
```

```python
import functools

import numpy as np
import jax
import jax.numpy as jnp
from jax import lax
from jax.experimental import pallas as pl
from jax.experimental.pallas import tpu as pltpu
from jax.experimental.pallas import tpu_sc as plsc

F32 = jnp.float32
BF16 = jnp.bfloat16
HI = lax.Precision.HIGHEST
MESH = pl.DeviceIdType.MESH

SEQ = 2048
D_MODEL = 1024
N_HEADS = 4
HEAD = 128
RET_CHUNK = 128
RET_CHUNKS_PER_STEP = 2
GDN_CHUNK = 64
GDN_CHUNKS_PER_STEP = 8
GROUP = N_HEADS * HEAD
MIX_MAIN = 8 * GROUP
D_FF = 2816
LRU_BLOCKS = 8
LRU_C = 8.0
ROPE_BASE = 10000.0
EPS = 1e-6
N_SHARD = 4
LANES = 128

ADAM_LR, ADAM_B1, ADAM_B2, ADAM_EPS, ADAM_WD, ADAM_STEP = 0.001, 0.9, 0.999, 1e-08, 0.01, 10

VMEM_LIMIT_BYTES = 56 * 1024 * 1024

_roll = pltpu.roll


def _params(**kw):
    return pltpu.CompilerParams(vmem_limit_bytes=VMEM_LIMIT_BYTES, **kw)


def _sds(shape, dtype):
    return jax.ShapeDtypeStruct(tuple(shape), dtype)


def _shift_raw(x, d):
    n = x.shape[0]
    t = lax.broadcasted_iota(jnp.int32, x.shape, 0)
    if d > 0:
        return jnp.where(t >= d, _roll(x, d, 0), 0.0)
    return jnp.where(t < n + d, _roll(x, n + d, 0), 0.0)


@functools.partial(jax.custom_vjp, nondiff_argnums=(1,))
def shift_rows(x, d):
    return _shift_raw(x, d)


def _shift_fwd(x, d):
    return _shift_raw(x, d), None


def _shift_bwd(d, _, g):
    return (_shift_raw(g, -d),)


shift_rows.defvjp(_shift_fwd, _shift_bwd)


@jax.custom_vjp
def swap_halves(x):
    return _roll(x, HEAD // 2, 1)


def _swap_fwd(x):
    return _roll(x, HEAD // 2, 1), None


def _swap_bwd(_, g):
    return (_roll(g, HEAD // 2, 1),)


swap_halves.defvjp(_swap_fwd, _swap_bwd)


SCAN_BLOCK_ROWS = 64


def _scan_block(a, u, reverse):
    n = a.shape[0]
    t = lax.broadcasted_iota(jnp.int32, a.shape, 0)
    d = 1
    while d < n:
        if reverse:
            m = t < n - d
            a_s, u_s = _roll(a, n - d, 0), _roll(u, n - d, 0)
        else:
            m = t >= d
            a_s, u_s = _roll(a, d, 0), _roll(u, d, 0)
        u = a * jnp.where(m, u_s, 0.0) + u
        a = a * jnp.where(m, a_s, 1.0)
        d *= 2
    return a, u


def _scan_raw(a, u, reverse):
    n = a.shape[0]
    blocks = range(n // SCAN_BLOCK_ROWS)
    out = [None] * len(blocks)
    entering = None
    for b in (reversed(blocks) if reverse else blocks):
        rows = slice(b * SCAN_BLOCK_ROWS, (b + 1) * SCAN_BLOCK_ROWS)
        a_run, h = _scan_block(a[rows], u[rows], reverse)
        if entering is not None:
            h = a_run * entering + h
        out[b] = h
        entering = h[:1] if reverse else h[SCAN_BLOCK_ROWS - 1:]
    return jnp.concatenate(out, axis=0)


@jax.custom_vjp
def lin_scan(a, u):
    return _scan_raw(a, u, False)


def _lin_scan_fwd(a, u):
    hs = _scan_raw(a, u, False)
    return hs, (a, hs)


def _lin_scan_bwd(res, g):
    a, hs = res
    lam = _scan_raw(_shift_raw(a, -1), g, True)
    return lam * _shift_raw(hs, 1), lam


lin_scan.defvjp(_lin_scan_fwd, _lin_scan_bwd)


def _bdot(a, b, dims=(((1,), (0,)), ((), ()))):
    return lax.dot_general(a.astype(BF16), b.astype(BF16), dims, preferred_element_type=F32)


def _each(f, *seqs):
    return tuple(f(*a) for a in zip(*seqs))


def _split_bf16(a):
    hi = a.astype(BF16)
    return hi, (a - hi.astype(F32)).astype(BF16)


def _dot3_raw(a_s, b_s):
    a_hl = _each(_split_bf16, a_s)
    b_hl = _each(_split_bf16, b_s)
    hh = _each(lambda a, b: _bdot(a[0], b[0]), a_hl, b_hl)
    hl = _each(lambda a, b: _bdot(a[0], b[1]), a_hl, b_hl)
    lh = _each(lambda a, b: _bdot(a[1], b[0]), a_hl, b_hl)
    return _each(lambda x, y, z: x + (y + z), hh, hl, lh)


@jax.custom_vjp
def dot3(a_s, b_s):
    return _dot3_raw(a_s, b_s)


def _dot3_fwd(a_s, b_s):
    return _dot3_raw(a_s, b_s), (a_s, b_s)


def _dot3_bwd(res, g_s):
    a_s, b_s = res
    return (_each(lambda g, b: _bdot(g, b, (((1,), (1,)), ((), ()))), g_s, b_s),
            _each(lambda a, g: _bdot(a, g, (((0,), (0,)), ((), ()))), a_s, g_s))


dot3.defvjp(_dot3_fwd, _dot3_bwd)


def _eye(n):
    i = lax.broadcasted_iota(jnp.int32, (n, n), 0)
    j = lax.broadcasted_iota(jnp.int32, (n, n), 1)
    return (i == j).astype(F32)


def _unit_lower_inverse_raw(lmats):
    n = lmats[0].shape[0]
    eye = _eye(n)
    ps = _each(lambda l: -l, lmats)
    invs = _each(lambda x: eye + x, ps)
    k = 1
    while 2 * k < n:
        ps = _each(lambda p: _bdot(p, p), ps)
        invs = _each(lambda inv, p: inv + _bdot(inv, p), invs, ps)
        k *= 2
    prods = _dot3_raw(lmats, invs)
    resids = _each(lambda inv, pr: eye - inv - pr, invs, prods)
    return _each(lambda inv, r: inv + _bdot(inv, r), invs, resids)


@jax.custom_vjp
def unit_lower_inverse(lmats):
    return _unit_lower_inverse_raw(lmats)


def _uli_fwd(lmats):
    invs = _unit_lower_inverse_raw(lmats)
    return invs, invs


def _uli_bwd(invs, g_s):
    ms = _each(lambda inv, g: _bdot(inv, g, (((0,), (0,)), ((), ()))), invs, g_s)
    return (_each(lambda m, inv: -_bdot(m, inv, (((1,), (1,)), ((), ()))), ms, invs),)


unit_lower_inverse.defvjp(_uli_fwd, _uli_bwd)


def _cumsum_raw(x, reverse):
    n = x.shape[0]
    t = lax.broadcasted_iota(jnp.int32, x.shape, 0)
    d = 1
    while d < n:
        if reverse:
            x = x + jnp.where(t < n - d, _roll(x, n - d, 0), 0.0)
        else:
            x = x + jnp.where(t >= d, _roll(x, d, 0), 0.0)
        d *= 2
    return x


@jax.custom_vjp
def cumsum_rows(x):
    return _cumsum_raw(x, False)


def _cumsum_fwd(x):
    return _cumsum_raw(x, False), None


def _cumsum_bwd(_, g):
    return (_cumsum_raw(g, True),)


cumsum_rows.defvjp(_cumsum_fwd, _cumsum_bwd)


_NT = (((1,), (1,)), ((), ()))
_TN = (((0,), (0,)), ((), ()))


def _softplus(x):
    return jnp.maximum(x, 0.0) + jnp.log1p(jnp.exp(-jnp.abs(x)))


def _expm1_nonpos(x):
    poly = x * (1.0 + x * (0.5 + x * (1.0 / 6 + x * (1.0 / 24 + x * (1.0 / 120 + x * (1.0 / 720))))))
    return jnp.where(x > -0.25, poly, jnp.exp(x) - 1.0)


def _rms(x):
    return x * lax.rsqrt(jnp.mean(x * x, axis=-1, keepdims=True) + EPS)


def _causal_conv(x, w, width):
    y = w[width - 1:width, :] * x
    for j in range(width - 1):
        y = y + w[j:j + 1, :] * shift_rows(x, width - 1 - j)
    return y


def _norm_fn(x, g):
    return _rms(x) * g


def _ffn_act_fn(ug, uv, wg, wv, bg, bv):
    return jax.nn.silu(_causal_conv(ug, wg, 3) + bg) * (_causal_conv(uv, wv, 3) + bv)


def _gdn_conv_fn(x, w):
    return jax.nn.silu(_causal_conv(x, w, 4))


def _lru_fn(gate, x, cw, cb, wa, ba, wx, bx, lam):
    xr = _causal_conv(x, cw, 4) + cb
    r = jax.nn.sigmoid(_bdot(xr, wa) + ba)
    i = jax.nn.sigmoid(_bdot(xr, wx) + bx)
    log_a = -LRU_C * r * _softplus(-lam)
    a = jnp.exp(log_a)
    u = jnp.sqrt(-_expm1_nonpos(2.0 * log_a)) * (i * xr)
    hs = lin_scan(a, u)
    return jax.nn.gelu(gate) * hs


def _ret_fn(qs, ks, vs, gates, states, cos2, sin2, dmasks, ktails, qdecs, cdecs):
    c = RET_CHUNK
    n_heads = len(qs)
    n_chunks = qs[0].shape[0] // c
    units = tuple((ci, h) for ci in range(n_chunks) for h in range(n_heads))

    def rows(x, ci):
        return x[ci * c:(ci + 1) * c]

    qrs = tuple(rows(qs[h], ci) * rows(cos2, ci) + swap_halves(rows(qs[h], ci)) * rows(sin2, ci) for ci, h in units)
    krs = tuple((rows(ks[h], ci) * rows(cos2, ci) + swap_halves(rows(ks[h], ci)) * rows(sin2, ci)) * (HEAD ** -0.5) for ci, h in units)
    vus = tuple(rows(vs[h], ci) for ci, h in units)
    scores = tuple(_bdot(q, k, _NT) * dmasks[h] for q, k, (_, h) in zip(qrs, krs, units))
    intra = _each(lambda sc, v: _bdot(sc, v), scores, vus)
    outs = []
    for ci in range(n_chunks):
        mine = slice(ci * n_heads, (ci + 1) * n_heads)
        inter = _each(lambda q, d, s: _bdot(q * d, s), qrs[mine], qdecs, states)
        outs.append(_each(lambda a, b: a + b, intra[mine], inter))
        states = _each(lambda s, cd, k, kt, v: s * cd + _bdot(k * kt, v, _TN), states, cdecs, krs[mine], ktails, vus[mine])
    ys = tuple(_rms(jnp.concatenate([outs[ci][h] for ci in range(n_chunks)], axis=0)) * jax.nn.silu(gates[h]) for h in range(n_heads))
    return ys, states


def _pick_lane(x, lane_idx):
    lane = lax.broadcasted_iota(jnp.int32, x.shape, 1)
    return jnp.sum(jnp.where(lane == lane_idx, x, 0.0), axis=1, keepdims=True)


def _l2norm(x):
    return x * lax.rsqrt(jnp.sum(x * x, axis=-1, keepdims=True) + EPS)


def _gdn_fn(qcs, kcs, vcs, gates, small, a_log, dt_bias, gain, states):
    c = GDN_CHUNK
    n_heads = len(qcs)
    n_chunks = qcs[0].shape[0] // c
    units = tuple((ci, h) for ci in range(n_chunks) for h in range(n_heads))

    def unit_rows(per_head):
        return tuple(per_head[h][ci * c:(ci + 1) * c] for ci, h in units)

    smalls = tuple(small[ci * c:(ci + 1) * c] for ci, _ in units)
    heads = tuple(h for _, h in units)
    intra = _gdn_intra(unit_rows(qcs), unit_rows(kcs), unit_rows(vcs), smalls, heads, a_log, dt_bias)
    outs = []
    for ci in range(n_chunks):
        mine = slice(ci * n_heads, (ci + 1) * n_heads)
        os_, states = _gdn_inter(*(part[mine] for part in intra), states)
        outs.append(os_)
    ys = tuple(_rms(jnp.concatenate([outs[ci][h] for ci in range(n_chunks)], axis=0)) * gain * jax.nn.silu(gates[h])
               for h in range(n_heads))
    return ys, states


def _gdn_inter(qs, ks, us, ws, attns, gcs, g_lasts, states):
    v_news = _each(lambda u, w, s: u - _bdot(w, s), us, ws, states)
    inter = _each(lambda q, gc, s: _bdot(q * jnp.exp(gc), s), qs, gcs, states)
    os_ = _each(lambda x, a, v: x + _bdot(a, v), inter, attns, v_news)
    new_states = _each(lambda s, gl, k, gc, v: s * jnp.exp(gl) + _bdot(k * jnp.exp(gl - gc), v, _TN), states, g_lasts, ks, gcs, v_news)
    return os_, new_states


def _gdn_intra(qcs, kcs, vcs, smalls, heads, a_log, dt_bias):
    c = GDN_CHUNK
    qs = _each(lambda x: _l2norm(x) * (HEAD ** -0.5), qcs)
    ks = _each(_l2norm, kcs)
    betas = _each(lambda sm, h: jax.nn.sigmoid(_pick_lane(sm, h)), smalls, heads)
    gs = _each(lambda sm, h: -jnp.exp(_pick_lane(a_log, h)) * _softplus(_pick_lane(sm, h + N_HEADS) + _pick_lane(dt_bias, h)),
               smalls, heads)
    i = lax.broadcasted_iota(jnp.int32, (c, c), 0)
    j = lax.broadcasted_iota(jnp.int32, (c, c), 1)
    tril = i >= j
    gcs = _each(lambda g: cumsum_rows(jnp.broadcast_to(g, (c, LANES)))[:, :1], gs)
    gc_rows = _each(lambda gc: jnp.broadcast_to(gc, (c, c)), gcs)
    decays = _each(lambda r: jnp.where(tril, jnp.exp(jnp.where(tril, r - r.T, 0.0)), 0.0), gc_rows)
    kbs = _each(lambda k, b: k * b, ks, betas)
    lmats = _each(lambda kb, k, d: jnp.where(i > j, _bdot(kb, k, _NT) * d, 0.0), kbs, ks, decays)
    attns = _each(lambda q, k, d: jnp.where(tril, _bdot(q, k, _NT) * d, 0.0), qs, ks, decays)
    invs = unit_lower_inverse(lmats)
    us = dot3(invs, _each(lambda v, b: v * b, vcs, betas))
    ws = dot3(invs, _each(lambda kb, gc: kb * jnp.exp(gc), kbs, gcs))
    g_lasts = _each(lambda g: jnp.sum(g, axis=0, keepdims=True), gs)
    return qs, ks, us, ws, attns, gcs, g_lasts


def _final_fn(h, g, target):
    y = _rms(h) * g
    return 0.5 * jnp.sum(jnp.mean(jnp.square(y - target), axis=-1, keepdims=True), axis=0, keepdims=True)


def _tile(n, candidates):
    for t in candidates:
        if n % t == 0:
            return t
    raise ValueError(f"no tile for {n}")


MATMUL_RESIDENT_LHS_BYTES = 8 * 1024 * 1024


def matmul(a, b, *, ta=False, tb=False, add=None, out_dtype=F32, tm=None, tn=None, split=None, layer=None, column_halves=None, name):
    m = a.shape[1] if ta else a.shape[0]
    k = a.shape[0] if ta else a.shape[1]
    n = b.shape[0] if tb else b.shape[1]
    assert k == (b.shape[1] if tb else b.shape[0])
    out_shape, out_block, out_index = (m, n), None, lambda i, j: (i, j)
    if split is not None:
        dims4, perm = split
        out_shape = tuple(dims4[p] for p in perm)
        r, cols = out_shape[2:]
        tm, tn = m, tn or _tile(cols, (1408, 512))
        cb = cols // tn
        if perm == (0, 2, 1, 3):
            out_block, out_index = (2, None, r, tn), lambda i, j: (0, j // cb, 0, j % cb)
        elif perm == (1, 0, 2, 3):
            out_block, out_index = (2, N_SHARD, r, tn), lambda i, j: (0, 0, 0, j)
        else:
            raise ValueError(perm)
    if tm is None and not ta and m * k * a.dtype.itemsize <= MATMUL_RESIDENT_LHS_BYTES:
        tm = m
    tm = tm or _tile(m, (1024, 512, 1408, 256, 128))
    tn = tn or _tile(n, (512, 1408, 256, 128))
    aliases, prev, keep_rows = {}, None, None
    if layer is not None:
        index, count, prev = layer
        out_shape, out_block, out_index = (count, m, n), (None, tm, tn), lambda i, j: (index, i, j)
    if column_halves is not None:
        total_rows, first_row, keep_rows, prev = column_halves
        tn = n // 2
        rows_out = keep_rows or tm
        out_shape, out_block = (2, total_rows, tn), (None, rows_out, tn)
        out_index = lambda i, j: (j, first_row // rows_out + i, 0)
    dims = (((0 if ta else 1,), (1 if tb else 0,)), ((), ()))

    def body(a_ref, b_ref, *rest):
        acc = lax.dot_general(a_ref[...].astype(BF16), b_ref[...].astype(BF16), dims, preferred_element_type=F32)
        if add is not None:
            acc = acc + rest[0][...]
        o_ref = rest[-1]
        acc = acc.astype(out_dtype)
        if split is not None and split[1] == (1, 0, 2, 3):
            rows = o_ref.shape[2]
            for s in range(N_SHARD):
                for h in range(2):
                    o_ref[h, s] = acc[(2 * s + h) * rows:(2 * s + h + 1) * rows]
        elif keep_rows is not None:
            o_ref[...] = acc[:keep_rows]
        else:
            o_ref[...] = acc.reshape(o_ref.shape)

    a_spec = pl.BlockSpec((k, tm), lambda i, j: (0, i)) if ta else pl.BlockSpec((tm, k), lambda i, j: (i, 0))
    b_spec = pl.BlockSpec((tn, k), lambda i, j: (j, 0)) if tb else pl.BlockSpec((k, tn), lambda i, j: (0, j))
    o_spec = pl.BlockSpec(out_block or (tm, tn), out_index)
    in_specs, args = [a_spec, b_spec], [a, b]
    if add is not None:
        in_specs.append(o_spec)
        args.append(add)
    if prev is not None:
        aliases = {len(args): 0}
        in_specs.append(pl.BlockSpec(memory_space=pl.ANY))
        args.append(prev)
    return pl.pallas_call(body, out_shape=_sds(out_shape, out_dtype), grid=(m // tm, n // tn), in_specs=in_specs,
                          out_specs=o_spec, input_output_aliases=aliases, compiler_params=_params(), name=name)(*args)


def norm_matmul(x, g, b, *, tb=False, name):
    t, k = x.shape
    n = b.shape[0] if tb else b.shape[1]
    tn = _tile(n, (512, 1408, 256, 128))
    dims = (((1,), (1 if tb else 0,)), ((), ()))

    def body(x_ref, g_ref, b_ref, o_ref, hn_ref):
        @pl.when(pl.program_id(0) == 0)
        def _():
            hn_ref[...] = _norm_fn(x_ref[...], g_ref[...]).astype(BF16)

        o_ref[...] = lax.dot_general(hn_ref[...], b_ref[...].astype(BF16), dims, preferred_element_type=F32)

    b_spec = pl.BlockSpec((tn, k), lambda j: (j, 0)) if tb else pl.BlockSpec((k, tn), lambda j: (0, j))
    whole = pl.BlockSpec((t, k), lambda j: (0, 0))
    return pl.pallas_call(body, out_shape=(_sds((t, n), F32), _sds((t, k), BF16)), grid=(n // tn,),
                          in_specs=[whole, pl.BlockSpec((1, k), lambda j: (0, 0)), b_spec],
                          out_specs=(pl.BlockSpec((t, tn), lambda j: (0, j)), whole), compiler_params=_params(), name=name)(x, g, b)


ROW_TILE = 256


def norm_bwd(x, g, dy, dres, *, name):
    t, d = x.shape

    def body(x_ref, g_ref, dy_ref, dres_ref, dx_ref, dg_ref):
        _, vjp = jax.vjp(_norm_fn, x_ref[...], g_ref[...])
        dx, dg = vjp(dy_ref[...])
        dx_ref[...] = dx + dres_ref[...]

        @pl.when(pl.program_id(0) == 0)
        def _():
            dg_ref[...] = jnp.zeros_like(dg_ref)

        dg_ref[...] += dg

    row = pl.BlockSpec((ROW_TILE, d), lambda i: (i, 0))
    vec = pl.BlockSpec((1, d), lambda i: (0, 0))
    return pl.pallas_call(body, out_shape=(_sds((t, d), F32), _sds((1, d), F32)), grid=(t // ROW_TILE,),
                          in_specs=[row, vec, row, row], out_specs=(row, vec), compiler_params=_params(), name=name)(x, g, dy, dres)


def final_fwd_bwd(h, g, target, *, name):
    t, d = h.shape

    def body(h_ref, g_ref, t_ref, loss_ref, dh_ref, dg_ref):
        tgt = t_ref[...]
        loss, vjp = jax.vjp(lambda hh, gg: _final_fn(hh, gg, tgt), h_ref[...], g_ref[...])
        dh, dg = vjp(jnp.ones((1, 1), F32))
        dh_ref[...] = dh

        @pl.when(pl.program_id(0) == 0)
        def _():
            dg_ref[...] = jnp.zeros_like(dg_ref)
            loss_ref[...] = jnp.zeros_like(loss_ref)

        dg_ref[...] += dg
        loss_ref[...] += jnp.broadcast_to(loss, loss_ref.shape)

    row = pl.BlockSpec((ROW_TILE, d), lambda i: (i, 0))
    vec = pl.BlockSpec((1, d), lambda i: (0, 0))
    return pl.pallas_call(body, out_shape=(_sds((1, LANES), F32), _sds((t, d), F32), _sds((1, d), F32)), grid=(t // ROW_TILE,),
                          in_specs=[row, vec, row], out_specs=(pl.BlockSpec((1, LANES), lambda i: (0, 0)), row, vec),
                          compiler_params=_params(), name=name)(h, g, target)


FFN_FWD_COLS = 256
FFN_BWD_COLS = 128


def ffn_act_fwd(u, cw, cb, *, name):
    t = u.shape[0]
    w = FFN_FWD_COLS
    nb = D_FF // w

    def body(ug_ref, uv_ref, wg_ref, wv_ref, bg_ref, bv_ref, o_ref):
        o_ref[...] = _ffn_act_fn(ug_ref[...], uv_ref[...], wg_ref[...], wv_ref[...], bg_ref[...], bv_ref[...]).astype(BF16)

    def col(rows, off):
        return pl.BlockSpec((rows, w), lambda j: (0, j + off))

    return pl.pallas_call(body, out_shape=_sds((t, D_FF), BF16), grid=(nb,),
                          in_specs=[col(t, 0), col(t, nb), col(3, 0), col(3, nb), col(1, 0), col(1, nb)],
                          out_specs=col(t, 0), compiler_params=_params(), name=name)(u, u, cw, cw, cb, cb)


def _put_column_blocks(step, n_steps, blocks, dst_ref, width, stage_ref, sems):
    def copies(at):
        slot = at % 2
        return [pltpu.make_async_copy(stage_ref.at[slot, p], dst_ref.at[:, pl.ds(pl.multiple_of((p * n_steps + at) * width, LANES), width)],
                                      sems.at[slot, p]) for p in range(len(blocks))]

    @pl.when(step >= 2)
    def _():
        for cp in copies(step - 2):
            cp.wait()

    for p, value in enumerate(blocks):
        stage_ref[step % 2, p] = value
    for cp in copies(step):
        cp.start()

    @pl.when(step == n_steps - 1)
    def _():
        for cp in copies(step - 1) + copies(step):
            cp.wait()


def ffn_act_bwd(u, cw, cb, da, *, name):
    t = u.shape[0]
    w = FFN_BWD_COLS
    nb = D_FF // w

    def body(ug_ref, uv_ref, wg_ref, wv_ref, bg_ref, bv_ref, da_ref, dug_ref, duv_ref, dwg_ref, dwv_ref, dbg_ref, dbv_ref):
        _, vjp = jax.vjp(_ffn_act_fn, ug_ref[...], uv_ref[...], wg_ref[...], wv_ref[...], bg_ref[...], bv_ref[...])
        dug, duv, dwg, dwv, dbg, dbv = vjp(da_ref[...])
        dug_ref[...] = dug.astype(BF16)
        duv_ref[...] = duv.astype(BF16)
        dwg_ref[...] = dwg
        dwv_ref[...] = dwv
        dbg_ref[...] = dbg
        dbv_ref[...] = dbv

    def col(rows, off):
        return pl.BlockSpec((rows, w), lambda j: (0, j + off))

    outs = pl.pallas_call(
        body, out_shape=(_sds((t, D_FF), BF16), _sds((t, D_FF), BF16), _sds((3, D_FF), F32), _sds((3, D_FF), F32),
                         _sds((1, D_FF), F32), _sds((1, D_FF), F32)),
        grid=(nb,), in_specs=[col(t, 0), col(t, nb), col(3, 0), col(3, nb), col(1, 0), col(1, nb), col(t, 0)],
        out_specs=(col(t, 0), col(t, 0), col(3, 0), col(3, 0), col(1, 0), col(1, 0)), compiler_params=_params(), name=name,
    )(u, u, cw, cw, cb, cb, da)
    dug, duv, dwg, dwv, dbg, dbv = outs
    return jnp.concatenate([dug, duv], axis=1), jnp.concatenate([dwg, dwv], axis=1), jnp.concatenate([dbg, dbv], axis=1)


GDN_CONV_COLS = 256
GDN_CONV_OFF = 4 * GROUP


def gdn_conv_fwd(p, cw, *, name):
    t = p.shape[0]
    w = GDN_CONV_COLS
    nb = 3 * GROUP // w
    off = GDN_CONV_OFF // w

    def body(x_ref, w_ref, o_ref):
        o_ref[...] = _gdn_conv_fn(x_ref[...], w_ref[...])

    return pl.pallas_call(body, out_shape=_sds((t, 3 * GROUP), F32), grid=(nb,),
                          in_specs=[pl.BlockSpec((t, w), lambda j: (0, j + off)), pl.BlockSpec((4, w), lambda j: (0, j))],
                          out_specs=pl.BlockSpec((t, w), lambda j: (0, j)), compiler_params=_params(), name=name)(p, cw)


def gdn_conv_bwd(p, cw, dc, *, name):
    t = p.shape[0]
    w = GDN_CONV_COLS
    nb = 3 * GROUP // w
    off = GDN_CONV_OFF // w

    def body(x_ref, w_ref, dc_ref, dx_ref, dw_ref):
        _, vjp = jax.vjp(_gdn_conv_fn, x_ref[...], w_ref[...])
        dx, dw = vjp(dc_ref[...])
        dx_ref[...] = dx.astype(BF16)
        dw_ref[...] = dw

    blk = pl.BlockSpec((t, w), lambda j: (0, j))
    wblk = pl.BlockSpec((4, w), lambda j: (0, j))
    return pl.pallas_call(body, out_shape=(_sds((t, 3 * GROUP), BF16), _sds((4, 3 * GROUP), F32)), grid=(nb,),
                          in_specs=[pl.BlockSpec((t, w), lambda j: (0, j + off)), wblk, blk], out_specs=(blk, wblk),
                          compiler_params=_params(), name=name)(p, cw, dc)


def _lru_specs(t):
    w = D_MODEL // LRU_BLOCKS
    gate = pl.BlockSpec((t, w), lambda j: (0, j))
    xin = pl.BlockSpec((t, w), lambda j: (0, j + LRU_BLOCKS))
    cw = pl.BlockSpec((4, w), lambda j: (0, j))
    vec = pl.BlockSpec((1, w), lambda j: (0, j))
    mat = pl.BlockSpec((None, w, w), lambda j: (j, 0, 0))
    return gate, xin, cw, vec, mat


def lru_fwd(gx, cw, cb, wa, ba, wx, bx, lam, *, name):
    t = gx.shape[0]
    gate, xin, cws, vec, mat = _lru_specs(t)

    def body(g_ref, x_ref, cw_ref, cb_ref, wa_ref, ba_ref, wx_ref, bx_ref, lam_ref, o_ref):
        o_ref[...] = _lru_fn(g_ref[...], x_ref[...], cw_ref[...], cb_ref[...], wa_ref[...], ba_ref[...], wx_ref[...],
                             bx_ref[...], lam_ref[...]).astype(BF16)

    return pl.pallas_call(body, out_shape=_sds((t, D_MODEL), BF16), grid=(LRU_BLOCKS,),
                          in_specs=[gate, xin, cws, vec, mat, vec, mat, vec, vec], out_specs=gate,
                          compiler_params=_params(), name=name)(gx, gx, cw, cb, wa, ba, wx, bx, lam)


def lru_bwd(gx, cw, cb, wa, ba, wx, bx, lam, dy, *, name):
    t = gx.shape[0]
    gate, xin, cws, vec, mat = _lru_specs(t)

    def body(g_ref, x_ref, cw_ref, cb_ref, wa_ref, ba_ref, wx_ref, bx_ref, lam_ref, dy_ref,
             dgx_ref, dcw_ref, dcb_ref, dwa_ref, dba_ref, dwx_ref, dbx_ref, dlam_ref, stage_ref, sems):
        _, vjp = jax.vjp(_lru_fn, g_ref[...], x_ref[...], cw_ref[...], cb_ref[...], wa_ref[...], ba_ref[...], wx_ref[...],
                         bx_ref[...], lam_ref[...])
        dg, dx, dcw, dcb, dwa, dba, dwx, dbx, dlam = vjp(dy_ref[...])
        _put_column_blocks(pl.program_id(0), LRU_BLOCKS, (dg.astype(BF16), dx.astype(BF16)), dgx_ref, D_MODEL // LRU_BLOCKS, stage_ref, sems)
        dcw_ref[...] = dcw
        dcb_ref[...] = dcb
        dwa_ref[...] = dwa
        dba_ref[...] = dba
        dwx_ref[...] = dwx
        dbx_ref[...] = dbx
        dlam_ref[...] = dlam

    d = D_MODEL
    w = d // LRU_BLOCKS
    out_shape = (_sds((t, 2 * d), BF16), _sds((4, d), F32), _sds((1, d), F32), _sds((LRU_BLOCKS, w, w), F32),
                 _sds((1, d), F32), _sds((LRU_BLOCKS, w, w), F32), _sds((1, d), F32), _sds((1, d), F32))
    return pl.pallas_call(body, out_shape=out_shape, grid=(LRU_BLOCKS,),
                          in_specs=[gate, xin, cws, vec, mat, vec, mat, vec, vec, gate],
                          out_specs=(pl.BlockSpec(memory_space=pl.ANY), cws, vec, mat, vec, mat, vec, vec),
                          scratch_shapes=[pltpu.VMEM((2, 2, t, w), BF16), pltpu.SemaphoreType.DMA((2, 2))],
                          compiler_params=_params(), name=name)(gx, gx, cw, cb, wa, ba, wx, bx, lam, dy)


def _ret_tables():
    half = HEAD // 2
    inv_freq = (np.float32(ROPE_BASE) ** (-np.arange(half, dtype=np.float32) / np.float32(half))).astype(np.float32)
    ang = (np.arange(SEQ, dtype=np.float32)[:, None] * inv_freq[None, :]).astype(np.float64)
    cos2 = np.concatenate([np.cos(ang), np.cos(ang)], axis=1).astype(np.float32)
    sin2 = np.concatenate([-np.sin(ang), np.sin(ang)], axis=1).astype(np.float32)
    c = RET_CHUNK
    log_gamma = np.log1p(-np.exp2(-5.0 - np.arange(N_HEADS, dtype=np.float64)))
    idx = np.arange(c, dtype=np.float64)
    rel = idx[:, None] - idx[None, :]
    dmask = np.where(rel >= 0, np.exp(log_gamma[:, None, None] * np.maximum(rel, 0.0)), 0.0)
    ones = np.ones((N_HEADS, c, HEAD))
    ktail = np.exp(log_gamma[:, None] * (c - 1 - idx))[:, :, None] * ones
    qdec = np.exp(log_gamma[:, None] * (idx + 1.0))[:, :, None] * ones
    cdec = np.exp(log_gamma * c)[:, None, None] * ones
    return tuple(jnp.asarray(a, F32) for a in (cos2, sin2, dmask, ktail, qdec, cdec))


def _ret_specs(rev):
    c = RET_CHUNK * RET_CHUNKS_PER_STEP
    nc = SEQ // c

    def n_of(n):
        return nc - 1 - n if rev else n

    def group(off):
        return pl.BlockSpec((c, GROUP), lambda n: (n_of(n), off))

    tab = pl.BlockSpec((c, HEAD), lambda n: (n_of(n), 0))
    const = pl.BlockSpec((N_HEADS, RET_CHUNK, HEAD), lambda n: (0, 0, 0))
    state = pl.BlockSpec((N_HEADS, None, HEAD, HEAD), lambda n: (0, n_of(n), 0, 0))
    return group, tab, const, state, nc


def _head(h):
    return slice(h * HEAD, (h + 1) * HEAD)


def ret_fwd(p, tables, *, name):
    group, tab, const, state, nc = _ret_specs(False)

    def body(q_ref, k_ref, v_ref, g_ref, cos_ref, sin_ref, dm_ref, kt_ref, qd_ref, cd_ref, y_ref, st_ref, s_scr):
        @pl.when(pl.program_id(0) == 0)
        def _():
            s_scr[...] = jnp.zeros_like(s_scr)

        heads = range(N_HEADS)
        states = tuple(s_scr[h] for h in heads)
        ys, new_states = _ret_fn(*(tuple(r[:, _head(h)] for h in heads) for r in (q_ref, k_ref, v_ref, g_ref)), states,
                                 cos_ref[...], sin_ref[...], *(tuple(r[h] for h in heads) for r in (dm_ref, kt_ref, qd_ref, cd_ref)))
        for h in heads:
            st_ref[h] = states[h]
            y_ref[:, _head(h)] = ys[h].astype(BF16)
            s_scr[h] = new_states[h]

    return pl.pallas_call(
        body, out_shape=(_sds((SEQ, 2 * GROUP), BF16), _sds((N_HEADS, nc, HEAD, HEAD), F32)), grid=(nc,),
        in_specs=[group(0), group(1), group(2), group(3), tab, tab, const, const, const, const],
        out_specs=(group(0), state), scratch_shapes=[pltpu.VMEM((N_HEADS, HEAD, HEAD), F32)], compiler_params=_params(), name=name,
    )(p, p, p, p, *tables)


def ret_bwd(p, tables, states, dy, *, name):
    group, tab, const, state, nc = _ret_specs(True)

    def body(q_ref, k_ref, v_ref, g_ref, cos_ref, sin_ref, dm_ref, kt_ref, qd_ref, cd_ref, st_ref, dy_ref,
             dq_ref, dk_ref, dv_ref, dg_ref, ds_scr):
        @pl.when(pl.program_id(0) == 0)
        def _():
            ds_scr[...] = jnp.zeros_like(ds_scr)

        heads = range(N_HEADS)
        consts = (cos_ref[...], sin_ref[...], *(tuple(r[h] for h in heads) for r in (dm_ref, kt_ref, qd_ref, cd_ref)))
        _, vjp = jax.vjp(lambda *a: _ret_fn(*a, *consts), *(tuple(r[:, _head(h)] for h in heads) for r in (q_ref, k_ref, v_ref, g_ref)),
                         tuple(st_ref[h] for h in heads))
        dqs, dks, dvs, dgs, dss = vjp((tuple(dy_ref[:, _head(h)] for h in heads), tuple(ds_scr[h] for h in heads)))
        for h in heads:
            dq_ref[:, _head(h)] = dqs[h].astype(BF16)
            dk_ref[:, _head(h)] = dks[h].astype(BF16)
            dv_ref[:, _head(h)] = dvs[h].astype(BF16)
            dg_ref[:, _head(h)] = dgs[h].astype(BF16)
            ds_scr[h] = dss[h]

    out = _sds((SEQ, GROUP), BF16)
    return pl.pallas_call(
        body, out_shape=(out, out, out, out), grid=(nc,),
        in_specs=[group(0), group(1), group(2), group(3), tab, tab, const, const, const, const, state, group(0)],
        out_specs=(group(0), group(0), group(0), group(0)), scratch_shapes=[pltpu.VMEM((N_HEADS, HEAD, HEAD), F32)],
        compiler_params=_params(), name=name,
    )(p, p, p, p, *tables, states, dy)


def _gdn_specs(rev):
    c = GDN_CHUNK * GDN_CHUNKS_PER_STEP
    nc = SEQ // c

    def n_of(n):
        return nc - 1 - n if rev else n

    def group(off):
        return pl.BlockSpec((c, GROUP), lambda n: (n_of(n), off))

    small = pl.BlockSpec((c, LANES), lambda n: (n_of(n), 0))
    vec = pl.BlockSpec((1, LANES), lambda n: (0, 0))
    state = pl.BlockSpec((N_HEADS, None, HEAD, HEAD), lambda n: (0, n_of(n), 0, 0))
    qkv = pl.BlockSpec((c, 3 * GROUP), lambda n: (n_of(n), 0))
    return group, small, vec, state, qkv, nc


GDN_GATE_GROUP = 7


def gdn_fwd(conv, p, small, a_log, dt_bias, gain, y_started, *, name):
    group, sm, vec, state, _, nc = _gdn_specs(False)

    def body(q_ref, k_ref, v_ref, g_ref, sm_ref, al_ref, dt_ref, gn_ref, _, y_ref, st_ref, s_scr):
        @pl.when(pl.program_id(0) == 0)
        def _():
            s_scr[...] = jnp.zeros_like(s_scr)

        states = tuple(s_scr[h] for h in range(N_HEADS))
        ys, new_states = _gdn_fn(*(tuple(r[:, _head(h)] for h in range(N_HEADS)) for r in (q_ref, k_ref, v_ref, g_ref)),
                                 sm_ref[...], al_ref[...], dt_ref[...], gn_ref[...], states)
        for h in range(N_HEADS):
            st_ref[h] = states[h]
            y_ref[:, _head(h)] = ys[h].astype(BF16)
            s_scr[h] = new_states[h]

    return pl.pallas_call(
        body, out_shape=(_sds((SEQ, 2 * GROUP), BF16), _sds((N_HEADS, nc, HEAD, HEAD), F32)), grid=(nc,),
        in_specs=[group(0), group(1), group(2), group(GDN_GATE_GROUP), sm, vec, vec, vec, pl.BlockSpec(memory_space=pl.ANY)],
        out_specs=(group(1), state), input_output_aliases={8: 0},
        scratch_shapes=[pltpu.VMEM((N_HEADS, HEAD, HEAD), F32)], compiler_params=_params(), name=name,
    )(conv, conv, conv, p, small, a_log, dt_bias, gain, y_started)


def gdn_bwd(conv, p, small, a_log, dt_bias, gain, states, dy, *, name):
    group, sm, vec, state, qkv, nc = _gdn_specs(True)

    def body(q_ref, k_ref, v_ref, g_ref, sm_ref, al_ref, dt_ref, gn_ref, st_ref, dy_ref,
             dqkv_ref, dg_ref, dsm_ref, dal_ref, ddt_ref, dgn_ref, ds_scr):
        @pl.when(pl.program_id(0) == 0)
        def _():
            ds_scr[...] = jnp.zeros_like(ds_scr)
            dal_ref[...] = jnp.zeros_like(dal_ref)
            ddt_ref[...] = jnp.zeros_like(ddt_ref)
            dgn_ref[...] = jnp.zeros_like(dgn_ref)

        per_head = tuple(tuple(r[:, _head(h)] for h in range(N_HEADS)) for r in (q_ref, k_ref, v_ref, g_ref))
        _, vjp = jax.vjp(_gdn_fn, *per_head, sm_ref[...], al_ref[...], dt_ref[...], gn_ref[...],
                         tuple(st_ref[h] for h in range(N_HEADS)))
        cts = (tuple(dy_ref[:, _head(h)] for h in range(N_HEADS)), tuple(ds_scr[h] for h in range(N_HEADS)))
        dqs, dks, dvs, dgs, dsm, dal, ddt, dgn, dss = vjp(cts)
        for h in range(N_HEADS):
            for part, blocks in enumerate((dqs, dks, dvs)):
                dqkv_ref[:, part * GROUP + h * HEAD:part * GROUP + (h + 1) * HEAD] = blocks[h]
            dg_ref[:, _head(h)] = dgs[h].astype(BF16)
            ds_scr[h] = dss[h]
        dsm_ref[...] = dsm
        dal_ref[...] += dal
        ddt_ref[...] += ddt
        dgn_ref[...] += dgn

    pv = _sds((1, LANES), F32)
    return pl.pallas_call(
        body, out_shape=(_sds((SEQ, 3 * GROUP), F32), _sds((SEQ, GROUP), BF16), _sds((SEQ, LANES), F32), pv, pv, pv), grid=(nc,),
        in_specs=[group(0), group(1), group(2), group(GDN_GATE_GROUP), sm, vec, vec, vec, state, group(1)],
        out_specs=(qkv, group(0), sm, vec, vec, vec), scratch_shapes=[pltpu.VMEM((N_HEADS, HEAD, HEAD), F32)],
        compiler_params=_params(), name=name,
    )(conv, conv, conv, p, small, a_log, dt_bias, gain, states, dy)


ELEMENTWISE_BLOCK_BYTES = 2 * 1024 * 1024


def _row_tile(r, c):
    best = None
    for tr in range(8, r + 1, 8):
        if r % tr == 0 and tr * c * 4 <= ELEMENTWISE_BLOCK_BYTES:
            best = tr
    if best is None:
        raise ValueError(f"no row tile for ({r}, {c})")
    return best


def _tile_2d(r, c):
    if any(r % tr == 0 for tr in range(8, r + 1, 8)):
        return _row_tile(r, c), c
    tc = max(t for t in range(LANES, c + 1, LANES) if c % t == 0 and r * t * 4 <= ELEMENTWISE_BLOCK_BYTES)
    return r, tc


def _core_index():
    return lax.axis_index("c").astype(jnp.int32).reshape(1)


def _chip_index():
    return (2 * lax.axis_index("x") + lax.axis_index("y")).astype(jnp.int32).reshape(1)


def adamw_halves(w, m, v, g_own, g_sib, *, layer=0, prev=None, name):
    n_layers, rows, c = w.shape
    r = rows // 2
    tr = _row_tile(r, c)
    nb = r // tr

    def body(c_ref, w_ref, m_ref, v_ref, own_ref, sib_ref, *rest):
        g_ref, d_ref, nm_ref, nv_ref = rest[-4:]
        gg = jnp.where(pl.program_id(0) == c_ref[0], own_ref[...], sib_ref[...])
        nm = ADAM_B1 * m_ref[...] + (1.0 - ADAM_B1) * gg
        nv = ADAM_B2 * v_ref[...] + (1.0 - ADAM_B2) * jnp.square(gg)
        m_hat = nm / (1.0 - ADAM_B1 ** ADAM_STEP)
        v_hat = nv / (1.0 - ADAM_B2 ** ADAM_STEP)
        g_ref[...] = gg
        d_ref[...] = -ADAM_LR * (m_hat / (jnp.sqrt(v_hat) + ADAM_EPS) + ADAM_WD * w_ref[...])
        nm_ref[...] = nm
        nv_ref[...] = nv

    full = pl.BlockSpec((None, tr, c), lambda h, i, cr: (layer, h * nb + i, 0))
    half = pl.BlockSpec((tr, c), lambda h, i, cr: (i, 0))
    o = _sds((n_layers, rows, c), F32)
    prev = list(prev or ())
    gs = pltpu.PrefetchScalarGridSpec(num_scalar_prefetch=1, grid=(2, nb), in_specs=[full, full, full, half, half] + [_ANY] * len(prev),
                                      out_specs=(full, full, full, full))
    n_fixed = 6
    return pl.pallas_call(body, out_shape=(o, o, o, o), grid_spec=gs, compiler_params=_params(), name=name,
                          input_output_aliases={n_fixed + k: k for k in range(len(prev))})(
        _core_index(), w, m, v, g_own, g_sib, *prev)


ADAMW_ROW_STEPS = 6


def adamw_rows(w, g, m, v, *, name):
    rows, _, cols = w.shape
    tr = rows // ADAMW_ROW_STEPS

    def body(w_ref, g_ref, m_ref, v_ref, g_out_ref, d_ref, nm_ref, nv_ref):
        gg = g_ref[...]
        nm = ADAM_B1 * m_ref[...] + (1.0 - ADAM_B1) * gg
        nv = ADAM_B2 * v_ref[...] + (1.0 - ADAM_B2) * jnp.square(gg)
        m_hat = nm / (1.0 - ADAM_B1 ** ADAM_STEP)
        v_hat = nv / (1.0 - ADAM_B2 ** ADAM_STEP)
        g_out_ref[...] = gg
        d_ref[...] = -ADAM_LR * (m_hat / (jnp.sqrt(v_hat) + ADAM_EPS) + ADAM_WD * w_ref[...])
        nm_ref[...] = nm
        nv_ref[...] = nv

    blk = pl.BlockSpec((tr, 1, cols), lambda i: (i, 0, 0))
    o = _sds(w.shape, F32)
    return pl.pallas_call(body, out_shape=(o, o, o, o), grid=(ADAMW_ROW_STEPS,), in_specs=[blk] * 4, out_specs=(blk, blk, blk, blk),
                          compiler_params=_params(), name=name)(w, g, m, v)


def adamw_many(ws, gs, ms, vs, *, name):
    n = len(ws)

    def body(*refs):
        w_refs, g_refs, m_refs, v_refs, d_refs, nm_refs, nv_refs = (refs[k * n:(k + 1) * n] for k in range(7))
        for i in range(n):
            gg = g_refs[i][...]
            nm = ADAM_B1 * m_refs[i][...] + (1.0 - ADAM_B1) * gg
            nv = ADAM_B2 * v_refs[i][...] + (1.0 - ADAM_B2) * jnp.square(gg)
            m_hat = nm / (1.0 - ADAM_B1 ** ADAM_STEP)
            v_hat = nv / (1.0 - ADAM_B2 ** ADAM_STEP)
            d_refs[i][...] = -ADAM_LR * (m_hat / (jnp.sqrt(v_hat) + ADAM_EPS) + ADAM_WD * w_refs[i][...])
            nm_refs[i][...] = nm
            nv_refs[i][...] = nv

    outs = pl.pallas_call(body, out_shape=[_sds(w.shape, F32) for w in ws] * 3, compiler_params=_params(), name=name)(*ws, *gs, *ms, *vs)
    return outs[:n], outs[n:2 * n], outs[2 * n:]


def add_core_halves(g2, land, *, out_dtype, name):
    _, ns, r, cols = g2.shape
    tr, tc = _tile_2d(r, cols)

    def body(c_ref, a_ref, b_ref, o_ref):
        o_ref[...] = (a_ref[...] + b_ref[...]).astype(out_dtype)

    gs = pltpu.PrefetchScalarGridSpec(
        num_scalar_prefetch=1, grid=(ns, r // tr, cols // tc),
        in_specs=[pl.BlockSpec((None, None, tr, tc), lambda s, i, j, cr: (cr[0], s, i, j)),
                  pl.BlockSpec((None, tr, tc), lambda s, i, j, cr: (s, i, j))],
        out_specs=pl.BlockSpec((None, tr, tc), lambda s, i, j, cr: (s, i, j)))
    return pl.pallas_call(body, out_shape=_sds((ns, r, cols), out_dtype), grid_spec=gs, compiler_params=_params(), name=name)(
        _core_index(), g2, land)


def sum_over_chips(own, land, *, scatter, name):
    _, r, cols = own.shape
    tr, tc = _tile_2d(r, cols)

    def body(mine_ref, own_ref, l0, l1, l2, l3, o_ref):
        mine = mine_ref[0]
        mine_val = own_ref[...]
        acc = None
        for s, l_ref in enumerate((l0, l1, l2, l3)):
            val = jnp.where(mine == s, mine_val, l_ref[...]).astype(F32)
            acc = val if acc is None else acc + val
        o_ref[...] = acc

    def slot(s):
        return pl.BlockSpec((None, tr, tc), lambda i, j, mr: (jnp.where(mr[0] == s, (s + 1) % N_SHARD, s), i, j))

    own_spec = pl.BlockSpec((None, tr, tc), lambda i, j, mr: (mr[0] if scatter else 0, i, j))
    gs = pltpu.PrefetchScalarGridSpec(num_scalar_prefetch=1, grid=(r // tr, cols // tc), in_specs=[own_spec] + [slot(s) for s in range(N_SHARD)],
                                      out_specs=pl.BlockSpec((tr, tc), lambda i, j, mr: (i, j)))
    return pl.pallas_call(body, out_shape=_sds((r, cols), F32), grid_spec=gs, compiler_params=_params(), name=name)(
        _chip_index(), own, land, land, land, land)


_ANY = pl.BlockSpec(memory_space=pl.ANY)


def xy_exchange(src, *, scatter, name):
    rh = src.shape[1]

    def body(src_ref, land_ref, send_sems, recv_sems, loc_sem):
        x, y, c = lax.axis_index("x"), lax.axis_index("y"), lax.axis_index("c")
        mine = 2 * x + y
        peers = [(1 - x, y), (x, 1 - y), (1 - x, 1 - y)]

        def piece(shard):
            return src_ref.at[shard] if scatter else src_ref.at[c]

        def copy(k, px, py, dst_slot):
            return pltpu.make_async_remote_copy(src_ref=piece(2 * px + py), dst_ref=land_ref.at[dst_slot], send_sem=send_sems.at[k],
                                                recv_sem=recv_sems.at[k], device_id=(px, py, c), device_id_type=MESH)

        keep = pltpu.make_async_copy(piece(mine), land_ref.at[mine], loc_sem)
        keep.start()
        sends = [copy(k, px, py, mine) for k, (px, py) in enumerate(peers)]
        for cp in sends:
            cp.start()
        for cp in sends:
            cp.wait_send()
        for k, (px, py) in enumerate(peers):
            copy(k, px, py, 2 * px + py).wait_recv()
        keep.wait()

    return pl.pallas_call(body, out_shape=_sds((N_SHARD, rh, LANES), src.dtype), in_specs=[_ANY], out_specs=_ANY,
                          scratch_shapes=[pltpu.SemaphoreType.DMA((3,)), pltpu.SemaphoreType.DMA((3,)), pltpu.SemaphoreType.DMA(())],
                          name=name)(src)


def core_exchange(src, *, send_other_half, name):
    def body(src_ref, out_ref, send_sem, recv_sem, loc_sem):
        x, y, c = lax.axis_index("x"), lax.axis_index("y"), lax.axis_index("c")
        if send_other_half:
            cp = pltpu.make_async_remote_copy(src_ref=src_ref.at[1 - c], dst_ref=out_ref, send_sem=send_sem, recv_sem=recv_sem,
                                              device_id=(x, y, 1 - c), device_id_type=MESH)
            cp.start()
            cp.wait_send()
            cp.wait_recv()
        else:
            keep = pltpu.make_async_copy(src_ref, out_ref.at[c], loc_sem)
            keep.start()
            cp = pltpu.make_async_remote_copy(src_ref=src_ref, dst_ref=out_ref.at[c], send_sem=send_sem, recv_sem=recv_sem,
                                              device_id=(x, y, 1 - c), device_id_type=MESH)
            cp.start()
            cp.wait_send()
            pltpu.make_async_remote_copy(src_ref=src_ref, dst_ref=out_ref.at[1 - c], send_sem=send_sem, recv_sem=recv_sem,
                                         device_id=(x, y, 1 - c), device_id_type=MESH).wait_recv()
            keep.wait()

    out_shape = _sds(src.shape[1:], src.dtype) if send_other_half else _sds((2,) + src.shape, src.dtype)
    return pl.pallas_call(body, out_shape=out_shape, in_specs=[_ANY], out_specs=_ANY,
                          scratch_shapes=[pltpu.SemaphoreType.DMA(()), pltpu.SemaphoreType.DMA(()), pltpu.SemaphoreType.DMA(())],
                          name=name)(src)


def _comm_call(body, ins, out_shapes, sem_counts, name):
    return pl.pallas_call(body, out_shape=tuple(out_shapes), in_specs=[_ANY] * len(ins), out_specs=tuple([_ANY] * len(out_shapes)),
                          scratch_shapes=[pltpu.SemaphoreType.DMA((k,)) for k in sem_counts], name=name)(*ins)


def _sequencer_call(body, ins, out_shapes, sem_counts, name, collective_id):
    return pl.kernel(body, out_type=list(out_shapes), mesh=plsc.ScalarSubcoreMesh(axis_name="sequencer", num_cores=1), name=name,
                     scratch_types=[pltpu.SemaphoreType.DMA((k,)) for k in sem_counts],
                     compiler_params=pltpu.CompilerParams(collective_id=collective_id))(*ins)


def _handshake(peers):
    barrier = pltpu.get_barrier_semaphore()
    for peer in peers:
        pl.semaphore_signal(barrier, inc=1, device_id=peer, device_id_type=MESH)
    pl.semaphore_wait(barrier, len(peers))


def _xy_peers(x, y):
    return [(1 - x, y), (x, 1 - y), (1 - x, 1 - y)]


def gather_halves(halves, *, name, collective_id):
    n = len(halves)

    def body(*refs):
        ins, lands, sibs = refs[:n], refs[n:2 * n], refs[2 * n:3 * n]
        ici_send, ici_recv, d2d_send, d2d_recv = refs[3 * n:]
        x, y, c = lax.axis_index("x"), lax.axis_index("y"), lax.axis_index("c")
        mine = 2 * x + y
        peers = _xy_peers(x, y)
        _handshake([(px, py, c) for px, py in peers] + [(x, y, 1 - c)])

        def ici(i, k, slot):
            px, py = peers[k]
            return pltpu.make_async_remote_copy(src_ref=ins[i].at[c], dst_ref=lands[i].at[slot], send_sem=ici_send.at[3 * i + k],
                                                recv_sem=ici_recv.at[3 * i + k], device_id=(px, py, c), device_id_type=MESH)

        def pass_on(i, k):
            px, py = peers[k]
            slot = 2 * px + py
            return pltpu.make_async_remote_copy(src_ref=lands[i].at[slot], dst_ref=sibs[i].at[slot], send_sem=d2d_send.at[3 * i + k],
                                                recv_sem=d2d_recv.at[3 * i + k], device_id=(x, y, 1 - c), device_id_type=MESH)

        sends = [ici(i, k, mine) for i in range(n) for k in range(3)]
        for cp in sends:
            cp.start()
        passed = []
        for i in range(n):
            for k in range(3):
                px, py = peers[k]
                ici(i, k, 2 * px + py).wait_recv()
                cp = pass_on(i, k)
                cp.start()
                passed.append(cp)
        for cp in passed:
            cp.wait_recv()
        for cp in sends + passed:
            cp.wait_send()

    outs = [_sds((N_SHARD,) + h.shape[1:], h.dtype) for h in halves]
    res = _sequencer_call(body, halves, outs + outs, [3 * n] * 4, name, collective_id)
    return res[:n], res[n:]


def send_other_half(arrays, *, name, collective_id):
    n = len(arrays)

    def body(*refs):
        ins, lands = refs[:n], refs[n:2 * n]
        send_sems, recv_sems = refs[2 * n:]
        x, y, c = lax.axis_index("x"), lax.axis_index("y"), lax.axis_index("c")
        _handshake([(x, y, 1 - c)])
        copies = [pltpu.make_async_remote_copy(src_ref=ins[i].at[1 - c], dst_ref=lands[i], send_sem=send_sems.at[i],
                                               recv_sem=recv_sems.at[i], device_id=(x, y, 1 - c), device_id_type=MESH) for i in range(n)]
        for cp in copies:
            cp.start()
        for cp in copies:
            cp.wait_recv()
        for cp in copies:
            cp.wait_send()

    return _sequencer_call(body, arrays, [_sds(a.shape[1:], a.dtype) for a in arrays], [n, n], name, collective_id)


_HBM = pl.BlockSpec(memory_space=pltpu.HBM)
_SEM = pl.BlockSpec(memory_space=pltpu.SEMAPHORE)
_SPLIT_COPY = dict(has_side_effects=pltpu.SideEffectType.DATAFLOW_SIDE_EFFECTING)


def _chip_copy(ins, lands, send_sems, recv_sems, scatter, i, k, receive):
    x, y, c = lax.axis_index("x"), lax.axis_index("y"), lax.axis_index("c")
    px, py = _xy_peers(x, y)[k]
    theirs, mine = 2 * px + py, 2 * x + y
    src = ins[i].at[theirs] if scatter[i] else ins[i].at[0]
    return pltpu.make_async_remote_copy(src_ref=src, dst_ref=lands[i].at[theirs if receive else mine], send_sem=send_sems.at[3 * i + k],
                                        recv_sem=recv_sems.at[3 * i + k], device_id=(px, py, c), device_id_type=MESH)


def send_to_chips_start(arrays, scatter, *, name):
    n = len(arrays)

    def body(*refs):
        send_sems, recv_sems = refs[2 * n], refs[2 * n + 1]
        ins, lands = refs[2 * n + 2:3 * n + 2], refs[3 * n + 2:4 * n + 2]
        token = refs[4 * n + 2]
        for i in range(n):
            for k in range(3):
                _chip_copy(ins, lands, send_sems, recv_sems, scatter, i, k, receive=False).start()
        token[...] = jnp.zeros_like(token)

    land_shapes = [(N_SHARD,) + a.shape[1:] for a in arrays]
    operands = [pltpu.with_memory_space_constraint(a, pltpu.HBM) for a in arrays]
    operands += [pltpu.with_memory_space_constraint(lax.empty(s, a.dtype), pltpu.HBM) for s, a in zip(land_shapes, arrays)]
    out_shape = ([pltpu.SemaphoreType.DMA((3 * n,)), pltpu.SemaphoreType.DMA((3 * n,))] + [pltpu.HBM(a.shape, a.dtype) for a in arrays]
                 + [pltpu.HBM(s, a.dtype) for s, a in zip(land_shapes, arrays)] + [_sds((8, LANES), F32)])
    res = pl.pallas_call(body, name=name, out_shape=out_shape, in_specs=[_HBM] * (2 * n),
                         out_specs=[_SEM, _SEM] + [_HBM] * (2 * n) + [pl.BlockSpec(memory_space=pltpu.VMEM)],
                         input_output_aliases={i: 2 + i for i in range(2 * n)}, compiler_params=pltpu.CompilerParams(**_SPLIT_COPY))(*operands)
    return (res[0], res[1], res[2:2 + n], res[2 + n:2 + 2 * n], scatter), res[-1]


def send_to_chips_wait(state, after, *, name):
    send_sems, recv_sems, arrays, lands, scatter = state
    n = len(arrays)

    def body(*refs):
        ins, landing = refs[:n], refs[n:2 * n]
        send_sems, recv_sems = refs[2 * n], refs[2 * n + 1]
        for i in range(n):
            for k in range(3):
                _chip_copy(ins, landing, send_sems, recv_sems, scatter, i, k, receive=True).wait_recv()
        for i in range(n):
            for k in range(3):
                _chip_copy(ins, landing, send_sems, recv_sems, scatter, i, k, receive=False).wait_send()

    out_shape = [pltpu.HBM(a.shape, a.dtype) for a in list(arrays) + list(lands)]
    res = pl.pallas_call(body, name=name, out_shape=out_shape, in_specs=[_HBM] * (2 * n) + [_SEM, _SEM] + [_ANY] * len(after),
                         out_specs=[_HBM] * (2 * n), input_output_aliases={i: i for i in range(2 * n)},
                         compiler_params=pltpu.CompilerParams(**_SPLIT_COPY))(*arrays, *lands, send_sems, recv_sems, *after)
    return res[:n], res[n:]


def swap_with_other_core(arrays, *, name, collective_id):
    n = len(arrays)

    def body(*refs):
        ins, lands = refs[:n], refs[n:2 * n]
        send_sems, recv_sems = refs[2 * n:]
        x, y, c = lax.axis_index("x"), lax.axis_index("y"), lax.axis_index("c")
        _handshake([(x, y, 1 - c)])
        copies = [pltpu.make_async_remote_copy(src_ref=ins[i], dst_ref=lands[i], send_sem=send_sems.at[i], recv_sem=recv_sems.at[i],
                                               device_id=(x, y, 1 - c), device_id_type=MESH) for i in range(n)]
        for cp in copies:
            cp.start()
        for cp in copies:
            cp.wait_recv()
        for cp in copies:
            cp.wait_send()

    return _sequencer_call(body, arrays, [_sds(a.shape, a.dtype) for a in arrays], [n, n], name, collective_id)


def _pack_rows(n_elems, row_multiple):
    rows = -(-n_elems // LANES)
    return -(-rows // row_multiple) * row_multiple


def _pack(arrays, rows, dtype):
    flat = jnp.concatenate([a.reshape(-1).astype(dtype) for a in arrays])
    return jnp.pad(flat, (0, rows * LANES - flat.shape[0])).reshape(rows, LANES)


def _unpack(packed, shapes):
    flat = packed.reshape(-1)
    out, off = [], 0
    for s in shapes:
        n = int(np.prod(s))
        out.append(flat[off:off + n].reshape(s))
        off += n
    return out


def all_gather_shards(shards, axes, dtype, row_multiple, tag):
    shapes = [s.shape for s in shards]
    rows = _pack_rows(sum(int(np.prod(s)) for s in shapes), row_multiple)
    packed = _pack(shards, rows, dtype).reshape(2, rows // 2, LANES)
    land = xy_exchange(packed, scatter=False, name=f"gather_xy_{tag}")
    both = core_exchange(land, send_other_half=False, name=f"gather_c_{tag}")
    per_shard = jnp.swapaxes(both, 0, 1).reshape(N_SHARD, rows, LANES)
    pieces = [_unpack(per_shard[s], shapes) for s in range(N_SHARD)]
    return [jnp.concatenate([pieces[s][i] for s in range(N_SHARD)], axis=ax) for i, ax in enumerate(axes)]


def _ordered_before(first, then):
    if then is None:
        return first, None
    return lax.optimization_barrier((first, then))


def reduce_between_cores(arrays, scatter, *, tag, collective_id, before=None):
    arrays, before = _ordered_before(arrays, before)
    land = send_other_half(arrays, name=f"reduce_core_send_{tag}", collective_id=collective_id)
    return (arrays, land, scatter, tag, collective_id), before


def reduce_between_chips(state, before=None):
    arrays, land, scatter, tag, collective_id = state
    chip = [add_core_halves(a, l, out_dtype=BF16 if sc else F32, name=f"reduce_core_add_{tag}_{i}")
            for i, (a, l, sc) in enumerate(zip(arrays, land, scatter))]
    sending, token = send_to_chips_start(chip, scatter, name=f"reduce_chip_start_{tag}")
    token, before = _ordered_before(token, before)
    return (sending, token, scatter, tag, collective_id), before


def reduce_finish(state, after):
    sending, token, scatter, tag, collective_id = state
    chip, land = send_to_chips_wait(sending, tuple(after) + (token,), name=f"reduce_chip_wait_{tag}")
    own = [sum_over_chips(ch, l, scatter=sc, name=f"reduce_chip_add_{tag}_{i}") for i, (ch, l, sc) in enumerate(zip(chip, land, scatter))]
    sib = swap_with_other_core(own, name=f"reduce_core_swap_{tag}", collective_id=collective_id + 2)
    return own, sib


def _ffn_layer_fwd(h, norm_g, w_up, cw, cb, w_down, tag):
    u, hn = norm_matmul(h, norm_g, w_up, name=f"ffn_up_{tag}")
    act = ffn_act_fwd(u, cw, cb, name=f"ffn_act_{tag}")
    out = matmul(act, w_down, add=h, name=f"ffn_down_{tag}")
    return out, (h, hn, u, act)


def _travel_layout(array):
    return BIG_ARRAYS[array][3], BIG_ARRAYS[array][4]


def _ffn_layer_bwd(saved, dout, norm_g, w_up, cw, cb, w_down, tag):
    h, hn, u, act = saved
    dact = matmul(dout, w_down, tb=True, name=f"ffn_down_dx_{tag}")
    d_w_down = matmul(act, dout, ta=True, split=_travel_layout(f"ffn_w_down_{tag}"), name=f"ffn_down_dw_{tag}")
    du, dcw, dcb = ffn_act_bwd(u, cw, cb, dact, name=f"ffn_act_bwd_{tag}")
    dhn = matmul(du, w_up, tb=True, name=f"ffn_up_dx_{tag}")
    d_w_up = matmul(hn, du, ta=True, split=_travel_layout(f"ffn_w_up_{tag}"), name=f"ffn_up_dw_{tag}")
    dh, dg = norm_bwd(h, norm_g, dhn, dout, name=f"ffn_norm_bwd_{tag}")
    return dh, dg, d_w_up, dcw, dcb, d_w_down


def local_step(x, target, w, stage=lambda name, tensors, grads=None: tensors):
    g = {}
    tables = _ret_tables()
    x = stage("start", x)
    w_in_t = w["ret_gdn_w_in"]
    w_main = w_in_t[:MIX_MAIN]
    w_small = jnp.pad(w_in_t[MIX_MAIN:], ((0, LANES - 2 * N_HEADS), (0, 0)))
    a_log = jnp.pad(w["gdn_a_log"], ((0, 0), (0, LANES - N_HEADS)))
    dt_bias = jnp.pad(w["gdn_dt_bias"], ((0, 0), (0, LANES - N_HEADS)))

    p, hn0 = norm_matmul(x, w["norm_mix"][0:1], w_main, tb=True, name="mix0_in")
    hn0 = stage("normed", hn0)
    small = matmul(hn0, w_small, tb=True, name="mix0_in_small")
    y_ret, s_ret = ret_fwd(p, tables, name="ret_fwd")
    conv = gdn_conv_fwd(p, w["gdn_conv_w"], name="gdn_conv")
    y0, s_gdn = gdn_fwd(conv, p, small, a_log, dt_bias, w["gdn_out_gain"], y_ret, name="gdn_fwd")
    y0 = stage("mixed", y0)
    h1 = matmul(y0, w["ret_gdn_w_out"], add=x, name="mix0_out")
    h2, ffn0 = _ffn_layer_fwd(h1, w["norm_ffn"][0:1], w["ffn_w_up"][0], w["ffn_conv_w"][0], w["ffn_conv_b"][0:1], w["ffn_w_down"][0], "0")
    h2 = stage("layer0", h2)

    gx, hn1 = norm_matmul(h2, w["norm_mix"][1:2], w["lru_w_in"], name="mix1_in")
    lru_p = (w["lru_conv_w"], w["lru_conv_b"], w["lru_w_a"], w["lru_b_a"], w["lru_w_x"], w["lru_b_x"], w["lru_lambda"])
    y1 = lru_fwd(gx, *lru_p, name="lru_fwd")
    h3 = stage("mixed1", matmul(y1, w["lru_w_out"], add=h2, name="mix1_out"))
    h4, ffn1 = _ffn_layer_fwd(h3, w["norm_ffn"][1:2], w["ffn_w_up"][1], w["ffn_conv_w"][1], w["ffn_conv_b"][1:2], w["ffn_w_down"][1], "1")

    loss, dh4, g["norm_final"] = final_fwd_bwd(h4, w["norm_final"], target, name="final")

    dh3, dgf1, dwu1, dcw1, dcb1, dwd1 = _ffn_layer_bwd(ffn1, dh4, w["norm_ffn"][1:2], w["ffn_w_up"][1], w["ffn_conv_w"][1],
                                                     w["ffn_conv_b"][1:2], w["ffn_w_down"][1], "1")
    g["ffn_w_up_1"], g["ffn_w_down_1"] = dwu1, dwd1
    dh3 = stage("grads0_ready", dh3, g)
    dy1 = matmul(dh3, w["lru_w_out"], tb=True, name="mix1_out_dx")
    g["lru_w_out"] = matmul(y1, dh3, ta=True, split=_travel_layout("lru_w_out"), name="mix1_out_dw")
    dgx, g["lru_conv_w"], g["lru_conv_b"], g["lru_w_a"], g["lru_b_a"], g["lru_w_x"], g["lru_b_x"], g["lru_lambda"] = lru_bwd(
        gx, *lru_p, dy1, name="lru_bwd")
    dgx = stage("grads0_send", dgx, g)
    dhn1 = matmul(dgx, w["lru_w_in"], tb=True, name="mix1_in_dx")
    g["lru_w_in"] = matmul(hn1, dgx, ta=True, split=_travel_layout("lru_w_in"), name="mix1_in_dw")
    dh2, dgm1 = norm_bwd(h2, w["norm_mix"][1:2], dhn1, dh3, name="mix1_norm_bwd")
    dh2 = stage("grads1_ready", dh2, g)

    dh1, dgf0, dwu0, dcw0, dcb0, dwd0 = _ffn_layer_bwd(ffn0, dh2, w["norm_ffn"][0:1], w["ffn_w_up"][0], w["ffn_conv_w"][0],
                                                     w["ffn_conv_b"][0:1], w["ffn_w_down"][0], "0")
    g["ffn_w_up_0"], g["ffn_w_down_0"] = dwu0, dwd0
    dh1 = stage("grads2_ready", stage("grads1_send", dh1, g), g)
    dy0 = matmul(dh1, w["ret_gdn_w_out"], tb=True, name="mix0_out_dx")
    g["ret_gdn_w_out"] = matmul(y0, dh1, ta=True, split=_travel_layout("ret_gdn_w_out"), name="mix0_out_dw")
    dq_r, dk_r, dv_r, dg_r = ret_bwd(p, tables, s_ret, dy0, name="ret_bwd")
    dy0, dq_r = stage("grads2_send", (dy0, dq_r), g)
    dconv, dg_d, dsmall, dal, ddt, dgain = gdn_bwd(conv, p, small, a_log, dt_bias, w["gdn_out_gain"], s_gdn, dy0, name="gdn_bwd")
    dp_conv, g["gdn_conv_w"] = gdn_conv_bwd(p, w["gdn_conv_w"], dconv, name="gdn_conv_bwd")
    dp = jnp.concatenate([dq_r, dk_r, dv_r, dg_r, dp_conv, dg_d], axis=1)
    dhn0 = matmul(dp, w_main, name="mix0_in_dx")
    dhn0 = matmul(dsmall, w_small, add=dhn0, name="mix0_in_small_dx")
    d_w_in = matmul(dp, hn0, ta=True, column_halves=(MIX_IN, 0, None, None), name="mix0_in_dw")
    d_w_in = matmul(dsmall, hn0, ta=True, column_halves=(MIX_IN, MIX_MAIN, 2 * N_HEADS, d_w_in), name="mix0_in_small_dw")
    g["ret_gdn_w_in"] = d_w_in.reshape(2, N_SHARD, MIX_IN // N_SHARD, D_MODEL // 2)
    dx, dgm0 = norm_bwd(x, w["norm_mix"][0:1], dhn0, dh1, name="mix0_norm_bwd")

    g["gdn_a_log"] = dal[:, :N_HEADS]
    g["gdn_dt_bias"] = ddt[:, :N_HEADS]
    g["gdn_out_gain"] = dgain
    g["norm_mix"] = jnp.concatenate([dgm0, dgm1], axis=0)
    g["norm_ffn"] = jnp.concatenate([dgf0, dgf1], axis=0)
    g["ffn_conv_w"] = jnp.stack([dcw0, dcw1])
    g["ffn_conv_b"] = jnp.concatenate([dcb0, dcb1], axis=0)
    return loss, dx, g


WEIGHTS = ("norm_mix", "norm_ffn", "ret_gdn_w_in", "gdn_conv_w", "gdn_a_log", "gdn_dt_bias", "gdn_out_gain", "ret_gdn_w_out",
           "lru_w_in", "lru_conv_w", "lru_conv_b", "lru_w_a", "lru_b_a", "lru_w_x", "lru_b_x", "lru_lambda", "lru_w_out",
           "ffn_w_up", "ffn_conv_w", "ffn_conv_b", "ffn_w_down", "norm_final")
MATMUL_SHARDED = {"ret_gdn_w_in": 1, "ret_gdn_w_out": 0, "lru_w_in": 1, "lru_w_out": 0, "ffn_w_up": 2, "ffn_w_down": 1}
VECTOR_SHARDED = {"gdn_conv_w": 1, "lru_conv_w": 1, "lru_conv_b": 1, "lru_b_a": 1, "lru_b_x": 1, "lru_lambda": 1, "ffn_conv_w": 2}
SHARDED = {**MATMUL_SHARDED, **VECTOR_SHARDED}
REPLICATED = tuple(n for n in WEIGHTS if n not in SHARDED)
SQUEEZE = {"ret_gdn_w_in", "gdn_conv_w", "ret_gdn_w_out", "lru_w_in", "lru_conv_w", "lru_w_a", "lru_w_x", "lru_w_out"}
MIX_IN = MIX_MAIN + 2 * N_HEADS
BIG_ARRAYS = {
    "ret_gdn_w_in": ("ret_gdn_w_in", None, (MIX_IN, D_MODEL), (N_SHARD, MIX_IN // N_SHARD, 2, D_MODEL // 2), (2, 0, 1, 3)),
    "ret_gdn_w_out": ("ret_gdn_w_out", None, (2 * GROUP, D_MODEL), (N_SHARD, 2, GROUP // N_SHARD, D_MODEL), (1, 0, 2, 3)),
    "lru_w_in": ("lru_w_in", None, (D_MODEL, 2 * D_MODEL), (2, D_MODEL // 2, N_SHARD, 2 * D_MODEL // N_SHARD), (0, 2, 1, 3)),
    "lru_w_out": ("lru_w_out", None, (D_MODEL, D_MODEL), (N_SHARD, 2, D_MODEL // (2 * N_SHARD), D_MODEL), (1, 0, 2, 3)),
    "ffn_w_up_0": ("ffn_w_up", 0, (D_MODEL, 2 * D_FF), (2, D_MODEL // 2, N_SHARD, 2 * D_FF // N_SHARD), (0, 2, 1, 3)),
    "ffn_w_up_1": ("ffn_w_up", 1, (D_MODEL, 2 * D_FF), (2, D_MODEL // 2, N_SHARD, 2 * D_FF // N_SHARD), (0, 2, 1, 3)),
    "ffn_w_down_0": ("ffn_w_down", 0, (D_FF, D_MODEL), (N_SHARD, 2, D_FF // (2 * N_SHARD), D_MODEL), (1, 0, 2, 3)),
    "ffn_w_down_1": ("ffn_w_down", 1, (D_FF, D_MODEL), (N_SHARD, 2, D_FF // (2 * N_SHARD), D_MODEL), (1, 0, 2, 3)),
}
GATHER_GROUPS = (("ret_gdn_w_in",), ("ret_gdn_w_out", "ffn_w_up_0", "ffn_w_down_0"), ("lru_w_in", "lru_w_out"), ("ffn_w_up_1", "ffn_w_down_1"))
REDUCE_GROUPS = (("ffn_w_up_1", "ffn_w_down_1"), ("lru_w_in", "lru_w_out"), ("ffn_w_up_0", "ffn_w_down_0"), ("ret_gdn_w_out", "ret_gdn_w_in"))
BLOCK_WEIGHTS = ("lru_w_a", "lru_w_x")
GATHER_COLLECTIVE_ID = 1
REDUCE_COLLECTIVE_ID = GATHER_COLLECTIVE_ID + len(GATHER_GROUPS)


TRANSPOSED = ("ret_gdn_w_in",)


def _shard_of(array, tensors):
    weight, layer = BIG_ARRAYS[array][:2]
    t = tensors[weight]
    if weight in TRANSPOSED:
        return jnp.swapaxes(t, 1, 2)[0]
    return _local_view(weight, t) if layer is None else t[layer]


def _core_halves(array, shard):
    _, _, _, split, perm = BIG_ARRAYS[array]
    kept = [k for k in range(4) if k != perm[1]]
    order = [kept.index(perm[0]), kept.index(perm[2]), kept.index(perm[3])]
    return shard.reshape([split[k] for k in kept]).transpose(order)


def _local_view(name, a):
    if name in SQUEEZE:
        return a[0]
    if a.ndim == 1:
        return a[None, :]
    return a


def kernel(x, norm_mix, norm_ffn, ret_gdn_w_in, gdn_conv_w, gdn_a_log, gdn_dt_bias, gdn_out_gain, ret_gdn_w_out, lru_w_in, lru_conv_w, lru_conv_b, lru_w_a, lru_b_a, lru_w_x, lru_b_x, lru_lambda, lru_w_out, ffn_w_up, ffn_conv_w, ffn_conv_b, ffn_w_down, norm_final, loss_target, m_norm_mix, m_norm_ffn, m_ret_gdn_w_in, m_gdn_conv_w, m_gdn_a_log, m_gdn_dt_bias, m_gdn_out_gain, m_ret_gdn_w_out, m_lru_w_in, m_lru_conv_w, m_lru_conv_b, m_lru_w_a, m_lru_b_a, m_lru_w_x, m_lru_b_x, m_lru_lambda, m_lru_w_out, m_ffn_w_up, m_ffn_conv_w, m_ffn_conv_b, m_ffn_w_down, m_norm_final, v_norm_mix, v_norm_ffn, v_ret_gdn_w_in, v_gdn_conv_w, v_gdn_a_log, v_gdn_dt_bias, v_gdn_out_gain, v_ret_gdn_w_out, v_lru_w_in, v_lru_conv_w, v_lru_conv_b, v_lru_w_a, v_lru_b_a, v_lru_w_x, v_lru_b_x, v_lru_lambda, v_lru_w_out, v_ffn_w_up, v_ffn_conv_w, v_ffn_conv_b, v_ffn_w_down, v_norm_final):
    given = dict(norm_mix=norm_mix, norm_ffn=norm_ffn, ret_gdn_w_in=ret_gdn_w_in, gdn_conv_w=gdn_conv_w, gdn_a_log=gdn_a_log, gdn_dt_bias=gdn_dt_bias, gdn_out_gain=gdn_out_gain, ret_gdn_w_out=ret_gdn_w_out, lru_w_in=lru_w_in, lru_conv_w=lru_conv_w, lru_conv_b=lru_conv_b, lru_w_a=lru_w_a, lru_b_a=lru_b_a, lru_w_x=lru_w_x, lru_b_x=lru_b_x, lru_lambda=lru_lambda, lru_w_out=lru_w_out, ffn_w_up=ffn_w_up, ffn_conv_w=ffn_conv_w, ffn_conv_b=ffn_conv_b, ffn_w_down=ffn_w_down, norm_final=norm_final)
    mom1 = dict(norm_mix=m_norm_mix, norm_ffn=m_norm_ffn, ret_gdn_w_in=m_ret_gdn_w_in, gdn_conv_w=m_gdn_conv_w, gdn_a_log=m_gdn_a_log, gdn_dt_bias=m_gdn_dt_bias, gdn_out_gain=m_gdn_out_gain, ret_gdn_w_out=m_ret_gdn_w_out, lru_w_in=m_lru_w_in, lru_conv_w=m_lru_conv_w, lru_conv_b=m_lru_conv_b, lru_w_a=m_lru_w_a, lru_b_a=m_lru_b_a, lru_w_x=m_lru_w_x, lru_b_x=m_lru_b_x, lru_lambda=m_lru_lambda, lru_w_out=m_lru_w_out, ffn_w_up=m_ffn_w_up, ffn_conv_w=m_ffn_conv_w, ffn_conv_b=m_ffn_conv_b, ffn_w_down=m_ffn_w_down, norm_final=m_norm_final)
    mom2 = dict(norm_mix=v_norm_mix, norm_ffn=v_norm_ffn, ret_gdn_w_in=v_ret_gdn_w_in, gdn_conv_w=v_gdn_conv_w, gdn_a_log=v_gdn_a_log, gdn_dt_bias=v_gdn_dt_bias, gdn_out_gain=v_gdn_out_gain, ret_gdn_w_out=v_ret_gdn_w_out, lru_w_in=v_lru_w_in, lru_conv_w=v_lru_conv_w, lru_conv_b=v_lru_conv_b, lru_w_a=v_lru_w_a, lru_b_a=v_lru_b_a, lru_w_x=v_lru_w_x, lru_b_x=v_lru_b_x, lru_lambda=v_lru_lambda, lru_w_out=v_lru_w_out, ffn_w_up=v_ffn_w_up, ffn_conv_w=v_ffn_conv_w, ffn_conv_b=v_ffn_conv_b, ffn_w_down=v_ffn_w_down, norm_final=v_norm_final)

    local = {n: _local_view(n, a) for n, a in given.items()}

    core = lax.axis_index("c")
    chip = 2 * lax.axis_index("x") + lax.axis_index("y")
    is_my_chip = lax.broadcasted_iota(jnp.int32, (N_SHARD, 1, 1), 0) == chip

    def by_core(mine, other):
        return jnp.where(core == 0, jnp.stack([mine, other]), jnp.stack([other, mine]))

    vec_names, rp_names = list(VECTOR_SHARDED), list(REPLICATED)
    full = dict(zip(vec_names, all_gather_shards([local[n] for n in vec_names], [SHARDED[n] for n in vec_names], F32, 32, "p")))
    for n in rp_names:
        full[n] = local[n]
    in_flight = {}

    bf16_halves = {}

    def cast_halves(gi):
        if gi not in bf16_halves:
            bf16_halves[gi] = [_core_halves(a, _shard_of(a, given).astype(BF16)) for a in GATHER_GROUPS[gi]]
        return bf16_halves[gi]

    def launch(gi, after=None):
        halves = cast_halves(gi)
        if after is not None:
            halves, after = lax.optimization_barrier((halves, after))
        in_flight[gi] = (halves,) + gather_halves(halves, name=f"gather_weights_{gi}", collective_id=GATHER_COLLECTIVE_ID + gi)
        return after

    def land(gi, after):
        halves, lands, sibs = in_flight[gi]
        (lands, sibs), after = lax.optimization_barrier(((lands, sibs), after))
        for a, mine, got, passed in zip(GATHER_GROUPS[gi], halves, lands, sibs):
            weight, layer, full_shape, split, perm = BIG_ARRAYS[a]
            half_mine = jnp.where(is_my_chip, jnp.where(core == 0, mine[0], mine[1])[None], got)
            half_other = jnp.where(is_my_chip, jnp.where(core == 0, mine[1], mine[0])[None], passed)
            value = by_core(half_mine, half_other).transpose(tuple(np.argsort(perm))).reshape(full_shape)
            if layer is None:
                full[weight] = value
            else:
                full.setdefault(weight, [None, None])[layer] = value
        return after

    reducing = {}

    def reduce_ready(gi, grads, then=None, extra=()):
        def travelling(a):
            split, perm = _travel_layout(a)
            return grads[a] if grads[a].ndim == 4 else grads[a].reshape(split).transpose(perm)

        arrays = [travelling(a) for a in REDUCE_GROUPS[gi]] + list(extra)
        scatter = [True] * len(REDUCE_GROUPS[gi]) + [False] * len(extra)
        reducing[gi], then = reduce_between_cores(arrays, scatter, tag=str(gi), collective_id=REDUCE_COLLECTIVE_ID + 3 * gi, before=then)
        return then

    def reduce_send(gi, then=None):
        reducing[gi], then = reduce_between_chips(reducing[gi], before=then)
        return then

    def stage(name, tensors, grads=None):
        if name == "start":
            launch(0)
            launch(1)
            fillers = (cast_halves(2), cast_halves(3), [full[n] for n in vec_names])
            (bf16_halves[2], bf16_halves[3], gathered_small), tensors = lax.optimization_barrier((fillers, tensors))
            full.update(zip(vec_names, gathered_small))
            return land(0, tensors)
        if name == "normed":
            return launch(3, launch(2, tensors))
        if name in ("mixed", "layer0", "mixed1"):
            return land({"mixed": 1, "layer0": 2, "mixed1": 3}[name], tensors)
        gi = int(name[len("grads")])
        return reduce_ready(gi, grads, tensors) if name.endswith("_ready") else reduce_send(gi, tensors)

    small_names = [n for n in rp_names if n not in BLOCK_WEIGHTS] + vec_names

    loss_part, dx, grads = local_step(x[0], loss_target[0], full, stage)
    small_shapes = [grads[n].shape for n in small_names] + [(1, 1)]
    small_rows = _pack_rows(sum(int(np.prod(s)) for s in small_shapes), 16)
    small = _pack([grads[n] for n in small_names] + [loss_part[:, :1]], small_rows, F32).reshape(2, 1, small_rows // 2, LANES)
    last = len(REDUCE_GROUPS) - 1
    halves_of_blocks = [grads[n].reshape(2, 1, LRU_BLOCKS * HEAD // 2, HEAD) for n in BLOCK_WEIGHTS]
    reduce_ready(last, grads, extra=[small] + halves_of_blocks)
    reduce_send(last)
    reduced, result = {}, {}

    def finish(gi, after):
        g_own, g_sib = reduce_finish(reducing[gi], after)
        reduced.update(zip(list(REDUCE_GROUPS[gi]) + ["small"] + list(BLOCK_WEIGHTS), zip(g_own, g_sib)))

    def update(n):
        if n in TRANSPOSED:
            n_rows, n_cols = given[n].shape[2], given[n].shape[1]

            def rows(t):
                return jnp.swapaxes(t, 1, 2).reshape(n_rows, 1, n_cols)

            def back(t):
                return jnp.swapaxes(t.reshape(1, n_rows, n_cols), 1, 2)

            g_rows = jnp.swapaxes(by_core(*reduced[n]), 0, 1).reshape(n_rows, 1, n_cols)
            result[n] = tuple(back(t) for t in adamw_rows(rows(given[n]), g_rows, rows(mom1[n]), rows(mom2[n]), name=f"adamw_{n}"))
            return
        done = None
        for a in (k for k, spec in BIG_ARRAYS.items() if spec[0] == n):
            r, cols = reduced[a][0].shape
            layer = BIG_ARRAYS[a][1] or 0
            w3, m3, v3 = (t if BIG_ARRAYS[a][1] is not None else t.reshape(1, 2 * r, cols) for t in (given[n], mom1[n], mom2[n]))
            done = adamw_halves(w3, m3, v3, *reduced[a], layer=layer, prev=done, name=f"adamw_{a}")
        result[n] = done

    for gi in range(last):
        finish(gi, (dx, reducing[last][1]))
    late = {BIG_ARRAYS[a][0] for a in REDUCE_GROUPS[last]}
    for n in MATMUL_SHARDED:
        if n not in late:
            update(n)
    finish(last, tuple(result[n][0] for n in MATMUL_SHARDED if n not in late))
    for n in MATMUL_SHARDED:
        if n in late:
            update(n)

    for n in BLOCK_WEIGHTS:
        w3, m3, v3 = (t.reshape(1, LRU_BLOCKS * HEAD, HEAD) for t in (given[n], mom1[n], mom2[n]))
        result[n] = adamw_halves(w3, m3, v3, *reduced[n], name=f"adamw_{n}")

    *small_sums, loss_sum = _unpack(by_core(*reduced["small"]).reshape(small_rows, LANES), small_shapes)
    loss = loss_sum[0, 0]
    g_small = dict(zip(small_names, small_sums))
    for n in vec_names:
        size = local[n].shape[SHARDED[n]]
        g_small[n] = lax.dynamic_slice_in_dim(g_small[n], chip * size, size, axis=SHARDED[n])
    views = [[_local_view(n, src[n]) for n in small_names] for src in (given, mom1, mom2)]
    d_s, m_s, v_s = adamw_many(views[0], [g_small[n] for n in small_names], views[1], views[2], name="adamw_small")
    for n, d, nm, nv in zip(small_names, d_s, m_s, v_s):
        result[n] = (g_small[n], d, nm, nv)

    outs = [[result[n][k].reshape(given[n].shape) for n in WEIGHTS] for k in range(4)]
    return (loss, dx[None], *outs[0], *outs[1], *outs[2], *outs[3])
```

```python
import functools

import numpy as np
import jax
import jax.numpy as jnp
from jax import lax
from jax.experimental import pallas as pl
from jax.experimental.pallas import tpu as pltpu
from jax.experimental.pallas import tpu_sc as plsc

F32 = jnp.float32
BF16 = jnp.bfloat16
HI = lax.Precision.HIGHEST
MESH = pl.DeviceIdType.MESH

SEQ = 2048
D_MODEL = 1024
N_HEADS = 4
HEAD = 128
RET_CHUNK = 128
RET_CHUNKS_PER_STEP = 2
GDN_CHUNK = 64
GDN_CHUNKS_PER_STEP = 8
GROUP = N_HEADS * HEAD
MIX_MAIN = 8 * GROUP
D_FF = 2816
LRU_BLOCKS = 8
LRU_C = 8.0
ROPE_BASE = 10000.0
EPS = 1e-6
N_SHARD = 4
LANES = 128

ADAM_LR, ADAM_B1, ADAM_B2, ADAM_EPS, ADAM_WD, ADAM_STEP = 0.001, 0.9, 0.999, 1e-08, 0.01, 10

VMEM_LIMIT_BYTES = 56 * 1024 * 1024

_roll = pltpu.roll


def _params(**kw):
    return pltpu.CompilerParams(vmem_limit_bytes=VMEM_LIMIT_BYTES, **kw)


def _sds(shape, dtype):
    return jax.ShapeDtypeStruct(tuple(shape), dtype)


def _shift_raw(x, d):
    n = x.shape[0]
    t = lax.broadcasted_iota(jnp.int32, x.shape, 0)
    if d > 0:
        return jnp.where(t >= d, _roll(x, d, 0), 0.0)
    return jnp.where(t < n + d, _roll(x, n + d, 0), 0.0)


@functools.partial(jax.custom_vjp, nondiff_argnums=(1,))
def shift_rows(x, d):
    return _shift_raw(x, d)


def _shift_fwd(x, d):
    return _shift_raw(x, d), None


def _shift_bwd(d, _, g):
    return (_shift_raw(g, -d),)


shift_rows.defvjp(_shift_fwd, _shift_bwd)


@jax.custom_vjp
def swap_halves(x):
    return _roll(x, HEAD // 2, 1)


def _swap_fwd(x):
    return _roll(x, HEAD // 2, 1), None


def _swap_bwd(_, g):
    return (_roll(g, HEAD // 2, 1),)


swap_halves.defvjp(_swap_fwd, _swap_bwd)


SCAN_BLOCK_ROWS = 64


def _scan_block(a, u, reverse):
    n = a.shape[0]
    t = lax.broadcasted_iota(jnp.int32, a.shape, 0)
    d = 1
    while d < n:
        if reverse:
            m = t < n - d
            a_s, u_s = _roll(a, n - d, 0), _roll(u, n - d, 0)
        else:
            m = t >= d
            a_s, u_s = _roll(a, d, 0), _roll(u, d, 0)
        u = a * jnp.where(m, u_s, 0.0) + u
        a = a * jnp.where(m, a_s, 1.0)
        d *= 2
    return a, u


def _scan_raw(a, u, reverse):
    n = a.shape[0]
    blocks = range(n // SCAN_BLOCK_ROWS)
    out = [None] * len(blocks)
    entering = None
    for b in (reversed(blocks) if reverse else blocks):
        rows = slice(b * SCAN_BLOCK_ROWS, (b + 1) * SCAN_BLOCK_ROWS)
        a_run, h = _scan_block(a[rows], u[rows], reverse)
        if entering is not None:
            h = a_run * entering + h
        out[b] = h
        entering = h[:1] if reverse else h[SCAN_BLOCK_ROWS - 1:]
    return jnp.concatenate(out, axis=0)


@jax.custom_vjp
def lin_scan(a, u):
    return _scan_raw(a, u, False)


def _lin_scan_fwd(a, u):
    hs = _scan_raw(a, u, False)
    return hs, (a, hs)


def _lin_scan_bwd(res, g):
    a, hs = res
    lam = _scan_raw(_shift_raw(a, -1), g, True)
    return lam * _shift_raw(hs, 1), lam


lin_scan.defvjp(_lin_scan_fwd, _lin_scan_bwd)


def _bdot(a, b, dims=(((1,), (0,)), ((), ()))):
    return lax.dot_general(a.astype(BF16), b.astype(BF16), dims, preferred_element_type=F32)


def _each(f, *seqs):
    return tuple(f(*a) for a in zip(*seqs))


def _split_bf16(a):
    hi = a.astype(BF16)
    return hi, (a - hi.astype(F32)).astype(BF16)


def _dot3_raw(a_s, b_s):
    a_hl = _each(_split_bf16, a_s)
    b_hl = _each(_split_bf16, b_s)
    hh = _each(lambda a, b: _bdot(a[0], b[0]), a_hl, b_hl)
    hl = _each(lambda a, b: _bdot(a[0], b[1]), a_hl, b_hl)
    lh = _each(lambda a, b: _bdot(a[1], b[0]), a_hl, b_hl)
    return _each(lambda x, y, z: x + (y + z), hh, hl, lh)


@jax.custom_vjp
def dot3(a_s, b_s):
    return _dot3_raw(a_s, b_s)


def _dot3_fwd(a_s, b_s):
    return _dot3_raw(a_s, b_s), (a_s, b_s)


def _dot3_bwd(res, g_s):
    a_s, b_s = res
    return (_each(lambda g, b: _bdot(g, b, (((1,), (1,)), ((), ()))), g_s, b_s),
            _each(lambda a, g: _bdot(a, g, (((0,), (0,)), ((), ()))), a_s, g_s))


dot3.defvjp(_dot3_fwd, _dot3_bwd)


def _eye(n):
    i = lax.broadcasted_iota(jnp.int32, (n, n), 0)
    j = lax.broadcasted_iota(jnp.int32, (n, n), 1)
    return (i == j).astype(F32)


def _unit_lower_inverse_raw(lmats):
    n = lmats[0].shape[0]
    eye = _eye(n)
    ps = _each(lambda l: -l, lmats)
    invs = _each(lambda x: eye + x, ps)
    k = 1
    while 2 * k < n:
        ps = _each(lambda p: _bdot(p, p), ps)
        invs = _each(lambda inv, p: inv + _bdot(inv, p), invs, ps)
        k *= 2
    prods = _dot3_raw(lmats, invs)
    resids = _each(lambda inv, pr: eye - inv - pr, invs, prods)
    return _each(lambda inv, r: inv + _bdot(inv, r), invs, resids)


@jax.custom_vjp
def unit_lower_inverse(lmats):
    return _unit_lower_inverse_raw(lmats)


def _uli_fwd(lmats):
    invs = _unit_lower_inverse_raw(lmats)
    return invs, invs


def _uli_bwd(invs, g_s):
    ms = _each(lambda inv, g: _bdot(inv, g, (((0,), (0,)), ((), ()))), invs, g_s)
    return (_each(lambda m, inv: -_bdot(m, inv, (((1,), (1,)), ((), ()))), ms, invs),)


unit_lower_inverse.defvjp(_uli_fwd, _uli_bwd)


def _cumsum_raw(x, reverse):
    n = x.shape[0]
    t = lax.broadcasted_iota(jnp.int32, x.shape, 0)
    d = 1
    while d < n:
        if reverse:
            x = x + jnp.where(t < n - d, _roll(x, n - d, 0), 0.0)
        else:
            x = x + jnp.where(t >= d, _roll(x, d, 0), 0.0)
        d *= 2
    return x


@jax.custom_vjp
def cumsum_rows(x):
    return _cumsum_raw(x, False)


def _cumsum_fwd(x):
    return _cumsum_raw(x, False), None


def _cumsum_bwd(_, g):
    return (_cumsum_raw(g, True),)


cumsum_rows.defvjp(_cumsum_fwd, _cumsum_bwd)


_NT = (((1,), (1,)), ((), ()))
_TN = (((0,), (0,)), ((), ()))


def _softplus(x):
    return jnp.maximum(x, 0.0) + jnp.log1p(jnp.exp(-jnp.abs(x)))


def _expm1_nonpos(x):
    poly = x * (1.0 + x * (0.5 + x * (1.0 / 6 + x * (1.0 / 24 + x * (1.0 / 120 + x * (1.0 / 720))))))
    return jnp.where(x > -0.25, poly, jnp.exp(x) - 1.0)


def _rms(x):
    return x * lax.rsqrt(jnp.mean(x * x, axis=-1, keepdims=True) + EPS)


def _causal_conv(x, w, width):
    y = w[width - 1:width, :] * x
    for j in range(width - 1):
        y = y + w[j:j + 1, :] * shift_rows(x, width - 1 - j)
    return y


def _norm_fn(x, g):
    return _rms(x) * g


def _ffn_act_fn(ug, uv, wg, wv, bg, bv):
    return jax.nn.silu(_causal_conv(ug, wg, 3) + bg) * (_causal_conv(uv, wv, 3) + bv)


def _gdn_conv_fn(x, w):
    return jax.nn.silu(_causal_conv(x, w, 4))


def _lru_fn(gate, x, cw, cb, wa, ba, wx, bx, lam):
    xr = _causal_conv(x, cw, 4) + cb
    r = jax.nn.sigmoid(_bdot(xr, wa) + ba)
    i = jax.nn.sigmoid(_bdot(xr, wx) + bx)
    log_a = -LRU_C * r * _softplus(-lam)
    a = jnp.exp(log_a)
    u = jnp.sqrt(-_expm1_nonpos(2.0 * log_a)) * (i * xr)
    hs = lin_scan(a, u)
    return jax.nn.gelu(gate) * hs


def _ret_fn(qs, ks, vs, gates, states, cos2, sin2, dmasks, ktails, qdecs, cdecs):
    c = RET_CHUNK
    n_heads = len(qs)
    n_chunks = qs[0].shape[0] // c
    units = tuple((ci, h) for ci in range(n_chunks) for h in range(n_heads))

    def rows(x, ci):
        return x[ci * c:(ci + 1) * c]

    qrs = tuple(rows(qs[h], ci) * rows(cos2, ci) + swap_halves(rows(qs[h], ci)) * rows(sin2, ci) for ci, h in units)
    krs = tuple((rows(ks[h], ci) * rows(cos2, ci) + swap_halves(rows(ks[h], ci)) * rows(sin2, ci)) * (HEAD ** -0.5) for ci, h in units)
    vus = tuple(rows(vs[h], ci) for ci, h in units)
    scores = tuple(_bdot(q, k, _NT) * dmasks[h] for q, k, (_, h) in zip(qrs, krs, units))
    intra = _each(lambda sc, v: _bdot(sc, v), scores, vus)
    outs = []
    for ci in range(n_chunks):
        mine = slice(ci * n_heads, (ci + 1) * n_heads)
        inter = _each(lambda q, d, s: _bdot(q * d, s), qrs[mine], qdecs, states)
        outs.append(_each(lambda a, b: a + b, intra[mine], inter))
        states = _each(lambda s, cd, k, kt, v: s * cd + _bdot(k * kt, v, _TN), states, cdecs, krs[mine], ktails, vus[mine])
    ys = tuple(_rms(jnp.concatenate([outs[ci][h] for ci in range(n_chunks)], axis=0)) * jax.nn.silu(gates[h]) for h in range(n_heads))
    return ys, states


def _pick_lane(x, lane_idx):
    lane = lax.broadcasted_iota(jnp.int32, x.shape, 1)
    return jnp.sum(jnp.where(lane == lane_idx, x, 0.0), axis=1, keepdims=True)


def _l2norm(x):
    return x * lax.rsqrt(jnp.sum(x * x, axis=-1, keepdims=True) + EPS)


def _gdn_fn(qcs, kcs, vcs, gates, small, a_log, dt_bias, gain, states):
    c = GDN_CHUNK
    n_heads = len(qcs)
    n_chunks = qcs[0].shape[0] // c
    units = tuple((ci, h) for ci in range(n_chunks) for h in range(n_heads))

    def unit_rows(per_head):
        return tuple(per_head[h][ci * c:(ci + 1) * c] for ci, h in units)

    smalls = tuple(small[ci * c:(ci + 1) * c] for ci, _ in units)
    heads = tuple(h for _, h in units)
    intra = _gdn_intra(unit_rows(qcs), unit_rows(kcs), unit_rows(vcs), smalls, heads, a_log, dt_bias)
    outs = []
    for ci in range(n_chunks):
        mine = slice(ci * n_heads, (ci + 1) * n_heads)
        os_, states = _gdn_inter(*(part[mine] for part in intra), states)
        outs.append(os_)
    ys = tuple(_rms(jnp.concatenate([outs[ci][h] for ci in range(n_chunks)], axis=0)) * gain * jax.nn.silu(gates[h])
               for h in range(n_heads))
    return ys, states


def _gdn_inter(qs, ks, us, ws, attns, gcs, g_lasts, states):
    v_news = _each(lambda u, w, s: u - _bdot(w, s), us, ws, states)
    inter = _each(lambda q, gc, s: _bdot(q * jnp.exp(gc), s), qs, gcs, states)
    os_ = _each(lambda x, a, v: x + _bdot(a, v), inter, attns, v_news)
    new_states = _each(lambda s, gl, k, gc, v: s * jnp.exp(gl) + _bdot(k * jnp.exp(gl - gc), v, _TN), states, g_lasts, ks, gcs, v_news)
    return os_, new_states


def _gdn_intra(qcs, kcs, vcs, smalls, heads, a_log, dt_bias):
    c = GDN_CHUNK
    qs = _each(lambda x: _l2norm(x) * (HEAD ** -0.5), qcs)
    ks = _each(_l2norm, kcs)
    betas = _each(lambda sm, h: jax.nn.sigmoid(_pick_lane(sm, h)), smalls, heads)
    gs = _each(lambda sm, h: -jnp.exp(_pick_lane(a_log, h)) * _softplus(_pick_lane(sm, h + N_HEADS) + _pick_lane(dt_bias, h)),
               smalls, heads)
    i = lax.broadcasted_iota(jnp.int32, (c, c), 0)
    j = lax.broadcasted_iota(jnp.int32, (c, c), 1)
    tril = i >= j
    gcs = _each(lambda g: cumsum_rows(jnp.broadcast_to(g, (c, LANES)))[:, :1], gs)
    gc_rows = _each(lambda gc: jnp.broadcast_to(gc, (c, c)), gcs)
    decays = _each(lambda r: jnp.where(tril, jnp.exp(jnp.where(tril, r - r.T, 0.0)), 0.0), gc_rows)
    kbs = _each(lambda k, b: k * b, ks, betas)
    lmats = _each(lambda kb, k, d: jnp.where(i > j, _bdot(kb, k, _NT) * d, 0.0), kbs, ks, decays)
    attns = _each(lambda q, k, d: jnp.where(tril, _bdot(q, k, _NT) * d, 0.0), qs, ks, decays)
    invs = unit_lower_inverse(lmats)
    us = dot3(invs, _each(lambda v, b: v * b, vcs, betas))
    ws = dot3(invs, _each(lambda kb, gc: kb * jnp.exp(gc), kbs, gcs))
    g_lasts = _each(lambda g: jnp.sum(g, axis=0, keepdims=True), gs)
    return qs, ks, us, ws, attns, gcs, g_lasts


def _final_fn(h, g, target):
    y = _rms(h) * g
    return 0.5 * jnp.sum(jnp.mean(jnp.square(y - target), axis=-1, keepdims=True), axis=0, keepdims=True)


def _tile(n, candidates):
    for t in candidates:
        if n % t == 0:
            return t
    raise ValueError(f"no tile for {n}")


MATMUL_RESIDENT_LHS_BYTES = 8 * 1024 * 1024


def matmul(a, b, *, ta=False, tb=False, add=None, out_dtype=F32, tm=None, tn=None, split=None, layer=None, column_halves=None, name):
    m = a.shape[1] if ta else a.shape[0]
    k = a.shape[0] if ta else a.shape[1]
    n = b.shape[0] if tb else b.shape[1]
    assert k == (b.shape[1] if tb else b.shape[0])
    out_shape, out_block, out_index = (m, n), None, lambda i, j: (i, j)
    if split is not None:
        dims4, perm = split
        out_shape = tuple(dims4[p] for p in perm)
        r, cols = out_shape[2:]
        tm, tn = m, tn or _tile(cols, (1408, 512))
        cb = cols // tn
        if perm == (0, 2, 1, 3):
            out_block, out_index = (2, None, r, tn), lambda i, j: (0, j // cb, 0, j % cb)
        elif perm == (1, 0, 2, 3):
            out_block, out_index = (2, N_SHARD, r, tn), lambda i, j: (0, 0, 0, j)
        else:
            raise ValueError(perm)
    if tm is None and not ta and m * k * a.dtype.itemsize <= MATMUL_RESIDENT_LHS_BYTES:
        tm = m
    tm = tm or _tile(m, (1024, 512, 1408, 256, 128))
    tn = tn or _tile(n, (512, 1408, 256, 128))
    aliases, prev, keep_rows = {}, None, None
    if layer is not None:
        index, count, prev = layer
        out_shape, out_block, out_index = (count, m, n), (None, tm, tn), lambda i, j: (index, i, j)
    if column_halves is not None:
        total_rows, first_row, keep_rows, prev = column_halves
        tn = n // 2
        rows_out = keep_rows or tm
        out_shape, out_block = (2, total_rows, tn), (None, rows_out, tn)
        out_index = lambda i, j: (j, first_row // rows_out + i, 0)
    dims = (((0 if ta else 1,), (1 if tb else 0,)), ((), ()))

    def body(a_ref, b_ref, *rest):
        acc = lax.dot_general(a_ref[...].astype(BF16), b_ref[...].astype(BF16), dims, preferred_element_type=F32)
        if add is not None:
            acc = acc + rest[0][...]
        o_ref = rest[-1]
        acc = acc.astype(out_dtype)
        if split is not None and split[1] == (1, 0, 2, 3):
            rows = o_ref.shape[2]
            for s in range(N_SHARD):
                for h in range(2):
                    o_ref[h, s] = acc[(2 * s + h) * rows:(2 * s + h + 1) * rows]
        elif keep_rows is not None:
            o_ref[...] = acc[:keep_rows]
        else:
            o_ref[...] = acc.reshape(o_ref.shape)

    a_spec = pl.BlockSpec((k, tm), lambda i, j: (0, i)) if ta else pl.BlockSpec((tm, k), lambda i, j: (i, 0))
    b_spec = pl.BlockSpec((tn, k), lambda i, j: (j, 0)) if tb else pl.BlockSpec((k, tn), lambda i, j: (0, j))
    o_spec = pl.BlockSpec(out_block or (tm, tn), out_index)
    in_specs, args = [a_spec, b_spec], [a, b]
    if add is not None:
        in_specs.append(o_spec)
        args.append(add)
    if prev is not None:
        aliases = {len(args): 0}
        in_specs.append(pl.BlockSpec(memory_space=pl.ANY))
        args.append(prev)
    return pl.pallas_call(body, out_shape=_sds(out_shape, out_dtype), grid=(m // tm, n // tn), in_specs=in_specs,
                          out_specs=o_spec, input_output_aliases=aliases, compiler_params=_params(), name=name)(*args)


def norm_matmul(x, g, b, *, tb=False, name):
    t, k = x.shape
    n = b.shape[0] if tb else b.shape[1]
    tn = _tile(n, (512, 1408, 256, 128))
    dims = (((1,), (1 if tb else 0,)), ((), ()))

    def body(x_ref, g_ref, b_ref, o_ref, hn_ref):
        @pl.when(pl.program_id(0) == 0)
        def _():
            hn_ref[...] = _norm_fn(x_ref[...], g_ref[...]).astype(BF16)

        o_ref[...] = lax.dot_general(hn_ref[...], b_ref[...].astype(BF16), dims, preferred_element_type=F32)

    b_spec = pl.BlockSpec((tn, k), lambda j: (j, 0)) if tb else pl.BlockSpec((k, tn), lambda j: (0, j))
    whole = pl.BlockSpec((t, k), lambda j: (0, 0))
    return pl.pallas_call(body, out_shape=(_sds((t, n), F32), _sds((t, k), BF16)), grid=(n // tn,),
                          in_specs=[whole, pl.BlockSpec((1, k), lambda j: (0, 0)), b_spec],
                          out_specs=(pl.BlockSpec((t, tn), lambda j: (0, j)), whole), compiler_params=_params(), name=name)(x, g, b)


ROW_TILE = 256


def norm_bwd(x, g, dy, dres, *, name):
    t, d = x.shape

    def body(x_ref, g_ref, dy_ref, dres_ref, dx_ref, dg_ref):
        _, vjp = jax.vjp(_norm_fn, x_ref[...], g_ref[...])
        dx, dg = vjp(dy_ref[...])
        dx_ref[...] = dx + dres_ref[...]

        @pl.when(pl.program_id(0) == 0)
        def _():
            dg_ref[...] = jnp.zeros_like(dg_ref)

        dg_ref[...] += dg

    row = pl.BlockSpec((ROW_TILE, d), lambda i: (i, 0))
    vec = pl.BlockSpec((1, d), lambda i: (0, 0))
    return pl.pallas_call(body, out_shape=(_sds((t, d), F32), _sds((1, d), F32)), grid=(t // ROW_TILE,),
                          in_specs=[row, vec, row, row], out_specs=(row, vec), compiler_params=_params(), name=name)(x, g, dy, dres)


def final_fwd_bwd(h, g, target, *, name):
    t, d = h.shape

    def body(h_ref, g_ref, t_ref, loss_ref, dh_ref, dg_ref):
        tgt = t_ref[...]
        loss, vjp = jax.vjp(lambda hh, gg: _final_fn(hh, gg, tgt), h_ref[...], g_ref[...])
        dh, dg = vjp(jnp.ones((1, 1), F32))
        dh_ref[...] = dh

        @pl.when(pl.program_id(0) == 0)
        def _():
            dg_ref[...] = jnp.zeros_like(dg_ref)
            loss_ref[...] = jnp.zeros_like(loss_ref)

        dg_ref[...] += dg
        loss_ref[...] += jnp.broadcast_to(loss, loss_ref.shape)

    row = pl.BlockSpec((ROW_TILE, d), lambda i: (i, 0))
    vec = pl.BlockSpec((1, d), lambda i: (0, 0))
    return pl.pallas_call(body, out_shape=(_sds((1, LANES), F32), _sds((t, d), F32), _sds((1, d), F32)), grid=(t // ROW_TILE,),
                          in_specs=[row, vec, row], out_specs=(pl.BlockSpec((1, LANES), lambda i: (0, 0)), row, vec),
                          compiler_params=_params(), name=name)(h, g, target)


FFN_FWD_COLS = 256
FFN_BWD_COLS = 128


def ffn_act_fwd(u, cw, cb, *, name):
    t = u.shape[0]
    w = FFN_FWD_COLS
    nb = D_FF // w

    def body(ug_ref, uv_ref, wg_ref, wv_ref, bg_ref, bv_ref, o_ref):
        o_ref[...] = _ffn_act_fn(ug_ref[...], uv_ref[...], wg_ref[...], wv_ref[...], bg_ref[...], bv_ref[...]).astype(BF16)

    def col(rows, off):
        return pl.BlockSpec((rows, w), lambda j: (0, j + off))

    return pl.pallas_call(body, out_shape=_sds((t, D_FF), BF16), grid=(nb,),
                          in_specs=[col(t, 0), col(t, nb), col(3, 0), col(3, nb), col(1, 0), col(1, nb)],
                          out_specs=col(t, 0), compiler_params=_params(), name=name)(u, u, cw, cw, cb, cb)


def _put_column_blocks(step, n_steps, blocks, dst_ref, width, stage_ref, sems):
    def copies(at):
        slot = at % 2
        return [pltpu.make_async_copy(stage_ref.at[slot, p], dst_ref.at[:, pl.ds(pl.multiple_of((p * n_steps + at) * width, LANES), width)],
                                      sems.at[slot, p]) for p in range(len(blocks))]

    @pl.when(step >= 2)
    def _():
        for cp in copies(step - 2):
            cp.wait()

    for p, value in enumerate(blocks):
        stage_ref[step % 2, p] = value
    for cp in copies(step):
        cp.start()

    @pl.when(step == n_steps - 1)
    def _():
        for cp in copies(step - 1) + copies(step):
            cp.wait()


def ffn_act_bwd(u, cw, cb, da, *, name):
    t = u.shape[0]
    w = FFN_BWD_COLS
    nb = D_FF // w

    def body(ug_ref, uv_ref, wg_ref, wv_ref, bg_ref, bv_ref, da_ref, dug_ref, duv_ref, dwg_ref, dwv_ref, dbg_ref, dbv_ref):
        _, vjp = jax.vjp(_ffn_act_fn, ug_ref[...], uv_ref[...], wg_ref[...], wv_ref[...], bg_ref[...], bv_ref[...])
        dug, duv, dwg, dwv, dbg, dbv = vjp(da_ref[...])
        dug_ref[...] = dug.astype(BF16)
        duv_ref[...] = duv.astype(BF16)
        dwg_ref[...] = dwg
        dwv_ref[...] = dwv
        dbg_ref[...] = dbg
        dbv_ref[...] = dbv

    def col(rows, off):
        return pl.BlockSpec((rows, w), lambda j: (0, j + off))

    outs = pl.pallas_call(
        body, out_shape=(_sds((t, D_FF), BF16), _sds((t, D_FF), BF16), _sds((3, D_FF), F32), _sds((3, D_FF), F32),
                         _sds((1, D_FF), F32), _sds((1, D_FF), F32)),
        grid=(nb,), in_specs=[col(t, 0), col(t, nb), col(3, 0), col(3, nb), col(1, 0), col(1, nb), col(t, 0)],
        out_specs=(col(t, 0), col(t, 0), col(3, 0), col(3, 0), col(1, 0), col(1, 0)), compiler_params=_params(), name=name,
    )(u, u, cw, cw, cb, cb, da)
    dug, duv, dwg, dwv, dbg, dbv = outs
    return jnp.concatenate([dug, duv], axis=1), jnp.concatenate([dwg, dwv], axis=1), jnp.concatenate([dbg, dbv], axis=1)


GDN_CONV_COLS = 256
GDN_CONV_OFF = 4 * GROUP


def gdn_conv_fwd(p, cw, *, name):
    t = p.shape[0]
    w = GDN_CONV_COLS
    nb = 3 * GROUP // w
    off = GDN_CONV_OFF // w

    def body(x_ref, w_ref, o_ref):
        o_ref[...] = _gdn_conv_fn(x_ref[...], w_ref[...])

    return pl.pallas_call(body, out_shape=_sds((t, 3 * GROUP), F32), grid=(nb,),
                          in_specs=[pl.BlockSpec((t, w), lambda j: (0, j + off)), pl.BlockSpec((4, w), lambda j: (0, j))],
                          out_specs=pl.BlockSpec((t, w), lambda j: (0, j)), compiler_params=_params(), name=name)(p, cw)


def gdn_conv_bwd(p, cw, dc, *, name):
    t = p.shape[0]
    w = GDN_CONV_COLS
    nb = 3 * GROUP // w
    off = GDN_CONV_OFF // w

    def body(x_ref, w_ref, dc_ref, dx_ref, dw_ref):
        _, vjp = jax.vjp(_gdn_conv_fn, x_ref[...], w_ref[...])
        dx, dw = vjp(dc_ref[...])
        dx_ref[...] = dx.astype(BF16)
        dw_ref[...] = dw

    blk = pl.BlockSpec((t, w), lambda j: (0, j))
    wblk = pl.BlockSpec((4, w), lambda j: (0, j))
    return pl.pallas_call(body, out_shape=(_sds((t, 3 * GROUP), BF16), _sds((4, 3 * GROUP), F32)), grid=(nb,),
                          in_specs=[pl.BlockSpec((t, w), lambda j: (0, j + off)), wblk, blk], out_specs=(blk, wblk),
                          compiler_params=_params(), name=name)(p, cw, dc)


def _lru_specs(t):
    w = D_MODEL // LRU_BLOCKS
    gate = pl.BlockSpec((t, w), lambda j: (0, j))
    xin = pl.BlockSpec((t, w), lambda j: (0, j + LRU_BLOCKS))
    cw = pl.BlockSpec((4, w), lambda j: (0, j))
    vec = pl.BlockSpec((1, w), lambda j: (0, j))
    mat = pl.BlockSpec((None, w, w), lambda j: (j, 0, 0))
    return gate, xin, cw, vec, mat


def lru_fwd(gx, cw, cb, wa, ba, wx, bx, lam, *, name):
    t = gx.shape[0]
    gate, xin, cws, vec, mat = _lru_specs(t)

    def body(g_ref, x_ref, cw_ref, cb_ref, wa_ref, ba_ref, wx_ref, bx_ref, lam_ref, o_ref):
        o_ref[...] = _lru_fn(g_ref[...], x_ref[...], cw_ref[...], cb_ref[...], wa_ref[...], ba_ref[...], wx_ref[...],
                             bx_ref[...], lam_ref[...]).astype(BF16)

    return pl.pallas_call(body, out_shape=_sds((t, D_MODEL), BF16), grid=(LRU_BLOCKS,),
                          in_specs=[gate, xin, cws, vec, mat, vec, mat, vec, vec], out_specs=gate,
                          compiler_params=_params(), name=name)(gx, gx, cw, cb, wa, ba, wx, bx, lam)


def lru_bwd(gx, cw, cb, wa, ba, wx, bx, lam, dy, *, name):
    t = gx.shape[0]
    gate, xin, cws, vec, mat = _lru_specs(t)

    def body(g_ref, x_ref, cw_ref, cb_ref, wa_ref, ba_ref, wx_ref, bx_ref, lam_ref, dy_ref,
             dgx_ref, dcw_ref, dcb_ref, dwa_ref, dba_ref, dwx_ref, dbx_ref, dlam_ref, stage_ref, sems):
        _, vjp = jax.vjp(_lru_fn, g_ref[...], x_ref[...], cw_ref[...], cb_ref[...], wa_ref[...], ba_ref[...], wx_ref[...],
                         bx_ref[...], lam_ref[...])
        dg, dx, dcw, dcb, dwa, dba, dwx, dbx, dlam = vjp(dy_ref[...])
        _put_column_blocks(pl.program_id(0), LRU_BLOCKS, (dg.astype(BF16), dx.astype(BF16)), dgx_ref, D_MODEL // LRU_BLOCKS, stage_ref, sems)
        dcw_ref[...] = dcw
        dcb_ref[...] = dcb
        dwa_ref[...] = dwa
        dba_ref[...] = dba
        dwx_ref[...] = dwx
        dbx_ref[...] = dbx
        dlam_ref[...] = dlam

    d = D_MODEL
    w = d // LRU_BLOCKS
    out_shape = (_sds((t, 2 * d), BF16), _sds((4, d), F32), _sds((1, d), F32), _sds((LRU_BLOCKS, w, w), F32),
                 _sds((1, d), F32), _sds((LRU_BLOCKS, w, w), F32), _sds((1, d), F32), _sds((1, d), F32))
    return pl.pallas_call(body, out_shape=out_shape, grid=(LRU_BLOCKS,),
                          in_specs=[gate, xin, cws, vec, mat, vec, mat, vec, vec, gate],
                          out_specs=(pl.BlockSpec(memory_space=pl.ANY), cws, vec, mat, vec, mat, vec, vec),
                          scratch_shapes=[pltpu.VMEM((2, 2, t, w), BF16), pltpu.SemaphoreType.DMA((2, 2))],
                          compiler_params=_params(), name=name)(gx, gx, cw, cb, wa, ba, wx, bx, lam, dy)


def _ret_tables():
    half = HEAD // 2
    inv_freq = (np.float32(ROPE_BASE) ** (-np.arange(half, dtype=np.float32) / np.float32(half))).astype(np.float32)
    ang = (np.arange(SEQ, dtype=np.float32)[:, None] * inv_freq[None, :]).astype(np.float64)
    cos2 = np.concatenate([np.cos(ang), np.cos(ang)], axis=1).astype(np.float32)
    sin2 = np.concatenate([-np.sin(ang), np.sin(ang)], axis=1).astype(np.float32)
    c = RET_CHUNK
    log_gamma = np.log1p(-np.exp2(-5.0 - np.arange(N_HEADS, dtype=np.float64)))
    idx = np.arange(c, dtype=np.float64)
    rel = idx[:, None] - idx[None, :]
    dmask = np.where(rel >= 0, np.exp(log_gamma[:, None, None] * np.maximum(rel, 0.0)), 0.0)
    ones = np.ones((N_HEADS, c, HEAD))
    ktail = np.exp(log_gamma[:, None] * (c - 1 - idx))[:, :, None] * ones
    qdec = np.exp(log_gamma[:, None] * (idx + 1.0))[:, :, None] * ones
    cdec = np.exp(log_gamma * c)[:, None, None] * ones
    return tuple(jnp.asarray(a, F32) for a in (cos2, sin2, dmask, ktail, qdec, cdec))


def _ret_specs(rev):
    c = RET_CHUNK * RET_CHUNKS_PER_STEP
    nc = SEQ // c

    def n_of(n):
        return nc - 1 - n if rev else n

    def group(off):
        return pl.BlockSpec((c, GROUP), lambda n: (n_of(n), off))

    tab = pl.BlockSpec((c, HEAD), lambda n: (n_of(n), 0))
    const = pl.BlockSpec((N_HEADS, RET_CHUNK, HEAD), lambda n: (0, 0, 0))
    state = pl.BlockSpec((N_HEADS, None, HEAD, HEAD), lambda n: (0, n_of(n), 0, 0))
    return group, tab, const, state, nc


def _head(h):
    return slice(h * HEAD, (h + 1) * HEAD)


def ret_fwd(p, tables, *, name):
    group, tab, const, state, nc = _ret_specs(False)

    def body(q_ref, k_ref, v_ref, g_ref, cos_ref, sin_ref, dm_ref, kt_ref, qd_ref, cd_ref, y_ref, st_ref, s_scr):
        @pl.when(pl.program_id(0) == 0)
        def _():
            s_scr[...] = jnp.zeros_like(s_scr)

        heads = range(N_HEADS)
        states = tuple(s_scr[h] for h in heads)
        ys, new_states = _ret_fn(*(tuple(r[:, _head(h)] for h in heads) for r in (q_ref, k_ref, v_ref, g_ref)), states,
                                 cos_ref[...], sin_ref[...], *(tuple(r[h] for h in heads) for r in (dm_ref, kt_ref, qd_ref, cd_ref)))
        for h in heads:
            st_ref[h] = states[h]
            y_ref[:, _head(h)] = ys[h].astype(BF16)
            s_scr[h] = new_states[h]

    return pl.pallas_call(
        body, out_shape=(_sds((SEQ, 2 * GROUP), BF16), _sds((N_HEADS, nc, HEAD, HEAD), F32)), grid=(nc,),
        in_specs=[group(0), group(1), group(2), group(3), tab, tab, const, const, const, const],
        out_specs=(group(0), state), scratch_shapes=[pltpu.VMEM((N_HEADS, HEAD, HEAD), F32)], compiler_params=_params(), name=name,
    )(p, p, p, p, *tables)


def ret_bwd(p, tables, states, dy, *, name):
    group, tab, const, state, nc = _ret_specs(True)

    def body(q_ref, k_ref, v_ref, g_ref, cos_ref, sin_ref, dm_ref, kt_ref, qd_ref, cd_ref, st_ref, dy_ref,
             dq_ref, dk_ref, dv_ref, dg_ref, ds_scr):
        @pl.when(pl.program_id(0) == 0)
        def _():
            ds_scr[...] = jnp.zeros_like(ds_scr)

        heads = range(N_HEADS)
        consts = (cos_ref[...], sin_ref[...], *(tuple(r[h] for h in heads) for r in (dm_ref, kt_ref, qd_ref, cd_ref)))
        _, vjp = jax.vjp(lambda *a: _ret_fn(*a, *consts), *(tuple(r[:, _head(h)] for h in heads) for r in (q_ref, k_ref, v_ref, g_ref)),
                         tuple(st_ref[h] for h in heads))
        dqs, dks, dvs, dgs, dss = vjp((tuple(dy_ref[:, _head(h)] for h in heads), tuple(ds_scr[h] for h in heads)))
        for h in heads:
            dq_ref[:, _head(h)] = dqs[h].astype(BF16)
            dk_ref[:, _head(h)] = dks[h].astype(BF16)
            dv_ref[:, _head(h)] = dvs[h].astype(BF16)
            dg_ref[:, _head(h)] = dgs[h].astype(BF16)
            ds_scr[h] = dss[h]

    out = _sds((SEQ, GROUP), BF16)
    return pl.pallas_call(
        body, out_shape=(out, out, out, out), grid=(nc,),
        in_specs=[group(0), group(1), group(2), group(3), tab, tab, const, const, const, const, state, group(0)],
        out_specs=(group(0), group(0), group(0), group(0)), scratch_shapes=[pltpu.VMEM((N_HEADS, HEAD, HEAD), F32)],
        compiler_params=_params(), name=name,
    )(p, p, p, p, *tables, states, dy)


def _gdn_specs(rev):
    c = GDN_CHUNK * GDN_CHUNKS_PER_STEP
    nc = SEQ // c

    def n_of(n):
        return nc - 1 - n if rev else n

    def group(off):
        return pl.BlockSpec((c, GROUP), lambda n: (n_of(n), off))

    small = pl.BlockSpec((c, LANES), lambda n: (n_of(n), 0))
    vec = pl.BlockSpec((1, LANES), lambda n: (0, 0))
    state = pl.BlockSpec((N_HEADS, None, HEAD, HEAD), lambda n: (0, n_of(n), 0, 0))
    qkv = pl.BlockSpec((c, 3 * GROUP), lambda n: (n_of(n), 0))
    return group, small, vec, state, qkv, nc


GDN_GATE_GROUP = 7


def gdn_fwd(conv, p, small, a_log, dt_bias, gain, y_started, *, name):
    group, sm, vec, state, _, nc = _gdn_specs(False)

    def body(q_ref, k_ref, v_ref, g_ref, sm_ref, al_ref, dt_ref, gn_ref, _, y_ref, st_ref, s_scr):
        @pl.when(pl.program_id(0) == 0)
        def _():
            s_scr[...] = jnp.zeros_like(s_scr)

        states = tuple(s_scr[h] for h in range(N_HEADS))
        ys, new_states = _gdn_fn(*(tuple(r[:, _head(h)] for h in range(N_HEADS)) for r in (q_ref, k_ref, v_ref, g_ref)),
                                 sm_ref[...], al_ref[...], dt_ref[...], gn_ref[...], states)
        for h in range(N_HEADS):
            st_ref[h] = states[h]
            y_ref[:, _head(h)] = ys[h].astype(BF16)
            s_scr[h] = new_states[h]

    return pl.pallas_call(
        body, out_shape=(_sds((SEQ, 2 * GROUP), BF16), _sds((N_HEADS, nc, HEAD, HEAD), F32)), grid=(nc,),
        in_specs=[group(0), group(1), group(2), group(GDN_GATE_GROUP), sm, vec, vec, vec, pl.BlockSpec(memory_space=pl.ANY)],
        out_specs=(group(1), state), input_output_aliases={8: 0},
        scratch_shapes=[pltpu.VMEM((N_HEADS, HEAD, HEAD), F32)], compiler_params=_params(), name=name,
    )(conv, conv, conv, p, small, a_log, dt_bias, gain, y_started)


def gdn_bwd(conv, p, small, a_log, dt_bias, gain, states, dy, *, name):
    group, sm, vec, state, qkv, nc = _gdn_specs(True)

    def body(q_ref, k_ref, v_ref, g_ref, sm_ref, al_ref, dt_ref, gn_ref, st_ref, dy_ref,
             dqkv_ref, dg_ref, dsm_ref, dal_ref, ddt_ref, dgn_ref, ds_scr):
        @pl.when(pl.program_id(0) == 0)
        def _():
            ds_scr[...] = jnp.zeros_like(ds_scr)
            dal_ref[...] = jnp.zeros_like(dal_ref)
            ddt_ref[...] = jnp.zeros_like(ddt_ref)
            dgn_ref[...] = jnp.zeros_like(dgn_ref)

        per_head = tuple(tuple(r[:, _head(h)] for h in range(N_HEADS)) for r in (q_ref, k_ref, v_ref, g_ref))
        _, vjp = jax.vjp(_gdn_fn, *per_head, sm_ref[...], al_ref[...], dt_ref[...], gn_ref[...],
                         tuple(st_ref[h] for h in range(N_HEADS)))
        cts = (tuple(dy_ref[:, _head(h)] for h in range(N_HEADS)), tuple(ds_scr[h] for h in range(N_HEADS)))
        dqs, dks, dvs, dgs, dsm, dal, ddt, dgn, dss = vjp(cts)
        for h in range(N_HEADS):
            for part, blocks in enumerate((dqs, dks, dvs)):
                dqkv_ref[:, part * GROUP + h * HEAD:part * GROUP + (h + 1) * HEAD] = blocks[h]
            dg_ref[:, _head(h)] = dgs[h].astype(BF16)
            ds_scr[h] = dss[h]
        dsm_ref[...] = dsm
        dal_ref[...] += dal
        ddt_ref[...] += ddt
        dgn_ref[...] += dgn

    pv = _sds((1, LANES), F32)
    return pl.pallas_call(
        body, out_shape=(_sds((SEQ, 3 * GROUP), F32), _sds((SEQ, GROUP), BF16), _sds((SEQ, LANES), F32), pv, pv, pv), grid=(nc,),
        in_specs=[group(0), group(1), group(2), group(GDN_GATE_GROUP), sm, vec, vec, vec, state, group(1)],
        out_specs=(qkv, group(0), sm, vec, vec, vec), scratch_shapes=[pltpu.VMEM((N_HEADS, HEAD, HEAD), F32)],
        compiler_params=_params(), name=name,
    )(conv, conv, conv, p, small, a_log, dt_bias, gain, states, dy)


ELEMENTWISE_BLOCK_BYTES = 2 * 1024 * 1024


def _row_tile(r, c):
    best = None
    for tr in range(8, r + 1, 8):
        if r % tr == 0 and tr * c * 4 <= ELEMENTWISE_BLOCK_BYTES:
            best = tr
    if best is None:
        raise ValueError(f"no row tile for ({r}, {c})")
    return best


def _tile_2d(r, c):
    if any(r % tr == 0 for tr in range(8, r + 1, 8)):
        return _row_tile(r, c), c
    tc = max(t for t in range(LANES, c + 1, LANES) if c % t == 0 and r * t * 4 <= ELEMENTWISE_BLOCK_BYTES)
    return r, tc


def _core_index():
    return lax.axis_index("c").astype(jnp.int32).reshape(1)


def _chip_index():
    return (2 * lax.axis_index("x") + lax.axis_index("y")).astype(jnp.int32).reshape(1)


def adamw_halves(w, m, v, g_own, g_sib, *, layer=0, prev=None, name):
    n_layers, rows, c = w.shape
    r = rows // 2
    tr = _row_tile(r, c)
    nb = r // tr

    def body(c_ref, w_ref, m_ref, v_ref, own_ref, sib_ref, *rest):
        g_ref, d_ref, nm_ref, nv_ref = rest[-4:]
        gg = jnp.where(pl.program_id(0) == c_ref[0], own_ref[...], sib_ref[...])
        nm = ADAM_B1 * m_ref[...] + (1.0 - ADAM_B1) * gg
        nv = ADAM_B2 * v_ref[...] + (1.0 - ADAM_B2) * jnp.square(gg)
        m_hat = nm / (1.0 - ADAM_B1 ** ADAM_STEP)
        v_hat = nv / (1.0 - ADAM_B2 ** ADAM_STEP)
        g_ref[...] = gg
        d_ref[...] = -ADAM_LR * (m_hat / (jnp.sqrt(v_hat) + ADAM_EPS) + ADAM_WD * w_ref[...])
        nm_ref[...] = nm
        nv_ref[...] = nv

    full = pl.BlockSpec((None, tr, c), lambda h, i, cr: (layer, h * nb + i, 0))
    half = pl.BlockSpec((tr, c), lambda h, i, cr: (i, 0))
    o = _sds((n_layers, rows, c), F32)
    prev = list(prev or ())
    gs = pltpu.PrefetchScalarGridSpec(num_scalar_prefetch=1, grid=(2, nb), in_specs=[full, full, full, half, half] + [_ANY] * len(prev),
                                      out_specs=(full, full, full, full))
    n_fixed = 6
    return pl.pallas_call(body, out_shape=(o, o, o, o), grid_spec=gs, compiler_params=_params(), name=name,
                          input_output_aliases={n_fixed + k: k for k in range(len(prev))})(
        _core_index(), w, m, v, g_own, g_sib, *prev)


ADAMW_ROW_STEPS = 6


def adamw_rows(w, g, m, v, *, name):
    rows, _, cols = w.shape
    tr = rows // ADAMW_ROW_STEPS

    def body(w_ref, g_ref, m_ref, v_ref, g_out_ref, d_ref, nm_ref, nv_ref):
        gg = g_ref[...]
        nm = ADAM_B1 * m_ref[...] + (1.0 - ADAM_B1) * gg
        nv = ADAM_B2 * v_ref[...] + (1.0 - ADAM_B2) * jnp.square(gg)
        m_hat = nm / (1.0 - ADAM_B1 ** ADAM_STEP)
        v_hat = nv / (1.0 - ADAM_B2 ** ADAM_STEP)
        g_out_ref[...] = gg
        d_ref[...] = -ADAM_LR * (m_hat / (jnp.sqrt(v_hat) + ADAM_EPS) + ADAM_WD * w_ref[...])
        nm_ref[...] = nm
        nv_ref[...] = nv

    blk = pl.BlockSpec((tr, 1, cols), lambda i: (i, 0, 0))
    o = _sds(w.shape, F32)
    return pl.pallas_call(body, out_shape=(o, o, o, o), grid=(ADAMW_ROW_STEPS,), in_specs=[blk] * 4, out_specs=(blk, blk, blk, blk),
                          compiler_params=_params(), name=name)(w, g, m, v)


def adamw_many(ws, gs, ms, vs, *, name):
    n = len(ws)

    def body(*refs):
        w_refs, g_refs, m_refs, v_refs, d_refs, nm_refs, nv_refs = (refs[k * n:(k + 1) * n] for k in range(7))
        for i in range(n):
            gg = g_refs[i][...]
            nm = ADAM_B1 * m_refs[i][...] + (1.0 - ADAM_B1) * gg
            nv = ADAM_B2 * v_refs[i][...] + (1.0 - ADAM_B2) * jnp.square(gg)
            m_hat = nm / (1.0 - ADAM_B1 ** ADAM_STEP)
            v_hat = nv / (1.0 - ADAM_B2 ** ADAM_STEP)
            d_refs[i][...] = -ADAM_LR * (m_hat / (jnp.sqrt(v_hat) + ADAM_EPS) + ADAM_WD * w_refs[i][...])
            nm_refs[i][...] = nm
            nv_refs[i][...] = nv

    outs = pl.pallas_call(body, out_shape=[_sds(w.shape, F32) for w in ws] * 3, compiler_params=_params(), name=name)(*ws, *gs, *ms, *vs)
    return outs[:n], outs[n:2 * n], outs[2 * n:]


def add_core_halves(g2, land, *, out_dtype, name):
    _, ns, r, cols = g2.shape
    tr, tc = _tile_2d(r, cols)

    def body(c_ref, a_ref, b_ref, o_ref):
        o_ref[...] = (a_ref[...] + b_ref[...]).astype(out_dtype)

    gs = pltpu.PrefetchScalarGridSpec(
        num_scalar_prefetch=1, grid=(ns, r // tr, cols // tc),
        in_specs=[pl.BlockSpec((None, None, tr, tc), lambda s, i, j, cr: (cr[0], s, i, j)),
                  pl.BlockSpec((None, tr, tc), lambda s, i, j, cr: (s, i, j))],
        out_specs=pl.BlockSpec((None, tr, tc), lambda s, i, j, cr: (s, i, j)))
    return pl.pallas_call(body, out_shape=_sds((ns, r, cols), out_dtype), grid_spec=gs, compiler_params=_params(), name=name)(
        _core_index(), g2, land)


def sum_over_chips(own, land, *, scatter, name):
    _, r, cols = own.shape
    tr, tc = _tile_2d(r, cols)

    def body(mine_ref, own_ref, l0, l1, l2, l3, o_ref):
        mine = mine_ref[0]
        mine_val = own_ref[...]
        acc = None
        for s, l_ref in enumerate((l0, l1, l2, l3)):
            val = jnp.where(mine == s, mine_val, l_ref[...]).astype(F32)
            acc = val if acc is None else acc + val
        o_ref[...] = acc

    def slot(s):
        return pl.BlockSpec((None, tr, tc), lambda i, j, mr: (jnp.where(mr[0] == s, (s + 1) % N_SHARD, s), i, j))

    own_spec = pl.BlockSpec((None, tr, tc), lambda i, j, mr: (mr[0] if scatter else 0, i, j))
    gs = pltpu.PrefetchScalarGridSpec(num_scalar_prefetch=1, grid=(r // tr, cols // tc), in_specs=[own_spec] + [slot(s) for s in range(N_SHARD)],
                                      out_specs=pl.BlockSpec((tr, tc), lambda i, j, mr: (i, j)))
    return pl.pallas_call(body, out_shape=_sds((r, cols), F32), grid_spec=gs, compiler_params=_params(), name=name)(
        _chip_index(), own, land, land, land, land)


_ANY = pl.BlockSpec(memory_space=pl.ANY)


def xy_exchange(src, *, scatter, name):
    rh = src.shape[1]

    def body(src_ref, land_ref, send_sems, recv_sems, loc_sem):
        x, y, c = lax.axis_index("x"), lax.axis_index("y"), lax.axis_index("c")
        mine = 2 * x + y
        peers = [(1 - x, y), (x, 1 - y), (1 - x, 1 - y)]

        def piece(shard):
            return src_ref.at[shard] if scatter else src_ref.at[c]

        def copy(k, px, py, dst_slot):
            return pltpu.make_async_remote_copy(src_ref=piece(2 * px + py), dst_ref=land_ref.at[dst_slot], send_sem=send_sems.at[k],
                                                recv_sem=recv_sems.at[k], device_id=(px, py, c), device_id_type=MESH)

        keep = pltpu.make_async_copy(piece(mine), land_ref.at[mine], loc_sem)
        keep.start()
        sends = [copy(k, px, py, mine) for k, (px, py) in enumerate(peers)]
        for cp in sends:
            cp.start()
        for cp in sends:
            cp.wait_send()
        for k, (px, py) in enumerate(peers):
            copy(k, px, py, 2 * px + py).wait_recv()
        keep.wait()

    return pl.pallas_call(body, out_shape=_sds((N_SHARD, rh, LANES), src.dtype), in_specs=[_ANY], out_specs=_ANY,
                          scratch_shapes=[pltpu.SemaphoreType.DMA((3,)), pltpu.SemaphoreType.DMA((3,)), pltpu.SemaphoreType.DMA(())],
                          name=name)(src)


def core_exchange(src, *, send_other_half, name):
    def body(src_ref, out_ref, send_sem, recv_sem, loc_sem):
        x, y, c = lax.axis_index("x"), lax.axis_index("y"), lax.axis_index("c")
        if send_other_half:
            cp = pltpu.make_async_remote_copy(src_ref=src_ref.at[1 - c], dst_ref=out_ref, send_sem=send_sem, recv_sem=recv_sem,
                                              device_id=(x, y, 1 - c), device_id_type=MESH)
            cp.start()
            cp.wait_send()
            cp.wait_recv()
        else:
            keep = pltpu.make_async_copy(src_ref, out_ref.at[c], loc_sem)
            keep.start()
            cp = pltpu.make_async_remote_copy(src_ref=src_ref, dst_ref=out_ref.at[c], send_sem=send_sem, recv_sem=recv_sem,
                                              device_id=(x, y, 1 - c), device_id_type=MESH)
            cp.start()
            cp.wait_send()
            pltpu.make_async_remote_copy(src_ref=src_ref, dst_ref=out_ref.at[1 - c], send_sem=send_sem, recv_sem=recv_sem,
                                         device_id=(x, y, 1 - c), device_id_type=MESH).wait_recv()
            keep.wait()

    out_shape = _sds(src.shape[1:], src.dtype) if send_other_half else _sds((2,) + src.shape, src.dtype)
    return pl.pallas_call(body, out_shape=out_shape, in_specs=[_ANY], out_specs=_ANY,
                          scratch_shapes=[pltpu.SemaphoreType.DMA(()), pltpu.SemaphoreType.DMA(()), pltpu.SemaphoreType.DMA(())],
                          name=name)(src)


def _comm_call(body, ins, out_shapes, sem_counts, name):
    return pl.pallas_call(body, out_shape=tuple(out_shapes), in_specs=[_ANY] * len(ins), out_specs=tuple([_ANY] * len(out_shapes)),
                          scratch_shapes=[pltpu.SemaphoreType.DMA((k,)) for k in sem_counts], name=name)(*ins)


def _sequencer_call(body, ins, out_shapes, sem_counts, name, collective_id):
    return pl.kernel(body, out_type=list(out_shapes), mesh=plsc.ScalarSubcoreMesh(axis_name="sequencer", num_cores=1), name=name,
                     scratch_types=[pltpu.SemaphoreType.DMA((k,)) for k in sem_counts],
                     compiler_params=pltpu.CompilerParams(collective_id=collective_id))(*ins)


def _handshake(peers):
    barrier = pltpu.get_barrier_semaphore()
    for peer in peers:
        pl.semaphore_signal(barrier, inc=1, device_id=peer, device_id_type=MESH)
    pl.semaphore_wait(barrier, len(peers))


def _xy_peers(x, y):
    return [(1 - x, y), (x, 1 - y), (1 - x, 1 - y)]


def gather_halves(halves, *, name, collective_id):
    n = len(halves)

    def body(*refs):
        ins, lands, sibs = refs[:n], refs[n:2 * n], refs[2 * n:3 * n]
        ici_send, ici_recv, d2d_send, d2d_recv = refs[3 * n:]
        x, y, c = lax.axis_index("x"), lax.axis_index("y"), lax.axis_index("c")
        mine = 2 * x + y
        peers = _xy_peers(x, y)
        _handshake([(px, py, c) for px, py in peers] + [(x, y, 1 - c)])

        def ici(i, k, slot):
            px, py = peers[k]
            return pltpu.make_async_remote_copy(src_ref=ins[i].at[c], dst_ref=lands[i].at[slot], send_sem=ici_send.at[3 * i + k],
                                                recv_sem=ici_recv.at[3 * i + k], device_id=(px, py, c), device_id_type=MESH)

        def pass_on(i, k):
            px, py = peers[k]
            slot = 2 * px + py
            return pltpu.make_async_remote_copy(src_ref=lands[i].at[slot], dst_ref=sibs[i].at[slot], send_sem=d2d_send.at[3 * i + k],
                                                recv_sem=d2d_recv.at[3 * i + k], device_id=(x, y, 1 - c), device_id_type=MESH)

        sends = [ici(i, k, mine) for i in range(n) for k in range(3)]
        for cp in sends:
            cp.start()
        passed = []
        for i in range(n):
            for k in range(3):
                px, py = peers[k]
                ici(i, k, 2 * px + py).wait_recv()
                cp = pass_on(i, k)
                cp.start()
                passed.append(cp)
        for cp in passed:
            cp.wait_recv()
        for cp in sends + passed:
            cp.wait_send()

    outs = [_sds((N_SHARD,) + h.shape[1:], h.dtype) for h in halves]
    res = _sequencer_call(body, halves, outs + outs, [3 * n] * 4, name, collective_id)
    return res[:n], res[n:]


def send_other_half(arrays, *, name, collective_id):
    n = len(arrays)

    def body(*refs):
        ins, lands = refs[:n], refs[n:2 * n]
        send_sems, recv_sems = refs[2 * n:]
        x, y, c = lax.axis_index("x"), lax.axis_index("y"), lax.axis_index("c")
        _handshake([(x, y, 1 - c)])
        copies = [pltpu.make_async_remote_copy(src_ref=ins[i].at[1 - c], dst_ref=lands[i], send_sem=send_sems.at[i],
                                               recv_sem=recv_sems.at[i], device_id=(x, y, 1 - c), device_id_type=MESH) for i in range(n)]
        for cp in copies:
            cp.start()
        for cp in copies:
            cp.wait_recv()
        for cp in copies:
            cp.wait_send()

    return _sequencer_call(body, arrays, [_sds(a.shape[1:], a.dtype) for a in arrays], [n, n], name, collective_id)


_HBM = pl.BlockSpec(memory_space=pltpu.HBM)
_SEM = pl.BlockSpec(memory_space=pltpu.SEMAPHORE)
_SPLIT_COPY = dict(has_side_effects=pltpu.SideEffectType.DATAFLOW_SIDE_EFFECTING)


def _chip_copy(ins, lands, send_sems, recv_sems, scatter, i, k, receive):
    x, y, c = lax.axis_index("x"), lax.axis_index("y"), lax.axis_index("c")
    px, py = _xy_peers(x, y)[k]
    theirs, mine = 2 * px + py, 2 * x + y
    src = ins[i].at[theirs] if scatter[i] else ins[i].at[0]
    return pltpu.make_async_remote_copy(src_ref=src, dst_ref=lands[i].at[theirs if receive else mine], send_sem=send_sems.at[3 * i + k],
                                        recv_sem=recv_sems.at[3 * i + k], device_id=(px, py, c), device_id_type=MESH)


def send_to_chips_start(arrays, scatter, *, name):
    n = len(arrays)

    def body(*refs):
        send_sems, recv_sems = refs[2 * n], refs[2 * n + 1]
        ins, lands = refs[2 * n + 2:3 * n + 2], refs[3 * n + 2:4 * n + 2]
        token = refs[4 * n + 2]
        for i in range(n):
            for k in range(3):
                _chip_copy(ins, lands, send_sems, recv_sems, scatter, i, k, receive=False).start()
        token[...] = jnp.zeros_like(token)

    land_shapes = [(N_SHARD,) + a.shape[1:] for a in arrays]
    operands = [pltpu.with_memory_space_constraint(a, pltpu.HBM) for a in arrays]
    operands += [pltpu.with_memory_space_constraint(lax.empty(s, a.dtype), pltpu.HBM) for s, a in zip(land_shapes, arrays)]
    out_shape = ([pltpu.SemaphoreType.DMA((3 * n,)), pltpu.SemaphoreType.DMA((3 * n,))] + [pltpu.HBM(a.shape, a.dtype) for a in arrays]
                 + [pltpu.HBM(s, a.dtype) for s, a in zip(land_shapes, arrays)] + [_sds((8, LANES), F32)])
    res = pl.pallas_call(body, name=name, out_shape=out_shape, in_specs=[_HBM] * (2 * n),
                         out_specs=[_SEM, _SEM] + [_HBM] * (2 * n) + [pl.BlockSpec(memory_space=pltpu.VMEM)],
                         input_output_aliases={i: 2 + i for i in range(2 * n)}, compiler_params=pltpu.CompilerParams(**_SPLIT_COPY))(*operands)
    return (res[0], res[1], res[2:2 + n], res[2 + n:2 + 2 * n], scatter), res[-1]


def send_to_chips_wait(state, after, *, name):
    send_sems, recv_sems, arrays, lands, scatter = state
    n = len(arrays)

    def body(*refs):
        ins, landing = refs[:n], refs[n:2 * n]
        send_sems, recv_sems = refs[2 * n], refs[2 * n + 1]
        for i in range(n):
            for k in range(3):
                _chip_copy(ins, landing, send_sems, recv_sems, scatter, i, k, receive=True).wait_recv()
        for i in range(n):
            for k in range(3):
                _chip_copy(ins, landing, send_sems, recv_sems, scatter, i, k, receive=False).wait_send()

    out_shape = [pltpu.HBM(a.shape, a.dtype) for a in list(arrays) + list(lands)]
    res = pl.pallas_call(body, name=name, out_shape=out_shape, in_specs=[_HBM] * (2 * n) + [_SEM, _SEM] + [_ANY] * len(after),
                         out_specs=[_HBM] * (2 * n), input_output_aliases={i: i for i in range(2 * n)},
                         compiler_params=pltpu.CompilerParams(**_SPLIT_COPY))(*arrays, *lands, send_sems, recv_sems, *after)
    return res[:n], res[n:]


def swap_with_other_core(arrays, *, name, collective_id):
    n = len(arrays)

    def body(*refs):
        ins, lands = refs[:n], refs[n:2 * n]
        send_sems, recv_sems = refs[2 * n:]
        x, y, c = lax.axis_index("x"), lax.axis_index("y"), lax.axis_index("c")
        _handshake([(x, y, 1 - c)])
        copies = [pltpu.make_async_remote_copy(src_ref=ins[i], dst_ref=lands[i], send_sem=send_sems.at[i], recv_sem=recv_sems.at[i],
                                               device_id=(x, y, 1 - c), device_id_type=MESH) for i in range(n)]
        for cp in copies:
            cp.start()
        for cp in copies:
            cp.wait_recv()
        for cp in copies:
            cp.wait_send()

    return _sequencer_call(body, arrays, [_sds(a.shape, a.dtype) for a in arrays], [n, n], name, collective_id)


def _pack_rows(n_elems, row_multiple):
    rows = -(-n_elems // LANES)
    return -(-rows // row_multiple) * row_multiple


def _pack(arrays, rows, dtype):
    flat = jnp.concatenate([a.reshape(-1).astype(dtype) for a in arrays])
    return jnp.pad(flat, (0, rows * LANES - flat.shape[0])).reshape(rows, LANES)


def _unpack(packed, shapes):
    flat = packed.reshape(-1)
    out, off = [], 0
    for s in shapes:
        n = int(np.prod(s))
        out.append(flat[off:off + n].reshape(s))
        off += n
    return out


def all_gather_shards(shards, axes, dtype, row_multiple, tag):
    shapes = [s.shape for s in shards]
    rows = _pack_rows(sum(int(np.prod(s)) for s in shapes), row_multiple)
    packed = _pack(shards, rows, dtype).reshape(2, rows // 2, LANES)
    land = xy_exchange(packed, scatter=False, name=f"gather_xy_{tag}")
    both = core_exchange(land, send_other_half=False, name=f"gather_c_{tag}")
    per_shard = jnp.swapaxes(both, 0, 1).reshape(N_SHARD, rows, LANES)
    pieces = [_unpack(per_shard[s], shapes) for s in range(N_SHARD)]
    return [jnp.concatenate([pieces[s][i] for s in range(N_SHARD)], axis=ax) for i, ax in enumerate(axes)]


def _ordered_before(first, then):
    if then is None:
        return first, None
    return lax.optimization_barrier((first, then))


def reduce_between_cores(arrays, scatter, *, tag, collective_id, before=None):
    arrays, before = _ordered_before(arrays, before)
    land = send_other_half(arrays, name=f"reduce_core_send_{tag}", collective_id=collective_id)
    return (arrays, land, scatter, tag, collective_id), before


def reduce_between_chips(state, before=None):
    arrays, land, scatter, tag, collective_id = state
    chip = [add_core_halves(a, l, out_dtype=BF16 if sc else F32, name=f"reduce_core_add_{tag}_{i}")
            for i, (a, l, sc) in enumerate(zip(arrays, land, scatter))]
    sending, token = send_to_chips_start(chip, scatter, name=f"reduce_chip_start_{tag}")
    token, before = _ordered_before(token, before)
    return (sending, token, scatter, tag, collective_id), before


def reduce_finish(state, after):
    sending, token, scatter, tag, collective_id = state
    chip, land = send_to_chips_wait(sending, tuple(after) + (token,), name=f"reduce_chip_wait_{tag}")
    own = [sum_over_chips(ch, l, scatter=sc, name=f"reduce_chip_add_{tag}_{i}") for i, (ch, l, sc) in enumerate(zip(chip, land, scatter))]
    sib = swap_with_other_core(own, name=f"reduce_core_swap_{tag}", collective_id=collective_id + 2)
    return own, sib


def _ffn_layer_fwd(h, norm_g, w_up, cw, cb, w_down, tag):
    u, hn = norm_matmul(h, norm_g, w_up, name=f"ffn_up_{tag}")
    act = ffn_act_fwd(u, cw, cb, name=f"ffn_act_{tag}")
    out = matmul(act, w_down, add=h, name=f"ffn_down_{tag}")
    return out, (h, hn, u, act)


def _travel_layout(array):
    return BIG_ARRAYS[array][3], BIG_ARRAYS[array][4]


def _ffn_layer_bwd(saved, dout, norm_g, w_up, cw, cb, w_down, tag):
    h, hn, u, act = saved
    dact = matmul(dout, w_down, tb=True, name=f"ffn_down_dx_{tag}")
    d_w_down = matmul(act, dout, ta=True, split=_travel_layout(f"ffn_w_down_{tag}"), name=f"ffn_down_dw_{tag}")
    du, dcw, dcb = ffn_act_bwd(u, cw, cb, dact, name=f"ffn_act_bwd_{tag}")
    dhn = matmul(du, w_up, tb=True, name=f"ffn_up_dx_{tag}")
    d_w_up = matmul(hn, du, ta=True, split=_travel_layout(f"ffn_w_up_{tag}"), name=f"ffn_up_dw_{tag}")
    dh, dg = norm_bwd(h, norm_g, dhn, dout, name=f"ffn_norm_bwd_{tag}")
    return dh, dg, d_w_up, dcw, dcb, d_w_down


def local_step(x, target, w, stage=lambda name, tensors, grads=None: tensors):
    g = {}
    tables = _ret_tables()
    x = stage("start", x)
    w_in_t = w["ret_gdn_w_in"]
    w_main = w_in_t[:MIX_MAIN]
    w_small = jnp.pad(w_in_t[MIX_MAIN:], ((0, LANES - 2 * N_HEADS), (0, 0)))
    a_log = jnp.pad(w["gdn_a_log"], ((0, 0), (0, LANES - N_HEADS)))
    dt_bias = jnp.pad(w["gdn_dt_bias"], ((0, 0), (0, LANES - N_HEADS)))

    p, hn0 = norm_matmul(x, w["norm_mix"][0:1], w_main, tb=True, name="mix0_in")
    hn0 = stage("normed", hn0)
    small = matmul(hn0, w_small, tb=True, name="mix0_in_small")
    y_ret, s_ret = ret_fwd(p, tables, name="ret_fwd")
    conv = gdn_conv_fwd(p, w["gdn_conv_w"], name="gdn_conv")
    y0, s_gdn = gdn_fwd(conv, p, small, a_log, dt_bias, w["gdn_out_gain"], y_ret, name="gdn_fwd")
    y0 = stage("mixed", y0)
    h1 = matmul(y0, w["ret_gdn_w_out"], add=x, name="mix0_out")
    h2, ffn0 = _ffn_layer_fwd(h1, w["norm_ffn"][0:1], w["ffn_w_up"][0], w["ffn_conv_w"][0], w["ffn_conv_b"][0:1], w["ffn_w_down"][0], "0")
    h2 = stage("layer0", h2)

    gx, hn1 = norm_matmul(h2, w["norm_mix"][1:2], w["lru_w_in"], name="mix1_in")
    lru_p = (w["lru_conv_w"], w["lru_conv_b"], w["lru_w_a"], w["lru_b_a"], w["lru_w_x"], w["lru_b_x"], w["lru_lambda"])
    y1 = lru_fwd(gx, *lru_p, name="lru_fwd")
    h3 = stage("mixed1", matmul(y1, w["lru_w_out"], add=h2, name="mix1_out"))
    h4, ffn1 = _ffn_layer_fwd(h3, w["norm_ffn"][1:2], w["ffn_w_up"][1], w["ffn_conv_w"][1], w["ffn_conv_b"][1:2], w["ffn_w_down"][1], "1")

    loss, dh4, g["norm_final"] = final_fwd_bwd(h4, w["norm_final"], target, name="final")

    dh3, dgf1, dwu1, dcw1, dcb1, dwd1 = _ffn_layer_bwd(ffn1, dh4, w["norm_ffn"][1:2], w["ffn_w_up"][1], w["ffn_conv_w"][1],
                                                     w["ffn_conv_b"][1:2], w["ffn_w_down"][1], "1")
    g["ffn_w_up_1"], g["ffn_w_down_1"] = dwu1, dwd1
    dh3 = stage("grads0_ready", dh3, g)
    dy1 = matmul(dh3, w["lru_w_out"], tb=True, name="mix1_out_dx")
    g["lru_w_out"] = matmul(y1, dh3, ta=True, split=_travel_layout("lru_w_out"), name="mix1_out_dw")
    dgx, g["lru_conv_w"], g["lru_conv_b"], g["lru_w_a"], g["lru_b_a"], g["lru_w_x"], g["lru_b_x"], g["lru_lambda"] = lru_bwd(
        gx, *lru_p, dy1, name="lru_bwd")
    dgx = stage("grads0_send", dgx, g)
    dhn1 = matmul(dgx, w["lru_w_in"], tb=True, name="mix1_in_dx")
    g["lru_w_in"] = matmul(hn1, dgx, ta=True, split=_travel_layout("lru_w_in"), name="mix1_in_dw")
    dh2, dgm1 = norm_bwd(h2, w["norm_mix"][1:2], dhn1, dh3, name="mix1_norm_bwd")
    dh2 = stage("grads1_ready", dh2, g)

    dh1, dgf0, dwu0, dcw0, dcb0, dwd0 = _ffn_layer_bwd(ffn0, dh2, w["norm_ffn"][0:1], w["ffn_w_up"][0], w["ffn_conv_w"][0],
                                                     w["ffn_conv_b"][0:1], w["ffn_w_down"][0], "0")
    g["ffn_w_up_0"], g["ffn_w_down_0"] = dwu0, dwd0
    dh1 = stage("grads2_ready", stage("grads1_send", dh1, g), g)
    dy0 = matmul(dh1, w["ret_gdn_w_out"], tb=True, name="mix0_out_dx")
    g["ret_gdn_w_out"] = matmul(y0, dh1, ta=True, split=_travel_layout("ret_gdn_w_out"), name="mix0_out_dw")
    dq_r, dk_r, dv_r, dg_r = ret_bwd(p, tables, s_ret, dy0, name="ret_bwd")
    dy0, dq_r = stage("grads2_send", (dy0, dq_r), g)
    dconv, dg_d, dsmall, dal, ddt, dgain = gdn_bwd(conv, p, small, a_log, dt_bias, w["gdn_out_gain"], s_gdn, dy0, name="gdn_bwd")
    dp_conv, g["gdn_conv_w"] = gdn_conv_bwd(p, w["gdn_conv_w"], dconv, name="gdn_conv_bwd")
    dp = jnp.concatenate([dq_r, dk_r, dv_r, dg_r, dp_conv, dg_d], axis=1)
    d_w_in = matmul(dp, hn0, ta=True, column_halves=(MIX_IN, 0, None, None), name="mix0_in_dw")
    d_w_in = matmul(dsmall, hn0, ta=True, column_halves=(MIX_IN, MIX_MAIN, 2 * N_HEADS, d_w_in), name="mix0_in_small_dw")
    g["ret_gdn_w_in"] = d_w_in.reshape(2, N_SHARD, MIX_IN // N_SHARD, D_MODEL // 2)
    dp = stage("grads3_ready", dp, g)
    dhn0 = stage("grads3_send", matmul(dp, w_main, name="mix0_in_dx"), g)
    dhn0 = matmul(dsmall, w_small, add=dhn0, name="mix0_in_small_dx")
    dx, dgm0 = norm_bwd(x, w["norm_mix"][0:1], dhn0, dh1, name="mix0_norm_bwd")

    g["gdn_a_log"] = dal[:, :N_HEADS]
    g["gdn_dt_bias"] = ddt[:, :N_HEADS]
    g["gdn_out_gain"] = dgain
    g["norm_mix"] = jnp.concatenate([dgm0, dgm1], axis=0)
    g["norm_ffn"] = jnp.concatenate([dgf0, dgf1], axis=0)
    g["ffn_conv_w"] = jnp.stack([dcw0, dcw1])
    g["ffn_conv_b"] = jnp.concatenate([dcb0, dcb1], axis=0)
    return loss, dx, g


WEIGHTS = ("norm_mix", "norm_ffn", "ret_gdn_w_in", "gdn_conv_w", "gdn_a_log", "gdn_dt_bias", "gdn_out_gain", "ret_gdn_w_out",
           "lru_w_in", "lru_conv_w", "lru_conv_b", "lru_w_a", "lru_b_a", "lru_w_x", "lru_b_x", "lru_lambda", "lru_w_out",
           "ffn_w_up", "ffn_conv_w", "ffn_conv_b", "ffn_w_down", "norm_final")
MATMUL_SHARDED = {"ret_gdn_w_in": 1, "ret_gdn_w_out": 0, "lru_w_in": 1, "lru_w_out": 0, "ffn_w_up": 2, "ffn_w_down": 1}
VECTOR_SHARDED = {"gdn_conv_w": 1, "lru_conv_w": 1, "lru_conv_b": 1, "lru_b_a": 1, "lru_b_x": 1, "lru_lambda": 1, "ffn_conv_w": 2}
SHARDED = {**MATMUL_SHARDED, **VECTOR_SHARDED}
REPLICATED = tuple(n for n in WEIGHTS if n not in SHARDED)
SQUEEZE = {"ret_gdn_w_in", "gdn_conv_w", "ret_gdn_w_out", "lru_w_in", "lru_conv_w", "lru_w_a", "lru_w_x", "lru_w_out"}
MIX_IN = MIX_MAIN + 2 * N_HEADS
BIG_ARRAYS = {
    "ret_gdn_w_in": ("ret_gdn_w_in", None, (MIX_IN, D_MODEL), (N_SHARD, MIX_IN // N_SHARD, 2, D_MODEL // 2), (2, 0, 1, 3)),
    "ret_gdn_w_out": ("ret_gdn_w_out", None, (2 * GROUP, D_MODEL), (N_SHARD, 2, GROUP // N_SHARD, D_MODEL), (1, 0, 2, 3)),
    "lru_w_in": ("lru_w_in", None, (D_MODEL, 2 * D_MODEL), (2, D_MODEL // 2, N_SHARD, 2 * D_MODEL // N_SHARD), (0, 2, 1, 3)),
    "lru_w_out": ("lru_w_out", None, (D_MODEL, D_MODEL), (N_SHARD, 2, D_MODEL // (2 * N_SHARD), D_MODEL), (1, 0, 2, 3)),
    "ffn_w_up_0": ("ffn_w_up", 0, (D_MODEL, 2 * D_FF), (2, D_MODEL // 2, N_SHARD, 2 * D_FF // N_SHARD), (0, 2, 1, 3)),
    "ffn_w_up_1": ("ffn_w_up", 1, (D_MODEL, 2 * D_FF), (2, D_MODEL // 2, N_SHARD, 2 * D_FF // N_SHARD), (0, 2, 1, 3)),
    "ffn_w_down_0": ("ffn_w_down", 0, (D_FF, D_MODEL), (N_SHARD, 2, D_FF // (2 * N_SHARD), D_MODEL), (1, 0, 2, 3)),
    "ffn_w_down_1": ("ffn_w_down", 1, (D_FF, D_MODEL), (N_SHARD, 2, D_FF // (2 * N_SHARD), D_MODEL), (1, 0, 2, 3)),
}
GATHER_GROUPS = (("ret_gdn_w_in",), ("ret_gdn_w_out", "ffn_w_up_0", "ffn_w_down_0"), ("lru_w_in", "lru_w_out"), ("ffn_w_up_1", "ffn_w_down_1"))
REDUCE_GROUPS = (("ffn_w_up_1", "ffn_w_down_1"), ("lru_w_in", "lru_w_out"), ("ffn_w_up_0", "ffn_w_down_0"), ("ret_gdn_w_out", "ret_gdn_w_in"),
                 ())
BLOCK_WEIGHTS = ("lru_w_a", "lru_w_x")
GATHER_COLLECTIVE_ID = 1
REDUCE_COLLECTIVE_ID = GATHER_COLLECTIVE_ID + len(GATHER_GROUPS)


TRANSPOSED = ("ret_gdn_w_in",)


def _shard_of(array, tensors):
    weight, layer = BIG_ARRAYS[array][:2]
    t = tensors[weight]
    if weight in TRANSPOSED:
        return jnp.swapaxes(t, 1, 2)[0]
    return _local_view(weight, t) if layer is None else t[layer]


def _core_halves(array, shard):
    _, _, _, split, perm = BIG_ARRAYS[array]
    kept = [k for k in range(4) if k != perm[1]]
    order = [kept.index(perm[0]), kept.index(perm[2]), kept.index(perm[3])]
    return shard.reshape([split[k] for k in kept]).transpose(order)


def _local_view(name, a):
    if name in SQUEEZE:
        return a[0]
    if a.ndim == 1:
        return a[None, :]
    return a


def kernel(x, norm_mix, norm_ffn, ret_gdn_w_in, gdn_conv_w, gdn_a_log, gdn_dt_bias, gdn_out_gain, ret_gdn_w_out, lru_w_in, lru_conv_w, lru_conv_b, lru_w_a, lru_b_a, lru_w_x, lru_b_x, lru_lambda, lru_w_out, ffn_w_up, ffn_conv_w, ffn_conv_b, ffn_w_down, norm_final, loss_target, m_norm_mix, m_norm_ffn, m_ret_gdn_w_in, m_gdn_conv_w, m_gdn_a_log, m_gdn_dt_bias, m_gdn_out_gain, m_ret_gdn_w_out, m_lru_w_in, m_lru_conv_w, m_lru_conv_b, m_lru_w_a, m_lru_b_a, m_lru_w_x, m_lru_b_x, m_lru_lambda, m_lru_w_out, m_ffn_w_up, m_ffn_conv_w, m_ffn_conv_b, m_ffn_w_down, m_norm_final, v_norm_mix, v_norm_ffn, v_ret_gdn_w_in, v_gdn_conv_w, v_gdn_a_log, v_gdn_dt_bias, v_gdn_out_gain, v_ret_gdn_w_out, v_lru_w_in, v_lru_conv_w, v_lru_conv_b, v_lru_w_a, v_lru_b_a, v_lru_w_x, v_lru_b_x, v_lru_lambda, v_lru_w_out, v_ffn_w_up, v_ffn_conv_w, v_ffn_conv_b, v_ffn_w_down, v_norm_final):
    given = dict(norm_mix=norm_mix, norm_ffn=norm_ffn, ret_gdn_w_in=ret_gdn_w_in, gdn_conv_w=gdn_conv_w, gdn_a_log=gdn_a_log, gdn_dt_bias=gdn_dt_bias, gdn_out_gain=gdn_out_gain, ret_gdn_w_out=ret_gdn_w_out, lru_w_in=lru_w_in, lru_conv_w=lru_conv_w, lru_conv_b=lru_conv_b, lru_w_a=lru_w_a, lru_b_a=lru_b_a, lru_w_x=lru_w_x, lru_b_x=lru_b_x, lru_lambda=lru_lambda, lru_w_out=lru_w_out, ffn_w_up=ffn_w_up, ffn_conv_w=ffn_conv_w, ffn_conv_b=ffn_conv_b, ffn_w_down=ffn_w_down, norm_final=norm_final)
    mom1 = dict(norm_mix=m_norm_mix, norm_ffn=m_norm_ffn, ret_gdn_w_in=m_ret_gdn_w_in, gdn_conv_w=m_gdn_conv_w, gdn_a_log=m_gdn_a_log, gdn_dt_bias=m_gdn_dt_bias, gdn_out_gain=m_gdn_out_gain, ret_gdn_w_out=m_ret_gdn_w_out, lru_w_in=m_lru_w_in, lru_conv_w=m_lru_conv_w, lru_conv_b=m_lru_conv_b, lru_w_a=m_lru_w_a, lru_b_a=m_lru_b_a, lru_w_x=m_lru_w_x, lru_b_x=m_lru_b_x, lru_lambda=m_lru_lambda, lru_w_out=m_lru_w_out, ffn_w_up=m_ffn_w_up, ffn_conv_w=m_ffn_conv_w, ffn_conv_b=m_ffn_conv_b, ffn_w_down=m_ffn_w_down, norm_final=m_norm_final)
    mom2 = dict(norm_mix=v_norm_mix, norm_ffn=v_norm_ffn, ret_gdn_w_in=v_ret_gdn_w_in, gdn_conv_w=v_gdn_conv_w, gdn_a_log=v_gdn_a_log, gdn_dt_bias=v_gdn_dt_bias, gdn_out_gain=v_gdn_out_gain, ret_gdn_w_out=v_ret_gdn_w_out, lru_w_in=v_lru_w_in, lru_conv_w=v_lru_conv_w, lru_conv_b=v_lru_conv_b, lru_w_a=v_lru_w_a, lru_b_a=v_lru_b_a, lru_w_x=v_lru_w_x, lru_b_x=v_lru_b_x, lru_lambda=v_lru_lambda, lru_w_out=v_lru_w_out, ffn_w_up=v_ffn_w_up, ffn_conv_w=v_ffn_conv_w, ffn_conv_b=v_ffn_conv_b, ffn_w_down=v_ffn_w_down, norm_final=v_norm_final)

    local = {n: _local_view(n, a) for n, a in given.items()}

    core = lax.axis_index("c")
    chip = 2 * lax.axis_index("x") + lax.axis_index("y")
    is_my_chip = lax.broadcasted_iota(jnp.int32, (N_SHARD, 1, 1), 0) == chip

    def by_core(mine, other):
        return jnp.where(core == 0, jnp.stack([mine, other]), jnp.stack([other, mine]))

    vec_names, rp_names = list(VECTOR_SHARDED), list(REPLICATED)
    full = dict(zip(vec_names, all_gather_shards([local[n] for n in vec_names], [SHARDED[n] for n in vec_names], F32, 32, "p")))
    for n in rp_names:
        full[n] = local[n]
    in_flight = {}

    bf16_halves = {}

    def cast_halves(gi):
        if gi not in bf16_halves:
            bf16_halves[gi] = [_core_halves(a, _shard_of(a, given).astype(BF16)) for a in GATHER_GROUPS[gi]]
        return bf16_halves[gi]

    def launch(gi, after=None):
        halves = cast_halves(gi)
        if after is not None:
            halves, after = lax.optimization_barrier((halves, after))
        in_flight[gi] = (halves,) + gather_halves(halves, name=f"gather_weights_{gi}", collective_id=GATHER_COLLECTIVE_ID + gi)
        return after

    def land(gi, after):
        halves, lands, sibs = in_flight[gi]
        (lands, sibs), after = lax.optimization_barrier(((lands, sibs), after))
        for a, mine, got, passed in zip(GATHER_GROUPS[gi], halves, lands, sibs):
            weight, layer, full_shape, split, perm = BIG_ARRAYS[a]
            half_mine = jnp.where(is_my_chip, jnp.where(core == 0, mine[0], mine[1])[None], got)
            half_other = jnp.where(is_my_chip, jnp.where(core == 0, mine[1], mine[0])[None], passed)
            value = by_core(half_mine, half_other).transpose(tuple(np.argsort(perm))).reshape(full_shape)
            if layer is None:
                full[weight] = value
            else:
                full.setdefault(weight, [None, None])[layer] = value
        return after

    reducing = {}

    def reduce_ready(gi, grads, then=None, extra=()):
        def travelling(a):
            split, perm = _travel_layout(a)
            return grads[a] if grads[a].ndim == 4 else grads[a].reshape(split).transpose(perm)

        arrays = [travelling(a) for a in REDUCE_GROUPS[gi]] + list(extra)
        scatter = [True] * len(REDUCE_GROUPS[gi]) + [False] * len(extra)
        reducing[gi], then = reduce_between_cores(arrays, scatter, tag=str(gi), collective_id=REDUCE_COLLECTIVE_ID + 3 * gi, before=then)
        return then

    def reduce_send(gi, then=None):
        reducing[gi], then = reduce_between_chips(reducing[gi], before=then)
        return then

    def stage(name, tensors, grads=None):
        if name == "start":
            launch(0)
            launch(1)
            fillers = (cast_halves(2), cast_halves(3), [full[n] for n in vec_names])
            (bf16_halves[2], bf16_halves[3], gathered_small), tensors = lax.optimization_barrier((fillers, tensors))
            full.update(zip(vec_names, gathered_small))
            return land(0, tensors)
        if name == "normed":
            return launch(3, launch(2, tensors))
        if name in ("mixed", "layer0", "mixed1"):
            return land({"mixed": 1, "layer0": 2, "mixed1": 3}[name], tensors)
        gi = int(name[len("grads")])
        return reduce_ready(gi, grads, tensors) if name.endswith("_ready") else reduce_send(gi, tensors)

    small_names = [n for n in rp_names if n not in BLOCK_WEIGHTS] + vec_names

    loss_part, dx, grads = local_step(x[0], loss_target[0], full, stage)
    small_shapes = [grads[n].shape for n in small_names] + [(1, 1)]
    small_rows = _pack_rows(sum(int(np.prod(s)) for s in small_shapes), 16)
    small = _pack([grads[n] for n in small_names] + [loss_part[:, :1]], small_rows, F32).reshape(2, 1, small_rows // 2, LANES)
    last = len(REDUCE_GROUPS) - 1
    halves_of_blocks = [grads[n].reshape(2, 1, LRU_BLOCKS * HEAD // 2, HEAD) for n in BLOCK_WEIGHTS]
    reduce_ready(last, grads, extra=[small] + halves_of_blocks)
    reduce_send(last)
    reduced, result = {}, {}

    def finish(gi, after):
        g_own, g_sib = reduce_finish(reducing[gi], after)
        reduced.update(zip(list(REDUCE_GROUPS[gi]) + ["small"] + list(BLOCK_WEIGHTS), zip(g_own, g_sib)))

    def update(n):
        if n in TRANSPOSED:
            n_rows, n_cols = given[n].shape[2], given[n].shape[1]

            def rows(t):
                return jnp.swapaxes(t, 1, 2).reshape(n_rows, 1, n_cols)

            def back(t):
                return jnp.swapaxes(t.reshape(1, n_rows, n_cols), 1, 2)

            g_rows = jnp.swapaxes(by_core(*reduced[n]), 0, 1).reshape(n_rows, 1, n_cols)
            result[n] = tuple(back(t) for t in adamw_rows(rows(given[n]), g_rows, rows(mom1[n]), rows(mom2[n]), name=f"adamw_{n}"))
            return
        done = None
        for a in (k for k, spec in BIG_ARRAYS.items() if spec[0] == n):
            r, cols = reduced[a][0].shape
            layer = BIG_ARRAYS[a][1] or 0
            w3, m3, v3 = (t if BIG_ARRAYS[a][1] is not None else t.reshape(1, 2 * r, cols) for t in (given[n], mom1[n], mom2[n]))
            done = adamw_halves(w3, m3, v3, *reduced[a], layer=layer, prev=done, name=f"adamw_{a}")
        result[n] = done

    for gi in range(last):
        finish(gi, (dx, reducing[last][1]))
    late = {BIG_ARRAYS[a][0] for a in REDUCE_GROUPS[last]}
    for n in MATMUL_SHARDED:
        if n not in late:
            update(n)
    finish(last, tuple(result[n][0] for n in MATMUL_SHARDED if n not in late))
    for n in MATMUL_SHARDED:
        if n in late:
            update(n)

    for n in BLOCK_WEIGHTS:
        w3, m3, v3 = (t.reshape(1, LRU_BLOCKS * HEAD, HEAD) for t in (given[n], mom1[n], mom2[n]))
        result[n] = adamw_halves(w3, m3, v3, *reduced[n], name=f"adamw_{n}")

    *small_sums, loss_sum = _unpack(by_core(*reduced["small"]).reshape(small_rows, LANES), small_shapes)
    loss = loss_sum[0, 0]
    g_small = dict(zip(small_names, small_sums))
    for n in vec_names:
        size = local[n].shape[SHARDED[n]]
        g_small[n] = lax.dynamic_slice_in_dim(g_small[n], chip * size, size, axis=SHARDED[n])
    views = [[_local_view(n, src[n]) for n in small_names] for src in (given, mom1, mom2)]
    d_s, m_s, v_s = adamw_many(views[0], [g_small[n] for n in small_names], views[1], views[2], name="adamw_small")
    for n, d, nm, nv in zip(small_names, d_s, m_s, v_s):
        result[n] = (g_small[n], d, nm, nv)

    outs = [[result[n][k].reshape(given[n].shape) for n in WEIGHTS] for k in range(4)]
    return (loss, dx[None], *outs[0], *outs[1], *outs[2], *outs[3])
```

```python
import functools

import numpy as np
import jax
import jax.numpy as jnp
from jax import lax
from jax.experimental import pallas as pl
from jax.experimental.pallas import tpu as pltpu
from jax.experimental.pallas import tpu_sc as plsc

F32 = jnp.float32
BF16 = jnp.bfloat16
HI = lax.Precision.HIGHEST
MESH = pl.DeviceIdType.MESH

SEQ = 2048
D_MODEL = 1024
N_HEADS = 4
HEAD = 128
RET_CHUNK = 128
RET_CHUNKS_PER_STEP = 2
GDN_CHUNK = 64
GDN_CHUNKS_PER_STEP = 8
GROUP = N_HEADS * HEAD
MIX_MAIN = 8 * GROUP
D_FF = 2816
LRU_BLOCKS = 8
LRU_C = 8.0
ROPE_BASE = 10000.0
EPS = 1e-6
N_SHARD = 4
LANES = 128

ADAM_LR, ADAM_B1, ADAM_B2, ADAM_EPS, ADAM_WD, ADAM_STEP = 0.001, 0.9, 0.999, 1e-08, 0.01, 10

VMEM_LIMIT_BYTES = 56 * 1024 * 1024

_roll = pltpu.roll


def _params(**kw):
    return pltpu.CompilerParams(vmem_limit_bytes=VMEM_LIMIT_BYTES, **kw)


def _sds(shape, dtype):
    return jax.ShapeDtypeStruct(tuple(shape), dtype)


def _shift_raw(x, d):
    n = x.shape[0]
    t = lax.broadcasted_iota(jnp.int32, x.shape, 0)
    if d > 0:
        return jnp.where(t >= d, _roll(x, d, 0), 0.0)
    return jnp.where(t < n + d, _roll(x, n + d, 0), 0.0)


@functools.partial(jax.custom_vjp, nondiff_argnums=(1,))
def shift_rows(x, d):
    return _shift_raw(x, d)


def _shift_fwd(x, d):
    return _shift_raw(x, d), None


def _shift_bwd(d, _, g):
    return (_shift_raw(g, -d),)


shift_rows.defvjp(_shift_fwd, _shift_bwd)


@jax.custom_vjp
def swap_halves(x):
    return _roll(x, HEAD // 2, 1)


def _swap_fwd(x):
    return _roll(x, HEAD // 2, 1), None


def _swap_bwd(_, g):
    return (_roll(g, HEAD // 2, 1),)


swap_halves.defvjp(_swap_fwd, _swap_bwd)


SCAN_BLOCK_ROWS = 64


def _scan_block(a, u, reverse):
    n = a.shape[0]
    t = lax.broadcasted_iota(jnp.int32, a.shape, 0)
    d = 1
    while d < n:
        if reverse:
            m = t < n - d
            a_s, u_s = _roll(a, n - d, 0), _roll(u, n - d, 0)
        else:
            m = t >= d
            a_s, u_s = _roll(a, d, 0), _roll(u, d, 0)
        u = a * jnp.where(m, u_s, 0.0) + u
        a = a * jnp.where(m, a_s, 1.0)
        d *= 2
    return a, u


def _scan_raw(a, u, reverse):
    n = a.shape[0]
    blocks = range(n // SCAN_BLOCK_ROWS)
    out = [None] * len(blocks)
    entering = None
    for b in (reversed(blocks) if reverse else blocks):
        rows = slice(b * SCAN_BLOCK_ROWS, (b + 1) * SCAN_BLOCK_ROWS)
        a_run, h = _scan_block(a[rows], u[rows], reverse)
        if entering is not None:
            h = a_run * entering + h
        out[b] = h
        entering = h[:1] if reverse else h[SCAN_BLOCK_ROWS - 1:]
    return jnp.concatenate(out, axis=0)


@jax.custom_vjp
def lin_scan(a, u):
    return _scan_raw(a, u, False)


def _lin_scan_fwd(a, u):
    hs = _scan_raw(a, u, False)
    return hs, (a, hs)


def _lin_scan_bwd(res, g):
    a, hs = res
    lam = _scan_raw(_shift_raw(a, -1), g, True)
    return lam * _shift_raw(hs, 1), lam


lin_scan.defvjp(_lin_scan_fwd, _lin_scan_bwd)


def _bdot(a, b, dims=(((1,), (0,)), ((), ()))):
    return lax.dot_general(a.astype(BF16), b.astype(BF16), dims, preferred_element_type=F32)


def _each(f, *seqs):
    return tuple(f(*a) for a in zip(*seqs))


def _split_bf16(a):
    hi = a.astype(BF16)
    return hi, (a - hi.astype(F32)).astype(BF16)


def _dot3_raw(a_s, b_s):
    a_hl = _each(_split_bf16, a_s)
    b_hl = _each(_split_bf16, b_s)
    hh = _each(lambda a, b: _bdot(a[0], b[0]), a_hl, b_hl)
    hl = _each(lambda a, b: _bdot(a[0], b[1]), a_hl, b_hl)
    lh = _each(lambda a, b: _bdot(a[1], b[0]), a_hl, b_hl)
    return _each(lambda x, y, z: x + (y + z), hh, hl, lh)


@jax.custom_vjp
def dot3(a_s, b_s):
    return _dot3_raw(a_s, b_s)


def _dot3_fwd(a_s, b_s):
    return _dot3_raw(a_s, b_s), (a_s, b_s)


def _dot3_bwd(res, g_s):
    a_s, b_s = res
    return (_each(lambda g, b: _bdot(g, b, (((1,), (1,)), ((), ()))), g_s, b_s),
            _each(lambda a, g: _bdot(a, g, (((0,), (0,)), ((), ()))), a_s, g_s))


dot3.defvjp(_dot3_fwd, _dot3_bwd)


def _eye(n):
    i = lax.broadcasted_iota(jnp.int32, (n, n), 0)
    j = lax.broadcasted_iota(jnp.int32, (n, n), 1)
    return (i == j).astype(F32)


def _unit_lower_inverse_raw(lmats):
    n = lmats[0].shape[0]
    eye = _eye(n)
    ps = _each(lambda l: -l, lmats)
    invs = _each(lambda x: eye + x, ps)
    k = 1
    while 2 * k < n:
        ps = _each(lambda p: _bdot(p, p), ps)
        invs = _each(lambda inv, p: inv + _bdot(inv, p), invs, ps)
        k *= 2
    prods = _dot3_raw(lmats, invs)
    resids = _each(lambda inv, pr: eye - inv - pr, invs, prods)
    return _each(lambda inv, r: inv + _bdot(inv, r), invs, resids)


@jax.custom_vjp
def unit_lower_inverse(lmats):
    return _unit_lower_inverse_raw(lmats)


def _uli_fwd(lmats):
    invs = _unit_lower_inverse_raw(lmats)
    return invs, invs


def _uli_bwd(invs, g_s):
    ms = _each(lambda inv, g: _bdot(inv, g, (((0,), (0,)), ((), ()))), invs, g_s)
    return (_each(lambda m, inv: -_bdot(m, inv, (((1,), (1,)), ((), ()))), ms, invs),)


unit_lower_inverse.defvjp(_uli_fwd, _uli_bwd)


def _cumsum_raw(x, reverse):
    n = x.shape[0]
    t = lax.broadcasted_iota(jnp.int32, x.shape, 0)
    d = 1
    while d < n:
        if reverse:
            x = x + jnp.where(t < n - d, _roll(x, n - d, 0), 0.0)
        else:
            x = x + jnp.where(t >= d, _roll(x, d, 0), 0.0)
        d *= 2
    return x


@jax.custom_vjp
def cumsum_rows(x):
    return _cumsum_raw(x, False)


def _cumsum_fwd(x):
    return _cumsum_raw(x, False), None


def _cumsum_bwd(_, g):
    return (_cumsum_raw(g, True),)


cumsum_rows.defvjp(_cumsum_fwd, _cumsum_bwd)


_NT = (((1,), (1,)), ((), ()))
_TN = (((0,), (0,)), ((), ()))


def _softplus(x):
    return jnp.maximum(x, 0.0) + jnp.log1p(jnp.exp(-jnp.abs(x)))


def _expm1_nonpos(x):
    poly = x * (1.0 + x * (0.5 + x * (1.0 / 6 + x * (1.0 / 24 + x * (1.0 / 120 + x * (1.0 / 720))))))
    return jnp.where(x > -0.25, poly, jnp.exp(x) - 1.0)


def _rms(x):
    return x * lax.rsqrt(jnp.mean(x * x, axis=-1, keepdims=True) + EPS)


def _causal_conv(x, w, width):
    y = w[width - 1:width, :] * x
    for j in range(width - 1):
        y = y + w[j:j + 1, :] * shift_rows(x, width - 1 - j)
    return y


def _norm_fn(x, g):
    return _rms(x) * g


def _ffn_act_fn(ug, uv, wg, wv, bg, bv):
    return jax.nn.silu(_causal_conv(ug, wg, 3) + bg) * (_causal_conv(uv, wv, 3) + bv)


def _gdn_conv_fn(x, w):
    return jax.nn.silu(_causal_conv(x, w, 4))


def _lru_fn(gate, x, cw, cb, wa, ba, wx, bx, lam):
    xr = _causal_conv(x, cw, 4) + cb
    r = jax.nn.sigmoid(_bdot(xr, wa) + ba)
    i = jax.nn.sigmoid(_bdot(xr, wx) + bx)
    log_a = -LRU_C * r * _softplus(-lam)
    a = jnp.exp(log_a)
    u = jnp.sqrt(-_expm1_nonpos(2.0 * log_a)) * (i * xr)
    hs = lin_scan(a, u)
    return jax.nn.gelu(gate) * hs


def _ret_fn(qs, ks, vs, gates, states, cos2, sin2, dmasks, ktails, qdecs, cdecs):
    c = RET_CHUNK
    n_heads = len(qs)
    n_chunks = qs[0].shape[0] // c
    units = tuple((ci, h) for ci in range(n_chunks) for h in range(n_heads))

    def rows(x, ci):
        return x[ci * c:(ci + 1) * c]

    qrs = tuple(rows(qs[h], ci) * rows(cos2, ci) + swap_halves(rows(qs[h], ci)) * rows(sin2, ci) for ci, h in units)
    krs = tuple((rows(ks[h], ci) * rows(cos2, ci) + swap_halves(rows(ks[h], ci)) * rows(sin2, ci)) * (HEAD ** -0.5) for ci, h in units)
    vus = tuple(rows(vs[h], ci) for ci, h in units)
    scores = tuple(_bdot(q, k, _NT) * dmasks[h] for q, k, (_, h) in zip(qrs, krs, units))
    intra = _each(lambda sc, v: _bdot(sc, v), scores, vus)
    outs = []
    for ci in range(n_chunks):
        mine = slice(ci * n_heads, (ci + 1) * n_heads)
        inter = _each(lambda q, d, s: _bdot(q * d, s), qrs[mine], qdecs, states)
        outs.append(_each(lambda a, b: a + b, intra[mine], inter))
        states = _each(lambda s, cd, k, kt, v: s * cd + _bdot(k * kt, v, _TN), states, cdecs, krs[mine], ktails, vus[mine])
    ys = tuple(_rms(jnp.concatenate([outs[ci][h] for ci in range(n_chunks)], axis=0)) * jax.nn.silu(gates[h]) for h in range(n_heads))
    return ys, states


def _pick_lane(x, lane_idx):
    lane = lax.broadcasted_iota(jnp.int32, x.shape, 1)
    return jnp.sum(jnp.where(lane == lane_idx, x, 0.0), axis=1, keepdims=True)


def _l2norm(x):
    return x * lax.rsqrt(jnp.sum(x * x, axis=-1, keepdims=True) + EPS)


def _gdn_fn(qcs, kcs, vcs, gates, small, a_log, dt_bias, gain, states):
    c = GDN_CHUNK
    n_heads = len(qcs)
    n_chunks = qcs[0].shape[0] // c
    units = tuple((ci, h) for ci in range(n_chunks) for h in range(n_heads))

    def unit_rows(per_head):
        return tuple(per_head[h][ci * c:(ci + 1) * c] for ci, h in units)

    smalls = tuple(small[ci * c:(ci + 1) * c] for ci, _ in units)
    heads = tuple(h for _, h in units)
    intra = _gdn_intra(unit_rows(qcs), unit_rows(kcs), unit_rows(vcs), smalls, heads, a_log, dt_bias)
    outs = []
    for ci in range(n_chunks):
        mine = slice(ci * n_heads, (ci + 1) * n_heads)
        os_, states = _gdn_inter(*(part[mine] for part in intra), states)
        outs.append(os_)
    ys = tuple(_rms(jnp.concatenate([outs[ci][h] for ci in range(n_chunks)], axis=0)) * gain * jax.nn.silu(gates[h])
               for h in range(n_heads))
    return ys, states


def _gdn_inter(qs, ks, us, ws, attns, gcs, g_lasts, states):
    v_news = _each(lambda u, w, s: u - _bdot(w, s), us, ws, states)
    inter = _each(lambda q, gc, s: _bdot(q * jnp.exp(gc), s), qs, gcs, states)
    os_ = _each(lambda x, a, v: x + _bdot(a, v), inter, attns, v_news)
    new_states = _each(lambda s, gl, k, gc, v: s * jnp.exp(gl) + _bdot(k * jnp.exp(gl - gc), v, _TN), states, g_lasts, ks, gcs, v_news)
    return os_, new_states


def _gdn_intra(qcs, kcs, vcs, smalls, heads, a_log, dt_bias):
    c = GDN_CHUNK
    qs = _each(lambda x: _l2norm(x) * (HEAD ** -0.5), qcs)
    ks = _each(_l2norm, kcs)
    betas = _each(lambda sm, h: jax.nn.sigmoid(_pick_lane(sm, h)), smalls, heads)
    gs = _each(lambda sm, h: -jnp.exp(_pick_lane(a_log, h)) * _softplus(_pick_lane(sm, h + N_HEADS) + _pick_lane(dt_bias, h)),
               smalls, heads)
    i = lax.broadcasted_iota(jnp.int32, (c, c), 0)
    j = lax.broadcasted_iota(jnp.int32, (c, c), 1)
    tril = i >= j
    gcs = _each(lambda g: cumsum_rows(jnp.broadcast_to(g, (c, LANES)))[:, :1], gs)
    gc_rows = _each(lambda gc: jnp.broadcast_to(gc, (c, c)), gcs)
    decays = _each(lambda r: jnp.where(tril, jnp.exp(jnp.where(tril, r - r.T, 0.0)), 0.0), gc_rows)
    kbs = _each(lambda k, b: k * b, ks, betas)
    lmats = _each(lambda kb, k, d: jnp.where(i > j, _bdot(kb, k, _NT) * d, 0.0), kbs, ks, decays)
    attns = _each(lambda q, k, d: jnp.where(tril, _bdot(q, k, _NT) * d, 0.0), qs, ks, decays)
    invs = unit_lower_inverse(lmats)
    us = dot3(invs, _each(lambda v, b: v * b, vcs, betas))
    ws = dot3(invs, _each(lambda kb, gc: kb * jnp.exp(gc), kbs, gcs))
    g_lasts = _each(lambda g: jnp.sum(g, axis=0, keepdims=True), gs)
    return qs, ks, us, ws, attns, gcs, g_lasts


def _final_fn(h, g, target):
    y = _rms(h) * g
    return 0.5 * jnp.sum(jnp.mean(jnp.square(y - target), axis=-1, keepdims=True), axis=0, keepdims=True)


def _tile(n, candidates):
    for t in candidates:
        if n % t == 0:
            return t
    raise ValueError(f"no tile for {n}")


MATMUL_RESIDENT_LHS_BYTES = 8 * 1024 * 1024


def matmul(a, b, *, ta=False, tb=False, add=None, out_dtype=F32, tm=None, tn=None, split=None, layer=None, column_halves=None, name):
    m = a.shape[1] if ta else a.shape[0]
    k = a.shape[0] if ta else a.shape[1]
    n = b.shape[0] if tb else b.shape[1]
    assert k == (b.shape[1] if tb else b.shape[0])
    out_shape, out_block, out_index = (m, n), None, lambda i, j: (i, j)
    if split is not None:
        dims4, perm = split
        out_shape = tuple(dims4[p] for p in perm)
        r, cols = out_shape[2:]
        tm, tn = m, tn or _tile(cols, (1408, 512))
        cb = cols // tn
        if perm == (0, 2, 1, 3):
            out_block, out_index = (2, None, r, tn), lambda i, j: (0, j // cb, 0, j % cb)
        elif perm == (1, 0, 2, 3):
            out_block, out_index = (2, N_SHARD, r, tn), lambda i, j: (0, 0, 0, j)
        else:
            raise ValueError(perm)
    if tm is None and not ta and m * k * a.dtype.itemsize <= MATMUL_RESIDENT_LHS_BYTES:
        tm = m
    tm = tm or _tile(m, (1024, 512, 1408, 256, 128))
    tn = tn or _tile(n, (512, 1408, 256, 128))
    aliases, prev, keep_rows = {}, None, None
    if layer is not None:
        index, count, prev = layer
        out_shape, out_block, out_index = (count, m, n), (None, tm, tn), lambda i, j: (index, i, j)
    if column_halves is not None:
        total_rows, first_row, keep_rows, prev = column_halves
        tn = n // 2
        rows_out = keep_rows or tm
        out_shape, out_block = (2, total_rows, tn), (None, rows_out, tn)
        out_index = lambda i, j: (j, first_row // rows_out + i, 0)
    dims = (((0 if ta else 1,), (1 if tb else 0,)), ((), ()))

    def body(a_ref, b_ref, *rest):
        acc = lax.dot_general(a_ref[...].astype(BF16), b_ref[...].astype(BF16), dims, preferred_element_type=F32)
        if add is not None:
            acc = acc + rest[0][...]
        o_ref = rest[-1]
        acc = acc.astype(out_dtype)
        if split is not None and split[1] == (1, 0, 2, 3):
            rows = o_ref.shape[2]
            for s in range(N_SHARD):
                for h in range(2):
                    o_ref[h, s] = acc[(2 * s + h) * rows:(2 * s + h + 1) * rows]
        elif keep_rows is not None:
            o_ref[...] = acc[:keep_rows]
        else:
            o_ref[...] = acc.reshape(o_ref.shape)

    a_spec = pl.BlockSpec((k, tm), lambda i, j: (0, i)) if ta else pl.BlockSpec((tm, k), lambda i, j: (i, 0))
    b_spec = pl.BlockSpec((tn, k), lambda i, j: (j, 0)) if tb else pl.BlockSpec((k, tn), lambda i, j: (0, j))
    o_spec = pl.BlockSpec(out_block or (tm, tn), out_index)
    in_specs, args = [a_spec, b_spec], [a, b]
    if add is not None:
        in_specs.append(o_spec)
        args.append(add)
    if prev is not None:
        aliases = {len(args): 0}
        in_specs.append(pl.BlockSpec(memory_space=pl.ANY))
        args.append(prev)
    return pl.pallas_call(body, out_shape=_sds(out_shape, out_dtype), grid=(m // tm, n // tn), in_specs=in_specs,
                          out_specs=o_spec, input_output_aliases=aliases, compiler_params=_params(), name=name)(*args)


def norm_matmul(x, g, b, *, tb=False, name):
    t, k = x.shape
    n = b.shape[0] if tb else b.shape[1]
    tn = _tile(n, (512, 1408, 256, 128))
    dims = (((1,), (1 if tb else 0,)), ((), ()))

    def body(x_ref, g_ref, b_ref, o_ref, hn_ref):
        @pl.when(pl.program_id(0) == 0)
        def _():
            hn_ref[...] = _norm_fn(x_ref[...], g_ref[...]).astype(BF16)

        o_ref[...] = lax.dot_general(hn_ref[...], b_ref[...].astype(BF16), dims, preferred_element_type=F32)

    b_spec = pl.BlockSpec((tn, k), lambda j: (j, 0)) if tb else pl.BlockSpec((k, tn), lambda j: (0, j))
    whole = pl.BlockSpec((t, k), lambda j: (0, 0))
    return pl.pallas_call(body, out_shape=(_sds((t, n), F32), _sds((t, k), BF16)), grid=(n // tn,),
                          in_specs=[whole, pl.BlockSpec((1, k), lambda j: (0, 0)), b_spec],
                          out_specs=(pl.BlockSpec((t, tn), lambda j: (0, j)), whole), compiler_params=_params(), name=name)(x, g, b)


ROW_TILE = 256


def norm_bwd(x, g, dy, dres, *, name):
    t, d = x.shape

    def body(x_ref, g_ref, dy_ref, dres_ref, dx_ref, dg_ref):
        _, vjp = jax.vjp(_norm_fn, x_ref[...], g_ref[...])
        dx, dg = vjp(dy_ref[...])
        dx_ref[...] = dx + dres_ref[...]

        @pl.when(pl.program_id(0) == 0)
        def _():
            dg_ref[...] = jnp.zeros_like(dg_ref)

        dg_ref[...] += dg

    row = pl.BlockSpec((ROW_TILE, d), lambda i: (i, 0))
    vec = pl.BlockSpec((1, d), lambda i: (0, 0))
    return pl.pallas_call(body, out_shape=(_sds((t, d), F32), _sds((1, d), F32)), grid=(t // ROW_TILE,),
                          in_specs=[row, vec, row, row], out_specs=(row, vec), compiler_params=_params(), name=name)(x, g, dy, dres)


def final_fwd_bwd(h, g, target, *, name):
    t, d = h.shape

    def body(h_ref, g_ref, t_ref, loss_ref, dh_ref, dg_ref):
        tgt = t_ref[...]
        loss, vjp = jax.vjp(lambda hh, gg: _final_fn(hh, gg, tgt), h_ref[...], g_ref[...])
        dh, dg = vjp(jnp.ones((1, 1), F32))
        dh_ref[...] = dh

        @pl.when(pl.program_id(0) == 0)
        def _():
            dg_ref[...] = jnp.zeros_like(dg_ref)
            loss_ref[...] = jnp.zeros_like(loss_ref)

        dg_ref[...] += dg
        loss_ref[...] += jnp.broadcast_to(loss, loss_ref.shape)

    row = pl.BlockSpec((ROW_TILE, d), lambda i: (i, 0))
    vec = pl.BlockSpec((1, d), lambda i: (0, 0))
    return pl.pallas_call(body, out_shape=(_sds((1, LANES), F32), _sds((t, d), F32), _sds((1, d), F32)), grid=(t // ROW_TILE,),
                          in_specs=[row, vec, row], out_specs=(pl.BlockSpec((1, LANES), lambda i: (0, 0)), row, vec),
                          compiler_params=_params(), name=name)(h, g, target)


FFN_FWD_COLS = 256
FFN_BWD_COLS = 128


def ffn_act_fwd(u, cw, cb, *, name):
    t = u.shape[0]
    w = FFN_FWD_COLS
    nb = D_FF // w

    def body(ug_ref, uv_ref, wg_ref, wv_ref, bg_ref, bv_ref, o_ref):
        o_ref[...] = _ffn_act_fn(ug_ref[...], uv_ref[...], wg_ref[...], wv_ref[...], bg_ref[...], bv_ref[...]).astype(BF16)

    def col(rows, off):
        return pl.BlockSpec((rows, w), lambda j: (0, j + off))

    return pl.pallas_call(body, out_shape=_sds((t, D_FF), BF16), grid=(nb,),
                          in_specs=[col(t, 0), col(t, nb), col(3, 0), col(3, nb), col(1, 0), col(1, nb)],
                          out_specs=col(t, 0), compiler_params=_params(), name=name)(u, u, cw, cw, cb, cb)


def _put_column_blocks(step, n_steps, blocks, dst_ref, width, stage_ref, sems):
    def copies(at):
        slot = at % 2
        return [pltpu.make_async_copy(stage_ref.at[slot, p], dst_ref.at[:, pl.ds(pl.multiple_of((p * n_steps + at) * width, LANES), width)],
                                      sems.at[slot, p]) for p in range(len(blocks))]

    @pl.when(step >= 2)
    def _():
        for cp in copies(step - 2):
            cp.wait()

    for p, value in enumerate(blocks):
        stage_ref[step % 2, p] = value
    for cp in copies(step):
        cp.start()

    @pl.when(step == n_steps - 1)
    def _():
        for cp in copies(step - 1) + copies(step):
            cp.wait()


def ffn_act_bwd(u, cw, cb, da, *, name):
    t = u.shape[0]
    w = FFN_BWD_COLS
    nb = D_FF // w

    def body(ug_ref, uv_ref, wg_ref, wv_ref, bg_ref, bv_ref, da_ref, dug_ref, duv_ref, dwg_ref, dwv_ref, dbg_ref, dbv_ref):
        _, vjp = jax.vjp(_ffn_act_fn, ug_ref[...], uv_ref[...], wg_ref[...], wv_ref[...], bg_ref[...], bv_ref[...])
        dug, duv, dwg, dwv, dbg, dbv = vjp(da_ref[...])
        dug_ref[...] = dug.astype(BF16)
        duv_ref[...] = duv.astype(BF16)
        dwg_ref[...] = dwg
        dwv_ref[...] = dwv
        dbg_ref[...] = dbg
        dbv_ref[...] = dbv

    def col(rows, off):
        return pl.BlockSpec((rows, w), lambda j: (0, j + off))

    outs = pl.pallas_call(
        body, out_shape=(_sds((t, D_FF), BF16), _sds((t, D_FF), BF16), _sds((3, D_FF), F32), _sds((3, D_FF), F32),
                         _sds((1, D_FF), F32), _sds((1, D_FF), F32)),
        grid=(nb,), in_specs=[col(t, 0), col(t, nb), col(3, 0), col(3, nb), col(1, 0), col(1, nb), col(t, 0)],
        out_specs=(col(t, 0), col(t, 0), col(3, 0), col(3, 0), col(1, 0), col(1, 0)), compiler_params=_params(), name=name,
    )(u, u, cw, cw, cb, cb, da)
    dug, duv, dwg, dwv, dbg, dbv = outs
    return jnp.concatenate([dug, duv], axis=1), jnp.concatenate([dwg, dwv], axis=1), jnp.concatenate([dbg, dbv], axis=1)


GDN_CONV_COLS = 256
GDN_CONV_OFF = 4 * GROUP


def gdn_conv_fwd(p, cw, *, name):
    t = p.shape[0]
    w = GDN_CONV_COLS
    nb = 3 * GROUP // w
    off = GDN_CONV_OFF // w

    def body(x_ref, w_ref, o_ref):
        o_ref[...] = _gdn_conv_fn(x_ref[...], w_ref[...])

    return pl.pallas_call(body, out_shape=_sds((t, 3 * GROUP), F32), grid=(nb,),
                          in_specs=[pl.BlockSpec((t, w), lambda j: (0, j + off)), pl.BlockSpec((4, w), lambda j: (0, j))],
                          out_specs=pl.BlockSpec((t, w), lambda j: (0, j)), compiler_params=_params(), name=name)(p, cw)


def gdn_conv_bwd(p, cw, dc, *, name):
    t = p.shape[0]
    w = GDN_CONV_COLS
    nb = 3 * GROUP // w
    off = GDN_CONV_OFF // w

    def body(x_ref, w_ref, dc_ref, dx_ref, dw_ref):
        _, vjp = jax.vjp(_gdn_conv_fn, x_ref[...], w_ref[...])
        dx, dw = vjp(dc_ref[...])
        dx_ref[...] = dx.astype(BF16)
        dw_ref[...] = dw

    blk = pl.BlockSpec((t, w), lambda j: (0, j))
    wblk = pl.BlockSpec((4, w), lambda j: (0, j))
    return pl.pallas_call(body, out_shape=(_sds((t, 3 * GROUP), BF16), _sds((4, 3 * GROUP), F32)), grid=(nb,),
                          in_specs=[pl.BlockSpec((t, w), lambda j: (0, j + off)), wblk, blk], out_specs=(blk, wblk),
                          compiler_params=_params(), name=name)(p, cw, dc)


def _lru_specs(t):
    w = D_MODEL // LRU_BLOCKS
    gate = pl.BlockSpec((t, w), lambda j: (0, j))
    xin = pl.BlockSpec((t, w), lambda j: (0, j + LRU_BLOCKS))
    cw = pl.BlockSpec((4, w), lambda j: (0, j))
    vec = pl.BlockSpec((1, w), lambda j: (0, j))
    mat = pl.BlockSpec((None, w, w), lambda j: (j, 0, 0))
    return gate, xin, cw, vec, mat


def lru_fwd(gx, cw, cb, wa, ba, wx, bx, lam, *, name):
    t = gx.shape[0]
    gate, xin, cws, vec, mat = _lru_specs(t)

    def body(g_ref, x_ref, cw_ref, cb_ref, wa_ref, ba_ref, wx_ref, bx_ref, lam_ref, o_ref):
        o_ref[...] = _lru_fn(g_ref[...], x_ref[...], cw_ref[...], cb_ref[...], wa_ref[...], ba_ref[...], wx_ref[...],
                             bx_ref[...], lam_ref[...]).astype(BF16)

    return pl.pallas_call(body, out_shape=_sds((t, D_MODEL), BF16), grid=(LRU_BLOCKS,),
                          in_specs=[gate, xin, cws, vec, mat, vec, mat, vec, vec], out_specs=gate,
                          compiler_params=_params(), name=name)(gx, gx, cw, cb, wa, ba, wx, bx, lam)


def lru_bwd(gx, cw, cb, wa, ba, wx, bx, lam, dy, *, name):
    t = gx.shape[0]
    gate, xin, cws, vec, mat = _lru_specs(t)

    def body(g_ref, x_ref, cw_ref, cb_ref, wa_ref, ba_ref, wx_ref, bx_ref, lam_ref, dy_ref,
             dgx_ref, dcw_ref, dcb_ref, dwa_ref, dba_ref, dwx_ref, dbx_ref, dlam_ref, stage_ref, sems):
        _, vjp = jax.vjp(_lru_fn, g_ref[...], x_ref[...], cw_ref[...], cb_ref[...], wa_ref[...], ba_ref[...], wx_ref[...],
                         bx_ref[...], lam_ref[...])
        dg, dx, dcw, dcb, dwa, dba, dwx, dbx, dlam = vjp(dy_ref[...])
        _put_column_blocks(pl.program_id(0), LRU_BLOCKS, (dg.astype(BF16), dx.astype(BF16)), dgx_ref, D_MODEL // LRU_BLOCKS, stage_ref, sems)
        dcw_ref[...] = dcw
        dcb_ref[...] = dcb
        dwa_ref[...] = dwa
        dba_ref[...] = dba
        dwx_ref[...] = dwx
        dbx_ref[...] = dbx
        dlam_ref[...] = dlam

    d = D_MODEL
    w = d // LRU_BLOCKS
    out_shape = (_sds((t, 2 * d), BF16), _sds((4, d), F32), _sds((1, d), F32), _sds((LRU_BLOCKS, w, w), F32),
                 _sds((1, d), F32), _sds((LRU_BLOCKS, w, w), F32), _sds((1, d), F32), _sds((1, d), F32))
    return pl.pallas_call(body, out_shape=out_shape, grid=(LRU_BLOCKS,),
                          in_specs=[gate, xin, cws, vec, mat, vec, mat, vec, vec, gate],
                          out_specs=(pl.BlockSpec(memory_space=pl.ANY), cws, vec, mat, vec, mat, vec, vec),
                          scratch_shapes=[pltpu.VMEM((2, 2, t, w), BF16), pltpu.SemaphoreType.DMA((2, 2))],
                          compiler_params=_params(), name=name)(gx, gx, cw, cb, wa, ba, wx, bx, lam, dy)


def _ret_tables():
    half = HEAD // 2
    inv_freq = (np.float32(ROPE_BASE) ** (-np.arange(half, dtype=np.float32) / np.float32(half))).astype(np.float32)
    ang = (np.arange(SEQ, dtype=np.float32)[:, None] * inv_freq[None, :]).astype(np.float64)
    cos2 = np.concatenate([np.cos(ang), np.cos(ang)], axis=1).astype(np.float32)
    sin2 = np.concatenate([-np.sin(ang), np.sin(ang)], axis=1).astype(np.float32)
    c = RET_CHUNK
    log_gamma = np.log1p(-np.exp2(-5.0 - np.arange(N_HEADS, dtype=np.float64)))
    idx = np.arange(c, dtype=np.float64)
    rel = idx[:, None] - idx[None, :]
    dmask = np.where(rel >= 0, np.exp(log_gamma[:, None, None] * np.maximum(rel, 0.0)), 0.0)
    ones = np.ones((N_HEADS, c, HEAD))
    ktail = np.exp(log_gamma[:, None] * (c - 1 - idx))[:, :, None] * ones
    qdec = np.exp(log_gamma[:, None] * (idx + 1.0))[:, :, None] * ones
    cdec = np.exp(log_gamma * c)[:, None, None] * ones
    return tuple(jnp.asarray(a, F32) for a in (cos2, sin2, dmask, ktail, qdec, cdec))


def _ret_specs(rev):
    c = RET_CHUNK * RET_CHUNKS_PER_STEP
    nc = SEQ // c

    def n_of(n):
        return nc - 1 - n if rev else n

    def group(off):
        return pl.BlockSpec((c, GROUP), lambda n: (n_of(n), off))

    tab = pl.BlockSpec((c, HEAD), lambda n: (n_of(n), 0))
    const = pl.BlockSpec((N_HEADS, RET_CHUNK, HEAD), lambda n: (0, 0, 0))
    state = pl.BlockSpec((N_HEADS, None, HEAD, HEAD), lambda n: (0, n_of(n), 0, 0))
    return group, tab, const, state, nc


def _head(h):
    return slice(h * HEAD, (h + 1) * HEAD)


def ret_fwd(p, tables, *, name):
    group, tab, const, state, nc = _ret_specs(False)

    def body(q_ref, k_ref, v_ref, g_ref, cos_ref, sin_ref, dm_ref, kt_ref, qd_ref, cd_ref, y_ref, st_ref, s_scr):
        @pl.when(pl.program_id(0) == 0)
        def _():
            s_scr[...] = jnp.zeros_like(s_scr)

        heads = range(N_HEADS)
        states = tuple(s_scr[h] for h in heads)
        ys, new_states = _ret_fn(*(tuple(r[:, _head(h)] for h in heads) for r in (q_ref, k_ref, v_ref, g_ref)), states,
                                 cos_ref[...], sin_ref[...], *(tuple(r[h] for h in heads) for r in (dm_ref, kt_ref, qd_ref, cd_ref)))
        for h in heads:
            st_ref[h] = states[h]
            y_ref[:, _head(h)] = ys[h].astype(BF16)
            s_scr[h] = new_states[h]

    return pl.pallas_call(
        body, out_shape=(_sds((SEQ, 2 * GROUP), BF16), _sds((N_HEADS, nc, HEAD, HEAD), F32)), grid=(nc,),
        in_specs=[group(0), group(1), group(2), group(3), tab, tab, const, const, const, const],
        out_specs=(group(0), state), scratch_shapes=[pltpu.VMEM((N_HEADS, HEAD, HEAD), F32)], compiler_params=_params(), name=name,
    )(p, p, p, p, *tables)


def ret_bwd(p, tables, states, dy, *, name):
    group, tab, const, state, nc = _ret_specs(True)

    def body(q_ref, k_ref, v_ref, g_ref, cos_ref, sin_ref, dm_ref, kt_ref, qd_ref, cd_ref, st_ref, dy_ref,
             dq_ref, dk_ref, dv_ref, dg_ref, ds_scr):
        @pl.when(pl.program_id(0) == 0)
        def _():
            ds_scr[...] = jnp.zeros_like(ds_scr)

        heads = range(N_HEADS)
        consts = (cos_ref[...], sin_ref[...], *(tuple(r[h] for h in heads) for r in (dm_ref, kt_ref, qd_ref, cd_ref)))
        _, vjp = jax.vjp(lambda *a: _ret_fn(*a, *consts), *(tuple(r[:, _head(h)] for h in heads) for r in (q_ref, k_ref, v_ref, g_ref)),
                         tuple(st_ref[h] for h in heads))
        dqs, dks, dvs, dgs, dss = vjp((tuple(dy_ref[:, _head(h)] for h in heads), tuple(ds_scr[h] for h in heads)))
        for h in heads:
            dq_ref[:, _head(h)] = dqs[h].astype(BF16)
            dk_ref[:, _head(h)] = dks[h].astype(BF16)
            dv_ref[:, _head(h)] = dvs[h].astype(BF16)
            dg_ref[:, _head(h)] = dgs[h].astype(BF16)
            ds_scr[h] = dss[h]

    out = _sds((SEQ, GROUP), BF16)
    return pl.pallas_call(
        body, out_shape=(out, out, out, out), grid=(nc,),
        in_specs=[group(0), group(1), group(2), group(3), tab, tab, const, const, const, const, state, group(0)],
        out_specs=(group(0), group(0), group(0), group(0)), scratch_shapes=[pltpu.VMEM((N_HEADS, HEAD, HEAD), F32)],
        compiler_params=_params(), name=name,
    )(p, p, p, p, *tables, states, dy)


def _gdn_specs(rev):
    c = GDN_CHUNK * GDN_CHUNKS_PER_STEP
    nc = SEQ // c

    def n_of(n):
        return nc - 1 - n if rev else n

    def group(off):
        return pl.BlockSpec((c, GROUP), lambda n: (n_of(n), off))

    small = pl.BlockSpec((c, LANES), lambda n: (n_of(n), 0))
    vec = pl.BlockSpec((1, LANES), lambda n: (0, 0))
    state = pl.BlockSpec((N_HEADS, None, HEAD, HEAD), lambda n: (0, n_of(n), 0, 0))
    qkv = pl.BlockSpec((c, 3 * GROUP), lambda n: (n_of(n), 0))
    return group, small, vec, state, qkv, nc


GDN_GATE_GROUP = 7


def gdn_fwd(conv, p, small, a_log, dt_bias, gain, y_started, *, name):
    group, sm, vec, state, _, nc = _gdn_specs(False)

    def body(q_ref, k_ref, v_ref, g_ref, sm_ref, al_ref, dt_ref, gn_ref, _, y_ref, st_ref, s_scr):
        @pl.when(pl.program_id(0) == 0)
        def _():
            s_scr[...] = jnp.zeros_like(s_scr)

        states = tuple(s_scr[h] for h in range(N_HEADS))
        ys, new_states = _gdn_fn(*(tuple(r[:, _head(h)] for h in range(N_HEADS)) for r in (q_ref, k_ref, v_ref, g_ref)),
                                 sm_ref[...], al_ref[...], dt_ref[...], gn_ref[...], states)
        for h in range(N_HEADS):
            st_ref[h] = states[h]
            y_ref[:, _head(h)] = ys[h].astype(BF16)
            s_scr[h] = new_states[h]

    return pl.pallas_call(
        body, out_shape=(_sds((SEQ, 2 * GROUP), BF16), _sds((N_HEADS, nc, HEAD, HEAD), F32)), grid=(nc,),
        in_specs=[group(0), group(1), group(2), group(GDN_GATE_GROUP), sm, vec, vec, vec, pl.BlockSpec(memory_space=pl.ANY)],
        out_specs=(group(1), state), input_output_aliases={8: 0},
        scratch_shapes=[pltpu.VMEM((N_HEADS, HEAD, HEAD), F32)], compiler_params=_params(), name=name,
    )(conv, conv, conv, p, small, a_log, dt_bias, gain, y_started)


def gdn_bwd(conv, p, small, a_log, dt_bias, gain, states, dy, *, name):
    group, sm, vec, state, qkv, nc = _gdn_specs(True)

    def body(q_ref, k_ref, v_ref, g_ref, sm_ref, al_ref, dt_ref, gn_ref, st_ref, dy_ref,
             dqkv_ref, dg_ref, dsm_ref, dal_ref, ddt_ref, dgn_ref, ds_scr):
        @pl.when(pl.program_id(0) == 0)
        def _():
            ds_scr[...] = jnp.zeros_like(ds_scr)
            dal_ref[...] = jnp.zeros_like(dal_ref)
            ddt_ref[...] = jnp.zeros_like(ddt_ref)
            dgn_ref[...] = jnp.zeros_like(dgn_ref)

        per_head = tuple(tuple(r[:, _head(h)] for h in range(N_HEADS)) for r in (q_ref, k_ref, v_ref, g_ref))
        _, vjp = jax.vjp(_gdn_fn, *per_head, sm_ref[...], al_ref[...], dt_ref[...], gn_ref[...],
                         tuple(st_ref[h] for h in range(N_HEADS)))
        cts = (tuple(dy_ref[:, _head(h)] for h in range(N_HEADS)), tuple(ds_scr[h] for h in range(N_HEADS)))
        dqs, dks, dvs, dgs, dsm, dal, ddt, dgn, dss = vjp(cts)
        for h in range(N_HEADS):
            for part, blocks in enumerate((dqs, dks, dvs)):
                dqkv_ref[:, part * GROUP + h * HEAD:part * GROUP + (h + 1) * HEAD] = blocks[h]
            dg_ref[:, _head(h)] = dgs[h].astype(BF16)
            ds_scr[h] = dss[h]
        dsm_ref[...] = dsm
        dal_ref[...] += dal
        ddt_ref[...] += ddt
        dgn_ref[...] += dgn

    pv = _sds((1, LANES), F32)
    return pl.pallas_call(
        body, out_shape=(_sds((SEQ, 3 * GROUP), F32), _sds((SEQ, GROUP), BF16), _sds((SEQ, LANES), F32), pv, pv, pv), grid=(nc,),
        in_specs=[group(0), group(1), group(2), group(GDN_GATE_GROUP), sm, vec, vec, vec, state, group(1)],
        out_specs=(qkv, group(0), sm, vec, vec, vec), scratch_shapes=[pltpu.VMEM((N_HEADS, HEAD, HEAD), F32)],
        compiler_params=_params(), name=name,
    )(conv, conv, conv, p, small, a_log, dt_bias, gain, states, dy)


ELEMENTWISE_BLOCK_BYTES = 2 * 1024 * 1024


def _row_tile(r, c):
    best = None
    for tr in range(8, r + 1, 8):
        if r % tr == 0 and tr * c * 4 <= ELEMENTWISE_BLOCK_BYTES:
            best = tr
    if best is None:
        raise ValueError(f"no row tile for ({r}, {c})")
    return best


def _tile_2d(r, c):
    if any(r % tr == 0 for tr in range(8, r + 1, 8)):
        return _row_tile(r, c), c
    tc = max(t for t in range(LANES, c + 1, LANES) if c % t == 0 and r * t * 4 <= ELEMENTWISE_BLOCK_BYTES)
    return r, tc


def _core_index():
    return lax.axis_index("c").astype(jnp.int32).reshape(1)


def _chip_index():
    return (2 * lax.axis_index("x") + lax.axis_index("y")).astype(jnp.int32).reshape(1)


def adamw_halves(w, m, v, g_own, g_sib, *, layer=0, prev=None, name):
    n_layers, rows, c = w.shape
    r = rows // 2
    tr = _row_tile(r, c)
    nb = r // tr

    def body(c_ref, w_ref, m_ref, v_ref, own_ref, sib_ref, *rest):
        g_ref, d_ref, nm_ref, nv_ref = rest[-4:]
        gg = jnp.where(pl.program_id(0) == c_ref[0], own_ref[...], sib_ref[...])
        nm = ADAM_B1 * m_ref[...] + (1.0 - ADAM_B1) * gg
        nv = ADAM_B2 * v_ref[...] + (1.0 - ADAM_B2) * jnp.square(gg)
        m_hat = nm / (1.0 - ADAM_B1 ** ADAM_STEP)
        v_hat = nv / (1.0 - ADAM_B2 ** ADAM_STEP)
        g_ref[...] = gg
        d_ref[...] = -ADAM_LR * (m_hat / (jnp.sqrt(v_hat) + ADAM_EPS) + ADAM_WD * w_ref[...])
        nm_ref[...] = nm
        nv_ref[...] = nv

    full = pl.BlockSpec((None, tr, c), lambda h, i, cr: (layer, h * nb + i, 0))
    half = pl.BlockSpec((tr, c), lambda h, i, cr: (i, 0))
    o = _sds((n_layers, rows, c), F32)
    prev = list(prev or ())
    gs = pltpu.PrefetchScalarGridSpec(num_scalar_prefetch=1, grid=(2, nb), in_specs=[full, full, full, half, half] + [_ANY] * len(prev),
                                      out_specs=(full, full, full, full))
    n_fixed = 6
    return pl.pallas_call(body, out_shape=(o, o, o, o), grid_spec=gs, compiler_params=_params(), name=name,
                          input_output_aliases={n_fixed + k: k for k in range(len(prev))})(
        _core_index(), w, m, v, g_own, g_sib, *prev)


ADAMW_ROW_STEPS = 6


def adamw_rows(w, g, m, v, *, name):
    rows, _, cols = w.shape
    tr = rows // ADAMW_ROW_STEPS

    def body(w_ref, g_ref, m_ref, v_ref, g_out_ref, d_ref, nm_ref, nv_ref):
        gg = g_ref[...]
        nm = ADAM_B1 * m_ref[...] + (1.0 - ADAM_B1) * gg
        nv = ADAM_B2 * v_ref[...] + (1.0 - ADAM_B2) * jnp.square(gg)
        m_hat = nm / (1.0 - ADAM_B1 ** ADAM_STEP)
        v_hat = nv / (1.0 - ADAM_B2 ** ADAM_STEP)
        g_out_ref[...] = gg
        d_ref[...] = -ADAM_LR * (m_hat / (jnp.sqrt(v_hat) + ADAM_EPS) + ADAM_WD * w_ref[...])
        nm_ref[...] = nm
        nv_ref[...] = nv

    blk = pl.BlockSpec((tr, 1, cols), lambda i: (i, 0, 0))
    o = _sds(w.shape, F32)
    return pl.pallas_call(body, out_shape=(o, o, o, o), grid=(ADAMW_ROW_STEPS,), in_specs=[blk] * 4, out_specs=(blk, blk, blk, blk),
                          compiler_params=_params(), name=name)(w, g, m, v)


def adamw_many(ws, gs, ms, vs, *, name):
    n = len(ws)

    def body(*refs):
        w_refs, g_refs, m_refs, v_refs, d_refs, nm_refs, nv_refs = (refs[k * n:(k + 1) * n] for k in range(7))
        for i in range(n):
            gg = g_refs[i][...]
            nm = ADAM_B1 * m_refs[i][...] + (1.0 - ADAM_B1) * gg
            nv = ADAM_B2 * v_refs[i][...] + (1.0 - ADAM_B2) * jnp.square(gg)
            m_hat = nm / (1.0 - ADAM_B1 ** ADAM_STEP)
            v_hat = nv / (1.0 - ADAM_B2 ** ADAM_STEP)
            d_refs[i][...] = -ADAM_LR * (m_hat / (jnp.sqrt(v_hat) + ADAM_EPS) + ADAM_WD * w_refs[i][...])
            nm_refs[i][...] = nm
            nv_refs[i][...] = nv

    outs = pl.pallas_call(body, out_shape=[_sds(w.shape, F32) for w in ws] * 3, compiler_params=_params(), name=name)(*ws, *gs, *ms, *vs)
    return outs[:n], outs[n:2 * n], outs[2 * n:]


def add_core_halves(g2, land, *, out_dtype, name):
    _, ns, r, cols = g2.shape
    tr, tc = _tile_2d(r, cols)

    def body(c_ref, a_ref, b_ref, o_ref):
        o_ref[...] = (a_ref[...] + b_ref[...]).astype(out_dtype)

    gs = pltpu.PrefetchScalarGridSpec(
        num_scalar_prefetch=1, grid=(ns, r // tr, cols // tc),
        in_specs=[pl.BlockSpec((None, None, tr, tc), lambda s, i, j, cr: (cr[0], s, i, j)),
                  pl.BlockSpec((None, tr, tc), lambda s, i, j, cr: (s, i, j))],
        out_specs=pl.BlockSpec((None, tr, tc), lambda s, i, j, cr: (s, i, j)))
    return pl.pallas_call(body, out_shape=_sds((ns, r, cols), out_dtype), grid_spec=gs, compiler_params=_params(), name=name)(
        _core_index(), g2, land)


def sum_over_chips(own, land, *, scatter, name):
    _, r, cols = own.shape
    tr, tc = _tile_2d(r, cols)

    def body(mine_ref, own_ref, l0, l1, l2, l3, o_ref):
        mine = mine_ref[0]
        mine_val = own_ref[...]
        acc = None
        for s, l_ref in enumerate((l0, l1, l2, l3)):
            val = jnp.where(mine == s, mine_val, l_ref[...]).astype(F32)
            acc = val if acc is None else acc + val
        o_ref[...] = acc

    def slot(s):
        return pl.BlockSpec((None, tr, tc), lambda i, j, mr: (jnp.where(mr[0] == s, (s + 1) % N_SHARD, s), i, j))

    own_spec = pl.BlockSpec((None, tr, tc), lambda i, j, mr: (mr[0] if scatter else 0, i, j))
    gs = pltpu.PrefetchScalarGridSpec(num_scalar_prefetch=1, grid=(r // tr, cols // tc), in_specs=[own_spec] + [slot(s) for s in range(N_SHARD)],
                                      out_specs=pl.BlockSpec((tr, tc), lambda i, j, mr: (i, j)))
    return pl.pallas_call(body, out_shape=_sds((r, cols), F32), grid_spec=gs, compiler_params=_params(), name=name)(
        _chip_index(), own, land, land, land, land)


_ANY = pl.BlockSpec(memory_space=pl.ANY)


def xy_exchange(src, *, scatter, name):
    rh = src.shape[1]

    def body(src_ref, land_ref, send_sems, recv_sems, loc_sem):
        x, y, c = lax.axis_index("x"), lax.axis_index("y"), lax.axis_index("c")
        mine = 2 * x + y
        peers = [(1 - x, y), (x, 1 - y), (1 - x, 1 - y)]

        def piece(shard):
            return src_ref.at[shard] if scatter else src_ref.at[c]

        def copy(k, px, py, dst_slot):
            return pltpu.make_async_remote_copy(src_ref=piece(2 * px + py), dst_ref=land_ref.at[dst_slot], send_sem=send_sems.at[k],
                                                recv_sem=recv_sems.at[k], device_id=(px, py, c), device_id_type=MESH)

        keep = pltpu.make_async_copy(piece(mine), land_ref.at[mine], loc_sem)
        keep.start()
        sends = [copy(k, px, py, mine) for k, (px, py) in enumerate(peers)]
        for cp in sends:
            cp.start()
        for cp in sends:
            cp.wait_send()
        for k, (px, py) in enumerate(peers):
            copy(k, px, py, 2 * px + py).wait_recv()
        keep.wait()

    return pl.pallas_call(body, out_shape=_sds((N_SHARD, rh, LANES), src.dtype), in_specs=[_ANY], out_specs=_ANY,
                          scratch_shapes=[pltpu.SemaphoreType.DMA((3,)), pltpu.SemaphoreType.DMA((3,)), pltpu.SemaphoreType.DMA(())],
                          name=name)(src)


def core_exchange(src, *, send_other_half, name):
    def body(src_ref, out_ref, send_sem, recv_sem, loc_sem):
        x, y, c = lax.axis_index("x"), lax.axis_index("y"), lax.axis_index("c")
        if send_other_half:
            cp = pltpu.make_async_remote_copy(src_ref=src_ref.at[1 - c], dst_ref=out_ref, send_sem=send_sem, recv_sem=recv_sem,
                                              device_id=(x, y, 1 - c), device_id_type=MESH)
            cp.start()
            cp.wait_send()
            cp.wait_recv()
        else:
            keep = pltpu.make_async_copy(src_ref, out_ref.at[c], loc_sem)
            keep.start()
            cp = pltpu.make_async_remote_copy(src_ref=src_ref, dst_ref=out_ref.at[c], send_sem=send_sem, recv_sem=recv_sem,
                                              device_id=(x, y, 1 - c), device_id_type=MESH)
            cp.start()
            cp.wait_send()
            pltpu.make_async_remote_copy(src_ref=src_ref, dst_ref=out_ref.at[1 - c], send_sem=send_sem, recv_sem=recv_sem,
                                         device_id=(x, y, 1 - c), device_id_type=MESH).wait_recv()
            keep.wait()

    out_shape = _sds(src.shape[1:], src.dtype) if send_other_half else _sds((2,) + src.shape, src.dtype)
    return pl.pallas_call(body, out_shape=out_shape, in_specs=[_ANY], out_specs=_ANY,
                          scratch_shapes=[pltpu.SemaphoreType.DMA(()), pltpu.SemaphoreType.DMA(()), pltpu.SemaphoreType.DMA(())],
                          name=name)(src)


def _comm_call(body, ins, out_shapes, sem_counts, name):
    return pl.pallas_call(body, out_shape=tuple(out_shapes), in_specs=[_ANY] * len(ins), out_specs=tuple([_ANY] * len(out_shapes)),
                          scratch_shapes=[pltpu.SemaphoreType.DMA((k,)) for k in sem_counts], name=name)(*ins)


def _sequencer_call(body, ins, out_shapes, sem_counts, name, collective_id):
    return pl.kernel(body, out_type=list(out_shapes), mesh=plsc.ScalarSubcoreMesh(axis_name="sequencer", num_cores=1), name=name,
                     scratch_types=[pltpu.SemaphoreType.DMA((k,)) for k in sem_counts],
                     compiler_params=pltpu.CompilerParams(collective_id=collective_id))(*ins)


def _handshake(peers):
    barrier = pltpu.get_barrier_semaphore()
    for peer in peers:
        pl.semaphore_signal(barrier, inc=1, device_id=peer, device_id_type=MESH)
    pl.semaphore_wait(barrier, len(peers))


def _xy_peers(x, y):
    return [(1 - x, y), (x, 1 - y), (1 - x, 1 - y)]


def gather_halves(halves, *, name, collective_id):
    n = len(halves)

    def body(*refs):
        ins, lands, sibs = refs[:n], refs[n:2 * n], refs[2 * n:3 * n]
        ici_send, ici_recv, d2d_send, d2d_recv = refs[3 * n:]
        x, y, c = lax.axis_index("x"), lax.axis_index("y"), lax.axis_index("c")
        mine = 2 * x + y
        peers = _xy_peers(x, y)
        _handshake([(px, py, c) for px, py in peers] + [(x, y, 1 - c)])

        def ici(i, k, slot):
            px, py = peers[k]
            return pltpu.make_async_remote_copy(src_ref=ins[i].at[c], dst_ref=lands[i].at[slot], send_sem=ici_send.at[3 * i + k],
                                                recv_sem=ici_recv.at[3 * i + k], device_id=(px, py, c), device_id_type=MESH)

        def pass_on(i, k):
            px, py = peers[k]
            slot = 2 * px + py
            return pltpu.make_async_remote_copy(src_ref=lands[i].at[slot], dst_ref=sibs[i].at[slot], send_sem=d2d_send.at[3 * i + k],
                                                recv_sem=d2d_recv.at[3 * i + k], device_id=(x, y, 1 - c), device_id_type=MESH)

        sends = [ici(i, k, mine) for i in range(n) for k in range(3)]
        for cp in sends:
            cp.start()
        passed = []
        for i in range(n):
            for k in range(3):
                px, py = peers[k]
                ici(i, k, 2 * px + py).wait_recv()
                cp = pass_on(i, k)
                cp.start()
                passed.append(cp)
        for cp in passed:
            cp.wait_recv()
        for cp in sends + passed:
            cp.wait_send()

    outs = [_sds((N_SHARD,) + h.shape[1:], h.dtype) for h in halves]
    res = _sequencer_call(body, halves, outs + outs, [3 * n] * 4, name, collective_id)
    return res[:n], res[n:]


def send_other_half(arrays, *, name, collective_id):
    n = len(arrays)

    def body(*refs):
        ins, lands = refs[:n], refs[n:2 * n]
        send_sems, recv_sems = refs[2 * n:]
        x, y, c = lax.axis_index("x"), lax.axis_index("y"), lax.axis_index("c")
        _handshake([(x, y, 1 - c)])
        copies = [pltpu.make_async_remote_copy(src_ref=ins[i].at[1 - c], dst_ref=lands[i], send_sem=send_sems.at[i],
                                               recv_sem=recv_sems.at[i], device_id=(x, y, 1 - c), device_id_type=MESH) for i in range(n)]
        for cp in copies:
            cp.start()
        for cp in copies:
            cp.wait_recv()
        for cp in copies:
            cp.wait_send()

    return _sequencer_call(body, arrays, [_sds(a.shape[1:], a.dtype) for a in arrays], [n, n], name, collective_id)


_HBM = pl.BlockSpec(memory_space=pltpu.HBM)
_SEM = pl.BlockSpec(memory_space=pltpu.SEMAPHORE)
_SPLIT_COPY = dict(has_side_effects=pltpu.SideEffectType.DATAFLOW_SIDE_EFFECTING)


def _chip_copy(ins, lands, send_sems, recv_sems, scatter, i, k, receive):
    x, y, c = lax.axis_index("x"), lax.axis_index("y"), lax.axis_index("c")
    px, py = _xy_peers(x, y)[k]
    theirs, mine = 2 * px + py, 2 * x + y
    src = ins[i].at[theirs] if scatter[i] else ins[i].at[0]
    return pltpu.make_async_remote_copy(src_ref=src, dst_ref=lands[i].at[theirs if receive else mine], send_sem=send_sems.at[3 * i + k],
                                        recv_sem=recv_sems.at[3 * i + k], device_id=(px, py, c), device_id_type=MESH)


def send_to_chips_start(arrays, scatter, *, name):
    n = len(arrays)

    def body(*refs):
        send_sems, recv_sems = refs[2 * n], refs[2 * n + 1]
        ins, lands = refs[2 * n + 2:3 * n + 2], refs[3 * n + 2:4 * n + 2]
        token = refs[4 * n + 2]
        for i in range(n):
            for k in range(3):
                _chip_copy(ins, lands, send_sems, recv_sems, scatter, i, k, receive=False).start()
        token[...] = jnp.zeros_like(token)

    land_shapes = [(N_SHARD,) + a.shape[1:] for a in arrays]
    operands = [pltpu.with_memory_space_constraint(a, pltpu.HBM) for a in arrays]
    operands += [pltpu.with_memory_space_constraint(lax.empty(s, a.dtype), pltpu.HBM) for s, a in zip(land_shapes, arrays)]
    out_shape = ([pltpu.SemaphoreType.DMA((3 * n,)), pltpu.SemaphoreType.DMA((3 * n,))] + [pltpu.HBM(a.shape, a.dtype) for a in arrays]
                 + [pltpu.HBM(s, a.dtype) for s, a in zip(land_shapes, arrays)] + [_sds((8, LANES), F32)])
    res = pl.pallas_call(body, name=name, out_shape=out_shape, in_specs=[_HBM] * (2 * n),
                         out_specs=[_SEM, _SEM] + [_HBM] * (2 * n) + [pl.BlockSpec(memory_space=pltpu.VMEM)],
                         input_output_aliases={i: 2 + i for i in range(2 * n)}, compiler_params=pltpu.CompilerParams(**_SPLIT_COPY))(*operands)
    return (res[0], res[1], res[2:2 + n], res[2 + n:2 + 2 * n], scatter), res[-1]


def send_to_chips_wait(state, after, *, name):
    send_sems, recv_sems, arrays, lands, scatter = state
    n = len(arrays)

    def body(*refs):
        ins, landing = refs[:n], refs[n:2 * n]
        send_sems, recv_sems = refs[2 * n], refs[2 * n + 1]
        for i in range(n):
            for k in range(3):
                _chip_copy(ins, landing, send_sems, recv_sems, scatter, i, k, receive=True).wait_recv()
        for i in range(n):
            for k in range(3):
                _chip_copy(ins, landing, send_sems, recv_sems, scatter, i, k, receive=False).wait_send()

    out_shape = [pltpu.HBM(a.shape, a.dtype) for a in list(arrays) + list(lands)]
    res = pl.pallas_call(body, name=name, out_shape=out_shape, in_specs=[_HBM] * (2 * n) + [_SEM, _SEM] + [_ANY] * len(after),
                         out_specs=[_HBM] * (2 * n), input_output_aliases={i: i for i in range(2 * n)},
                         compiler_params=pltpu.CompilerParams(**_SPLIT_COPY))(*arrays, *lands, send_sems, recv_sems, *after)
    return res[:n], res[n:]


def swap_with_other_core(arrays, *, name, collective_id):
    n = len(arrays)

    def body(*refs):
        ins, lands = refs[:n], refs[n:2 * n]
        send_sems, recv_sems = refs[2 * n:]
        x, y, c = lax.axis_index("x"), lax.axis_index("y"), lax.axis_index("c")
        _handshake([(x, y, 1 - c)])
        copies = [pltpu.make_async_remote_copy(src_ref=ins[i], dst_ref=lands[i], send_sem=send_sems.at[i], recv_sem=recv_sems.at[i],
                                               device_id=(x, y, 1 - c), device_id_type=MESH) for i in range(n)]
        for cp in copies:
            cp.start()
        for cp in copies:
            cp.wait_recv()
        for cp in copies:
            cp.wait_send()

    return _sequencer_call(body, arrays, [_sds(a.shape, a.dtype) for a in arrays], [n, n], name, collective_id)


def _pack_rows(n_elems, row_multiple):
    rows = -(-n_elems // LANES)
    return -(-rows // row_multiple) * row_multiple


def _pack(arrays, rows, dtype):
    flat = jnp.concatenate([a.reshape(-1).astype(dtype) for a in arrays])
    return jnp.pad(flat, (0, rows * LANES - flat.shape[0])).reshape(rows, LANES)


def _unpack(packed, shapes):
    flat = packed.reshape(-1)
    out, off = [], 0
    for s in shapes:
        n = int(np.prod(s))
        out.append(flat[off:off + n].reshape(s))
        off += n
    return out


def all_gather_shards(shards, axes, dtype, row_multiple, tag):
    shapes = [s.shape for s in shards]
    rows = _pack_rows(sum(int(np.prod(s)) for s in shapes), row_multiple)
    packed = _pack(shards, rows, dtype).reshape(2, rows // 2, LANES)
    land = xy_exchange(packed, scatter=False, name=f"gather_xy_{tag}")
    both = core_exchange(land, send_other_half=False, name=f"gather_c_{tag}")
    per_shard = jnp.swapaxes(both, 0, 1).reshape(N_SHARD, rows, LANES)
    pieces = [_unpack(per_shard[s], shapes) for s in range(N_SHARD)]
    return [jnp.concatenate([pieces[s][i] for s in range(N_SHARD)], axis=ax) for i, ax in enumerate(axes)]


def _ordered_before(first, then):
    if then is None:
        return first, None
    return lax.optimization_barrier((first, then))


def reduce_between_cores(arrays, scatter, *, tag, collective_id, before=None):
    arrays, before = _ordered_before(arrays, before)
    land = send_other_half(arrays, name=f"reduce_core_send_{tag}", collective_id=collective_id)
    return (arrays, land, scatter, tag, collective_id), before


def reduce_between_chips(state, before=None):
    arrays, land, scatter, tag, collective_id = state
    chip = [add_core_halves(a, l, out_dtype=BF16 if sc else F32, name=f"reduce_core_add_{tag}_{i}")
            for i, (a, l, sc) in enumerate(zip(arrays, land, scatter))]
    sending, token = send_to_chips_start(chip, scatter, name=f"reduce_chip_start_{tag}")
    token, before = _ordered_before(token, before)
    return (sending, token, scatter, tag, collective_id), before


def reduce_finish(state, after):
    sending, token, scatter, tag, collective_id = state
    chip, land = send_to_chips_wait(sending, tuple(after) + (token,), name=f"reduce_chip_wait_{tag}")
    own = [sum_over_chips(ch, l, scatter=sc, name=f"reduce_chip_add_{tag}_{i}") for i, (ch, l, sc) in enumerate(zip(chip, land, scatter))]
    sib = swap_with_other_core(own, name=f"reduce_core_swap_{tag}", collective_id=collective_id + 2)
    return own, sib


def _ffn_layer_fwd(h, norm_g, w_up, cw, cb, w_down, tag):
    u, hn = norm_matmul(h, norm_g, w_up, name=f"ffn_up_{tag}")
    act = ffn_act_fwd(u, cw, cb, name=f"ffn_act_{tag}")
    out = matmul(act, w_down, add=h, name=f"ffn_down_{tag}")
    return out, (h, hn, u, act)


def _travel_layout(array):
    return BIG_ARRAYS[array][3], BIG_ARRAYS[array][4]


def _ffn_layer_bwd(saved, dout, norm_g, w_up, cw, cb, w_down, tag):
    h, hn, u, act = saved
    dact = matmul(dout, w_down, tb=True, name=f"ffn_down_dx_{tag}")
    d_w_down = matmul(act, dout, ta=True, split=_travel_layout(f"ffn_w_down_{tag}"), name=f"ffn_down_dw_{tag}")
    du, dcw, dcb = ffn_act_bwd(u, cw, cb, dact, name=f"ffn_act_bwd_{tag}")
    dhn = matmul(du, w_up, tb=True, name=f"ffn_up_dx_{tag}")
    d_w_up = matmul(hn, du, ta=True, split=_travel_layout(f"ffn_w_up_{tag}"), name=f"ffn_up_dw_{tag}")
    dh, dg = norm_bwd(h, norm_g, dhn, dout, name=f"ffn_norm_bwd_{tag}")
    return dh, dg, d_w_up, dcw, dcb, d_w_down


def local_step(x, target, w, stage=lambda name, tensors, grads=None: tensors):
    g = {}
    tables = _ret_tables()
    x = stage("start", x)
    w_in_t = w["ret_gdn_w_in"]
    w_main = w_in_t[:MIX_MAIN]
    w_small = jnp.pad(w_in_t[MIX_MAIN:], ((0, LANES - 2 * N_HEADS), (0, 0)))
    a_log = jnp.pad(w["gdn_a_log"], ((0, 0), (0, LANES - N_HEADS)))
    dt_bias = jnp.pad(w["gdn_dt_bias"], ((0, 0), (0, LANES - N_HEADS)))

    p, hn0 = norm_matmul(x, w["norm_mix"][0:1], w_main, tb=True, name="mix0_in")
    hn0 = stage("normed", hn0)
    small = matmul(hn0, w_small, tb=True, name="mix0_in_small")
    y_ret, s_ret = ret_fwd(p, tables, name="ret_fwd")
    conv = gdn_conv_fwd(p, w["gdn_conv_w"], name="gdn_conv")
    y0, s_gdn = gdn_fwd(conv, p, small, a_log, dt_bias, w["gdn_out_gain"], y_ret, name="gdn_fwd")
    y0 = stage("mixed", y0)
    h1 = matmul(y0, w["ret_gdn_w_out"], add=x, name="mix0_out")
    h2, ffn0 = _ffn_layer_fwd(h1, w["norm_ffn"][0:1], w["ffn_w_up"][0], w["ffn_conv_w"][0], w["ffn_conv_b"][0:1], w["ffn_w_down"][0], "0")
    h2 = stage("layer0", h2)

    gx, hn1 = norm_matmul(h2, w["norm_mix"][1:2], w["lru_w_in"], name="mix1_in")
    lru_p = (w["lru_conv_w"], w["lru_conv_b"], w["lru_w_a"], w["lru_b_a"], w["lru_w_x"], w["lru_b_x"], w["lru_lambda"])
    y1 = lru_fwd(gx, *lru_p, name="lru_fwd")
    h3 = stage("mixed1", matmul(y1, w["lru_w_out"], add=h2, name="mix1_out"))
    h4, ffn1 = _ffn_layer_fwd(h3, w["norm_ffn"][1:2], w["ffn_w_up"][1], w["ffn_conv_w"][1], w["ffn_conv_b"][1:2], w["ffn_w_down"][1], "1")

    loss, dh4, g["norm_final"] = final_fwd_bwd(h4, w["norm_final"], target, name="final")

    dh3, dgf1, dwu1, dcw1, dcb1, dwd1 = _ffn_layer_bwd(ffn1, dh4, w["norm_ffn"][1:2], w["ffn_w_up"][1], w["ffn_conv_w"][1],
                                                     w["ffn_conv_b"][1:2], w["ffn_w_down"][1], "1")
    g["ffn_w_up_1"], g["ffn_w_down_1"] = dwu1, dwd1
    dh3 = stage("grads0_ready", dh3, g)
    dy1 = matmul(dh3, w["lru_w_out"], tb=True, name="mix1_out_dx")
    g["lru_w_out"] = matmul(y1, dh3, ta=True, split=_travel_layout("lru_w_out"), name="mix1_out_dw")
    dgx, g["lru_conv_w"], g["lru_conv_b"], g["lru_w_a"], g["lru_b_a"], g["lru_w_x"], g["lru_b_x"], g["lru_lambda"] = lru_bwd(
        gx, *lru_p, dy1, name="lru_bwd")
    dgx = stage("grads0_send", dgx, g)
    dhn1 = matmul(dgx, w["lru_w_in"], tb=True, name="mix1_in_dx")
    g["lru_w_in"] = matmul(hn1, dgx, ta=True, split=_travel_layout("lru_w_in"), name="mix1_in_dw")
    dh2, dgm1 = norm_bwd(h2, w["norm_mix"][1:2], dhn1, dh3, name="mix1_norm_bwd")
    dh2 = stage("grads1_ready", dh2, g)

    dh1, dgf0, dwu0, dcw0, dcb0, dwd0 = _ffn_layer_bwd(ffn0, dh2, w["norm_ffn"][0:1], w["ffn_w_up"][0], w["ffn_conv_w"][0],
                                                     w["ffn_conv_b"][0:1], w["ffn_w_down"][0], "0")
    g["ffn_w_up_0"], g["ffn_w_down_0"] = dwu0, dwd0
    dh1 = stage("grads2_ready", stage("grads1_send", dh1, g), g)
    dy0 = matmul(dh1, w["ret_gdn_w_out"], tb=True, name="mix0_out_dx")
    g["ret_gdn_w_out"] = matmul(y0, dh1, ta=True, split=_travel_layout("ret_gdn_w_out"), name="mix0_out_dw")
    dq_r, dk_r, dv_r, dg_r = ret_bwd(p, tables, s_ret, dy0, name="ret_bwd")
    dy0, dq_r = stage("grads2_send", (dy0, dq_r), g)
    dconv, dg_d, dsmall, dal, ddt, dgain = gdn_bwd(conv, p, small, a_log, dt_bias, w["gdn_out_gain"], s_gdn, dy0, name="gdn_bwd")
    dp_conv, g["gdn_conv_w"] = gdn_conv_bwd(p, w["gdn_conv_w"], dconv, name="gdn_conv_bwd")
    dp = jnp.concatenate([dq_r, dk_r, dv_r, dg_r, dp_conv, dg_d], axis=1)
    d_w_in = matmul(dp, hn0, ta=True, column_halves=(MIX_IN, 0, None, None), name="mix0_in_dw")
    d_w_in = matmul(dsmall, hn0, ta=True, column_halves=(MIX_IN, MIX_MAIN, 2 * N_HEADS, d_w_in), name="mix0_in_small_dw")
    g["ret_gdn_w_in"] = d_w_in.reshape(2, N_SHARD, MIX_IN // N_SHARD, D_MODEL // 2)
    dp = stage("grads3_ready", dp, g)
    dhn0 = stage("grads3_send", matmul(dp, w_main, name="mix0_in_dx"), g)
    dhn0 = matmul(dsmall, w_small, add=dhn0, name="mix0_in_small_dx")
    dx, dgm0 = norm_bwd(x, w["norm_mix"][0:1], dhn0, dh1, name="mix0_norm_bwd")

    g["gdn_a_log"] = dal[:, :N_HEADS]
    g["gdn_dt_bias"] = ddt[:, :N_HEADS]
    g["gdn_out_gain"] = dgain
    g["norm_mix"] = jnp.concatenate([dgm0, dgm1], axis=0)
    g["norm_ffn"] = jnp.concatenate([dgf0, dgf1], axis=0)
    g["ffn_conv_w"] = jnp.stack([dcw0, dcw1])
    g["ffn_conv_b"] = jnp.concatenate([dcb0, dcb1], axis=0)
    return loss, dx, g


WEIGHTS = ("norm_mix", "norm_ffn", "ret_gdn_w_in", "gdn_conv_w", "gdn_a_log", "gdn_dt_bias", "gdn_out_gain", "ret_gdn_w_out",
           "lru_w_in", "lru_conv_w", "lru_conv_b", "lru_w_a", "lru_b_a", "lru_w_x", "lru_b_x", "lru_lambda", "lru_w_out",
           "ffn_w_up", "ffn_conv_w", "ffn_conv_b", "ffn_w_down", "norm_final")
MATMUL_SHARDED = {"ret_gdn_w_in": 1, "ret_gdn_w_out": 0, "lru_w_in": 1, "lru_w_out": 0, "ffn_w_up": 2, "ffn_w_down": 1}
VECTOR_SHARDED = {"gdn_conv_w": 1, "lru_conv_w": 1, "lru_conv_b": 1, "lru_b_a": 1, "lru_b_x": 1, "lru_lambda": 1, "ffn_conv_w": 2}
SHARDED = {**MATMUL_SHARDED, **VECTOR_SHARDED}
REPLICATED = tuple(n for n in WEIGHTS if n not in SHARDED)
SQUEEZE = {"ret_gdn_w_in", "gdn_conv_w", "ret_gdn_w_out", "lru_w_in", "lru_conv_w", "lru_w_a", "lru_w_x", "lru_w_out"}
MIX_IN = MIX_MAIN + 2 * N_HEADS
BIG_ARRAYS = {
    "ret_gdn_w_in": ("ret_gdn_w_in", None, (MIX_IN, D_MODEL), (N_SHARD, MIX_IN // N_SHARD, 2, D_MODEL // 2), (2, 0, 1, 3)),
    "ret_gdn_w_out": ("ret_gdn_w_out", None, (2 * GROUP, D_MODEL), (N_SHARD, 2, GROUP // N_SHARD, D_MODEL), (1, 0, 2, 3)),
    "lru_w_in": ("lru_w_in", None, (D_MODEL, 2 * D_MODEL), (2, D_MODEL // 2, N_SHARD, 2 * D_MODEL // N_SHARD), (0, 2, 1, 3)),
    "lru_w_out": ("lru_w_out", None, (D_MODEL, D_MODEL), (N_SHARD, 2, D_MODEL // (2 * N_SHARD), D_MODEL), (1, 0, 2, 3)),
    "ffn_w_up_0": ("ffn_w_up", 0, (D_MODEL, 2 * D_FF), (2, D_MODEL // 2, N_SHARD, 2 * D_FF // N_SHARD), (0, 2, 1, 3)),
    "ffn_w_up_1": ("ffn_w_up", 1, (D_MODEL, 2 * D_FF), (2, D_MODEL // 2, N_SHARD, 2 * D_FF // N_SHARD), (0, 2, 1, 3)),
    "ffn_w_down_0": ("ffn_w_down", 0, (D_FF, D_MODEL), (N_SHARD, 2, D_FF // (2 * N_SHARD), D_MODEL), (1, 0, 2, 3)),
    "ffn_w_down_1": ("ffn_w_down", 1, (D_FF, D_MODEL), (N_SHARD, 2, D_FF // (2 * N_SHARD), D_MODEL), (1, 0, 2, 3)),
}
GATHER_GROUPS = (("ret_gdn_w_in",), ("ret_gdn_w_out", "ffn_w_up_0", "ffn_w_down_0"), ("lru_w_in", "lru_w_out"), ("ffn_w_up_1", "ffn_w_down_1"))
REDUCE_GROUPS = (("ffn_w_up_1", "ffn_w_down_1"), ("lru_w_in", "lru_w_out"), ("ffn_w_up_0", "ffn_w_down_0"), ("ret_gdn_w_out", "ret_gdn_w_in"),
                 ())
BLOCK_WEIGHTS = ("lru_w_a", "lru_w_x")
GATHER_COLLECTIVE_ID = 1
REDUCE_COLLECTIVE_ID = GATHER_COLLECTIVE_ID + len(GATHER_GROUPS)


TRANSPOSED = ("ret_gdn_w_in",)


def _shard_of(array, tensors):
    weight, layer = BIG_ARRAYS[array][:2]
    t = tensors[weight]
    if weight in TRANSPOSED:
        return jnp.swapaxes(t, 1, 2)[0]
    return _local_view(weight, t) if layer is None else t[layer]


def _core_halves(array, shard):
    _, _, _, split, perm = BIG_ARRAYS[array]
    kept = [k for k in range(4) if k != perm[1]]
    order = [kept.index(perm[0]), kept.index(perm[2]), kept.index(perm[3])]
    return shard.reshape([split[k] for k in kept]).transpose(order)


def _local_view(name, a):
    if name in SQUEEZE:
        return a[0]
    if a.ndim == 1:
        return a[None, :]
    return a


def kernel(x, norm_mix, norm_ffn, ret_gdn_w_in, gdn_conv_w, gdn_a_log, gdn_dt_bias, gdn_out_gain, ret_gdn_w_out, lru_w_in, lru_conv_w, lru_conv_b, lru_w_a, lru_b_a, lru_w_x, lru_b_x, lru_lambda, lru_w_out, ffn_w_up, ffn_conv_w, ffn_conv_b, ffn_w_down, norm_final, loss_target, m_norm_mix, m_norm_ffn, m_ret_gdn_w_in, m_gdn_conv_w, m_gdn_a_log, m_gdn_dt_bias, m_gdn_out_gain, m_ret_gdn_w_out, m_lru_w_in, m_lru_conv_w, m_lru_conv_b, m_lru_w_a, m_lru_b_a, m_lru_w_x, m_lru_b_x, m_lru_lambda, m_lru_w_out, m_ffn_w_up, m_ffn_conv_w, m_ffn_conv_b, m_ffn_w_down, m_norm_final, v_norm_mix, v_norm_ffn, v_ret_gdn_w_in, v_gdn_conv_w, v_gdn_a_log, v_gdn_dt_bias, v_gdn_out_gain, v_ret_gdn_w_out, v_lru_w_in, v_lru_conv_w, v_lru_conv_b, v_lru_w_a, v_lru_b_a, v_lru_w_x, v_lru_b_x, v_lru_lambda, v_lru_w_out, v_ffn_w_up, v_ffn_conv_w, v_ffn_conv_b, v_ffn_w_down, v_norm_final):
    given = dict(norm_mix=norm_mix, norm_ffn=norm_ffn, ret_gdn_w_in=ret_gdn_w_in, gdn_conv_w=gdn_conv_w, gdn_a_log=gdn_a_log, gdn_dt_bias=gdn_dt_bias, gdn_out_gain=gdn_out_gain, ret_gdn_w_out=ret_gdn_w_out, lru_w_in=lru_w_in, lru_conv_w=lru_conv_w, lru_conv_b=lru_conv_b, lru_w_a=lru_w_a, lru_b_a=lru_b_a, lru_w_x=lru_w_x, lru_b_x=lru_b_x, lru_lambda=lru_lambda, lru_w_out=lru_w_out, ffn_w_up=ffn_w_up, ffn_conv_w=ffn_conv_w, ffn_conv_b=ffn_conv_b, ffn_w_down=ffn_w_down, norm_final=norm_final)
    mom1 = dict(norm_mix=m_norm_mix, norm_ffn=m_norm_ffn, ret_gdn_w_in=m_ret_gdn_w_in, gdn_conv_w=m_gdn_conv_w, gdn_a_log=m_gdn_a_log, gdn_dt_bias=m_gdn_dt_bias, gdn_out_gain=m_gdn_out_gain, ret_gdn_w_out=m_ret_gdn_w_out, lru_w_in=m_lru_w_in, lru_conv_w=m_lru_conv_w, lru_conv_b=m_lru_conv_b, lru_w_a=m_lru_w_a, lru_b_a=m_lru_b_a, lru_w_x=m_lru_w_x, lru_b_x=m_lru_b_x, lru_lambda=m_lru_lambda, lru_w_out=m_lru_w_out, ffn_w_up=m_ffn_w_up, ffn_conv_w=m_ffn_conv_w, ffn_conv_b=m_ffn_conv_b, ffn_w_down=m_ffn_w_down, norm_final=m_norm_final)
    mom2 = dict(norm_mix=v_norm_mix, norm_ffn=v_norm_ffn, ret_gdn_w_in=v_ret_gdn_w_in, gdn_conv_w=v_gdn_conv_w, gdn_a_log=v_gdn_a_log, gdn_dt_bias=v_gdn_dt_bias, gdn_out_gain=v_gdn_out_gain, ret_gdn_w_out=v_ret_gdn_w_out, lru_w_in=v_lru_w_in, lru_conv_w=v_lru_conv_w, lru_conv_b=v_lru_conv_b, lru_w_a=v_lru_w_a, lru_b_a=v_lru_b_a, lru_w_x=v_lru_w_x, lru_b_x=v_lru_b_x, lru_lambda=v_lru_lambda, lru_w_out=v_lru_w_out, ffn_w_up=v_ffn_w_up, ffn_conv_w=v_ffn_conv_w, ffn_conv_b=v_ffn_conv_b, ffn_w_down=v_ffn_w_down, norm_final=v_norm_final)

    local = {n: _local_view(n, a) for n, a in given.items()}

    core = lax.axis_index("c")
    chip = 2 * lax.axis_index("x") + lax.axis_index("y")
    is_my_chip = lax.broadcasted_iota(jnp.int32, (N_SHARD, 1, 1), 0) == chip

    def by_core(mine, other):
        return jnp.where(core == 0, jnp.stack([mine, other]), jnp.stack([other, mine]))

    vec_names, rp_names = list(VECTOR_SHARDED), list(REPLICATED)
    full = dict(zip(vec_names, all_gather_shards([local[n] for n in vec_names], [SHARDED[n] for n in vec_names], F32, 32, "p")))
    for n in rp_names:
        full[n] = local[n]
    in_flight = {}

    bf16_halves = {}

    def cast_halves(gi):
        if gi not in bf16_halves:
            bf16_halves[gi] = [_core_halves(a, _shard_of(a, given).astype(BF16)) for a in GATHER_GROUPS[gi]]
        return bf16_halves[gi]

    def launch(gi, after=None):
        halves = cast_halves(gi)
        if after is not None:
            halves, after = lax.optimization_barrier((halves, after))
        in_flight[gi] = (halves,) + gather_halves(halves, name=f"gather_weights_{gi}", collective_id=GATHER_COLLECTIVE_ID + gi)
        return after

    def land(gi, after):
        halves, lands, sibs = in_flight[gi]
        (lands, sibs), after = lax.optimization_barrier(((lands, sibs), after))
        for a, mine, got, passed in zip(GATHER_GROUPS[gi], halves, lands, sibs):
            weight, layer, full_shape, split, perm = BIG_ARRAYS[a]
            half_mine = jnp.where(is_my_chip, jnp.where(core == 0, mine[0], mine[1])[None], got)
            half_other = jnp.where(is_my_chip, jnp.where(core == 0, mine[1], mine[0])[None], passed)
            value = by_core(half_mine, half_other).transpose(tuple(np.argsort(perm))).reshape(full_shape)
            if layer is None:
                full[weight] = value
            else:
                full.setdefault(weight, [None, None])[layer] = value
        return after

    reducing = {}

    def reduce_ready(gi, grads, then=None, extra=()):
        def travelling(a):
            split, perm = _travel_layout(a)
            return grads[a] if grads[a].ndim == 4 else grads[a].reshape(split).transpose(perm)

        arrays = [travelling(a) for a in REDUCE_GROUPS[gi]] + list(extra)
        scatter = [True] * len(REDUCE_GROUPS[gi]) + [False] * len(extra)
        reducing[gi], then = reduce_between_cores(arrays, scatter, tag=str(gi), collective_id=REDUCE_COLLECTIVE_ID + 3 * gi, before=then)
        return then

    def reduce_send(gi, then=None):
        reducing[gi], then = reduce_between_chips(reducing[gi], before=then)
        return then

    def stage(name, tensors, grads=None):
        if name == "start":
            launch(0)
            launch(1)
            fillers = (cast_halves(2), cast_halves(3), [full[n] for n in vec_names])
            (bf16_halves[2], bf16_halves[3], gathered_small), tensors = lax.optimization_barrier((fillers, tensors))
            full.update(zip(vec_names, gathered_small))
            return land(0, tensors)
        if name == "normed":
            return launch(3, launch(2, tensors))
        if name in ("mixed", "layer0", "mixed1"):
            return land({"mixed": 1, "layer0": 2, "mixed1": 3}[name], tensors)
        gi = int(name[len("grads")])
        return reduce_ready(gi, grads, tensors) if name.endswith("_ready") else reduce_send(gi, tensors)

    small_names = [n for n in rp_names if n not in BLOCK_WEIGHTS] + vec_names

    loss_part, dx, grads = local_step(x[0], loss_target[0], full, stage)
    small_shapes = [grads[n].shape for n in small_names] + [(1, 1)]
    small_rows = _pack_rows(sum(int(np.prod(s)) for s in small_shapes), 16)
    small = _pack([grads[n] for n in small_names] + [loss_part[:, :1]], small_rows, F32).reshape(2, 1, small_rows // 2, LANES)
    last = len(REDUCE_GROUPS) - 1
    halves_of_blocks = [grads[n].reshape(2, 1, LRU_BLOCKS * HEAD // 2, HEAD) for n in BLOCK_WEIGHTS]
    reduce_ready(last, grads, extra=[small] + halves_of_blocks)
    reduce_send(last)
    reduced, result = {}, {}

    def finish(gi, after):
        g_own, g_sib = reduce_finish(reducing[gi], after)
        reduced.update(zip(list(REDUCE_GROUPS[gi]) + ["small"] + list(BLOCK_WEIGHTS), zip(g_own, g_sib)))

    def update(n):
        if n in TRANSPOSED:
            n_rows, n_cols = given[n].shape[2], given[n].shape[1]

            def rows(t):
                return jnp.swapaxes(t, 1, 2).reshape(n_rows, 1, n_cols)

            def back(t):
                return jnp.swapaxes(t.reshape(1, n_rows, n_cols), 1, 2)

            g_rows = jnp.swapaxes(by_core(*reduced[n]), 0, 1).reshape(n_rows, 1, n_cols)
            result[n] = tuple(back(t) for t in adamw_rows(rows(given[n]), g_rows, rows(mom1[n]), rows(mom2[n]), name=f"adamw_{n}"))
            return
        done = None
        for a in (k for k, spec in BIG_ARRAYS.items() if spec[0] == n):
            r, cols = reduced[a][0].shape
            layer = BIG_ARRAYS[a][1] or 0
            w3, m3, v3 = (t if BIG_ARRAYS[a][1] is not None else t.reshape(1, 2 * r, cols) for t in (given[n], mom1[n], mom2[n]))
            done = adamw_halves(w3, m3, v3, *reduced[a], layer=layer, prev=done, name=f"adamw_{a}")
        result[n] = done

    updated = []
    for gi in range(last + 1):
        finish(gi, tuple(result[n][0] for n in updated) if updated else (dx, reducing[last][1]))
        for n in MATMUL_SHARDED:
            if n not in updated and all(a in reduced for a, spec in BIG_ARRAYS.items() if spec[0] == n):
                update(n)
                updated.append(n)

    for n in BLOCK_WEIGHTS:
        w3, m3, v3 = (t.reshape(1, LRU_BLOCKS * HEAD, HEAD) for t in (given[n], mom1[n], mom2[n]))
        result[n] = adamw_halves(w3, m3, v3, *reduced[n], name=f"adamw_{n}")

    *small_sums, loss_sum = _unpack(by_core(*reduced["small"]).reshape(small_rows, LANES), small_shapes)
    loss = loss_sum[0, 0]
    g_small = dict(zip(small_names, small_sums))
    for n in vec_names:
        size = local[n].shape[SHARDED[n]]
        g_small[n] = lax.dynamic_slice_in_dim(g_small[n], chip * size, size, axis=SHARDED[n])
    views = [[_local_view(n, src[n]) for n in small_names] for src in (given, mom1, mom2)]
    d_s, m_s, v_s = adamw_many(views[0], [g_small[n] for n in small_names], views[1], views[2], name="adamw_small")
    for n, d, nm, nv in zip(small_names, d_s, m_s, v_s):
        result[n] = (g_small[n], d, nm, nv)

    outs = [[result[n][k].reshape(given[n].shape) for n in WEIGHTS] for k in range(4)]
    return (loss, dx[None], *outs[0], *outs[1], *outs[2], *outs[3])
```

```python
import functools

import numpy as np
import jax
import jax.numpy as jnp
from jax import lax
from jax.experimental import pallas as pl
from jax.experimental.pallas import tpu as pltpu
from jax.experimental.pallas import tpu_sc as plsc

F32 = jnp.float32
BF16 = jnp.bfloat16
HI = lax.Precision.HIGHEST
MESH = pl.DeviceIdType.MESH

SEQ = 2048
D_MODEL = 1024
N_HEADS = 4
HEAD = 128
RET_CHUNK = 128
RET_CHUNKS_PER_STEP = 2
GDN_CHUNK = 64
GDN_CHUNKS_PER_STEP = 8
GROUP = N_HEADS * HEAD
MIX_MAIN = 8 * GROUP
D_FF = 2816
LRU_BLOCKS = 8
LRU_C = 8.0
ROPE_BASE = 10000.0
EPS = 1e-6
N_SHARD = 4
LANES = 128

ADAM_LR, ADAM_B1, ADAM_B2, ADAM_EPS, ADAM_WD, ADAM_STEP = 0.001, 0.9, 0.999, 1e-08, 0.01, 10

VMEM_LIMIT_BYTES = 56 * 1024 * 1024

_roll = pltpu.roll


def _params(**kw):
    return pltpu.CompilerParams(vmem_limit_bytes=VMEM_LIMIT_BYTES, **kw)


def _sds(shape, dtype):
    return jax.ShapeDtypeStruct(tuple(shape), dtype)


def _shift_raw(x, d):
    n = x.shape[0]
    t = lax.broadcasted_iota(jnp.int32, x.shape, 0)
    if d > 0:
        return jnp.where(t >= d, _roll(x, d, 0), 0.0)
    return jnp.where(t < n + d, _roll(x, n + d, 0), 0.0)


@functools.partial(jax.custom_vjp, nondiff_argnums=(1,))
def shift_rows(x, d):
    return _shift_raw(x, d)


def _shift_fwd(x, d):
    return _shift_raw(x, d), None


def _shift_bwd(d, _, g):
    return (_shift_raw(g, -d),)


shift_rows.defvjp(_shift_fwd, _shift_bwd)


@jax.custom_vjp
def swap_halves(x):
    return _roll(x, HEAD // 2, 1)


def _swap_fwd(x):
    return _roll(x, HEAD // 2, 1), None


def _swap_bwd(_, g):
    return (_roll(g, HEAD // 2, 1),)


swap_halves.defvjp(_swap_fwd, _swap_bwd)


SCAN_BLOCK_ROWS = 16


def _scan_block(a, u, reverse):
    n = a.shape[0]
    t = lax.broadcasted_iota(jnp.int32, a.shape, 0)
    d = 1
    while d < n:
        if reverse:
            m = t < n - d
            a_s, u_s = _roll(a, n - d, 0), _roll(u, n - d, 0)
        else:
            m = t >= d
            a_s, u_s = _roll(a, d, 0), _roll(u, d, 0)
        u = a * jnp.where(m, u_s, 0.0) + u
        a = a * jnp.where(m, a_s, 1.0)
        d *= 2
    return a, u


def _scan_raw(a, u, reverse):
    n = a.shape[0]
    blocks = range(n // SCAN_BLOCK_ROWS)
    out = [None] * len(blocks)
    entering = None
    for b in (reversed(blocks) if reverse else blocks):
        rows = slice(b * SCAN_BLOCK_ROWS, (b + 1) * SCAN_BLOCK_ROWS)
        a_run, h = _scan_block(a[rows], u[rows], reverse)
        if entering is not None:
            h = a_run * entering + h
        out[b] = h
        entering = h[:1] if reverse else h[SCAN_BLOCK_ROWS - 1:]
    return jnp.concatenate(out, axis=0)


@jax.custom_vjp
def lin_scan(a, u):
    return _scan_raw(a, u, False)


def _lin_scan_fwd(a, u):
    hs = _scan_raw(a, u, False)
    return hs, (a, hs)


def _lin_scan_bwd(res, g):
    a, hs = res
    lam = _scan_raw(_shift_raw(a, -1), g, True)
    return lam * _shift_raw(hs, 1), lam


lin_scan.defvjp(_lin_scan_fwd, _lin_scan_bwd)


def _bdot(a, b, dims=(((1,), (0,)), ((), ()))):
    return lax.dot_general(a.astype(BF16), b.astype(BF16), dims, preferred_element_type=F32)


def _each(f, *seqs):
    return tuple(f(*a) for a in zip(*seqs))


def _split_bf16(a):
    hi = a.astype(BF16)
    return hi, (a - hi.astype(F32)).astype(BF16)


def _dot3_raw(a_s, b_s):
    a_hl = _each(_split_bf16, a_s)
    b_hl = _each(_split_bf16, b_s)
    hh = _each(lambda a, b: _bdot(a[0], b[0]), a_hl, b_hl)
    hl = _each(lambda a, b: _bdot(a[0], b[1]), a_hl, b_hl)
    lh = _each(lambda a, b: _bdot(a[1], b[0]), a_hl, b_hl)
    return _each(lambda x, y, z: x + (y + z), hh, hl, lh)


@jax.custom_vjp
def dot3(a_s, b_s):
    return _dot3_raw(a_s, b_s)


def _dot3_fwd(a_s, b_s):
    return _dot3_raw(a_s, b_s), (a_s, b_s)


def _dot3_bwd(res, g_s):
    a_s, b_s = res
    return (_each(lambda g, b: _bdot(g, b, (((1,), (1,)), ((), ()))), g_s, b_s),
            _each(lambda a, g: _bdot(a, g, (((0,), (0,)), ((), ()))), a_s, g_s))


dot3.defvjp(_dot3_fwd, _dot3_bwd)


def _eye(n):
    i = lax.broadcasted_iota(jnp.int32, (n, n), 0)
    j = lax.broadcasted_iota(jnp.int32, (n, n), 1)
    return (i == j).astype(F32)


def _unit_lower_inverse_raw(lmats):
    n = lmats[0].shape[0]
    eye = _eye(n)
    ps = _each(lambda l: -l, lmats)
    invs = _each(lambda x: eye + x, ps)
    k = 1
    while 2 * k < n:
        ps = _each(lambda p: _bdot(p, p), ps)
        invs = _each(lambda inv, p: inv + _bdot(inv, p), invs, ps)
        k *= 2
    prods = _dot3_raw(lmats, invs)
    resids = _each(lambda inv, pr: eye - inv - pr, invs, prods)
    return _each(lambda inv, r: inv + _bdot(inv, r), invs, resids)


@jax.custom_vjp
def unit_lower_inverse(lmats):
    return _unit_lower_inverse_raw(lmats)


def _uli_fwd(lmats):
    invs = _unit_lower_inverse_raw(lmats)
    return invs, invs


def _uli_bwd(invs, g_s):
    ms = _each(lambda inv, g: _bdot(inv, g, (((0,), (0,)), ((), ()))), invs, g_s)
    return (_each(lambda m, inv: -_bdot(m, inv, (((1,), (1,)), ((), ()))), ms, invs),)


unit_lower_inverse.defvjp(_uli_fwd, _uli_bwd)


def _cumsum_raw(x, reverse):
    n = x.shape[0]
    t = lax.broadcasted_iota(jnp.int32, x.shape, 0)
    d = 1
    while d < n:
        if reverse:
            x = x + jnp.where(t < n - d, _roll(x, n - d, 0), 0.0)
        else:
            x = x + jnp.where(t >= d, _roll(x, d, 0), 0.0)
        d *= 2
    return x


@jax.custom_vjp
def cumsum_rows(x):
    return _cumsum_raw(x, False)


def _cumsum_fwd(x):
    return _cumsum_raw(x, False), None


def _cumsum_bwd(_, g):
    return (_cumsum_raw(g, True),)


cumsum_rows.defvjp(_cumsum_fwd, _cumsum_bwd)


_NT = (((1,), (1,)), ((), ()))
_TN = (((0,), (0,)), ((), ()))


def _softplus(x):
    return jnp.maximum(x, 0.0) + jnp.log1p(jnp.exp(-jnp.abs(x)))


def _expm1_nonpos(x):
    poly = x * (1.0 + x * (0.5 + x * (1.0 / 6 + x * (1.0 / 24 + x * (1.0 / 120 + x * (1.0 / 720))))))
    return jnp.where(x > -0.25, poly, jnp.exp(x) - 1.0)


def _rms(x):
    return x * lax.rsqrt(jnp.mean(x * x, axis=-1, keepdims=True) + EPS)


def _causal_conv(x, w, width):
    y = w[width - 1:width, :] * x
    for j in range(width - 1):
        y = y + w[j:j + 1, :] * shift_rows(x, width - 1 - j)
    return y


def _norm_fn(x, g):
    return _rms(x) * g


def _ffn_act_fn(ug, uv, wg, wv, bg, bv):
    return jax.nn.silu(_causal_conv(ug, wg, 3) + bg) * (_causal_conv(uv, wv, 3) + bv)


def _gdn_conv_fn(x, w):
    return jax.nn.silu(_causal_conv(x, w, 4))


def _lru_fn(gate, x, cw, cb, wa, ba, wx, bx, lam):
    xr = _causal_conv(x, cw, 4) + cb
    r = jax.nn.sigmoid(_bdot(xr, wa) + ba)
    i = jax.nn.sigmoid(_bdot(xr, wx) + bx)
    log_a = -LRU_C * r * _softplus(-lam)
    a = jnp.exp(log_a)
    u = jnp.sqrt(-_expm1_nonpos(2.0 * log_a)) * (i * xr)
    hs = lin_scan(a, u)
    return jax.nn.gelu(gate) * hs


def _ret_fn(qs, ks, vs, gates, states, cos2, sin2, dmasks, ktails, qdecs, cdecs):
    c = RET_CHUNK
    n_heads = len(qs)
    n_chunks = qs[0].shape[0] // c
    units = tuple((ci, h) for ci in range(n_chunks) for h in range(n_heads))

    def rows(x, ci):
        return x[ci * c:(ci + 1) * c]

    qrs = tuple(rows(qs[h], ci) * rows(cos2, ci) + swap_halves(rows(qs[h], ci)) * rows(sin2, ci) for ci, h in units)
    krs = tuple((rows(ks[h], ci) * rows(cos2, ci) + swap_halves(rows(ks[h], ci)) * rows(sin2, ci)) * (HEAD ** -0.5) for ci, h in units)
    vus = tuple(rows(vs[h], ci) for ci, h in units)
    scores = tuple(_bdot(q, k, _NT) * dmasks[h] for q, k, (_, h) in zip(qrs, krs, units))
    intra = _each(lambda sc, v: _bdot(sc, v), scores, vus)
    outs = []
    for ci in range(n_chunks):
        mine = slice(ci * n_heads, (ci + 1) * n_heads)
        inter = _each(lambda q, d, s: _bdot(q * d, s), qrs[mine], qdecs, states)
        outs.append(_each(lambda a, b: a + b, intra[mine], inter))
        states = _each(lambda s, cd, k, kt, v: s * cd + _bdot(k * kt, v, _TN), states, cdecs, krs[mine], ktails, vus[mine])
    ys = tuple(_rms(jnp.concatenate([outs[ci][h] for ci in range(n_chunks)], axis=0)) * jax.nn.silu(gates[h]) for h in range(n_heads))
    return ys, states


def _pick_lane(x, lane_idx):
    lane = lax.broadcasted_iota(jnp.int32, x.shape, 1)
    return jnp.sum(jnp.where(lane == lane_idx, x, 0.0), axis=1, keepdims=True)


def _l2norm(x):
    return x * lax.rsqrt(jnp.sum(x * x, axis=-1, keepdims=True) + EPS)


def _gdn_fn(qcs, kcs, vcs, gates, small, a_log, dt_bias, gain, states):
    c = GDN_CHUNK
    n_heads = len(qcs)
    n_chunks = qcs[0].shape[0] // c
    units = tuple((ci, h) for ci in range(n_chunks) for h in range(n_heads))

    def unit_rows(per_head):
        return tuple(per_head[h][ci * c:(ci + 1) * c] for ci, h in units)

    smalls = tuple(small[ci * c:(ci + 1) * c] for ci, _ in units)
    heads = tuple(h for _, h in units)
    intra = _gdn_intra(unit_rows(qcs), unit_rows(kcs), unit_rows(vcs), smalls, heads, a_log, dt_bias)
    outs = []
    for ci in range(n_chunks):
        mine = slice(ci * n_heads, (ci + 1) * n_heads)
        os_, states = _gdn_inter(*(part[mine] for part in intra), states)
        outs.append(os_)
    ys = tuple(_rms(jnp.concatenate([outs[ci][h] for ci in range(n_chunks)], axis=0)) * gain * jax.nn.silu(gates[h])
               for h in range(n_heads))
    return ys, states


def _gdn_inter(qs, ks, us, ws, attns, gcs, g_lasts, states):
    v_news = _each(lambda u, w, s: u - _bdot(w, s), us, ws, states)
    inter = _each(lambda q, gc, s: _bdot(q * jnp.exp(gc), s), qs, gcs, states)
    os_ = _each(lambda x, a, v: x + _bdot(a, v), inter, attns, v_news)
    new_states = _each(lambda s, gl, k, gc, v: s * jnp.exp(gl) + _bdot(k * jnp.exp(gl - gc), v, _TN), states, g_lasts, ks, gcs, v_news)
    return os_, new_states


def _gdn_intra(qcs, kcs, vcs, smalls, heads, a_log, dt_bias):
    c = GDN_CHUNK
    qs = _each(lambda x: _l2norm(x) * (HEAD ** -0.5), qcs)
    ks = _each(_l2norm, kcs)
    betas = _each(lambda sm, h: jax.nn.sigmoid(_pick_lane(sm, h)), smalls, heads)
    gs = _each(lambda sm, h: -jnp.exp(_pick_lane(a_log, h)) * _softplus(_pick_lane(sm, h + N_HEADS) + _pick_lane(dt_bias, h)),
               smalls, heads)
    i = lax.broadcasted_iota(jnp.int32, (c, c), 0)
    j = lax.broadcasted_iota(jnp.int32, (c, c), 1)
    tril = i >= j
    gcs = _each(lambda g: cumsum_rows(jnp.broadcast_to(g, (c, LANES)))[:, :1], gs)
    gc_rows = _each(lambda gc: jnp.broadcast_to(gc, (c, c)), gcs)
    decays = _each(lambda r: jnp.where(tril, jnp.exp(jnp.where(tril, r - r.T, 0.0)), 0.0), gc_rows)
    kbs = _each(lambda k, b: k * b, ks, betas)
    lmats = _each(lambda kb, k, d: jnp.where(i > j, _bdot(kb, k, _NT) * d, 0.0), kbs, ks, decays)
    attns = _each(lambda q, k, d: jnp.where(tril, _bdot(q, k, _NT) * d, 0.0), qs, ks, decays)
    invs = unit_lower_inverse(lmats)
    us = dot3(invs, _each(lambda v, b: v * b, vcs, betas))
    ws = dot3(invs, _each(lambda kb, gc: kb * jnp.exp(gc), kbs, gcs))
    g_lasts = _each(lambda g: jnp.sum(g, axis=0, keepdims=True), gs)
    return qs, ks, us, ws, attns, gcs, g_lasts


def _final_fn(h, g, target):
    y = _rms(h) * g
    return 0.5 * jnp.sum(jnp.mean(jnp.square(y - target), axis=-1, keepdims=True), axis=0, keepdims=True)


def _tile(n, candidates):
    for t in candidates:
        if n % t == 0:
            return t
    raise ValueError(f"no tile for {n}")


MATMUL_RESIDENT_LHS_BYTES = 8 * 1024 * 1024


def matmul(a, b, *, ta=False, tb=False, add=None, out_dtype=F32, tm=None, tn=None, split=None, layer=None, column_halves=None, name):
    m = a.shape[1] if ta else a.shape[0]
    k = a.shape[0] if ta else a.shape[1]
    n = b.shape[0] if tb else b.shape[1]
    assert k == (b.shape[1] if tb else b.shape[0])
    out_shape, out_block, out_index = (m, n), None, lambda i, j: (i, j)
    if split is not None:
        dims4, perm = split
        out_shape = tuple(dims4[p] for p in perm)
        r, cols = out_shape[2:]
        tm, tn = m, tn or _tile(cols, (1408, 512))
        cb = cols // tn
        if perm == (0, 2, 1, 3):
            out_block, out_index = (2, None, r, tn), lambda i, j: (0, j // cb, 0, j % cb)
        elif perm == (1, 0, 2, 3):
            out_block, out_index = (2, N_SHARD, r, tn), lambda i, j: (0, 0, 0, j)
        else:
            raise ValueError(perm)
    if tm is None and not ta and m * k * a.dtype.itemsize <= MATMUL_RESIDENT_LHS_BYTES:
        tm = m
    tm = tm or _tile(m, (1024, 512, 1408, 256, 128))
    tn = tn or _tile(n, (512, 1408, 256, 128))
    aliases, prev, keep_rows = {}, None, None
    if layer is not None:
        index, count, prev = layer
        out_shape, out_block, out_index = (count, m, n), (None, tm, tn), lambda i, j: (index, i, j)
    if column_halves is not None:
        total_rows, first_row, keep_rows, prev = column_halves
        tn = n // 2
        rows_out = keep_rows or tm
        out_shape, out_block = (2, total_rows, tn), (None, rows_out, tn)
        out_index = lambda i, j: (j, first_row // rows_out + i, 0)
    dims = (((0 if ta else 1,), (1 if tb else 0,)), ((), ()))

    def body(a_ref, b_ref, *rest):
        acc = lax.dot_general(a_ref[...].astype(BF16), b_ref[...].astype(BF16), dims, preferred_element_type=F32)
        if add is not None:
            acc = acc + rest[0][...]
        o_ref = rest[-1]
        acc = acc.astype(out_dtype)
        if split is not None and split[1] == (1, 0, 2, 3):
            rows = o_ref.shape[2]
            for s in range(N_SHARD):
                for h in range(2):
                    o_ref[h, s] = acc[(2 * s + h) * rows:(2 * s + h + 1) * rows]
        elif keep_rows is not None:
            o_ref[...] = acc[:keep_rows]
        else:
            o_ref[...] = acc.reshape(o_ref.shape)

    a_spec = pl.BlockSpec((k, tm), lambda i, j: (0, i)) if ta else pl.BlockSpec((tm, k), lambda i, j: (i, 0))
    b_spec = pl.BlockSpec((tn, k), lambda i, j: (j, 0)) if tb else pl.BlockSpec((k, tn), lambda i, j: (0, j))
    o_spec = pl.BlockSpec(out_block or (tm, tn), out_index)
    in_specs, args = [a_spec, b_spec], [a, b]
    if add is not None:
        in_specs.append(o_spec)
        args.append(add)
    if prev is not None:
        aliases = {len(args): 0}
        in_specs.append(pl.BlockSpec(memory_space=pl.ANY))
        args.append(prev)
    return pl.pallas_call(body, out_shape=_sds(out_shape, out_dtype), grid=(m // tm, n // tn), in_specs=in_specs,
                          out_specs=o_spec, input_output_aliases=aliases, compiler_params=_params(), name=name)(*args)


def norm_matmul(x, g, b, *, tb=False, name):
    t, k = x.shape
    n = b.shape[0] if tb else b.shape[1]
    tn = _tile(n, (512, 1408, 256, 128))
    dims = (((1,), (1 if tb else 0,)), ((), ()))

    def body(x_ref, g_ref, b_ref, o_ref, hn_ref):
        @pl.when(pl.program_id(0) == 0)
        def _():
            hn_ref[...] = _norm_fn(x_ref[...], g_ref[...]).astype(BF16)

        o_ref[...] = lax.dot_general(hn_ref[...], b_ref[...].astype(BF16), dims, preferred_element_type=F32)

    b_spec = pl.BlockSpec((tn, k), lambda j: (j, 0)) if tb else pl.BlockSpec((k, tn), lambda j: (0, j))
    whole = pl.BlockSpec((t, k), lambda j: (0, 0))
    return pl.pallas_call(body, out_shape=(_sds((t, n), F32), _sds((t, k), BF16)), grid=(n // tn,),
                          in_specs=[whole, pl.BlockSpec((1, k), lambda j: (0, 0)), b_spec],
                          out_specs=(pl.BlockSpec((t, tn), lambda j: (0, j)), whole), compiler_params=_params(), name=name)(x, g, b)


ROW_TILE = 256


def norm_bwd(x, g, dy, dres, *, name):
    t, d = x.shape

    def body(x_ref, g_ref, dy_ref, dres_ref, dx_ref, dg_ref):
        _, vjp = jax.vjp(_norm_fn, x_ref[...], g_ref[...])
        dx, dg = vjp(dy_ref[...])
        dx_ref[...] = dx + dres_ref[...]

        @pl.when(pl.program_id(0) == 0)
        def _():
            dg_ref[...] = jnp.zeros_like(dg_ref)

        dg_ref[...] += dg

    row = pl.BlockSpec((ROW_TILE, d), lambda i: (i, 0))
    vec = pl.BlockSpec((1, d), lambda i: (0, 0))
    return pl.pallas_call(body, out_shape=(_sds((t, d), F32), _sds((1, d), F32)), grid=(t // ROW_TILE,),
                          in_specs=[row, vec, row, row], out_specs=(row, vec), compiler_params=_params(), name=name)(x, g, dy, dres)


def final_fwd_bwd(h, g, target, *, name):
    t, d = h.shape

    def body(h_ref, g_ref, t_ref, loss_ref, dh_ref, dg_ref):
        tgt = t_ref[...]
        loss, vjp = jax.vjp(lambda hh, gg: _final_fn(hh, gg, tgt), h_ref[...], g_ref[...])
        dh, dg = vjp(jnp.ones((1, 1), F32))
        dh_ref[...] = dh

        @pl.when(pl.program_id(0) == 0)
        def _():
            dg_ref[...] = jnp.zeros_like(dg_ref)
            loss_ref[...] = jnp.zeros_like(loss_ref)

        dg_ref[...] += dg
        loss_ref[...] += jnp.broadcast_to(loss, loss_ref.shape)

    row = pl.BlockSpec((ROW_TILE, d), lambda i: (i, 0))
    vec = pl.BlockSpec((1, d), lambda i: (0, 0))
    return pl.pallas_call(body, out_shape=(_sds((1, LANES), F32), _sds((t, d), F32), _sds((1, d), F32)), grid=(t // ROW_TILE,),
                          in_specs=[row, vec, row], out_specs=(pl.BlockSpec((1, LANES), lambda i: (0, 0)), row, vec),
                          compiler_params=_params(), name=name)(h, g, target)


FFN_FWD_COLS = 256
FFN_BWD_COLS = 128


def ffn_act_fwd(u, cw, cb, *, name):
    t = u.shape[0]
    w = FFN_FWD_COLS
    nb = D_FF // w

    def body(ug_ref, uv_ref, wg_ref, wv_ref, bg_ref, bv_ref, o_ref):
        o_ref[...] = _ffn_act_fn(ug_ref[...], uv_ref[...], wg_ref[...], wv_ref[...], bg_ref[...], bv_ref[...]).astype(BF16)

    def col(rows, off):
        return pl.BlockSpec((rows, w), lambda j: (0, j + off))

    return pl.pallas_call(body, out_shape=_sds((t, D_FF), BF16), grid=(nb,),
                          in_specs=[col(t, 0), col(t, nb), col(3, 0), col(3, nb), col(1, 0), col(1, nb)],
                          out_specs=col(t, 0), compiler_params=_params(), name=name)(u, u, cw, cw, cb, cb)


def _put_column_blocks(step, n_steps, blocks, dst_ref, width, stage_ref, sems):
    def copies(at):
        slot = at % 2
        return [pltpu.make_async_copy(stage_ref.at[slot, p], dst_ref.at[:, pl.ds(pl.multiple_of((p * n_steps + at) * width, LANES), width)],
                                      sems.at[slot, p]) for p in range(len(blocks))]

    @pl.when(step >= 2)
    def _():
        for cp in copies(step - 2):
            cp.wait()

    for p, value in enumerate(blocks):
        stage_ref[step % 2, p] = value
    for cp in copies(step):
        cp.start()

    @pl.when(step == n_steps - 1)
    def _():
        for cp in copies(step - 1) + copies(step):
            cp.wait()


def ffn_act_bwd(u, cw, cb, da, *, name):
    t = u.shape[0]
    w = FFN_BWD_COLS
    nb = D_FF // w

    def body(ug_ref, uv_ref, wg_ref, wv_ref, bg_ref, bv_ref, da_ref, dug_ref, duv_ref, dwg_ref, dwv_ref, dbg_ref, dbv_ref):
        _, vjp = jax.vjp(_ffn_act_fn, ug_ref[...], uv_ref[...], wg_ref[...], wv_ref[...], bg_ref[...], bv_ref[...])
        dug, duv, dwg, dwv, dbg, dbv = vjp(da_ref[...])
        dug_ref[...] = dug.astype(BF16)
        duv_ref[...] = duv.astype(BF16)
        dwg_ref[...] = dwg
        dwv_ref[...] = dwv
        dbg_ref[...] = dbg
        dbv_ref[...] = dbv

    def col(rows, off):
        return pl.BlockSpec((rows, w), lambda j: (0, j + off))

    outs = pl.pallas_call(
        body, out_shape=(_sds((t, D_FF), BF16), _sds((t, D_FF), BF16), _sds((3, D_FF), F32), _sds((3, D_FF), F32),
                         _sds((1, D_FF), F32), _sds((1, D_FF), F32)),
        grid=(nb,), in_specs=[col(t, 0), col(t, nb), col(3, 0), col(3, nb), col(1, 0), col(1, nb), col(t, 0)],
        out_specs=(col(t, 0), col(t, 0), col(3, 0), col(3, 0), col(1, 0), col(1, 0)), compiler_params=_params(), name=name,
    )(u, u, cw, cw, cb, cb, da)
    dug, duv, dwg, dwv, dbg, dbv = outs
    return jnp.concatenate([dug, duv], axis=1), jnp.concatenate([dwg, dwv], axis=1), jnp.concatenate([dbg, dbv], axis=1)


GDN_CONV_COLS = 256
GDN_CONV_OFF = 4 * GROUP


def gdn_conv_fwd(p, cw, *, name):
    t = p.shape[0]
    w = GDN_CONV_COLS
    nb = 3 * GROUP // w
    off = GDN_CONV_OFF // w

    def body(x_ref, w_ref, o_ref):
        o_ref[...] = _gdn_conv_fn(x_ref[...], w_ref[...])

    return pl.pallas_call(body, out_shape=_sds((t, 3 * GROUP), F32), grid=(nb,),
                          in_specs=[pl.BlockSpec((t, w), lambda j: (0, j + off)), pl.BlockSpec((4, w), lambda j: (0, j))],
                          out_specs=pl.BlockSpec((t, w), lambda j: (0, j)), compiler_params=_params(), name=name)(p, cw)


def gdn_conv_bwd(p, cw, dc, *, name):
    t = p.shape[0]
    w = GDN_CONV_COLS
    nb = 3 * GROUP // w
    off = GDN_CONV_OFF // w

    def body(x_ref, w_ref, dc_ref, dx_ref, dw_ref):
        _, vjp = jax.vjp(_gdn_conv_fn, x_ref[...], w_ref[...])
        dx, dw = vjp(dc_ref[...])
        dx_ref[...] = dx.astype(BF16)
        dw_ref[...] = dw

    blk = pl.BlockSpec((t, w), lambda j: (0, j))
    wblk = pl.BlockSpec((4, w), lambda j: (0, j))
    return pl.pallas_call(body, out_shape=(_sds((t, 3 * GROUP), BF16), _sds((4, 3 * GROUP), F32)), grid=(nb,),
                          in_specs=[pl.BlockSpec((t, w), lambda j: (0, j + off)), wblk, blk], out_specs=(blk, wblk),
                          compiler_params=_params(), name=name)(p, cw, dc)


def _lru_specs(t):
    w = D_MODEL // LRU_BLOCKS
    gate = pl.BlockSpec((t, w), lambda j: (0, j))
    xin = pl.BlockSpec((t, w), lambda j: (0, j + LRU_BLOCKS))
    cw = pl.BlockSpec((4, w), lambda j: (0, j))
    vec = pl.BlockSpec((1, w), lambda j: (0, j))
    mat = pl.BlockSpec((None, w, w), lambda j: (j, 0, 0))
    return gate, xin, cw, vec, mat


def lru_fwd(gx, cw, cb, wa, ba, wx, bx, lam, *, name):
    t = gx.shape[0]
    gate, xin, cws, vec, mat = _lru_specs(t)

    def body(g_ref, x_ref, cw_ref, cb_ref, wa_ref, ba_ref, wx_ref, bx_ref, lam_ref, o_ref):
        o_ref[...] = _lru_fn(g_ref[...], x_ref[...], cw_ref[...], cb_ref[...], wa_ref[...], ba_ref[...], wx_ref[...],
                             bx_ref[...], lam_ref[...]).astype(BF16)

    return pl.pallas_call(body, out_shape=_sds((t, D_MODEL), BF16), grid=(LRU_BLOCKS,),
                          in_specs=[gate, xin, cws, vec, mat, vec, mat, vec, vec], out_specs=gate,
                          compiler_params=_params(), name=name)(gx, gx, cw, cb, wa, ba, wx, bx, lam)


def lru_bwd(gx, cw, cb, wa, ba, wx, bx, lam, dy, *, name):
    t = gx.shape[0]
    gate, xin, cws, vec, mat = _lru_specs(t)

    def body(g_ref, x_ref, cw_ref, cb_ref, wa_ref, ba_ref, wx_ref, bx_ref, lam_ref, dy_ref,
             dgx_ref, dcw_ref, dcb_ref, dwa_ref, dba_ref, dwx_ref, dbx_ref, dlam_ref, stage_ref, sems):
        _, vjp = jax.vjp(_lru_fn, g_ref[...], x_ref[...], cw_ref[...], cb_ref[...], wa_ref[...], ba_ref[...], wx_ref[...],
                         bx_ref[...], lam_ref[...])
        dg, dx, dcw, dcb, dwa, dba, dwx, dbx, dlam = vjp(dy_ref[...])
        _put_column_blocks(pl.program_id(0), LRU_BLOCKS, (dg.astype(BF16), dx.astype(BF16)), dgx_ref, D_MODEL // LRU_BLOCKS, stage_ref, sems)
        dcw_ref[...] = dcw
        dcb_ref[...] = dcb
        dwa_ref[...] = dwa
        dba_ref[...] = dba
        dwx_ref[...] = dwx
        dbx_ref[...] = dbx
        dlam_ref[...] = dlam

    d = D_MODEL
    w = d // LRU_BLOCKS
    out_shape = (_sds((t, 2 * d), BF16), _sds((4, d), F32), _sds((1, d), F32), _sds((LRU_BLOCKS, w, w), F32),
                 _sds((1, d), F32), _sds((LRU_BLOCKS, w, w), F32), _sds((1, d), F32), _sds((1, d), F32))
    return pl.pallas_call(body, out_shape=out_shape, grid=(LRU_BLOCKS,),
                          in_specs=[gate, xin, cws, vec, mat, vec, mat, vec, vec, gate],
                          out_specs=(pl.BlockSpec(memory_space=pl.ANY), cws, vec, mat, vec, mat, vec, vec),
                          scratch_shapes=[pltpu.VMEM((2, 2, t, w), BF16), pltpu.SemaphoreType.DMA((2, 2))],
                          compiler_params=_params(), name=name)(gx, gx, cw, cb, wa, ba, wx, bx, lam, dy)


def _ret_tables():
    half = HEAD // 2
    inv_freq = (np.float32(ROPE_BASE) ** (-np.arange(half, dtype=np.float32) / np.float32(half))).astype(np.float32)
    ang = (np.arange(SEQ, dtype=np.float32)[:, None] * inv_freq[None, :]).astype(np.float64)
    cos2 = np.concatenate([np.cos(ang), np.cos(ang)], axis=1).astype(np.float32)
    sin2 = np.concatenate([-np.sin(ang), np.sin(ang)], axis=1).astype(np.float32)
    c = RET_CHUNK
    log_gamma = np.log1p(-np.exp2(-5.0 - np.arange(N_HEADS, dtype=np.float64)))
    idx = np.arange(c, dtype=np.float64)
    rel = idx[:, None] - idx[None, :]
    dmask = np.where(rel >= 0, np.exp(log_gamma[:, None, None] * np.maximum(rel, 0.0)), 0.0)
    ones = np.ones((N_HEADS, c, HEAD))
    ktail = np.exp(log_gamma[:, None] * (c - 1 - idx))[:, :, None] * ones
    qdec = np.exp(log_gamma[:, None] * (idx + 1.0))[:, :, None] * ones
    cdec = np.exp(log_gamma * c)[:, None, None] * ones
    return tuple(jnp.asarray(a, F32) for a in (cos2, sin2, dmask, ktail, qdec, cdec))


def _ret_specs(rev):
    c = RET_CHUNK * RET_CHUNKS_PER_STEP
    nc = SEQ // c

    def n_of(n):
        return nc - 1 - n if rev else n

    def group(off):
        return pl.BlockSpec((c, GROUP), lambda n: (n_of(n), off))

    tab = pl.BlockSpec((c, HEAD), lambda n: (n_of(n), 0))
    const = pl.BlockSpec((N_HEADS, RET_CHUNK, HEAD), lambda n: (0, 0, 0))
    state = pl.BlockSpec((N_HEADS, None, HEAD, HEAD), lambda n: (0, n_of(n), 0, 0))
    return group, tab, const, state, nc


def _head(h):
    return slice(h * HEAD, (h + 1) * HEAD)


def ret_fwd(p, tables, *, name):
    group, tab, const, state, nc = _ret_specs(False)

    def body(q_ref, k_ref, v_ref, g_ref, cos_ref, sin_ref, dm_ref, kt_ref, qd_ref, cd_ref, y_ref, st_ref, s_scr):
        @pl.when(pl.program_id(0) == 0)
        def _():
            s_scr[...] = jnp.zeros_like(s_scr)

        heads = range(N_HEADS)
        states = tuple(s_scr[h] for h in heads)
        ys, new_states = _ret_fn(*(tuple(r[:, _head(h)] for h in heads) for r in (q_ref, k_ref, v_ref, g_ref)), states,
                                 cos_ref[...], sin_ref[...], *(tuple(r[h] for h in heads) for r in (dm_ref, kt_ref, qd_ref, cd_ref)))
        for h in heads:
            st_ref[h] = states[h]
            y_ref[:, _head(h)] = ys[h].astype(BF16)
            s_scr[h] = new_states[h]

    return pl.pallas_call(
        body, out_shape=(_sds((SEQ, 2 * GROUP), BF16), _sds((N_HEADS, nc, HEAD, HEAD), F32)), grid=(nc,),
        in_specs=[group(0), group(1), group(2), group(3), tab, tab, const, const, const, const],
        out_specs=(group(0), state), scratch_shapes=[pltpu.VMEM((N_HEADS, HEAD, HEAD), F32)], compiler_params=_params(), name=name,
    )(p, p, p, p, *tables)


def ret_bwd(p, tables, states, dy, *, name):
    group, tab, const, state, nc = _ret_specs(True)

    def body(q_ref, k_ref, v_ref, g_ref, cos_ref, sin_ref, dm_ref, kt_ref, qd_ref, cd_ref, st_ref, dy_ref,
             dq_ref, dk_ref, dv_ref, dg_ref, ds_scr):
        @pl.when(pl.program_id(0) == 0)
        def _():
            ds_scr[...] = jnp.zeros_like(ds_scr)

        heads = range(N_HEADS)
        consts = (cos_ref[...], sin_ref[...], *(tuple(r[h] for h in heads) for r in (dm_ref, kt_ref, qd_ref, cd_ref)))
        _, vjp = jax.vjp(lambda *a: _ret_fn(*a, *consts), *(tuple(r[:, _head(h)] for h in heads) for r in (q_ref, k_ref, v_ref, g_ref)),
                         tuple(st_ref[h] for h in heads))
        dqs, dks, dvs, dgs, dss = vjp((tuple(dy_ref[:, _head(h)] for h in heads), tuple(ds_scr[h] for h in heads)))
        for h in heads:
            dq_ref[:, _head(h)] = dqs[h].astype(BF16)
            dk_ref[:, _head(h)] = dks[h].astype(BF16)
            dv_ref[:, _head(h)] = dvs[h].astype(BF16)
            dg_ref[:, _head(h)] = dgs[h].astype(BF16)
            ds_scr[h] = dss[h]

    out = _sds((SEQ, GROUP), BF16)
    return pl.pallas_call(
        body, out_shape=(out, out, out, out), grid=(nc,),
        in_specs=[group(0), group(1), group(2), group(3), tab, tab, const, const, const, const, state, group(0)],
        out_specs=(group(0), group(0), group(0), group(0)), scratch_shapes=[pltpu.VMEM((N_HEADS, HEAD, HEAD), F32)],
        compiler_params=_params(), name=name,
    )(p, p, p, p, *tables, states, dy)


def _gdn_specs(rev):
    c = GDN_CHUNK * GDN_CHUNKS_PER_STEP
    nc = SEQ // c

    def n_of(n):
        return nc - 1 - n if rev else n

    def group(off):
        return pl.BlockSpec((c, GROUP), lambda n: (n_of(n), off))

    small = pl.BlockSpec((c, LANES), lambda n: (n_of(n), 0))
    vec = pl.BlockSpec((1, LANES), lambda n: (0, 0))
    state = pl.BlockSpec((N_HEADS, None, HEAD, HEAD), lambda n: (0, n_of(n), 0, 0))
    qkv = pl.BlockSpec((c, 3 * GROUP), lambda n: (n_of(n), 0))
    return group, small, vec, state, qkv, nc


GDN_GATE_GROUP = 7


def gdn_fwd(conv, p, small, a_log, dt_bias, gain, y_started, *, name):
    group, sm, vec, state, _, nc = _gdn_specs(False)

    def body(q_ref, k_ref, v_ref, g_ref, sm_ref, al_ref, dt_ref, gn_ref, _, y_ref, st_ref, s_scr):
        @pl.when(pl.program_id(0) == 0)
        def _():
            s_scr[...] = jnp.zeros_like(s_scr)

        states = tuple(s_scr[h] for h in range(N_HEADS))
        ys, new_states = _gdn_fn(*(tuple(r[:, _head(h)] for h in range(N_HEADS)) for r in (q_ref, k_ref, v_ref, g_ref)),
                                 sm_ref[...], al_ref[...], dt_ref[...], gn_ref[...], states)
        for h in range(N_HEADS):
            st_ref[h] = states[h]
            y_ref[:, _head(h)] = ys[h].astype(BF16)
            s_scr[h] = new_states[h]

    return pl.pallas_call(
        body, out_shape=(_sds((SEQ, 2 * GROUP), BF16), _sds((N_HEADS, nc, HEAD, HEAD), F32)), grid=(nc,),
        in_specs=[group(0), group(1), group(2), group(GDN_GATE_GROUP), sm, vec, vec, vec, pl.BlockSpec(memory_space=pl.ANY)],
        out_specs=(group(1), state), input_output_aliases={8: 0},
        scratch_shapes=[pltpu.VMEM((N_HEADS, HEAD, HEAD), F32)], compiler_params=_params(), name=name,
    )(conv, conv, conv, p, small, a_log, dt_bias, gain, y_started)


def gdn_bwd(conv, p, small, a_log, dt_bias, gain, states, dy, *, name):
    group, sm, vec, state, qkv, nc = _gdn_specs(True)

    def body(q_ref, k_ref, v_ref, g_ref, sm_ref, al_ref, dt_ref, gn_ref, st_ref, dy_ref,
             dqkv_ref, dg_ref, dsm_ref, dal_ref, ddt_ref, dgn_ref, ds_scr):
        @pl.when(pl.program_id(0) == 0)
        def _():
            ds_scr[...] = jnp.zeros_like(ds_scr)
            dal_ref[...] = jnp.zeros_like(dal_ref)
            ddt_ref[...] = jnp.zeros_like(ddt_ref)
            dgn_ref[...] = jnp.zeros_like(dgn_ref)

        per_head = tuple(tuple(r[:, _head(h)] for h in range(N_HEADS)) for r in (q_ref, k_ref, v_ref, g_ref))
        _, vjp = jax.vjp(_gdn_fn, *per_head, sm_ref[...], al_ref[...], dt_ref[...], gn_ref[...],
                         tuple(st_ref[h] for h in range(N_HEADS)))
        cts = (tuple(dy_ref[:, _head(h)] for h in range(N_HEADS)), tuple(ds_scr[h] for h in range(N_HEADS)))
        dqs, dks, dvs, dgs, dsm, dal, ddt, dgn, dss = vjp(cts)
        for h in range(N_HEADS):
            for part, blocks in enumerate((dqs, dks, dvs)):
                dqkv_ref[:, part * GROUP + h * HEAD:part * GROUP + (h + 1) * HEAD] = blocks[h]
            dg_ref[:, _head(h)] = dgs[h].astype(BF16)
            ds_scr[h] = dss[h]
        dsm_ref[...] = dsm
        dal_ref[...] += dal
        ddt_ref[...] += ddt
        dgn_ref[...] += dgn

    pv = _sds((1, LANES), F32)
    return pl.pallas_call(
        body, out_shape=(_sds((SEQ, 3 * GROUP), F32), _sds((SEQ, GROUP), BF16), _sds((SEQ, LANES), F32), pv, pv, pv), grid=(nc,),
        in_specs=[group(0), group(1), group(2), group(GDN_GATE_GROUP), sm, vec, vec, vec, state, group(1)],
        out_specs=(qkv, group(0), sm, vec, vec, vec), scratch_shapes=[pltpu.VMEM((N_HEADS, HEAD, HEAD), F32)],
        compiler_params=_params(), name=name,
    )(conv, conv, conv, p, small, a_log, dt_bias, gain, states, dy)


ELEMENTWISE_BLOCK_BYTES = 2 * 1024 * 1024


def _row_tile(r, c):
    best = None
    for tr in range(8, r + 1, 8):
        if r % tr == 0 and tr * c * 4 <= ELEMENTWISE_BLOCK_BYTES:
            best = tr
    if best is None:
        raise ValueError(f"no row tile for ({r}, {c})")
    return best


def _tile_2d(r, c):
    if any(r % tr == 0 for tr in range(8, r + 1, 8)):
        return _row_tile(r, c), c
    tc = max(t for t in range(LANES, c + 1, LANES) if c % t == 0 and r * t * 4 <= ELEMENTWISE_BLOCK_BYTES)
    return r, tc


def _core_index():
    return lax.axis_index("c").astype(jnp.int32).reshape(1)


def _chip_index():
    return (2 * lax.axis_index("x") + lax.axis_index("y")).astype(jnp.int32).reshape(1)


def adamw_halves(w, m, v, g_own, g_sib, *, layer=0, prev=None, name):
    n_layers, rows, c = w.shape
    r = rows // 2
    tr = _row_tile(r, c)
    nb = r // tr

    def body(c_ref, w_ref, m_ref, v_ref, own_ref, sib_ref, *rest):
        g_ref, d_ref, nm_ref, nv_ref = rest[-4:]
        gg = jnp.where(pl.program_id(0) == c_ref[0], own_ref[...], sib_ref[...])
        nm = ADAM_B1 * m_ref[...] + (1.0 - ADAM_B1) * gg
        nv = ADAM_B2 * v_ref[...] + (1.0 - ADAM_B2) * jnp.square(gg)
        m_hat = nm / (1.0 - ADAM_B1 ** ADAM_STEP)
        v_hat = nv / (1.0 - ADAM_B2 ** ADAM_STEP)
        g_ref[...] = gg
        d_ref[...] = -ADAM_LR * (m_hat / (jnp.sqrt(v_hat) + ADAM_EPS) + ADAM_WD * w_ref[...])
        nm_ref[...] = nm
        nv_ref[...] = nv

    full = pl.BlockSpec((None, tr, c), lambda h, i, cr: (layer, h * nb + i, 0))
    half = pl.BlockSpec((tr, c), lambda h, i, cr: (i, 0))
    o = _sds((n_layers, rows, c), F32)
    prev = list(prev or ())
    gs = pltpu.PrefetchScalarGridSpec(num_scalar_prefetch=1, grid=(2, nb), in_specs=[full, full, full, half, half] + [_ANY] * len(prev),
                                      out_specs=(full, full, full, full))
    n_fixed = 6
    return pl.pallas_call(body, out_shape=(o, o, o, o), grid_spec=gs, compiler_params=_params(), name=name,
                          input_output_aliases={n_fixed + k: k for k in range(len(prev))})(
        _core_index(), w, m, v, g_own, g_sib, *prev)


ADAMW_ROW_STEPS = 6


def adamw_rows(w, g, m, v, *, name):
    rows, _, cols = w.shape
    tr = rows // ADAMW_ROW_STEPS

    def body(w_ref, g_ref, m_ref, v_ref, g_out_ref, d_ref, nm_ref, nv_ref):
        gg = g_ref[...]
        nm = ADAM_B1 * m_ref[...] + (1.0 - ADAM_B1) * gg
        nv = ADAM_B2 * v_ref[...] + (1.0 - ADAM_B2) * jnp.square(gg)
        m_hat = nm / (1.0 - ADAM_B1 ** ADAM_STEP)
        v_hat = nv / (1.0 - ADAM_B2 ** ADAM_STEP)
        g_out_ref[...] = gg
        d_ref[...] = -ADAM_LR * (m_hat / (jnp.sqrt(v_hat) + ADAM_EPS) + ADAM_WD * w_ref[...])
        nm_ref[...] = nm
        nv_ref[...] = nv

    blk = pl.BlockSpec((tr, 1, cols), lambda i: (i, 0, 0))
    o = _sds(w.shape, F32)
    return pl.pallas_call(body, out_shape=(o, o, o, o), grid=(ADAMW_ROW_STEPS,), in_specs=[blk] * 4, out_specs=(blk, blk, blk, blk),
                          compiler_params=_params(), name=name)(w, g, m, v)


def adamw_many(ws, gs, ms, vs, *, name):
    n = len(ws)

    def body(*refs):
        w_refs, g_refs, m_refs, v_refs, d_refs, nm_refs, nv_refs = (refs[k * n:(k + 1) * n] for k in range(7))
        for i in range(n):
            gg = g_refs[i][...]
            nm = ADAM_B1 * m_refs[i][...] + (1.0 - ADAM_B1) * gg
            nv = ADAM_B2 * v_refs[i][...] + (1.0 - ADAM_B2) * jnp.square(gg)
            m_hat = nm / (1.0 - ADAM_B1 ** ADAM_STEP)
            v_hat = nv / (1.0 - ADAM_B2 ** ADAM_STEP)
            d_refs[i][...] = -ADAM_LR * (m_hat / (jnp.sqrt(v_hat) + ADAM_EPS) + ADAM_WD * w_refs[i][...])
            nm_refs[i][...] = nm
            nv_refs[i][...] = nv

    outs = pl.pallas_call(body, out_shape=[_sds(w.shape, F32) for w in ws] * 3, compiler_params=_params(), name=name)(*ws, *gs, *ms, *vs)
    return outs[:n], outs[n:2 * n], outs[2 * n:]


def add_core_halves(g2, land, *, out_dtype, name):
    _, ns, r, cols = g2.shape
    tr, tc = _tile_2d(r, cols)

    def body(c_ref, a_ref, b_ref, o_ref):
        o_ref[...] = (a_ref[...] + b_ref[...]).astype(out_dtype)

    gs = pltpu.PrefetchScalarGridSpec(
        num_scalar_prefetch=1, grid=(ns, r // tr, cols // tc),
        in_specs=[pl.BlockSpec((None, None, tr, tc), lambda s, i, j, cr: (cr[0], s, i, j)),
                  pl.BlockSpec((None, tr, tc), lambda s, i, j, cr: (s, i, j))],
        out_specs=pl.BlockSpec((None, tr, tc), lambda s, i, j, cr: (s, i, j)))
    return pl.pallas_call(body, out_shape=_sds((ns, r, cols), out_dtype), grid_spec=gs, compiler_params=_params(), name=name)(
        _core_index(), g2, land)


def sum_over_chips(own, land, *, scatter, name):
    _, r, cols = own.shape
    tr, tc = _tile_2d(r, cols)

    def body(mine_ref, own_ref, l0, l1, l2, l3, o_ref):
        mine = mine_ref[0]
        mine_val = own_ref[...]
        acc = None
        for s, l_ref in enumerate((l0, l1, l2, l3)):
            val = jnp.where(mine == s, mine_val, l_ref[...]).astype(F32)
            acc = val if acc is None else acc + val
        o_ref[...] = acc

    def slot(s):
        return pl.BlockSpec((None, tr, tc), lambda i, j, mr: (jnp.where(mr[0] == s, (s + 1) % N_SHARD, s), i, j))

    own_spec = pl.BlockSpec((None, tr, tc), lambda i, j, mr: (mr[0] if scatter else 0, i, j))
    gs = pltpu.PrefetchScalarGridSpec(num_scalar_prefetch=1, grid=(r // tr, cols // tc), in_specs=[own_spec] + [slot(s) for s in range(N_SHARD)],
                                      out_specs=pl.BlockSpec((tr, tc), lambda i, j, mr: (i, j)))
    return pl.pallas_call(body, out_shape=_sds((r, cols), F32), grid_spec=gs, compiler_params=_params(), name=name)(
        _chip_index(), own, land, land, land, land)


_ANY = pl.BlockSpec(memory_space=pl.ANY)


def xy_exchange(src, *, scatter, name):
    rh = src.shape[1]

    def body(src_ref, land_ref, send_sems, recv_sems, loc_sem):
        x, y, c = lax.axis_index("x"), lax.axis_index("y"), lax.axis_index("c")
        mine = 2 * x + y
        peers = [(1 - x, y), (x, 1 - y), (1 - x, 1 - y)]

        def piece(shard):
            return src_ref.at[shard] if scatter else src_ref.at[c]

        def copy(k, px, py, dst_slot):
            return pltpu.make_async_remote_copy(src_ref=piece(2 * px + py), dst_ref=land_ref.at[dst_slot], send_sem=send_sems.at[k],
                                                recv_sem=recv_sems.at[k], device_id=(px, py, c), device_id_type=MESH)

        keep = pltpu.make_async_copy(piece(mine), land_ref.at[mine], loc_sem)
        keep.start()
        sends = [copy(k, px, py, mine) for k, (px, py) in enumerate(peers)]
        for cp in sends:
            cp.start()
        for cp in sends:
            cp.wait_send()
        for k, (px, py) in enumerate(peers):
            copy(k, px, py, 2 * px + py).wait_recv()
        keep.wait()

    return pl.pallas_call(body, out_shape=_sds((N_SHARD, rh, LANES), src.dtype), in_specs=[_ANY], out_specs=_ANY,
                          scratch_shapes=[pltpu.SemaphoreType.DMA((3,)), pltpu.SemaphoreType.DMA((3,)), pltpu.SemaphoreType.DMA(())],
                          name=name)(src)


def core_exchange(src, *, send_other_half, name):
    def body(src_ref, out_ref, send_sem, recv_sem, loc_sem):
        x, y, c = lax.axis_index("x"), lax.axis_index("y"), lax.axis_index("c")
        if send_other_half:
            cp = pltpu.make_async_remote_copy(src_ref=src_ref.at[1 - c], dst_ref=out_ref, send_sem=send_sem, recv_sem=recv_sem,
                                              device_id=(x, y, 1 - c), device_id_type=MESH)
            cp.start()
            cp.wait_send()
            cp.wait_recv()
        else:
            keep = pltpu.make_async_copy(src_ref, out_ref.at[c], loc_sem)
            keep.start()
            cp = pltpu.make_async_remote_copy(src_ref=src_ref, dst_ref=out_ref.at[c], send_sem=send_sem, recv_sem=recv_sem,
                                              device_id=(x, y, 1 - c), device_id_type=MESH)
            cp.start()
            cp.wait_send()
            pltpu.make_async_remote_copy(src_ref=src_ref, dst_ref=out_ref.at[1 - c], send_sem=send_sem, recv_sem=recv_sem,
                                         device_id=(x, y, 1 - c), device_id_type=MESH).wait_recv()
            keep.wait()

    out_shape = _sds(src.shape[1:], src.dtype) if send_other_half else _sds((2,) + src.shape, src.dtype)
    return pl.pallas_call(body, out_shape=out_shape, in_specs=[_ANY], out_specs=_ANY,
                          scratch_shapes=[pltpu.SemaphoreType.DMA(()), pltpu.SemaphoreType.DMA(()), pltpu.SemaphoreType.DMA(())],
                          name=name)(src)


def _comm_call(body, ins, out_shapes, sem_counts, name):
    return pl.pallas_call(body, out_shape=tuple(out_shapes), in_specs=[_ANY] * len(ins), out_specs=tuple([_ANY] * len(out_shapes)),
                          scratch_shapes=[pltpu.SemaphoreType.DMA((k,)) for k in sem_counts], name=name)(*ins)


def _sequencer_call(body, ins, out_shapes, sem_counts, name, collective_id):
    return pl.kernel(body, out_type=list(out_shapes), mesh=plsc.ScalarSubcoreMesh(axis_name="sequencer", num_cores=1), name=name,
                     scratch_types=[pltpu.SemaphoreType.DMA((k,)) for k in sem_counts],
                     compiler_params=pltpu.CompilerParams(collective_id=collective_id))(*ins)


def _handshake(peers):
    barrier = pltpu.get_barrier_semaphore()
    for peer in peers:
        pl.semaphore_signal(barrier, inc=1, device_id=peer, device_id_type=MESH)
    pl.semaphore_wait(barrier, len(peers))


def _xy_peers(x, y):
    return [(1 - x, y), (x, 1 - y), (1 - x, 1 - y)]


def gather_halves(halves, *, name, collective_id):
    n = len(halves)

    def body(*refs):
        ins, lands, sibs = refs[:n], refs[n:2 * n], refs[2 * n:3 * n]
        ici_send, ici_recv, d2d_send, d2d_recv = refs[3 * n:]
        x, y, c = lax.axis_index("x"), lax.axis_index("y"), lax.axis_index("c")
        mine = 2 * x + y
        peers = _xy_peers(x, y)
        _handshake([(px, py, c) for px, py in peers] + [(x, y, 1 - c)])

        def ici(i, k, slot):
            px, py = peers[k]
            return pltpu.make_async_remote_copy(src_ref=ins[i].at[c], dst_ref=lands[i].at[slot], send_sem=ici_send.at[3 * i + k],
                                                recv_sem=ici_recv.at[3 * i + k], device_id=(px, py, c), device_id_type=MESH)

        def pass_on(i, k):
            px, py = peers[k]
            slot = 2 * px + py
            return pltpu.make_async_remote_copy(src_ref=lands[i].at[slot], dst_ref=sibs[i].at[slot], send_sem=d2d_send.at[3 * i + k],
                                                recv_sem=d2d_recv.at[3 * i + k], device_id=(x, y, 1 - c), device_id_type=MESH)

        sends = [ici(i, k, mine) for i in range(n) for k in range(3)]
        for cp in sends:
            cp.start()
        passed = []
        for i in range(n):
            for k in range(3):
                px, py = peers[k]
                ici(i, k, 2 * px + py).wait_recv()
                cp = pass_on(i, k)
                cp.start()
                passed.append(cp)
        for cp in passed:
            cp.wait_recv()
        for cp in sends + passed:
            cp.wait_send()

    outs = [_sds((N_SHARD,) + h.shape[1:], h.dtype) for h in halves]
    res = _sequencer_call(body, halves, outs + outs, [3 * n] * 4, name, collective_id)
    return res[:n], res[n:]


def send_other_half(arrays, *, name, collective_id):
    n = len(arrays)

    def body(*refs):
        ins, lands = refs[:n], refs[n:2 * n]
        send_sems, recv_sems = refs[2 * n:]
        x, y, c = lax.axis_index("x"), lax.axis_index("y"), lax.axis_index("c")
        _handshake([(x, y, 1 - c)])
        copies = [pltpu.make_async_remote_copy(src_ref=ins[i].at[1 - c], dst_ref=lands[i], send_sem=send_sems.at[i],
                                               recv_sem=recv_sems.at[i], device_id=(x, y, 1 - c), device_id_type=MESH) for i in range(n)]
        for cp in copies:
            cp.start()
        for cp in copies:
            cp.wait_recv()
        for cp in copies:
            cp.wait_send()

    return _sequencer_call(body, arrays, [_sds(a.shape[1:], a.dtype) for a in arrays], [n, n], name, collective_id)


_HBM = pl.BlockSpec(memory_space=pltpu.HBM)
_SEM = pl.BlockSpec(memory_space=pltpu.SEMAPHORE)
_SPLIT_COPY = dict(has_side_effects=pltpu.SideEffectType.DATAFLOW_SIDE_EFFECTING)


def _chip_copy(ins, lands, send_sems, recv_sems, scatter, i, k, receive):
    x, y, c = lax.axis_index("x"), lax.axis_index("y"), lax.axis_index("c")
    px, py = _xy_peers(x, y)[k]
    theirs, mine = 2 * px + py, 2 * x + y
    src = ins[i].at[theirs] if scatter[i] else ins[i].at[0]
    return pltpu.make_async_remote_copy(src_ref=src, dst_ref=lands[i].at[theirs if receive else mine], send_sem=send_sems.at[3 * i + k],
                                        recv_sem=recv_sems.at[3 * i + k], device_id=(px, py, c), device_id_type=MESH)


def send_to_chips_start(arrays, scatter, *, name):
    n = len(arrays)

    def body(*refs):
        send_sems, recv_sems = refs[2 * n], refs[2 * n + 1]
        ins, lands = refs[2 * n + 2:3 * n + 2], refs[3 * n + 2:4 * n + 2]
        token = refs[4 * n + 2]
        for i in range(n):
            for k in range(3):
                _chip_copy(ins, lands, send_sems, recv_sems, scatter, i, k, receive=False).start()
        token[...] = jnp.zeros_like(token)

    land_shapes = [(N_SHARD,) + a.shape[1:] for a in arrays]
    operands = [pltpu.with_memory_space_constraint(a, pltpu.HBM) for a in arrays]
    operands += [pltpu.with_memory_space_constraint(lax.empty(s, a.dtype), pltpu.HBM) for s, a in zip(land_shapes, arrays)]
    out_shape = ([pltpu.SemaphoreType.DMA((3 * n,)), pltpu.SemaphoreType.DMA((3 * n,))] + [pltpu.HBM(a.shape, a.dtype) for a in arrays]
                 + [pltpu.HBM(s, a.dtype) for s, a in zip(land_shapes, arrays)] + [_sds((8, LANES), F32)])
    res = pl.pallas_call(body, name=name, out_shape=out_shape, in_specs=[_HBM] * (2 * n),
                         out_specs=[_SEM, _SEM] + [_HBM] * (2 * n) + [pl.BlockSpec(memory_space=pltpu.VMEM)],
                         input_output_aliases={i: 2 + i for i in range(2 * n)}, compiler_params=pltpu.CompilerParams(**_SPLIT_COPY))(*operands)
    return (res[0], res[1], res[2:2 + n], res[2 + n:2 + 2 * n], scatter), res[-1]


def send_to_chips_wait(state, after, *, name):
    send_sems, recv_sems, arrays, lands, scatter = state
    n = len(arrays)

    def body(*refs):
        ins, landing = refs[:n], refs[n:2 * n]
        send_sems, recv_sems = refs[2 * n], refs[2 * n + 1]
        for i in range(n):
            for k in range(3):
                _chip_copy(ins, landing, send_sems, recv_sems, scatter, i, k, receive=True).wait_recv()
        for i in range(n):
            for k in range(3):
                _chip_copy(ins, landing, send_sems, recv_sems, scatter, i, k, receive=False).wait_send()

    out_shape = [pltpu.HBM(a.shape, a.dtype) for a in list(arrays) + list(lands)]
    res = pl.pallas_call(body, name=name, out_shape=out_shape, in_specs=[_HBM] * (2 * n) + [_SEM, _SEM] + [_ANY] * len(after),
                         out_specs=[_HBM] * (2 * n), input_output_aliases={i: i for i in range(2 * n)},
                         compiler_params=pltpu.CompilerParams(**_SPLIT_COPY))(*arrays, *lands, send_sems, recv_sems, *after)
    return res[:n], res[n:]


def swap_with_other_core(arrays, *, name, collective_id):
    n = len(arrays)

    def body(*refs):
        ins, lands = refs[:n], refs[n:2 * n]
        send_sems, recv_sems = refs[2 * n:]
        x, y, c = lax.axis_index("x"), lax.axis_index("y"), lax.axis_index("c")
        _handshake([(x, y, 1 - c)])
        copies = [pltpu.make_async_remote_copy(src_ref=ins[i], dst_ref=lands[i], send_sem=send_sems.at[i], recv_sem=recv_sems.at[i],
                                               device_id=(x, y, 1 - c), device_id_type=MESH) for i in range(n)]
        for cp in copies:
            cp.start()
        for cp in copies:
            cp.wait_recv()
        for cp in copies:
            cp.wait_send()

    return _sequencer_call(body, arrays, [_sds(a.shape, a.dtype) for a in arrays], [n, n], name, collective_id)


def _pack_rows(n_elems, row_multiple):
    rows = -(-n_elems // LANES)
    return -(-rows // row_multiple) * row_multiple


def _pack(arrays, rows, dtype):
    flat = jnp.concatenate([a.reshape(-1).astype(dtype) for a in arrays])
    return jnp.pad(flat, (0, rows * LANES - flat.shape[0])).reshape(rows, LANES)


def _unpack(packed, shapes):
    flat = packed.reshape(-1)
    out, off = [], 0
    for s in shapes:
        n = int(np.prod(s))
        out.append(flat[off:off + n].reshape(s))
        off += n
    return out


def all_gather_shards(shards, axes, dtype, row_multiple, tag):
    shapes = [s.shape for s in shards]
    rows = _pack_rows(sum(int(np.prod(s)) for s in shapes), row_multiple)
    packed = _pack(shards, rows, dtype).reshape(2, rows // 2, LANES)
    land = xy_exchange(packed, scatter=False, name=f"gather_xy_{tag}")
    both = core_exchange(land, send_other_half=False, name=f"gather_c_{tag}")
    per_shard = jnp.swapaxes(both, 0, 1).reshape(N_SHARD, rows, LANES)
    pieces = [_unpack(per_shard[s], shapes) for s in range(N_SHARD)]
    return [jnp.concatenate([pieces[s][i] for s in range(N_SHARD)], axis=ax) for i, ax in enumerate(axes)]


def _ordered_before(first, then):
    if then is None:
        return first, None
    return lax.optimization_barrier((first, then))


def reduce_between_cores(arrays, scatter, *, tag, collective_id, before=None):
    arrays, before = _ordered_before(arrays, before)
    land = send_other_half(arrays, name=f"reduce_core_send_{tag}", collective_id=collective_id)
    return (arrays, land, scatter, tag, collective_id), before


def reduce_between_chips(state, before=None):
    arrays, land, scatter, tag, collective_id = state
    chip = [add_core_halves(a, l, out_dtype=BF16 if sc else F32, name=f"reduce_core_add_{tag}_{i}")
            for i, (a, l, sc) in enumerate(zip(arrays, land, scatter))]
    sending, token = send_to_chips_start(chip, scatter, name=f"reduce_chip_start_{tag}")
    token, before = _ordered_before(token, before)
    return (sending, token, scatter, tag, collective_id), before


def reduce_finish(state, after):
    sending, token, scatter, tag, collective_id = state
    chip, land = send_to_chips_wait(sending, tuple(after) + (token,), name=f"reduce_chip_wait_{tag}")
    own = [sum_over_chips(ch, l, scatter=sc, name=f"reduce_chip_add_{tag}_{i}") for i, (ch, l, sc) in enumerate(zip(chip, land, scatter))]
    sib = swap_with_other_core(own, name=f"reduce_core_swap_{tag}", collective_id=collective_id + 2)
    return own, sib


def _ffn_layer_fwd(h, norm_g, w_up, cw, cb, w_down, tag):
    u, hn = norm_matmul(h, norm_g, w_up, name=f"ffn_up_{tag}")
    act = ffn_act_fwd(u, cw, cb, name=f"ffn_act_{tag}")
    out = matmul(act, w_down, add=h, name=f"ffn_down_{tag}")
    return out, (h, hn, u, act)


def _travel_layout(array):
    return BIG_ARRAYS[array][3], BIG_ARRAYS[array][4]


def _ffn_layer_bwd(saved, dout, norm_g, w_up, cw, cb, w_down, tag):
    h, hn, u, act = saved
    dact = matmul(dout, w_down, tb=True, name=f"ffn_down_dx_{tag}")
    d_w_down = matmul(act, dout, ta=True, split=_travel_layout(f"ffn_w_down_{tag}"), name=f"ffn_down_dw_{tag}")
    du, dcw, dcb = ffn_act_bwd(u, cw, cb, dact, name=f"ffn_act_bwd_{tag}")
    dhn = matmul(du, w_up, tb=True, name=f"ffn_up_dx_{tag}")
    d_w_up = matmul(hn, du, ta=True, split=_travel_layout(f"ffn_w_up_{tag}"), name=f"ffn_up_dw_{tag}")
    dh, dg = norm_bwd(h, norm_g, dhn, dout, name=f"ffn_norm_bwd_{tag}")
    return dh, dg, d_w_up, dcw, dcb, d_w_down


def local_step(x, target, w, stage=lambda name, tensors, grads=None: tensors):
    g = {}
    tables = _ret_tables()
    x = stage("start", x)
    w_in_t = w["ret_gdn_w_in"]
    w_main = w_in_t[:MIX_MAIN]
    w_small = jnp.pad(w_in_t[MIX_MAIN:], ((0, LANES - 2 * N_HEADS), (0, 0)))
    a_log = jnp.pad(w["gdn_a_log"], ((0, 0), (0, LANES - N_HEADS)))
    dt_bias = jnp.pad(w["gdn_dt_bias"], ((0, 0), (0, LANES - N_HEADS)))

    p, hn0 = norm_matmul(x, w["norm_mix"][0:1], w_main, tb=True, name="mix0_in")
    hn0 = stage("normed", hn0)
    small = matmul(hn0, w_small, tb=True, name="mix0_in_small")
    y_ret, s_ret = ret_fwd(p, tables, name="ret_fwd")
    conv = gdn_conv_fwd(p, w["gdn_conv_w"], name="gdn_conv")
    y0, s_gdn = gdn_fwd(conv, p, small, a_log, dt_bias, w["gdn_out_gain"], y_ret, name="gdn_fwd")
    y0 = stage("mixed", y0)
    h1 = matmul(y0, w["ret_gdn_w_out"], add=x, name="mix0_out")
    h2, ffn0 = _ffn_layer_fwd(h1, w["norm_ffn"][0:1], w["ffn_w_up"][0], w["ffn_conv_w"][0], w["ffn_conv_b"][0:1], w["ffn_w_down"][0], "0")
    h2 = stage("layer0", h2)

    gx, hn1 = norm_matmul(h2, w["norm_mix"][1:2], w["lru_w_in"], name="mix1_in")
    lru_p = (w["lru_conv_w"], w["lru_conv_b"], w["lru_w_a"], w["lru_b_a"], w["lru_w_x"], w["lru_b_x"], w["lru_lambda"])
    y1 = lru_fwd(gx, *lru_p, name="lru_fwd")
    h3 = stage("mixed1", matmul(y1, w["lru_w_out"], add=h2, name="mix1_out"))
    h4, ffn1 = _ffn_layer_fwd(h3, w["norm_ffn"][1:2], w["ffn_w_up"][1], w["ffn_conv_w"][1], w["ffn_conv_b"][1:2], w["ffn_w_down"][1], "1")

    loss, dh4, g["norm_final"] = final_fwd_bwd(h4, w["norm_final"], target, name="final")

    dh3, dgf1, dwu1, dcw1, dcb1, dwd1 = _ffn_layer_bwd(ffn1, dh4, w["norm_ffn"][1:2], w["ffn_w_up"][1], w["ffn_conv_w"][1],
                                                     w["ffn_conv_b"][1:2], w["ffn_w_down"][1], "1")
    g["ffn_w_up_1"], g["ffn_w_down_1"] = dwu1, dwd1
    dh3 = stage("grads0_ready", dh3, g)
    dy1 = matmul(dh3, w["lru_w_out"], tb=True, name="mix1_out_dx")
    g["lru_w_out"] = matmul(y1, dh3, ta=True, split=_travel_layout("lru_w_out"), name="mix1_out_dw")
    dgx, g["lru_conv_w"], g["lru_conv_b"], g["lru_w_a"], g["lru_b_a"], g["lru_w_x"], g["lru_b_x"], g["lru_lambda"] = lru_bwd(
        gx, *lru_p, dy1, name="lru_bwd")
    dgx = stage("grads0_send", dgx, g)
    dhn1 = matmul(dgx, w["lru_w_in"], tb=True, name="mix1_in_dx")
    g["lru_w_in"] = matmul(hn1, dgx, ta=True, split=_travel_layout("lru_w_in"), name="mix1_in_dw")
    dh2, dgm1 = norm_bwd(h2, w["norm_mix"][1:2], dhn1, dh3, name="mix1_norm_bwd")
    dh2 = stage("grads1_ready", dh2, g)

    dh1, dgf0, dwu0, dcw0, dcb0, dwd0 = _ffn_layer_bwd(ffn0, dh2, w["norm_ffn"][0:1], w["ffn_w_up"][0], w["ffn_conv_w"][0],
                                                     w["ffn_conv_b"][0:1], w["ffn_w_down"][0], "0")
    g["ffn_w_up_0"], g["ffn_w_down_0"] = dwu0, dwd0
    dh1 = stage("grads2_ready", stage("grads1_send", dh1, g), g)
    dy0 = matmul(dh1, w["ret_gdn_w_out"], tb=True, name="mix0_out_dx")
    g["ret_gdn_w_out"] = matmul(y0, dh1, ta=True, split=_travel_layout("ret_gdn_w_out"), name="mix0_out_dw")
    dq_r, dk_r, dv_r, dg_r = ret_bwd(p, tables, s_ret, dy0, name="ret_bwd")
    dy0, dq_r = stage("grads2_send", (dy0, dq_r), g)
    dconv, dg_d, dsmall, dal, ddt, dgain = gdn_bwd(conv, p, small, a_log, dt_bias, w["gdn_out_gain"], s_gdn, dy0, name="gdn_bwd")
    dp_conv, g["gdn_conv_w"] = gdn_conv_bwd(p, w["gdn_conv_w"], dconv, name="gdn_conv_bwd")
    dp = jnp.concatenate([dq_r, dk_r, dv_r, dg_r, dp_conv, dg_d], axis=1)
    dhn0 = matmul(dp, w_main, name="mix0_in_dx")
    dhn0 = matmul(dsmall, w_small, add=dhn0, name="mix0_in_small_dx")
    d_w_in = matmul(dp, hn0, ta=True, column_halves=(MIX_IN, 0, None, None), name="mix0_in_dw")
    d_w_in = matmul(dsmall, hn0, ta=True, column_halves=(MIX_IN, MIX_MAIN, 2 * N_HEADS, d_w_in), name="mix0_in_small_dw")
    g["ret_gdn_w_in"] = d_w_in.reshape(2, N_SHARD, MIX_IN // N_SHARD, D_MODEL // 2)
    dx, dgm0 = norm_bwd(x, w["norm_mix"][0:1], dhn0, dh1, name="mix0_norm_bwd")

    g["gdn_a_log"] = dal[:, :N_HEADS]
    g["gdn_dt_bias"] = ddt[:, :N_HEADS]
    g["gdn_out_gain"] = dgain
    g["norm_mix"] = jnp.concatenate([dgm0, dgm1], axis=0)
    g["norm_ffn"] = jnp.concatenate([dgf0, dgf1], axis=0)
    g["ffn_conv_w"] = jnp.stack([dcw0, dcw1])
    g["ffn_conv_b"] = jnp.concatenate([dcb0, dcb1], axis=0)
    return loss, dx, g


WEIGHTS = ("norm_mix", "norm_ffn", "ret_gdn_w_in", "gdn_conv_w", "gdn_a_log", "gdn_dt_bias", "gdn_out_gain", "ret_gdn_w_out",
           "lru_w_in", "lru_conv_w", "lru_conv_b", "lru_w_a", "lru_b_a", "lru_w_x", "lru_b_x", "lru_lambda", "lru_w_out",
           "ffn_w_up", "ffn_conv_w", "ffn_conv_b", "ffn_w_down", "norm_final")
MATMUL_SHARDED = {"ret_gdn_w_in": 1, "ret_gdn_w_out": 0, "lru_w_in": 1, "lru_w_out": 0, "ffn_w_up": 2, "ffn_w_down": 1}
VECTOR_SHARDED = {"gdn_conv_w": 1, "lru_conv_w": 1, "lru_conv_b": 1, "lru_b_a": 1, "lru_b_x": 1, "lru_lambda": 1, "ffn_conv_w": 2}
SHARDED = {**MATMUL_SHARDED, **VECTOR_SHARDED}
REPLICATED = tuple(n for n in WEIGHTS if n not in SHARDED)
SQUEEZE = {"ret_gdn_w_in", "gdn_conv_w", "ret_gdn_w_out", "lru_w_in", "lru_conv_w", "lru_w_a", "lru_w_x", "lru_w_out"}
MIX_IN = MIX_MAIN + 2 * N_HEADS
BIG_ARRAYS = {
    "ret_gdn_w_in": ("ret_gdn_w_in", None, (MIX_IN, D_MODEL), (N_SHARD, MIX_IN // N_SHARD, 2, D_MODEL // 2), (2, 0, 1, 3)),
    "ret_gdn_w_out": ("ret_gdn_w_out", None, (2 * GROUP, D_MODEL), (N_SHARD, 2, GROUP // N_SHARD, D_MODEL), (1, 0, 2, 3)),
    "lru_w_in": ("lru_w_in", None, (D_MODEL, 2 * D_MODEL), (2, D_MODEL // 2, N_SHARD, 2 * D_MODEL // N_SHARD), (0, 2, 1, 3)),
    "lru_w_out": ("lru_w_out", None, (D_MODEL, D_MODEL), (N_SHARD, 2, D_MODEL // (2 * N_SHARD), D_MODEL), (1, 0, 2, 3)),
    "ffn_w_up_0": ("ffn_w_up", 0, (D_MODEL, 2 * D_FF), (2, D_MODEL // 2, N_SHARD, 2 * D_FF // N_SHARD), (0, 2, 1, 3)),
    "ffn_w_up_1": ("ffn_w_up", 1, (D_MODEL, 2 * D_FF), (2, D_MODEL // 2, N_SHARD, 2 * D_FF // N_SHARD), (0, 2, 1, 3)),
    "ffn_w_down_0": ("ffn_w_down", 0, (D_FF, D_MODEL), (N_SHARD, 2, D_FF // (2 * N_SHARD), D_MODEL), (1, 0, 2, 3)),
    "ffn_w_down_1": ("ffn_w_down", 1, (D_FF, D_MODEL), (N_SHARD, 2, D_FF // (2 * N_SHARD), D_MODEL), (1, 0, 2, 3)),
}
GATHER_GROUPS = (("ret_gdn_w_in",), ("ret_gdn_w_out", "ffn_w_up_0", "ffn_w_down_0"), ("lru_w_in", "lru_w_out"), ("ffn_w_up_1", "ffn_w_down_1"))
REDUCE_GROUPS = (("ffn_w_up_1", "ffn_w_down_1"), ("lru_w_in", "lru_w_out"), ("ffn_w_up_0", "ffn_w_down_0"), ("ret_gdn_w_out", "ret_gdn_w_in"))
BLOCK_WEIGHTS = ("lru_w_a", "lru_w_x")
GATHER_COLLECTIVE_ID = 1
REDUCE_COLLECTIVE_ID = GATHER_COLLECTIVE_ID + len(GATHER_GROUPS)


TRANSPOSED = ("ret_gdn_w_in",)


def _shard_of(array, tensors):
    weight, layer = BIG_ARRAYS[array][:2]
    t = tensors[weight]
    if weight in TRANSPOSED:
        return jnp.swapaxes(t, 1, 2)[0]
    return _local_view(weight, t) if layer is None else t[layer]


def _core_halves(array, shard):
    _, _, _, split, perm = BIG_ARRAYS[array]
    kept = [k for k in range(4) if k != perm[1]]
    order = [kept.index(perm[0]), kept.index(perm[2]), kept.index(perm[3])]
    return shard.reshape([split[k] for k in kept]).transpose(order)


def _local_view(name, a):
    if name in SQUEEZE:
        return a[0]
    if a.ndim == 1:
        return a[None, :]
    return a


def kernel(x, norm_mix, norm_ffn, ret_gdn_w_in, gdn_conv_w, gdn_a_log, gdn_dt_bias, gdn_out_gain, ret_gdn_w_out, lru_w_in, lru_conv_w, lru_conv_b, lru_w_a, lru_b_a, lru_w_x, lru_b_x, lru_lambda, lru_w_out, ffn_w_up, ffn_conv_w, ffn_conv_b, ffn_w_down, norm_final, loss_target, m_norm_mix, m_norm_ffn, m_ret_gdn_w_in, m_gdn_conv_w, m_gdn_a_log, m_gdn_dt_bias, m_gdn_out_gain, m_ret_gdn_w_out, m_lru_w_in, m_lru_conv_w, m_lru_conv_b, m_lru_w_a, m_lru_b_a, m_lru_w_x, m_lru_b_x, m_lru_lambda, m_lru_w_out, m_ffn_w_up, m_ffn_conv_w, m_ffn_conv_b, m_ffn_w_down, m_norm_final, v_norm_mix, v_norm_ffn, v_ret_gdn_w_in, v_gdn_conv_w, v_gdn_a_log, v_gdn_dt_bias, v_gdn_out_gain, v_ret_gdn_w_out, v_lru_w_in, v_lru_conv_w, v_lru_conv_b, v_lru_w_a, v_lru_b_a, v_lru_w_x, v_lru_b_x, v_lru_lambda, v_lru_w_out, v_ffn_w_up, v_ffn_conv_w, v_ffn_conv_b, v_ffn_w_down, v_norm_final):
    given = dict(norm_mix=norm_mix, norm_ffn=norm_ffn, ret_gdn_w_in=ret_gdn_w_in, gdn_conv_w=gdn_conv_w, gdn_a_log=gdn_a_log, gdn_dt_bias=gdn_dt_bias, gdn_out_gain=gdn_out_gain, ret_gdn_w_out=ret_gdn_w_out, lru_w_in=lru_w_in, lru_conv_w=lru_conv_w, lru_conv_b=lru_conv_b, lru_w_a=lru_w_a, lru_b_a=lru_b_a, lru_w_x=lru_w_x, lru_b_x=lru_b_x, lru_lambda=lru_lambda, lru_w_out=lru_w_out, ffn_w_up=ffn_w_up, ffn_conv_w=ffn_conv_w, ffn_conv_b=ffn_conv_b, ffn_w_down=ffn_w_down, norm_final=norm_final)
    mom1 = dict(norm_mix=m_norm_mix, norm_ffn=m_norm_ffn, ret_gdn_w_in=m_ret_gdn_w_in, gdn_conv_w=m_gdn_conv_w, gdn_a_log=m_gdn_a_log, gdn_dt_bias=m_gdn_dt_bias, gdn_out_gain=m_gdn_out_gain, ret_gdn_w_out=m_ret_gdn_w_out, lru_w_in=m_lru_w_in, lru_conv_w=m_lru_conv_w, lru_conv_b=m_lru_conv_b, lru_w_a=m_lru_w_a, lru_b_a=m_lru_b_a, lru_w_x=m_lru_w_x, lru_b_x=m_lru_b_x, lru_lambda=m_lru_lambda, lru_w_out=m_lru_w_out, ffn_w_up=m_ffn_w_up, ffn_conv_w=m_ffn_conv_w, ffn_conv_b=m_ffn_conv_b, ffn_w_down=m_ffn_w_down, norm_final=m_norm_final)
    mom2 = dict(norm_mix=v_norm_mix, norm_ffn=v_norm_ffn, ret_gdn_w_in=v_ret_gdn_w_in, gdn_conv_w=v_gdn_conv_w, gdn_a_log=v_gdn_a_log, gdn_dt_bias=v_gdn_dt_bias, gdn_out_gain=v_gdn_out_gain, ret_gdn_w_out=v_ret_gdn_w_out, lru_w_in=v_lru_w_in, lru_conv_w=v_lru_conv_w, lru_conv_b=v_lru_conv_b, lru_w_a=v_lru_w_a, lru_b_a=v_lru_b_a, lru_w_x=v_lru_w_x, lru_b_x=v_lru_b_x, lru_lambda=v_lru_lambda, lru_w_out=v_lru_w_out, ffn_w_up=v_ffn_w_up, ffn_conv_w=v_ffn_conv_w, ffn_conv_b=v_ffn_conv_b, ffn_w_down=v_ffn_w_down, norm_final=v_norm_final)

    local = {n: _local_view(n, a) for n, a in given.items()}

    core = lax.axis_index("c")
    chip = 2 * lax.axis_index("x") + lax.axis_index("y")
    is_my_chip = lax.broadcasted_iota(jnp.int32, (N_SHARD, 1, 1), 0) == chip

    def by_core(mine, other):
        return jnp.where(core == 0, jnp.stack([mine, other]), jnp.stack([other, mine]))

    vec_names, rp_names = list(VECTOR_SHARDED), list(REPLICATED)
    full = dict(zip(vec_names, all_gather_shards([local[n] for n in vec_names], [SHARDED[n] for n in vec_names], F32, 32, "p")))
    for n in rp_names:
        full[n] = local[n]
    in_flight = {}

    bf16_halves = {}

    def cast_halves(gi):
        if gi not in bf16_halves:
            bf16_halves[gi] = [_core_halves(a, _shard_of(a, given).astype(BF16)) for a in GATHER_GROUPS[gi]]
        return bf16_halves[gi]

    def launch(gi, after=None):
        halves = cast_halves(gi)
        if after is not None:
            halves, after = lax.optimization_barrier((halves, after))
        in_flight[gi] = (halves,) + gather_halves(halves, name=f"gather_weights_{gi}", collective_id=GATHER_COLLECTIVE_ID + gi)
        return after

    def land(gi, after):
        halves, lands, sibs = in_flight[gi]
        (lands, sibs), after = lax.optimization_barrier(((lands, sibs), after))
        for a, mine, got, passed in zip(GATHER_GROUPS[gi], halves, lands, sibs):
            weight, layer, full_shape, split, perm = BIG_ARRAYS[a]
            half_mine = jnp.where(is_my_chip, jnp.where(core == 0, mine[0], mine[1])[None], got)
            half_other = jnp.where(is_my_chip, jnp.where(core == 0, mine[1], mine[0])[None], passed)
            value = by_core(half_mine, half_other).transpose(tuple(np.argsort(perm))).reshape(full_shape)
            if layer is None:
                full[weight] = value
            else:
                full.setdefault(weight, [None, None])[layer] = value
        return after

    reducing = {}

    def reduce_ready(gi, grads, then=None, extra=()):
        def travelling(a):
            split, perm = _travel_layout(a)
            return grads[a] if grads[a].ndim == 4 else grads[a].reshape(split).transpose(perm)

        arrays = [travelling(a) for a in REDUCE_GROUPS[gi]] + list(extra)
        scatter = [True] * len(REDUCE_GROUPS[gi]) + [False] * len(extra)
        reducing[gi], then = reduce_between_cores(arrays, scatter, tag=str(gi), collective_id=REDUCE_COLLECTIVE_ID + 3 * gi, before=then)
        return then

    def reduce_send(gi, then=None):
        reducing[gi], then = reduce_between_chips(reducing[gi], before=then)
        return then

    def stage(name, tensors, grads=None):
        if name == "start":
            launch(0)
            launch(1)
            fillers = (cast_halves(2), cast_halves(3), [full[n] for n in vec_names])
            (bf16_halves[2], bf16_halves[3], gathered_small), tensors = lax.optimization_barrier((fillers, tensors))
            full.update(zip(vec_names, gathered_small))
            return land(0, tensors)
        if name == "normed":
            return launch(3, launch(2, tensors))
        if name in ("mixed", "layer0", "mixed1"):
            return land({"mixed": 1, "layer0": 2, "mixed1": 3}[name], tensors)
        gi = int(name[len("grads")])
        return reduce_ready(gi, grads, tensors) if name.endswith("_ready") else reduce_send(gi, tensors)

    small_names = [n for n in rp_names if n not in BLOCK_WEIGHTS] + vec_names

    loss_part, dx, grads = local_step(x[0], loss_target[0], full, stage)
    small_shapes = [grads[n].shape for n in small_names] + [(1, 1)]
    small_rows = _pack_rows(sum(int(np.prod(s)) for s in small_shapes), 16)
    small = _pack([grads[n] for n in small_names] + [loss_part[:, :1]], small_rows, F32).reshape(2, 1, small_rows // 2, LANES)
    last = len(REDUCE_GROUPS) - 1
    halves_of_blocks = [grads[n].reshape(2, 1, LRU_BLOCKS * HEAD // 2, HEAD) for n in BLOCK_WEIGHTS]
    reduce_ready(last, grads, extra=[small] + halves_of_blocks)
    reduce_send(last)
    reduced, result = {}, {}

    def finish(gi, after):
        g_own, g_sib = reduce_finish(reducing[gi], after)
        reduced.update(zip(list(REDUCE_GROUPS[gi]) + ["small"] + list(BLOCK_WEIGHTS), zip(g_own, g_sib)))

    def update(n):
        if n in TRANSPOSED:
            n_rows, n_cols = given[n].shape[2], given[n].shape[1]

            def rows(t):
                return jnp.swapaxes(t, 1, 2).reshape(n_rows, 1, n_cols)

            def back(t):
                return jnp.swapaxes(t.reshape(1, n_rows, n_cols), 1, 2)

            g_rows = jnp.swapaxes(by_core(*reduced[n]), 0, 1).reshape(n_rows, 1, n_cols)
            result[n] = tuple(back(t) for t in adamw_rows(rows(given[n]), g_rows, rows(mom1[n]), rows(mom2[n]), name=f"adamw_{n}"))
            return
        done = None
        for a in (k for k, spec in BIG_ARRAYS.items() if spec[0] == n):
            r, cols = reduced[a][0].shape
            layer = BIG_ARRAYS[a][1] or 0
            w3, m3, v3 = (t if BIG_ARRAYS[a][1] is not None else t.reshape(1, 2 * r, cols) for t in (given[n], mom1[n], mom2[n]))
            done = adamw_halves(w3, m3, v3, *reduced[a], layer=layer, prev=done, name=f"adamw_{a}")
        result[n] = done

    updated = []
    for gi in range(last + 1):
        finish(gi, tuple(result[n][0] for n in updated) if updated else (dx, reducing[last][1]))
        for n in MATMUL_SHARDED:
            if n not in updated and all(a in reduced for a, spec in BIG_ARRAYS.items() if spec[0] == n):
                update(n)
                updated.append(n)

    for n in BLOCK_WEIGHTS:
        w3, m3, v3 = (t.reshape(1, LRU_BLOCKS * HEAD, HEAD) for t in (given[n], mom1[n], mom2[n]))
        result[n] = adamw_halves(w3, m3, v3, *reduced[n], name=f"adamw_{n}")

    *small_sums, loss_sum = _unpack(by_core(*reduced["small"]).reshape(small_rows, LANES), small_shapes)
    loss = loss_sum[0, 0]
    g_small = dict(zip(small_names, small_sums))
    for n in vec_names:
        size = local[n].shape[SHARDED[n]]
        g_small[n] = lax.dynamic_slice_in_dim(g_small[n], chip * size, size, axis=SHARDED[n])
    views = [[_local_view(n, src[n]) for n in small_names] for src in (given, mom1, mom2)]
    d_s, m_s, v_s = adamw_many(views[0], [g_small[n] for n in small_names], views[1], views[2], name="adamw_small")
    for n, d, nm, nv in zip(small_names, d_s, m_s, v_s):
        result[n] = (g_small[n], d, nm, nv)

    outs = [[result[n][k].reshape(given[n].shape) for n in WEIGHTS] for k in range(4)]
    return (loss, dx[None], *outs[0], *outs[1], *outs[2], *outs[3])
```

```python
import functools

import numpy as np
import jax
import jax.numpy as jnp
from jax import lax
from jax.experimental import pallas as pl
from jax.experimental.pallas import tpu as pltpu
from jax.experimental.pallas import tpu_sc as plsc

F32 = jnp.float32
BF16 = jnp.bfloat16
MESH = pl.DeviceIdType.MESH

SEQ = 2048
D_MODEL = 1024
N_HEADS = 4
HEAD = 128
RET_CHUNK = 128
RET_CHUNKS_PER_STEP = 2
GDN_CHUNK = 64
GDN_CHUNKS_PER_STEP = 8
GROUP = N_HEADS * HEAD
MIX_MAIN = 8 * GROUP
D_FF = 2816
LRU_BLOCKS = 8
LRU_C = 8.0
ROPE_BASE = 10000.0
EPS = 1e-6
N_SHARD = 4
LANES = 128

ADAM_LR, ADAM_B1, ADAM_B2, ADAM_EPS, ADAM_WD, ADAM_STEP = 0.001, 0.9, 0.999, 1e-08, 0.01, 10

VMEM_LIMIT_BYTES = 56 * 1024 * 1024

_roll = pltpu.roll


def _params(**kw):
    return pltpu.CompilerParams(vmem_limit_bytes=VMEM_LIMIT_BYTES, **kw)


def _sds(shape, dtype):
    return jax.ShapeDtypeStruct(tuple(shape), dtype)


def _shift_raw(x, d):
    n = x.shape[0]
    t = lax.broadcasted_iota(jnp.int32, x.shape, 0)
    if d > 0:
        return jnp.where(t >= d, _roll(x, d, 0), 0.0)
    return jnp.where(t < n + d, _roll(x, n + d, 0), 0.0)


@functools.partial(jax.custom_vjp, nondiff_argnums=(1,))
def shift_rows(x, d):
    return _shift_raw(x, d)


def _shift_fwd(x, d):
    return _shift_raw(x, d), None


def _shift_bwd(d, _, g):
    return (_shift_raw(g, -d),)


shift_rows.defvjp(_shift_fwd, _shift_bwd)


@jax.custom_vjp
def swap_halves(x):
    return _roll(x, HEAD // 2, 1)


def _swap_fwd(x):
    return _roll(x, HEAD // 2, 1), None


def _swap_bwd(_, g):
    return (_roll(g, HEAD // 2, 1),)


swap_halves.defvjp(_swap_fwd, _swap_bwd)


SCAN_BLOCK_ROWS = 16


def _scan_block(a, u, reverse):
    n = a.shape[0]
    t = lax.broadcasted_iota(jnp.int32, a.shape, 0)
    d = 1
    while d < n:
        if reverse:
            m = t < n - d
            a_s, u_s = _roll(a, n - d, 0), _roll(u, n - d, 0)
        else:
            m = t >= d
            a_s, u_s = _roll(a, d, 0), _roll(u, d, 0)
        u = a * jnp.where(m, u_s, 0.0) + u
        a = a * jnp.where(m, a_s, 1.0)
        d *= 2
    return a, u


def _scan_raw(a, u, reverse):
    n = a.shape[0]
    blocks = range(n // SCAN_BLOCK_ROWS)
    out = [None] * len(blocks)
    entering = None
    for b in (reversed(blocks) if reverse else blocks):
        rows = slice(b * SCAN_BLOCK_ROWS, (b + 1) * SCAN_BLOCK_ROWS)
        a_run, h = _scan_block(a[rows], u[rows], reverse)
        if entering is not None:
            h = a_run * entering + h
        out[b] = h
        entering = h[:1] if reverse else h[SCAN_BLOCK_ROWS - 1:]
    return jnp.concatenate(out, axis=0)


@jax.custom_vjp
def lin_scan(a, u):
    return _scan_raw(a, u, False)


def _lin_scan_fwd(a, u):
    hs = _scan_raw(a, u, False)
    return hs, (a, hs)


def _lin_scan_bwd(res, g):
    a, hs = res
    lam = _scan_raw(_shift_raw(a, -1), g, True)
    return lam * _shift_raw(hs, 1), lam


lin_scan.defvjp(_lin_scan_fwd, _lin_scan_bwd)


def _bdot(a, b, dims=(((1,), (0,)), ((), ()))):
    return lax.dot_general(a.astype(BF16), b.astype(BF16), dims, preferred_element_type=F32)


def _each(f, *seqs):
    return tuple(f(*a) for a in zip(*seqs))


def _split_bf16(a):
    hi = a.astype(BF16)
    return hi, (a - hi.astype(F32)).astype(BF16)


def _dot3_raw(a_s, b_s):
    a_hl = _each(_split_bf16, a_s)
    b_hl = _each(_split_bf16, b_s)
    hh = _each(lambda a, b: _bdot(a[0], b[0]), a_hl, b_hl)
    hl = _each(lambda a, b: _bdot(a[0], b[1]), a_hl, b_hl)
    lh = _each(lambda a, b: _bdot(a[1], b[0]), a_hl, b_hl)
    return _each(lambda x, y, z: x + (y + z), hh, hl, lh)


@jax.custom_vjp
def dot3(a_s, b_s):
    return _dot3_raw(a_s, b_s)


def _dot3_fwd(a_s, b_s):
    return _dot3_raw(a_s, b_s), (a_s, b_s)


def _dot3_bwd(res, g_s):
    a_s, b_s = res
    return (_each(lambda g, b: _bdot(g, b, (((1,), (1,)), ((), ()))), g_s, b_s),
            _each(lambda a, g: _bdot(a, g, (((0,), (0,)), ((), ()))), a_s, g_s))


dot3.defvjp(_dot3_fwd, _dot3_bwd)


def _eye(n):
    i = lax.broadcasted_iota(jnp.int32, (n, n), 0)
    j = lax.broadcasted_iota(jnp.int32, (n, n), 1)
    return (i == j).astype(F32)


def _unit_lower_inverse_raw(lmats):
    n = lmats[0].shape[0]
    eye = _eye(n)
    ps = _each(lambda l: -l, lmats)
    invs = _each(lambda x: eye + x, ps)
    k = 1
    while 2 * k < n:
        ps = _each(lambda p: _bdot(p, p), ps)
        invs = _each(lambda inv, p: inv + _bdot(inv, p), invs, ps)
        k *= 2
    prods = _dot3_raw(lmats, invs)
    resids = _each(lambda inv, pr: eye - inv - pr, invs, prods)
    return _each(lambda inv, r: inv + _bdot(inv, r), invs, resids)


@jax.custom_vjp
def unit_lower_inverse(lmats):
    return _unit_lower_inverse_raw(lmats)


def _uli_fwd(lmats):
    invs = _unit_lower_inverse_raw(lmats)
    return invs, invs


def _uli_bwd(invs, g_s):
    ms = _each(lambda inv, g: _bdot(inv, g, (((0,), (0,)), ((), ()))), invs, g_s)
    return (_each(lambda m, inv: -_bdot(m, inv, (((1,), (1,)), ((), ()))), ms, invs),)


unit_lower_inverse.defvjp(_uli_fwd, _uli_bwd)


def _cumsum_raw(x, reverse):
    n = x.shape[0]
    t = lax.broadcasted_iota(jnp.int32, x.shape, 0)
    d = 1
    while d < n:
        if reverse:
            x = x + jnp.where(t < n - d, _roll(x, n - d, 0), 0.0)
        else:
            x = x + jnp.where(t >= d, _roll(x, d, 0), 0.0)
        d *= 2
    return x


@jax.custom_vjp
def cumsum_rows(x):
    return _cumsum_raw(x, False)


def _cumsum_fwd(x):
    return _cumsum_raw(x, False), None


def _cumsum_bwd(_, g):
    return (_cumsum_raw(g, True),)


cumsum_rows.defvjp(_cumsum_fwd, _cumsum_bwd)


_NT = (((1,), (1,)), ((), ()))
_TN = (((0,), (0,)), ((), ()))


def _softplus(x):
    return jnp.maximum(x, 0.0) + jnp.log1p(jnp.exp(-jnp.abs(x)))


def _expm1_nonpos(x):
    poly = x * (1.0 + x * (0.5 + x * (1.0 / 6 + x * (1.0 / 24 + x * (1.0 / 120 + x * (1.0 / 720))))))
    return jnp.where(x > -0.25, poly, jnp.exp(x) - 1.0)


def _rms(x):
    return x * lax.rsqrt(jnp.mean(x * x, axis=-1, keepdims=True) + EPS)


def _causal_conv(x, w, width):
    y = w[width - 1:width, :] * x
    for j in range(width - 1):
        y = y + w[j:j + 1, :] * shift_rows(x, width - 1 - j)
    return y


def _norm_fn(x, g):
    return _rms(x) * g


def _ffn_act_fn(ug, uv, wg, wv, bg, bv):
    return jax.nn.silu(_causal_conv(ug, wg, 3) + bg) * (_causal_conv(uv, wv, 3) + bv)


def _gdn_conv_fn(x, w):
    return jax.nn.silu(_causal_conv(x, w, 4))


def _lru_fn(gate, x, cw, cb, wa, ba, wx, bx, lam):
    xr = _causal_conv(x, cw, 4) + cb
    r = jax.nn.sigmoid(_bdot(xr, wa) + ba)
    i = jax.nn.sigmoid(_bdot(xr, wx) + bx)
    log_a = -LRU_C * r * _softplus(-lam)
    a = jnp.exp(log_a)
    u = jnp.sqrt(-_expm1_nonpos(2.0 * log_a)) * (i * xr)
    hs = lin_scan(a, u)
    return jax.nn.gelu(gate) * hs


def _ret_fn(qs, ks, vs, gates, states, cos2, sin2, dmasks, ktails, qdecs, cdecs):
    c = RET_CHUNK
    n_heads = len(qs)
    n_chunks = qs[0].shape[0] // c
    units = tuple((ci, h) for ci in range(n_chunks) for h in range(n_heads))

    def rows(x, ci):
        return x[ci * c:(ci + 1) * c]

    qrs = tuple(rows(qs[h], ci) * rows(cos2, ci) + swap_halves(rows(qs[h], ci)) * rows(sin2, ci) for ci, h in units)
    krs = tuple((rows(ks[h], ci) * rows(cos2, ci) + swap_halves(rows(ks[h], ci)) * rows(sin2, ci)) * (HEAD ** -0.5) for ci, h in units)
    vus = tuple(rows(vs[h], ci) for ci, h in units)
    scores = tuple(_bdot(q, k, _NT) * dmasks[h] for q, k, (_, h) in zip(qrs, krs, units))
    intra = _each(lambda sc, v: _bdot(sc, v), scores, vus)
    outs = []
    for ci in range(n_chunks):
        mine = slice(ci * n_heads, (ci + 1) * n_heads)
        inter = _each(lambda q, d, s: _bdot(q * d, s), qrs[mine], qdecs, states)
        outs.append(_each(lambda a, b: a + b, intra[mine], inter))
        states = _each(lambda s, cd, k, kt, v: s * cd + _bdot(k * kt, v, _TN), states, cdecs, krs[mine], ktails, vus[mine])
    ys = tuple(_rms(jnp.concatenate([outs[ci][h] for ci in range(n_chunks)], axis=0)) * jax.nn.silu(gates[h]) for h in range(n_heads))
    return ys, states


def _pick_lane(x, lane_idx):
    lane = lax.broadcasted_iota(jnp.int32, x.shape, 1)
    return jnp.sum(jnp.where(lane == lane_idx, x, 0.0), axis=1, keepdims=True)


def _l2norm(x):
    return x * lax.rsqrt(jnp.sum(x * x, axis=-1, keepdims=True) + EPS)


def _gdn_fn(qcs, kcs, vcs, gates, small, a_log, dt_bias, gain, states):
    c = GDN_CHUNK
    n_heads = len(qcs)
    n_chunks = qcs[0].shape[0] // c
    units = tuple((ci, h) for ci in range(n_chunks) for h in range(n_heads))

    def unit_rows(per_head):
        return tuple(per_head[h][ci * c:(ci + 1) * c] for ci, h in units)

    smalls = tuple(small[ci * c:(ci + 1) * c] for ci, _ in units)
    heads = tuple(h for _, h in units)
    intra = _gdn_intra(unit_rows(qcs), unit_rows(kcs), unit_rows(vcs), smalls, heads, a_log, dt_bias)
    outs = []
    for ci in range(n_chunks):
        mine = slice(ci * n_heads, (ci + 1) * n_heads)
        os_, states = _gdn_inter(*(part[mine] for part in intra), states)
        outs.append(os_)
    ys = tuple(_rms(jnp.concatenate([outs[ci][h] for ci in range(n_chunks)], axis=0)) * gain * jax.nn.silu(gates[h])
               for h in range(n_heads))
    return ys, states


def _gdn_inter(qs, ks, us, ws, attns, gcs, g_lasts, states):
    v_news = _each(lambda u, w, s: u - _bdot(w, s), us, ws, states)
    inter = _each(lambda q, gc, s: _bdot(q * jnp.exp(gc), s), qs, gcs, states)
    os_ = _each(lambda x, a, v: x + _bdot(a, v), inter, attns, v_news)
    new_states = _each(lambda s, gl, k, gc, v: s * jnp.exp(gl) + _bdot(k * jnp.exp(gl - gc), v, _TN), states, g_lasts, ks, gcs, v_news)
    return os_, new_states


def _gdn_intra(qcs, kcs, vcs, smalls, heads, a_log, dt_bias):
    c = GDN_CHUNK
    qs = _each(lambda x: _l2norm(x) * (HEAD ** -0.5), qcs)
    ks = _each(_l2norm, kcs)
    betas = _each(lambda sm, h: jax.nn.sigmoid(_pick_lane(sm, h)), smalls, heads)
    gs = _each(lambda sm, h: -jnp.exp(_pick_lane(a_log, h)) * _softplus(_pick_lane(sm, h + N_HEADS) + _pick_lane(dt_bias, h)),
               smalls, heads)
    i = lax.broadcasted_iota(jnp.int32, (c, c), 0)
    j = lax.broadcasted_iota(jnp.int32, (c, c), 1)
    tril = i >= j
    gcs = _each(lambda g: cumsum_rows(jnp.broadcast_to(g, (c, LANES)))[:, :1], gs)
    gc_rows = _each(lambda gc: jnp.broadcast_to(gc, (c, c)), gcs)
    decays = _each(lambda r: jnp.where(tril, jnp.exp(jnp.where(tril, r - r.T, 0.0)), 0.0), gc_rows)
    kbs = _each(lambda k, b: k * b, ks, betas)
    lmats = _each(lambda kb, k, d: jnp.where(i > j, _bdot(kb, k, _NT) * d, 0.0), kbs, ks, decays)
    attns = _each(lambda q, k, d: jnp.where(tril, _bdot(q, k, _NT) * d, 0.0), qs, ks, decays)
    invs = unit_lower_inverse(lmats)
    us = dot3(invs, _each(lambda v, b: v * b, vcs, betas))
    ws = dot3(invs, _each(lambda kb, gc: kb * jnp.exp(gc), kbs, gcs))
    g_lasts = _each(lambda g: jnp.sum(g, axis=0, keepdims=True), gs)
    return qs, ks, us, ws, attns, gcs, g_lasts


def _final_fn(h, g, target):
    y = _rms(h) * g
    return 0.5 * jnp.sum(jnp.mean(jnp.square(y - target), axis=-1, keepdims=True), axis=0, keepdims=True)


def _tile(n, candidates):
    for t in candidates:
        if n % t == 0:
            return t
    raise ValueError(f"no tile for {n}")


MATMUL_RESIDENT_LHS_BYTES = 8 * 1024 * 1024


def matmul(a, b, *, ta=False, tb=False, add=None, out_dtype=F32, tm=None, tn=None, split=None, column_halves=None, name):
    m = a.shape[1] if ta else a.shape[0]
    k = a.shape[0] if ta else a.shape[1]
    n = b.shape[0] if tb else b.shape[1]
    assert k == (b.shape[1] if tb else b.shape[0])
    out_shape, out_block, out_index = (m, n), None, lambda i, j: (i, j)
    if split is not None:
        dims4, perm = split
        out_shape = tuple(dims4[p] for p in perm)
        r, cols = out_shape[2:]
        tm, tn = m, tn or _tile(cols, (1408, 512))
        cb = cols // tn
        if perm == (0, 2, 1, 3):
            out_block, out_index = (2, None, r, tn), lambda i, j: (0, j // cb, 0, j % cb)
        elif perm == (1, 0, 2, 3):
            out_block, out_index = (2, N_SHARD, r, tn), lambda i, j: (0, 0, 0, j)
        else:
            raise ValueError(perm)
    if tm is None and not ta and m * k * a.dtype.itemsize <= MATMUL_RESIDENT_LHS_BYTES:
        tm = m
    tm = tm or _tile(m, (1024, 512, 1408, 256, 128))
    tn = tn or _tile(n, (512, 1408, 256, 128))
    aliases, prev, keep_rows = {}, None, None
    if column_halves is not None:
        total_rows, first_row, keep_rows, prev = column_halves
        tn = n // 2
        rows_out = keep_rows or tm
        out_shape, out_block = (2, total_rows, tn), (None, rows_out, tn)
        out_index = lambda i, j: (j, first_row // rows_out + i, 0)
    dims = (((0 if ta else 1,), (1 if tb else 0,)), ((), ()))

    def body(a_ref, b_ref, *rest):
        acc = lax.dot_general(a_ref[...].astype(BF16), b_ref[...].astype(BF16), dims, preferred_element_type=F32)
        if add is not None:
            acc = acc + rest[0][...]
        o_ref = rest[-1]
        acc = acc.astype(out_dtype)
        if split is not None and split[1] == (1, 0, 2, 3):
            rows = o_ref.shape[2]
            for s in range(N_SHARD):
                for h in range(2):
                    o_ref[h, s] = acc[(2 * s + h) * rows:(2 * s + h + 1) * rows]
        elif keep_rows is not None:
            o_ref[...] = acc[:keep_rows]
        else:
            o_ref[...] = acc.reshape(o_ref.shape)

    a_spec = pl.BlockSpec((k, tm), lambda i, j: (0, i)) if ta else pl.BlockSpec((tm, k), lambda i, j: (i, 0))
    b_spec = pl.BlockSpec((tn, k), lambda i, j: (j, 0)) if tb else pl.BlockSpec((k, tn), lambda i, j: (0, j))
    o_spec = pl.BlockSpec(out_block or (tm, tn), out_index)
    in_specs, args = [a_spec, b_spec], [a, b]
    if add is not None:
        in_specs.append(o_spec)
        args.append(add)
    if prev is not None:
        aliases = {len(args): 0}
        in_specs.append(pl.BlockSpec(memory_space=pl.ANY))
        args.append(prev)
    return pl.pallas_call(body, out_shape=_sds(out_shape, out_dtype), grid=(m // tm, n // tn), in_specs=in_specs,
                          out_specs=o_spec, input_output_aliases=aliases, compiler_params=_params(), name=name)(*args)


def norm_matmul(x, g, b, *, tb=False, name):
    t, k = x.shape
    n = b.shape[0] if tb else b.shape[1]
    tn = _tile(n, (512, 1408, 256, 128))
    dims = (((1,), (1 if tb else 0,)), ((), ()))

    def body(x_ref, g_ref, b_ref, o_ref, hn_ref):
        @pl.when(pl.program_id(0) == 0)
        def _():
            hn_ref[...] = _norm_fn(x_ref[...], g_ref[...]).astype(BF16)

        o_ref[...] = lax.dot_general(hn_ref[...], b_ref[...].astype(BF16), dims, preferred_element_type=F32)

    b_spec = pl.BlockSpec((tn, k), lambda j: (j, 0)) if tb else pl.BlockSpec((k, tn), lambda j: (0, j))
    whole = pl.BlockSpec((t, k), lambda j: (0, 0))
    return pl.pallas_call(body, out_shape=(_sds((t, n), F32), _sds((t, k), BF16)), grid=(n // tn,),
                          in_specs=[whole, pl.BlockSpec((1, k), lambda j: (0, 0)), b_spec],
                          out_specs=(pl.BlockSpec((t, tn), lambda j: (0, j)), whole), compiler_params=_params(), name=name)(x, g, b)


ROW_TILE = 512


def norm_bwd(x, g, dy, dres, *, name):
    t, d = x.shape

    def body(x_ref, g_ref, dy_ref, dres_ref, dx_ref, dg_ref):
        _, vjp = jax.vjp(_norm_fn, x_ref[...], g_ref[...])
        dx, dg = vjp(dy_ref[...])
        dx_ref[...] = dx + dres_ref[...]

        @pl.when(pl.program_id(0) == 0)
        def _():
            dg_ref[...] = jnp.zeros_like(dg_ref)

        dg_ref[...] += dg

    row = pl.BlockSpec((ROW_TILE, d), lambda i: (i, 0))
    vec = pl.BlockSpec((1, d), lambda i: (0, 0))
    return pl.pallas_call(body, out_shape=(_sds((t, d), F32), _sds((1, d), F32)), grid=(t // ROW_TILE,),
                          in_specs=[row, vec, row, row], out_specs=(row, vec), compiler_params=_params(), name=name)(x, g, dy, dres)


def final_fwd_bwd(h, g, target, *, name):
    t, d = h.shape

    def body(h_ref, g_ref, t_ref, loss_ref, dh_ref, dg_ref):
        tgt = t_ref[...]
        loss, vjp = jax.vjp(lambda hh, gg: _final_fn(hh, gg, tgt), h_ref[...], g_ref[...])
        dh, dg = vjp(jnp.ones((1, 1), F32))
        dh_ref[...] = dh

        @pl.when(pl.program_id(0) == 0)
        def _():
            dg_ref[...] = jnp.zeros_like(dg_ref)
            loss_ref[...] = jnp.zeros_like(loss_ref)

        dg_ref[...] += dg
        loss_ref[...] += jnp.broadcast_to(loss, loss_ref.shape)

    row = pl.BlockSpec((ROW_TILE, d), lambda i: (i, 0))
    vec = pl.BlockSpec((1, d), lambda i: (0, 0))
    return pl.pallas_call(body, out_shape=(_sds((1, LANES), F32), _sds((t, d), F32), _sds((1, d), F32)), grid=(t // ROW_TILE,),
                          in_specs=[row, vec, row], out_specs=(pl.BlockSpec((1, LANES), lambda i: (0, 0)), row, vec),
                          compiler_params=_params(), name=name)(h, g, target)


FFN_FWD_COLS = 256
FFN_BWD_COLS = 128


def ffn_act_fwd(u, cw, cb, *, name):
    t = u.shape[0]
    w = FFN_FWD_COLS
    nb = D_FF // w

    def body(ug_ref, uv_ref, wg_ref, wv_ref, bg_ref, bv_ref, o_ref):
        o_ref[...] = _ffn_act_fn(ug_ref[...], uv_ref[...], wg_ref[...], wv_ref[...], bg_ref[...], bv_ref[...]).astype(BF16)

    def col(rows, off):
        return pl.BlockSpec((rows, w), lambda j: (0, j + off))

    return pl.pallas_call(body, out_shape=_sds((t, D_FF), BF16), grid=(nb,),
                          in_specs=[col(t, 0), col(t, nb), col(3, 0), col(3, nb), col(1, 0), col(1, nb)],
                          out_specs=col(t, 0), compiler_params=_params(), name=name)(u, u, cw, cw, cb, cb)


def _put_column_blocks(step, n_steps, blocks, dst_ref, width, stage_ref, sems):
    def copies(at):
        slot = at % 2
        return [pltpu.make_async_copy(stage_ref.at[slot, p], dst_ref.at[:, pl.ds(pl.multiple_of((p * n_steps + at) * width, LANES), width)],
                                      sems.at[slot, p]) for p in range(len(blocks))]

    @pl.when(step >= 2)
    def _():
        for cp in copies(step - 2):
            cp.wait()

    for p, value in enumerate(blocks):
        stage_ref[step % 2, p] = value
    for cp in copies(step):
        cp.start()

    @pl.when(step == n_steps - 1)
    def _():
        for cp in copies(step - 1) + copies(step):
            cp.wait()


def ffn_act_bwd(u, cw, cb, da, *, name):
    t = u.shape[0]
    w = FFN_BWD_COLS
    nb = D_FF // w

    def body(ug_ref, uv_ref, wg_ref, wv_ref, bg_ref, bv_ref, da_ref, dug_ref, duv_ref, dwg_ref, dwv_ref, dbg_ref, dbv_ref):
        _, vjp = jax.vjp(_ffn_act_fn, ug_ref[...], uv_ref[...], wg_ref[...], wv_ref[...], bg_ref[...], bv_ref[...])
        dug, duv, dwg, dwv, dbg, dbv = vjp(da_ref[...])
        dug_ref[...] = dug.astype(BF16)
        duv_ref[...] = duv.astype(BF16)
        dwg_ref[...] = dwg
        dwv_ref[...] = dwv
        dbg_ref[...] = dbg
        dbv_ref[...] = dbv

    def col(rows, off):
        return pl.BlockSpec((rows, w), lambda j: (0, j + off))

    outs = pl.pallas_call(
        body, out_shape=(_sds((t, D_FF), BF16), _sds((t, D_FF), BF16), _sds((3, D_FF), F32), _sds((3, D_FF), F32),
                         _sds((1, D_FF), F32), _sds((1, D_FF), F32)),
        grid=(nb,), in_specs=[col(t, 0), col(t, nb), col(3, 0), col(3, nb), col(1, 0), col(1, nb), col(t, 0)],
        out_specs=(col(t, 0), col(t, 0), col(3, 0), col(3, 0), col(1, 0), col(1, 0)), compiler_params=_params(), name=name,
    )(u, u, cw, cw, cb, cb, da)
    dug, duv, dwg, dwv, dbg, dbv = outs
    return jnp.concatenate([dug, duv], axis=1), jnp.concatenate([dwg, dwv], axis=1), jnp.concatenate([dbg, dbv], axis=1)


GDN_CONV_COLS = 256
GDN_CONV_OFF = 4 * GROUP


def gdn_conv_fwd(p, cw, *, name):
    t = p.shape[0]
    w = GDN_CONV_COLS
    nb = 3 * GROUP // w
    off = GDN_CONV_OFF // w

    def body(x_ref, w_ref, o_ref):
        o_ref[...] = _gdn_conv_fn(x_ref[...], w_ref[...])

    return pl.pallas_call(body, out_shape=_sds((t, 3 * GROUP), F32), grid=(nb,),
                          in_specs=[pl.BlockSpec((t, w), lambda j: (0, j + off)), pl.BlockSpec((4, w), lambda j: (0, j))],
                          out_specs=pl.BlockSpec((t, w), lambda j: (0, j)), compiler_params=_params(), name=name)(p, cw)


def gdn_conv_bwd(p, cw, dc, *, name):
    t = p.shape[0]
    w = GDN_CONV_COLS
    nb = 3 * GROUP // w
    off = GDN_CONV_OFF // w

    def body(x_ref, w_ref, dc_ref, dx_ref, dw_ref):
        _, vjp = jax.vjp(_gdn_conv_fn, x_ref[...], w_ref[...])
        dx, dw = vjp(dc_ref[...])
        dx_ref[...] = dx.astype(BF16)
        dw_ref[...] = dw

    blk = pl.BlockSpec((t, w), lambda j: (0, j))
    wblk = pl.BlockSpec((4, w), lambda j: (0, j))
    return pl.pallas_call(body, out_shape=(_sds((t, 3 * GROUP), BF16), _sds((4, 3 * GROUP), F32)), grid=(nb,),
                          in_specs=[pl.BlockSpec((t, w), lambda j: (0, j + off)), wblk, blk], out_specs=(blk, wblk),
                          compiler_params=_params(), name=name)(p, cw, dc)


def _lru_specs(t):
    w = D_MODEL // LRU_BLOCKS
    gate = pl.BlockSpec((t, w), lambda j: (0, j))
    xin = pl.BlockSpec((t, w), lambda j: (0, j + LRU_BLOCKS))
    cw = pl.BlockSpec((4, w), lambda j: (0, j))
    vec = pl.BlockSpec((1, w), lambda j: (0, j))
    mat = pl.BlockSpec((None, w, w), lambda j: (j, 0, 0))
    return gate, xin, cw, vec, mat


def lru_fwd(gx, cw, cb, wa, ba, wx, bx, lam, *, name):
    t = gx.shape[0]
    gate, xin, cws, vec, mat = _lru_specs(t)

    def body(g_ref, x_ref, cw_ref, cb_ref, wa_ref, ba_ref, wx_ref, bx_ref, lam_ref, o_ref):
        o_ref[...] = _lru_fn(g_ref[...], x_ref[...], cw_ref[...], cb_ref[...], wa_ref[...], ba_ref[...], wx_ref[...],
                             bx_ref[...], lam_ref[...]).astype(BF16)

    return pl.pallas_call(body, out_shape=_sds((t, D_MODEL), BF16), grid=(LRU_BLOCKS,),
                          in_specs=[gate, xin, cws, vec, mat, vec, mat, vec, vec], out_specs=gate,
                          compiler_params=_params(), name=name)(gx, gx, cw, cb, wa, ba, wx, bx, lam)


def lru_bwd(gx, cw, cb, wa, ba, wx, bx, lam, dy, *, name):
    t = gx.shape[0]
    gate, xin, cws, vec, mat = _lru_specs(t)

    def body(g_ref, x_ref, cw_ref, cb_ref, wa_ref, ba_ref, wx_ref, bx_ref, lam_ref, dy_ref,
             dgx_ref, dcw_ref, dcb_ref, dwa_ref, dba_ref, dwx_ref, dbx_ref, dlam_ref, stage_ref, sems):
        _, vjp = jax.vjp(_lru_fn, g_ref[...], x_ref[...], cw_ref[...], cb_ref[...], wa_ref[...], ba_ref[...], wx_ref[...],
                         bx_ref[...], lam_ref[...])
        dg, dx, dcw, dcb, dwa, dba, dwx, dbx, dlam = vjp(dy_ref[...])
        _put_column_blocks(pl.program_id(0), LRU_BLOCKS, (dg.astype(BF16), dx.astype(BF16)), dgx_ref, D_MODEL // LRU_BLOCKS, stage_ref, sems)
        dcw_ref[...] = dcw
        dcb_ref[...] = dcb
        dwa_ref[...] = dwa
        dba_ref[...] = dba
        dwx_ref[...] = dwx
        dbx_ref[...] = dbx
        dlam_ref[...] = dlam

    d = D_MODEL
    w = d // LRU_BLOCKS
    out_shape = (_sds((t, 2 * d), BF16), _sds((4, d), F32), _sds((1, d), F32), _sds((LRU_BLOCKS, w, w), F32),
                 _sds((1, d), F32), _sds((LRU_BLOCKS, w, w), F32), _sds((1, d), F32), _sds((1, d), F32))
    return pl.pallas_call(body, out_shape=out_shape, grid=(LRU_BLOCKS,),
                          in_specs=[gate, xin, cws, vec, mat, vec, mat, vec, vec, gate],
                          out_specs=(pl.BlockSpec(memory_space=pl.ANY), cws, vec, mat, vec, mat, vec, vec),
                          scratch_shapes=[pltpu.VMEM((2, 2, t, w), BF16), pltpu.SemaphoreType.DMA((2, 2))],
                          compiler_params=_params(), name=name)(gx, gx, cw, cb, wa, ba, wx, bx, lam, dy)


def _ret_tables():
    half = HEAD // 2
    inv_freq = (np.float32(ROPE_BASE) ** (-np.arange(half, dtype=np.float32) / np.float32(half))).astype(np.float32)
    ang = (np.arange(SEQ, dtype=np.float32)[:, None] * inv_freq[None, :]).astype(np.float64)
    cos2 = np.concatenate([np.cos(ang), np.cos(ang)], axis=1).astype(np.float32)
    sin2 = np.concatenate([-np.sin(ang), np.sin(ang)], axis=1).astype(np.float32)
    c = RET_CHUNK
    log_gamma = np.log1p(-np.exp2(-5.0 - np.arange(N_HEADS, dtype=np.float64)))
    idx = np.arange(c, dtype=np.float64)
    rel = idx[:, None] - idx[None, :]
    dmask = np.where(rel >= 0, np.exp(log_gamma[:, None, None] * np.maximum(rel, 0.0)), 0.0)
    ones = np.ones((N_HEADS, c, HEAD))
    ktail = np.exp(log_gamma[:, None] * (c - 1 - idx))[:, :, None] * ones
    qdec = np.exp(log_gamma[:, None] * (idx + 1.0))[:, :, None] * ones
    cdec = np.exp(log_gamma * c)[:, None, None] * ones
    return tuple(jnp.asarray(a, F32) for a in (cos2, sin2, dmask, ktail, qdec, cdec))


def _ret_specs(rev):
    c = RET_CHUNK * RET_CHUNKS_PER_STEP
    nc = SEQ // c

    def n_of(n):
        return nc - 1 - n if rev else n

    def group(off):
        return pl.BlockSpec((c, GROUP), lambda n: (n_of(n), off))

    tab = pl.BlockSpec((c, HEAD), lambda n: (n_of(n), 0))
    const = pl.BlockSpec((N_HEADS, RET_CHUNK, HEAD), lambda n: (0, 0, 0))
    state = pl.BlockSpec((N_HEADS, None, HEAD, HEAD), lambda n: (0, n_of(n), 0, 0))
    return group, tab, const, state, nc


def _head(h):
    return slice(h * HEAD, (h + 1) * HEAD)


def ret_fwd(p, tables, *, name):
    group, tab, const, state, nc = _ret_specs(False)

    def body(q_ref, k_ref, v_ref, g_ref, cos_ref, sin_ref, dm_ref, kt_ref, qd_ref, cd_ref, y_ref, st_ref, s_scr):
        @pl.when(pl.program_id(0) == 0)
        def _():
            s_scr[...] = jnp.zeros_like(s_scr)

        heads = range(N_HEADS)
        states = tuple(s_scr[h] for h in heads)
        ys, new_states = _ret_fn(*(tuple(r[:, _head(h)] for h in heads) for r in (q_ref, k_ref, v_ref, g_ref)), states,
                                 cos_ref[...], sin_ref[...], *(tuple(r[h] for h in heads) for r in (dm_ref, kt_ref, qd_ref, cd_ref)))
        for h in heads:
            st_ref[h] = states[h]
            y_ref[:, _head(h)] = ys[h].astype(BF16)
            s_scr[h] = new_states[h]

    return pl.pallas_call(
        body, out_shape=(_sds((SEQ, 2 * GROUP), BF16), _sds((N_HEADS, nc, HEAD, HEAD), F32)), grid=(nc,),
        in_specs=[group(0), group(1), group(2), group(3), tab, tab, const, const, const, const],
        out_specs=(group(0), state), scratch_shapes=[pltpu.VMEM((N_HEADS, HEAD, HEAD), F32)], compiler_params=_params(), name=name,
    )(p, p, p, p, *tables)


def ret_bwd(p, tables, states, dy, *, name):
    group, tab, const, state, nc = _ret_specs(True)

    def body(q_ref, k_ref, v_ref, g_ref, cos_ref, sin_ref, dm_ref, kt_ref, qd_ref, cd_ref, st_ref, dy_ref,
             dq_ref, dk_ref, dv_ref, dg_ref, ds_scr):
        @pl.when(pl.program_id(0) == 0)
        def _():
            ds_scr[...] = jnp.zeros_like(ds_scr)

        heads = range(N_HEADS)
        consts = (cos_ref[...], sin_ref[...], *(tuple(r[h] for h in heads) for r in (dm_ref, kt_ref, qd_ref, cd_ref)))
        _, vjp = jax.vjp(lambda *a: _ret_fn(*a, *consts), *(tuple(r[:, _head(h)] for h in heads) for r in (q_ref, k_ref, v_ref, g_ref)),
                         tuple(st_ref[h] for h in heads))
        dqs, dks, dvs, dgs, dss = vjp((tuple(dy_ref[:, _head(h)] for h in heads), tuple(ds_scr[h] for h in heads)))
        for h in heads:
            dq_ref[:, _head(h)] = dqs[h].astype(BF16)
            dk_ref[:, _head(h)] = dks[h].astype(BF16)
            dv_ref[:, _head(h)] = dvs[h].astype(BF16)
            dg_ref[:, _head(h)] = dgs[h].astype(BF16)
            ds_scr[h] = dss[h]

    out = _sds((SEQ, GROUP), BF16)
    return pl.pallas_call(
        body, out_shape=(out, out, out, out), grid=(nc,),
        in_specs=[group(0), group(1), group(2), group(3), tab, tab, const, const, const, const, state, group(0)],
        out_specs=(group(0), group(0), group(0), group(0)), scratch_shapes=[pltpu.VMEM((N_HEADS, HEAD, HEAD), F32)],
        compiler_params=_params(), name=name,
    )(p, p, p, p, *tables, states, dy)


def _gdn_specs(rev):
    c = GDN_CHUNK * GDN_CHUNKS_PER_STEP
    nc = SEQ // c

    def n_of(n):
        return nc - 1 - n if rev else n

    def group(off):
        return pl.BlockSpec((c, GROUP), lambda n: (n_of(n), off))

    small = pl.BlockSpec((c, LANES), lambda n: (n_of(n), 0))
    vec = pl.BlockSpec((1, LANES), lambda n: (0, 0))
    state = pl.BlockSpec((N_HEADS, None, HEAD, HEAD), lambda n: (0, n_of(n), 0, 0))
    qkv = pl.BlockSpec((c, 3 * GROUP), lambda n: (n_of(n), 0))
    return group, small, vec, state, qkv, nc


GDN_GATE_GROUP = 7


def gdn_fwd(conv, p, small, a_log, dt_bias, gain, y_started, *, name):
    group, sm, vec, state, _, nc = _gdn_specs(False)

    def body(q_ref, k_ref, v_ref, g_ref, sm_ref, al_ref, dt_ref, gn_ref, _, y_ref, st_ref, s_scr):
        @pl.when(pl.program_id(0) == 0)
        def _():
            s_scr[...] = jnp.zeros_like(s_scr)

        states = tuple(s_scr[h] for h in range(N_HEADS))
        ys, new_states = _gdn_fn(*(tuple(r[:, _head(h)] for h in range(N_HEADS)) for r in (q_ref, k_ref, v_ref, g_ref)),
                                 sm_ref[...], al_ref[...], dt_ref[...], gn_ref[...], states)
        for h in range(N_HEADS):
            st_ref[h] = states[h]
            y_ref[:, _head(h)] = ys[h].astype(BF16)
            s_scr[h] = new_states[h]

    return pl.pallas_call(
        body, out_shape=(_sds((SEQ, 2 * GROUP), BF16), _sds((N_HEADS, nc, HEAD, HEAD), F32)), grid=(nc,),
        in_specs=[group(0), group(1), group(2), group(GDN_GATE_GROUP), sm, vec, vec, vec, pl.BlockSpec(memory_space=pl.ANY)],
        out_specs=(group(1), state), input_output_aliases={8: 0},
        scratch_shapes=[pltpu.VMEM((N_HEADS, HEAD, HEAD), F32)], compiler_params=_params(), name=name,
    )(conv, conv, conv, p, small, a_log, dt_bias, gain, y_started)


def gdn_bwd(conv, p, small, a_log, dt_bias, gain, states, dy, *, name):
    group, sm, vec, state, qkv, nc = _gdn_specs(True)

    def body(q_ref, k_ref, v_ref, g_ref, sm_ref, al_ref, dt_ref, gn_ref, st_ref, dy_ref,
             dqkv_ref, dg_ref, dsm_ref, dal_ref, ddt_ref, dgn_ref, ds_scr):
        @pl.when(pl.program_id(0) == 0)
        def _():
            ds_scr[...] = jnp.zeros_like(ds_scr)
            dal_ref[...] = jnp.zeros_like(dal_ref)
            ddt_ref[...] = jnp.zeros_like(ddt_ref)
            dgn_ref[...] = jnp.zeros_like(dgn_ref)

        per_head = tuple(tuple(r[:, _head(h)] for h in range(N_HEADS)) for r in (q_ref, k_ref, v_ref, g_ref))
        _, vjp = jax.vjp(_gdn_fn, *per_head, sm_ref[...], al_ref[...], dt_ref[...], gn_ref[...],
                         tuple(st_ref[h] for h in range(N_HEADS)))
        cts = (tuple(dy_ref[:, _head(h)] for h in range(N_HEADS)), tuple(ds_scr[h] for h in range(N_HEADS)))
        dqs, dks, dvs, dgs, dsm, dal, ddt, dgn, dss = vjp(cts)
        for h in range(N_HEADS):
            for part, blocks in enumerate((dqs, dks, dvs)):
                dqkv_ref[:, part * GROUP + h * HEAD:part * GROUP + (h + 1) * HEAD] = blocks[h]
            dg_ref[:, _head(h)] = dgs[h].astype(BF16)
            ds_scr[h] = dss[h]
        dsm_ref[...] = dsm
        dal_ref[...] += dal
        ddt_ref[...] += ddt
        dgn_ref[...] += dgn

    pv = _sds((1, LANES), F32)
    return pl.pallas_call(
        body, out_shape=(_sds((SEQ, 3 * GROUP), F32), _sds((SEQ, GROUP), BF16), _sds((SEQ, LANES), F32), pv, pv, pv), grid=(nc,),
        in_specs=[group(0), group(1), group(2), group(GDN_GATE_GROUP), sm, vec, vec, vec, state, group(1)],
        out_specs=(qkv, group(0), sm, vec, vec, vec), scratch_shapes=[pltpu.VMEM((N_HEADS, HEAD, HEAD), F32)],
        compiler_params=_params(), name=name,
    )(conv, conv, conv, p, small, a_log, dt_bias, gain, states, dy)


ELEMENTWISE_BLOCK_BYTES = 2 * 1024 * 1024


def _row_tile(r, c):
    best = None
    for tr in range(8, r + 1, 8):
        if r % tr == 0 and tr * c * 4 <= ELEMENTWISE_BLOCK_BYTES:
            best = tr
    if best is None:
        raise ValueError(f"no row tile for ({r}, {c})")
    return best


def _tile_2d(r, c):
    if any(r % tr == 0 for tr in range(8, r + 1, 8)):
        return _row_tile(r, c), c
    tc = max(t for t in range(LANES, c + 1, LANES) if c % t == 0 and r * t * 4 <= ELEMENTWISE_BLOCK_BYTES)
    return r, tc


def _core_index():
    return lax.axis_index("c").astype(jnp.int32).reshape(1)


def _chip_index():
    return (2 * lax.axis_index("x") + lax.axis_index("y")).astype(jnp.int32).reshape(1)


def adamw_halves(w, m, v, g_own, g_sib, *, layer=0, prev=None, name):
    n_layers, rows, c = w.shape
    r = rows // 2
    tr = _row_tile(r, c)
    nb = r // tr

    def body(c_ref, w_ref, m_ref, v_ref, own_ref, sib_ref, *rest):
        g_ref, d_ref, nm_ref, nv_ref = rest[-4:]
        gg = jnp.where(pl.program_id(0) == c_ref[0], own_ref[...], sib_ref[...])
        nm = ADAM_B1 * m_ref[...] + (1.0 - ADAM_B1) * gg
        nv = ADAM_B2 * v_ref[...] + (1.0 - ADAM_B2) * jnp.square(gg)
        m_hat = nm / (1.0 - ADAM_B1 ** ADAM_STEP)
        v_hat = nv / (1.0 - ADAM_B2 ** ADAM_STEP)
        g_ref[...] = gg
        d_ref[...] = -ADAM_LR * (m_hat / (jnp.sqrt(v_hat) + ADAM_EPS) + ADAM_WD * w_ref[...])
        nm_ref[...] = nm
        nv_ref[...] = nv

    full = pl.BlockSpec((None, tr, c), lambda h, i, cr: (layer, h * nb + i, 0))
    half = pl.BlockSpec((tr, c), lambda h, i, cr: (i, 0))
    o = _sds((n_layers, rows, c), F32)
    prev = list(prev or ())
    gs = pltpu.PrefetchScalarGridSpec(num_scalar_prefetch=1, grid=(2, nb), in_specs=[full, full, full, half, half] + [_ANY] * len(prev),
                                      out_specs=(full, full, full, full))
    n_fixed = 6
    return pl.pallas_call(body, out_shape=(o, o, o, o), grid_spec=gs, compiler_params=_params(), name=name,
                          input_output_aliases={n_fixed + k: k for k in range(len(prev))})(
        _core_index(), w, m, v, g_own, g_sib, *prev)


ADAMW_ROW_STEPS = 6


def adamw_rows(w, g, m, v, *, name):
    rows, _, cols = w.shape
    tr = rows // ADAMW_ROW_STEPS

    def body(w_ref, g_ref, m_ref, v_ref, g_out_ref, d_ref, nm_ref, nv_ref):
        gg = g_ref[...]
        nm = ADAM_B1 * m_ref[...] + (1.0 - ADAM_B1) * gg
        nv = ADAM_B2 * v_ref[...] + (1.0 - ADAM_B2) * jnp.square(gg)
        m_hat = nm / (1.0 - ADAM_B1 ** ADAM_STEP)
        v_hat = nv / (1.0 - ADAM_B2 ** ADAM_STEP)
        g_out_ref[...] = gg
        d_ref[...] = -ADAM_LR * (m_hat / (jnp.sqrt(v_hat) + ADAM_EPS) + ADAM_WD * w_ref[...])
        nm_ref[...] = nm
        nv_ref[...] = nv

    blk = pl.BlockSpec((tr, 1, cols), lambda i: (i, 0, 0))
    o = _sds(w.shape, F32)
    return pl.pallas_call(body, out_shape=(o, o, o, o), grid=(ADAMW_ROW_STEPS,), in_specs=[blk] * 4, out_specs=(blk, blk, blk, blk),
                          compiler_params=_params(), name=name)(w, g, m, v)


def adamw_many(ws, gs, ms, vs, *, name):
    n = len(ws)

    def body(*refs):
        w_refs, g_refs, m_refs, v_refs, d_refs, nm_refs, nv_refs = (refs[k * n:(k + 1) * n] for k in range(7))
        for i in range(n):
            gg = g_refs[i][...]
            nm = ADAM_B1 * m_refs[i][...] + (1.0 - ADAM_B1) * gg
            nv = ADAM_B2 * v_refs[i][...] + (1.0 - ADAM_B2) * jnp.square(gg)
            m_hat = nm / (1.0 - ADAM_B1 ** ADAM_STEP)
            v_hat = nv / (1.0 - ADAM_B2 ** ADAM_STEP)
            d_refs[i][...] = -ADAM_LR * (m_hat / (jnp.sqrt(v_hat) + ADAM_EPS) + ADAM_WD * w_refs[i][...])
            nm_refs[i][...] = nm
            nv_refs[i][...] = nv

    outs = pl.pallas_call(body, out_shape=[_sds(w.shape, F32) for w in ws] * 3, compiler_params=_params(), name=name)(*ws, *gs, *ms, *vs)
    return outs[:n], outs[n:2 * n], outs[2 * n:]


def add_core_halves(g2, land, *, out_dtype, name):
    _, ns, r, cols = g2.shape
    tr, tc = _tile_2d(r, cols)

    def body(c_ref, a_ref, b_ref, o_ref):
        o_ref[...] = (a_ref[...] + b_ref[...]).astype(out_dtype)

    gs = pltpu.PrefetchScalarGridSpec(
        num_scalar_prefetch=1, grid=(ns, r // tr, cols // tc),
        in_specs=[pl.BlockSpec((None, None, tr, tc), lambda s, i, j, cr: (cr[0], s, i, j)),
                  pl.BlockSpec((None, tr, tc), lambda s, i, j, cr: (s, i, j))],
        out_specs=pl.BlockSpec((None, tr, tc), lambda s, i, j, cr: (s, i, j)))
    return pl.pallas_call(body, out_shape=_sds((ns, r, cols), out_dtype), grid_spec=gs, compiler_params=_params(), name=name)(
        _core_index(), g2, land)


def sum_over_chips(own, land, *, scatter, name):
    _, r, cols = own.shape
    tr, tc = _tile_2d(r, cols)

    def body(mine_ref, own_ref, l0, l1, l2, l3, o_ref):
        mine = mine_ref[0]
        mine_val = own_ref[...]
        acc = None
        for s, l_ref in enumerate((l0, l1, l2, l3)):
            val = jnp.where(mine == s, mine_val, l_ref[...]).astype(F32)
            acc = val if acc is None else acc + val
        o_ref[...] = acc

    def slot(s):
        return pl.BlockSpec((None, tr, tc), lambda i, j, mr: (jnp.where(mr[0] == s, (s + 1) % N_SHARD, s), i, j))

    own_spec = pl.BlockSpec((None, tr, tc), lambda i, j, mr: (mr[0] if scatter else 0, i, j))
    gs = pltpu.PrefetchScalarGridSpec(num_scalar_prefetch=1, grid=(r // tr, cols // tc), in_specs=[own_spec] + [slot(s) for s in range(N_SHARD)],
                                      out_specs=pl.BlockSpec((tr, tc), lambda i, j, mr: (i, j)))
    return pl.pallas_call(body, out_shape=_sds((r, cols), F32), grid_spec=gs, compiler_params=_params(), name=name)(
        _chip_index(), own, land, land, land, land)


_ANY = pl.BlockSpec(memory_space=pl.ANY)


def xy_exchange(src, *, name):
    rh = src.shape[1]

    def body(src_ref, land_ref, send_sems, recv_sems, loc_sem):
        x, y, c = lax.axis_index("x"), lax.axis_index("y"), lax.axis_index("c")
        mine = 2 * x + y
        peers = [(1 - x, y), (x, 1 - y), (1 - x, 1 - y)]

        def copy(k, px, py, dst_slot):
            return pltpu.make_async_remote_copy(src_ref=src_ref.at[c], dst_ref=land_ref.at[dst_slot], send_sem=send_sems.at[k],
                                                recv_sem=recv_sems.at[k], device_id=(px, py, c), device_id_type=MESH)

        keep = pltpu.make_async_copy(src_ref.at[c], land_ref.at[mine], loc_sem)
        keep.start()
        sends = [copy(k, px, py, mine) for k, (px, py) in enumerate(peers)]
        for cp in sends:
            cp.start()
        for cp in sends:
            cp.wait_send()
        for k, (px, py) in enumerate(peers):
            copy(k, px, py, 2 * px + py).wait_recv()
        keep.wait()

    return pl.pallas_call(body, out_shape=_sds((N_SHARD, rh, LANES), src.dtype), in_specs=[_ANY], out_specs=_ANY,
                          scratch_shapes=[pltpu.SemaphoreType.DMA((3,)), pltpu.SemaphoreType.DMA((3,)), pltpu.SemaphoreType.DMA(())],
                          name=name)(src)


def core_exchange(src, *, name):
    def body(src_ref, out_ref, send_sem, recv_sem, loc_sem):
        x, y, c = lax.axis_index("x"), lax.axis_index("y"), lax.axis_index("c")
        keep = pltpu.make_async_copy(src_ref, out_ref.at[c], loc_sem)
        keep.start()
        cp = pltpu.make_async_remote_copy(src_ref=src_ref, dst_ref=out_ref.at[c], send_sem=send_sem, recv_sem=recv_sem,
                                          device_id=(x, y, 1 - c), device_id_type=MESH)
        cp.start()
        cp.wait_send()
        pltpu.make_async_remote_copy(src_ref=src_ref, dst_ref=out_ref.at[1 - c], send_sem=send_sem, recv_sem=recv_sem,
                                     device_id=(x, y, 1 - c), device_id_type=MESH).wait_recv()
        keep.wait()

    return pl.pallas_call(body, out_shape=_sds((2,) + src.shape, src.dtype), in_specs=[_ANY], out_specs=_ANY,
                          scratch_shapes=[pltpu.SemaphoreType.DMA(()), pltpu.SemaphoreType.DMA(()), pltpu.SemaphoreType.DMA(())],
                          name=name)(src)


def _sequencer_call(body, ins, out_shapes, sem_counts, name, collective_id):
    return pl.kernel(body, out_type=list(out_shapes), mesh=plsc.ScalarSubcoreMesh(axis_name="sequencer", num_cores=1), name=name,
                     scratch_types=[pltpu.SemaphoreType.DMA((k,)) for k in sem_counts],
                     compiler_params=pltpu.CompilerParams(collective_id=collective_id))(*ins)


def _handshake(peers):
    barrier = pltpu.get_barrier_semaphore()
    for peer in peers:
        pl.semaphore_signal(barrier, inc=1, device_id=peer, device_id_type=MESH)
    pl.semaphore_wait(barrier, len(peers))


def _xy_peers(x, y):
    return [(1 - x, y), (x, 1 - y), (1 - x, 1 - y)]


def gather_halves(halves, *, name, collective_id):
    n = len(halves)

    def body(*refs):
        ins, lands, sibs = refs[:n], refs[n:2 * n], refs[2 * n:3 * n]
        ici_send, ici_recv, d2d_send, d2d_recv = refs[3 * n:]
        x, y, c = lax.axis_index("x"), lax.axis_index("y"), lax.axis_index("c")
        mine = 2 * x + y
        peers = _xy_peers(x, y)
        _handshake([(px, py, c) for px, py in peers] + [(x, y, 1 - c)])

        def ici(i, k, slot):
            px, py = peers[k]
            return pltpu.make_async_remote_copy(src_ref=ins[i].at[c], dst_ref=lands[i].at[slot], send_sem=ici_send.at[3 * i + k],
                                                recv_sem=ici_recv.at[3 * i + k], device_id=(px, py, c), device_id_type=MESH)

        def pass_on(i, k):
            px, py = peers[k]
            slot = 2 * px + py
            return pltpu.make_async_remote_copy(src_ref=lands[i].at[slot], dst_ref=sibs[i].at[slot], send_sem=d2d_send.at[3 * i + k],
                                                recv_sem=d2d_recv.at[3 * i + k], device_id=(x, y, 1 - c), device_id_type=MESH)

        sends = [ici(i, k, mine) for i in range(n) for k in range(3)]
        for cp in sends:
            cp.start()
        passed = []
        for i in range(n):
            for k in range(3):
                px, py = peers[k]
                ici(i, k, 2 * px + py).wait_recv()
                cp = pass_on(i, k)
                cp.start()
                passed.append(cp)
        for cp in passed:
            cp.wait_recv()
        for cp in sends + passed:
            cp.wait_send()

    outs = [_sds((N_SHARD,) + h.shape[1:], h.dtype) for h in halves]
    res = _sequencer_call(body, halves, outs + outs, [3 * n] * 4, name, collective_id)
    return res[:n], res[n:]


def send_other_half(arrays, *, name, collective_id):
    n = len(arrays)

    def body(*refs):
        ins, lands = refs[:n], refs[n:2 * n]
        send_sems, recv_sems = refs[2 * n:]
        x, y, c = lax.axis_index("x"), lax.axis_index("y"), lax.axis_index("c")
        _handshake([(x, y, 1 - c)])
        copies = [pltpu.make_async_remote_copy(src_ref=ins[i].at[1 - c], dst_ref=lands[i], send_sem=send_sems.at[i],
                                               recv_sem=recv_sems.at[i], device_id=(x, y, 1 - c), device_id_type=MESH) for i in range(n)]
        for cp in copies:
            cp.start()
        for cp in copies:
            cp.wait_recv()
        for cp in copies:
            cp.wait_send()

    return _sequencer_call(body, arrays, [_sds(a.shape[1:], a.dtype) for a in arrays], [n, n], name, collective_id)


_HBM = pl.BlockSpec(memory_space=pltpu.HBM)
_SEM = pl.BlockSpec(memory_space=pltpu.SEMAPHORE)
_SPLIT_COPY = dict(has_side_effects=pltpu.SideEffectType.DATAFLOW_SIDE_EFFECTING)


def _chip_copy(ins, lands, send_sems, recv_sems, scatter, i, k, receive):
    x, y, c = lax.axis_index("x"), lax.axis_index("y"), lax.axis_index("c")
    px, py = _xy_peers(x, y)[k]
    theirs, mine = 2 * px + py, 2 * x + y
    src = ins[i].at[theirs] if scatter[i] else ins[i].at[0]
    return pltpu.make_async_remote_copy(src_ref=src, dst_ref=lands[i].at[theirs if receive else mine], send_sem=send_sems.at[3 * i + k],
                                        recv_sem=recv_sems.at[3 * i + k], device_id=(px, py, c), device_id_type=MESH)


def send_to_chips_start(arrays, scatter, *, name):
    n = len(arrays)

    def body(*refs):
        send_sems, recv_sems = refs[2 * n], refs[2 * n + 1]
        ins, lands = refs[2 * n + 2:3 * n + 2], refs[3 * n + 2:4 * n + 2]
        token = refs[4 * n + 2]
        for i in range(n):
            for k in range(3):
                _chip_copy(ins, lands, send_sems, recv_sems, scatter, i, k, receive=False).start()
        token[...] = jnp.zeros_like(token)

    land_shapes = [(N_SHARD,) + a.shape[1:] for a in arrays]
    operands = [pltpu.with_memory_space_constraint(a, pltpu.HBM) for a in arrays]
    operands += [pltpu.with_memory_space_constraint(lax.empty(s, a.dtype), pltpu.HBM) for s, a in zip(land_shapes, arrays)]
    out_shape = ([pltpu.SemaphoreType.DMA((3 * n,)), pltpu.SemaphoreType.DMA((3 * n,))] + [pltpu.HBM(a.shape, a.dtype) for a in arrays]
                 + [pltpu.HBM(s, a.dtype) for s, a in zip(land_shapes, arrays)] + [_sds((8, LANES), F32)])
    res = pl.pallas_call(body, name=name, out_shape=out_shape, in_specs=[_HBM] * (2 * n),
                         out_specs=[_SEM, _SEM] + [_HBM] * (2 * n) + [pl.BlockSpec(memory_space=pltpu.VMEM)],
                         input_output_aliases={i: 2 + i for i in range(2 * n)}, compiler_params=pltpu.CompilerParams(**_SPLIT_COPY))(*operands)
    return (res[0], res[1], res[2:2 + n], res[2 + n:2 + 2 * n], scatter), res[-1]


def send_to_chips_wait(state, after, *, name):
    send_sems, recv_sems, arrays, lands, scatter = state
    n = len(arrays)

    def body(*refs):
        ins, landing = refs[:n], refs[n:2 * n]
        send_sems, recv_sems = refs[2 * n], refs[2 * n + 1]
        for i in range(n):
            for k in range(3):
                _chip_copy(ins, landing, send_sems, recv_sems, scatter, i, k, receive=True).wait_recv()
        for i in range(n):
            for k in range(3):
                _chip_copy(ins, landing, send_sems, recv_sems, scatter, i, k, receive=False).wait_send()

    out_shape = [pltpu.HBM(a.shape, a.dtype) for a in list(arrays) + list(lands)]
    res = pl.pallas_call(body, name=name, out_shape=out_shape, in_specs=[_HBM] * (2 * n) + [_SEM, _SEM] + [_ANY] * len(after),
                         out_specs=[_HBM] * (2 * n), input_output_aliases={i: i for i in range(2 * n)},
                         compiler_params=pltpu.CompilerParams(**_SPLIT_COPY))(*arrays, *lands, send_sems, recv_sems, *after)
    return res[:n], res[n:]


def swap_with_other_core(arrays, *, name, collective_id):
    n = len(arrays)

    def body(*refs):
        ins, lands = refs[:n], refs[n:2 * n]
        send_sems, recv_sems = refs[2 * n:]
        x, y, c = lax.axis_index("x"), lax.axis_index("y"), lax.axis_index("c")
        _handshake([(x, y, 1 - c)])
        copies = [pltpu.make_async_remote_copy(src_ref=ins[i], dst_ref=lands[i], send_sem=send_sems.at[i], recv_sem=recv_sems.at[i],
                                               device_id=(x, y, 1 - c), device_id_type=MESH) for i in range(n)]
        for cp in copies:
            cp.start()
        for cp in copies:
            cp.wait_recv()
        for cp in copies:
            cp.wait_send()

    return _sequencer_call(body, arrays, [_sds(a.shape, a.dtype) for a in arrays], [n, n], name, collective_id)


def _pack_rows(n_elems, row_multiple):
    rows = -(-n_elems // LANES)
    return -(-rows // row_multiple) * row_multiple


def _pack(arrays, rows, dtype):
    flat = jnp.concatenate([a.reshape(-1).astype(dtype) for a in arrays])
    return jnp.pad(flat, (0, rows * LANES - flat.shape[0])).reshape(rows, LANES)


def _unpack(packed, shapes):
    flat = packed.reshape(-1)
    out, off = [], 0
    for s in shapes:
        n = int(np.prod(s))
        out.append(flat[off:off + n].reshape(s))
        off += n
    return out


def all_gather_shards(shards, axes, dtype, row_multiple, tag):
    shapes = [s.shape for s in shards]
    rows = _pack_rows(sum(int(np.prod(s)) for s in shapes), row_multiple)
    packed = _pack(shards, rows, dtype).reshape(2, rows // 2, LANES)
    land = xy_exchange(packed, name=f"gather_xy_{tag}")
    both = core_exchange(land, name=f"gather_c_{tag}")
    per_shard = jnp.swapaxes(both, 0, 1).reshape(N_SHARD, rows, LANES)
    pieces = [_unpack(per_shard[s], shapes) for s in range(N_SHARD)]
    return [jnp.concatenate([pieces[s][i] for s in range(N_SHARD)], axis=ax) for i, ax in enumerate(axes)]


def _ordered_before(first, then):
    if then is None:
        return first, None
    return lax.optimization_barrier((first, then))


def reduce_between_cores(arrays, scatter, *, tag, collective_id, before=None):
    arrays, before = _ordered_before(arrays, before)
    land = send_other_half(arrays, name=f"reduce_core_send_{tag}", collective_id=collective_id)
    return (arrays, land, scatter, tag, collective_id), before


def reduce_between_chips(state, before=None):
    arrays, land, scatter, tag, collective_id = state
    chip = [add_core_halves(a, l, out_dtype=BF16 if sc else F32, name=f"reduce_core_add_{tag}_{i}")
            for i, (a, l, sc) in enumerate(zip(arrays, land, scatter))]
    sending, token = send_to_chips_start(chip, scatter, name=f"reduce_chip_start_{tag}")
    token, before = _ordered_before(token, before)
    return (sending, token, scatter, tag, collective_id), before


def reduce_finish(state, after):
    sending, token, scatter, tag, collective_id = state
    chip, land = send_to_chips_wait(sending, tuple(after) + (token,), name=f"reduce_chip_wait_{tag}")
    own = [sum_over_chips(ch, l, scatter=sc, name=f"reduce_chip_add_{tag}_{i}") for i, (ch, l, sc) in enumerate(zip(chip, land, scatter))]
    sib = swap_with_other_core(own, name=f"reduce_core_swap_{tag}", collective_id=collective_id + 2)
    return own, sib


def _ffn_layer_fwd(h, norm_g, w_up, cw, cb, w_down, tag):
    u, hn = norm_matmul(h, norm_g, w_up, name=f"ffn_up_{tag}")
    act = ffn_act_fwd(u, cw, cb, name=f"ffn_act_{tag}")
    out = matmul(act, w_down, add=h, name=f"ffn_down_{tag}")
    return out, (h, hn, u, act)


def _travel_layout(array):
    return BIG_ARRAYS[array][3], BIG_ARRAYS[array][4]


def _ffn_layer_bwd(saved, dout, norm_g, w_up, cw, cb, w_down, tag):
    h, hn, u, act = saved
    dact = matmul(dout, w_down, tb=True, name=f"ffn_down_dx_{tag}")
    d_w_down = matmul(act, dout, ta=True, split=_travel_layout(f"ffn_w_down_{tag}"), name=f"ffn_down_dw_{tag}")
    du, dcw, dcb = ffn_act_bwd(u, cw, cb, dact, name=f"ffn_act_bwd_{tag}")
    dhn = matmul(du, w_up, tb=True, name=f"ffn_up_dx_{tag}")
    d_w_up = matmul(hn, du, ta=True, split=_travel_layout(f"ffn_w_up_{tag}"), name=f"ffn_up_dw_{tag}")
    dh, dg = norm_bwd(h, norm_g, dhn, dout, name=f"ffn_norm_bwd_{tag}")
    return dh, dg, d_w_up, dcw, dcb, d_w_down


def local_step(x, target, w, stage=lambda name, tensors, grads=None: tensors):
    g = {}
    tables = _ret_tables()
    x = stage("start", x)
    w_in_t = w["ret_gdn_w_in"]
    w_main = w_in_t[:MIX_MAIN]
    w_small = jnp.pad(w_in_t[MIX_MAIN:], ((0, LANES - 2 * N_HEADS), (0, 0)))
    a_log = jnp.pad(w["gdn_a_log"], ((0, 0), (0, LANES - N_HEADS)))
    dt_bias = jnp.pad(w["gdn_dt_bias"], ((0, 0), (0, LANES - N_HEADS)))

    p, hn0 = norm_matmul(x, w["norm_mix"][0:1], w_main, tb=True, name="mix0_in")
    hn0 = stage("normed", hn0)
    small = matmul(hn0, w_small, tb=True, name="mix0_in_small")
    y_ret, s_ret = ret_fwd(p, tables, name="ret_fwd")
    conv = gdn_conv_fwd(p, w["gdn_conv_w"], name="gdn_conv")
    y0, s_gdn = gdn_fwd(conv, p, small, a_log, dt_bias, w["gdn_out_gain"], y_ret, name="gdn_fwd")
    y0 = stage("mixed", y0)
    h1 = matmul(y0, w["ret_gdn_w_out"], add=x, name="mix0_out")
    h2, ffn0 = _ffn_layer_fwd(h1, w["norm_ffn"][0:1], w["ffn_w_up"][0], w["ffn_conv_w"][0], w["ffn_conv_b"][0:1], w["ffn_w_down"][0], "0")
    h2 = stage("layer0", h2)

    gx, hn1 = norm_matmul(h2, w["norm_mix"][1:2], w["lru_w_in"], name="mix1_in")
    lru_p = (w["lru_conv_w"], w["lru_conv_b"], w["lru_w_a"], w["lru_b_a"], w["lru_w_x"], w["lru_b_x"], w["lru_lambda"])
    y1 = lru_fwd(gx, *lru_p, name="lru_fwd")
    h3 = stage("mixed1", matmul(y1, w["lru_w_out"], add=h2, name="mix1_out"))
    h4, ffn1 = _ffn_layer_fwd(h3, w["norm_ffn"][1:2], w["ffn_w_up"][1], w["ffn_conv_w"][1], w["ffn_conv_b"][1:2], w["ffn_w_down"][1], "1")

    loss, dh4, g["norm_final"] = final_fwd_bwd(h4, w["norm_final"], target, name="final")

    dh3, dgf1, dwu1, dcw1, dcb1, dwd1 = _ffn_layer_bwd(ffn1, dh4, w["norm_ffn"][1:2], w["ffn_w_up"][1], w["ffn_conv_w"][1],
                                                     w["ffn_conv_b"][1:2], w["ffn_w_down"][1], "1")
    g["ffn_w_up_1"], g["ffn_w_down_1"] = dwu1, dwd1
    dh3 = stage("grads0_ready", dh3, g)
    dy1 = matmul(dh3, w["lru_w_out"], tb=True, name="mix1_out_dx")
    g["lru_w_out"] = matmul(y1, dh3, ta=True, split=_travel_layout("lru_w_out"), name="mix1_out_dw")
    dgx, g["lru_conv_w"], g["lru_conv_b"], g["lru_w_a"], g["lru_b_a"], g["lru_w_x"], g["lru_b_x"], g["lru_lambda"] = lru_bwd(
        gx, *lru_p, dy1, name="lru_bwd")
    dgx = stage("grads0_send", dgx, g)
    dhn1 = matmul(dgx, w["lru_w_in"], tb=True, name="mix1_in_dx")
    g["lru_w_in"] = matmul(hn1, dgx, ta=True, split=_travel_layout("lru_w_in"), name="mix1_in_dw")
    dh2, dgm1 = norm_bwd(h2, w["norm_mix"][1:2], dhn1, dh3, name="mix1_norm_bwd")
    dh2 = stage("grads1_ready", dh2, g)

    dh1, dgf0, dwu0, dcw0, dcb0, dwd0 = _ffn_layer_bwd(ffn0, dh2, w["norm_ffn"][0:1], w["ffn_w_up"][0], w["ffn_conv_w"][0],
                                                     w["ffn_conv_b"][0:1], w["ffn_w_down"][0], "0")
    g["ffn_w_up_0"], g["ffn_w_down_0"] = dwu0, dwd0
    dh1 = stage("grads2_ready", stage("grads1_send", dh1, g), g)
    dy0 = matmul(dh1, w["ret_gdn_w_out"], tb=True, name="mix0_out_dx")
    g["ret_gdn_w_out"] = matmul(y0, dh1, ta=True, split=_travel_layout("ret_gdn_w_out"), name="mix0_out_dw")
    dq_r, dk_r, dv_r, dg_r = ret_bwd(p, tables, s_ret, dy0, name="ret_bwd")
    dy0, dq_r = stage("grads2_send", (dy0, dq_r), g)
    dconv, dg_d, dsmall, dal, ddt, dgain = gdn_bwd(conv, p, small, a_log, dt_bias, w["gdn_out_gain"], s_gdn, dy0, name="gdn_bwd")
    dp_conv, g["gdn_conv_w"] = gdn_conv_bwd(p, w["gdn_conv_w"], dconv, name="gdn_conv_bwd")
    dp = jnp.concatenate([dq_r, dk_r, dv_r, dg_r, dp_conv, dg_d], axis=1)
    dhn0 = matmul(dp, w_main, name="mix0_in_dx")
    dhn0 = matmul(dsmall, w_small, add=dhn0, name="mix0_in_small_dx")
    d_w_in = matmul(dp, hn0, ta=True, column_halves=(MIX_IN, 0, None, None), name="mix0_in_dw")
    d_w_in = matmul(dsmall, hn0, ta=True, column_halves=(MIX_IN, MIX_MAIN, 2 * N_HEADS, d_w_in), name="mix0_in_small_dw")
    g["ret_gdn_w_in"] = d_w_in.reshape(2, N_SHARD, MIX_IN // N_SHARD, D_MODEL // 2)
    dx, dgm0 = norm_bwd(x, w["norm_mix"][0:1], dhn0, dh1, name="mix0_norm_bwd")

    g["gdn_a_log"] = dal[:, :N_HEADS]
    g["gdn_dt_bias"] = ddt[:, :N_HEADS]
    g["gdn_out_gain"] = dgain
    g["norm_mix"] = jnp.concatenate([dgm0, dgm1], axis=0)
    g["norm_ffn"] = jnp.concatenate([dgf0, dgf1], axis=0)
    g["ffn_conv_w"] = jnp.stack([dcw0, dcw1])
    g["ffn_conv_b"] = jnp.concatenate([dcb0, dcb1], axis=0)
    return loss, dx, g


WEIGHTS = ("norm_mix", "norm_ffn", "ret_gdn_w_in", "gdn_conv_w", "gdn_a_log", "gdn_dt_bias", "gdn_out_gain", "ret_gdn_w_out",
           "lru_w_in", "lru_conv_w", "lru_conv_b", "lru_w_a", "lru_b_a", "lru_w_x", "lru_b_x", "lru_lambda", "lru_w_out",
           "ffn_w_up", "ffn_conv_w", "ffn_conv_b", "ffn_w_down", "norm_final")
MATMUL_SHARDED = {"ret_gdn_w_in": 1, "ret_gdn_w_out": 0, "lru_w_in": 1, "lru_w_out": 0, "ffn_w_up": 2, "ffn_w_down": 1}
VECTOR_SHARDED = {"gdn_conv_w": 1, "lru_conv_w": 1, "lru_conv_b": 1, "lru_b_a": 1, "lru_b_x": 1, "lru_lambda": 1, "ffn_conv_w": 2}
SHARDED = {**MATMUL_SHARDED, **VECTOR_SHARDED}
REPLICATED = tuple(n for n in WEIGHTS if n not in SHARDED)
SQUEEZE = {"ret_gdn_w_in", "gdn_conv_w", "ret_gdn_w_out", "lru_w_in", "lru_conv_w", "lru_w_a", "lru_w_x", "lru_w_out"}
MIX_IN = MIX_MAIN + 2 * N_HEADS
BIG_ARRAYS = {
    "ret_gdn_w_in": ("ret_gdn_w_in", None, (MIX_IN, D_MODEL), (N_SHARD, MIX_IN // N_SHARD, 2, D_MODEL // 2), (2, 0, 1, 3)),
    "ret_gdn_w_out": ("ret_gdn_w_out", None, (2 * GROUP, D_MODEL), (N_SHARD, 2, GROUP // N_SHARD, D_MODEL), (1, 0, 2, 3)),
    "lru_w_in": ("lru_w_in", None, (D_MODEL, 2 * D_MODEL), (2, D_MODEL // 2, N_SHARD, 2 * D_MODEL // N_SHARD), (0, 2, 1, 3)),
    "lru_w_out": ("lru_w_out", None, (D_MODEL, D_MODEL), (N_SHARD, 2, D_MODEL // (2 * N_SHARD), D_MODEL), (1, 0, 2, 3)),
    "ffn_w_up_0": ("ffn_w_up", 0, (D_MODEL, 2 * D_FF), (2, D_MODEL // 2, N_SHARD, 2 * D_FF // N_SHARD), (0, 2, 1, 3)),
    "ffn_w_up_1": ("ffn_w_up", 1, (D_MODEL, 2 * D_FF), (2, D_MODEL // 2, N_SHARD, 2 * D_FF // N_SHARD), (0, 2, 1, 3)),
    "ffn_w_down_0": ("ffn_w_down", 0, (D_FF, D_MODEL), (N_SHARD, 2, D_FF // (2 * N_SHARD), D_MODEL), (1, 0, 2, 3)),
    "ffn_w_down_1": ("ffn_w_down", 1, (D_FF, D_MODEL), (N_SHARD, 2, D_FF // (2 * N_SHARD), D_MODEL), (1, 0, 2, 3)),
}
GATHER_GROUPS = (("ret_gdn_w_in",), ("ret_gdn_w_out", "ffn_w_up_0", "ffn_w_down_0"), ("lru_w_in", "lru_w_out"), ("ffn_w_up_1", "ffn_w_down_1"))
REDUCE_GROUPS = (("ffn_w_up_1", "ffn_w_down_1"), ("lru_w_in", "lru_w_out"), ("ffn_w_up_0", "ffn_w_down_0"), ("ret_gdn_w_out", "ret_gdn_w_in"))
BLOCK_WEIGHTS = ("lru_w_a", "lru_w_x")
GATHER_COLLECTIVE_ID = 1
REDUCE_COLLECTIVE_ID = GATHER_COLLECTIVE_ID + len(GATHER_GROUPS)


TRANSPOSED = ("ret_gdn_w_in",)


def _shard_of(array, tensors):
    weight, layer = BIG_ARRAYS[array][:2]
    t = tensors[weight]
    if weight in TRANSPOSED:
        return jnp.swapaxes(t, 1, 2)[0]
    return _local_view(weight, t) if layer is None else t[layer]


def _core_halves(array, shard):
    _, _, _, split, perm = BIG_ARRAYS[array]
    kept = [k for k in range(4) if k != perm[1]]
    order = [kept.index(perm[0]), kept.index(perm[2]), kept.index(perm[3])]
    return shard.reshape([split[k] for k in kept]).transpose(order)


def _local_view(name, a):
    if name in SQUEEZE:
        return a[0]
    if a.ndim == 1:
        return a[None, :]
    return a


def kernel(x, norm_mix, norm_ffn, ret_gdn_w_in, gdn_conv_w, gdn_a_log, gdn_dt_bias, gdn_out_gain, ret_gdn_w_out, lru_w_in, lru_conv_w, lru_conv_b, lru_w_a, lru_b_a, lru_w_x, lru_b_x, lru_lambda, lru_w_out, ffn_w_up, ffn_conv_w, ffn_conv_b, ffn_w_down, norm_final, loss_target, m_norm_mix, m_norm_ffn, m_ret_gdn_w_in, m_gdn_conv_w, m_gdn_a_log, m_gdn_dt_bias, m_gdn_out_gain, m_ret_gdn_w_out, m_lru_w_in, m_lru_conv_w, m_lru_conv_b, m_lru_w_a, m_lru_b_a, m_lru_w_x, m_lru_b_x, m_lru_lambda, m_lru_w_out, m_ffn_w_up, m_ffn_conv_w, m_ffn_conv_b, m_ffn_w_down, m_norm_final, v_norm_mix, v_norm_ffn, v_ret_gdn_w_in, v_gdn_conv_w, v_gdn_a_log, v_gdn_dt_bias, v_gdn_out_gain, v_ret_gdn_w_out, v_lru_w_in, v_lru_conv_w, v_lru_conv_b, v_lru_w_a, v_lru_b_a, v_lru_w_x, v_lru_b_x, v_lru_lambda, v_lru_w_out, v_ffn_w_up, v_ffn_conv_w, v_ffn_conv_b, v_ffn_w_down, v_norm_final):
    given = dict(norm_mix=norm_mix, norm_ffn=norm_ffn, ret_gdn_w_in=ret_gdn_w_in, gdn_conv_w=gdn_conv_w, gdn_a_log=gdn_a_log, gdn_dt_bias=gdn_dt_bias, gdn_out_gain=gdn_out_gain, ret_gdn_w_out=ret_gdn_w_out, lru_w_in=lru_w_in, lru_conv_w=lru_conv_w, lru_conv_b=lru_conv_b, lru_w_a=lru_w_a, lru_b_a=lru_b_a, lru_w_x=lru_w_x, lru_b_x=lru_b_x, lru_lambda=lru_lambda, lru_w_out=lru_w_out, ffn_w_up=ffn_w_up, ffn_conv_w=ffn_conv_w, ffn_conv_b=ffn_conv_b, ffn_w_down=ffn_w_down, norm_final=norm_final)
    mom1 = dict(norm_mix=m_norm_mix, norm_ffn=m_norm_ffn, ret_gdn_w_in=m_ret_gdn_w_in, gdn_conv_w=m_gdn_conv_w, gdn_a_log=m_gdn_a_log, gdn_dt_bias=m_gdn_dt_bias, gdn_out_gain=m_gdn_out_gain, ret_gdn_w_out=m_ret_gdn_w_out, lru_w_in=m_lru_w_in, lru_conv_w=m_lru_conv_w, lru_conv_b=m_lru_conv_b, lru_w_a=m_lru_w_a, lru_b_a=m_lru_b_a, lru_w_x=m_lru_w_x, lru_b_x=m_lru_b_x, lru_lambda=m_lru_lambda, lru_w_out=m_lru_w_out, ffn_w_up=m_ffn_w_up, ffn_conv_w=m_ffn_conv_w, ffn_conv_b=m_ffn_conv_b, ffn_w_down=m_ffn_w_down, norm_final=m_norm_final)
    mom2 = dict(norm_mix=v_norm_mix, norm_ffn=v_norm_ffn, ret_gdn_w_in=v_ret_gdn_w_in, gdn_conv_w=v_gdn_conv_w, gdn_a_log=v_gdn_a_log, gdn_dt_bias=v_gdn_dt_bias, gdn_out_gain=v_gdn_out_gain, ret_gdn_w_out=v_ret_gdn_w_out, lru_w_in=v_lru_w_in, lru_conv_w=v_lru_conv_w, lru_conv_b=v_lru_conv_b, lru_w_a=v_lru_w_a, lru_b_a=v_lru_b_a, lru_w_x=v_lru_w_x, lru_b_x=v_lru_b_x, lru_lambda=v_lru_lambda, lru_w_out=v_lru_w_out, ffn_w_up=v_ffn_w_up, ffn_conv_w=v_ffn_conv_w, ffn_conv_b=v_ffn_conv_b, ffn_w_down=v_ffn_w_down, norm_final=v_norm_final)

    local = {n: _local_view(n, a) for n, a in given.items()}

    core = lax.axis_index("c")
    chip = 2 * lax.axis_index("x") + lax.axis_index("y")
    is_my_chip = lax.broadcasted_iota(jnp.int32, (N_SHARD, 1, 1), 0) == chip

    def by_core(mine, other):
        return jnp.where(core == 0, jnp.stack([mine, other]), jnp.stack([other, mine]))

    vec_names, rp_names = list(VECTOR_SHARDED), list(REPLICATED)
    full = dict(zip(vec_names, all_gather_shards([local[n] for n in vec_names], [SHARDED[n] for n in vec_names], F32, 32, "p")))
    for n in rp_names:
        full[n] = local[n]
    in_flight = {}

    bf16_halves = {}

    def cast_halves(gi):
        if gi not in bf16_halves:
            bf16_halves[gi] = [_core_halves(a, _shard_of(a, given).astype(BF16)) for a in GATHER_GROUPS[gi]]
        return bf16_halves[gi]

    def launch(gi, after=None):
        halves = cast_halves(gi)
        if after is not None:
            halves, after = lax.optimization_barrier((halves, after))
        in_flight[gi] = (halves,) + gather_halves(halves, name=f"gather_weights_{gi}", collective_id=GATHER_COLLECTIVE_ID + gi)
        return after

    def land(gi, after):
        halves, lands, sibs = in_flight[gi]
        (lands, sibs), after = lax.optimization_barrier(((lands, sibs), after))
        for a, mine, got, passed in zip(GATHER_GROUPS[gi], halves, lands, sibs):
            weight, layer, full_shape, split, perm = BIG_ARRAYS[a]
            half_mine = jnp.where(is_my_chip, jnp.where(core == 0, mine[0], mine[1])[None], got)
            half_other = jnp.where(is_my_chip, jnp.where(core == 0, mine[1], mine[0])[None], passed)
            value = by_core(half_mine, half_other).transpose(tuple(np.argsort(perm))).reshape(full_shape)
            if layer is None:
                full[weight] = value
            else:
                full.setdefault(weight, [None, None])[layer] = value
        return after

    reducing = {}

    def reduce_ready(gi, grads, then=None, extra=()):
        def travelling(a):
            split, perm = _travel_layout(a)
            return grads[a] if grads[a].ndim == 4 else grads[a].reshape(split).transpose(perm)

        arrays = [travelling(a) for a in REDUCE_GROUPS[gi]] + list(extra)
        scatter = [True] * len(REDUCE_GROUPS[gi]) + [False] * len(extra)
        reducing[gi], then = reduce_between_cores(arrays, scatter, tag=str(gi), collective_id=REDUCE_COLLECTIVE_ID + 3 * gi, before=then)
        return then

    def reduce_send(gi, then=None):
        reducing[gi], then = reduce_between_chips(reducing[gi], before=then)
        return then

    def stage(name, tensors, grads=None):
        if name == "start":
            launch(0)
            launch(1)
            fillers = (cast_halves(2), cast_halves(3), [full[n] for n in vec_names])
            (bf16_halves[2], bf16_halves[3], gathered_small), tensors = lax.optimization_barrier((fillers, tensors))
            full.update(zip(vec_names, gathered_small))
            return land(0, tensors)
        if name == "normed":
            return launch(3, launch(2, tensors))
        if name in ("mixed", "layer0", "mixed1"):
            return land({"mixed": 1, "layer0": 2, "mixed1": 3}[name], tensors)
        gi = int(name[len("grads")])
        return reduce_ready(gi, grads, tensors) if name.endswith("_ready") else reduce_send(gi, tensors)

    small_names = [n for n in rp_names if n not in BLOCK_WEIGHTS] + vec_names

    loss_part, dx, grads = local_step(x[0], loss_target[0], full, stage)
    small_shapes = [grads[n].shape for n in small_names] + [(1, 1)]
    small_rows = _pack_rows(sum(int(np.prod(s)) for s in small_shapes), 16)
    small = _pack([grads[n] for n in small_names] + [loss_part[:, :1]], small_rows, F32).reshape(2, 1, small_rows // 2, LANES)
    last = len(REDUCE_GROUPS) - 1
    halves_of_blocks = [grads[n].reshape(2, 1, LRU_BLOCKS * HEAD // 2, HEAD) for n in BLOCK_WEIGHTS]
    reduce_ready(last, grads, extra=[small] + halves_of_blocks)
    reduce_send(last)
    reduced, result = {}, {}

    def finish(gi, after):
        g_own, g_sib = reduce_finish(reducing[gi], after)
        reduced.update(zip(list(REDUCE_GROUPS[gi]) + ["small"] + list(BLOCK_WEIGHTS), zip(g_own, g_sib)))

    def update(n):
        if n in TRANSPOSED:
            n_rows, n_cols = given[n].shape[2], given[n].shape[1]

            def rows(t):
                return jnp.swapaxes(t, 1, 2).reshape(n_rows, 1, n_cols)

            def back(t):
                return jnp.swapaxes(t.reshape(1, n_rows, n_cols), 1, 2)

            g_rows = jnp.swapaxes(by_core(*reduced[n]), 0, 1).reshape(n_rows, 1, n_cols)
            result[n] = tuple(back(t) for t in adamw_rows(rows(given[n]), g_rows, rows(mom1[n]), rows(mom2[n]), name=f"adamw_{n}"))
            return
        done = None
        for a in (k for k, spec in BIG_ARRAYS.items() if spec[0] == n):
            r, cols = reduced[a][0].shape
            layer = BIG_ARRAYS[a][1] or 0
            w3, m3, v3 = (t if BIG_ARRAYS[a][1] is not None else t.reshape(1, 2 * r, cols) for t in (given[n], mom1[n], mom2[n]))
            done = adamw_halves(w3, m3, v3, *reduced[a], layer=layer, prev=done, name=f"adamw_{a}")
        result[n] = done

    updated = []
    for gi in range(last + 1):
        finish(gi, tuple(result[n][0] for n in updated) if updated else (dx, reducing[last][1]))
        for n in MATMUL_SHARDED:
            if n not in updated and all(a in reduced for a, spec in BIG_ARRAYS.items() if spec[0] == n):
                update(n)
                updated.append(n)

    for n in BLOCK_WEIGHTS:
        w3, m3, v3 = (t.reshape(1, LRU_BLOCKS * HEAD, HEAD) for t in (given[n], mom1[n], mom2[n]))
        result[n] = adamw_halves(w3, m3, v3, *reduced[n], name=f"adamw_{n}")

    *small_sums, loss_sum = _unpack(by_core(*reduced["small"]).reshape(small_rows, LANES), small_shapes)
    loss = loss_sum[0, 0]
    g_small = dict(zip(small_names, small_sums))
    for n in vec_names:
        size = local[n].shape[SHARDED[n]]
        g_small[n] = lax.dynamic_slice_in_dim(g_small[n], chip * size, size, axis=SHARDED[n])
    views = [[_local_view(n, src[n]) for n in small_names] for src in (given, mom1, mom2)]
    d_s, m_s, v_s = adamw_many(views[0], [g_small[n] for n in small_names], views[1], views[2], name="adamw_small")
    for n, d, nm, nv in zip(small_names, d_s, m_s, v_s):
        result[n] = (g_small[n], d, nm, nv)

    outs = [[result[n][k].reshape(given[n].shape) for n in WEIGHTS] for k in range(4)]
    return (loss, dx[None], *outs[0], *outs[1], *outs[2], *outs[3])
```

```python
import functools

import numpy as np
import jax
import jax.numpy as jnp
from jax import lax
from jax.experimental import pallas as pl
from jax.experimental.pallas import tpu as pltpu
from jax.experimental.pallas import tpu_sc as plsc

F32 = jnp.float32
BF16 = jnp.bfloat16
MESH = pl.DeviceIdType.MESH

SEQ = 2048
D_MODEL = 1024
N_HEADS = 4
HEAD = 128
RET_CHUNK = 128
RET_CHUNKS_PER_STEP = 2
GDN_CHUNK = 64
GDN_CHUNKS_PER_STEP = 8
GROUP = N_HEADS * HEAD
MIX_MAIN = 8 * GROUP
D_FF = 2816
LRU_BLOCKS = 8
LRU_C = 8.0
ROPE_BASE = 10000.0
EPS = 1e-6
N_SHARD = 4
LANES = 128

ADAM_LR, ADAM_B1, ADAM_B2, ADAM_EPS, ADAM_WD, ADAM_STEP = 0.001, 0.9, 0.999, 1e-08, 0.01, 10

VMEM_LIMIT_BYTES = 56 * 1024 * 1024

_roll = pltpu.roll


def _params(**kw):
    return pltpu.CompilerParams(vmem_limit_bytes=VMEM_LIMIT_BYTES, **kw)


def _sds(shape, dtype):
    return jax.ShapeDtypeStruct(tuple(shape), dtype)


def _shift_raw(x, d):
    n = x.shape[0]
    t = lax.broadcasted_iota(jnp.int32, x.shape, 0)
    if d > 0:
        return jnp.where(t >= d, _roll(x, d, 0), 0.0)
    return jnp.where(t < n + d, _roll(x, n + d, 0), 0.0)


@functools.partial(jax.custom_vjp, nondiff_argnums=(1,))
def shift_rows(x, d):
    return _shift_raw(x, d)


def _shift_fwd(x, d):
    return _shift_raw(x, d), None


def _shift_bwd(d, _, g):
    return (_shift_raw(g, -d),)


shift_rows.defvjp(_shift_fwd, _shift_bwd)


@jax.custom_vjp
def swap_halves(x):
    return _roll(x, HEAD // 2, 1)


def _swap_fwd(x):
    return _roll(x, HEAD // 2, 1), None


def _swap_bwd(_, g):
    return (_roll(g, HEAD // 2, 1),)


swap_halves.defvjp(_swap_fwd, _swap_bwd)


SCAN_BLOCK_ROWS = 16


def _scan_block(a, u, reverse):
    n = a.shape[0]
    t = lax.broadcasted_iota(jnp.int32, a.shape, 0)
    d = 1
    while d < n:
        if reverse:
            m = t < n - d
            a_s, u_s = _roll(a, n - d, 0), _roll(u, n - d, 0)
        else:
            m = t >= d
            a_s, u_s = _roll(a, d, 0), _roll(u, d, 0)
        u = a * jnp.where(m, u_s, 0.0) + u
        a = a * jnp.where(m, a_s, 1.0)
        d *= 2
    return a, u


def _scan_raw(a, u, reverse):
    n = a.shape[0]
    blocks = range(n // SCAN_BLOCK_ROWS)
    out = [None] * len(blocks)
    entering = None
    for b in (reversed(blocks) if reverse else blocks):
        rows = slice(b * SCAN_BLOCK_ROWS, (b + 1) * SCAN_BLOCK_ROWS)
        a_run, h = _scan_block(a[rows], u[rows], reverse)
        if entering is not None:
            h = a_run * entering + h
        out[b] = h
        entering = h[:1] if reverse else h[SCAN_BLOCK_ROWS - 1:]
    return jnp.concatenate(out, axis=0)


@jax.custom_vjp
def lin_scan(a, u):
    return _scan_raw(a, u, False)


def _lin_scan_fwd(a, u):
    hs = _scan_raw(a, u, False)
    return hs, (a, hs)


def _lin_scan_bwd(res, g):
    a, hs = res
    lam = _scan_raw(_shift_raw(a, -1), g, True)
    return lam * _shift_raw(hs, 1), lam


lin_scan.defvjp(_lin_scan_fwd, _lin_scan_bwd)


def _bdot(a, b, dims=(((1,), (0,)), ((), ()))):
    return lax.dot_general(a.astype(BF16), b.astype(BF16), dims, preferred_element_type=F32)


def _each(f, *seqs):
    return tuple(f(*a) for a in zip(*seqs))


def _split_bf16(a):
    hi = a.astype(BF16)
    return hi, (a - hi.astype(F32)).astype(BF16)


def _dot3_raw(a_s, b_s):
    a_hl = _each(_split_bf16, a_s)
    b_hl = _each(_split_bf16, b_s)
    hh = _each(lambda a, b: _bdot(a[0], b[0]), a_hl, b_hl)
    hl = _each(lambda a, b: _bdot(a[0], b[1]), a_hl, b_hl)
    lh = _each(lambda a, b: _bdot(a[1], b[0]), a_hl, b_hl)
    return _each(lambda x, y, z: x + (y + z), hh, hl, lh)


@jax.custom_vjp
def dot3(a_s, b_s):
    return _dot3_raw(a_s, b_s)


def _dot3_fwd(a_s, b_s):
    return _dot3_raw(a_s, b_s), (a_s, b_s)


def _dot3_bwd(res, g_s):
    a_s, b_s = res
    return (_each(lambda g, b: _bdot(g, b, (((1,), (1,)), ((), ()))), g_s, b_s),
            _each(lambda a, g: _bdot(a, g, (((0,), (0,)), ((), ()))), a_s, g_s))


dot3.defvjp(_dot3_fwd, _dot3_bwd)


def _eye(n):
    i = lax.broadcasted_iota(jnp.int32, (n, n), 0)
    j = lax.broadcasted_iota(jnp.int32, (n, n), 1)
    return (i == j).astype(F32)


def _unit_lower_inverse_raw(lmats):
    n = lmats[0].shape[0]
    eye = _eye(n)
    ps = _each(lambda l: -l, lmats)
    invs = _each(lambda x: eye + x, ps)
    k = 1
    while 2 * k < n:
        ps = _each(lambda p: _bdot(p, p), ps)
        invs = _each(lambda inv, p: inv + _bdot(inv, p), invs, ps)
        k *= 2
    prods = _dot3_raw(lmats, invs)
    resids = _each(lambda inv, pr: eye - inv - pr, invs, prods)
    return _each(lambda inv, r: inv + _bdot(inv, r), invs, resids)


@jax.custom_vjp
def unit_lower_inverse(lmats):
    return _unit_lower_inverse_raw(lmats)


def _uli_fwd(lmats):
    invs = _unit_lower_inverse_raw(lmats)
    return invs, invs


def _uli_bwd(invs, g_s):
    ms = _each(lambda inv, g: _bdot(inv, g, (((0,), (0,)), ((), ()))), invs, g_s)
    return (_each(lambda m, inv: -_bdot(m, inv, (((1,), (1,)), ((), ()))), ms, invs),)


unit_lower_inverse.defvjp(_uli_fwd, _uli_bwd)


def _cumsum_raw(x, reverse):
    n = x.shape[0]
    t = lax.broadcasted_iota(jnp.int32, x.shape, 0)
    d = 1
    while d < n:
        if reverse:
            x = x + jnp.where(t < n - d, _roll(x, n - d, 0), 0.0)
        else:
            x = x + jnp.where(t >= d, _roll(x, d, 0), 0.0)
        d *= 2
    return x


@jax.custom_vjp
def cumsum_rows(x):
    return _cumsum_raw(x, False)


def _cumsum_fwd(x):
    return _cumsum_raw(x, False), None


def _cumsum_bwd(_, g):
    return (_cumsum_raw(g, True),)


cumsum_rows.defvjp(_cumsum_fwd, _cumsum_bwd)


_NT = (((1,), (1,)), ((), ()))
_TN = (((0,), (0,)), ((), ()))


def _softplus(x):
    return jnp.maximum(x, 0.0) + jnp.log1p(jnp.exp(-jnp.abs(x)))


def _expm1_nonpos(x):
    poly = x * (1.0 + x * (0.5 + x * (1.0 / 6 + x * (1.0 / 24 + x * (1.0 / 120 + x * (1.0 / 720))))))
    return jnp.where(x > -0.25, poly, jnp.exp(x) - 1.0)


def _rms(x):
    return x * lax.rsqrt(jnp.mean(x * x, axis=-1, keepdims=True) + EPS)


def _causal_conv(x, w, width):
    y = w[width - 1:width, :] * x
    for j in range(width - 1):
        y = y + w[j:j + 1, :] * shift_rows(x, width - 1 - j)
    return y


def _norm_fn(x, g):
    return _rms(x) * g


def _ffn_act_fn(ug, uv, wg, wv, bg, bv):
    return jax.nn.silu(_causal_conv(ug, wg, 3) + bg) * (_causal_conv(uv, wv, 3) + bv)


def _gdn_conv_fn(x, w):
    return jax.nn.silu(_causal_conv(x, w, 4))


def _lru_fn(gate, x, cw, cb, wa, ba, wx, bx, lam):
    xr = _causal_conv(x, cw, 4) + cb
    r = jax.nn.sigmoid(_bdot(xr, wa) + ba)
    i = jax.nn.sigmoid(_bdot(xr, wx) + bx)
    log_a = -LRU_C * r * _softplus(-lam)
    a = jnp.exp(log_a)
    u = jnp.sqrt(-_expm1_nonpos(2.0 * log_a)) * (i * xr)
    hs = lin_scan(a, u)
    return jax.nn.gelu(gate) * hs


def _ret_fn(qs, ks, vs, gates, states, cos2, sin2, dmasks, ktails, qdecs, cdecs):
    c = RET_CHUNK
    n_heads = len(qs)
    n_chunks = qs[0].shape[0] // c
    units = tuple((ci, h) for ci in range(n_chunks) for h in range(n_heads))

    def rows(x, ci):
        return x[ci * c:(ci + 1) * c]

    qrs = tuple(rows(qs[h], ci) * rows(cos2, ci) + swap_halves(rows(qs[h], ci)) * rows(sin2, ci) for ci, h in units)
    krs = tuple((rows(ks[h], ci) * rows(cos2, ci) + swap_halves(rows(ks[h], ci)) * rows(sin2, ci)) * (HEAD ** -0.5) for ci, h in units)
    vus = tuple(rows(vs[h], ci) for ci, h in units)
    scores = tuple(_bdot(q, k, _NT) * dmasks[h] for q, k, (_, h) in zip(qrs, krs, units))
    intra = _each(lambda sc, v: _bdot(sc, v), scores, vus)
    outs = []
    for ci in range(n_chunks):
        mine = slice(ci * n_heads, (ci + 1) * n_heads)
        inter = _each(lambda q, d, s: _bdot(q * d, s), qrs[mine], qdecs, states)
        outs.append(_each(lambda a, b: a + b, intra[mine], inter))
        states = _each(lambda s, cd, k, kt, v: s * cd + _bdot(k * kt, v, _TN), states, cdecs, krs[mine], ktails, vus[mine])
    ys = tuple(_rms(jnp.concatenate([outs[ci][h] for ci in range(n_chunks)], axis=0)) * jax.nn.silu(gates[h]) for h in range(n_heads))
    return ys, states


def _pick_lane(x, lane_idx):
    lane = lax.broadcasted_iota(jnp.int32, x.shape, 1)
    return jnp.sum(jnp.where(lane == lane_idx, x, 0.0), axis=1, keepdims=True)


def _l2norm(x):
    return x * lax.rsqrt(jnp.sum(x * x, axis=-1, keepdims=True) + EPS)


def _gdn_fn(qcs, kcs, vcs, gates, small, a_log, dt_bias, gain, states):
    c = GDN_CHUNK
    n_heads = len(qcs)
    n_chunks = qcs[0].shape[0] // c
    units = tuple((ci, h) for ci in range(n_chunks) for h in range(n_heads))

    def unit_rows(per_head):
        return tuple(per_head[h][ci * c:(ci + 1) * c] for ci, h in units)

    smalls = tuple(small[ci * c:(ci + 1) * c] for ci, _ in units)
    heads = tuple(h for _, h in units)
    intra = _gdn_intra(unit_rows(qcs), unit_rows(kcs), unit_rows(vcs), smalls, heads, a_log, dt_bias)
    outs = []
    for ci in range(n_chunks):
        mine = slice(ci * n_heads, (ci + 1) * n_heads)
        os_, states = _gdn_inter(*(part[mine] for part in intra), states)
        outs.append(os_)
    ys = tuple(_rms(jnp.concatenate([outs[ci][h] for ci in range(n_chunks)], axis=0)) * gain * jax.nn.silu(gates[h])
               for h in range(n_heads))
    return ys, states


def _gdn_inter(qs, ks, us, ws, attns, gcs, g_lasts, states):
    v_news = _each(lambda u, w, s: u - _bdot(w, s), us, ws, states)
    inter = _each(lambda q, gc, s: _bdot(q * jnp.exp(gc), s), qs, gcs, states)
    os_ = _each(lambda x, a, v: x + _bdot(a, v), inter, attns, v_news)
    new_states = _each(lambda s, gl, k, gc, v: s * jnp.exp(gl) + _bdot(k * jnp.exp(gl - gc), v, _TN), states, g_lasts, ks, gcs, v_news)
    return os_, new_states


def _gdn_intra(qcs, kcs, vcs, smalls, heads, a_log, dt_bias):
    c = GDN_CHUNK
    qs = _each(lambda x: _l2norm(x) * (HEAD ** -0.5), qcs)
    ks = _each(_l2norm, kcs)
    betas = _each(lambda sm, h: jax.nn.sigmoid(_pick_lane(sm, h)), smalls, heads)
    gs = _each(lambda sm, h: -jnp.exp(_pick_lane(a_log, h)) * _softplus(_pick_lane(sm, h + N_HEADS) + _pick_lane(dt_bias, h)),
               smalls, heads)
    i = lax.broadcasted_iota(jnp.int32, (c, c), 0)
    j = lax.broadcasted_iota(jnp.int32, (c, c), 1)
    tril = i >= j
    gcs = _each(lambda g: cumsum_rows(jnp.broadcast_to(g, (c, LANES)))[:, :1], gs)
    gc_rows = _each(lambda gc: jnp.broadcast_to(gc, (c, c)), gcs)
    decays = _each(lambda r: jnp.where(tril, jnp.exp(jnp.where(tril, r - r.T, 0.0)), 0.0), gc_rows)
    kbs = _each(lambda k, b: k * b, ks, betas)
    lmats = _each(lambda kb, k, d: jnp.where(i > j, _bdot(kb, k, _NT) * d, 0.0), kbs, ks, decays)
    attns = _each(lambda q, k, d: jnp.where(tril, _bdot(q, k, _NT) * d, 0.0), qs, ks, decays)
    invs = unit_lower_inverse(lmats)
    us = dot3(invs, _each(lambda v, b: v * b, vcs, betas))
    ws = dot3(invs, _each(lambda kb, gc: kb * jnp.exp(gc), kbs, gcs))
    g_lasts = _each(lambda g: jnp.sum(g, axis=0, keepdims=True), gs)
    return qs, ks, us, ws, attns, gcs, g_lasts


def _final_fn(h, g, target):
    y = _rms(h) * g
    return 0.5 * jnp.sum(jnp.mean(jnp.square(y - target), axis=-1, keepdims=True), axis=0, keepdims=True)


def _tile(n, candidates):
    for t in candidates:
        if n % t == 0:
            return t
    raise ValueError(f"no tile for {n}")


MATMUL_RESIDENT_LHS_BYTES = 8 * 1024 * 1024


def matmul(a, b, *, ta=False, tb=False, add=None, more=None, out_dtype=F32, tm=None, tn=None, split=None, column_halves=None, name):
    m = a.shape[1] if ta else a.shape[0]
    k = a.shape[0] if ta else a.shape[1]
    n = b.shape[0] if tb else b.shape[1]
    assert k == (b.shape[1] if tb else b.shape[0])
    out_shape, out_block, out_index = (m, n), None, lambda i, j: (i, j)
    if split is not None:
        dims4, perm = split
        out_shape = tuple(dims4[p] for p in perm)
        r, cols = out_shape[2:]
        tm, tn = m, tn or _tile(cols, (1408, 512))
        cb = cols // tn
        if perm == (0, 2, 1, 3):
            out_block, out_index = (2, None, r, tn), lambda i, j: (0, j // cb, 0, j % cb)
        elif perm == (1, 0, 2, 3):
            out_block, out_index = (2, N_SHARD, r, tn), lambda i, j: (0, 0, 0, j)
        else:
            raise ValueError(perm)
    if tm is None and not ta and m * k * a.dtype.itemsize <= MATMUL_RESIDENT_LHS_BYTES:
        tm = m
    tm = tm or _tile(m, (1024, 512, 1408, 256, 128))
    tn = tn or _tile(n, (512, 1408, 256, 128))
    aliases, prev, keep_rows = {}, None, None
    if column_halves is not None:
        total_rows, first_row, keep_rows, prev = column_halves
        tn = n // 2
        rows_out = keep_rows or tm
        out_shape, out_block = (2, total_rows, tn), (None, rows_out, tn)
        out_index = lambda i, j: (j, first_row // rows_out + i, 0)
    dims = (((0 if ta else 1,), (1 if tb else 0,)), ((), ()))

    def body(a_ref, b_ref, *rest):
        acc = lax.dot_general(a_ref[...].astype(BF16), b_ref[...].astype(BF16), dims, preferred_element_type=F32)
        if more is not None:
            acc = acc + _bdot(rest[0][...], rest[1][...])
        if add is not None:
            acc = acc + rest[2 if more is not None else 0][...]
        o_ref = rest[-1]
        acc = acc.astype(out_dtype)
        if split is not None and split[1] == (1, 0, 2, 3):
            rows = o_ref.shape[2]
            for s in range(N_SHARD):
                for h in range(2):
                    o_ref[h, s] = acc[(2 * s + h) * rows:(2 * s + h + 1) * rows]
        elif keep_rows is not None:
            o_ref[...] = acc[:keep_rows]
        else:
            o_ref[...] = acc.reshape(o_ref.shape)

    a_spec = pl.BlockSpec((k, tm), lambda i, j: (0, i)) if ta else pl.BlockSpec((tm, k), lambda i, j: (i, 0))
    b_spec = pl.BlockSpec((tn, k), lambda i, j: (j, 0)) if tb else pl.BlockSpec((k, tn), lambda i, j: (0, j))
    o_spec = pl.BlockSpec(out_block or (tm, tn), out_index)
    in_specs, args = [a_spec, b_spec], [a, b]
    if more is not None:
        k2 = more[0].shape[1]
        in_specs += [pl.BlockSpec((tm, k2), lambda i, j: (i, 0)), pl.BlockSpec((k2, tn), lambda i, j: (0, j))]
        args += list(more)
    if add is not None:
        in_specs.append(o_spec)
        args.append(add)
    if prev is not None:
        aliases = {len(args): 0}
        in_specs.append(pl.BlockSpec(memory_space=pl.ANY))
        args.append(prev)
    return pl.pallas_call(body, out_shape=_sds(out_shape, out_dtype), grid=(m // tm, n // tn), in_specs=in_specs,
                          out_specs=o_spec, input_output_aliases=aliases, compiler_params=_params(), name=name)(*args)


def norm_matmul(x, g, b, *, tb=False, name):
    t, k = x.shape
    n = b.shape[0] if tb else b.shape[1]
    tn = _tile(n, (512, 1408, 256, 128))
    dims = (((1,), (1 if tb else 0,)), ((), ()))

    def body(x_ref, g_ref, b_ref, o_ref, hn_ref):
        @pl.when(pl.program_id(0) == 0)
        def _():
            hn_ref[...] = _norm_fn(x_ref[...], g_ref[...]).astype(BF16)

        o_ref[...] = lax.dot_general(hn_ref[...], b_ref[...].astype(BF16), dims, preferred_element_type=F32)

    b_spec = pl.BlockSpec((tn, k), lambda j: (j, 0)) if tb else pl.BlockSpec((k, tn), lambda j: (0, j))
    whole = pl.BlockSpec((t, k), lambda j: (0, 0))
    return pl.pallas_call(body, out_shape=(_sds((t, n), F32), _sds((t, k), BF16)), grid=(n // tn,),
                          in_specs=[whole, pl.BlockSpec((1, k), lambda j: (0, 0)), b_spec],
                          out_specs=(pl.BlockSpec((t, tn), lambda j: (0, j)), whole), compiler_params=_params(), name=name)(x, g, b)


ROW_TILE = 512


def norm_bwd(x, g, dy, dres, *, name):
    t, d = x.shape

    def body(x_ref, g_ref, dy_ref, dres_ref, dx_ref, dg_ref):
        _, vjp = jax.vjp(_norm_fn, x_ref[...], g_ref[...])
        dx, dg = vjp(dy_ref[...])
        dx_ref[...] = dx + dres_ref[...]

        @pl.when(pl.program_id(0) == 0)
        def _():
            dg_ref[...] = jnp.zeros_like(dg_ref)

        dg_ref[...] += dg

    row = pl.BlockSpec((ROW_TILE, d), lambda i: (i, 0))
    vec = pl.BlockSpec((1, d), lambda i: (0, 0))
    return pl.pallas_call(body, out_shape=(_sds((t, d), F32), _sds((1, d), F32)), grid=(t // ROW_TILE,),
                          in_specs=[row, vec, row, row], out_specs=(row, vec), compiler_params=_params(), name=name)(x, g, dy, dres)


def final_fwd_bwd(h, g, target, *, name):
    t, d = h.shape

    def body(h_ref, g_ref, t_ref, loss_ref, dh_ref, dg_ref):
        tgt = t_ref[...]
        loss, vjp = jax.vjp(lambda hh, gg: _final_fn(hh, gg, tgt), h_ref[...], g_ref[...])
        dh, dg = vjp(jnp.ones((1, 1), F32))
        dh_ref[...] = dh

        @pl.when(pl.program_id(0) == 0)
        def _():
            dg_ref[...] = jnp.zeros_like(dg_ref)
            loss_ref[...] = jnp.zeros_like(loss_ref)

        dg_ref[...] += dg
        loss_ref[...] += jnp.broadcast_to(loss, loss_ref.shape)

    row = pl.BlockSpec((ROW_TILE, d), lambda i: (i, 0))
    vec = pl.BlockSpec((1, d), lambda i: (0, 0))
    return pl.pallas_call(body, out_shape=(_sds((1, LANES), F32), _sds((t, d), F32), _sds((1, d), F32)), grid=(t // ROW_TILE,),
                          in_specs=[row, vec, row], out_specs=(pl.BlockSpec((1, LANES), lambda i: (0, 0)), row, vec),
                          compiler_params=_params(), name=name)(h, g, target)


FFN_FWD_COLS = 256
FFN_BWD_COLS = 128


def ffn_act_fwd(u, cw, cb, *, name):
    t = u.shape[0]
    w = FFN_FWD_COLS
    nb = D_FF // w

    def body(ug_ref, uv_ref, wg_ref, wv_ref, bg_ref, bv_ref, o_ref):
        o_ref[...] = _ffn_act_fn(ug_ref[...], uv_ref[...], wg_ref[...], wv_ref[...], bg_ref[...], bv_ref[...]).astype(BF16)

    def col(rows, off):
        return pl.BlockSpec((rows, w), lambda j: (0, j + off))

    return pl.pallas_call(body, out_shape=_sds((t, D_FF), BF16), grid=(nb,),
                          in_specs=[col(t, 0), col(t, nb), col(3, 0), col(3, nb), col(1, 0), col(1, nb)],
                          out_specs=col(t, 0), compiler_params=_params(), name=name)(u, u, cw, cw, cb, cb)


def _put_column_blocks(step, n_steps, blocks, dst_ref, width, stage_ref, sems):
    def copies(at):
        slot = at % 2
        return [pltpu.make_async_copy(stage_ref.at[slot, p], dst_ref.at[:, pl.ds(pl.multiple_of((p * n_steps + at) * width, LANES), width)],
                                      sems.at[slot, p]) for p in range(len(blocks))]

    @pl.when(step >= 2)
    def _():
        for cp in copies(step - 2):
            cp.wait()

    for p, value in enumerate(blocks):
        stage_ref[step % 2, p] = value
    for cp in copies(step):
        cp.start()

    @pl.when(step == n_steps - 1)
    def _():
        for cp in copies(step - 1) + copies(step):
            cp.wait()


def ffn_act_bwd(u, cw, cb, da, *, name):
    t = u.shape[0]
    w = FFN_BWD_COLS
    nb = D_FF // w

    def body(ug_ref, uv_ref, wg_ref, wv_ref, bg_ref, bv_ref, da_ref, dug_ref, duv_ref, dwg_ref, dwv_ref, dbg_ref, dbv_ref):
        _, vjp = jax.vjp(_ffn_act_fn, ug_ref[...], uv_ref[...], wg_ref[...], wv_ref[...], bg_ref[...], bv_ref[...])
        dug, duv, dwg, dwv, dbg, dbv = vjp(da_ref[...])
        dug_ref[...] = dug.astype(BF16)
        duv_ref[...] = duv.astype(BF16)
        dwg_ref[...] = dwg
        dwv_ref[...] = dwv
        dbg_ref[...] = dbg
        dbv_ref[...] = dbv

    def col(rows, off):
        return pl.BlockSpec((rows, w), lambda j: (0, j + off))

    outs = pl.pallas_call(
        body, out_shape=(_sds((t, D_FF), BF16), _sds((t, D_FF), BF16), _sds((3, D_FF), F32), _sds((3, D_FF), F32),
                         _sds((1, D_FF), F32), _sds((1, D_FF), F32)),
        grid=(nb,), in_specs=[col(t, 0), col(t, nb), col(3, 0), col(3, nb), col(1, 0), col(1, nb), col(t, 0)],
        out_specs=(col(t, 0), col(t, 0), col(3, 0), col(3, 0), col(1, 0), col(1, 0)), compiler_params=_params(), name=name,
    )(u, u, cw, cw, cb, cb, da)
    dug, duv, dwg, dwv, dbg, dbv = outs
    return jnp.concatenate([dug, duv], axis=1), jnp.concatenate([dwg, dwv], axis=1), jnp.concatenate([dbg, dbv], axis=1)


GDN_CONV_COLS = 256
GDN_CONV_OFF = 4 * GROUP


def gdn_conv_fwd(p, cw, *, name):
    t = p.shape[0]
    w = GDN_CONV_COLS
    nb = 3 * GROUP // w
    off = GDN_CONV_OFF // w

    def body(x_ref, w_ref, o_ref):
        o_ref[...] = _gdn_conv_fn(x_ref[...], w_ref[...])

    return pl.pallas_call(body, out_shape=_sds((t, 3 * GROUP), F32), grid=(nb,),
                          in_specs=[pl.BlockSpec((t, w), lambda j: (0, j + off)), pl.BlockSpec((4, w), lambda j: (0, j))],
                          out_specs=pl.BlockSpec((t, w), lambda j: (0, j)), compiler_params=_params(), name=name)(p, cw)


def gdn_conv_bwd(p, cw, dc, *, name):
    t = p.shape[0]
    w = GDN_CONV_COLS
    nb = 3 * GROUP // w
    off = GDN_CONV_OFF // w

    def body(x_ref, w_ref, dc_ref, dx_ref, dw_ref):
        _, vjp = jax.vjp(_gdn_conv_fn, x_ref[...], w_ref[...])
        dx, dw = vjp(dc_ref[...])
        dx_ref[...] = dx.astype(BF16)
        dw_ref[...] = dw

    blk = pl.BlockSpec((t, w), lambda j: (0, j))
    wblk = pl.BlockSpec((4, w), lambda j: (0, j))
    return pl.pallas_call(body, out_shape=(_sds((t, 3 * GROUP), BF16), _sds((4, 3 * GROUP), F32)), grid=(nb,),
                          in_specs=[pl.BlockSpec((t, w), lambda j: (0, j + off)), wblk, blk], out_specs=(blk, wblk),
                          compiler_params=_params(), name=name)(p, cw, dc)


def _lru_specs(t):
    w = D_MODEL // LRU_BLOCKS
    gate = pl.BlockSpec((t, w), lambda j: (0, j))
    xin = pl.BlockSpec((t, w), lambda j: (0, j + LRU_BLOCKS))
    cw = pl.BlockSpec((4, w), lambda j: (0, j))
    vec = pl.BlockSpec((1, w), lambda j: (0, j))
    mat = pl.BlockSpec((None, w, w), lambda j: (j, 0, 0))
    return gate, xin, cw, vec, mat


def lru_fwd(gx, cw, cb, wa, ba, wx, bx, lam, *, name):
    t = gx.shape[0]
    gate, xin, cws, vec, mat = _lru_specs(t)

    def body(g_ref, x_ref, cw_ref, cb_ref, wa_ref, ba_ref, wx_ref, bx_ref, lam_ref, o_ref):
        o_ref[...] = _lru_fn(g_ref[...], x_ref[...], cw_ref[...], cb_ref[...], wa_ref[...], ba_ref[...], wx_ref[...],
                             bx_ref[...], lam_ref[...]).astype(BF16)

    return pl.pallas_call(body, out_shape=_sds((t, D_MODEL), BF16), grid=(LRU_BLOCKS,),
                          in_specs=[gate, xin, cws, vec, mat, vec, mat, vec, vec], out_specs=gate,
                          compiler_params=_params(), name=name)(gx, gx, cw, cb, wa, ba, wx, bx, lam)


def lru_bwd(gx, cw, cb, wa, ba, wx, bx, lam, dy, *, name):
    t = gx.shape[0]
    gate, xin, cws, vec, mat = _lru_specs(t)

    def body(g_ref, x_ref, cw_ref, cb_ref, wa_ref, ba_ref, wx_ref, bx_ref, lam_ref, dy_ref,
             dgx_ref, dcw_ref, dcb_ref, dwa_ref, dba_ref, dwx_ref, dbx_ref, dlam_ref, stage_ref, sems):
        _, vjp = jax.vjp(_lru_fn, g_ref[...], x_ref[...], cw_ref[...], cb_ref[...], wa_ref[...], ba_ref[...], wx_ref[...],
                         bx_ref[...], lam_ref[...])
        dg, dx, dcw, dcb, dwa, dba, dwx, dbx, dlam = vjp(dy_ref[...])
        _put_column_blocks(pl.program_id(0), LRU_BLOCKS, (dg.astype(BF16), dx.astype(BF16)), dgx_ref, D_MODEL // LRU_BLOCKS, stage_ref, sems)
        dcw_ref[...] = dcw
        dcb_ref[...] = dcb
        dwa_ref[...] = dwa
        dba_ref[...] = dba
        dwx_ref[...] = dwx
        dbx_ref[...] = dbx
        dlam_ref[...] = dlam

    d = D_MODEL
    w = d // LRU_BLOCKS
    out_shape = (_sds((t, 2 * d), BF16), _sds((4, d), F32), _sds((1, d), F32), _sds((LRU_BLOCKS, w, w), F32),
                 _sds((1, d), F32), _sds((LRU_BLOCKS, w, w), F32), _sds((1, d), F32), _sds((1, d), F32))
    return pl.pallas_call(body, out_shape=out_shape, grid=(LRU_BLOCKS,),
                          in_specs=[gate, xin, cws, vec, mat, vec, mat, vec, vec, gate],
                          out_specs=(pl.BlockSpec(memory_space=pl.ANY), cws, vec, mat, vec, mat, vec, vec),
                          scratch_shapes=[pltpu.VMEM((2, 2, t, w), BF16), pltpu.SemaphoreType.DMA((2, 2))],
                          compiler_params=_params(), name=name)(gx, gx, cw, cb, wa, ba, wx, bx, lam, dy)


def _ret_tables():
    half = HEAD // 2
    inv_freq = (np.float32(ROPE_BASE) ** (-np.arange(half, dtype=np.float32) / np.float32(half))).astype(np.float32)
    ang = (np.arange(SEQ, dtype=np.float32)[:, None] * inv_freq[None, :]).astype(np.float64)
    cos2 = np.concatenate([np.cos(ang), np.cos(ang)], axis=1).astype(np.float32)
    sin2 = np.concatenate([-np.sin(ang), np.sin(ang)], axis=1).astype(np.float32)
    c = RET_CHUNK
    log_gamma = np.log1p(-np.exp2(-5.0 - np.arange(N_HEADS, dtype=np.float64)))
    idx = np.arange(c, dtype=np.float64)
    rel = idx[:, None] - idx[None, :]
    dmask = np.where(rel >= 0, np.exp(log_gamma[:, None, None] * np.maximum(rel, 0.0)), 0.0)
    ones = np.ones((N_HEADS, c, HEAD))
    ktail = np.exp(log_gamma[:, None] * (c - 1 - idx))[:, :, None] * ones
    qdec = np.exp(log_gamma[:, None] * (idx + 1.0))[:, :, None] * ones
    cdec = np.exp(log_gamma * c)[:, None, None] * ones
    return tuple(jnp.asarray(a, F32) for a in (cos2, sin2, dmask, ktail, qdec, cdec))


def _ret_specs(rev):
    c = RET_CHUNK * RET_CHUNKS_PER_STEP
    nc = SEQ // c

    def n_of(n):
        return nc - 1 - n if rev else n

    def group(off):
        return pl.BlockSpec((c, GROUP), lambda n: (n_of(n), off))

    tab = pl.BlockSpec((c, HEAD), lambda n: (n_of(n), 0))
    const = pl.BlockSpec((N_HEADS, RET_CHUNK, HEAD), lambda n: (0, 0, 0))
    state = pl.BlockSpec((N_HEADS, None, HEAD, HEAD), lambda n: (0, n_of(n), 0, 0))
    return group, tab, const, state, nc


def _head(h):
    return slice(h * HEAD, (h + 1) * HEAD)


def ret_fwd(p, tables, *, name):
    group, tab, const, state, nc = _ret_specs(False)

    def body(q_ref, k_ref, v_ref, g_ref, cos_ref, sin_ref, dm_ref, kt_ref, qd_ref, cd_ref, y_ref, st_ref, s_scr):
        @pl.when(pl.program_id(0) == 0)
        def _():
            s_scr[...] = jnp.zeros_like(s_scr)

        heads = range(N_HEADS)
        states = tuple(s_scr[h] for h in heads)
        ys, new_states = _ret_fn(*(tuple(r[:, _head(h)] for h in heads) for r in (q_ref, k_ref, v_ref, g_ref)), states,
                                 cos_ref[...], sin_ref[...], *(tuple(r[h] for h in heads) for r in (dm_ref, kt_ref, qd_ref, cd_ref)))
        for h in heads:
            st_ref[h] = states[h]
            y_ref[:, _head(h)] = ys[h].astype(BF16)
            s_scr[h] = new_states[h]

    return pl.pallas_call(
        body, out_shape=(_sds((SEQ, 2 * GROUP), BF16), _sds((N_HEADS, nc, HEAD, HEAD), F32)), grid=(nc,),
        in_specs=[group(0), group(1), group(2), group(3), tab, tab, const, const, const, const],
        out_specs=(group(0), state), scratch_shapes=[pltpu.VMEM((N_HEADS, HEAD, HEAD), F32)], compiler_params=_params(), name=name,
    )(p, p, p, p, *tables)


def ret_bwd(p, tables, states, dy, *, name):
    group, tab, const, state, nc = _ret_specs(True)

    def body(q_ref, k_ref, v_ref, g_ref, cos_ref, sin_ref, dm_ref, kt_ref, qd_ref, cd_ref, st_ref, dy_ref,
             dq_ref, dk_ref, dv_ref, dg_ref, ds_scr):
        @pl.when(pl.program_id(0) == 0)
        def _():
            ds_scr[...] = jnp.zeros_like(ds_scr)

        heads = range(N_HEADS)
        consts = (cos_ref[...], sin_ref[...], *(tuple(r[h] for h in heads) for r in (dm_ref, kt_ref, qd_ref, cd_ref)))
        _, vjp = jax.vjp(lambda *a: _ret_fn(*a, *consts), *(tuple(r[:, _head(h)] for h in heads) for r in (q_ref, k_ref, v_ref, g_ref)),
                         tuple(st_ref[h] for h in heads))
        dqs, dks, dvs, dgs, dss = vjp((tuple(dy_ref[:, _head(h)] for h in heads), tuple(ds_scr[h] for h in heads)))
        for h in heads:
            dq_ref[:, _head(h)] = dqs[h].astype(BF16)
            dk_ref[:, _head(h)] = dks[h].astype(BF16)
            dv_ref[:, _head(h)] = dvs[h].astype(BF16)
            dg_ref[:, _head(h)] = dgs[h].astype(BF16)
            ds_scr[h] = dss[h]

    out = _sds((SEQ, GROUP), BF16)
    return pl.pallas_call(
        body, out_shape=(out, out, out, out), grid=(nc,),
        in_specs=[group(0), group(1), group(2), group(3), tab, tab, const, const, const, const, state, group(0)],
        out_specs=(group(0), group(0), group(0), group(0)), scratch_shapes=[pltpu.VMEM((N_HEADS, HEAD, HEAD), F32)],
        compiler_params=_params(), name=name,
    )(p, p, p, p, *tables, states, dy)


def _gdn_specs(rev):
    c = GDN_CHUNK * GDN_CHUNKS_PER_STEP
    nc = SEQ // c

    def n_of(n):
        return nc - 1 - n if rev else n

    def group(off):
        return pl.BlockSpec((c, GROUP), lambda n: (n_of(n), off))

    small = pl.BlockSpec((c, LANES), lambda n: (n_of(n), 0))
    vec = pl.BlockSpec((1, LANES), lambda n: (0, 0))
    state = pl.BlockSpec((N_HEADS, None, HEAD, HEAD), lambda n: (0, n_of(n), 0, 0))
    qkv = pl.BlockSpec((c, 3 * GROUP), lambda n: (n_of(n), 0))
    return group, small, vec, state, qkv, nc


GDN_GATE_GROUP = 7


def gdn_fwd(conv, p, small, a_log, dt_bias, gain, y_started, *, name):
    group, sm, vec, state, _, nc = _gdn_specs(False)

    def body(q_ref, k_ref, v_ref, g_ref, sm_ref, al_ref, dt_ref, gn_ref, _, y_ref, st_ref, s_scr):
        @pl.when(pl.program_id(0) == 0)
        def _():
            s_scr[...] = jnp.zeros_like(s_scr)

        states = tuple(s_scr[h] for h in range(N_HEADS))
        ys, new_states = _gdn_fn(*(tuple(r[:, _head(h)] for h in range(N_HEADS)) for r in (q_ref, k_ref, v_ref, g_ref)),
                                 sm_ref[...], al_ref[...], dt_ref[...], gn_ref[...], states)
        for h in range(N_HEADS):
            st_ref[h] = states[h]
            y_ref[:, _head(h)] = ys[h].astype(BF16)
            s_scr[h] = new_states[h]

    return pl.pallas_call(
        body, out_shape=(_sds((SEQ, 2 * GROUP), BF16), _sds((N_HEADS, nc, HEAD, HEAD), F32)), grid=(nc,),
        in_specs=[group(0), group(1), group(2), group(GDN_GATE_GROUP), sm, vec, vec, vec, pl.BlockSpec(memory_space=pl.ANY)],
        out_specs=(group(1), state), input_output_aliases={8: 0},
        scratch_shapes=[pltpu.VMEM((N_HEADS, HEAD, HEAD), F32)], compiler_params=_params(), name=name,
    )(conv, conv, conv, p, small, a_log, dt_bias, gain, y_started)


def gdn_bwd(conv, p, small, a_log, dt_bias, gain, states, dy, *, name):
    group, sm, vec, state, qkv, nc = _gdn_specs(True)

    def body(q_ref, k_ref, v_ref, g_ref, sm_ref, al_ref, dt_ref, gn_ref, st_ref, dy_ref,
             dqkv_ref, dg_ref, dsm_ref, dal_ref, ddt_ref, dgn_ref, ds_scr):
        @pl.when(pl.program_id(0) == 0)
        def _():
            ds_scr[...] = jnp.zeros_like(ds_scr)
            dal_ref[...] = jnp.zeros_like(dal_ref)
            ddt_ref[...] = jnp.zeros_like(ddt_ref)
            dgn_ref[...] = jnp.zeros_like(dgn_ref)

        per_head = tuple(tuple(r[:, _head(h)] for h in range(N_HEADS)) for r in (q_ref, k_ref, v_ref, g_ref))
        _, vjp = jax.vjp(_gdn_fn, *per_head, sm_ref[...], al_ref[...], dt_ref[...], gn_ref[...],
                         tuple(st_ref[h] for h in range(N_HEADS)))
        cts = (tuple(dy_ref[:, _head(h)] for h in range(N_HEADS)), tuple(ds_scr[h] for h in range(N_HEADS)))
        dqs, dks, dvs, dgs, dsm, dal, ddt, dgn, dss = vjp(cts)
        for h in range(N_HEADS):
            for part, blocks in enumerate((dqs, dks, dvs)):
                dqkv_ref[:, part * GROUP + h * HEAD:part * GROUP + (h + 1) * HEAD] = blocks[h]
            dg_ref[:, _head(h)] = dgs[h].astype(BF16)
            ds_scr[h] = dss[h]
        dsm_ref[...] = dsm
        dal_ref[...] += dal
        ddt_ref[...] += ddt
        dgn_ref[...] += dgn

    pv = _sds((1, LANES), F32)
    return pl.pallas_call(
        body, out_shape=(_sds((SEQ, 3 * GROUP), F32), _sds((SEQ, GROUP), BF16), _sds((SEQ, LANES), F32), pv, pv, pv), grid=(nc,),
        in_specs=[group(0), group(1), group(2), group(GDN_GATE_GROUP), sm, vec, vec, vec, state, group(1)],
        out_specs=(qkv, group(0), sm, vec, vec, vec), scratch_shapes=[pltpu.VMEM((N_HEADS, HEAD, HEAD), F32)],
        compiler_params=_params(), name=name,
    )(conv, conv, conv, p, small, a_log, dt_bias, gain, states, dy)


ELEMENTWISE_BLOCK_BYTES = 2 * 1024 * 1024


def _row_tile(r, c):
    best = None
    for tr in range(8, r + 1, 8):
        if r % tr == 0 and tr * c * 4 <= ELEMENTWISE_BLOCK_BYTES:
            best = tr
    if best is None:
        raise ValueError(f"no row tile for ({r}, {c})")
    return best


def _tile_2d(r, c):
    if any(r % tr == 0 for tr in range(8, r + 1, 8)):
        return _row_tile(r, c), c
    tc = max(t for t in range(LANES, c + 1, LANES) if c % t == 0 and r * t * 4 <= ELEMENTWISE_BLOCK_BYTES)
    return r, tc


def _core_index():
    return lax.axis_index("c").astype(jnp.int32).reshape(1)


def _chip_index():
    return (2 * lax.axis_index("x") + lax.axis_index("y")).astype(jnp.int32).reshape(1)


def adamw_halves(w, m, v, g_own, g_sib, *, layer=0, prev=None, name):
    n_layers, rows, c = w.shape
    r = rows // 2
    tr = _row_tile(r, c)
    nb = r // tr

    def body(c_ref, w_ref, m_ref, v_ref, own_ref, sib_ref, *rest):
        g_ref, d_ref, nm_ref, nv_ref = rest[-4:]
        gg = jnp.where(pl.program_id(0) == c_ref[0], own_ref[...], sib_ref[...])
        nm = ADAM_B1 * m_ref[...] + (1.0 - ADAM_B1) * gg
        nv = ADAM_B2 * v_ref[...] + (1.0 - ADAM_B2) * jnp.square(gg)
        m_hat = nm / (1.0 - ADAM_B1 ** ADAM_STEP)
        v_hat = nv / (1.0 - ADAM_B2 ** ADAM_STEP)
        g_ref[...] = gg
        d_ref[...] = -ADAM_LR * (m_hat / (jnp.sqrt(v_hat) + ADAM_EPS) + ADAM_WD * w_ref[...])
        nm_ref[...] = nm
        nv_ref[...] = nv

    full = pl.BlockSpec((None, tr, c), lambda h, i, cr: (layer, h * nb + i, 0))
    half = pl.BlockSpec((tr, c), lambda h, i, cr: (i, 0))
    o = _sds((n_layers, rows, c), F32)
    prev = list(prev or ())
    gs = pltpu.PrefetchScalarGridSpec(num_scalar_prefetch=1, grid=(2, nb), in_specs=[full, full, full, half, half] + [_ANY] * len(prev),
                                      out_specs=(full, full, full, full))
    n_fixed = 6
    return pl.pallas_call(body, out_shape=(o, o, o, o), grid_spec=gs, compiler_params=_params(), name=name,
                          input_output_aliases={n_fixed + k: k for k in range(len(prev))})(
        _core_index(), w, m, v, g_own, g_sib, *prev)


ADAMW_ROW_STEPS = 6


def adamw_rows(w, g, m, v, *, name):
    rows, _, cols = w.shape
    tr = rows // ADAMW_ROW_STEPS

    def body(w_ref, g_ref, m_ref, v_ref, g_out_ref, d_ref, nm_ref, nv_ref):
        gg = g_ref[...]
        nm = ADAM_B1 * m_ref[...] + (1.0 - ADAM_B1) * gg
        nv = ADAM_B2 * v_ref[...] + (1.0 - ADAM_B2) * jnp.square(gg)
        m_hat = nm / (1.0 - ADAM_B1 ** ADAM_STEP)
        v_hat = nv / (1.0 - ADAM_B2 ** ADAM_STEP)
        g_out_ref[...] = gg
        d_ref[...] = -ADAM_LR * (m_hat / (jnp.sqrt(v_hat) + ADAM_EPS) + ADAM_WD * w_ref[...])
        nm_ref[...] = nm
        nv_ref[...] = nv

    blk = pl.BlockSpec((tr, 1, cols), lambda i: (i, 0, 0))
    o = _sds(w.shape, F32)
    return pl.pallas_call(body, out_shape=(o, o, o, o), grid=(ADAMW_ROW_STEPS,), in_specs=[blk] * 4, out_specs=(blk, blk, blk, blk),
                          compiler_params=_params(), name=name)(w, g, m, v)


def adamw_many(ws, gs, ms, vs, *, name):
    n = len(ws)

    def body(*refs):
        w_refs, g_refs, m_refs, v_refs, d_refs, nm_refs, nv_refs = (refs[k * n:(k + 1) * n] for k in range(7))
        for i in range(n):
            gg = g_refs[i][...]
            nm = ADAM_B1 * m_refs[i][...] + (1.0 - ADAM_B1) * gg
            nv = ADAM_B2 * v_refs[i][...] + (1.0 - ADAM_B2) * jnp.square(gg)
            m_hat = nm / (1.0 - ADAM_B1 ** ADAM_STEP)
            v_hat = nv / (1.0 - ADAM_B2 ** ADAM_STEP)
            d_refs[i][...] = -ADAM_LR * (m_hat / (jnp.sqrt(v_hat) + ADAM_EPS) + ADAM_WD * w_refs[i][...])
            nm_refs[i][...] = nm
            nv_refs[i][...] = nv

    outs = pl.pallas_call(body, out_shape=[_sds(w.shape, F32) for w in ws] * 3, compiler_params=_params(), name=name)(*ws, *gs, *ms, *vs)
    return outs[:n], outs[n:2 * n], outs[2 * n:]


def add_core_halves(g2, land, *, out_dtype, name):
    _, ns, r, cols = g2.shape
    tr, tc = _tile_2d(r, cols)

    def body(c_ref, a_ref, b_ref, o_ref):
        o_ref[...] = (a_ref[...] + b_ref[...]).astype(out_dtype)

    gs = pltpu.PrefetchScalarGridSpec(
        num_scalar_prefetch=1, grid=(ns, r // tr, cols // tc),
        in_specs=[pl.BlockSpec((None, None, tr, tc), lambda s, i, j, cr: (cr[0], s, i, j)),
                  pl.BlockSpec((None, tr, tc), lambda s, i, j, cr: (s, i, j))],
        out_specs=pl.BlockSpec((None, tr, tc), lambda s, i, j, cr: (s, i, j)))
    return pl.pallas_call(body, out_shape=_sds((ns, r, cols), out_dtype), grid_spec=gs, compiler_params=_params(), name=name)(
        _core_index(), g2, land)


def sum_over_chips(own, land, *, scatter, name):
    _, r, cols = own.shape
    tr, tc = _tile_2d(r, cols)

    def body(mine_ref, own_ref, l0, l1, l2, l3, o_ref):
        mine = mine_ref[0]
        mine_val = own_ref[...]
        acc = None
        for s, l_ref in enumerate((l0, l1, l2, l3)):
            val = jnp.where(mine == s, mine_val, l_ref[...]).astype(F32)
            acc = val if acc is None else acc + val
        o_ref[...] = acc

    def slot(s):
        return pl.BlockSpec((None, tr, tc), lambda i, j, mr: (jnp.where(mr[0] == s, (s + 1) % N_SHARD, s), i, j))

    own_spec = pl.BlockSpec((None, tr, tc), lambda i, j, mr: (mr[0] if scatter else 0, i, j))
    gs = pltpu.PrefetchScalarGridSpec(num_scalar_prefetch=1, grid=(r // tr, cols // tc), in_specs=[own_spec] + [slot(s) for s in range(N_SHARD)],
                                      out_specs=pl.BlockSpec((tr, tc), lambda i, j, mr: (i, j)))
    return pl.pallas_call(body, out_shape=_sds((r, cols), F32), grid_spec=gs, compiler_params=_params(), name=name)(
        _chip_index(), own, land, land, land, land)


_ANY = pl.BlockSpec(memory_space=pl.ANY)


def xy_exchange(src, *, name):
    rh = src.shape[1]

    def body(src_ref, land_ref, send_sems, recv_sems, loc_sem):
        x, y, c = lax.axis_index("x"), lax.axis_index("y"), lax.axis_index("c")
        mine = 2 * x + y
        peers = [(1 - x, y), (x, 1 - y), (1 - x, 1 - y)]

        def copy(k, px, py, dst_slot):
            return pltpu.make_async_remote_copy(src_ref=src_ref.at[c], dst_ref=land_ref.at[dst_slot], send_sem=send_sems.at[k],
                                                recv_sem=recv_sems.at[k], device_id=(px, py, c), device_id_type=MESH)

        keep = pltpu.make_async_copy(src_ref.at[c], land_ref.at[mine], loc_sem)
        keep.start()
        sends = [copy(k, px, py, mine) for k, (px, py) in enumerate(peers)]
        for cp in sends:
            cp.start()
        for cp in sends:
            cp.wait_send()
        for k, (px, py) in enumerate(peers):
            copy(k, px, py, 2 * px + py).wait_recv()
        keep.wait()

    return pl.pallas_call(body, out_shape=_sds((N_SHARD, rh, LANES), src.dtype), in_specs=[_ANY], out_specs=_ANY,
                          scratch_shapes=[pltpu.SemaphoreType.DMA((3,)), pltpu.SemaphoreType.DMA((3,)), pltpu.SemaphoreType.DMA(())],
                          name=name)(src)


def core_exchange(src, *, name):
    def body(src_ref, out_ref, send_sem, recv_sem, loc_sem):
        x, y, c = lax.axis_index("x"), lax.axis_index("y"), lax.axis_index("c")
        keep = pltpu.make_async_copy(src_ref, out_ref.at[c], loc_sem)
        keep.start()
        cp = pltpu.make_async_remote_copy(src_ref=src_ref, dst_ref=out_ref.at[c], send_sem=send_sem, recv_sem=recv_sem,
                                          device_id=(x, y, 1 - c), device_id_type=MESH)
        cp.start()
        cp.wait_send()
        pltpu.make_async_remote_copy(src_ref=src_ref, dst_ref=out_ref.at[1 - c], send_sem=send_sem, recv_sem=recv_sem,
                                     device_id=(x, y, 1 - c), device_id_type=MESH).wait_recv()
        keep.wait()

    return pl.pallas_call(body, out_shape=_sds((2,) + src.shape, src.dtype), in_specs=[_ANY], out_specs=_ANY,
                          scratch_shapes=[pltpu.SemaphoreType.DMA(()), pltpu.SemaphoreType.DMA(()), pltpu.SemaphoreType.DMA(())],
                          name=name)(src)


def _sequencer_call(body, ins, out_shapes, sem_counts, name, collective_id):
    return pl.kernel(body, out_type=list(out_shapes), mesh=plsc.ScalarSubcoreMesh(axis_name="sequencer", num_cores=1), name=name,
                     scratch_types=[pltpu.SemaphoreType.DMA((k,)) for k in sem_counts],
                     compiler_params=pltpu.CompilerParams(collective_id=collective_id))(*ins)


def _handshake(peers):
    barrier = pltpu.get_barrier_semaphore()
    for peer in peers:
        pl.semaphore_signal(barrier, inc=1, device_id=peer, device_id_type=MESH)
    pl.semaphore_wait(barrier, len(peers))


def _xy_peers(x, y):
    return [(1 - x, y), (x, 1 - y), (1 - x, 1 - y)]


def gather_halves(halves, *, name, collective_id):
    n = len(halves)

    def body(*refs):
        ins, lands, sibs = refs[:n], refs[n:2 * n], refs[2 * n:3 * n]
        ici_send, ici_recv, d2d_send, d2d_recv = refs[3 * n:]
        x, y, c = lax.axis_index("x"), lax.axis_index("y"), lax.axis_index("c")
        mine = 2 * x + y
        peers = _xy_peers(x, y)
        _handshake([(px, py, c) for px, py in peers] + [(x, y, 1 - c)])

        def ici(i, k, slot):
            px, py = peers[k]
            return pltpu.make_async_remote_copy(src_ref=ins[i].at[c], dst_ref=lands[i].at[slot], send_sem=ici_send.at[3 * i + k],
                                                recv_sem=ici_recv.at[3 * i + k], device_id=(px, py, c), device_id_type=MESH)

        def pass_on(i, k):
            px, py = peers[k]
            slot = 2 * px + py
            return pltpu.make_async_remote_copy(src_ref=lands[i].at[slot], dst_ref=sibs[i].at[slot], send_sem=d2d_send.at[3 * i + k],
                                                recv_sem=d2d_recv.at[3 * i + k], device_id=(x, y, 1 - c), device_id_type=MESH)

        sends = [ici(i, k, mine) for i in range(n) for k in range(3)]
        for cp in sends:
            cp.start()
        passed = []
        for i in range(n):
            for k in range(3):
                px, py = peers[k]
                ici(i, k, 2 * px + py).wait_recv()
                cp = pass_on(i, k)
                cp.start()
                passed.append(cp)
        for cp in passed:
            cp.wait_recv()
        for cp in sends + passed:
            cp.wait_send()

    outs = [_sds((N_SHARD,) + h.shape[1:], h.dtype) for h in halves]
    res = _sequencer_call(body, halves, outs + outs, [3 * n] * 4, name, collective_id)
    return res[:n], res[n:]


def send_other_half(arrays, *, name, collective_id):
    n = len(arrays)

    def body(*refs):
        ins, lands = refs[:n], refs[n:2 * n]
        send_sems, recv_sems = refs[2 * n:]
        x, y, c = lax.axis_index("x"), lax.axis_index("y"), lax.axis_index("c")
        _handshake([(x, y, 1 - c)])
        copies = [pltpu.make_async_remote_copy(src_ref=ins[i].at[1 - c], dst_ref=lands[i], send_sem=send_sems.at[i],
                                               recv_sem=recv_sems.at[i], device_id=(x, y, 1 - c), device_id_type=MESH) for i in range(n)]
        for cp in copies:
            cp.start()
        for cp in copies:
            cp.wait_recv()
        for cp in copies:
            cp.wait_send()

    return _sequencer_call(body, arrays, [_sds(a.shape[1:], a.dtype) for a in arrays], [n, n], name, collective_id)


_HBM = pl.BlockSpec(memory_space=pltpu.HBM)
_SEM = pl.BlockSpec(memory_space=pltpu.SEMAPHORE)
_SPLIT_COPY = dict(has_side_effects=pltpu.SideEffectType.DATAFLOW_SIDE_EFFECTING)


def _chip_copy(ins, lands, send_sems, recv_sems, scatter, i, k, receive):
    x, y, c = lax.axis_index("x"), lax.axis_index("y"), lax.axis_index("c")
    px, py = _xy_peers(x, y)[k]
    theirs, mine = 2 * px + py, 2 * x + y
    src = ins[i].at[theirs] if scatter[i] else ins[i].at[0]
    return pltpu.make_async_remote_copy(src_ref=src, dst_ref=lands[i].at[theirs if receive else mine], send_sem=send_sems.at[3 * i + k],
                                        recv_sem=recv_sems.at[3 * i + k], device_id=(px, py, c), device_id_type=MESH)


def send_to_chips_start(arrays, scatter, *, name):
    n = len(arrays)

    def body(*refs):
        send_sems, recv_sems = refs[2 * n], refs[2 * n + 1]
        ins, lands = refs[2 * n + 2:3 * n + 2], refs[3 * n + 2:4 * n + 2]
        token = refs[4 * n + 2]
        for i in range(n):
            for k in range(3):
                _chip_copy(ins, lands, send_sems, recv_sems, scatter, i, k, receive=False).start()
        token[...] = jnp.zeros_like(token)

    land_shapes = [(N_SHARD,) + a.shape[1:] for a in arrays]
    operands = [pltpu.with_memory_space_constraint(a, pltpu.HBM) for a in arrays]
    operands += [pltpu.with_memory_space_constraint(lax.empty(s, a.dtype), pltpu.HBM) for s, a in zip(land_shapes, arrays)]
    out_shape = ([pltpu.SemaphoreType.DMA((3 * n,)), pltpu.SemaphoreType.DMA((3 * n,))] + [pltpu.HBM(a.shape, a.dtype) for a in arrays]
                 + [pltpu.HBM(s, a.dtype) for s, a in zip(land_shapes, arrays)] + [_sds((8, LANES), F32)])
    res = pl.pallas_call(body, name=name, out_shape=out_shape, in_specs=[_HBM] * (2 * n),
                         out_specs=[_SEM, _SEM] + [_HBM] * (2 * n) + [pl.BlockSpec(memory_space=pltpu.VMEM)],
                         input_output_aliases={i: 2 + i for i in range(2 * n)}, compiler_params=pltpu.CompilerParams(**_SPLIT_COPY))(*operands)
    return (res[0], res[1], res[2:2 + n], res[2 + n:2 + 2 * n], scatter), res[-1]


def send_to_chips_wait(state, after, *, name):
    send_sems, recv_sems, arrays, lands, scatter = state
    n = len(arrays)

    def body(*refs):
        ins, landing = refs[:n], refs[n:2 * n]
        send_sems, recv_sems = refs[2 * n], refs[2 * n + 1]
        for i in range(n):
            for k in range(3):
                _chip_copy(ins, landing, send_sems, recv_sems, scatter, i, k, receive=True).wait_recv()
        for i in range(n):
            for k in range(3):
                _chip_copy(ins, landing, send_sems, recv_sems, scatter, i, k, receive=False).wait_send()

    out_shape = [pltpu.HBM(a.shape, a.dtype) for a in list(arrays) + list(lands)]
    res = pl.pallas_call(body, name=name, out_shape=out_shape, in_specs=[_HBM] * (2 * n) + [_SEM, _SEM] + [_ANY] * len(after),
                         out_specs=[_HBM] * (2 * n), input_output_aliases={i: i for i in range(2 * n)},
                         compiler_params=pltpu.CompilerParams(**_SPLIT_COPY))(*arrays, *lands, send_sems, recv_sems, *after)
    return res[:n], res[n:]


def swap_with_other_core(arrays, *, name, collective_id):
    n = len(arrays)

    def body(*refs):
        ins, lands = refs[:n], refs[n:2 * n]
        send_sems, recv_sems = refs[2 * n:]
        x, y, c = lax.axis_index("x"), lax.axis_index("y"), lax.axis_index("c")
        _handshake([(x, y, 1 - c)])
        copies = [pltpu.make_async_remote_copy(src_ref=ins[i], dst_ref=lands[i], send_sem=send_sems.at[i], recv_sem=recv_sems.at[i],
                                               device_id=(x, y, 1 - c), device_id_type=MESH) for i in range(n)]
        for cp in copies:
            cp.start()
        for cp in copies:
            cp.wait_recv()
        for cp in copies:
            cp.wait_send()

    return _sequencer_call(body, arrays, [_sds(a.shape, a.dtype) for a in arrays], [n, n], name, collective_id)


def _pack_rows(n_elems, row_multiple):
    rows = -(-n_elems // LANES)
    return -(-rows // row_multiple) * row_multiple


def _pack(arrays, rows, dtype):
    flat = jnp.concatenate([a.reshape(-1).astype(dtype) for a in arrays])
    return jnp.pad(flat, (0, rows * LANES - flat.shape[0])).reshape(rows, LANES)


def _unpack(packed, shapes):
    flat = packed.reshape(-1)
    out, off = [], 0
    for s in shapes:
        n = int(np.prod(s))
        out.append(flat[off:off + n].reshape(s))
        off += n
    return out


def all_gather_shards(shards, axes, dtype, row_multiple, tag):
    shapes = [s.shape for s in shards]
    rows = _pack_rows(sum(int(np.prod(s)) for s in shapes), row_multiple)
    packed = _pack(shards, rows, dtype).reshape(2, rows // 2, LANES)
    land = xy_exchange(packed, name=f"gather_xy_{tag}")
    both = core_exchange(land, name=f"gather_c_{tag}")
    per_shard = jnp.swapaxes(both, 0, 1).reshape(N_SHARD, rows, LANES)
    pieces = [_unpack(per_shard[s], shapes) for s in range(N_SHARD)]
    return [jnp.concatenate([pieces[s][i] for s in range(N_SHARD)], axis=ax) for i, ax in enumerate(axes)]


def _ordered_before(first, then):
    if then is None:
        return first, None
    return lax.optimization_barrier((first, then))


def reduce_between_cores(arrays, scatter, *, tag, collective_id, before=None):
    arrays, before = _ordered_before(arrays, before)
    land = send_other_half(arrays, name=f"reduce_core_send_{tag}", collective_id=collective_id)
    return (arrays, land, scatter, tag, collective_id), before


def reduce_between_chips(state, before=None):
    arrays, land, scatter, tag, collective_id = state
    chip = [add_core_halves(a, l, out_dtype=BF16 if sc else F32, name=f"reduce_core_add_{tag}_{i}")
            for i, (a, l, sc) in enumerate(zip(arrays, land, scatter))]
    sending, token = send_to_chips_start(chip, scatter, name=f"reduce_chip_start_{tag}")
    token, before = _ordered_before(token, before)
    return (sending, token, scatter, tag, collective_id), before


def reduce_finish(state, after):
    sending, token, scatter, tag, collective_id = state
    chip, land = send_to_chips_wait(sending, tuple(after) + (token,), name=f"reduce_chip_wait_{tag}")
    own = [sum_over_chips(ch, l, scatter=sc, name=f"reduce_chip_add_{tag}_{i}") for i, (ch, l, sc) in enumerate(zip(chip, land, scatter))]
    sib = swap_with_other_core(own, name=f"reduce_core_swap_{tag}", collective_id=collective_id + 2)
    return own, sib


def _ffn_layer_fwd(h, norm_g, w_up, cw, cb, w_down, tag):
    u, hn = norm_matmul(h, norm_g, w_up, name=f"ffn_up_{tag}")
    act = ffn_act_fwd(u, cw, cb, name=f"ffn_act_{tag}")
    out = matmul(act, w_down, add=h, name=f"ffn_down_{tag}")
    return out, (h, hn, u, act)


def _travel_layout(array):
    return BIG_ARRAYS[array][3], BIG_ARRAYS[array][4]


def _ffn_layer_bwd(saved, dout, norm_g, w_up, cw, cb, w_down, tag):
    h, hn, u, act = saved
    dact = matmul(dout, w_down, tb=True, name=f"ffn_down_dx_{tag}")
    d_w_down = matmul(act, dout, ta=True, split=_travel_layout(f"ffn_w_down_{tag}"), name=f"ffn_down_dw_{tag}")
    du, dcw, dcb = ffn_act_bwd(u, cw, cb, dact, name=f"ffn_act_bwd_{tag}")
    dhn = matmul(du, w_up, tb=True, name=f"ffn_up_dx_{tag}")
    d_w_up = matmul(hn, du, ta=True, split=_travel_layout(f"ffn_w_up_{tag}"), name=f"ffn_up_dw_{tag}")
    dh, dg = norm_bwd(h, norm_g, dhn, dout, name=f"ffn_norm_bwd_{tag}")
    return dh, dg, d_w_up, dcw, dcb, d_w_down


def local_step(x, target, w, stage=lambda name, tensors, grads=None: tensors):
    g = {}
    tables = _ret_tables()
    x = stage("start", x)
    w_in_t = w["ret_gdn_w_in"]
    w_main = w_in_t[:MIX_MAIN]
    w_small = jnp.pad(w_in_t[MIX_MAIN:], ((0, LANES - 2 * N_HEADS), (0, 0)))
    a_log = jnp.pad(w["gdn_a_log"], ((0, 0), (0, LANES - N_HEADS)))
    dt_bias = jnp.pad(w["gdn_dt_bias"], ((0, 0), (0, LANES - N_HEADS)))

    p, hn0 = norm_matmul(x, w["norm_mix"][0:1], w_main, tb=True, name="mix0_in")
    hn0 = stage("normed", hn0)
    small = matmul(hn0, w_small, tb=True, name="mix0_in_small")
    y_ret, s_ret = ret_fwd(p, tables, name="ret_fwd")
    conv = gdn_conv_fwd(p, w["gdn_conv_w"], name="gdn_conv")
    y0, s_gdn = gdn_fwd(conv, p, small, a_log, dt_bias, w["gdn_out_gain"], y_ret, name="gdn_fwd")
    y0 = stage("mixed", y0)
    h1 = matmul(y0, w["ret_gdn_w_out"], add=x, name="mix0_out")
    h2, ffn0 = _ffn_layer_fwd(h1, w["norm_ffn"][0:1], w["ffn_w_up"][0], w["ffn_conv_w"][0], w["ffn_conv_b"][0:1], w["ffn_w_down"][0], "0")
    h2 = stage("layer0", h2)

    gx, hn1 = norm_matmul(h2, w["norm_mix"][1:2], w["lru_w_in"], name="mix1_in")
    lru_p = (w["lru_conv_w"], w["lru_conv_b"], w["lru_w_a"], w["lru_b_a"], w["lru_w_x"], w["lru_b_x"], w["lru_lambda"])
    y1 = lru_fwd(gx, *lru_p, name="lru_fwd")
    h3 = stage("mixed1", matmul(y1, w["lru_w_out"], add=h2, name="mix1_out"))
    h4, ffn1 = _ffn_layer_fwd(h3, w["norm_ffn"][1:2], w["ffn_w_up"][1], w["ffn_conv_w"][1], w["ffn_conv_b"][1:2], w["ffn_w_down"][1], "1")

    loss, dh4, g["norm_final"] = final_fwd_bwd(h4, w["norm_final"], target, name="final")

    dh3, dgf1, dwu1, dcw1, dcb1, dwd1 = _ffn_layer_bwd(ffn1, dh4, w["norm_ffn"][1:2], w["ffn_w_up"][1], w["ffn_conv_w"][1],
                                                     w["ffn_conv_b"][1:2], w["ffn_w_down"][1], "1")
    g["ffn_w_up_1"], g["ffn_w_down_1"] = dwu1, dwd1
    dh3 = stage("grads0_ready", dh3, g)
    dy1 = matmul(dh3, w["lru_w_out"], tb=True, name="mix1_out_dx")
    g["lru_w_out"] = matmul(y1, dh3, ta=True, split=_travel_layout("lru_w_out"), name="mix1_out_dw")
    dgx, g["lru_conv_w"], g["lru_conv_b"], g["lru_w_a"], g["lru_b_a"], g["lru_w_x"], g["lru_b_x"], g["lru_lambda"] = lru_bwd(
        gx, *lru_p, dy1, name="lru_bwd")
    dgx = stage("grads0_send", dgx, g)
    dhn1 = matmul(dgx, w["lru_w_in"], tb=True, name="mix1_in_dx")
    g["lru_w_in"] = matmul(hn1, dgx, ta=True, split=_travel_layout("lru_w_in"), name="mix1_in_dw")
    dh2, dgm1 = norm_bwd(h2, w["norm_mix"][1:2], dhn1, dh3, name="mix1_norm_bwd")
    dh2 = stage("grads1_ready", dh2, g)

    dh1, dgf0, dwu0, dcw0, dcb0, dwd0 = _ffn_layer_bwd(ffn0, dh2, w["norm_ffn"][0:1], w["ffn_w_up"][0], w["ffn_conv_w"][0],
                                                     w["ffn_conv_b"][0:1], w["ffn_w_down"][0], "0")
    g["ffn_w_up_0"], g["ffn_w_down_0"] = dwu0, dwd0
    dh1 = stage("grads2_ready", stage("grads1_send", dh1, g), g)
    dy0 = matmul(dh1, w["ret_gdn_w_out"], tb=True, name="mix0_out_dx")
    g["ret_gdn_w_out"] = matmul(y0, dh1, ta=True, split=_travel_layout("ret_gdn_w_out"), name="mix0_out_dw")
    dq_r, dk_r, dv_r, dg_r = ret_bwd(p, tables, s_ret, dy0, name="ret_bwd")
    dy0, dq_r = stage("grads2_send", (dy0, dq_r), g)
    dconv, dg_d, dsmall, dal, ddt, dgain = gdn_bwd(conv, p, small, a_log, dt_bias, w["gdn_out_gain"], s_gdn, dy0, name="gdn_bwd")
    dp_conv, g["gdn_conv_w"] = gdn_conv_bwd(p, w["gdn_conv_w"], dconv, name="gdn_conv_bwd")
    dp = jnp.concatenate([dq_r, dk_r, dv_r, dg_r, dp_conv, dg_d], axis=1)
    dhn0 = matmul(dp, w_main, more=(dsmall, w_small), name="mix0_in_dx")
    d_w_in = matmul(dp, hn0, ta=True, column_halves=(MIX_IN, 0, None, None), name="mix0_in_dw")
    d_w_in = matmul(dsmall, hn0, ta=True, column_halves=(MIX_IN, MIX_MAIN, 2 * N_HEADS, d_w_in), name="mix0_in_small_dw")
    g["ret_gdn_w_in"] = d_w_in.reshape(2, N_SHARD, MIX_IN // N_SHARD, D_MODEL // 2)
    dx, dgm0 = norm_bwd(x, w["norm_mix"][0:1], dhn0, dh1, name="mix0_norm_bwd")

    g["gdn_a_log"] = dal[:, :N_HEADS]
    g["gdn_dt_bias"] = ddt[:, :N_HEADS]
    g["gdn_out_gain"] = dgain
    g["norm_mix"] = jnp.concatenate([dgm0, dgm1], axis=0)
    g["norm_ffn"] = jnp.concatenate([dgf0, dgf1], axis=0)
    g["ffn_conv_w"] = jnp.stack([dcw0, dcw1])
    g["ffn_conv_b"] = jnp.concatenate([dcb0, dcb1], axis=0)
    return loss, dx, g


WEIGHTS = ("norm_mix", "norm_ffn", "ret_gdn_w_in", "gdn_conv_w", "gdn_a_log", "gdn_dt_bias", "gdn_out_gain", "ret_gdn_w_out",
           "lru_w_in", "lru_conv_w", "lru_conv_b", "lru_w_a", "lru_b_a", "lru_w_x", "lru_b_x", "lru_lambda", "lru_w_out",
           "ffn_w_up", "ffn_conv_w", "ffn_conv_b", "ffn_w_down", "norm_final")
MATMUL_SHARDED = {"ret_gdn_w_in": 1, "ret_gdn_w_out": 0, "lru_w_in": 1, "lru_w_out": 0, "ffn_w_up": 2, "ffn_w_down": 1}
VECTOR_SHARDED = {"gdn_conv_w": 1, "lru_conv_w": 1, "lru_conv_b": 1, "lru_b_a": 1, "lru_b_x": 1, "lru_lambda": 1, "ffn_conv_w": 2}
SHARDED = {**MATMUL_SHARDED, **VECTOR_SHARDED}
REPLICATED = tuple(n for n in WEIGHTS if n not in SHARDED)
SQUEEZE = {"ret_gdn_w_in", "gdn_conv_w", "ret_gdn_w_out", "lru_w_in", "lru_conv_w", "lru_w_a", "lru_w_x", "lru_w_out"}
MIX_IN = MIX_MAIN + 2 * N_HEADS
BIG_ARRAYS = {
    "ret_gdn_w_in": ("ret_gdn_w_in", None, (MIX_IN, D_MODEL), (N_SHARD, MIX_IN // N_SHARD, 2, D_MODEL // 2), (2, 0, 1, 3)),
    "ret_gdn_w_out": ("ret_gdn_w_out", None, (2 * GROUP, D_MODEL), (N_SHARD, 2, GROUP // N_SHARD, D_MODEL), (1, 0, 2, 3)),
    "lru_w_in": ("lru_w_in", None, (D_MODEL, 2 * D_MODEL), (2, D_MODEL // 2, N_SHARD, 2 * D_MODEL // N_SHARD), (0, 2, 1, 3)),
    "lru_w_out": ("lru_w_out", None, (D_MODEL, D_MODEL), (N_SHARD, 2, D_MODEL // (2 * N_SHARD), D_MODEL), (1, 0, 2, 3)),
    "ffn_w_up_0": ("ffn_w_up", 0, (D_MODEL, 2 * D_FF), (2, D_MODEL // 2, N_SHARD, 2 * D_FF // N_SHARD), (0, 2, 1, 3)),
    "ffn_w_up_1": ("ffn_w_up", 1, (D_MODEL, 2 * D_FF), (2, D_MODEL // 2, N_SHARD, 2 * D_FF // N_SHARD), (0, 2, 1, 3)),
    "ffn_w_down_0": ("ffn_w_down", 0, (D_FF, D_MODEL), (N_SHARD, 2, D_FF // (2 * N_SHARD), D_MODEL), (1, 0, 2, 3)),
    "ffn_w_down_1": ("ffn_w_down", 1, (D_FF, D_MODEL), (N_SHARD, 2, D_FF // (2 * N_SHARD), D_MODEL), (1, 0, 2, 3)),
}
GATHER_GROUPS = (("ret_gdn_w_in",), ("ret_gdn_w_out", "ffn_w_up_0", "ffn_w_down_0"), ("lru_w_in", "lru_w_out"), ("ffn_w_up_1", "ffn_w_down_1"))
REDUCE_GROUPS = (("ffn_w_up_1", "ffn_w_down_1"), ("lru_w_in", "lru_w_out"), ("ffn_w_up_0", "ffn_w_down_0"), ("ret_gdn_w_out", "ret_gdn_w_in"))
BLOCK_WEIGHTS = ("lru_w_a", "lru_w_x")
GATHER_COLLECTIVE_ID = 1
REDUCE_COLLECTIVE_ID = GATHER_COLLECTIVE_ID + len(GATHER_GROUPS)


TRANSPOSED = ("ret_gdn_w_in",)


def _shard_of(array, tensors):
    weight, layer = BIG_ARRAYS[array][:2]
    t = tensors[weight]
    if weight in TRANSPOSED:
        return jnp.swapaxes(t, 1, 2)[0]
    return _local_view(weight, t) if layer is None else t[layer]


def _core_halves(array, shard):
    _, _, _, split, perm = BIG_ARRAYS[array]
    kept = [k for k in range(4) if k != perm[1]]
    order = [kept.index(perm[0]), kept.index(perm[2]), kept.index(perm[3])]
    return shard.reshape([split[k] for k in kept]).transpose(order)


def _local_view(name, a):
    if name in SQUEEZE:
        return a[0]
    if a.ndim == 1:
        return a[None, :]
    return a


def kernel(x, norm_mix, norm_ffn, ret_gdn_w_in, gdn_conv_w, gdn_a_log, gdn_dt_bias, gdn_out_gain, ret_gdn_w_out, lru_w_in, lru_conv_w, lru_conv_b, lru_w_a, lru_b_a, lru_w_x, lru_b_x, lru_lambda, lru_w_out, ffn_w_up, ffn_conv_w, ffn_conv_b, ffn_w_down, norm_final, loss_target, m_norm_mix, m_norm_ffn, m_ret_gdn_w_in, m_gdn_conv_w, m_gdn_a_log, m_gdn_dt_bias, m_gdn_out_gain, m_ret_gdn_w_out, m_lru_w_in, m_lru_conv_w, m_lru_conv_b, m_lru_w_a, m_lru_b_a, m_lru_w_x, m_lru_b_x, m_lru_lambda, m_lru_w_out, m_ffn_w_up, m_ffn_conv_w, m_ffn_conv_b, m_ffn_w_down, m_norm_final, v_norm_mix, v_norm_ffn, v_ret_gdn_w_in, v_gdn_conv_w, v_gdn_a_log, v_gdn_dt_bias, v_gdn_out_gain, v_ret_gdn_w_out, v_lru_w_in, v_lru_conv_w, v_lru_conv_b, v_lru_w_a, v_lru_b_a, v_lru_w_x, v_lru_b_x, v_lru_lambda, v_lru_w_out, v_ffn_w_up, v_ffn_conv_w, v_ffn_conv_b, v_ffn_w_down, v_norm_final):
    given = dict(norm_mix=norm_mix, norm_ffn=norm_ffn, ret_gdn_w_in=ret_gdn_w_in, gdn_conv_w=gdn_conv_w, gdn_a_log=gdn_a_log, gdn_dt_bias=gdn_dt_bias, gdn_out_gain=gdn_out_gain, ret_gdn_w_out=ret_gdn_w_out, lru_w_in=lru_w_in, lru_conv_w=lru_conv_w, lru_conv_b=lru_conv_b, lru_w_a=lru_w_a, lru_b_a=lru_b_a, lru_w_x=lru_w_x, lru_b_x=lru_b_x, lru_lambda=lru_lambda, lru_w_out=lru_w_out, ffn_w_up=ffn_w_up, ffn_conv_w=ffn_conv_w, ffn_conv_b=ffn_conv_b, ffn_w_down=ffn_w_down, norm_final=norm_final)
    mom1 = dict(norm_mix=m_norm_mix, norm_ffn=m_norm_ffn, ret_gdn_w_in=m_ret_gdn_w_in, gdn_conv_w=m_gdn_conv_w, gdn_a_log=m_gdn_a_log, gdn_dt_bias=m_gdn_dt_bias, gdn_out_gain=m_gdn_out_gain, ret_gdn_w_out=m_ret_gdn_w_out, lru_w_in=m_lru_w_in, lru_conv_w=m_lru_conv_w, lru_conv_b=m_lru_conv_b, lru_w_a=m_lru_w_a, lru_b_a=m_lru_b_a, lru_w_x=m_lru_w_x, lru_b_x=m_lru_b_x, lru_lambda=m_lru_lambda, lru_w_out=m_lru_w_out, ffn_w_up=m_ffn_w_up, ffn_conv_w=m_ffn_conv_w, ffn_conv_b=m_ffn_conv_b, ffn_w_down=m_ffn_w_down, norm_final=m_norm_final)
    mom2 = dict(norm_mix=v_norm_mix, norm_ffn=v_norm_ffn, ret_gdn_w_in=v_ret_gdn_w_in, gdn_conv_w=v_gdn_conv_w, gdn_a_log=v_gdn_a_log, gdn_dt_bias=v_gdn_dt_bias, gdn_out_gain=v_gdn_out_gain, ret_gdn_w_out=v_ret_gdn_w_out, lru_w_in=v_lru_w_in, lru_conv_w=v_lru_conv_w, lru_conv_b=v_lru_conv_b, lru_w_a=v_lru_w_a, lru_b_a=v_lru_b_a, lru_w_x=v_lru_w_x, lru_b_x=v_lru_b_x, lru_lambda=v_lru_lambda, lru_w_out=v_lru_w_out, ffn_w_up=v_ffn_w_up, ffn_conv_w=v_ffn_conv_w, ffn_conv_b=v_ffn_conv_b, ffn_w_down=v_ffn_w_down, norm_final=v_norm_final)

    local = {n: _local_view(n, a) for n, a in given.items()}

    core = lax.axis_index("c")
    chip = 2 * lax.axis_index("x") + lax.axis_index("y")
    is_my_chip = lax.broadcasted_iota(jnp.int32, (N_SHARD, 1, 1), 0) == chip

    def by_core(mine, other):
        return jnp.where(core == 0, jnp.stack([mine, other]), jnp.stack([other, mine]))

    vec_names, rp_names = list(VECTOR_SHARDED), list(REPLICATED)
    full = dict(zip(vec_names, all_gather_shards([local[n] for n in vec_names], [SHARDED[n] for n in vec_names], F32, 32, "p")))
    for n in rp_names:
        full[n] = local[n]
    in_flight = {}

    bf16_halves = {}

    def cast_halves(gi):
        if gi not in bf16_halves:
            bf16_halves[gi] = [_core_halves(a, _shard_of(a, given).astype(BF16)) for a in GATHER_GROUPS[gi]]
        return bf16_halves[gi]

    def launch(gi, after=None):
        halves = cast_halves(gi)
        if after is not None:
            halves, after = lax.optimization_barrier((halves, after))
        in_flight[gi] = (halves,) + gather_halves(halves, name=f"gather_weights_{gi}", collective_id=GATHER_COLLECTIVE_ID + gi)
        return after

    def land(gi, after):
        halves, lands, sibs = in_flight[gi]
        (lands, sibs), after = lax.optimization_barrier(((lands, sibs), after))
        for a, mine, got, passed in zip(GATHER_GROUPS[gi], halves, lands, sibs):
            weight, layer, full_shape, split, perm = BIG_ARRAYS[a]
            half_mine = jnp.where(is_my_chip, jnp.where(core == 0, mine[0], mine[1])[None], got)
            half_other = jnp.where(is_my_chip, jnp.where(core == 0, mine[1], mine[0])[None], passed)
            value = by_core(half_mine, half_other).transpose(tuple(np.argsort(perm))).reshape(full_shape)
            if layer is None:
                full[weight] = value
            else:
                full.setdefault(weight, [None, None])[layer] = value
        return after

    reducing = {}

    def reduce_ready(gi, grads, then=None, extra=()):
        def travelling(a):
            split, perm = _travel_layout(a)
            return grads[a] if grads[a].ndim == 4 else grads[a].reshape(split).transpose(perm)

        arrays = [travelling(a) for a in REDUCE_GROUPS[gi]] + list(extra)
        scatter = [True] * len(REDUCE_GROUPS[gi]) + [False] * len(extra)
        reducing[gi], then = reduce_between_cores(arrays, scatter, tag=str(gi), collective_id=REDUCE_COLLECTIVE_ID + 3 * gi, before=then)
        return then

    def reduce_send(gi, then=None):
        reducing[gi], then = reduce_between_chips(reducing[gi], before=then)
        return then

    def stage(name, tensors, grads=None):
        if name == "start":
            launch(0)
            launch(1)
            fillers = (cast_halves(2), cast_halves(3), [full[n] for n in vec_names])
            (bf16_halves[2], bf16_halves[3], gathered_small), tensors = lax.optimization_barrier((fillers, tensors))
            full.update(zip(vec_names, gathered_small))
            return land(0, tensors)
        if name == "normed":
            return launch(3, launch(2, tensors))
        if name in ("mixed", "layer0", "mixed1"):
            return land({"mixed": 1, "layer0": 2, "mixed1": 3}[name], tensors)
        gi = int(name[len("grads")])
        return reduce_ready(gi, grads, tensors) if name.endswith("_ready") else reduce_send(gi, tensors)

    small_names = [n for n in rp_names if n not in BLOCK_WEIGHTS] + vec_names

    loss_part, dx, grads = local_step(x[0], loss_target[0], full, stage)
    small_shapes = [grads[n].shape for n in small_names] + [(1, 1)]
    small_rows = _pack_rows(sum(int(np.prod(s)) for s in small_shapes), 16)
    small = _pack([grads[n] for n in small_names] + [loss_part[:, :1]], small_rows, F32).reshape(2, 1, small_rows // 2, LANES)
    last = len(REDUCE_GROUPS) - 1
    halves_of_blocks = [grads[n].reshape(2, 1, LRU_BLOCKS * HEAD // 2, HEAD) for n in BLOCK_WEIGHTS]
    reduce_ready(last, grads, extra=[small] + halves_of_blocks)
    reduce_send(last)
    reduced, result = {}, {}

    def finish(gi, after):
        g_own, g_sib = reduce_finish(reducing[gi], after)
        reduced.update(zip(list(REDUCE_GROUPS[gi]) + ["small"] + list(BLOCK_WEIGHTS), zip(g_own, g_sib)))

    def update(n):
        if n in TRANSPOSED:
            n_rows, n_cols = given[n].shape[2], given[n].shape[1]

            def rows(t):
                return jnp.swapaxes(t, 1, 2).reshape(n_rows, 1, n_cols)

            def back(t):
                return jnp.swapaxes(t.reshape(1, n_rows, n_cols), 1, 2)

            g_rows = jnp.swapaxes(by_core(*reduced[n]), 0, 1).reshape(n_rows, 1, n_cols)
            result[n] = tuple(back(t) for t in adamw_rows(rows(given[n]), g_rows, rows(mom1[n]), rows(mom2[n]), name=f"adamw_{n}"))
            return
        done = None
        for a in (k for k, spec in BIG_ARRAYS.items() if spec[0] == n):
            r, cols = reduced[a][0].shape
            layer = BIG_ARRAYS[a][1] or 0
            w3, m3, v3 = (t if BIG_ARRAYS[a][1] is not None else t.reshape(1, 2 * r, cols) for t in (given[n], mom1[n], mom2[n]))
            done = adamw_halves(w3, m3, v3, *reduced[a], layer=layer, prev=done, name=f"adamw_{a}")
        result[n] = done

    updated = []
    for gi in range(last + 1):
        finish(gi, tuple(result[n][0] for n in updated) if updated else (dx, reducing[last][1]))
        for n in MATMUL_SHARDED:
            if n not in updated and all(a in reduced for a, spec in BIG_ARRAYS.items() if spec[0] == n):
                update(n)
                updated.append(n)

    for n in BLOCK_WEIGHTS:
        w3, m3, v3 = (t.reshape(1, LRU_BLOCKS * HEAD, HEAD) for t in (given[n], mom1[n], mom2[n]))
        result[n] = adamw_halves(w3, m3, v3, *reduced[n], name=f"adamw_{n}")

    *small_sums, loss_sum = _unpack(by_core(*reduced["small"]).reshape(small_rows, LANES), small_shapes)
    loss = loss_sum[0, 0]
    g_small = dict(zip(small_names, small_sums))
    for n in vec_names:
        size = local[n].shape[SHARDED[n]]
        g_small[n] = lax.dynamic_slice_in_dim(g_small[n], chip * size, size, axis=SHARDED[n])
    views = [[_local_view(n, src[n]) for n in small_names] for src in (given, mom1, mom2)]
    d_s, m_s, v_s = adamw_many(views[0], [g_small[n] for n in small_names], views[1], views[2], name="adamw_small")
    for n, d, nm, nv in zip(small_names, d_s, m_s, v_s):
        result[n] = (g_small[n], d, nm, nv)

    outs = [[result[n][k].reshape(given[n].shape) for n in WEIGHTS] for k in range(4)]
    return (loss, dx[None], *outs[0], *outs[1], *outs[2], *outs[3])
```

```python
import functools

import numpy as np
import jax
import jax.numpy as jnp
from jax import lax
from jax.experimental import pallas as pl
from jax.experimental.pallas import tpu as pltpu
from jax.experimental.pallas import tpu_sc as plsc

F32 = jnp.float32
BF16 = jnp.bfloat16
MESH = pl.DeviceIdType.MESH

SEQ = 2048
D_MODEL = 1024
N_HEADS = 4
HEAD = 128
RET_CHUNK = 128
RET_CHUNKS_PER_STEP = 2
GDN_CHUNK = 64
GDN_CHUNKS_PER_STEP = 8
GROUP = N_HEADS * HEAD
MIX_MAIN = 8 * GROUP
D_FF = 2816
LRU_BLOCKS = 8
LRU_C = 8.0
ROPE_BASE = 10000.0
EPS = 1e-6
N_SHARD = 4
LANES = 128

ADAM_LR, ADAM_B1, ADAM_B2, ADAM_EPS, ADAM_WD, ADAM_STEP = 0.001, 0.9, 0.999, 1e-08, 0.01, 10

VMEM_LIMIT_BYTES = 56 * 1024 * 1024

_roll = pltpu.roll


def _params(**kw):
    return pltpu.CompilerParams(vmem_limit_bytes=VMEM_LIMIT_BYTES, **kw)


def _sds(shape, dtype):
    return jax.ShapeDtypeStruct(tuple(shape), dtype)


def _shift_raw(x, d):
    n = x.shape[0]
    t = lax.broadcasted_iota(jnp.int32, x.shape, 0)
    if d > 0:
        return jnp.where(t >= d, _roll(x, d, 0), 0.0)
    return jnp.where(t < n + d, _roll(x, n + d, 0), 0.0)


@functools.partial(jax.custom_vjp, nondiff_argnums=(1,))
def shift_rows(x, d):
    return _shift_raw(x, d)


def _shift_fwd(x, d):
    return _shift_raw(x, d), None


def _shift_bwd(d, _, g):
    return (_shift_raw(g, -d),)


shift_rows.defvjp(_shift_fwd, _shift_bwd)


@jax.custom_vjp
def swap_halves(x):
    return _roll(x, HEAD // 2, 1)


def _swap_fwd(x):
    return _roll(x, HEAD // 2, 1), None


def _swap_bwd(_, g):
    return (_roll(g, HEAD // 2, 1),)


swap_halves.defvjp(_swap_fwd, _swap_bwd)


SCAN_BLOCK_ROWS = 16


def _scan_block(a, u, reverse):
    n = a.shape[0]
    t = lax.broadcasted_iota(jnp.int32, a.shape, 0)
    d = 1
    while d < n:
        if reverse:
            m = t < n - d
            a_s, u_s = _roll(a, n - d, 0), _roll(u, n - d, 0)
        else:
            m = t >= d
            a_s, u_s = _roll(a, d, 0), _roll(u, d, 0)
        u = a * jnp.where(m, u_s, 0.0) + u
        a = a * jnp.where(m, a_s, 1.0)
        d *= 2
    return a, u


def _scan_raw(a, u, reverse):
    n = a.shape[0]
    blocks = range(n // SCAN_BLOCK_ROWS)
    out = [None] * len(blocks)
    entering = None
    for b in (reversed(blocks) if reverse else blocks):
        rows = slice(b * SCAN_BLOCK_ROWS, (b + 1) * SCAN_BLOCK_ROWS)
        a_run, h = _scan_block(a[rows], u[rows], reverse)
        if entering is not None:
            h = a_run * entering + h
        out[b] = h
        entering = h[:1] if reverse else h[SCAN_BLOCK_ROWS - 1:]
    return jnp.concatenate(out, axis=0)


@jax.custom_vjp
def lin_scan(a, u):
    return _scan_raw(a, u, False)


def _lin_scan_fwd(a, u):
    hs = _scan_raw(a, u, False)
    return hs, (a, hs)


def _lin_scan_bwd(res, g):
    a, hs = res
    lam = _scan_raw(_shift_raw(a, -1), g, True)
    return lam * _shift_raw(hs, 1), lam


lin_scan.defvjp(_lin_scan_fwd, _lin_scan_bwd)


def _bdot(a, b, dims=(((1,), (0,)), ((), ()))):
    return lax.dot_general(a.astype(BF16), b.astype(BF16), dims, preferred_element_type=F32)


def _each(f, *seqs):
    return tuple(f(*a) for a in zip(*seqs))


def _split_bf16(a):
    hi = a.astype(BF16)
    return hi, (a - hi.astype(F32)).astype(BF16)


def _dot3_raw(a_s, b_s):
    a_hl = _each(_split_bf16, a_s)
    b_hl = _each(_split_bf16, b_s)
    hh = _each(lambda a, b: _bdot(a[0], b[0]), a_hl, b_hl)
    hl = _each(lambda a, b: _bdot(a[0], b[1]), a_hl, b_hl)
    lh = _each(lambda a, b: _bdot(a[1], b[0]), a_hl, b_hl)
    return _each(lambda x, y, z: x + (y + z), hh, hl, lh)


@jax.custom_vjp
def dot3(a_s, b_s):
    return _dot3_raw(a_s, b_s)


def _dot3_fwd(a_s, b_s):
    return _dot3_raw(a_s, b_s), (a_s, b_s)


def _dot3_bwd(res, g_s):
    a_s, b_s = res
    return (_each(lambda g, b: _bdot(g, b, (((1,), (1,)), ((), ()))), g_s, b_s),
            _each(lambda a, g: _bdot(a, g, (((0,), (0,)), ((), ()))), a_s, g_s))


dot3.defvjp(_dot3_fwd, _dot3_bwd)


def _eye(n):
    i = lax.broadcasted_iota(jnp.int32, (n, n), 0)
    j = lax.broadcasted_iota(jnp.int32, (n, n), 1)
    return (i == j).astype(F32)


def _unit_lower_inverse_raw(lmats):
    n = lmats[0].shape[0]
    eye = _eye(n)
    ps = _each(lambda l: -l, lmats)
    invs = _each(lambda x: eye + x, ps)
    k = 1
    while 2 * k < n:
        ps = _each(lambda p: _bdot(p, p), ps)
        invs = _each(lambda inv, p: inv + _bdot(inv, p), invs, ps)
        k *= 2
    prods = _dot3_raw(lmats, invs)
    resids = _each(lambda inv, pr: eye - inv - pr, invs, prods)
    return _each(lambda inv, r: inv + _bdot(inv, r), invs, resids)


@jax.custom_vjp
def unit_lower_inverse(lmats):
    return _unit_lower_inverse_raw(lmats)


def _uli_fwd(lmats):
    invs = _unit_lower_inverse_raw(lmats)
    return invs, invs


def _uli_bwd(invs, g_s):
    ms = _each(lambda inv, g: _bdot(inv, g, (((0,), (0,)), ((), ()))), invs, g_s)
    return (_each(lambda m, inv: -_bdot(m, inv, (((1,), (1,)), ((), ()))), ms, invs),)


unit_lower_inverse.defvjp(_uli_fwd, _uli_bwd)


def _cumsum_raw(x, reverse):
    n = x.shape[0]
    t = lax.broadcasted_iota(jnp.int32, x.shape, 0)
    d = 1
    while d < n:
        if reverse:
            x = x + jnp.where(t < n - d, _roll(x, n - d, 0), 0.0)
        else:
            x = x + jnp.where(t >= d, _roll(x, d, 0), 0.0)
        d *= 2
    return x


@jax.custom_vjp
def cumsum_rows(x):
    return _cumsum_raw(x, False)


def _cumsum_fwd(x):
    return _cumsum_raw(x, False), None


def _cumsum_bwd(_, g):
    return (_cumsum_raw(g, True),)


cumsum_rows.defvjp(_cumsum_fwd, _cumsum_bwd)


_NT = (((1,), (1,)), ((), ()))
_TN = (((0,), (0,)), ((), ()))


def _softplus(x):
    return jnp.maximum(x, 0.0) + jnp.log1p(jnp.exp(-jnp.abs(x)))


def _expm1_nonpos(x):
    poly = x * (1.0 + x * (0.5 + x * (1.0 / 6 + x * (1.0 / 24 + x * (1.0 / 120 + x * (1.0 / 720))))))
    return jnp.where(x > -0.25, poly, jnp.exp(x) - 1.0)


def _rms(x):
    return x * lax.rsqrt(jnp.mean(x * x, axis=-1, keepdims=True) + EPS)


def _causal_conv(x, w, width):
    y = w[width - 1:width, :] * x
    for j in range(width - 1):
        y = y + w[j:j + 1, :] * shift_rows(x, width - 1 - j)
    return y


def _norm_fn(x, g):
    return _rms(x) * g


def _ffn_act_fn(ug, uv, wg, wv, bg, bv):
    return jax.nn.silu(_causal_conv(ug, wg, 3) + bg) * (_causal_conv(uv, wv, 3) + bv)


def _gdn_conv_fn(x, w):
    return jax.nn.silu(_causal_conv(x, w, 4))


def _lru_fn(gate, x, cw, cb, wa, ba, wx, bx, lam):
    xr = _causal_conv(x, cw, 4) + cb
    r = jax.nn.sigmoid(_bdot(xr, wa) + ba)
    i = jax.nn.sigmoid(_bdot(xr, wx) + bx)
    log_a = -LRU_C * r * _softplus(-lam)
    a = jnp.exp(log_a)
    u = jnp.sqrt(-_expm1_nonpos(2.0 * log_a)) * (i * xr)
    hs = lin_scan(a, u)
    return jax.nn.gelu(gate) * hs


def _ret_fn(qs, ks, vs, gates, states, cos2, sin2, dmasks, ktails, qdecs, cdecs):
    c = RET_CHUNK
    n_heads = len(qs)
    n_chunks = qs[0].shape[0] // c
    units = tuple((ci, h) for ci in range(n_chunks) for h in range(n_heads))

    def rows(x, ci):
        return x[ci * c:(ci + 1) * c]

    qrs = tuple(rows(qs[h], ci) * rows(cos2, ci) + swap_halves(rows(qs[h], ci)) * rows(sin2, ci) for ci, h in units)
    krs = tuple((rows(ks[h], ci) * rows(cos2, ci) + swap_halves(rows(ks[h], ci)) * rows(sin2, ci)) * (HEAD ** -0.5) for ci, h in units)
    vus = tuple(rows(vs[h], ci) for ci, h in units)
    scores = tuple(_bdot(q, k, _NT) * dmasks[h] for q, k, (_, h) in zip(qrs, krs, units))
    intra = _each(lambda sc, v: _bdot(sc, v), scores, vus)
    outs = []
    for ci in range(n_chunks):
        mine = slice(ci * n_heads, (ci + 1) * n_heads)
        inter = _each(lambda q, d, s: _bdot(q * d, s), qrs[mine], qdecs, states)
        outs.append(_each(lambda a, b: a + b, intra[mine], inter))
        states = _each(lambda s, cd, k, kt, v: s * cd + _bdot(k * kt, v, _TN), states, cdecs, krs[mine], ktails, vus[mine])
    ys = tuple(_rms(jnp.concatenate([outs[ci][h] for ci in range(n_chunks)], axis=0)) * jax.nn.silu(gates[h]) for h in range(n_heads))
    return ys, states


def _pick_lane(x, lane_idx):
    lane = lax.broadcasted_iota(jnp.int32, x.shape, 1)
    return jnp.sum(jnp.where(lane == lane_idx, x, 0.0), axis=1, keepdims=True)


def _l2norm(x):
    return x * lax.rsqrt(jnp.sum(x * x, axis=-1, keepdims=True) + EPS)


def _gdn_fn(qcs, kcs, vcs, gates, small, a_log, dt_bias, gain, states):
    c = GDN_CHUNK
    n_heads = len(qcs)
    n_chunks = qcs[0].shape[0] // c
    units = tuple((ci, h) for ci in range(n_chunks) for h in range(n_heads))

    def unit_rows(per_head):
        return tuple(per_head[h][ci * c:(ci + 1) * c] for ci, h in units)

    smalls = tuple(small[ci * c:(ci + 1) * c] for ci, _ in units)
    heads = tuple(h for _, h in units)
    intra = _gdn_intra(unit_rows(qcs), unit_rows(kcs), unit_rows(vcs), smalls, heads, a_log, dt_bias)
    outs = []
    for ci in range(n_chunks):
        mine = slice(ci * n_heads, (ci + 1) * n_heads)
        os_, states = _gdn_inter(*(part[mine] for part in intra), states)
        outs.append(os_)
    ys = tuple(_rms(jnp.concatenate([outs[ci][h] for ci in range(n_chunks)], axis=0)) * gain * jax.nn.silu(gates[h])
               for h in range(n_heads))
    return ys, states


def _gdn_inter(qs, ks, us, ws, attns, gcs, g_lasts, states):
    v_news = _each(lambda u, w, s: u - _bdot(w, s), us, ws, states)
    inter = _each(lambda q, gc, s: _bdot(q * jnp.exp(gc), s), qs, gcs, states)
    os_ = _each(lambda x, a, v: x + _bdot(a, v), inter, attns, v_news)
    new_states = _each(lambda s, gl, k, gc, v: s * jnp.exp(gl) + _bdot(k * jnp.exp(gl - gc), v, _TN), states, g_lasts, ks, gcs, v_news)
    return os_, new_states


def _gdn_intra(qcs, kcs, vcs, smalls, heads, a_log, dt_bias):
    c = GDN_CHUNK
    qs = _each(lambda x: _l2norm(x) * (HEAD ** -0.5), qcs)
    ks = _each(_l2norm, kcs)
    betas = _each(lambda sm, h: jax.nn.sigmoid(_pick_lane(sm, h)), smalls, heads)
    gs = _each(lambda sm, h: -jnp.exp(_pick_lane(a_log, h)) * _softplus(_pick_lane(sm, h + N_HEADS) + _pick_lane(dt_bias, h)),
               smalls, heads)
    i = lax.broadcasted_iota(jnp.int32, (c, c), 0)
    j = lax.broadcasted_iota(jnp.int32, (c, c), 1)
    tril = i >= j
    gcs = _each(lambda g: cumsum_rows(jnp.broadcast_to(g, (c, LANES)))[:, :1], gs)
    gc_rows = _each(lambda gc: jnp.broadcast_to(gc, (c, c)), gcs)
    decays = _each(lambda r: jnp.where(tril, jnp.exp(jnp.where(tril, r - r.T, 0.0)), 0.0), gc_rows)
    kbs = _each(lambda k, b: k * b, ks, betas)
    lmats = _each(lambda kb, k, d: jnp.where(i > j, _bdot(kb, k, _NT) * d, 0.0), kbs, ks, decays)
    attns = _each(lambda q, k, d: jnp.where(tril, _bdot(q, k, _NT) * d, 0.0), qs, ks, decays)
    invs = unit_lower_inverse(lmats)
    us = dot3(invs, _each(lambda v, b: v * b, vcs, betas))
    ws = dot3(invs, _each(lambda kb, gc: kb * jnp.exp(gc), kbs, gcs))
    g_lasts = _each(lambda g: jnp.sum(g, axis=0, keepdims=True), gs)
    return qs, ks, us, ws, attns, gcs, g_lasts


def _final_fn(h, g, target):
    y = _rms(h) * g
    return 0.5 * jnp.sum(jnp.mean(jnp.square(y - target), axis=-1, keepdims=True), axis=0, keepdims=True)


def _tile(n, candidates):
    for t in candidates:
        if n % t == 0:
            return t
    raise ValueError(f"no tile for {n}")


MATMUL_RESIDENT_LHS_BYTES = 8 * 1024 * 1024


def matmul(a, b, *, ta=False, tb=False, add=None, more=None, out_dtype=F32, tm=None, tn=None, split=None, column_halves=None, name):
    m = a.shape[1] if ta else a.shape[0]
    k = a.shape[0] if ta else a.shape[1]
    n = b.shape[0] if tb else b.shape[1]
    assert k == (b.shape[1] if tb else b.shape[0])
    out_shape, out_block, out_index = (m, n), None, lambda i, j: (i, j)
    if split is not None:
        dims4, perm = split
        out_shape = tuple(dims4[p] for p in perm)
        r, cols = out_shape[2:]
        tm, tn = m, tn or _tile(cols, (1408, 512))
        cb = cols // tn
        if perm == (0, 2, 1, 3):
            out_block, out_index = (2, None, r, tn), lambda i, j: (0, j // cb, 0, j % cb)
        elif perm == (1, 0, 2, 3):
            out_block, out_index = (2, N_SHARD, r, tn), lambda i, j: (0, 0, 0, j)
        else:
            raise ValueError(perm)
    if tm is None and not ta and m * k * a.dtype.itemsize <= MATMUL_RESIDENT_LHS_BYTES:
        tm = m
    tm = tm or _tile(m, (1024, 512, 1408, 256, 128))
    tn = tn or _tile(n, (512, 1408, 256, 128))
    aliases, prev, keep_rows = {}, None, None
    if column_halves is not None:
        total_rows, first_row, keep_rows, prev = column_halves
        tn = n // 2
        rows_out = keep_rows or tm
        out_shape, out_block = (2, total_rows, tn), (None, rows_out, tn)
        out_index = lambda i, j: (j, first_row // rows_out + i, 0)
    dims = (((0 if ta else 1,), (1 if tb else 0,)), ((), ()))

    def body(a_ref, b_ref, *rest):
        acc = lax.dot_general(a_ref[...].astype(BF16), b_ref[...].astype(BF16), dims, preferred_element_type=F32)
        if more is not None:
            acc = acc + _bdot(rest[0][...], rest[1][...])
        if add is not None:
            acc = acc + rest[2 if more is not None else 0][...]
        o_ref = rest[-1]
        acc = acc.astype(out_dtype)
        if split is not None and split[1] == (1, 0, 2, 3):
            rows = o_ref.shape[2]
            for s in range(N_SHARD):
                for h in range(2):
                    o_ref[h, s] = acc[(2 * s + h) * rows:(2 * s + h + 1) * rows]
        elif keep_rows is not None:
            o_ref[...] = acc[:keep_rows]
        else:
            o_ref[...] = acc.reshape(o_ref.shape)

    a_spec = pl.BlockSpec((k, tm), lambda i, j: (0, i)) if ta else pl.BlockSpec((tm, k), lambda i, j: (i, 0))
    b_spec = pl.BlockSpec((tn, k), lambda i, j: (j, 0)) if tb else pl.BlockSpec((k, tn), lambda i, j: (0, j))
    o_spec = pl.BlockSpec(out_block or (tm, tn), out_index)
    in_specs, args = [a_spec, b_spec], [a, b]
    if more is not None:
        k2 = more[0].shape[1]
        in_specs += [pl.BlockSpec((tm, k2), lambda i, j: (i, 0)), pl.BlockSpec((k2, tn), lambda i, j: (0, j))]
        args += list(more)
    if add is not None:
        in_specs.append(o_spec)
        args.append(add)
    if prev is not None:
        aliases = {len(args): 0}
        in_specs.append(pl.BlockSpec(memory_space=pl.ANY))
        args.append(prev)
    return pl.pallas_call(body, out_shape=_sds(out_shape, out_dtype), grid=(m // tm, n // tn), in_specs=in_specs,
                          out_specs=o_spec, input_output_aliases=aliases, compiler_params=_params(), name=name)(*args)


def norm_matmul(x, g, b, *, tb=False, also=None, name):
    t, k = x.shape
    n = b.shape[0] if tb else b.shape[1]
    tn = _tile(n, (512, 1408, 256, 128))
    dims = (((1,), (1 if tb else 0,)), ((), ()))

    def body(x_ref, g_ref, b_ref, *rest):
        o_ref, hn_ref = rest[-3:-1] if also is not None else rest[-2:]

        @pl.when(pl.program_id(0) == 0)
        def _():
            hn = _norm_fn(x_ref[...], g_ref[...]).astype(BF16)
            hn_ref[...] = hn
            if also is not None:
                rest[-1][...] = _bdot(hn, rest[0][...], _NT)

        o_ref[...] = lax.dot_general(hn_ref[...], b_ref[...].astype(BF16), dims, preferred_element_type=F32)

    b_spec = pl.BlockSpec((tn, k), lambda j: (j, 0)) if tb else pl.BlockSpec((k, tn), lambda j: (0, j))
    whole = pl.BlockSpec((t, k), lambda j: (0, 0))
    in_specs, args = [whole, pl.BlockSpec((1, k), lambda j: (0, 0)), b_spec], [x, g, b]
    out_shape, out_specs = [_sds((t, n), F32), _sds((t, k), BF16)], [pl.BlockSpec((t, tn), lambda j: (0, j)), whole]
    if also is not None:
        n2 = also.shape[0]
        in_specs.append(pl.BlockSpec((n2, k), lambda j: (0, 0)))
        args.append(also)
        out_shape.append(_sds((t, n2), F32))
        out_specs.append(pl.BlockSpec((t, n2), lambda j: (0, 0)))
    return pl.pallas_call(body, out_shape=out_shape, grid=(n // tn,), in_specs=in_specs, out_specs=out_specs,
                          compiler_params=_params(), name=name)(*args)


ROW_TILE = 512


def norm_bwd(x, g, dy, dres, *, name):
    t, d = x.shape

    def body(x_ref, g_ref, dy_ref, dres_ref, dx_ref, dg_ref):
        _, vjp = jax.vjp(_norm_fn, x_ref[...], g_ref[...])
        dx, dg = vjp(dy_ref[...])
        dx_ref[...] = dx + dres_ref[...]

        @pl.when(pl.program_id(0) == 0)
        def _():
            dg_ref[...] = jnp.zeros_like(dg_ref)

        dg_ref[...] += dg

    row = pl.BlockSpec((ROW_TILE, d), lambda i: (i, 0))
    vec = pl.BlockSpec((1, d), lambda i: (0, 0))
    return pl.pallas_call(body, out_shape=(_sds((t, d), F32), _sds((1, d), F32)), grid=(t // ROW_TILE,),
                          in_specs=[row, vec, row, row], out_specs=(row, vec), compiler_params=_params(), name=name)(x, g, dy, dres)


def final_fwd_bwd(h, g, target, *, name):
    t, d = h.shape

    def body(h_ref, g_ref, t_ref, loss_ref, dh_ref, dg_ref):
        tgt = t_ref[...]
        loss, vjp = jax.vjp(lambda hh, gg: _final_fn(hh, gg, tgt), h_ref[...], g_ref[...])
        dh, dg = vjp(jnp.ones((1, 1), F32))
        dh_ref[...] = dh

        @pl.when(pl.program_id(0) == 0)
        def _():
            dg_ref[...] = jnp.zeros_like(dg_ref)
            loss_ref[...] = jnp.zeros_like(loss_ref)

        dg_ref[...] += dg
        loss_ref[...] += jnp.broadcast_to(loss, loss_ref.shape)

    row = pl.BlockSpec((ROW_TILE, d), lambda i: (i, 0))
    vec = pl.BlockSpec((1, d), lambda i: (0, 0))
    return pl.pallas_call(body, out_shape=(_sds((1, LANES), F32), _sds((t, d), F32), _sds((1, d), F32)), grid=(t // ROW_TILE,),
                          in_specs=[row, vec, row], out_specs=(pl.BlockSpec((1, LANES), lambda i: (0, 0)), row, vec),
                          compiler_params=_params(), name=name)(h, g, target)


FFN_FWD_COLS = 256
FFN_BWD_COLS = 128


def ffn_act_fwd(u, cw, cb, *, name):
    t = u.shape[0]
    w = FFN_FWD_COLS
    nb = D_FF // w

    def body(ug_ref, uv_ref, wg_ref, wv_ref, bg_ref, bv_ref, o_ref):
        o_ref[...] = _ffn_act_fn(ug_ref[...], uv_ref[...], wg_ref[...], wv_ref[...], bg_ref[...], bv_ref[...]).astype(BF16)

    def col(rows, off):
        return pl.BlockSpec((rows, w), lambda j: (0, j + off))

    return pl.pallas_call(body, out_shape=_sds((t, D_FF), BF16), grid=(nb,),
                          in_specs=[col(t, 0), col(t, nb), col(3, 0), col(3, nb), col(1, 0), col(1, nb)],
                          out_specs=col(t, 0), compiler_params=_params(), name=name)(u, u, cw, cw, cb, cb)


def _put_column_blocks(step, n_steps, blocks, dst_ref, width, stage_ref, sems):
    def copies(at):
        slot = at % 2
        return [pltpu.make_async_copy(stage_ref.at[slot, p], dst_ref.at[:, pl.ds(pl.multiple_of((p * n_steps + at) * width, LANES), width)],
                                      sems.at[slot, p]) for p in range(len(blocks))]

    @pl.when(step >= 2)
    def _():
        for cp in copies(step - 2):
            cp.wait()

    for p, value in enumerate(blocks):
        stage_ref[step % 2, p] = value
    for cp in copies(step):
        cp.start()

    @pl.when(step == n_steps - 1)
    def _():
        for cp in copies(step - 1) + copies(step):
            cp.wait()


def ffn_act_bwd(u, cw, cb, da, *, name):
    t = u.shape[0]
    w = FFN_BWD_COLS
    nb = D_FF // w

    def body(ug_ref, uv_ref, wg_ref, wv_ref, bg_ref, bv_ref, da_ref, dug_ref, duv_ref, dwg_ref, dwv_ref, dbg_ref, dbv_ref):
        _, vjp = jax.vjp(_ffn_act_fn, ug_ref[...], uv_ref[...], wg_ref[...], wv_ref[...], bg_ref[...], bv_ref[...])
        dug, duv, dwg, dwv, dbg, dbv = vjp(da_ref[...])
        dug_ref[...] = dug.astype(BF16)
        duv_ref[...] = duv.astype(BF16)
        dwg_ref[...] = dwg
        dwv_ref[...] = dwv
        dbg_ref[...] = dbg
        dbv_ref[...] = dbv

    def col(rows, off):
        return pl.BlockSpec((rows, w), lambda j: (0, j + off))

    outs = pl.pallas_call(
        body, out_shape=(_sds((t, D_FF), BF16), _sds((t, D_FF), BF16), _sds((3, D_FF), F32), _sds((3, D_FF), F32),
                         _sds((1, D_FF), F32), _sds((1, D_FF), F32)),
        grid=(nb,), in_specs=[col(t, 0), col(t, nb), col(3, 0), col(3, nb), col(1, 0), col(1, nb), col(t, 0)],
        out_specs=(col(t, 0), col(t, 0), col(3, 0), col(3, 0), col(1, 0), col(1, 0)), compiler_params=_params(), name=name,
    )(u, u, cw, cw, cb, cb, da)
    dug, duv, dwg, dwv, dbg, dbv = outs
    return jnp.concatenate([dug, duv], axis=1), jnp.concatenate([dwg, dwv], axis=1), jnp.concatenate([dbg, dbv], axis=1)


GDN_CONV_COLS = 256
GDN_CONV_OFF = 4 * GROUP


def gdn_conv_fwd(p, cw, *, name):
    t = p.shape[0]
    w = GDN_CONV_COLS
    nb = 3 * GROUP // w
    off = GDN_CONV_OFF // w

    def body(x_ref, w_ref, o_ref):
        o_ref[...] = _gdn_conv_fn(x_ref[...], w_ref[...])

    return pl.pallas_call(body, out_shape=_sds((t, 3 * GROUP), F32), grid=(nb,),
                          in_specs=[pl.BlockSpec((t, w), lambda j: (0, j + off)), pl.BlockSpec((4, w), lambda j: (0, j))],
                          out_specs=pl.BlockSpec((t, w), lambda j: (0, j)), compiler_params=_params(), name=name)(p, cw)


def gdn_conv_bwd(p, cw, dc, *, name):
    t = p.shape[0]
    w = GDN_CONV_COLS
    nb = 3 * GROUP // w
    off = GDN_CONV_OFF // w

    def body(x_ref, w_ref, dc_ref, dx_ref, dw_ref):
        _, vjp = jax.vjp(_gdn_conv_fn, x_ref[...], w_ref[...])
        dx, dw = vjp(dc_ref[...])
        dx_ref[...] = dx.astype(BF16)
        dw_ref[...] = dw

    blk = pl.BlockSpec((t, w), lambda j: (0, j))
    wblk = pl.BlockSpec((4, w), lambda j: (0, j))
    return pl.pallas_call(body, out_shape=(_sds((t, 3 * GROUP), BF16), _sds((4, 3 * GROUP), F32)), grid=(nb,),
                          in_specs=[pl.BlockSpec((t, w), lambda j: (0, j + off)), wblk, blk], out_specs=(blk, wblk),
                          compiler_params=_params(), name=name)(p, cw, dc)


def _lru_specs(t):
    w = D_MODEL // LRU_BLOCKS
    gate = pl.BlockSpec((t, w), lambda j: (0, j))
    xin = pl.BlockSpec((t, w), lambda j: (0, j + LRU_BLOCKS))
    cw = pl.BlockSpec((4, w), lambda j: (0, j))
    vec = pl.BlockSpec((1, w), lambda j: (0, j))
    mat = pl.BlockSpec((None, w, w), lambda j: (j, 0, 0))
    return gate, xin, cw, vec, mat


def lru_fwd(gx, cw, cb, wa, ba, wx, bx, lam, *, name):
    t = gx.shape[0]
    gate, xin, cws, vec, mat = _lru_specs(t)

    def body(g_ref, x_ref, cw_ref, cb_ref, wa_ref, ba_ref, wx_ref, bx_ref, lam_ref, o_ref):
        o_ref[...] = _lru_fn(g_ref[...], x_ref[...], cw_ref[...], cb_ref[...], wa_ref[...], ba_ref[...], wx_ref[...],
                             bx_ref[...], lam_ref[...]).astype(BF16)

    return pl.pallas_call(body, out_shape=_sds((t, D_MODEL), BF16), grid=(LRU_BLOCKS,),
                          in_specs=[gate, xin, cws, vec, mat, vec, mat, vec, vec], out_specs=gate,
                          compiler_params=_params(), name=name)(gx, gx, cw, cb, wa, ba, wx, bx, lam)


def lru_bwd(gx, cw, cb, wa, ba, wx, bx, lam, dy, *, name):
    t = gx.shape[0]
    gate, xin, cws, vec, mat = _lru_specs(t)

    def body(g_ref, x_ref, cw_ref, cb_ref, wa_ref, ba_ref, wx_ref, bx_ref, lam_ref, dy_ref,
             dgx_ref, dcw_ref, dcb_ref, dwa_ref, dba_ref, dwx_ref, dbx_ref, dlam_ref, stage_ref, sems):
        _, vjp = jax.vjp(_lru_fn, g_ref[...], x_ref[...], cw_ref[...], cb_ref[...], wa_ref[...], ba_ref[...], wx_ref[...],
                         bx_ref[...], lam_ref[...])
        dg, dx, dcw, dcb, dwa, dba, dwx, dbx, dlam = vjp(dy_ref[...])
        _put_column_blocks(pl.program_id(0), LRU_BLOCKS, (dg.astype(BF16), dx.astype(BF16)), dgx_ref, D_MODEL // LRU_BLOCKS, stage_ref, sems)
        dcw_ref[...] = dcw
        dcb_ref[...] = dcb
        dwa_ref[...] = dwa
        dba_ref[...] = dba
        dwx_ref[...] = dwx
        dbx_ref[...] = dbx
        dlam_ref[...] = dlam

    d = D_MODEL
    w = d // LRU_BLOCKS
    out_shape = (_sds((t, 2 * d), BF16), _sds((4, d), F32), _sds((1, d), F32), _sds((LRU_BLOCKS, w, w), F32),
                 _sds((1, d), F32), _sds((LRU_BLOCKS, w, w), F32), _sds((1, d), F32), _sds((1, d), F32))
    return pl.pallas_call(body, out_shape=out_shape, grid=(LRU_BLOCKS,),
                          in_specs=[gate, xin, cws, vec, mat, vec, mat, vec, vec, gate],
                          out_specs=(pl.BlockSpec(memory_space=pl.ANY), cws, vec, mat, vec, mat, vec, vec),
                          scratch_shapes=[pltpu.VMEM((2, 2, t, w), BF16), pltpu.SemaphoreType.DMA((2, 2))],
                          compiler_params=_params(), name=name)(gx, gx, cw, cb, wa, ba, wx, bx, lam, dy)


def _ret_tables():
    half = HEAD // 2
    inv_freq = (np.float32(ROPE_BASE) ** (-np.arange(half, dtype=np.float32) / np.float32(half))).astype(np.float32)
    ang = (np.arange(SEQ, dtype=np.float32)[:, None] * inv_freq[None, :]).astype(np.float64)
    cos2 = np.concatenate([np.cos(ang), np.cos(ang)], axis=1).astype(np.float32)
    sin2 = np.concatenate([-np.sin(ang), np.sin(ang)], axis=1).astype(np.float32)
    c = RET_CHUNK
    log_gamma = np.log1p(-np.exp2(-5.0 - np.arange(N_HEADS, dtype=np.float64)))
    idx = np.arange(c, dtype=np.float64)
    rel = idx[:, None] - idx[None, :]
    dmask = np.where(rel >= 0, np.exp(log_gamma[:, None, None] * np.maximum(rel, 0.0)), 0.0)
    ones = np.ones((N_HEADS, c, HEAD))
    ktail = np.exp(log_gamma[:, None] * (c - 1 - idx))[:, :, None] * ones
    qdec = np.exp(log_gamma[:, None] * (idx + 1.0))[:, :, None] * ones
    cdec = np.exp(log_gamma * c)[:, None, None] * ones
    return tuple(jnp.asarray(a, F32) for a in (cos2, sin2, dmask, ktail, qdec, cdec))


def _ret_specs(rev):
    c = RET_CHUNK * RET_CHUNKS_PER_STEP
    nc = SEQ // c

    def n_of(n):
        return nc - 1 - n if rev else n

    def group(off):
        return pl.BlockSpec((c, GROUP), lambda n: (n_of(n), off))

    tab = pl.BlockSpec((c, HEAD), lambda n: (n_of(n), 0))
    const = pl.BlockSpec((N_HEADS, RET_CHUNK, HEAD), lambda n: (0, 0, 0))
    state = pl.BlockSpec((N_HEADS, None, HEAD, HEAD), lambda n: (0, n_of(n), 0, 0))
    return group, tab, const, state, nc


def _head(h):
    return slice(h * HEAD, (h + 1) * HEAD)


def ret_fwd(p, tables, *, name):
    group, tab, const, state, nc = _ret_specs(False)

    def body(q_ref, k_ref, v_ref, g_ref, cos_ref, sin_ref, dm_ref, kt_ref, qd_ref, cd_ref, y_ref, st_ref, s_scr):
        @pl.when(pl.program_id(0) == 0)
        def _():
            s_scr[...] = jnp.zeros_like(s_scr)

        heads = range(N_HEADS)
        states = tuple(s_scr[h] for h in heads)
        ys, new_states = _ret_fn(*(tuple(r[:, _head(h)] for h in heads) for r in (q_ref, k_ref, v_ref, g_ref)), states,
                                 cos_ref[...], sin_ref[...], *(tuple(r[h] for h in heads) for r in (dm_ref, kt_ref, qd_ref, cd_ref)))
        for h in heads:
            st_ref[h] = states[h]
            y_ref[:, _head(h)] = ys[h].astype(BF16)
            s_scr[h] = new_states[h]

    return pl.pallas_call(
        body, out_shape=(_sds((SEQ, 2 * GROUP), BF16), _sds((N_HEADS, nc, HEAD, HEAD), F32)), grid=(nc,),
        in_specs=[group(0), group(1), group(2), group(3), tab, tab, const, const, const, const],
        out_specs=(group(0), state), scratch_shapes=[pltpu.VMEM((N_HEADS, HEAD, HEAD), F32)], compiler_params=_params(), name=name,
    )(p, p, p, p, *tables)


def ret_bwd(p, tables, states, dy, *, name):
    group, tab, const, state, nc = _ret_specs(True)

    def body(q_ref, k_ref, v_ref, g_ref, cos_ref, sin_ref, dm_ref, kt_ref, qd_ref, cd_ref, st_ref, dy_ref,
             dq_ref, dk_ref, dv_ref, dg_ref, ds_scr):
        @pl.when(pl.program_id(0) == 0)
        def _():
            ds_scr[...] = jnp.zeros_like(ds_scr)

        heads = range(N_HEADS)
        consts = (cos_ref[...], sin_ref[...], *(tuple(r[h] for h in heads) for r in (dm_ref, kt_ref, qd_ref, cd_ref)))
        _, vjp = jax.vjp(lambda *a: _ret_fn(*a, *consts), *(tuple(r[:, _head(h)] for h in heads) for r in (q_ref, k_ref, v_ref, g_ref)),
                         tuple(st_ref[h] for h in heads))
        dqs, dks, dvs, dgs, dss = vjp((tuple(dy_ref[:, _head(h)] for h in heads), tuple(ds_scr[h] for h in heads)))
        for h in heads:
            dq_ref[:, _head(h)] = dqs[h].astype(BF16)
            dk_ref[:, _head(h)] = dks[h].astype(BF16)
            dv_ref[:, _head(h)] = dvs[h].astype(BF16)
            dg_ref[:, _head(h)] = dgs[h].astype(BF16)
            ds_scr[h] = dss[h]

    out = _sds((SEQ, GROUP), BF16)
    return pl.pallas_call(
        body, out_shape=(out, out, out, out), grid=(nc,),
        in_specs=[group(0), group(1), group(2), group(3), tab, tab, const, const, const, const, state, group(0)],
        out_specs=(group(0), group(0), group(0), group(0)), scratch_shapes=[pltpu.VMEM((N_HEADS, HEAD, HEAD), F32)],
        compiler_params=_params(), name=name,
    )(p, p, p, p, *tables, states, dy)


def _gdn_specs(rev):
    c = GDN_CHUNK * GDN_CHUNKS_PER_STEP
    nc = SEQ // c

    def n_of(n):
        return nc - 1 - n if rev else n

    def group(off):
        return pl.BlockSpec((c, GROUP), lambda n: (n_of(n), off))

    small = pl.BlockSpec((c, LANES), lambda n: (n_of(n), 0))
    vec = pl.BlockSpec((1, LANES), lambda n: (0, 0))
    state = pl.BlockSpec((N_HEADS, None, HEAD, HEAD), lambda n: (0, n_of(n), 0, 0))
    qkv = pl.BlockSpec((c, 3 * GROUP), lambda n: (n_of(n), 0))
    return group, small, vec, state, qkv, nc


GDN_GATE_GROUP = 7


def gdn_fwd(conv, p, small, a_log, dt_bias, gain, y_started, *, name):
    group, sm, vec, state, _, nc = _gdn_specs(False)

    def body(q_ref, k_ref, v_ref, g_ref, sm_ref, al_ref, dt_ref, gn_ref, _, y_ref, st_ref, s_scr):
        @pl.when(pl.program_id(0) == 0)
        def _():
            s_scr[...] = jnp.zeros_like(s_scr)

        states = tuple(s_scr[h] for h in range(N_HEADS))
        ys, new_states = _gdn_fn(*(tuple(r[:, _head(h)] for h in range(N_HEADS)) for r in (q_ref, k_ref, v_ref, g_ref)),
                                 sm_ref[...], al_ref[...], dt_ref[...], gn_ref[...], states)
        for h in range(N_HEADS):
            st_ref[h] = states[h]
            y_ref[:, _head(h)] = ys[h].astype(BF16)
            s_scr[h] = new_states[h]

    return pl.pallas_call(
        body, out_shape=(_sds((SEQ, 2 * GROUP), BF16), _sds((N_HEADS, nc, HEAD, HEAD), F32)), grid=(nc,),
        in_specs=[group(0), group(1), group(2), group(GDN_GATE_GROUP), sm, vec, vec, vec, pl.BlockSpec(memory_space=pl.ANY)],
        out_specs=(group(1), state), input_output_aliases={8: 0},
        scratch_shapes=[pltpu.VMEM((N_HEADS, HEAD, HEAD), F32)], compiler_params=_params(), name=name,
    )(conv, conv, conv, p, small, a_log, dt_bias, gain, y_started)


def gdn_bwd(conv, p, small, a_log, dt_bias, gain, states, dy, *, name):
    group, sm, vec, state, qkv, nc = _gdn_specs(True)

    def body(q_ref, k_ref, v_ref, g_ref, sm_ref, al_ref, dt_ref, gn_ref, st_ref, dy_ref,
             dqkv_ref, dg_ref, dsm_ref, dal_ref, ddt_ref, dgn_ref, ds_scr):
        @pl.when(pl.program_id(0) == 0)
        def _():
            ds_scr[...] = jnp.zeros_like(ds_scr)
            dal_ref[...] = jnp.zeros_like(dal_ref)
            ddt_ref[...] = jnp.zeros_like(ddt_ref)
            dgn_ref[...] = jnp.zeros_like(dgn_ref)

        per_head = tuple(tuple(r[:, _head(h)] for h in range(N_HEADS)) for r in (q_ref, k_ref, v_ref, g_ref))
        _, vjp = jax.vjp(_gdn_fn, *per_head, sm_ref[...], al_ref[...], dt_ref[...], gn_ref[...],
                         tuple(st_ref[h] for h in range(N_HEADS)))
        cts = (tuple(dy_ref[:, _head(h)] for h in range(N_HEADS)), tuple(ds_scr[h] for h in range(N_HEADS)))
        dqs, dks, dvs, dgs, dsm, dal, ddt, dgn, dss = vjp(cts)
        for h in range(N_HEADS):
            for part, blocks in enumerate((dqs, dks, dvs)):
                dqkv_ref[:, part * GROUP + h * HEAD:part * GROUP + (h + 1) * HEAD] = blocks[h]
            dg_ref[:, _head(h)] = dgs[h].astype(BF16)
            ds_scr[h] = dss[h]
        dsm_ref[...] = dsm
        dal_ref[...] += dal
        ddt_ref[...] += ddt
        dgn_ref[...] += dgn

    pv = _sds((1, LANES), F32)
    return pl.pallas_call(
        body, out_shape=(_sds((SEQ, 3 * GROUP), F32), _sds((SEQ, GROUP), BF16), _sds((SEQ, LANES), F32), pv, pv, pv), grid=(nc,),
        in_specs=[group(0), group(1), group(2), group(GDN_GATE_GROUP), sm, vec, vec, vec, state, group(1)],
        out_specs=(qkv, group(0), sm, vec, vec, vec), scratch_shapes=[pltpu.VMEM((N_HEADS, HEAD, HEAD), F32)],
        compiler_params=_params(), name=name,
    )(conv, conv, conv, p, small, a_log, dt_bias, gain, states, dy)


ELEMENTWISE_BLOCK_BYTES = 2 * 1024 * 1024


def _row_tile(r, c):
    best = None
    for tr in range(8, r + 1, 8):
        if r % tr == 0 and tr * c * 4 <= ELEMENTWISE_BLOCK_BYTES:
            best = tr
    if best is None:
        raise ValueError(f"no row tile for ({r}, {c})")
    return best


def _tile_2d(r, c):
    if any(r % tr == 0 for tr in range(8, r + 1, 8)):
        return _row_tile(r, c), c
    tc = max(t for t in range(LANES, c + 1, LANES) if c % t == 0 and r * t * 4 <= ELEMENTWISE_BLOCK_BYTES)
    return r, tc


def _core_index():
    return lax.axis_index("c").astype(jnp.int32).reshape(1)


def _chip_index():
    return (2 * lax.axis_index("x") + lax.axis_index("y")).astype(jnp.int32).reshape(1)


def adamw_halves(w, m, v, g_own, g_sib, *, layer=0, prev=None, name):
    n_layers, rows, c = w.shape
    r = rows // 2
    tr = _row_tile(r, c)
    nb = r // tr

    def body(c_ref, w_ref, m_ref, v_ref, own_ref, sib_ref, *rest):
        g_ref, d_ref, nm_ref, nv_ref = rest[-4:]
        gg = jnp.where(pl.program_id(0) == c_ref[0], own_ref[...], sib_ref[...])
        nm = ADAM_B1 * m_ref[...] + (1.0 - ADAM_B1) * gg
        nv = ADAM_B2 * v_ref[...] + (1.0 - ADAM_B2) * jnp.square(gg)
        m_hat = nm / (1.0 - ADAM_B1 ** ADAM_STEP)
        v_hat = nv / (1.0 - ADAM_B2 ** ADAM_STEP)
        g_ref[...] = gg
        d_ref[...] = -ADAM_LR * (m_hat / (jnp.sqrt(v_hat) + ADAM_EPS) + ADAM_WD * w_ref[...])
        nm_ref[...] = nm
        nv_ref[...] = nv

    full = pl.BlockSpec((None, tr, c), lambda h, i, cr: (layer, h * nb + i, 0))
    half = pl.BlockSpec((tr, c), lambda h, i, cr: (i, 0))
    o = _sds((n_layers, rows, c), F32)
    prev = list(prev or ())
    gs = pltpu.PrefetchScalarGridSpec(num_scalar_prefetch=1, grid=(2, nb), in_specs=[full, full, full, half, half] + [_ANY] * len(prev),
                                      out_specs=(full, full, full, full))
    n_fixed = 6
    return pl.pallas_call(body, out_shape=(o, o, o, o), grid_spec=gs, compiler_params=_params(), name=name,
                          input_output_aliases={n_fixed + k: k for k in range(len(prev))})(
        _core_index(), w, m, v, g_own, g_sib, *prev)


ADAMW_ROW_STEPS = 6


def adamw_rows(w, g, m, v, *, name):
    rows, _, cols = w.shape
    tr = rows // ADAMW_ROW_STEPS

    def body(w_ref, g_ref, m_ref, v_ref, g_out_ref, d_ref, nm_ref, nv_ref):
        gg = g_ref[...]
        nm = ADAM_B1 * m_ref[...] + (1.0 - ADAM_B1) * gg
        nv = ADAM_B2 * v_ref[...] + (1.0 - ADAM_B2) * jnp.square(gg)
        m_hat = nm / (1.0 - ADAM_B1 ** ADAM_STEP)
        v_hat = nv / (1.0 - ADAM_B2 ** ADAM_STEP)
        g_out_ref[...] = gg
        d_ref[...] = -ADAM_LR * (m_hat / (jnp.sqrt(v_hat) + ADAM_EPS) + ADAM_WD * w_ref[...])
        nm_ref[...] = nm
        nv_ref[...] = nv

    blk = pl.BlockSpec((tr, 1, cols), lambda i: (i, 0, 0))
    o = _sds(w.shape, F32)
    return pl.pallas_call(body, out_shape=(o, o, o, o), grid=(ADAMW_ROW_STEPS,), in_specs=[blk] * 4, out_specs=(blk, blk, blk, blk),
                          compiler_params=_params(), name=name)(w, g, m, v)


def adamw_many(ws, gs, ms, vs, *, name):
    n = len(ws)

    def body(*refs):
        w_refs, g_refs, m_refs, v_refs, d_refs, nm_refs, nv_refs = (refs[k * n:(k + 1) * n] for k in range(7))
        for i in range(n):
            gg = g_refs[i][...]
            nm = ADAM_B1 * m_refs[i][...] + (1.0 - ADAM_B1) * gg
            nv = ADAM_B2 * v_refs[i][...] + (1.0 - ADAM_B2) * jnp.square(gg)
            m_hat = nm / (1.0 - ADAM_B1 ** ADAM_STEP)
            v_hat = nv / (1.0 - ADAM_B2 ** ADAM_STEP)
            d_refs[i][...] = -ADAM_LR * (m_hat / (jnp.sqrt(v_hat) + ADAM_EPS) + ADAM_WD * w_refs[i][...])
            nm_refs[i][...] = nm
            nv_refs[i][...] = nv

    outs = pl.pallas_call(body, out_shape=[_sds(w.shape, F32) for w in ws] * 3, compiler_params=_params(), name=name)(*ws, *gs, *ms, *vs)
    return outs[:n], outs[n:2 * n], outs[2 * n:]


def add_core_halves(g2, land, *, out_dtype, name):
    _, ns, r, cols = g2.shape
    tr, tc = _tile_2d(r, cols)

    def body(c_ref, a_ref, b_ref, o_ref):
        o_ref[...] = (a_ref[...] + b_ref[...]).astype(out_dtype)

    gs = pltpu.PrefetchScalarGridSpec(
        num_scalar_prefetch=1, grid=(ns, r // tr, cols // tc),
        in_specs=[pl.BlockSpec((None, None, tr, tc), lambda s, i, j, cr: (cr[0], s, i, j)),
                  pl.BlockSpec((None, tr, tc), lambda s, i, j, cr: (s, i, j))],
        out_specs=pl.BlockSpec((None, tr, tc), lambda s, i, j, cr: (s, i, j)))
    return pl.pallas_call(body, out_shape=_sds((ns, r, cols), out_dtype), grid_spec=gs, compiler_params=_params(), name=name)(
        _core_index(), g2, land)


def sum_over_chips(own, land, *, scatter, name):
    _, r, cols = own.shape
    tr, tc = _tile_2d(r, cols)

    def body(mine_ref, own_ref, l0, l1, l2, l3, o_ref):
        mine = mine_ref[0]
        mine_val = own_ref[...]
        acc = None
        for s, l_ref in enumerate((l0, l1, l2, l3)):
            val = jnp.where(mine == s, mine_val, l_ref[...]).astype(F32)
            acc = val if acc is None else acc + val
        o_ref[...] = acc

    def slot(s):
        return pl.BlockSpec((None, tr, tc), lambda i, j, mr: (jnp.where(mr[0] == s, (s + 1) % N_SHARD, s), i, j))

    own_spec = pl.BlockSpec((None, tr, tc), lambda i, j, mr: (mr[0] if scatter else 0, i, j))
    gs = pltpu.PrefetchScalarGridSpec(num_scalar_prefetch=1, grid=(r // tr, cols // tc), in_specs=[own_spec] + [slot(s) for s in range(N_SHARD)],
                                      out_specs=pl.BlockSpec((tr, tc), lambda i, j, mr: (i, j)))
    return pl.pallas_call(body, out_shape=_sds((r, cols), F32), grid_spec=gs, compiler_params=_params(), name=name)(
        _chip_index(), own, land, land, land, land)


_ANY = pl.BlockSpec(memory_space=pl.ANY)


def xy_exchange(src, *, name):
    rh = src.shape[1]

    def body(src_ref, land_ref, send_sems, recv_sems, loc_sem):
        x, y, c = lax.axis_index("x"), lax.axis_index("y"), lax.axis_index("c")
        mine = 2 * x + y
        peers = [(1 - x, y), (x, 1 - y), (1 - x, 1 - y)]

        def copy(k, px, py, dst_slot):
            return pltpu.make_async_remote_copy(src_ref=src_ref.at[c], dst_ref=land_ref.at[dst_slot], send_sem=send_sems.at[k],
                                                recv_sem=recv_sems.at[k], device_id=(px, py, c), device_id_type=MESH)

        keep = pltpu.make_async_copy(src_ref.at[c], land_ref.at[mine], loc_sem)
        keep.start()
        sends = [copy(k, px, py, mine) for k, (px, py) in enumerate(peers)]
        for cp in sends:
            cp.start()
        for cp in sends:
            cp.wait_send()
        for k, (px, py) in enumerate(peers):
            copy(k, px, py, 2 * px + py).wait_recv()
        keep.wait()

    return pl.pallas_call(body, out_shape=_sds((N_SHARD, rh, LANES), src.dtype), in_specs=[_ANY], out_specs=_ANY,
                          scratch_shapes=[pltpu.SemaphoreType.DMA((3,)), pltpu.SemaphoreType.DMA((3,)), pltpu.SemaphoreType.DMA(())],
                          name=name)(src)


def core_exchange(src, *, name):
    def body(src_ref, out_ref, send_sem, recv_sem, loc_sem):
        x, y, c = lax.axis_index("x"), lax.axis_index("y"), lax.axis_index("c")
        keep = pltpu.make_async_copy(src_ref, out_ref.at[c], loc_sem)
        keep.start()
        cp = pltpu.make_async_remote_copy(src_ref=src_ref, dst_ref=out_ref.at[c], send_sem=send_sem, recv_sem=recv_sem,
                                          device_id=(x, y, 1 - c), device_id_type=MESH)
        cp.start()
        cp.wait_send()
        pltpu.make_async_remote_copy(src_ref=src_ref, dst_ref=out_ref.at[1 - c], send_sem=send_sem, recv_sem=recv_sem,
                                     device_id=(x, y, 1 - c), device_id_type=MESH).wait_recv()
        keep.wait()

    return pl.pallas_call(body, out_shape=_sds((2,) + src.shape, src.dtype), in_specs=[_ANY], out_specs=_ANY,
                          scratch_shapes=[pltpu.SemaphoreType.DMA(()), pltpu.SemaphoreType.DMA(()), pltpu.SemaphoreType.DMA(())],
                          name=name)(src)


def _sequencer_call(body, ins, out_shapes, sem_counts, name, collective_id):
    return pl.kernel(body, out_type=list(out_shapes), mesh=plsc.ScalarSubcoreMesh(axis_name="sequencer", num_cores=1), name=name,
                     scratch_types=[pltpu.SemaphoreType.DMA((k,)) for k in sem_counts],
                     compiler_params=pltpu.CompilerParams(collective_id=collective_id))(*ins)


def _handshake(peers):
    barrier = pltpu.get_barrier_semaphore()
    for peer in peers:
        pl.semaphore_signal(barrier, inc=1, device_id=peer, device_id_type=MESH)
    pl.semaphore_wait(barrier, len(peers))


def _xy_peers(x, y):
    return [(1 - x, y), (x, 1 - y), (1 - x, 1 - y)]


def gather_halves(halves, *, name, collective_id):
    n = len(halves)

    def body(*refs):
        ins, lands, sibs = refs[:n], refs[n:2 * n], refs[2 * n:3 * n]
        ici_send, ici_recv, d2d_send, d2d_recv = refs[3 * n:]
        x, y, c = lax.axis_index("x"), lax.axis_index("y"), lax.axis_index("c")
        mine = 2 * x + y
        peers = _xy_peers(x, y)
        _handshake([(px, py, c) for px, py in peers] + [(x, y, 1 - c)])

        def ici(i, k, slot):
            px, py = peers[k]
            return pltpu.make_async_remote_copy(src_ref=ins[i].at[c], dst_ref=lands[i].at[slot], send_sem=ici_send.at[3 * i + k],
                                                recv_sem=ici_recv.at[3 * i + k], device_id=(px, py, c), device_id_type=MESH)

        def pass_on(i, k):
            px, py = peers[k]
            slot = 2 * px + py
            return pltpu.make_async_remote_copy(src_ref=lands[i].at[slot], dst_ref=sibs[i].at[slot], send_sem=d2d_send.at[3 * i + k],
                                                recv_sem=d2d_recv.at[3 * i + k], device_id=(x, y, 1 - c), device_id_type=MESH)

        sends = [ici(i, k, mine) for i in range(n) for k in range(3)]
        for cp in sends:
            cp.start()
        passed = []
        for i in range(n):
            for k in range(3):
                px, py = peers[k]
                ici(i, k, 2 * px + py).wait_recv()
                cp = pass_on(i, k)
                cp.start()
                passed.append(cp)
        for cp in passed:
            cp.wait_recv()
        for cp in sends + passed:
            cp.wait_send()

    outs = [_sds((N_SHARD,) + h.shape[1:], h.dtype) for h in halves]
    res = _sequencer_call(body, halves, outs + outs, [3 * n] * 4, name, collective_id)
    return res[:n], res[n:]


def send_other_half(arrays, *, name, collective_id):
    n = len(arrays)

    def body(*refs):
        ins, lands = refs[:n], refs[n:2 * n]
        send_sems, recv_sems = refs[2 * n:]
        x, y, c = lax.axis_index("x"), lax.axis_index("y"), lax.axis_index("c")
        _handshake([(x, y, 1 - c)])
        copies = [pltpu.make_async_remote_copy(src_ref=ins[i].at[1 - c], dst_ref=lands[i], send_sem=send_sems.at[i],
                                               recv_sem=recv_sems.at[i], device_id=(x, y, 1 - c), device_id_type=MESH) for i in range(n)]
        for cp in copies:
            cp.start()
        for cp in copies:
            cp.wait_recv()
        for cp in copies:
            cp.wait_send()

    return _sequencer_call(body, arrays, [_sds(a.shape[1:], a.dtype) for a in arrays], [n, n], name, collective_id)


_HBM = pl.BlockSpec(memory_space=pltpu.HBM)
_SEM = pl.BlockSpec(memory_space=pltpu.SEMAPHORE)
_SPLIT_COPY = dict(has_side_effects=pltpu.SideEffectType.DATAFLOW_SIDE_EFFECTING)


def _chip_copy(ins, lands, send_sems, recv_sems, scatter, i, k, receive):
    x, y, c = lax.axis_index("x"), lax.axis_index("y"), lax.axis_index("c")
    px, py = _xy_peers(x, y)[k]
    theirs, mine = 2 * px + py, 2 * x + y
    src = ins[i].at[theirs] if scatter[i] else ins[i].at[0]
    return pltpu.make_async_remote_copy(src_ref=src, dst_ref=lands[i].at[theirs if receive else mine], send_sem=send_sems.at[3 * i + k],
                                        recv_sem=recv_sems.at[3 * i + k], device_id=(px, py, c), device_id_type=MESH)


def send_to_chips_start(arrays, scatter, *, name):
    n = len(arrays)

    def body(*refs):
        send_sems, recv_sems = refs[2 * n], refs[2 * n + 1]
        ins, lands = refs[2 * n + 2:3 * n + 2], refs[3 * n + 2:4 * n + 2]
        token = refs[4 * n + 2]
        for i in range(n):
            for k in range(3):
                _chip_copy(ins, lands, send_sems, recv_sems, scatter, i, k, receive=False).start()
        token[...] = jnp.zeros_like(token)

    land_shapes = [(N_SHARD,) + a.shape[1:] for a in arrays]
    operands = [pltpu.with_memory_space_constraint(a, pltpu.HBM) for a in arrays]
    operands += [pltpu.with_memory_space_constraint(lax.empty(s, a.dtype), pltpu.HBM) for s, a in zip(land_shapes, arrays)]
    out_shape = ([pltpu.SemaphoreType.DMA((3 * n,)), pltpu.SemaphoreType.DMA((3 * n,))] + [pltpu.HBM(a.shape, a.dtype) for a in arrays]
                 + [pltpu.HBM(s, a.dtype) for s, a in zip(land_shapes, arrays)] + [_sds((8, LANES), F32)])
    res = pl.pallas_call(body, name=name, out_shape=out_shape, in_specs=[_HBM] * (2 * n),
                         out_specs=[_SEM, _SEM] + [_HBM] * (2 * n) + [pl.BlockSpec(memory_space=pltpu.VMEM)],
                         input_output_aliases={i: 2 + i for i in range(2 * n)}, compiler_params=pltpu.CompilerParams(**_SPLIT_COPY))(*operands)
    return (res[0], res[1], res[2:2 + n], res[2 + n:2 + 2 * n], scatter), res[-1]


def send_to_chips_wait(state, after, *, name):
    send_sems, recv_sems, arrays, lands, scatter = state
    n = len(arrays)

    def body(*refs):
        ins, landing = refs[:n], refs[n:2 * n]
        send_sems, recv_sems = refs[2 * n], refs[2 * n + 1]
        for i in range(n):
            for k in range(3):
                _chip_copy(ins, landing, send_sems, recv_sems, scatter, i, k, receive=True).wait_recv()
        for i in range(n):
            for k in range(3):
                _chip_copy(ins, landing, send_sems, recv_sems, scatter, i, k, receive=False).wait_send()

    out_shape = [pltpu.HBM(a.shape, a.dtype) for a in list(arrays) + list(lands)]
    res = pl.pallas_call(body, name=name, out_shape=out_shape, in_specs=[_HBM] * (2 * n) + [_SEM, _SEM] + [_ANY] * len(after),
                         out_specs=[_HBM] * (2 * n), input_output_aliases={i: i for i in range(2 * n)},
                         compiler_params=pltpu.CompilerParams(**_SPLIT_COPY))(*arrays, *lands, send_sems, recv_sems, *after)
    return res[:n], res[n:]


def swap_with_other_core(arrays, *, name, collective_id):
    n = len(arrays)

    def body(*refs):
        ins, lands = refs[:n], refs[n:2 * n]
        send_sems, recv_sems = refs[2 * n:]
        x, y, c = lax.axis_index("x"), lax.axis_index("y"), lax.axis_index("c")
        _handshake([(x, y, 1 - c)])
        copies = [pltpu.make_async_remote_copy(src_ref=ins[i], dst_ref=lands[i], send_sem=send_sems.at[i], recv_sem=recv_sems.at[i],
                                               device_id=(x, y, 1 - c), device_id_type=MESH) for i in range(n)]
        for cp in copies:
            cp.start()
        for cp in copies:
            cp.wait_recv()
        for cp in copies:
            cp.wait_send()

    return _sequencer_call(body, arrays, [_sds(a.shape, a.dtype) for a in arrays], [n, n], name, collective_id)


def _pack_rows(n_elems, row_multiple):
    rows = -(-n_elems // LANES)
    return -(-rows // row_multiple) * row_multiple


def _pack(arrays, rows, dtype):
    flat = jnp.concatenate([a.reshape(-1).astype(dtype) for a in arrays])
    return jnp.pad(flat, (0, rows * LANES - flat.shape[0])).reshape(rows, LANES)


def _unpack(packed, shapes):
    flat = packed.reshape(-1)
    out, off = [], 0
    for s in shapes:
        n = int(np.prod(s))
        out.append(flat[off:off + n].reshape(s))
        off += n
    return out


def all_gather_shards(shards, axes, dtype, row_multiple, tag):
    shapes = [s.shape for s in shards]
    rows = _pack_rows(sum(int(np.prod(s)) for s in shapes), row_multiple)
    packed = _pack(shards, rows, dtype).reshape(2, rows // 2, LANES)
    land = xy_exchange(packed, name=f"gather_xy_{tag}")
    both = core_exchange(land, name=f"gather_c_{tag}")
    per_shard = jnp.swapaxes(both, 0, 1).reshape(N_SHARD, rows, LANES)
    pieces = [_unpack(per_shard[s], shapes) for s in range(N_SHARD)]
    return [jnp.concatenate([pieces[s][i] for s in range(N_SHARD)], axis=ax) for i, ax in enumerate(axes)]


def _ordered_before(first, then):
    if then is None:
        return first, None
    return lax.optimization_barrier((first, then))


def reduce_between_cores(arrays, scatter, *, tag, collective_id, before=None):
    arrays, before = _ordered_before(arrays, before)
    land = send_other_half(arrays, name=f"reduce_core_send_{tag}", collective_id=collective_id)
    return (arrays, land, scatter, tag, collective_id), before


def reduce_between_chips(state, before=None):
    arrays, land, scatter, tag, collective_id = state
    chip = [add_core_halves(a, l, out_dtype=BF16 if sc else F32, name=f"reduce_core_add_{tag}_{i}")
            for i, (a, l, sc) in enumerate(zip(arrays, land, scatter))]
    sending, token = send_to_chips_start(chip, scatter, name=f"reduce_chip_start_{tag}")
    token, before = _ordered_before(token, before)
    return (sending, token, scatter, tag, collective_id), before


def reduce_finish(state, after):
    sending, token, scatter, tag, collective_id = state
    chip, land = send_to_chips_wait(sending, tuple(after) + (token,), name=f"reduce_chip_wait_{tag}")
    own = [sum_over_chips(ch, l, scatter=sc, name=f"reduce_chip_add_{tag}_{i}") for i, (ch, l, sc) in enumerate(zip(chip, land, scatter))]
    sib = swap_with_other_core(own, name=f"reduce_core_swap_{tag}", collective_id=collective_id + 2)
    return own, sib


def _ffn_layer_fwd(h, norm_g, w_up, cw, cb, w_down, tag):
    u, hn = norm_matmul(h, norm_g, w_up, name=f"ffn_up_{tag}")
    act = ffn_act_fwd(u, cw, cb, name=f"ffn_act_{tag}")
    out = matmul(act, w_down, add=h, name=f"ffn_down_{tag}")
    return out, (h, hn, u, act)


def _travel_layout(array):
    return BIG_ARRAYS[array][3], BIG_ARRAYS[array][4]


def _ffn_layer_bwd(saved, dout, norm_g, w_up, cw, cb, w_down, tag):
    h, hn, u, act = saved
    dact = matmul(dout, w_down, tb=True, name=f"ffn_down_dx_{tag}")
    d_w_down = matmul(act, dout, ta=True, split=_travel_layout(f"ffn_w_down_{tag}"), name=f"ffn_down_dw_{tag}")
    du, dcw, dcb = ffn_act_bwd(u, cw, cb, dact, name=f"ffn_act_bwd_{tag}")
    dhn = matmul(du, w_up, tb=True, name=f"ffn_up_dx_{tag}")
    d_w_up = matmul(hn, du, ta=True, split=_travel_layout(f"ffn_w_up_{tag}"), name=f"ffn_up_dw_{tag}")
    dh, dg = norm_bwd(h, norm_g, dhn, dout, name=f"ffn_norm_bwd_{tag}")
    return dh, dg, d_w_up, dcw, dcb, d_w_down


def local_step(x, target, w, stage=lambda name, tensors, grads=None: tensors):
    g = {}
    tables = _ret_tables()
    x = stage("start", x)
    w_in_t = w["ret_gdn_w_in"]
    w_main = w_in_t[:MIX_MAIN]
    w_small = jnp.pad(w_in_t[MIX_MAIN:], ((0, LANES - 2 * N_HEADS), (0, 0)))
    a_log = jnp.pad(w["gdn_a_log"], ((0, 0), (0, LANES - N_HEADS)))
    dt_bias = jnp.pad(w["gdn_dt_bias"], ((0, 0), (0, LANES - N_HEADS)))

    p, hn0, small = norm_matmul(x, w["norm_mix"][0:1], w_main, tb=True, also=w_small, name="mix0_in")
    hn0 = stage("normed", hn0)
    y_ret, s_ret = ret_fwd(p, tables, name="ret_fwd")
    conv = gdn_conv_fwd(p, w["gdn_conv_w"], name="gdn_conv")
    y0, s_gdn = gdn_fwd(conv, p, small, a_log, dt_bias, w["gdn_out_gain"], y_ret, name="gdn_fwd")
    y0 = stage("mixed", y0)
    h1 = matmul(y0, w["ret_gdn_w_out"], add=x, name="mix0_out")
    h2, ffn0 = _ffn_layer_fwd(h1, w["norm_ffn"][0:1], w["ffn_w_up"][0], w["ffn_conv_w"][0], w["ffn_conv_b"][0:1], w["ffn_w_down"][0], "0")
    h2 = stage("layer0", h2)

    gx, hn1 = norm_matmul(h2, w["norm_mix"][1:2], w["lru_w_in"], name="mix1_in")
    lru_p = (w["lru_conv_w"], w["lru_conv_b"], w["lru_w_a"], w["lru_b_a"], w["lru_w_x"], w["lru_b_x"], w["lru_lambda"])
    y1 = lru_fwd(gx, *lru_p, name="lru_fwd")
    h3 = stage("mixed1", matmul(y1, w["lru_w_out"], add=h2, name="mix1_out"))
    h4, ffn1 = _ffn_layer_fwd(h3, w["norm_ffn"][1:2], w["ffn_w_up"][1], w["ffn_conv_w"][1], w["ffn_conv_b"][1:2], w["ffn_w_down"][1], "1")

    loss, dh4, g["norm_final"] = final_fwd_bwd(h4, w["norm_final"], target, name="final")

    dh3, dgf1, dwu1, dcw1, dcb1, dwd1 = _ffn_layer_bwd(ffn1, dh4, w["norm_ffn"][1:2], w["ffn_w_up"][1], w["ffn_conv_w"][1],
                                                     w["ffn_conv_b"][1:2], w["ffn_w_down"][1], "1")
    g["ffn_w_up_1"], g["ffn_w_down_1"] = dwu1, dwd1
    dh3 = stage("grads0_ready", dh3, g)
    dy1 = matmul(dh3, w["lru_w_out"], tb=True, name="mix1_out_dx")
    g["lru_w_out"] = matmul(y1, dh3, ta=True, split=_travel_layout("lru_w_out"), name="mix1_out_dw")
    dgx, g["lru_conv_w"], g["lru_conv_b"], g["lru_w_a"], g["lru_b_a"], g["lru_w_x"], g["lru_b_x"], g["lru_lambda"] = lru_bwd(
        gx, *lru_p, dy1, name="lru_bwd")
    dgx = stage("grads0_send", dgx, g)
    dhn1 = matmul(dgx, w["lru_w_in"], tb=True, name="mix1_in_dx")
    g["lru_w_in"] = matmul(hn1, dgx, ta=True, split=_travel_layout("lru_w_in"), name="mix1_in_dw")
    dh2, dgm1 = norm_bwd(h2, w["norm_mix"][1:2], dhn1, dh3, name="mix1_norm_bwd")
    dh2 = stage("grads1_ready", dh2, g)

    dh1, dgf0, dwu0, dcw0, dcb0, dwd0 = _ffn_layer_bwd(ffn0, dh2, w["norm_ffn"][0:1], w["ffn_w_up"][0], w["ffn_conv_w"][0],
                                                     w["ffn_conv_b"][0:1], w["ffn_w_down"][0], "0")
    g["ffn_w_up_0"], g["ffn_w_down_0"] = dwu0, dwd0
    dh1 = stage("grads2_ready", stage("grads1_send", dh1, g), g)
    dy0 = matmul(dh1, w["ret_gdn_w_out"], tb=True, name="mix0_out_dx")
    g["ret_gdn_w_out"] = matmul(y0, dh1, ta=True, split=_travel_layout("ret_gdn_w_out"), name="mix0_out_dw")
    dq_r, dk_r, dv_r, dg_r = ret_bwd(p, tables, s_ret, dy0, name="ret_bwd")
    dy0, dq_r = stage("grads2_send", (dy0, dq_r), g)
    dconv, dg_d, dsmall, dal, ddt, dgain = gdn_bwd(conv, p, small, a_log, dt_bias, w["gdn_out_gain"], s_gdn, dy0, name="gdn_bwd")
    dp_conv, g["gdn_conv_w"] = gdn_conv_bwd(p, w["gdn_conv_w"], dconv, name="gdn_conv_bwd")
    dp = jnp.concatenate([dq_r, dk_r, dv_r, dg_r, dp_conv, dg_d], axis=1)
    dhn0 = matmul(dp, w_main, more=(dsmall, w_small), name="mix0_in_dx")
    d_w_in = matmul(dp, hn0, ta=True, column_halves=(MIX_IN, 0, None, None), name="mix0_in_dw")
    d_w_in = matmul(dsmall, hn0, ta=True, column_halves=(MIX_IN, MIX_MAIN, 2 * N_HEADS, d_w_in), name="mix0_in_small_dw")
    g["ret_gdn_w_in"] = d_w_in.reshape(2, N_SHARD, MIX_IN // N_SHARD, D_MODEL // 2)
    dx, dgm0 = norm_bwd(x, w["norm_mix"][0:1], dhn0, dh1, name="mix0_norm_bwd")

    g["gdn_a_log"] = dal[:, :N_HEADS]
    g["gdn_dt_bias"] = ddt[:, :N_HEADS]
    g["gdn_out_gain"] = dgain
    g["norm_mix"] = jnp.concatenate([dgm0, dgm1], axis=0)
    g["norm_ffn"] = jnp.concatenate([dgf0, dgf1], axis=0)
    g["ffn_conv_w"] = jnp.stack([dcw0, dcw1])
    g["ffn_conv_b"] = jnp.concatenate([dcb0, dcb1], axis=0)
    return loss, dx, g


WEIGHTS = ("norm_mix", "norm_ffn", "ret_gdn_w_in", "gdn_conv_w", "gdn_a_log", "gdn_dt_bias", "gdn_out_gain", "ret_gdn_w_out",
           "lru_w_in", "lru_conv_w", "lru_conv_b", "lru_w_a", "lru_b_a", "lru_w_x", "lru_b_x", "lru_lambda", "lru_w_out",
           "ffn_w_up", "ffn_conv_w", "ffn_conv_b", "ffn_w_down", "norm_final")
MATMUL_SHARDED = {"ret_gdn_w_in": 1, "ret_gdn_w_out": 0, "lru_w_in": 1, "lru_w_out": 0, "ffn_w_up": 2, "ffn_w_down": 1}
VECTOR_SHARDED = {"gdn_conv_w": 1, "lru_conv_w": 1, "lru_conv_b": 1, "lru_b_a": 1, "lru_b_x": 1, "lru_lambda": 1, "ffn_conv_w": 2}
SHARDED = {**MATMUL_SHARDED, **VECTOR_SHARDED}
REPLICATED = tuple(n for n in WEIGHTS if n not in SHARDED)
SQUEEZE = {"ret_gdn_w_in", "gdn_conv_w", "ret_gdn_w_out", "lru_w_in", "lru_conv_w", "lru_w_a", "lru_w_x", "lru_w_out"}
MIX_IN = MIX_MAIN + 2 * N_HEADS
BIG_ARRAYS = {
    "ret_gdn_w_in": ("ret_gdn_w_in", None, (MIX_IN, D_MODEL), (N_SHARD, MIX_IN // N_SHARD, 2, D_MODEL // 2), (2, 0, 1, 3)),
    "ret_gdn_w_out": ("ret_gdn_w_out", None, (2 * GROUP, D_MODEL), (N_SHARD, 2, GROUP // N_SHARD, D_MODEL), (1, 0, 2, 3)),
    "lru_w_in": ("lru_w_in", None, (D_MODEL, 2 * D_MODEL), (2, D_MODEL // 2, N_SHARD, 2 * D_MODEL // N_SHARD), (0, 2, 1, 3)),
    "lru_w_out": ("lru_w_out", None, (D_MODEL, D_MODEL), (N_SHARD, 2, D_MODEL // (2 * N_SHARD), D_MODEL), (1, 0, 2, 3)),
    "ffn_w_up_0": ("ffn_w_up", 0, (D_MODEL, 2 * D_FF), (2, D_MODEL // 2, N_SHARD, 2 * D_FF // N_SHARD), (0, 2, 1, 3)),
    "ffn_w_up_1": ("ffn_w_up", 1, (D_MODEL, 2 * D_FF), (2, D_MODEL // 2, N_SHARD, 2 * D_FF // N_SHARD), (0, 2, 1, 3)),
    "ffn_w_down_0": ("ffn_w_down", 0, (D_FF, D_MODEL), (N_SHARD, 2, D_FF // (2 * N_SHARD), D_MODEL), (1, 0, 2, 3)),
    "ffn_w_down_1": ("ffn_w_down", 1, (D_FF, D_MODEL), (N_SHARD, 2, D_FF // (2 * N_SHARD), D_MODEL), (1, 0, 2, 3)),
}
GATHER_GROUPS = (("ret_gdn_w_in",), ("ret_gdn_w_out", "ffn_w_up_0", "ffn_w_down_0"), ("lru_w_in", "lru_w_out"), ("ffn_w_up_1", "ffn_w_down_1"))
REDUCE_GROUPS = (("ffn_w_up_1", "ffn_w_down_1"), ("lru_w_in", "lru_w_out"), ("ffn_w_up_0", "ffn_w_down_0"), ("ret_gdn_w_out", "ret_gdn_w_in"))
BLOCK_WEIGHTS = ("lru_w_a", "lru_w_x")
GATHER_COLLECTIVE_ID = 1
REDUCE_COLLECTIVE_ID = GATHER_COLLECTIVE_ID + len(GATHER_GROUPS)


TRANSPOSED = ("ret_gdn_w_in",)


def _shard_of(array, tensors):
    weight, layer = BIG_ARRAYS[array][:2]
    t = tensors[weight]
    if weight in TRANSPOSED:
        return jnp.swapaxes(t, 1, 2)[0]
    return _local_view(weight, t) if layer is None else t[layer]


def _core_halves(array, shard):
    _, _, _, split, perm = BIG_ARRAYS[array]
    kept = [k for k in range(4) if k != perm[1]]
    order = [kept.index(perm[0]), kept.index(perm[2]), kept.index(perm[3])]
    return shard.reshape([split[k] for k in kept]).transpose(order)


def _local_view(name, a):
    if name in SQUEEZE:
        return a[0]
    if a.ndim == 1:
        return a[None, :]
    return a


def kernel(x, norm_mix, norm_ffn, ret_gdn_w_in, gdn_conv_w, gdn_a_log, gdn_dt_bias, gdn_out_gain, ret_gdn_w_out, lru_w_in, lru_conv_w, lru_conv_b, lru_w_a, lru_b_a, lru_w_x, lru_b_x, lru_lambda, lru_w_out, ffn_w_up, ffn_conv_w, ffn_conv_b, ffn_w_down, norm_final, loss_target, m_norm_mix, m_norm_ffn, m_ret_gdn_w_in, m_gdn_conv_w, m_gdn_a_log, m_gdn_dt_bias, m_gdn_out_gain, m_ret_gdn_w_out, m_lru_w_in, m_lru_conv_w, m_lru_conv_b, m_lru_w_a, m_lru_b_a, m_lru_w_x, m_lru_b_x, m_lru_lambda, m_lru_w_out, m_ffn_w_up, m_ffn_conv_w, m_ffn_conv_b, m_ffn_w_down, m_norm_final, v_norm_mix, v_norm_ffn, v_ret_gdn_w_in, v_gdn_conv_w, v_gdn_a_log, v_gdn_dt_bias, v_gdn_out_gain, v_ret_gdn_w_out, v_lru_w_in, v_lru_conv_w, v_lru_conv_b, v_lru_w_a, v_lru_b_a, v_lru_w_x, v_lru_b_x, v_lru_lambda, v_lru_w_out, v_ffn_w_up, v_ffn_conv_w, v_ffn_conv_b, v_ffn_w_down, v_norm_final):
    given = dict(norm_mix=norm_mix, norm_ffn=norm_ffn, ret_gdn_w_in=ret_gdn_w_in, gdn_conv_w=gdn_conv_w, gdn_a_log=gdn_a_log, gdn_dt_bias=gdn_dt_bias, gdn_out_gain=gdn_out_gain, ret_gdn_w_out=ret_gdn_w_out, lru_w_in=lru_w_in, lru_conv_w=lru_conv_w, lru_conv_b=lru_conv_b, lru_w_a=lru_w_a, lru_b_a=lru_b_a, lru_w_x=lru_w_x, lru_b_x=lru_b_x, lru_lambda=lru_lambda, lru_w_out=lru_w_out, ffn_w_up=ffn_w_up, ffn_conv_w=ffn_conv_w, ffn_conv_b=ffn_conv_b, ffn_w_down=ffn_w_down, norm_final=norm_final)
    mom1 = dict(norm_mix=m_norm_mix, norm_ffn=m_norm_ffn, ret_gdn_w_in=m_ret_gdn_w_in, gdn_conv_w=m_gdn_conv_w, gdn_a_log=m_gdn_a_log, gdn_dt_bias=m_gdn_dt_bias, gdn_out_gain=m_gdn_out_gain, ret_gdn_w_out=m_ret_gdn_w_out, lru_w_in=m_lru_w_in, lru_conv_w=m_lru_conv_w, lru_conv_b=m_lru_conv_b, lru_w_a=m_lru_w_a, lru_b_a=m_lru_b_a, lru_w_x=m_lru_w_x, lru_b_x=m_lru_b_x, lru_lambda=m_lru_lambda, lru_w_out=m_lru_w_out, ffn_w_up=m_ffn_w_up, ffn_conv_w=m_ffn_conv_w, ffn_conv_b=m_ffn_conv_b, ffn_w_down=m_ffn_w_down, norm_final=m_norm_final)
    mom2 = dict(norm_mix=v_norm_mix, norm_ffn=v_norm_ffn, ret_gdn_w_in=v_ret_gdn_w_in, gdn_conv_w=v_gdn_conv_w, gdn_a_log=v_gdn_a_log, gdn_dt_bias=v_gdn_dt_bias, gdn_out_gain=v_gdn_out_gain, ret_gdn_w_out=v_ret_gdn_w_out, lru_w_in=v_lru_w_in, lru_conv_w=v_lru_conv_w, lru_conv_b=v_lru_conv_b, lru_w_a=v_lru_w_a, lru_b_a=v_lru_b_a, lru_w_x=v_lru_w_x, lru_b_x=v_lru_b_x, lru_lambda=v_lru_lambda, lru_w_out=v_lru_w_out, ffn_w_up=v_ffn_w_up, ffn_conv_w=v_ffn_conv_w, ffn_conv_b=v_ffn_conv_b, ffn_w_down=v_ffn_w_down, norm_final=v_norm_final)

    local = {n: _local_view(n, a) for n, a in given.items()}

    core = lax.axis_index("c")
    chip = 2 * lax.axis_index("x") + lax.axis_index("y")
    is_my_chip = lax.broadcasted_iota(jnp.int32, (N_SHARD, 1, 1), 0) == chip

    def by_core(mine, other):
        return jnp.where(core == 0, jnp.stack([mine, other]), jnp.stack([other, mine]))

    vec_names, rp_names = list(VECTOR_SHARDED), list(REPLICATED)
    full = dict(zip(vec_names, all_gather_shards([local[n] for n in vec_names], [SHARDED[n] for n in vec_names], F32, 32, "p")))
    for n in rp_names:
        full[n] = local[n]
    in_flight = {}

    bf16_halves = {}

    def cast_halves(gi):
        if gi not in bf16_halves:
            bf16_halves[gi] = [_core_halves(a, _shard_of(a, given).astype(BF16)) for a in GATHER_GROUPS[gi]]
        return bf16_halves[gi]

    def launch(gi, after=None):
        halves = cast_halves(gi)
        if after is not None:
            halves, after = lax.optimization_barrier((halves, after))
        in_flight[gi] = (halves,) + gather_halves(halves, name=f"gather_weights_{gi}", collective_id=GATHER_COLLECTIVE_ID + gi)
        return after

    def land(gi, after):
        halves, lands, sibs = in_flight[gi]
        (lands, sibs), after = lax.optimization_barrier(((lands, sibs), after))
        for a, mine, got, passed in zip(GATHER_GROUPS[gi], halves, lands, sibs):
            weight, layer, full_shape, split, perm = BIG_ARRAYS[a]
            half_mine = jnp.where(is_my_chip, jnp.where(core == 0, mine[0], mine[1])[None], got)
            half_other = jnp.where(is_my_chip, jnp.where(core == 0, mine[1], mine[0])[None], passed)
            value = by_core(half_mine, half_other).transpose(tuple(np.argsort(perm))).reshape(full_shape)
            if layer is None:
                full[weight] = value
            else:
                full.setdefault(weight, [None, None])[layer] = value
        return after

    reducing = {}

    def reduce_ready(gi, grads, then=None, extra=()):
        def travelling(a):
            split, perm = _travel_layout(a)
            return grads[a] if grads[a].ndim == 4 else grads[a].reshape(split).transpose(perm)

        arrays = [travelling(a) for a in REDUCE_GROUPS[gi]] + list(extra)
        scatter = [True] * len(REDUCE_GROUPS[gi]) + [False] * len(extra)
        reducing[gi], then = reduce_between_cores(arrays, scatter, tag=str(gi), collective_id=REDUCE_COLLECTIVE_ID + 3 * gi, before=then)
        return then

    def reduce_send(gi, then=None):
        reducing[gi], then = reduce_between_chips(reducing[gi], before=then)
        return then

    def stage(name, tensors, grads=None):
        if name == "start":
            launch(0)
            launch(1)
            fillers = (cast_halves(2), cast_halves(3), [full[n] for n in vec_names])
            (bf16_halves[2], bf16_halves[3], gathered_small), tensors = lax.optimization_barrier((fillers, tensors))
            full.update(zip(vec_names, gathered_small))
            return land(0, tensors)
        if name == "normed":
            return launch(3, launch(2, tensors))
        if name in ("mixed", "layer0", "mixed1"):
            return land({"mixed": 1, "layer0": 2, "mixed1": 3}[name], tensors)
        gi = int(name[len("grads")])
        return reduce_ready(gi, grads, tensors) if name.endswith("_ready") else reduce_send(gi, tensors)

    small_names = [n for n in rp_names if n not in BLOCK_WEIGHTS] + vec_names

    loss_part, dx, grads = local_step(x[0], loss_target[0], full, stage)
    small_shapes = [grads[n].shape for n in small_names] + [(1, 1)]
    small_rows = _pack_rows(sum(int(np.prod(s)) for s in small_shapes), 16)
    small = _pack([grads[n] for n in small_names] + [loss_part[:, :1]], small_rows, F32).reshape(2, 1, small_rows // 2, LANES)
    last = len(REDUCE_GROUPS) - 1
    halves_of_blocks = [grads[n].reshape(2, 1, LRU_BLOCKS * HEAD // 2, HEAD) for n in BLOCK_WEIGHTS]
    reduce_ready(last, grads, extra=[small] + halves_of_blocks)
    reduce_send(last)
    reduced, result = {}, {}

    def finish(gi, after):
        g_own, g_sib = reduce_finish(reducing[gi], after)
        reduced.update(zip(list(REDUCE_GROUPS[gi]) + ["small"] + list(BLOCK_WEIGHTS), zip(g_own, g_sib)))

    def update(n):
        if n in TRANSPOSED:
            n_rows, n_cols = given[n].shape[2], given[n].shape[1]

            def rows(t):
                return jnp.swapaxes(t, 1, 2).reshape(n_rows, 1, n_cols)

            def back(t):
                return jnp.swapaxes(t.reshape(1, n_rows, n_cols), 1, 2)

            g_rows = jnp.swapaxes(by_core(*reduced[n]), 0, 1).reshape(n_rows, 1, n_cols)
            result[n] = tuple(back(t) for t in adamw_rows(rows(given[n]), g_rows, rows(mom1[n]), rows(mom2[n]), name=f"adamw_{n}"))
            return
        done = None
        for a in (k for k, spec in BIG_ARRAYS.items() if spec[0] == n):
            r, cols = reduced[a][0].shape
            layer = BIG_ARRAYS[a][1] or 0
            w3, m3, v3 = (t if BIG_ARRAYS[a][1] is not None else t.reshape(1, 2 * r, cols) for t in (given[n], mom1[n], mom2[n]))
            done = adamw_halves(w3, m3, v3, *reduced[a], layer=layer, prev=done, name=f"adamw_{a}")
        result[n] = done

    updated = []
    for gi in range(last + 1):
        finish(gi, tuple(result[n][0] for n in updated) if updated else (dx, reducing[last][1]))
        for n in MATMUL_SHARDED:
            if n not in updated and all(a in reduced for a, spec in BIG_ARRAYS.items() if spec[0] == n):
                update(n)
                updated.append(n)

    for n in BLOCK_WEIGHTS:
        w3, m3, v3 = (t.reshape(1, LRU_BLOCKS * HEAD, HEAD) for t in (given[n], mom1[n], mom2[n]))
        result[n] = adamw_halves(w3, m3, v3, *reduced[n], name=f"adamw_{n}")

    *small_sums, loss_sum = _unpack(by_core(*reduced["small"]).reshape(small_rows, LANES), small_shapes)
    loss = loss_sum[0, 0]
    g_small = dict(zip(small_names, small_sums))
    for n in vec_names:
        size = local[n].shape[SHARDED[n]]
        g_small[n] = lax.dynamic_slice_in_dim(g_small[n], chip * size, size, axis=SHARDED[n])
    views = [[_local_view(n, src[n]) for n in small_names] for src in (given, mom1, mom2)]
    d_s, m_s, v_s = adamw_many(views[0], [g_small[n] for n in small_names], views[1], views[2], name="adamw_small")
    for n, d, nm, nv in zip(small_names, d_s, m_s, v_s):
        result[n] = (g_small[n], d, nm, nv)

    outs = [[result[n][k].reshape(given[n].shape) for n in WEIGHTS] for k in range(4)]
    return (loss, dx[None], *outs[0], *outs[1], *outs[2], *outs[3])
```

```python
import functools

import numpy as np
import jax
import jax.numpy as jnp
from jax import lax
from jax.experimental import pallas as pl
from jax.experimental.pallas import tpu as pltpu
from jax.experimental.pallas import tpu_sc as plsc

F32 = jnp.float32
BF16 = jnp.bfloat16
MESH = pl.DeviceIdType.MESH

SEQ = 2048
D_MODEL = 1024
N_HEADS = 4
HEAD = 128
RET_CHUNK = 128
RET_CHUNKS_PER_STEP = 2
GDN_CHUNK = 64
GDN_CHUNKS_PER_STEP = 8
GROUP = N_HEADS * HEAD
MIX_MAIN = 8 * GROUP
D_FF = 2816
LRU_BLOCKS = 8
LRU_C = 8.0
ROPE_BASE = 10000.0
EPS = 1e-6
N_SHARD = 4
LANES = 128

ADAM_LR, ADAM_B1, ADAM_B2, ADAM_EPS, ADAM_WD, ADAM_STEP = 0.001, 0.9, 0.999, 1e-08, 0.01, 10

VMEM_LIMIT_BYTES = 56 * 1024 * 1024

_roll = pltpu.roll


def _params(**kw):
    return pltpu.CompilerParams(vmem_limit_bytes=VMEM_LIMIT_BYTES, **kw)


def _sds(shape, dtype):
    return jax.ShapeDtypeStruct(tuple(shape), dtype)


def _shift_raw(x, d):
    n = x.shape[0]
    t = lax.broadcasted_iota(jnp.int32, x.shape, 0)
    if d > 0:
        return jnp.where(t >= d, _roll(x, d, 0), 0.0)
    return jnp.where(t < n + d, _roll(x, n + d, 0), 0.0)


@functools.partial(jax.custom_vjp, nondiff_argnums=(1,))
def shift_rows(x, d):
    return _shift_raw(x, d)


def _shift_fwd(x, d):
    return _shift_raw(x, d), None


def _shift_bwd(d, _, g):
    return (_shift_raw(g, -d),)


shift_rows.defvjp(_shift_fwd, _shift_bwd)


@jax.custom_vjp
def swap_halves(x):
    return _roll(x, HEAD // 2, 1)


def _swap_fwd(x):
    return _roll(x, HEAD // 2, 1), None


def _swap_bwd(_, g):
    return (_roll(g, HEAD // 2, 1),)


swap_halves.defvjp(_swap_fwd, _swap_bwd)


SCAN_BLOCK_ROWS = 16


def _scan_block(a, u, reverse):
    n = a.shape[0]
    t = lax.broadcasted_iota(jnp.int32, a.shape, 0)
    d = 1
    while d < n:
        if reverse:
            m = t < n - d
            a_s, u_s = _roll(a, n - d, 0), _roll(u, n - d, 0)
        else:
            m = t >= d
            a_s, u_s = _roll(a, d, 0), _roll(u, d, 0)
        u = a * jnp.where(m, u_s, 0.0) + u
        a = a * jnp.where(m, a_s, 1.0)
        d *= 2
    return a, u


def _scan_raw(a, u, reverse):
    n = a.shape[0]
    blocks = range(n // SCAN_BLOCK_ROWS)
    out = [None] * len(blocks)
    entering = None
    for b in (reversed(blocks) if reverse else blocks):
        rows = slice(b * SCAN_BLOCK_ROWS, (b + 1) * SCAN_BLOCK_ROWS)
        a_run, h = _scan_block(a[rows], u[rows], reverse)
        if entering is not None:
            h = a_run * entering + h
        out[b] = h
        entering = h[:1] if reverse else h[SCAN_BLOCK_ROWS - 1:]
    return jnp.concatenate(out, axis=0)


@jax.custom_vjp
def lin_scan(a, u):
    return _scan_raw(a, u, False)


def _lin_scan_fwd(a, u):
    hs = _scan_raw(a, u, False)
    return hs, (a, hs)


def _lin_scan_bwd(res, g):
    a, hs = res
    lam = _scan_raw(_shift_raw(a, -1), g, True)
    return lam * _shift_raw(hs, 1), lam


lin_scan.defvjp(_lin_scan_fwd, _lin_scan_bwd)


def _bdot(a, b, dims=(((1,), (0,)), ((), ()))):
    return lax.dot_general(a.astype(BF16), b.astype(BF16), dims, preferred_element_type=F32)


def _each(f, *seqs):
    return tuple(f(*a) for a in zip(*seqs))


def _split_bf16(a):
    hi = a.astype(BF16)
    return hi, (a - hi.astype(F32)).astype(BF16)


def _dot3_raw(a_s, b_s):
    a_hl = _each(_split_bf16, a_s)
    b_hl = _each(_split_bf16, b_s)
    hh = _each(lambda a, b: _bdot(a[0], b[0]), a_hl, b_hl)
    hl = _each(lambda a, b: _bdot(a[0], b[1]), a_hl, b_hl)
    lh = _each(lambda a, b: _bdot(a[1], b[0]), a_hl, b_hl)
    return _each(lambda x, y, z: x + (y + z), hh, hl, lh)


@jax.custom_vjp
def dot3(a_s, b_s):
    return _dot3_raw(a_s, b_s)


def _dot3_fwd(a_s, b_s):
    return _dot3_raw(a_s, b_s), (a_s, b_s)


def _dot3_bwd(res, g_s):
    a_s, b_s = res
    return (_each(lambda g, b: _bdot(g, b, (((1,), (1,)), ((), ()))), g_s, b_s),
            _each(lambda a, g: _bdot(a, g, (((0,), (0,)), ((), ()))), a_s, g_s))


dot3.defvjp(_dot3_fwd, _dot3_bwd)


def _eye(n):
    i = lax.broadcasted_iota(jnp.int32, (n, n), 0)
    j = lax.broadcasted_iota(jnp.int32, (n, n), 1)
    return (i == j).astype(F32)


def _unit_lower_inverse_raw(lmats):
    n = lmats[0].shape[0]
    eye = _eye(n)
    ps = _each(lambda l: -l, lmats)
    invs = _each(lambda x: eye + x, ps)
    k = 1
    while 2 * k < n:
        ps = _each(lambda p: _bdot(p, p), ps)
        invs = _each(lambda inv, p: inv + _bdot(inv, p), invs, ps)
        k *= 2
    prods = _dot3_raw(lmats, invs)
    resids = _each(lambda inv, pr: eye - inv - pr, invs, prods)
    return _each(lambda inv, r: inv + _bdot(inv, r), invs, resids)


@jax.custom_vjp
def unit_lower_inverse(lmats):
    return _unit_lower_inverse_raw(lmats)


def _uli_fwd(lmats):
    invs = _unit_lower_inverse_raw(lmats)
    return invs, invs


def _uli_bwd(invs, g_s):
    ms = _each(lambda inv, g: _bdot(inv, g, (((0,), (0,)), ((), ()))), invs, g_s)
    return (_each(lambda m, inv: -_bdot(m, inv, (((1,), (1,)), ((), ()))), ms, invs),)


unit_lower_inverse.defvjp(_uli_fwd, _uli_bwd)


def _cumsum_raw(x, reverse):
    n = x.shape[0]
    t = lax.broadcasted_iota(jnp.int32, x.shape, 0)
    d = 1
    while d < n:
        if reverse:
            x = x + jnp.where(t < n - d, _roll(x, n - d, 0), 0.0)
        else:
            x = x + jnp.where(t >= d, _roll(x, d, 0), 0.0)
        d *= 2
    return x


@jax.custom_vjp
def cumsum_rows(x):
    return _cumsum_raw(x, False)


def _cumsum_fwd(x):
    return _cumsum_raw(x, False), None


def _cumsum_bwd(_, g):
    return (_cumsum_raw(g, True),)


cumsum_rows.defvjp(_cumsum_fwd, _cumsum_bwd)


_NT = (((1,), (1,)), ((), ()))
_TN = (((0,), (0,)), ((), ()))


def _softplus(x):
    return jnp.maximum(x, 0.0) + jnp.log1p(jnp.exp(-jnp.abs(x)))


def _expm1_nonpos(x):
    poly = x * (1.0 + x * (0.5 + x * (1.0 / 6 + x * (1.0 / 24 + x * (1.0 / 120 + x * (1.0 / 720))))))
    return jnp.where(x > -0.25, poly, jnp.exp(x) - 1.0)


def _rms(x):
    return x * lax.rsqrt(jnp.mean(x * x, axis=-1, keepdims=True) + EPS)


def _causal_conv(x, w, width):
    y = w[width - 1:width, :] * x
    for j in range(width - 1):
        y = y + w[j:j + 1, :] * shift_rows(x, width - 1 - j)
    return y


def _norm_fn(x, g):
    return _rms(x) * g


def _ffn_act_fn(ug, uv, wg, wv, bg, bv):
    return jax.nn.silu(_causal_conv(ug, wg, 3) + bg) * (_causal_conv(uv, wv, 3) + bv)


def _gdn_conv_fn(x, w):
    return jax.nn.silu(_causal_conv(x, w, 4))


def _lru_fn(gate, x, cw, cb, wa, ba, wx, bx, lam):
    xr = _causal_conv(x, cw, 4) + cb
    r = jax.nn.sigmoid(_bdot(xr, wa) + ba)
    i = jax.nn.sigmoid(_bdot(xr, wx) + bx)
    log_a = -LRU_C * r * _softplus(-lam)
    a = jnp.exp(log_a)
    u = jnp.sqrt(-_expm1_nonpos(2.0 * log_a)) * (i * xr)
    hs = lin_scan(a, u)
    return jax.nn.gelu(gate) * hs


def _ret_fn(qs, ks, vs, gates, states, cos2, sin2, dmasks, ktails, qdecs, cdecs):
    c = RET_CHUNK
    n_heads = len(qs)
    n_chunks = qs[0].shape[0] // c
    units = tuple((ci, h) for ci in range(n_chunks) for h in range(n_heads))

    def rows(x, ci):
        return x[ci * c:(ci + 1) * c]

    qrs = tuple(rows(qs[h], ci) * rows(cos2, ci) + swap_halves(rows(qs[h], ci)) * rows(sin2, ci) for ci, h in units)
    krs = tuple((rows(ks[h], ci) * rows(cos2, ci) + swap_halves(rows(ks[h], ci)) * rows(sin2, ci)) * (HEAD ** -0.5) for ci, h in units)
    vus = tuple(rows(vs[h], ci) for ci, h in units)
    scores = tuple(_bdot(q, k, _NT) * dmasks[h] for q, k, (_, h) in zip(qrs, krs, units))
    intra = _each(lambda sc, v: _bdot(sc, v), scores, vus)
    outs = []
    for ci in range(n_chunks):
        mine = slice(ci * n_heads, (ci + 1) * n_heads)
        inter = _each(lambda q, d, s: _bdot(q * d, s), qrs[mine], qdecs, states)
        outs.append(_each(lambda a, b: a + b, intra[mine], inter))
        states = _each(lambda s, cd, k, kt, v: s * cd + _bdot(k * kt, v, _TN), states, cdecs, krs[mine], ktails, vus[mine])
    ys = tuple(_rms(jnp.concatenate([outs[ci][h] for ci in range(n_chunks)], axis=0)) * jax.nn.silu(gates[h]) for h in range(n_heads))
    return ys, states


def _pick_lane(x, lane_idx):
    lane = lax.broadcasted_iota(jnp.int32, x.shape, 1)
    return jnp.sum(jnp.where(lane == lane_idx, x, 0.0), axis=1, keepdims=True)


def _l2norm(x):
    return x * lax.rsqrt(jnp.sum(x * x, axis=-1, keepdims=True) + EPS)


def _gdn_fn(qcs, kcs, vcs, gates, small, a_log, dt_bias, gain, states):
    c = GDN_CHUNK
    n_heads = len(qcs)
    n_chunks = qcs[0].shape[0] // c
    units = tuple((ci, h) for ci in range(n_chunks) for h in range(n_heads))

    def unit_rows(per_head):
        return tuple(per_head[h][ci * c:(ci + 1) * c] for ci, h in units)

    smalls = tuple(small[ci * c:(ci + 1) * c] for ci, _ in units)
    heads = tuple(h for _, h in units)
    intra = _gdn_intra(unit_rows(qcs), unit_rows(kcs), unit_rows(vcs), smalls, heads, a_log, dt_bias)
    outs = []
    for ci in range(n_chunks):
        mine = slice(ci * n_heads, (ci + 1) * n_heads)
        os_, states = _gdn_inter(*(part[mine] for part in intra), states)
        outs.append(os_)
    ys = tuple(_rms(jnp.concatenate([outs[ci][h] for ci in range(n_chunks)], axis=0)) * gain * jax.nn.silu(gates[h])
               for h in range(n_heads))
    return ys, states


def _gdn_inter(qs, ks, us, ws, attns, gcs, g_lasts, states):
    v_news = _each(lambda u, w, s: u - _bdot(w, s), us, ws, states)
    inter = _each(lambda q, gc, s: _bdot(q * jnp.exp(gc), s), qs, gcs, states)
    os_ = _each(lambda x, a, v: x + _bdot(a, v), inter, attns, v_news)
    new_states = _each(lambda s, gl, k, gc, v: s * jnp.exp(gl) + _bdot(k * jnp.exp(gl - gc), v, _TN), states, g_lasts, ks, gcs, v_news)
    return os_, new_states


def _gdn_intra(qcs, kcs, vcs, smalls, heads, a_log, dt_bias):
    c = GDN_CHUNK
    qs = _each(lambda x: _l2norm(x) * (HEAD ** -0.5), qcs)
    ks = _each(_l2norm, kcs)
    betas = _each(lambda sm, h: jax.nn.sigmoid(_pick_lane(sm, h)), smalls, heads)
    gs = _each(lambda sm, h: -jnp.exp(_pick_lane(a_log, h)) * _softplus(_pick_lane(sm, h + N_HEADS) + _pick_lane(dt_bias, h)),
               smalls, heads)
    i = lax.broadcasted_iota(jnp.int32, (c, c), 0)
    j = lax.broadcasted_iota(jnp.int32, (c, c), 1)
    tril = i >= j
    gcs = _each(lambda g: cumsum_rows(jnp.broadcast_to(g, (c, LANES)))[:, :1], gs)
    gc_rows = _each(lambda gc: jnp.broadcast_to(gc, (c, c)), gcs)
    decays = _each(lambda r: jnp.where(tril, jnp.exp(jnp.where(tril, r - r.T, 0.0)), 0.0), gc_rows)
    kbs = _each(lambda k, b: k * b, ks, betas)
    lmats = _each(lambda kb, k, d: jnp.where(i > j, _bdot(kb, k, _NT) * d, 0.0), kbs, ks, decays)
    attns = _each(lambda q, k, d: jnp.where(tril, _bdot(q, k, _NT) * d, 0.0), qs, ks, decays)
    invs = unit_lower_inverse(lmats)
    us = dot3(invs, _each(lambda v, b: v * b, vcs, betas))
    ws = dot3(invs, _each(lambda kb, gc: kb * jnp.exp(gc), kbs, gcs))
    g_lasts = _each(lambda g: jnp.sum(g, axis=0, keepdims=True), gs)
    return qs, ks, us, ws, attns, gcs, g_lasts


def _final_fn(h, g, target):
    y = _rms(h) * g
    return 0.5 * jnp.sum(jnp.mean(jnp.square(y - target), axis=-1, keepdims=True), axis=0, keepdims=True)


def _tile(n, candidates):
    for t in candidates:
        if n % t == 0:
            return t
    raise ValueError(f"no tile for {n}")


MATMUL_RESIDENT_LHS_BYTES = 12 * 1024 * 1024


def matmul(a, b, *, ta=False, tb=False, add=None, more=None, out_dtype=F32, tm=None, tn=None, split=None, column_halves=None, name):
    m = a.shape[1] if ta else a.shape[0]
    k = a.shape[0] if ta else a.shape[1]
    n = b.shape[0] if tb else b.shape[1]
    assert k == (b.shape[1] if tb else b.shape[0])
    out_shape, out_block, out_index = (m, n), None, lambda i, j: (i, j)
    if split is not None:
        dims4, perm = split
        out_shape = tuple(dims4[p] for p in perm)
        r, cols = out_shape[2:]
        tm, tn = m, tn or _tile(cols, (1408, 512))
        cb = cols // tn
        if perm == (0, 2, 1, 3):
            out_block, out_index = (2, None, r, tn), lambda i, j: (0, j // cb, 0, j % cb)
        elif perm == (1, 0, 2, 3):
            out_block, out_index = (2, N_SHARD, r, tn), lambda i, j: (0, 0, 0, j)
        else:
            raise ValueError(perm)
    if tm is None and not ta and m * k * a.dtype.itemsize <= MATMUL_RESIDENT_LHS_BYTES:
        tm = m
    tm = tm or _tile(m, (1024, 512, 1408, 256, 128))
    tn = tn or _tile(n, (512, 1408, 256, 128))
    aliases, prev, keep_rows = {}, None, None
    if column_halves is not None:
        total_rows, first_row, keep_rows, prev = column_halves
        tn = n // 2
        rows_out = keep_rows or tm
        out_shape, out_block = (2, total_rows, tn), (None, rows_out, tn)
        out_index = lambda i, j: (j, first_row // rows_out + i, 0)
    dims = (((0 if ta else 1,), (1 if tb else 0,)), ((), ()))

    def body(a_ref, b_ref, *rest):
        acc = lax.dot_general(a_ref[...].astype(BF16), b_ref[...].astype(BF16), dims, preferred_element_type=F32)
        if more is not None:
            acc = acc + _bdot(rest[0][...], rest[1][...])
        if add is not None:
            acc = acc + rest[2 if more is not None else 0][...]
        o_ref = rest[-1]
        acc = acc.astype(out_dtype)
        if split is not None and split[1] == (1, 0, 2, 3):
            rows = o_ref.shape[2]
            for s in range(N_SHARD):
                for h in range(2):
                    o_ref[h, s] = acc[(2 * s + h) * rows:(2 * s + h + 1) * rows]
        elif keep_rows is not None:
            o_ref[...] = acc[:keep_rows]
        else:
            o_ref[...] = acc.reshape(o_ref.shape)

    a_spec = pl.BlockSpec((k, tm), lambda i, j: (0, i)) if ta else pl.BlockSpec((tm, k), lambda i, j: (i, 0))
    b_spec = pl.BlockSpec((tn, k), lambda i, j: (j, 0)) if tb else pl.BlockSpec((k, tn), lambda i, j: (0, j))
    o_spec = pl.BlockSpec(out_block or (tm, tn), out_index)
    in_specs, args = [a_spec, b_spec], [a, b]
    if more is not None:
        k2 = more[0].shape[1]
        in_specs += [pl.BlockSpec((tm, k2), lambda i, j: (i, 0)), pl.BlockSpec((k2, tn), lambda i, j: (0, j))]
        args += list(more)
    if add is not None:
        in_specs.append(o_spec)
        args.append(add)
    if prev is not None:
        aliases = {len(args): 0}
        in_specs.append(pl.BlockSpec(memory_space=pl.ANY))
        args.append(prev)
    return pl.pallas_call(body, out_shape=_sds(out_shape, out_dtype), grid=(m // tm, n // tn), in_specs=in_specs,
                          out_specs=o_spec, input_output_aliases=aliases, compiler_params=_params(), name=name)(*args)


def norm_matmul(x, g, b, *, tb=False, also=None, name):
    t, k = x.shape
    n = b.shape[0] if tb else b.shape[1]
    tn = _tile(n, (512, 1408, 256, 128))
    dims = (((1,), (1 if tb else 0,)), ((), ()))

    def body(x_ref, g_ref, b_ref, *rest):
        o_ref, hn_ref = rest[-3:-1] if also is not None else rest[-2:]

        @pl.when(pl.program_id(0) == 0)
        def _():
            hn = _norm_fn(x_ref[...], g_ref[...]).astype(BF16)
            hn_ref[...] = hn
            if also is not None:
                rest[-1][...] = _bdot(hn, rest[0][...], _NT)

        o_ref[...] = lax.dot_general(hn_ref[...], b_ref[...].astype(BF16), dims, preferred_element_type=F32)

    b_spec = pl.BlockSpec((tn, k), lambda j: (j, 0)) if tb else pl.BlockSpec((k, tn), lambda j: (0, j))
    whole = pl.BlockSpec((t, k), lambda j: (0, 0))
    in_specs, args = [whole, pl.BlockSpec((1, k), lambda j: (0, 0)), b_spec], [x, g, b]
    out_shape, out_specs = [_sds((t, n), F32), _sds((t, k), BF16)], [pl.BlockSpec((t, tn), lambda j: (0, j)), whole]
    if also is not None:
        n2 = also.shape[0]
        in_specs.append(pl.BlockSpec((n2, k), lambda j: (0, 0)))
        args.append(also)
        out_shape.append(_sds((t, n2), F32))
        out_specs.append(pl.BlockSpec((t, n2), lambda j: (0, 0)))
    return pl.pallas_call(body, out_shape=out_shape, grid=(n // tn,), in_specs=in_specs, out_specs=out_specs,
                          compiler_params=_params(), name=name)(*args)


ROW_TILE = 512


def norm_bwd(x, g, dy, dres, *, name):
    t, d = x.shape

    def body(x_ref, g_ref, dy_ref, dres_ref, dx_ref, dg_ref):
        _, vjp = jax.vjp(_norm_fn, x_ref[...], g_ref[...])
        dx, dg = vjp(dy_ref[...])
        dx_ref[...] = dx + dres_ref[...]

        @pl.when(pl.program_id(0) == 0)
        def _():
            dg_ref[...] = jnp.zeros_like(dg_ref)

        dg_ref[...] += dg

    row = pl.BlockSpec((ROW_TILE, d), lambda i: (i, 0))
    vec = pl.BlockSpec((1, d), lambda i: (0, 0))
    return pl.pallas_call(body, out_shape=(_sds((t, d), F32), _sds((1, d), F32)), grid=(t // ROW_TILE,),
                          in_specs=[row, vec, row, row], out_specs=(row, vec), compiler_params=_params(), name=name)(x, g, dy, dres)


def final_fwd_bwd(h, g, target, *, name):
    t, d = h.shape

    def body(h_ref, g_ref, t_ref, loss_ref, dh_ref, dg_ref):
        tgt = t_ref[...]
        loss, vjp = jax.vjp(lambda hh, gg: _final_fn(hh, gg, tgt), h_ref[...], g_ref[...])
        dh, dg = vjp(jnp.ones((1, 1), F32))
        dh_ref[...] = dh

        @pl.when(pl.program_id(0) == 0)
        def _():
            dg_ref[...] = jnp.zeros_like(dg_ref)
            loss_ref[...] = jnp.zeros_like(loss_ref)

        dg_ref[...] += dg
        loss_ref[...] += jnp.broadcast_to(loss, loss_ref.shape)

    row = pl.BlockSpec((ROW_TILE, d), lambda i: (i, 0))
    vec = pl.BlockSpec((1, d), lambda i: (0, 0))
    return pl.pallas_call(body, out_shape=(_sds((1, LANES), F32), _sds((t, d), F32), _sds((1, d), F32)), grid=(t // ROW_TILE,),
                          in_specs=[row, vec, row], out_specs=(pl.BlockSpec((1, LANES), lambda i: (0, 0)), row, vec),
                          compiler_params=_params(), name=name)(h, g, target)


FFN_FWD_COLS = 256
FFN_BWD_COLS = 128


def ffn_act_fwd(u, cw, cb, *, name):
    t = u.shape[0]
    w = FFN_FWD_COLS
    nb = D_FF // w

    def body(ug_ref, uv_ref, wg_ref, wv_ref, bg_ref, bv_ref, o_ref):
        o_ref[...] = _ffn_act_fn(ug_ref[...], uv_ref[...], wg_ref[...], wv_ref[...], bg_ref[...], bv_ref[...]).astype(BF16)

    def col(rows, off):
        return pl.BlockSpec((rows, w), lambda j: (0, j + off))

    return pl.pallas_call(body, out_shape=_sds((t, D_FF), BF16), grid=(nb,),
                          in_specs=[col(t, 0), col(t, nb), col(3, 0), col(3, nb), col(1, 0), col(1, nb)],
                          out_specs=col(t, 0), compiler_params=_params(), name=name)(u, u, cw, cw, cb, cb)


def _put_column_blocks(step, n_steps, blocks, dst_ref, width, stage_ref, sems):
    def copies(at):
        slot = at % 2
        return [pltpu.make_async_copy(stage_ref.at[slot, p], dst_ref.at[:, pl.ds(pl.multiple_of((p * n_steps + at) * width, LANES), width)],
                                      sems.at[slot, p]) for p in range(len(blocks))]

    @pl.when(step >= 2)
    def _():
        for cp in copies(step - 2):
            cp.wait()

    for p, value in enumerate(blocks):
        stage_ref[step % 2, p] = value
    for cp in copies(step):
        cp.start()

    @pl.when(step == n_steps - 1)
    def _():
        for cp in copies(step - 1) + copies(step):
            cp.wait()


def ffn_act_bwd(u, cw, cb, da, *, name):
    t = u.shape[0]
    w = FFN_BWD_COLS
    nb = D_FF // w

    def body(ug_ref, uv_ref, wg_ref, wv_ref, bg_ref, bv_ref, da_ref, dug_ref, duv_ref, dwg_ref, dwv_ref, dbg_ref, dbv_ref):
        _, vjp = jax.vjp(_ffn_act_fn, ug_ref[...], uv_ref[...], wg_ref[...], wv_ref[...], bg_ref[...], bv_ref[...])
        dug, duv, dwg, dwv, dbg, dbv = vjp(da_ref[...])
        dug_ref[...] = dug.astype(BF16)
        duv_ref[...] = duv.astype(BF16)
        dwg_ref[...] = dwg
        dwv_ref[...] = dwv
        dbg_ref[...] = dbg
        dbv_ref[...] = dbv

    def col(rows, off):
        return pl.BlockSpec((rows, w), lambda j: (0, j + off))

    outs = pl.pallas_call(
        body, out_shape=(_sds((t, D_FF), BF16), _sds((t, D_FF), BF16), _sds((3, D_FF), F32), _sds((3, D_FF), F32),
                         _sds((1, D_FF), F32), _sds((1, D_FF), F32)),
        grid=(nb,), in_specs=[col(t, 0), col(t, nb), col(3, 0), col(3, nb), col(1, 0), col(1, nb), col(t, 0)],
        out_specs=(col(t, 0), col(t, 0), col(3, 0), col(3, 0), col(1, 0), col(1, 0)), compiler_params=_params(), name=name,
    )(u, u, cw, cw, cb, cb, da)
    dug, duv, dwg, dwv, dbg, dbv = outs
    return jnp.concatenate([dug, duv], axis=1), jnp.concatenate([dwg, dwv], axis=1), jnp.concatenate([dbg, dbv], axis=1)


GDN_CONV_COLS = 256
GDN_CONV_OFF = 4 * GROUP


def gdn_conv_fwd(p, cw, *, name):
    t = p.shape[0]
    w = GDN_CONV_COLS
    nb = 3 * GROUP // w
    off = GDN_CONV_OFF // w

    def body(x_ref, w_ref, o_ref):
        o_ref[...] = _gdn_conv_fn(x_ref[...], w_ref[...])

    return pl.pallas_call(body, out_shape=_sds((t, 3 * GROUP), F32), grid=(nb,),
                          in_specs=[pl.BlockSpec((t, w), lambda j: (0, j + off)), pl.BlockSpec((4, w), lambda j: (0, j))],
                          out_specs=pl.BlockSpec((t, w), lambda j: (0, j)), compiler_params=_params(), name=name)(p, cw)


def gdn_conv_bwd(p, cw, dc, *, name):
    t = p.shape[0]
    w = GDN_CONV_COLS
    nb = 3 * GROUP // w
    off = GDN_CONV_OFF // w

    def body(x_ref, w_ref, dc_ref, dx_ref, dw_ref):
        _, vjp = jax.vjp(_gdn_conv_fn, x_ref[...], w_ref[...])
        dx, dw = vjp(dc_ref[...])
        dx_ref[...] = dx.astype(BF16)
        dw_ref[...] = dw

    blk = pl.BlockSpec((t, w), lambda j: (0, j))
    wblk = pl.BlockSpec((4, w), lambda j: (0, j))
    return pl.pallas_call(body, out_shape=(_sds((t, 3 * GROUP), BF16), _sds((4, 3 * GROUP), F32)), grid=(nb,),
                          in_specs=[pl.BlockSpec((t, w), lambda j: (0, j + off)), wblk, blk], out_specs=(blk, wblk),
                          compiler_params=_params(), name=name)(p, cw, dc)


def _lru_specs(t):
    w = D_MODEL // LRU_BLOCKS
    gate = pl.BlockSpec((t, w), lambda j: (0, j))
    xin = pl.BlockSpec((t, w), lambda j: (0, j + LRU_BLOCKS))
    cw = pl.BlockSpec((4, w), lambda j: (0, j))
    vec = pl.BlockSpec((1, w), lambda j: (0, j))
    mat = pl.BlockSpec((None, w, w), lambda j: (j, 0, 0))
    return gate, xin, cw, vec, mat


def lru_fwd(gx, cw, cb, wa, ba, wx, bx, lam, *, name):
    t = gx.shape[0]
    gate, xin, cws, vec, mat = _lru_specs(t)

    def body(g_ref, x_ref, cw_ref, cb_ref, wa_ref, ba_ref, wx_ref, bx_ref, lam_ref, o_ref):
        o_ref[...] = _lru_fn(g_ref[...], x_ref[...], cw_ref[...], cb_ref[...], wa_ref[...], ba_ref[...], wx_ref[...],
                             bx_ref[...], lam_ref[...]).astype(BF16)

    return pl.pallas_call(body, out_shape=_sds((t, D_MODEL), BF16), grid=(LRU_BLOCKS,),
                          in_specs=[gate, xin, cws, vec, mat, vec, mat, vec, vec], out_specs=gate,
                          compiler_params=_params(), name=name)(gx, gx, cw, cb, wa, ba, wx, bx, lam)


def lru_bwd(gx, cw, cb, wa, ba, wx, bx, lam, dy, *, name):
    t = gx.shape[0]
    gate, xin, cws, vec, mat = _lru_specs(t)

    def body(g_ref, x_ref, cw_ref, cb_ref, wa_ref, ba_ref, wx_ref, bx_ref, lam_ref, dy_ref,
             dgx_ref, dcw_ref, dcb_ref, dwa_ref, dba_ref, dwx_ref, dbx_ref, dlam_ref, stage_ref, sems):
        _, vjp = jax.vjp(_lru_fn, g_ref[...], x_ref[...], cw_ref[...], cb_ref[...], wa_ref[...], ba_ref[...], wx_ref[...],
                         bx_ref[...], lam_ref[...])
        dg, dx, dcw, dcb, dwa, dba, dwx, dbx, dlam = vjp(dy_ref[...])
        _put_column_blocks(pl.program_id(0), LRU_BLOCKS, (dg.astype(BF16), dx.astype(BF16)), dgx_ref, D_MODEL // LRU_BLOCKS, stage_ref, sems)
        dcw_ref[...] = dcw
        dcb_ref[...] = dcb
        dwa_ref[...] = dwa
        dba_ref[...] = dba
        dwx_ref[...] = dwx
        dbx_ref[...] = dbx
        dlam_ref[...] = dlam

    d = D_MODEL
    w = d // LRU_BLOCKS
    out_shape = (_sds((t, 2 * d), BF16), _sds((4, d), F32), _sds((1, d), F32), _sds((LRU_BLOCKS, w, w), F32),
                 _sds((1, d), F32), _sds((LRU_BLOCKS, w, w), F32), _sds((1, d), F32), _sds((1, d), F32))
    return pl.pallas_call(body, out_shape=out_shape, grid=(LRU_BLOCKS,),
                          in_specs=[gate, xin, cws, vec, mat, vec, mat, vec, vec, gate],
                          out_specs=(pl.BlockSpec(memory_space=pl.ANY), cws, vec, mat, vec, mat, vec, vec),
                          scratch_shapes=[pltpu.VMEM((2, 2, t, w), BF16), pltpu.SemaphoreType.DMA((2, 2))],
                          compiler_params=_params(), name=name)(gx, gx, cw, cb, wa, ba, wx, bx, lam, dy)


def _ret_tables():
    half = HEAD // 2
    inv_freq = (np.float32(ROPE_BASE) ** (-np.arange(half, dtype=np.float32) / np.float32(half))).astype(np.float32)
    ang = (np.arange(SEQ, dtype=np.float32)[:, None] * inv_freq[None, :]).astype(np.float64)
    cos2 = np.concatenate([np.cos(ang), np.cos(ang)], axis=1).astype(np.float32)
    sin2 = np.concatenate([-np.sin(ang), np.sin(ang)], axis=1).astype(np.float32)
    c = RET_CHUNK
    log_gamma = np.log1p(-np.exp2(-5.0 - np.arange(N_HEADS, dtype=np.float64)))
    idx = np.arange(c, dtype=np.float64)
    rel = idx[:, None] - idx[None, :]
    dmask = np.where(rel >= 0, np.exp(log_gamma[:, None, None] * np.maximum(rel, 0.0)), 0.0)
    ones = np.ones((N_HEADS, c, HEAD))
    ktail = np.exp(log_gamma[:, None] * (c - 1 - idx))[:, :, None] * ones
    qdec = np.exp(log_gamma[:, None] * (idx + 1.0))[:, :, None] * ones
    cdec = np.exp(log_gamma * c)[:, None, None] * ones
    return tuple(jnp.asarray(a, F32) for a in (cos2, sin2, dmask, ktail, qdec, cdec))


def _ret_specs(rev):
    c = RET_CHUNK * RET_CHUNKS_PER_STEP
    nc = SEQ // c

    def n_of(n):
        return nc - 1 - n if rev else n

    def group(off):
        return pl.BlockSpec((c, GROUP), lambda n: (n_of(n), off))

    tab = pl.BlockSpec((c, HEAD), lambda n: (n_of(n), 0))
    const = pl.BlockSpec((N_HEADS, RET_CHUNK, HEAD), lambda n: (0, 0, 0))
    state = pl.BlockSpec((N_HEADS, None, HEAD, HEAD), lambda n: (0, n_of(n), 0, 0))
    return group, tab, const, state, nc


def _head(h):
    return slice(h * HEAD, (h + 1) * HEAD)


def ret_fwd(p, tables, *, name):
    group, tab, const, state, nc = _ret_specs(False)

    def body(q_ref, k_ref, v_ref, g_ref, cos_ref, sin_ref, dm_ref, kt_ref, qd_ref, cd_ref, y_ref, st_ref, s_scr):
        @pl.when(pl.program_id(0) == 0)
        def _():
            s_scr[...] = jnp.zeros_like(s_scr)

        heads = range(N_HEADS)
        states = tuple(s_scr[h] for h in heads)
        ys, new_states = _ret_fn(*(tuple(r[:, _head(h)] for h in heads) for r in (q_ref, k_ref, v_ref, g_ref)), states,
                                 cos_ref[...], sin_ref[...], *(tuple(r[h] for h in heads) for r in (dm_ref, kt_ref, qd_ref, cd_ref)))
        for h in heads:
            st_ref[h] = states[h]
            y_ref[:, _head(h)] = ys[h].astype(BF16)
            s_scr[h] = new_states[h]

    return pl.pallas_call(
        body, out_shape=(_sds((SEQ, 2 * GROUP), BF16), _sds((N_HEADS, nc, HEAD, HEAD), F32)), grid=(nc,),
        in_specs=[group(0), group(1), group(2), group(3), tab, tab, const, const, const, const],
        out_specs=(group(0), state), scratch_shapes=[pltpu.VMEM((N_HEADS, HEAD, HEAD), F32)], compiler_params=_params(), name=name,
    )(p, p, p, p, *tables)


def ret_bwd(p, tables, states, dy, *, name):
    group, tab, const, state, nc = _ret_specs(True)

    def body(q_ref, k_ref, v_ref, g_ref, cos_ref, sin_ref, dm_ref, kt_ref, qd_ref, cd_ref, st_ref, dy_ref,
             dq_ref, dk_ref, dv_ref, dg_ref, ds_scr):
        @pl.when(pl.program_id(0) == 0)
        def _():
            ds_scr[...] = jnp.zeros_like(ds_scr)

        heads = range(N_HEADS)
        consts = (cos_ref[...], sin_ref[...], *(tuple(r[h] for h in heads) for r in (dm_ref, kt_ref, qd_ref, cd_ref)))
        _, vjp = jax.vjp(lambda *a: _ret_fn(*a, *consts), *(tuple(r[:, _head(h)] for h in heads) for r in (q_ref, k_ref, v_ref, g_ref)),
                         tuple(st_ref[h] for h in heads))
        dqs, dks, dvs, dgs, dss = vjp((tuple(dy_ref[:, _head(h)] for h in heads), tuple(ds_scr[h] for h in heads)))
        for h in heads:
            dq_ref[:, _head(h)] = dqs[h].astype(BF16)
            dk_ref[:, _head(h)] = dks[h].astype(BF16)
            dv_ref[:, _head(h)] = dvs[h].astype(BF16)
            dg_ref[:, _head(h)] = dgs[h].astype(BF16)
            ds_scr[h] = dss[h]

    out = _sds((SEQ, GROUP), BF16)
    return pl.pallas_call(
        body, out_shape=(out, out, out, out), grid=(nc,),
        in_specs=[group(0), group(1), group(2), group(3), tab, tab, const, const, const, const, state, group(0)],
        out_specs=(group(0), group(0), group(0), group(0)), scratch_shapes=[pltpu.VMEM((N_HEADS, HEAD, HEAD), F32)],
        compiler_params=_params(), name=name,
    )(p, p, p, p, *tables, states, dy)


def _gdn_specs(rev):
    c = GDN_CHUNK * GDN_CHUNKS_PER_STEP
    nc = SEQ // c

    def n_of(n):
        return nc - 1 - n if rev else n

    def group(off):
        return pl.BlockSpec((c, GROUP), lambda n: (n_of(n), off))

    small = pl.BlockSpec((c, LANES), lambda n: (n_of(n), 0))
    vec = pl.BlockSpec((1, LANES), lambda n: (0, 0))
    state = pl.BlockSpec((N_HEADS, None, HEAD, HEAD), lambda n: (0, n_of(n), 0, 0))
    qkv = pl.BlockSpec((c, 3 * GROUP), lambda n: (n_of(n), 0))
    return group, small, vec, state, qkv, nc


GDN_GATE_GROUP = 7


def gdn_fwd(conv, p, small, a_log, dt_bias, gain, y_started, *, name):
    group, sm, vec, state, _, nc = _gdn_specs(False)

    def body(q_ref, k_ref, v_ref, g_ref, sm_ref, al_ref, dt_ref, gn_ref, _, y_ref, st_ref, s_scr):
        @pl.when(pl.program_id(0) == 0)
        def _():
            s_scr[...] = jnp.zeros_like(s_scr)

        states = tuple(s_scr[h] for h in range(N_HEADS))
        ys, new_states = _gdn_fn(*(tuple(r[:, _head(h)] for h in range(N_HEADS)) for r in (q_ref, k_ref, v_ref, g_ref)),
                                 sm_ref[...], al_ref[...], dt_ref[...], gn_ref[...], states)
        for h in range(N_HEADS):
            st_ref[h] = states[h]
            y_ref[:, _head(h)] = ys[h].astype(BF16)
            s_scr[h] = new_states[h]

    return pl.pallas_call(
        body, out_shape=(_sds((SEQ, 2 * GROUP), BF16), _sds((N_HEADS, nc, HEAD, HEAD), F32)), grid=(nc,),
        in_specs=[group(0), group(1), group(2), group(GDN_GATE_GROUP), sm, vec, vec, vec, pl.BlockSpec(memory_space=pl.ANY)],
        out_specs=(group(1), state), input_output_aliases={8: 0},
        scratch_shapes=[pltpu.VMEM((N_HEADS, HEAD, HEAD), F32)], compiler_params=_params(), name=name,
    )(conv, conv, conv, p, small, a_log, dt_bias, gain, y_started)


def gdn_bwd(conv, p, small, a_log, dt_bias, gain, states, dy, *, name):
    group, sm, vec, state, qkv, nc = _gdn_specs(True)

    def body(q_ref, k_ref, v_ref, g_ref, sm_ref, al_ref, dt_ref, gn_ref, st_ref, dy_ref,
             dqkv_ref, dg_ref, dsm_ref, dal_ref, ddt_ref, dgn_ref, ds_scr):
        @pl.when(pl.program_id(0) == 0)
        def _():
            ds_scr[...] = jnp.zeros_like(ds_scr)
            dal_ref[...] = jnp.zeros_like(dal_ref)
            ddt_ref[...] = jnp.zeros_like(ddt_ref)
            dgn_ref[...] = jnp.zeros_like(dgn_ref)

        per_head = tuple(tuple(r[:, _head(h)] for h in range(N_HEADS)) for r in (q_ref, k_ref, v_ref, g_ref))
        _, vjp = jax.vjp(_gdn_fn, *per_head, sm_ref[...], al_ref[...], dt_ref[...], gn_ref[...],
                         tuple(st_ref[h] for h in range(N_HEADS)))
        cts = (tuple(dy_ref[:, _head(h)] for h in range(N_HEADS)), tuple(ds_scr[h] for h in range(N_HEADS)))
        dqs, dks, dvs, dgs, dsm, dal, ddt, dgn, dss = vjp(cts)
        for h in range(N_HEADS):
            for part, blocks in enumerate((dqs, dks, dvs)):
                dqkv_ref[:, part * GROUP + h * HEAD:part * GROUP + (h + 1) * HEAD] = blocks[h]
            dg_ref[:, _head(h)] = dgs[h].astype(BF16)
            ds_scr[h] = dss[h]
        dsm_ref[...] = dsm
        dal_ref[...] += dal
        ddt_ref[...] += ddt
        dgn_ref[...] += dgn

    pv = _sds((1, LANES), F32)
    return pl.pallas_call(
        body, out_shape=(_sds((SEQ, 3 * GROUP), F32), _sds((SEQ, GROUP), BF16), _sds((SEQ, LANES), F32), pv, pv, pv), grid=(nc,),
        in_specs=[group(0), group(1), group(2), group(GDN_GATE_GROUP), sm, vec, vec, vec, state, group(1)],
        out_specs=(qkv, group(0), sm, vec, vec, vec), scratch_shapes=[pltpu.VMEM((N_HEADS, HEAD, HEAD), F32)],
        compiler_params=_params(), name=name,
    )(conv, conv, conv, p, small, a_log, dt_bias, gain, states, dy)


ELEMENTWISE_BLOCK_BYTES = 2 * 1024 * 1024


def _row_tile(r, c):
    best = None
    for tr in range(8, r + 1, 8):
        if r % tr == 0 and tr * c * 4 <= ELEMENTWISE_BLOCK_BYTES:
            best = tr
    if best is None:
        raise ValueError(f"no row tile for ({r}, {c})")
    return best


def _tile_2d(r, c):
    if any(r % tr == 0 for tr in range(8, r + 1, 8)):
        return _row_tile(r, c), c
    tc = max(t for t in range(LANES, c + 1, LANES) if c % t == 0 and r * t * 4 <= ELEMENTWISE_BLOCK_BYTES)
    return r, tc


def _core_index():
    return lax.axis_index("c").astype(jnp.int32).reshape(1)


def _chip_index():
    return (2 * lax.axis_index("x") + lax.axis_index("y")).astype(jnp.int32).reshape(1)


def adamw_halves(w, m, v, g_own, g_sib, *, layer=0, prev=None, name):
    n_layers, rows, c = w.shape
    r = rows // 2
    tr = _row_tile(r, c)
    nb = r // tr

    def body(c_ref, w_ref, m_ref, v_ref, own_ref, sib_ref, *rest):
        g_ref, d_ref, nm_ref, nv_ref = rest[-4:]
        gg = jnp.where(pl.program_id(0) == c_ref[0], own_ref[...], sib_ref[...])
        nm = ADAM_B1 * m_ref[...] + (1.0 - ADAM_B1) * gg
        nv = ADAM_B2 * v_ref[...] + (1.0 - ADAM_B2) * jnp.square(gg)
        m_hat = nm / (1.0 - ADAM_B1 ** ADAM_STEP)
        v_hat = nv / (1.0 - ADAM_B2 ** ADAM_STEP)
        g_ref[...] = gg
        d_ref[...] = -ADAM_LR * (m_hat / (jnp.sqrt(v_hat) + ADAM_EPS) + ADAM_WD * w_ref[...])
        nm_ref[...] = nm
        nv_ref[...] = nv

    full = pl.BlockSpec((None, tr, c), lambda h, i, cr: (layer, h * nb + i, 0))
    half = pl.BlockSpec((tr, c), lambda h, i, cr: (i, 0))
    o = _sds((n_layers, rows, c), F32)
    prev = list(prev or ())
    gs = pltpu.PrefetchScalarGridSpec(num_scalar_prefetch=1, grid=(2, nb), in_specs=[full, full, full, half, half] + [_ANY] * len(prev),
                                      out_specs=(full, full, full, full))
    n_fixed = 6
    return pl.pallas_call(body, out_shape=(o, o, o, o), grid_spec=gs, compiler_params=_params(), name=name,
                          input_output_aliases={n_fixed + k: k for k in range(len(prev))})(
        _core_index(), w, m, v, g_own, g_sib, *prev)


ADAMW_ROW_STEPS = 6


def adamw_rows(w, g, m, v, *, name):
    rows, _, cols = w.shape
    tr = rows // ADAMW_ROW_STEPS

    def body(w_ref, g_ref, m_ref, v_ref, g_out_ref, d_ref, nm_ref, nv_ref):
        gg = g_ref[...]
        nm = ADAM_B1 * m_ref[...] + (1.0 - ADAM_B1) * gg
        nv = ADAM_B2 * v_ref[...] + (1.0 - ADAM_B2) * jnp.square(gg)
        m_hat = nm / (1.0 - ADAM_B1 ** ADAM_STEP)
        v_hat = nv / (1.0 - ADAM_B2 ** ADAM_STEP)
        g_out_ref[...] = gg
        d_ref[...] = -ADAM_LR * (m_hat / (jnp.sqrt(v_hat) + ADAM_EPS) + ADAM_WD * w_ref[...])
        nm_ref[...] = nm
        nv_ref[...] = nv

    blk = pl.BlockSpec((tr, 1, cols), lambda i: (i, 0, 0))
    o = _sds(w.shape, F32)
    return pl.pallas_call(body, out_shape=(o, o, o, o), grid=(ADAMW_ROW_STEPS,), in_specs=[blk] * 4, out_specs=(blk, blk, blk, blk),
                          compiler_params=_params(), name=name)(w, g, m, v)


def adamw_many(ws, gs, ms, vs, *, name):
    n = len(ws)

    def body(*refs):
        w_refs, g_refs, m_refs, v_refs, d_refs, nm_refs, nv_refs = (refs[k * n:(k + 1) * n] for k in range(7))
        for i in range(n):
            gg = g_refs[i][...]
            nm = ADAM_B1 * m_refs[i][...] + (1.0 - ADAM_B1) * gg
            nv = ADAM_B2 * v_refs[i][...] + (1.0 - ADAM_B2) * jnp.square(gg)
            m_hat = nm / (1.0 - ADAM_B1 ** ADAM_STEP)
            v_hat = nv / (1.0 - ADAM_B2 ** ADAM_STEP)
            d_refs[i][...] = -ADAM_LR * (m_hat / (jnp.sqrt(v_hat) + ADAM_EPS) + ADAM_WD * w_refs[i][...])
            nm_refs[i][...] = nm
            nv_refs[i][...] = nv

    outs = pl.pallas_call(body, out_shape=[_sds(w.shape, F32) for w in ws] * 3, compiler_params=_params(), name=name)(*ws, *gs, *ms, *vs)
    return outs[:n], outs[n:2 * n], outs[2 * n:]


def add_core_halves(g2, land, *, out_dtype, name):
    _, ns, r, cols = g2.shape
    tr, tc = _tile_2d(r, cols)

    def body(c_ref, a_ref, b_ref, o_ref):
        o_ref[...] = (a_ref[...] + b_ref[...]).astype(out_dtype)

    gs = pltpu.PrefetchScalarGridSpec(
        num_scalar_prefetch=1, grid=(ns, r // tr, cols // tc),
        in_specs=[pl.BlockSpec((None, None, tr, tc), lambda s, i, j, cr: (cr[0], s, i, j)),
                  pl.BlockSpec((None, tr, tc), lambda s, i, j, cr: (s, i, j))],
        out_specs=pl.BlockSpec((None, tr, tc), lambda s, i, j, cr: (s, i, j)))
    return pl.pallas_call(body, out_shape=_sds((ns, r, cols), out_dtype), grid_spec=gs, compiler_params=_params(), name=name)(
        _core_index(), g2, land)


def sum_over_chips(own, land, *, scatter, name):
    _, r, cols = own.shape
    tr, tc = _tile_2d(r, cols)

    def body(mine_ref, own_ref, l0, l1, l2, l3, o_ref):
        mine = mine_ref[0]
        mine_val = own_ref[...]
        acc = None
        for s, l_ref in enumerate((l0, l1, l2, l3)):
            val = jnp.where(mine == s, mine_val, l_ref[...]).astype(F32)
            acc = val if acc is None else acc + val
        o_ref[...] = acc

    def slot(s):
        return pl.BlockSpec((None, tr, tc), lambda i, j, mr: (jnp.where(mr[0] == s, (s + 1) % N_SHARD, s), i, j))

    own_spec = pl.BlockSpec((None, tr, tc), lambda i, j, mr: (mr[0] if scatter else 0, i, j))
    gs = pltpu.PrefetchScalarGridSpec(num_scalar_prefetch=1, grid=(r // tr, cols // tc), in_specs=[own_spec] + [slot(s) for s in range(N_SHARD)],
                                      out_specs=pl.BlockSpec((tr, tc), lambda i, j, mr: (i, j)))
    return pl.pallas_call(body, out_shape=_sds((r, cols), F32), grid_spec=gs, compiler_params=_params(), name=name)(
        _chip_index(), own, land, land, land, land)


_ANY = pl.BlockSpec(memory_space=pl.ANY)


def xy_exchange(src, *, name):
    rh = src.shape[1]

    def body(src_ref, land_ref, send_sems, recv_sems, loc_sem):
        x, y, c = lax.axis_index("x"), lax.axis_index("y"), lax.axis_index("c")
        mine = 2 * x + y
        peers = [(1 - x, y), (x, 1 - y), (1 - x, 1 - y)]

        def copy(k, px, py, dst_slot):
            return pltpu.make_async_remote_copy(src_ref=src_ref.at[c], dst_ref=land_ref.at[dst_slot], send_sem=send_sems.at[k],
                                                recv_sem=recv_sems.at[k], device_id=(px, py, c), device_id_type=MESH)

        keep = pltpu.make_async_copy(src_ref.at[c], land_ref.at[mine], loc_sem)
        keep.start()
        sends = [copy(k, px, py, mine) for k, (px, py) in enumerate(peers)]
        for cp in sends:
            cp.start()
        for cp in sends:
            cp.wait_send()
        for k, (px, py) in enumerate(peers):
            copy(k, px, py, 2 * px + py).wait_recv()
        keep.wait()

    return pl.pallas_call(body, out_shape=_sds((N_SHARD, rh, LANES), src.dtype), in_specs=[_ANY], out_specs=_ANY,
                          scratch_shapes=[pltpu.SemaphoreType.DMA((3,)), pltpu.SemaphoreType.DMA((3,)), pltpu.SemaphoreType.DMA(())],
                          name=name)(src)


def core_exchange(src, *, name):
    def body(src_ref, out_ref, send_sem, recv_sem, loc_sem):
        x, y, c = lax.axis_index("x"), lax.axis_index("y"), lax.axis_index("c")
        keep = pltpu.make_async_copy(src_ref, out_ref.at[c], loc_sem)
        keep.start()
        cp = pltpu.make_async_remote_copy(src_ref=src_ref, dst_ref=out_ref.at[c], send_sem=send_sem, recv_sem=recv_sem,
                                          device_id=(x, y, 1 - c), device_id_type=MESH)
        cp.start()
        cp.wait_send()
        pltpu.make_async_remote_copy(src_ref=src_ref, dst_ref=out_ref.at[1 - c], send_sem=send_sem, recv_sem=recv_sem,
                                     device_id=(x, y, 1 - c), device_id_type=MESH).wait_recv()
        keep.wait()

    return pl.pallas_call(body, out_shape=_sds((2,) + src.shape, src.dtype), in_specs=[_ANY], out_specs=_ANY,
                          scratch_shapes=[pltpu.SemaphoreType.DMA(()), pltpu.SemaphoreType.DMA(()), pltpu.SemaphoreType.DMA(())],
                          name=name)(src)


def _sequencer_call(body, ins, out_shapes, sem_counts, name, collective_id):
    return pl.kernel(body, out_type=list(out_shapes), mesh=plsc.ScalarSubcoreMesh(axis_name="sequencer", num_cores=1), name=name,
                     scratch_types=[pltpu.SemaphoreType.DMA((k,)) for k in sem_counts],
                     compiler_params=pltpu.CompilerParams(collective_id=collective_id))(*ins)


def _handshake(peers):
    barrier = pltpu.get_barrier_semaphore()
    for peer in peers:
        pl.semaphore_signal(barrier, inc=1, device_id=peer, device_id_type=MESH)
    pl.semaphore_wait(barrier, len(peers))


def _xy_peers(x, y):
    return [(1 - x, y), (x, 1 - y), (1 - x, 1 - y)]


def gather_halves(halves, *, name, collective_id):
    n = len(halves)

    def body(*refs):
        ins, lands, sibs = refs[:n], refs[n:2 * n], refs[2 * n:3 * n]
        ici_send, ici_recv, d2d_send, d2d_recv = refs[3 * n:]
        x, y, c = lax.axis_index("x"), lax.axis_index("y"), lax.axis_index("c")
        mine = 2 * x + y
        peers = _xy_peers(x, y)
        _handshake([(px, py, c) for px, py in peers] + [(x, y, 1 - c)])

        def ici(i, k, slot):
            px, py = peers[k]
            return pltpu.make_async_remote_copy(src_ref=ins[i].at[c], dst_ref=lands[i].at[slot], send_sem=ici_send.at[3 * i + k],
                                                recv_sem=ici_recv.at[3 * i + k], device_id=(px, py, c), device_id_type=MESH)

        def pass_on(i, k):
            px, py = peers[k]
            slot = 2 * px + py
            return pltpu.make_async_remote_copy(src_ref=lands[i].at[slot], dst_ref=sibs[i].at[slot], send_sem=d2d_send.at[3 * i + k],
                                                recv_sem=d2d_recv.at[3 * i + k], device_id=(x, y, 1 - c), device_id_type=MESH)

        sends = [ici(i, k, mine) for i in range(n) for k in range(3)]
        for cp in sends:
            cp.start()
        passed = []
        for i in range(n):
            for k in range(3):
                px, py = peers[k]
                ici(i, k, 2 * px + py).wait_recv()
                cp = pass_on(i, k)
                cp.start()
                passed.append(cp)
        for cp in passed:
            cp.wait_recv()
        for cp in sends + passed:
            cp.wait_send()

    outs = [_sds((N_SHARD,) + h.shape[1:], h.dtype) for h in halves]
    res = _sequencer_call(body, halves, outs + outs, [3 * n] * 4, name, collective_id)
    return res[:n], res[n:]


def send_other_half(arrays, *, name, collective_id):
    n = len(arrays)

    def body(*refs):
        ins, lands = refs[:n], refs[n:2 * n]
        send_sems, recv_sems = refs[2 * n:]
        x, y, c = lax.axis_index("x"), lax.axis_index("y"), lax.axis_index("c")
        _handshake([(x, y, 1 - c)])
        copies = [pltpu.make_async_remote_copy(src_ref=ins[i].at[1 - c], dst_ref=lands[i], send_sem=send_sems.at[i],
                                               recv_sem=recv_sems.at[i], device_id=(x, y, 1 - c), device_id_type=MESH) for i in range(n)]
        for cp in copies:
            cp.start()
        for cp in copies:
            cp.wait_recv()
        for cp in copies:
            cp.wait_send()

    return _sequencer_call(body, arrays, [_sds(a.shape[1:], a.dtype) for a in arrays], [n, n], name, collective_id)


_HBM = pl.BlockSpec(memory_space=pltpu.HBM)
_SEM = pl.BlockSpec(memory_space=pltpu.SEMAPHORE)
_SPLIT_COPY = dict(has_side_effects=pltpu.SideEffectType.DATAFLOW_SIDE_EFFECTING)


def _chip_copy(ins, lands, send_sems, recv_sems, scatter, i, k, receive):
    x, y, c = lax.axis_index("x"), lax.axis_index("y"), lax.axis_index("c")
    px, py = _xy_peers(x, y)[k]
    theirs, mine = 2 * px + py, 2 * x + y
    src = ins[i].at[theirs] if scatter[i] else ins[i].at[0]
    return pltpu.make_async_remote_copy(src_ref=src, dst_ref=lands[i].at[theirs if receive else mine], send_sem=send_sems.at[3 * i + k],
                                        recv_sem=recv_sems.at[3 * i + k], device_id=(px, py, c), device_id_type=MESH)


def send_to_chips_start(arrays, scatter, *, name):
    n = len(arrays)

    def body(*refs):
        send_sems, recv_sems = refs[2 * n], refs[2 * n + 1]
        ins, lands = refs[2 * n + 2:3 * n + 2], refs[3 * n + 2:4 * n + 2]
        token = refs[4 * n + 2]
        for i in range(n):
            for k in range(3):
                _chip_copy(ins, lands, send_sems, recv_sems, scatter, i, k, receive=False).start()
        token[...] = jnp.zeros_like(token)

    land_shapes = [(N_SHARD,) + a.shape[1:] for a in arrays]
    operands = [pltpu.with_memory_space_constraint(a, pltpu.HBM) for a in arrays]
    operands += [pltpu.with_memory_space_constraint(lax.empty(s, a.dtype), pltpu.HBM) for s, a in zip(land_shapes, arrays)]
    out_shape = ([pltpu.SemaphoreType.DMA((3 * n,)), pltpu.SemaphoreType.DMA((3 * n,))] + [pltpu.HBM(a.shape, a.dtype) for a in arrays]
                 + [pltpu.HBM(s, a.dtype) for s, a in zip(land_shapes, arrays)] + [_sds((8, LANES), F32)])
    res = pl.pallas_call(body, name=name, out_shape=out_shape, in_specs=[_HBM] * (2 * n),
                         out_specs=[_SEM, _SEM] + [_HBM] * (2 * n) + [pl.BlockSpec(memory_space=pltpu.VMEM)],
                         input_output_aliases={i: 2 + i for i in range(2 * n)}, compiler_params=pltpu.CompilerParams(**_SPLIT_COPY))(*operands)
    return (res[0], res[1], res[2:2 + n], res[2 + n:2 + 2 * n], scatter), res[-1]


def send_to_chips_wait(state, after, *, name):
    send_sems, recv_sems, arrays, lands, scatter = state
    n = len(arrays)

    def body(*refs):
        ins, landing = refs[:n], refs[n:2 * n]
        send_sems, recv_sems = refs[2 * n], refs[2 * n + 1]
        for i in range(n):
            for k in range(3):
                _chip_copy(ins, landing, send_sems, recv_sems, scatter, i, k, receive=True).wait_recv()
        for i in range(n):
            for k in range(3):
                _chip_copy(ins, landing, send_sems, recv_sems, scatter, i, k, receive=False).wait_send()

    out_shape = [pltpu.HBM(a.shape, a.dtype) for a in list(arrays) + list(lands)]
    res = pl.pallas_call(body, name=name, out_shape=out_shape, in_specs=[_HBM] * (2 * n) + [_SEM, _SEM] + [_ANY] * len(after),
                         out_specs=[_HBM] * (2 * n), input_output_aliases={i: i for i in range(2 * n)},
                         compiler_params=pltpu.CompilerParams(**_SPLIT_COPY))(*arrays, *lands, send_sems, recv_sems, *after)
    return res[:n], res[n:]


def swap_with_other_core(arrays, *, name, collective_id):
    n = len(arrays)

    def body(*refs):
        ins, lands = refs[:n], refs[n:2 * n]
        send_sems, recv_sems = refs[2 * n:]
        x, y, c = lax.axis_index("x"), lax.axis_index("y"), lax.axis_index("c")
        _handshake([(x, y, 1 - c)])
        copies = [pltpu.make_async_remote_copy(src_ref=ins[i], dst_ref=lands[i], send_sem=send_sems.at[i], recv_sem=recv_sems.at[i],
                                               device_id=(x, y, 1 - c), device_id_type=MESH) for i in range(n)]
        for cp in copies:
            cp.start()
        for cp in copies:
            cp.wait_recv()
        for cp in copies:
            cp.wait_send()

    return _sequencer_call(body, arrays, [_sds(a.shape, a.dtype) for a in arrays], [n, n], name, collective_id)


def _pack_rows(n_elems, row_multiple):
    rows = -(-n_elems // LANES)
    return -(-rows // row_multiple) * row_multiple


def _pack(arrays, rows, dtype):
    flat = jnp.concatenate([a.reshape(-1).astype(dtype) for a in arrays])
    return jnp.pad(flat, (0, rows * LANES - flat.shape[0])).reshape(rows, LANES)


def _unpack(packed, shapes):
    flat = packed.reshape(-1)
    out, off = [], 0
    for s in shapes:
        n = int(np.prod(s))
        out.append(flat[off:off + n].reshape(s))
        off += n
    return out


def all_gather_shards(shards, axes, dtype, row_multiple, tag):
    shapes = [s.shape for s in shards]
    rows = _pack_rows(sum(int(np.prod(s)) for s in shapes), row_multiple)
    packed = _pack(shards, rows, dtype).reshape(2, rows // 2, LANES)
    land = xy_exchange(packed, name=f"gather_xy_{tag}")
    both = core_exchange(land, name=f"gather_c_{tag}")
    per_shard = jnp.swapaxes(both, 0, 1).reshape(N_SHARD, rows, LANES)
    pieces = [_unpack(per_shard[s], shapes) for s in range(N_SHARD)]
    return [jnp.concatenate([pieces[s][i] for s in range(N_SHARD)], axis=ax) for i, ax in enumerate(axes)]


def _ordered_before(first, then):
    if then is None:
        return first, None
    return lax.optimization_barrier((first, then))


def reduce_between_cores(arrays, scatter, *, tag, collective_id, before=None):
    arrays, before = _ordered_before(arrays, before)
    land = send_other_half(arrays, name=f"reduce_core_send_{tag}", collective_id=collective_id)
    return (arrays, land, scatter, tag, collective_id), before


def reduce_between_chips(state, before=None):
    arrays, land, scatter, tag, collective_id = state
    chip = [add_core_halves(a, l, out_dtype=BF16 if sc else F32, name=f"reduce_core_add_{tag}_{i}")
            for i, (a, l, sc) in enumerate(zip(arrays, land, scatter))]
    sending, token = send_to_chips_start(chip, scatter, name=f"reduce_chip_start_{tag}")
    token, before = _ordered_before(token, before)
    return (sending, token, scatter, tag, collective_id), before


def reduce_finish(state, after):
    sending, token, scatter, tag, collective_id = state
    chip, land = send_to_chips_wait(sending, tuple(after) + (token,), name=f"reduce_chip_wait_{tag}")
    own = [sum_over_chips(ch, l, scatter=sc, name=f"reduce_chip_add_{tag}_{i}") for i, (ch, l, sc) in enumerate(zip(chip, land, scatter))]
    sib = swap_with_other_core(own, name=f"reduce_core_swap_{tag}", collective_id=collective_id + 2)
    return own, sib


def _ffn_layer_fwd(h, norm_g, w_up, cw, cb, w_down, tag):
    u, hn = norm_matmul(h, norm_g, w_up, name=f"ffn_up_{tag}")
    act = ffn_act_fwd(u, cw, cb, name=f"ffn_act_{tag}")
    out = matmul(act, w_down, add=h, name=f"ffn_down_{tag}")
    return out, (h, hn, u, act)


def _travel_layout(array):
    return BIG_ARRAYS[array][3], BIG_ARRAYS[array][4]


def _ffn_layer_bwd(saved, dout, norm_g, w_up, cw, cb, w_down, tag):
    h, hn, u, act = saved
    dact = matmul(dout, w_down, tb=True, name=f"ffn_down_dx_{tag}")
    d_w_down = matmul(act, dout, ta=True, split=_travel_layout(f"ffn_w_down_{tag}"), name=f"ffn_down_dw_{tag}")
    du, dcw, dcb = ffn_act_bwd(u, cw, cb, dact, name=f"ffn_act_bwd_{tag}")
    dhn = matmul(du, w_up, tb=True, name=f"ffn_up_dx_{tag}")
    d_w_up = matmul(hn, du, ta=True, split=_travel_layout(f"ffn_w_up_{tag}"), name=f"ffn_up_dw_{tag}")
    dh, dg = norm_bwd(h, norm_g, dhn, dout, name=f"ffn_norm_bwd_{tag}")
    return dh, dg, d_w_up, dcw, dcb, d_w_down


def local_step(x, target, w, stage=lambda name, tensors, grads=None: tensors):
    g = {}
    tables = _ret_tables()
    x = stage("start", x)
    w_in_t = w["ret_gdn_w_in"]
    w_main = w_in_t[:MIX_MAIN]
    w_small = jnp.pad(w_in_t[MIX_MAIN:], ((0, LANES - 2 * N_HEADS), (0, 0)))
    a_log = jnp.pad(w["gdn_a_log"], ((0, 0), (0, LANES - N_HEADS)))
    dt_bias = jnp.pad(w["gdn_dt_bias"], ((0, 0), (0, LANES - N_HEADS)))

    p, hn0, small = norm_matmul(x, w["norm_mix"][0:1], w_main, tb=True, also=w_small, name="mix0_in")
    hn0 = stage("normed", hn0)
    y_ret, s_ret = ret_fwd(p, tables, name="ret_fwd")
    conv = gdn_conv_fwd(p, w["gdn_conv_w"], name="gdn_conv")
    y0, s_gdn = gdn_fwd(conv, p, small, a_log, dt_bias, w["gdn_out_gain"], y_ret, name="gdn_fwd")
    y0 = stage("mixed", y0)
    h1 = matmul(y0, w["ret_gdn_w_out"], add=x, name="mix0_out")
    h2, ffn0 = _ffn_layer_fwd(h1, w["norm_ffn"][0:1], w["ffn_w_up"][0], w["ffn_conv_w"][0], w["ffn_conv_b"][0:1], w["ffn_w_down"][0], "0")
    h2 = stage("layer0", h2)

    gx, hn1 = norm_matmul(h2, w["norm_mix"][1:2], w["lru_w_in"], name="mix1_in")
    lru_p = (w["lru_conv_w"], w["lru_conv_b"], w["lru_w_a"], w["lru_b_a"], w["lru_w_x"], w["lru_b_x"], w["lru_lambda"])
    y1 = lru_fwd(gx, *lru_p, name="lru_fwd")
    h3 = stage("mixed1", matmul(y1, w["lru_w_out"], add=h2, name="mix1_out"))
    h4, ffn1 = _ffn_layer_fwd(h3, w["norm_ffn"][1:2], w["ffn_w_up"][1], w["ffn_conv_w"][1], w["ffn_conv_b"][1:2], w["ffn_w_down"][1], "1")

    loss, dh4, g["norm_final"] = final_fwd_bwd(h4, w["norm_final"], target, name="final")

    dh3, dgf1, dwu1, dcw1, dcb1, dwd1 = _ffn_layer_bwd(ffn1, dh4, w["norm_ffn"][1:2], w["ffn_w_up"][1], w["ffn_conv_w"][1],
                                                     w["ffn_conv_b"][1:2], w["ffn_w_down"][1], "1")
    g["ffn_w_up_1"], g["ffn_w_down_1"] = dwu1, dwd1
    dh3 = stage("grads0_ready", dh3, g)
    dy1 = matmul(dh3, w["lru_w_out"], tb=True, name="mix1_out_dx")
    g["lru_w_out"] = matmul(y1, dh3, ta=True, split=_travel_layout("lru_w_out"), name="mix1_out_dw")
    dgx, g["lru_conv_w"], g["lru_conv_b"], g["lru_w_a"], g["lru_b_a"], g["lru_w_x"], g["lru_b_x"], g["lru_lambda"] = lru_bwd(
        gx, *lru_p, dy1, name="lru_bwd")
    dgx = stage("grads0_send", dgx, g)
    dhn1 = matmul(dgx, w["lru_w_in"], tb=True, name="mix1_in_dx")
    g["lru_w_in"] = matmul(hn1, dgx, ta=True, split=_travel_layout("lru_w_in"), name="mix1_in_dw")
    dh2, dgm1 = norm_bwd(h2, w["norm_mix"][1:2], dhn1, dh3, name="mix1_norm_bwd")
    dh2 = stage("grads1_ready", dh2, g)

    dh1, dgf0, dwu0, dcw0, dcb0, dwd0 = _ffn_layer_bwd(ffn0, dh2, w["norm_ffn"][0:1], w["ffn_w_up"][0], w["ffn_conv_w"][0],
                                                     w["ffn_conv_b"][0:1], w["ffn_w_down"][0], "0")
    g["ffn_w_up_0"], g["ffn_w_down_0"] = dwu0, dwd0
    dh1 = stage("grads2_ready", stage("grads1_send", dh1, g), g)
    dy0 = matmul(dh1, w["ret_gdn_w_out"], tb=True, name="mix0_out_dx")
    g["ret_gdn_w_out"] = matmul(y0, dh1, ta=True, split=_travel_layout("ret_gdn_w_out"), name="mix0_out_dw")
    dq_r, dk_r, dv_r, dg_r = ret_bwd(p, tables, s_ret, dy0, name="ret_bwd")
    dy0, dq_r = stage("grads2_send", (dy0, dq_r), g)
    dconv, dg_d, dsmall, dal, ddt, dgain = gdn_bwd(conv, p, small, a_log, dt_bias, w["gdn_out_gain"], s_gdn, dy0, name="gdn_bwd")
    dp_conv, g["gdn_conv_w"] = gdn_conv_bwd(p, w["gdn_conv_w"], dconv, name="gdn_conv_bwd")
    dp = jnp.concatenate([dq_r, dk_r, dv_r, dg_r, dp_conv, dg_d], axis=1)
    dhn0 = matmul(dp, w_main, more=(dsmall, w_small), name="mix0_in_dx")
    d_w_in = matmul(dp, hn0, ta=True, column_halves=(MIX_IN, 0, None, None), name="mix0_in_dw")
    d_w_in = matmul(dsmall, hn0, ta=True, column_halves=(MIX_IN, MIX_MAIN, 2 * N_HEADS, d_w_in), name="mix0_in_small_dw")
    g["ret_gdn_w_in"] = d_w_in.reshape(2, N_SHARD, MIX_IN // N_SHARD, D_MODEL // 2)
    dx, dgm0 = norm_bwd(x, w["norm_mix"][0:1], dhn0, dh1, name="mix0_norm_bwd")

    g["gdn_a_log"] = dal[:, :N_HEADS]
    g["gdn_dt_bias"] = ddt[:, :N_HEADS]
    g["gdn_out_gain"] = dgain
    g["norm_mix"] = jnp.concatenate([dgm0, dgm1], axis=0)
    g["norm_ffn"] = jnp.concatenate([dgf0, dgf1], axis=0)
    g["ffn_conv_w"] = jnp.stack([dcw0, dcw1])
    g["ffn_conv_b"] = jnp.concatenate([dcb0, dcb1], axis=0)
    return loss, dx, g


WEIGHTS = ("norm_mix", "norm_ffn", "ret_gdn_w_in", "gdn_conv_w", "gdn_a_log", "gdn_dt_bias", "gdn_out_gain", "ret_gdn_w_out",
           "lru_w_in", "lru_conv_w", "lru_conv_b", "lru_w_a", "lru_b_a", "lru_w_x", "lru_b_x", "lru_lambda", "lru_w_out",
           "ffn_w_up", "ffn_conv_w", "ffn_conv_b", "ffn_w_down", "norm_final")
MATMUL_SHARDED = {"ret_gdn_w_in": 1, "ret_gdn_w_out": 0, "lru_w_in": 1, "lru_w_out": 0, "ffn_w_up": 2, "ffn_w_down": 1}
VECTOR_SHARDED = {"gdn_conv_w": 1, "lru_conv_w": 1, "lru_conv_b": 1, "lru_b_a": 1, "lru_b_x": 1, "lru_lambda": 1, "ffn_conv_w": 2}
SHARDED = {**MATMUL_SHARDED, **VECTOR_SHARDED}
REPLICATED = tuple(n for n in WEIGHTS if n not in SHARDED)
SQUEEZE = {"ret_gdn_w_in", "gdn_conv_w", "ret_gdn_w_out", "lru_w_in", "lru_conv_w", "lru_w_a", "lru_w_x", "lru_w_out"}
MIX_IN = MIX_MAIN + 2 * N_HEADS
BIG_ARRAYS = {
    "ret_gdn_w_in": ("ret_gdn_w_in", None, (MIX_IN, D_MODEL), (N_SHARD, MIX_IN // N_SHARD, 2, D_MODEL // 2), (2, 0, 1, 3)),
    "ret_gdn_w_out": ("ret_gdn_w_out", None, (2 * GROUP, D_MODEL), (N_SHARD, 2, GROUP // N_SHARD, D_MODEL), (1, 0, 2, 3)),
    "lru_w_in": ("lru_w_in", None, (D_MODEL, 2 * D_MODEL), (2, D_MODEL // 2, N_SHARD, 2 * D_MODEL // N_SHARD), (0, 2, 1, 3)),
    "lru_w_out": ("lru_w_out", None, (D_MODEL, D_MODEL), (N_SHARD, 2, D_MODEL // (2 * N_SHARD), D_MODEL), (1, 0, 2, 3)),
    "ffn_w_up_0": ("ffn_w_up", 0, (D_MODEL, 2 * D_FF), (2, D_MODEL // 2, N_SHARD, 2 * D_FF // N_SHARD), (0, 2, 1, 3)),
    "ffn_w_up_1": ("ffn_w_up", 1, (D_MODEL, 2 * D_FF), (2, D_MODEL // 2, N_SHARD, 2 * D_FF // N_SHARD), (0, 2, 1, 3)),
    "ffn_w_down_0": ("ffn_w_down", 0, (D_FF, D_MODEL), (N_SHARD, 2, D_FF // (2 * N_SHARD), D_MODEL), (1, 0, 2, 3)),
    "ffn_w_down_1": ("ffn_w_down", 1, (D_FF, D_MODEL), (N_SHARD, 2, D_FF // (2 * N_SHARD), D_MODEL), (1, 0, 2, 3)),
}
GATHER_GROUPS = (("ret_gdn_w_in",), ("ret_gdn_w_out", "ffn_w_up_0", "ffn_w_down_0"), ("lru_w_in", "lru_w_out"), ("ffn_w_up_1", "ffn_w_down_1"))
REDUCE_GROUPS = (("ffn_w_up_1", "ffn_w_down_1"), ("lru_w_in", "lru_w_out"), ("ffn_w_up_0", "ffn_w_down_0"), ("ret_gdn_w_out", "ret_gdn_w_in"))
BLOCK_WEIGHTS = ("lru_w_a", "lru_w_x")
GATHER_COLLECTIVE_ID = 1
REDUCE_COLLECTIVE_ID = GATHER_COLLECTIVE_ID + len(GATHER_GROUPS)


TRANSPOSED = ("ret_gdn_w_in",)


def _shard_of(array, tensors):
    weight, layer = BIG_ARRAYS[array][:2]
    t = tensors[weight]
    if weight in TRANSPOSED:
        return jnp.swapaxes(t, 1, 2)[0]
    return _local_view(weight, t) if layer is None else t[layer]


def _core_halves(array, shard):
    _, _, _, split, perm = BIG_ARRAYS[array]
    kept = [k for k in range(4) if k != perm[1]]
    order = [kept.index(perm[0]), kept.index(perm[2]), kept.index(perm[3])]
    return shard.reshape([split[k] for k in kept]).transpose(order)


def _local_view(name, a):
    if name in SQUEEZE:
        return a[0]
    if a.ndim == 1:
        return a[None, :]
    return a


def kernel(x, norm_mix, norm_ffn, ret_gdn_w_in, gdn_conv_w, gdn_a_log, gdn_dt_bias, gdn_out_gain, ret_gdn_w_out, lru_w_in, lru_conv_w, lru_conv_b, lru_w_a, lru_b_a, lru_w_x, lru_b_x, lru_lambda, lru_w_out, ffn_w_up, ffn_conv_w, ffn_conv_b, ffn_w_down, norm_final, loss_target, m_norm_mix, m_norm_ffn, m_ret_gdn_w_in, m_gdn_conv_w, m_gdn_a_log, m_gdn_dt_bias, m_gdn_out_gain, m_ret_gdn_w_out, m_lru_w_in, m_lru_conv_w, m_lru_conv_b, m_lru_w_a, m_lru_b_a, m_lru_w_x, m_lru_b_x, m_lru_lambda, m_lru_w_out, m_ffn_w_up, m_ffn_conv_w, m_ffn_conv_b, m_ffn_w_down, m_norm_final, v_norm_mix, v_norm_ffn, v_ret_gdn_w_in, v_gdn_conv_w, v_gdn_a_log, v_gdn_dt_bias, v_gdn_out_gain, v_ret_gdn_w_out, v_lru_w_in, v_lru_conv_w, v_lru_conv_b, v_lru_w_a, v_lru_b_a, v_lru_w_x, v_lru_b_x, v_lru_lambda, v_lru_w_out, v_ffn_w_up, v_ffn_conv_w, v_ffn_conv_b, v_ffn_w_down, v_norm_final):
    given = dict(norm_mix=norm_mix, norm_ffn=norm_ffn, ret_gdn_w_in=ret_gdn_w_in, gdn_conv_w=gdn_conv_w, gdn_a_log=gdn_a_log, gdn_dt_bias=gdn_dt_bias, gdn_out_gain=gdn_out_gain, ret_gdn_w_out=ret_gdn_w_out, lru_w_in=lru_w_in, lru_conv_w=lru_conv_w, lru_conv_b=lru_conv_b, lru_w_a=lru_w_a, lru_b_a=lru_b_a, lru_w_x=lru_w_x, lru_b_x=lru_b_x, lru_lambda=lru_lambda, lru_w_out=lru_w_out, ffn_w_up=ffn_w_up, ffn_conv_w=ffn_conv_w, ffn_conv_b=ffn_conv_b, ffn_w_down=ffn_w_down, norm_final=norm_final)
    mom1 = dict(norm_mix=m_norm_mix, norm_ffn=m_norm_ffn, ret_gdn_w_in=m_ret_gdn_w_in, gdn_conv_w=m_gdn_conv_w, gdn_a_log=m_gdn_a_log, gdn_dt_bias=m_gdn_dt_bias, gdn_out_gain=m_gdn_out_gain, ret_gdn_w_out=m_ret_gdn_w_out, lru_w_in=m_lru_w_in, lru_conv_w=m_lru_conv_w, lru_conv_b=m_lru_conv_b, lru_w_a=m_lru_w_a, lru_b_a=m_lru_b_a, lru_w_x=m_lru_w_x, lru_b_x=m_lru_b_x, lru_lambda=m_lru_lambda, lru_w_out=m_lru_w_out, ffn_w_up=m_ffn_w_up, ffn_conv_w=m_ffn_conv_w, ffn_conv_b=m_ffn_conv_b, ffn_w_down=m_ffn_w_down, norm_final=m_norm_final)
    mom2 = dict(norm_mix=v_norm_mix, norm_ffn=v_norm_ffn, ret_gdn_w_in=v_ret_gdn_w_in, gdn_conv_w=v_gdn_conv_w, gdn_a_log=v_gdn_a_log, gdn_dt_bias=v_gdn_dt_bias, gdn_out_gain=v_gdn_out_gain, ret_gdn_w_out=v_ret_gdn_w_out, lru_w_in=v_lru_w_in, lru_conv_w=v_lru_conv_w, lru_conv_b=v_lru_conv_b, lru_w_a=v_lru_w_a, lru_b_a=v_lru_b_a, lru_w_x=v_lru_w_x, lru_b_x=v_lru_b_x, lru_lambda=v_lru_lambda, lru_w_out=v_lru_w_out, ffn_w_up=v_ffn_w_up, ffn_conv_w=v_ffn_conv_w, ffn_conv_b=v_ffn_conv_b, ffn_w_down=v_ffn_w_down, norm_final=v_norm_final)

    local = {n: _local_view(n, a) for n, a in given.items()}

    core = lax.axis_index("c")
    chip = 2 * lax.axis_index("x") + lax.axis_index("y")
    is_my_chip = lax.broadcasted_iota(jnp.int32, (N_SHARD, 1, 1), 0) == chip

    def by_core(mine, other):
        return jnp.where(core == 0, jnp.stack([mine, other]), jnp.stack([other, mine]))

    vec_names, rp_names = list(VECTOR_SHARDED), list(REPLICATED)
    full = dict(zip(vec_names, all_gather_shards([local[n] for n in vec_names], [SHARDED[n] for n in vec_names], F32, 32, "p")))
    for n in rp_names:
        full[n] = local[n]
    in_flight = {}

    bf16_halves = {}

    def cast_halves(gi):
        if gi not in bf16_halves:
            bf16_halves[gi] = [_core_halves(a, _shard_of(a, given).astype(BF16)) for a in GATHER_GROUPS[gi]]
        return bf16_halves[gi]

    def launch(gi, after=None):
        halves = cast_halves(gi)
        if after is not None:
            halves, after = lax.optimization_barrier((halves, after))
        in_flight[gi] = (halves,) + gather_halves(halves, name=f"gather_weights_{gi}", collective_id=GATHER_COLLECTIVE_ID + gi)
        return after

    def land(gi, after):
        halves, lands, sibs = in_flight[gi]
        (lands, sibs), after = lax.optimization_barrier(((lands, sibs), after))
        for a, mine, got, passed in zip(GATHER_GROUPS[gi], halves, lands, sibs):
            weight, layer, full_shape, split, perm = BIG_ARRAYS[a]
            half_mine = jnp.where(is_my_chip, jnp.where(core == 0, mine[0], mine[1])[None], got)
            half_other = jnp.where(is_my_chip, jnp.where(core == 0, mine[1], mine[0])[None], passed)
            value = by_core(half_mine, half_other).transpose(tuple(np.argsort(perm))).reshape(full_shape)
            if layer is None:
                full[weight] = value
            else:
                full.setdefault(weight, [None, None])[layer] = value
        return after

    reducing = {}

    def reduce_ready(gi, grads, then=None, extra=()):
        def travelling(a):
            split, perm = _travel_layout(a)
            return grads[a] if grads[a].ndim == 4 else grads[a].reshape(split).transpose(perm)

        arrays = [travelling(a) for a in REDUCE_GROUPS[gi]] + list(extra)
        scatter = [True] * len(REDUCE_GROUPS[gi]) + [False] * len(extra)
        reducing[gi], then = reduce_between_cores(arrays, scatter, tag=str(gi), collective_id=REDUCE_COLLECTIVE_ID + 3 * gi, before=then)
        return then

    def reduce_send(gi, then=None):
        reducing[gi], then = reduce_between_chips(reducing[gi], before=then)
        return then

    def stage(name, tensors, grads=None):
        if name == "start":
            launch(0)
            launch(1)
            fillers = (cast_halves(2), cast_halves(3), [full[n] for n in vec_names])
            (bf16_halves[2], bf16_halves[3], gathered_small), tensors = lax.optimization_barrier((fillers, tensors))
            full.update(zip(vec_names, gathered_small))
            return land(0, tensors)
        if name == "normed":
            return launch(3, launch(2, tensors))
        if name in ("mixed", "layer0", "mixed1"):
            return land({"mixed": 1, "layer0": 2, "mixed1": 3}[name], tensors)
        gi = int(name[len("grads")])
        return reduce_ready(gi, grads, tensors) if name.endswith("_ready") else reduce_send(gi, tensors)

    small_names = [n for n in rp_names if n not in BLOCK_WEIGHTS] + vec_names

    loss_part, dx, grads = local_step(x[0], loss_target[0], full, stage)
    small_shapes = [grads[n].shape for n in small_names] + [(1, 1)]
    small_rows = _pack_rows(sum(int(np.prod(s)) for s in small_shapes), 16)
    small = _pack([grads[n] for n in small_names] + [loss_part[:, :1]], small_rows, F32).reshape(2, 1, small_rows // 2, LANES)
    last = len(REDUCE_GROUPS) - 1
    halves_of_blocks = [grads[n].reshape(2, 1, LRU_BLOCKS * HEAD // 2, HEAD) for n in BLOCK_WEIGHTS]
    reduce_ready(last, grads, extra=[small] + halves_of_blocks)
    reduce_send(last)
    reduced, result = {}, {}

    def finish(gi, after):
        g_own, g_sib = reduce_finish(reducing[gi], after)
        reduced.update(zip(list(REDUCE_GROUPS[gi]) + ["small"] + list(BLOCK_WEIGHTS), zip(g_own, g_sib)))

    def update(n):
        if n in TRANSPOSED:
            n_rows, n_cols = given[n].shape[2], given[n].shape[1]

            def rows(t):
                return jnp.swapaxes(t, 1, 2).reshape(n_rows, 1, n_cols)

            def back(t):
                return jnp.swapaxes(t.reshape(1, n_rows, n_cols), 1, 2)

            g_rows = jnp.swapaxes(by_core(*reduced[n]), 0, 1).reshape(n_rows, 1, n_cols)
            result[n] = tuple(back(t) for t in adamw_rows(rows(given[n]), g_rows, rows(mom1[n]), rows(mom2[n]), name=f"adamw_{n}"))
            return
        done = None
        for a in (k for k, spec in BIG_ARRAYS.items() if spec[0] == n):
            r, cols = reduced[a][0].shape
            layer = BIG_ARRAYS[a][1] or 0
            w3, m3, v3 = (t if BIG_ARRAYS[a][1] is not None else t.reshape(1, 2 * r, cols) for t in (given[n], mom1[n], mom2[n]))
            done = adamw_halves(w3, m3, v3, *reduced[a], layer=layer, prev=done, name=f"adamw_{a}")
        result[n] = done

    updated = []
    for gi in range(last + 1):
        finish(gi, tuple(result[n][0] for n in updated) if updated else (dx, reducing[last][1]))
        for n in MATMUL_SHARDED:
            if n not in updated and all(a in reduced for a, spec in BIG_ARRAYS.items() if spec[0] == n):
                update(n)
                updated.append(n)

    for n in BLOCK_WEIGHTS:
        w3, m3, v3 = (t.reshape(1, LRU_BLOCKS * HEAD, HEAD) for t in (given[n], mom1[n], mom2[n]))
        result[n] = adamw_halves(w3, m3, v3, *reduced[n], name=f"adamw_{n}")

    *small_sums, loss_sum = _unpack(by_core(*reduced["small"]).reshape(small_rows, LANES), small_shapes)
    loss = loss_sum[0, 0]
    g_small = dict(zip(small_names, small_sums))
    for n in vec_names:
        size = local[n].shape[SHARDED[n]]
        g_small[n] = lax.dynamic_slice_in_dim(g_small[n], chip * size, size, axis=SHARDED[n])
    views = [[_local_view(n, src[n]) for n in small_names] for src in (given, mom1, mom2)]
    d_s, m_s, v_s = adamw_many(views[0], [g_small[n] for n in small_names], views[1], views[2], name="adamw_small")
    for n, d, nm, nv in zip(small_names, d_s, m_s, v_s):
        result[n] = (g_small[n], d, nm, nv)

    outs = [[result[n][k].reshape(given[n].shape) for n in WEIGHTS] for k in range(4)]
    return (loss, dx[None], *outs[0], *outs[1], *outs[2], *outs[3])
```
